```python
import jax, jax.numpy as jnp
from jax import lax
import numpy as np

D_MODEL = 1024
BATCH = 16
SEQ = 2048
DEPTH = 1

N_META = 16
EPS = 1e-6
SSD_HEADS = 16
SSD_HEAD_DIM = 64
SSD_INNER = SSD_HEADS * SSD_HEAD_DIM
SSD_GROUPS = 4
SSD_HPG = SSD_HEADS // SSD_GROUPS
SSD_STATE = 128
SSD_CONV = 4
SSD_CHUNK = 128
SSD_CONV_CH = SSD_INNER + 2 * SSD_GROUPS * SSD_STATE
HG_WIDTH = 1024
HG_EXPAND = 128
HG_HEADS = HG_WIDTH // HG_EXPAND
HG_HEAD_I = HG_WIDTH // HG_HEADS
HG_CHUNK = 16
D_FF = 2816
IN_SIZES = (SSD_INNER, SSD_CONV_CH, SSD_HEADS, HG_WIDTH, HG_WIDTH, HG_WIDTH, HG_WIDTH, D_MODEL, D_MODEL)
IN_TOTAL = sum(IN_SIZES)

kernel_name = "hybrid_ssd_hgrn2_macaron_block"


def _split_points():
    pts, acc = [], 0
    for s in IN_SIZES[:-1]:
        acc += s
        pts.append(acc)
    return pts


def rmsnorm(x, w):
    xf = x.astype(jnp.float32)
    y = xf * lax.rsqrt(jnp.mean(xf * xf, axis=-1, keepdims=True) + EPS)
    return (y * w.astype(jnp.float32)).astype(x.dtype)


def swiglu(x, w_gu, w_down):
    g, u = jnp.split(x @ w_gu, 2, axis=-1)
    return (jax.nn.silu(g) * u) @ w_down


def causal_depthwise_conv(x, w, b):
    y = lax.conv_general_dilated(x, w[:, None, :], window_strides=(1,), padding=[(w.shape[0] - 1, 0)],
                                 dimension_numbers=("NWC", "WIO", "NWC"), feature_group_count=x.shape[-1])
    return y + b


def segsum_exp(a):
    T = a.shape[-1]
    cs = jnp.cumsum(a, axis=-1)
    mask = jnp.tril(jnp.ones((T, T), dtype=bool))
    return jnp.exp(jnp.where(mask, cs[..., :, None] - cs[..., None, :], -jnp.inf))


def ssd_mixer(z, xbc, dt_raw, conv_w, conv_b, dt_bias, a_log, d_skip, norm_w):
    f32 = jnp.float32
    Bsz, L, _ = z.shape
    G, R, P, N, Q = SSD_GROUPS, SSD_HPG, SSD_HEAD_DIM, SSD_STATE, SSD_CHUNK
    xbc = jax.nn.silu(causal_depthwise_conv(xbc, conv_w, conv_b)).astype(f32)
    xs, Bm, Cm = jnp.split(xbc, [SSD_INNER, SSD_INNER + G * N], axis=-1)
    dt = jax.nn.softplus(dt_raw.astype(f32) + dt_bias.astype(f32))
    A = -jnp.exp(a_log.astype(f32))
    pad = (-L) % Q
    padf = lambda t: jnp.pad(t, ((0, 0), (pad, 0)) + ((0, 0),) * (t.ndim - 2))
    Lp = L + pad
    nc = Lp // Q
    x4 = padf(xs).reshape(Bsz, nc, Q, G, R, P)
    dtp = padf(dt).reshape(Bsz, nc, Q, SSD_HEADS)
    Bc = padf(Bm).reshape(Bsz, nc, Q, G, N)
    Cc = padf(Cm).reshape(Bsz, nc, Q, G, N)
    X = x4 * dtp.reshape(Bsz, nc, Q, G, R)[..., None]
    a = (dtp * A).transpose(0, 3, 1, 2)
    a_cs = jnp.cumsum(a, axis=-1)
    Lmat = segsum_exp(a).reshape(Bsz, G, R, nc, Q, Q)
    CB = jnp.einsum("bclgn,bcsgn->bcgls", Cc, Bc)
    y_diag = jnp.einsum("bcgls,bgrcls,bcsgrp->bclgrp", CB, Lmat, X)
    decay_states = jnp.exp(a_cs[..., -1:] - a_cs).reshape(Bsz, G, R, nc, Q)
    states = jnp.einsum("bclgn,bgrcl,bclgrp->cbgrpn", Bc, decay_states, X)
    chunk_decay = jnp.moveaxis(jnp.exp(a_cs[..., -1]).reshape(Bsz, G, R, nc), -1, 0)

    def step(hs, inp):
        s, dec = inp
        return hs * dec[..., None, None] + s, hs

    _, prev = lax.scan(step, jnp.zeros((Bsz, G, R, P, N), f32), (states, chunk_decay))
    y_off = jnp.einsum("bclgn,cbgrpn,bgrcl->bclgrp", Cc, prev, jnp.exp(a_cs).reshape(Bsz, G, R, nc, Q))
    y = y_diag + y_off + x4 * d_skip.astype(f32).reshape(G, R)[:, :, None]
    y = y.reshape(Bsz, Lp, SSD_INNER)[:, pad:]
    yg = (y * jax.nn.silu(z.astype(f32))).reshape(Bsz, L, G, SSD_INNER // G)
    yg = yg * lax.rsqrt(jnp.mean(yg * yg, axis=-1, keepdims=True) + EPS)
    return (yg.reshape(Bsz, L, SSD_INNER) * norm_w.astype(f32)).astype(z.dtype)


def hgrn2_mixer(q, f_logit, i_in, g_out, lb, norm_w):
    f32 = jnp.float32
    Bsz, L, _ = q.shape
    H, K, V, C = HG_HEADS, HG_EXPAND, HG_HEAD_I, HG_CHUNK
    nc = L // C
    f = lb + (1.0 - lb) * jax.nn.sigmoid(f_logit.astype(f32))
    chunked = lambda t, d: jnp.moveaxis(t.reshape(Bsz, nc, C, H, d), 1, 0)
    qs = chunked(jax.nn.silu(q.astype(f32)), K)
    ks = chunked(1.0 - f, K)
    vs = chunked(i_in.astype(f32), V)
    gs = chunked(jnp.log(f), K)
    tri = jnp.tril(jnp.ones((C, C), dtype=bool))[None, :, :, None, None]

    def step(S, inp):
        qc, kc, vc, gc = inp
        Gc = jnp.cumsum(gc, axis=1)
        o_inter = jnp.einsum("blhk,bhkv->blhv", qc * jnp.exp(Gc), S)
        dec = jnp.exp(jnp.where(tri, Gc[:, :, None] - Gc[:, None, :], -jnp.inf))
        att = jnp.einsum("blhk,bshk,blshk->bhls", qc, kc, dec)
        o = o_inter + jnp.einsum("bhls,bshv->blhv", att, vc)
        G_last = Gc[:, -1]
        S_new = jnp.exp(G_last)[..., None] * S + jnp.einsum(
            "bshk,bshv->bhkv", kc * jnp.exp(G_last[:, None] - Gc), vc)
        return S_new, o

    _, o = lax.scan(step, jnp.zeros((Bsz, H, K, V), f32), (qs, ks, vs, gs))
    o = jnp.moveaxis(o, 0, 1).reshape(Bsz, L, H, V)
    o = o * lax.rsqrt(jnp.mean(o * o, axis=-1, keepdims=True) + EPS) * norm_w.astype(f32).reshape(H, V)
    o = o.reshape(Bsz, L, HG_WIDTH) * jax.nn.silu(g_out.astype(f32))
    return o.astype(q.dtype)


def _fwd_setup_inputs(seed: int = 0) -> dict:
    key = jax.random.key(seed)
    ks = jax.random.split(key, 24)
    nrm = lambda k, shape, s: jax.random.normal(k, shape, jnp.float32) * s
    gain = lambda k, shape: 1.0 + 0.02 * jax.random.normal(k, shape, jnp.float32)
    dt0 = jnp.exp(jax.random.uniform(ks[9], (DEPTH, SSD_HEADS), jnp.float32, np.log(1e-3), np.log(1e-1)))
    return {
        "x": nrm(ks[0], (BATCH, SEQ, D_MODEL), 1.0),
        "meta_tokens": nrm(ks[1], (N_META, D_MODEL), 1.0),
        "ffn1_norm": gain(ks[2], (DEPTH, D_MODEL)),
        "ffn1_w_gu": nrm(ks[3], (DEPTH, D_MODEL, 2 * D_FF), D_MODEL ** -0.5),
        "ffn1_w_down": nrm(ks[4], (DEPTH, D_FF, D_MODEL), D_FF ** -0.5),
        "mix_norm": gain(ks[5], (DEPTH, D_MODEL)),
        "w_in": nrm(ks[6], (DEPTH, D_MODEL, IN_TOTAL), D_MODEL ** -0.5),
        "ssd_conv_w": nrm(ks[7], (DEPTH, SSD_CONV, SSD_CONV_CH), SSD_CONV ** -0.5),
        "ssd_conv_b": nrm(ks[8], (DEPTH, SSD_CONV_CH), 0.01),
        "ssd_dt_bias": dt0 + jnp.log(-jnp.expm1(-dt0)),
        "ssd_a_log": jnp.log(jax.random.uniform(ks[10], (DEPTH, SSD_HEADS), jnp.float32, 1.0, 16.0)),
        "ssd_d": 1.0 + 0.1 * jax.random.normal(ks[11], (DEPTH, SSD_HEADS), jnp.float32),
        "ssd_norm": gain(ks[12], (DEPTH, SSD_INNER)),
        "hg_lower_bound": 1.0 + 0.1 * jax.random.normal(ks[13], (DEPTH + 1, HG_WIDTH), jnp.float32),
        "hg_norm": gain(ks[14], (DEPTH, HG_WIDTH)),
        "w_branch_a": nrm(ks[15], (DEPTH, SSD_INNER, D_MODEL), SSD_INNER ** -0.5),
        "w_branch_b": nrm(ks[16], (DEPTH, HG_WIDTH, D_MODEL), HG_WIDTH ** -0.5),
        "w_out": nrm(ks[17], (DEPTH, D_MODEL, D_MODEL), D_MODEL ** -0.5),
        "ffn2_norm": gain(ks[18], (DEPTH, D_MODEL)),
        "ffn2_w_gu": nrm(ks[19], (DEPTH, D_MODEL, 2 * D_FF), D_MODEL ** -0.5),
        "ffn2_w_down": nrm(ks[20], (DEPTH, D_FF, D_MODEL), D_FF ** -0.5),
        "final_norm": gain(ks[21], (D_MODEL,)),
    }


def _fwd_reference(x, meta_tokens, ffn1_norm, ffn1_w_gu, ffn1_w_down, mix_norm, w_in, ssd_conv_w, ssd_conv_b,
              ssd_dt_bias, ssd_a_log, ssd_d, ssd_norm, hg_lower_bound, hg_norm, w_branch_a, w_branch_b,
              w_out, ffn2_norm, ffn2_w_gu, ffn2_w_down, final_norm):
    Bsz = x.shape[0]
    meta = jnp.broadcast_to(meta_tokens[None].astype(x.dtype), (Bsz, N_META, D_MODEL))
    h = jnp.concatenate([meta, x], axis=1)
    lb_all = jnp.cumsum(jax.nn.softmax(hg_lower_bound.astype(jnp.float32), axis=0), axis=0)
    splits = _split_points()
    for l in range(DEPTH):
        h = h + 0.5 * swiglu(rmsnorm(h, ffn1_norm[l]), ffn1_w_gu[l], ffn1_w_down[l])
        u = rmsnorm(h, mix_norm[l])
        z, xbc, dt_raw, q, f_logit, i_in, g_out, gate_a, gate_b = jnp.split(u @ w_in[l], splits, axis=-1)
        y_a = ssd_mixer(z, xbc, dt_raw, ssd_conv_w[l], ssd_conv_b[l], ssd_dt_bias[l], ssd_a_log[l],
                        ssd_d[l], ssd_norm[l])
        y_b = hgrn2_mixer(q, f_logit, i_in, g_out, lb_all[l], hg_norm[l])
        merged = jax.nn.sigmoid(gate_a) * (y_a @ w_branch_a[l]) + jax.nn.sigmoid(gate_b) * (y_b @ w_branch_b[l])
        h = h + merged @ w_out[l]
        h = h + 0.5 * swiglu(rmsnorm(h, ffn2_norm[l]), ffn2_w_gu[l], ffn2_w_down[l])
    h = rmsnorm(h, final_norm)
    return h[:, N_META:]


import jax as _jax
import jax.numpy as _jnp

TWIN_FORMAT = 'train_step'
FWD_PARAMS = ['x', 'meta_tokens', 'ffn1_norm', 'ffn1_w_gu', 'ffn1_w_down', 'mix_norm', 'w_in', 'ssd_conv_w', 'ssd_conv_b', 'ssd_dt_bias', 'ssd_a_log', 'ssd_d', 'ssd_norm', 'hg_lower_bound', 'hg_norm', 'w_branch_a', 'w_branch_b', 'w_out', 'ffn2_norm', 'ffn2_w_gu', 'ffn2_w_down', 'final_norm']
TWIN_WEIGHTS = ['meta_tokens', 'ffn1_norm', 'ffn1_w_gu', 'ffn1_w_down', 'mix_norm', 'w_in', 'ssd_conv_w', 'ssd_conv_b', 'ssd_dt_bias', 'ssd_a_log', 'ssd_d', 'ssd_norm', 'hg_lower_bound', 'hg_norm', 'w_branch_a', 'w_branch_b', 'w_out', 'ffn2_norm', 'ffn2_w_gu', 'ffn2_w_down', 'final_norm']
TWIN_DIFF_INPUT = 'x'
TWIN_INPUTS = ['x', 'meta_tokens', 'ffn1_norm', 'ffn1_w_gu', 'ffn1_w_down', 'mix_norm', 'w_in', 'ssd_conv_w', 'ssd_conv_b', 'ssd_dt_bias', 'ssd_a_log', 'ssd_d', 'ssd_norm', 'hg_lower_bound', 'hg_norm', 'w_branch_a', 'w_branch_b', 'w_out', 'ffn2_norm', 'ffn2_w_gu', 'ffn2_w_down', 'final_norm', 'loss_target', 'm_meta_tokens', 'm_ffn1_norm', 'm_ffn1_w_gu', 'm_ffn1_w_down', 'm_mix_norm', 'm_w_in', 'm_ssd_conv_w', 'm_ssd_conv_b', 'm_ssd_dt_bias', 'm_ssd_a_log', 'm_ssd_d', 'm_ssd_norm', 'm_hg_lower_bound', 'm_hg_norm', 'm_w_branch_a', 'm_w_branch_b', 'm_w_out', 'm_ffn2_norm', 'm_ffn2_w_gu', 'm_ffn2_w_down', 'm_final_norm', 'v_meta_tokens', 'v_ffn1_norm', 'v_ffn1_w_gu', 'v_ffn1_w_down', 'v_mix_norm', 'v_w_in', 'v_ssd_conv_w', 'v_ssd_conv_b', 'v_ssd_dt_bias', 'v_ssd_a_log', 'v_ssd_d', 'v_ssd_norm', 'v_hg_lower_bound', 'v_hg_norm', 'v_w_branch_a', 'v_w_branch_b', 'v_w_out', 'v_ffn2_norm', 'v_ffn2_w_gu', 'v_ffn2_w_down', 'v_final_norm']
TWIN_OUTPUTS = ['loss', 'grad_x', 'grad_meta_tokens', 'grad_ffn1_norm', 'grad_ffn1_w_gu', 'grad_ffn1_w_down', 'grad_mix_norm', 'grad_w_in', 'grad_ssd_conv_w', 'grad_ssd_conv_b', 'grad_ssd_dt_bias', 'grad_ssd_a_log', 'grad_ssd_d', 'grad_ssd_norm', 'grad_hg_lower_bound', 'grad_hg_norm', 'grad_w_branch_a', 'grad_w_branch_b', 'grad_w_out', 'grad_ffn2_norm', 'grad_ffn2_w_gu', 'grad_ffn2_w_down', 'grad_final_norm', 'delta_meta_tokens', 'delta_ffn1_norm', 'delta_ffn1_w_gu', 'delta_ffn1_w_down', 'delta_mix_norm', 'delta_w_in', 'delta_ssd_conv_w', 'delta_ssd_conv_b', 'delta_ssd_dt_bias', 'delta_ssd_a_log', 'delta_ssd_d', 'delta_ssd_norm', 'delta_hg_lower_bound', 'delta_hg_norm', 'delta_w_branch_a', 'delta_w_branch_b', 'delta_w_out', 'delta_ffn2_norm', 'delta_ffn2_w_gu', 'delta_ffn2_w_down', 'delta_final_norm', 'new_m_meta_tokens', 'new_m_ffn1_norm', 'new_m_ffn1_w_gu', 'new_m_ffn1_w_down', 'new_m_mix_norm', 'new_m_w_in', 'new_m_ssd_conv_w', 'new_m_ssd_conv_b', 'new_m_ssd_dt_bias', 'new_m_ssd_a_log', 'new_m_ssd_d', 'new_m_ssd_norm', 'new_m_hg_lower_bound', 'new_m_hg_norm', 'new_m_w_branch_a', 'new_m_w_branch_b', 'new_m_w_out', 'new_m_ffn2_norm', 'new_m_ffn2_w_gu', 'new_m_ffn2_w_down', 'new_m_final_norm', 'new_v_meta_tokens', 'new_v_ffn1_norm', 'new_v_ffn1_w_gu', 'new_v_ffn1_w_down', 'new_v_mix_norm', 'new_v_w_in', 'new_v_ssd_conv_w', 'new_v_ssd_conv_b', 'new_v_ssd_dt_bias', 'new_v_ssd_a_log', 'new_v_ssd_d', 'new_v_ssd_norm', 'new_v_hg_lower_bound', 'new_v_hg_norm', 'new_v_w_branch_a', 'new_v_w_branch_b', 'new_v_w_out', 'new_v_ffn2_norm', 'new_v_ffn2_w_gu', 'new_v_ffn2_w_down', 'new_v_final_norm']
TWIN_LEAF_KINDS = {'loss': 'loss', 'grad_x': 'grad_x', 'grad_meta_tokens': 'grad_w', 'grad_ffn1_norm': 'grad_w', 'grad_ffn1_w_gu': 'grad_w', 'grad_ffn1_w_down': 'grad_w', 'grad_mix_norm': 'grad_w', 'grad_w_in': 'grad_w', 'grad_ssd_conv_w': 'grad_w', 'grad_ssd_conv_b': 'grad_w', 'grad_ssd_dt_bias': 'grad_w', 'grad_ssd_a_log': 'grad_w', 'grad_ssd_d': 'grad_w', 'grad_ssd_norm': 'grad_w', 'grad_hg_lower_bound': 'grad_w', 'grad_hg_norm': 'grad_w', 'grad_w_branch_a': 'grad_w', 'grad_w_branch_b': 'grad_w', 'grad_w_out': 'grad_w', 'grad_ffn2_norm': 'grad_w', 'grad_ffn2_w_gu': 'grad_w', 'grad_ffn2_w_down': 'grad_w', 'grad_final_norm': 'grad_w', 'delta_meta_tokens': 'delta_w', 'delta_ffn1_norm': 'delta_w', 'delta_ffn1_w_gu': 'delta_w', 'delta_ffn1_w_down': 'delta_w', 'delta_mix_norm': 'delta_w', 'delta_w_in': 'delta_w', 'delta_ssd_conv_w': 'delta_w', 'delta_ssd_conv_b': 'delta_w', 'delta_ssd_dt_bias': 'delta_w', 'delta_ssd_a_log': 'delta_w', 'delta_ssd_d': 'delta_w', 'delta_ssd_norm': 'delta_w', 'delta_hg_lower_bound': 'delta_w', 'delta_hg_norm': 'delta_w', 'delta_w_branch_a': 'delta_w', 'delta_w_branch_b': 'delta_w', 'delta_w_out': 'delta_w', 'delta_ffn2_norm': 'delta_w', 'delta_ffn2_w_gu': 'delta_w', 'delta_ffn2_w_down': 'delta_w', 'delta_final_norm': 'delta_w', 'new_m_meta_tokens': 'new_m', 'new_m_ffn1_norm': 'new_m', 'new_m_ffn1_w_gu': 'new_m', 'new_m_ffn1_w_down': 'new_m', 'new_m_mix_norm': 'new_m', 'new_m_w_in': 'new_m', 'new_m_ssd_conv_w': 'new_m', 'new_m_ssd_conv_b': 'new_m', 'new_m_ssd_dt_bias': 'new_m', 'new_m_ssd_a_log': 'new_m', 'new_m_ssd_d': 'new_m', 'new_m_ssd_norm': 'new_m', 'new_m_hg_lower_bound': 'new_m', 'new_m_hg_norm': 'new_m', 'new_m_w_branch_a': 'new_m', 'new_m_w_branch_b': 'new_m', 'new_m_w_out': 'new_m', 'new_m_ffn2_norm': 'new_m', 'new_m_ffn2_w_gu': 'new_m', 'new_m_ffn2_w_down': 'new_m', 'new_m_final_norm': 'new_m', 'new_v_meta_tokens': 'new_v', 'new_v_ffn1_norm': 'new_v', 'new_v_ffn1_w_gu': 'new_v', 'new_v_ffn1_w_down': 'new_v', 'new_v_mix_norm': 'new_v', 'new_v_w_in': 'new_v', 'new_v_ssd_conv_w': 'new_v', 'new_v_ssd_conv_b': 'new_v', 'new_v_ssd_dt_bias': 'new_v', 'new_v_ssd_a_log': 'new_v', 'new_v_ssd_d': 'new_v', 'new_v_ssd_norm': 'new_v', 'new_v_hg_lower_bound': 'new_v', 'new_v_hg_norm': 'new_v', 'new_v_w_branch_a': 'new_v', 'new_v_w_branch_b': 'new_v', 'new_v_w_out': 'new_v', 'new_v_ffn2_norm': 'new_v', 'new_v_ffn2_w_gu': 'new_v', 'new_v_ffn2_w_down': 'new_v', 'new_v_final_norm': 'new_v'}


def _forward(args):
    return _fwd_reference(*[args[k] for k in FWD_PARAMS])


def _output_shape():
    out = _jax.eval_shape(lambda: _forward(_fwd_setup_inputs(0)))
    return out.shape, out.dtype

N_MICROBATCH = 1
ADAM_LR = 0.001
ADAM_B1 = 0.9
ADAM_B2 = 0.999
ADAM_EPS = 1e-08
ADAM_WD = 0.01
ADAM_STEP = 10
PER_EXAMPLE_BATCH_AXIS = {'x': 0, 'loss_target': 0}
SHARED_INPUTS = []
_WEIGHT_DTYPES = {'meta_tokens': _jnp.float32, 'ffn1_norm': _jnp.float32, 'ffn1_w_gu': _jnp.float32, 'ffn1_w_down': _jnp.float32, 'mix_norm': _jnp.float32, 'w_in': _jnp.float32, 'ssd_conv_w': _jnp.float32, 'ssd_conv_b': _jnp.float32, 'ssd_dt_bias': _jnp.float32, 'ssd_a_log': _jnp.float32, 'ssd_d': _jnp.float32, 'ssd_norm': _jnp.float32, 'hg_lower_bound': _jnp.float32, 'hg_norm': _jnp.float32, 'w_branch_a': _jnp.float32, 'w_branch_b': _jnp.float32, 'w_out': _jnp.float32, 'ffn2_norm': _jnp.float32, 'ffn2_w_gu': _jnp.float32, 'ffn2_w_down': _jnp.float32, 'final_norm': _jnp.float32}
MOMENT_SCALE = {'meta_tokens': 2.544295e-03, 'ffn1_norm': 8.832074e-02, 'ffn1_w_gu': 3.752476e-02, 'ffn1_w_down': 6.133012e-02, 'mix_norm': 1.467097e-01, 'w_in': 4.886541e-02, 'ssd_conv_w': 6.225123e-02, 'ssd_conv_b': 8.075882e-02, 'ssd_dt_bias': 4.156354e-01, 'ssd_a_log': 3.465659e-01, 'ssd_d': 6.733745e-01, 'ssd_norm': 8.096054e-02, 'hg_lower_bound': 4.546841e-03, 'hg_norm': 4.956272e-02, 'w_branch_a': 8.142990e-02, 'w_branch_b': 4.837480e-02, 'w_out': 9.493777e-02, 'ffn2_norm': 6.220967e-02, 'ffn2_w_gu': 2.590583e-02, 'ffn2_w_down': 4.234451e-02, 'final_norm': 3.200092e+01}


def _to_microbatches(a, axis):
    t = _jnp.moveaxis(a, axis, 0)
    t = t.reshape((N_MICROBATCH, t.shape[0] // N_MICROBATCH) + t.shape[1:])
    return _jnp.moveaxis(t, 1, axis + 1)


def setup_inputs(seed: int = 0) -> dict:
    inp = _fwd_setup_inputs(seed)
    key = _jax.random.fold_in(_jax.random.key(seed), 7919)
    shape, _ = _output_shape()
    out = dict(inp)
    out["loss_target"] = _jax.random.normal(_jax.random.fold_in(key, 0), shape, _jnp.float32)
    for i, name in enumerate(TWIN_WEIGHTS):
        w = inp[name].astype(_jnp.float32)
        if MOMENT_SCALE is None:
            s = _jnp.sqrt(_jnp.mean(_jnp.square(w)) + 1e-30)
        else:
            s = MOMENT_SCALE[name]
        km, kv = _jax.random.split(_jax.random.fold_in(key, i + 1))
        out[name] = w
        out["m_" + name] = s * _jax.random.normal(km, w.shape, _jnp.float32)
        out["v_" + name] = (s * s) * _jax.random.uniform(kv, w.shape, _jnp.float32, 0.5, 1.5)
    if N_MICROBATCH > 1:
        for name, axis in PER_EXAMPLE_BATCH_AXIS.items():
            out[name] = _to_microbatches(out[name], axis)
    return {'x': out['x'], 'meta_tokens': out['meta_tokens'], 'ffn1_norm': out['ffn1_norm'], 'ffn1_w_gu': out['ffn1_w_gu'], 'ffn1_w_down': out['ffn1_w_down'], 'mix_norm': out['mix_norm'], 'w_in': out['w_in'], 'ssd_conv_w': out['ssd_conv_w'], 'ssd_conv_b': out['ssd_conv_b'], 'ssd_dt_bias': out['ssd_dt_bias'], 'ssd_a_log': out['ssd_a_log'], 'ssd_d': out['ssd_d'], 'ssd_norm': out['ssd_norm'], 'hg_lower_bound': out['hg_lower_bound'], 'hg_norm': out['hg_norm'], 'w_branch_a': out['w_branch_a'], 'w_branch_b': out['w_branch_b'], 'w_out': out['w_out'], 'ffn2_norm': out['ffn2_norm'], 'ffn2_w_gu': out['ffn2_w_gu'], 'ffn2_w_down': out['ffn2_w_down'], 'final_norm': out['final_norm'], 'loss_target': out['loss_target'], 'm_meta_tokens': out['m_meta_tokens'], 'm_ffn1_norm': out['m_ffn1_norm'], 'm_ffn1_w_gu': out['m_ffn1_w_gu'], 'm_ffn1_w_down': out['m_ffn1_w_down'], 'm_mix_norm': out['m_mix_norm'], 'm_w_in': out['m_w_in'], 'm_ssd_conv_w': out['m_ssd_conv_w'], 'm_ssd_conv_b': out['m_ssd_conv_b'], 'm_ssd_dt_bias': out['m_ssd_dt_bias'], 'm_ssd_a_log': out['m_ssd_a_log'], 'm_ssd_d': out['m_ssd_d'], 'm_ssd_norm': out['m_ssd_norm'], 'm_hg_lower_bound': out['m_hg_lower_bound'], 'm_hg_norm': out['m_hg_norm'], 'm_w_branch_a': out['m_w_branch_a'], 'm_w_branch_b': out['m_w_branch_b'], 'm_w_out': out['m_w_out'], 'm_ffn2_norm': out['m_ffn2_norm'], 'm_ffn2_w_gu': out['m_ffn2_w_gu'], 'm_ffn2_w_down': out['m_ffn2_w_down'], 'm_final_norm': out['m_final_norm'], 'v_meta_tokens': out['v_meta_tokens'], 'v_ffn1_norm': out['v_ffn1_norm'], 'v_ffn1_w_gu': out['v_ffn1_w_gu'], 'v_ffn1_w_down': out['v_ffn1_w_down'], 'v_mix_norm': out['v_mix_norm'], 'v_w_in': out['v_w_in'], 'v_ssd_conv_w': out['v_ssd_conv_w'], 'v_ssd_conv_b': out['v_ssd_conv_b'], 'v_ssd_dt_bias': out['v_ssd_dt_bias'], 'v_ssd_a_log': out['v_ssd_a_log'], 'v_ssd_d': out['v_ssd_d'], 'v_ssd_norm': out['v_ssd_norm'], 'v_hg_lower_bound': out['v_hg_lower_bound'], 'v_hg_norm': out['v_hg_norm'], 'v_w_branch_a': out['v_w_branch_a'], 'v_w_branch_b': out['v_w_branch_b'], 'v_w_out': out['v_w_out'], 'v_ffn2_norm': out['v_ffn2_norm'], 'v_ffn2_w_gu': out['v_ffn2_w_gu'], 'v_ffn2_w_down': out['v_ffn2_w_down'], 'v_final_norm': out['v_final_norm']}


def _loss(weights, diff, rest, loss_target):
    with _jax.named_scope("forward"):
        args = {**rest, TWIN_DIFF_INPUT: diff, **{k: w.astype(_WEIGHT_DTYPES[k]) for k, w in weights.items()}}
        y = _forward(args)
    with _jax.named_scope("loss_head"):
        err = _jnp.square(y.astype(_jnp.float32) - loss_target)
        return 0.5 * _jnp.sum(_jnp.mean(err, axis=-1)) if err.ndim else 0.5 * err


def _adamw(w, g, m, v):
    m = ADAM_B1 * m + (1.0 - ADAM_B1) * g
    v = ADAM_B2 * v + (1.0 - ADAM_B2) * _jnp.square(g)
    m_hat = m / (1.0 - ADAM_B1 ** ADAM_STEP)
    v_hat = v / (1.0 - ADAM_B2 ** ADAM_STEP)
    delta = -ADAM_LR * (m_hat / (_jnp.sqrt(v_hat) + ADAM_EPS) + ADAM_WD * w)
    return delta, m, v


def reference(x, meta_tokens, ffn1_norm, ffn1_w_gu, ffn1_w_down, mix_norm, w_in, ssd_conv_w, ssd_conv_b, ssd_dt_bias, ssd_a_log, ssd_d, ssd_norm, hg_lower_bound, hg_norm, w_branch_a, w_branch_b, w_out, ffn2_norm, ffn2_w_gu, ffn2_w_down, final_norm, loss_target, m_meta_tokens, m_ffn1_norm, m_ffn1_w_gu, m_ffn1_w_down, m_mix_norm, m_w_in, m_ssd_conv_w, m_ssd_conv_b, m_ssd_dt_bias, m_ssd_a_log, m_ssd_d, m_ssd_norm, m_hg_lower_bound, m_hg_norm, m_w_branch_a, m_w_branch_b, m_w_out, m_ffn2_norm, m_ffn2_w_gu, m_ffn2_w_down, m_final_norm, v_meta_tokens, v_ffn1_norm, v_ffn1_w_gu, v_ffn1_w_down, v_mix_norm, v_w_in, v_ssd_conv_w, v_ssd_conv_b, v_ssd_dt_bias, v_ssd_a_log, v_ssd_d, v_ssd_norm, v_hg_lower_bound, v_hg_norm, v_w_branch_a, v_w_branch_b, v_w_out, v_ffn2_norm, v_ffn2_w_gu, v_ffn2_w_down, v_final_norm):
    given = dict(x=x, meta_tokens=meta_tokens, ffn1_norm=ffn1_norm, ffn1_w_gu=ffn1_w_gu, ffn1_w_down=ffn1_w_down, mix_norm=mix_norm, w_in=w_in, ssd_conv_w=ssd_conv_w, ssd_conv_b=ssd_conv_b, ssd_dt_bias=ssd_dt_bias, ssd_a_log=ssd_a_log, ssd_d=ssd_d, ssd_norm=ssd_norm, hg_lower_bound=hg_lower_bound, hg_norm=hg_norm, w_branch_a=w_branch_a, w_branch_b=w_branch_b, w_out=w_out, ffn2_norm=ffn2_norm, ffn2_w_gu=ffn2_w_gu, ffn2_w_down=ffn2_w_down, final_norm=final_norm, loss_target=loss_target, m_meta_tokens=m_meta_tokens, m_ffn1_norm=m_ffn1_norm, m_ffn1_w_gu=m_ffn1_w_gu, m_ffn1_w_down=m_ffn1_w_down, m_mix_norm=m_mix_norm, m_w_in=m_w_in, m_ssd_conv_w=m_ssd_conv_w, m_ssd_conv_b=m_ssd_conv_b, m_ssd_dt_bias=m_ssd_dt_bias, m_ssd_a_log=m_ssd_a_log, m_ssd_d=m_ssd_d, m_ssd_norm=m_ssd_norm, m_hg_lower_bound=m_hg_lower_bound, m_hg_norm=m_hg_norm, m_w_branch_a=m_w_branch_a, m_w_branch_b=m_w_branch_b, m_w_out=m_w_out, m_ffn2_norm=m_ffn2_norm, m_ffn2_w_gu=m_ffn2_w_gu, m_ffn2_w_down=m_ffn2_w_down, m_final_norm=m_final_norm, v_meta_tokens=v_meta_tokens, v_ffn1_norm=v_ffn1_norm, v_ffn1_w_gu=v_ffn1_w_gu, v_ffn1_w_down=v_ffn1_w_down, v_mix_norm=v_mix_norm, v_w_in=v_w_in, v_ssd_conv_w=v_ssd_conv_w, v_ssd_conv_b=v_ssd_conv_b, v_ssd_dt_bias=v_ssd_dt_bias, v_ssd_a_log=v_ssd_a_log, v_ssd_d=v_ssd_d, v_ssd_norm=v_ssd_norm, v_hg_lower_bound=v_hg_lower_bound, v_hg_norm=v_hg_norm, v_w_branch_a=v_w_branch_a, v_w_branch_b=v_w_branch_b, v_w_out=v_w_out, v_ffn2_norm=v_ffn2_norm, v_ffn2_w_gu=v_ffn2_w_gu, v_ffn2_w_down=v_ffn2_w_down, v_final_norm=v_final_norm)
    weights = {n: given[n] for n in TWIN_WEIGHTS}
    shared = {n: given[n] for n in SHARED_INPUTS}
    per_example = {n: given[n] for n in ['x']}
    grad_fn = _jax.value_and_grad(_loss, argnums=(0, 1))

    def one_microbatch(ex, loss_target):
        ex = dict(ex)
        diff = ex.pop(TWIN_DIFF_INPUT)
        return grad_fn(weights, diff, {**shared, **ex}, loss_target)

    if N_MICROBATCH == 1:
        loss, (grad_w, grad_x) = one_microbatch(per_example, given["loss_target"])
    else:
        def body(carry, xs):
            loss_sum, grad_sum = carry
            l_k, (gw_k, gx_k) = one_microbatch(xs[0], xs[1])
            with _jax.named_scope("update"):
                return (loss_sum + l_k, _jax.tree.map(_jnp.add, grad_sum, gw_k)), gx_k

        init = (_jnp.zeros((), _jnp.float32), _jax.tree.map(_jnp.zeros_like, weights))
        (loss, grad_w), grad_x = _jax.lax.scan(body, init, (per_example, given["loss_target"]))
    with _jax.named_scope("update"):
        delta_w, new_m, new_v = {}, {}, {}
        for n in TWIN_WEIGHTS:
            delta_w[n], new_m[n], new_v[n] = _adamw(weights[n], grad_w[n], given["m_" + n], given["v_" + n])
    return (loss, grad_x, *[grad_w[n] for n in TWIN_WEIGHTS], *[delta_w[n] for n in TWIN_WEIGHTS],
            *[new_m[n] for n in TWIN_WEIGHTS], *[new_v[n] for n in TWIN_WEIGHTS])
```

```python
import functools

import jax
import jax.numpy as jnp
from jax import lax
from jax.experimental import pallas as pl
from jax.experimental.pallas import tpu as pltpu

F32, BF16 = jnp.float32, jnp.bfloat16
NN, NT, TN = ((1,), (0,)), ((1,), (1,)), ((0,), (0,))
HI = lax.Precision.HIGHEST
MESH_AXES = ("x", "y", "c")
N_DEV = 8

D_MODEL = 1024
N_META = 16
EPS = 1e-6
SSD_HEADS, SSD_HEAD_DIM, SSD_GROUPS, SSD_STATE, SSD_CONV, Q = 16, 64, 4, 128, 4, 128
SSD_INNER = SSD_HEADS * SSD_HEAD_DIM
SSD_CONV_CH = SSD_INNER + 2 * SSD_GROUPS * SSD_STATE
HG_WIDTH, HG_HEADS, HG_CHUNK = 1024, 8, 16
PAD = Q - N_META
N_MAIN = 9 * 1024
ADAM_LR, ADAM_B1, ADAM_B2, ADAM_EPS, ADAM_WD, ADAM_STEP = 0.001, 0.9, 0.999, 1e-08, 0.01, 10
VMEM_LIMIT = 52 * 1024 * 1024


def _dot(a, b, dims, prec=None):
    return lax.dot_general(a, b, (dims, ((), ())), precision=prec, preferred_element_type=F32)


def _sigmoid(x):
    return 1.0 / (1.0 + jnp.exp(-x))


def _dsilu(x, s):
    return s * (1.0 + x * (1.0 - s))


def _softplus(x):
    e = jnp.exp(-jnp.abs(x))
    u = 1.0 + e
    log1p_e = jnp.where(u == 1.0, e, jnp.log(u) * e / (u - 1.0))
    return jnp.maximum(x, 0.0) + log1p_e


def _params(sem):
    return pltpu.CompilerParams(dimension_semantics=sem, vmem_limit_bytes=VMEM_LIMIT)


def _tile(n, prefs):
    for p in prefs:
        if n % p == 0:
            return p
    return n


def _fused_matmul(name, M, N, K, pairs, extras, epilogue, out_dtypes, n_acc, tm, tn, tk, outer="i"):
    nk = K // tk
    n_pairs, n_ex, n_out = len(pairs), len(extras), len(out_dtypes)

    def ij(g0, g1):
        return (g0, g1) if outer == "i" else (g1, g0)

    in_specs, args = [], []
    for p in pairs:
        ao, bk, bn = p.get("a_off", 0), p.get("bk_off", 0), p.get("bn_off", 0)
        in_specs.append(pl.BlockSpec((tm, tk), lambda g0, g1, k, ao=ao: (ij(g0, g1)[0], k + ao)))
        if p.get("trans_b"):
            in_specs.append(pl.BlockSpec((tn, tk), lambda g0, g1, k, bk=bk, bn=bn: (ij(g0, g1)[1] + bn, k + bk)))
        else:
            in_specs.append(pl.BlockSpec((tk, tn), lambda g0, g1, k, bk=bk, bn=bn: (k + bk, ij(g0, g1)[1] + bn)))
        args += [p["a"], p["b"]]
    for arr, off in extras:
        in_specs.append(pl.BlockSpec((tm, tn), lambda g0, g1, k, off=off: (ij(g0, g1)[0], ij(g0, g1)[1] + off)))
        args.append(arr)
    out_specs = [pl.BlockSpec((tm, tn), lambda g0, g1, k: ij(g0, g1)) for _ in out_dtypes]
    out_shape = [jax.ShapeDtypeStruct((M, N), dt) for dt in out_dtypes]
    grid = (M // tm, N // tn, nk) if outer == "i" else (N // tn, M // tm, nk)

    def partials(refs):
        accs = [None] * n_acc
        for idx, p in enumerate(pairs):
            d = _dot(refs[2 * idx][...], refs[2 * idx + 1][...], NT if p.get("trans_b") else NN)
            accs[p["acc"]] = d if accs[p["acc"]] is None else accs[p["acc"]] + d
        return accs

    def finish(accs, refs):
        ex = [r[...] for r in refs[2 * n_pairs:2 * n_pairs + n_ex]]
        outs = refs[2 * n_pairs + n_ex:2 * n_pairs + n_ex + n_out]
        for o, r in zip(outs, epilogue(accs, ex)):
            o[...] = r.astype(o.dtype)

    if nk == 1:
        def body(*refs):
            finish(partials(refs), refs)
        scratch = []
    else:
        def body(*refs):
            acc_refs = refs[-n_acc:]
            k = pl.program_id(2)
            new = partials(refs)

            @pl.when(k == 0)
            def _():
                for a, v in zip(acc_refs, new):
                    a[...] = v

            @pl.when(k > 0)
            def _():
                for a, v in zip(acc_refs, new):
                    a[...] += v

            @pl.when(k == nk - 1)
            def _():
                finish([a[...] for a in acc_refs], refs)
        scratch = [pltpu.VMEM((tm, tn), F32) for _ in range(n_acc)]

    return pl.pallas_call(
        body, name=name, grid=grid, in_specs=in_specs, out_specs=out_specs, out_shape=out_shape,
        scratch_shapes=scratch, compiler_params=_params(("parallel", "parallel", "arbitrary")),
    )(*args)


def _matmul_tn(name, x, y, t1, t2, tr, scale=1.0):
    R, K1 = x.shape
    N1 = y.shape[1]
    nr = R // tr

    def body(x_ref, y_ref, o_ref):
        r = pl.program_id(2)
        d = _dot(x_ref[...], y_ref[...], TN)

        @pl.when(r == 0)
        def _():
            o_ref[...] = d

        @pl.when(r > 0)
        def _():
            o_ref[...] += d

        if scale != 1.0:
            @pl.when(r == nr - 1)
            def _():
                o_ref[...] = o_ref[...] * scale

    return pl.pallas_call(
        body, name=name, grid=(K1 // t1, N1 // t2, nr),
        in_specs=[pl.BlockSpec((tr, t1), lambda i, j, r: (r, i)), pl.BlockSpec((tr, t2), lambda i, j, r: (r, j))],
        out_specs=pl.BlockSpec((t1, t2), lambda i, j, r: (i, j)),
        out_shape=jax.ShapeDtypeStruct((K1, N1), F32),
        compiler_params=_params(("parallel", "parallel", "arbitrary")),
    )(x, y)


def _rmsnorm_fwd(name, h, w):
    M, D = h.shape
    tm = _tile(M, (544, 256, 128))

    def body(h_ref, w_ref, o_ref):
        x = h_ref[...]
        r = lax.rsqrt(jnp.mean(x * x, axis=-1, keepdims=True) + EPS)
        o_ref[...] = (x * r * w_ref[...]).astype(o_ref.dtype)

    return pl.pallas_call(
        body, name=name, grid=(M // tm,),
        in_specs=[pl.BlockSpec((tm, D), lambda i: (i, 0)), pl.BlockSpec((1, D), lambda i: (0, 0))],
        out_specs=pl.BlockSpec((tm, D), lambda i: (i, 0)),
        out_shape=jax.ShapeDtypeStruct((M, D), BF16), compiler_params=_params(("parallel",)),
    )(h, w)


def _rmsnorm_bwd(name, dn, h, w, dh_in):
    M, D = h.shape
    tm = _tile(M, (544, 256, 128))

    def body(dn_ref, h_ref, w_ref, dhi_ref, dh_ref, dhb_ref, dw_ref):
        x = h_ref[...]
        r = lax.rsqrt(jnp.mean(x * x, axis=-1, keepdims=True) + EPS)
        xhat = x * r
        dn_v = dn_ref[...]
        gw = dn_v * w_ref[...]
        dx = r * (gw - xhat * jnp.mean(gw * xhat, axis=-1, keepdims=True))
        dh = dhi_ref[...] + dx
        dh_ref[...] = dh
        dhb_ref[...] = dh.astype(BF16)
        dw = jnp.sum(dn_v * xhat, axis=0, keepdims=True)

        @pl.when(pl.program_id(0) == 0)
        def _():
            dw_ref[...] = dw

        @pl.when(pl.program_id(0) > 0)
        def _():
            dw_ref[...] += dw

    row = pl.BlockSpec((tm, D), lambda i: (i, 0))
    vec = pl.BlockSpec((1, D), lambda i: (0, 0))
    return pl.pallas_call(
        body, name=name, grid=(M // tm,), in_specs=[row, row, vec, row], out_specs=[row, row, vec],
        out_shape=[jax.ShapeDtypeStruct((M, D), F32), jax.ShapeDtypeStruct((M, D), BF16), jax.ShapeDtypeStruct((1, D), F32)],
        compiler_params=_params(("arbitrary",)),
    )(dn, h, w, dh_in)


def _loss_head(h, w, target, Bl, nb):
    M, D = h.shape

    def body(h_ref, w_ref, t_ref, dh_ref, dhb_ref, dw_ref, loss_ref):
        b, t = pl.program_id(0), pl.program_id(1)
        live = (t > 0).astype(F32)
        x = h_ref[...]
        r = lax.rsqrt(jnp.mean(x * x, axis=-1, keepdims=True) + EPS)
        xhat = x * r
        wv = w_ref[...]
        err = (xhat * wv - t_ref[0]) * live
        dy = err * (1.0 / D)
        gw = dy * wv
        dx = r * (gw - xhat * jnp.mean(gw * xhat, axis=-1, keepdims=True))
        dh_ref[...] = dx
        dhb_ref[...] = dx.astype(BF16)
        dw = jnp.sum(dy * xhat, axis=0, keepdims=True)
        part = 0.5 * jnp.sum(jnp.sum(err * err, axis=-1, keepdims=True) * (1.0 / D), axis=0, keepdims=True)
        first = jnp.logical_and(b == 0, t == 0)

        @pl.when(first)
        def _():
            dw_ref[...] = dw
            loss_ref[...] = jnp.broadcast_to(part, loss_ref.shape)

        @pl.when(jnp.logical_not(first))
        def _():
            dw_ref[...] += dw
            loss_ref[...] += jnp.broadcast_to(part, loss_ref.shape)

    row = pl.BlockSpec((Q, D), lambda b, t: (b * nb + t, 0))
    vec = pl.BlockSpec((1, D), lambda b, t: (0, 0))
    return pl.pallas_call(
        body, name="loss_head", grid=(Bl, nb),
        in_specs=[row, vec, pl.BlockSpec((1, Q, D), lambda b, t: (b, jnp.maximum(t - 1, 0), 0))],
        out_specs=[row, row, vec, pl.BlockSpec((8, 128), lambda b, t: (0, 0))],
        out_shape=[jax.ShapeDtypeStruct((M, D), F32), jax.ShapeDtypeStruct((M, D), BF16),
                   jax.ShapeDtypeStruct((1, D), F32), jax.ShapeDtypeStruct((8, 128), F32)],
        compiler_params=_params(("arbitrary", "arbitrary")),
    )(h, w, target)


CONV_TC = 256


def _conv_pre(xr_ref, w_ref, b_ref):
    x = xr_ref[...].astype(F32)
    acc = b_ref[...] + w_ref[SSD_CONV - 1:SSD_CONV, :] * x
    for k in range(1, SSD_CONV):
        acc = acc + w_ref[SSD_CONV - 1 - k:SSD_CONV - k, :] * pltpu.roll(x, k, 0)
    return x, acc


def _conv_fwd(proj, w, b, Bl, T):
    M = proj.shape[0]
    off = 1024 // CONV_TC

    def body(xr_ref, w_ref, b_ref, o_ref):
        _, acc = _conv_pre(xr_ref, w_ref, b_ref)
        row = lax.broadcasted_iota(jnp.int32, acc.shape, 0)
        o_ref[...] = jnp.where(row >= PAD, acc * _sigmoid(acc), 0.0).astype(o_ref.dtype)

    return pl.pallas_call(
        body, name="conv_fwd", grid=(Bl, SSD_CONV_CH // CONV_TC),
        in_specs=[pl.BlockSpec((T, CONV_TC), lambda bb, j: (bb, j + off)),
                  pl.BlockSpec((SSD_CONV, CONV_TC), lambda bb, j: (0, j)), pl.BlockSpec((1, CONV_TC), lambda bb, j: (0, j))],
        out_specs=pl.BlockSpec((T, CONV_TC), lambda bb, j: (bb, j)),
        out_shape=jax.ShapeDtypeStruct((M, SSD_CONV_CH), BF16), compiler_params=_params(("parallel", "parallel")),
    )(proj, w, b)


def _conv_bwd(proj, w, b, dxc, Bl, T):
    M = proj.shape[0]
    off = 1024 // CONV_TC

    def body(xr_ref, w_ref, b_ref, d_ref, dx_ref, dw_ref, db_ref):
        x, acc = _conv_pre(xr_ref, w_ref, b_ref)
        row = lax.broadcasted_iota(jnp.int32, acc.shape, 0)
        s = _sigmoid(acc)
        dpre = jnp.where(row >= PAD, d_ref[...].astype(F32) * _dsilu(acc, s), 0.0)
        dx = w_ref[SSD_CONV - 1:SSD_CONV, :] * dpre
        dws = [jnp.sum(dpre * x, axis=0, keepdims=True)]
        for k in range(1, SSD_CONV):
            dx = dx + w_ref[SSD_CONV - 1 - k:SSD_CONV - k, :] * pltpu.roll(dpre, T - k, 0)
            dws.append(jnp.sum(dpre * pltpu.roll(x, k, 0), axis=0, keepdims=True))
        dx_ref[...] = dx.astype(dx_ref.dtype)
        dw = jnp.concatenate(dws[::-1], axis=0)
        db = jnp.sum(dpre, axis=0, keepdims=True)

        @pl.when(pl.program_id(1) == 0)
        def _():
            dw_ref[...] = dw
            db_ref[...] = db

        @pl.when(pl.program_id(1) > 0)
        def _():
            dw_ref[...] += dw
            db_ref[...] += db

    return pl.pallas_call(
        body, name="conv_bwd", grid=(SSD_CONV_CH // CONV_TC, Bl),
        in_specs=[pl.BlockSpec((T, CONV_TC), lambda j, bb: (bb, j + off)),
                  pl.BlockSpec((SSD_CONV, CONV_TC), lambda j, bb: (0, j)), pl.BlockSpec((1, CONV_TC), lambda j, bb: (0, j)),
                  pl.BlockSpec((T, CONV_TC), lambda j, bb: (bb, j))],
        out_specs=[pl.BlockSpec((T, CONV_TC), lambda j, bb: (bb, j)),
                   pl.BlockSpec((SSD_CONV, CONV_TC), lambda j, bb: (0, j)), pl.BlockSpec((1, CONV_TC), lambda j, bb: (0, j))],
        out_shape=[jax.ShapeDtypeStruct((M, SSD_CONV_CH), BF16), jax.ShapeDtypeStruct((SSD_CONV, SSD_CONV_CH), F32),
                   jax.ShapeDtypeStruct((1, SSD_CONV_CH), F32)],
        compiler_params=_params(("parallel", "arbitrary")),
    )(proj, w, b, dxc)


def _ssd_setup(c, dtr_ref, bias_ref, alog_ref):
    row = lax.broadcasted_iota(jnp.int32, (Q, 128), 0)
    col = lax.broadcasted_iota(jnp.int32, (Q, 128), 1)
    valid = jnp.logical_or(c > 0, row >= PAD)
    pre = dtr_ref[...] + bias_ref[...]
    dt = jnp.where(valid, _softplus(pre), 0.0)
    A = -jnp.exp(alog_ref[...])
    tri = row >= col
    cs = _dot(tri.astype(F32), dt * A, NN, HI)
    cst = _dot((row == col).astype(F32), cs, NT, HI)
    return dt, A, cs, cst, tri, valid, pre, row, col


def _pair_terms(p, dt, cs, col):
    h0, h1 = 2 * p, 2 * p + 1
    first = col < SSD_HEAD_DIM
    pick = lambda a: jnp.where(first, a[:, h0:h0 + 1], a[:, h1:h1 + 1])
    cl0, cl1 = cs[Q - 1:Q, h0:h0 + 1], cs[Q - 1:Q, h1:h1 + 1]
    cs_p = pick(cs)
    cl_p = jnp.where(first[0:1], cl0, cl1)
    return first, pick(dt), jnp.exp(cs_p), jnp.exp(cl_p - cs_p), (cl0, cl1)


def _ssd_core(x_ref, b_ref, c_ref, d_ref, state_of, dt, cs, cst, tri, row, col):
    xv = x_ref[...].astype(F32)
    Bg, Cg = b_ref[...], c_ref[...]
    CB = _dot(Cg, Bg, NT)
    ys, new_states, Ms = [], [], []
    for p in range(2):
        first, dt_p, ecs_p, decay_p, (cl0, cl1) = _pair_terms(p, dt, cs, col)
        x_p = xv[:, 128 * p:128 * (p + 1)]
        X_p = x_p * dt_p
        yd = jnp.zeros((Q, 128), F32)
        for hh in range(2):
            h = 2 * p + hh
            Lm = jnp.exp(jnp.where(tri, cs[:, h:h + 1] - cst[h:h + 1, :], -jnp.inf))
            Mh = CB * Lm
            Ms.append(Mh)
            Xm = jnp.where(first if hh == 0 else jnp.logical_not(first), X_p, 0.0).astype(BF16)
            yd = yd + _dot(Mh.astype(BF16), Xm, NN)
        prev = state_of(p)
        yo = _dot(Cg, prev.astype(BF16), NT) * ecs_p
        st = _dot((X_p * decay_p).astype(BF16), Bg, TN)
        ecl_rows = jnp.where(row < SSD_HEAD_DIM, jnp.exp(cl0), jnp.exp(cl1))
        new_states.append(prev * ecl_rows + st)
        d_p = jnp.where(first[0:1], d_ref[:, 2 * p:2 * p + 1], d_ref[:, 2 * p + 1:2 * p + 2])
        ys.append(yd + yo + x_p * d_p)
    return jnp.concatenate(ys, axis=1), new_states, Ms, CB, xv


def _ssd_specs(nc):
    rb = lambda g, b, c: b * nc + c
    return [
        pl.BlockSpec((Q, 256), lambda g, b, c: (rb(g, b, c), g)),
        pl.BlockSpec((Q, 128), lambda g, b, c: (rb(g, b, c), 8 + g)),
        pl.BlockSpec((Q, 128), lambda g, b, c: (rb(g, b, c), 12 + g)),
        pl.BlockSpec((Q, 128), lambda g, b, c: (rb(g, b, c), g)),
        pl.BlockSpec((Q, 256), lambda g, b, c: (rb(g, b, c), g)),
        pl.BlockSpec((1, 128), lambda g, b, c: (0, g)),
        pl.BlockSpec((1, 128), lambda g, b, c: (0, g)),
        pl.BlockSpec((1, 128), lambda g, b, c: (0, g)),
        pl.BlockSpec((1, 256), lambda g, b, c: (0, g)),
    ]


def _ssd_fwd(xc, dtr, proj, bias_p, alog_p, d_p, nw, Bl, nc):
    M = xc.shape[0]

    def body(x_ref, b_ref, c_ref, dtr_ref, z_ref, bias_ref, alog_ref, d_ref, nw_ref, y_ref, prev_ref, state):
        c = pl.program_id(2)

        @pl.when(c == 0)
        def _():
            state[...] = jnp.zeros_like(state)

        dt, _, cs, cst, tri, _, _, row, col = _ssd_setup(c, dtr_ref, bias_ref, alog_ref)
        y, new_states, _, _, _ = _ssd_core(x_ref, b_ref, c_ref, d_ref, lambda p: state[p], dt, cs, cst, tri, row, col)
        for p in range(2):
            prev_ref[0, 0, 0, p] = state[p]
            state[p] = new_states[p]
        zz = z_ref[...].astype(F32)
        yg = y * zz * _sigmoid(zz)
        r = lax.rsqrt(jnp.mean(yg * yg, axis=-1, keepdims=True) + EPS)
        y_ref[...] = (yg * r * nw_ref[...]).astype(y_ref.dtype)

    return pl.pallas_call(
        body, name="ssd_fwd", grid=(SSD_GROUPS, Bl, nc), in_specs=_ssd_specs(nc),
        out_specs=[pl.BlockSpec((Q, 256), lambda g, b, c: (b * nc + c, g)),
                   pl.BlockSpec((1, 1, 1, 2, 128, 128), lambda g, b, c: (b, g, c, 0, 0, 0))],
        out_shape=[jax.ShapeDtypeStruct((M, SSD_INNER), BF16), jax.ShapeDtypeStruct((Bl, SSD_GROUPS, nc, 2, 128, 128), F32)],
        scratch_shapes=[pltpu.VMEM((2, 128, 128), F32)],
        compiler_params=_params(("parallel", "arbitrary", "arbitrary")),
    )(xc, xc, xc, dtr, proj, bias_p, alog_p, d_p, nw)


def _ssd_bwd(xc, dtr, proj, bias_p, alog_p, d_p, nw, prev, dya, Bl, nc):
    M = xc.shape[0]
    rev = lambda spec: pl.BlockSpec(spec.block_shape, lambda g, b, c, f=spec.index_map: f(g, b, nc - 1 - c))

    def body(x_ref, b_ref, c_ref, dtr_ref, z_ref, bias_ref, alog_ref, d_ref, nw_ref, prev_ref, dy_ref,
             dx_ref, dB_ref, dC_ref, dz_ref, ddtr_ref, dbias_ref, dalog_ref, dd_ref, dnw_ref, dS):
        b, t = pl.program_id(1), pl.program_id(2)
        c = nc - 1 - t

        @pl.when(t == 0)
        def _():
            dS[...] = jnp.zeros_like(dS)

        dt, A, cs, cst, tri, valid, pre, row, col = _ssd_setup(c, dtr_ref, bias_ref, alog_ref)
        y, _, Ms, CB, xv = _ssd_core(x_ref, b_ref, c_ref, d_ref, lambda p: prev_ref[0, 0, 0, p], dt, cs, cst, tri, row, col)
        Bg, Cg = b_ref[...], c_ref[...]
        Bf = Bg.astype(F32)

        zz = z_ref[...].astype(F32)
        sz = _sigmoid(zz)
        silu_z = zz * sz
        yg = y * silu_z
        r = lax.rsqrt(jnp.mean(yg * yg, axis=-1, keepdims=True) + EPS)
        xhat = yg * r
        dout = dy_ref[...].astype(F32)
        gw = dout * nw_ref[...]
        dyg = r * (gw - xhat * jnp.mean(gw * xhat, axis=-1, keepdims=True))
        dnw = jnp.sum(dout * xhat, axis=0, keepdims=True)
        dz_ref[...] = (dyg * y * _dsilu(zz, sz)).astype(dz_ref.dtype)
        dy = dyg * silu_z

        lane1 = lax.broadcasted_iota(jnp.int32, (1, 128), 1)
        put_col = lambda h, v: jnp.where(col == h, v, 0.0)
        put_lane = lambda h, v: jnp.where(lane1 == h, v, 0.0)
        dcs = jnp.zeros((Q, 128), F32)
        dcs_t = jnp.zeros((128, Q), F32)
        dcl = jnp.zeros((1, 128), F32)
        ddt = jnp.zeros((Q, 128), F32)
        dD = jnp.zeros((1, 128), F32)
        dCB = jnp.zeros((Q, Q), F32)
        dBacc = jnp.zeros((Q, 128), F32)
        dCacc = jnp.zeros((Q, 128), F32)
        dxs = []
        for p in range(2):
            first, dt_p, ecs_p, decay_p, (cl0, cl1) = _pair_terms(p, dt, cs, col)
            halves = (first, jnp.logical_not(first))
            x_p = xv[:, 128 * p:128 * (p + 1)]
            X_p = x_p * dt_p
            dy_p = dy[:, 128 * p:128 * (p + 1)]
            prev_p = prev_ref[0, 0, 0, p]
            prev_b = prev_p.astype(BF16)
            dS_p = dS[p]
            dS_b = dS_p.astype(BF16)
            dX = decay_p * _dot(Bg, dS_b, NT)
            Yo = _dot(Cg, prev_b, NT)
            dYo = (dy_p * ecs_p).astype(BF16)
            dCacc = dCacc + _dot(dYo, prev_b, NN)
            dprev = _dot(dYo, Cg, TN)
            off_cs = dy_p * Yo * ecs_p
            state_cs = dS_p * prev_p
            for hh in range(2):
                h = 2 * p + hh
                hm = halves[hh]
                Mh = Ms[h]
                dyh = jnp.where(hm, dy_p, 0.0).astype(BF16)
                Xm = jnp.where(hm, X_p, 0.0).astype(BF16)
                dX = dX + _dot(Mh.astype(BF16), dyh, TN)
                dM = _dot(dyh, Xm, NT)
                W = dM * Mh
                Lm = jnp.exp(jnp.where(tri, cs[:, h:h + 1] - cst[h:h + 1, :], -jnp.inf))
                dCB = dCB + dM * Lm
                XdS = _dot(Xm, dS_b, NN)
                decay_h = decay_p[:, 64 * hh:64 * hh + 1]
                dBacc = dBacc + decay_h * XdS
                tdec = jnp.sum(XdS * Bf, axis=1, keepdims=True) * decay_h
                dcs = dcs + put_col(h, jnp.sum(W, axis=1, keepdims=True)
                                    + jnp.sum(jnp.where(hm, off_cs, 0.0), axis=1, keepdims=True) - tdec)
                dcs_t = dcs_t - jnp.where(lax.broadcasted_iota(jnp.int32, (128, Q), 0) == h,
                                          jnp.sum(W, axis=0, keepdims=True), 0.0)
                ecl = jnp.exp(cl0 if hh == 0 else cl1)
                rows_h = (row < SSD_HEAD_DIM) if hh == 0 else (row >= SSD_HEAD_DIM)
                dcl = dcl + put_lane(h, jnp.sum(tdec, axis=0, keepdims=True)
                                     + ecl * jnp.sum(jnp.sum(jnp.where(rows_h, state_cs, 0.0), axis=1, keepdims=True),
                                                     axis=0, keepdims=True))
                ddt = ddt + put_col(h, jnp.sum(jnp.where(hm, dX * x_p, 0.0), axis=1, keepdims=True))
                dD = dD + put_lane(h, jnp.sum(jnp.sum(jnp.where(hm, dy_p * x_p, 0.0), axis=1, keepdims=True),
                                              axis=0, keepdims=True))
            ecl_rows = jnp.where(row < SSD_HEAD_DIM, jnp.exp(cl0), jnp.exp(cl1))
            dS[p] = dS_p * ecl_rows + dprev
            d_pp = jnp.where(first[0:1], d_ref[:, 2 * p:2 * p + 1], d_ref[:, 2 * p + 1:2 * p + 2])
            dxs.append(dy_p * d_pp + dX * dt_p)
        dCB_b = dCB.astype(BF16)
        dCacc = dCacc + _dot(dCB_b, Bg, NN)
        dBacc = dBacc + _dot(dCB_b, Cg, TN)
        dx_ref[...] = jnp.concatenate(dxs, axis=1).astype(dx_ref.dtype)
        dB_ref[...] = dBacc.astype(dB_ref.dtype)
        dC_ref[...] = dCacc.astype(dC_ref.dtype)

        dcs = dcs + _dot((row == col).astype(F32), dcs_t, NT, HI) + jnp.where(row == Q - 1, dcl, 0.0)
        da = _dot((row <= col).astype(F32), dcs, NN, HI)
        ddt = ddt + da * A
        dpre = jnp.where(valid, ddt * _sigmoid(pre), 0.0)
        ddtr_ref[...] = dpre
        dbias = jnp.sum(dpre, axis=0, keepdims=True)
        dalog = jnp.sum(da * dt, axis=0, keepdims=True) * A
        first_step = jnp.logical_and(b == 0, t == 0)

        @pl.when(first_step)
        def _():
            dbias_ref[...] = dbias
            dalog_ref[...] = dalog
            dd_ref[...] = dD
            dnw_ref[...] = dnw

        @pl.when(jnp.logical_not(first_step))
        def _():
            dbias_ref[...] += dbias
            dalog_ref[...] += dalog
            dd_ref[...] += dD
            dnw_ref[...] += dnw

    fwd_specs = _ssd_specs(nc)
    in_specs = [rev(s) for s in fwd_specs] + [
        pl.BlockSpec((1, 1, 1, 2, 128, 128), lambda g, b, c: (b, g, nc - 1 - c, 0, 0, 0)),
        pl.BlockSpec((Q, 256), lambda g, b, c: (b * nc + nc - 1 - c, g))]
    rowblk = lambda w: pl.BlockSpec((Q, w), lambda g, b, c: (b * nc + nc - 1 - c, g))
    vec = lambda w: pl.BlockSpec((1, w), lambda g, b, c: (0, g))
    return pl.pallas_call(
        body, name="ssd_bwd", grid=(SSD_GROUPS, Bl, nc), in_specs=in_specs,
        out_specs=[rowblk(256), rowblk(128), rowblk(128), rowblk(256), rowblk(128), vec(128), vec(128), vec(128), vec(256)],
        out_shape=[jax.ShapeDtypeStruct((M, SSD_INNER), BF16), jax.ShapeDtypeStruct((M, 512), BF16),
                   jax.ShapeDtypeStruct((M, 512), BF16), jax.ShapeDtypeStruct((M, SSD_INNER), BF16),
                   jax.ShapeDtypeStruct((M, 512), F32), jax.ShapeDtypeStruct((1, 512), F32),
                   jax.ShapeDtypeStruct((1, 512), F32), jax.ShapeDtypeStruct((1, 512), F32),
                   jax.ShapeDtypeStruct((1, SSD_INNER), F32)],
        scratch_shapes=[pltpu.VMEM((2, 128, 128), F32)],
        compiler_params=_params(("parallel", "arbitrary", "arbitrary")),
    )(xc, xc, xc, dtr, proj, bias_p, alog_p, d_p, nw, prev, dya)


NSUB = Q // HG_CHUNK
EXP_CAP = 80.0


def _hg_setup(blk, q_ref, f_ref, hb_ref):
    row = lax.broadcasted_iota(jnp.int32, (Q, Q), 0)
    col = lax.broadcasted_iota(jnp.int32, (Q, Q), 1)
    same = (row // HG_CHUNK) == (col // HG_CHUNK)
    causal = jnp.logical_and(same, col <= row)
    lb = _sigmoid(hb_ref[0:1, :] - hb_ref[1:2, :])
    fl = f_ref[...].astype(F32)
    sg = _sigmoid(fl)
    fg = lb + (1.0 - lb) * sg
    k = (1.0 - lb) * (1.0 - sg)
    gl = jnp.log(fg)
    G = _dot(causal.astype(F32), gl, NN, HI)
    T = _dot(same.astype(F32), gl, NN, HI)
    qv = q_ref[...].astype(F32)
    sq = _sigmoid(qv)
    eG = jnp.exp(G)
    eGn = jnp.exp(jnp.minimum(-G, EXP_CAP))
    eTG = jnp.exp(T - G)
    qt = qv * sq * eG
    kt = k * eGn
    kh = k * eTG
    valid = jnp.logical_or(blk > 0, row >= PAD)
    return dict(row=row, col=col, same=same, causal=causal, lb=lb, sg=sg, fg=fg, k=k, T=T, qv=qv, sq=sq,
                eG=eG, eGn=eGn, eTG=eTG, qt=qt, kt=kt, kh=kh, valid=valid)


def _hg_specs(nb, rev=False):
    rb = (lambda h, b, t: b * nb + nb - 1 - t) if rev else (lambda h, b, t: b * nb + t)
    blk = lambda off: pl.BlockSpec((Q, 128), lambda h, b, t, off=off: (rb(h, b, t), off + h))
    return [blk(24), blk(32), blk(40), blk(48),
            pl.BlockSpec((2, 128), lambda h, b, t: (0, h)), pl.BlockSpec((1, 128), lambda h, b, t: (0, h))]


def _hgrn_fwd(proj, hb, nw, Bl, nb):
    M = proj.shape[0]

    def body(q_ref, f_ref, i_ref, g_ref, hb_ref, nw_ref, y_ref, o_ref, st_ref, S):
        blk = pl.program_id(2)

        @pl.when(blk == 0)
        def _():
            S[...] = jnp.zeros_like(S)

        s = _hg_setup(blk, q_ref, f_ref, hb_ref)
        v = i_ref[...]
        qt_b, kt_b, kh_b = s["qt"].astype(BF16), s["kt"].astype(BF16), s["kh"].astype(BF16)
        att = jnp.where(s["causal"], _dot(qt_b, kt_b, NT), 0.0)
        o_intra = _dot(att.astype(BF16), v, NN)
        for j in range(NSUB):
            sl = slice(HG_CHUNK * j, HG_CHUNK * (j + 1))
            St = S[...]
            st_ref[0, 0, 0, j] = St
            o_ref[sl, :] = o_intra[sl] + _dot(qt_b[sl], St.astype(BF16), NT)
            S[...] = St * jnp.exp(s["T"][HG_CHUNK * j:HG_CHUNK * j + 1, :]) + _dot(v[sl], kh_b[sl], TN)
        o = o_ref[...]
        r = lax.rsqrt(jnp.mean(o * o, axis=-1, keepdims=True) + EPS)
        gv = g_ref[...].astype(F32)
        y_ref[...] = (o * r * nw_ref[...] * gv * _sigmoid(gv)).astype(y_ref.dtype)

    rowblk = pl.BlockSpec((Q, 128), lambda h, b, t: (b * nb + t, h))
    return pl.pallas_call(
        body, name="hgrn_fwd", grid=(HG_HEADS, Bl, nb), in_specs=_hg_specs(nb),
        out_specs=[rowblk, rowblk, pl.BlockSpec((1, 1, 1, NSUB, 128, 128), lambda h, b, t: (b, h, t, 0, 0, 0))],
        out_shape=[jax.ShapeDtypeStruct((M, HG_WIDTH), BF16), jax.ShapeDtypeStruct((M, HG_WIDTH), F32),
                   jax.ShapeDtypeStruct((Bl, HG_HEADS, nb, NSUB, 128, 128), F32)],
        scratch_shapes=[pltpu.VMEM((128, 128), F32)],
        compiler_params=_params(("parallel", "arbitrary", "arbitrary")),
    )(proj, proj, proj, proj, hb, nw)


def _hgrn_bwd(proj, hb, nw, o_saved, st_saved, dyb, Bl, nb):
    M = proj.shape[0]

    def body(q_ref, f_ref, i_ref, g_ref, hb_ref, nw_ref, o_ref, st_ref, dy_ref,
             dq_ref, df_ref, di_ref, dg_ref, dhb_ref, dnw_ref, dS, a_dqt, a_dv, a_dkh, a_dgl):
        b, t = pl.program_id(1), pl.program_id(2)
        blk = nb - 1 - t

        @pl.when(t == 0)
        def _():
            dS[...] = jnp.zeros_like(dS)

        s = _hg_setup(blk, q_ref, f_ref, hb_ref)
        v = i_ref[...]
        qt_b, kt_b, kh_b = s["qt"].astype(BF16), s["kt"].astype(BF16), s["kh"].astype(BF16)
        att = jnp.where(s["causal"], _dot(qt_b, kt_b, NT), 0.0).astype(BF16)

        o = o_ref[...]
        r = lax.rsqrt(jnp.mean(o * o, axis=-1, keepdims=True) + EPS)
        xhat = o * r
        gv = g_ref[...].astype(F32)
        sgv = _sigmoid(gv)
        dyv = dy_ref[...].astype(F32)
        d_on = dyv * gv * sgv
        dg_out = dyv * xhat * nw_ref[...] * _dsilu(gv, sgv)
        gw = d_on * nw_ref[...]
        do = r * (gw - xhat * jnp.mean(gw * xhat, axis=-1, keepdims=True))
        dnw = jnp.sum(d_on * xhat, axis=0, keepdims=True)
        do_b = do.astype(BF16)

        datt = jnp.where(s["causal"], _dot(do_b, v, NT), 0.0).astype(BF16)
        dqt = _dot(datt, kt_b, NN)
        dkt = _dot(datt, qt_b, TN)
        dv = _dot(att, do_b, TN)
        last_row = (lax.broadcasted_iota(jnp.int32, (HG_CHUNK, 128), 0) == HG_CHUNK - 1)
        for j in reversed(range(NSUB)):
            sl = slice(HG_CHUNK * j, HG_CHUNK * (j + 1))
            St = st_ref[0, 0, 0, j]
            dSt = dS[...]
            St_b, dSt_b = St.astype(BF16), dSt.astype(BF16)
            eT = jnp.exp(s["T"][HG_CHUNK * j:HG_CHUNK * j + 1, :])
            dkh_j = _dot(v[sl], dSt_b, NN)
            a_dqt[sl, :] = _dot(do_b[sl], St_b, NN)
            a_dv[sl, :] = _dot(kh_b[sl], dSt_b, NT)
            a_dkh[sl, :] = dkh_j
            dlast = (jnp.sum(St * dSt, axis=0, keepdims=True) * eT
                     + jnp.sum(dkh_j * s["kh"][sl], axis=0, keepdims=True))
            a_dgl[sl, :] = jnp.where(last_row, dlast, 0.0)
            dS[...] = dSt * eT + _dot(do_b[sl], qt_b[sl], TN)
        dqt = dqt + a_dqt[...]
        dv = dv + a_dv[...]
        dkh = a_dkh[...]
        dG = dqt * s["qt"] - dkt * s["kt"] - dkh * s["kh"] + a_dgl[...]
        rev_causal = jnp.logical_and(s["same"], s["col"] >= s["row"])
        dgl = _dot(rev_causal.astype(F32), dG, NN, HI)
        dk = dkt * s["eGn"] + dkh * s["eTG"]
        dfg = dgl / s["fg"] - dk
        lb, sg = s["lb"], s["sg"]
        keep = s["valid"].astype(F32)
        df_ref[...] = (dfg * (1.0 - lb) * sg * (1.0 - sg) * keep).astype(df_ref.dtype)
        dq_ref[...] = (dqt * s["eG"] * _dsilu(s["qv"], s["sq"]) * keep).astype(dq_ref.dtype)
        di_ref[...] = (dv * keep).astype(di_ref.dtype)
        dg_ref[...] = (dg_out * keep).astype(dg_ref.dtype)
        dlb = jnp.sum(dfg * (1.0 - sg) * keep, axis=0, keepdims=True) * lb * (1.0 - lb)
        dhb = jnp.concatenate([dlb, -dlb], axis=0)
        first_step = jnp.logical_and(b == 0, t == 0)

        @pl.when(first_step)
        def _():
            dhb_ref[...] = dhb
            dnw_ref[...] = dnw

        @pl.when(jnp.logical_not(first_step))
        def _():
            dhb_ref[...] += dhb
            dnw_ref[...] += dnw

    rowblk = pl.BlockSpec((Q, 128), lambda h, b, t: (b * nb + nb - 1 - t, h))
    return pl.pallas_call(
        body, name="hgrn_bwd", grid=(HG_HEADS, Bl, nb),
        in_specs=_hg_specs(nb, rev=True) + [
            rowblk, pl.BlockSpec((1, 1, 1, NSUB, 128, 128), lambda h, b, t: (b, h, nb - 1 - t, 0, 0, 0)), rowblk],
        out_specs=[rowblk, rowblk, rowblk, rowblk,
                   pl.BlockSpec((2, 128), lambda h, b, t: (0, h)), pl.BlockSpec((1, 128), lambda h, b, t: (0, h))],
        out_shape=[jax.ShapeDtypeStruct((M, HG_WIDTH), BF16)] * 4 + [
            jax.ShapeDtypeStruct((2, HG_WIDTH), F32), jax.ShapeDtypeStruct((1, HG_WIDTH), F32)],
        scratch_shapes=[pltpu.VMEM((128, 128), F32)] * 5,
        compiler_params=_params(("parallel", "arbitrary", "arbitrary")),
    )(proj, proj, proj, proj, hb, nw, o_saved, st_saved, dyb)


def _exchange(name, arrays, scatter):
    n = len(arrays)

    def body(*refs):
        srcs, outs = refs[:n], refs[n:2 * n]
        send_sems, recv_sems, local_sems = refs[2 * n:]
        x, y, c = lax.axis_index("x"), lax.axis_index("y"), lax.axis_index("c")
        me = 4 * x + 2 * y + c
        copies = []
        for w in range(n):
            local = pltpu.make_async_copy(srcs[w].at[me] if scatter else srcs[w], outs[w].at[me], local_sems.at[w])
            local.start()
            copies.append(local)
            for k in range(1, N_DEV):
                px, py, pc = x ^ (k >> 2), y ^ ((k >> 1) & 1), c ^ (k & 1)
                peer = 4 * px + 2 * py + pc
                rc = pltpu.make_async_remote_copy(
                    src_ref=srcs[w].at[peer] if scatter else srcs[w], dst_ref=outs[w].at[me],
                    send_sem=send_sems.at[w, k - 1], recv_sem=recv_sems.at[w, k - 1],
                    device_id=(px, py, pc), device_id_type=pl.DeviceIdType.MESH)
                rc.start()
                copies.append(rc)
        for cp in copies:
            cp.wait()

    any_spec = pl.BlockSpec(memory_space=pl.ANY)
    return pl.pallas_call(
        body, name=name, in_specs=[any_spec] * n, out_specs=[any_spec] * n,
        out_shape=[jax.ShapeDtypeStruct((N_DEV,) + (a.shape[1:] if scatter else a.shape), a.dtype) for a in arrays],
        scratch_shapes=[pltpu.SemaphoreType.DMA((n, N_DEV - 1)), pltpu.SemaphoreType.DMA((n, N_DEV - 1)),
                        pltpu.SemaphoreType.DMA((n,))],
        compiler_params=pltpu.CompilerParams(has_side_effects=True),
    )(*arrays)


def _adamw(name, parts, w, m, v):
    R, C = w.shape
    S = parts.shape[0]
    tr = _tile(R, (256, 176, 128, 64, 8))
    c1, c2 = 1.0 - ADAM_B1 ** ADAM_STEP, 1.0 - ADAM_B2 ** ADAM_STEP

    def body(p_ref, w_ref, m_ref, v_ref, g_ref, d_ref, nm_ref, nv_ref):
        g = p_ref[0].astype(F32)
        for s in range(1, S):
            g = g + p_ref[s].astype(F32)
        nm = ADAM_B1 * m_ref[...] + (1.0 - ADAM_B1) * g
        nv = ADAM_B2 * v_ref[...] + (1.0 - ADAM_B2) * (g * g)
        g_ref[...] = g
        nm_ref[...] = nm
        nv_ref[...] = nv
        d_ref[...] = -ADAM_LR * ((nm / c1) / (jnp.sqrt(nv / c2) + ADAM_EPS) + ADAM_WD * w_ref[...])

    blk = pl.BlockSpec((tr, C), lambda i: (i, 0))
    return pl.pallas_call(
        body, name=name, grid=(R // tr,),
        in_specs=[pl.BlockSpec((S, tr, C), lambda i: (0, i, 0)), blk, blk, blk], out_specs=[blk] * 4,
        out_shape=[jax.ShapeDtypeStruct((R, C), F32)] * 4, compiler_params=_params(("parallel",)),
    )(parts, w, m, v)


def _sum_parts(name, parts):
    S, R, C = parts.shape

    def body(p_ref, o_ref):
        g = p_ref[0]
        for s in range(1, S):
            g = g + p_ref[s]
        o_ref[...] = g

    return pl.pallas_call(
        body, name=name, out_shape=jax.ShapeDtypeStruct((R, C), F32),
        in_specs=[pl.BlockSpec(memory_space=pltpu.VMEM)], out_specs=pl.BlockSpec(memory_space=pltpu.VMEM),
    )(parts)


def _heads_to_lanes(p):
    lead = p.shape[:-1]
    p4 = p.reshape(lead + (SSD_GROUPS, 4))
    p4 = jnp.pad(p4, [(0, 0)] * len(lead) + [(0, 0), (0, 124)])
    return p4.reshape(lead + (512,))


def _lanes_to_heads(p):
    lead = p.shape[:-1]
    return p.reshape(lead + (SSD_GROUPS, 128))[..., :4].reshape(lead + (SSD_HEADS,))


def _pack_rows(arrs):
    rows = []
    for a in arrs:
        f = a.reshape(-1).astype(F32)
        n = -(-f.shape[0] // D_MODEL) * D_MODEL
        rows.append(jnp.pad(f, (0, n - f.shape[0])).reshape(-1, D_MODEL))
    out = jnp.concatenate(rows, axis=0)
    return jnp.pad(out, ((0, (-out.shape[0]) % 8), (0, 0)))


def _unpack_rows(packed, like):
    outs, r = [], 0
    for a in like:
        n = 1
        for s in a.shape:
            n *= s
        nr = -(-n // D_MODEL)
        outs.append(packed[r:r + nr].reshape(-1)[:n].reshape(a.shape))
        r += nr
    return outs


def _ffn_fwd(tag, h, norm_w, w_gu, w_down):
    M = h.shape[0]
    F = w_down.shape[0]
    tm = _tile(M, (544, 256))
    n = _rmsnorm_fwd(tag + "_norm", h, norm_w)
    tn = _tile(F, (1408, 704, 256))
    g, u, a = _fused_matmul(
        tag + "_gu", M, F, D_MODEL,
        [dict(a=n, b=w_gu, acc=0), dict(a=n, b=w_gu, bn_off=F // tn, acc=1)], [],
        lambda accs, ex: (accs[0], accs[1], accs[0] * _sigmoid(accs[0]) * accs[1]),
        [BF16, BF16, BF16], 2, tm, tn, D_MODEL, outer="j")
    (h_out,) = _fused_matmul(
        tag + "_down", M, D_MODEL, F, [dict(a=a, b=w_down, acc=0)], [(h, 0)],
        lambda accs, ex: (ex[0] + 0.5 * accs[0],), [F32], 1, tm, D_MODEL, F, outer="j")
    return h_out, (n, g, u, a)


def _ffn_bwd(tag, dh, dh_b, h, norm_w, w_gu, w_down, saved):
    n, g, u, a = saved
    M = h.shape[0]
    F = w_down.shape[0]
    tm = _tile(M, (544, 256))
    tn = _tile(F, (1408, 704, 256))

    def swiglu_bwd(accs, ex):
        da, gv, uv = 0.5 * accs[0], ex[0].astype(F32), ex[1].astype(F32)
        s = _sigmoid(gv)
        return da * uv * _dsilu(gv, s), da * gv * s

    dg, du = _fused_matmul(
        tag + "_dact", M, F, D_MODEL, [dict(a=dh_b, b=w_down, trans_b=True, acc=0)], [(g, 0), (u, 0)],
        swiglu_bwd, [BF16, BF16], 1, tm, tn, D_MODEL, outer="j")
    dw_down = _matmul_tn(tag + "_dwd", a, dh_b, tn, D_MODEL, tm, scale=0.5)
    dw_g = _matmul_tn(tag + "_dwg", n, dg, D_MODEL, tn, tm)
    dw_u = _matmul_tn(tag + "_dwu", n, du, D_MODEL, tn, tm)
    (dn,) = _fused_matmul(
        tag + "_dn", M, D_MODEL, F,
        [dict(a=dg, b=w_gu, trans_b=True, acc=0), dict(a=du, b=w_gu, trans_b=True, bk_off=1, acc=0)], [],
        lambda accs, ex: (accs[0],), [F32], 1, tm, D_MODEL, F, outer="i")
    dh_prev, dh_prev_b, dnorm = _rmsnorm_bwd(tag + "_dnorm", dn, h, norm_w, dh)
    return dh_prev, dh_prev_b, dnorm, jnp.concatenate([dw_g, dw_u], axis=1), dw_down


def kernel(x, meta_tokens, ffn1_norm, ffn1_w_gu, ffn1_w_down, mix_norm, w_in, ssd_conv_w, ssd_conv_b, ssd_dt_bias, ssd_a_log, ssd_d, ssd_norm, hg_lower_bound, hg_norm, w_branch_a, w_branch_b, w_out, ffn2_norm, ffn2_w_gu, ffn2_w_down, final_norm, loss_target, m_meta_tokens, m_ffn1_norm, m_ffn1_w_gu, m_ffn1_w_down, m_mix_norm, m_w_in, m_ssd_conv_w, m_ssd_conv_b, m_ssd_dt_bias, m_ssd_a_log, m_ssd_d, m_ssd_norm, m_hg_lower_bound, m_hg_norm, m_w_branch_a, m_w_branch_b, m_w_out, m_ffn2_norm, m_ffn2_w_gu, m_ffn2_w_down, m_final_norm, v_meta_tokens, v_ffn1_norm, v_ffn1_w_gu, v_ffn1_w_down, v_mix_norm, v_w_in, v_ssd_conv_w, v_ssd_conv_b, v_ssd_dt_bias, v_ssd_a_log, v_ssd_d, v_ssd_norm, v_hg_lower_bound, v_hg_norm, v_w_branch_a, v_w_branch_b, v_w_out, v_ffn2_norm, v_ffn2_w_gu, v_ffn2_w_down, v_final_norm):
    Bl, S, D = x.shape
    T = PAD + N_META + S
    nc = T // Q
    M = Bl * T
    me = 4 * lax.axis_index("x") + 2 * lax.axis_index("y") + lax.axis_index("c")

    big = [ffn1_w_gu[0], ffn1_w_down[0], w_in[0], w_branch_a[0], w_branch_b[0], w_out[0], ffn2_w_gu[0], ffn2_w_down[0]]
    gathered = _exchange("gather_weights", [a.astype(BF16) for a in big] + [meta_tokens, ssd_conv_w[0]], scatter=False)
    cols = lambda gth: jnp.transpose(gth, (1, 0, 2)).reshape(gth.shape[1], -1)
    rows = lambda gth: gth.reshape(-1, gth.shape[2])
    wgu1, wd1, win_nat = cols(gathered[0]), rows(gathered[1]), cols(gathered[2])
    wa, wb, wo = rows(gathered[3]), rows(gathered[4]), rows(gathered[5])
    wgu2, wd2 = cols(gathered[6]), rows(gathered[7])
    meta_full, conv_w_full = cols(gathered[8]), cols(gathered[9])
    win_main = jnp.concatenate([win_nat[:, :3072], win_nat[:, 3088:]], axis=1)
    win_dt = _heads_to_lanes(win_nat[:, 3072:3088])
    bias_p, alog_p, d_p = _heads_to_lanes(ssd_dt_bias), _heads_to_lanes(ssd_a_log), _heads_to_lanes(ssd_d)
    final_w = final_norm.reshape(1, D)

    h0 = jnp.concatenate([jnp.zeros((Bl, PAD, D), F32), jnp.broadcast_to(meta_full[None], (Bl, N_META, D)), x],
                         axis=1).reshape(M, D)
    tm = _tile(M, (544, 256))
    h1, ffn1_saved = _ffn_fwd("ffn1", h0, ffn1_norm, wgu1, wd1)
    un = _rmsnorm_fwd("mix_norm", h1, mix_norm)
    plain = lambda accs, ex: (accs[0],)
    (proj,) = _fused_matmul("in_proj", M, N_MAIN, D, [dict(a=un, b=win_main, acc=0)], [], plain, [BF16], 1,
                            tm, 1536, D, outer="j")
    (dtr,) = _fused_matmul("in_proj_dt", M, 512, D, [dict(a=un, b=win_dt, acc=0)], [], plain, [F32], 1,
                           tm, 512, D, outer="j")
    xc = _conv_fwd(proj, conv_w_full, ssd_conv_b, Bl, T)
    ya, ssd_prev = _ssd_fwd(xc, dtr, proj, bias_p, alog_p, d_p, ssd_norm, Bl, nc)
    yb, hg_o, hg_st = _hgrn_fwd(proj, hg_lower_bound, hg_norm, Bl, nc)

    def branch_fwd(accs, ex):
        pa, pb = accs
        return pa, pb, _sigmoid(ex[0].astype(F32)) * pa + _sigmoid(ex[1].astype(F32)) * pb

    pa, pb, merged = _fused_matmul(
        "branches", M, D, D, [dict(a=ya, b=wa, acc=0), dict(a=yb, b=wb, acc=1)], [(proj, 7), (proj, 8)],
        branch_fwd, [BF16, BF16, BF16], 2, tm, D, D, outer="j")
    (h2,) = _fused_matmul("out_proj", M, D, D, [dict(a=merged, b=wo, acc=0)], [(h1, 0)],
                          lambda accs, ex: (ex[0] + accs[0],), [F32], 1, tm, D, D, outer="j")
    h3, ffn2_saved = _ffn_fwd("ffn2", h2, ffn2_norm, wgu2, wd2)

    dh3, dh3_b, d_final, loss_part = _loss_head(h3, final_w, loss_target, Bl, nc)
    dh2, dh2_b, d_ffn2_norm, d_wgu2, d_wd2 = _ffn_bwd("ffn2", dh3, dh3_b, h2, ffn2_norm, wgu2, wd2, ffn2_saved)

    def branch_bwd(accs, ex):
        dm = accs[0]
        ga, gb, pav, pbv = (e.astype(F32) for e in ex)
        sa, sb = _sigmoid(ga), _sigmoid(gb)
        return dm * sa, dm * sb, dm * pav * sa * (1.0 - sa), dm * pbv * sb * (1.0 - sb)

    dpa, dpb, dga, dgb = _fused_matmul(
        "d_merged", M, D, D, [dict(a=dh2_b, b=wo, trans_b=True, acc=0)], [(proj, 7), (proj, 8), (pa, 0), (pb, 0)],
        branch_bwd, [BF16] * 4, 1, tm, D, D, outer="j")
    d_wo = _matmul_tn("d_w_out", merged, dh2_b, D, D, tm)
    d_wa = _matmul_tn("d_w_a", ya, dpa, D, D, tm)
    d_wb = _matmul_tn("d_w_b", yb, dpb, D, D, tm)
    dya, dyb = _fused_matmul(
        "d_branches", M, D, D, [dict(a=dpa, b=wa, trans_b=True, acc=0), dict(a=dpb, b=wb, trans_b=True, acc=1)], [],
        lambda accs, ex: (accs[0], accs[1]), [BF16, BF16], 2, tm, D, D, outer="j")
    dxs, dB, dC, dz, ddtr, d_bias_p, d_alog_p, d_d_p, d_ssd_norm = _ssd_bwd(
        xc, dtr, proj, bias_p, alog_p, d_p, ssd_norm, ssd_prev, dya, Bl, nc)
    dxbc, d_conv_w, d_conv_b = _conv_bwd(proj, conv_w_full, ssd_conv_b, jnp.concatenate([dxs, dB, dC], axis=1), Bl, T)
    dq, df, di, dg, d_hb, d_hg_norm = _hgrn_bwd(proj, hg_lower_bound, hg_norm, hg_o, hg_st, dyb, Bl, nc)
    dproj = jnp.concatenate([dz, dxbc, dq, df, di, dg, dga, dgb], axis=1)
    ddtr_b = ddtr.astype(BF16)
    d_win_main = _matmul_tn("d_w_in", un, dproj, D, 1536, tm)
    d_win_dt = _matmul_tn("d_w_in_dt", un, ddtr_b, D, 512, tm)
    (dun_dt,) = _fused_matmul("d_un_dt", M, D, 512, [dict(a=ddtr_b, b=win_dt, trans_b=True, acc=0)], [], plain, [F32], 1,
                              tm, D, 512, outer="j")
    (dun,) = _fused_matmul("d_un", M, D, N_MAIN, [dict(a=dproj, b=win_main, trans_b=True, acc=0)], [(dun_dt, 0)],
                           lambda accs, ex: (accs[0] + ex[0],), [F32], 1, tm, D, 3072, outer="i")
    dh1, dh1_b, d_mix_norm = _rmsnorm_bwd("d_mix_norm", dun, h1, mix_norm, dh2)
    dh0, _, d_ffn1_norm, d_wgu1, d_wd1 = _ffn_bwd("ffn1", dh1, dh1_b, h0, ffn1_norm, wgu1, wd1, ffn1_saved)

    dh0 = dh0.reshape(Bl, T, D)
    grad_x = dh0[:, PAD + N_META:]
    d_meta = dh0[:, PAD:PAD + N_META]
    d_win_nat = jnp.concatenate([d_win_main[:, :3072], _lanes_to_heads(d_win_dt), d_win_main[:, 3072:]], axis=1)

    to_cols = lambda g: jnp.transpose(g.reshape(g.shape[0], N_DEV, -1), (1, 0, 2)).astype(BF16)
    to_rows = lambda g: g.reshape(N_DEV, -1, g.shape[1]).astype(BF16)
    small_grads = [d_ffn1_norm, d_mix_norm, d_conv_b, _lanes_to_heads(d_bias_p), _lanes_to_heads(d_alog_p),
                   _lanes_to_heads(d_d_p), d_ssd_norm, d_hb, d_hg_norm, d_ffn2_norm, d_final.reshape(D), d_conv_w]
    small_packed = _pack_rows(small_grads + [d_meta[b] for b in range(Bl)])
    big_grads = [to_cols(d_wgu1), to_rows(d_wd1), to_cols(d_win_nat), to_rows(d_wa), to_rows(d_wb), to_rows(d_wo),
                 to_cols(d_wgu2), to_rows(d_wd2)]
    parts = _exchange("scatter_grads", big_grads, scatter=True)
    (small_all,) = _exchange("gather_small_grads", [small_packed], scatter=False)
    small_sum = _sum_parts("sum_small_grads", small_all)
    unpacked = _unpack_rows(small_sum, small_grads + [d_meta[b] for b in range(Bl)])
    g_small = unpacked[:len(small_grads)]
    g_meta_full = unpacked[len(small_grads)]
    for b in range(1, Bl):
        g_meta_full = g_meta_full + unpacked[len(small_grads) + b]
    g_meta = lax.dynamic_slice_in_dim(g_meta_full, me * (D // N_DEV), D // N_DEV, axis=1)
    g_conv_w = lax.dynamic_slice_in_dim(g_small[11], me * (SSD_CONV_CH // N_DEV), SSD_CONV_CH // N_DEV, axis=1)

    names = ["meta_tokens", "ffn1_norm", "ffn1_w_gu", "ffn1_w_down", "mix_norm", "w_in", "ssd_conv_w", "ssd_conv_b",
             "ssd_dt_bias", "ssd_a_log", "ssd_d", "ssd_norm", "hg_lower_bound", "hg_norm", "w_branch_a", "w_branch_b",
             "w_out", "ffn2_norm", "ffn2_w_gu", "ffn2_w_down", "final_norm"]
    W = dict(meta_tokens=meta_tokens, ffn1_norm=ffn1_norm, ffn1_w_gu=ffn1_w_gu, ffn1_w_down=ffn1_w_down, mix_norm=mix_norm,
             w_in=w_in, ssd_conv_w=ssd_conv_w, ssd_conv_b=ssd_conv_b, ssd_dt_bias=ssd_dt_bias, ssd_a_log=ssd_a_log,
             ssd_d=ssd_d, ssd_norm=ssd_norm, hg_lower_bound=hg_lower_bound, hg_norm=hg_norm, w_branch_a=w_branch_a,
             w_branch_b=w_branch_b, w_out=w_out, ffn2_norm=ffn2_norm, ffn2_w_gu=ffn2_w_gu, ffn2_w_down=ffn2_w_down,
             final_norm=final_norm)
    Mo = dict(meta_tokens=m_meta_tokens, ffn1_norm=m_ffn1_norm, ffn1_w_gu=m_ffn1_w_gu, ffn1_w_down=m_ffn1_w_down,
              mix_norm=m_mix_norm, w_in=m_w_in, ssd_conv_w=m_ssd_conv_w, ssd_conv_b=m_ssd_conv_b, ssd_dt_bias=m_ssd_dt_bias,
              ssd_a_log=m_ssd_a_log, ssd_d=m_ssd_d, ssd_norm=m_ssd_norm, hg_lower_bound=m_hg_lower_bound, hg_norm=m_hg_norm,
              w_branch_a=m_w_branch_a, w_branch_b=m_w_branch_b, w_out=m_w_out, ffn2_norm=m_ffn2_norm, ffn2_w_gu=m_ffn2_w_gu,
              ffn2_w_down=m_ffn2_w_down, final_norm=m_final_norm)
    Vo = dict(meta_tokens=v_meta_tokens, ffn1_norm=v_ffn1_norm, ffn1_w_gu=v_ffn1_w_gu, ffn1_w_down=v_ffn1_w_down,
              mix_norm=v_mix_norm, w_in=v_w_in, ssd_conv_w=v_ssd_conv_w, ssd_conv_b=v_ssd_conv_b, ssd_dt_bias=v_ssd_dt_bias,
              ssd_a_log=v_ssd_a_log, ssd_d=v_ssd_d, ssd_norm=v_ssd_norm, hg_lower_bound=v_hg_lower_bound, hg_norm=v_hg_norm,
              w_branch_a=v_w_branch_a, w_branch_b=v_w_branch_b, w_out=v_w_out, ffn2_norm=v_ffn2_norm, ffn2_w_gu=v_ffn2_w_gu,
              ffn2_w_down=v_ffn2_w_down, final_norm=v_final_norm)
    grads, deltas, new_m, new_v = {}, {}, {}, {}
    big_names = ["ffn1_w_gu", "ffn1_w_down", "w_in", "w_branch_a", "w_branch_b", "w_out", "ffn2_w_gu", "ffn2_w_down"]
    for nm, part in zip(big_names, parts):
        shp = W[nm].shape
        outs = _adamw("adamw_" + nm, part, W[nm][0], Mo[nm][0], Vo[nm][0])
        grads[nm], deltas[nm], new_m[nm], new_v[nm] = (o.reshape(shp) for o in outs)
    small_names = ["ffn1_norm", "mix_norm", "ssd_conv_b", "ssd_dt_bias", "ssd_a_log", "ssd_d", "ssd_norm", "hg_lower_bound",
                   "hg_norm", "ffn2_norm", "final_norm", "ssd_conv_w", "meta_tokens"]
    small_g = g_small[:11] + [g_conv_w.reshape(ssd_conv_w.shape), g_meta]
    pk = lambda d: _pack_rows([d[nm] for nm in small_names])
    outs = _adamw("adamw_small", _pack_rows(small_g)[None], pk(W), pk(Mo), pk(Vo))
    like = [W[nm] for nm in small_names]
    for dst, o in zip((grads, deltas, new_m, new_v), outs):
        for nm, val in zip(small_names, _unpack_rows(o, like)):
            dst[nm] = val

    loss = lax.psum(loss_part[0, 0], MESH_AXES)
    return (loss, grad_x, *[grads[nm] for nm in names], *[deltas[nm] for nm in names],
            *[new_m[nm] for nm in names], *[new_v[nm] for nm in names])
```

```python
import functools

import jax
import jax.numpy as jnp
from jax import lax
from jax.experimental import pallas as pl
from jax.experimental.pallas import tpu as pltpu

F32, BF16 = jnp.float32, jnp.bfloat16
NN, NT, TN = ((1,), (0,)), ((1,), (1,)), ((0,), (0,))
HI = lax.Precision.HIGHEST
MESH_AXES = ("x", "y", "c")
N_DEV = 8

D_MODEL = 1024
N_META = 16
EPS = 1e-6
SSD_HEADS, SSD_HEAD_DIM, SSD_GROUPS, SSD_STATE, SSD_CONV, Q = 16, 64, 4, 128, 4, 128
SSD_INNER = SSD_HEADS * SSD_HEAD_DIM
SSD_CONV_CH = SSD_INNER + 2 * SSD_GROUPS * SSD_STATE
HG_WIDTH, HG_HEADS, HG_CHUNK = 1024, 8, 16
PAD = Q - N_META
N_MAIN = 9 * 1024
ADAM_LR, ADAM_B1, ADAM_B2, ADAM_EPS, ADAM_WD, ADAM_STEP = 0.001, 0.9, 0.999, 1e-08, 0.01, 10
VMEM_LIMIT = 52 * 1024 * 1024


def _dot(a, b, dims, prec=None):
    return lax.dot_general(a, b, (dims, ((), ())), precision=prec, preferred_element_type=F32)


def _sigmoid(x):
    return 1.0 / (1.0 + jnp.exp(-x))


def _dsilu(x, s):
    return s * (1.0 + x * (1.0 - s))


def _softplus(x):
    e = jnp.exp(-jnp.abs(x))
    u = 1.0 + e
    log1p_e = jnp.where(u == 1.0, e, jnp.log(u) * e / (u - 1.0))
    return jnp.maximum(x, 0.0) + log1p_e


def _params(sem):
    return pltpu.CompilerParams(dimension_semantics=sem, vmem_limit_bytes=VMEM_LIMIT)


def _tile(n, prefs):
    for p in prefs:
        if n % p == 0:
            return p
    return n


CHIP_FLIPS = ((1, 0), (0, 1), (1, 1))
N_PEER = N_DEV - 1


def _comm_gather(srcs, outs, send_sems, recv_sems, local_sems):
    n = len(srcs)
    x, y, c = (lax.axis_index(a) for a in MESH_AXES)
    dev = lambda px, py, pc: 4 * px + 2 * py + pc
    me, sib = dev(x, y, c), (x, y, 1 - c)

    def rc(w, k, slot, to, src=None):
        return pltpu.make_async_remote_copy(
            src_ref=outs[w].at[slot] if src is None else src, dst_ref=outs[w].at[slot],
            send_sem=send_sems.at[w, k], recv_sem=recv_sems.at[w, k], device_id=to, device_id_type=pl.DeviceIdType.MESH)

    def local(w):
        return pltpu.make_async_copy(srcs[w], outs[w].at[me], local_sems.at[w])

    def start():
        for w in range(n):
            local(w).start()
            rc(w, 0, me, sib, src=srcs[w]).start()
            for j, (fx, fy) in enumerate(CHIP_FLIPS):
                rc(w, 1 + j, me, (x ^ fx, y ^ fy, c), src=srcs[w]).start()

    def finish():
        for w in range(n):
            for j, (fx, fy) in enumerate(CHIP_FLIPS):
                slot = dev(x ^ fx, y ^ fy, c)
                rc(w, 1 + j, slot, sib).wait_recv()
                rc(w, 4 + j, slot, sib).start()
        for w in range(n):
            rc(w, 0, dev(x, y, 1 - c), sib).wait_recv()
            rc(w, 0, me, sib, src=srcs[w]).wait_send()
            for j, (fx, fy) in enumerate(CHIP_FLIPS):
                rc(w, 4 + j, dev(x ^ fx, y ^ fy, 1 - c), sib).wait_recv()
                rc(w, 1 + j, me, sib, src=srcs[w]).wait_send()
                rc(w, 4 + j, dev(x ^ fx, y ^ fy, c), sib).wait_send()
            local(w).wait()

    return start, finish


def _comm_scatter(srcs, outs, send_sems, recv_sems, local_sems):
    n = len(srcs)
    x, y, c = (lax.axis_index(a) for a in MESH_AXES)
    me = 4 * x + 2 * y + c

    def copies():
        out = []
        for w in range(n):
            out.append(pltpu.make_async_copy(srcs[w].at[me], outs[w].at[me], local_sems.at[w]))
            for k in range(1, N_DEV):
                px, py, pc = x ^ (k >> 2), y ^ ((k >> 1) & 1), c ^ (k & 1)
                out.append(pltpu.make_async_remote_copy(
                    src_ref=srcs[w].at[4 * px + 2 * py + pc], dst_ref=outs[w].at[me],
                    send_sem=send_sems.at[w, k - 1], recv_sem=recv_sems.at[w, k - 1],
                    device_id=(px, py, pc), device_id_type=pl.DeviceIdType.MESH))
        return out

    def start():
        for cp in copies():
            cp.start()

    def finish():
        for cp in copies():
            cp.wait()

    return start, finish


def _comm_parts(comm):
    kind, arrays = comm
    n = len(arrays)
    shapes = [jax.ShapeDtypeStruct((N_DEV,) + (a.shape[1:] if kind == "scatter" else a.shape), a.dtype) for a in arrays]
    sems = [pltpu.SemaphoreType.DMA((n, N_PEER)), pltpu.SemaphoreType.DMA((n, N_PEER)), pltpu.SemaphoreType.DMA((n,))]
    return n, shapes, sems, (_comm_scatter if kind == "scatter" else _comm_gather)


def _exchange(name, kind, arrays):
    n, shapes, sems, make = _comm_parts((kind, arrays))

    def body(*refs):
        start, finish = make(refs[:n], refs[n:2 * n], *refs[2 * n:])
        start()
        finish()

    any_spec = pl.BlockSpec(memory_space=pl.ANY)
    return pl.pallas_call(
        body, name=name, in_specs=[any_spec] * n, out_specs=[any_spec] * n, out_shape=shapes, scratch_shapes=sems,
        compiler_params=pltpu.CompilerParams(has_side_effects=True),
    )(*arrays)


def _call(body, *, name, grid, in_specs, out_specs, out_shape, scratch, sem, args, comm=None):
    if comm is None:
        return pl.pallas_call(body, name=name, grid=grid, in_specs=in_specs, out_specs=out_specs, out_shape=out_shape,
                              scratch_shapes=scratch, compiler_params=_params(sem))(*args)
    n, shapes, sems, make = _comm_parts(comm)
    n_in, n_out, n_scr = len(in_specs), len(out_specs), len(scratch)

    def carrier(*refs):
        ins, csrc = refs[:n_in], refs[n_in:n_in + n]
        outs, cout = refs[n_in + n:n_in + n + n_out], refs[n_in + n + n_out:n_in + 2 * n + n_out]
        rest = refs[n_in + 2 * n + n_out:]
        start, finish = make(csrc, cout, *rest[n_scr:])
        ids = [pl.program_id(a) for a in range(len(grid))]
        first = functools.reduce(jnp.logical_and, [i == 0 for i in ids])
        last = functools.reduce(jnp.logical_and, [i == g - 1 for i, g in zip(ids, grid)])
        pl.when(first)(start)
        body(*ins, *outs, *rest[:n_scr])
        pl.when(last)(finish)

    any_spec = pl.BlockSpec(memory_space=pl.ANY)
    return pl.pallas_call(
        carrier, name=name, grid=grid, in_specs=list(in_specs) + [any_spec] * n,
        out_specs=list(out_specs) + [any_spec] * n, out_shape=list(out_shape) + shapes,
        scratch_shapes=list(scratch) + sems,
        compiler_params=pltpu.CompilerParams(dimension_semantics=("arbitrary",) * len(grid),
                                             vmem_limit_bytes=VMEM_LIMIT, has_side_effects=True),
    )(*args, *comm[1])


def _fused_matmul(name, M, N, K, pairs, extras, epilogue, out_dtypes, n_acc, tm, tn, tk, outer="i", comm=None):
    nk = K // tk
    n_pairs, n_ex, n_out = len(pairs), len(extras), len(out_dtypes)

    def ij(g0, g1):
        return (g0, g1) if outer == "i" else (g1, g0)

    in_specs, args = [], []
    for p in pairs:
        ao, bk, bn = p.get("a_off", 0), p.get("bk_off", 0), p.get("bn_off", 0)
        in_specs.append(pl.BlockSpec((tm, tk), lambda g0, g1, k, ao=ao: (ij(g0, g1)[0], k + ao)))
        if p.get("trans_b"):
            in_specs.append(pl.BlockSpec((tn, tk), lambda g0, g1, k, bk=bk, bn=bn: (ij(g0, g1)[1] + bn, k + bk)))
        else:
            in_specs.append(pl.BlockSpec((tk, tn), lambda g0, g1, k, bk=bk, bn=bn: (k + bk, ij(g0, g1)[1] + bn)))
        args += [p["a"], p["b"]]
    for arr, off in extras:
        in_specs.append(pl.BlockSpec((tm, tn), lambda g0, g1, k, off=off: (ij(g0, g1)[0], ij(g0, g1)[1] + off)))
        args.append(arr)
    out_specs = [pl.BlockSpec((tm, tn), lambda g0, g1, k: ij(g0, g1)) for _ in out_dtypes]
    out_shape = [jax.ShapeDtypeStruct((M, N), dt) for dt in out_dtypes]
    grid = (M // tm, N // tn, nk) if outer == "i" else (N // tn, M // tm, nk)

    def partials(refs):
        accs = [None] * n_acc
        for idx, p in enumerate(pairs):
            d = _dot(refs[2 * idx][...], refs[2 * idx + 1][...], NT if p.get("trans_b") else NN)
            accs[p["acc"]] = d if accs[p["acc"]] is None else accs[p["acc"]] + d
        return accs

    def finish(accs, refs):
        ex = [r[...] for r in refs[2 * n_pairs:2 * n_pairs + n_ex]]
        outs = refs[2 * n_pairs + n_ex:2 * n_pairs + n_ex + n_out]
        for o, r in zip(outs, epilogue(accs, ex)):
            o[...] = r.astype(o.dtype)

    if nk == 1:
        def body(*refs):
            finish(partials(refs), refs)
        scratch = []
    else:
        def body(*refs):
            acc_refs = refs[-n_acc:]
            k = pl.program_id(2)
            new = partials(refs)

            @pl.when(k == 0)
            def _():
                for a, v in zip(acc_refs, new):
                    a[...] = v

            @pl.when(k > 0)
            def _():
                for a, v in zip(acc_refs, new):
                    a[...] += v

            @pl.when(k == nk - 1)
            def _():
                finish([a[...] for a in acc_refs], refs)
        scratch = [pltpu.VMEM((tm, tn), F32) for _ in range(n_acc)]

    return _call(body, name=name, grid=grid, in_specs=in_specs, out_specs=out_specs, out_shape=out_shape,
                 scratch=scratch, sem=("parallel", "parallel", "arbitrary"), args=args, comm=comm)


def _matmul_tn(name, x, y, t1, t2, tr, scale=1.0):
    R, K1 = x.shape
    N1 = y.shape[1]
    nr = R // tr

    def body(x_ref, y_ref, o_ref):
        r = pl.program_id(2)
        d = _dot(x_ref[...], y_ref[...], TN)

        @pl.when(r == 0)
        def _():
            o_ref[...] = d

        @pl.when(r > 0)
        def _():
            o_ref[...] += d

        if scale != 1.0:
            @pl.when(r == nr - 1)
            def _():
                o_ref[...] = o_ref[...] * scale

    return pl.pallas_call(
        body, name=name, grid=(K1 // t1, N1 // t2, nr),
        in_specs=[pl.BlockSpec((tr, t1), lambda i, j, r: (r, i)), pl.BlockSpec((tr, t2), lambda i, j, r: (r, j))],
        out_specs=pl.BlockSpec((t1, t2), lambda i, j, r: (i, j)),
        out_shape=jax.ShapeDtypeStruct((K1, N1), F32),
        compiler_params=_params(("parallel", "parallel", "arbitrary")),
    )(x, y)


def _rmsnorm_fwd(name, h, w):
    M, D = h.shape
    tm = _tile(M, (544, 256, 128))

    def body(h_ref, w_ref, o_ref):
        x = h_ref[...]
        r = lax.rsqrt(jnp.mean(x * x, axis=-1, keepdims=True) + EPS)
        o_ref[...] = (x * r * w_ref[...]).astype(o_ref.dtype)

    return pl.pallas_call(
        body, name=name, grid=(M // tm,),
        in_specs=[pl.BlockSpec((tm, D), lambda i: (i, 0)), pl.BlockSpec((1, D), lambda i: (0, 0))],
        out_specs=pl.BlockSpec((tm, D), lambda i: (i, 0)),
        out_shape=jax.ShapeDtypeStruct((M, D), BF16), compiler_params=_params(("parallel",)),
    )(h, w)


def _rmsnorm_bwd(name, dn, h, w, dh_in):
    M, D = h.shape
    tm = _tile(M, (544, 256, 128))

    def body(dn_ref, h_ref, w_ref, dhi_ref, dh_ref, dhb_ref, dw_ref):
        x = h_ref[...]
        r = lax.rsqrt(jnp.mean(x * x, axis=-1, keepdims=True) + EPS)
        xhat = x * r
        dn_v = dn_ref[...]
        gw = dn_v * w_ref[...]
        dx = r * (gw - xhat * jnp.mean(gw * xhat, axis=-1, keepdims=True))
        dh = dhi_ref[...] + dx
        dh_ref[...] = dh
        dhb_ref[...] = dh.astype(BF16)
        dw = jnp.sum(dn_v * xhat, axis=0, keepdims=True)

        @pl.when(pl.program_id(0) == 0)
        def _():
            dw_ref[...] = dw

        @pl.when(pl.program_id(0) > 0)
        def _():
            dw_ref[...] += dw

    row = pl.BlockSpec((tm, D), lambda i: (i, 0))
    vec = pl.BlockSpec((1, D), lambda i: (0, 0))
    return pl.pallas_call(
        body, name=name, grid=(M // tm,), in_specs=[row, row, vec, row], out_specs=[row, row, vec],
        out_shape=[jax.ShapeDtypeStruct((M, D), F32), jax.ShapeDtypeStruct((M, D), BF16), jax.ShapeDtypeStruct((1, D), F32)],
        compiler_params=_params(("arbitrary",)),
    )(dn, h, w, dh_in)


def _loss_head(h, w, target, Bl, nb):
    M, D = h.shape

    def body(h_ref, w_ref, t_ref, dh_ref, dhb_ref, dw_ref, loss_ref):
        b, t = pl.program_id(0), pl.program_id(1)
        live = (t > 0).astype(F32)
        x = h_ref[...]
        r = lax.rsqrt(jnp.mean(x * x, axis=-1, keepdims=True) + EPS)
        xhat = x * r
        wv = w_ref[...]
        err = (xhat * wv - t_ref[0]) * live
        dy = err * (1.0 / D)
        gw = dy * wv
        dx = r * (gw - xhat * jnp.mean(gw * xhat, axis=-1, keepdims=True))
        dh_ref[...] = dx
        dhb_ref[...] = dx.astype(BF16)
        dw = jnp.sum(dy * xhat, axis=0, keepdims=True)
        part = 0.5 * jnp.sum(jnp.sum(err * err, axis=-1, keepdims=True) * (1.0 / D), axis=0, keepdims=True)
        first = jnp.logical_and(b == 0, t == 0)

        @pl.when(first)
        def _():
            dw_ref[...] = dw
            loss_ref[...] = jnp.broadcast_to(part, loss_ref.shape)

        @pl.when(jnp.logical_not(first))
        def _():
            dw_ref[...] += dw
            loss_ref[...] += jnp.broadcast_to(part, loss_ref.shape)

    row = pl.BlockSpec((Q, D), lambda b, t: (b * nb + t, 0))
    vec = pl.BlockSpec((1, D), lambda b, t: (0, 0))
    return pl.pallas_call(
        body, name="loss_head", grid=(Bl, nb),
        in_specs=[row, vec, pl.BlockSpec((1, Q, D), lambda b, t: (b, jnp.maximum(t - 1, 0), 0))],
        out_specs=[row, row, vec, pl.BlockSpec((8, 128), lambda b, t: (0, 0))],
        out_shape=[jax.ShapeDtypeStruct((M, D), F32), jax.ShapeDtypeStruct((M, D), BF16),
                   jax.ShapeDtypeStruct((1, D), F32), jax.ShapeDtypeStruct((8, 128), F32)],
        compiler_params=_params(("arbitrary", "arbitrary")),
    )(h, w, target)


CONV_TC = 256


def _conv_pre(xr_ref, w_ref, b_ref):
    x = xr_ref[...].astype(F32)
    acc = b_ref[...] + w_ref[SSD_CONV - 1:SSD_CONV, :] * x
    for k in range(1, SSD_CONV):
        acc = acc + w_ref[SSD_CONV - 1 - k:SSD_CONV - k, :] * pltpu.roll(x, k, 0)
    return x, acc


def _conv_fwd(proj, w, b, Bl, T):
    M = proj.shape[0]
    off = 1024 // CONV_TC

    def body(xr_ref, w_ref, b_ref, o_ref):
        _, acc = _conv_pre(xr_ref, w_ref, b_ref)
        row = lax.broadcasted_iota(jnp.int32, acc.shape, 0)
        o_ref[...] = jnp.where(row >= PAD, acc * _sigmoid(acc), 0.0).astype(o_ref.dtype)

    return pl.pallas_call(
        body, name="conv_fwd", grid=(Bl, SSD_CONV_CH // CONV_TC),
        in_specs=[pl.BlockSpec((T, CONV_TC), lambda bb, j: (bb, j + off)),
                  pl.BlockSpec((SSD_CONV, CONV_TC), lambda bb, j: (0, j)), pl.BlockSpec((1, CONV_TC), lambda bb, j: (0, j))],
        out_specs=pl.BlockSpec((T, CONV_TC), lambda bb, j: (bb, j)),
        out_shape=jax.ShapeDtypeStruct((M, SSD_CONV_CH), BF16), compiler_params=_params(("parallel", "parallel")),
    )(proj, w, b)


def _conv_bwd(proj, w, b, dxc, Bl, T):
    M = proj.shape[0]
    off = 1024 // CONV_TC

    def body(xr_ref, w_ref, b_ref, d_ref, dx_ref, dw_ref, db_ref):
        x, acc = _conv_pre(xr_ref, w_ref, b_ref)
        row = lax.broadcasted_iota(jnp.int32, acc.shape, 0)
        s = _sigmoid(acc)
        dpre = jnp.where(row >= PAD, d_ref[...].astype(F32) * _dsilu(acc, s), 0.0)
        dx = w_ref[SSD_CONV - 1:SSD_CONV, :] * dpre
        dws = [jnp.sum(dpre * x, axis=0, keepdims=True)]
        for k in range(1, SSD_CONV):
            dx = dx + w_ref[SSD_CONV - 1 - k:SSD_CONV - k, :] * pltpu.roll(dpre, T - k, 0)
            dws.append(jnp.sum(dpre * pltpu.roll(x, k, 0), axis=0, keepdims=True))
        dx_ref[...] = dx.astype(dx_ref.dtype)
        dw = jnp.concatenate(dws[::-1], axis=0)
        db = jnp.sum(dpre, axis=0, keepdims=True)

        @pl.when(pl.program_id(1) == 0)
        def _():
            dw_ref[...] = dw
            db_ref[...] = db

        @pl.when(pl.program_id(1) > 0)
        def _():
            dw_ref[...] += dw
            db_ref[...] += db

    return pl.pallas_call(
        body, name="conv_bwd", grid=(SSD_CONV_CH // CONV_TC, Bl),
        in_specs=[pl.BlockSpec((T, CONV_TC), lambda j, bb: (bb, j + off)),
                  pl.BlockSpec((SSD_CONV, CONV_TC), lambda j, bb: (0, j)), pl.BlockSpec((1, CONV_TC), lambda j, bb: (0, j)),
                  pl.BlockSpec((T, CONV_TC), lambda j, bb: (bb, j))],
        out_specs=[pl.BlockSpec((T, CONV_TC), lambda j, bb: (bb, j)),
                   pl.BlockSpec((SSD_CONV, CONV_TC), lambda j, bb: (0, j)), pl.BlockSpec((1, CONV_TC), lambda j, bb: (0, j))],
        out_shape=[jax.ShapeDtypeStruct((M, SSD_CONV_CH), BF16), jax.ShapeDtypeStruct((SSD_CONV, SSD_CONV_CH), F32),
                   jax.ShapeDtypeStruct((1, SSD_CONV_CH), F32)],
        compiler_params=_params(("parallel", "arbitrary")),
    )(proj, w, b, dxc)


def _ssd_setup(c, dtr_ref, bias_ref, alog_ref):
    row = lax.broadcasted_iota(jnp.int32, (Q, 128), 0)
    col = lax.broadcasted_iota(jnp.int32, (Q, 128), 1)
    valid = jnp.logical_or(c > 0, row >= PAD)
    pre = dtr_ref[...] + bias_ref[...]
    dt = jnp.where(valid, _softplus(pre), 0.0)
    A = -jnp.exp(alog_ref[...])
    tri = row >= col
    cs = _dot(tri.astype(F32), dt * A, NN, HI)
    cst = _dot((row == col).astype(F32), cs, NT, HI)
    return dt, A, cs, cst, tri, valid, pre, row, col


def _pair_terms(p, dt, cs, col):
    h0, h1 = 2 * p, 2 * p + 1
    first = col < SSD_HEAD_DIM
    pick = lambda a: jnp.where(first, a[:, h0:h0 + 1], a[:, h1:h1 + 1])
    cl0, cl1 = cs[Q - 1:Q, h0:h0 + 1], cs[Q - 1:Q, h1:h1 + 1]
    cs_p = pick(cs)
    cl_p = jnp.where(first[0:1], cl0, cl1)
    return first, pick(dt), jnp.exp(cs_p), jnp.exp(cl_p - cs_p), (cl0, cl1)


def _ssd_core(x_ref, b_ref, c_ref, d_ref, state_of, dt, cs, cst, tri, row, col):
    xv = x_ref[...].astype(F32)
    Bg, Cg = b_ref[...], c_ref[...]
    CB = _dot(Cg, Bg, NT)
    ys, new_states, Ms = [], [], []
    for p in range(2):
        first, dt_p, ecs_p, decay_p, (cl0, cl1) = _pair_terms(p, dt, cs, col)
        x_p = xv[:, 128 * p:128 * (p + 1)]
        X_p = x_p * dt_p
        yd = jnp.zeros((Q, 128), F32)
        for hh in range(2):
            h = 2 * p + hh
            Lm = jnp.exp(jnp.where(tri, cs[:, h:h + 1] - cst[h:h + 1, :], -jnp.inf))
            Mh = CB * Lm
            Ms.append(Mh)
            Xm = jnp.where(first if hh == 0 else jnp.logical_not(first), X_p, 0.0).astype(BF16)
            yd = yd + _dot(Mh.astype(BF16), Xm, NN)
        prev = state_of(p)
        yo = _dot(Cg, prev.astype(BF16), NT) * ecs_p
        st = _dot((X_p * decay_p).astype(BF16), Bg, TN)
        ecl_rows = jnp.where(row < SSD_HEAD_DIM, jnp.exp(cl0), jnp.exp(cl1))
        new_states.append(prev * ecl_rows + st)
        d_p = jnp.where(first[0:1], d_ref[:, 2 * p:2 * p + 1], d_ref[:, 2 * p + 1:2 * p + 2])
        ys.append(yd + yo + x_p * d_p)
    return jnp.concatenate(ys, axis=1), new_states, Ms, CB, xv


def _ssd_specs(nc):
    rb = lambda g, b, c: b * nc + c
    return [
        pl.BlockSpec((Q, 256), lambda g, b, c: (rb(g, b, c), g)),
        pl.BlockSpec((Q, 128), lambda g, b, c: (rb(g, b, c), 8 + g)),
        pl.BlockSpec((Q, 128), lambda g, b, c: (rb(g, b, c), 12 + g)),
        pl.BlockSpec((Q, 128), lambda g, b, c: (rb(g, b, c), g)),
        pl.BlockSpec((Q, 256), lambda g, b, c: (rb(g, b, c), g)),
        pl.BlockSpec((1, 128), lambda g, b, c: (0, g)),
        pl.BlockSpec((1, 128), lambda g, b, c: (0, g)),
        pl.BlockSpec((1, 128), lambda g, b, c: (0, g)),
        pl.BlockSpec((1, 256), lambda g, b, c: (0, g)),
    ]


def _ssd_fwd(xc, dtr, proj, bias_p, alog_p, d_p, nw, Bl, nc):
    M = xc.shape[0]

    def body(x_ref, b_ref, c_ref, dtr_ref, z_ref, bias_ref, alog_ref, d_ref, nw_ref, y_ref, prev_ref, state):
        c = pl.program_id(2)

        @pl.when(c == 0)
        def _():
            state[...] = jnp.zeros_like(state)

        dt, _, cs, cst, tri, _, _, row, col = _ssd_setup(c, dtr_ref, bias_ref, alog_ref)
        y, new_states, _, _, _ = _ssd_core(x_ref, b_ref, c_ref, d_ref, lambda p: state[p], dt, cs, cst, tri, row, col)
        for p in range(2):
            prev_ref[0, 0, 0, p] = state[p]
            state[p] = new_states[p]
        zz = z_ref[...].astype(F32)
        yg = y * zz * _sigmoid(zz)
        r = lax.rsqrt(jnp.mean(yg * yg, axis=-1, keepdims=True) + EPS)
        y_ref[...] = (yg * r * nw_ref[...]).astype(y_ref.dtype)

    return pl.pallas_call(
        body, name="ssd_fwd", grid=(SSD_GROUPS, Bl, nc), in_specs=_ssd_specs(nc),
        out_specs=[pl.BlockSpec((Q, 256), lambda g, b, c: (b * nc + c, g)),
                   pl.BlockSpec((1, 1, 1, 2, 128, 128), lambda g, b, c: (b, g, c, 0, 0, 0))],
        out_shape=[jax.ShapeDtypeStruct((M, SSD_INNER), BF16), jax.ShapeDtypeStruct((Bl, SSD_GROUPS, nc, 2, 128, 128), F32)],
        scratch_shapes=[pltpu.VMEM((2, 128, 128), F32)],
        compiler_params=_params(("parallel", "arbitrary", "arbitrary")),
    )(xc, xc, xc, dtr, proj, bias_p, alog_p, d_p, nw)


def _ssd_bwd(xc, dtr, proj, bias_p, alog_p, d_p, nw, prev, dya, Bl, nc, comm=None):
    M = xc.shape[0]
    rev = lambda spec: pl.BlockSpec(spec.block_shape, lambda g, b, c, f=spec.index_map: f(g, b, nc - 1 - c))

    def body(x_ref, b_ref, c_ref, dtr_ref, z_ref, bias_ref, alog_ref, d_ref, nw_ref, prev_ref, dy_ref,
             dx_ref, dB_ref, dC_ref, dz_ref, ddtr_ref, dbias_ref, dalog_ref, dd_ref, dnw_ref, dS):
        b, t = pl.program_id(1), pl.program_id(2)
        c = nc - 1 - t

        @pl.when(t == 0)
        def _():
            dS[...] = jnp.zeros_like(dS)

        dt, A, cs, cst, tri, valid, pre, row, col = _ssd_setup(c, dtr_ref, bias_ref, alog_ref)
        y, _, Ms, CB, xv = _ssd_core(x_ref, b_ref, c_ref, d_ref, lambda p: prev_ref[0, 0, 0, p], dt, cs, cst, tri, row, col)
        Bg, Cg = b_ref[...], c_ref[...]
        Bf = Bg.astype(F32)

        zz = z_ref[...].astype(F32)
        sz = _sigmoid(zz)
        silu_z = zz * sz
        yg = y * silu_z
        r = lax.rsqrt(jnp.mean(yg * yg, axis=-1, keepdims=True) + EPS)
        xhat = yg * r
        dout = dy_ref[...].astype(F32)
        gw = dout * nw_ref[...]
        dyg = r * (gw - xhat * jnp.mean(gw * xhat, axis=-1, keepdims=True))
        dnw = jnp.sum(dout * xhat, axis=0, keepdims=True)
        dz_ref[...] = (dyg * y * _dsilu(zz, sz)).astype(dz_ref.dtype)
        dy = dyg * silu_z

        lane1 = lax.broadcasted_iota(jnp.int32, (1, 128), 1)
        put_col = lambda h, v: jnp.where(col == h, v, 0.0)
        put_lane = lambda h, v: jnp.where(lane1 == h, v, 0.0)
        dcs = jnp.zeros((Q, 128), F32)
        dcs_t = jnp.zeros((128, Q), F32)
        dcl = jnp.zeros((1, 128), F32)
        ddt = jnp.zeros((Q, 128), F32)
        dD = jnp.zeros((1, 128), F32)
        dCB = jnp.zeros((Q, Q), F32)
        dBacc = jnp.zeros((Q, 128), F32)
        dCacc = jnp.zeros((Q, 128), F32)
        dxs = []
        for p in range(2):
            first, dt_p, ecs_p, decay_p, (cl0, cl1) = _pair_terms(p, dt, cs, col)
            halves = (first, jnp.logical_not(first))
            x_p = xv[:, 128 * p:128 * (p + 1)]
            X_p = x_p * dt_p
            dy_p = dy[:, 128 * p:128 * (p + 1)]
            prev_p = prev_ref[0, 0, 0, p]
            prev_b = prev_p.astype(BF16)
            dS_p = dS[p]
            dS_b = dS_p.astype(BF16)
            dX = decay_p * _dot(Bg, dS_b, NT)
            Yo = _dot(Cg, prev_b, NT)
            dYo = (dy_p * ecs_p).astype(BF16)
            dCacc = dCacc + _dot(dYo, prev_b, NN)
            dprev = _dot(dYo, Cg, TN)
            off_cs = dy_p * Yo * ecs_p
            state_cs = dS_p * prev_p
            for hh in range(2):
                h = 2 * p + hh
                hm = halves[hh]
                Mh = Ms[h]
                dyh = jnp.where(hm, dy_p, 0.0).astype(BF16)
                Xm = jnp.where(hm, X_p, 0.0).astype(BF16)
                dX = dX + _dot(Mh.astype(BF16), dyh, TN)
                dM = _dot(dyh, Xm, NT)
                W = dM * Mh
                Lm = jnp.exp(jnp.where(tri, cs[:, h:h + 1] - cst[h:h + 1, :], -jnp.inf))
                dCB = dCB + dM * Lm
                XdS = _dot(Xm, dS_b, NN)
                decay_h = decay_p[:, 64 * hh:64 * hh + 1]
                dBacc = dBacc + decay_h * XdS
                tdec = jnp.sum(XdS * Bf, axis=1, keepdims=True) * decay_h
                dcs = dcs + put_col(h, jnp.sum(W, axis=1, keepdims=True)
                                    + jnp.sum(jnp.where(hm, off_cs, 0.0), axis=1, keepdims=True) - tdec)
                dcs_t = dcs_t - jnp.where(lax.broadcasted_iota(jnp.int32, (128, Q), 0) == h,
                                          jnp.sum(W, axis=0, keepdims=True), 0.0)
                ecl = jnp.exp(cl0 if hh == 0 else cl1)
                rows_h = (row < SSD_HEAD_DIM) if hh == 0 else (row >= SSD_HEAD_DIM)
                dcl = dcl + put_lane(h, jnp.sum(tdec, axis=0, keepdims=True)
                                     + ecl * jnp.sum(jnp.sum(jnp.where(rows_h, state_cs, 0.0), axis=1, keepdims=True),
                                                     axis=0, keepdims=True))
                ddt = ddt + put_col(h, jnp.sum(jnp.where(hm, dX * x_p, 0.0), axis=1, keepdims=True))
                dD = dD + put_lane(h, jnp.sum(jnp.sum(jnp.where(hm, dy_p * x_p, 0.0), axis=1, keepdims=True),
                                              axis=0, keepdims=True))
            ecl_rows = jnp.where(row < SSD_HEAD_DIM, jnp.exp(cl0), jnp.exp(cl1))
            dS[p] = dS_p * ecl_rows + dprev
            d_pp = jnp.where(first[0:1], d_ref[:, 2 * p:2 * p + 1], d_ref[:, 2 * p + 1:2 * p + 2])
            dxs.append(dy_p * d_pp + dX * dt_p)
        dCB_b = dCB.astype(BF16)
        dCacc = dCacc + _dot(dCB_b, Bg, NN)
        dBacc = dBacc + _dot(dCB_b, Cg, TN)
        dx_ref[...] = jnp.concatenate(dxs, axis=1).astype(dx_ref.dtype)
        dB_ref[...] = dBacc.astype(dB_ref.dtype)
        dC_ref[...] = dCacc.astype(dC_ref.dtype)

        dcs = dcs + _dot((row == col).astype(F32), dcs_t, NT, HI) + jnp.where(row == Q - 1, dcl, 0.0)
        da = _dot((row <= col).astype(F32), dcs, NN, HI)
        ddt = ddt + da * A
        dpre = jnp.where(valid, ddt * _sigmoid(pre), 0.0)
        ddtr_ref[...] = dpre
        dbias = jnp.sum(dpre, axis=0, keepdims=True)
        dalog = jnp.sum(da * dt, axis=0, keepdims=True) * A
        first_step = jnp.logical_and(b == 0, t == 0)

        @pl.when(first_step)
        def _():
            dbias_ref[...] = dbias
            dalog_ref[...] = dalog
            dd_ref[...] = dD
            dnw_ref[...] = dnw

        @pl.when(jnp.logical_not(first_step))
        def _():
            dbias_ref[...] += dbias
            dalog_ref[...] += dalog
            dd_ref[...] += dD
            dnw_ref[...] += dnw

    fwd_specs = _ssd_specs(nc)
    in_specs = [rev(s) for s in fwd_specs] + [
        pl.BlockSpec((1, 1, 1, 2, 128, 128), lambda g, b, c: (b, g, nc - 1 - c, 0, 0, 0)),
        pl.BlockSpec((Q, 256), lambda g, b, c: (b * nc + nc - 1 - c, g))]
    rowblk = lambda w: pl.BlockSpec((Q, w), lambda g, b, c: (b * nc + nc - 1 - c, g))
    vec = lambda w: pl.BlockSpec((1, w), lambda g, b, c: (0, g))
    return _call(
        body, name="ssd_bwd", grid=(SSD_GROUPS, Bl, nc), in_specs=in_specs,
        out_specs=[rowblk(256), rowblk(128), rowblk(128), rowblk(256), rowblk(128), vec(128), vec(128), vec(128), vec(256)],
        out_shape=[jax.ShapeDtypeStruct((M, SSD_INNER), BF16), jax.ShapeDtypeStruct((M, 512), BF16),
                   jax.ShapeDtypeStruct((M, 512), BF16), jax.ShapeDtypeStruct((M, SSD_INNER), BF16),
                   jax.ShapeDtypeStruct((M, 512), F32), jax.ShapeDtypeStruct((1, 512), F32),
                   jax.ShapeDtypeStruct((1, 512), F32), jax.ShapeDtypeStruct((1, 512), F32),
                   jax.ShapeDtypeStruct((1, SSD_INNER), F32)],
        scratch=[pltpu.VMEM((2, 128, 128), F32)], sem=("parallel", "arbitrary", "arbitrary"),
        args=(xc, xc, xc, dtr, proj, bias_p, alog_p, d_p, nw, prev, dya), comm=comm)


NSUB = Q // HG_CHUNK
EXP_CAP = 80.0


def _hg_setup(blk, q_ref, f_ref, hb_ref):
    row = lax.broadcasted_iota(jnp.int32, (Q, Q), 0)
    col = lax.broadcasted_iota(jnp.int32, (Q, Q), 1)
    same = (row // HG_CHUNK) == (col // HG_CHUNK)
    causal = jnp.logical_and(same, col <= row)
    lb = _sigmoid(hb_ref[0:1, :] - hb_ref[1:2, :])
    fl = f_ref[...].astype(F32)
    sg = _sigmoid(fl)
    fg = lb + (1.0 - lb) * sg
    k = (1.0 - lb) * (1.0 - sg)
    gl = jnp.log(fg)
    G = _dot(causal.astype(F32), gl, NN, HI)
    T = _dot(same.astype(F32), gl, NN, HI)
    qv = q_ref[...].astype(F32)
    sq = _sigmoid(qv)
    eG = jnp.exp(G)
    eGn = jnp.exp(jnp.minimum(-G, EXP_CAP))
    eTG = jnp.exp(T - G)
    qt = qv * sq * eG
    kt = k * eGn
    kh = k * eTG
    valid = jnp.logical_or(blk > 0, row >= PAD)
    return dict(row=row, col=col, same=same, causal=causal, lb=lb, sg=sg, fg=fg, k=k, T=T, qv=qv, sq=sq,
                eG=eG, eGn=eGn, eTG=eTG, qt=qt, kt=kt, kh=kh, valid=valid)


def _hg_specs(nb, rev=False):
    rb = (lambda h, b, t: b * nb + nb - 1 - t) if rev else (lambda h, b, t: b * nb + t)
    blk = lambda off: pl.BlockSpec((Q, 128), lambda h, b, t, off=off: (rb(h, b, t), off + h))
    return [blk(24), blk(32), blk(40), blk(48),
            pl.BlockSpec((2, 128), lambda h, b, t: (0, h)), pl.BlockSpec((1, 128), lambda h, b, t: (0, h))]


def _hgrn_fwd(proj, hb, nw, Bl, nb, comm=None):
    M = proj.shape[0]

    def body(q_ref, f_ref, i_ref, g_ref, hb_ref, nw_ref, y_ref, o_ref, st_ref, S):
        blk = pl.program_id(2)

        @pl.when(blk == 0)
        def _():
            S[...] = jnp.zeros_like(S)

        s = _hg_setup(blk, q_ref, f_ref, hb_ref)
        v = i_ref[...]
        qt_b, kt_b, kh_b = s["qt"].astype(BF16), s["kt"].astype(BF16), s["kh"].astype(BF16)
        att = jnp.where(s["causal"], _dot(qt_b, kt_b, NT), 0.0)
        o_intra = _dot(att.astype(BF16), v, NN)
        for j in range(NSUB):
            sl = slice(HG_CHUNK * j, HG_CHUNK * (j + 1))
            St = S[...]
            st_ref[0, 0, 0, j] = St
            o_ref[sl, :] = o_intra[sl] + _dot(qt_b[sl], St.astype(BF16), NT)
            S[...] = St * jnp.exp(s["T"][HG_CHUNK * j:HG_CHUNK * j + 1, :]) + _dot(v[sl], kh_b[sl], TN)
        o = o_ref[...]
        r = lax.rsqrt(jnp.mean(o * o, axis=-1, keepdims=True) + EPS)
        gv = g_ref[...].astype(F32)
        y_ref[...] = (o * r * nw_ref[...] * gv * _sigmoid(gv)).astype(y_ref.dtype)

    rowblk = pl.BlockSpec((Q, 128), lambda h, b, t: (b * nb + t, h))
    return _call(
        body, name="hgrn_fwd", grid=(HG_HEADS, Bl, nb), in_specs=_hg_specs(nb),
        out_specs=[rowblk, rowblk, pl.BlockSpec((1, 1, 1, NSUB, 128, 128), lambda h, b, t: (b, h, t, 0, 0, 0))],
        out_shape=[jax.ShapeDtypeStruct((M, HG_WIDTH), BF16), jax.ShapeDtypeStruct((M, HG_WIDTH), F32),
                   jax.ShapeDtypeStruct((Bl, HG_HEADS, nb, NSUB, 128, 128), F32)],
        scratch=[pltpu.VMEM((128, 128), F32)], sem=("parallel", "arbitrary", "arbitrary"),
        args=(proj, proj, proj, proj, hb, nw), comm=comm)


def _hgrn_bwd(proj, hb, nw, o_saved, st_saved, dyb, Bl, nb, comm=None):
    M = proj.shape[0]

    def body(q_ref, f_ref, i_ref, g_ref, hb_ref, nw_ref, o_ref, st_ref, dy_ref,
             dq_ref, df_ref, di_ref, dg_ref, dhb_ref, dnw_ref, dS, a_dqt, a_dv, a_dkh, a_dgl):
        b, t = pl.program_id(1), pl.program_id(2)
        blk = nb - 1 - t

        @pl.when(t == 0)
        def _():
            dS[...] = jnp.zeros_like(dS)

        s = _hg_setup(blk, q_ref, f_ref, hb_ref)
        v = i_ref[...]
        qt_b, kt_b, kh_b = s["qt"].astype(BF16), s["kt"].astype(BF16), s["kh"].astype(BF16)
        att = jnp.where(s["causal"], _dot(qt_b, kt_b, NT), 0.0).astype(BF16)

        o = o_ref[...]
        r = lax.rsqrt(jnp.mean(o * o, axis=-1, keepdims=True) + EPS)
        xhat = o * r
        gv = g_ref[...].astype(F32)
        sgv = _sigmoid(gv)
        dyv = dy_ref[...].astype(F32)
        d_on = dyv * gv * sgv
        dg_out = dyv * xhat * nw_ref[...] * _dsilu(gv, sgv)
        gw = d_on * nw_ref[...]
        do = r * (gw - xhat * jnp.mean(gw * xhat, axis=-1, keepdims=True))
        dnw = jnp.sum(d_on * xhat, axis=0, keepdims=True)
        do_b = do.astype(BF16)

        datt = jnp.where(s["causal"], _dot(do_b, v, NT), 0.0).astype(BF16)
        dqt = _dot(datt, kt_b, NN)
        dkt = _dot(datt, qt_b, TN)
        dv = _dot(att, do_b, TN)
        last_row = (lax.broadcasted_iota(jnp.int32, (HG_CHUNK, 128), 0) == HG_CHUNK - 1)
        for j in reversed(range(NSUB)):
            sl = slice(HG_CHUNK * j, HG_CHUNK * (j + 1))
            St = st_ref[0, 0, 0, j]
            dSt = dS[...]
            St_b, dSt_b = St.astype(BF16), dSt.astype(BF16)
            eT = jnp.exp(s["T"][HG_CHUNK * j:HG_CHUNK * j + 1, :])
            dkh_j = _dot(v[sl], dSt_b, NN)
            a_dqt[sl, :] = _dot(do_b[sl], St_b, NN)
            a_dv[sl, :] = _dot(kh_b[sl], dSt_b, NT)
            a_dkh[sl, :] = dkh_j
            dlast = (jnp.sum(St * dSt, axis=0, keepdims=True) * eT
                     + jnp.sum(dkh_j * s["kh"][sl], axis=0, keepdims=True))
            a_dgl[sl, :] = jnp.where(last_row, dlast, 0.0)
            dS[...] = dSt * eT + _dot(do_b[sl], qt_b[sl], TN)
        dqt = dqt + a_dqt[...]
        dv = dv + a_dv[...]
        dkh = a_dkh[...]
        dG = dqt * s["qt"] - dkt * s["kt"] - dkh * s["kh"] + a_dgl[...]
        rev_causal = jnp.logical_and(s["same"], s["col"] >= s["row"])
        dgl = _dot(rev_causal.astype(F32), dG, NN, HI)
        dk = dkt * s["eGn"] + dkh * s["eTG"]
        dfg = dgl / s["fg"] - dk
        lb, sg = s["lb"], s["sg"]
        keep = s["valid"].astype(F32)
        df_ref[...] = (dfg * (1.0 - lb) * sg * (1.0 - sg) * keep).astype(df_ref.dtype)
        dq_ref[...] = (dqt * s["eG"] * _dsilu(s["qv"], s["sq"]) * keep).astype(dq_ref.dtype)
        di_ref[...] = (dv * keep).astype(di_ref.dtype)
        dg_ref[...] = (dg_out * keep).astype(dg_ref.dtype)
        dlb = jnp.sum(dfg * (1.0 - sg) * keep, axis=0, keepdims=True) * lb * (1.0 - lb)
        dhb = jnp.concatenate([dlb, -dlb], axis=0)
        first_step = jnp.logical_and(b == 0, t == 0)

        @pl.when(first_step)
        def _():
            dhb_ref[...] = dhb
            dnw_ref[...] = dnw

        @pl.when(jnp.logical_not(first_step))
        def _():
            dhb_ref[...] += dhb
            dnw_ref[...] += dnw

    rowblk = pl.BlockSpec((Q, 128), lambda h, b, t: (b * nb + nb - 1 - t, h))
    return _call(
        body, name="hgrn_bwd", grid=(HG_HEADS, Bl, nb),
        in_specs=_hg_specs(nb, rev=True) + [
            rowblk, pl.BlockSpec((1, 1, 1, NSUB, 128, 128), lambda h, b, t: (b, h, nb - 1 - t, 0, 0, 0)), rowblk],
        out_specs=[rowblk, rowblk, rowblk, rowblk,
                   pl.BlockSpec((2, 128), lambda h, b, t: (0, h)), pl.BlockSpec((1, 128), lambda h, b, t: (0, h))],
        out_shape=[jax.ShapeDtypeStruct((M, HG_WIDTH), BF16)] * 4 + [
            jax.ShapeDtypeStruct((2, HG_WIDTH), F32), jax.ShapeDtypeStruct((1, HG_WIDTH), F32)],
        scratch=[pltpu.VMEM((128, 128), F32)] * 5, sem=("parallel", "arbitrary", "arbitrary"),
        args=(proj, proj, proj, proj, hb, nw, o_saved, st_saved, dyb), comm=comm)


def _adamw(name, parts, w, m, v):
    R, C = w.shape
    S = parts.shape[0]
    tr = _tile(R, (256, 176, 128, 64, 8))
    c1, c2 = 1.0 - ADAM_B1 ** ADAM_STEP, 1.0 - ADAM_B2 ** ADAM_STEP

    def body(p_ref, w_ref, m_ref, v_ref, g_ref, d_ref, nm_ref, nv_ref):
        g = p_ref[0].astype(F32)
        for s in range(1, S):
            g = g + p_ref[s].astype(F32)
        nm = ADAM_B1 * m_ref[...] + (1.0 - ADAM_B1) * g
        nv = ADAM_B2 * v_ref[...] + (1.0 - ADAM_B2) * (g * g)
        g_ref[...] = g
        nm_ref[...] = nm
        nv_ref[...] = nv
        d_ref[...] = -ADAM_LR * ((nm / c1) / (jnp.sqrt(nv / c2) + ADAM_EPS) + ADAM_WD * w_ref[...])

    blk = pl.BlockSpec((tr, C), lambda i: (i, 0))
    return pl.pallas_call(
        body, name=name, grid=(R // tr,),
        in_specs=[pl.BlockSpec((S, tr, C), lambda i: (0, i, 0)), blk, blk, blk], out_specs=[blk] * 4,
        out_shape=[jax.ShapeDtypeStruct((R, C), F32)] * 4, compiler_params=_params(("parallel",)),
    )(parts, w, m, v)


def _sum_parts(name, parts):
    S, R, C = parts.shape

    def body(p_ref, o_ref):
        g = p_ref[0]
        for s in range(1, S):
            g = g + p_ref[s]
        o_ref[...] = g

    return pl.pallas_call(
        body, name=name, out_shape=jax.ShapeDtypeStruct((R, C), F32),
        in_specs=[pl.BlockSpec(memory_space=pltpu.VMEM)], out_specs=pl.BlockSpec(memory_space=pltpu.VMEM),
    )(parts)


def _heads_to_lanes(p):
    lead = p.shape[:-1]
    p4 = p.reshape(lead + (SSD_GROUPS, 4))
    p4 = jnp.pad(p4, [(0, 0)] * len(lead) + [(0, 0), (0, 124)])
    return p4.reshape(lead + (512,))


def _lanes_to_heads(p):
    lead = p.shape[:-1]
    return p.reshape(lead + (SSD_GROUPS, 128))[..., :4].reshape(lead + (SSD_HEADS,))


def _pack_rows(arrs):
    rows = []
    for a in arrs:
        f = a.reshape(-1).astype(F32)
        n = -(-f.shape[0] // D_MODEL) * D_MODEL
        rows.append(jnp.pad(f, (0, n - f.shape[0])).reshape(-1, D_MODEL))
    out = jnp.concatenate(rows, axis=0)
    return jnp.pad(out, ((0, (-out.shape[0]) % 8), (0, 0)))


def _unpack_rows(packed, like):
    outs, r = [], 0
    for a in like:
        n = 1
        for s in a.shape:
            n *= s
        nr = -(-n // D_MODEL)
        outs.append(packed[r:r + nr].reshape(-1)[:n].reshape(a.shape))
        r += nr
    return outs


def _cols(gth):
    return jnp.transpose(gth, (1, 0, 2)).reshape(gth.shape[1], -1)


def _rows(gth):
    return gth.reshape(-1, gth.shape[2])


def _to_cols(g):
    return jnp.transpose(g.reshape(g.shape[0], N_DEV, -1), (1, 0, 2)).astype(BF16)


def _to_rows(g):
    return g.reshape(N_DEV, -1, g.shape[1]).astype(BF16)


def _ffn_fwd_gu(tag, h, norm_w, w_gu, comm=None):
    M = h.shape[0]
    F = w_gu.shape[1] // 2
    tm = _tile(M, (544, 256))
    n = _rmsnorm_fwd(tag + "_norm", h, norm_w)
    tn = _tile(F, (1408, 704, 256))
    outs = _fused_matmul(
        tag + "_gu", M, F, D_MODEL,
        [dict(a=n, b=w_gu, acc=0), dict(a=n, b=w_gu, bn_off=F // tn, acc=1)], [],
        lambda accs, ex: (accs[0], accs[1], accs[0] * _sigmoid(accs[0]) * accs[1]),
        [BF16, BF16, BF16], 2, tm, tn, D_MODEL, outer="j", comm=comm)
    return (n, *outs[:3]), outs[3:]


def _ffn_fwd_down(tag, h, a, w_down):
    M = h.shape[0]
    F = w_down.shape[0]
    (h_out,) = _fused_matmul(
        tag + "_down", M, D_MODEL, F, [dict(a=a, b=w_down, acc=0)], [(h, 0)],
        lambda accs, ex: (ex[0] + 0.5 * accs[0],), [F32], 1, _tile(M, (544, 256)), D_MODEL, F, outer="j")
    return h_out


def _ffn_bwd(tag, dh, dh_b, h, norm_w, w_gu, w_down, saved, scatter=False):
    n, g, u, a = saved
    M = h.shape[0]
    F = w_down.shape[0]
    tm = _tile(M, (544, 256))
    tn = _tile(F, (1408, 704, 256))

    def swiglu_bwd(accs, ex):
        da, gv, uv = 0.5 * accs[0], ex[0].astype(F32), ex[1].astype(F32)
        s = _sigmoid(gv)
        return da * uv * _dsilu(gv, s), da * gv * s

    dg, du = _fused_matmul(
        tag + "_dact", M, F, D_MODEL, [dict(a=dh_b, b=w_down, trans_b=True, acc=0)], [(g, 0), (u, 0)],
        swiglu_bwd, [BF16, BF16], 1, tm, tn, D_MODEL, outer="j")
    dw_down = _matmul_tn(tag + "_dwd", a, dh_b, tn, D_MODEL, tm, scale=0.5)
    dw_g = _matmul_tn(tag + "_dwg", n, dg, D_MODEL, tn, tm)
    dw_u = _matmul_tn(tag + "_dwu", n, du, D_MODEL, tn, tm)
    dw_gu = jnp.concatenate([dw_g, dw_u], axis=1)
    dn, *parts = _fused_matmul(
        tag + "_dn", M, D_MODEL, F,
        [dict(a=dg, b=w_gu, trans_b=True, acc=0), dict(a=du, b=w_gu, trans_b=True, bk_off=1, acc=0)], [],
        lambda accs, ex: (accs[0],), [F32], 1, tm, D_MODEL, F, outer="i",
        comm=("scatter", [_to_cols(dw_gu), _to_rows(dw_down)]) if scatter else None)
    dh_prev, dh_prev_b, dnorm = _rmsnorm_bwd(tag + "_dnorm", dn, h, norm_w, dh)
    return (dh_prev, dh_prev_b, dnorm, *(parts if scatter else (dw_gu, dw_down)))


def kernel(x, meta_tokens, ffn1_norm, ffn1_w_gu, ffn1_w_down, mix_norm, w_in, ssd_conv_w, ssd_conv_b, ssd_dt_bias, ssd_a_log, ssd_d, ssd_norm, hg_lower_bound, hg_norm, w_branch_a, w_branch_b, w_out, ffn2_norm, ffn2_w_gu, ffn2_w_down, final_norm, loss_target, m_meta_tokens, m_ffn1_norm, m_ffn1_w_gu, m_ffn1_w_down, m_mix_norm, m_w_in, m_ssd_conv_w, m_ssd_conv_b, m_ssd_dt_bias, m_ssd_a_log, m_ssd_d, m_ssd_norm, m_hg_lower_bound, m_hg_norm, m_w_branch_a, m_w_branch_b, m_w_out, m_ffn2_norm, m_ffn2_w_gu, m_ffn2_w_down, m_final_norm, v_meta_tokens, v_ffn1_norm, v_ffn1_w_gu, v_ffn1_w_down, v_mix_norm, v_w_in, v_ssd_conv_w, v_ssd_conv_b, v_ssd_dt_bias, v_ssd_a_log, v_ssd_d, v_ssd_norm, v_hg_lower_bound, v_hg_norm, v_w_branch_a, v_w_branch_b, v_w_out, v_ffn2_norm, v_ffn2_w_gu, v_ffn2_w_down, v_final_norm):
    Bl, S, D = x.shape
    T = PAD + N_META + S
    nc = T // Q
    M = Bl * T
    me = 4 * lax.axis_index("x") + 2 * lax.axis_index("y") + lax.axis_index("c")

    bf = lambda a: a[0].astype(BF16)
    g_wgu1, g_meta, g_conv_w = _exchange("gather_first", "gather", [bf(ffn1_w_gu), meta_tokens, ssd_conv_w[0]])
    wgu1, meta_full, conv_w_full = _cols(g_wgu1), _cols(g_meta), _cols(g_conv_w)
    bias_p, alog_p, d_p = _heads_to_lanes(ssd_dt_bias), _heads_to_lanes(ssd_a_log), _heads_to_lanes(ssd_d)
    final_w = final_norm.reshape(1, D)

    h0 = jnp.concatenate([jnp.zeros((Bl, PAD, D), F32), jnp.broadcast_to(meta_full[None], (Bl, N_META, D)), x],
                         axis=1).reshape(M, D)
    tm = _tile(M, (544, 256))
    ffn1_saved, (g_wd1, g_win) = _ffn_fwd_gu("ffn1", h0, ffn1_norm, wgu1, comm=("gather", [bf(ffn1_w_down), bf(w_in)]))
    wd1, win_nat = _rows(g_wd1), _cols(g_win)
    win_main = jnp.concatenate([win_nat[:, :3072], win_nat[:, 3088:]], axis=1)
    win_dt = _heads_to_lanes(win_nat[:, 3072:3088])
    h1 = _ffn_fwd_down("ffn1", h0, ffn1_saved[3], wd1)
    un = _rmsnorm_fwd("mix_norm", h1, mix_norm)
    plain = lambda accs, ex: (accs[0],)
    proj, g_wa, g_wb, g_wo = _fused_matmul(
        "in_proj", M, N_MAIN, D, [dict(a=un, b=win_main, acc=0)], [], plain, [BF16], 1, tm, 1536, D, outer="j",
        comm=("gather", [bf(w_branch_a), bf(w_branch_b), bf(w_out)]))
    wa, wb, wo = _rows(g_wa), _rows(g_wb), _rows(g_wo)
    (dtr,) = _fused_matmul("in_proj_dt", M, 512, D, [dict(a=un, b=win_dt, acc=0)], [], plain, [F32], 1,
                           tm, 512, D, outer="j")
    xc = _conv_fwd(proj, conv_w_full, ssd_conv_b, Bl, T)
    ya, ssd_prev = _ssd_fwd(xc, dtr, proj, bias_p, alog_p, d_p, ssd_norm, Bl, nc)
    yb, hg_o, hg_st, g_wgu2, g_wd2 = _hgrn_fwd(proj, hg_lower_bound, hg_norm, Bl, nc,
                                               comm=("gather", [bf(ffn2_w_gu), bf(ffn2_w_down)]))
    wgu2, wd2 = _cols(g_wgu2), _rows(g_wd2)

    def branch_fwd(accs, ex):
        pa, pb = accs
        return pa, pb, _sigmoid(ex[0].astype(F32)) * pa + _sigmoid(ex[1].astype(F32)) * pb

    pa, pb, merged = _fused_matmul(
        "branches", M, D, D, [dict(a=ya, b=wa, acc=0), dict(a=yb, b=wb, acc=1)], [(proj, 7), (proj, 8)],
        branch_fwd, [BF16, BF16, BF16], 2, tm, D, D, outer="j")
    (h2,) = _fused_matmul("out_proj", M, D, D, [dict(a=merged, b=wo, acc=0)], [(h1, 0)],
                          lambda accs, ex: (ex[0] + accs[0],), [F32], 1, tm, D, D, outer="j")
    ffn2_saved, _ = _ffn_fwd_gu("ffn2", h2, ffn2_norm, wgu2)
    h3 = _ffn_fwd_down("ffn2", h2, ffn2_saved[3], wd2)

    dh3, dh3_b, d_final, loss_part = _loss_head(h3, final_w, loss_target, Bl, nc)
    dh2, dh2_b, d_ffn2_norm, d_wgu2, d_wd2 = _ffn_bwd("ffn2", dh3, dh3_b, h2, ffn2_norm, wgu2, wd2, ffn2_saved)

    def branch_bwd(accs, ex):
        dm = accs[0]
        ga, gb, pav, pbv = (e.astype(F32) for e in ex)
        sa, sb = _sigmoid(ga), _sigmoid(gb)
        return dm * sa, dm * sb, dm * pav * sa * (1.0 - sa), dm * pbv * sb * (1.0 - sb)

    dpa, dpb, dga, dgb = _fused_matmul(
        "d_merged", M, D, D, [dict(a=dh2_b, b=wo, trans_b=True, acc=0)], [(proj, 7), (proj, 8), (pa, 0), (pb, 0)],
        branch_bwd, [BF16] * 4, 1, tm, D, D, outer="j")
    d_wo = _matmul_tn("d_w_out", merged, dh2_b, D, D, tm)
    d_wa = _matmul_tn("d_w_a", ya, dpa, D, D, tm)
    d_wb = _matmul_tn("d_w_b", yb, dpb, D, D, tm)
    dya, dyb = _fused_matmul(
        "d_branches", M, D, D, [dict(a=dpa, b=wa, trans_b=True, acc=0), dict(a=dpb, b=wb, trans_b=True, acc=1)], [],
        lambda accs, ex: (accs[0], accs[1]), [BF16, BF16], 2, tm, D, D, outer="j")
    *ssd_grads, p_wa, p_wb, p_wo = _ssd_bwd(xc, dtr, proj, bias_p, alog_p, d_p, ssd_norm, ssd_prev, dya, Bl, nc,
                                            comm=("scatter", [_to_rows(d_wa), _to_rows(d_wb), _to_rows(d_wo)]))
    dxs, dB, dC, dz, ddtr, d_bias_p, d_alog_p, d_d_p, d_ssd_norm = ssd_grads
    dxbc, d_conv_w, d_conv_b = _conv_bwd(proj, conv_w_full, ssd_conv_b, jnp.concatenate([dxs, dB, dC], axis=1), Bl, T)
    dq, df, di, dg, d_hb, d_hg_norm, p_wgu2, p_wd2 = _hgrn_bwd(
        proj, hg_lower_bound, hg_norm, hg_o, hg_st, dyb, Bl, nc, comm=("scatter", [_to_cols(d_wgu2), _to_rows(d_wd2)]))
    dproj = jnp.concatenate([dz, dxbc, dq, df, di, dg, dga, dgb], axis=1)
    ddtr_b = ddtr.astype(BF16)
    d_win_main = _matmul_tn("d_w_in", un, dproj, D, 1536, tm)
    d_win_dt = _matmul_tn("d_w_in_dt", un, ddtr_b, D, 512, tm)
    d_win_nat = jnp.concatenate([d_win_main[:, :3072], _lanes_to_heads(d_win_dt), d_win_main[:, 3072:]], axis=1)
    (dun_dt,) = _fused_matmul("d_un_dt", M, D, 512, [dict(a=ddtr_b, b=win_dt, trans_b=True, acc=0)], [], plain, [F32], 1,
                              tm, D, 512, outer="j")
    dun, p_win = _fused_matmul("d_un", M, D, N_MAIN, [dict(a=dproj, b=win_main, trans_b=True, acc=0)], [(dun_dt, 0)],
                               lambda accs, ex: (accs[0] + ex[0],), [F32], 1, tm, D, 3072, outer="i",
                               comm=("scatter", [_to_cols(d_win_nat)]))
    dh1, dh1_b, d_mix_norm = _rmsnorm_bwd("d_mix_norm", dun, h1, mix_norm, dh2)
    dh0, _, d_ffn1_norm, p_wgu1, p_wd1 = _ffn_bwd("ffn1", dh1, dh1_b, h0, ffn1_norm, wgu1, wd1, ffn1_saved, scatter=True)

    dh0 = dh0.reshape(Bl, T, D)
    grad_x = dh0[:, PAD + N_META:]
    d_meta = dh0[:, PAD:PAD + N_META]

    small_grads = [d_ffn1_norm, d_mix_norm, d_conv_b, _lanes_to_heads(d_bias_p), _lanes_to_heads(d_alog_p),
                   _lanes_to_heads(d_d_p), d_ssd_norm, d_hb, d_hg_norm, d_ffn2_norm, d_final.reshape(D), d_conv_w]
    small_packed = _pack_rows(small_grads + [d_meta[b] for b in range(Bl)])
    parts = [p_wgu1, p_wd1, p_win, p_wa, p_wb, p_wo, p_wgu2, p_wd2]
    (small_all,) = _exchange("gather_small_grads", "gather", [small_packed])
    small_sum = _sum_parts("sum_small_grads", small_all)
    unpacked = _unpack_rows(small_sum, small_grads + [d_meta[b] for b in range(Bl)])
    g_small = unpacked[:len(small_grads)]
    g_meta_full = unpacked[len(small_grads)]
    for b in range(1, Bl):
        g_meta_full = g_meta_full + unpacked[len(small_grads) + b]
    g_meta = lax.dynamic_slice_in_dim(g_meta_full, me * (D // N_DEV), D // N_DEV, axis=1)
    g_conv_w = lax.dynamic_slice_in_dim(g_small[11], me * (SSD_CONV_CH // N_DEV), SSD_CONV_CH // N_DEV, axis=1)

    names = ["meta_tokens", "ffn1_norm", "ffn1_w_gu", "ffn1_w_down", "mix_norm", "w_in", "ssd_conv_w", "ssd_conv_b",
             "ssd_dt_bias", "ssd_a_log", "ssd_d", "ssd_norm", "hg_lower_bound", "hg_norm", "w_branch_a", "w_branch_b",
             "w_out", "ffn2_norm", "ffn2_w_gu", "ffn2_w_down", "final_norm"]
    W = dict(meta_tokens=meta_tokens, ffn1_norm=ffn1_norm, ffn1_w_gu=ffn1_w_gu, ffn1_w_down=ffn1_w_down, mix_norm=mix_norm,
             w_in=w_in, ssd_conv_w=ssd_conv_w, ssd_conv_b=ssd_conv_b, ssd_dt_bias=ssd_dt_bias, ssd_a_log=ssd_a_log,
             ssd_d=ssd_d, ssd_norm=ssd_norm, hg_lower_bound=hg_lower_bound, hg_norm=hg_norm, w_branch_a=w_branch_a,
             w_branch_b=w_branch_b, w_out=w_out, ffn2_norm=ffn2_norm, ffn2_w_gu=ffn2_w_gu, ffn2_w_down=ffn2_w_down,
             final_norm=final_norm)
    Mo = dict(meta_tokens=m_meta_tokens, ffn1_norm=m_ffn1_norm, ffn1_w_gu=m_ffn1_w_gu, ffn1_w_down=m_ffn1_w_down,
              mix_norm=m_mix_norm, w_in=m_w_in, ssd_conv_w=m_ssd_conv_w, ssd_conv_b=m_ssd_conv_b, ssd_dt_bias=m_ssd_dt_bias,
              ssd_a_log=m_ssd_a_log, ssd_d=m_ssd_d, ssd_norm=m_ssd_norm, hg_lower_bound=m_hg_lower_bound, hg_norm=m_hg_norm,
              w_branch_a=m_w_branch_a, w_branch_b=m_w_branch_b, w_out=m_w_out, ffn2_norm=m_ffn2_norm, ffn2_w_gu=m_ffn2_w_gu,
              ffn2_w_down=m_ffn2_w_down, final_norm=m_final_norm)
    Vo = dict(meta_tokens=v_meta_tokens, ffn1_norm=v_ffn1_norm, ffn1_w_gu=v_ffn1_w_gu, ffn1_w_down=v_ffn1_w_down,
              mix_norm=v_mix_norm, w_in=v_w_in, ssd_conv_w=v_ssd_conv_w, ssd_conv_b=v_ssd_conv_b, ssd_dt_bias=v_ssd_dt_bias,
              ssd_a_log=v_ssd_a_log, ssd_d=v_ssd_d, ssd_norm=v_ssd_norm, hg_lower_bound=v_hg_lower_bound, hg_norm=v_hg_norm,
              w_branch_a=v_w_branch_a, w_branch_b=v_w_branch_b, w_out=v_w_out, ffn2_norm=v_ffn2_norm, ffn2_w_gu=v_ffn2_w_gu,
              ffn2_w_down=v_ffn2_w_down, final_norm=v_final_norm)
    grads, deltas, new_m, new_v = {}, {}, {}, {}
    big_names = ["ffn1_w_gu", "ffn1_w_down", "w_in", "w_branch_a", "w_branch_b", "w_out", "ffn2_w_gu", "ffn2_w_down"]
    for nm, part in zip(big_names, parts):
        shp = W[nm].shape
        outs = _adamw("adamw_" + nm, part, W[nm][0], Mo[nm][0], Vo[nm][0])
        grads[nm], deltas[nm], new_m[nm], new_v[nm] = (o.reshape(shp) for o in outs)
    small_names = ["ffn1_norm", "mix_norm", "ssd_conv_b", "ssd_dt_bias", "ssd_a_log", "ssd_d", "ssd_norm", "hg_lower_bound",
                   "hg_norm", "ffn2_norm", "final_norm", "ssd_conv_w", "meta_tokens"]
    small_g = g_small[:11] + [g_conv_w.reshape(ssd_conv_w.shape), g_meta]
    pk = lambda d: _pack_rows([d[nm] for nm in small_names])
    outs = _adamw("adamw_small", _pack_rows(small_g)[None], pk(W), pk(Mo), pk(Vo))
    like = [W[nm] for nm in small_names]
    for dst, o in zip((grads, deltas, new_m, new_v), outs):
        for nm, val in zip(small_names, _unpack_rows(o, like)):
            dst[nm] = val

    loss = lax.psum(loss_part[0, 0], MESH_AXES)
    return (loss, grad_x, *[grads[nm] for nm in names], *[deltas[nm] for nm in names],
            *[new_m[nm] for nm in names], *[new_v[nm] for nm in names])
```

```python
import functools

import jax
import jax.numpy as jnp
from jax import lax
from jax.experimental import pallas as pl
from jax.experimental.pallas import tpu as pltpu

F32, BF16 = jnp.float32, jnp.bfloat16
NN, NT, TN = ((1,), (0,)), ((1,), (1,)), ((0,), (0,))
HI = lax.Precision.HIGHEST
MESH_AXES = ("x", "y", "c")
N_DEV = 8

D_MODEL = 1024
N_META = 16
EPS = 1e-6
SSD_HEADS, SSD_HEAD_DIM, SSD_GROUPS, SSD_STATE, SSD_CONV, Q = 16, 64, 4, 128, 4, 128
SSD_INNER = SSD_HEADS * SSD_HEAD_DIM
SSD_CONV_CH = SSD_INNER + 2 * SSD_GROUPS * SSD_STATE
HG_WIDTH, HG_HEADS, HG_CHUNK = 1024, 8, 16
PAD = Q - N_META
N_MAIN = 9 * 1024
ADAM_LR, ADAM_B1, ADAM_B2, ADAM_EPS, ADAM_WD, ADAM_STEP = 0.001, 0.9, 0.999, 1e-08, 0.01, 10
VMEM_LIMIT = 52 * 1024 * 1024


def _dot(a, b, dims, prec=None):
    return lax.dot_general(a, b, (dims, ((), ())), precision=prec, preferred_element_type=F32)


def _sigmoid(x):
    return 1.0 / (1.0 + jnp.exp(-x))


def _dsilu(x, s):
    return s * (1.0 + x * (1.0 - s))


def _softplus(x):
    e = jnp.exp(-jnp.abs(x))
    u = 1.0 + e
    log1p_e = jnp.where(u == 1.0, e, jnp.log(u) * e / (u - 1.0))
    return jnp.maximum(x, 0.0) + log1p_e


def _params(sem):
    return pltpu.CompilerParams(dimension_semantics=sem, vmem_limit_bytes=VMEM_LIMIT)


def _tile(n, prefs):
    for p in prefs:
        if n % p == 0:
            return p
    return n


CHIP_FLIPS = ((1, 0), (0, 1), (1, 1))
N_PEER = N_DEV - 1


def _comm_gather(srcs, outs, send_sems, recv_sems, local_sems):
    n = len(srcs)
    x, y, c = (lax.axis_index(a) for a in MESH_AXES)
    dev = lambda px, py, pc: 4 * px + 2 * py + pc
    me, sib = dev(x, y, c), (x, y, 1 - c)

    def rc(w, k, slot, to, src=None):
        return pltpu.make_async_remote_copy(
            src_ref=outs[w].at[slot] if src is None else src, dst_ref=outs[w].at[slot],
            send_sem=send_sems.at[w, k], recv_sem=recv_sems.at[w, k], device_id=to, device_id_type=pl.DeviceIdType.MESH)

    def local(w):
        return pltpu.make_async_copy(srcs[w], outs[w].at[me], local_sems.at[w])

    def start():
        for w in range(n):
            local(w).start()
            rc(w, 0, me, sib, src=srcs[w]).start()
            for j, (fx, fy) in enumerate(CHIP_FLIPS):
                rc(w, 1 + j, me, (x ^ fx, y ^ fy, c), src=srcs[w]).start()

    def finish():
        for w in range(n):
            for j, (fx, fy) in enumerate(CHIP_FLIPS):
                slot = dev(x ^ fx, y ^ fy, c)
                rc(w, 1 + j, slot, sib).wait_recv()
                rc(w, 4 + j, slot, sib).start()
        for w in range(n):
            rc(w, 0, dev(x, y, 1 - c), sib).wait_recv()
            rc(w, 0, me, sib, src=srcs[w]).wait_send()
            for j, (fx, fy) in enumerate(CHIP_FLIPS):
                rc(w, 4 + j, dev(x ^ fx, y ^ fy, 1 - c), sib).wait_recv()
                rc(w, 1 + j, me, sib, src=srcs[w]).wait_send()
                rc(w, 4 + j, dev(x ^ fx, y ^ fy, c), sib).wait_send()
            local(w).wait()

    return start, finish


def _comm_scatter(srcs, outs, send_sems, recv_sems, local_sems):
    n = len(srcs)
    x, y, c = (lax.axis_index(a) for a in MESH_AXES)
    me = 4 * x + 2 * y + c

    def copies():
        out = []
        for w in range(n):
            out.append(pltpu.make_async_copy(srcs[w].at[me], outs[w].at[me], local_sems.at[w]))
            for k in range(1, N_DEV):
                px, py, pc = x ^ (k >> 2), y ^ ((k >> 1) & 1), c ^ (k & 1)
                out.append(pltpu.make_async_remote_copy(
                    src_ref=srcs[w].at[4 * px + 2 * py + pc], dst_ref=outs[w].at[me],
                    send_sem=send_sems.at[w, k - 1], recv_sem=recv_sems.at[w, k - 1],
                    device_id=(px, py, pc), device_id_type=pl.DeviceIdType.MESH))
        return out

    def start():
        for cp in copies():
            cp.start()

    def finish():
        for cp in copies():
            cp.wait()

    return start, finish


def _comm_parts(comm):
    kind, arrays = comm
    n = len(arrays)
    shapes = [jax.ShapeDtypeStruct((N_DEV,) + (a.shape[1:] if kind == "scatter" else a.shape), a.dtype) for a in arrays]
    sems = [pltpu.SemaphoreType.DMA((n, N_PEER)), pltpu.SemaphoreType.DMA((n, N_PEER)), pltpu.SemaphoreType.DMA((n,))]
    return n, shapes, sems, (_comm_scatter if kind == "scatter" else _comm_gather)


def _exchange(name, kind, arrays):
    n, shapes, sems, make = _comm_parts((kind, arrays))

    def body(*refs):
        start, finish = make(refs[:n], refs[n:2 * n], *refs[2 * n:])
        start()
        finish()

    any_spec = pl.BlockSpec(memory_space=pl.ANY)
    return pl.pallas_call(
        body, name=name, in_specs=[any_spec] * n, out_specs=[any_spec] * n, out_shape=shapes, scratch_shapes=sems,
        compiler_params=pltpu.CompilerParams(has_side_effects=True),
    )(*arrays)


def _call(body, *, name, grid, in_specs, out_specs, out_shape, scratch, sem, args, comm=None):
    if comm is None:
        return pl.pallas_call(body, name=name, grid=grid, in_specs=in_specs, out_specs=out_specs, out_shape=out_shape,
                              scratch_shapes=scratch, compiler_params=_params(sem))(*args)
    n, shapes, sems, make = _comm_parts(comm)
    n_in, n_out, n_scr = len(in_specs), len(out_specs), len(scratch)

    def carrier(*refs):
        ins, csrc = refs[:n_in], refs[n_in:n_in + n]
        outs, cout = refs[n_in + n:n_in + n + n_out], refs[n_in + n + n_out:n_in + 2 * n + n_out]
        rest = refs[n_in + 2 * n + n_out:]
        start, finish = make(csrc, cout, *rest[n_scr:])
        ids = [pl.program_id(a) for a in range(len(grid))]
        first = functools.reduce(jnp.logical_and, [i == 0 for i in ids])
        last = functools.reduce(jnp.logical_and, [i == g - 1 for i, g in zip(ids, grid)])
        pl.when(first)(start)
        body(*ins, *outs, *rest[:n_scr])
        pl.when(last)(finish)

    any_spec = pl.BlockSpec(memory_space=pl.ANY)
    return pl.pallas_call(
        carrier, name=name, grid=grid, in_specs=list(in_specs) + [any_spec] * n,
        out_specs=list(out_specs) + [any_spec] * n, out_shape=list(out_shape) + shapes,
        scratch_shapes=list(scratch) + sems,
        compiler_params=pltpu.CompilerParams(dimension_semantics=("arbitrary",) * len(grid),
                                             vmem_limit_bytes=VMEM_LIMIT, has_side_effects=True),
    )(*args, *comm[1])


def _fused_matmul(name, M, N, K, pairs, extras, epilogue, out_dtypes, n_acc, tm, tn, tk, outer="i", comm=None):
    nk = K // tk
    n_pairs, n_ex, n_out = len(pairs), len(extras), len(out_dtypes)

    def ij(g0, g1):
        return (g0, g1) if outer == "i" else (g1, g0)

    in_specs, args = [], []
    for p in pairs:
        ao, bk, bn = p.get("a_off", 0), p.get("bk_off", 0), p.get("bn_off", 0)
        in_specs.append(pl.BlockSpec((tm, tk), lambda g0, g1, k, ao=ao: (ij(g0, g1)[0], k + ao)))
        if p.get("trans_b"):
            in_specs.append(pl.BlockSpec((tn, tk), lambda g0, g1, k, bk=bk, bn=bn: (ij(g0, g1)[1] + bn, k + bk)))
        else:
            in_specs.append(pl.BlockSpec((tk, tn), lambda g0, g1, k, bk=bk, bn=bn: (k + bk, ij(g0, g1)[1] + bn)))
        args += [p["a"], p["b"]]
    for arr, off in extras:
        in_specs.append(pl.BlockSpec((tm, tn), lambda g0, g1, k, off=off: (ij(g0, g1)[0], ij(g0, g1)[1] + off)))
        args.append(arr)
    out_specs = [pl.BlockSpec((tm, tn), lambda g0, g1, k: ij(g0, g1)) for _ in out_dtypes]
    out_shape = [jax.ShapeDtypeStruct((M, N), dt) for dt in out_dtypes]
    grid = (M // tm, N // tn, nk) if outer == "i" else (N // tn, M // tm, nk)

    def partials(refs):
        accs = [None] * n_acc
        for idx, p in enumerate(pairs):
            d = _dot(refs[2 * idx][...], refs[2 * idx + 1][...], NT if p.get("trans_b") else NN)
            accs[p["acc"]] = d if accs[p["acc"]] is None else accs[p["acc"]] + d
        return accs

    def finish(accs, refs):
        ex = [r[...] for r in refs[2 * n_pairs:2 * n_pairs + n_ex]]
        outs = refs[2 * n_pairs + n_ex:2 * n_pairs + n_ex + n_out]
        for o, r in zip(outs, epilogue(accs, ex)):
            o[...] = r.astype(o.dtype)

    if nk == 1:
        def body(*refs):
            finish(partials(refs), refs)
        scratch = []
    else:
        def body(*refs):
            acc_refs = refs[-n_acc:]
            k = pl.program_id(2)
            new = partials(refs)

            @pl.when(k == 0)
            def _():
                for a, v in zip(acc_refs, new):
                    a[...] = v

            @pl.when(k > 0)
            def _():
                for a, v in zip(acc_refs, new):
                    a[...] += v

            @pl.when(k == nk - 1)
            def _():
                finish([a[...] for a in acc_refs], refs)
        scratch = [pltpu.VMEM((tm, tn), F32) for _ in range(n_acc)]

    return _call(body, name=name, grid=grid, in_specs=in_specs, out_specs=out_specs, out_shape=out_shape,
                 scratch=scratch, sem=("parallel", "parallel", "arbitrary"), args=args, comm=comm)


def _matmul_tn(name, x, y, t1, t2, tr, scale=1.0):
    R, K1 = x.shape
    N1 = y.shape[1]
    nr = R // tr

    def body(x_ref, y_ref, o_ref):
        r = pl.program_id(2)
        d = _dot(x_ref[...], y_ref[...], TN)

        @pl.when(r == 0)
        def _():
            o_ref[...] = d

        @pl.when(r > 0)
        def _():
            o_ref[...] += d

        if scale != 1.0:
            @pl.when(r == nr - 1)
            def _():
                o_ref[...] = o_ref[...] * scale

    return pl.pallas_call(
        body, name=name, grid=(K1 // t1, N1 // t2, nr),
        in_specs=[pl.BlockSpec((tr, t1), lambda i, j, r: (r, i)), pl.BlockSpec((tr, t2), lambda i, j, r: (r, j))],
        out_specs=pl.BlockSpec((t1, t2), lambda i, j, r: (i, j)),
        out_shape=jax.ShapeDtypeStruct((K1, N1), F32),
        compiler_params=_params(("parallel", "parallel", "arbitrary")),
    )(x, y)


def _rmsnorm_fwd(name, h, w):
    M, D = h.shape
    tm = _tile(M, (544, 256, 128))

    def body(h_ref, w_ref, o_ref):
        x = h_ref[...]
        r = lax.rsqrt(jnp.mean(x * x, axis=-1, keepdims=True) + EPS)
        o_ref[...] = (x * r * w_ref[...]).astype(o_ref.dtype)

    return pl.pallas_call(
        body, name=name, grid=(M // tm,),
        in_specs=[pl.BlockSpec((tm, D), lambda i: (i, 0)), pl.BlockSpec((1, D), lambda i: (0, 0))],
        out_specs=pl.BlockSpec((tm, D), lambda i: (i, 0)),
        out_shape=jax.ShapeDtypeStruct((M, D), BF16), compiler_params=_params(("parallel",)),
    )(h, w)


def _rmsnorm_bwd(name, dn, h, w, dh_in):
    M, D = h.shape
    tm = _tile(M, (544, 256, 128))

    def body(dn_ref, h_ref, w_ref, dhi_ref, dh_ref, dhb_ref, dw_ref):
        x = h_ref[...]
        r = lax.rsqrt(jnp.mean(x * x, axis=-1, keepdims=True) + EPS)
        xhat = x * r
        dn_v = dn_ref[...]
        gw = dn_v * w_ref[...]
        dx = r * (gw - xhat * jnp.mean(gw * xhat, axis=-1, keepdims=True))
        dh = dhi_ref[...] + dx
        dh_ref[...] = dh
        dhb_ref[...] = dh.astype(BF16)
        dw = jnp.sum(dn_v * xhat, axis=0, keepdims=True)

        @pl.when(pl.program_id(0) == 0)
        def _():
            dw_ref[...] = dw

        @pl.when(pl.program_id(0) > 0)
        def _():
            dw_ref[...] += dw

    row = pl.BlockSpec((tm, D), lambda i: (i, 0))
    vec = pl.BlockSpec((1, D), lambda i: (0, 0))
    return pl.pallas_call(
        body, name=name, grid=(M // tm,), in_specs=[row, row, vec, row], out_specs=[row, row, vec],
        out_shape=[jax.ShapeDtypeStruct((M, D), F32), jax.ShapeDtypeStruct((M, D), BF16), jax.ShapeDtypeStruct((1, D), F32)],
        compiler_params=_params(("arbitrary",)),
    )(dn, h, w, dh_in)


def _loss_head(h, w, target, Bl, nb):
    M, D = h.shape

    def body(h_ref, w_ref, t_ref, dh_ref, dhb_ref, dw_ref, loss_ref):
        b, t = pl.program_id(0), pl.program_id(1)
        live = (t > 0).astype(F32)
        x = h_ref[...]
        r = lax.rsqrt(jnp.mean(x * x, axis=-1, keepdims=True) + EPS)
        xhat = x * r
        wv = w_ref[...]
        err = (xhat * wv - t_ref[0]) * live
        dy = err * (1.0 / D)
        gw = dy * wv
        dx = r * (gw - xhat * jnp.mean(gw * xhat, axis=-1, keepdims=True))
        dh_ref[...] = dx
        dhb_ref[...] = dx.astype(BF16)
        dw = jnp.sum(dy * xhat, axis=0, keepdims=True)
        part = 0.5 * jnp.sum(jnp.sum(err * err, axis=-1, keepdims=True) * (1.0 / D), axis=0, keepdims=True)
        first = jnp.logical_and(b == 0, t == 0)

        @pl.when(first)
        def _():
            dw_ref[...] = dw
            loss_ref[...] = jnp.broadcast_to(part, loss_ref.shape)

        @pl.when(jnp.logical_not(first))
        def _():
            dw_ref[...] += dw
            loss_ref[...] += jnp.broadcast_to(part, loss_ref.shape)

    row = pl.BlockSpec((Q, D), lambda b, t: (b * nb + t, 0))
    vec = pl.BlockSpec((1, D), lambda b, t: (0, 0))
    return pl.pallas_call(
        body, name="loss_head", grid=(Bl, nb),
        in_specs=[row, vec, pl.BlockSpec((1, Q, D), lambda b, t: (b, jnp.maximum(t - 1, 0), 0))],
        out_specs=[row, row, vec, pl.BlockSpec((8, 128), lambda b, t: (0, 0))],
        out_shape=[jax.ShapeDtypeStruct((M, D), F32), jax.ShapeDtypeStruct((M, D), BF16),
                   jax.ShapeDtypeStruct((1, D), F32), jax.ShapeDtypeStruct((8, 128), F32)],
        compiler_params=_params(("arbitrary", "arbitrary")),
    )(h, w, target)


CONV_TC = 256


def _conv_pre(xr_ref, w_ref, b_ref):
    x = xr_ref[...].astype(F32)
    acc = b_ref[...] + w_ref[SSD_CONV - 1:SSD_CONV, :] * x
    for k in range(1, SSD_CONV):
        acc = acc + w_ref[SSD_CONV - 1 - k:SSD_CONV - k, :] * pltpu.roll(x, k, 0)
    return x, acc


def _conv_fwd(proj, w, b, Bl, T):
    M = proj.shape[0]
    off = 1024 // CONV_TC

    def body(xr_ref, w_ref, b_ref, o_ref):
        _, acc = _conv_pre(xr_ref, w_ref, b_ref)
        row = lax.broadcasted_iota(jnp.int32, acc.shape, 0)
        o_ref[...] = jnp.where(row >= PAD, acc * _sigmoid(acc), 0.0).astype(o_ref.dtype)

    return pl.pallas_call(
        body, name="conv_fwd", grid=(Bl, SSD_CONV_CH // CONV_TC),
        in_specs=[pl.BlockSpec((T, CONV_TC), lambda bb, j: (bb, j + off)),
                  pl.BlockSpec((SSD_CONV, CONV_TC), lambda bb, j: (0, j)), pl.BlockSpec((1, CONV_TC), lambda bb, j: (0, j))],
        out_specs=pl.BlockSpec((T, CONV_TC), lambda bb, j: (bb, j)),
        out_shape=jax.ShapeDtypeStruct((M, SSD_CONV_CH), BF16), compiler_params=_params(("parallel", "parallel")),
    )(proj, w, b)


def _conv_bwd(proj, w, b, dxc, Bl, T):
    M = proj.shape[0]
    off = 1024 // CONV_TC

    def body(xr_ref, w_ref, b_ref, d_ref, dx_ref, dw_ref, db_ref):
        x, acc = _conv_pre(xr_ref, w_ref, b_ref)
        row = lax.broadcasted_iota(jnp.int32, acc.shape, 0)
        s = _sigmoid(acc)
        dpre = jnp.where(row >= PAD, d_ref[...].astype(F32) * _dsilu(acc, s), 0.0)
        dx = w_ref[SSD_CONV - 1:SSD_CONV, :] * dpre
        dws = [jnp.sum(dpre * x, axis=0, keepdims=True)]
        for k in range(1, SSD_CONV):
            dx = dx + w_ref[SSD_CONV - 1 - k:SSD_CONV - k, :] * pltpu.roll(dpre, T - k, 0)
            dws.append(jnp.sum(dpre * pltpu.roll(x, k, 0), axis=0, keepdims=True))
        dx_ref[...] = dx.astype(dx_ref.dtype)
        dw = jnp.concatenate(dws[::-1], axis=0)
        db = jnp.sum(dpre, axis=0, keepdims=True)

        @pl.when(pl.program_id(1) == 0)
        def _():
            dw_ref[...] = dw
            db_ref[...] = db

        @pl.when(pl.program_id(1) > 0)
        def _():
            dw_ref[...] += dw
            db_ref[...] += db

    return pl.pallas_call(
        body, name="conv_bwd", grid=(SSD_CONV_CH // CONV_TC, Bl),
        in_specs=[pl.BlockSpec((T, CONV_TC), lambda j, bb: (bb, j + off)),
                  pl.BlockSpec((SSD_CONV, CONV_TC), lambda j, bb: (0, j)), pl.BlockSpec((1, CONV_TC), lambda j, bb: (0, j)),
                  pl.BlockSpec((T, CONV_TC), lambda j, bb: (bb, j))],
        out_specs=[pl.BlockSpec((T, CONV_TC), lambda j, bb: (bb, j)),
                   pl.BlockSpec((SSD_CONV, CONV_TC), lambda j, bb: (0, j)), pl.BlockSpec((1, CONV_TC), lambda j, bb: (0, j))],
        out_shape=[jax.ShapeDtypeStruct((M, SSD_CONV_CH), BF16), jax.ShapeDtypeStruct((SSD_CONV, SSD_CONV_CH), F32),
                   jax.ShapeDtypeStruct((1, SSD_CONV_CH), F32)],
        compiler_params=_params(("parallel", "arbitrary")),
    )(proj, w, b, dxc)


def _ssd_setup(c, dtr_ref, bias_ref, alog_ref):
    row = lax.broadcasted_iota(jnp.int32, (Q, 128), 0)
    col = lax.broadcasted_iota(jnp.int32, (Q, 128), 1)
    valid = jnp.logical_or(c > 0, row >= PAD)
    pre = dtr_ref[...] + bias_ref[...]
    dt = jnp.where(valid, _softplus(pre), 0.0)
    A = -jnp.exp(alog_ref[...])
    tri = row >= col
    cs = _dot(tri.astype(F32), dt * A, NN, HI)
    cst = _dot((row == col).astype(F32), cs, NT, HI)
    return dt, A, cs, cst, tri, valid, pre, row, col


def _pair_terms(p, dt, cs, col):
    h0, h1 = 2 * p, 2 * p + 1
    first = col < SSD_HEAD_DIM
    pick = lambda a: jnp.where(first, a[:, h0:h0 + 1], a[:, h1:h1 + 1])
    cl0, cl1 = cs[Q - 1:Q, h0:h0 + 1], cs[Q - 1:Q, h1:h1 + 1]
    cs_p = pick(cs)
    cl_p = jnp.where(first[0:1], cl0, cl1)
    return first, pick(dt), jnp.exp(cs_p), jnp.exp(cl_p - cs_p), (cl0, cl1)


def _ssd_core(x_ref, b_ref, c_ref, d_ref, state_of, dt, cs, cst, tri, row, col):
    xv = x_ref[...].astype(F32)
    Bg, Cg = b_ref[...], c_ref[...]
    CB = _dot(Cg, Bg, NT)
    ys, new_states, Ms = [], [], []
    for p in range(2):
        first, dt_p, ecs_p, decay_p, (cl0, cl1) = _pair_terms(p, dt, cs, col)
        x_p = xv[:, 128 * p:128 * (p + 1)]
        X_p = x_p * dt_p
        yd = jnp.zeros((Q, 128), F32)
        for hh in range(2):
            h = 2 * p + hh
            Lm = jnp.exp(jnp.where(tri, cs[:, h:h + 1] - cst[h:h + 1, :], -jnp.inf))
            Mh = CB * Lm
            Ms.append(Mh)
            Xm = jnp.where(first if hh == 0 else jnp.logical_not(first), X_p, 0.0).astype(BF16)
            yd = yd + _dot(Mh.astype(BF16), Xm, NN)
        prev = state_of(p)
        yo = _dot(Cg, prev.astype(BF16), NT) * ecs_p
        st = _dot((X_p * decay_p).astype(BF16), Bg, TN)
        ecl_rows = jnp.where(row < SSD_HEAD_DIM, jnp.exp(cl0), jnp.exp(cl1))
        new_states.append(prev * ecl_rows + st)
        d_p = jnp.where(first[0:1], d_ref[:, 2 * p:2 * p + 1], d_ref[:, 2 * p + 1:2 * p + 2])
        ys.append(yd + yo + x_p * d_p)
    return jnp.concatenate(ys, axis=1), new_states, Ms, CB, xv


def _ssd_specs(nc):
    rb = lambda g, b, c: b * nc + c
    return [
        pl.BlockSpec((Q, 256), lambda g, b, c: (rb(g, b, c), g)),
        pl.BlockSpec((Q, 128), lambda g, b, c: (rb(g, b, c), 8 + g)),
        pl.BlockSpec((Q, 128), lambda g, b, c: (rb(g, b, c), 12 + g)),
        pl.BlockSpec((Q, 128), lambda g, b, c: (rb(g, b, c), g)),
        pl.BlockSpec((Q, 256), lambda g, b, c: (rb(g, b, c), g)),
        pl.BlockSpec((1, 128), lambda g, b, c: (0, g)),
        pl.BlockSpec((1, 128), lambda g, b, c: (0, g)),
        pl.BlockSpec((1, 128), lambda g, b, c: (0, g)),
        pl.BlockSpec((1, 256), lambda g, b, c: (0, g)),
    ]


def _ssd_fwd(xc, dtr, proj, bias_p, alog_p, d_p, nw, Bl, nc):
    M = xc.shape[0]

    def body(x_ref, b_ref, c_ref, dtr_ref, z_ref, bias_ref, alog_ref, d_ref, nw_ref, y_ref, prev_ref, state):
        c = pl.program_id(2)

        @pl.when(c == 0)
        def _():
            state[...] = jnp.zeros_like(state)

        dt, _, cs, cst, tri, _, _, row, col = _ssd_setup(c, dtr_ref, bias_ref, alog_ref)
        y, new_states, _, _, _ = _ssd_core(x_ref, b_ref, c_ref, d_ref, lambda p: state[p], dt, cs, cst, tri, row, col)
        for p in range(2):
            prev_ref[0, 0, 0, p] = state[p]
            state[p] = new_states[p]
        zz = z_ref[...].astype(F32)
        yg = y * zz * _sigmoid(zz)
        r = lax.rsqrt(jnp.mean(yg * yg, axis=-1, keepdims=True) + EPS)
        y_ref[...] = (yg * r * nw_ref[...]).astype(y_ref.dtype)

    return pl.pallas_call(
        body, name="ssd_fwd", grid=(SSD_GROUPS, Bl, nc), in_specs=_ssd_specs(nc),
        out_specs=[pl.BlockSpec((Q, 256), lambda g, b, c: (b * nc + c, g)),
                   pl.BlockSpec((1, 1, 1, 2, 128, 128), lambda g, b, c: (b, g, c, 0, 0, 0))],
        out_shape=[jax.ShapeDtypeStruct((M, SSD_INNER), BF16), jax.ShapeDtypeStruct((Bl, SSD_GROUPS, nc, 2, 128, 128), F32)],
        scratch_shapes=[pltpu.VMEM((2, 128, 128), F32)],
        compiler_params=_params(("parallel", "arbitrary", "arbitrary")),
    )(xc, xc, xc, dtr, proj, bias_p, alog_p, d_p, nw)


def _ssd_bwd(xc, dtr, proj, bias_p, alog_p, d_p, nw, prev, dya, Bl, nc, comm=None):
    M = xc.shape[0]
    rev = lambda spec: pl.BlockSpec(spec.block_shape, lambda g, b, c, f=spec.index_map: f(g, b, nc - 1 - c))

    def body(x_ref, b_ref, c_ref, dtr_ref, z_ref, bias_ref, alog_ref, d_ref, nw_ref, prev_ref, dy_ref,
             dx_ref, dB_ref, dC_ref, dz_ref, ddtr_ref, dbias_ref, dalog_ref, dd_ref, dnw_ref, dS):
        b, t = pl.program_id(1), pl.program_id(2)
        c = nc - 1 - t

        @pl.when(t == 0)
        def _():
            dS[...] = jnp.zeros_like(dS)

        dt, A, cs, cst, tri, valid, pre, row, col = _ssd_setup(c, dtr_ref, bias_ref, alog_ref)
        y, _, Ms, CB, xv = _ssd_core(x_ref, b_ref, c_ref, d_ref, lambda p: prev_ref[0, 0, 0, p], dt, cs, cst, tri, row, col)
        Bg, Cg = b_ref[...], c_ref[...]
        Bf = Bg.astype(F32)

        zz = z_ref[...].astype(F32)
        sz = _sigmoid(zz)
        silu_z = zz * sz
        yg = y * silu_z
        r = lax.rsqrt(jnp.mean(yg * yg, axis=-1, keepdims=True) + EPS)
        xhat = yg * r
        dout = dy_ref[...].astype(F32)
        gw = dout * nw_ref[...]
        dyg = r * (gw - xhat * jnp.mean(gw * xhat, axis=-1, keepdims=True))
        dnw = jnp.sum(dout * xhat, axis=0, keepdims=True)
        dz_ref[...] = (dyg * y * _dsilu(zz, sz)).astype(dz_ref.dtype)
        dy = dyg * silu_z

        lane1 = lax.broadcasted_iota(jnp.int32, (1, 128), 1)
        put_col = lambda h, v: jnp.where(col == h, v, 0.0)
        put_lane = lambda h, v: jnp.where(lane1 == h, v, 0.0)
        dcs = jnp.zeros((Q, 128), F32)
        dcs_t = jnp.zeros((128, Q), F32)
        dcl = jnp.zeros((1, 128), F32)
        ddt = jnp.zeros((Q, 128), F32)
        dD = jnp.zeros((1, 128), F32)
        dCB = jnp.zeros((Q, Q), F32)
        dBacc = jnp.zeros((Q, 128), F32)
        dCacc = jnp.zeros((Q, 128), F32)
        dxs = []
        for p in range(2):
            first, dt_p, ecs_p, decay_p, (cl0, cl1) = _pair_terms(p, dt, cs, col)
            halves = (first, jnp.logical_not(first))
            x_p = xv[:, 128 * p:128 * (p + 1)]
            X_p = x_p * dt_p
            dy_p = dy[:, 128 * p:128 * (p + 1)]
            prev_p = prev_ref[0, 0, 0, p]
            prev_b = prev_p.astype(BF16)
            dS_p = dS[p]
            dS_b = dS_p.astype(BF16)
            dX = decay_p * _dot(Bg, dS_b, NT)
            Yo = _dot(Cg, prev_b, NT)
            dYo = (dy_p * ecs_p).astype(BF16)
            dCacc = dCacc + _dot(dYo, prev_b, NN)
            dprev = _dot(dYo, Cg, TN)
            off_cs = dy_p * Yo * ecs_p
            state_cs = dS_p * prev_p
            for hh in range(2):
                h = 2 * p + hh
                hm = halves[hh]
                Mh = Ms[h]
                dyh = jnp.where(hm, dy_p, 0.0).astype(BF16)
                Xm = jnp.where(hm, X_p, 0.0).astype(BF16)
                dX = dX + _dot(Mh.astype(BF16), dyh, TN)
                dM = _dot(dyh, Xm, NT)
                W = dM * Mh
                Lm = jnp.exp(jnp.where(tri, cs[:, h:h + 1] - cst[h:h + 1, :], -jnp.inf))
                dCB = dCB + dM * Lm
                XdS = _dot(Xm, dS_b, NN)
                decay_h = decay_p[:, 64 * hh:64 * hh + 1]
                dBacc = dBacc + decay_h * XdS
                tdec = jnp.sum(XdS * Bf, axis=1, keepdims=True) * decay_h
                dcs = dcs + put_col(h, jnp.sum(W, axis=1, keepdims=True)
                                    + jnp.sum(jnp.where(hm, off_cs, 0.0), axis=1, keepdims=True) - tdec)
                dcs_t = dcs_t - jnp.where(lax.broadcasted_iota(jnp.int32, (128, Q), 0) == h,
                                          jnp.sum(W, axis=0, keepdims=True), 0.0)
                ecl = jnp.exp(cl0 if hh == 0 else cl1)
                rows_h = (row < SSD_HEAD_DIM) if hh == 0 else (row >= SSD_HEAD_DIM)
                dcl = dcl + put_lane(h, jnp.sum(tdec, axis=0, keepdims=True)
                                     + ecl * jnp.sum(jnp.sum(jnp.where(rows_h, state_cs, 0.0), axis=1, keepdims=True),
                                                     axis=0, keepdims=True))
                ddt = ddt + put_col(h, jnp.sum(jnp.where(hm, dX * x_p, 0.0), axis=1, keepdims=True))
                dD = dD + put_lane(h, jnp.sum(jnp.sum(jnp.where(hm, dy_p * x_p, 0.0), axis=1, keepdims=True),
                                              axis=0, keepdims=True))
            ecl_rows = jnp.where(row < SSD_HEAD_DIM, jnp.exp(cl0), jnp.exp(cl1))
            dS[p] = dS_p * ecl_rows + dprev
            d_pp = jnp.where(first[0:1], d_ref[:, 2 * p:2 * p + 1], d_ref[:, 2 * p + 1:2 * p + 2])
            dxs.append(dy_p * d_pp + dX * dt_p)
        dCB_b = dCB.astype(BF16)
        dCacc = dCacc + _dot(dCB_b, Bg, NN)
        dBacc = dBacc + _dot(dCB_b, Cg, TN)
        dx_ref[...] = jnp.concatenate(dxs, axis=1).astype(dx_ref.dtype)
        dB_ref[...] = dBacc.astype(dB_ref.dtype)
        dC_ref[...] = dCacc.astype(dC_ref.dtype)

        dcs = dcs + _dot((row == col).astype(F32), dcs_t, NT, HI) + jnp.where(row == Q - 1, dcl, 0.0)
        da = _dot((row <= col).astype(F32), dcs, NN, HI)
        ddt = ddt + da * A
        dpre = jnp.where(valid, ddt * _sigmoid(pre), 0.0)
        ddtr_ref[...] = dpre
        dbias = jnp.sum(dpre, axis=0, keepdims=True)
        dalog = jnp.sum(da * dt, axis=0, keepdims=True) * A
        first_step = jnp.logical_and(b == 0, t == 0)

        @pl.when(first_step)
        def _():
            dbias_ref[...] = dbias
            dalog_ref[...] = dalog
            dd_ref[...] = dD
            dnw_ref[...] = dnw

        @pl.when(jnp.logical_not(first_step))
        def _():
            dbias_ref[...] += dbias
            dalog_ref[...] += dalog
            dd_ref[...] += dD
            dnw_ref[...] += dnw

    fwd_specs = _ssd_specs(nc)
    in_specs = [rev(s) for s in fwd_specs] + [
        pl.BlockSpec((1, 1, 1, 2, 128, 128), lambda g, b, c: (b, g, nc - 1 - c, 0, 0, 0)),
        pl.BlockSpec((Q, 256), lambda g, b, c: (b * nc + nc - 1 - c, g))]
    rowblk = lambda w: pl.BlockSpec((Q, w), lambda g, b, c: (b * nc + nc - 1 - c, g))
    vec = lambda w: pl.BlockSpec((1, w), lambda g, b, c: (0, g))
    return _call(
        body, name="ssd_bwd", grid=(SSD_GROUPS, Bl, nc), in_specs=in_specs,
        out_specs=[rowblk(256), rowblk(128), rowblk(128), rowblk(256), rowblk(128), vec(128), vec(128), vec(128), vec(256)],
        out_shape=[jax.ShapeDtypeStruct((M, SSD_INNER), BF16), jax.ShapeDtypeStruct((M, 512), BF16),
                   jax.ShapeDtypeStruct((M, 512), BF16), jax.ShapeDtypeStruct((M, SSD_INNER), BF16),
                   jax.ShapeDtypeStruct((M, 512), F32), jax.ShapeDtypeStruct((1, 512), F32),
                   jax.ShapeDtypeStruct((1, 512), F32), jax.ShapeDtypeStruct((1, 512), F32),
                   jax.ShapeDtypeStruct((1, SSD_INNER), F32)],
        scratch=[pltpu.VMEM((2, 128, 128), F32)], sem=("parallel", "arbitrary", "arbitrary"),
        args=(xc, xc, xc, dtr, proj, bias_p, alog_p, d_p, nw, prev, dya), comm=comm)


NSUB = Q // HG_CHUNK
HG_HP = 8
EXP_CAP = 80.0


def _hg_setup(blk, q_ref, f_ref, hb_ref):
    row = lax.broadcasted_iota(jnp.int32, (Q, Q), 0)
    col = lax.broadcasted_iota(jnp.int32, (Q, Q), 1)
    same = (row // HG_CHUNK) == (col // HG_CHUNK)
    causal = jnp.logical_and(same, col <= row)
    lb = _sigmoid(hb_ref[0:1, :] - hb_ref[1:2, :])
    fl = f_ref[...].astype(F32)
    sg = _sigmoid(fl)
    fg = lb + (1.0 - lb) * sg
    k = (1.0 - lb) * (1.0 - sg)
    gl = jnp.log(fg)
    G = _dot(causal.astype(F32), gl, NN, HI)
    T = _dot(same.astype(F32), gl, NN, HI)
    qv = q_ref[...].astype(F32)
    sq = _sigmoid(qv)
    eG = jnp.exp(G)
    eGn = jnp.exp(jnp.minimum(-G, EXP_CAP))
    eTG = jnp.exp(T - G)
    qt = qv * sq * eG
    kt = k * eGn
    kh = k * eTG
    valid = jnp.logical_or(blk > 0, row[:, :1] >= PAD)
    return dict(row=row, col=col, same=same, causal=causal, lb=lb, sg=sg, fg=fg, k=k, T=T, qv=qv, sq=sq,
                eG=eG, eGn=eGn, eTG=eTG, qt=qt, kt=kt, kh=kh, valid=valid)


def _hg_specs(nb, rev=False):
    rb = (lambda h, b, t: b * nb + nb - 1 - t) if rev else (lambda h, b, t: b * nb + t)
    w = 128 * HG_HP
    blk = lambda off: pl.BlockSpec((Q, w), lambda h, b, t, off=off: (rb(h, b, t), off // HG_HP + h))
    return [blk(24), blk(32), blk(40), blk(48),
            pl.BlockSpec((2, w), lambda h, b, t: (0, h)), pl.BlockSpec((1, w), lambda h, b, t: (0, h))]


HEAD_LANES = tuple(slice(128 * hh, 128 * (hh + 1)) for hh in range(HG_HP))


def _per_head(fn, *arrs):
    return jnp.concatenate([jnp.broadcast_to(fn(*(a[:, ln] for a in arrs)), (arrs[0].shape[0], 128))
                            for ln in HEAD_LANES], axis=1)


def _hgrn_fwd(proj, hb, nw, Bl, nb, comm=None):
    M = proj.shape[0]

    def body(q_ref, f_ref, i_ref, g_ref, hb_ref, nw_ref, y_ref, o_ref, st_ref, S):
        blk = pl.program_id(2)

        @pl.when(blk == 0)
        def _():
            S[...] = jnp.zeros_like(S)

        s = _hg_setup(blk, q_ref, f_ref, hb_ref)
        v = i_ref[...]
        qt_b, kt_b, kh_b = s["qt"].astype(BF16), s["kt"].astype(BF16), s["kh"].astype(BF16)
        eT = jnp.exp(s["T"])
        att = [jnp.where(s["causal"], _dot(qt_b[:, ln], kt_b[:, ln], NT), 0.0).astype(BF16) for ln in HEAD_LANES]
        o_intra = [_dot(att[hh], v[:, ln], NN) for hh, ln in enumerate(HEAD_LANES)]
        for j in range(NSUB):
            sl = slice(HG_CHUNK * j, HG_CHUNK * (j + 1))
            for hh, ln in enumerate(HEAD_LANES):
                St = S[hh]
                st_ref[0, hh, 0, j] = St
                o_ref[sl, ln] = o_intra[hh][sl] + _dot(qt_b[sl, ln], St.astype(BF16), NT)
                S[hh] = St * eT[HG_CHUNK * j:HG_CHUNK * j + 1, ln] + _dot(v[sl, ln], kh_b[sl, ln], TN)
        o = o_ref[...]
        r = _per_head(lambda a: lax.rsqrt(jnp.mean(a * a, axis=-1, keepdims=True) + EPS), o)
        gv = g_ref[...].astype(F32)
        y_ref[...] = (o * r * nw_ref[...] * gv * _sigmoid(gv)).astype(y_ref.dtype)

    rowblk = pl.BlockSpec((Q, 128 * HG_HP), lambda h, b, t: (b * nb + t, h))
    return _call(
        body, name="hgrn_fwd", grid=(HG_HEADS // HG_HP, Bl, nb), in_specs=_hg_specs(nb),
        out_specs=[rowblk, rowblk,
                   pl.BlockSpec((1, HG_HP, 1, NSUB, 128, 128), lambda h, b, t: (b, h, t, 0, 0, 0))],
        out_shape=[jax.ShapeDtypeStruct((M, HG_WIDTH), BF16), jax.ShapeDtypeStruct((M, HG_WIDTH), F32),
                   jax.ShapeDtypeStruct((Bl, HG_HEADS, nb, NSUB, 128, 128), F32)],
        scratch=[pltpu.VMEM((HG_HP, 128, 128), F32)], sem=("parallel", "arbitrary", "arbitrary"),
        args=(proj, proj, proj, proj, hb, nw), comm=comm)


def _hgrn_bwd(proj, hb, nw, o_saved, st_saved, dyb, Bl, nb, comm=None):
    M = proj.shape[0]

    def body(q_ref, f_ref, i_ref, g_ref, hb_ref, nw_ref, o_ref, st_ref, dy_ref,
             dq_ref, df_ref, di_ref, dg_ref, dhb_ref, dnw_ref, dS, a_dqt, a_dv, a_dkh, a_dgl):
        b, t = pl.program_id(1), pl.program_id(2)

        @pl.when(t == 0)
        def _():
            dS[...] = jnp.zeros_like(dS)

        first_step = jnp.logical_and(b == 0, t == 0)
        s = _hg_setup(nb - 1 - t, q_ref, f_ref, hb_ref)
        v = i_ref[...]
        qt_b, kt_b, kh_b = s["qt"].astype(BF16), s["kt"].astype(BF16), s["kh"].astype(BF16)
        eT = jnp.exp(s["T"])
        att = [jnp.where(s["causal"], _dot(qt_b[:, ln], kt_b[:, ln], NT), 0.0).astype(BF16) for ln in HEAD_LANES]

        o = o_ref[...]
        r = _per_head(lambda a: lax.rsqrt(jnp.mean(a * a, axis=-1, keepdims=True) + EPS), o)
        xhat = o * r
        gv = g_ref[...].astype(F32)
        sgv = _sigmoid(gv)
        dyv = dy_ref[...].astype(F32)
        d_on = dyv * gv * sgv
        dg_out = dyv * xhat * nw_ref[...] * _dsilu(gv, sgv)
        gw = d_on * nw_ref[...]
        do = r * (gw - xhat * _per_head(lambda a, c: jnp.mean(a * c, axis=-1, keepdims=True), gw, xhat))
        dnw = jnp.sum(d_on * xhat, axis=0, keepdims=True)
        do_b = do.astype(BF16)

        datt = [jnp.where(s["causal"], _dot(do_b[:, ln], v[:, ln], NT), 0.0).astype(BF16) for ln in HEAD_LANES]
        dqt = jnp.concatenate([_dot(datt[hh], kt_b[:, ln], NN) for hh, ln in enumerate(HEAD_LANES)], axis=1)
        dkt = jnp.concatenate([_dot(datt[hh], qt_b[:, ln], TN) for hh, ln in enumerate(HEAD_LANES)], axis=1)
        dv = jnp.concatenate([_dot(att[hh], do_b[:, ln], TN) for hh, ln in enumerate(HEAD_LANES)], axis=1)
        last_row = (lax.broadcasted_iota(jnp.int32, (HG_CHUNK, 128), 0) == HG_CHUNK - 1)
        for j in reversed(range(NSUB)):
            sl = slice(HG_CHUNK * j, HG_CHUNK * (j + 1))
            for hh, ln in enumerate(HEAD_LANES):
                St = st_ref[0, hh, 0, j]
                dSt = dS[hh]
                St_b, dSt_b = St.astype(BF16), dSt.astype(BF16)
                eT_j = eT[HG_CHUNK * j:HG_CHUNK * j + 1, ln]
                dkh_j = _dot(v[sl, ln], dSt_b, NN)
                a_dqt[sl, ln] = _dot(do_b[sl, ln], St_b, NN)
                a_dv[sl, ln] = _dot(kh_b[sl, ln], dSt_b, NT)
                a_dkh[sl, ln] = dkh_j
                dlast = (jnp.sum(St * dSt, axis=0, keepdims=True) * eT_j
                         + jnp.sum(dkh_j * s["kh"][sl, ln], axis=0, keepdims=True))
                a_dgl[sl, ln] = jnp.where(last_row, dlast, 0.0)
                dS[hh] = dSt * eT_j + _dot(do_b[sl, ln], qt_b[sl, ln], TN)
        dqt = dqt + a_dqt[...]
        dv = dv + a_dv[...]
        dkh = a_dkh[...]
        dG = dqt * s["qt"] - dkt * s["kt"] - dkh * s["kh"] + a_dgl[...]
        rev_causal = jnp.logical_and(s["same"], s["col"] >= s["row"])
        dgl = _dot(rev_causal.astype(F32), dG, NN, HI)
        dk = dkt * s["eGn"] + dkh * s["eTG"]
        dfg = dgl / s["fg"] - dk
        lb, sg = s["lb"], s["sg"]
        keep = s["valid"].astype(F32)
        df_ref[...] = (dfg * (1.0 - lb) * sg * (1.0 - sg) * keep).astype(df_ref.dtype)
        dq_ref[...] = (dqt * s["eG"] * _dsilu(s["qv"], s["sq"]) * keep).astype(dq_ref.dtype)
        di_ref[...] = (dv * keep).astype(di_ref.dtype)
        dg_ref[...] = (dg_out * keep).astype(dg_ref.dtype)
        dlb = jnp.sum(dfg * (1.0 - sg) * keep, axis=0, keepdims=True) * lb * (1.0 - lb)
        dhb = jnp.concatenate([dlb, -dlb], axis=0)

        @pl.when(first_step)
        def _():
            dhb_ref[...] = dhb
            dnw_ref[...] = dnw

        @pl.when(jnp.logical_not(first_step))
        def _():
            dhb_ref[...] += dhb
            dnw_ref[...] += dnw

    w = 128 * HG_HP
    rowblk = pl.BlockSpec((Q, w), lambda h, b, t: (b * nb + nb - 1 - t, h))
    return _call(
        body, name="hgrn_bwd", grid=(HG_HEADS // HG_HP, Bl, nb),
        in_specs=_hg_specs(nb, rev=True) + [
            rowblk, pl.BlockSpec((1, HG_HP, 1, NSUB, 128, 128), lambda h, b, t: (b, h, nb - 1 - t, 0, 0, 0)), rowblk],
        out_specs=[rowblk, rowblk, rowblk, rowblk,
                   pl.BlockSpec((2, w), lambda h, b, t: (0, h)), pl.BlockSpec((1, w), lambda h, b, t: (0, h))],
        out_shape=[jax.ShapeDtypeStruct((M, HG_WIDTH), BF16)] * 4 + [
            jax.ShapeDtypeStruct((2, HG_WIDTH), F32), jax.ShapeDtypeStruct((1, HG_WIDTH), F32)],
        scratch=[pltpu.VMEM((HG_HP, 128, 128), F32)] + [pltpu.VMEM((Q, w), F32)] * 4,
        sem=("parallel", "arbitrary", "arbitrary"),
        args=(proj, proj, proj, proj, hb, nw, o_saved, st_saved, dyb), comm=comm)


def _adamw(name, parts, w, m, v):
    R, C = w.shape
    S = parts.shape[0]
    tr = _tile(R, (256, 176, 128, 64, 8))
    c1, c2 = 1.0 - ADAM_B1 ** ADAM_STEP, 1.0 - ADAM_B2 ** ADAM_STEP

    def body(p_ref, w_ref, m_ref, v_ref, g_ref, d_ref, nm_ref, nv_ref):
        g = p_ref[0].astype(F32)
        for s in range(1, S):
            g = g + p_ref[s].astype(F32)
        nm = ADAM_B1 * m_ref[...] + (1.0 - ADAM_B1) * g
        nv = ADAM_B2 * v_ref[...] + (1.0 - ADAM_B2) * (g * g)
        g_ref[...] = g
        nm_ref[...] = nm
        nv_ref[...] = nv
        d_ref[...] = -ADAM_LR * ((nm / c1) / (jnp.sqrt(nv / c2) + ADAM_EPS) + ADAM_WD * w_ref[...])

    blk = pl.BlockSpec((tr, C), lambda i: (i, 0))
    return pl.pallas_call(
        body, name=name, grid=(R // tr,),
        in_specs=[pl.BlockSpec((S, tr, C), lambda i: (0, i, 0)), blk, blk, blk], out_specs=[blk] * 4,
        out_shape=[jax.ShapeDtypeStruct((R, C), F32)] * 4, compiler_params=_params(("parallel",)),
    )(parts, w, m, v)


def _sum_parts(name, parts):
    S, R, C = parts.shape

    def body(p_ref, o_ref):
        g = p_ref[0]
        for s in range(1, S):
            g = g + p_ref[s]
        o_ref[...] = g

    return pl.pallas_call(
        body, name=name, out_shape=jax.ShapeDtypeStruct((R, C), F32),
        in_specs=[pl.BlockSpec(memory_space=pltpu.VMEM)], out_specs=pl.BlockSpec(memory_space=pltpu.VMEM),
    )(parts)


def _heads_to_lanes(p):
    lead = p.shape[:-1]
    p4 = p.reshape(lead + (SSD_GROUPS, 4))
    p4 = jnp.pad(p4, [(0, 0)] * len(lead) + [(0, 0), (0, 124)])
    return p4.reshape(lead + (512,))


def _lanes_to_heads(p):
    lead = p.shape[:-1]
    return p.reshape(lead + (SSD_GROUPS, 128))[..., :4].reshape(lead + (SSD_HEADS,))


def _pack_rows(arrs):
    rows = []
    for a in arrs:
        f = a.reshape(-1).astype(F32)
        n = -(-f.shape[0] // D_MODEL) * D_MODEL
        rows.append(jnp.pad(f, (0, n - f.shape[0])).reshape(-1, D_MODEL))
    out = jnp.concatenate(rows, axis=0)
    return jnp.pad(out, ((0, (-out.shape[0]) % 8), (0, 0)))


def _unpack_rows(packed, like):
    outs, r = [], 0
    for a in like:
        n = 1
        for s in a.shape:
            n *= s
        nr = -(-n // D_MODEL)
        outs.append(packed[r:r + nr].reshape(-1)[:n].reshape(a.shape))
        r += nr
    return outs


def _cols(gth):
    return jnp.transpose(gth, (1, 0, 2)).reshape(gth.shape[1], -1)


def _rows(gth):
    return gth.reshape(-1, gth.shape[2])


def _to_cols(g):
    return jnp.transpose(g.reshape(g.shape[0], N_DEV, -1), (1, 0, 2)).astype(BF16)


def _to_rows(g):
    return g.reshape(N_DEV, -1, g.shape[1]).astype(BF16)


def _ffn_fwd_gu(tag, h, norm_w, w_gu, comm=None):
    M = h.shape[0]
    F = w_gu.shape[1] // 2
    tm = _tile(M, (544, 256))
    n = _rmsnorm_fwd(tag + "_norm", h, norm_w)
    tn = _tile(F, (1408, 704, 256))
    outs = _fused_matmul(
        tag + "_gu", M, F, D_MODEL,
        [dict(a=n, b=w_gu, acc=0), dict(a=n, b=w_gu, bn_off=F // tn, acc=1)], [],
        lambda accs, ex: (accs[0], accs[1], accs[0] * _sigmoid(accs[0]) * accs[1]),
        [BF16, BF16, BF16], 2, tm, tn, D_MODEL, outer="j", comm=comm)
    return (n, *outs[:3]), outs[3:]


def _ffn_fwd_down(tag, h, a, w_down):
    M = h.shape[0]
    F = w_down.shape[0]
    (h_out,) = _fused_matmul(
        tag + "_down", M, D_MODEL, F, [dict(a=a, b=w_down, acc=0)], [(h, 0)],
        lambda accs, ex: (ex[0] + 0.5 * accs[0],), [F32], 1, _tile(M, (544, 256)), D_MODEL, F, outer="j")
    return h_out


def _ffn_bwd(tag, dh, dh_b, h, norm_w, w_gu, w_down, saved, scatter=False):
    n, g, u, a = saved
    M = h.shape[0]
    F = w_down.shape[0]
    tm = _tile(M, (544, 256))
    tn = _tile(F, (1408, 704, 256))

    def swiglu_bwd(accs, ex):
        da, gv, uv = 0.5 * accs[0], ex[0].astype(F32), ex[1].astype(F32)
        s = _sigmoid(gv)
        return da * uv * _dsilu(gv, s), da * gv * s

    dg, du = _fused_matmul(
        tag + "_dact", M, F, D_MODEL, [dict(a=dh_b, b=w_down, trans_b=True, acc=0)], [(g, 0), (u, 0)],
        swiglu_bwd, [BF16, BF16], 1, tm, tn, D_MODEL, outer="j")
    dw_down = _matmul_tn(tag + "_dwd", a, dh_b, tn, D_MODEL, tm, scale=0.5)
    dw_g = _matmul_tn(tag + "_dwg", n, dg, D_MODEL, tn, tm)
    dw_u = _matmul_tn(tag + "_dwu", n, du, D_MODEL, tn, tm)
    dw_gu = jnp.concatenate([dw_g, dw_u], axis=1)
    dn, *parts = _fused_matmul(
        tag + "_dn", M, D_MODEL, F,
        [dict(a=dg, b=w_gu, trans_b=True, acc=0), dict(a=du, b=w_gu, trans_b=True, bk_off=1, acc=0)], [],
        lambda accs, ex: (accs[0],), [F32], 1, tm, D_MODEL, F, outer="i",
        comm=("scatter", [_to_cols(dw_gu), _to_rows(dw_down)]) if scatter else None)
    dh_prev, dh_prev_b, dnorm = _rmsnorm_bwd(tag + "_dnorm", dn, h, norm_w, dh)
    return (dh_prev, dh_prev_b, dnorm, *(parts if scatter else (dw_gu, dw_down)))


def kernel(x, meta_tokens, ffn1_norm, ffn1_w_gu, ffn1_w_down, mix_norm, w_in, ssd_conv_w, ssd_conv_b, ssd_dt_bias, ssd_a_log, ssd_d, ssd_norm, hg_lower_bound, hg_norm, w_branch_a, w_branch_b, w_out, ffn2_norm, ffn2_w_gu, ffn2_w_down, final_norm, loss_target, m_meta_tokens, m_ffn1_norm, m_ffn1_w_gu, m_ffn1_w_down, m_mix_norm, m_w_in, m_ssd_conv_w, m_ssd_conv_b, m_ssd_dt_bias, m_ssd_a_log, m_ssd_d, m_ssd_norm, m_hg_lower_bound, m_hg_norm, m_w_branch_a, m_w_branch_b, m_w_out, m_ffn2_norm, m_ffn2_w_gu, m_ffn2_w_down, m_final_norm, v_meta_tokens, v_ffn1_norm, v_ffn1_w_gu, v_ffn1_w_down, v_mix_norm, v_w_in, v_ssd_conv_w, v_ssd_conv_b, v_ssd_dt_bias, v_ssd_a_log, v_ssd_d, v_ssd_norm, v_hg_lower_bound, v_hg_norm, v_w_branch_a, v_w_branch_b, v_w_out, v_ffn2_norm, v_ffn2_w_gu, v_ffn2_w_down, v_final_norm):
    Bl, S, D = x.shape
    T = PAD + N_META + S
    nc = T // Q
    M = Bl * T
    me = 4 * lax.axis_index("x") + 2 * lax.axis_index("y") + lax.axis_index("c")

    bf = lambda a: a[0].astype(BF16)
    g_wgu1, g_meta, g_conv_w = _exchange("gather_first", "gather", [bf(ffn1_w_gu), meta_tokens, ssd_conv_w[0]])
    wgu1, meta_full, conv_w_full = _cols(g_wgu1), _cols(g_meta), _cols(g_conv_w)
    bias_p, alog_p, d_p = _heads_to_lanes(ssd_dt_bias), _heads_to_lanes(ssd_a_log), _heads_to_lanes(ssd_d)
    final_w = final_norm.reshape(1, D)

    h0 = jnp.concatenate([jnp.zeros((Bl, PAD, D), F32), jnp.broadcast_to(meta_full[None], (Bl, N_META, D)), x],
                         axis=1).reshape(M, D)
    tm = _tile(M, (544, 256))
    ffn1_saved, (g_wd1, g_win) = _ffn_fwd_gu("ffn1", h0, ffn1_norm, wgu1, comm=("gather", [bf(ffn1_w_down), bf(w_in)]))
    wd1, win_nat = _rows(g_wd1), _cols(g_win)
    win_main = jnp.concatenate([win_nat[:, :3072], win_nat[:, 3088:]], axis=1)
    win_dt = _heads_to_lanes(win_nat[:, 3072:3088])
    h1 = _ffn_fwd_down("ffn1", h0, ffn1_saved[3], wd1)
    un = _rmsnorm_fwd("mix_norm", h1, mix_norm)
    plain = lambda accs, ex: (accs[0],)
    proj, g_wa, g_wb, g_wo = _fused_matmul(
        "in_proj", M, N_MAIN, D, [dict(a=un, b=win_main, acc=0)], [], plain, [BF16], 1, tm, 1536, D, outer="j",
        comm=("gather", [bf(w_branch_a), bf(w_branch_b), bf(w_out)]))
    wa, wb, wo = _rows(g_wa), _rows(g_wb), _rows(g_wo)
    (dtr,) = _fused_matmul("in_proj_dt", M, 512, D, [dict(a=un, b=win_dt, acc=0)], [], plain, [F32], 1,
                           tm, 512, D, outer="j")
    xc = _conv_fwd(proj, conv_w_full, ssd_conv_b, Bl, T)
    ya, ssd_prev = _ssd_fwd(xc, dtr, proj, bias_p, alog_p, d_p, ssd_norm, Bl, nc)
    yb, hg_o, hg_st, g_wgu2, g_wd2 = _hgrn_fwd(proj, hg_lower_bound, hg_norm, Bl, nc,
                                               comm=("gather", [bf(ffn2_w_gu), bf(ffn2_w_down)]))
    wgu2, wd2 = _cols(g_wgu2), _rows(g_wd2)

    def branch_fwd(accs, ex):
        pa, pb = accs
        return pa, pb, _sigmoid(ex[0].astype(F32)) * pa + _sigmoid(ex[1].astype(F32)) * pb

    pa, pb, merged = _fused_matmul(
        "branches", M, D, D, [dict(a=ya, b=wa, acc=0), dict(a=yb, b=wb, acc=1)], [(proj, 7), (proj, 8)],
        branch_fwd, [BF16, BF16, BF16], 2, tm, D, D, outer="j")
    (h2,) = _fused_matmul("out_proj", M, D, D, [dict(a=merged, b=wo, acc=0)], [(h1, 0)],
                          lambda accs, ex: (ex[0] + accs[0],), [F32], 1, tm, D, D, outer="j")
    ffn2_saved, _ = _ffn_fwd_gu("ffn2", h2, ffn2_norm, wgu2)
    h3 = _ffn_fwd_down("ffn2", h2, ffn2_saved[3], wd2)

    dh3, dh3_b, d_final, loss_part = _loss_head(h3, final_w, loss_target, Bl, nc)
    dh2, dh2_b, d_ffn2_norm, d_wgu2, d_wd2 = _ffn_bwd("ffn2", dh3, dh3_b, h2, ffn2_norm, wgu2, wd2, ffn2_saved)

    def branch_bwd(accs, ex):
        dm = accs[0]
        ga, gb, pav, pbv = (e.astype(F32) for e in ex)
        sa, sb = _sigmoid(ga), _sigmoid(gb)
        return dm * sa, dm * sb, dm * pav * sa * (1.0 - sa), dm * pbv * sb * (1.0 - sb)

    dpa, dpb, dga, dgb = _fused_matmul(
        "d_merged", M, D, D, [dict(a=dh2_b, b=wo, trans_b=True, acc=0)], [(proj, 7), (proj, 8), (pa, 0), (pb, 0)],
        branch_bwd, [BF16] * 4, 1, tm, D, D, outer="j")
    d_wo = _matmul_tn("d_w_out", merged, dh2_b, D, D, tm)
    d_wa = _matmul_tn("d_w_a", ya, dpa, D, D, tm)
    d_wb = _matmul_tn("d_w_b", yb, dpb, D, D, tm)
    dya, dyb = _fused_matmul(
        "d_branches", M, D, D, [dict(a=dpa, b=wa, trans_b=True, acc=0), dict(a=dpb, b=wb, trans_b=True, acc=1)], [],
        lambda accs, ex: (accs[0], accs[1]), [BF16, BF16], 2, tm, D, D, outer="j")
    *ssd_grads, p_wa, p_wb, p_wo = _ssd_bwd(xc, dtr, proj, bias_p, alog_p, d_p, ssd_norm, ssd_prev, dya, Bl, nc,
                                            comm=("scatter", [_to_rows(d_wa), _to_rows(d_wb), _to_rows(d_wo)]))
    dxs, dB, dC, dz, ddtr, d_bias_p, d_alog_p, d_d_p, d_ssd_norm = ssd_grads
    dxbc, d_conv_w, d_conv_b = _conv_bwd(proj, conv_w_full, ssd_conv_b, jnp.concatenate([dxs, dB, dC], axis=1), Bl, T)
    dq, df, di, dg, d_hb, d_hg_norm, p_wgu2, p_wd2 = _hgrn_bwd(
        proj, hg_lower_bound, hg_norm, hg_o, hg_st, dyb, Bl, nc, comm=("scatter", [_to_cols(d_wgu2), _to_rows(d_wd2)]))
    dproj = jnp.concatenate([dz, dxbc, dq, df, di, dg, dga, dgb], axis=1)
    ddtr_b = ddtr.astype(BF16)
    d_win_main = _matmul_tn("d_w_in", un, dproj, D, 1536, tm)
    d_win_dt = _matmul_tn("d_w_in_dt", un, ddtr_b, D, 512, tm)
    d_win_nat = jnp.concatenate([d_win_main[:, :3072], _lanes_to_heads(d_win_dt), d_win_main[:, 3072:]], axis=1)
    (dun_dt,) = _fused_matmul("d_un_dt", M, D, 512, [dict(a=ddtr_b, b=win_dt, trans_b=True, acc=0)], [], plain, [F32], 1,
                              tm, D, 512, outer="j")
    dun, p_win = _fused_matmul("d_un", M, D, N_MAIN, [dict(a=dproj, b=win_main, trans_b=True, acc=0)], [(dun_dt, 0)],
                               lambda accs, ex: (accs[0] + ex[0],), [F32], 1, tm, D, 3072, outer="i",
                               comm=("scatter", [_to_cols(d_win_nat)]))
    dh1, dh1_b, d_mix_norm = _rmsnorm_bwd("d_mix_norm", dun, h1, mix_norm, dh2)
    dh0, _, d_ffn1_norm, p_wgu1, p_wd1 = _ffn_bwd("ffn1", dh1, dh1_b, h0, ffn1_norm, wgu1, wd1, ffn1_saved, scatter=True)

    dh0 = dh0.reshape(Bl, T, D)
    grad_x = dh0[:, PAD + N_META:]
    d_meta = dh0[:, PAD:PAD + N_META]

    small_grads = [d_ffn1_norm, d_mix_norm, d_conv_b, _lanes_to_heads(d_bias_p), _lanes_to_heads(d_alog_p),
                   _lanes_to_heads(d_d_p), d_ssd_norm, d_hb, d_hg_norm, d_ffn2_norm, d_final.reshape(D), d_conv_w]
    small_packed = _pack_rows(small_grads + [d_meta[b] for b in range(Bl)])
    parts = [p_wgu1, p_wd1, p_win, p_wa, p_wb, p_wo, p_wgu2, p_wd2]
    (small_all,) = _exchange("gather_small_grads", "gather", [small_packed])
    small_sum = _sum_parts("sum_small_grads", small_all)
    unpacked = _unpack_rows(small_sum, small_grads + [d_meta[b] for b in range(Bl)])
    g_small = unpacked[:len(small_grads)]
    g_meta_full = unpacked[len(small_grads)]
    for b in range(1, Bl):
        g_meta_full = g_meta_full + unpacked[len(small_grads) + b]
    g_meta = lax.dynamic_slice_in_dim(g_meta_full, me * (D // N_DEV), D // N_DEV, axis=1)
    g_conv_w = lax.dynamic_slice_in_dim(g_small[11], me * (SSD_CONV_CH // N_DEV), SSD_CONV_CH // N_DEV, axis=1)

    names = ["meta_tokens", "ffn1_norm", "ffn1_w_gu", "ffn1_w_down", "mix_norm", "w_in", "ssd_conv_w", "ssd_conv_b",
             "ssd_dt_bias", "ssd_a_log", "ssd_d", "ssd_norm", "hg_lower_bound", "hg_norm", "w_branch_a", "w_branch_b",
             "w_out", "ffn2_norm", "ffn2_w_gu", "ffn2_w_down", "final_norm"]
    W = dict(meta_tokens=meta_tokens, ffn1_norm=ffn1_norm, ffn1_w_gu=ffn1_w_gu, ffn1_w_down=ffn1_w_down, mix_norm=mix_norm,
             w_in=w_in, ssd_conv_w=ssd_conv_w, ssd_conv_b=ssd_conv_b, ssd_dt_bias=ssd_dt_bias, ssd_a_log=ssd_a_log,
             ssd_d=ssd_d, ssd_norm=ssd_norm, hg_lower_bound=hg_lower_bound, hg_norm=hg_norm, w_branch_a=w_branch_a,
             w_branch_b=w_branch_b, w_out=w_out, ffn2_norm=ffn2_norm, ffn2_w_gu=ffn2_w_gu, ffn2_w_down=ffn2_w_down,
             final_norm=final_norm)
    Mo = dict(meta_tokens=m_meta_tokens, ffn1_norm=m_ffn1_norm, ffn1_w_gu=m_ffn1_w_gu, ffn1_w_down=m_ffn1_w_down,
              mix_norm=m_mix_norm, w_in=m_w_in, ssd_conv_w=m_ssd_conv_w, ssd_conv_b=m_ssd_conv_b, ssd_dt_bias=m_ssd_dt_bias,
              ssd_a_log=m_ssd_a_log, ssd_d=m_ssd_d, ssd_norm=m_ssd_norm, hg_lower_bound=m_hg_lower_bound, hg_norm=m_hg_norm,
              w_branch_a=m_w_branch_a, w_branch_b=m_w_branch_b, w_out=m_w_out, ffn2_norm=m_ffn2_norm, ffn2_w_gu=m_ffn2_w_gu,
              ffn2_w_down=m_ffn2_w_down, final_norm=m_final_norm)
    Vo = dict(meta_tokens=v_meta_tokens, ffn1_norm=v_ffn1_norm, ffn1_w_gu=v_ffn1_w_gu, ffn1_w_down=v_ffn1_w_down,
              mix_norm=v_mix_norm, w_in=v_w_in, ssd_conv_w=v_ssd_conv_w, ssd_conv_b=v_ssd_conv_b, ssd_dt_bias=v_ssd_dt_bias,
              ssd_a_log=v_ssd_a_log, ssd_d=v_ssd_d, ssd_norm=v_ssd_norm, hg_lower_bound=v_hg_lower_bound, hg_norm=v_hg_norm,
              w_branch_a=v_w_branch_a, w_branch_b=v_w_branch_b, w_out=v_w_out, ffn2_norm=v_ffn2_norm, ffn2_w_gu=v_ffn2_w_gu,
              ffn2_w_down=v_ffn2_w_down, final_norm=v_final_norm)
    grads, deltas, new_m, new_v = {}, {}, {}, {}
    big_names = ["ffn1_w_gu", "ffn1_w_down", "w_in", "w_branch_a", "w_branch_b", "w_out", "ffn2_w_gu", "ffn2_w_down"]
    for nm, part in zip(big_names, parts):
        shp = W[nm].shape
        outs = _adamw("adamw_" + nm, part, W[nm][0], Mo[nm][0], Vo[nm][0])
        grads[nm], deltas[nm], new_m[nm], new_v[nm] = (o.reshape(shp) for o in outs)
    small_names = ["ffn1_norm", "mix_norm", "ssd_conv_b", "ssd_dt_bias", "ssd_a_log", "ssd_d", "ssd_norm", "hg_lower_bound",
                   "hg_norm", "ffn2_norm", "final_norm", "ssd_conv_w", "meta_tokens"]
    small_g = g_small[:11] + [g_conv_w.reshape(ssd_conv_w.shape), g_meta]
    pk = lambda d: _pack_rows([d[nm] for nm in small_names])
    outs = _adamw("adamw_small", _pack_rows(small_g)[None], pk(W), pk(Mo), pk(Vo))
    like = [W[nm] for nm in small_names]
    for dst, o in zip((grads, deltas, new_m, new_v), outs):
        for nm, val in zip(small_names, _unpack_rows(o, like)):
            dst[nm] = val

    loss = lax.psum(loss_part[0, 0], MESH_AXES)
    return (loss, grad_x, *[grads[nm] for nm in names], *[deltas[nm] for nm in names],
            *[new_m[nm] for nm in names], *[new_v[nm] for nm in names])
```

```python
import functools

import jax
import jax.numpy as jnp
from jax import lax
from jax.experimental import pallas as pl
from jax.experimental.pallas import tpu as pltpu

F32, BF16 = jnp.float32, jnp.bfloat16
NN, NT, TN = ((1,), (0,)), ((1,), (1,)), ((0,), (0,))
HI = lax.Precision.HIGHEST
MESH_AXES = ("x", "y", "c")
N_DEV = 8

D_MODEL = 1024
N_META = 16
EPS = 1e-6
SSD_HEADS, SSD_HEAD_DIM, SSD_GROUPS, SSD_STATE, SSD_CONV, Q = 16, 64, 4, 128, 4, 128
SSD_INNER = SSD_HEADS * SSD_HEAD_DIM
SSD_CONV_CH = SSD_INNER + 2 * SSD_GROUPS * SSD_STATE
HG_WIDTH, HG_HEADS, HG_CHUNK = 1024, 8, 16
PAD = Q - N_META
N_MAIN = 9 * 1024
ADAM_LR, ADAM_B1, ADAM_B2, ADAM_EPS, ADAM_WD, ADAM_STEP = 0.001, 0.9, 0.999, 1e-08, 0.01, 10
VMEM_LIMIT = 52 * 1024 * 1024


def _dot(a, b, dims, prec=None):
    return lax.dot_general(a, b, (dims, ((), ())), precision=prec, preferred_element_type=F32)


def _sigmoid(x):
    return 1.0 / (1.0 + jnp.exp(-x))


def _dsilu(x, s):
    return s * (1.0 + x * (1.0 - s))


def _softplus(x):
    e = jnp.exp(-jnp.abs(x))
    u = 1.0 + e
    log1p_e = jnp.where(u == 1.0, e, jnp.log(u) * e / (u - 1.0))
    return jnp.maximum(x, 0.0) + log1p_e


def _params(sem):
    return pltpu.CompilerParams(dimension_semantics=sem, vmem_limit_bytes=VMEM_LIMIT)


def _tile(n, prefs):
    for p in prefs:
        if n % p == 0:
            return p
    return n


CHIP_FLIPS = ((1, 0), (0, 1), (1, 1))
N_PEER = N_DEV - 1


def _comm_gather(srcs, outs, send_sems, recv_sems, local_sems):
    n = len(srcs)
    x, y, c = (lax.axis_index(a) for a in MESH_AXES)
    dev = lambda px, py, pc: 4 * px + 2 * py + pc
    me, sib = dev(x, y, c), (x, y, 1 - c)

    def rc(w, k, slot, to, src=None):
        return pltpu.make_async_remote_copy(
            src_ref=outs[w].at[slot] if src is None else src, dst_ref=outs[w].at[slot],
            send_sem=send_sems.at[w, k], recv_sem=recv_sems.at[w, k], device_id=to, device_id_type=pl.DeviceIdType.MESH)

    def local(w):
        return pltpu.make_async_copy(srcs[w], outs[w].at[me], local_sems.at[w])

    def start():
        for w in range(n):
            local(w).start()
            rc(w, 0, me, sib, src=srcs[w]).start()
            for j, (fx, fy) in enumerate(CHIP_FLIPS):
                rc(w, 1 + j, me, (x ^ fx, y ^ fy, c), src=srcs[w]).start()

    def finish():
        for w in range(n):
            for j, (fx, fy) in enumerate(CHIP_FLIPS):
                slot = dev(x ^ fx, y ^ fy, c)
                rc(w, 1 + j, slot, sib).wait_recv()
                rc(w, 4 + j, slot, sib).start()
        for w in range(n):
            rc(w, 0, dev(x, y, 1 - c), sib).wait_recv()
            rc(w, 0, me, sib, src=srcs[w]).wait_send()
            for j, (fx, fy) in enumerate(CHIP_FLIPS):
                rc(w, 4 + j, dev(x ^ fx, y ^ fy, 1 - c), sib).wait_recv()
                rc(w, 1 + j, me, sib, src=srcs[w]).wait_send()
                rc(w, 4 + j, dev(x ^ fx, y ^ fy, c), sib).wait_send()
            local(w).wait()

    return start, finish


def _comm_scatter(srcs, outs, send_sems, recv_sems, local_sems):
    n = len(srcs)
    x, y, c = (lax.axis_index(a) for a in MESH_AXES)
    me = 4 * x + 2 * y + c

    def copies():
        out = []
        for w in range(n):
            out.append(pltpu.make_async_copy(srcs[w].at[me], outs[w].at[me], local_sems.at[w]))
            for k in range(1, N_DEV):
                px, py, pc = x ^ (k >> 2), y ^ ((k >> 1) & 1), c ^ (k & 1)
                out.append(pltpu.make_async_remote_copy(
                    src_ref=srcs[w].at[4 * px + 2 * py + pc], dst_ref=outs[w].at[me],
                    send_sem=send_sems.at[w, k - 1], recv_sem=recv_sems.at[w, k - 1],
                    device_id=(px, py, pc), device_id_type=pl.DeviceIdType.MESH))
        return out

    def start():
        for cp in copies():
            cp.start()

    def finish():
        for cp in copies():
            cp.wait()

    return start, finish


def _comm_parts(comm):
    kind, arrays = comm
    n = len(arrays)
    shapes = [jax.ShapeDtypeStruct((N_DEV,) + (a.shape[1:] if kind == "scatter" else a.shape), a.dtype) for a in arrays]
    sems = [pltpu.SemaphoreType.DMA((n, N_PEER)), pltpu.SemaphoreType.DMA((n, N_PEER)), pltpu.SemaphoreType.DMA((n,))]
    return n, shapes, sems, (_comm_scatter if kind == "scatter" else _comm_gather)


def _exchange(name, kind, arrays):
    n, shapes, sems, make = _comm_parts((kind, arrays))

    def body(*refs):
        start, finish = make(refs[:n], refs[n:2 * n], *refs[2 * n:])
        start()
        finish()

    any_spec = pl.BlockSpec(memory_space=pl.ANY)
    return pl.pallas_call(
        body, name=name, in_specs=[any_spec] * n, out_specs=[any_spec] * n, out_shape=shapes, scratch_shapes=sems,
        compiler_params=pltpu.CompilerParams(has_side_effects=True),
    )(*arrays)


def _call(body, *, name, grid, in_specs, out_specs, out_shape, scratch, sem, args, comm=None):
    if comm is None:
        return pl.pallas_call(body, name=name, grid=grid, in_specs=in_specs, out_specs=out_specs, out_shape=out_shape,
                              scratch_shapes=scratch, compiler_params=_params(sem))(*args)
    n, shapes, sems, make = _comm_parts(comm)
    n_in, n_out, n_scr = len(in_specs), len(out_specs), len(scratch)

    def carrier(*refs):
        ins, csrc = refs[:n_in], refs[n_in:n_in + n]
        outs, cout = refs[n_in + n:n_in + n + n_out], refs[n_in + n + n_out:n_in + 2 * n + n_out]
        rest = refs[n_in + 2 * n + n_out:]
        start, finish = make(csrc, cout, *rest[n_scr:])
        ids = [pl.program_id(a) for a in range(len(grid))]
        first = functools.reduce(jnp.logical_and, [i == 0 for i in ids])
        last = functools.reduce(jnp.logical_and, [i == g - 1 for i, g in zip(ids, grid)])
        pl.when(first)(start)
        body(*ins, *outs, *rest[:n_scr])
        pl.when(last)(finish)

    any_spec = pl.BlockSpec(memory_space=pl.ANY)
    return pl.pallas_call(
        carrier, name=name, grid=grid, in_specs=list(in_specs) + [any_spec] * n,
        out_specs=list(out_specs) + [any_spec] * n, out_shape=list(out_shape) + shapes,
        scratch_shapes=list(scratch) + sems,
        compiler_params=pltpu.CompilerParams(dimension_semantics=("arbitrary",) * len(grid),
                                             vmem_limit_bytes=VMEM_LIMIT, has_side_effects=True),
    )(*args, *comm[1])


def _fused_matmul(name, M, N, K, pairs, extras, epilogue, out_dtypes, n_acc, tm, tn, tk, outer="i", comm=None):
    nk = K // tk
    n_pairs, n_ex, n_out = len(pairs), len(extras), len(out_dtypes)

    def ij(g0, g1):
        return (g0, g1) if outer == "i" else (g1, g0)

    in_specs, args = [], []
    for p in pairs:
        ao, bk, bn = p.get("a_off", 0), p.get("bk_off", 0), p.get("bn_off", 0)
        in_specs.append(pl.BlockSpec((tm, tk), lambda g0, g1, k, ao=ao: (ij(g0, g1)[0], k + ao)))
        if p.get("trans_b"):
            in_specs.append(pl.BlockSpec((tn, tk), lambda g0, g1, k, bk=bk, bn=bn: (ij(g0, g1)[1] + bn, k + bk)))
        else:
            in_specs.append(pl.BlockSpec((tk, tn), lambda g0, g1, k, bk=bk, bn=bn: (k + bk, ij(g0, g1)[1] + bn)))
        args += [p["a"], p["b"]]
    for arr, off in extras:
        in_specs.append(pl.BlockSpec((tm, tn), lambda g0, g1, k, off=off: (ij(g0, g1)[0], ij(g0, g1)[1] + off)))
        args.append(arr)
    out_specs = [pl.BlockSpec((tm, tn), lambda g0, g1, k: ij(g0, g1)) for _ in out_dtypes]
    out_shape = [jax.ShapeDtypeStruct((M, N), dt) for dt in out_dtypes]
    grid = (M // tm, N // tn, nk) if outer == "i" else (N // tn, M // tm, nk)

    def partials(refs):
        accs = [None] * n_acc
        for idx, p in enumerate(pairs):
            d = _dot(refs[2 * idx][...], refs[2 * idx + 1][...], NT if p.get("trans_b") else NN)
            accs[p["acc"]] = d if accs[p["acc"]] is None else accs[p["acc"]] + d
        return accs

    def finish(accs, refs):
        ex = [r[...] for r in refs[2 * n_pairs:2 * n_pairs + n_ex]]
        outs = refs[2 * n_pairs + n_ex:2 * n_pairs + n_ex + n_out]
        for o, r in zip(outs, epilogue(accs, ex)):
            o[...] = r.astype(o.dtype)

    if nk == 1:
        def body(*refs):
            finish(partials(refs), refs)
        scratch = []
    else:
        def body(*refs):
            acc_refs = refs[-n_acc:]
            k = pl.program_id(2)
            new = partials(refs)

            @pl.when(k == 0)
            def _():
                for a, v in zip(acc_refs, new):
                    a[...] = v

            @pl.when(k > 0)
            def _():
                for a, v in zip(acc_refs, new):
                    a[...] += v

            @pl.when(k == nk - 1)
            def _():
                finish([a[...] for a in acc_refs], refs)
        scratch = [pltpu.VMEM((tm, tn), F32) for _ in range(n_acc)]

    return _call(body, name=name, grid=grid, in_specs=in_specs, out_specs=out_specs, out_shape=out_shape,
                 scratch=scratch, sem=("parallel", "parallel", "arbitrary"), args=args, comm=comm)


def _matmul_tn(name, x, y, t1, t2, tr, scale=1.0, comm=None):
    R, K1 = x.shape
    N1 = y.shape[1]
    nr = R // tr

    def body(x_ref, y_ref, o_ref):
        r = pl.program_id(2)
        d = _dot(x_ref[...], y_ref[...], TN)

        @pl.when(r == 0)
        def _():
            o_ref[...] = d

        @pl.when(r > 0)
        def _():
            o_ref[...] += d

        if scale != 1.0:
            @pl.when(r == nr - 1)
            def _():
                o_ref[...] = o_ref[...] * scale

    return _call(
        body, name=name, grid=(K1 // t1, N1 // t2, nr),
        in_specs=[pl.BlockSpec((tr, t1), lambda i, j, r: (r, i)), pl.BlockSpec((tr, t2), lambda i, j, r: (r, j))],
        out_specs=[pl.BlockSpec((t1, t2), lambda i, j, r: (i, j))],
        out_shape=[jax.ShapeDtypeStruct((K1, N1), F32)], scratch=[],
        sem=("parallel", "parallel", "arbitrary"), args=(x, y), comm=comm)


def _rmsnorm_fwd(name, h, w):
    M, D = h.shape
    tm = _tile(M, (544, 256, 128))

    def body(h_ref, w_ref, o_ref):
        x = h_ref[...]
        r = lax.rsqrt(jnp.mean(x * x, axis=-1, keepdims=True) + EPS)
        o_ref[...] = (x * r * w_ref[...]).astype(o_ref.dtype)

    return pl.pallas_call(
        body, name=name, grid=(M // tm,),
        in_specs=[pl.BlockSpec((tm, D), lambda i: (i, 0)), pl.BlockSpec((1, D), lambda i: (0, 0))],
        out_specs=pl.BlockSpec((tm, D), lambda i: (i, 0)),
        out_shape=jax.ShapeDtypeStruct((M, D), BF16), compiler_params=_params(("parallel",)),
    )(h, w)


def _rmsnorm_bwd(name, dn, h, w, dh_in):
    M, D = h.shape
    tm = _tile(M, (544, 256, 128))

    def body(dn_ref, h_ref, w_ref, dhi_ref, dh_ref, dhb_ref, dw_ref):
        x = h_ref[...]
        r = lax.rsqrt(jnp.mean(x * x, axis=-1, keepdims=True) + EPS)
        xhat = x * r
        dn_v = dn_ref[...]
        gw = dn_v * w_ref[...]
        dx = r * (gw - xhat * jnp.mean(gw * xhat, axis=-1, keepdims=True))
        dh = dhi_ref[...] + dx
        dh_ref[...] = dh
        dhb_ref[...] = dh.astype(BF16)
        dw = jnp.sum(dn_v * xhat, axis=0, keepdims=True)

        @pl.when(pl.program_id(0) == 0)
        def _():
            dw_ref[...] = dw

        @pl.when(pl.program_id(0) > 0)
        def _():
            dw_ref[...] += dw

    row = pl.BlockSpec((tm, D), lambda i: (i, 0))
    vec = pl.BlockSpec((1, D), lambda i: (0, 0))
    return pl.pallas_call(
        body, name=name, grid=(M // tm,), in_specs=[row, row, vec, row], out_specs=[row, row, vec],
        out_shape=[jax.ShapeDtypeStruct((M, D), F32), jax.ShapeDtypeStruct((M, D), BF16), jax.ShapeDtypeStruct((1, D), F32)],
        compiler_params=_params(("arbitrary",)),
    )(dn, h, w, dh_in)


def _loss_head(h, w, target, Bl, nb):
    M, D = h.shape

    def body(h_ref, w_ref, t_ref, dh_ref, dhb_ref, dw_ref, loss_ref):
        b, t = pl.program_id(0), pl.program_id(1)
        live = (t > 0).astype(F32)
        x = h_ref[...]
        r = lax.rsqrt(jnp.mean(x * x, axis=-1, keepdims=True) + EPS)
        xhat = x * r
        wv = w_ref[...]
        err = (xhat * wv - t_ref[0]) * live
        dy = err * (1.0 / D)
        gw = dy * wv
        dx = r * (gw - xhat * jnp.mean(gw * xhat, axis=-1, keepdims=True))
        dh_ref[...] = dx
        dhb_ref[...] = dx.astype(BF16)
        dw = jnp.sum(dy * xhat, axis=0, keepdims=True)
        part = 0.5 * jnp.sum(jnp.sum(err * err, axis=-1, keepdims=True) * (1.0 / D), axis=0, keepdims=True)
        first = jnp.logical_and(b == 0, t == 0)

        @pl.when(first)
        def _():
            dw_ref[...] = dw
            loss_ref[...] = jnp.broadcast_to(part, loss_ref.shape)

        @pl.when(jnp.logical_not(first))
        def _():
            dw_ref[...] += dw
            loss_ref[...] += jnp.broadcast_to(part, loss_ref.shape)

    row = pl.BlockSpec((Q, D), lambda b, t: (b * nb + t, 0))
    vec = pl.BlockSpec((1, D), lambda b, t: (0, 0))
    return pl.pallas_call(
        body, name="loss_head", grid=(Bl, nb),
        in_specs=[row, vec, pl.BlockSpec((1, Q, D), lambda b, t: (b, jnp.maximum(t - 1, 0), 0))],
        out_specs=[row, row, vec, pl.BlockSpec((8, 128), lambda b, t: (0, 0))],
        out_shape=[jax.ShapeDtypeStruct((M, D), F32), jax.ShapeDtypeStruct((M, D), BF16),
                   jax.ShapeDtypeStruct((1, D), F32), jax.ShapeDtypeStruct((8, 128), F32)],
        compiler_params=_params(("arbitrary", "arbitrary")),
    )(h, w, target)


CONV_TC = 256


def _conv_pre(xr_ref, w_ref, b_ref):
    x = xr_ref[...].astype(F32)
    acc = b_ref[...] + w_ref[SSD_CONV - 1:SSD_CONV, :] * x
    for k in range(1, SSD_CONV):
        acc = acc + w_ref[SSD_CONV - 1 - k:SSD_CONV - k, :] * pltpu.roll(x, k, 0)
    return x, acc


def _conv_fwd(proj, w, b, Bl, T):
    M = proj.shape[0]
    off = 1024 // CONV_TC

    def body(xr_ref, w_ref, b_ref, o_ref):
        _, acc = _conv_pre(xr_ref, w_ref, b_ref)
        row = lax.broadcasted_iota(jnp.int32, acc.shape, 0)
        o_ref[...] = jnp.where(row >= PAD, acc * _sigmoid(acc), 0.0).astype(o_ref.dtype)

    return pl.pallas_call(
        body, name="conv_fwd", grid=(Bl, SSD_CONV_CH // CONV_TC),
        in_specs=[pl.BlockSpec((T, CONV_TC), lambda bb, j: (bb, j + off)),
                  pl.BlockSpec((SSD_CONV, CONV_TC), lambda bb, j: (0, j)), pl.BlockSpec((1, CONV_TC), lambda bb, j: (0, j))],
        out_specs=pl.BlockSpec((T, CONV_TC), lambda bb, j: (bb, j)),
        out_shape=jax.ShapeDtypeStruct((M, SSD_CONV_CH), BF16), compiler_params=_params(("parallel", "parallel")),
    )(proj, w, b)


def _conv_bwd(proj, w, b, dxc, Bl, T):
    M = proj.shape[0]
    off = 1024 // CONV_TC

    def body(xr_ref, w_ref, b_ref, d_ref, dx_ref, dw_ref, db_ref):
        x, acc = _conv_pre(xr_ref, w_ref, b_ref)
        row = lax.broadcasted_iota(jnp.int32, acc.shape, 0)
        s = _sigmoid(acc)
        dpre = jnp.where(row >= PAD, d_ref[...].astype(F32) * _dsilu(acc, s), 0.0)
        dx = w_ref[SSD_CONV - 1:SSD_CONV, :] * dpre
        dws = [jnp.sum(dpre * x, axis=0, keepdims=True)]
        for k in range(1, SSD_CONV):
            dx = dx + w_ref[SSD_CONV - 1 - k:SSD_CONV - k, :] * pltpu.roll(dpre, T - k, 0)
            dws.append(jnp.sum(dpre * pltpu.roll(x, k, 0), axis=0, keepdims=True))
        dx_ref[...] = dx.astype(dx_ref.dtype)
        dw = jnp.concatenate(dws[::-1], axis=0)
        db = jnp.sum(dpre, axis=0, keepdims=True)

        @pl.when(pl.program_id(1) == 0)
        def _():
            dw_ref[...] = dw
            db_ref[...] = db

        @pl.when(pl.program_id(1) > 0)
        def _():
            dw_ref[...] += dw
            db_ref[...] += db

    return pl.pallas_call(
        body, name="conv_bwd", grid=(SSD_CONV_CH // CONV_TC, Bl),
        in_specs=[pl.BlockSpec((T, CONV_TC), lambda j, bb: (bb, j + off)),
                  pl.BlockSpec((SSD_CONV, CONV_TC), lambda j, bb: (0, j)), pl.BlockSpec((1, CONV_TC), lambda j, bb: (0, j)),
                  pl.BlockSpec((T, CONV_TC), lambda j, bb: (bb, j))],
        out_specs=[pl.BlockSpec((T, CONV_TC), lambda j, bb: (bb, j)),
                   pl.BlockSpec((SSD_CONV, CONV_TC), lambda j, bb: (0, j)), pl.BlockSpec((1, CONV_TC), lambda j, bb: (0, j))],
        out_shape=[jax.ShapeDtypeStruct((M, SSD_CONV_CH), BF16), jax.ShapeDtypeStruct((SSD_CONV, SSD_CONV_CH), F32),
                   jax.ShapeDtypeStruct((1, SSD_CONV_CH), F32)],
        compiler_params=_params(("parallel", "arbitrary")),
    )(proj, w, b, dxc)


def _ssd_setup(c, dtr_ref, bias_ref, alog_ref):
    row = lax.broadcasted_iota(jnp.int32, (Q, 128), 0)
    col = lax.broadcasted_iota(jnp.int32, (Q, 128), 1)
    valid = jnp.logical_or(c > 0, row >= PAD)
    pre = dtr_ref[...] + bias_ref[...]
    dt = jnp.where(valid, _softplus(pre), 0.0)
    A = -jnp.exp(alog_ref[...])
    tri = row >= col
    cs = _dot(tri.astype(F32), dt * A, NN, HI)
    cst = _dot((row == col).astype(F32), cs, NT, HI)
    return dt, A, cs, cst, tri, valid, pre, row, col


def _pair_terms(p, dt, cs, col):
    h0, h1 = 2 * p, 2 * p + 1
    first = col < SSD_HEAD_DIM
    pick = lambda a: jnp.where(first, a[:, h0:h0 + 1], a[:, h1:h1 + 1])
    cl0, cl1 = cs[Q - 1:Q, h0:h0 + 1], cs[Q - 1:Q, h1:h1 + 1]
    cs_p = pick(cs)
    cl_p = jnp.where(first[0:1], cl0, cl1)
    return first, pick(dt), jnp.exp(cs_p), jnp.exp(cl_p - cs_p), (cl0, cl1)


def _ssd_core(x_ref, b_ref, c_ref, d_ref, state_of, dt, cs, cst, tri, row, col):
    xv = x_ref[...].astype(F32)
    Bg, Cg = b_ref[...], c_ref[...]
    CB = _dot(Cg, Bg, NT)
    ys, new_states, Ms = [], [], []
    for p in range(2):
        first, dt_p, ecs_p, decay_p, (cl0, cl1) = _pair_terms(p, dt, cs, col)
        x_p = xv[:, 128 * p:128 * (p + 1)]
        X_p = x_p * dt_p
        yd = jnp.zeros((Q, 128), F32)
        for hh in range(2):
            h = 2 * p + hh
            Lm = jnp.exp(jnp.where(tri, cs[:, h:h + 1] - cst[h:h + 1, :], -jnp.inf))
            Mh = CB * Lm
            Ms.append(Mh)
            Xm = jnp.where(first if hh == 0 else jnp.logical_not(first), X_p, 0.0).astype(BF16)
            yd = yd + _dot(Mh.astype(BF16), Xm, NN)
        prev = state_of(p)
        yo = _dot(Cg, prev.astype(BF16), NT) * ecs_p
        st = _dot((X_p * decay_p).astype(BF16), Bg, TN)
        ecl_rows = jnp.where(row < SSD_HEAD_DIM, jnp.exp(cl0), jnp.exp(cl1))
        new_states.append(prev * ecl_rows + st)
        d_p = jnp.where(first[0:1], d_ref[:, 2 * p:2 * p + 1], d_ref[:, 2 * p + 1:2 * p + 2])
        ys.append(yd + yo + x_p * d_p)
    return jnp.concatenate(ys, axis=1), new_states, Ms, CB, xv


def _ssd_specs(nc):
    rb = lambda g, b, c: b * nc + c
    return [
        pl.BlockSpec((Q, 256), lambda g, b, c: (rb(g, b, c), g)),
        pl.BlockSpec((Q, 128), lambda g, b, c: (rb(g, b, c), 8 + g)),
        pl.BlockSpec((Q, 128), lambda g, b, c: (rb(g, b, c), 12 + g)),
        pl.BlockSpec((Q, 128), lambda g, b, c: (rb(g, b, c), g)),
        pl.BlockSpec((Q, 256), lambda g, b, c: (rb(g, b, c), g)),
        pl.BlockSpec((1, 128), lambda g, b, c: (0, g)),
        pl.BlockSpec((1, 128), lambda g, b, c: (0, g)),
        pl.BlockSpec((1, 128), lambda g, b, c: (0, g)),
        pl.BlockSpec((1, 256), lambda g, b, c: (0, g)),
    ]


def _ssd_fwd(xc, dtr, proj, bias_p, alog_p, d_p, nw, Bl, nc):
    M = xc.shape[0]

    def body(x_ref, b_ref, c_ref, dtr_ref, z_ref, bias_ref, alog_ref, d_ref, nw_ref, y_ref, prev_ref, state):
        c = pl.program_id(2)

        @pl.when(c == 0)
        def _():
            state[...] = jnp.zeros_like(state)

        dt, _, cs, cst, tri, _, _, row, col = _ssd_setup(c, dtr_ref, bias_ref, alog_ref)
        y, new_states, _, _, _ = _ssd_core(x_ref, b_ref, c_ref, d_ref, lambda p: state[p], dt, cs, cst, tri, row, col)
        for p in range(2):
            prev_ref[0, 0, 0, p] = state[p]
            state[p] = new_states[p]
        zz = z_ref[...].astype(F32)
        yg = y * zz * _sigmoid(zz)
        r = lax.rsqrt(jnp.mean(yg * yg, axis=-1, keepdims=True) + EPS)
        y_ref[...] = (yg * r * nw_ref[...]).astype(y_ref.dtype)

    return pl.pallas_call(
        body, name="ssd_fwd", grid=(SSD_GROUPS, Bl, nc), in_specs=_ssd_specs(nc),
        out_specs=[pl.BlockSpec((Q, 256), lambda g, b, c: (b * nc + c, g)),
                   pl.BlockSpec((1, 1, 1, 2, 128, 128), lambda g, b, c: (b, g, c, 0, 0, 0))],
        out_shape=[jax.ShapeDtypeStruct((M, SSD_INNER), BF16), jax.ShapeDtypeStruct((Bl, SSD_GROUPS, nc, 2, 128, 128), F32)],
        scratch_shapes=[pltpu.VMEM((2, 128, 128), F32)],
        compiler_params=_params(("parallel", "arbitrary", "arbitrary")),
    )(xc, xc, xc, dtr, proj, bias_p, alog_p, d_p, nw)


def _ssd_bwd(xc, dtr, proj, bias_p, alog_p, d_p, nw, prev, dya, Bl, nc, comm=None):
    M = xc.shape[0]
    rev = lambda spec: pl.BlockSpec(spec.block_shape, lambda g, b, c, f=spec.index_map: f(g, b, nc - 1 - c))

    def body(x_ref, b_ref, c_ref, dtr_ref, z_ref, bias_ref, alog_ref, d_ref, nw_ref, prev_ref, dy_ref,
             dx_ref, dB_ref, dC_ref, dz_ref, ddtr_ref, dbias_ref, dalog_ref, dd_ref, dnw_ref, dS):
        b, t = pl.program_id(1), pl.program_id(2)
        c = nc - 1 - t

        @pl.when(t == 0)
        def _():
            dS[...] = jnp.zeros_like(dS)

        dt, A, cs, cst, tri, valid, pre, row, col = _ssd_setup(c, dtr_ref, bias_ref, alog_ref)
        y, _, Ms, CB, xv = _ssd_core(x_ref, b_ref, c_ref, d_ref, lambda p: prev_ref[0, 0, 0, p], dt, cs, cst, tri, row, col)
        Bg, Cg = b_ref[...], c_ref[...]
        Bf = Bg.astype(F32)

        zz = z_ref[...].astype(F32)
        sz = _sigmoid(zz)
        silu_z = zz * sz
        yg = y * silu_z
        r = lax.rsqrt(jnp.mean(yg * yg, axis=-1, keepdims=True) + EPS)
        xhat = yg * r
        dout = dy_ref[...].astype(F32)
        gw = dout * nw_ref[...]
        dyg = r * (gw - xhat * jnp.mean(gw * xhat, axis=-1, keepdims=True))
        dnw = jnp.sum(dout * xhat, axis=0, keepdims=True)
        dz_ref[...] = (dyg * y * _dsilu(zz, sz)).astype(dz_ref.dtype)
        dy = dyg * silu_z

        lane1 = lax.broadcasted_iota(jnp.int32, (1, 128), 1)
        put_col = lambda h, v: jnp.where(col == h, v, 0.0)
        put_lane = lambda h, v: jnp.where(lane1 == h, v, 0.0)
        dcs = jnp.zeros((Q, 128), F32)
        dcs_t = jnp.zeros((128, Q), F32)
        dcl = jnp.zeros((1, 128), F32)
        ddt = jnp.zeros((Q, 128), F32)
        dD = jnp.zeros((1, 128), F32)
        dCB = jnp.zeros((Q, Q), F32)
        dBacc = jnp.zeros((Q, 128), F32)
        dCacc = jnp.zeros((Q, 128), F32)
        dxs = []
        for p in range(2):
            first, dt_p, ecs_p, decay_p, (cl0, cl1) = _pair_terms(p, dt, cs, col)
            halves = (first, jnp.logical_not(first))
            x_p = xv[:, 128 * p:128 * (p + 1)]
            X_p = x_p * dt_p
            dy_p = dy[:, 128 * p:128 * (p + 1)]
            prev_p = prev_ref[0, 0, 0, p]
            prev_b = prev_p.astype(BF16)
            dS_p = dS[p]
            dS_b = dS_p.astype(BF16)
            dX = decay_p * _dot(Bg, dS_b, NT)
            Yo = _dot(Cg, prev_b, NT)
            dYo = (dy_p * ecs_p).astype(BF16)
            dCacc = dCacc + _dot(dYo, prev_b, NN)
            dprev = _dot(dYo, Cg, TN)
            off_cs = dy_p * Yo * ecs_p
            state_cs = dS_p * prev_p
            for hh in range(2):
                h = 2 * p + hh
                hm = halves[hh]
                Mh = Ms[h]
                dyh = jnp.where(hm, dy_p, 0.0).astype(BF16)
                Xm = jnp.where(hm, X_p, 0.0).astype(BF16)
                dX = dX + _dot(Mh.astype(BF16), dyh, TN)
                dM = _dot(dyh, Xm, NT)
                W = dM * Mh
                Lm = jnp.exp(jnp.where(tri, cs[:, h:h + 1] - cst[h:h + 1, :], -jnp.inf))
                dCB = dCB + dM * Lm
                XdS = _dot(Xm, dS_b, NN)
                decay_h = decay_p[:, 64 * hh:64 * hh + 1]
                dBacc = dBacc + decay_h * XdS
                tdec = jnp.sum(XdS * Bf, axis=1, keepdims=True) * decay_h
                dcs = dcs + put_col(h, jnp.sum(W, axis=1, keepdims=True)
                                    + jnp.sum(jnp.where(hm, off_cs, 0.0), axis=1, keepdims=True) - tdec)
                dcs_t = dcs_t - jnp.where(lax.broadcasted_iota(jnp.int32, (128, Q), 0) == h,
                                          jnp.sum(W, axis=0, keepdims=True), 0.0)
                ecl = jnp.exp(cl0 if hh == 0 else cl1)
                rows_h = (row < SSD_HEAD_DIM) if hh == 0 else (row >= SSD_HEAD_DIM)
                dcl = dcl + put_lane(h, jnp.sum(tdec, axis=0, keepdims=True)
                                     + ecl * jnp.sum(jnp.sum(jnp.where(rows_h, state_cs, 0.0), axis=1, keepdims=True),
                                                     axis=0, keepdims=True))
                ddt = ddt + put_col(h, jnp.sum(jnp.where(hm, dX * x_p, 0.0), axis=1, keepdims=True))
                dD = dD + put_lane(h, jnp.sum(jnp.sum(jnp.where(hm, dy_p * x_p, 0.0), axis=1, keepdims=True),
                                              axis=0, keepdims=True))
            ecl_rows = jnp.where(row < SSD_HEAD_DIM, jnp.exp(cl0), jnp.exp(cl1))
            dS[p] = dS_p * ecl_rows + dprev
            d_pp = jnp.where(first[0:1], d_ref[:, 2 * p:2 * p + 1], d_ref[:, 2 * p + 1:2 * p + 2])
            dxs.append(dy_p * d_pp + dX * dt_p)
        dCB_b = dCB.astype(BF16)
        dCacc = dCacc + _dot(dCB_b, Bg, NN)
        dBacc = dBacc + _dot(dCB_b, Cg, TN)
        dx_ref[...] = jnp.concatenate(dxs, axis=1).astype(dx_ref.dtype)
        dB_ref[...] = dBacc.astype(dB_ref.dtype)
        dC_ref[...] = dCacc.astype(dC_ref.dtype)

        dcs = dcs + _dot((row == col).astype(F32), dcs_t, NT, HI) + jnp.where(row == Q - 1, dcl, 0.0)
        da = _dot((row <= col).astype(F32), dcs, NN, HI)
        ddt = ddt + da * A
        dpre = jnp.where(valid, ddt * _sigmoid(pre), 0.0)
        ddtr_ref[...] = dpre
        dbias = jnp.sum(dpre, axis=0, keepdims=True)
        dalog = jnp.sum(da * dt, axis=0, keepdims=True) * A
        first_step = jnp.logical_and(b == 0, t == 0)

        @pl.when(first_step)
        def _():
            dbias_ref[...] = dbias
            dalog_ref[...] = dalog
            dd_ref[...] = dD
            dnw_ref[...] = dnw

        @pl.when(jnp.logical_not(first_step))
        def _():
            dbias_ref[...] += dbias
            dalog_ref[...] += dalog
            dd_ref[...] += dD
            dnw_ref[...] += dnw

    fwd_specs = _ssd_specs(nc)
    in_specs = [rev(s) for s in fwd_specs] + [
        pl.BlockSpec((1, 1, 1, 2, 128, 128), lambda g, b, c: (b, g, nc - 1 - c, 0, 0, 0)),
        pl.BlockSpec((Q, 256), lambda g, b, c: (b * nc + nc - 1 - c, g))]
    rowblk = lambda w: pl.BlockSpec((Q, w), lambda g, b, c: (b * nc + nc - 1 - c, g))
    vec = lambda w: pl.BlockSpec((1, w), lambda g, b, c: (0, g))
    return _call(
        body, name="ssd_bwd", grid=(SSD_GROUPS, Bl, nc), in_specs=in_specs,
        out_specs=[rowblk(256), rowblk(128), rowblk(128), rowblk(256), rowblk(128), vec(128), vec(128), vec(128), vec(256)],
        out_shape=[jax.ShapeDtypeStruct((M, SSD_INNER), BF16), jax.ShapeDtypeStruct((M, 512), BF16),
                   jax.ShapeDtypeStruct((M, 512), BF16), jax.ShapeDtypeStruct((M, SSD_INNER), BF16),
                   jax.ShapeDtypeStruct((M, 512), F32), jax.ShapeDtypeStruct((1, 512), F32),
                   jax.ShapeDtypeStruct((1, 512), F32), jax.ShapeDtypeStruct((1, 512), F32),
                   jax.ShapeDtypeStruct((1, SSD_INNER), F32)],
        scratch=[pltpu.VMEM((2, 128, 128), F32)], sem=("parallel", "arbitrary", "arbitrary"),
        args=(xc, xc, xc, dtr, proj, bias_p, alog_p, d_p, nw, prev, dya), comm=comm)


NSUB = Q // HG_CHUNK
HG_HP = 8
EXP_CAP = 80.0


def _hg_setup(blk, q_ref, f_ref, hb_ref):
    row = lax.broadcasted_iota(jnp.int32, (Q, Q), 0)
    col = lax.broadcasted_iota(jnp.int32, (Q, Q), 1)
    same = (row // HG_CHUNK) == (col // HG_CHUNK)
    causal = jnp.logical_and(same, col <= row)
    lb = _sigmoid(hb_ref[0:1, :] - hb_ref[1:2, :])
    fl = f_ref[...].astype(F32)
    sg = _sigmoid(fl)
    fg = lb + (1.0 - lb) * sg
    k = (1.0 - lb) * (1.0 - sg)
    gl = jnp.log(fg)
    G = _dot(causal.astype(F32), gl, NN, HI)
    T = _dot(same.astype(F32), gl, NN, HI)
    qv = q_ref[...].astype(F32)
    sq = _sigmoid(qv)
    eG = jnp.exp(G)
    eGn = jnp.exp(jnp.minimum(-G, EXP_CAP))
    eTG = jnp.exp(T - G)
    qt = qv * sq * eG
    kt = k * eGn
    kh = k * eTG
    valid = jnp.logical_or(blk > 0, row[:, :1] >= PAD)
    return dict(row=row, col=col, same=same, causal=causal, lb=lb, sg=sg, fg=fg, k=k, T=T, qv=qv, sq=sq,
                eG=eG, eGn=eGn, eTG=eTG, qt=qt, kt=kt, kh=kh, valid=valid)


def _hg_specs(nb, rev=False):
    rb = (lambda h, b, t: b * nb + nb - 1 - t) if rev else (lambda h, b, t: b * nb + t)
    w = 128 * HG_HP
    blk = lambda off: pl.BlockSpec((Q, w), lambda h, b, t, off=off: (rb(h, b, t), off // HG_HP + h))
    return [blk(24), blk(32), blk(40), blk(48),
            pl.BlockSpec((2, w), lambda h, b, t: (0, h)), pl.BlockSpec((1, w), lambda h, b, t: (0, h))]


HEAD_LANES = tuple(slice(128 * hh, 128 * (hh + 1)) for hh in range(HG_HP))


def _per_head(fn, *arrs):
    return jnp.concatenate([jnp.broadcast_to(fn(*(a[:, ln] for a in arrs)), (arrs[0].shape[0], 128))
                            for ln in HEAD_LANES], axis=1)


def _hgrn_fwd(proj, hb, nw, Bl, nb, comm=None):
    M = proj.shape[0]

    def body(q_ref, f_ref, i_ref, g_ref, hb_ref, nw_ref, y_ref, o_ref, st_ref, S):
        blk = pl.program_id(2)

        @pl.when(blk == 0)
        def _():
            S[...] = jnp.zeros_like(S)

        s = _hg_setup(blk, q_ref, f_ref, hb_ref)
        v = i_ref[...]
        qt_b, kt_b, kh_b = s["qt"].astype(BF16), s["kt"].astype(BF16), s["kh"].astype(BF16)
        eT = jnp.exp(s["T"])
        att = [jnp.where(s["causal"], _dot(qt_b[:, ln], kt_b[:, ln], NT), 0.0).astype(BF16) for ln in HEAD_LANES]
        o_intra = [_dot(att[hh], v[:, ln], NN) for hh, ln in enumerate(HEAD_LANES)]
        for j in range(NSUB):
            sl = slice(HG_CHUNK * j, HG_CHUNK * (j + 1))
            for hh, ln in enumerate(HEAD_LANES):
                St = S[hh]
                st_ref[0, hh, 0, j] = St
                o_ref[sl, ln] = o_intra[hh][sl] + _dot(qt_b[sl, ln], St.astype(BF16), NT)
                S[hh] = St * eT[HG_CHUNK * j:HG_CHUNK * j + 1, ln] + _dot(v[sl, ln], kh_b[sl, ln], TN)
        o = o_ref[...]
        r = _per_head(lambda a: lax.rsqrt(jnp.mean(a * a, axis=-1, keepdims=True) + EPS), o)
        gv = g_ref[...].astype(F32)
        y_ref[...] = (o * r * nw_ref[...] * gv * _sigmoid(gv)).astype(y_ref.dtype)

    rowblk = pl.BlockSpec((Q, 128 * HG_HP), lambda h, b, t: (b * nb + t, h))
    return _call(
        body, name="hgrn_fwd", grid=(HG_HEADS // HG_HP, Bl, nb), in_specs=_hg_specs(nb),
        out_specs=[rowblk, rowblk,
                   pl.BlockSpec((1, HG_HP, 1, NSUB, 128, 128), lambda h, b, t: (b, h, t, 0, 0, 0))],
        out_shape=[jax.ShapeDtypeStruct((M, HG_WIDTH), BF16), jax.ShapeDtypeStruct((M, HG_WIDTH), F32),
                   jax.ShapeDtypeStruct((Bl, HG_HEADS, nb, NSUB, 128, 128), F32)],
        scratch=[pltpu.VMEM((HG_HP, 128, 128), F32)], sem=("parallel", "arbitrary", "arbitrary"),
        args=(proj, proj, proj, proj, hb, nw), comm=comm)


def _hgrn_bwd(proj, hb, nw, o_saved, st_saved, dyb, Bl, nb, comm=None):
    M = proj.shape[0]

    def body(q_ref, f_ref, i_ref, g_ref, hb_ref, nw_ref, o_ref, st_ref, dy_ref,
             dq_ref, df_ref, di_ref, dg_ref, dhb_ref, dnw_ref, dS, a_dqt, a_dv, a_dkh, a_dgl):
        b, t = pl.program_id(1), pl.program_id(2)

        @pl.when(t == 0)
        def _():
            dS[...] = jnp.zeros_like(dS)

        first_step = jnp.logical_and(b == 0, t == 0)
        s = _hg_setup(nb - 1 - t, q_ref, f_ref, hb_ref)
        v = i_ref[...]
        qt_b, kt_b, kh_b = s["qt"].astype(BF16), s["kt"].astype(BF16), s["kh"].astype(BF16)
        eT = jnp.exp(s["T"])
        att = [jnp.where(s["causal"], _dot(qt_b[:, ln], kt_b[:, ln], NT), 0.0).astype(BF16) for ln in HEAD_LANES]

        o = o_ref[...]
        r = _per_head(lambda a: lax.rsqrt(jnp.mean(a * a, axis=-1, keepdims=True) + EPS), o)
        xhat = o * r
        gv = g_ref[...].astype(F32)
        sgv = _sigmoid(gv)
        dyv = dy_ref[...].astype(F32)
        d_on = dyv * gv * sgv
        dg_out = dyv * xhat * nw_ref[...] * _dsilu(gv, sgv)
        gw = d_on * nw_ref[...]
        do = r * (gw - xhat * _per_head(lambda a, c: jnp.mean(a * c, axis=-1, keepdims=True), gw, xhat))
        dnw = jnp.sum(d_on * xhat, axis=0, keepdims=True)
        do_b = do.astype(BF16)

        datt = [jnp.where(s["causal"], _dot(do_b[:, ln], v[:, ln], NT), 0.0).astype(BF16) for ln in HEAD_LANES]
        dqt = jnp.concatenate([_dot(datt[hh], kt_b[:, ln], NN) for hh, ln in enumerate(HEAD_LANES)], axis=1)
        dkt = jnp.concatenate([_dot(datt[hh], qt_b[:, ln], TN) for hh, ln in enumerate(HEAD_LANES)], axis=1)
        dv = jnp.concatenate([_dot(att[hh], do_b[:, ln], TN) for hh, ln in enumerate(HEAD_LANES)], axis=1)
        last_row = (lax.broadcasted_iota(jnp.int32, (HG_CHUNK, 128), 0) == HG_CHUNK - 1)
        for j in reversed(range(NSUB)):
            sl = slice(HG_CHUNK * j, HG_CHUNK * (j + 1))
            for hh, ln in enumerate(HEAD_LANES):
                St = st_ref[0, hh, 0, j]
                dSt = dS[hh]
                St_b, dSt_b = St.astype(BF16), dSt.astype(BF16)
                eT_j = eT[HG_CHUNK * j:HG_CHUNK * j + 1, ln]
                dkh_j = _dot(v[sl, ln], dSt_b, NN)
                a_dqt[sl, ln] = _dot(do_b[sl, ln], St_b, NN)
                a_dv[sl, ln] = _dot(kh_b[sl, ln], dSt_b, NT)
                a_dkh[sl, ln] = dkh_j
                dlast = (jnp.sum(St * dSt, axis=0, keepdims=True) * eT_j
                         + jnp.sum(dkh_j * s["kh"][sl, ln], axis=0, keepdims=True))
                a_dgl[sl, ln] = jnp.where(last_row, dlast, 0.0)
                dS[hh] = dSt * eT_j + _dot(do_b[sl, ln], qt_b[sl, ln], TN)
        dqt = dqt + a_dqt[...]
        dv = dv + a_dv[...]
        dkh = a_dkh[...]
        dG = dqt * s["qt"] - dkt * s["kt"] - dkh * s["kh"] + a_dgl[...]
        rev_causal = jnp.logical_and(s["same"], s["col"] >= s["row"])
        dgl = _dot(rev_causal.astype(F32), dG, NN, HI)
        dk = dkt * s["eGn"] + dkh * s["eTG"]
        dfg = dgl / s["fg"] - dk
        lb, sg = s["lb"], s["sg"]
        keep = s["valid"].astype(F32)
        df_ref[...] = (dfg * (1.0 - lb) * sg * (1.0 - sg) * keep).astype(df_ref.dtype)
        dq_ref[...] = (dqt * s["eG"] * _dsilu(s["qv"], s["sq"]) * keep).astype(dq_ref.dtype)
        di_ref[...] = (dv * keep).astype(di_ref.dtype)
        dg_ref[...] = (dg_out * keep).astype(dg_ref.dtype)
        dlb = jnp.sum(dfg * (1.0 - sg) * keep, axis=0, keepdims=True) * lb * (1.0 - lb)
        dhb = jnp.concatenate([dlb, -dlb], axis=0)

        @pl.when(first_step)
        def _():
            dhb_ref[...] = dhb
            dnw_ref[...] = dnw

        @pl.when(jnp.logical_not(first_step))
        def _():
            dhb_ref[...] += dhb
            dnw_ref[...] += dnw

    w = 128 * HG_HP
    rowblk = pl.BlockSpec((Q, w), lambda h, b, t: (b * nb + nb - 1 - t, h))
    return _call(
        body, name="hgrn_bwd", grid=(HG_HEADS // HG_HP, Bl, nb),
        in_specs=_hg_specs(nb, rev=True) + [
            rowblk, pl.BlockSpec((1, HG_HP, 1, NSUB, 128, 128), lambda h, b, t: (b, h, nb - 1 - t, 0, 0, 0)), rowblk],
        out_specs=[rowblk, rowblk, rowblk, rowblk,
                   pl.BlockSpec((2, w), lambda h, b, t: (0, h)), pl.BlockSpec((1, w), lambda h, b, t: (0, h))],
        out_shape=[jax.ShapeDtypeStruct((M, HG_WIDTH), BF16)] * 4 + [
            jax.ShapeDtypeStruct((2, HG_WIDTH), F32), jax.ShapeDtypeStruct((1, HG_WIDTH), F32)],
        scratch=[pltpu.VMEM((HG_HP, 128, 128), F32)] + [pltpu.VMEM((Q, w), F32)] * 4,
        sem=("parallel", "arbitrary", "arbitrary"),
        args=(proj, proj, proj, proj, hb, nw, o_saved, st_saved, dyb), comm=comm)


def _adamw(name, parts, w, m, v):
    R, C = w.shape
    S = parts.shape[0]
    tr, tc = (_tile(R, (256, 176, 128, 64, 8)), C) if R % 8 == 0 else (R, 256)
    c1, c2 = 1.0 - ADAM_B1 ** ADAM_STEP, 1.0 - ADAM_B2 ** ADAM_STEP

    def body(p_ref, w_ref, m_ref, v_ref, g_ref, d_ref, nm_ref, nv_ref):
        g = p_ref[0].astype(F32)
        for s in range(1, S):
            g = g + p_ref[s].astype(F32)
        nm = ADAM_B1 * m_ref[...] + (1.0 - ADAM_B1) * g
        nv = ADAM_B2 * v_ref[...] + (1.0 - ADAM_B2) * (g * g)
        g_ref[...] = g
        nm_ref[...] = nm
        nv_ref[...] = nv
        d_ref[...] = -ADAM_LR * ((nm / c1) / (jnp.sqrt(nv / c2) + ADAM_EPS) + ADAM_WD * w_ref[...])

    blk = pl.BlockSpec((tr, tc), lambda i, j: (i, j))
    return pl.pallas_call(
        body, name=name, grid=(R // tr, C // tc),
        in_specs=[pl.BlockSpec((S, tr, tc), lambda i, j: (0, i, j)), blk, blk, blk], out_specs=[blk] * 4,
        out_shape=[jax.ShapeDtypeStruct((R, C), F32)] * 4, compiler_params=_params(("parallel", "parallel")),
    )(parts, w, m, v)


def _sum_parts(name, parts):
    S, R, C = parts.shape

    def body(p_ref, o_ref):
        g = p_ref[0]
        for s in range(1, S):
            g = g + p_ref[s]
        o_ref[...] = g

    return pl.pallas_call(
        body, name=name, out_shape=jax.ShapeDtypeStruct((R, C), F32),
        in_specs=[pl.BlockSpec(memory_space=pltpu.VMEM)], out_specs=pl.BlockSpec(memory_space=pltpu.VMEM),
    )(parts)


def _heads_to_lanes(p):
    lead = p.shape[:-1]
    p4 = p.reshape(lead + (SSD_GROUPS, 4))
    p4 = jnp.pad(p4, [(0, 0)] * len(lead) + [(0, 0), (0, 124)])
    return p4.reshape(lead + (512,))


def _lanes_to_heads(p):
    lead = p.shape[:-1]
    return p.reshape(lead + (SSD_GROUPS, 128))[..., :4].reshape(lead + (SSD_HEADS,))


def _pack_rows(arrs):
    rows = []
    for a in arrs:
        f = a.reshape(-1).astype(F32)
        n = -(-f.shape[0] // D_MODEL) * D_MODEL
        rows.append(jnp.pad(f, (0, n - f.shape[0])).reshape(-1, D_MODEL))
    out = jnp.concatenate(rows, axis=0)
    return jnp.pad(out, ((0, (-out.shape[0]) % 8), (0, 0)))


def _unpack_rows(packed, like):
    outs, r = [], 0
    for a in like:
        n = 1
        for s in a.shape:
            n *= s
        nr = -(-n // D_MODEL)
        outs.append(packed[r:r + nr].reshape(-1)[:n].reshape(a.shape))
        r += nr
    return outs


def _cols(gth):
    return jnp.transpose(gth, (1, 0, 2)).reshape(gth.shape[1], -1)


def _rows(gth):
    return gth.reshape(-1, gth.shape[2])


def _to_rows(g):
    return g.reshape(N_DEV, -1, g.shape[1]).astype(BF16)


def _ffn_fwd_gu(tag, h, norm_w, w_gu_t, comm=None):
    M = h.shape[0]
    F = w_gu_t.shape[0] // 2
    tm = _tile(M, (544, 256))
    n = _rmsnorm_fwd(tag + "_norm", h, norm_w)
    tn = _tile(F, (1408, 704, 256))
    outs = _fused_matmul(
        tag + "_gu", M, F, D_MODEL,
        [dict(a=n, b=w_gu_t, trans_b=True, acc=0), dict(a=n, b=w_gu_t, trans_b=True, bn_off=F // tn, acc=1)], [],
        lambda accs, ex: (accs[0], accs[1], accs[0] * _sigmoid(accs[0]) * accs[1]),
        [BF16, BF16, BF16], 2, tm, tn, D_MODEL, outer="j", comm=comm)
    return (n, *outs[:3]), outs[3:]


def _ffn_fwd_down(tag, h, a, w_down):
    M = h.shape[0]
    F = w_down.shape[0]
    (h_out,) = _fused_matmul(
        tag + "_down", M, D_MODEL, F, [dict(a=a, b=w_down, acc=0)], [(h, 0)],
        lambda accs, ex: (ex[0] + 0.5 * accs[0],), [F32], 1, _tile(M, (544, 256)), D_MODEL, F, outer="j")
    return h_out


def _ffn_bwd(tag, dh, dh_b, h, norm_w, w_gu_t, w_down, saved, scatter=False):
    n, g, u, a = saved
    M = h.shape[0]
    F = w_down.shape[0]
    tm = _tile(M, (544, 256))
    tn = _tile(F, (1408, 704, 256))

    def swiglu_bwd(accs, ex):
        da, gv, uv = 0.5 * accs[0], ex[0].astype(F32), ex[1].astype(F32)
        s = _sigmoid(gv)
        return da * uv * _dsilu(gv, s), da * gv * s

    dg, du = _fused_matmul(
        tag + "_dact", M, F, D_MODEL, [dict(a=dh_b, b=w_down, trans_b=True, acc=0)], [(g, 0), (u, 0)],
        swiglu_bwd, [BF16, BF16], 1, tm, tn, D_MODEL, outer="j")
    (dw_down,) = _matmul_tn(tag + "_dwd", a, dh_b, tn, D_MODEL, tm, scale=0.5)
    dw_g, *p_down = _matmul_tn(tag + "_dwg", dg, n, tn, D_MODEL, tm,
                               comm=("scatter", [_to_rows(dw_down)]) if scatter else None)
    (dw_u,) = _matmul_tn(tag + "_dwu", du, n, tn, D_MODEL, tm)
    dw_gu_t = jnp.concatenate([dw_g, dw_u], axis=0)
    dn, *p_gu = _fused_matmul(
        tag + "_dn", M, D_MODEL, F,
        [dict(a=dg, b=w_gu_t, acc=0), dict(a=du, b=w_gu_t, bk_off=1, acc=0)], [],
        lambda accs, ex: (accs[0],), [F32], 1, tm, D_MODEL, F, outer="i",
        comm=("scatter", [_to_rows(dw_gu_t)]) if scatter else None)
    dh_prev, dh_prev_b, dnorm = _rmsnorm_bwd(tag + "_dnorm", dn, h, norm_w, dh)
    return (dh_prev, dh_prev_b, dnorm, *((p_gu[0], p_down[0]) if scatter else (dw_gu_t, dw_down)))


def kernel(x, meta_tokens, ffn1_norm, ffn1_w_gu, ffn1_w_down, mix_norm, w_in, ssd_conv_w, ssd_conv_b, ssd_dt_bias, ssd_a_log, ssd_d, ssd_norm, hg_lower_bound, hg_norm, w_branch_a, w_branch_b, w_out, ffn2_norm, ffn2_w_gu, ffn2_w_down, final_norm, loss_target, m_meta_tokens, m_ffn1_norm, m_ffn1_w_gu, m_ffn1_w_down, m_mix_norm, m_w_in, m_ssd_conv_w, m_ssd_conv_b, m_ssd_dt_bias, m_ssd_a_log, m_ssd_d, m_ssd_norm, m_hg_lower_bound, m_hg_norm, m_w_branch_a, m_w_branch_b, m_w_out, m_ffn2_norm, m_ffn2_w_gu, m_ffn2_w_down, m_final_norm, v_meta_tokens, v_ffn1_norm, v_ffn1_w_gu, v_ffn1_w_down, v_mix_norm, v_w_in, v_ssd_conv_w, v_ssd_conv_b, v_ssd_dt_bias, v_ssd_a_log, v_ssd_d, v_ssd_norm, v_hg_lower_bound, v_hg_norm, v_w_branch_a, v_w_branch_b, v_w_out, v_ffn2_norm, v_ffn2_w_gu, v_ffn2_w_down, v_final_norm):
    Bl, S, D = x.shape
    T = PAD + N_META + S
    nc = T // Q
    M = Bl * T
    me = 4 * lax.axis_index("x") + 2 * lax.axis_index("y") + lax.axis_index("c")

    bf = lambda a: a[0].astype(BF16)
    bft = lambda a: a[0].T.astype(BF16)
    g_wgu1, g_meta, g_conv_w = _exchange("gather_first", "gather", [bft(ffn1_w_gu), meta_tokens, ssd_conv_w[0]])
    wgu1, meta_full, conv_w_full = _rows(g_wgu1), _cols(g_meta), _cols(g_conv_w)
    bias_p, alog_p, d_p = _heads_to_lanes(ssd_dt_bias), _heads_to_lanes(ssd_a_log), _heads_to_lanes(ssd_d)
    final_w = final_norm.reshape(1, D)

    h0 = jnp.concatenate([jnp.zeros((Bl, PAD, D), F32), jnp.broadcast_to(meta_full[None], (Bl, N_META, D)), x],
                         axis=1).reshape(M, D)
    tm = _tile(M, (544, 256))
    ffn1_saved, (g_wd1, g_win) = _ffn_fwd_gu("ffn1", h0, ffn1_norm, wgu1, comm=("gather", [bf(ffn1_w_down), bft(w_in)]))
    wd1, win_t = _rows(g_wd1), _rows(g_win)
    win_main = jnp.concatenate([win_t[:3072], win_t[3088:]], axis=0)
    win_dt = _heads_to_lanes(win_t[3072:3088].T).T
    h1 = _ffn_fwd_down("ffn1", h0, ffn1_saved[3], wd1)
    un = _rmsnorm_fwd("mix_norm", h1, mix_norm)
    plain = lambda accs, ex: (accs[0],)
    proj, g_wa, g_wb, g_wo = _fused_matmul(
        "in_proj", M, N_MAIN, D, [dict(a=un, b=win_main, trans_b=True, acc=0)], [], plain, [BF16], 1, tm, 1536, D,
        outer="j", comm=("gather", [bf(w_branch_a), bf(w_branch_b), bf(w_out)]))
    wa, wb, wo = _rows(g_wa), _rows(g_wb), _rows(g_wo)
    (dtr,) = _fused_matmul("in_proj_dt", M, 512, D, [dict(a=un, b=win_dt, trans_b=True, acc=0)], [], plain, [F32], 1,
                           tm, 512, D, outer="j")
    xc = _conv_fwd(proj, conv_w_full, ssd_conv_b, Bl, T)
    ya, ssd_prev = _ssd_fwd(xc, dtr, proj, bias_p, alog_p, d_p, ssd_norm, Bl, nc)
    yb, hg_o, hg_st, g_wgu2, g_wd2 = _hgrn_fwd(proj, hg_lower_bound, hg_norm, Bl, nc,
                                               comm=("gather", [bft(ffn2_w_gu), bf(ffn2_w_down)]))
    wgu2, wd2 = _rows(g_wgu2), _rows(g_wd2)

    def branch_fwd(accs, ex):
        pa, pb = accs
        return pa, pb, _sigmoid(ex[0].astype(F32)) * pa + _sigmoid(ex[1].astype(F32)) * pb

    pa, pb, merged = _fused_matmul(
        "branches", M, D, D, [dict(a=ya, b=wa, acc=0), dict(a=yb, b=wb, acc=1)], [(proj, 7), (proj, 8)],
        branch_fwd, [BF16, BF16, BF16], 2, tm, D, D, outer="j")
    (h2,) = _fused_matmul("out_proj", M, D, D, [dict(a=merged, b=wo, acc=0)], [(h1, 0)],
                          lambda accs, ex: (ex[0] + accs[0],), [F32], 1, tm, D, D, outer="j")
    ffn2_saved, _ = _ffn_fwd_gu("ffn2", h2, ffn2_norm, wgu2)
    h3 = _ffn_fwd_down("ffn2", h2, ffn2_saved[3], wd2)

    dh3, dh3_b, d_final, loss_part = _loss_head(h3, final_w, loss_target, Bl, nc)
    dh2, dh2_b, d_ffn2_norm, d_wgu2, d_wd2 = _ffn_bwd("ffn2", dh3, dh3_b, h2, ffn2_norm, wgu2, wd2, ffn2_saved)

    def branch_bwd(accs, ex):
        dm = accs[0]
        ga, gb, pav, pbv = (e.astype(F32) for e in ex)
        sa, sb = _sigmoid(ga), _sigmoid(gb)
        return dm * sa, dm * sb, dm * pav * sa * (1.0 - sa), dm * pbv * sb * (1.0 - sb)

    dpa, dpb, dga, dgb = _fused_matmul(
        "d_merged", M, D, D, [dict(a=dh2_b, b=wo, trans_b=True, acc=0)], [(proj, 7), (proj, 8), (pa, 0), (pb, 0)],
        branch_bwd, [BF16] * 4, 1, tm, D, D, outer="j")
    (d_wo,) = _matmul_tn("d_w_out", merged, dh2_b, D, D, tm)
    (d_wa,) = _matmul_tn("d_w_a", ya, dpa, D, D, tm)
    (d_wb,) = _matmul_tn("d_w_b", yb, dpb, D, D, tm)
    dya, dyb = _fused_matmul(
        "d_branches", M, D, D, [dict(a=dpa, b=wa, trans_b=True, acc=0), dict(a=dpb, b=wb, trans_b=True, acc=1)], [],
        lambda accs, ex: (accs[0], accs[1]), [BF16, BF16], 2, tm, D, D, outer="j")
    *ssd_grads, p_wa, p_wb, p_wo, p_wgu2, p_wd2 = _ssd_bwd(
        xc, dtr, proj, bias_p, alog_p, d_p, ssd_norm, ssd_prev, dya, Bl, nc,
        comm=("scatter", [_to_rows(d_wa), _to_rows(d_wb), _to_rows(d_wo), _to_rows(d_wgu2), _to_rows(d_wd2)]))
    dxs, dB, dC, dz, ddtr, d_bias_p, d_alog_p, d_d_p, d_ssd_norm = ssd_grads
    dxbc, d_conv_w, d_conv_b = _conv_bwd(proj, conv_w_full, ssd_conv_b, jnp.concatenate([dxs, dB, dC], axis=1), Bl, T)
    dq, df, di, dg, d_hb, d_hg_norm = _hgrn_bwd(proj, hg_lower_bound, hg_norm, hg_o, hg_st, dyb, Bl, nc)
    dproj = jnp.concatenate([dz, dxbc, dq, df, di, dg, dga, dgb], axis=1)
    ddtr_b = ddtr.astype(BF16)
    (d_win_main,) = _matmul_tn("d_w_in", dproj, un, 1536, D, tm)
    (d_win_dt,) = _matmul_tn("d_w_in_dt", ddtr_b, un, 512, D, tm)
    d_win_t = jnp.concatenate([d_win_main[:3072], _lanes_to_heads(d_win_dt.T).T, d_win_main[3072:]], axis=0)
    (dun_dt,) = _fused_matmul("d_un_dt", M, D, 512, [dict(a=ddtr_b, b=win_dt, acc=0)], [], plain, [F32], 1,
                              tm, D, 512, outer="j")
    dun, p_win = _fused_matmul("d_un", M, D, N_MAIN, [dict(a=dproj, b=win_main, acc=0)], [(dun_dt, 0)],
                               lambda accs, ex: (accs[0] + ex[0],), [F32], 1, tm, D, 3072, outer="i",
                               comm=("scatter", [_to_rows(d_win_t)]))
    dh1, dh1_b, d_mix_norm = _rmsnorm_bwd("d_mix_norm", dun, h1, mix_norm, dh2)
    dh0, _, d_ffn1_norm, p_wgu1, p_wd1 = _ffn_bwd("ffn1", dh1, dh1_b, h0, ffn1_norm, wgu1, wd1, ffn1_saved, scatter=True)

    dh0 = dh0.reshape(Bl, T, D)
    grad_x = dh0[:, PAD + N_META:]
    d_meta = dh0[:, PAD:PAD + N_META]

    small_grads = [d_ffn1_norm, d_mix_norm, d_conv_b, _lanes_to_heads(d_bias_p), _lanes_to_heads(d_alog_p),
                   _lanes_to_heads(d_d_p), d_ssd_norm, d_hb, d_hg_norm, d_ffn2_norm, d_final.reshape(D), d_conv_w]
    small_packed = _pack_rows(small_grads + [d_meta[b] for b in range(Bl)])
    parts = [p_wgu1, p_wd1, p_win, p_wa, p_wb, p_wo, p_wgu2, p_wd2]
    (small_all,) = _exchange("gather_small_grads", "gather", [small_packed])
    small_sum = _sum_parts("sum_small_grads", small_all)
    unpacked = _unpack_rows(small_sum, small_grads + [d_meta[b] for b in range(Bl)])
    g_small = unpacked[:len(small_grads)]
    g_meta_full = unpacked[len(small_grads)]
    for b in range(1, Bl):
        g_meta_full = g_meta_full + unpacked[len(small_grads) + b]
    g_meta = lax.dynamic_slice_in_dim(g_meta_full, me * (D // N_DEV), D // N_DEV, axis=1)
    g_conv_w = lax.dynamic_slice_in_dim(g_small[11], me * (SSD_CONV_CH // N_DEV), SSD_CONV_CH // N_DEV, axis=1)

    names = ["meta_tokens", "ffn1_norm", "ffn1_w_gu", "ffn1_w_down", "mix_norm", "w_in", "ssd_conv_w", "ssd_conv_b",
             "ssd_dt_bias", "ssd_a_log", "ssd_d", "ssd_norm", "hg_lower_bound", "hg_norm", "w_branch_a", "w_branch_b",
             "w_out", "ffn2_norm", "ffn2_w_gu", "ffn2_w_down", "final_norm"]
    W = dict(meta_tokens=meta_tokens, ffn1_norm=ffn1_norm, ffn1_w_gu=ffn1_w_gu, ffn1_w_down=ffn1_w_down, mix_norm=mix_norm,
             w_in=w_in, ssd_conv_w=ssd_conv_w, ssd_conv_b=ssd_conv_b, ssd_dt_bias=ssd_dt_bias, ssd_a_log=ssd_a_log,
             ssd_d=ssd_d, ssd_norm=ssd_norm, hg_lower_bound=hg_lower_bound, hg_norm=hg_norm, w_branch_a=w_branch_a,
             w_branch_b=w_branch_b, w_out=w_out, ffn2_norm=ffn2_norm, ffn2_w_gu=ffn2_w_gu, ffn2_w_down=ffn2_w_down,
             final_norm=final_norm)
    Mo = dict(meta_tokens=m_meta_tokens, ffn1_norm=m_ffn1_norm, ffn1_w_gu=m_ffn1_w_gu, ffn1_w_down=m_ffn1_w_down,
              mix_norm=m_mix_norm, w_in=m_w_in, ssd_conv_w=m_ssd_conv_w, ssd_conv_b=m_ssd_conv_b, ssd_dt_bias=m_ssd_dt_bias,
              ssd_a_log=m_ssd_a_log, ssd_d=m_ssd_d, ssd_norm=m_ssd_norm, hg_lower_bound=m_hg_lower_bound, hg_norm=m_hg_norm,
              w_branch_a=m_w_branch_a, w_branch_b=m_w_branch_b, w_out=m_w_out, ffn2_norm=m_ffn2_norm, ffn2_w_gu=m_ffn2_w_gu,
              ffn2_w_down=m_ffn2_w_down, final_norm=m_final_norm)
    Vo = dict(meta_tokens=v_meta_tokens, ffn1_norm=v_ffn1_norm, ffn1_w_gu=v_ffn1_w_gu, ffn1_w_down=v_ffn1_w_down,
              mix_norm=v_mix_norm, w_in=v_w_in, ssd_conv_w=v_ssd_conv_w, ssd_conv_b=v_ssd_conv_b, ssd_dt_bias=v_ssd_dt_bias,
              ssd_a_log=v_ssd_a_log, ssd_d=v_ssd_d, ssd_norm=v_ssd_norm, hg_lower_bound=v_hg_lower_bound, hg_norm=v_hg_norm,
              w_branch_a=v_w_branch_a, w_branch_b=v_w_branch_b, w_out=v_w_out, ffn2_norm=v_ffn2_norm, ffn2_w_gu=v_ffn2_w_gu,
              ffn2_w_down=v_ffn2_w_down, final_norm=v_final_norm)
    grads, deltas, new_m, new_v = {}, {}, {}, {}
    big_names = ["ffn1_w_gu", "ffn1_w_down", "w_in", "w_branch_a", "w_branch_b", "w_out", "ffn2_w_gu", "ffn2_w_down"]
    transposed = ("ffn1_w_gu", "ffn2_w_gu", "w_in")
    for nm, part in zip(big_names, parts):
        view = (lambda a: a[0].T) if nm in transposed else (lambda a: a[0])
        back = (lambda o: o.T[None]) if nm in transposed else (lambda o: o[None])
        outs = _adamw("adamw_" + nm, part, view(W[nm]), view(Mo[nm]), view(Vo[nm]))
        grads[nm], deltas[nm], new_m[nm], new_v[nm] = (back(o) for o in outs)
    small_names = ["ffn1_norm", "mix_norm", "ssd_conv_b", "ssd_dt_bias", "ssd_a_log", "ssd_d", "ssd_norm", "hg_lower_bound",
                   "hg_norm", "ffn2_norm", "final_norm", "ssd_conv_w", "meta_tokens"]
    small_g = g_small[:11] + [g_conv_w.reshape(ssd_conv_w.shape), g_meta]
    pk = lambda d: _pack_rows([d[nm] for nm in small_names])
    outs = _adamw("adamw_small", _pack_rows(small_g)[None], pk(W), pk(Mo), pk(Vo))
    like = [W[nm] for nm in small_names]
    for dst, o in zip((grads, deltas, new_m, new_v), outs):
        for nm, val in zip(small_names, _unpack_rows(o, like)):
            dst[nm] = val

    loss = lax.psum(loss_part[0, 0], MESH_AXES)
    return (loss, grad_x, *[grads[nm] for nm in names], *[deltas[nm] for nm in names],
            *[new_m[nm] for nm in names], *[new_v[nm] for nm in names])
```

```python
import functools

import jax
import jax.numpy as jnp
from jax import lax
from jax.experimental import pallas as pl
from jax.experimental.pallas import tpu as pltpu

F32, BF16 = jnp.float32, jnp.bfloat16
NN, NT, TN = ((1,), (0,)), ((1,), (1,)), ((0,), (0,))
MESH_AXES = ("x", "y", "c")
N_DEV = 8

D_MODEL = 1024
N_META = 16
EPS = 1e-6
SSD_HEADS, SSD_HEAD_DIM, SSD_GROUPS, SSD_STATE, SSD_CONV, Q = 16, 64, 4, 128, 4, 128
SSD_INNER = SSD_HEADS * SSD_HEAD_DIM
SSD_CONV_CH = SSD_INNER + 2 * SSD_GROUPS * SSD_STATE
HG_WIDTH, HG_HEADS, HG_CHUNK = 1024, 8, 16
PAD = Q - N_META
N_MAIN = 9 * 1024
ADAM_LR, ADAM_B1, ADAM_B2, ADAM_EPS, ADAM_WD, ADAM_STEP = 0.001, 0.9, 0.999, 1e-08, 0.01, 10
VMEM_LIMIT = 52 * 1024 * 1024


def _dot(a, b, dims, prec=None):
    return lax.dot_general(a, b, (dims, ((), ())), precision=prec, preferred_element_type=F32)


def _dot01(a, b, dims, sel):
    x = b if sel == "a" else a
    hi = x.astype(BF16)
    r1 = x - hi.astype(F32)
    mid = r1.astype(BF16)
    lo = (r1 - mid.astype(F32)).astype(BF16)
    s = (a if sel == "a" else b).astype(BF16)
    parts = [_dot(s, p, dims) if sel == "a" else _dot(p, s, dims) for p in (hi, mid, lo)]
    return parts[0] + parts[1] + parts[2]


def _sigmoid(x):
    return 1.0 / (1.0 + jnp.exp(-x))


def _dsilu(x, s):
    return s * (1.0 + x * (1.0 - s))


def _softplus(x):
    e = jnp.exp(-jnp.abs(x))
    u = 1.0 + e
    log1p_e = jnp.where(u == 1.0, e, jnp.log(u) * e / (u - 1.0))
    return jnp.maximum(x, 0.0) + log1p_e


def _params(sem):
    return pltpu.CompilerParams(dimension_semantics=sem, vmem_limit_bytes=VMEM_LIMIT)


def _tile(n, prefs):
    for p in prefs:
        if n % p == 0:
            return p
    return n


CHIP_FLIPS = ((1, 0), (0, 1), (1, 1))
N_PEER = N_DEV - 1


def _comm_gather(srcs, outs, send_sems, recv_sems, local_sems):
    n = len(srcs)
    x, y, c = (lax.axis_index(a) for a in MESH_AXES)
    dev = lambda px, py, pc: 4 * px + 2 * py + pc
    me, sib = dev(x, y, c), (x, y, 1 - c)

    def rc(w, k, slot, to, src=None):
        return pltpu.make_async_remote_copy(
            src_ref=outs[w].at[slot] if src is None else src, dst_ref=outs[w].at[slot],
            send_sem=send_sems.at[w, k], recv_sem=recv_sems.at[w, k], device_id=to, device_id_type=pl.DeviceIdType.MESH)

    def local(w):
        return pltpu.make_async_copy(srcs[w], outs[w].at[me], local_sems.at[w])

    def start():
        for w in range(n):
            local(w).start()
            rc(w, 0, me, sib, src=srcs[w]).start()
            for j, (fx, fy) in enumerate(CHIP_FLIPS):
                rc(w, 1 + j, me, (x ^ fx, y ^ fy, c), src=srcs[w]).start()

    def finish():
        for w in range(n):
            for j, (fx, fy) in enumerate(CHIP_FLIPS):
                slot = dev(x ^ fx, y ^ fy, c)
                rc(w, 1 + j, slot, sib).wait_recv()
                rc(w, 4 + j, slot, sib).start()
        for w in range(n):
            rc(w, 0, dev(x, y, 1 - c), sib).wait_recv()
            rc(w, 0, me, sib, src=srcs[w]).wait_send()
            for j, (fx, fy) in enumerate(CHIP_FLIPS):
                rc(w, 4 + j, dev(x ^ fx, y ^ fy, 1 - c), sib).wait_recv()
                rc(w, 1 + j, me, sib, src=srcs[w]).wait_send()
                rc(w, 4 + j, dev(x ^ fx, y ^ fy, c), sib).wait_send()
            local(w).wait()

    return start, finish


def _comm_scatter(srcs, outs, send_sems, recv_sems, local_sems):
    n = len(srcs)
    x, y, c = (lax.axis_index(a) for a in MESH_AXES)
    me = 4 * x + 2 * y + c

    def copies():
        out = []
        for w in range(n):
            out.append(pltpu.make_async_copy(srcs[w].at[me], outs[w].at[me], local_sems.at[w]))
            for k in range(1, N_DEV):
                px, py, pc = x ^ (k >> 2), y ^ ((k >> 1) & 1), c ^ (k & 1)
                out.append(pltpu.make_async_remote_copy(
                    src_ref=srcs[w].at[4 * px + 2 * py + pc], dst_ref=outs[w].at[me],
                    send_sem=send_sems.at[w, k - 1], recv_sem=recv_sems.at[w, k - 1],
                    device_id=(px, py, pc), device_id_type=pl.DeviceIdType.MESH))
        return out

    def start():
        for cp in copies():
            cp.start()

    def finish():
        for cp in copies():
            cp.wait()

    return start, finish


def _comm_parts(comm):
    kind, arrays = comm
    n = len(arrays)
    shapes = [jax.ShapeDtypeStruct((N_DEV,) + (a.shape[1:] if kind == "scatter" else a.shape), a.dtype) for a in arrays]
    sems = [pltpu.SemaphoreType.DMA((n, N_PEER)), pltpu.SemaphoreType.DMA((n, N_PEER)), pltpu.SemaphoreType.DMA((n,))]
    return n, shapes, sems, (_comm_scatter if kind == "scatter" else _comm_gather)


def _exchange(name, kind, arrays):
    n, shapes, sems, make = _comm_parts((kind, arrays))

    def body(*refs):
        start, finish = make(refs[:n], refs[n:2 * n], *refs[2 * n:])
        start()
        finish()

    any_spec = pl.BlockSpec(memory_space=pl.ANY)
    return pl.pallas_call(
        body, name=name, in_specs=[any_spec] * n, out_specs=[any_spec] * n, out_shape=shapes, scratch_shapes=sems,
        compiler_params=pltpu.CompilerParams(has_side_effects=True),
    )(*arrays)


def _call(body, *, name, grid, in_specs, out_specs, out_shape, scratch, sem, args, comm=None):
    if comm is None:
        return pl.pallas_call(body, name=name, grid=grid, in_specs=in_specs, out_specs=out_specs, out_shape=out_shape,
                              scratch_shapes=scratch, compiler_params=_params(sem))(*args)
    n, shapes, sems, make = _comm_parts(comm)
    n_in, n_out, n_scr = len(in_specs), len(out_specs), len(scratch)

    def carrier(*refs):
        ins, csrc = refs[:n_in], refs[n_in:n_in + n]
        outs, cout = refs[n_in + n:n_in + n + n_out], refs[n_in + n + n_out:n_in + 2 * n + n_out]
        rest = refs[n_in + 2 * n + n_out:]
        start, finish = make(csrc, cout, *rest[n_scr:])
        ids = [pl.program_id(a) for a in range(len(grid))]
        first = functools.reduce(jnp.logical_and, [i == 0 for i in ids])
        last = functools.reduce(jnp.logical_and, [i == g - 1 for i, g in zip(ids, grid)])
        pl.when(first)(start)
        body(*ins, *outs, *rest[:n_scr])
        pl.when(last)(finish)

    any_spec = pl.BlockSpec(memory_space=pl.ANY)
    return pl.pallas_call(
        carrier, name=name, grid=grid, in_specs=list(in_specs) + [any_spec] * n,
        out_specs=list(out_specs) + [any_spec] * n, out_shape=list(out_shape) + shapes,
        scratch_shapes=list(scratch) + sems,
        compiler_params=pltpu.CompilerParams(dimension_semantics=("arbitrary",) * len(grid),
                                             vmem_limit_bytes=VMEM_LIMIT, has_side_effects=True),
    )(*args, *comm[1])


def _fused_matmul(name, M, N, K, pairs, extras, epilogue, out_dtypes, n_acc, tm, tn, tk, outer="i", comm=None):
    nk = K // tk
    n_pairs, n_ex, n_out = len(pairs), len(extras), len(out_dtypes)

    def ij(g0, g1):
        return (g0, g1) if outer == "i" else (g1, g0)

    in_specs, args = [], []
    for p in pairs:
        ao, bk, bn = p.get("a_off", 0), p.get("bk_off", 0), p.get("bn_off", 0)
        in_specs.append(pl.BlockSpec((tm, tk), lambda g0, g1, k, ao=ao: (ij(g0, g1)[0], k + ao)))
        if p.get("trans_b"):
            in_specs.append(pl.BlockSpec((tn, tk), lambda g0, g1, k, bk=bk, bn=bn: (ij(g0, g1)[1] + bn, k + bk)))
        else:
            in_specs.append(pl.BlockSpec((tk, tn), lambda g0, g1, k, bk=bk, bn=bn: (k + bk, ij(g0, g1)[1] + bn)))
        args += [p["a"], p["b"]]
    for arr, off in extras:
        in_specs.append(pl.BlockSpec((tm, tn), lambda g0, g1, k, off=off: (ij(g0, g1)[0], ij(g0, g1)[1] + off)))
        args.append(arr)
    out_specs = [pl.BlockSpec((tm, tn), lambda g0, g1, k: ij(g0, g1)) for _ in out_dtypes]
    out_shape = [jax.ShapeDtypeStruct((M, N), dt) for dt in out_dtypes]
    grid = (M // tm, N // tn, nk) if outer == "i" else (N // tn, M // tm, nk)

    def partials(refs):
        accs = [None] * n_acc
        for idx, p in enumerate(pairs):
            d = _dot(refs[2 * idx][...], refs[2 * idx + 1][...], NT if p.get("trans_b") else NN)
            accs[p["acc"]] = d if accs[p["acc"]] is None else accs[p["acc"]] + d
        return accs

    def finish(accs, refs):
        ex = [r[...] for r in refs[2 * n_pairs:2 * n_pairs + n_ex]]
        outs = refs[2 * n_pairs + n_ex:2 * n_pairs + n_ex + n_out]
        for o, r in zip(outs, epilogue(accs, ex)):
            o[...] = r.astype(o.dtype)

    if nk == 1:
        def body(*refs):
            finish(partials(refs), refs)
        scratch = []
    else:
        def body(*refs):
            acc_refs = refs[-n_acc:]
            k = pl.program_id(2)
            new = partials(refs)

            @pl.when(k == 0)
            def _():
                for a, v in zip(acc_refs, new):
                    a[...] = v

            @pl.when(k > 0)
            def _():
                for a, v in zip(acc_refs, new):
                    a[...] += v

            @pl.when(k == nk - 1)
            def _():
                finish([a[...] for a in acc_refs], refs)
        scratch = [pltpu.VMEM((tm, tn), F32) for _ in range(n_acc)]

    return _call(body, name=name, grid=grid, in_specs=in_specs, out_specs=out_specs, out_shape=out_shape,
                 scratch=scratch, sem=("parallel", "parallel", "arbitrary"), args=args, comm=comm)


def _matmul_tn(name, x, y, t1, t2, tr, scale=1.0, comm=None):
    R, K1 = x.shape
    N1 = y.shape[1]
    nr = R // tr

    def body(x_ref, y_ref, o_ref):
        r = pl.program_id(2)
        d = _dot(x_ref[...], y_ref[...], TN)

        @pl.when(r == 0)
        def _():
            o_ref[...] = d

        @pl.when(r > 0)
        def _():
            o_ref[...] += d

        if scale != 1.0:
            @pl.when(r == nr - 1)
            def _():
                o_ref[...] = o_ref[...] * scale

    return _call(
        body, name=name, grid=(K1 // t1, N1 // t2, nr),
        in_specs=[pl.BlockSpec((tr, t1), lambda i, j, r: (r, i)), pl.BlockSpec((tr, t2), lambda i, j, r: (r, j))],
        out_specs=[pl.BlockSpec((t1, t2), lambda i, j, r: (i, j))],
        out_shape=[jax.ShapeDtypeStruct((K1, N1), F32)], scratch=[],
        sem=("parallel", "parallel", "arbitrary"), args=(x, y), comm=comm)


def _rmsnorm_fwd(name, h, w):
    M, D = h.shape
    tm = _tile(M, (544, 256, 128))

    def body(h_ref, w_ref, o_ref):
        x = h_ref[...]
        r = lax.rsqrt(jnp.mean(x * x, axis=-1, keepdims=True) + EPS)
        o_ref[...] = (x * r * w_ref[...]).astype(o_ref.dtype)

    return pl.pallas_call(
        body, name=name, grid=(M // tm,),
        in_specs=[pl.BlockSpec((tm, D), lambda i: (i, 0)), pl.BlockSpec((1, D), lambda i: (0, 0))],
        out_specs=pl.BlockSpec((tm, D), lambda i: (i, 0)),
        out_shape=jax.ShapeDtypeStruct((M, D), BF16), compiler_params=_params(("parallel",)),
    )(h, w)


def _rmsnorm_bwd(name, dn, h, w, dh_in):
    M, D = h.shape
    tm = _tile(M, (544, 256, 128))

    def body(dn_ref, h_ref, w_ref, dhi_ref, dh_ref, dhb_ref, dw_ref):
        x = h_ref[...]
        r = lax.rsqrt(jnp.mean(x * x, axis=-1, keepdims=True) + EPS)
        xhat = x * r
        dn_v = dn_ref[...]
        gw = dn_v * w_ref[...]
        dx = r * (gw - xhat * jnp.mean(gw * xhat, axis=-1, keepdims=True))
        dh = dhi_ref[...] + dx
        dh_ref[...] = dh
        dhb_ref[...] = dh.astype(BF16)
        dw = jnp.sum(dn_v * xhat, axis=0, keepdims=True)

        @pl.when(pl.program_id(0) == 0)
        def _():
            dw_ref[...] = dw

        @pl.when(pl.program_id(0) > 0)
        def _():
            dw_ref[...] += dw

    row = pl.BlockSpec((tm, D), lambda i: (i, 0))
    vec = pl.BlockSpec((1, D), lambda i: (0, 0))
    return pl.pallas_call(
        body, name=name, grid=(M // tm,), in_specs=[row, row, vec, row], out_specs=[row, row, vec],
        out_shape=[jax.ShapeDtypeStruct((M, D), F32), jax.ShapeDtypeStruct((M, D), BF16), jax.ShapeDtypeStruct((1, D), F32)],
        compiler_params=_params(("arbitrary",)),
    )(dn, h, w, dh_in)


def _loss_head(h, w, target, Bl, nb):
    M, D = h.shape

    def body(h_ref, w_ref, t_ref, dh_ref, dhb_ref, dw_ref, loss_ref):
        b, t = pl.program_id(0), pl.program_id(1)
        live = (t > 0).astype(F32)
        x = h_ref[...]
        r = lax.rsqrt(jnp.mean(x * x, axis=-1, keepdims=True) + EPS)
        xhat = x * r
        wv = w_ref[...]
        err = (xhat * wv - t_ref[0]) * live
        dy = err * (1.0 / D)
        gw = dy * wv
        dx = r * (gw - xhat * jnp.mean(gw * xhat, axis=-1, keepdims=True))
        dh_ref[...] = dx
        dhb_ref[...] = dx.astype(BF16)
        dw = jnp.sum(dy * xhat, axis=0, keepdims=True)
        part = 0.5 * jnp.sum(jnp.sum(err * err, axis=-1, keepdims=True) * (1.0 / D), axis=0, keepdims=True)
        first = jnp.logical_and(b == 0, t == 0)

        @pl.when(first)
        def _():
            dw_ref[...] = dw
            loss_ref[...] = jnp.broadcast_to(part, loss_ref.shape)

        @pl.when(jnp.logical_not(first))
        def _():
            dw_ref[...] += dw
            loss_ref[...] += jnp.broadcast_to(part, loss_ref.shape)

    row = pl.BlockSpec((Q, D), lambda b, t: (b * nb + t, 0))
    vec = pl.BlockSpec((1, D), lambda b, t: (0, 0))
    return pl.pallas_call(
        body, name="loss_head", grid=(Bl, nb),
        in_specs=[row, vec, pl.BlockSpec((1, Q, D), lambda b, t: (b, jnp.maximum(t - 1, 0), 0))],
        out_specs=[row, row, vec, pl.BlockSpec((8, 128), lambda b, t: (0, 0))],
        out_shape=[jax.ShapeDtypeStruct((M, D), F32), jax.ShapeDtypeStruct((M, D), BF16),
                   jax.ShapeDtypeStruct((1, D), F32), jax.ShapeDtypeStruct((8, 128), F32)],
        compiler_params=_params(("arbitrary", "arbitrary")),
    )(h, w, target)


CONV_TC = 256


def _conv_pre(xr_ref, w_ref, b_ref):
    x = xr_ref[...].astype(F32)
    acc = b_ref[...] + w_ref[SSD_CONV - 1:SSD_CONV, :] * x
    for k in range(1, SSD_CONV):
        acc = acc + w_ref[SSD_CONV - 1 - k:SSD_CONV - k, :] * pltpu.roll(x, k, 0)
    return x, acc


def _conv_fwd(proj, w, b, Bl, T):
    M = proj.shape[0]
    off = 1024 // CONV_TC

    def body(xr_ref, w_ref, b_ref, o_ref):
        _, acc = _conv_pre(xr_ref, w_ref, b_ref)
        row = lax.broadcasted_iota(jnp.int32, acc.shape, 0)
        o_ref[...] = jnp.where(row >= PAD, acc * _sigmoid(acc), 0.0).astype(o_ref.dtype)

    return pl.pallas_call(
        body, name="conv_fwd", grid=(Bl, SSD_CONV_CH // CONV_TC),
        in_specs=[pl.BlockSpec((T, CONV_TC), lambda bb, j: (bb, j + off)),
                  pl.BlockSpec((SSD_CONV, CONV_TC), lambda bb, j: (0, j)), pl.BlockSpec((1, CONV_TC), lambda bb, j: (0, j))],
        out_specs=pl.BlockSpec((T, CONV_TC), lambda bb, j: (bb, j)),
        out_shape=jax.ShapeDtypeStruct((M, SSD_CONV_CH), BF16), compiler_params=_params(("parallel", "parallel")),
    )(proj, w, b)


def _conv_bwd(proj, w, b, dxc, Bl, T):
    M = proj.shape[0]
    off = 1024 // CONV_TC

    def body(xr_ref, w_ref, b_ref, d_ref, dx_ref, dw_ref, db_ref):
        x, acc = _conv_pre(xr_ref, w_ref, b_ref)
        row = lax.broadcasted_iota(jnp.int32, acc.shape, 0)
        s = _sigmoid(acc)
        dpre = jnp.where(row >= PAD, d_ref[...].astype(F32) * _dsilu(acc, s), 0.0)
        dx = w_ref[SSD_CONV - 1:SSD_CONV, :] * dpre
        dws = [jnp.sum(dpre * x, axis=0, keepdims=True)]
        for k in range(1, SSD_CONV):
            dx = dx + w_ref[SSD_CONV - 1 - k:SSD_CONV - k, :] * pltpu.roll(dpre, T - k, 0)
            dws.append(jnp.sum(dpre * pltpu.roll(x, k, 0), axis=0, keepdims=True))
        dx_ref[...] = dx.astype(dx_ref.dtype)
        dw = jnp.concatenate(dws[::-1], axis=0)
        db = jnp.sum(dpre, axis=0, keepdims=True)

        @pl.when(pl.program_id(1) == 0)
        def _():
            dw_ref[...] = dw
            db_ref[...] = db

        @pl.when(pl.program_id(1) > 0)
        def _():
            dw_ref[...] += dw
            db_ref[...] += db

    return pl.pallas_call(
        body, name="conv_bwd", grid=(SSD_CONV_CH // CONV_TC, Bl),
        in_specs=[pl.BlockSpec((T, CONV_TC), lambda j, bb: (bb, j + off)),
                  pl.BlockSpec((SSD_CONV, CONV_TC), lambda j, bb: (0, j)), pl.BlockSpec((1, CONV_TC), lambda j, bb: (0, j)),
                  pl.BlockSpec((T, CONV_TC), lambda j, bb: (bb, j))],
        out_specs=[pl.BlockSpec((T, CONV_TC), lambda j, bb: (bb, j)),
                   pl.BlockSpec((SSD_CONV, CONV_TC), lambda j, bb: (0, j)), pl.BlockSpec((1, CONV_TC), lambda j, bb: (0, j))],
        out_shape=[jax.ShapeDtypeStruct((M, SSD_CONV_CH), BF16), jax.ShapeDtypeStruct((SSD_CONV, SSD_CONV_CH), F32),
                   jax.ShapeDtypeStruct((1, SSD_CONV_CH), F32)],
        compiler_params=_params(("parallel", "arbitrary")),
    )(proj, w, b, dxc)


N_PAIR = SSD_HEADS // 2
HPG = SSD_HEADS // SSD_GROUPS
GW = SSD_INNER // SSD_GROUPS


def _per_group(fn, *arrs):
    return jnp.concatenate([jnp.broadcast_to(fn(*(a[:, GW * g:GW * (g + 1)] for a in arrs)), (arrs[0].shape[0], GW))
                            for g in range(SSD_GROUPS)], axis=1)


def _ssd_prep(c, dtr_ref, bias_ref, alog_ref, d_ref):
    row = lax.broadcasted_iota(jnp.int32, (Q, 128), 0)
    col = lax.broadcasted_iota(jnp.int32, (Q, 128), 1)
    live = col < SSD_HEADS
    valid = jnp.logical_and(jnp.logical_or(c > 0, row >= PAD), live)
    pre = dtr_ref[...] + bias_ref[...]
    dt = jnp.where(valid, _softplus(pre), 0.0)
    A = jnp.where(live[0:1], -jnp.exp(alog_ref[...]), 0.0)
    tri = row >= col
    eye = (row == col).astype(BF16)
    cs = _dot01(tri, dt * A, NN, "a")
    cst = _dot01(eye, cs, NT, "a")
    spread = (lax.broadcasted_iota(jnp.int32, (128, SSD_INNER), 0)
              == lax.broadcasted_iota(jnp.int32, (128, SSD_INNER), 1) // SSD_HEAD_DIM).astype(BF16)
    dt_w = _dot01(dt, spread, NN, "b")
    cs_w = _dot01(cs, spread, NN, "b")
    d_w = _dot01(jnp.broadcast_to(d_ref[...], (8, 128)), spread, NN, "b")[0:1]
    lane = lax.broadcasted_iota(jnp.int32, (Q, SSD_INNER), 1)
    first = (lane % 128) < SSD_HEAD_DIM
    return dict(row=row, col=col, valid=valid, pre=pre, dt=dt, A=A, tri=tri, eye=eye, cs=cs, cst=cst, spread=spread,
                dt_w=dt_w, cs_w=cs_w, d_w=d_w, ecs_w=jnp.exp(cs_w), decay_w=jnp.exp(cs_w[Q - 1:Q] - cs_w), first=first)


def _ssd_chunk(xc_ref, s, states):
    xv = xc_ref[:, 0:SSD_INNER].astype(F32)
    Bs = [xc_ref[:, SSD_INNER + 128 * g:SSD_INNER + 128 * (g + 1)] for g in range(SSD_GROUPS)]
    Cs = [xc_ref[:, SSD_INNER + 512 + 128 * g:SSD_INNER + 512 + 128 * (g + 1)] for g in range(SSD_GROUPS)]
    X = xv * s["dt_w"]
    X0 = jnp.where(s["first"], X, 0.0)
    Xb = (X0.astype(BF16), (X - X0).astype(BF16))
    Xd = (X * s["decay_w"]).astype(BF16)
    CB = [_dot(Cs[g], Bs[g], NT) for g in range(SSD_GROUPS)]
    Lms = [jnp.exp(jnp.where(s["tri"], s["cs"][:, h:h + 1] - s["cst"][h:h + 1, :], -jnp.inf)) for h in range(SSD_HEADS)]
    Ms = [CB[h // HPG] * Lms[h] for h in range(SSD_HEADS)]
    Mb = [m.astype(BF16) for m in Ms]
    prev_b = [st.astype(BF16) for st in states]
    yds, yos, sts = [], [], []
    for p in range(N_PAIR):
        g, ln = p // 2, slice(128 * p, 128 * (p + 1))
        yds.append(_dot(Mb[2 * p], Xb[0][:, ln], NN) + _dot(Mb[2 * p + 1], Xb[1][:, ln], NN))
        yos.append(_dot(Cs[g], prev_b[p], NT))
        sts.append(_dot(Xd[:, ln], Bs[g], TN))
    yo = jnp.concatenate(yos, axis=1)
    y = jnp.concatenate(yds, axis=1) + yo * s["ecs_w"] + xv * s["d_w"]
    upper = s["row"] < SSD_HEAD_DIM
    cl = s["cs"][Q - 1:Q, :]
    ecl_rows = [jnp.where(upper, jnp.exp(cl[:, 2 * p:2 * p + 1]), jnp.exp(cl[:, 2 * p + 1:2 * p + 2])) for p in range(N_PAIR)]
    new_states = [states[p] * ecl_rows[p] + sts[p] for p in range(N_PAIR)]
    return y, new_states, dict(xv=xv, Bs=Bs, Cs=Cs, X=X, Xb=Xb, CB=CB, Lms=Lms, Ms=Ms, Mb=Mb, prev_b=prev_b, yo=yo,
                               ecl_rows=ecl_rows)


def _ssd_in_specs(nc, rev=False):
    rb = (lambda b, c: b * nc + nc - 1 - c) if rev else (lambda b, c: b * nc + c)
    vec = pl.BlockSpec((1, 128), lambda b, c: (0, 0))
    return [pl.BlockSpec((Q, SSD_CONV_CH), lambda b, c: (rb(b, c), 0)),
            pl.BlockSpec((Q, 128), lambda b, c: (rb(b, c), 0)),
            pl.BlockSpec((Q, SSD_INNER), lambda b, c: (rb(b, c), 0)),
            vec, vec, vec, pl.BlockSpec((1, SSD_INNER), lambda b, c: (0, 0))]


def _ssd_fwd(xc, dtr, proj, bias_p, alog_p, d_p, nw, Bl, nc):
    M = xc.shape[0]

    def body(xc_ref, dtr_ref, z_ref, bias_ref, alog_ref, d_ref, nw_ref, y_ref, prev_ref, state):
        c = pl.program_id(1)

        @pl.when(c == 0)
        def _():
            state[...] = jnp.zeros_like(state)

        s = _ssd_prep(c, dtr_ref, bias_ref, alog_ref, d_ref)
        states = [state[p] for p in range(N_PAIR)]
        y, new_states, _ = _ssd_chunk(xc_ref, s, states)
        for p in range(N_PAIR):
            prev_ref[0, 0, p] = states[p]
            state[p] = new_states[p]
        zz = z_ref[...].astype(F32)
        yg = y * zz * _sigmoid(zz)
        r = _per_group(lambda a: lax.rsqrt(jnp.mean(a * a, axis=-1, keepdims=True) + EPS), yg)
        y_ref[...] = (yg * r * nw_ref[...]).astype(y_ref.dtype)

    return pl.pallas_call(
        body, name="ssd_fwd", grid=(Bl, nc), in_specs=_ssd_in_specs(nc),
        out_specs=[pl.BlockSpec((Q, SSD_INNER), lambda b, c: (b * nc + c, 0)),
                   pl.BlockSpec((1, 1, N_PAIR, 128, 128), lambda b, c: (b, c, 0, 0, 0))],
        out_shape=[jax.ShapeDtypeStruct((M, SSD_INNER), BF16), jax.ShapeDtypeStruct((Bl, nc, N_PAIR, 128, 128), F32)],
        scratch_shapes=[pltpu.VMEM((N_PAIR, 128, 128), F32)],
        compiler_params=_params(("arbitrary", "arbitrary")),
    )(xc, dtr, proj, bias_p, alog_p, d_p, nw)


def _ssd_bwd(xc, dtr, proj, bias_p, alog_p, d_p, nw, prev, dya, Bl, nc, comm=None):
    M = xc.shape[0]

    def body(xc_ref, dtr_ref, z_ref, bias_ref, alog_ref, d_ref, nw_ref, prev_ref, dy_ref,
             dxc_ref, dz_ref, ddtr_ref, dbias_ref, dalog_ref, dd_ref, dnw_ref, dS):
        b, t = pl.program_id(0), pl.program_id(1)

        @pl.when(t == 0)
        def _():
            dS[...] = jnp.zeros_like(dS)

        s = _ssd_prep(nc - 1 - t, dtr_ref, bias_ref, alog_ref, d_ref)
        states = [prev_ref[0, 0, p] for p in range(N_PAIR)]
        y, _, k = _ssd_chunk(xc_ref, s, states)
        xv, Bs, Cs, Xb = k["xv"], k["Bs"], k["Cs"], k["Xb"]

        zz = z_ref[...].astype(F32)
        sz = _sigmoid(zz)
        silu_z = zz * sz
        yg = y * silu_z
        r = _per_group(lambda a: lax.rsqrt(jnp.mean(a * a, axis=-1, keepdims=True) + EPS), yg)
        xhat = yg * r
        dout = dy_ref[...].astype(F32)
        gw = dout * nw_ref[...]
        dyg = r * (gw - xhat * _per_group(lambda a, c2: jnp.mean(a * c2, axis=-1, keepdims=True), gw, xhat))
        dnw = jnp.sum(dout * xhat, axis=0, keepdims=True)
        dz_ref[...] = (dyg * y * _dsilu(zz, sz)).astype(dz_ref.dtype)
        dy = dyg * silu_z
        dy0 = jnp.where(s["first"], dy, 0.0)
        dyb = (dy0.astype(BF16), (dy - dy0).astype(BF16))
        dYo = (dy * s["ecs_w"]).astype(BF16)

        dS_f = [dS[p] for p in range(N_PAIR)]
        dS_b = [d.astype(BF16) for d in dS_f]
        BdS, dXm, dprev, dCs, dMs, XdS = [], [], [], [[] for _ in range(SSD_GROUPS)], [], []
        for p in range(N_PAIR):
            g, ln = p // 2, slice(128 * p, 128 * (p + 1))
            BdS.append(_dot(Bs[g], dS_b[p], NT))
            dXm.append(_dot(k["Mb"][2 * p], dyb[0][:, ln], TN) + _dot(k["Mb"][2 * p + 1], dyb[1][:, ln], TN))
            dprev.append(_dot(dYo[:, ln], Cs[g], TN))
            dCs[g].append(_dot(dYo[:, ln], k["prev_b"][p], NN))
            for hh in range(2):
                dMs.append(_dot(dyb[hh][:, ln], Xb[hh][:, ln], NT))
                XdS.append(_dot(Xb[hh][:, ln], dS_b[p], NN))
        dX = jnp.concatenate(dXm, axis=1) + s["decay_w"] * jnp.concatenate(BdS, axis=1)
        dxs = dy * s["d_w"] + dX * s["dt_w"]

        heads = lambda a: _dot01(a, s["spread"], NT, "b")
        ddt = heads(dX * xv)
        dcs = heads(dy * k["yo"] * s["ecs_w"])
        dD = jnp.sum(heads(dy * xv), axis=0, keepdims=True)

        col, row = s["col"], s["row"]
        lane1 = col[0:1]
        rowsT = lax.broadcasted_iota(jnp.int32, (128, Q), 0)
        dcs_t = jnp.zeros((128, Q), F32)
        dcl = jnp.zeros((1, 128), F32)
        dB_out, dC_out = [], []
        for g in range(SSD_GROUPS):
            Bf = Bs[g].astype(F32)
            dCB = jnp.zeros((Q, Q), F32)
            dBacc = jnp.zeros((Q, 128), F32)
            for r4 in range(HPG):
                h = HPG * g + r4
                p, hh = h // 2, h % 2
                W = dMs[h] * k["Ms"][h]
                dCB = dCB + dMs[h] * k["Lms"][h]
                decay_h = s["decay_w"][:, SSD_HEAD_DIM * h:SSD_HEAD_DIM * h + 1]
                dBacc = dBacc + decay_h * XdS[h]
                tdec = jnp.sum(XdS[h] * Bf, axis=1, keepdims=True) * decay_h
                dcs = dcs + jnp.where(col == h, jnp.sum(W, axis=1, keepdims=True) - tdec, 0.0)
                dcs_t = dcs_t - jnp.where(rowsT == h, jnp.sum(W, axis=0, keepdims=True), 0.0)
                rows_h = (row < SSD_HEAD_DIM) if hh == 0 else (row >= SSD_HEAD_DIM)
                sprev = jnp.sum(jnp.sum(jnp.where(rows_h, dS_f[p] * states[p], 0.0), axis=1, keepdims=True),
                                axis=0, keepdims=True)
                ecl = jnp.exp(s["cs"][Q - 1:Q, h:h + 1])
                dcl = dcl + jnp.where(lane1 == h, jnp.sum(tdec, axis=0, keepdims=True) + ecl * sprev, 0.0)
            dCB_b = dCB.astype(BF16)
            dC_out.append(dCs[g][0] + dCs[g][1] + _dot(dCB_b, Bs[g], NN))
            dB_out.append(dBacc + _dot(dCB_b, Cs[g], TN))
        for p in range(N_PAIR):
            dS[p] = dS_f[p] * k["ecl_rows"][p] + dprev[p]
        dxc_ref[...] = jnp.concatenate([dxs] + dB_out + dC_out, axis=1).astype(dxc_ref.dtype)

        dcs = dcs + _dot01(s["eye"], dcs_t, NT, "a") + jnp.where(row == Q - 1, dcl, 0.0)
        da = _dot01(row <= col, dcs, NN, "a")
        ddt = ddt + da * s["A"]
        dpre = jnp.where(s["valid"], ddt * _sigmoid(s["pre"]), 0.0)
        ddtr_ref[...] = dpre
        dbias = jnp.sum(dpre, axis=0, keepdims=True)
        dalog = jnp.sum(da * s["dt"], axis=0, keepdims=True) * s["A"]
        first_step = jnp.logical_and(b == 0, t == 0)

        @pl.when(first_step)
        def _():
            dbias_ref[...] = dbias
            dalog_ref[...] = dalog
            dd_ref[...] = dD
            dnw_ref[...] = dnw

        @pl.when(jnp.logical_not(first_step))
        def _():
            dbias_ref[...] += dbias
            dalog_ref[...] += dalog
            dd_ref[...] += dD
            dnw_ref[...] += dnw

    rb = lambda b, c: b * nc + nc - 1 - c
    rowblk = lambda w: pl.BlockSpec((Q, w), lambda b, c: (rb(b, c), 0))
    vec = lambda w: pl.BlockSpec((1, w), lambda b, c: (0, 0))
    return _call(
        body, name="ssd_bwd", grid=(Bl, nc),
        in_specs=_ssd_in_specs(nc, rev=True) + [
            pl.BlockSpec((1, 1, N_PAIR, 128, 128), lambda b, c: (b, nc - 1 - c, 0, 0, 0)), rowblk(SSD_INNER)],
        out_specs=[rowblk(SSD_CONV_CH), rowblk(SSD_INNER), rowblk(128), vec(128), vec(128), vec(128), vec(SSD_INNER)],
        out_shape=[jax.ShapeDtypeStruct((M, SSD_CONV_CH), BF16), jax.ShapeDtypeStruct((M, SSD_INNER), BF16),
                   jax.ShapeDtypeStruct((M, 128), F32), jax.ShapeDtypeStruct((1, 128), F32),
                   jax.ShapeDtypeStruct((1, 128), F32), jax.ShapeDtypeStruct((1, 128), F32),
                   jax.ShapeDtypeStruct((1, SSD_INNER), F32)],
        scratch=[pltpu.VMEM((N_PAIR, 128, 128), F32)], sem=("arbitrary", "arbitrary"),
        args=(xc, dtr, proj, bias_p, alog_p, d_p, nw, prev, dya), comm=comm)


NSUB = Q // HG_CHUNK
HG_HP = 8
EXP_CAP = 80.0


def _hg_setup(blk, q_ref, f_ref, hb_ref):
    row = lax.broadcasted_iota(jnp.int32, (Q, Q), 0)
    col = lax.broadcasted_iota(jnp.int32, (Q, Q), 1)
    same = (row // HG_CHUNK) == (col // HG_CHUNK)
    causal = jnp.logical_and(same, col <= row)
    lb = _sigmoid(hb_ref[0:1, :] - hb_ref[1:2, :])
    fl = f_ref[...].astype(F32)
    sg = _sigmoid(fl)
    fg = lb + (1.0 - lb) * sg
    k = (1.0 - lb) * (1.0 - sg)
    gl = jnp.log(fg)
    G = _dot01(causal, gl, NN, "a")
    T = _dot01(same, gl, NN, "a")
    qv = q_ref[...].astype(F32)
    sq = _sigmoid(qv)
    eG = jnp.exp(G)
    eGn = jnp.exp(jnp.minimum(-G, EXP_CAP))
    eTG = jnp.exp(T - G)
    qt = qv * sq * eG
    kt = k * eGn
    kh = k * eTG
    valid = jnp.logical_or(blk > 0, row[:, :1] >= PAD)
    return dict(row=row, col=col, same=same, causal=causal, lb=lb, sg=sg, fg=fg, k=k, T=T, qv=qv, sq=sq,
                eG=eG, eGn=eGn, eTG=eTG, qt=qt, kt=kt, kh=kh, valid=valid)


def _hg_specs(nb, rev=False):
    rb = (lambda h, b, t: b * nb + nb - 1 - t) if rev else (lambda h, b, t: b * nb + t)
    w = 128 * HG_HP
    blk = lambda off: pl.BlockSpec((Q, w), lambda h, b, t, off=off: (rb(h, b, t), off // HG_HP + h))
    return [blk(24), blk(32), blk(40), blk(48),
            pl.BlockSpec((2, w), lambda h, b, t: (0, h)), pl.BlockSpec((1, w), lambda h, b, t: (0, h))]


HEAD_LANES = tuple(slice(128 * hh, 128 * (hh + 1)) for hh in range(HG_HP))


def _per_head(fn, *arrs):
    return jnp.concatenate([jnp.broadcast_to(fn(*(a[:, ln] for a in arrs)), (arrs[0].shape[0], 128))
                            for ln in HEAD_LANES], axis=1)


def _hgrn_fwd(proj, hb, nw, Bl, nb, comm=None):
    M = proj.shape[0]

    def body(q_ref, f_ref, i_ref, g_ref, hb_ref, nw_ref, y_ref, o_ref, st_ref, S):
        blk = pl.program_id(2)

        @pl.when(blk == 0)
        def _():
            S[...] = jnp.zeros_like(S)

        s = _hg_setup(blk, q_ref, f_ref, hb_ref)
        v = i_ref[...]
        qt_b, kt_b, kh_b = s["qt"].astype(BF16), s["kt"].astype(BF16), s["kh"].astype(BF16)
        eT = jnp.exp(s["T"])
        att = [jnp.where(s["causal"], _dot(qt_b[:, ln], kt_b[:, ln], NT), 0.0).astype(BF16) for ln in HEAD_LANES]
        o_intra = [_dot(att[hh], v[:, ln], NN) for hh, ln in enumerate(HEAD_LANES)]
        for j in range(NSUB):
            sl = slice(HG_CHUNK * j, HG_CHUNK * (j + 1))
            for hh, ln in enumerate(HEAD_LANES):
                St = S[hh]
                st_ref[0, hh, 0, j] = St
                o_ref[sl, ln] = o_intra[hh][sl] + _dot(qt_b[sl, ln], St.astype(BF16), NT)
                S[hh] = St * eT[HG_CHUNK * j:HG_CHUNK * j + 1, ln] + _dot(v[sl, ln], kh_b[sl, ln], TN)
        o = o_ref[...]
        r = _per_head(lambda a: lax.rsqrt(jnp.mean(a * a, axis=-1, keepdims=True) + EPS), o)
        gv = g_ref[...].astype(F32)
        y_ref[...] = (o * r * nw_ref[...] * gv * _sigmoid(gv)).astype(y_ref.dtype)

    rowblk = pl.BlockSpec((Q, 128 * HG_HP), lambda h, b, t: (b * nb + t, h))
    return _call(
        body, name="hgrn_fwd", grid=(HG_HEADS // HG_HP, Bl, nb), in_specs=_hg_specs(nb),
        out_specs=[rowblk, rowblk,
                   pl.BlockSpec((1, HG_HP, 1, NSUB, 128, 128), lambda h, b, t: (b, h, t, 0, 0, 0))],
        out_shape=[jax.ShapeDtypeStruct((M, HG_WIDTH), BF16), jax.ShapeDtypeStruct((M, HG_WIDTH), F32),
                   jax.ShapeDtypeStruct((Bl, HG_HEADS, nb, NSUB, 128, 128), F32)],
        scratch=[pltpu.VMEM((HG_HP, 128, 128), F32)], sem=("parallel", "arbitrary", "arbitrary"),
        args=(proj, proj, proj, proj, hb, nw), comm=comm)


def _hgrn_bwd(proj, hb, nw, o_saved, st_saved, dyb, Bl, nb, comm=None):
    M = proj.shape[0]

    def body(q_ref, f_ref, i_ref, g_ref, hb_ref, nw_ref, o_ref, st_ref, dy_ref,
             dq_ref, df_ref, di_ref, dg_ref, dhb_ref, dnw_ref, dS, a_dqt, a_dv, a_dkh, a_dgl):
        b, t = pl.program_id(1), pl.program_id(2)

        @pl.when(t == 0)
        def _():
            dS[...] = jnp.zeros_like(dS)

        first_step = jnp.logical_and(b == 0, t == 0)
        s = _hg_setup(nb - 1 - t, q_ref, f_ref, hb_ref)
        v = i_ref[...]
        qt_b, kt_b, kh_b = s["qt"].astype(BF16), s["kt"].astype(BF16), s["kh"].astype(BF16)
        eT = jnp.exp(s["T"])
        att = [jnp.where(s["causal"], _dot(qt_b[:, ln], kt_b[:, ln], NT), 0.0).astype(BF16) for ln in HEAD_LANES]

        o = o_ref[...]
        r = _per_head(lambda a: lax.rsqrt(jnp.mean(a * a, axis=-1, keepdims=True) + EPS), o)
        xhat = o * r
        gv = g_ref[...].astype(F32)
        sgv = _sigmoid(gv)
        dyv = dy_ref[...].astype(F32)
        d_on = dyv * gv * sgv
        dg_out = dyv * xhat * nw_ref[...] * _dsilu(gv, sgv)
        gw = d_on * nw_ref[...]
        do = r * (gw - xhat * _per_head(lambda a, c: jnp.mean(a * c, axis=-1, keepdims=True), gw, xhat))
        dnw = jnp.sum(d_on * xhat, axis=0, keepdims=True)
        do_b = do.astype(BF16)

        datt = [jnp.where(s["causal"], _dot(do_b[:, ln], v[:, ln], NT), 0.0).astype(BF16) for ln in HEAD_LANES]
        dqt = jnp.concatenate([_dot(datt[hh], kt_b[:, ln], NN) for hh, ln in enumerate(HEAD_LANES)], axis=1)
        dkt = jnp.concatenate([_dot(datt[hh], qt_b[:, ln], TN) for hh, ln in enumerate(HEAD_LANES)], axis=1)
        dv = jnp.concatenate([_dot(att[hh], do_b[:, ln], TN) for hh, ln in enumerate(HEAD_LANES)], axis=1)
        last_row = (lax.broadcasted_iota(jnp.int32, (HG_CHUNK, 128), 0) == HG_CHUNK - 1)
        for j in reversed(range(NSUB)):
            sl = slice(HG_CHUNK * j, HG_CHUNK * (j + 1))
            for hh, ln in enumerate(HEAD_LANES):
                St = st_ref[0, hh, 0, j]
                dSt = dS[hh]
                St_b, dSt_b = St.astype(BF16), dSt.astype(BF16)
                eT_j = eT[HG_CHUNK * j:HG_CHUNK * j + 1, ln]
                dkh_j = _dot(v[sl, ln], dSt_b, NN)
                a_dqt[sl, ln] = _dot(do_b[sl, ln], St_b, NN)
                a_dv[sl, ln] = _dot(kh_b[sl, ln], dSt_b, NT)
                a_dkh[sl, ln] = dkh_j
                dlast = (jnp.sum(St * dSt, axis=0, keepdims=True) * eT_j
                         + jnp.sum(dkh_j * s["kh"][sl, ln], axis=0, keepdims=True))
                a_dgl[sl, ln] = jnp.where(last_row, dlast, 0.0)
                dS[hh] = dSt * eT_j + _dot(do_b[sl, ln], qt_b[sl, ln], TN)
        dqt = dqt + a_dqt[...]
        dv = dv + a_dv[...]
        dkh = a_dkh[...]
        dG = dqt * s["qt"] - dkt * s["kt"] - dkh * s["kh"] + a_dgl[...]
        rev_causal = jnp.logical_and(s["same"], s["col"] >= s["row"])
        dgl = _dot01(rev_causal, dG, NN, "a")
        dk = dkt * s["eGn"] + dkh * s["eTG"]
        dfg = dgl / s["fg"] - dk
        lb, sg = s["lb"], s["sg"]
        keep = s["valid"].astype(F32)
        df_ref[...] = (dfg * (1.0 - lb) * sg * (1.0 - sg) * keep).astype(df_ref.dtype)
        dq_ref[...] = (dqt * s["eG"] * _dsilu(s["qv"], s["sq"]) * keep).astype(dq_ref.dtype)
        di_ref[...] = (dv * keep).astype(di_ref.dtype)
        dg_ref[...] = (dg_out * keep).astype(dg_ref.dtype)
        dlb = jnp.sum(dfg * (1.0 - sg) * keep, axis=0, keepdims=True) * lb * (1.0 - lb)
        dhb = jnp.concatenate([dlb, -dlb], axis=0)

        @pl.when(first_step)
        def _():
            dhb_ref[...] = dhb
            dnw_ref[...] = dnw

        @pl.when(jnp.logical_not(first_step))
        def _():
            dhb_ref[...] += dhb
            dnw_ref[...] += dnw

    w = 128 * HG_HP
    rowblk = pl.BlockSpec((Q, w), lambda h, b, t: (b * nb + nb - 1 - t, h))
    return _call(
        body, name="hgrn_bwd", grid=(HG_HEADS // HG_HP, Bl, nb),
        in_specs=_hg_specs(nb, rev=True) + [
            rowblk, pl.BlockSpec((1, HG_HP, 1, NSUB, 128, 128), lambda h, b, t: (b, h, nb - 1 - t, 0, 0, 0)), rowblk],
        out_specs=[rowblk, rowblk, rowblk, rowblk,
                   pl.BlockSpec((2, w), lambda h, b, t: (0, h)), pl.BlockSpec((1, w), lambda h, b, t: (0, h))],
        out_shape=[jax.ShapeDtypeStruct((M, HG_WIDTH), BF16)] * 4 + [
            jax.ShapeDtypeStruct((2, HG_WIDTH), F32), jax.ShapeDtypeStruct((1, HG_WIDTH), F32)],
        scratch=[pltpu.VMEM((HG_HP, 128, 128), F32)] + [pltpu.VMEM((Q, w), F32)] * 4,
        sem=("parallel", "arbitrary", "arbitrary"),
        args=(proj, proj, proj, proj, hb, nw, o_saved, st_saved, dyb), comm=comm)


def _adamw(name, parts, w, m, v):
    R, C = w.shape
    S = parts.shape[0]
    tr, tc = (_tile(R, (256, 176, 128, 64, 8)), C) if R % 8 == 0 else (R, 256)
    c1, c2 = 1.0 - ADAM_B1 ** ADAM_STEP, 1.0 - ADAM_B2 ** ADAM_STEP

    def body(p_ref, w_ref, m_ref, v_ref, g_ref, d_ref, nm_ref, nv_ref):
        g = p_ref[0].astype(F32)
        for s in range(1, S):
            g = g + p_ref[s].astype(F32)
        nm = ADAM_B1 * m_ref[...] + (1.0 - ADAM_B1) * g
        nv = ADAM_B2 * v_ref[...] + (1.0 - ADAM_B2) * (g * g)
        g_ref[...] = g
        nm_ref[...] = nm
        nv_ref[...] = nv
        d_ref[...] = -ADAM_LR * ((nm / c1) / (jnp.sqrt(nv / c2) + ADAM_EPS) + ADAM_WD * w_ref[...])

    blk = pl.BlockSpec((tr, tc), lambda i, j: (i, j))
    return pl.pallas_call(
        body, name=name, grid=(R // tr, C // tc),
        in_specs=[pl.BlockSpec((S, tr, tc), lambda i, j: (0, i, j)), blk, blk, blk], out_specs=[blk] * 4,
        out_shape=[jax.ShapeDtypeStruct((R, C), F32)] * 4, compiler_params=_params(("parallel", "parallel")),
    )(parts, w, m, v)


def _sum_parts(name, parts):
    S, R, C = parts.shape

    def body(p_ref, o_ref):
        g = p_ref[0]
        for s in range(1, S):
            g = g + p_ref[s]
        o_ref[...] = g

    return pl.pallas_call(
        body, name=name, out_shape=jax.ShapeDtypeStruct((R, C), F32),
        in_specs=[pl.BlockSpec(memory_space=pltpu.VMEM)], out_specs=pl.BlockSpec(memory_space=pltpu.VMEM),
    )(parts)


def _heads_to_lanes(p):
    return jnp.pad(p, [(0, 0)] * (p.ndim - 1) + [(0, 128 - SSD_HEADS)])


def _lanes_to_heads(p):
    return p[..., :SSD_HEADS]


def _pack_rows(arrs):
    rows = []
    for a in arrs:
        f = a.reshape(-1).astype(F32)
        n = -(-f.shape[0] // D_MODEL) * D_MODEL
        rows.append(jnp.pad(f, (0, n - f.shape[0])).reshape(-1, D_MODEL))
    out = jnp.concatenate(rows, axis=0)
    return jnp.pad(out, ((0, (-out.shape[0]) % 8), (0, 0)))


def _unpack_rows(packed, like):
    outs, r = [], 0
    for a in like:
        n = 1
        for s in a.shape:
            n *= s
        nr = -(-n // D_MODEL)
        outs.append(packed[r:r + nr].reshape(-1)[:n].reshape(a.shape))
        r += nr
    return outs


def _cols(gth):
    return jnp.transpose(gth, (1, 0, 2)).reshape(gth.shape[1], -1)


def _rows(gth):
    return gth.reshape(-1, gth.shape[2])


def _to_rows(g):
    return g.reshape(N_DEV, -1, g.shape[1]).astype(BF16)


def _ffn_fwd_gu(tag, h, norm_w, w_gu_t, comm=None):
    M = h.shape[0]
    F = w_gu_t.shape[0] // 2
    tm = _tile(M, (544, 256))
    n = _rmsnorm_fwd(tag + "_norm", h, norm_w)
    tn = _tile(F, (1408, 704, 256))
    outs = _fused_matmul(
        tag + "_gu", M, F, D_MODEL,
        [dict(a=n, b=w_gu_t, trans_b=True, acc=0), dict(a=n, b=w_gu_t, trans_b=True, bn_off=F // tn, acc=1)], [],
        lambda accs, ex: (accs[0], accs[1], accs[0] * _sigmoid(accs[0]) * accs[1]),
        [BF16, BF16, BF16], 2, tm, tn, D_MODEL, outer="j", comm=comm)
    return (n, *outs[:3]), outs[3:]


def _ffn_fwd_down(tag, h, a, w_down):
    M = h.shape[0]
    F = w_down.shape[0]
    (h_out,) = _fused_matmul(
        tag + "_down", M, D_MODEL, F, [dict(a=a, b=w_down, acc=0)], [(h, 0)],
        lambda accs, ex: (ex[0] + 0.5 * accs[0],), [F32], 1, _tile(M, (544, 256)), D_MODEL, F, outer="j")
    return h_out


def _ffn_bwd(tag, dh, dh_b, h, norm_w, w_gu_t, w_down, saved, scatter=False):
    n, g, u, a = saved
    M = h.shape[0]
    F = w_down.shape[0]
    tm = _tile(M, (544, 256))
    tn = _tile(F, (1408, 704, 256))

    def swiglu_bwd(accs, ex):
        da, gv, uv = 0.5 * accs[0], ex[0].astype(F32), ex[1].astype(F32)
        s = _sigmoid(gv)
        return da * uv * _dsilu(gv, s), da * gv * s

    dg, du = _fused_matmul(
        tag + "_dact", M, F, D_MODEL, [dict(a=dh_b, b=w_down, trans_b=True, acc=0)], [(g, 0), (u, 0)],
        swiglu_bwd, [BF16, BF16], 1, tm, tn, D_MODEL, outer="j")
    (dw_down,) = _matmul_tn(tag + "_dwd", a, dh_b, tn, D_MODEL, tm, scale=0.5)
    dw_g, *p_down = _matmul_tn(tag + "_dwg", dg, n, tn, D_MODEL, tm,
                               comm=("scatter", [_to_rows(dw_down)]) if scatter else None)
    (dw_u,) = _matmul_tn(tag + "_dwu", du, n, tn, D_MODEL, tm)
    dw_gu_t = jnp.concatenate([dw_g, dw_u], axis=0)
    dn, *p_gu = _fused_matmul(
        tag + "_dn", M, D_MODEL, F,
        [dict(a=dg, b=w_gu_t, acc=0), dict(a=du, b=w_gu_t, bk_off=1, acc=0)], [],
        lambda accs, ex: (accs[0],), [F32], 1, tm, D_MODEL, F, outer="i",
        comm=("scatter", [_to_rows(dw_gu_t)]) if scatter else None)
    dh_prev, dh_prev_b, dnorm = _rmsnorm_bwd(tag + "_dnorm", dn, h, norm_w, dh)
    return (dh_prev, dh_prev_b, dnorm, *((p_gu[0], p_down[0]) if scatter else (dw_gu_t, dw_down)))


def kernel(x, meta_tokens, ffn1_norm, ffn1_w_gu, ffn1_w_down, mix_norm, w_in, ssd_conv_w, ssd_conv_b, ssd_dt_bias, ssd_a_log, ssd_d, ssd_norm, hg_lower_bound, hg_norm, w_branch_a, w_branch_b, w_out, ffn2_norm, ffn2_w_gu, ffn2_w_down, final_norm, loss_target, m_meta_tokens, m_ffn1_norm, m_ffn1_w_gu, m_ffn1_w_down, m_mix_norm, m_w_in, m_ssd_conv_w, m_ssd_conv_b, m_ssd_dt_bias, m_ssd_a_log, m_ssd_d, m_ssd_norm, m_hg_lower_bound, m_hg_norm, m_w_branch_a, m_w_branch_b, m_w_out, m_ffn2_norm, m_ffn2_w_gu, m_ffn2_w_down, m_final_norm, v_meta_tokens, v_ffn1_norm, v_ffn1_w_gu, v_ffn1_w_down, v_mix_norm, v_w_in, v_ssd_conv_w, v_ssd_conv_b, v_ssd_dt_bias, v_ssd_a_log, v_ssd_d, v_ssd_norm, v_hg_lower_bound, v_hg_norm, v_w_branch_a, v_w_branch_b, v_w_out, v_ffn2_norm, v_ffn2_w_gu, v_ffn2_w_down, v_final_norm):
    Bl, S, D = x.shape
    T = PAD + N_META + S
    nc = T // Q
    M = Bl * T
    me = 4 * lax.axis_index("x") + 2 * lax.axis_index("y") + lax.axis_index("c")

    bf = lambda a: a[0].astype(BF16)
    bft = lambda a: a[0].T.astype(BF16)
    g_wgu1, g_meta, g_conv_w = _exchange("gather_first", "gather", [bft(ffn1_w_gu), meta_tokens, ssd_conv_w[0]])
    wgu1, meta_full, conv_w_full = _rows(g_wgu1), _cols(g_meta), _cols(g_conv_w)
    bias_p, alog_p, d_p = _heads_to_lanes(ssd_dt_bias), _heads_to_lanes(ssd_a_log), _heads_to_lanes(ssd_d)
    final_w = final_norm.reshape(1, D)

    h0 = jnp.concatenate([jnp.zeros((Bl, PAD, D), F32), jnp.broadcast_to(meta_full[None], (Bl, N_META, D)), x],
                         axis=1).reshape(M, D)
    tm = _tile(M, (544, 256))
    ffn1_saved, (g_wd1, g_win) = _ffn_fwd_gu("ffn1", h0, ffn1_norm, wgu1, comm=("gather", [bf(ffn1_w_down), bft(w_in)]))
    wd1, win_t = _rows(g_wd1), _rows(g_win)
    win_main = jnp.concatenate([win_t[:3072], win_t[3088:]], axis=0)
    win_dt = jnp.pad(win_t[3072:3088], ((0, 128 - SSD_HEADS), (0, 0)))
    h1 = _ffn_fwd_down("ffn1", h0, ffn1_saved[3], wd1)
    un = _rmsnorm_fwd("mix_norm", h1, mix_norm)
    plain = lambda accs, ex: (accs[0],)
    proj, g_wa, g_wb, g_wo = _fused_matmul(
        "in_proj", M, N_MAIN, D, [dict(a=un, b=win_main, trans_b=True, acc=0)], [], plain, [BF16], 1, tm, 1536, D,
        outer="j", comm=("gather", [bf(w_branch_a), bf(w_branch_b), bf(w_out)]))
    wa, wb, wo = _rows(g_wa), _rows(g_wb), _rows(g_wo)
    (dtr,) = _fused_matmul("in_proj_dt", M, 128, D, [dict(a=un, b=win_dt, trans_b=True, acc=0)], [], plain, [F32], 1,
                           tm, 128, D, outer="j")
    xc = _conv_fwd(proj, conv_w_full, ssd_conv_b, Bl, T)
    ya, ssd_prev = _ssd_fwd(xc, dtr, proj, bias_p, alog_p, d_p, ssd_norm, Bl, nc)
    yb, hg_o, hg_st, g_wgu2, g_wd2 = _hgrn_fwd(proj, hg_lower_bound, hg_norm, Bl, nc,
                                               comm=("gather", [bft(ffn2_w_gu), bf(ffn2_w_down)]))
    wgu2, wd2 = _rows(g_wgu2), _rows(g_wd2)

    def branch_fwd(accs, ex):
        pa, pb = accs
        return pa, pb, _sigmoid(ex[0].astype(F32)) * pa + _sigmoid(ex[1].astype(F32)) * pb

    pa, pb, merged = _fused_matmul(
        "branches", M, D, D, [dict(a=ya, b=wa, acc=0), dict(a=yb, b=wb, acc=1)], [(proj, 7), (proj, 8)],
        branch_fwd, [BF16, BF16, BF16], 2, tm, D, D, outer="j")
    (h2,) = _fused_matmul("out_proj", M, D, D, [dict(a=merged, b=wo, acc=0)], [(h1, 0)],
                          lambda accs, ex: (ex[0] + accs[0],), [F32], 1, tm, D, D, outer="j")
    ffn2_saved, _ = _ffn_fwd_gu("ffn2", h2, ffn2_norm, wgu2)
    h3 = _ffn_fwd_down("ffn2", h2, ffn2_saved[3], wd2)

    dh3, dh3_b, d_final, loss_part = _loss_head(h3, final_w, loss_target, Bl, nc)
    dh2, dh2_b, d_ffn2_norm, d_wgu2, d_wd2 = _ffn_bwd("ffn2", dh3, dh3_b, h2, ffn2_norm, wgu2, wd2, ffn2_saved)

    def branch_bwd(accs, ex):
        dm = accs[0]
        ga, gb, pav, pbv = (e.astype(F32) for e in ex)
        sa, sb = _sigmoid(ga), _sigmoid(gb)
        return dm * sa, dm * sb, dm * pav * sa * (1.0 - sa), dm * pbv * sb * (1.0 - sb)

    dpa, dpb, dga, dgb = _fused_matmul(
        "d_merged", M, D, D, [dict(a=dh2_b, b=wo, trans_b=True, acc=0)], [(proj, 7), (proj, 8), (pa, 0), (pb, 0)],
        branch_bwd, [BF16] * 4, 1, tm, D, D, outer="j")
    (d_wo,) = _matmul_tn("d_w_out", merged, dh2_b, D, D, tm)
    (d_wa,) = _matmul_tn("d_w_a", ya, dpa, D, D, tm)
    (d_wb,) = _matmul_tn("d_w_b", yb, dpb, D, D, tm)
    dya, dyb = _fused_matmul(
        "d_branches", M, D, D, [dict(a=dpa, b=wa, trans_b=True, acc=0), dict(a=dpb, b=wb, trans_b=True, acc=1)], [],
        lambda accs, ex: (accs[0], accs[1]), [BF16, BF16], 2, tm, D, D, outer="j")
    *ssd_grads, p_wa, p_wb, p_wo, p_wgu2, p_wd2 = _ssd_bwd(
        xc, dtr, proj, bias_p, alog_p, d_p, ssd_norm, ssd_prev, dya, Bl, nc,
        comm=("scatter", [_to_rows(d_wa), _to_rows(d_wb), _to_rows(d_wo), _to_rows(d_wgu2), _to_rows(d_wd2)]))
    dxc, dz, ddtr, d_bias_p, d_alog_p, d_d_p, d_ssd_norm = ssd_grads
    dxbc, d_conv_w, d_conv_b = _conv_bwd(proj, conv_w_full, ssd_conv_b, dxc, Bl, T)
    dq, df, di, dg, d_hb, d_hg_norm = _hgrn_bwd(proj, hg_lower_bound, hg_norm, hg_o, hg_st, dyb, Bl, nc)
    dproj = jnp.concatenate([dz, dxbc, dq, df, di, dg, dga, dgb], axis=1)
    ddtr_b = ddtr.astype(BF16)
    (d_win_main,) = _matmul_tn("d_w_in", dproj, un, 1536, D, tm)
    (d_win_dt,) = _matmul_tn("d_w_in_dt", ddtr_b, un, 128, D, tm)
    d_win_t = jnp.concatenate([d_win_main[:3072], d_win_dt[:SSD_HEADS], d_win_main[3072:]], axis=0)
    (dun_dt,) = _fused_matmul("d_un_dt", M, D, 128, [dict(a=ddtr_b, b=win_dt, acc=0)], [], plain, [F32], 1,
                              tm, D, 128, outer="j")
    dun, p_win = _fused_matmul("d_un", M, D, N_MAIN, [dict(a=dproj, b=win_main, acc=0)], [(dun_dt, 0)],
                               lambda accs, ex: (accs[0] + ex[0],), [F32], 1, tm, D, 3072, outer="i",
                               comm=("scatter", [_to_rows(d_win_t)]))
    dh1, dh1_b, d_mix_norm = _rmsnorm_bwd("d_mix_norm", dun, h1, mix_norm, dh2)
    dh0, _, d_ffn1_norm, p_wgu1, p_wd1 = _ffn_bwd("ffn1", dh1, dh1_b, h0, ffn1_norm, wgu1, wd1, ffn1_saved, scatter=True)

    dh0 = dh0.reshape(Bl, T, D)
    grad_x = dh0[:, PAD + N_META:]
    d_meta = dh0[:, PAD:PAD + N_META]

    small_grads = [d_ffn1_norm, d_mix_norm, d_conv_b, _lanes_to_heads(d_bias_p), _lanes_to_heads(d_alog_p),
                   _lanes_to_heads(d_d_p), d_ssd_norm, d_hb, d_hg_norm, d_ffn2_norm, d_final.reshape(D), d_conv_w]
    small_packed = _pack_rows(small_grads + [d_meta[b] for b in range(Bl)])
    parts = [p_wgu1, p_wd1, p_win, p_wa, p_wb, p_wo, p_wgu2, p_wd2]
    (small_all,) = _exchange("gather_small_grads", "gather", [small_packed])
    small_sum = _sum_parts("sum_small_grads", small_all)
    unpacked = _unpack_rows(small_sum, small_grads + [d_meta[b] for b in range(Bl)])
    g_small = unpacked[:len(small_grads)]
    g_meta_full = unpacked[len(small_grads)]
    for b in range(1, Bl):
        g_meta_full = g_meta_full + unpacked[len(small_grads) + b]
    g_meta = lax.dynamic_slice_in_dim(g_meta_full, me * (D // N_DEV), D // N_DEV, axis=1)
    g_conv_w = lax.dynamic_slice_in_dim(g_small[11], me * (SSD_CONV_CH // N_DEV), SSD_CONV_CH // N_DEV, axis=1)

    names = ["meta_tokens", "ffn1_norm", "ffn1_w_gu", "ffn1_w_down", "mix_norm", "w_in", "ssd_conv_w", "ssd_conv_b",
             "ssd_dt_bias", "ssd_a_log", "ssd_d", "ssd_norm", "hg_lower_bound", "hg_norm", "w_branch_a", "w_branch_b",
             "w_out", "ffn2_norm", "ffn2_w_gu", "ffn2_w_down", "final_norm"]
    W = dict(meta_tokens=meta_tokens, ffn1_norm=ffn1_norm, ffn1_w_gu=ffn1_w_gu, ffn1_w_down=ffn1_w_down, mix_norm=mix_norm,
             w_in=w_in, ssd_conv_w=ssd_conv_w, ssd_conv_b=ssd_conv_b, ssd_dt_bias=ssd_dt_bias, ssd_a_log=ssd_a_log,
             ssd_d=ssd_d, ssd_norm=ssd_norm, hg_lower_bound=hg_lower_bound, hg_norm=hg_norm, w_branch_a=w_branch_a,
             w_branch_b=w_branch_b, w_out=w_out, ffn2_norm=ffn2_norm, ffn2_w_gu=ffn2_w_gu, ffn2_w_down=ffn2_w_down,
             final_norm=final_norm)
    Mo = dict(meta_tokens=m_meta_tokens, ffn1_norm=m_ffn1_norm, ffn1_w_gu=m_ffn1_w_gu, ffn1_w_down=m_ffn1_w_down,
              mix_norm=m_mix_norm, w_in=m_w_in, ssd_conv_w=m_ssd_conv_w, ssd_conv_b=m_ssd_conv_b, ssd_dt_bias=m_ssd_dt_bias,
              ssd_a_log=m_ssd_a_log, ssd_d=m_ssd_d, ssd_norm=m_ssd_norm, hg_lower_bound=m_hg_lower_bound, hg_norm=m_hg_norm,
              w_branch_a=m_w_branch_a, w_branch_b=m_w_branch_b, w_out=m_w_out, ffn2_norm=m_ffn2_norm, ffn2_w_gu=m_ffn2_w_gu,
              ffn2_w_down=m_ffn2_w_down, final_norm=m_final_norm)
    Vo = dict(meta_tokens=v_meta_tokens, ffn1_norm=v_ffn1_norm, ffn1_w_gu=v_ffn1_w_gu, ffn1_w_down=v_ffn1_w_down,
              mix_norm=v_mix_norm, w_in=v_w_in, ssd_conv_w=v_ssd_conv_w, ssd_conv_b=v_ssd_conv_b, ssd_dt_bias=v_ssd_dt_bias,
              ssd_a_log=v_ssd_a_log, ssd_d=v_ssd_d, ssd_norm=v_ssd_norm, hg_lower_bound=v_hg_lower_bound, hg_norm=v_hg_norm,
              w_branch_a=v_w_branch_a, w_branch_b=v_w_branch_b, w_out=v_w_out, ffn2_norm=v_ffn2_norm, ffn2_w_gu=v_ffn2_w_gu,
              ffn2_w_down=v_ffn2_w_down, final_norm=v_final_norm)
    grads, deltas, new_m, new_v = {}, {}, {}, {}
    big_names = ["ffn1_w_gu", "ffn1_w_down", "w_in", "w_branch_a", "w_branch_b", "w_out", "ffn2_w_gu", "ffn2_w_down"]
    transposed = ("ffn1_w_gu", "ffn2_w_gu", "w_in")
    for nm, part in zip(big_names, parts):
        view = (lambda a: a[0].T) if nm in transposed else (lambda a: a[0])
        back = (lambda o: o.T[None]) if nm in transposed else (lambda o: o[None])
        outs = _adamw("adamw_" + nm, part, view(W[nm]), view(Mo[nm]), view(Vo[nm]))
        grads[nm], deltas[nm], new_m[nm], new_v[nm] = (back(o) for o in outs)
    small_names = ["ffn1_norm", "mix_norm", "ssd_conv_b", "ssd_dt_bias", "ssd_a_log", "ssd_d", "ssd_norm", "hg_lower_bound",
                   "hg_norm", "ffn2_norm", "final_norm", "ssd_conv_w", "meta_tokens"]
    small_g = g_small[:11] + [g_conv_w.reshape(ssd_conv_w.shape), g_meta]
    pk = lambda d: _pack_rows([d[nm] for nm in small_names])
    outs = _adamw("adamw_small", _pack_rows(small_g)[None], pk(W), pk(Mo), pk(Vo))
    like = [W[nm] for nm in small_names]
    for dst, o in zip((grads, deltas, new_m, new_v), outs):
        for nm, val in zip(small_names, _unpack_rows(o, like)):
            dst[nm] = val

    loss = lax.psum(loss_part[0, 0], MESH_AXES)
    return (loss, grad_x, *[grads[nm] for nm in names], *[deltas[nm] for nm in names],
            *[new_m[nm] for nm in names], *[new_v[nm] for nm in names])
```

```python
import functools

import jax
import jax.numpy as jnp
from jax import lax
from jax.experimental import pallas as pl
from jax.experimental.pallas import tpu as pltpu

F32, BF16 = jnp.float32, jnp.bfloat16
NN, NT, TN = ((1,), (0,)), ((1,), (1,)), ((0,), (0,))
MESH_AXES = ("x", "y", "c")
N_DEV = 8

D_MODEL = 1024
N_META = 16
EPS = 1e-6
SSD_HEADS, SSD_HEAD_DIM, SSD_GROUPS, SSD_STATE, SSD_CONV, Q = 16, 64, 4, 128, 4, 128
SSD_INNER = SSD_HEADS * SSD_HEAD_DIM
SSD_CONV_CH = SSD_INNER + 2 * SSD_GROUPS * SSD_STATE
HG_WIDTH, HG_HEADS, HG_CHUNK = 1024, 8, 16
PAD = Q - N_META
N_MAIN = 9 * 1024
ADAM_LR, ADAM_B1, ADAM_B2, ADAM_EPS, ADAM_WD, ADAM_STEP = 0.001, 0.9, 0.999, 1e-08, 0.01, 10
VMEM_LIMIT = 52 * 1024 * 1024


def _dot(a, b, dims, prec=None):
    return lax.dot_general(a, b, (dims, ((), ())), precision=prec, preferred_element_type=F32)


def _dot01(a, b, dims, sel):
    x = b if sel == "a" else a
    hi = x.astype(BF16)
    r1 = x - hi.astype(F32)
    mid = r1.astype(BF16)
    lo = (r1 - mid.astype(F32)).astype(BF16)
    s = (a if sel == "a" else b).astype(BF16)
    parts = [_dot(s, p, dims) if sel == "a" else _dot(p, s, dims) for p in (hi, mid, lo)]
    return parts[0] + parts[1] + parts[2]


def _sigmoid(x):
    return 1.0 / (1.0 + jnp.exp(-x))


def _dsilu(x, s):
    return s * (1.0 + x * (1.0 - s))


def _softplus(x):
    e = jnp.exp(-jnp.abs(x))
    u = 1.0 + e
    log1p_e = jnp.where(u == 1.0, e, jnp.log(u) * e / (u - 1.0))
    return jnp.maximum(x, 0.0) + log1p_e


def _params(sem):
    return pltpu.CompilerParams(dimension_semantics=sem, vmem_limit_bytes=VMEM_LIMIT)


def _tile(n, prefs):
    for p in prefs:
        if n % p == 0:
            return p
    return n


CHIP_FLIPS = ((1, 0), (0, 1), (1, 1))
N_PEER = N_DEV - 1


def _comm_gather(srcs, outs, send_sems, recv_sems, local_sems):
    n = len(srcs)
    x, y, c = (lax.axis_index(a) for a in MESH_AXES)
    dev = lambda px, py, pc: 4 * px + 2 * py + pc
    me, sib = dev(x, y, c), (x, y, 1 - c)

    def rc(w, k, slot, to, src=None):
        return pltpu.make_async_remote_copy(
            src_ref=outs[w].at[slot] if src is None else src, dst_ref=outs[w].at[slot],
            send_sem=send_sems.at[w, k], recv_sem=recv_sems.at[w, k], device_id=to, device_id_type=pl.DeviceIdType.MESH)

    def local(w):
        return pltpu.make_async_copy(srcs[w], outs[w].at[me], local_sems.at[w])

    def start():
        for w in range(n):
            local(w).start()
            rc(w, 0, me, sib, src=srcs[w]).start()
            for j, (fx, fy) in enumerate(CHIP_FLIPS):
                rc(w, 1 + j, me, (x ^ fx, y ^ fy, c), src=srcs[w]).start()

    def finish():
        for w in range(n):
            for j, (fx, fy) in enumerate(CHIP_FLIPS):
                slot = dev(x ^ fx, y ^ fy, c)
                rc(w, 1 + j, slot, sib).wait_recv()
                rc(w, 4 + j, slot, sib).start()
        for w in range(n):
            rc(w, 0, dev(x, y, 1 - c), sib).wait_recv()
            rc(w, 0, me, sib, src=srcs[w]).wait_send()
            for j, (fx, fy) in enumerate(CHIP_FLIPS):
                rc(w, 4 + j, dev(x ^ fx, y ^ fy, 1 - c), sib).wait_recv()
                rc(w, 1 + j, me, sib, src=srcs[w]).wait_send()
                rc(w, 4 + j, dev(x ^ fx, y ^ fy, c), sib).wait_send()
            local(w).wait()

    return start, finish


def _comm_scatter(srcs, outs, send_sems, recv_sems, local_sems):
    n = len(srcs)
    x, y, c = (lax.axis_index(a) for a in MESH_AXES)
    me = 4 * x + 2 * y + c

    def copies():
        out = []
        for w in range(n):
            out.append(pltpu.make_async_copy(srcs[w].at[me], outs[w].at[me], local_sems.at[w]))
            for k in range(1, N_DEV):
                px, py, pc = x ^ (k >> 2), y ^ ((k >> 1) & 1), c ^ (k & 1)
                out.append(pltpu.make_async_remote_copy(
                    src_ref=srcs[w].at[4 * px + 2 * py + pc], dst_ref=outs[w].at[me],
                    send_sem=send_sems.at[w, k - 1], recv_sem=recv_sems.at[w, k - 1],
                    device_id=(px, py, pc), device_id_type=pl.DeviceIdType.MESH))
        return out

    def start():
        for cp in copies():
            cp.start()

    def finish():
        for cp in copies():
            cp.wait()

    return start, finish


def _comm_swap(srcs, outs, send_sems, recv_sems, local_sems):
    x, y, c = (lax.axis_index(a) for a in MESH_AXES)

    def copies():
        return [pltpu.make_async_remote_copy(
            src_ref=srcs[w], dst_ref=outs[w], send_sem=send_sems.at[w, 0], recv_sem=recv_sems.at[w, 0],
            device_id=(x, y, 1 - c), device_id_type=pl.DeviceIdType.MESH) for w in range(len(srcs))]

    def start():
        for cp in copies():
            cp.start()

    def finish():
        for cp in copies():
            cp.wait()

    return start, finish


def _comm_chips(srcs, outs, send_sems, recv_sems, local_sems):
    n = len(srcs)
    x, y, c = (lax.axis_index(a) for a in MESH_AXES)
    mine = 2 * x + y

    def copies():
        out = []
        for w in range(n):
            out.append(pltpu.make_async_copy(srcs[w].at[mine], outs[w].at[mine], local_sems.at[w]))
            for j, (fx, fy) in enumerate(CHIP_FLIPS):
                px, py = x ^ fx, y ^ fy
                out.append(pltpu.make_async_remote_copy(
                    src_ref=srcs[w].at[2 * px + py], dst_ref=outs[w].at[mine],
                    send_sem=send_sems.at[w, j], recv_sem=recv_sems.at[w, j],
                    device_id=(px, py, c), device_id_type=pl.DeviceIdType.MESH))
        return out

    def start():
        for cp in copies():
            cp.start()

    def finish():
        for cp in copies():
            cp.wait()

    return start, finish


def _comm_parts(comm):
    kind, arrays = comm
    n = len(arrays)
    lead = {"gather": lambda a: (N_DEV,) + a.shape, "scatter": lambda a: (N_DEV,) + a.shape[1:],
            "swap": lambda a: a.shape, "chips": lambda a: a.shape}[kind]
    shapes = [jax.ShapeDtypeStruct(lead(a), a.dtype) for a in arrays]
    sems = [pltpu.SemaphoreType.DMA((n, N_PEER)), pltpu.SemaphoreType.DMA((n, N_PEER)), pltpu.SemaphoreType.DMA((n,))]
    make = {"gather": _comm_gather, "scatter": _comm_scatter, "swap": _comm_swap, "chips": _comm_chips}[kind]
    return n, shapes, sems, make


def _exchange(name, kind, arrays):
    n, shapes, sems, make = _comm_parts((kind, arrays))

    def body(*refs):
        start, finish = make(refs[:n], refs[n:2 * n], *refs[2 * n:])
        start()
        finish()

    any_spec = pl.BlockSpec(memory_space=pl.ANY)
    return pl.pallas_call(
        body, name=name, in_specs=[any_spec] * n, out_specs=[any_spec] * n, out_shape=shapes, scratch_shapes=sems,
        compiler_params=pltpu.CompilerParams(has_side_effects=True),
    )(*arrays)


def _call(body, *, name, grid, in_specs, out_specs, out_shape, scratch, sem, args, comm=None):
    if comm is None:
        return pl.pallas_call(body, name=name, grid=grid, in_specs=in_specs, out_specs=out_specs, out_shape=out_shape,
                              scratch_shapes=scratch, compiler_params=_params(sem))(*args)
    n, shapes, sems, make = _comm_parts(comm)
    n_in, n_out, n_scr = len(in_specs), len(out_specs), len(scratch)

    def carrier(*refs):
        ins, csrc = refs[:n_in], refs[n_in:n_in + n]
        outs, cout = refs[n_in + n:n_in + n + n_out], refs[n_in + n + n_out:n_in + 2 * n + n_out]
        rest = refs[n_in + 2 * n + n_out:]
        start, finish = make(csrc, cout, *rest[n_scr:])
        ids = [pl.program_id(a) for a in range(len(grid))]
        first = functools.reduce(jnp.logical_and, [i == 0 for i in ids])
        last = functools.reduce(jnp.logical_and, [i == g - 1 for i, g in zip(ids, grid)])
        pl.when(first)(start)
        body(*ins, *outs, *rest[:n_scr])
        pl.when(last)(finish)

    any_spec = pl.BlockSpec(memory_space=pl.ANY)
    return pl.pallas_call(
        carrier, name=name, grid=grid, in_specs=list(in_specs) + [any_spec] * n,
        out_specs=list(out_specs) + [any_spec] * n, out_shape=list(out_shape) + shapes,
        scratch_shapes=list(scratch) + sems,
        compiler_params=pltpu.CompilerParams(dimension_semantics=("arbitrary",) * len(grid),
                                             vmem_limit_bytes=VMEM_LIMIT, has_side_effects=True),
    )(*args, *comm[1])


def _fused_matmul(name, M, N, K, pairs, extras, epilogue, out_dtypes, n_acc, tm, tn, tk, outer="i", comm=None):
    nk = K // tk
    n_pairs, n_ex, n_out = len(pairs), len(extras), len(out_dtypes)

    def ij(g0, g1):
        return (g0, g1) if outer == "i" else (g1, g0)

    in_specs, args = [], []
    for p in pairs:
        ao, bk, bn = p.get("a_off", 0), p.get("bk_off", 0), p.get("bn_off", 0)
        in_specs.append(pl.BlockSpec((tm, tk), lambda g0, g1, k, ao=ao: (ij(g0, g1)[0], k + ao)))
        if p.get("trans_b"):
            in_specs.append(pl.BlockSpec((tn, tk), lambda g0, g1, k, bk=bk, bn=bn: (ij(g0, g1)[1] + bn, k + bk)))
        else:
            in_specs.append(pl.BlockSpec((tk, tn), lambda g0, g1, k, bk=bk, bn=bn: (k + bk, ij(g0, g1)[1] + bn)))
        args += [p["a"], p["b"]]
    for arr, off in extras:
        in_specs.append(pl.BlockSpec((tm, tn), lambda g0, g1, k, off=off: (ij(g0, g1)[0], ij(g0, g1)[1] + off)))
        args.append(arr)
    out_specs = [pl.BlockSpec((tm, tn), lambda g0, g1, k: ij(g0, g1)) for _ in out_dtypes]
    out_shape = [jax.ShapeDtypeStruct((M, N), dt) for dt in out_dtypes]
    grid = (M // tm, N // tn, nk) if outer == "i" else (N // tn, M // tm, nk)

    def partials(refs):
        accs = [None] * n_acc
        for idx, p in enumerate(pairs):
            d = _dot(refs[2 * idx][...], refs[2 * idx + 1][...], NT if p.get("trans_b") else NN)
            accs[p["acc"]] = d if accs[p["acc"]] is None else accs[p["acc"]] + d
        return accs

    def finish(accs, refs):
        ex = [r[...] for r in refs[2 * n_pairs:2 * n_pairs + n_ex]]
        outs = refs[2 * n_pairs + n_ex:2 * n_pairs + n_ex + n_out]
        for o, r in zip(outs, epilogue(accs, ex)):
            o[...] = r.astype(o.dtype)

    if nk == 1:
        def body(*refs):
            finish(partials(refs), refs)
        scratch = []
    else:
        def body(*refs):
            acc_refs = refs[-n_acc:]
            k = pl.program_id(2)
            new = partials(refs)

            @pl.when(k == 0)
            def _():
                for a, v in zip(acc_refs, new):
                    a[...] = v

            @pl.when(k > 0)
            def _():
                for a, v in zip(acc_refs, new):
                    a[...] += v

            @pl.when(k == nk - 1)
            def _():
                finish([a[...] for a in acc_refs], refs)
        scratch = [pltpu.VMEM((tm, tn), F32) for _ in range(n_acc)]

    return _call(body, name=name, grid=grid, in_specs=in_specs, out_specs=out_specs, out_shape=out_shape,
                 scratch=scratch, sem=("parallel", "parallel", "arbitrary"), args=args, comm=comm)


def _matmul_tn(name, x, y, t1, t2, tr, scale=1.0, comm=None):
    R, K1 = x.shape
    N1 = y.shape[1]
    nr = R // tr

    def body(x_ref, y_ref, o_ref):
        r = pl.program_id(2)
        d = _dot(x_ref[...], y_ref[...], TN)

        @pl.when(r == 0)
        def _():
            o_ref[...] = d

        @pl.when(r > 0)
        def _():
            o_ref[...] += d

        if scale != 1.0:
            @pl.when(r == nr - 1)
            def _():
                o_ref[...] = o_ref[...] * scale

    return _call(
        body, name=name, grid=(K1 // t1, N1 // t2, nr),
        in_specs=[pl.BlockSpec((tr, t1), lambda i, j, r: (r, i)), pl.BlockSpec((tr, t2), lambda i, j, r: (r, j))],
        out_specs=[pl.BlockSpec((t1, t2), lambda i, j, r: (i, j))],
        out_shape=[jax.ShapeDtypeStruct((K1, N1), F32)], scratch=[],
        sem=("parallel", "parallel", "arbitrary"), args=(x, y), comm=comm)


def _rmsnorm_fwd(name, h, w):
    M, D = h.shape
    tm = _tile(M, (544, 256, 128))

    def body(h_ref, w_ref, o_ref):
        x = h_ref[...]
        r = lax.rsqrt(jnp.mean(x * x, axis=-1, keepdims=True) + EPS)
        o_ref[...] = (x * r * w_ref[...]).astype(o_ref.dtype)

    return pl.pallas_call(
        body, name=name, grid=(M // tm,),
        in_specs=[pl.BlockSpec((tm, D), lambda i: (i, 0)), pl.BlockSpec((1, D), lambda i: (0, 0))],
        out_specs=pl.BlockSpec((tm, D), lambda i: (i, 0)),
        out_shape=jax.ShapeDtypeStruct((M, D), BF16), compiler_params=_params(("parallel",)),
    )(h, w)


def _rmsnorm_bwd(name, dn, h, w, dh_in):
    M, D = h.shape
    tm = _tile(M, (544, 256, 128))

    def body(dn_ref, h_ref, w_ref, dhi_ref, dh_ref, dhb_ref, dw_ref):
        x = h_ref[...]
        r = lax.rsqrt(jnp.mean(x * x, axis=-1, keepdims=True) + EPS)
        xhat = x * r
        dn_v = dn_ref[...]
        gw = dn_v * w_ref[...]
        dx = r * (gw - xhat * jnp.mean(gw * xhat, axis=-1, keepdims=True))
        dh = dhi_ref[...] + dx
        dh_ref[...] = dh
        dhb_ref[...] = dh.astype(BF16)
        dw = jnp.sum(dn_v * xhat, axis=0, keepdims=True)

        @pl.when(pl.program_id(0) == 0)
        def _():
            dw_ref[...] = dw

        @pl.when(pl.program_id(0) > 0)
        def _():
            dw_ref[...] += dw

    row = pl.BlockSpec((tm, D), lambda i: (i, 0))
    vec = pl.BlockSpec((1, D), lambda i: (0, 0))
    return pl.pallas_call(
        body, name=name, grid=(M // tm,), in_specs=[row, row, vec, row], out_specs=[row, row, vec],
        out_shape=[jax.ShapeDtypeStruct((M, D), F32), jax.ShapeDtypeStruct((M, D), BF16), jax.ShapeDtypeStruct((1, D), F32)],
        compiler_params=_params(("arbitrary",)),
    )(dn, h, w, dh_in)


def _loss_head(h, w, target, Bl, nb):
    M, D = h.shape

    def body(h_ref, w_ref, t_ref, dh_ref, dhb_ref, dw_ref, loss_ref):
        b, t = pl.program_id(0), pl.program_id(1)
        live = (t > 0).astype(F32)
        x = h_ref[...]
        r = lax.rsqrt(jnp.mean(x * x, axis=-1, keepdims=True) + EPS)
        xhat = x * r
        wv = w_ref[...]
        err = (xhat * wv - t_ref[0]) * live
        dy = err * (1.0 / D)
        gw = dy * wv
        dx = r * (gw - xhat * jnp.mean(gw * xhat, axis=-1, keepdims=True))
        dh_ref[...] = dx
        dhb_ref[...] = dx.astype(BF16)
        dw = jnp.sum(dy * xhat, axis=0, keepdims=True)
        part = 0.5 * jnp.sum(jnp.sum(err * err, axis=-1, keepdims=True) * (1.0 / D), axis=0, keepdims=True)
        first = jnp.logical_and(b == 0, t == 0)

        @pl.when(first)
        def _():
            dw_ref[...] = dw
            loss_ref[...] = jnp.broadcast_to(part, loss_ref.shape)

        @pl.when(jnp.logical_not(first))
        def _():
            dw_ref[...] += dw
            loss_ref[...] += jnp.broadcast_to(part, loss_ref.shape)

    row = pl.BlockSpec((Q, D), lambda b, t: (b * nb + t, 0))
    vec = pl.BlockSpec((1, D), lambda b, t: (0, 0))
    return pl.pallas_call(
        body, name="loss_head", grid=(Bl, nb),
        in_specs=[row, vec, pl.BlockSpec((1, Q, D), lambda b, t: (b, jnp.maximum(t - 1, 0), 0))],
        out_specs=[row, row, vec, pl.BlockSpec((8, 128), lambda b, t: (0, 0))],
        out_shape=[jax.ShapeDtypeStruct((M, D), F32), jax.ShapeDtypeStruct((M, D), BF16),
                   jax.ShapeDtypeStruct((1, D), F32), jax.ShapeDtypeStruct((8, 128), F32)],
        compiler_params=_params(("arbitrary", "arbitrary")),
    )(h, w, target)


CONV_TC = 256


def _conv_pre(xr_ref, w_ref, b_ref):
    x = xr_ref[...].astype(F32)
    acc = b_ref[...] + w_ref[SSD_CONV - 1:SSD_CONV, :] * x
    for k in range(1, SSD_CONV):
        acc = acc + w_ref[SSD_CONV - 1 - k:SSD_CONV - k, :] * pltpu.roll(x, k, 0)
    return x, acc


def _conv_fwd(proj, w, b, Bl, T):
    M = proj.shape[0]
    off = 1024 // CONV_TC

    def body(xr_ref, w_ref, b_ref, o_ref):
        _, acc = _conv_pre(xr_ref, w_ref, b_ref)
        row = lax.broadcasted_iota(jnp.int32, acc.shape, 0)
        o_ref[...] = jnp.where(row >= PAD, acc * _sigmoid(acc), 0.0).astype(o_ref.dtype)

    return pl.pallas_call(
        body, name="conv_fwd", grid=(Bl, SSD_CONV_CH // CONV_TC),
        in_specs=[pl.BlockSpec((T, CONV_TC), lambda bb, j: (bb, j + off)),
                  pl.BlockSpec((SSD_CONV, CONV_TC), lambda bb, j: (0, j)), pl.BlockSpec((1, CONV_TC), lambda bb, j: (0, j))],
        out_specs=pl.BlockSpec((T, CONV_TC), lambda bb, j: (bb, j)),
        out_shape=jax.ShapeDtypeStruct((M, SSD_CONV_CH), BF16), compiler_params=_params(("parallel", "parallel")),
    )(proj, w, b)


def _conv_bwd(proj, w, b, dxc, Bl, T):
    M = proj.shape[0]
    off = 1024 // CONV_TC

    def body(xr_ref, w_ref, b_ref, d_ref, dx_ref, dw_ref, db_ref):
        x, acc = _conv_pre(xr_ref, w_ref, b_ref)
        row = lax.broadcasted_iota(jnp.int32, acc.shape, 0)
        s = _sigmoid(acc)
        dpre = jnp.where(row >= PAD, d_ref[...].astype(F32) * _dsilu(acc, s), 0.0)
        dx = w_ref[SSD_CONV - 1:SSD_CONV, :] * dpre
        dws = [jnp.sum(dpre * x, axis=0, keepdims=True)]
        for k in range(1, SSD_CONV):
            dx = dx + w_ref[SSD_CONV - 1 - k:SSD_CONV - k, :] * pltpu.roll(dpre, T - k, 0)
            dws.append(jnp.sum(dpre * pltpu.roll(x, k, 0), axis=0, keepdims=True))
        dx_ref[...] = dx.astype(dx_ref.dtype)
        dw = jnp.concatenate(dws[::-1], axis=0)
        db = jnp.sum(dpre, axis=0, keepdims=True)

        @pl.when(pl.program_id(1) == 0)
        def _():
            dw_ref[...] = dw
            db_ref[...] = db

        @pl.when(pl.program_id(1) > 0)
        def _():
            dw_ref[...] += dw
            db_ref[...] += db

    return pl.pallas_call(
        body, name="conv_bwd", grid=(SSD_CONV_CH // CONV_TC, Bl),
        in_specs=[pl.BlockSpec((T, CONV_TC), lambda j, bb: (bb, j + off)),
                  pl.BlockSpec((SSD_CONV, CONV_TC), lambda j, bb: (0, j)), pl.BlockSpec((1, CONV_TC), lambda j, bb: (0, j)),
                  pl.BlockSpec((T, CONV_TC), lambda j, bb: (bb, j))],
        out_specs=[pl.BlockSpec((T, CONV_TC), lambda j, bb: (bb, j)),
                   pl.BlockSpec((SSD_CONV, CONV_TC), lambda j, bb: (0, j)), pl.BlockSpec((1, CONV_TC), lambda j, bb: (0, j))],
        out_shape=[jax.ShapeDtypeStruct((M, SSD_CONV_CH), BF16), jax.ShapeDtypeStruct((SSD_CONV, SSD_CONV_CH), F32),
                   jax.ShapeDtypeStruct((1, SSD_CONV_CH), F32)],
        compiler_params=_params(("parallel", "arbitrary")),
    )(proj, w, b, dxc)


N_PAIR = SSD_HEADS // 2
HPG = SSD_HEADS // SSD_GROUPS
GW = SSD_INNER // SSD_GROUPS


def _per_group(fn, *arrs):
    return jnp.concatenate([jnp.broadcast_to(fn(*(a[:, GW * g:GW * (g + 1)] for a in arrs)), (arrs[0].shape[0], GW))
                            for g in range(SSD_GROUPS)], axis=1)


def _ssd_prep(c, dtr_ref, bias_ref, alog_ref, d_ref):
    row = lax.broadcasted_iota(jnp.int32, (Q, 128), 0)
    col = lax.broadcasted_iota(jnp.int32, (Q, 128), 1)
    live = col < SSD_HEADS
    valid = jnp.logical_and(jnp.logical_or(c > 0, row >= PAD), live)
    pre = dtr_ref[...] + bias_ref[...]
    dt = jnp.where(valid, _softplus(pre), 0.0)
    A = jnp.where(live[0:1], -jnp.exp(alog_ref[...]), 0.0)
    tri = row >= col
    eye = (row == col).astype(BF16)
    cs = _dot01(tri, dt * A, NN, "a")
    cst = _dot01(eye, cs, NT, "a")
    spread = (lax.broadcasted_iota(jnp.int32, (128, SSD_INNER), 0)
              == lax.broadcasted_iota(jnp.int32, (128, SSD_INNER), 1) // SSD_HEAD_DIM).astype(BF16)
    dt_w = _dot01(dt, spread, NN, "b")
    cs_w = _dot01(cs, spread, NN, "b")
    d_w = _dot01(jnp.broadcast_to(d_ref[...], (8, 128)), spread, NN, "b")[0:1]
    lane = lax.broadcasted_iota(jnp.int32, (Q, SSD_INNER), 1)
    first = (lane % 128) < SSD_HEAD_DIM
    return dict(row=row, col=col, valid=valid, pre=pre, dt=dt, A=A, tri=tri, eye=eye, cs=cs, cst=cst, spread=spread,
                dt_w=dt_w, cs_w=cs_w, d_w=d_w, ecs_w=jnp.exp(cs_w), decay_w=jnp.exp(cs_w[Q - 1:Q] - cs_w), first=first)


def _ssd_chunk(xc_ref, s, states):
    xv = xc_ref[:, 0:SSD_INNER].astype(F32)
    Bs = [xc_ref[:, SSD_INNER + 128 * g:SSD_INNER + 128 * (g + 1)] for g in range(SSD_GROUPS)]
    Cs = [xc_ref[:, SSD_INNER + 512 + 128 * g:SSD_INNER + 512 + 128 * (g + 1)] for g in range(SSD_GROUPS)]
    X = xv * s["dt_w"]
    X0 = jnp.where(s["first"], X, 0.0)
    Xb = (X0.astype(BF16), (X - X0).astype(BF16))
    Xd = (X * s["decay_w"]).astype(BF16)
    CB = [_dot(Cs[g], Bs[g], NT) for g in range(SSD_GROUPS)]
    Lms = [jnp.exp(jnp.where(s["tri"], s["cs"][:, h:h + 1] - s["cst"][h:h + 1, :], -jnp.inf)) for h in range(SSD_HEADS)]
    Ms = [CB[h // HPG] * Lms[h] for h in range(SSD_HEADS)]
    Mb = [m.astype(BF16) for m in Ms]
    prev_b = [st.astype(BF16) for st in states]
    yds, yos, sts = [], [], []
    for p in range(N_PAIR):
        g, ln = p // 2, slice(128 * p, 128 * (p + 1))
        yds.append(_dot(Mb[2 * p], Xb[0][:, ln], NN) + _dot(Mb[2 * p + 1], Xb[1][:, ln], NN))
        yos.append(_dot(Cs[g], prev_b[p], NT))
        sts.append(_dot(Xd[:, ln], Bs[g], TN))
    yo = jnp.concatenate(yos, axis=1)
    y = jnp.concatenate(yds, axis=1) + yo * s["ecs_w"] + xv * s["d_w"]
    upper = s["row"] < SSD_HEAD_DIM
    cl = s["cs"][Q - 1:Q, :]
    ecl_rows = [jnp.where(upper, jnp.exp(cl[:, 2 * p:2 * p + 1]), jnp.exp(cl[:, 2 * p + 1:2 * p + 2])) for p in range(N_PAIR)]
    new_states = [states[p] * ecl_rows[p] + sts[p] for p in range(N_PAIR)]
    return y, new_states, dict(xv=xv, Bs=Bs, Cs=Cs, X=X, Xb=Xb, CB=CB, Lms=Lms, Ms=Ms, Mb=Mb, prev_b=prev_b, yo=yo,
                               ecl_rows=ecl_rows)


def _ssd_in_specs(nc, rev=False):
    rb = (lambda b, c: b * nc + nc - 1 - c) if rev else (lambda b, c: b * nc + c)
    vec = pl.BlockSpec((1, 128), lambda b, c: (0, 0))
    return [pl.BlockSpec((Q, SSD_CONV_CH), lambda b, c: (rb(b, c), 0)),
            pl.BlockSpec((Q, 128), lambda b, c: (rb(b, c), 0)),
            pl.BlockSpec((Q, SSD_INNER), lambda b, c: (rb(b, c), 0)),
            vec, vec, vec, pl.BlockSpec((1, SSD_INNER), lambda b, c: (0, 0))]


def _ssd_fwd(xc, dtr, proj, bias_p, alog_p, d_p, nw, Bl, nc):
    M = xc.shape[0]

    def body(xc_ref, dtr_ref, z_ref, bias_ref, alog_ref, d_ref, nw_ref, y_ref, prev_ref, state):
        c = pl.program_id(1)

        @pl.when(c == 0)
        def _():
            state[...] = jnp.zeros_like(state)

        s = _ssd_prep(c, dtr_ref, bias_ref, alog_ref, d_ref)
        states = [state[p] for p in range(N_PAIR)]
        y, new_states, _ = _ssd_chunk(xc_ref, s, states)
        for p in range(N_PAIR):
            prev_ref[0, 0, p] = states[p]
            state[p] = new_states[p]
        zz = z_ref[...].astype(F32)
        yg = y * zz * _sigmoid(zz)
        r = _per_group(lambda a: lax.rsqrt(jnp.mean(a * a, axis=-1, keepdims=True) + EPS), yg)
        y_ref[...] = (yg * r * nw_ref[...]).astype(y_ref.dtype)

    return pl.pallas_call(
        body, name="ssd_fwd", grid=(Bl, nc), in_specs=_ssd_in_specs(nc),
        out_specs=[pl.BlockSpec((Q, SSD_INNER), lambda b, c: (b * nc + c, 0)),
                   pl.BlockSpec((1, 1, N_PAIR, 128, 128), lambda b, c: (b, c, 0, 0, 0))],
        out_shape=[jax.ShapeDtypeStruct((M, SSD_INNER), BF16), jax.ShapeDtypeStruct((Bl, nc, N_PAIR, 128, 128), F32)],
        scratch_shapes=[pltpu.VMEM((N_PAIR, 128, 128), F32)],
        compiler_params=_params(("arbitrary", "arbitrary")),
    )(xc, dtr, proj, bias_p, alog_p, d_p, nw)


def _ssd_bwd(xc, dtr, proj, bias_p, alog_p, d_p, nw, prev, dya, Bl, nc, comm=None):
    M = xc.shape[0]

    def body(xc_ref, dtr_ref, z_ref, bias_ref, alog_ref, d_ref, nw_ref, prev_ref, dy_ref,
             dxc_ref, dz_ref, ddtr_ref, dbias_ref, dalog_ref, dd_ref, dnw_ref, dS):
        b, t = pl.program_id(0), pl.program_id(1)

        @pl.when(t == 0)
        def _():
            dS[...] = jnp.zeros_like(dS)

        s = _ssd_prep(nc - 1 - t, dtr_ref, bias_ref, alog_ref, d_ref)
        states = [prev_ref[0, 0, p] for p in range(N_PAIR)]
        y, _, k = _ssd_chunk(xc_ref, s, states)
        xv, Bs, Cs, Xb = k["xv"], k["Bs"], k["Cs"], k["Xb"]

        zz = z_ref[...].astype(F32)
        sz = _sigmoid(zz)
        silu_z = zz * sz
        yg = y * silu_z
        r = _per_group(lambda a: lax.rsqrt(jnp.mean(a * a, axis=-1, keepdims=True) + EPS), yg)
        xhat = yg * r
        dout = dy_ref[...].astype(F32)
        gw = dout * nw_ref[...]
        dyg = r * (gw - xhat * _per_group(lambda a, c2: jnp.mean(a * c2, axis=-1, keepdims=True), gw, xhat))
        dnw = jnp.sum(dout * xhat, axis=0, keepdims=True)
        dz_ref[...] = (dyg * y * _dsilu(zz, sz)).astype(dz_ref.dtype)
        dy = dyg * silu_z
        dy0 = jnp.where(s["first"], dy, 0.0)
        dyb = (dy0.astype(BF16), (dy - dy0).astype(BF16))
        dYo = (dy * s["ecs_w"]).astype(BF16)

        dS_f = [dS[p] for p in range(N_PAIR)]
        dS_b = [d.astype(BF16) for d in dS_f]
        BdS, dXm, dprev, dCs, dMs, XdS = [], [], [], [[] for _ in range(SSD_GROUPS)], [], []
        for p in range(N_PAIR):
            g, ln = p // 2, slice(128 * p, 128 * (p + 1))
            BdS.append(_dot(Bs[g], dS_b[p], NT))
            dXm.append(_dot(k["Mb"][2 * p], dyb[0][:, ln], TN) + _dot(k["Mb"][2 * p + 1], dyb[1][:, ln], TN))
            dprev.append(_dot(dYo[:, ln], Cs[g], TN))
            dCs[g].append(_dot(dYo[:, ln], k["prev_b"][p], NN))
            for hh in range(2):
                dMs.append(_dot(dyb[hh][:, ln], Xb[hh][:, ln], NT))
                XdS.append(_dot(Xb[hh][:, ln], dS_b[p], NN))
        dX = jnp.concatenate(dXm, axis=1) + s["decay_w"] * jnp.concatenate(BdS, axis=1)
        dxs = dy * s["d_w"] + dX * s["dt_w"]

        heads = lambda a: _dot01(a, s["spread"], NT, "b")
        ddt = heads(dX * xv)
        dcs = heads(dy * k["yo"] * s["ecs_w"])
        dD = jnp.sum(heads(dy * xv), axis=0, keepdims=True)

        col, row = s["col"], s["row"]
        lane1 = col[0:1]
        rowsT = lax.broadcasted_iota(jnp.int32, (128, Q), 0)
        dcs_t = jnp.zeros((128, Q), F32)
        dcl = jnp.zeros((1, 128), F32)
        dB_out, dC_out = [], []
        for g in range(SSD_GROUPS):
            Bf = Bs[g].astype(F32)
            dCB = jnp.zeros((Q, Q), F32)
            dBacc = jnp.zeros((Q, 128), F32)
            for r4 in range(HPG):
                h = HPG * g + r4
                p, hh = h // 2, h % 2
                W = dMs[h] * k["Ms"][h]
                dCB = dCB + dMs[h] * k["Lms"][h]
                decay_h = s["decay_w"][:, SSD_HEAD_DIM * h:SSD_HEAD_DIM * h + 1]
                dBacc = dBacc + decay_h * XdS[h]
                tdec = jnp.sum(XdS[h] * Bf, axis=1, keepdims=True) * decay_h
                dcs = dcs + jnp.where(col == h, jnp.sum(W, axis=1, keepdims=True) - tdec, 0.0)
                dcs_t = dcs_t - jnp.where(rowsT == h, jnp.sum(W, axis=0, keepdims=True), 0.0)
                rows_h = (row < SSD_HEAD_DIM) if hh == 0 else (row >= SSD_HEAD_DIM)
                sprev = jnp.sum(jnp.sum(jnp.where(rows_h, dS_f[p] * states[p], 0.0), axis=1, keepdims=True),
                                axis=0, keepdims=True)
                ecl = jnp.exp(s["cs"][Q - 1:Q, h:h + 1])
                dcl = dcl + jnp.where(lane1 == h, jnp.sum(tdec, axis=0, keepdims=True) + ecl * sprev, 0.0)
            dCB_b = dCB.astype(BF16)
            dC_out.append(dCs[g][0] + dCs[g][1] + _dot(dCB_b, Bs[g], NN))
            dB_out.append(dBacc + _dot(dCB_b, Cs[g], TN))
        for p in range(N_PAIR):
            dS[p] = dS_f[p] * k["ecl_rows"][p] + dprev[p]
        dxc_ref[...] = jnp.concatenate([dxs] + dB_out + dC_out, axis=1).astype(dxc_ref.dtype)

        dcs = dcs + _dot01(s["eye"], dcs_t, NT, "a") + jnp.where(row == Q - 1, dcl, 0.0)
        da = _dot01(row <= col, dcs, NN, "a")
        ddt = ddt + da * s["A"]
        dpre = jnp.where(s["valid"], ddt * _sigmoid(s["pre"]), 0.0)
        ddtr_ref[...] = dpre
        dbias = jnp.sum(dpre, axis=0, keepdims=True)
        dalog = jnp.sum(da * s["dt"], axis=0, keepdims=True) * s["A"]
        first_step = jnp.logical_and(b == 0, t == 0)

        @pl.when(first_step)
        def _():
            dbias_ref[...] = dbias
            dalog_ref[...] = dalog
            dd_ref[...] = dD
            dnw_ref[...] = dnw

        @pl.when(jnp.logical_not(first_step))
        def _():
            dbias_ref[...] += dbias
            dalog_ref[...] += dalog
            dd_ref[...] += dD
            dnw_ref[...] += dnw

    rb = lambda b, c: b * nc + nc - 1 - c
    rowblk = lambda w: pl.BlockSpec((Q, w), lambda b, c: (rb(b, c), 0))
    vec = lambda w: pl.BlockSpec((1, w), lambda b, c: (0, 0))
    return _call(
        body, name="ssd_bwd", grid=(Bl, nc),
        in_specs=_ssd_in_specs(nc, rev=True) + [
            pl.BlockSpec((1, 1, N_PAIR, 128, 128), lambda b, c: (b, nc - 1 - c, 0, 0, 0)), rowblk(SSD_INNER)],
        out_specs=[rowblk(SSD_CONV_CH), rowblk(SSD_INNER), rowblk(128), vec(128), vec(128), vec(128), vec(SSD_INNER)],
        out_shape=[jax.ShapeDtypeStruct((M, SSD_CONV_CH), BF16), jax.ShapeDtypeStruct((M, SSD_INNER), BF16),
                   jax.ShapeDtypeStruct((M, 128), F32), jax.ShapeDtypeStruct((1, 128), F32),
                   jax.ShapeDtypeStruct((1, 128), F32), jax.ShapeDtypeStruct((1, 128), F32),
                   jax.ShapeDtypeStruct((1, SSD_INNER), F32)],
        scratch=[pltpu.VMEM((N_PAIR, 128, 128), F32)], sem=("arbitrary", "arbitrary"),
        args=(xc, dtr, proj, bias_p, alog_p, d_p, nw, prev, dya), comm=comm)


NSUB = Q // HG_CHUNK
HG_HP = 8
EXP_CAP = 80.0


def _hg_setup(blk, q_ref, f_ref, hb_ref):
    row = lax.broadcasted_iota(jnp.int32, (Q, Q), 0)
    col = lax.broadcasted_iota(jnp.int32, (Q, Q), 1)
    same = (row // HG_CHUNK) == (col // HG_CHUNK)
    causal = jnp.logical_and(same, col <= row)
    lb = _sigmoid(hb_ref[0:1, :] - hb_ref[1:2, :])
    fl = f_ref[...].astype(F32)
    sg = _sigmoid(fl)
    fg = lb + (1.0 - lb) * sg
    k = (1.0 - lb) * (1.0 - sg)
    gl = jnp.log(fg)
    G = _dot01(causal, gl, NN, "a")
    T = _dot01(same, gl, NN, "a")
    qv = q_ref[...].astype(F32)
    sq = _sigmoid(qv)
    eG = jnp.exp(G)
    eGn = jnp.exp(jnp.minimum(-G, EXP_CAP))
    eTG = jnp.exp(T - G)
    qt = qv * sq * eG
    kt = k * eGn
    kh = k * eTG
    valid = jnp.logical_or(blk > 0, row[:, :1] >= PAD)
    return dict(row=row, col=col, same=same, causal=causal, lb=lb, sg=sg, fg=fg, k=k, T=T, qv=qv, sq=sq,
                eG=eG, eGn=eGn, eTG=eTG, qt=qt, kt=kt, kh=kh, valid=valid)


def _hg_specs(nb, rev=False):
    rb = (lambda h, b, t: b * nb + nb - 1 - t) if rev else (lambda h, b, t: b * nb + t)
    w = 128 * HG_HP
    blk = lambda off: pl.BlockSpec((Q, w), lambda h, b, t, off=off: (rb(h, b, t), off // HG_HP + h))
    return [blk(24), blk(32), blk(40), blk(48),
            pl.BlockSpec((2, w), lambda h, b, t: (0, h)), pl.BlockSpec((1, w), lambda h, b, t: (0, h))]


HEAD_LANES = tuple(slice(128 * hh, 128 * (hh + 1)) for hh in range(HG_HP))


def _per_head(fn, *arrs):
    return jnp.concatenate([jnp.broadcast_to(fn(*(a[:, ln] for a in arrs)), (arrs[0].shape[0], 128))
                            for ln in HEAD_LANES], axis=1)


def _hgrn_fwd(proj, hb, nw, Bl, nb, comm=None):
    M = proj.shape[0]

    def body(q_ref, f_ref, i_ref, g_ref, hb_ref, nw_ref, y_ref, o_ref, st_ref, S):
        blk = pl.program_id(2)

        @pl.when(blk == 0)
        def _():
            S[...] = jnp.zeros_like(S)

        s = _hg_setup(blk, q_ref, f_ref, hb_ref)
        v = i_ref[...]
        qt_b, kt_b, kh_b = s["qt"].astype(BF16), s["kt"].astype(BF16), s["kh"].astype(BF16)
        eT = jnp.exp(s["T"])
        att = [jnp.where(s["causal"], _dot(qt_b[:, ln], kt_b[:, ln], NT), 0.0).astype(BF16) for ln in HEAD_LANES]
        o_intra = [_dot(att[hh], v[:, ln], NN) for hh, ln in enumerate(HEAD_LANES)]
        for j in range(NSUB):
            sl = slice(HG_CHUNK * j, HG_CHUNK * (j + 1))
            for hh, ln in enumerate(HEAD_LANES):
                St = S[hh]
                st_ref[0, hh, 0, j] = St
                o_ref[sl, ln] = o_intra[hh][sl] + _dot(qt_b[sl, ln], St.astype(BF16), NT)
                S[hh] = St * eT[HG_CHUNK * j:HG_CHUNK * j + 1, ln] + _dot(v[sl, ln], kh_b[sl, ln], TN)
        o = o_ref[...]
        r = _per_head(lambda a: lax.rsqrt(jnp.mean(a * a, axis=-1, keepdims=True) + EPS), o)
        gv = g_ref[...].astype(F32)
        y_ref[...] = (o * r * nw_ref[...] * gv * _sigmoid(gv)).astype(y_ref.dtype)

    rowblk = pl.BlockSpec((Q, 128 * HG_HP), lambda h, b, t: (b * nb + t, h))
    return _call(
        body, name="hgrn_fwd", grid=(HG_HEADS // HG_HP, Bl, nb), in_specs=_hg_specs(nb),
        out_specs=[rowblk, rowblk,
                   pl.BlockSpec((1, HG_HP, 1, NSUB, 128, 128), lambda h, b, t: (b, h, t, 0, 0, 0))],
        out_shape=[jax.ShapeDtypeStruct((M, HG_WIDTH), BF16), jax.ShapeDtypeStruct((M, HG_WIDTH), F32),
                   jax.ShapeDtypeStruct((Bl, HG_HEADS, nb, NSUB, 128, 128), F32)],
        scratch=[pltpu.VMEM((HG_HP, 128, 128), F32)], sem=("parallel", "arbitrary", "arbitrary"),
        args=(proj, proj, proj, proj, hb, nw), comm=comm)


def _hgrn_bwd(proj, hb, nw, o_saved, st_saved, dyb, Bl, nb, comm=None):
    M = proj.shape[0]

    def body(q_ref, f_ref, i_ref, g_ref, hb_ref, nw_ref, o_ref, st_ref, dy_ref,
             dq_ref, df_ref, di_ref, dg_ref, dhb_ref, dnw_ref, dS, a_dqt, a_dv, a_dkh, a_dgl):
        b, t = pl.program_id(1), pl.program_id(2)

        @pl.when(t == 0)
        def _():
            dS[...] = jnp.zeros_like(dS)

        first_step = jnp.logical_and(b == 0, t == 0)
        s = _hg_setup(nb - 1 - t, q_ref, f_ref, hb_ref)
        v = i_ref[...]
        qt_b, kt_b, kh_b = s["qt"].astype(BF16), s["kt"].astype(BF16), s["kh"].astype(BF16)
        eT = jnp.exp(s["T"])
        att = [jnp.where(s["causal"], _dot(qt_b[:, ln], kt_b[:, ln], NT), 0.0).astype(BF16) for ln in HEAD_LANES]

        o = o_ref[...]
        r = _per_head(lambda a: lax.rsqrt(jnp.mean(a * a, axis=-1, keepdims=True) + EPS), o)
        xhat = o * r
        gv = g_ref[...].astype(F32)
        sgv = _sigmoid(gv)
        dyv = dy_ref[...].astype(F32)
        d_on = dyv * gv * sgv
        dg_out = dyv * xhat * nw_ref[...] * _dsilu(gv, sgv)
        gw = d_on * nw_ref[...]
        do = r * (gw - xhat * _per_head(lambda a, c: jnp.mean(a * c, axis=-1, keepdims=True), gw, xhat))
        dnw = jnp.sum(d_on * xhat, axis=0, keepdims=True)
        do_b = do.astype(BF16)

        datt = [jnp.where(s["causal"], _dot(do_b[:, ln], v[:, ln], NT), 0.0).astype(BF16) for ln in HEAD_LANES]
        dqt = jnp.concatenate([_dot(datt[hh], kt_b[:, ln], NN) for hh, ln in enumerate(HEAD_LANES)], axis=1)
        dkt = jnp.concatenate([_dot(datt[hh], qt_b[:, ln], TN) for hh, ln in enumerate(HEAD_LANES)], axis=1)
        dv = jnp.concatenate([_dot(att[hh], do_b[:, ln], TN) for hh, ln in enumerate(HEAD_LANES)], axis=1)
        last_row = (lax.broadcasted_iota(jnp.int32, (HG_CHUNK, 128), 0) == HG_CHUNK - 1)
        for j in reversed(range(NSUB)):
            sl = slice(HG_CHUNK * j, HG_CHUNK * (j + 1))
            for hh, ln in enumerate(HEAD_LANES):
                St = st_ref[0, hh, 0, j]
                dSt = dS[hh]
                St_b, dSt_b = St.astype(BF16), dSt.astype(BF16)
                eT_j = eT[HG_CHUNK * j:HG_CHUNK * j + 1, ln]
                dkh_j = _dot(v[sl, ln], dSt_b, NN)
                a_dqt[sl, ln] = _dot(do_b[sl, ln], St_b, NN)
                a_dv[sl, ln] = _dot(kh_b[sl, ln], dSt_b, NT)
                a_dkh[sl, ln] = dkh_j
                dlast = (jnp.sum(St * dSt, axis=0, keepdims=True) * eT_j
                         + jnp.sum(dkh_j * s["kh"][sl, ln], axis=0, keepdims=True))
                a_dgl[sl, ln] = jnp.where(last_row, dlast, 0.0)
                dS[hh] = dSt * eT_j + _dot(do_b[sl, ln], qt_b[sl, ln], TN)
        dqt = dqt + a_dqt[...]
        dv = dv + a_dv[...]
        dkh = a_dkh[...]
        dG = dqt * s["qt"] - dkt * s["kt"] - dkh * s["kh"] + a_dgl[...]
        rev_causal = jnp.logical_and(s["same"], s["col"] >= s["row"])
        dgl = _dot01(rev_causal, dG, NN, "a")
        dk = dkt * s["eGn"] + dkh * s["eTG"]
        dfg = dgl / s["fg"] - dk
        lb, sg = s["lb"], s["sg"]
        keep = s["valid"].astype(F32)
        df_ref[...] = (dfg * (1.0 - lb) * sg * (1.0 - sg) * keep).astype(df_ref.dtype)
        dq_ref[...] = (dqt * s["eG"] * _dsilu(s["qv"], s["sq"]) * keep).astype(dq_ref.dtype)
        di_ref[...] = (dv * keep).astype(di_ref.dtype)
        dg_ref[...] = (dg_out * keep).astype(dg_ref.dtype)
        dlb = jnp.sum(dfg * (1.0 - sg) * keep, axis=0, keepdims=True) * lb * (1.0 - lb)
        dhb = jnp.concatenate([dlb, -dlb], axis=0)

        @pl.when(first_step)
        def _():
            dhb_ref[...] = dhb
            dnw_ref[...] = dnw

        @pl.when(jnp.logical_not(first_step))
        def _():
            dhb_ref[...] += dhb
            dnw_ref[...] += dnw

    w = 128 * HG_HP
    rowblk = pl.BlockSpec((Q, w), lambda h, b, t: (b * nb + nb - 1 - t, h))
    return _call(
        body, name="hgrn_bwd", grid=(HG_HEADS // HG_HP, Bl, nb),
        in_specs=_hg_specs(nb, rev=True) + [
            rowblk, pl.BlockSpec((1, HG_HP, 1, NSUB, 128, 128), lambda h, b, t: (b, h, nb - 1 - t, 0, 0, 0)), rowblk],
        out_specs=[rowblk, rowblk, rowblk, rowblk,
                   pl.BlockSpec((2, w), lambda h, b, t: (0, h)), pl.BlockSpec((1, w), lambda h, b, t: (0, h))],
        out_shape=[jax.ShapeDtypeStruct((M, HG_WIDTH), BF16)] * 4 + [
            jax.ShapeDtypeStruct((2, HG_WIDTH), F32), jax.ShapeDtypeStruct((1, HG_WIDTH), F32)],
        scratch=[pltpu.VMEM((HG_HP, 128, 128), F32)] + [pltpu.VMEM((Q, w), F32)] * 4,
        sem=("parallel", "arbitrary", "arbitrary"),
        args=(proj, proj, proj, proj, hb, nw, o_saved, st_saved, dyb), comm=comm)


def _adamw(name, parts, w, m, v):
    R, C = w.shape
    S = parts.shape[0]
    tr, tc = (_tile(R, (256, 176, 128, 64, 8)), C) if R % 8 == 0 else (R, 256)
    c1, c2 = 1.0 - ADAM_B1 ** ADAM_STEP, 1.0 - ADAM_B2 ** ADAM_STEP

    def body(p_ref, w_ref, m_ref, v_ref, g_ref, d_ref, nm_ref, nv_ref):
        g = p_ref[0].astype(F32)
        for s in range(1, S):
            g = g + p_ref[s].astype(F32)
        nm = ADAM_B1 * m_ref[...] + (1.0 - ADAM_B1) * g
        nv = ADAM_B2 * v_ref[...] + (1.0 - ADAM_B2) * (g * g)
        g_ref[...] = g
        nm_ref[...] = nm
        nv_ref[...] = nv
        d_ref[...] = -ADAM_LR * ((nm / c1) / (jnp.sqrt(nv / c2) + ADAM_EPS) + ADAM_WD * w_ref[...])

    blk = pl.BlockSpec((tr, tc), lambda i, j: (i, j))
    return pl.pallas_call(
        body, name=name, grid=(R // tr, C // tc),
        in_specs=[pl.BlockSpec((S, tr, tc), lambda i, j: (0, i, j)), blk, blk, blk], out_specs=[blk] * 4,
        out_shape=[jax.ShapeDtypeStruct((R, C), F32)] * 4, compiler_params=_params(("parallel", "parallel")),
    )(parts, w, m, v)


def _pair_sum(name, a, b):
    J, R, C = a.shape
    tc = _tile(C, (512, 256, 128))

    def body(a_ref, b_ref, o_ref):
        o_ref[...] = (a_ref[...].astype(F32) + b_ref[...].astype(F32)).astype(o_ref.dtype)

    blk = pl.BlockSpec((1, R, tc), lambda j, k: (j, 0, k))
    return pl.pallas_call(
        body, name=name, grid=(J, C // tc), in_specs=[blk, blk], out_specs=blk,
        out_shape=jax.ShapeDtypeStruct(a.shape, a.dtype), compiler_params=_params(("parallel", "parallel")),
    )(a, b)


def _sum_parts(name, parts):
    S, R, C = parts.shape

    def body(p_ref, o_ref):
        g = p_ref[0]
        for s in range(1, S):
            g = g + p_ref[s]
        o_ref[...] = g

    return pl.pallas_call(
        body, name=name, out_shape=jax.ShapeDtypeStruct((R, C), F32),
        in_specs=[pl.BlockSpec(memory_space=pltpu.VMEM)], out_specs=pl.BlockSpec(memory_space=pltpu.VMEM),
    )(parts)


def _heads_to_lanes(p):
    return jnp.pad(p, [(0, 0)] * (p.ndim - 1) + [(0, 128 - SSD_HEADS)])


def _lanes_to_heads(p):
    return p[..., :SSD_HEADS]


def _pack_rows(arrs):
    rows = []
    for a in arrs:
        f = a.reshape(-1).astype(F32)
        n = -(-f.shape[0] // D_MODEL) * D_MODEL
        rows.append(jnp.pad(f, (0, n - f.shape[0])).reshape(-1, D_MODEL))
    out = jnp.concatenate(rows, axis=0)
    return jnp.pad(out, ((0, (-out.shape[0]) % 8), (0, 0)))


def _unpack_rows(packed, like):
    outs, r = [], 0
    for a in like:
        n = 1
        for s in a.shape:
            n *= s
        nr = -(-n // D_MODEL)
        outs.append(packed[r:r + nr].reshape(-1)[:n].reshape(a.shape))
        r += nr
    return outs


def _cols(gth):
    return jnp.transpose(gth, (1, 0, 2)).reshape(gth.shape[1], -1)


def _rows(gth):
    return gth.reshape(-1, gth.shape[2])


def _to_rows(g):
    return g.reshape(N_DEV, -1, g.shape[1]).astype(BF16)


def _mine_theirs(g):
    by_core = jnp.transpose(g.reshape(N_DEV // 2, 2, -1, g.shape[1]), (1, 0, 2, 3)).astype(BF16)
    c = lax.axis_index("c")
    return (lax.dynamic_index_in_dim(by_core, c, 0, keepdims=False),
            lax.dynamic_index_in_dim(by_core, 1 - c, 0, keepdims=False))


def _chip_sums(tag, grads, swap_in=None):
    pairs = [_mine_theirs(g) for g in grads]
    theirs = [t for _, t in pairs]
    arrived = swap_in(theirs) if swap_in else _exchange(tag + "_swap", "swap", theirs)
    return [_pair_sum(f"{tag}_chipsum{i}", m, a) for i, ((m, _), a) in enumerate(zip(pairs, arrived))]


def _ffn_fwd_gu(tag, h, norm_w, w_gu_t, comm=None):
    M = h.shape[0]
    F = w_gu_t.shape[0] // 2
    tm = _tile(M, (544, 256))
    n = _rmsnorm_fwd(tag + "_norm", h, norm_w)
    tn = _tile(F, (1408, 704, 256))
    outs = _fused_matmul(
        tag + "_gu", M, F, D_MODEL,
        [dict(a=n, b=w_gu_t, trans_b=True, acc=0), dict(a=n, b=w_gu_t, trans_b=True, bn_off=F // tn, acc=1)], [],
        lambda accs, ex: (accs[0], accs[1], accs[0] * _sigmoid(accs[0]) * accs[1]),
        [BF16, BF16, BF16], 2, tm, tn, D_MODEL, outer="j", comm=comm)
    return (n, *outs[:3]), outs[3:]


def _ffn_fwd_down(tag, h, a, w_down):
    M = h.shape[0]
    F = w_down.shape[0]
    (h_out,) = _fused_matmul(
        tag + "_down", M, D_MODEL, F, [dict(a=a, b=w_down, acc=0)], [(h, 0)],
        lambda accs, ex: (ex[0] + 0.5 * accs[0],), [F32], 1, _tile(M, (544, 256)), D_MODEL, F, outer="j")
    return h_out


def _ffn_bwd(tag, dh, dh_b, h, norm_w, w_gu_t, w_down, saved, scatter=False):
    n, g, u, a = saved
    M = h.shape[0]
    F = w_down.shape[0]
    tm = _tile(M, (544, 256))
    tn = _tile(F, (1408, 704, 256))

    def swiglu_bwd(accs, ex):
        da, gv, uv = 0.5 * accs[0], ex[0].astype(F32), ex[1].astype(F32)
        s = _sigmoid(gv)
        return da * uv * _dsilu(gv, s), da * gv * s

    dg, du = _fused_matmul(
        tag + "_dact", M, F, D_MODEL, [dict(a=dh_b, b=w_down, trans_b=True, acc=0)], [(g, 0), (u, 0)],
        swiglu_bwd, [BF16, BF16], 1, tm, tn, D_MODEL, outer="j")
    (dw_down,) = _matmul_tn(tag + "_dwd", a, dh_b, tn, D_MODEL, tm, scale=0.5)
    dw_g, *p_down = _matmul_tn(tag + "_dwg", dg, n, tn, D_MODEL, tm,
                               comm=("scatter", [_to_rows(dw_down)]) if scatter else None)
    (dw_u,) = _matmul_tn(tag + "_dwu", du, n, tn, D_MODEL, tm)
    dw_gu_t = jnp.concatenate([dw_g, dw_u], axis=0)
    comm = None
    if scatter:
        comm = ("chips", _chip_sums(tag + "_wgu", [dw_gu_t]))
    dn, *p_gu = _fused_matmul(
        tag + "_dn", M, D_MODEL, F,
        [dict(a=dg, b=w_gu_t, acc=0), dict(a=du, b=w_gu_t, bk_off=1, acc=0)], [],
        lambda accs, ex: (accs[0],), [F32], 1, tm, D_MODEL, F, outer="i", comm=comm)
    dh_prev, dh_prev_b, dnorm = _rmsnorm_bwd(tag + "_dnorm", dn, h, norm_w, dh)
    return (dh_prev, dh_prev_b, dnorm, *((p_gu[0], p_down[0]) if scatter else (dw_gu_t, dw_down)))


def kernel(x, meta_tokens, ffn1_norm, ffn1_w_gu, ffn1_w_down, mix_norm, w_in, ssd_conv_w, ssd_conv_b, ssd_dt_bias, ssd_a_log, ssd_d, ssd_norm, hg_lower_bound, hg_norm, w_branch_a, w_branch_b, w_out, ffn2_norm, ffn2_w_gu, ffn2_w_down, final_norm, loss_target, m_meta_tokens, m_ffn1_norm, m_ffn1_w_gu, m_ffn1_w_down, m_mix_norm, m_w_in, m_ssd_conv_w, m_ssd_conv_b, m_ssd_dt_bias, m_ssd_a_log, m_ssd_d, m_ssd_norm, m_hg_lower_bound, m_hg_norm, m_w_branch_a, m_w_branch_b, m_w_out, m_ffn2_norm, m_ffn2_w_gu, m_ffn2_w_down, m_final_norm, v_meta_tokens, v_ffn1_norm, v_ffn1_w_gu, v_ffn1_w_down, v_mix_norm, v_w_in, v_ssd_conv_w, v_ssd_conv_b, v_ssd_dt_bias, v_ssd_a_log, v_ssd_d, v_ssd_norm, v_hg_lower_bound, v_hg_norm, v_w_branch_a, v_w_branch_b, v_w_out, v_ffn2_norm, v_ffn2_w_gu, v_ffn2_w_down, v_final_norm):
    Bl, S, D = x.shape
    T = PAD + N_META + S
    nc = T // Q
    M = Bl * T
    me = 4 * lax.axis_index("x") + 2 * lax.axis_index("y") + lax.axis_index("c")

    bf = lambda a: a[0].astype(BF16)
    bft = lambda a: a[0].T.astype(BF16)
    g_wgu1, g_meta, g_conv_w = _exchange("gather_first", "gather", [bft(ffn1_w_gu), meta_tokens, ssd_conv_w[0]])
    wgu1, meta_full, conv_w_full = _rows(g_wgu1), _cols(g_meta), _cols(g_conv_w)
    bias_p, alog_p, d_p = _heads_to_lanes(ssd_dt_bias), _heads_to_lanes(ssd_a_log), _heads_to_lanes(ssd_d)
    final_w = final_norm.reshape(1, D)

    h0 = jnp.concatenate([jnp.zeros((Bl, PAD, D), F32), jnp.broadcast_to(meta_full[None], (Bl, N_META, D)), x],
                         axis=1).reshape(M, D)
    tm = _tile(M, (544, 256))
    ffn1_saved, (g_wd1, g_win) = _ffn_fwd_gu("ffn1", h0, ffn1_norm, wgu1, comm=("gather", [bf(ffn1_w_down), bft(w_in)]))
    wd1, win_t = _rows(g_wd1), _rows(g_win)
    win_main = jnp.concatenate([win_t[:3072], win_t[3088:]], axis=0)
    win_dt = jnp.pad(win_t[3072:3088], ((0, 128 - SSD_HEADS), (0, 0)))
    h1 = _ffn_fwd_down("ffn1", h0, ffn1_saved[3], wd1)
    un = _rmsnorm_fwd("mix_norm", h1, mix_norm)
    plain = lambda accs, ex: (accs[0],)
    proj, g_wa, g_wb, g_wo = _fused_matmul(
        "in_proj", M, N_MAIN, D, [dict(a=un, b=win_main, trans_b=True, acc=0)], [], plain, [BF16], 1, tm, 1536, D,
        outer="j", comm=("gather", [bf(w_branch_a), bf(w_branch_b), bf(w_out)]))
    wa, wb, wo = _rows(g_wa), _rows(g_wb), _rows(g_wo)
    (dtr,) = _fused_matmul("in_proj_dt", M, 128, D, [dict(a=un, b=win_dt, trans_b=True, acc=0)], [], plain, [F32], 1,
                           tm, 128, D, outer="j")
    xc = _conv_fwd(proj, conv_w_full, ssd_conv_b, Bl, T)
    ya, ssd_prev = _ssd_fwd(xc, dtr, proj, bias_p, alog_p, d_p, ssd_norm, Bl, nc)
    yb, hg_o, hg_st, g_wgu2, g_wd2 = _hgrn_fwd(proj, hg_lower_bound, hg_norm, Bl, nc,
                                               comm=("gather", [bft(ffn2_w_gu), bf(ffn2_w_down)]))
    wgu2, wd2 = _rows(g_wgu2), _rows(g_wd2)

    def branch_fwd(accs, ex):
        pa, pb = accs
        return pa, pb, _sigmoid(ex[0].astype(F32)) * pa + _sigmoid(ex[1].astype(F32)) * pb

    pa, pb, merged = _fused_matmul(
        "branches", M, D, D, [dict(a=ya, b=wa, acc=0), dict(a=yb, b=wb, acc=1)], [(proj, 7), (proj, 8)],
        branch_fwd, [BF16, BF16, BF16], 2, tm, D, D, outer="j")
    (h2,) = _fused_matmul("out_proj", M, D, D, [dict(a=merged, b=wo, acc=0)], [(h1, 0)],
                          lambda accs, ex: (ex[0] + accs[0],), [F32], 1, tm, D, D, outer="j")
    ffn2_saved, _ = _ffn_fwd_gu("ffn2", h2, ffn2_norm, wgu2)
    h3 = _ffn_fwd_down("ffn2", h2, ffn2_saved[3], wd2)

    dh3, dh3_b, d_final, loss_part = _loss_head(h3, final_w, loss_target, Bl, nc)
    dh2, dh2_b, d_ffn2_norm, d_wgu2, d_wd2 = _ffn_bwd("ffn2", dh3, dh3_b, h2, ffn2_norm, wgu2, wd2, ffn2_saved)

    def branch_bwd(accs, ex):
        dm = accs[0]
        ga, gb, pav, pbv = (e.astype(F32) for e in ex)
        sa, sb = _sigmoid(ga), _sigmoid(gb)
        return dm * sa, dm * sb, dm * pav * sa * (1.0 - sa), dm * pbv * sb * (1.0 - sb)

    d_merged_outs = []

    def d_merged_with_swap(theirs):
        d_merged_outs.extend(_fused_matmul(
            "d_merged", M, D, D, [dict(a=dh2_b, b=wo, trans_b=True, acc=0)], [(proj, 7), (proj, 8), (pa, 0), (pb, 0)],
            branch_bwd, [BF16] * 4, 1, tm, D, D, outer="j", comm=("swap", theirs)))
        return d_merged_outs[4:]

    s_ffn2 = _chip_sums("ffn2", [d_wgu2, d_wd2], swap_in=d_merged_with_swap)
    dpa, dpb, dga, dgb = d_merged_outs[:4]
    (d_wo,) = _matmul_tn("d_w_out", merged, dh2_b, D, D, tm)
    (d_wa,) = _matmul_tn("d_w_a", ya, dpa, D, D, tm)
    (d_wb,) = _matmul_tn("d_w_b", yb, dpb, D, D, tm)
    dya, dyb = _fused_matmul(
        "d_branches", M, D, D, [dict(a=dpa, b=wa, trans_b=True, acc=0), dict(a=dpb, b=wb, trans_b=True, acc=1)], [],
        lambda accs, ex: (accs[0], accs[1]), [BF16, BF16], 2, tm, D, D, outer="j")
    *ssd_grads, p_wgu2, p_wd2 = _ssd_bwd(xc, dtr, proj, bias_p, alog_p, d_p, ssd_norm, ssd_prev, dya, Bl, nc,
                                         comm=("chips", s_ffn2))
    dxc, dz, ddtr, d_bias_p, d_alog_p, d_d_p, d_ssd_norm = ssd_grads
    dxbc, d_conv_w, d_conv_b = _conv_bwd(proj, conv_w_full, ssd_conv_b, dxc, Bl, T)
    dq, df, di, dg, d_hb, d_hg_norm, p_wa, p_wb, p_wo = _hgrn_bwd(
        proj, hg_lower_bound, hg_norm, hg_o, hg_st, dyb, Bl, nc,
        comm=("scatter", [_to_rows(d_wa), _to_rows(d_wb), _to_rows(d_wo)]))
    dproj = jnp.concatenate([dz, dxbc, dq, df, di, dg, dga, dgb], axis=1)
    ddtr_b = ddtr.astype(BF16)
    (d_win_main,) = _matmul_tn("d_w_in", dproj, un, 1536, D, tm)
    (d_win_dt,) = _matmul_tn("d_w_in_dt", ddtr_b, un, 128, D, tm)
    d_win_t = jnp.concatenate([d_win_main[:3072], d_win_dt[:SSD_HEADS], d_win_main[3072:]], axis=0)
    d_un_dt_outs = []

    def d_un_dt_with_swap(theirs):
        d_un_dt_outs.extend(_fused_matmul("d_un_dt", M, D, 128, [dict(a=ddtr_b, b=win_dt, acc=0)], [], plain, [F32], 1,
                                          tm, D, 128, outer="j", comm=("swap", theirs)))
        return d_un_dt_outs[1:]

    s_win = _chip_sums("w_in", [d_win_t], swap_in=d_un_dt_with_swap)
    dun, p_win = _fused_matmul("d_un", M, D, N_MAIN, [dict(a=dproj, b=win_main, acc=0)], [(d_un_dt_outs[0], 0)],
                               lambda accs, ex: (accs[0] + ex[0],), [F32], 1, tm, D, 3072, outer="i",
                               comm=("chips", s_win))
    dh1, dh1_b, d_mix_norm = _rmsnorm_bwd("d_mix_norm", dun, h1, mix_norm, dh2)
    dh0, _, d_ffn1_norm, p_wgu1, p_wd1 = _ffn_bwd("ffn1", dh1, dh1_b, h0, ffn1_norm, wgu1, wd1, ffn1_saved, scatter=True)

    dh0 = dh0.reshape(Bl, T, D)
    grad_x = dh0[:, PAD + N_META:]
    d_meta = dh0[:, PAD:PAD + N_META]

    small_grads = [d_ffn1_norm, d_mix_norm, d_conv_b, _lanes_to_heads(d_bias_p), _lanes_to_heads(d_alog_p),
                   _lanes_to_heads(d_d_p), d_ssd_norm, d_hb, d_hg_norm, d_ffn2_norm, d_final.reshape(D), d_conv_w]
    small_packed = _pack_rows(small_grads + [d_meta[b] for b in range(Bl)])
    parts = [p_wgu1, p_wd1, p_win, p_wa, p_wb, p_wo, p_wgu2, p_wd2]
    (small_all,) = _exchange("gather_small_grads", "gather", [small_packed])
    small_sum = _sum_parts("sum_small_grads", small_all)
    unpacked = _unpack_rows(small_sum, small_grads + [d_meta[b] for b in range(Bl)])
    g_small = unpacked[:len(small_grads)]
    g_meta_full = unpacked[len(small_grads)]
    for b in range(1, Bl):
        g_meta_full = g_meta_full + unpacked[len(small_grads) + b]
    g_meta = lax.dynamic_slice_in_dim(g_meta_full, me * (D // N_DEV), D // N_DEV, axis=1)
    g_conv_w = lax.dynamic_slice_in_dim(g_small[11], me * (SSD_CONV_CH // N_DEV), SSD_CONV_CH // N_DEV, axis=1)

    names = ["meta_tokens", "ffn1_norm", "ffn1_w_gu", "ffn1_w_down", "mix_norm", "w_in", "ssd_conv_w", "ssd_conv_b",
             "ssd_dt_bias", "ssd_a_log", "ssd_d", "ssd_norm", "hg_lower_bound", "hg_norm", "w_branch_a", "w_branch_b",
             "w_out", "ffn2_norm", "ffn2_w_gu", "ffn2_w_down", "final_norm"]
    W = dict(meta_tokens=meta_tokens, ffn1_norm=ffn1_norm, ffn1_w_gu=ffn1_w_gu, ffn1_w_down=ffn1_w_down, mix_norm=mix_norm,
             w_in=w_in, ssd_conv_w=ssd_conv_w, ssd_conv_b=ssd_conv_b, ssd_dt_bias=ssd_dt_bias, ssd_a_log=ssd_a_log,
             ssd_d=ssd_d, ssd_norm=ssd_norm, hg_lower_bound=hg_lower_bound, hg_norm=hg_norm, w_branch_a=w_branch_a,
             w_branch_b=w_branch_b, w_out=w_out, ffn2_norm=ffn2_norm, ffn2_w_gu=ffn2_w_gu, ffn2_w_down=ffn2_w_down,
             final_norm=final_norm)
    Mo = dict(meta_tokens=m_meta_tokens, ffn1_norm=m_ffn1_norm, ffn1_w_gu=m_ffn1_w_gu, ffn1_w_down=m_ffn1_w_down,
              mix_norm=m_mix_norm, w_in=m_w_in, ssd_conv_w=m_ssd_conv_w, ssd_conv_b=m_ssd_conv_b, ssd_dt_bias=m_ssd_dt_bias,
              ssd_a_log=m_ssd_a_log, ssd_d=m_ssd_d, ssd_norm=m_ssd_norm, hg_lower_bound=m_hg_lower_bound, hg_norm=m_hg_norm,
              w_branch_a=m_w_branch_a, w_branch_b=m_w_branch_b, w_out=m_w_out, ffn2_norm=m_ffn2_norm, ffn2_w_gu=m_ffn2_w_gu,
              ffn2_w_down=m_ffn2_w_down, final_norm=m_final_norm)
    Vo = dict(meta_tokens=v_meta_tokens, ffn1_norm=v_ffn1_norm, ffn1_w_gu=v_ffn1_w_gu, ffn1_w_down=v_ffn1_w_down,
              mix_norm=v_mix_norm, w_in=v_w_in, ssd_conv_w=v_ssd_conv_w, ssd_conv_b=v_ssd_conv_b, ssd_dt_bias=v_ssd_dt_bias,
              ssd_a_log=v_ssd_a_log, ssd_d=v_ssd_d, ssd_norm=v_ssd_norm, hg_lower_bound=v_hg_lower_bound, hg_norm=v_hg_norm,
              w_branch_a=v_w_branch_a, w_branch_b=v_w_branch_b, w_out=v_w_out, ffn2_norm=v_ffn2_norm, ffn2_w_gu=v_ffn2_w_gu,
              ffn2_w_down=v_ffn2_w_down, final_norm=v_final_norm)
    grads, deltas, new_m, new_v = {}, {}, {}, {}
    big_names = ["ffn1_w_gu", "ffn1_w_down", "w_in", "w_branch_a", "w_branch_b", "w_out", "ffn2_w_gu", "ffn2_w_down"]
    transposed = ("ffn1_w_gu", "ffn2_w_gu", "w_in")
    for nm, part in zip(big_names, parts):
        view = (lambda a: a[0].T) if nm in transposed else (lambda a: a[0])
        back = (lambda o: o.T[None]) if nm in transposed else (lambda o: o[None])
        outs = _adamw("adamw_" + nm, part, view(W[nm]), view(Mo[nm]), view(Vo[nm]))
        grads[nm], deltas[nm], new_m[nm], new_v[nm] = (back(o) for o in outs)
    small_names = ["ffn1_norm", "mix_norm", "ssd_conv_b", "ssd_dt_bias", "ssd_a_log", "ssd_d", "ssd_norm", "hg_lower_bound",
                   "hg_norm", "ffn2_norm", "final_norm", "ssd_conv_w", "meta_tokens"]
    small_g = g_small[:11] + [g_conv_w.reshape(ssd_conv_w.shape), g_meta]
    pk = lambda d: _pack_rows([d[nm] for nm in small_names])
    outs = _adamw("adamw_small", _pack_rows(small_g)[None], pk(W), pk(Mo), pk(Vo))
    like = [W[nm] for nm in small_names]
    for dst, o in zip((grads, deltas, new_m, new_v), outs):
        for nm, val in zip(small_names, _unpack_rows(o, like)):
            dst[nm] = val

    loss = lax.psum(loss_part[0, 0], MESH_AXES)
    return (loss, grad_x, *[grads[nm] for nm in names], *[deltas[nm] for nm in names],
            *[new_m[nm] for nm in names], *[new_v[nm] for nm in names])
```

```python
import functools

import jax
import jax.numpy as jnp
from jax import lax
from jax.experimental import pallas as pl
from jax.experimental.pallas import tpu as pltpu

F32, BF16 = jnp.float32, jnp.bfloat16
NN, NT, TN = ((1,), (0,)), ((1,), (1,)), ((0,), (0,))
MESH_AXES = ("x", "y", "c")
N_DEV = 8

D_MODEL = 1024
N_META = 16
EPS = 1e-6
SSD_HEADS, SSD_HEAD_DIM, SSD_GROUPS, SSD_STATE, SSD_CONV, Q = 16, 64, 4, 128, 4, 128
SSD_INNER = SSD_HEADS * SSD_HEAD_DIM
SSD_CONV_CH = SSD_INNER + 2 * SSD_GROUPS * SSD_STATE
HG_WIDTH, HG_HEADS, HG_CHUNK = 1024, 8, 16
PAD = Q - N_META
N_MAIN = 9 * 1024
ADAM_LR, ADAM_B1, ADAM_B2, ADAM_EPS, ADAM_WD, ADAM_STEP = 0.001, 0.9, 0.999, 1e-08, 0.01, 10
VMEM_LIMIT = 52 * 1024 * 1024


def _dot(a, b, dims, prec=None):
    return lax.dot_general(a, b, (dims, ((), ())), precision=prec, preferred_element_type=F32)


def _dot01(a, b, dims, sel):
    x = b if sel == "a" else a
    hi = x.astype(BF16)
    r1 = x - hi.astype(F32)
    mid = r1.astype(BF16)
    lo = (r1 - mid.astype(F32)).astype(BF16)
    s = (a if sel == "a" else b).astype(BF16)
    parts = [_dot(s, p, dims) if sel == "a" else _dot(p, s, dims) for p in (hi, mid, lo)]
    return parts[0] + parts[1] + parts[2]


def _sigmoid(x):
    return 1.0 / (1.0 + jnp.exp(-x))


def _dsilu(x, s):
    return s * (1.0 + x * (1.0 - s))


def _softplus(x):
    e = jnp.exp(-jnp.abs(x))
    u = 1.0 + e
    log1p_e = jnp.where(u == 1.0, e, jnp.log(u) * e / (u - 1.0))
    return jnp.maximum(x, 0.0) + log1p_e


def _params(sem):
    return pltpu.CompilerParams(dimension_semantics=sem, vmem_limit_bytes=VMEM_LIMIT)


def _tile(n, prefs):
    for p in prefs:
        if n % p == 0:
            return p
    return n


CHIP_FLIPS = ((1, 0), (0, 1), (1, 1))
N_PEER = N_DEV - 1


def _comm_gather(srcs, outs, send_sems, recv_sems, local_sems):
    n = len(srcs)
    x, y, c = (lax.axis_index(a) for a in MESH_AXES)
    dev = lambda px, py, pc: 4 * px + 2 * py + pc
    me, sib = dev(x, y, c), (x, y, 1 - c)

    def rc(w, k, slot, to, src=None):
        return pltpu.make_async_remote_copy(
            src_ref=outs[w].at[slot] if src is None else src, dst_ref=outs[w].at[slot],
            send_sem=send_sems.at[w, k], recv_sem=recv_sems.at[w, k], device_id=to, device_id_type=pl.DeviceIdType.MESH)

    def local(w):
        return pltpu.make_async_copy(srcs[w], outs[w].at[me], local_sems.at[w])

    def start():
        for w in range(n):
            local(w).start()
            rc(w, 0, me, sib, src=srcs[w]).start()
            for j, (fx, fy) in enumerate(CHIP_FLIPS):
                rc(w, 1 + j, me, (x ^ fx, y ^ fy, c), src=srcs[w]).start()

    def finish():
        for w in range(n):
            for j, (fx, fy) in enumerate(CHIP_FLIPS):
                slot = dev(x ^ fx, y ^ fy, c)
                rc(w, 1 + j, slot, sib).wait_recv()
                rc(w, 4 + j, slot, sib).start()
        for w in range(n):
            rc(w, 0, dev(x, y, 1 - c), sib).wait_recv()
            rc(w, 0, me, sib, src=srcs[w]).wait_send()
            for j, (fx, fy) in enumerate(CHIP_FLIPS):
                rc(w, 4 + j, dev(x ^ fx, y ^ fy, 1 - c), sib).wait_recv()
                rc(w, 1 + j, me, sib, src=srcs[w]).wait_send()
                rc(w, 4 + j, dev(x ^ fx, y ^ fy, c), sib).wait_send()
            local(w).wait()

    return start, finish


def _comm_scatter(srcs, outs, send_sems, recv_sems, local_sems):
    n = len(srcs)
    x, y, c = (lax.axis_index(a) for a in MESH_AXES)
    me = 4 * x + 2 * y + c

    def copies():
        out = []
        for w in range(n):
            out.append(pltpu.make_async_copy(srcs[w].at[me], outs[w].at[me], local_sems.at[w]))
            for k in range(1, N_DEV):
                px, py, pc = x ^ (k >> 2), y ^ ((k >> 1) & 1), c ^ (k & 1)
                out.append(pltpu.make_async_remote_copy(
                    src_ref=srcs[w].at[4 * px + 2 * py + pc], dst_ref=outs[w].at[me],
                    send_sem=send_sems.at[w, k - 1], recv_sem=recv_sems.at[w, k - 1],
                    device_id=(px, py, pc), device_id_type=pl.DeviceIdType.MESH))
        return out

    def start():
        for cp in copies():
            cp.start()

    def finish():
        for cp in copies():
            cp.wait()

    return start, finish


def _comm_swap(srcs, outs, send_sems, recv_sems, local_sems):
    x, y, c = (lax.axis_index(a) for a in MESH_AXES)

    def copies():
        return [pltpu.make_async_remote_copy(
            src_ref=srcs[w].at[1 - c], dst_ref=outs[w], send_sem=send_sems.at[w, 0], recv_sem=recv_sems.at[w, 0],
            device_id=(x, y, 1 - c), device_id_type=pl.DeviceIdType.MESH) for w in range(len(srcs))]

    def start():
        for cp in copies():
            cp.start()

    def finish():
        for cp in copies():
            cp.wait()

    return start, finish


def _comm_chips(srcs, outs, send_sems, recv_sems, local_sems):
    n = len(srcs)
    x, y, c = (lax.axis_index(a) for a in MESH_AXES)
    mine = 2 * x + y

    def copies():
        out = []
        for w in range(n):
            out.append(pltpu.make_async_copy(srcs[w].at[mine], outs[w].at[mine], local_sems.at[w]))
            for j, (fx, fy) in enumerate(CHIP_FLIPS):
                px, py = x ^ fx, y ^ fy
                out.append(pltpu.make_async_remote_copy(
                    src_ref=srcs[w].at[2 * px + py], dst_ref=outs[w].at[mine],
                    send_sem=send_sems.at[w, j], recv_sem=recv_sems.at[w, j],
                    device_id=(px, py, c), device_id_type=pl.DeviceIdType.MESH))
        return out

    def start():
        for cp in copies():
            cp.start()

    def finish():
        for cp in copies():
            cp.wait()

    return start, finish


def _comm_parts(comm):
    kind, arrays = comm
    n = len(arrays)
    lead = {"gather": lambda a: (N_DEV,) + a.shape, "scatter": lambda a: (N_DEV,) + a.shape[1:],
            "swap": lambda a: a.shape[1:], "chips": lambda a: a.shape}[kind]
    shapes = [jax.ShapeDtypeStruct(lead(a), a.dtype) for a in arrays]
    sems = [pltpu.SemaphoreType.DMA((n, N_PEER)), pltpu.SemaphoreType.DMA((n, N_PEER)), pltpu.SemaphoreType.DMA((n,))]
    make = {"gather": _comm_gather, "scatter": _comm_scatter, "swap": _comm_swap, "chips": _comm_chips}[kind]
    return n, shapes, sems, make


def _exchange(name, kind, arrays):
    n, shapes, sems, make = _comm_parts((kind, arrays))

    def body(*refs):
        start, finish = make(refs[:n], refs[n:2 * n], *refs[2 * n:])
        start()
        finish()

    any_spec = pl.BlockSpec(memory_space=pl.ANY)
    return pl.pallas_call(
        body, name=name, in_specs=[any_spec] * n, out_specs=[any_spec] * n, out_shape=shapes, scratch_shapes=sems,
        compiler_params=pltpu.CompilerParams(has_side_effects=True),
    )(*arrays)


def _call(body, *, name, grid, in_specs, out_specs, out_shape, scratch, sem, args, comm=None):
    if comm is None:
        return pl.pallas_call(body, name=name, grid=grid, in_specs=in_specs, out_specs=out_specs, out_shape=out_shape,
                              scratch_shapes=scratch, compiler_params=_params(sem))(*args)
    n, shapes, sems, make = _comm_parts(comm)
    n_in, n_out, n_scr = len(in_specs), len(out_specs), len(scratch)

    def carrier(*refs):
        ins, csrc = refs[:n_in], refs[n_in:n_in + n]
        outs, cout = refs[n_in + n:n_in + n + n_out], refs[n_in + n + n_out:n_in + 2 * n + n_out]
        rest = refs[n_in + 2 * n + n_out:]
        start, finish = make(csrc, cout, *rest[n_scr:])
        ids = [pl.program_id(a) for a in range(len(grid))]
        first = functools.reduce(jnp.logical_and, [i == 0 for i in ids])
        last = functools.reduce(jnp.logical_and, [i == g - 1 for i, g in zip(ids, grid)])
        pl.when(first)(start)
        body(*ins, *outs, *rest[:n_scr])
        pl.when(last)(finish)

    any_spec = pl.BlockSpec(memory_space=pl.ANY)
    return pl.pallas_call(
        carrier, name=name, grid=grid, in_specs=list(in_specs) + [any_spec] * n,
        out_specs=list(out_specs) + [any_spec] * n, out_shape=list(out_shape) + shapes,
        scratch_shapes=list(scratch) + sems,
        compiler_params=pltpu.CompilerParams(dimension_semantics=("arbitrary",) * len(grid),
                                             vmem_limit_bytes=VMEM_LIMIT, has_side_effects=True),
    )(*args, *comm[1])


def _fused_matmul(name, M, N, K, pairs, extras, epilogue, out_dtypes, n_acc, tm, tn, tk, outer="i", comm=None,
                  stack=False):
    nk = K // tk
    n_pairs, n_ex, n_out = len(pairs), len(extras), len(out_dtypes)

    def ij(g0, g1):
        return (g0, g1) if outer == "i" else (g1, g0)

    in_specs, args = [], []
    for p in pairs:
        ao, bk, bn = p.get("a_off", 0), p.get("bk_off", 0), p.get("bn_off", 0)
        if "a_lead" in p:
            in_specs.append(pl.BlockSpec((None, tm, tk),
                                         lambda g0, g1, k, ao=ao, ld=p["a_lead"]: (ld, ij(g0, g1)[0], k + ao)))
        else:
            in_specs.append(pl.BlockSpec((tm, tk), lambda g0, g1, k, ao=ao: (ij(g0, g1)[0], k + ao)))
        if p.get("trans_b"):
            in_specs.append(pl.BlockSpec((tn, tk), lambda g0, g1, k, bk=bk, bn=bn: (ij(g0, g1)[1] + bn, k + bk)))
        else:
            in_specs.append(pl.BlockSpec((tk, tn), lambda g0, g1, k, bk=bk, bn=bn: (k + bk, ij(g0, g1)[1] + bn)))
        args += [p["a"], p["b"]]
    for arr, off in extras:
        in_specs.append(pl.BlockSpec((tm, tn), lambda g0, g1, k, off=off: (ij(g0, g1)[0], ij(g0, g1)[1] + off)))
        args.append(arr)
    if stack:
        out_specs = [pl.BlockSpec((n_out, tm, tn), lambda g0, g1, k: (0,) + ij(g0, g1))]
        out_shape = [jax.ShapeDtypeStruct((n_out, M, N), out_dtypes[0])]
    else:
        out_specs = [pl.BlockSpec((tm, tn), lambda g0, g1, k: ij(g0, g1)) for _ in out_dtypes]
        out_shape = [jax.ShapeDtypeStruct((M, N), dt) for dt in out_dtypes]
    grid = (M // tm, N // tn, nk) if outer == "i" else (N // tn, M // tm, nk)

    def partials(refs):
        accs = [None] * n_acc
        for idx, p in enumerate(pairs):
            d = _dot(refs[2 * idx][...], refs[2 * idx + 1][...], NT if p.get("trans_b") else NN)
            accs[p["acc"]] = d if accs[p["acc"]] is None else accs[p["acc"]] + d
        return accs

    def finish(accs, refs):
        ex = [r[...] for r in refs[2 * n_pairs:2 * n_pairs + n_ex]]
        res = epilogue(accs, ex)
        if stack:
            o = refs[2 * n_pairs + n_ex]
            for idx, r in enumerate(res):
                o[idx] = r.astype(o.dtype)
        else:
            for o, r in zip(refs[2 * n_pairs + n_ex:2 * n_pairs + n_ex + n_out], res):
                o[...] = r.astype(o.dtype)

    if nk == 1:
        def body(*refs):
            finish(partials(refs), refs)
        scratch = []
    else:
        def body(*refs):
            acc_refs = refs[-n_acc:]
            k = pl.program_id(2)
            new = partials(refs)

            @pl.when(k == 0)
            def _():
                for a, v in zip(acc_refs, new):
                    a[...] = v

            @pl.when(k > 0)
            def _():
                for a, v in zip(acc_refs, new):
                    a[...] += v

            @pl.when(k == nk - 1)
            def _():
                finish([a[...] for a in acc_refs], refs)
        scratch = [pltpu.VMEM((tm, tn), F32) for _ in range(n_acc)]

    return _call(body, name=name, grid=grid, in_specs=in_specs, out_specs=out_specs, out_shape=out_shape,
                 scratch=scratch, sem=("parallel", "parallel", "arbitrary"), args=args, comm=comm)


def _matmul_tn(name, x, y, t1, t2, tr, scale=1.0, comm=None):
    L = x.shape[0] if x.ndim == 3 else 1
    R, K1 = x.shape[-2:]
    N1 = y.shape[1]
    nr, n1 = R // tr, K1 // t1
    if x.ndim == 3:
        x_spec = pl.BlockSpec((None, tr, t1), lambda i, j, r: (i // n1, r, i % n1))
    else:
        x_spec = pl.BlockSpec((tr, t1), lambda i, j, r: (r, i))

    def body(x_ref, y_ref, o_ref):
        r = pl.program_id(2)
        d = _dot(x_ref[...], y_ref[...], TN)

        @pl.when(r == 0)
        def _():
            o_ref[...] = d

        @pl.when(r > 0)
        def _():
            o_ref[...] += d

        if scale != 1.0:
            @pl.when(r == nr - 1)
            def _():
                o_ref[...] = o_ref[...] * scale

    return _call(
        body, name=name, grid=(L * n1, N1 // t2, nr),
        in_specs=[x_spec, pl.BlockSpec((tr, t2), lambda i, j, r: (r, j))],
        out_specs=[pl.BlockSpec((t1, t2), lambda i, j, r: (i, j))],
        out_shape=[jax.ShapeDtypeStruct((L * K1, N1), F32)], scratch=[],
        sem=("parallel", "parallel", "arbitrary"), args=(x, y), comm=comm)


def _rmsnorm_fwd(name, h, w):
    M, D = h.shape
    tm = _tile(M, (544, 256, 128))

    def body(h_ref, w_ref, o_ref):
        x = h_ref[...]
        r = lax.rsqrt(jnp.mean(x * x, axis=-1, keepdims=True) + EPS)
        o_ref[...] = (x * r * w_ref[...]).astype(o_ref.dtype)

    return pl.pallas_call(
        body, name=name, grid=(M // tm,),
        in_specs=[pl.BlockSpec((tm, D), lambda i: (i, 0)), pl.BlockSpec((1, D), lambda i: (0, 0))],
        out_specs=pl.BlockSpec((tm, D), lambda i: (i, 0)),
        out_shape=jax.ShapeDtypeStruct((M, D), BF16), compiler_params=_params(("parallel",)),
    )(h, w)


def _rmsnorm_bwd(name, dn, h, w, dh_in):
    M, D = h.shape
    tm = _tile(M, (544, 256, 128))

    def body(dn_ref, h_ref, w_ref, dhi_ref, dh_ref, dhb_ref, dw_ref):
        x = h_ref[...]
        r = lax.rsqrt(jnp.mean(x * x, axis=-1, keepdims=True) + EPS)
        xhat = x * r
        dn_v = dn_ref[...]
        gw = dn_v * w_ref[...]
        dx = r * (gw - xhat * jnp.mean(gw * xhat, axis=-1, keepdims=True))
        dh = dhi_ref[...] + dx
        dh_ref[...] = dh
        dhb_ref[...] = dh.astype(BF16)
        dw = jnp.sum(dn_v * xhat, axis=0, keepdims=True)

        @pl.when(pl.program_id(0) == 0)
        def _():
            dw_ref[...] = dw

        @pl.when(pl.program_id(0) > 0)
        def _():
            dw_ref[...] += dw

    row = pl.BlockSpec((tm, D), lambda i: (i, 0))
    vec = pl.BlockSpec((1, D), lambda i: (0, 0))
    return pl.pallas_call(
        body, name=name, grid=(M // tm,), in_specs=[row, row, vec, row], out_specs=[row, row, vec],
        out_shape=[jax.ShapeDtypeStruct((M, D), F32), jax.ShapeDtypeStruct((M, D), BF16), jax.ShapeDtypeStruct((1, D), F32)],
        compiler_params=_params(("arbitrary",)),
    )(dn, h, w, dh_in)


def _loss_head(h, w, target, Bl, nb):
    M, D = h.shape

    def body(h_ref, w_ref, t_ref, dh_ref, dhb_ref, dw_ref, loss_ref):
        b, t = pl.program_id(0), pl.program_id(1)
        live = (t > 0).astype(F32)
        x = h_ref[...]
        r = lax.rsqrt(jnp.mean(x * x, axis=-1, keepdims=True) + EPS)
        xhat = x * r
        wv = w_ref[...]
        err = (xhat * wv - t_ref[0]) * live
        dy = err * (1.0 / D)
        gw = dy * wv
        dx = r * (gw - xhat * jnp.mean(gw * xhat, axis=-1, keepdims=True))
        dh_ref[...] = dx
        dhb_ref[...] = dx.astype(BF16)
        dw = jnp.sum(dy * xhat, axis=0, keepdims=True)
        part = 0.5 * jnp.sum(jnp.sum(err * err, axis=-1, keepdims=True) * (1.0 / D), axis=0, keepdims=True)
        first = jnp.logical_and(b == 0, t == 0)

        @pl.when(first)
        def _():
            dw_ref[...] = dw
            loss_ref[...] = jnp.broadcast_to(part, loss_ref.shape)

        @pl.when(jnp.logical_not(first))
        def _():
            dw_ref[...] += dw
            loss_ref[...] += jnp.broadcast_to(part, loss_ref.shape)

    row = pl.BlockSpec((Q, D), lambda b, t: (b * nb + t, 0))
    vec = pl.BlockSpec((1, D), lambda b, t: (0, 0))
    return pl.pallas_call(
        body, name="loss_head", grid=(Bl, nb),
        in_specs=[row, vec, pl.BlockSpec((1, Q, D), lambda b, t: (b, jnp.maximum(t - 1, 0), 0))],
        out_specs=[row, row, vec, pl.BlockSpec((8, 128), lambda b, t: (0, 0))],
        out_shape=[jax.ShapeDtypeStruct((M, D), F32), jax.ShapeDtypeStruct((M, D), BF16),
                   jax.ShapeDtypeStruct((1, D), F32), jax.ShapeDtypeStruct((8, 128), F32)],
        compiler_params=_params(("arbitrary", "arbitrary")),
    )(h, w, target)


CONV_TC = 256


def _conv_pre(xr_ref, w_ref, b_ref):
    x = xr_ref[...].astype(F32)
    acc = b_ref[...] + w_ref[SSD_CONV - 1:SSD_CONV, :] * x
    for k in range(1, SSD_CONV):
        acc = acc + w_ref[SSD_CONV - 1 - k:SSD_CONV - k, :] * pltpu.roll(x, k, 0)
    return x, acc


def _conv_fwd(proj, w, b, Bl, T):
    M = proj.shape[0]
    off = 1024 // CONV_TC

    def body(xr_ref, w_ref, b_ref, o_ref):
        _, acc = _conv_pre(xr_ref, w_ref, b_ref)
        row = lax.broadcasted_iota(jnp.int32, acc.shape, 0)
        o_ref[...] = jnp.where(row >= PAD, acc * _sigmoid(acc), 0.0).astype(o_ref.dtype)

    return pl.pallas_call(
        body, name="conv_fwd", grid=(Bl, SSD_CONV_CH // CONV_TC),
        in_specs=[pl.BlockSpec((T, CONV_TC), lambda bb, j: (bb, j + off)),
                  pl.BlockSpec((SSD_CONV, CONV_TC), lambda bb, j: (0, j)), pl.BlockSpec((1, CONV_TC), lambda bb, j: (0, j))],
        out_specs=pl.BlockSpec((T, CONV_TC), lambda bb, j: (bb, j)),
        out_shape=jax.ShapeDtypeStruct((M, SSD_CONV_CH), BF16), compiler_params=_params(("parallel", "parallel")),
    )(proj, w, b)


def _conv_bwd(proj, w, b, dxc, Bl, T):
    M = proj.shape[0]
    off = 1024 // CONV_TC

    def body(xr_ref, w_ref, b_ref, d_ref, dx_ref, dw_ref, db_ref):
        x, acc = _conv_pre(xr_ref, w_ref, b_ref)
        row = lax.broadcasted_iota(jnp.int32, acc.shape, 0)
        s = _sigmoid(acc)
        dpre = jnp.where(row >= PAD, d_ref[...].astype(F32) * _dsilu(acc, s), 0.0)
        dx = w_ref[SSD_CONV - 1:SSD_CONV, :] * dpre
        dws = [jnp.sum(dpre * x, axis=0, keepdims=True)]
        for k in range(1, SSD_CONV):
            dx = dx + w_ref[SSD_CONV - 1 - k:SSD_CONV - k, :] * pltpu.roll(dpre, T - k, 0)
            dws.append(jnp.sum(dpre * pltpu.roll(x, k, 0), axis=0, keepdims=True))
        dx_ref[...] = dx.astype(dx_ref.dtype)
        dw = jnp.concatenate(dws[::-1], axis=0)
        db = jnp.sum(dpre, axis=0, keepdims=True)

        @pl.when(pl.program_id(1) == 0)
        def _():
            dw_ref[...] = dw
            db_ref[...] = db

        @pl.when(pl.program_id(1) > 0)
        def _():
            dw_ref[...] += dw
            db_ref[...] += db

    return pl.pallas_call(
        body, name="conv_bwd", grid=(SSD_CONV_CH // CONV_TC, Bl),
        in_specs=[pl.BlockSpec((T, CONV_TC), lambda j, bb: (bb, j + off)),
                  pl.BlockSpec((SSD_CONV, CONV_TC), lambda j, bb: (0, j)), pl.BlockSpec((1, CONV_TC), lambda j, bb: (0, j)),
                  pl.BlockSpec((T, CONV_TC), lambda j, bb: (bb, j))],
        out_specs=[pl.BlockSpec((T, CONV_TC), lambda j, bb: (bb, j)),
                   pl.BlockSpec((SSD_CONV, CONV_TC), lambda j, bb: (0, j)), pl.BlockSpec((1, CONV_TC), lambda j, bb: (0, j))],
        out_shape=[jax.ShapeDtypeStruct((M, SSD_CONV_CH), BF16), jax.ShapeDtypeStruct((SSD_CONV, SSD_CONV_CH), F32),
                   jax.ShapeDtypeStruct((1, SSD_CONV_CH), F32)],
        compiler_params=_params(("parallel", "arbitrary")),
    )(proj, w, b, dxc)


N_PAIR = SSD_HEADS // 2
HPG = SSD_HEADS // SSD_GROUPS
GW = SSD_INNER // SSD_GROUPS


def _per_group(fn, *arrs):
    return jnp.concatenate([jnp.broadcast_to(fn(*(a[:, GW * g:GW * (g + 1)] for a in arrs)), (arrs[0].shape[0], GW))
                            for g in range(SSD_GROUPS)], axis=1)


def _ssd_prep(c, dtr_ref, bias_ref, alog_ref, d_ref):
    row = lax.broadcasted_iota(jnp.int32, (Q, 128), 0)
    col = lax.broadcasted_iota(jnp.int32, (Q, 128), 1)
    live = col < SSD_HEADS
    valid = jnp.logical_and(jnp.logical_or(c > 0, row >= PAD), live)
    pre = dtr_ref[...] + bias_ref[...]
    dt = jnp.where(valid, _softplus(pre), 0.0)
    A = jnp.where(live[0:1], -jnp.exp(alog_ref[...]), 0.0)
    tri = row >= col
    eye = (row == col).astype(BF16)
    cs = _dot01(tri, dt * A, NN, "a")
    cst = _dot01(eye, cs, NT, "a")
    spread = (lax.broadcasted_iota(jnp.int32, (128, SSD_INNER), 0)
              == lax.broadcasted_iota(jnp.int32, (128, SSD_INNER), 1) // SSD_HEAD_DIM).astype(BF16)
    dt_w = _dot01(dt, spread, NN, "b")
    cs_w = _dot01(cs, spread, NN, "b")
    d_w = _dot01(jnp.broadcast_to(d_ref[...], (8, 128)), spread, NN, "b")[0:1]
    lane = lax.broadcasted_iota(jnp.int32, (Q, SSD_INNER), 1)
    first = (lane % 128) < SSD_HEAD_DIM
    return dict(row=row, col=col, valid=valid, pre=pre, dt=dt, A=A, tri=tri, eye=eye, cs=cs, cst=cst, spread=spread,
                dt_w=dt_w, cs_w=cs_w, d_w=d_w, ecs_w=jnp.exp(cs_w), decay_w=jnp.exp(cs_w[Q - 1:Q] - cs_w), first=first)


def _ssd_chunk(xc_ref, s, states):
    xv = xc_ref[:, 0:SSD_INNER].astype(F32)
    Bs = [xc_ref[:, SSD_INNER + 128 * g:SSD_INNER + 128 * (g + 1)] for g in range(SSD_GROUPS)]
    Cs = [xc_ref[:, SSD_INNER + 512 + 128 * g:SSD_INNER + 512 + 128 * (g + 1)] for g in range(SSD_GROUPS)]
    X = xv * s["dt_w"]
    X0 = jnp.where(s["first"], X, 0.0)
    Xb = (X0.astype(BF16), (X - X0).astype(BF16))
    Xd = (X * s["decay_w"]).astype(BF16)
    CB = [_dot(Cs[g], Bs[g], NT) for g in range(SSD_GROUPS)]
    Lms = [jnp.exp(jnp.where(s["tri"], s["cs"][:, h:h + 1] - s["cst"][h:h + 1, :], -jnp.inf)) for h in range(SSD_HEADS)]
    Ms = [CB[h // HPG] * Lms[h] for h in range(SSD_HEADS)]
    Mb = [m.astype(BF16) for m in Ms]
    prev_b = [st.astype(BF16) for st in states]
    yds, yos, sts = [], [], []
    for p in range(N_PAIR):
        g, ln = p // 2, slice(128 * p, 128 * (p + 1))
        yds.append(_dot(Mb[2 * p], Xb[0][:, ln], NN) + _dot(Mb[2 * p + 1], Xb[1][:, ln], NN))
        yos.append(_dot(Cs[g], prev_b[p], NT))
        sts.append(_dot(Xd[:, ln], Bs[g], TN))
    yo = jnp.concatenate(yos, axis=1)
    y = jnp.concatenate(yds, axis=1) + yo * s["ecs_w"] + xv * s["d_w"]
    upper = s["row"] < SSD_HEAD_DIM
    cl = s["cs"][Q - 1:Q, :]
    ecl_rows = [jnp.where(upper, jnp.exp(cl[:, 2 * p:2 * p + 1]), jnp.exp(cl[:, 2 * p + 1:2 * p + 2])) for p in range(N_PAIR)]
    new_states = [states[p] * ecl_rows[p] + sts[p] for p in range(N_PAIR)]
    return y, new_states, dict(xv=xv, Bs=Bs, Cs=Cs, X=X, Xb=Xb, CB=CB, Lms=Lms, Ms=Ms, Mb=Mb, prev_b=prev_b, yo=yo,
                               ecl_rows=ecl_rows)


def _ssd_in_specs(nc, rev=False):
    rb = (lambda b, c: b * nc + nc - 1 - c) if rev else (lambda b, c: b * nc + c)
    vec = pl.BlockSpec((1, 128), lambda b, c: (0, 0))
    return [pl.BlockSpec((Q, SSD_CONV_CH), lambda b, c: (rb(b, c), 0)),
            pl.BlockSpec((Q, 128), lambda b, c: (rb(b, c), 0)),
            pl.BlockSpec((Q, SSD_INNER), lambda b, c: (rb(b, c), 0)),
            vec, vec, vec, pl.BlockSpec((1, SSD_INNER), lambda b, c: (0, 0))]


def _ssd_fwd(xc, dtr, proj, bias_p, alog_p, d_p, nw, Bl, nc):
    M = xc.shape[0]

    def body(xc_ref, dtr_ref, z_ref, bias_ref, alog_ref, d_ref, nw_ref, y_ref, prev_ref, state):
        c = pl.program_id(1)

        @pl.when(c == 0)
        def _():
            state[...] = jnp.zeros_like(state)

        s = _ssd_prep(c, dtr_ref, bias_ref, alog_ref, d_ref)
        states = [state[p] for p in range(N_PAIR)]
        y, new_states, _ = _ssd_chunk(xc_ref, s, states)
        for p in range(N_PAIR):
            prev_ref[0, 0, p] = states[p]
            state[p] = new_states[p]
        zz = z_ref[...].astype(F32)
        yg = y * zz * _sigmoid(zz)
        r = _per_group(lambda a: lax.rsqrt(jnp.mean(a * a, axis=-1, keepdims=True) + EPS), yg)
        y_ref[...] = (yg * r * nw_ref[...]).astype(y_ref.dtype)

    return pl.pallas_call(
        body, name="ssd_fwd", grid=(Bl, nc), in_specs=_ssd_in_specs(nc),
        out_specs=[pl.BlockSpec((Q, SSD_INNER), lambda b, c: (b * nc + c, 0)),
                   pl.BlockSpec((1, 1, N_PAIR, 128, 128), lambda b, c: (b, c, 0, 0, 0))],
        out_shape=[jax.ShapeDtypeStruct((M, SSD_INNER), BF16), jax.ShapeDtypeStruct((Bl, nc, N_PAIR, 128, 128), F32)],
        scratch_shapes=[pltpu.VMEM((N_PAIR, 128, 128), F32)],
        compiler_params=_params(("arbitrary", "arbitrary")),
    )(xc, dtr, proj, bias_p, alog_p, d_p, nw)


def _ssd_bwd(xc, dtr, proj, bias_p, alog_p, d_p, nw, prev, dya, Bl, nc, comm=None):
    M = xc.shape[0]

    def body(xc_ref, dtr_ref, z_ref, bias_ref, alog_ref, d_ref, nw_ref, prev_ref, dy_ref,
             dxc_ref, dz_ref, ddtr_ref, dbias_ref, dalog_ref, dd_ref, dnw_ref, dS):
        b, t = pl.program_id(0), pl.program_id(1)

        @pl.when(t == 0)
        def _():
            dS[...] = jnp.zeros_like(dS)

        s = _ssd_prep(nc - 1 - t, dtr_ref, bias_ref, alog_ref, d_ref)
        states = [prev_ref[0, 0, p] for p in range(N_PAIR)]
        y, _, k = _ssd_chunk(xc_ref, s, states)
        xv, Bs, Cs, Xb = k["xv"], k["Bs"], k["Cs"], k["Xb"]

        zz = z_ref[...].astype(F32)
        sz = _sigmoid(zz)
        silu_z = zz * sz
        yg = y * silu_z
        r = _per_group(lambda a: lax.rsqrt(jnp.mean(a * a, axis=-1, keepdims=True) + EPS), yg)
        xhat = yg * r
        dout = dy_ref[...].astype(F32)
        gw = dout * nw_ref[...]
        dyg = r * (gw - xhat * _per_group(lambda a, c2: jnp.mean(a * c2, axis=-1, keepdims=True), gw, xhat))
        dnw = jnp.sum(dout * xhat, axis=0, keepdims=True)
        dz_ref[...] = (dyg * y * _dsilu(zz, sz)).astype(dz_ref.dtype)
        dy = dyg * silu_z
        dy0 = jnp.where(s["first"], dy, 0.0)
        dyb = (dy0.astype(BF16), (dy - dy0).astype(BF16))
        dYo = (dy * s["ecs_w"]).astype(BF16)

        dS_f = [dS[p] for p in range(N_PAIR)]
        dS_b = [d.astype(BF16) for d in dS_f]
        BdS, dXm, dprev, dCs, dMs, XdS = [], [], [], [[] for _ in range(SSD_GROUPS)], [], []
        for p in range(N_PAIR):
            g, ln = p // 2, slice(128 * p, 128 * (p + 1))
            BdS.append(_dot(Bs[g], dS_b[p], NT))
            dXm.append(_dot(k["Mb"][2 * p], dyb[0][:, ln], TN) + _dot(k["Mb"][2 * p + 1], dyb[1][:, ln], TN))
            dprev.append(_dot(dYo[:, ln], Cs[g], TN))
            dCs[g].append(_dot(dYo[:, ln], k["prev_b"][p], NN))
            for hh in range(2):
                dMs.append(_dot(dyb[hh][:, ln], Xb[hh][:, ln], NT))
                XdS.append(_dot(Xb[hh][:, ln], dS_b[p], NN))
        dX = jnp.concatenate(dXm, axis=1) + s["decay_w"] * jnp.concatenate(BdS, axis=1)
        dxs = dy * s["d_w"] + dX * s["dt_w"]

        heads = lambda a: _dot01(a, s["spread"], NT, "b")
        ddt = heads(dX * xv)
        dcs = heads(dy * k["yo"] * s["ecs_w"])
        dD = jnp.sum(heads(dy * xv), axis=0, keepdims=True)

        col, row = s["col"], s["row"]
        lane1 = col[0:1]
        rowsT = lax.broadcasted_iota(jnp.int32, (128, Q), 0)
        dcs_t = jnp.zeros((128, Q), F32)
        dcl = jnp.zeros((1, 128), F32)
        dB_out, dC_out = [], []
        for g in range(SSD_GROUPS):
            Bf = Bs[g].astype(F32)
            dCB = jnp.zeros((Q, Q), F32)
            dBacc = jnp.zeros((Q, 128), F32)
            for r4 in range(HPG):
                h = HPG * g + r4
                p, hh = h // 2, h % 2
                W = dMs[h] * k["Ms"][h]
                dCB = dCB + dMs[h] * k["Lms"][h]
                decay_h = s["decay_w"][:, SSD_HEAD_DIM * h:SSD_HEAD_DIM * h + 1]
                dBacc = dBacc + decay_h * XdS[h]
                tdec = jnp.sum(XdS[h] * Bf, axis=1, keepdims=True) * decay_h
                dcs = dcs + jnp.where(col == h, jnp.sum(W, axis=1, keepdims=True) - tdec, 0.0)
                dcs_t = dcs_t - jnp.where(rowsT == h, jnp.sum(W, axis=0, keepdims=True), 0.0)
                rows_h = (row < SSD_HEAD_DIM) if hh == 0 else (row >= SSD_HEAD_DIM)
                sprev = jnp.sum(jnp.sum(jnp.where(rows_h, dS_f[p] * states[p], 0.0), axis=1, keepdims=True),
                                axis=0, keepdims=True)
                ecl = jnp.exp(s["cs"][Q - 1:Q, h:h + 1])
                dcl = dcl + jnp.where(lane1 == h, jnp.sum(tdec, axis=0, keepdims=True) + ecl * sprev, 0.0)
            dCB_b = dCB.astype(BF16)
            dC_out.append(dCs[g][0] + dCs[g][1] + _dot(dCB_b, Bs[g], NN))
            dB_out.append(dBacc + _dot(dCB_b, Cs[g], TN))
        for p in range(N_PAIR):
            dS[p] = dS_f[p] * k["ecl_rows"][p] + dprev[p]
        dxc_ref[...] = jnp.concatenate([dxs] + dB_out + dC_out, axis=1).astype(dxc_ref.dtype)

        dcs = dcs + _dot01(s["eye"], dcs_t, NT, "a") + jnp.where(row == Q - 1, dcl, 0.0)
        da = _dot01(row <= col, dcs, NN, "a")
        ddt = ddt + da * s["A"]
        dpre = jnp.where(s["valid"], ddt * _sigmoid(s["pre"]), 0.0)
        ddtr_ref[...] = dpre
        dbias = jnp.sum(dpre, axis=0, keepdims=True)
        dalog = jnp.sum(da * s["dt"], axis=0, keepdims=True) * s["A"]
        first_step = jnp.logical_and(b == 0, t == 0)

        @pl.when(first_step)
        def _():
            dbias_ref[...] = dbias
            dalog_ref[...] = dalog
            dd_ref[...] = dD
            dnw_ref[...] = dnw

        @pl.when(jnp.logical_not(first_step))
        def _():
            dbias_ref[...] += dbias
            dalog_ref[...] += dalog
            dd_ref[...] += dD
            dnw_ref[...] += dnw

    rb = lambda b, c: b * nc + nc - 1 - c
    rowblk = lambda w: pl.BlockSpec((Q, w), lambda b, c: (rb(b, c), 0))
    vec = lambda w: pl.BlockSpec((1, w), lambda b, c: (0, 0))
    return _call(
        body, name="ssd_bwd", grid=(Bl, nc),
        in_specs=_ssd_in_specs(nc, rev=True) + [
            pl.BlockSpec((1, 1, N_PAIR, 128, 128), lambda b, c: (b, nc - 1 - c, 0, 0, 0)), rowblk(SSD_INNER)],
        out_specs=[rowblk(SSD_CONV_CH), rowblk(SSD_INNER), rowblk(128), vec(128), vec(128), vec(128), vec(SSD_INNER)],
        out_shape=[jax.ShapeDtypeStruct((M, SSD_CONV_CH), BF16), jax.ShapeDtypeStruct((M, SSD_INNER), BF16),
                   jax.ShapeDtypeStruct((M, 128), F32), jax.ShapeDtypeStruct((1, 128), F32),
                   jax.ShapeDtypeStruct((1, 128), F32), jax.ShapeDtypeStruct((1, 128), F32),
                   jax.ShapeDtypeStruct((1, SSD_INNER), F32)],
        scratch=[pltpu.VMEM((N_PAIR, 128, 128), F32)], sem=("arbitrary", "arbitrary"),
        args=(xc, dtr, proj, bias_p, alog_p, d_p, nw, prev, dya), comm=comm)


NSUB = Q // HG_CHUNK
HG_HP = 8
EXP_CAP = 80.0


def _hg_setup(blk, q_ref, f_ref, hb_ref):
    row = lax.broadcasted_iota(jnp.int32, (Q, Q), 0)
    col = lax.broadcasted_iota(jnp.int32, (Q, Q), 1)
    same = (row // HG_CHUNK) == (col // HG_CHUNK)
    causal = jnp.logical_and(same, col <= row)
    lb = _sigmoid(hb_ref[0:1, :] - hb_ref[1:2, :])
    fl = f_ref[...].astype(F32)
    sg = _sigmoid(fl)
    fg = lb + (1.0 - lb) * sg
    k = (1.0 - lb) * (1.0 - sg)
    gl = jnp.log(fg)
    G = _dot01(causal, gl, NN, "a")
    T = _dot01(same, gl, NN, "a")
    qv = q_ref[...].astype(F32)
    sq = _sigmoid(qv)
    eG = jnp.exp(G)
    eGn = jnp.exp(jnp.minimum(-G, EXP_CAP))
    eTG = jnp.exp(T - G)
    qt = qv * sq * eG
    kt = k * eGn
    kh = k * eTG
    valid = jnp.logical_or(blk > 0, row[:, :1] >= PAD)
    return dict(row=row, col=col, same=same, causal=causal, lb=lb, sg=sg, fg=fg, k=k, T=T, qv=qv, sq=sq,
                eG=eG, eGn=eGn, eTG=eTG, qt=qt, kt=kt, kh=kh, valid=valid)


def _hg_specs(nb, rev=False):
    rb = (lambda h, b, t: b * nb + nb - 1 - t) if rev else (lambda h, b, t: b * nb + t)
    w = 128 * HG_HP
    blk = lambda off: pl.BlockSpec((Q, w), lambda h, b, t, off=off: (rb(h, b, t), off // HG_HP + h))
    return [blk(24), blk(32), blk(40), blk(48),
            pl.BlockSpec((2, w), lambda h, b, t: (0, h)), pl.BlockSpec((1, w), lambda h, b, t: (0, h))]


HEAD_LANES = tuple(slice(128 * hh, 128 * (hh + 1)) for hh in range(HG_HP))


def _per_head(fn, *arrs):
    return jnp.concatenate([jnp.broadcast_to(fn(*(a[:, ln] for a in arrs)), (arrs[0].shape[0], 128))
                            for ln in HEAD_LANES], axis=1)


def _hgrn_fwd(proj, hb, nw, Bl, nb, comm=None):
    M = proj.shape[0]

    def body(q_ref, f_ref, i_ref, g_ref, hb_ref, nw_ref, y_ref, o_ref, st_ref, S):
        blk = pl.program_id(2)

        @pl.when(blk == 0)
        def _():
            S[...] = jnp.zeros_like(S)

        s = _hg_setup(blk, q_ref, f_ref, hb_ref)
        v = i_ref[...]
        qt_b, kt_b, kh_b = s["qt"].astype(BF16), s["kt"].astype(BF16), s["kh"].astype(BF16)
        eT = jnp.exp(s["T"])
        att = [jnp.where(s["causal"], _dot(qt_b[:, ln], kt_b[:, ln], NT), 0.0).astype(BF16) for ln in HEAD_LANES]
        o_intra = [_dot(att[hh], v[:, ln], NN) for hh, ln in enumerate(HEAD_LANES)]
        for j in range(NSUB):
            sl = slice(HG_CHUNK * j, HG_CHUNK * (j + 1))
            for hh, ln in enumerate(HEAD_LANES):
                St = S[hh]
                st_ref[0, hh, 0, j] = St
                o_ref[sl, ln] = o_intra[hh][sl] + _dot(qt_b[sl, ln], St.astype(BF16), NT)
                S[hh] = St * eT[HG_CHUNK * j:HG_CHUNK * j + 1, ln] + _dot(v[sl, ln], kh_b[sl, ln], TN)
        o = o_ref[...]
        r = _per_head(lambda a: lax.rsqrt(jnp.mean(a * a, axis=-1, keepdims=True) + EPS), o)
        gv = g_ref[...].astype(F32)
        y_ref[...] = (o * r * nw_ref[...] * gv * _sigmoid(gv)).astype(y_ref.dtype)

    rowblk = pl.BlockSpec((Q, 128 * HG_HP), lambda h, b, t: (b * nb + t, h))
    return _call(
        body, name="hgrn_fwd", grid=(HG_HEADS // HG_HP, Bl, nb), in_specs=_hg_specs(nb),
        out_specs=[rowblk, rowblk,
                   pl.BlockSpec((1, HG_HP, 1, NSUB, 128, 128), lambda h, b, t: (b, h, t, 0, 0, 0))],
        out_shape=[jax.ShapeDtypeStruct((M, HG_WIDTH), BF16), jax.ShapeDtypeStruct((M, HG_WIDTH), F32),
                   jax.ShapeDtypeStruct((Bl, HG_HEADS, nb, NSUB, 128, 128), F32)],
        scratch=[pltpu.VMEM((HG_HP, 128, 128), F32)], sem=("parallel", "arbitrary", "arbitrary"),
        args=(proj, proj, proj, proj, hb, nw), comm=comm)


def _hgrn_bwd(proj, hb, nw, o_saved, st_saved, dyb, Bl, nb, comm=None):
    M = proj.shape[0]

    def body(q_ref, f_ref, i_ref, g_ref, hb_ref, nw_ref, o_ref, st_ref, dy_ref,
             dq_ref, df_ref, di_ref, dg_ref, dhb_ref, dnw_ref, dS, a_dqt, a_dv, a_dkh, a_dgl):
        b, t = pl.program_id(1), pl.program_id(2)

        @pl.when(t == 0)
        def _():
            dS[...] = jnp.zeros_like(dS)

        first_step = jnp.logical_and(b == 0, t == 0)
        s = _hg_setup(nb - 1 - t, q_ref, f_ref, hb_ref)
        v = i_ref[...]
        qt_b, kt_b, kh_b = s["qt"].astype(BF16), s["kt"].astype(BF16), s["kh"].astype(BF16)
        eT = jnp.exp(s["T"])
        att = [jnp.where(s["causal"], _dot(qt_b[:, ln], kt_b[:, ln], NT), 0.0).astype(BF16) for ln in HEAD_LANES]

        o = o_ref[...]
        r = _per_head(lambda a: lax.rsqrt(jnp.mean(a * a, axis=-1, keepdims=True) + EPS), o)
        xhat = o * r
        gv = g_ref[...].astype(F32)
        sgv = _sigmoid(gv)
        dyv = dy_ref[...].astype(F32)
        d_on = dyv * gv * sgv
        dg_out = dyv * xhat * nw_ref[...] * _dsilu(gv, sgv)
        gw = d_on * nw_ref[...]
        do = r * (gw - xhat * _per_head(lambda a, c: jnp.mean(a * c, axis=-1, keepdims=True), gw, xhat))
        dnw = jnp.sum(d_on * xhat, axis=0, keepdims=True)
        do_b = do.astype(BF16)

        datt = [jnp.where(s["causal"], _dot(do_b[:, ln], v[:, ln], NT), 0.0).astype(BF16) for ln in HEAD_LANES]
        dqt = jnp.concatenate([_dot(datt[hh], kt_b[:, ln], NN) for hh, ln in enumerate(HEAD_LANES)], axis=1)
        dkt = jnp.concatenate([_dot(datt[hh], qt_b[:, ln], TN) for hh, ln in enumerate(HEAD_LANES)], axis=1)
        dv = jnp.concatenate([_dot(att[hh], do_b[:, ln], TN) for hh, ln in enumerate(HEAD_LANES)], axis=1)
        last_row = (lax.broadcasted_iota(jnp.int32, (HG_CHUNK, 128), 0) == HG_CHUNK - 1)
        for j in reversed(range(NSUB)):
            sl = slice(HG_CHUNK * j, HG_CHUNK * (j + 1))
            for hh, ln in enumerate(HEAD_LANES):
                St = st_ref[0, hh, 0, j]
                dSt = dS[hh]
                St_b, dSt_b = St.astype(BF16), dSt.astype(BF16)
                eT_j = eT[HG_CHUNK * j:HG_CHUNK * j + 1, ln]
                dkh_j = _dot(v[sl, ln], dSt_b, NN)
                a_dqt[sl, ln] = _dot(do_b[sl, ln], St_b, NN)
                a_dv[sl, ln] = _dot(kh_b[sl, ln], dSt_b, NT)
                a_dkh[sl, ln] = dkh_j
                dlast = (jnp.sum(St * dSt, axis=0, keepdims=True) * eT_j
                         + jnp.sum(dkh_j * s["kh"][sl, ln], axis=0, keepdims=True))
                a_dgl[sl, ln] = jnp.where(last_row, dlast, 0.0)
                dS[hh] = dSt * eT_j + _dot(do_b[sl, ln], qt_b[sl, ln], TN)
        dqt = dqt + a_dqt[...]
        dv = dv + a_dv[...]
        dkh = a_dkh[...]
        dG = dqt * s["qt"] - dkt * s["kt"] - dkh * s["kh"] + a_dgl[...]
        rev_causal = jnp.logical_and(s["same"], s["col"] >= s["row"])
        dgl = _dot01(rev_causal, dG, NN, "a")
        dk = dkt * s["eGn"] + dkh * s["eTG"]
        dfg = dgl / s["fg"] - dk
        lb, sg = s["lb"], s["sg"]
        keep = s["valid"].astype(F32)
        df_ref[...] = (dfg * (1.0 - lb) * sg * (1.0 - sg) * keep).astype(df_ref.dtype)
        dq_ref[...] = (dqt * s["eG"] * _dsilu(s["qv"], s["sq"]) * keep).astype(dq_ref.dtype)
        di_ref[...] = (dv * keep).astype(di_ref.dtype)
        dg_ref[...] = (dg_out * keep).astype(dg_ref.dtype)
        dlb = jnp.sum(dfg * (1.0 - sg) * keep, axis=0, keepdims=True) * lb * (1.0 - lb)
        dhb = jnp.concatenate([dlb, -dlb], axis=0)

        @pl.when(first_step)
        def _():
            dhb_ref[...] = dhb
            dnw_ref[...] = dnw

        @pl.when(jnp.logical_not(first_step))
        def _():
            dhb_ref[...] += dhb
            dnw_ref[...] += dnw

    w = 128 * HG_HP
    rowblk = pl.BlockSpec((Q, w), lambda h, b, t: (b * nb + nb - 1 - t, h))
    return _call(
        body, name="hgrn_bwd", grid=(HG_HEADS // HG_HP, Bl, nb),
        in_specs=_hg_specs(nb, rev=True) + [
            rowblk, pl.BlockSpec((1, HG_HP, 1, NSUB, 128, 128), lambda h, b, t: (b, h, nb - 1 - t, 0, 0, 0)), rowblk],
        out_specs=[rowblk, rowblk, rowblk, rowblk,
                   pl.BlockSpec((2, w), lambda h, b, t: (0, h)), pl.BlockSpec((1, w), lambda h, b, t: (0, h))],
        out_shape=[jax.ShapeDtypeStruct((M, HG_WIDTH), BF16)] * 4 + [
            jax.ShapeDtypeStruct((2, HG_WIDTH), F32), jax.ShapeDtypeStruct((1, HG_WIDTH), F32)],
        scratch=[pltpu.VMEM((HG_HP, 128, 128), F32)] + [pltpu.VMEM((Q, w), F32)] * 4,
        sem=("parallel", "arbitrary", "arbitrary"),
        args=(proj, proj, proj, proj, hb, nw, o_saved, st_saved, dyb), comm=comm)


def _adamw(name, parts, w, m, v):
    R, C = w.shape
    S = parts.shape[0]
    tr, tc = (_tile(R, (256, 176, 128, 64, 8)), C) if R % 8 == 0 else (R, 256)
    c1, c2 = 1.0 - ADAM_B1 ** ADAM_STEP, 1.0 - ADAM_B2 ** ADAM_STEP

    def body(p_ref, w_ref, m_ref, v_ref, g_ref, d_ref, nm_ref, nv_ref):
        g = p_ref[0].astype(F32)
        for s in range(1, S):
            g = g + p_ref[s].astype(F32)
        nm = ADAM_B1 * m_ref[...] + (1.0 - ADAM_B1) * g
        nv = ADAM_B2 * v_ref[...] + (1.0 - ADAM_B2) * (g * g)
        g_ref[...] = g
        nm_ref[...] = nm
        nv_ref[...] = nv
        d_ref[...] = -ADAM_LR * ((nm / c1) / (jnp.sqrt(nv / c2) + ADAM_EPS) + ADAM_WD * w_ref[...])

    blk = pl.BlockSpec((tr, tc), lambda i, j: (i, j))
    return pl.pallas_call(
        body, name=name, grid=(R // tr, C // tc),
        in_specs=[pl.BlockSpec((S, tr, tc), lambda i, j: (0, i, j)), blk, blk, blk], out_specs=[blk] * 4,
        out_shape=[jax.ShapeDtypeStruct((R, C), F32)] * 4, compiler_params=_params(("parallel", "parallel")),
    )(parts, w, m, v)


def _pair_sum(name, by_core, arrived):
    _, J, R, C = by_core.shape
    tc = _tile(C, (512, 256, 128))

    def body(c_ref, a_ref, b_ref, o_ref):
        o_ref[...] = (a_ref[0].astype(F32) + b_ref[...].astype(F32)).astype(o_ref.dtype)

    blk = pl.BlockSpec((1, R, tc), lambda j, k, c_ref: (j, 0, k))
    return pl.pallas_call(
        body, name=name,
        grid_spec=pltpu.PrefetchScalarGridSpec(
            num_scalar_prefetch=1, grid=(J, C // tc),
            in_specs=[pl.BlockSpec((1, 1, R, tc), lambda j, k, c_ref: (c_ref[0], j, 0, k)), blk], out_specs=blk),
        out_shape=jax.ShapeDtypeStruct(arrived.shape, arrived.dtype), compiler_params=_params(("parallel", "parallel")),
    )(lax.axis_index("c").astype(jnp.int32).reshape(1), by_core, arrived)


def _sum_parts(name, parts):
    S, R, C = parts.shape

    def body(p_ref, o_ref):
        g = p_ref[0]
        for s in range(1, S):
            g = g + p_ref[s]
        o_ref[...] = g

    return pl.pallas_call(
        body, name=name, out_shape=jax.ShapeDtypeStruct((R, C), F32),
        in_specs=[pl.BlockSpec(memory_space=pltpu.VMEM)], out_specs=pl.BlockSpec(memory_space=pltpu.VMEM),
    )(parts)


def _heads_to_lanes(p):
    return jnp.pad(p, [(0, 0)] * (p.ndim - 1) + [(0, 128 - SSD_HEADS)])


def _lanes_to_heads(p):
    return p[..., :SSD_HEADS]


def _pack_rows(arrs):
    rows = []
    for a in arrs:
        f = a.reshape(-1).astype(F32)
        n = -(-f.shape[0] // D_MODEL) * D_MODEL
        rows.append(jnp.pad(f, (0, n - f.shape[0])).reshape(-1, D_MODEL))
    out = jnp.concatenate(rows, axis=0)
    return jnp.pad(out, ((0, (-out.shape[0]) % 8), (0, 0)))


def _unpack_rows(packed, like):
    outs, r = [], 0
    for a in like:
        n = 1
        for s in a.shape:
            n *= s
        nr = -(-n // D_MODEL)
        outs.append(packed[r:r + nr].reshape(-1)[:n].reshape(a.shape))
        r += nr
    return outs


def _cols(gth):
    return jnp.transpose(gth, (1, 0, 2)).reshape(gth.shape[1], -1)


def _rows(gth):
    return gth.reshape(-1, gth.shape[2])


def _to_rows(g):
    return g.reshape(N_DEV, -1, g.shape[1]).astype(BF16)


def _by_core(g):
    return jnp.transpose(g.reshape(N_DEV // 2, 2, -1, g.shape[1]), (1, 0, 2, 3)).astype(BF16)


DT_ROW = 3072


def _win_split(shards):
    win_t = _rows(shards)
    return jnp.concatenate([win_t[:DT_ROW], win_t[DT_ROW + SSD_HEADS:]], axis=0), win_t[DT_ROW:DT_ROW + SSD_HEADS]


def _win_by_core(d_main, d_dt):
    return _by_core(jnp.concatenate([d_main[:DT_ROW], d_dt, d_main[DT_ROW:]], axis=0))


def _chip_sums(tag, by_core, swap_in=None):
    arrived = swap_in(by_core) if swap_in else _exchange(tag + "_swap", "swap", by_core)
    return [_pair_sum(f"{tag}_chipsum{i}", m, a) for i, (m, a) in enumerate(zip(by_core, arrived))]


def _ffn_fwd_gu(tag, h, norm_w, w_gu_t, comm=None):
    M = h.shape[0]
    F = w_gu_t.shape[0] // 2
    tm = _tile(M, (544, 256))
    n = _rmsnorm_fwd(tag + "_norm", h, norm_w)
    tn = _tile(F, (1408, 704, 256))
    outs = _fused_matmul(
        tag + "_gu", M, F, D_MODEL,
        [dict(a=n, b=w_gu_t, trans_b=True, acc=0), dict(a=n, b=w_gu_t, trans_b=True, bn_off=F // tn, acc=1)], [],
        lambda accs, ex: (accs[0], accs[1], accs[0] * _sigmoid(accs[0]) * accs[1]),
        [BF16, BF16, BF16], 2, tm, tn, D_MODEL, outer="j", comm=comm)
    return (n, *outs[:3]), outs[3:]


def _ffn_fwd_down(tag, h, a, w_down):
    M = h.shape[0]
    F = w_down.shape[0]
    (h_out,) = _fused_matmul(
        tag + "_down", M, D_MODEL, F, [dict(a=a, b=w_down, acc=0)], [(h, 0)],
        lambda accs, ex: (ex[0] + 0.5 * accs[0],), [F32], 1, _tile(M, (544, 256)), D_MODEL, F, outer="j")
    return h_out


def _ffn_bwd(tag, dh, dh_b, h, norm_w, w_gu_t, w_down, saved, scatter=False):
    n, g, u, a = saved
    M = h.shape[0]
    F = w_down.shape[0]
    tm = _tile(M, (544, 256))
    tn = _tile(F, (1408, 704, 256))

    def swiglu_bwd(accs, ex):
        da, gv, uv = 0.5 * accs[0], ex[0].astype(F32), ex[1].astype(F32)
        s = _sigmoid(gv)
        return da * uv * _dsilu(gv, s), da * gv * s

    (dgu,) = _fused_matmul(
        tag + "_dact", M, F, D_MODEL, [dict(a=dh_b, b=w_down, trans_b=True, acc=0)], [(g, 0), (u, 0)],
        swiglu_bwd, [BF16, BF16], 1, tm, tn, D_MODEL, outer="j", stack=True)
    tr = _tile(M, (2176, 256))
    (dw_down,) = _matmul_tn(tag + "_dwd", a, dh_b, tn, D_MODEL, tr, scale=0.5)
    dw_gu_t, *p_down = _matmul_tn(tag + "_dwgu", dgu, n, tn, D_MODEL, tr,
                                  comm=("scatter", [_to_rows(dw_down)]) if scatter else None)
    comm = None
    if scatter:
        comm = ("chips", _chip_sums(tag + "_wgu", [_by_core(dw_gu_t)]))
    dn, *p_gu = _fused_matmul(
        tag + "_dn", M, D_MODEL, F,
        [dict(a=dgu, a_lead=0, b=w_gu_t, acc=0), dict(a=dgu, a_lead=1, b=w_gu_t, bk_off=1, acc=0)], [],
        lambda accs, ex: (accs[0],), [F32], 1, tm, D_MODEL, F, outer="i", comm=comm)
    dh_prev, dh_prev_b, dnorm = _rmsnorm_bwd(tag + "_dnorm", dn, h, norm_w, dh)
    return (dh_prev, dh_prev_b, dnorm, *((p_gu[0], p_down[0]) if scatter else (dw_gu_t, dw_down)))


def kernel(x, meta_tokens, ffn1_norm, ffn1_w_gu, ffn1_w_down, mix_norm, w_in, ssd_conv_w, ssd_conv_b, ssd_dt_bias, ssd_a_log, ssd_d, ssd_norm, hg_lower_bound, hg_norm, w_branch_a, w_branch_b, w_out, ffn2_norm, ffn2_w_gu, ffn2_w_down, final_norm, loss_target, m_meta_tokens, m_ffn1_norm, m_ffn1_w_gu, m_ffn1_w_down, m_mix_norm, m_w_in, m_ssd_conv_w, m_ssd_conv_b, m_ssd_dt_bias, m_ssd_a_log, m_ssd_d, m_ssd_norm, m_hg_lower_bound, m_hg_norm, m_w_branch_a, m_w_branch_b, m_w_out, m_ffn2_norm, m_ffn2_w_gu, m_ffn2_w_down, m_final_norm, v_meta_tokens, v_ffn1_norm, v_ffn1_w_gu, v_ffn1_w_down, v_mix_norm, v_w_in, v_ssd_conv_w, v_ssd_conv_b, v_ssd_dt_bias, v_ssd_a_log, v_ssd_d, v_ssd_norm, v_hg_lower_bound, v_hg_norm, v_w_branch_a, v_w_branch_b, v_w_out, v_ffn2_norm, v_ffn2_w_gu, v_ffn2_w_down, v_final_norm):
    Bl, S, D = x.shape
    T = PAD + N_META + S
    nc = T // Q
    M = Bl * T
    me = 4 * lax.axis_index("x") + 2 * lax.axis_index("y") + lax.axis_index("c")

    bf = lambda a: a[0].astype(BF16)
    bft = lambda a: a[0].T.astype(BF16)
    g_wgu1, g_meta, g_conv_w = _exchange("gather_first", "gather", [bft(ffn1_w_gu), meta_tokens, ssd_conv_w[0]])
    wgu1, meta_full, conv_w_full = _rows(g_wgu1), _cols(g_meta), _cols(g_conv_w)
    bias_p, alog_p, d_p = _heads_to_lanes(ssd_dt_bias), _heads_to_lanes(ssd_a_log), _heads_to_lanes(ssd_d)
    final_w = final_norm.reshape(1, D)

    h0 = jnp.concatenate([jnp.zeros((Bl, PAD, D), F32), jnp.broadcast_to(meta_full[None], (Bl, N_META, D)), x],
                         axis=1).reshape(M, D)
    tm = _tile(M, (544, 256))
    ffn1_saved, (g_wd1, g_win) = _ffn_fwd_gu("ffn1", h0, ffn1_norm, wgu1, comm=("gather", [bf(ffn1_w_down), bft(w_in)]))
    wd1 = _rows(g_wd1)
    win_main, win_dt16 = _win_split(g_win)
    win_dt = jnp.pad(win_dt16, ((0, 128 - SSD_HEADS), (0, 0)))
    h1 = _ffn_fwd_down("ffn1", h0, ffn1_saved[3], wd1)
    un = _rmsnorm_fwd("mix_norm", h1, mix_norm)
    plain = lambda accs, ex: (accs[0],)
    proj, g_wa, g_wb, g_wo = _fused_matmul(
        "in_proj", M, N_MAIN, D, [dict(a=un, b=win_main, trans_b=True, acc=0)], [], plain, [BF16], 1, tm, 1536, D,
        outer="j", comm=("gather", [bf(w_branch_a), bf(w_branch_b), bf(w_out)]))
    wa, wb, wo = _rows(g_wa), _rows(g_wb), _rows(g_wo)
    (dtr,) = _fused_matmul("in_proj_dt", M, 128, D, [dict(a=un, b=win_dt, trans_b=True, acc=0)], [], plain, [F32], 1,
                           tm, 128, D, outer="j")
    xc = _conv_fwd(proj, conv_w_full, ssd_conv_b, Bl, T)
    ya, ssd_prev = _ssd_fwd(xc, dtr, proj, bias_p, alog_p, d_p, ssd_norm, Bl, nc)
    yb, hg_o, hg_st, g_wgu2, g_wd2 = _hgrn_fwd(proj, hg_lower_bound, hg_norm, Bl, nc,
                                               comm=("gather", [bft(ffn2_w_gu), bf(ffn2_w_down)]))
    wgu2, wd2 = _rows(g_wgu2), _rows(g_wd2)

    def branch_fwd(accs, ex):
        pa, pb = accs
        return pa, pb, _sigmoid(ex[0].astype(F32)) * pa + _sigmoid(ex[1].astype(F32)) * pb

    pa, pb, merged = _fused_matmul(
        "branches", M, D, D, [dict(a=ya, b=wa, acc=0), dict(a=yb, b=wb, acc=1)], [(proj, 7), (proj, 8)],
        branch_fwd, [BF16, BF16, BF16], 2, tm, D, D, outer="j")
    (h2,) = _fused_matmul("out_proj", M, D, D, [dict(a=merged, b=wo, acc=0)], [(h1, 0)],
                          lambda accs, ex: (ex[0] + accs[0],), [F32], 1, tm, D, D, outer="j")
    ffn2_saved, _ = _ffn_fwd_gu("ffn2", h2, ffn2_norm, wgu2)
    h3 = _ffn_fwd_down("ffn2", h2, ffn2_saved[3], wd2)

    dh3, dh3_b, d_final, loss_part = _loss_head(h3, final_w, loss_target, Bl, nc)
    dh2, dh2_b, d_ffn2_norm, d_wgu2, d_wd2 = _ffn_bwd("ffn2", dh3, dh3_b, h2, ffn2_norm, wgu2, wd2, ffn2_saved)

    def branch_bwd(accs, ex):
        dm = accs[0]
        ga, gb, pav, pbv = (e.astype(F32) for e in ex)
        sa, sb = _sigmoid(ga), _sigmoid(gb)
        return dm * sa, dm * sb, dm * pav * sa * (1.0 - sa), dm * pbv * sb * (1.0 - sb)

    d_merged_outs = []

    def d_merged_with_swap(theirs):
        d_merged_outs.extend(_fused_matmul(
            "d_merged", M, D, D, [dict(a=dh2_b, b=wo, trans_b=True, acc=0)], [(proj, 7), (proj, 8), (pa, 0), (pb, 0)],
            branch_bwd, [BF16] * 4, 1, tm, D, D, outer="j", comm=("swap", theirs)))
        return d_merged_outs[4:]

    s_ffn2 = _chip_sums("ffn2", [_by_core(d_wgu2), _by_core(d_wd2)], swap_in=d_merged_with_swap)
    dpa, dpb, dga, dgb = d_merged_outs[:4]
    (d_wo,) = _matmul_tn("d_w_out", merged, dh2_b, 512, D, M)
    (d_wa,) = _matmul_tn("d_w_a", ya, dpa, 512, D, M)
    (d_wb,) = _matmul_tn("d_w_b", yb, dpb, 512, D, M)
    dya, dyb = _fused_matmul(
        "d_branches", M, D, D, [dict(a=dpa, b=wa, trans_b=True, acc=0), dict(a=dpb, b=wb, trans_b=True, acc=1)], [],
        lambda accs, ex: (accs[0], accs[1]), [BF16, BF16], 2, tm, D, D, outer="j")
    *ssd_grads, p_wgu2, p_wd2 = _ssd_bwd(xc, dtr, proj, bias_p, alog_p, d_p, ssd_norm, ssd_prev, dya, Bl, nc,
                                         comm=("chips", s_ffn2))
    dxc, dz, ddtr, d_bias_p, d_alog_p, d_d_p, d_ssd_norm = ssd_grads
    dxbc, d_conv_w, d_conv_b = _conv_bwd(proj, conv_w_full, ssd_conv_b, dxc, Bl, T)
    dq, df, di, dg, d_hb, d_hg_norm, p_wa, p_wb, p_wo = _hgrn_bwd(
        proj, hg_lower_bound, hg_norm, hg_o, hg_st, dyb, Bl, nc,
        comm=("scatter", [_to_rows(d_wa), _to_rows(d_wb), _to_rows(d_wo)]))
    dproj = jnp.concatenate([dz, dxbc, dq, df, di, dg, dga, dgb], axis=1)
    ddtr_b = ddtr.astype(BF16)
    (d_win_main,) = _matmul_tn("d_w_in", dproj, un, 768, D, M)
    (d_win_dt,) = _matmul_tn("d_w_in_dt", ddtr_b, un, 128, D, M)
    d_un_dt_outs = []

    def d_un_dt_with_swap(theirs):
        d_un_dt_outs.extend(_fused_matmul("d_un_dt", M, D, 128, [dict(a=ddtr_b, b=win_dt, acc=0)], [], plain, [F32], 1,
                                          tm, D, 128, outer="j", comm=("swap", theirs)))
        return d_un_dt_outs[1:]

    s_win = _chip_sums("w_in", [_win_by_core(d_win_main, d_win_dt[:SSD_HEADS])], swap_in=d_un_dt_with_swap)
    dun, p_win = _fused_matmul("d_un", M, D, N_MAIN, [dict(a=dproj, b=win_main, acc=0)], [(d_un_dt_outs[0], 0)],
                               lambda accs, ex: (accs[0] + ex[0],), [F32], 1, tm, D, 3072, outer="i",
                               comm=("chips", s_win))
    dh1, dh1_b, d_mix_norm = _rmsnorm_bwd("d_mix_norm", dun, h1, mix_norm, dh2)
    dh0, _, d_ffn1_norm, p_wgu1, p_wd1 = _ffn_bwd("ffn1", dh1, dh1_b, h0, ffn1_norm, wgu1, wd1, ffn1_saved, scatter=True)

    dh0 = dh0.reshape(Bl, T, D)
    grad_x = dh0[:, PAD + N_META:]
    d_meta = dh0[:, PAD:PAD + N_META]

    small_grads = [d_ffn1_norm, d_mix_norm, d_conv_b, _lanes_to_heads(d_bias_p), _lanes_to_heads(d_alog_p),
                   _lanes_to_heads(d_d_p), d_ssd_norm, d_hb, d_hg_norm, d_ffn2_norm, d_final.reshape(D), d_conv_w]
    small_packed = _pack_rows(small_grads + [d_meta[b] for b in range(Bl)])
    parts = [p_wgu1, p_wd1, p_win, p_wa, p_wb, p_wo, p_wgu2, p_wd2]
    (small_all,) = _exchange("gather_small_grads", "gather", [small_packed])
    small_sum = _sum_parts("sum_small_grads", small_all)
    unpacked = _unpack_rows(small_sum, small_grads + [d_meta[b] for b in range(Bl)])
    g_small = unpacked[:len(small_grads)]
    g_meta_full = unpacked[len(small_grads)]
    for b in range(1, Bl):
        g_meta_full = g_meta_full + unpacked[len(small_grads) + b]
    g_meta = lax.dynamic_slice_in_dim(g_meta_full, me * (D // N_DEV), D // N_DEV, axis=1)
    g_conv_w = lax.dynamic_slice_in_dim(g_small[11], me * (SSD_CONV_CH // N_DEV), SSD_CONV_CH // N_DEV, axis=1)

    names = ["meta_tokens", "ffn1_norm", "ffn1_w_gu", "ffn1_w_down", "mix_norm", "w_in", "ssd_conv_w", "ssd_conv_b",
             "ssd_dt_bias", "ssd_a_log", "ssd_d", "ssd_norm", "hg_lower_bound", "hg_norm", "w_branch_a", "w_branch_b",
             "w_out", "ffn2_norm", "ffn2_w_gu", "ffn2_w_down", "final_norm"]
    W = dict(meta_tokens=meta_tokens, ffn1_norm=ffn1_norm, ffn1_w_gu=ffn1_w_gu, ffn1_w_down=ffn1_w_down, mix_norm=mix_norm,
             w_in=w_in, ssd_conv_w=ssd_conv_w, ssd_conv_b=ssd_conv_b, ssd_dt_bias=ssd_dt_bias, ssd_a_log=ssd_a_log,
             ssd_d=ssd_d, ssd_norm=ssd_norm, hg_lower_bound=hg_lower_bound, hg_norm=hg_norm, w_branch_a=w_branch_a,
             w_branch_b=w_branch_b, w_out=w_out, ffn2_norm=ffn2_norm, ffn2_w_gu=ffn2_w_gu, ffn2_w_down=ffn2_w_down,
             final_norm=final_norm)
    Mo = dict(meta_tokens=m_meta_tokens, ffn1_norm=m_ffn1_norm, ffn1_w_gu=m_ffn1_w_gu, ffn1_w_down=m_ffn1_w_down,
              mix_norm=m_mix_norm, w_in=m_w_in, ssd_conv_w=m_ssd_conv_w, ssd_conv_b=m_ssd_conv_b, ssd_dt_bias=m_ssd_dt_bias,
              ssd_a_log=m_ssd_a_log, ssd_d=m_ssd_d, ssd_norm=m_ssd_norm, hg_lower_bound=m_hg_lower_bound, hg_norm=m_hg_norm,
              w_branch_a=m_w_branch_a, w_branch_b=m_w_branch_b, w_out=m_w_out, ffn2_norm=m_ffn2_norm, ffn2_w_gu=m_ffn2_w_gu,
              ffn2_w_down=m_ffn2_w_down, final_norm=m_final_norm)
    Vo = dict(meta_tokens=v_meta_tokens, ffn1_norm=v_ffn1_norm, ffn1_w_gu=v_ffn1_w_gu, ffn1_w_down=v_ffn1_w_down,
              mix_norm=v_mix_norm, w_in=v_w_in, ssd_conv_w=v_ssd_conv_w, ssd_conv_b=v_ssd_conv_b, ssd_dt_bias=v_ssd_dt_bias,
              ssd_a_log=v_ssd_a_log, ssd_d=v_ssd_d, ssd_norm=v_ssd_norm, hg_lower_bound=v_hg_lower_bound, hg_norm=v_hg_norm,
              w_branch_a=v_w_branch_a, w_branch_b=v_w_branch_b, w_out=v_w_out, ffn2_norm=v_ffn2_norm, ffn2_w_gu=v_ffn2_w_gu,
              ffn2_w_down=v_ffn2_w_down, final_norm=v_final_norm)
    grads, deltas, new_m, new_v = {}, {}, {}, {}
    big_names = ["ffn1_w_gu", "ffn1_w_down", "w_in", "w_branch_a", "w_branch_b", "w_out", "ffn2_w_gu", "ffn2_w_down"]
    transposed = ("ffn1_w_gu", "ffn2_w_gu", "w_in")
    for nm, part in zip(big_names, parts):
        view = (lambda a: a[0].T) if nm in transposed else (lambda a: a[0])
        back = (lambda o: o.T[None]) if nm in transposed else (lambda o: o[None])
        outs = _adamw("adamw_" + nm, part, view(W[nm]), view(Mo[nm]), view(Vo[nm]))
        grads[nm], deltas[nm], new_m[nm], new_v[nm] = (back(o) for o in outs)
    small_names = ["ffn1_norm", "mix_norm", "ssd_conv_b", "ssd_dt_bias", "ssd_a_log", "ssd_d", "ssd_norm", "hg_lower_bound",
                   "hg_norm", "ffn2_norm", "final_norm", "ssd_conv_w", "meta_tokens"]
    small_g = g_small[:11] + [g_conv_w.reshape(ssd_conv_w.shape), g_meta]
    pk = lambda d: _pack_rows([d[nm] for nm in small_names])
    outs = _adamw("adamw_small", _pack_rows(small_g)[None], pk(W), pk(Mo), pk(Vo))
    like = [W[nm] for nm in small_names]
    for dst, o in zip((grads, deltas, new_m, new_v), outs):
        for nm, val in zip(small_names, _unpack_rows(o, like)):
            dst[nm] = val

    loss = lax.psum(loss_part[0, 0], MESH_AXES)
    return (loss, grad_x, *[grads[nm] for nm in names], *[deltas[nm] for nm in names],
            *[new_m[nm] for nm in names], *[new_v[nm] for nm in names])
```

```python
import functools

import jax
import jax.numpy as jnp
from jax import lax
from jax.experimental import pallas as pl
from jax.experimental.pallas import tpu as pltpu

F32, BF16 = jnp.float32, jnp.bfloat16
NN, NT, TN = ((1,), (0,)), ((1,), (1,)), ((0,), (0,))
MESH_AXES = ("x", "y", "c")
N_DEV = 8

D_MODEL = 1024
N_META = 16
EPS = 1e-6
SSD_HEADS, SSD_HEAD_DIM, SSD_GROUPS, SSD_STATE, SSD_CONV, Q = 16, 64, 4, 128, 4, 128
SSD_INNER = SSD_HEADS * SSD_HEAD_DIM
SSD_CONV_CH = SSD_INNER + 2 * SSD_GROUPS * SSD_STATE
HG_WIDTH, HG_HEADS, HG_CHUNK = 1024, 8, 16
PAD = Q - N_META
N_MAIN = 9 * 1024
ADAM_LR, ADAM_B1, ADAM_B2, ADAM_EPS, ADAM_WD, ADAM_STEP = 0.001, 0.9, 0.999, 1e-08, 0.01, 10
VMEM_LIMIT = 52 * 1024 * 1024


def _dot(a, b, dims, prec=None):
    return lax.dot_general(a, b, (dims, ((), ())), precision=prec, preferred_element_type=F32)


def _dot01(a, b, dims, sel):
    x = b if sel == "a" else a
    hi = x.astype(BF16)
    r1 = x - hi.astype(F32)
    mid = r1.astype(BF16)
    lo = (r1 - mid.astype(F32)).astype(BF16)
    s = (a if sel == "a" else b).astype(BF16)
    parts = [_dot(s, p, dims) if sel == "a" else _dot(p, s, dims) for p in (hi, mid, lo)]
    return parts[0] + parts[1] + parts[2]


def _sigmoid(x):
    return 1.0 / (1.0 + jnp.exp(-x))


def _dsilu(x, s):
    return s * (1.0 + x * (1.0 - s))


def _softplus(x):
    e = jnp.exp(-jnp.abs(x))
    u = 1.0 + e
    log1p_e = jnp.where(u == 1.0, e, jnp.log(u) * e / (u - 1.0))
    return jnp.maximum(x, 0.0) + log1p_e


def _params(sem):
    return pltpu.CompilerParams(dimension_semantics=sem, vmem_limit_bytes=VMEM_LIMIT)


def _tile(n, prefs):
    for p in prefs:
        if n % p == 0:
            return p
    return n


CHIP_FLIPS = ((1, 0), (0, 1), (1, 1))
N_PEER = N_DEV - 1


def _comm_gather(srcs, outs, send_sems, recv_sems, local_sems):
    n = len(srcs)
    x, y, c = (lax.axis_index(a) for a in MESH_AXES)
    dev = lambda px, py, pc: 4 * px + 2 * py + pc
    me, sib = dev(x, y, c), (x, y, 1 - c)

    def rc(w, k, slot, to, src=None):
        return pltpu.make_async_remote_copy(
            src_ref=outs[w].at[slot] if src is None else src, dst_ref=outs[w].at[slot],
            send_sem=send_sems.at[w, k], recv_sem=recv_sems.at[w, k], device_id=to, device_id_type=pl.DeviceIdType.MESH)

    def local(w):
        return pltpu.make_async_copy(srcs[w], outs[w].at[me], local_sems.at[w])

    def start():
        for w in range(n):
            local(w).start()
            rc(w, 0, me, sib, src=srcs[w]).start()
            for j, (fx, fy) in enumerate(CHIP_FLIPS):
                rc(w, 1 + j, me, (x ^ fx, y ^ fy, c), src=srcs[w]).start()

    def finish():
        for w in range(n):
            for j, (fx, fy) in enumerate(CHIP_FLIPS):
                slot = dev(x ^ fx, y ^ fy, c)
                rc(w, 1 + j, slot, sib).wait_recv()
                rc(w, 4 + j, slot, sib).start()
        for w in range(n):
            rc(w, 0, dev(x, y, 1 - c), sib).wait_recv()
            rc(w, 0, me, sib, src=srcs[w]).wait_send()
            for j, (fx, fy) in enumerate(CHIP_FLIPS):
                rc(w, 4 + j, dev(x ^ fx, y ^ fy, 1 - c), sib).wait_recv()
                rc(w, 1 + j, me, sib, src=srcs[w]).wait_send()
                rc(w, 4 + j, dev(x ^ fx, y ^ fy, c), sib).wait_send()
            local(w).wait()

    return start, finish


def _comm_scatter(srcs, outs, send_sems, recv_sems, local_sems):
    n = len(srcs)
    x, y, c = (lax.axis_index(a) for a in MESH_AXES)
    me = 4 * x + 2 * y + c

    def copies():
        out = []
        for w in range(n):
            out.append(pltpu.make_async_copy(srcs[w].at[me], outs[w].at[me], local_sems.at[w]))
            for k in range(1, N_DEV):
                px, py, pc = x ^ (k >> 2), y ^ ((k >> 1) & 1), c ^ (k & 1)
                out.append(pltpu.make_async_remote_copy(
                    src_ref=srcs[w].at[4 * px + 2 * py + pc], dst_ref=outs[w].at[me],
                    send_sem=send_sems.at[w, k - 1], recv_sem=recv_sems.at[w, k - 1],
                    device_id=(px, py, pc), device_id_type=pl.DeviceIdType.MESH))
        return out

    def start():
        for cp in copies():
            cp.start()

    def finish():
        for cp in copies():
            cp.wait()

    return start, finish


def _comm_swap(srcs, outs, send_sems, recv_sems, local_sems):
    x, y, c = (lax.axis_index(a) for a in MESH_AXES)

    def copies():
        return [pltpu.make_async_remote_copy(
            src_ref=srcs[w].at[1 - c], dst_ref=outs[w], send_sem=send_sems.at[w, 0], recv_sem=recv_sems.at[w, 0],
            device_id=(x, y, 1 - c), device_id_type=pl.DeviceIdType.MESH) for w in range(len(srcs))]

    def start():
        for cp in copies():
            cp.start()

    def finish():
        for cp in copies():
            cp.wait()

    return start, finish


def _comm_chips(srcs, outs, send_sems, recv_sems, local_sems):
    n = len(srcs)
    x, y, c = (lax.axis_index(a) for a in MESH_AXES)
    mine = 2 * x + y

    def copies():
        out = []
        for w in range(n):
            out.append(pltpu.make_async_copy(srcs[w].at[mine], outs[w].at[mine], local_sems.at[w]))
            for j, (fx, fy) in enumerate(CHIP_FLIPS):
                px, py = x ^ fx, y ^ fy
                out.append(pltpu.make_async_remote_copy(
                    src_ref=srcs[w].at[2 * px + py], dst_ref=outs[w].at[mine],
                    send_sem=send_sems.at[w, j], recv_sem=recv_sems.at[w, j],
                    device_id=(px, py, c), device_id_type=pl.DeviceIdType.MESH))
        return out

    def start():
        for cp in copies():
            cp.start()

    def finish():
        for cp in copies():
            cp.wait()

    return start, finish


def _comm_parts(comm):
    kind, arrays = comm
    n = len(arrays)
    lead = {"gather": lambda a: (N_DEV,) + a.shape, "scatter": lambda a: (N_DEV,) + a.shape[1:],
            "swap": lambda a: a.shape[1:], "chips": lambda a: a.shape}[kind]
    shapes = [jax.ShapeDtypeStruct(lead(a), a.dtype) for a in arrays]
    sems = [pltpu.SemaphoreType.DMA((n, N_PEER)), pltpu.SemaphoreType.DMA((n, N_PEER)), pltpu.SemaphoreType.DMA((n,))]
    make = {"gather": _comm_gather, "scatter": _comm_scatter, "swap": _comm_swap, "chips": _comm_chips}[kind]
    return n, shapes, sems, make


def _exchange(name, kind, arrays):
    n, shapes, sems, make = _comm_parts((kind, arrays))

    def body(*refs):
        start, finish = make(refs[:n], refs[n:2 * n], *refs[2 * n:])
        start()
        finish()

    any_spec = pl.BlockSpec(memory_space=pl.ANY)
    return pl.pallas_call(
        body, name=name, in_specs=[any_spec] * n, out_specs=[any_spec] * n, out_shape=shapes, scratch_shapes=sems,
        compiler_params=pltpu.CompilerParams(has_side_effects=True),
    )(*arrays)


def _call(body, *, name, grid, in_specs, out_specs, out_shape, scratch, sem, args, comm=None):
    if comm is None:
        return pl.pallas_call(body, name=name, grid=grid, in_specs=in_specs, out_specs=out_specs, out_shape=out_shape,
                              scratch_shapes=scratch, compiler_params=_params(sem))(*args)
    n, shapes, sems, make = _comm_parts(comm)
    n_in, n_out, n_scr = len(in_specs), len(out_specs), len(scratch)

    def carrier(*refs):
        ins, csrc = refs[:n_in], refs[n_in:n_in + n]
        outs, cout = refs[n_in + n:n_in + n + n_out], refs[n_in + n + n_out:n_in + 2 * n + n_out]
        rest = refs[n_in + 2 * n + n_out:]
        start, finish = make(csrc, cout, *rest[n_scr:])
        ids = [pl.program_id(a) for a in range(len(grid))]
        first = functools.reduce(jnp.logical_and, [i == 0 for i in ids])
        last = functools.reduce(jnp.logical_and, [i == g - 1 for i, g in zip(ids, grid)])
        pl.when(first)(start)
        body(*ins, *outs, *rest[:n_scr])
        pl.when(last)(finish)

    any_spec = pl.BlockSpec(memory_space=pl.ANY)
    return pl.pallas_call(
        carrier, name=name, grid=grid, in_specs=list(in_specs) + [any_spec] * n,
        out_specs=list(out_specs) + [any_spec] * n, out_shape=list(out_shape) + shapes,
        scratch_shapes=list(scratch) + sems,
        compiler_params=pltpu.CompilerParams(dimension_semantics=("arbitrary",) * len(grid),
                                             vmem_limit_bytes=VMEM_LIMIT, has_side_effects=True),
    )(*args, *comm[1])


def _fused_matmul(name, M, N, K, pairs, extras, epilogue, out_dtypes, n_acc, tm, tn, tk, outer="i", comm=None,
                  stack=False, vecs=(), row_sums=0):
    nk = K // tk
    n_pairs, n_ex, n_out = len(pairs), len(extras), len(out_dtypes)
    assert not row_sums or (outer == "i" and N == tn)

    def ij(g0, g1):
        return (g0, g1) if outer == "i" else (g1, g0)

    in_specs, args = [], []
    for p in pairs:
        ao, bk, bn = p.get("a_off", 0), p.get("bk_off", 0), p.get("bn_off", 0)
        mode = dict(pipeline_mode=pl.Buffered(1)) if p.get("resident") else {}
        if "a_lead" in p:
            in_specs.append(pl.BlockSpec((None, tm, tk),
                                         lambda g0, g1, k, ao=ao, ld=p["a_lead"]: (ld, ij(g0, g1)[0], k + ao)))
        else:
            in_specs.append(pl.BlockSpec((tm, tk), lambda g0, g1, k, ao=ao: (ij(g0, g1)[0], k + ao)))
        if p.get("trans_b"):
            in_specs.append(pl.BlockSpec((tn, tk), lambda g0, g1, k, bk=bk, bn=bn: (ij(g0, g1)[1] + bn, k + bk), **mode))
        else:
            in_specs.append(pl.BlockSpec((tk, tn), lambda g0, g1, k, bk=bk, bn=bn: (k + bk, ij(g0, g1)[1] + bn), **mode))
        args += [p["a"], p["b"]]
    for arr, off in extras:
        in_specs.append(pl.BlockSpec((tm, tn), lambda g0, g1, k, off=off: (ij(g0, g1)[0], ij(g0, g1)[1] + off)))
        args.append(arr)
    for arr in vecs:
        in_specs.append(pl.BlockSpec((1, tn), lambda g0, g1, k: (0, ij(g0, g1)[1])))
        args.append(arr)
    if stack:
        out_specs = [pl.BlockSpec((n_out, tm, tn), lambda g0, g1, k: (0,) + ij(g0, g1))]
        out_shape = [jax.ShapeDtypeStruct((n_out, M, N), out_dtypes[0])]
    else:
        out_specs = [pl.BlockSpec((tm, tn), lambda g0, g1, k: ij(g0, g1)) for _ in out_dtypes]
        out_shape = [jax.ShapeDtypeStruct((M, N), dt) for dt in out_dtypes]
    n_tile_out = len(out_specs)
    out_specs += [pl.BlockSpec((1, tn), lambda g0, g1, k: (0, 0)) for _ in range(row_sums)]
    out_shape += [jax.ShapeDtypeStruct((1, N), F32) for _ in range(row_sums)]
    grid = (M // tm, N // tn, nk) if outer == "i" else (N // tn, M // tm, nk)
    n_in = 2 * n_pairs + n_ex + len(vecs)

    def partials(refs):
        accs = [None] * n_acc
        for idx, p in enumerate(pairs):
            d = _dot(refs[2 * idx][...], refs[2 * idx + 1][...], NT if p.get("trans_b") else NN)
            accs[p["acc"]] = d if accs[p["acc"]] is None else accs[p["acc"]] + d
        return accs

    def finish(accs, refs, first_rows):
        res = epilogue(accs, [r[...] for r in refs[2 * n_pairs:n_in]])
        if stack:
            o = refs[n_in]
            for idx in range(n_out):
                o[idx] = res[idx].astype(o.dtype)
        else:
            for o, r in zip(refs[n_in:n_in + n_out], res):
                o[...] = r.astype(o.dtype)
        for o, r in zip(refs[n_in + n_tile_out:n_in + n_tile_out + row_sums], res[n_out:]):
            @pl.when(first_rows)
            def _(o=o, r=r):
                o[...] = r

            @pl.when(jnp.logical_not(first_rows))
            def _(o=o, r=r):
                o[...] += r

    if nk == 1:
        def body(*refs):
            finish(partials(refs), refs, pl.program_id(0) == 0)
        scratch = []
    else:
        def body(*refs):
            acc_refs = refs[-n_acc:]
            k = pl.program_id(2)
            first_rows = pl.program_id(0) == 0
            new = partials(refs)

            @pl.when(k == 0)
            def _():
                for a, v in zip(acc_refs, new):
                    a[...] = v

            @pl.when(k > 0)
            def _():
                for a, v in zip(acc_refs, new):
                    a[...] += v

            @pl.when(k == nk - 1)
            def _():
                finish([a[...] for a in acc_refs], refs, first_rows)
        scratch = [pltpu.VMEM((tm, tn), F32) for _ in range(n_acc)]

    return _call(body, name=name, grid=grid, in_specs=in_specs, out_specs=out_specs, out_shape=out_shape,
                 scratch=scratch, sem=("parallel", "parallel", "arbitrary"), args=args, comm=comm)


def _matmul_tn(name, x, y, t1, t2, tr, scale=1.0, comm=None):
    L = x.shape[0] if x.ndim == 3 else 1
    R, K1 = x.shape[-2:]
    N1 = y.shape[1]
    nr, n1 = R // tr, K1 // t1
    if x.ndim == 3:
        x_spec = pl.BlockSpec((None, tr, t1), lambda i, j, r: (i // n1, r, i % n1))
    else:
        x_spec = pl.BlockSpec((tr, t1), lambda i, j, r: (r, i))

    def body(x_ref, y_ref, o_ref):
        r = pl.program_id(2)
        d = _dot(x_ref[...], y_ref[...], TN)

        @pl.when(r == 0)
        def _():
            o_ref[...] = d

        @pl.when(r > 0)
        def _():
            o_ref[...] += d

        if scale != 1.0:
            @pl.when(r == nr - 1)
            def _():
                o_ref[...] = o_ref[...] * scale

    return _call(
        body, name=name, grid=(L * n1, N1 // t2, nr),
        in_specs=[x_spec, pl.BlockSpec((tr, t2), lambda i, j, r: (r, j))],
        out_specs=[pl.BlockSpec((t1, t2), lambda i, j, r: (i, j))],
        out_shape=[jax.ShapeDtypeStruct((L * K1, N1), F32)], scratch=[],
        sem=("parallel", "parallel", "arbitrary"), args=(x, y), comm=comm)


def _rmsnorm_fwd(name, h, w):
    M, D = h.shape
    tm = _tile(M, (544, 256, 128))

    def body(h_ref, w_ref, o_ref):
        x = h_ref[...]
        r = lax.rsqrt(jnp.mean(x * x, axis=-1, keepdims=True) + EPS)
        o_ref[...] = (x * r * w_ref[...]).astype(o_ref.dtype)

    return pl.pallas_call(
        body, name=name, grid=(M // tm,),
        in_specs=[pl.BlockSpec((tm, D), lambda i: (i, 0)), pl.BlockSpec((1, D), lambda i: (0, 0))],
        out_specs=pl.BlockSpec((tm, D), lambda i: (i, 0)),
        out_shape=jax.ShapeDtypeStruct((M, D), BF16), compiler_params=_params(("parallel",)),
    )(h, w)


def _rmsnorm_bwd_tile(dn, h, w, dh_in):
    r = lax.rsqrt(jnp.mean(h * h, axis=-1, keepdims=True) + EPS)
    xhat = h * r
    gw = dn * w
    dh = dh_in + r * (gw - xhat * jnp.mean(gw * xhat, axis=-1, keepdims=True))
    return dh, jnp.sum(dn * xhat, axis=0, keepdims=True)


def _loss_head(h, w, target, Bl, nb):
    M, D = h.shape

    def body(h_ref, w_ref, t_ref, dh_ref, dhb_ref, dw_ref, loss_ref):
        b, t = pl.program_id(0), pl.program_id(1)
        live = (t > 0).astype(F32)
        x = h_ref[...]
        r = lax.rsqrt(jnp.mean(x * x, axis=-1, keepdims=True) + EPS)
        xhat = x * r
        wv = w_ref[...]
        err = (xhat * wv - t_ref[0]) * live
        dy = err * (1.0 / D)
        gw = dy * wv
        dx = r * (gw - xhat * jnp.mean(gw * xhat, axis=-1, keepdims=True))
        dh_ref[...] = dx
        dhb_ref[...] = dx.astype(BF16)
        dw = jnp.sum(dy * xhat, axis=0, keepdims=True)
        part = 0.5 * jnp.sum(jnp.sum(err * err, axis=-1, keepdims=True) * (1.0 / D), axis=0, keepdims=True)
        first = jnp.logical_and(b == 0, t == 0)

        @pl.when(first)
        def _():
            dw_ref[...] = dw
            loss_ref[...] = jnp.broadcast_to(part, loss_ref.shape)

        @pl.when(jnp.logical_not(first))
        def _():
            dw_ref[...] += dw
            loss_ref[...] += jnp.broadcast_to(part, loss_ref.shape)

    row = pl.BlockSpec((Q, D), lambda b, t: (b * nb + t, 0))
    vec = pl.BlockSpec((1, D), lambda b, t: (0, 0))
    return pl.pallas_call(
        body, name="loss_head", grid=(Bl, nb),
        in_specs=[row, vec, pl.BlockSpec((1, Q, D), lambda b, t: (b, jnp.maximum(t - 1, 0), 0))],
        out_specs=[row, row, vec, pl.BlockSpec((8, 128), lambda b, t: (0, 0))],
        out_shape=[jax.ShapeDtypeStruct((M, D), F32), jax.ShapeDtypeStruct((M, D), BF16),
                   jax.ShapeDtypeStruct((1, D), F32), jax.ShapeDtypeStruct((8, 128), F32)],
        compiler_params=_params(("arbitrary", "arbitrary")),
    )(h, w, target)


CONV_TC = 256


def _conv_pre(xr_ref, w_ref, b_ref):
    x = xr_ref[...].astype(F32)
    acc = b_ref[...] + w_ref[SSD_CONV - 1:SSD_CONV, :] * x
    for k in range(1, SSD_CONV):
        acc = acc + w_ref[SSD_CONV - 1 - k:SSD_CONV - k, :] * pltpu.roll(x, k, 0)
    return x, acc


def _conv_fwd(proj, w, b, Bl, T):
    M = proj.shape[0]
    off = 1024 // CONV_TC

    def body(xr_ref, w_ref, b_ref, o_ref):
        _, acc = _conv_pre(xr_ref, w_ref, b_ref)
        row = lax.broadcasted_iota(jnp.int32, acc.shape, 0)
        o_ref[...] = jnp.where(row >= PAD, acc * _sigmoid(acc), 0.0).astype(o_ref.dtype)

    return pl.pallas_call(
        body, name="conv_fwd", grid=(Bl, SSD_CONV_CH // CONV_TC),
        in_specs=[pl.BlockSpec((T, CONV_TC), lambda bb, j: (bb, j + off)),
                  pl.BlockSpec((SSD_CONV, CONV_TC), lambda bb, j: (0, j)), pl.BlockSpec((1, CONV_TC), lambda bb, j: (0, j))],
        out_specs=pl.BlockSpec((T, CONV_TC), lambda bb, j: (bb, j)),
        out_shape=jax.ShapeDtypeStruct((M, SSD_CONV_CH), BF16), compiler_params=_params(("parallel", "parallel")),
    )(proj, w, b)


def _conv_bwd(proj, w, b, dxc, Bl, T):
    M = proj.shape[0]
    off = 1024 // CONV_TC

    def body(xr_ref, w_ref, b_ref, d_ref, dx_ref, dw_ref, db_ref):
        x, acc = _conv_pre(xr_ref, w_ref, b_ref)
        row = lax.broadcasted_iota(jnp.int32, acc.shape, 0)
        s = _sigmoid(acc)
        dpre = jnp.where(row >= PAD, d_ref[...].astype(F32) * _dsilu(acc, s), 0.0)
        dx = w_ref[SSD_CONV - 1:SSD_CONV, :] * dpre
        dws = [jnp.sum(dpre * x, axis=0, keepdims=True)]
        for k in range(1, SSD_CONV):
            dx = dx + w_ref[SSD_CONV - 1 - k:SSD_CONV - k, :] * pltpu.roll(dpre, T - k, 0)
            dws.append(jnp.sum(dpre * pltpu.roll(x, k, 0), axis=0, keepdims=True))
        dx_ref[...] = dx.astype(dx_ref.dtype)
        dw = jnp.concatenate(dws[::-1], axis=0)
        db = jnp.sum(dpre, axis=0, keepdims=True)

        @pl.when(pl.program_id(1) == 0)
        def _():
            dw_ref[...] = dw
            db_ref[...] = db

        @pl.when(pl.program_id(1) > 0)
        def _():
            dw_ref[...] += dw
            db_ref[...] += db

    return pl.pallas_call(
        body, name="conv_bwd", grid=(SSD_CONV_CH // CONV_TC, Bl),
        in_specs=[pl.BlockSpec((T, CONV_TC), lambda j, bb: (bb, j + off)),
                  pl.BlockSpec((SSD_CONV, CONV_TC), lambda j, bb: (0, j)), pl.BlockSpec((1, CONV_TC), lambda j, bb: (0, j)),
                  pl.BlockSpec((T, CONV_TC), lambda j, bb: (bb, j))],
        out_specs=[pl.BlockSpec((T, CONV_TC), lambda j, bb: (bb, j)),
                   pl.BlockSpec((SSD_CONV, CONV_TC), lambda j, bb: (0, j)), pl.BlockSpec((1, CONV_TC), lambda j, bb: (0, j))],
        out_shape=[jax.ShapeDtypeStruct((M, SSD_CONV_CH), BF16), jax.ShapeDtypeStruct((SSD_CONV, SSD_CONV_CH), F32),
                   jax.ShapeDtypeStruct((1, SSD_CONV_CH), F32)],
        compiler_params=_params(("parallel", "arbitrary")),
    )(proj, w, b, dxc)


N_PAIR = SSD_HEADS // 2
HPG = SSD_HEADS // SSD_GROUPS
GW = SSD_INNER // SSD_GROUPS


def _per_group(fn, *arrs):
    return jnp.concatenate([jnp.broadcast_to(fn(*(a[:, GW * g:GW * (g + 1)] for a in arrs)), (arrs[0].shape[0], GW))
                            for g in range(SSD_GROUPS)], axis=1)


def _ssd_prep(c, dtr_ref, bias_ref, alog_ref, d_ref):
    row = lax.broadcasted_iota(jnp.int32, (Q, 128), 0)
    col = lax.broadcasted_iota(jnp.int32, (Q, 128), 1)
    live = col < SSD_HEADS
    valid = jnp.logical_and(jnp.logical_or(c > 0, row >= PAD), live)
    pre = dtr_ref[...] + bias_ref[...]
    dt = jnp.where(valid, _softplus(pre), 0.0)
    A = jnp.where(live[0:1], -jnp.exp(alog_ref[...]), 0.0)
    tri = row >= col
    eye = (row == col).astype(BF16)
    cs = _dot01(tri, dt * A, NN, "a")
    cst = _dot01(eye, cs, NT, "a")
    spread = (lax.broadcasted_iota(jnp.int32, (128, SSD_INNER), 0)
              == lax.broadcasted_iota(jnp.int32, (128, SSD_INNER), 1) // SSD_HEAD_DIM).astype(BF16)
    dt_w = _dot01(dt, spread, NN, "b")
    cs_w = _dot01(cs, spread, NN, "b")
    d_w = _dot01(jnp.broadcast_to(d_ref[...], (8, 128)), spread, NN, "b")[0:1]
    lane = lax.broadcasted_iota(jnp.int32, (Q, SSD_INNER), 1)
    first = (lane % 128) < SSD_HEAD_DIM
    return dict(row=row, col=col, valid=valid, pre=pre, dt=dt, A=A, tri=tri, eye=eye, cs=cs, cst=cst, spread=spread,
                dt_w=dt_w, cs_w=cs_w, d_w=d_w, ecs_w=jnp.exp(cs_w), decay_w=jnp.exp(cs_w[Q - 1:Q] - cs_w), first=first)


def _ssd_chunk(xc_ref, s, states):
    xv = xc_ref[:, 0:SSD_INNER].astype(F32)
    Bs = [xc_ref[:, SSD_INNER + 128 * g:SSD_INNER + 128 * (g + 1)] for g in range(SSD_GROUPS)]
    Cs = [xc_ref[:, SSD_INNER + 512 + 128 * g:SSD_INNER + 512 + 128 * (g + 1)] for g in range(SSD_GROUPS)]
    X = xv * s["dt_w"]
    X0 = jnp.where(s["first"], X, 0.0)
    Xb = (X0.astype(BF16), (X - X0).astype(BF16))
    Xd = (X * s["decay_w"]).astype(BF16)
    CB = [_dot(Cs[g], Bs[g], NT) for g in range(SSD_GROUPS)]
    Lms = [jnp.exp(jnp.where(s["tri"], s["cs"][:, h:h + 1] - s["cst"][h:h + 1, :], -jnp.inf)) for h in range(SSD_HEADS)]
    Ms = [CB[h // HPG] * Lms[h] for h in range(SSD_HEADS)]
    Mb = [m.astype(BF16) for m in Ms]
    prev_b = [st.astype(BF16) for st in states]
    yds, yos, sts = [], [], []
    for p in range(N_PAIR):
        g, ln = p // 2, slice(128 * p, 128 * (p + 1))
        yds.append(_dot(Mb[2 * p], Xb[0][:, ln], NN) + _dot(Mb[2 * p + 1], Xb[1][:, ln], NN))
        yos.append(_dot(Cs[g], prev_b[p], NT))
        sts.append(_dot(Xd[:, ln], Bs[g], TN))
    yo = jnp.concatenate(yos, axis=1)
    y = jnp.concatenate(yds, axis=1) + yo * s["ecs_w"] + xv * s["d_w"]
    upper = s["row"] < SSD_HEAD_DIM
    cl = s["cs"][Q - 1:Q, :]
    ecl_rows = [jnp.where(upper, jnp.exp(cl[:, 2 * p:2 * p + 1]), jnp.exp(cl[:, 2 * p + 1:2 * p + 2])) for p in range(N_PAIR)]
    new_states = [states[p] * ecl_rows[p] + sts[p] for p in range(N_PAIR)]
    return y, new_states, dict(xv=xv, Bs=Bs, Cs=Cs, X=X, Xb=Xb, CB=CB, Lms=Lms, Ms=Ms, Mb=Mb, prev_b=prev_b, yo=yo,
                               ecl_rows=ecl_rows)


def _ssd_in_specs(nc, rev=False):
    rb = (lambda b, c: b * nc + nc - 1 - c) if rev else (lambda b, c: b * nc + c)
    vec = pl.BlockSpec((1, 128), lambda b, c: (0, 0))
    return [pl.BlockSpec((Q, SSD_CONV_CH), lambda b, c: (rb(b, c), 0)),
            pl.BlockSpec((Q, 128), lambda b, c: (rb(b, c), 0)),
            pl.BlockSpec((Q, SSD_INNER), lambda b, c: (rb(b, c), 0)),
            vec, vec, vec, pl.BlockSpec((1, SSD_INNER), lambda b, c: (0, 0))]


def _ssd_fwd(xc, dtr, proj, bias_p, alog_p, d_p, nw, Bl, nc):
    M = xc.shape[0]

    def body(xc_ref, dtr_ref, z_ref, bias_ref, alog_ref, d_ref, nw_ref, y_ref, prev_ref, state):
        c = pl.program_id(1)

        @pl.when(c == 0)
        def _():
            state[...] = jnp.zeros_like(state)

        s = _ssd_prep(c, dtr_ref, bias_ref, alog_ref, d_ref)
        states = [state[p] for p in range(N_PAIR)]
        y, new_states, _ = _ssd_chunk(xc_ref, s, states)
        for p in range(N_PAIR):
            prev_ref[0, 0, p] = states[p]
            state[p] = new_states[p]
        zz = z_ref[...].astype(F32)
        yg = y * zz * _sigmoid(zz)
        r = _per_group(lambda a: lax.rsqrt(jnp.mean(a * a, axis=-1, keepdims=True) + EPS), yg)
        y_ref[...] = (yg * r * nw_ref[...]).astype(y_ref.dtype)

    return pl.pallas_call(
        body, name="ssd_fwd", grid=(Bl, nc), in_specs=_ssd_in_specs(nc),
        out_specs=[pl.BlockSpec((Q, SSD_INNER), lambda b, c: (b * nc + c, 0)),
                   pl.BlockSpec((1, 1, N_PAIR, 128, 128), lambda b, c: (b, c, 0, 0, 0))],
        out_shape=[jax.ShapeDtypeStruct((M, SSD_INNER), BF16), jax.ShapeDtypeStruct((Bl, nc, N_PAIR, 128, 128), F32)],
        scratch_shapes=[pltpu.VMEM((N_PAIR, 128, 128), F32)],
        compiler_params=_params(("arbitrary", "arbitrary")),
    )(xc, dtr, proj, bias_p, alog_p, d_p, nw)


def _ssd_bwd(xc, dtr, proj, bias_p, alog_p, d_p, nw, prev, dya, Bl, nc, comm=None):
    M = xc.shape[0]

    def body(xc_ref, dtr_ref, z_ref, bias_ref, alog_ref, d_ref, nw_ref, prev_ref, dy_ref,
             dxc_ref, dz_ref, ddtr_ref, dbias_ref, dalog_ref, dd_ref, dnw_ref, dS):
        b, t = pl.program_id(0), pl.program_id(1)

        @pl.when(t == 0)
        def _():
            dS[...] = jnp.zeros_like(dS)

        s = _ssd_prep(nc - 1 - t, dtr_ref, bias_ref, alog_ref, d_ref)
        states = [prev_ref[0, 0, p] for p in range(N_PAIR)]
        y, _, k = _ssd_chunk(xc_ref, s, states)
        xv, Bs, Cs, Xb = k["xv"], k["Bs"], k["Cs"], k["Xb"]

        zz = z_ref[...].astype(F32)
        sz = _sigmoid(zz)
        silu_z = zz * sz
        yg = y * silu_z
        r = _per_group(lambda a: lax.rsqrt(jnp.mean(a * a, axis=-1, keepdims=True) + EPS), yg)
        xhat = yg * r
        dout = dy_ref[...].astype(F32)
        gw = dout * nw_ref[...]
        dyg = r * (gw - xhat * _per_group(lambda a, c2: jnp.mean(a * c2, axis=-1, keepdims=True), gw, xhat))
        dnw = jnp.sum(dout * xhat, axis=0, keepdims=True)
        dz_ref[...] = (dyg * y * _dsilu(zz, sz)).astype(dz_ref.dtype)
        dy = dyg * silu_z
        dy0 = jnp.where(s["first"], dy, 0.0)
        dyb = (dy0.astype(BF16), (dy - dy0).astype(BF16))
        dYo = (dy * s["ecs_w"]).astype(BF16)

        dS_f = [dS[p] for p in range(N_PAIR)]
        dS_b = [d.astype(BF16) for d in dS_f]
        BdS, dXm, dprev, dCs, dMs, XdS = [], [], [], [[] for _ in range(SSD_GROUPS)], [], []
        for p in range(N_PAIR):
            g, ln = p // 2, slice(128 * p, 128 * (p + 1))
            BdS.append(_dot(Bs[g], dS_b[p], NT))
            dXm.append(_dot(k["Mb"][2 * p], dyb[0][:, ln], TN) + _dot(k["Mb"][2 * p + 1], dyb[1][:, ln], TN))
            dprev.append(_dot(dYo[:, ln], Cs[g], TN))
            dCs[g].append(_dot(dYo[:, ln], k["prev_b"][p], NN))
            for hh in range(2):
                dMs.append(_dot(dyb[hh][:, ln], Xb[hh][:, ln], NT))
                XdS.append(_dot(Xb[hh][:, ln], dS_b[p], NN))
        dX = jnp.concatenate(dXm, axis=1) + s["decay_w"] * jnp.concatenate(BdS, axis=1)
        dxs = dy * s["d_w"] + dX * s["dt_w"]

        heads = lambda a: _dot01(a, s["spread"], NT, "b")
        ddt = heads(dX * xv)
        dcs = heads(dy * k["yo"] * s["ecs_w"])
        dD = jnp.sum(heads(dy * xv), axis=0, keepdims=True)

        col, row = s["col"], s["row"]
        lane1 = col[0:1]
        rowsT = lax.broadcasted_iota(jnp.int32, (128, Q), 0)
        dcs_t = jnp.zeros((128, Q), F32)
        dcl = jnp.zeros((1, 128), F32)
        dB_out, dC_out = [], []
        for g in range(SSD_GROUPS):
            Bf = Bs[g].astype(F32)
            dCB = jnp.zeros((Q, Q), F32)
            dBacc = jnp.zeros((Q, 128), F32)
            for r4 in range(HPG):
                h = HPG * g + r4
                p, hh = h // 2, h % 2
                W = dMs[h] * k["Ms"][h]
                dCB = dCB + dMs[h] * k["Lms"][h]
                decay_h = s["decay_w"][:, SSD_HEAD_DIM * h:SSD_HEAD_DIM * h + 1]
                dBacc = dBacc + decay_h * XdS[h]
                tdec = jnp.sum(XdS[h] * Bf, axis=1, keepdims=True) * decay_h
                dcs = dcs + jnp.where(col == h, jnp.sum(W, axis=1, keepdims=True) - tdec, 0.0)
                dcs_t = dcs_t - jnp.where(rowsT == h, jnp.sum(W, axis=0, keepdims=True), 0.0)
                rows_h = (row < SSD_HEAD_DIM) if hh == 0 else (row >= SSD_HEAD_DIM)
                sprev = jnp.sum(jnp.sum(jnp.where(rows_h, dS_f[p] * states[p], 0.0), axis=1, keepdims=True),
                                axis=0, keepdims=True)
                ecl = jnp.exp(s["cs"][Q - 1:Q, h:h + 1])
                dcl = dcl + jnp.where(lane1 == h, jnp.sum(tdec, axis=0, keepdims=True) + ecl * sprev, 0.0)
            dCB_b = dCB.astype(BF16)
            dC_out.append(dCs[g][0] + dCs[g][1] + _dot(dCB_b, Bs[g], NN))
            dB_out.append(dBacc + _dot(dCB_b, Cs[g], TN))
        for p in range(N_PAIR):
            dS[p] = dS_f[p] * k["ecl_rows"][p] + dprev[p]
        dxc_ref[...] = jnp.concatenate([dxs] + dB_out + dC_out, axis=1).astype(dxc_ref.dtype)

        dcs = dcs + _dot01(s["eye"], dcs_t, NT, "a") + jnp.where(row == Q - 1, dcl, 0.0)
        da = _dot01(row <= col, dcs, NN, "a")
        ddt = ddt + da * s["A"]
        dpre = jnp.where(s["valid"], ddt * _sigmoid(s["pre"]), 0.0)
        ddtr_ref[...] = dpre
        dbias = jnp.sum(dpre, axis=0, keepdims=True)
        dalog = jnp.sum(da * s["dt"], axis=0, keepdims=True) * s["A"]
        first_step = jnp.logical_and(b == 0, t == 0)

        @pl.when(first_step)
        def _():
            dbias_ref[...] = dbias
            dalog_ref[...] = dalog
            dd_ref[...] = dD
            dnw_ref[...] = dnw

        @pl.when(jnp.logical_not(first_step))
        def _():
            dbias_ref[...] += dbias
            dalog_ref[...] += dalog
            dd_ref[...] += dD
            dnw_ref[...] += dnw

    rb = lambda b, c: b * nc + nc - 1 - c
    rowblk = lambda w: pl.BlockSpec((Q, w), lambda b, c: (rb(b, c), 0))
    vec = lambda w: pl.BlockSpec((1, w), lambda b, c: (0, 0))
    return _call(
        body, name="ssd_bwd", grid=(Bl, nc),
        in_specs=_ssd_in_specs(nc, rev=True) + [
            pl.BlockSpec((1, 1, N_PAIR, 128, 128), lambda b, c: (b, nc - 1 - c, 0, 0, 0)), rowblk(SSD_INNER)],
        out_specs=[rowblk(SSD_CONV_CH), rowblk(SSD_INNER), rowblk(128), vec(128), vec(128), vec(128), vec(SSD_INNER)],
        out_shape=[jax.ShapeDtypeStruct((M, SSD_CONV_CH), BF16), jax.ShapeDtypeStruct((M, SSD_INNER), BF16),
                   jax.ShapeDtypeStruct((M, 128), F32), jax.ShapeDtypeStruct((1, 128), F32),
                   jax.ShapeDtypeStruct((1, 128), F32), jax.ShapeDtypeStruct((1, 128), F32),
                   jax.ShapeDtypeStruct((1, SSD_INNER), F32)],
        scratch=[pltpu.VMEM((N_PAIR, 128, 128), F32)], sem=("arbitrary", "arbitrary"),
        args=(xc, dtr, proj, bias_p, alog_p, d_p, nw, prev, dya), comm=comm)


NSUB = Q // HG_CHUNK
HG_HP = 8
EXP_CAP = 80.0


def _hg_setup(blk, q_ref, f_ref, hb_ref):
    row = lax.broadcasted_iota(jnp.int32, (Q, Q), 0)
    col = lax.broadcasted_iota(jnp.int32, (Q, Q), 1)
    same = (row // HG_CHUNK) == (col // HG_CHUNK)
    causal = jnp.logical_and(same, col <= row)
    lb = _sigmoid(hb_ref[0:1, :] - hb_ref[1:2, :])
    fl = f_ref[...].astype(F32)
    sg = _sigmoid(fl)
    fg = lb + (1.0 - lb) * sg
    k = (1.0 - lb) * (1.0 - sg)
    gl = jnp.log(fg)
    G = _dot01(causal, gl, NN, "a")
    T = _dot01(same, gl, NN, "a")
    qv = q_ref[...].astype(F32)
    sq = _sigmoid(qv)
    eG = jnp.exp(G)
    eGn = jnp.exp(jnp.minimum(-G, EXP_CAP))
    eTG = jnp.exp(T - G)
    qt = qv * sq * eG
    kt = k * eGn
    kh = k * eTG
    valid = jnp.logical_or(blk > 0, row[:, :1] >= PAD)
    return dict(row=row, col=col, same=same, causal=causal, lb=lb, sg=sg, fg=fg, k=k, T=T, qv=qv, sq=sq,
                eG=eG, eGn=eGn, eTG=eTG, qt=qt, kt=kt, kh=kh, valid=valid)


def _hg_specs(nb, rev=False):
    rb = (lambda h, b, t: b * nb + nb - 1 - t) if rev else (lambda h, b, t: b * nb + t)
    w = 128 * HG_HP
    blk = lambda off: pl.BlockSpec((Q, w), lambda h, b, t, off=off: (rb(h, b, t), off // HG_HP + h))
    return [blk(24), blk(32), blk(40), blk(48),
            pl.BlockSpec((2, w), lambda h, b, t: (0, h)), pl.BlockSpec((1, w), lambda h, b, t: (0, h))]


HEAD_LANES = tuple(slice(128 * hh, 128 * (hh + 1)) for hh in range(HG_HP))


def _per_head(fn, *arrs):
    return jnp.concatenate([jnp.broadcast_to(fn(*(a[:, ln] for a in arrs)), (arrs[0].shape[0], 128))
                            for ln in HEAD_LANES], axis=1)


def _hgrn_fwd(proj, hb, nw, Bl, nb, comm=None):
    M = proj.shape[0]

    def body(q_ref, f_ref, i_ref, g_ref, hb_ref, nw_ref, y_ref, o_ref, st_ref, S):
        blk = pl.program_id(2)

        @pl.when(blk == 0)
        def _():
            S[...] = jnp.zeros_like(S)

        s = _hg_setup(blk, q_ref, f_ref, hb_ref)
        v = i_ref[...]
        qt_b, kt_b, kh_b = s["qt"].astype(BF16), s["kt"].astype(BF16), s["kh"].astype(BF16)
        eT = jnp.exp(s["T"])
        att = [jnp.where(s["causal"], _dot(qt_b[:, ln], kt_b[:, ln], NT), 0.0).astype(BF16) for ln in HEAD_LANES]
        o_intra = [_dot(att[hh], v[:, ln], NN) for hh, ln in enumerate(HEAD_LANES)]
        for j in range(NSUB):
            sl = slice(HG_CHUNK * j, HG_CHUNK * (j + 1))
            for hh, ln in enumerate(HEAD_LANES):
                St = S[hh]
                st_ref[0, hh, 0, j] = St
                o_ref[sl, ln] = o_intra[hh][sl] + _dot(qt_b[sl, ln], St.astype(BF16), NT)
                S[hh] = St * eT[HG_CHUNK * j:HG_CHUNK * j + 1, ln] + _dot(v[sl, ln], kh_b[sl, ln], TN)
        o = o_ref[...]
        r = _per_head(lambda a: lax.rsqrt(jnp.mean(a * a, axis=-1, keepdims=True) + EPS), o)
        gv = g_ref[...].astype(F32)
        y_ref[...] = (o * r * nw_ref[...] * gv * _sigmoid(gv)).astype(y_ref.dtype)

    rowblk = pl.BlockSpec((Q, 128 * HG_HP), lambda h, b, t: (b * nb + t, h))
    return _call(
        body, name="hgrn_fwd", grid=(HG_HEADS // HG_HP, Bl, nb), in_specs=_hg_specs(nb),
        out_specs=[rowblk, rowblk,
                   pl.BlockSpec((1, HG_HP, 1, NSUB, 128, 128), lambda h, b, t: (b, h, t, 0, 0, 0))],
        out_shape=[jax.ShapeDtypeStruct((M, HG_WIDTH), BF16), jax.ShapeDtypeStruct((M, HG_WIDTH), F32),
                   jax.ShapeDtypeStruct((Bl, HG_HEADS, nb, NSUB, 128, 128), F32)],
        scratch=[pltpu.VMEM((HG_HP, 128, 128), F32)], sem=("parallel", "arbitrary", "arbitrary"),
        args=(proj, proj, proj, proj, hb, nw), comm=comm)


def _hgrn_bwd(proj, hb, nw, o_saved, st_saved, dyb, Bl, nb, comm=None):
    M = proj.shape[0]

    def body(q_ref, f_ref, i_ref, g_ref, hb_ref, nw_ref, o_ref, st_ref, dy_ref,
             dq_ref, df_ref, di_ref, dg_ref, dhb_ref, dnw_ref, dS, a_dqt, a_dv, a_dkh, a_dgl):
        b, t = pl.program_id(1), pl.program_id(2)

        @pl.when(t == 0)
        def _():
            dS[...] = jnp.zeros_like(dS)

        first_step = jnp.logical_and(b == 0, t == 0)
        s = _hg_setup(nb - 1 - t, q_ref, f_ref, hb_ref)
        v = i_ref[...]
        qt_b, kt_b, kh_b = s["qt"].astype(BF16), s["kt"].astype(BF16), s["kh"].astype(BF16)
        eT = jnp.exp(s["T"])
        att = [jnp.where(s["causal"], _dot(qt_b[:, ln], kt_b[:, ln], NT), 0.0).astype(BF16) for ln in HEAD_LANES]

        o = o_ref[...]
        r = _per_head(lambda a: lax.rsqrt(jnp.mean(a * a, axis=-1, keepdims=True) + EPS), o)
        xhat = o * r
        gv = g_ref[...].astype(F32)
        sgv = _sigmoid(gv)
        dyv = dy_ref[...].astype(F32)
        d_on = dyv * gv * sgv
        dg_out = dyv * xhat * nw_ref[...] * _dsilu(gv, sgv)
        gw = d_on * nw_ref[...]
        do = r * (gw - xhat * _per_head(lambda a, c: jnp.mean(a * c, axis=-1, keepdims=True), gw, xhat))
        dnw = jnp.sum(d_on * xhat, axis=0, keepdims=True)
        do_b = do.astype(BF16)

        datt = [jnp.where(s["causal"], _dot(do_b[:, ln], v[:, ln], NT), 0.0).astype(BF16) for ln in HEAD_LANES]
        dqt = jnp.concatenate([_dot(datt[hh], kt_b[:, ln], NN) for hh, ln in enumerate(HEAD_LANES)], axis=1)
        dkt = jnp.concatenate([_dot(datt[hh], qt_b[:, ln], TN) for hh, ln in enumerate(HEAD_LANES)], axis=1)
        dv = jnp.concatenate([_dot(att[hh], do_b[:, ln], TN) for hh, ln in enumerate(HEAD_LANES)], axis=1)
        last_row = (lax.broadcasted_iota(jnp.int32, (HG_CHUNK, 128), 0) == HG_CHUNK - 1)
        for j in reversed(range(NSUB)):
            sl = slice(HG_CHUNK * j, HG_CHUNK * (j + 1))
            for hh, ln in enumerate(HEAD_LANES):
                St = st_ref[0, hh, 0, j]
                dSt = dS[hh]
                St_b, dSt_b = St.astype(BF16), dSt.astype(BF16)
                eT_j = eT[HG_CHUNK * j:HG_CHUNK * j + 1, ln]
                dkh_j = _dot(v[sl, ln], dSt_b, NN)
                a_dqt[sl, ln] = _dot(do_b[sl, ln], St_b, NN)
                a_dv[sl, ln] = _dot(kh_b[sl, ln], dSt_b, NT)
                a_dkh[sl, ln] = dkh_j
                dlast = (jnp.sum(St * dSt, axis=0, keepdims=True) * eT_j
                         + jnp.sum(dkh_j * s["kh"][sl, ln], axis=0, keepdims=True))
                a_dgl[sl, ln] = jnp.where(last_row, dlast, 0.0)
                dS[hh] = dSt * eT_j + _dot(do_b[sl, ln], qt_b[sl, ln], TN)
        dqt = dqt + a_dqt[...]
        dv = dv + a_dv[...]
        dkh = a_dkh[...]
        dG = dqt * s["qt"] - dkt * s["kt"] - dkh * s["kh"] + a_dgl[...]
        rev_causal = jnp.logical_and(s["same"], s["col"] >= s["row"])
        dgl = _dot01(rev_causal, dG, NN, "a")
        dk = dkt * s["eGn"] + dkh * s["eTG"]
        dfg = dgl / s["fg"] - dk
        lb, sg = s["lb"], s["sg"]
        keep = s["valid"].astype(F32)
        df_ref[...] = (dfg * (1.0 - lb) * sg * (1.0 - sg) * keep).astype(df_ref.dtype)
        dq_ref[...] = (dqt * s["eG"] * _dsilu(s["qv"], s["sq"]) * keep).astype(dq_ref.dtype)
        di_ref[...] = (dv * keep).astype(di_ref.dtype)
        dg_ref[...] = (dg_out * keep).astype(dg_ref.dtype)
        dlb = jnp.sum(dfg * (1.0 - sg) * keep, axis=0, keepdims=True) * lb * (1.0 - lb)
        dhb = jnp.concatenate([dlb, -dlb], axis=0)

        @pl.when(first_step)
        def _():
            dhb_ref[...] = dhb
            dnw_ref[...] = dnw

        @pl.when(jnp.logical_not(first_step))
        def _():
            dhb_ref[...] += dhb
            dnw_ref[...] += dnw

    w = 128 * HG_HP
    rowblk = pl.BlockSpec((Q, w), lambda h, b, t: (b * nb + nb - 1 - t, h))
    return _call(
        body, name="hgrn_bwd", grid=(HG_HEADS // HG_HP, Bl, nb),
        in_specs=_hg_specs(nb, rev=True) + [
            rowblk, pl.BlockSpec((1, HG_HP, 1, NSUB, 128, 128), lambda h, b, t: (b, h, nb - 1 - t, 0, 0, 0)), rowblk],
        out_specs=[rowblk, rowblk, rowblk, rowblk,
                   pl.BlockSpec((2, w), lambda h, b, t: (0, h)), pl.BlockSpec((1, w), lambda h, b, t: (0, h))],
        out_shape=[jax.ShapeDtypeStruct((M, HG_WIDTH), BF16)] * 4 + [
            jax.ShapeDtypeStruct((2, HG_WIDTH), F32), jax.ShapeDtypeStruct((1, HG_WIDTH), F32)],
        scratch=[pltpu.VMEM((HG_HP, 128, 128), F32)] + [pltpu.VMEM((Q, w), F32)] * 4,
        sem=("parallel", "arbitrary", "arbitrary"),
        args=(proj, proj, proj, proj, hb, nw, o_saved, st_saved, dyb), comm=comm)


def _adamw(name, parts, w, m, v):
    R, C = w.shape
    S = parts.shape[0]
    tr, tc = (_tile(R, (256, 176, 128, 64, 8)), C) if R % 8 == 0 else (R, 256)
    c1, c2 = 1.0 - ADAM_B1 ** ADAM_STEP, 1.0 - ADAM_B2 ** ADAM_STEP

    def body(p_ref, w_ref, m_ref, v_ref, g_ref, d_ref, nm_ref, nv_ref):
        g = p_ref[0].astype(F32)
        for s in range(1, S):
            g = g + p_ref[s].astype(F32)
        nm = ADAM_B1 * m_ref[...] + (1.0 - ADAM_B1) * g
        nv = ADAM_B2 * v_ref[...] + (1.0 - ADAM_B2) * (g * g)
        g_ref[...] = g
        nm_ref[...] = nm
        nv_ref[...] = nv
        d_ref[...] = -ADAM_LR * ((nm / c1) / (jnp.sqrt(nv / c2) + ADAM_EPS) + ADAM_WD * w_ref[...])

    blk = pl.BlockSpec((tr, tc), lambda i, j: (i, j))
    return pl.pallas_call(
        body, name=name, grid=(R // tr, C // tc),
        in_specs=[pl.BlockSpec((S, tr, tc), lambda i, j: (0, i, j)), blk, blk, blk], out_specs=[blk] * 4,
        out_shape=[jax.ShapeDtypeStruct((R, C), F32)] * 4, compiler_params=_params(("parallel", "parallel")),
    )(parts, w, m, v)


def _pair_sum(name, by_core, arrived):
    _, J, R, C = by_core.shape
    tc = _tile(C, (512, 256, 128))

    def body(c_ref, a_ref, b_ref, o_ref):
        o_ref[...] = (a_ref[0].astype(F32) + b_ref[...].astype(F32)).astype(o_ref.dtype)

    blk = pl.BlockSpec((1, R, tc), lambda j, k, c_ref: (j, 0, k))
    return pl.pallas_call(
        body, name=name,
        grid_spec=pltpu.PrefetchScalarGridSpec(
            num_scalar_prefetch=1, grid=(J, C // tc),
            in_specs=[pl.BlockSpec((1, 1, R, tc), lambda j, k, c_ref: (c_ref[0], j, 0, k)), blk], out_specs=blk),
        out_shape=jax.ShapeDtypeStruct(arrived.shape, arrived.dtype), compiler_params=_params(("parallel", "parallel")),
    )(lax.axis_index("c").astype(jnp.int32).reshape(1), by_core, arrived)


def _sum_parts(name, parts):
    S, R, C = parts.shape

    def body(p_ref, o_ref):
        g = p_ref[0]
        for s in range(1, S):
            g = g + p_ref[s]
        o_ref[...] = g

    return pl.pallas_call(
        body, name=name, out_shape=jax.ShapeDtypeStruct((R, C), F32),
        in_specs=[pl.BlockSpec(memory_space=pltpu.VMEM)], out_specs=pl.BlockSpec(memory_space=pltpu.VMEM),
    )(parts)


def _heads_to_lanes(p):
    return jnp.pad(p, [(0, 0)] * (p.ndim - 1) + [(0, 128 - SSD_HEADS)])


def _lanes_to_heads(p):
    return p[..., :SSD_HEADS]


def _pack_rows(arrs):
    rows = []
    for a in arrs:
        f = a.reshape(-1).astype(F32)
        n = -(-f.shape[0] // D_MODEL) * D_MODEL
        rows.append(jnp.pad(f, (0, n - f.shape[0])).reshape(-1, D_MODEL))
    out = jnp.concatenate(rows, axis=0)
    return jnp.pad(out, ((0, (-out.shape[0]) % 8), (0, 0)))


def _unpack_rows(packed, like):
    outs, r = [], 0
    for a in like:
        n = 1
        for s in a.shape:
            n *= s
        nr = -(-n // D_MODEL)
        outs.append(packed[r:r + nr].reshape(-1)[:n].reshape(a.shape))
        r += nr
    return outs


def _cols(gth):
    return jnp.transpose(gth, (1, 0, 2)).reshape(gth.shape[1], -1)


def _rows(gth):
    return gth.reshape(-1, gth.shape[2])


def _to_rows(g):
    return g.reshape(N_DEV, -1, g.shape[1]).astype(BF16)


def _by_core(g):
    return jnp.transpose(g.reshape(N_DEV // 2, 2, -1, g.shape[1]), (1, 0, 2, 3)).astype(BF16)


DT_ROW = 3072


def _win_split(shards):
    win_t = _rows(shards)
    return jnp.concatenate([win_t[:DT_ROW], win_t[DT_ROW + SSD_HEADS:]], axis=0), win_t[DT_ROW:DT_ROW + SSD_HEADS]


def _win_by_core(d_main, d_dt):
    return _by_core(jnp.concatenate([d_main[:DT_ROW], d_dt, d_main[DT_ROW:]], axis=0))


def _chip_sums(tag, by_core, swap_in=None):
    arrived = swap_in(by_core) if swap_in else _exchange(tag + "_swap", "swap", by_core)
    return [_pair_sum(f"{tag}_chipsum{i}", m, a) for i, (m, a) in enumerate(zip(by_core, arrived))]


def _ffn_fwd_gu(tag, h, norm_w, w_gu_t, comm=None):
    M = h.shape[0]
    F = w_gu_t.shape[0] // 2
    tm = _tile(M, (544, 256))
    n = _rmsnorm_fwd(tag + "_norm", h, norm_w)
    tn = _tile(F, (1408, 704, 256))
    outs = _fused_matmul(
        tag + "_gu", M, F, D_MODEL,
        [dict(a=n, b=w_gu_t, trans_b=True, acc=0), dict(a=n, b=w_gu_t, trans_b=True, bn_off=F // tn, acc=1)], [],
        lambda accs, ex: (accs[0], accs[1], accs[0] * _sigmoid(accs[0]) * accs[1]),
        [BF16, BF16, BF16], 2, tm, tn, D_MODEL, outer="j", comm=comm)
    return (n, *outs[:3]), outs[3:]


def _ffn_fwd_down(tag, h, a, w_down):
    M = h.shape[0]
    F = w_down.shape[0]
    (h_out,) = _fused_matmul(
        tag + "_down", M, D_MODEL, F, [dict(a=a, b=w_down, acc=0)], [(h, 0)],
        lambda accs, ex: (ex[0] + 0.5 * accs[0],), [F32], 1, _tile(M, (1088, 544, 256)), D_MODEL, F, outer="j")
    return h_out


def _ffn_bwd(tag, dh, dh_b, h, norm_w, w_gu_t, w_down, saved, scatter=False):
    n, g, u, a = saved
    M = h.shape[0]
    F = w_down.shape[0]
    tm = _tile(M, (544, 256))
    tn = _tile(F, (1408, 704, 256))

    def swiglu_bwd(accs, ex):
        da, gv, uv = 0.5 * accs[0], ex[0].astype(F32), ex[1].astype(F32)
        s = _sigmoid(gv)
        return da * uv * _dsilu(gv, s), da * gv * s

    (dgu,) = _fused_matmul(
        tag + "_dact", M, F, D_MODEL, [dict(a=dh_b, b=w_down, trans_b=True, acc=0)], [(g, 0), (u, 0)],
        swiglu_bwd, [BF16, BF16], 1, _tile(M, (1088, 544, 256)), tn, D_MODEL, outer="j", stack=True)
    tr = _tile(M, (2176, 256))
    (dw_down,) = _matmul_tn(tag + "_dwd", a, dh_b, tn, D_MODEL, tr, scale=0.5)
    dw_gu_t, *p_down = _matmul_tn(tag + "_dwgu", dgu, n, tn, D_MODEL, tr,
                                  comm=("scatter", [_to_rows(dw_down)]) if scatter else None)
    comm = None
    if scatter:
        comm = ("chips", _chip_sums(tag + "_wgu", [_by_core(dw_gu_t)]))
    def norm_bwd(accs, ex):
        dh_prev, dw = _rmsnorm_bwd_tile(accs[0], ex[0], ex[2], ex[1])
        return dh_prev, dh_prev, dw

    dh_prev, dh_prev_b, dnorm, *p_gu = _fused_matmul(
        tag + "_dn", M, D_MODEL, F,
        [dict(a=dgu, a_lead=0, b=w_gu_t, acc=0, resident=True),
         dict(a=dgu, a_lead=1, b=w_gu_t, bk_off=1, acc=0, resident=True)], [(h, 0), (dh, 0)],
        norm_bwd, [F32, BF16], 1, tm, D_MODEL, F, outer="i", comm=comm, vecs=[norm_w], row_sums=1)
    return (dh_prev, dh_prev_b, dnorm, *((p_gu[0], p_down[0]) if scatter else (dw_gu_t, dw_down)))


def kernel(x, meta_tokens, ffn1_norm, ffn1_w_gu, ffn1_w_down, mix_norm, w_in, ssd_conv_w, ssd_conv_b, ssd_dt_bias, ssd_a_log, ssd_d, ssd_norm, hg_lower_bound, hg_norm, w_branch_a, w_branch_b, w_out, ffn2_norm, ffn2_w_gu, ffn2_w_down, final_norm, loss_target, m_meta_tokens, m_ffn1_norm, m_ffn1_w_gu, m_ffn1_w_down, m_mix_norm, m_w_in, m_ssd_conv_w, m_ssd_conv_b, m_ssd_dt_bias, m_ssd_a_log, m_ssd_d, m_ssd_norm, m_hg_lower_bound, m_hg_norm, m_w_branch_a, m_w_branch_b, m_w_out, m_ffn2_norm, m_ffn2_w_gu, m_ffn2_w_down, m_final_norm, v_meta_tokens, v_ffn1_norm, v_ffn1_w_gu, v_ffn1_w_down, v_mix_norm, v_w_in, v_ssd_conv_w, v_ssd_conv_b, v_ssd_dt_bias, v_ssd_a_log, v_ssd_d, v_ssd_norm, v_hg_lower_bound, v_hg_norm, v_w_branch_a, v_w_branch_b, v_w_out, v_ffn2_norm, v_ffn2_w_gu, v_ffn2_w_down, v_final_norm):
    Bl, S, D = x.shape
    T = PAD + N_META + S
    nc = T // Q
    M = Bl * T
    me = 4 * lax.axis_index("x") + 2 * lax.axis_index("y") + lax.axis_index("c")

    bf = lambda a: a[0].astype(BF16)
    bft = lambda a: a[0].T.astype(BF16)
    g_wgu1, g_meta, g_conv_w = _exchange("gather_first", "gather", [bft(ffn1_w_gu), meta_tokens, ssd_conv_w[0]])
    wgu1, meta_full, conv_w_full = _rows(g_wgu1), _cols(g_meta), _cols(g_conv_w)
    bias_p, alog_p, d_p = _heads_to_lanes(ssd_dt_bias), _heads_to_lanes(ssd_a_log), _heads_to_lanes(ssd_d)
    final_w = final_norm.reshape(1, D)

    h0 = jnp.concatenate([jnp.zeros((Bl, PAD, D), F32), jnp.broadcast_to(meta_full[None], (Bl, N_META, D)), x],
                         axis=1).reshape(M, D)
    tm = _tile(M, (1088, 544, 256))
    ffn1_saved, (g_wd1, g_win) = _ffn_fwd_gu("ffn1", h0, ffn1_norm, wgu1, comm=("gather", [bf(ffn1_w_down), bft(w_in)]))
    wd1 = _rows(g_wd1)
    win_main, win_dt16 = _win_split(g_win)
    win_dt = jnp.pad(win_dt16, ((0, 128 - SSD_HEADS), (0, 0)))
    h1 = _ffn_fwd_down("ffn1", h0, ffn1_saved[3], wd1)
    un = _rmsnorm_fwd("mix_norm", h1, mix_norm)
    plain = lambda accs, ex: (accs[0],)
    proj, g_wa, g_wb, g_wo = _fused_matmul(
        "in_proj", M, N_MAIN, D, [dict(a=un, b=win_main, trans_b=True, acc=0)], [], plain, [BF16], 1, tm, 1536, D,
        outer="j", comm=("gather", [bf(w_branch_a), bf(w_branch_b), bf(w_out)]))
    wa, wb, wo = _rows(g_wa), _rows(g_wb), _rows(g_wo)
    (dtr,) = _fused_matmul("in_proj_dt", M, 128, D, [dict(a=un, b=win_dt, trans_b=True, acc=0)], [], plain, [F32], 1,
                           tm, 128, D, outer="j")
    xc = _conv_fwd(proj, conv_w_full, ssd_conv_b, Bl, T)
    ya, ssd_prev = _ssd_fwd(xc, dtr, proj, bias_p, alog_p, d_p, ssd_norm, Bl, nc)
    yb, hg_o, hg_st, g_wgu2, g_wd2 = _hgrn_fwd(proj, hg_lower_bound, hg_norm, Bl, nc,
                                               comm=("gather", [bft(ffn2_w_gu), bf(ffn2_w_down)]))
    wgu2, wd2 = _rows(g_wgu2), _rows(g_wd2)

    def branch_fwd(accs, ex):
        pa, pb = accs
        return pa, pb, _sigmoid(ex[0].astype(F32)) * pa + _sigmoid(ex[1].astype(F32)) * pb

    pa, pb, merged = _fused_matmul(
        "branches", M, D, D, [dict(a=ya, b=wa, acc=0), dict(a=yb, b=wb, acc=1)], [(proj, 7), (proj, 8)],
        branch_fwd, [BF16, BF16, BF16], 2, tm, D, D, outer="j")
    (h2,) = _fused_matmul("out_proj", M, D, D, [dict(a=merged, b=wo, acc=0)], [(h1, 0)],
                          lambda accs, ex: (ex[0] + accs[0],), [F32], 1, tm, D, D, outer="j")
    ffn2_saved, _ = _ffn_fwd_gu("ffn2", h2, ffn2_norm, wgu2)
    h3 = _ffn_fwd_down("ffn2", h2, ffn2_saved[3], wd2)

    dh3, dh3_b, d_final, loss_part = _loss_head(h3, final_w, loss_target, Bl, nc)
    dh2, dh2_b, d_ffn2_norm, d_wgu2, d_wd2 = _ffn_bwd("ffn2", dh3, dh3_b, h2, ffn2_norm, wgu2, wd2, ffn2_saved)

    def branch_bwd(accs, ex):
        dm = accs[0]
        ga, gb, pav, pbv = (e.astype(F32) for e in ex)
        sa, sb = _sigmoid(ga), _sigmoid(gb)
        return dm * sa, dm * sb, dm * pav * sa * (1.0 - sa), dm * pbv * sb * (1.0 - sb)

    d_merged_outs = []

    def d_merged_with_swap(theirs):
        d_merged_outs.extend(_fused_matmul(
            "d_merged", M, D, D, [dict(a=dh2_b, b=wo, trans_b=True, acc=0)], [(proj, 7), (proj, 8), (pa, 0), (pb, 0)],
            branch_bwd, [BF16] * 4, 1, tm, D, D, outer="j", comm=("swap", theirs)))
        return d_merged_outs[4:]

    s_ffn2 = _chip_sums("ffn2", [_by_core(d_wgu2), _by_core(d_wd2)], swap_in=d_merged_with_swap)
    dpa, dpb, dga, dgb = d_merged_outs[:4]
    (d_wo,) = _matmul_tn("d_w_out", merged, dh2_b, 512, D, M)
    (d_wa,) = _matmul_tn("d_w_a", ya, dpa, 512, D, M)
    (d_wb,) = _matmul_tn("d_w_b", yb, dpb, 512, D, M)
    dya, dyb = _fused_matmul(
        "d_branches", M, D, D, [dict(a=dpa, b=wa, trans_b=True, acc=0), dict(a=dpb, b=wb, trans_b=True, acc=1)], [],
        lambda accs, ex: (accs[0], accs[1]), [BF16, BF16], 2, tm, D, D, outer="j")
    *ssd_grads, p_wgu2, p_wd2 = _ssd_bwd(xc, dtr, proj, bias_p, alog_p, d_p, ssd_norm, ssd_prev, dya, Bl, nc,
                                         comm=("chips", s_ffn2))
    dxc, dz, ddtr, d_bias_p, d_alog_p, d_d_p, d_ssd_norm = ssd_grads
    dxbc, d_conv_w, d_conv_b = _conv_bwd(proj, conv_w_full, ssd_conv_b, dxc, Bl, T)
    dq, df, di, dg, d_hb, d_hg_norm, p_wa, p_wb, p_wo = _hgrn_bwd(
        proj, hg_lower_bound, hg_norm, hg_o, hg_st, dyb, Bl, nc,
        comm=("scatter", [_to_rows(d_wa), _to_rows(d_wb), _to_rows(d_wo)]))
    dproj = jnp.concatenate([dz, dxbc, dq, df, di, dg, dga, dgb], axis=1)
    ddtr_b = ddtr.astype(BF16)
    (d_win_main,) = _matmul_tn("d_w_in", dproj, un, 768, D, M)
    (d_win_dt,) = _matmul_tn("d_w_in_dt", ddtr_b, un, 128, D, M)
    d_un_dt_outs = []

    def d_un_dt_with_swap(theirs):
        d_un_dt_outs.extend(_fused_matmul("d_un_dt", M, D, 128, [dict(a=ddtr_b, b=win_dt, acc=0)], [], plain, [F32], 1,
                                          tm, D, 128, outer="j", comm=("swap", theirs)))
        return d_un_dt_outs[1:]

    s_win = _chip_sums("w_in", [_win_by_core(d_win_main, d_win_dt[:SSD_HEADS])], swap_in=d_un_dt_with_swap)
    def mix_norm_bwd(accs, ex):
        dh, dw = _rmsnorm_bwd_tile(accs[0] + ex[0], ex[1], ex[3], ex[2])
        return dh, dh, dw

    dh1, dh1_b, d_mix_norm, p_win = _fused_matmul(
        "d_un", M, D, N_MAIN, [dict(a=dproj, b=win_main, acc=0)], [(d_un_dt_outs[0], 0), (h1, 0), (dh2, 0)],
        mix_norm_bwd, [F32, BF16], 1, _tile(M, (544, 256)), D, 3072, outer="i", comm=("chips", s_win),
        vecs=[mix_norm], row_sums=1)
    dh0, _, d_ffn1_norm, p_wgu1, p_wd1 = _ffn_bwd("ffn1", dh1, dh1_b, h0, ffn1_norm, wgu1, wd1, ffn1_saved, scatter=True)

    dh0 = dh0.reshape(Bl, T, D)
    grad_x = dh0[:, PAD + N_META:]
    d_meta = dh0[:, PAD:PAD + N_META]

    small_grads = [d_ffn1_norm, d_mix_norm, d_conv_b, _lanes_to_heads(d_bias_p), _lanes_to_heads(d_alog_p),
                   _lanes_to_heads(d_d_p), d_ssd_norm, d_hb, d_hg_norm, d_ffn2_norm, d_final.reshape(D), d_conv_w]
    small_packed = _pack_rows(small_grads + [d_meta[b] for b in range(Bl)])
    parts = [p_wgu1, p_wd1, p_win, p_wa, p_wb, p_wo, p_wgu2, p_wd2]
    (small_all,) = _exchange("gather_small_grads", "gather", [small_packed])
    small_sum = _sum_parts("sum_small_grads", small_all)
    unpacked = _unpack_rows(small_sum, small_grads + [d_meta[b] for b in range(Bl)])
    g_small = unpacked[:len(small_grads)]
    g_meta_full = unpacked[len(small_grads)]
    for b in range(1, Bl):
        g_meta_full = g_meta_full + unpacked[len(small_grads) + b]
    g_meta = lax.dynamic_slice_in_dim(g_meta_full, me * (D // N_DEV), D // N_DEV, axis=1)
    g_conv_w = lax.dynamic_slice_in_dim(g_small[11], me * (SSD_CONV_CH // N_DEV), SSD_CONV_CH // N_DEV, axis=1)

    names = ["meta_tokens", "ffn1_norm", "ffn1_w_gu", "ffn1_w_down", "mix_norm", "w_in", "ssd_conv_w", "ssd_conv_b",
             "ssd_dt_bias", "ssd_a_log", "ssd_d", "ssd_norm", "hg_lower_bound", "hg_norm", "w_branch_a", "w_branch_b",
             "w_out", "ffn2_norm", "ffn2_w_gu", "ffn2_w_down", "final_norm"]
    W = dict(meta_tokens=meta_tokens, ffn1_norm=ffn1_norm, ffn1_w_gu=ffn1_w_gu, ffn1_w_down=ffn1_w_down, mix_norm=mix_norm,
             w_in=w_in, ssd_conv_w=ssd_conv_w, ssd_conv_b=ssd_conv_b, ssd_dt_bias=ssd_dt_bias, ssd_a_log=ssd_a_log,
             ssd_d=ssd_d, ssd_norm=ssd_norm, hg_lower_bound=hg_lower_bound, hg_norm=hg_norm, w_branch_a=w_branch_a,
             w_branch_b=w_branch_b, w_out=w_out, ffn2_norm=ffn2_norm, ffn2_w_gu=ffn2_w_gu, ffn2_w_down=ffn2_w_down,
             final_norm=final_norm)
    Mo = dict(meta_tokens=m_meta_tokens, ffn1_norm=m_ffn1_norm, ffn1_w_gu=m_ffn1_w_gu, ffn1_w_down=m_ffn1_w_down,
              mix_norm=m_mix_norm, w_in=m_w_in, ssd_conv_w=m_ssd_conv_w, ssd_conv_b=m_ssd_conv_b, ssd_dt_bias=m_ssd_dt_bias,
              ssd_a_log=m_ssd_a_log, ssd_d=m_ssd_d, ssd_norm=m_ssd_norm, hg_lower_bound=m_hg_lower_bound, hg_norm=m_hg_norm,
              w_branch_a=m_w_branch_a, w_branch_b=m_w_branch_b, w_out=m_w_out, ffn2_norm=m_ffn2_norm, ffn2_w_gu=m_ffn2_w_gu,
              ffn2_w_down=m_ffn2_w_down, final_norm=m_final_norm)
    Vo = dict(meta_tokens=v_meta_tokens, ffn1_norm=v_ffn1_norm, ffn1_w_gu=v_ffn1_w_gu, ffn1_w_down=v_ffn1_w_down,
              mix_norm=v_mix_norm, w_in=v_w_in, ssd_conv_w=v_ssd_conv_w, ssd_conv_b=v_ssd_conv_b, ssd_dt_bias=v_ssd_dt_bias,
              ssd_a_log=v_ssd_a_log, ssd_d=v_ssd_d, ssd_norm=v_ssd_norm, hg_lower_bound=v_hg_lower_bound, hg_norm=v_hg_norm,
              w_branch_a=v_w_branch_a, w_branch_b=v_w_branch_b, w_out=v_w_out, ffn2_norm=v_ffn2_norm, ffn2_w_gu=v_ffn2_w_gu,
              ffn2_w_down=v_ffn2_w_down, final_norm=v_final_norm)
    grads, deltas, new_m, new_v = {}, {}, {}, {}
    big_names = ["ffn1_w_gu", "ffn1_w_down", "w_in", "w_branch_a", "w_branch_b", "w_out", "ffn2_w_gu", "ffn2_w_down"]
    transposed = ("ffn1_w_gu", "ffn2_w_gu", "w_in")
    for nm, part in zip(big_names, parts):
        view = (lambda a: a[0].T) if nm in transposed else (lambda a: a[0])
        back = (lambda o: o.T[None]) if nm in transposed else (lambda o: o[None])
        outs = _adamw("adamw_" + nm, part, view(W[nm]), view(Mo[nm]), view(Vo[nm]))
        grads[nm], deltas[nm], new_m[nm], new_v[nm] = (back(o) for o in outs)
    small_names = ["ffn1_norm", "mix_norm", "ssd_conv_b", "ssd_dt_bias", "ssd_a_log", "ssd_d", "ssd_norm", "hg_lower_bound",
                   "hg_norm", "ffn2_norm", "final_norm", "ssd_conv_w", "meta_tokens"]
    small_g = g_small[:11] + [g_conv_w.reshape(ssd_conv_w.shape), g_meta]
    pk = lambda d: _pack_rows([d[nm] for nm in small_names])
    outs = _adamw("adamw_small", _pack_rows(small_g)[None], pk(W), pk(Mo), pk(Vo))
    like = [W[nm] for nm in small_names]
    for dst, o in zip((grads, deltas, new_m, new_v), outs):
        for nm, val in zip(small_names, _unpack_rows(o, like)):
            dst[nm] = val

    loss = lax.psum(loss_part[0, 0], MESH_AXES)
    return (loss, grad_x, *[grads[nm] for nm in names], *[deltas[nm] for nm in names],
            *[new_m[nm] for nm in names], *[new_v[nm] for nm in names])
```

```python
import functools

import jax
import jax.numpy as jnp
from jax import lax
from jax.experimental import pallas as pl
from jax.experimental.pallas import tpu as pltpu

F32, BF16 = jnp.float32, jnp.bfloat16
NN, NT, TN = ((1,), (0,)), ((1,), (1,)), ((0,), (0,))
MESH_AXES = ("x", "y", "c")
N_DEV = 8

D_MODEL = 1024
N_META = 16
EPS = 1e-6
SSD_HEADS, SSD_HEAD_DIM, SSD_GROUPS, SSD_STATE, SSD_CONV, Q = 16, 64, 4, 128, 4, 128
SSD_INNER = SSD_HEADS * SSD_HEAD_DIM
SSD_CONV_CH = SSD_INNER + 2 * SSD_GROUPS * SSD_STATE
HG_WIDTH, HG_HEADS, HG_CHUNK = 1024, 8, 16
PAD = Q - N_META
N_MAIN = 9 * 1024
ADAM_LR, ADAM_B1, ADAM_B2, ADAM_EPS, ADAM_WD, ADAM_STEP = 0.001, 0.9, 0.999, 1e-08, 0.01, 10
VMEM_LIMIT = 52 * 1024 * 1024


def _dot(a, b, dims, prec=None):
    return lax.dot_general(a, b, (dims, ((), ())), precision=prec, preferred_element_type=F32)


def _dot01(a, b, dims, sel):
    x = b if sel == "a" else a
    hi = x.astype(BF16)
    r1 = x - hi.astype(F32)
    mid = r1.astype(BF16)
    lo = (r1 - mid.astype(F32)).astype(BF16)
    s = (a if sel == "a" else b).astype(BF16)
    parts = [_dot(s, p, dims) if sel == "a" else _dot(p, s, dims) for p in (hi, mid, lo)]
    return parts[0] + parts[1] + parts[2]


def _sigmoid(x):
    return 1.0 / (1.0 + jnp.exp(-x))


def _dsilu(x, s):
    return s * (1.0 + x * (1.0 - s))


def _softplus(x):
    e = jnp.exp(-jnp.abs(x))
    u = 1.0 + e
    log1p_e = jnp.where(u == 1.0, e, jnp.log(u) * e / (u - 1.0))
    return jnp.maximum(x, 0.0) + log1p_e


def _params(sem):
    return pltpu.CompilerParams(dimension_semantics=sem, vmem_limit_bytes=VMEM_LIMIT)


def _tile(n, prefs):
    for p in prefs:
        if n % p == 0:
            return p
    return n


CHIP_FLIPS = ((1, 0), (0, 1), (1, 1))
N_PEER = N_DEV - 1


def _comm_gather(srcs, outs, send_sems, recv_sems, local_sems):
    n = len(srcs)
    x, y, c = (lax.axis_index(a) for a in MESH_AXES)
    dev = lambda px, py, pc: 4 * px + 2 * py + pc
    me, sib = dev(x, y, c), (x, y, 1 - c)

    def rc(w, k, slot, to, src=None):
        return pltpu.make_async_remote_copy(
            src_ref=outs[w].at[slot] if src is None else src, dst_ref=outs[w].at[slot],
            send_sem=send_sems.at[w, k], recv_sem=recv_sems.at[w, k], device_id=to, device_id_type=pl.DeviceIdType.MESH)

    def local(w):
        return pltpu.make_async_copy(srcs[w], outs[w].at[me], local_sems.at[w])

    def start():
        for w in range(n):
            local(w).start()
            rc(w, 0, me, sib, src=srcs[w]).start()
            for j, (fx, fy) in enumerate(CHIP_FLIPS):
                rc(w, 1 + j, me, (x ^ fx, y ^ fy, c), src=srcs[w]).start()

    def finish():
        for w in range(n):
            for j, (fx, fy) in enumerate(CHIP_FLIPS):
                slot = dev(x ^ fx, y ^ fy, c)
                rc(w, 1 + j, slot, sib).wait_recv()
                rc(w, 4 + j, slot, sib).start()
        for w in range(n):
            rc(w, 0, dev(x, y, 1 - c), sib).wait_recv()
            rc(w, 0, me, sib, src=srcs[w]).wait_send()
            for j, (fx, fy) in enumerate(CHIP_FLIPS):
                rc(w, 4 + j, dev(x ^ fx, y ^ fy, 1 - c), sib).wait_recv()
                rc(w, 1 + j, me, sib, src=srcs[w]).wait_send()
                rc(w, 4 + j, dev(x ^ fx, y ^ fy, c), sib).wait_send()
            local(w).wait()

    return start, finish


def _comm_scatter(srcs, outs, send_sems, recv_sems, local_sems):
    n = len(srcs)
    x, y, c = (lax.axis_index(a) for a in MESH_AXES)
    me = 4 * x + 2 * y + c

    def copies():
        out = []
        for w in range(n):
            out.append(pltpu.make_async_copy(srcs[w].at[me], outs[w].at[me], local_sems.at[w]))
            for k in range(1, N_DEV):
                px, py, pc = x ^ (k >> 2), y ^ ((k >> 1) & 1), c ^ (k & 1)
                out.append(pltpu.make_async_remote_copy(
                    src_ref=srcs[w].at[4 * px + 2 * py + pc], dst_ref=outs[w].at[me],
                    send_sem=send_sems.at[w, k - 1], recv_sem=recv_sems.at[w, k - 1],
                    device_id=(px, py, pc), device_id_type=pl.DeviceIdType.MESH))
        return out

    def start():
        for cp in copies():
            cp.start()

    def finish():
        for cp in copies():
            cp.wait()

    return start, finish


def _comm_swap(srcs, outs, send_sems, recv_sems, local_sems):
    x, y, c = (lax.axis_index(a) for a in MESH_AXES)

    def copies():
        return [pltpu.make_async_remote_copy(
            src_ref=srcs[w].at[1 - c], dst_ref=outs[w], send_sem=send_sems.at[w, 0], recv_sem=recv_sems.at[w, 0],
            device_id=(x, y, 1 - c), device_id_type=pl.DeviceIdType.MESH) for w in range(len(srcs))]

    def start():
        for cp in copies():
            cp.start()

    def finish():
        for cp in copies():
            cp.wait()

    return start, finish


def _comm_chips(srcs, outs, send_sems, recv_sems, local_sems):
    n = len(srcs)
    x, y, c = (lax.axis_index(a) for a in MESH_AXES)
    mine = 2 * x + y

    def copies():
        out = []
        for w in range(n):
            out.append(pltpu.make_async_copy(srcs[w].at[mine], outs[w].at[mine], local_sems.at[w]))
            for j, (fx, fy) in enumerate(CHIP_FLIPS):
                px, py = x ^ fx, y ^ fy
                out.append(pltpu.make_async_remote_copy(
                    src_ref=srcs[w].at[2 * px + py], dst_ref=outs[w].at[mine],
                    send_sem=send_sems.at[w, j], recv_sem=recv_sems.at[w, j],
                    device_id=(px, py, c), device_id_type=pl.DeviceIdType.MESH))
        return out

    def start():
        for cp in copies():
            cp.start()

    def finish():
        for cp in copies():
            cp.wait()

    return start, finish


def _comm_parts(comm):
    kind, arrays = comm
    n = len(arrays)
    lead = {"gather": lambda a: (N_DEV,) + a.shape, "scatter": lambda a: (N_DEV,) + a.shape[1:],
            "swap": lambda a: a.shape[1:], "chips": lambda a: a.shape}[kind]
    shapes = [jax.ShapeDtypeStruct(lead(a), a.dtype) for a in arrays]
    sems = [pltpu.SemaphoreType.DMA((n, N_PEER)), pltpu.SemaphoreType.DMA((n, N_PEER)), pltpu.SemaphoreType.DMA((n,))]
    make = {"gather": _comm_gather, "scatter": _comm_scatter, "swap": _comm_swap, "chips": _comm_chips}[kind]
    return n, shapes, sems, make


def _exchange(name, kind, arrays):
    n, shapes, sems, make = _comm_parts((kind, arrays))

    def body(*refs):
        start, finish = make(refs[:n], refs[n:2 * n], *refs[2 * n:])
        start()
        finish()

    any_spec = pl.BlockSpec(memory_space=pl.ANY)
    return pl.pallas_call(
        body, name=name, in_specs=[any_spec] * n, out_specs=[any_spec] * n, out_shape=shapes, scratch_shapes=sems,
        compiler_params=pltpu.CompilerParams(has_side_effects=True),
    )(*arrays)


def _call(body, *, name, grid, in_specs, out_specs, out_shape, scratch, sem, args, comm=None, into=None):
    any_spec = pl.BlockSpec(memory_space=pl.ANY)
    in_specs, args, aliases, n_body_in = list(in_specs), list(args), {}, len(in_specs)
    if into is not None:
        in_specs.append(any_spec)
        args.append(into[0])
        aliases = {n_body_in: into[1]}
    n_in, n_out, n_scr = len(in_specs), len(out_specs), len(scratch)
    if comm is None:
        def plain(*refs):
            body(*refs[:n_body_in], *refs[n_in:])

        return pl.pallas_call(plain, name=name, grid=grid, in_specs=in_specs, out_specs=out_specs, out_shape=out_shape,
                              scratch_shapes=scratch, input_output_aliases=aliases, compiler_params=_params(sem))(*args)
    n, shapes, sems, make = _comm_parts(comm)

    def carrier(*refs):
        ins, csrc = refs[:n_body_in], refs[n_in:n_in + n]
        outs, cout = refs[n_in + n:n_in + n + n_out], refs[n_in + n + n_out:n_in + 2 * n + n_out]
        rest = refs[n_in + 2 * n + n_out:]
        start, finish = make(csrc, cout, *rest[n_scr:])
        ids = [pl.program_id(a) for a in range(len(grid))]
        first = functools.reduce(jnp.logical_and, [i == 0 for i in ids])
        last = functools.reduce(jnp.logical_and, [i == g - 1 for i, g in zip(ids, grid)])
        pl.when(first)(start)
        body(*ins, *outs, *rest[:n_scr])
        pl.when(last)(finish)

    return pl.pallas_call(
        carrier, name=name, grid=grid, in_specs=in_specs + [any_spec] * n,
        out_specs=list(out_specs) + [any_spec] * n, out_shape=list(out_shape) + shapes,
        scratch_shapes=list(scratch) + sems, input_output_aliases=aliases,
        compiler_params=pltpu.CompilerParams(dimension_semantics=("arbitrary",) * len(grid),
                                             vmem_limit_bytes=VMEM_LIMIT, has_side_effects=True),
    )(*args, *comm[1])


def _fused_matmul(name, M, N, K, pairs, extras, epilogue, out_dtypes, n_acc, tm, tn, tk, outer="i", comm=None,
                  stack=False, vecs=(), row_sums=0, wide=None):
    nk = K // tk
    n_pairs, n_ex, n_out = len(pairs), len(extras), len(out_dtypes)
    assert not row_sums or (outer == "i" and N == tn)

    def ij(g0, g1):
        return (g0, g1) if outer == "i" else (g1, g0)

    in_specs, args = [], []
    for p in pairs:
        ao, bk, bn = p.get("a_off", 0), p.get("bk_off", 0), p.get("bn_off", 0)
        mode = dict(pipeline_mode=pl.Buffered(1)) if p.get("resident") else {}
        if "a_lead" in p:
            in_specs.append(pl.BlockSpec((None, tm, tk),
                                         lambda g0, g1, k, ao=ao, ld=p["a_lead"]: (ld, ij(g0, g1)[0], k + ao)))
        else:
            in_specs.append(pl.BlockSpec((tm, tk), lambda g0, g1, k, ao=ao: (ij(g0, g1)[0], k + ao)))
        if "b_shift" in p:
            first, shift = p["b_shift"]
            if p.get("trans_b"):
                in_specs.append(pl.BlockSpec(
                    (pl.Element(tn), pl.Element(tk)),
                    lambda g0, g1, k, bk=bk: (
                        pl.multiple_of(ij(g0, g1)[1] * tn + jnp.where(ij(g0, g1)[1] >= first, shift, 0), 16),
                        (k + bk) * tk)))
            else:
                in_specs.append(pl.BlockSpec(
                    (pl.Element(tk), pl.Element(tn)),
                    lambda g0, g1, k, bn=bn: (pl.multiple_of(k * tk + jnp.where(k >= first, shift, 0), 16),
                                              (ij(g0, g1)[1] + bn) * tn)))
        elif p.get("trans_b"):
            in_specs.append(pl.BlockSpec((tn, tk), lambda g0, g1, k, bk=bk, bn=bn: (ij(g0, g1)[1] + bn, k + bk), **mode))
        else:
            in_specs.append(pl.BlockSpec((tk, tn), lambda g0, g1, k, bk=bk, bn=bn: (k + bk, ij(g0, g1)[1] + bn), **mode))
        args += [p["a"], p["b"]]
    for arr, off in extras:
        in_specs.append(pl.BlockSpec((tm, tn), lambda g0, g1, k, off=off: (ij(g0, g1)[0], ij(g0, g1)[1] + off)))
        args.append(arr)
    for arr in vecs:
        in_specs.append(pl.BlockSpec((1, tn), lambda g0, g1, k: (0, ij(g0, g1)[1])))
        args.append(arr)
    if stack:
        out_specs = [pl.BlockSpec((n_out, tm, tn), lambda g0, g1, k: (0,) + ij(g0, g1))]
        out_shape = [jax.ShapeDtypeStruct((n_out, M, N), out_dtypes[0])]
    else:
        out_specs = [pl.BlockSpec((tm, tn), lambda g0, g1, k: ij(g0, g1)) for _ in out_dtypes]
        out_shape = [jax.ShapeDtypeStruct((M, N), dt) for dt in out_dtypes]
    if wide:
        out_specs.append(pl.BlockSpec((pl.Element(tm), pl.Element(wide["width"])),
                                      lambda g0, g1, k: (pl.multiple_of(ij(g0, g1)[0] * tm, 16), wide["col"])))
        out_shape.append(jax.ShapeDtypeStruct((M, wide["total"]), wide["dtype"]))
    n_tile_out = len(out_specs)
    out_specs += [pl.BlockSpec((1, tn), lambda g0, g1, k: (0, 0)) for _ in range(row_sums)]
    out_shape += [jax.ShapeDtypeStruct((1, N), F32) for _ in range(row_sums)]
    grid = (M // tm, N // tn, nk) if outer == "i" else (N // tn, M // tm, nk)
    n_in = 2 * n_pairs + n_ex + len(vecs)

    def partials(refs):
        accs = [None] * n_acc
        for idx, p in enumerate(pairs):
            d = _dot(refs[2 * idx][...], refs[2 * idx + 1][...], NT if p.get("trans_b") else NN)
            accs[p["acc"]] = d if accs[p["acc"]] is None else accs[p["acc"]] + d
        return accs

    def finish(accs, refs, first_rows):
        res = epilogue(accs, [r[...] for r in refs[2 * n_pairs:n_in]])
        if stack:
            o = refs[n_in]
            for idx in range(n_out):
                o[idx] = res[idx].astype(o.dtype)
        else:
            for o, r in zip(refs[n_in:n_in + n_out], res):
                o[...] = r.astype(o.dtype)
        if wide:
            o = refs[n_in + n_tile_out - 1]
            o[...] = res[n_out].astype(o.dtype)
        for o, r in zip(refs[n_in + n_tile_out:n_in + n_tile_out + row_sums], res[n_out + bool(wide):]):
            @pl.when(first_rows)
            def _(o=o, r=r):
                o[...] = r

            @pl.when(jnp.logical_not(first_rows))
            def _(o=o, r=r):
                o[...] += r

    if nk == 1:
        def body(*refs):
            finish(partials(refs), refs, pl.program_id(0) == 0)
        scratch = []
    else:
        def body(*refs):
            acc_refs = refs[-n_acc:]
            k = pl.program_id(2)
            first_rows = pl.program_id(0) == 0
            new = partials(refs)

            @pl.when(k == 0)
            def _():
                for a, v in zip(acc_refs, new):
                    a[...] = v

            @pl.when(k > 0)
            def _():
                for a, v in zip(acc_refs, new):
                    a[...] += v

            @pl.when(k == nk - 1)
            def _():
                finish([a[...] for a in acc_refs], refs, first_rows)
        scratch = [pltpu.VMEM((tm, tn), F32) for _ in range(n_acc)]

    return _call(body, name=name, grid=grid, in_specs=in_specs, out_specs=out_specs, out_shape=out_shape,
                 scratch=scratch, sem=("parallel", "parallel", "arbitrary"), args=args, comm=comm)


def _matmul_tn(name, x, y, t1, t2, tr, scale=1.0, comm=None, out_dtype=BF16, out_skip=None):
    L = x.shape[0] if x.ndim == 3 else 1
    R, K1 = x.shape[-2:]
    N1 = y.shape[1]
    nr, n1 = R // tr, K1 // t1
    if x.ndim == 3:
        x_spec = pl.BlockSpec((None, tr, t1), lambda i, j, r: (i // n1, r, i % n1))
    else:
        x_spec = pl.BlockSpec((tr, t1), lambda i, j, r: (r, i))
    rows_out = L * K1
    o_spec = pl.BlockSpec((t1, t2), lambda i, j, r: (i, j))
    if out_skip:
        row, count = out_skip
        rows_out += count
        o_spec = pl.BlockSpec(
            (pl.Element(t1), pl.Element(t2)),
            lambda i, j, r: (pl.multiple_of(i * t1 + jnp.where(i * t1 >= row, count, 0), 16), j * t2))

    def body(x_ref, y_ref, o_ref, *acc):
        d = _dot(x_ref[...], y_ref[...], TN)
        if nr == 1:
            o_ref[...] = (d * scale).astype(o_ref.dtype)
            return
        r = pl.program_id(2)

        @pl.when(r == 0)
        def _():
            acc[0][...] = d

        @pl.when(jnp.logical_and(r > 0, r < nr - 1))
        def _():
            acc[0][...] += d

        @pl.when(r == nr - 1)
        def _():
            o_ref[...] = ((acc[0][...] + d) * scale).astype(o_ref.dtype)

    return _call(
        body, name=name, grid=(L * n1, N1 // t2, nr),
        in_specs=[x_spec, pl.BlockSpec((tr, t2), lambda i, j, r: (r, j))], out_specs=[o_spec],
        out_shape=[jax.ShapeDtypeStruct((rows_out, N1), out_dtype)],
        scratch=[pltpu.VMEM((t1, t2), F32)] if nr > 1 else [],
        sem=("parallel", "parallel", "arbitrary"), args=(x, y), comm=comm)


def _rmsnorm_fwd(name, h, w):
    M, D = h.shape
    tm = _tile(M, (544, 256, 128))

    def body(h_ref, w_ref, o_ref):
        x = h_ref[...]
        r = lax.rsqrt(jnp.mean(x * x, axis=-1, keepdims=True) + EPS)
        o_ref[...] = (x * r * w_ref[...]).astype(o_ref.dtype)

    return pl.pallas_call(
        body, name=name, grid=(M // tm,),
        in_specs=[pl.BlockSpec((tm, D), lambda i: (i, 0)), pl.BlockSpec((1, D), lambda i: (0, 0))],
        out_specs=pl.BlockSpec((tm, D), lambda i: (i, 0)),
        out_shape=jax.ShapeDtypeStruct((M, D), BF16), compiler_params=_params(("parallel",)),
    )(h, w)


def _rmsnorm_bwd_tile(dn, h, w, dh_in):
    r = lax.rsqrt(jnp.mean(h * h, axis=-1, keepdims=True) + EPS)
    xhat = h * r
    gw = dn * w
    dh = dh_in + r * (gw - xhat * jnp.mean(gw * xhat, axis=-1, keepdims=True))
    return dh, jnp.sum(dn * xhat, axis=0, keepdims=True)


def _loss_head(h, w, target, Bl, nb):
    M, D = h.shape

    def body(h_ref, w_ref, t_ref, dh_ref, dhb_ref, dw_ref, loss_ref):
        b, t = pl.program_id(0), pl.program_id(1)
        live = (t > 0).astype(F32)
        x = h_ref[...]
        r = lax.rsqrt(jnp.mean(x * x, axis=-1, keepdims=True) + EPS)
        xhat = x * r
        wv = w_ref[...]
        err = (xhat * wv - t_ref[0]) * live
        dy = err * (1.0 / D)
        gw = dy * wv
        dx = r * (gw - xhat * jnp.mean(gw * xhat, axis=-1, keepdims=True))
        dh_ref[...] = dx
        dhb_ref[...] = dx.astype(BF16)
        dw = jnp.sum(dy * xhat, axis=0, keepdims=True)
        part = 0.5 * jnp.sum(jnp.sum(err * err, axis=-1, keepdims=True) * (1.0 / D), axis=0, keepdims=True)
        first = jnp.logical_and(b == 0, t == 0)

        @pl.when(first)
        def _():
            dw_ref[...] = dw
            loss_ref[...] = jnp.broadcast_to(part, loss_ref.shape)

        @pl.when(jnp.logical_not(first))
        def _():
            dw_ref[...] += dw
            loss_ref[...] += jnp.broadcast_to(part, loss_ref.shape)

    row = pl.BlockSpec((Q, D), lambda b, t: (b * nb + t, 0))
    vec = pl.BlockSpec((1, D), lambda b, t: (0, 0))
    return pl.pallas_call(
        body, name="loss_head", grid=(Bl, nb),
        in_specs=[row, vec, pl.BlockSpec((1, Q, D), lambda b, t: (b, jnp.maximum(t - 1, 0), 0))],
        out_specs=[row, row, vec, pl.BlockSpec((8, 128), lambda b, t: (0, 0))],
        out_shape=[jax.ShapeDtypeStruct((M, D), F32), jax.ShapeDtypeStruct((M, D), BF16),
                   jax.ShapeDtypeStruct((1, D), F32), jax.ShapeDtypeStruct((8, 128), F32)],
        compiler_params=_params(("arbitrary", "arbitrary")),
    )(h, w, target)


CONV_TC = 256


def _conv_pre(xr_ref, w_ref, b_ref):
    x = xr_ref[...].astype(F32)
    acc = b_ref[...] + w_ref[SSD_CONV - 1:SSD_CONV, :] * x
    for k in range(1, SSD_CONV):
        acc = acc + w_ref[SSD_CONV - 1 - k:SSD_CONV - k, :] * pltpu.roll(x, k, 0)
    return x, acc


def _conv_fwd(proj, w, b, Bl, T):
    M = proj.shape[0]
    off = 1024 // CONV_TC

    def body(xr_ref, w_ref, b_ref, o_ref):
        _, acc = _conv_pre(xr_ref, w_ref, b_ref)
        row = lax.broadcasted_iota(jnp.int32, acc.shape, 0)
        o_ref[...] = jnp.where(row >= PAD, acc * _sigmoid(acc), 0.0).astype(o_ref.dtype)

    return pl.pallas_call(
        body, name="conv_fwd", grid=(Bl, SSD_CONV_CH // CONV_TC),
        in_specs=[pl.BlockSpec((T, CONV_TC), lambda bb, j: (bb, j + off)),
                  pl.BlockSpec((SSD_CONV, CONV_TC), lambda bb, j: (0, j)), pl.BlockSpec((1, CONV_TC), lambda bb, j: (0, j))],
        out_specs=pl.BlockSpec((T, CONV_TC), lambda bb, j: (bb, j)),
        out_shape=jax.ShapeDtypeStruct((M, SSD_CONV_CH), BF16), compiler_params=_params(("parallel", "parallel")),
    )(proj, w, b)


def _conv_bwd(proj, w, b, dxc, dproj, Bl, T):
    M = proj.shape[0]
    off = 1024 // CONV_TC

    def body(xr_ref, w_ref, b_ref, d_ref, dx_ref, dw_ref, db_ref):
        x, acc = _conv_pre(xr_ref, w_ref, b_ref)
        row = lax.broadcasted_iota(jnp.int32, acc.shape, 0)
        s = _sigmoid(acc)
        dpre = jnp.where(row >= PAD, d_ref[...].astype(F32) * _dsilu(acc, s), 0.0)
        dx = w_ref[SSD_CONV - 1:SSD_CONV, :] * dpre
        dws = [jnp.sum(dpre * x, axis=0, keepdims=True)]
        for k in range(1, SSD_CONV):
            dx = dx + w_ref[SSD_CONV - 1 - k:SSD_CONV - k, :] * pltpu.roll(dpre, T - k, 0)
            dws.append(jnp.sum(dpre * pltpu.roll(x, k, 0), axis=0, keepdims=True))
        dx_ref[...] = dx.astype(dx_ref.dtype)
        dw = jnp.concatenate(dws[::-1], axis=0)
        db = jnp.sum(dpre, axis=0, keepdims=True)

        @pl.when(pl.program_id(1) == 0)
        def _():
            dw_ref[...] = dw
            db_ref[...] = db

        @pl.when(pl.program_id(1) > 0)
        def _():
            dw_ref[...] += dw
            db_ref[...] += db

    return _call(
        body, name="conv_bwd", grid=(SSD_CONV_CH // CONV_TC, Bl),
        in_specs=[pl.BlockSpec((T, CONV_TC), lambda j, bb: (bb, j + off)),
                  pl.BlockSpec((SSD_CONV, CONV_TC), lambda j, bb: (0, j)), pl.BlockSpec((1, CONV_TC), lambda j, bb: (0, j)),
                  pl.BlockSpec((T, CONV_TC), lambda j, bb: (bb, j))],
        out_specs=[pl.BlockSpec((T, CONV_TC), lambda j, bb: (bb, j + off)),
                   pl.BlockSpec((SSD_CONV, CONV_TC), lambda j, bb: (0, j)), pl.BlockSpec((1, CONV_TC), lambda j, bb: (0, j))],
        out_shape=[jax.ShapeDtypeStruct(dproj.shape, BF16), jax.ShapeDtypeStruct((SSD_CONV, SSD_CONV_CH), F32),
                   jax.ShapeDtypeStruct((1, SSD_CONV_CH), F32)],
        scratch=[], sem=("parallel", "arbitrary"), args=(proj, w, b, dxc), into=(dproj, 0))


N_PAIR = SSD_HEADS // 2
HPG = SSD_HEADS // SSD_GROUPS
GW = SSD_INNER // SSD_GROUPS


def _per_group(fn, *arrs):
    return jnp.concatenate([jnp.broadcast_to(fn(*(a[:, GW * g:GW * (g + 1)] for a in arrs)), (arrs[0].shape[0], GW))
                            for g in range(SSD_GROUPS)], axis=1)


def _ssd_prep(c, dtr_ref, bias_ref, alog_ref, d_ref):
    row = lax.broadcasted_iota(jnp.int32, (Q, 128), 0)
    col = lax.broadcasted_iota(jnp.int32, (Q, 128), 1)
    live = col < SSD_HEADS
    valid = jnp.logical_and(jnp.logical_or(c > 0, row >= PAD), live)
    pre = dtr_ref[...] + bias_ref[...]
    dt = jnp.where(valid, _softplus(pre), 0.0)
    A = jnp.where(live[0:1], -jnp.exp(alog_ref[...]), 0.0)
    tri = row >= col
    eye = (row == col).astype(BF16)
    cs = _dot01(tri, dt * A, NN, "a")
    cst = _dot01(eye, cs, NT, "a")
    spread = (lax.broadcasted_iota(jnp.int32, (128, SSD_INNER), 0)
              == lax.broadcasted_iota(jnp.int32, (128, SSD_INNER), 1) // SSD_HEAD_DIM).astype(BF16)
    dt_w = _dot01(dt, spread, NN, "b")
    cs_w = _dot01(cs, spread, NN, "b")
    d_w = _dot01(jnp.broadcast_to(d_ref[...], (8, 128)), spread, NN, "b")[0:1]
    lane = lax.broadcasted_iota(jnp.int32, (Q, SSD_INNER), 1)
    first = (lane % 128) < SSD_HEAD_DIM
    return dict(row=row, col=col, valid=valid, pre=pre, dt=dt, A=A, tri=tri, eye=eye, cs=cs, cst=cst, spread=spread,
                dt_w=dt_w, cs_w=cs_w, d_w=d_w, ecs_w=jnp.exp(cs_w), decay_w=jnp.exp(cs_w[Q - 1:Q] - cs_w), first=first)


def _ssd_chunk(xc_ref, s, states):
    xv = xc_ref[:, 0:SSD_INNER].astype(F32)
    Bs = [xc_ref[:, SSD_INNER + 128 * g:SSD_INNER + 128 * (g + 1)] for g in range(SSD_GROUPS)]
    Cs = [xc_ref[:, SSD_INNER + 512 + 128 * g:SSD_INNER + 512 + 128 * (g + 1)] for g in range(SSD_GROUPS)]
    X = xv * s["dt_w"]
    X0 = jnp.where(s["first"], X, 0.0)
    Xb = (X0.astype(BF16), (X - X0).astype(BF16))
    Xd = (X * s["decay_w"]).astype(BF16)
    CB = [_dot(Cs[g], Bs[g], NT) for g in range(SSD_GROUPS)]
    Lms = [jnp.exp(jnp.where(s["tri"], s["cs"][:, h:h + 1] - s["cst"][h:h + 1, :], -jnp.inf)) for h in range(SSD_HEADS)]
    Ms = [CB[h // HPG] * Lms[h] for h in range(SSD_HEADS)]
    Mb = [m.astype(BF16) for m in Ms]
    prev_b = [st.astype(BF16) for st in states]
    yds, yos, sts = [], [], []
    for p in range(N_PAIR):
        g, ln = p // 2, slice(128 * p, 128 * (p + 1))
        yds.append(_dot(Mb[2 * p], Xb[0][:, ln], NN) + _dot(Mb[2 * p + 1], Xb[1][:, ln], NN))
        yos.append(_dot(Cs[g], prev_b[p], NT))
        sts.append(_dot(Xd[:, ln], Bs[g], TN))
    yo = jnp.concatenate(yos, axis=1)
    y = jnp.concatenate(yds, axis=1) + yo * s["ecs_w"] + xv * s["d_w"]
    upper = s["row"] < SSD_HEAD_DIM
    cl = s["cs"][Q - 1:Q, :]
    ecl_rows = [jnp.where(upper, jnp.exp(cl[:, 2 * p:2 * p + 1]), jnp.exp(cl[:, 2 * p + 1:2 * p + 2])) for p in range(N_PAIR)]
    new_states = [states[p] * ecl_rows[p] + sts[p] for p in range(N_PAIR)]
    return y, new_states, dict(xv=xv, Bs=Bs, Cs=Cs, X=X, Xb=Xb, CB=CB, Lms=Lms, Ms=Ms, Mb=Mb, prev_b=prev_b, yo=yo,
                               ecl_rows=ecl_rows)


def _ssd_in_specs(nc, rev=False):
    rb = (lambda b, c: b * nc + nc - 1 - c) if rev else (lambda b, c: b * nc + c)
    vec = pl.BlockSpec((1, 128), lambda b, c: (0, 0))
    return [pl.BlockSpec((Q, SSD_CONV_CH), lambda b, c: (rb(b, c), 0)),
            pl.BlockSpec((Q, 128), lambda b, c: (rb(b, c), 0)),
            pl.BlockSpec((Q, SSD_INNER), lambda b, c: (rb(b, c), 0)),
            vec, vec, vec, pl.BlockSpec((1, SSD_INNER), lambda b, c: (0, 0))]


def _ssd_fwd(xc, dtr, proj, bias_p, alog_p, d_p, nw, Bl, nc):
    M = xc.shape[0]

    def body(xc_ref, dtr_ref, z_ref, bias_ref, alog_ref, d_ref, nw_ref, y_ref, prev_ref, state):
        c = pl.program_id(1)

        @pl.when(c == 0)
        def _():
            state[...] = jnp.zeros_like(state)

        s = _ssd_prep(c, dtr_ref, bias_ref, alog_ref, d_ref)
        states = [state[p] for p in range(N_PAIR)]
        y, new_states, _ = _ssd_chunk(xc_ref, s, states)
        for p in range(N_PAIR):
            prev_ref[0, 0, p] = states[p]
            state[p] = new_states[p]
        zz = z_ref[...].astype(F32)
        yg = y * zz * _sigmoid(zz)
        r = _per_group(lambda a: lax.rsqrt(jnp.mean(a * a, axis=-1, keepdims=True) + EPS), yg)
        y_ref[...] = (yg * r * nw_ref[...]).astype(y_ref.dtype)

    return pl.pallas_call(
        body, name="ssd_fwd", grid=(Bl, nc), in_specs=_ssd_in_specs(nc),
        out_specs=[pl.BlockSpec((Q, SSD_INNER), lambda b, c: (b * nc + c, 0)),
                   pl.BlockSpec((1, 1, N_PAIR, 128, 128), lambda b, c: (b, c, 0, 0, 0))],
        out_shape=[jax.ShapeDtypeStruct((M, SSD_INNER), BF16), jax.ShapeDtypeStruct((Bl, nc, N_PAIR, 128, 128), F32)],
        scratch_shapes=[pltpu.VMEM((N_PAIR, 128, 128), F32)],
        compiler_params=_params(("arbitrary", "arbitrary")),
    )(xc, dtr, proj, bias_p, alog_p, d_p, nw)


def _ssd_bwd(xc, dtr, proj, bias_p, alog_p, d_p, nw, prev, dya, dproj, Bl, nc, comm=None):
    M = xc.shape[0]

    def body(xc_ref, dtr_ref, z_ref, bias_ref, alog_ref, d_ref, nw_ref, prev_ref, dy_ref,
             dxc_ref, dz_ref, ddtr_ref, dbias_ref, dalog_ref, dd_ref, dnw_ref, dS):
        b, t = pl.program_id(0), pl.program_id(1)

        @pl.when(t == 0)
        def _():
            dS[...] = jnp.zeros_like(dS)

        s = _ssd_prep(nc - 1 - t, dtr_ref, bias_ref, alog_ref, d_ref)
        states = [prev_ref[0, 0, p] for p in range(N_PAIR)]
        y, _, k = _ssd_chunk(xc_ref, s, states)
        xv, Bs, Cs, Xb = k["xv"], k["Bs"], k["Cs"], k["Xb"]

        zz = z_ref[...].astype(F32)
        sz = _sigmoid(zz)
        silu_z = zz * sz
        yg = y * silu_z
        r = _per_group(lambda a: lax.rsqrt(jnp.mean(a * a, axis=-1, keepdims=True) + EPS), yg)
        xhat = yg * r
        dout = dy_ref[...].astype(F32)
        gw = dout * nw_ref[...]
        dyg = r * (gw - xhat * _per_group(lambda a, c2: jnp.mean(a * c2, axis=-1, keepdims=True), gw, xhat))
        dnw = jnp.sum(dout * xhat, axis=0, keepdims=True)
        dz_ref[...] = (dyg * y * _dsilu(zz, sz)).astype(dz_ref.dtype)
        dy = dyg * silu_z
        dy0 = jnp.where(s["first"], dy, 0.0)
        dyb = (dy0.astype(BF16), (dy - dy0).astype(BF16))
        dYo = (dy * s["ecs_w"]).astype(BF16)

        dS_f = [dS[p] for p in range(N_PAIR)]
        dS_b = [d.astype(BF16) for d in dS_f]
        BdS, dXm, dprev, dCs, dMs, XdS = [], [], [], [[] for _ in range(SSD_GROUPS)], [], []
        for p in range(N_PAIR):
            g, ln = p // 2, slice(128 * p, 128 * (p + 1))
            BdS.append(_dot(Bs[g], dS_b[p], NT))
            dXm.append(_dot(k["Mb"][2 * p], dyb[0][:, ln], TN) + _dot(k["Mb"][2 * p + 1], dyb[1][:, ln], TN))
            dprev.append(_dot(dYo[:, ln], Cs[g], TN))
            dCs[g].append(_dot(dYo[:, ln], k["prev_b"][p], NN))
            for hh in range(2):
                dMs.append(_dot(dyb[hh][:, ln], Xb[hh][:, ln], NT))
                XdS.append(_dot(Xb[hh][:, ln], dS_b[p], NN))
        dX = jnp.concatenate(dXm, axis=1) + s["decay_w"] * jnp.concatenate(BdS, axis=1)
        dxs = dy * s["d_w"] + dX * s["dt_w"]

        heads = lambda a: _dot01(a, s["spread"], NT, "b")
        ddt = heads(dX * xv)
        dcs = heads(dy * k["yo"] * s["ecs_w"])
        dD = jnp.sum(heads(dy * xv), axis=0, keepdims=True)

        col, row = s["col"], s["row"]
        lane1 = col[0:1]
        rowsT = lax.broadcasted_iota(jnp.int32, (128, Q), 0)
        dcs_t = jnp.zeros((128, Q), F32)
        dcl = jnp.zeros((1, 128), F32)
        dB_out, dC_out = [], []
        for g in range(SSD_GROUPS):
            Bf = Bs[g].astype(F32)
            dCB = jnp.zeros((Q, Q), F32)
            dBacc = jnp.zeros((Q, 128), F32)
            for r4 in range(HPG):
                h = HPG * g + r4
                p, hh = h // 2, h % 2
                W = dMs[h] * k["Ms"][h]
                dCB = dCB + dMs[h] * k["Lms"][h]
                decay_h = s["decay_w"][:, SSD_HEAD_DIM * h:SSD_HEAD_DIM * h + 1]
                dBacc = dBacc + decay_h * XdS[h]
                tdec = jnp.sum(XdS[h] * Bf, axis=1, keepdims=True) * decay_h
                dcs = dcs + jnp.where(col == h, jnp.sum(W, axis=1, keepdims=True) - tdec, 0.0)
                dcs_t = dcs_t - jnp.where(rowsT == h, jnp.sum(W, axis=0, keepdims=True), 0.0)
                rows_h = (row < SSD_HEAD_DIM) if hh == 0 else (row >= SSD_HEAD_DIM)
                sprev = jnp.sum(jnp.sum(jnp.where(rows_h, dS_f[p] * states[p], 0.0), axis=1, keepdims=True),
                                axis=0, keepdims=True)
                ecl = jnp.exp(s["cs"][Q - 1:Q, h:h + 1])
                dcl = dcl + jnp.where(lane1 == h, jnp.sum(tdec, axis=0, keepdims=True) + ecl * sprev, 0.0)
            dCB_b = dCB.astype(BF16)
            dC_out.append(dCs[g][0] + dCs[g][1] + _dot(dCB_b, Bs[g], NN))
            dB_out.append(dBacc + _dot(dCB_b, Cs[g], TN))
        for p in range(N_PAIR):
            dS[p] = dS_f[p] * k["ecl_rows"][p] + dprev[p]
        dxc_ref[...] = jnp.concatenate([dxs] + dB_out + dC_out, axis=1).astype(dxc_ref.dtype)

        dcs = dcs + _dot01(s["eye"], dcs_t, NT, "a") + jnp.where(row == Q - 1, dcl, 0.0)
        da = _dot01(row <= col, dcs, NN, "a")
        ddt = ddt + da * s["A"]
        dpre = jnp.where(s["valid"], ddt * _sigmoid(s["pre"]), 0.0)
        ddtr_ref[...] = dpre
        dbias = jnp.sum(dpre, axis=0, keepdims=True)
        dalog = jnp.sum(da * s["dt"], axis=0, keepdims=True) * s["A"]
        first_step = jnp.logical_and(b == 0, t == 0)

        @pl.when(first_step)
        def _():
            dbias_ref[...] = dbias
            dalog_ref[...] = dalog
            dd_ref[...] = dD
            dnw_ref[...] = dnw

        @pl.when(jnp.logical_not(first_step))
        def _():
            dbias_ref[...] += dbias
            dalog_ref[...] += dalog
            dd_ref[...] += dD
            dnw_ref[...] += dnw

    rb = lambda b, c: b * nc + nc - 1 - c
    rowblk = lambda w: pl.BlockSpec((Q, w), lambda b, c: (rb(b, c), 0))
    vec = lambda w: pl.BlockSpec((1, w), lambda b, c: (0, 0))
    return _call(
        body, name="ssd_bwd", grid=(Bl, nc),
        in_specs=_ssd_in_specs(nc, rev=True) + [
            pl.BlockSpec((1, 1, N_PAIR, 128, 128), lambda b, c: (b, nc - 1 - c, 0, 0, 0)), rowblk(SSD_INNER)],
        out_specs=[rowblk(SSD_CONV_CH), rowblk(SSD_INNER), rowblk(128), vec(128), vec(128), vec(128), vec(SSD_INNER)],
        out_shape=[jax.ShapeDtypeStruct((M, SSD_CONV_CH), BF16), jax.ShapeDtypeStruct(dproj.shape, BF16),
                   jax.ShapeDtypeStruct((M, 128), F32), jax.ShapeDtypeStruct((1, 128), F32),
                   jax.ShapeDtypeStruct((1, 128), F32), jax.ShapeDtypeStruct((1, 128), F32),
                   jax.ShapeDtypeStruct((1, SSD_INNER), F32)],
        scratch=[pltpu.VMEM((N_PAIR, 128, 128), F32)], sem=("arbitrary", "arbitrary"),
        args=(xc, dtr, proj, bias_p, alog_p, d_p, nw, prev, dya), comm=comm, into=(dproj, 1))


NSUB = Q // HG_CHUNK
HG_HP = 8
EXP_CAP = 80.0


def _hg_setup(blk, q_ref, f_ref, hb_ref):
    row = lax.broadcasted_iota(jnp.int32, (Q, Q), 0)
    col = lax.broadcasted_iota(jnp.int32, (Q, Q), 1)
    same = (row // HG_CHUNK) == (col // HG_CHUNK)
    causal = jnp.logical_and(same, col <= row)
    lb = _sigmoid(hb_ref[0:1, :] - hb_ref[1:2, :])
    fl = f_ref[...].astype(F32)
    sg = _sigmoid(fl)
    fg = lb + (1.0 - lb) * sg
    k = (1.0 - lb) * (1.0 - sg)
    gl = jnp.log(fg)
    G = _dot01(causal, gl, NN, "a")
    T = _dot01(same, gl, NN, "a")
    qv = q_ref[...].astype(F32)
    sq = _sigmoid(qv)
    eG = jnp.exp(G)
    eGn = jnp.exp(jnp.minimum(-G, EXP_CAP))
    eTG = jnp.exp(T - G)
    qt = qv * sq * eG
    kt = k * eGn
    kh = k * eTG
    valid = jnp.logical_or(blk > 0, row[:, :1] >= PAD)
    return dict(row=row, col=col, same=same, causal=causal, lb=lb, sg=sg, fg=fg, k=k, T=T, qv=qv, sq=sq,
                eG=eG, eGn=eGn, eTG=eTG, qt=qt, kt=kt, kh=kh, valid=valid)


def _hg_specs(nb, rev=False):
    rb = (lambda h, b, t: b * nb + nb - 1 - t) if rev else (lambda h, b, t: b * nb + t)
    w = 128 * HG_HP
    blk = lambda off: pl.BlockSpec((Q, w), lambda h, b, t, off=off: (rb(h, b, t), off // HG_HP + h))
    return [blk(24), blk(32), blk(40), blk(48),
            pl.BlockSpec((2, w), lambda h, b, t: (0, h)), pl.BlockSpec((1, w), lambda h, b, t: (0, h))]


HEAD_LANES = tuple(slice(128 * hh, 128 * (hh + 1)) for hh in range(HG_HP))


def _per_head(fn, *arrs):
    return jnp.concatenate([jnp.broadcast_to(fn(*(a[:, ln] for a in arrs)), (arrs[0].shape[0], 128))
                            for ln in HEAD_LANES], axis=1)


def _hgrn_fwd(proj, hb, nw, Bl, nb, comm=None):
    M = proj.shape[0]

    def body(q_ref, f_ref, i_ref, g_ref, hb_ref, nw_ref, y_ref, o_ref, st_ref, S):
        blk = pl.program_id(2)

        @pl.when(blk == 0)
        def _():
            S[...] = jnp.zeros_like(S)

        s = _hg_setup(blk, q_ref, f_ref, hb_ref)
        v = i_ref[...]
        qt_b, kt_b, kh_b = s["qt"].astype(BF16), s["kt"].astype(BF16), s["kh"].astype(BF16)
        eT = jnp.exp(s["T"])
        att = [jnp.where(s["causal"], _dot(qt_b[:, ln], kt_b[:, ln], NT), 0.0).astype(BF16) for ln in HEAD_LANES]
        o_intra = [_dot(att[hh], v[:, ln], NN) for hh, ln in enumerate(HEAD_LANES)]
        for j in range(NSUB):
            sl = slice(HG_CHUNK * j, HG_CHUNK * (j + 1))
            for hh, ln in enumerate(HEAD_LANES):
                St = S[hh]
                st_ref[0, hh, 0, j] = St
                o_ref[sl, ln] = o_intra[hh][sl] + _dot(qt_b[sl, ln], St.astype(BF16), NT)
                S[hh] = St * eT[HG_CHUNK * j:HG_CHUNK * j + 1, ln] + _dot(v[sl, ln], kh_b[sl, ln], TN)
        o = o_ref[...]
        r = _per_head(lambda a: lax.rsqrt(jnp.mean(a * a, axis=-1, keepdims=True) + EPS), o)
        gv = g_ref[...].astype(F32)
        y_ref[...] = (o * r * nw_ref[...] * gv * _sigmoid(gv)).astype(y_ref.dtype)

    rowblk = pl.BlockSpec((Q, 128 * HG_HP), lambda h, b, t: (b * nb + t, h))
    return _call(
        body, name="hgrn_fwd", grid=(HG_HEADS // HG_HP, Bl, nb), in_specs=_hg_specs(nb),
        out_specs=[rowblk, rowblk,
                   pl.BlockSpec((1, HG_HP, 1, NSUB, 128, 128), lambda h, b, t: (b, h, t, 0, 0, 0))],
        out_shape=[jax.ShapeDtypeStruct((M, HG_WIDTH), BF16), jax.ShapeDtypeStruct((M, HG_WIDTH), F32),
                   jax.ShapeDtypeStruct((Bl, HG_HEADS, nb, NSUB, 128, 128), F32)],
        scratch=[pltpu.VMEM((HG_HP, 128, 128), F32)], sem=("parallel", "arbitrary", "arbitrary"),
        args=(proj, proj, proj, proj, hb, nw), comm=comm)


def _hgrn_bwd(proj, hb, nw, o_saved, st_saved, dyb, dproj, Bl, nb, comm=None):
    assert HG_HP == HG_HEADS

    def body(q_ref, f_ref, i_ref, g_ref, hb_ref, nw_ref, o_ref, st_ref, dy_ref,
             d_ref, dhb_ref, dnw_ref, dS, a_dqt, a_dv, a_dkh, a_dgl):
        b, t = pl.program_id(1), pl.program_id(2)

        @pl.when(t == 0)
        def _():
            dS[...] = jnp.zeros_like(dS)

        first_step = jnp.logical_and(b == 0, t == 0)
        s = _hg_setup(nb - 1 - t, q_ref, f_ref, hb_ref)
        v = i_ref[...]
        qt_b, kt_b, kh_b = s["qt"].astype(BF16), s["kt"].astype(BF16), s["kh"].astype(BF16)
        eT = jnp.exp(s["T"])
        att = [jnp.where(s["causal"], _dot(qt_b[:, ln], kt_b[:, ln], NT), 0.0).astype(BF16) for ln in HEAD_LANES]

        o = o_ref[...]
        r = _per_head(lambda a: lax.rsqrt(jnp.mean(a * a, axis=-1, keepdims=True) + EPS), o)
        xhat = o * r
        gv = g_ref[...].astype(F32)
        sgv = _sigmoid(gv)
        dyv = dy_ref[...].astype(F32)
        d_on = dyv * gv * sgv
        dg_out = dyv * xhat * nw_ref[...] * _dsilu(gv, sgv)
        gw = d_on * nw_ref[...]
        do = r * (gw - xhat * _per_head(lambda a, c: jnp.mean(a * c, axis=-1, keepdims=True), gw, xhat))
        dnw = jnp.sum(d_on * xhat, axis=0, keepdims=True)
        do_b = do.astype(BF16)

        datt = [jnp.where(s["causal"], _dot(do_b[:, ln], v[:, ln], NT), 0.0).astype(BF16) for ln in HEAD_LANES]
        dqt = jnp.concatenate([_dot(datt[hh], kt_b[:, ln], NN) for hh, ln in enumerate(HEAD_LANES)], axis=1)
        dkt = jnp.concatenate([_dot(datt[hh], qt_b[:, ln], TN) for hh, ln in enumerate(HEAD_LANES)], axis=1)
        dv = jnp.concatenate([_dot(att[hh], do_b[:, ln], TN) for hh, ln in enumerate(HEAD_LANES)], axis=1)
        last_row = (lax.broadcasted_iota(jnp.int32, (HG_CHUNK, 128), 0) == HG_CHUNK - 1)
        for j in reversed(range(NSUB)):
            sl = slice(HG_CHUNK * j, HG_CHUNK * (j + 1))
            for hh, ln in enumerate(HEAD_LANES):
                St = st_ref[0, hh, 0, j]
                dSt = dS[hh]
                St_b, dSt_b = St.astype(BF16), dSt.astype(BF16)
                eT_j = eT[HG_CHUNK * j:HG_CHUNK * j + 1, ln]
                dkh_j = _dot(v[sl, ln], dSt_b, NN)
                a_dqt[sl, ln] = _dot(do_b[sl, ln], St_b, NN)
                a_dv[sl, ln] = _dot(kh_b[sl, ln], dSt_b, NT)
                a_dkh[sl, ln] = dkh_j
                dlast = (jnp.sum(St * dSt, axis=0, keepdims=True) * eT_j
                         + jnp.sum(dkh_j * s["kh"][sl, ln], axis=0, keepdims=True))
                a_dgl[sl, ln] = jnp.where(last_row, dlast, 0.0)
                dS[hh] = dSt * eT_j + _dot(do_b[sl, ln], qt_b[sl, ln], TN)
        dqt = dqt + a_dqt[...]
        dv = dv + a_dv[...]
        dkh = a_dkh[...]
        dG = dqt * s["qt"] - dkt * s["kt"] - dkh * s["kh"] + a_dgl[...]
        rev_causal = jnp.logical_and(s["same"], s["col"] >= s["row"])
        dgl = _dot01(rev_causal, dG, NN, "a")
        dk = dkt * s["eGn"] + dkh * s["eTG"]
        dfg = dgl / s["fg"] - dk
        lb, sg = s["lb"], s["sg"]
        keep = s["valid"].astype(F32)
        d_ref[:, 0:w] = (dqt * s["eG"] * _dsilu(s["qv"], s["sq"]) * keep).astype(d_ref.dtype)
        d_ref[:, w:2 * w] = (dfg * (1.0 - lb) * sg * (1.0 - sg) * keep).astype(d_ref.dtype)
        d_ref[:, 2 * w:3 * w] = (dv * keep).astype(d_ref.dtype)
        d_ref[:, 3 * w:4 * w] = (dg_out * keep).astype(d_ref.dtype)
        dlb = jnp.sum(dfg * (1.0 - sg) * keep, axis=0, keepdims=True) * lb * (1.0 - lb)
        dhb = jnp.concatenate([dlb, -dlb], axis=0)

        @pl.when(first_step)
        def _():
            dhb_ref[...] = dhb
            dnw_ref[...] = dnw

        @pl.when(jnp.logical_not(first_step))
        def _():
            dhb_ref[...] += dhb
            dnw_ref[...] += dnw

    w = 128 * HG_HP
    rowblk = pl.BlockSpec((Q, w), lambda h, b, t: (b * nb + nb - 1 - t, h))
    return _call(
        body, name="hgrn_bwd", grid=(HG_HEADS // HG_HP, Bl, nb),
        in_specs=_hg_specs(nb, rev=True) + [
            rowblk, pl.BlockSpec((1, HG_HP, 1, NSUB, 128, 128), lambda h, b, t: (b, h, nb - 1 - t, 0, 0, 0)), rowblk],
        out_specs=[pl.BlockSpec((pl.Element(Q), pl.Element(4 * w)),
                                lambda h, b, t: (pl.multiple_of((b * nb + nb - 1 - t) * Q, Q), 3 * HG_WIDTH)),
                   pl.BlockSpec((2, w), lambda h, b, t: (0, h)), pl.BlockSpec((1, w), lambda h, b, t: (0, h))],
        out_shape=[jax.ShapeDtypeStruct(dproj.shape, BF16),
                   jax.ShapeDtypeStruct((2, HG_WIDTH), F32), jax.ShapeDtypeStruct((1, HG_WIDTH), F32)],
        scratch=[pltpu.VMEM((HG_HP, 128, 128), F32)] + [pltpu.VMEM((Q, w), F32)] * 4,
        sem=("parallel", "arbitrary", "arbitrary"),
        args=(proj, proj, proj, proj, hb, nw, o_saved, st_saved, dyb), comm=comm, into=(dproj, 0))


def _adamw(name, parts, w, m, v):
    R, C = w.shape
    S = parts.shape[0]
    tr, tc = (_tile(R, (256, 176, 128, 64, 8)), C) if R % 8 == 0 else (R, 256)
    c1, c2 = 1.0 - ADAM_B1 ** ADAM_STEP, 1.0 - ADAM_B2 ** ADAM_STEP

    def body(p_ref, w_ref, m_ref, v_ref, g_ref, d_ref, nm_ref, nv_ref):
        g = p_ref[0].astype(F32)
        for s in range(1, S):
            g = g + p_ref[s].astype(F32)
        nm = ADAM_B1 * m_ref[...] + (1.0 - ADAM_B1) * g
        nv = ADAM_B2 * v_ref[...] + (1.0 - ADAM_B2) * (g * g)
        g_ref[...] = g
        nm_ref[...] = nm
        nv_ref[...] = nv
        d_ref[...] = -ADAM_LR * ((nm / c1) / (jnp.sqrt(nv / c2) + ADAM_EPS) + ADAM_WD * w_ref[...])

    blk = pl.BlockSpec((tr, tc), lambda i, j: (i, j))
    return pl.pallas_call(
        body, name=name, grid=(R // tr, C // tc),
        in_specs=[pl.BlockSpec((S, tr, tc), lambda i, j: (0, i, j)), blk, blk, blk], out_specs=[blk] * 4,
        out_shape=[jax.ShapeDtypeStruct((R, C), F32)] * 4, compiler_params=_params(("parallel", "parallel")),
    )(parts, w, m, v)


def _pair_sum(name, by_core, arrived):
    _, J, R, C = by_core.shape
    tc = _tile(C, (512, 256, 128))

    def body(c_ref, a_ref, b_ref, o_ref):
        o_ref[...] = (a_ref[0].astype(F32) + b_ref[...].astype(F32)).astype(o_ref.dtype)

    blk = pl.BlockSpec((1, R, tc), lambda j, k, c_ref: (j, 0, k))
    return pl.pallas_call(
        body, name=name,
        grid_spec=pltpu.PrefetchScalarGridSpec(
            num_scalar_prefetch=1, grid=(J, C // tc),
            in_specs=[pl.BlockSpec((1, 1, R, tc), lambda j, k, c_ref: (c_ref[0], j, 0, k)), blk], out_specs=blk),
        out_shape=jax.ShapeDtypeStruct(arrived.shape, arrived.dtype), compiler_params=_params(("parallel", "parallel")),
    )(lax.axis_index("c").astype(jnp.int32).reshape(1), by_core, arrived)


def _sum_parts(name, parts):
    S, R, C = parts.shape

    def body(p_ref, o_ref):
        g = p_ref[0]
        for s in range(1, S):
            g = g + p_ref[s]
        o_ref[...] = g

    return pl.pallas_call(
        body, name=name, out_shape=jax.ShapeDtypeStruct((R, C), F32),
        in_specs=[pl.BlockSpec(memory_space=pltpu.VMEM)], out_specs=pl.BlockSpec(memory_space=pltpu.VMEM),
    )(parts)


def _heads_to_lanes(p):
    return jnp.pad(p, [(0, 0)] * (p.ndim - 1) + [(0, 128 - SSD_HEADS)])


def _lanes_to_heads(p):
    return p[..., :SSD_HEADS]


def _pack_rows(arrs):
    rows = []
    for a in arrs:
        f = a.reshape(-1).astype(F32)
        n = -(-f.shape[0] // D_MODEL) * D_MODEL
        rows.append(jnp.pad(f, (0, n - f.shape[0])).reshape(-1, D_MODEL))
    out = jnp.concatenate(rows, axis=0)
    return jnp.pad(out, ((0, (-out.shape[0]) % 8), (0, 0)))


def _unpack_rows(packed, like):
    outs, r = [], 0
    for a in like:
        n = 1
        for s in a.shape:
            n *= s
        nr = -(-n // D_MODEL)
        outs.append(packed[r:r + nr].reshape(-1)[:n].reshape(a.shape))
        r += nr
    return outs


def _cols(gth):
    return jnp.transpose(gth, (1, 0, 2)).reshape(gth.shape[1], -1)


def _rows(gth):
    return gth.reshape(-1, gth.shape[2])


def _to_rows(g):
    return g.reshape(N_DEV, -1, g.shape[1]).astype(BF16)


def _by_core(g):
    return jnp.transpose(g.reshape(N_DEV // 2, 2, -1, g.shape[1]), (1, 0, 2, 3)).astype(BF16)


DT_ROW = 3072


def _chip_sums(tag, by_core, swap_in=None):
    arrived = swap_in(by_core) if swap_in else _exchange(tag + "_swap", "swap", by_core)
    return [_pair_sum(f"{tag}_chipsum{i}", m, a) for i, (m, a) in enumerate(zip(by_core, arrived))]


def _ffn_fwd_gu(tag, h, norm_w, w_gu_t, comm=None):
    M = h.shape[0]
    F = w_gu_t.shape[0] // 2
    tm = _tile(M, (544, 256))
    n = _rmsnorm_fwd(tag + "_norm", h, norm_w)
    tn = _tile(F, (1408, 704, 256))
    outs = _fused_matmul(
        tag + "_gu", M, F, D_MODEL,
        [dict(a=n, b=w_gu_t, trans_b=True, acc=0), dict(a=n, b=w_gu_t, trans_b=True, bn_off=F // tn, acc=1)], [],
        lambda accs, ex: (accs[0], accs[1], accs[0] * _sigmoid(accs[0]) * accs[1]),
        [BF16, BF16, BF16], 2, tm, tn, D_MODEL, outer="j", comm=comm)
    return (n, *outs[:3]), outs[3:]


def _ffn_fwd_down(tag, h, a, w_down):
    M = h.shape[0]
    F = w_down.shape[0]
    (h_out,) = _fused_matmul(
        tag + "_down", M, D_MODEL, F, [dict(a=a, b=w_down, acc=0)], [(h, 0)],
        lambda accs, ex: (ex[0] + 0.5 * accs[0],), [F32], 1, _tile(M, (1088, 544, 256)), D_MODEL, F, outer="j")
    return h_out


def _ffn_bwd(tag, dh, dh_b, h, norm_w, w_gu_t, w_down, saved, scatter=False):
    n, g, u, a = saved
    M = h.shape[0]
    F = w_down.shape[0]
    tm = _tile(M, (544, 256))
    tn = _tile(F, (1408, 704, 256))

    def swiglu_bwd(accs, ex):
        da, gv, uv = 0.5 * accs[0], ex[0].astype(F32), ex[1].astype(F32)
        s = _sigmoid(gv)
        return da * uv * _dsilu(gv, s), da * gv * s

    (dgu,) = _fused_matmul(
        tag + "_dact", M, F, D_MODEL, [dict(a=dh_b, b=w_down, trans_b=True, acc=0)], [(g, 0), (u, 0)],
        swiglu_bwd, [BF16, BF16], 1, _tile(M, (1088, 544, 256)), tn, D_MODEL, outer="j", stack=True)
    tr = _tile(M, (2176, 256))
    (dw_down,) = _matmul_tn(tag + "_dwd", a, dh_b, tn, D_MODEL, tr, scale=0.5)
    dw_gu_t, *p_down = _matmul_tn(tag + "_dwgu", dgu, n, tn, D_MODEL, tr,
                                  comm=("scatter", [_to_rows(dw_down)]) if scatter else None)
    comm = None
    if scatter:
        comm = ("chips", _chip_sums(tag + "_wgu", [_by_core(dw_gu_t)]))
    def norm_bwd(accs, ex):
        dh_prev, dw = _rmsnorm_bwd_tile(accs[0], ex[0], ex[2], ex[1])
        return dh_prev, dh_prev, dw

    dh_prev, dh_prev_b, dnorm, *p_gu = _fused_matmul(
        tag + "_dn", M, D_MODEL, F,
        [dict(a=dgu, a_lead=0, b=w_gu_t, acc=0, resident=True),
         dict(a=dgu, a_lead=1, b=w_gu_t, bk_off=1, acc=0, resident=True)], [(h, 0), (dh, 0)],
        norm_bwd, [F32, BF16], 1, tm, D_MODEL, F, outer="i", comm=comm, vecs=[norm_w], row_sums=1)
    return (dh_prev, dh_prev_b, dnorm, *((p_gu[0], p_down[0]) if scatter else (dw_gu_t, dw_down)))


def kernel(x, meta_tokens, ffn1_norm, ffn1_w_gu, ffn1_w_down, mix_norm, w_in, ssd_conv_w, ssd_conv_b, ssd_dt_bias, ssd_a_log, ssd_d, ssd_norm, hg_lower_bound, hg_norm, w_branch_a, w_branch_b, w_out, ffn2_norm, ffn2_w_gu, ffn2_w_down, final_norm, loss_target, m_meta_tokens, m_ffn1_norm, m_ffn1_w_gu, m_ffn1_w_down, m_mix_norm, m_w_in, m_ssd_conv_w, m_ssd_conv_b, m_ssd_dt_bias, m_ssd_a_log, m_ssd_d, m_ssd_norm, m_hg_lower_bound, m_hg_norm, m_w_branch_a, m_w_branch_b, m_w_out, m_ffn2_norm, m_ffn2_w_gu, m_ffn2_w_down, m_final_norm, v_meta_tokens, v_ffn1_norm, v_ffn1_w_gu, v_ffn1_w_down, v_mix_norm, v_w_in, v_ssd_conv_w, v_ssd_conv_b, v_ssd_dt_bias, v_ssd_a_log, v_ssd_d, v_ssd_norm, v_hg_lower_bound, v_hg_norm, v_w_branch_a, v_w_branch_b, v_w_out, v_ffn2_norm, v_ffn2_w_gu, v_ffn2_w_down, v_final_norm):
    Bl, S, D = x.shape
    T = PAD + N_META + S
    nc = T // Q
    M = Bl * T
    me = 4 * lax.axis_index("x") + 2 * lax.axis_index("y") + lax.axis_index("c")

    bf = lambda a: a[0].astype(BF16)
    bft = lambda a: a[0].T.astype(BF16)
    g_wgu1, g_meta, g_conv_w = _exchange("gather_first", "gather", [bft(ffn1_w_gu), meta_tokens, ssd_conv_w[0]])
    wgu1, meta_full, conv_w_full = _rows(g_wgu1), _cols(g_meta), _cols(g_conv_w)
    bias_p, alog_p, d_p = _heads_to_lanes(ssd_dt_bias), _heads_to_lanes(ssd_a_log), _heads_to_lanes(ssd_d)
    final_w = final_norm.reshape(1, D)

    h0 = jnp.concatenate([jnp.zeros((Bl, PAD, D), F32), jnp.broadcast_to(meta_full[None], (Bl, N_META, D)), x],
                         axis=1).reshape(M, D)
    tm = _tile(M, (1088, 544, 256))
    ffn1_saved, (g_wd1, g_win) = _ffn_fwd_gu("ffn1", h0, ffn1_norm, wgu1, comm=("gather", [bf(ffn1_w_down), bft(w_in)]))
    wd1 = _rows(g_wd1)
    win_t = _rows(g_win)
    win_dt = jnp.pad(win_t[DT_ROW:DT_ROW + SSD_HEADS], ((0, 128 - SSD_HEADS), (0, 0)))
    h1 = _ffn_fwd_down("ffn1", h0, ffn1_saved[3], wd1)
    un = _rmsnorm_fwd("mix_norm", h1, mix_norm)
    plain = lambda accs, ex: (accs[0],)
    proj, g_wa, g_wb, g_wo = _fused_matmul(
        "in_proj", M, N_MAIN, D, [dict(a=un, b=win_t, trans_b=True, acc=0, b_shift=(DT_ROW // 1536, SSD_HEADS))], [],
        plain, [BF16], 1, tm, 1536, D,
        outer="j", comm=("gather", [bf(w_branch_a), bf(w_branch_b), bf(w_out)]))
    wa, wb, wo = _rows(g_wa), _rows(g_wb), _rows(g_wo)
    (dtr,) = _fused_matmul("in_proj_dt", M, 128, D, [dict(a=un, b=win_dt, trans_b=True, acc=0)], [], plain, [F32], 1,
                           tm, 128, D, outer="j")
    xc = _conv_fwd(proj, conv_w_full, ssd_conv_b, Bl, T)
    ya, ssd_prev = _ssd_fwd(xc, dtr, proj, bias_p, alog_p, d_p, ssd_norm, Bl, nc)
    yb, hg_o, hg_st, g_wgu2, g_wd2 = _hgrn_fwd(proj, hg_lower_bound, hg_norm, Bl, nc,
                                               comm=("gather", [bft(ffn2_w_gu), bf(ffn2_w_down)]))
    wgu2, wd2 = _rows(g_wgu2), _rows(g_wd2)

    def branch_fwd(accs, ex):
        pa, pb = accs
        return pa, pb, _sigmoid(ex[0].astype(F32)) * pa + _sigmoid(ex[1].astype(F32)) * pb

    pa, pb, merged = _fused_matmul(
        "branches", M, D, D, [dict(a=ya, b=wa, acc=0), dict(a=yb, b=wb, acc=1)], [(proj, 7), (proj, 8)],
        branch_fwd, [BF16, BF16, BF16], 2, tm, D, D, outer="j")
    (h2,) = _fused_matmul("out_proj", M, D, D, [dict(a=merged, b=wo, acc=0)], [(h1, 0)],
                          lambda accs, ex: (ex[0] + accs[0],), [F32], 1, tm, D, D, outer="j")
    ffn2_saved, _ = _ffn_fwd_gu("ffn2", h2, ffn2_norm, wgu2)
    h3 = _ffn_fwd_down("ffn2", h2, ffn2_saved[3], wd2)

    dh3, dh3_b, d_final, loss_part = _loss_head(h3, final_w, loss_target, Bl, nc)
    dh2, dh2_b, d_ffn2_norm, d_wgu2, d_wd2 = _ffn_bwd("ffn2", dh3, dh3_b, h2, ffn2_norm, wgu2, wd2, ffn2_saved)

    def branch_bwd(accs, ex):
        dm = accs[0]
        ga, gb, pav, pbv = (e.astype(F32) for e in ex)
        sa, sb = _sigmoid(ga), _sigmoid(gb)
        return (dm * sa, dm * sb,
                jnp.concatenate([dm * pav * sa * (1.0 - sa), dm * pbv * sb * (1.0 - sb)], axis=1))

    d_merged_outs = []

    def d_merged_with_swap(theirs):
        d_merged_outs.extend(_fused_matmul(
            "d_merged", M, D, D, [dict(a=dh2_b, b=wo, trans_b=True, acc=0)], [(proj, 7), (proj, 8), (pa, 0), (pb, 0)],
            branch_bwd, [BF16] * 2, 1, tm, D, D, outer="j", comm=("swap", theirs),
            wide=dict(width=2 * D, col=7 * D, total=N_MAIN, dtype=BF16)))
        return d_merged_outs[3:]

    s_ffn2 = _chip_sums("ffn2", [_by_core(d_wgu2), _by_core(d_wd2)], swap_in=d_merged_with_swap)
    dpa, dpb, dproj = d_merged_outs[:3]
    (d_wo,) = _matmul_tn("d_w_out", merged, dh2_b, 512, D, M)
    (d_wa,) = _matmul_tn("d_w_a", ya, dpa, 512, D, M)
    (d_wb,) = _matmul_tn("d_w_b", yb, dpb, 512, D, M)
    dya, dyb = _fused_matmul(
        "d_branches", M, D, D, [dict(a=dpa, b=wa, trans_b=True, acc=0), dict(a=dpb, b=wb, trans_b=True, acc=1)], [],
        lambda accs, ex: (accs[0], accs[1]), [BF16, BF16], 2, tm, D, D, outer="j")
    *ssd_grads, p_wgu2, p_wd2 = _ssd_bwd(xc, dtr, proj, bias_p, alog_p, d_p, ssd_norm, ssd_prev, dya, dproj, Bl, nc,
                                         comm=("chips", s_ffn2))
    dxc, dproj, ddtr, d_bias_p, d_alog_p, d_d_p, d_ssd_norm = ssd_grads
    dproj, d_conv_w, d_conv_b = _conv_bwd(proj, conv_w_full, ssd_conv_b, dxc, dproj, Bl, T)
    dproj, d_hb, d_hg_norm, p_wa, p_wb, p_wo = _hgrn_bwd(
        proj, hg_lower_bound, hg_norm, hg_o, hg_st, dyb, dproj, Bl, nc,
        comm=("scatter", [_to_rows(d_wa), _to_rows(d_wb), _to_rows(d_wo)]))
    ddtr_b = ddtr.astype(BF16)
    (d_win_t,) = _matmul_tn("d_w_in", dproj, un, 768, D, M, out_skip=(DT_ROW, SSD_HEADS))
    (d_win_dt,) = _matmul_tn("d_w_in_dt", ddtr_b, un, 128, D, M)
    d_win_t = lax.dynamic_update_slice(d_win_t, d_win_dt[:SSD_HEADS], (DT_ROW, 0))
    d_un_dt_outs = []

    def d_un_dt_with_swap(theirs):
        d_un_dt_outs.extend(_fused_matmul("d_un_dt", M, D, 128, [dict(a=ddtr_b, b=win_dt, acc=0)], [], plain, [F32], 1,
                                          tm, D, 128, outer="j", comm=("swap", theirs)))
        return d_un_dt_outs[1:]

    s_win = _chip_sums("w_in", [_by_core(d_win_t)], swap_in=d_un_dt_with_swap)
    def mix_norm_bwd(accs, ex):
        dh, dw = _rmsnorm_bwd_tile(accs[0] + ex[0], ex[1], ex[3], ex[2])
        return dh, dh, dw

    dh1, dh1_b, d_mix_norm, p_win = _fused_matmul(
        "d_un", M, D, N_MAIN, [dict(a=dproj, b=win_t, acc=0, b_shift=(DT_ROW // 3072, SSD_HEADS))],
        [(d_un_dt_outs[0], 0), (h1, 0), (dh2, 0)],
        mix_norm_bwd, [F32, BF16], 1, _tile(M, (544, 256)), D, 3072, outer="i", comm=("chips", s_win),
        vecs=[mix_norm], row_sums=1)
    dh0, _, d_ffn1_norm, p_wgu1, p_wd1 = _ffn_bwd("ffn1", dh1, dh1_b, h0, ffn1_norm, wgu1, wd1, ffn1_saved, scatter=True)

    dh0 = dh0.reshape(Bl, T, D)
    grad_x = dh0[:, PAD + N_META:]
    d_meta = dh0[:, PAD:PAD + N_META]

    small_grads = [d_ffn1_norm, d_mix_norm, d_conv_b, _lanes_to_heads(d_bias_p), _lanes_to_heads(d_alog_p),
                   _lanes_to_heads(d_d_p), d_ssd_norm, d_hb, d_hg_norm, d_ffn2_norm, d_final.reshape(D), d_conv_w]
    small_packed = _pack_rows(small_grads + [d_meta[b] for b in range(Bl)])
    parts = [p_wgu1, p_wd1, p_win, p_wa, p_wb, p_wo, p_wgu2, p_wd2]
    (small_all,) = _exchange("gather_small_grads", "gather", [small_packed])
    small_sum = _sum_parts("sum_small_grads", small_all)
    unpacked = _unpack_rows(small_sum, small_grads + [d_meta[b] for b in range(Bl)])
    g_small = unpacked[:len(small_grads)]
    g_meta_full = unpacked[len(small_grads)]
    for b in range(1, Bl):
        g_meta_full = g_meta_full + unpacked[len(small_grads) + b]
    g_meta = lax.dynamic_slice_in_dim(g_meta_full, me * (D // N_DEV), D // N_DEV, axis=1)
    g_conv_w = lax.dynamic_slice_in_dim(g_small[11], me * (SSD_CONV_CH // N_DEV), SSD_CONV_CH // N_DEV, axis=1)

    names = ["meta_tokens", "ffn1_norm", "ffn1_w_gu", "ffn1_w_down", "mix_norm", "w_in", "ssd_conv_w", "ssd_conv_b",
             "ssd_dt_bias", "ssd_a_log", "ssd_d", "ssd_norm", "hg_lower_bound", "hg_norm", "w_branch_a", "w_branch_b",
             "w_out", "ffn2_norm", "ffn2_w_gu", "ffn2_w_down", "final_norm"]
    W = dict(meta_tokens=meta_tokens, ffn1_norm=ffn1_norm, ffn1_w_gu=ffn1_w_gu, ffn1_w_down=ffn1_w_down, mix_norm=mix_norm,
             w_in=w_in, ssd_conv_w=ssd_conv_w, ssd_conv_b=ssd_conv_b, ssd_dt_bias=ssd_dt_bias, ssd_a_log=ssd_a_log,
             ssd_d=ssd_d, ssd_norm=ssd_norm, hg_lower_bound=hg_lower_bound, hg_norm=hg_norm, w_branch_a=w_branch_a,
             w_branch_b=w_branch_b, w_out=w_out, ffn2_norm=ffn2_norm, ffn2_w_gu=ffn2_w_gu, ffn2_w_down=ffn2_w_down,
             final_norm=final_norm)
    Mo = dict(meta_tokens=m_meta_tokens, ffn1_norm=m_ffn1_norm, ffn1_w_gu=m_ffn1_w_gu, ffn1_w_down=m_ffn1_w_down,
              mix_norm=m_mix_norm, w_in=m_w_in, ssd_conv_w=m_ssd_conv_w, ssd_conv_b=m_ssd_conv_b, ssd_dt_bias=m_ssd_dt_bias,
              ssd_a_log=m_ssd_a_log, ssd_d=m_ssd_d, ssd_norm=m_ssd_norm, hg_lower_bound=m_hg_lower_bound, hg_norm=m_hg_norm,
              w_branch_a=m_w_branch_a, w_branch_b=m_w_branch_b, w_out=m_w_out, ffn2_norm=m_ffn2_norm, ffn2_w_gu=m_ffn2_w_gu,
              ffn2_w_down=m_ffn2_w_down, final_norm=m_final_norm)
    Vo = dict(meta_tokens=v_meta_tokens, ffn1_norm=v_ffn1_norm, ffn1_w_gu=v_ffn1_w_gu, ffn1_w_down=v_ffn1_w_down,
              mix_norm=v_mix_norm, w_in=v_w_in, ssd_conv_w=v_ssd_conv_w, ssd_conv_b=v_ssd_conv_b, ssd_dt_bias=v_ssd_dt_bias,
              ssd_a_log=v_ssd_a_log, ssd_d=v_ssd_d, ssd_norm=v_ssd_norm, hg_lower_bound=v_hg_lower_bound, hg_norm=v_hg_norm,
              w_branch_a=v_w_branch_a, w_branch_b=v_w_branch_b, w_out=v_w_out, ffn2_norm=v_ffn2_norm, ffn2_w_gu=v_ffn2_w_gu,
              ffn2_w_down=v_ffn2_w_down, final_norm=v_final_norm)
    grads, deltas, new_m, new_v = {}, {}, {}, {}
    big_names = ["ffn1_w_gu", "ffn1_w_down", "w_in", "w_branch_a", "w_branch_b", "w_out", "ffn2_w_gu", "ffn2_w_down"]
    transposed = ("ffn1_w_gu", "ffn2_w_gu", "w_in")
    for nm, part in zip(big_names, parts):
        view = (lambda a: a[0].T) if nm in transposed else (lambda a: a[0])
        back = (lambda o: o.T[None]) if nm in transposed else (lambda o: o[None])
        outs = _adamw("adamw_" + nm, part, view(W[nm]), view(Mo[nm]), view(Vo[nm]))
        grads[nm], deltas[nm], new_m[nm], new_v[nm] = (back(o) for o in outs)
    small_names = ["ffn1_norm", "mix_norm", "ssd_conv_b", "ssd_dt_bias", "ssd_a_log", "ssd_d", "ssd_norm", "hg_lower_bound",
                   "hg_norm", "ffn2_norm", "final_norm", "ssd_conv_w", "meta_tokens"]
    small_g = g_small[:11] + [g_conv_w.reshape(ssd_conv_w.shape), g_meta]
    pk = lambda d: _pack_rows([d[nm] for nm in small_names])
    outs = _adamw("adamw_small", _pack_rows(small_g)[None], pk(W), pk(Mo), pk(Vo))
    like = [W[nm] for nm in small_names]
    for dst, o in zip((grads, deltas, new_m, new_v), outs):
        for nm, val in zip(small_names, _unpack_rows(o, like)):
            dst[nm] = val

    loss = lax.psum(loss_part[0, 0], MESH_AXES)
    return (loss, grad_x, *[grads[nm] for nm in names], *[deltas[nm] for nm in names],
            *[new_m[nm] for nm in names], *[new_v[nm] for nm in names])
```

```python
import functools

import jax
import jax.numpy as jnp
from jax import lax
from jax.experimental import pallas as pl
from jax.experimental.pallas import tpu as pltpu

F32, BF16 = jnp.float32, jnp.bfloat16
NN, NT, TN = ((1,), (0,)), ((1,), (1,)), ((0,), (0,))
MESH_AXES = ("x", "y", "c")
N_DEV = 8

D_MODEL = 1024
N_META = 16
EPS = 1e-6
SSD_HEADS, SSD_HEAD_DIM, SSD_GROUPS, SSD_STATE, SSD_CONV, Q = 16, 64, 4, 128, 4, 128
SSD_INNER = SSD_HEADS * SSD_HEAD_DIM
SSD_CONV_CH = SSD_INNER + 2 * SSD_GROUPS * SSD_STATE
HG_WIDTH, HG_HEADS, HG_CHUNK = 1024, 8, 16
PAD = Q - N_META
N_MAIN = 9 * 1024
ADAM_LR, ADAM_B1, ADAM_B2, ADAM_EPS, ADAM_WD, ADAM_STEP = 0.001, 0.9, 0.999, 1e-08, 0.01, 10
VMEM_LIMIT = 52 * 1024 * 1024


def _dot(a, b, dims, prec=None):
    return lax.dot_general(a, b, (dims, ((), ())), precision=prec, preferred_element_type=F32)


def _dot01(a, b, dims, sel):
    x = b if sel == "a" else a
    hi = x.astype(BF16)
    r1 = x - hi.astype(F32)
    mid = r1.astype(BF16)
    lo = (r1 - mid.astype(F32)).astype(BF16)
    s = (a if sel == "a" else b).astype(BF16)
    parts = [_dot(s, p, dims) if sel == "a" else _dot(p, s, dims) for p in (hi, mid, lo)]
    return parts[0] + parts[1] + parts[2]


def _sigmoid(x):
    return 1.0 / (1.0 + jnp.exp(-x))


def _dsilu(x, s):
    return s * (1.0 + x * (1.0 - s))


def _softplus(x):
    e = jnp.exp(-jnp.abs(x))
    u = 1.0 + e
    log1p_e = jnp.where(u == 1.0, e, jnp.log(u) * e / (u - 1.0))
    return jnp.maximum(x, 0.0) + log1p_e


def _params(sem):
    return pltpu.CompilerParams(dimension_semantics=sem, vmem_limit_bytes=VMEM_LIMIT)


def _tile(n, prefs):
    for p in prefs:
        if n % p == 0:
            return p
    return n


CHIP_FLIPS = ((1, 0), (0, 1), (1, 1))
N_PEER = N_DEV - 1


def _comm_gather(srcs, outs, send_sems, recv_sems, local_sems):
    n = len(srcs)
    x, y, c = (lax.axis_index(a) for a in MESH_AXES)
    dev = lambda px, py, pc: 4 * px + 2 * py + pc
    me, sib = dev(x, y, c), (x, y, 1 - c)

    def rc(w, k, slot, to, src=None):
        return pltpu.make_async_remote_copy(
            src_ref=outs[w].at[slot] if src is None else src, dst_ref=outs[w].at[slot],
            send_sem=send_sems.at[w, k], recv_sem=recv_sems.at[w, k], device_id=to, device_id_type=pl.DeviceIdType.MESH)

    def local(w):
        return pltpu.make_async_copy(srcs[w], outs[w].at[me], local_sems.at[w])

    def start():
        for w in range(n):
            local(w).start()
            rc(w, 0, me, sib, src=srcs[w]).start()
            for j, (fx, fy) in enumerate(CHIP_FLIPS):
                rc(w, 1 + j, me, (x ^ fx, y ^ fy, c), src=srcs[w]).start()

    def finish():
        for w in range(n):
            for j, (fx, fy) in enumerate(CHIP_FLIPS):
                slot = dev(x ^ fx, y ^ fy, c)
                rc(w, 1 + j, slot, sib).wait_recv()
                rc(w, 4 + j, slot, sib).start()
        for w in range(n):
            rc(w, 0, dev(x, y, 1 - c), sib).wait_recv()
            rc(w, 0, me, sib, src=srcs[w]).wait_send()
            for j, (fx, fy) in enumerate(CHIP_FLIPS):
                rc(w, 4 + j, dev(x ^ fx, y ^ fy, 1 - c), sib).wait_recv()
                rc(w, 1 + j, me, sib, src=srcs[w]).wait_send()
                rc(w, 4 + j, dev(x ^ fx, y ^ fy, c), sib).wait_send()
            local(w).wait()

    return start, finish


def _comm_scatter(srcs, outs, send_sems, recv_sems, local_sems):
    n = len(srcs)
    x, y, c = (lax.axis_index(a) for a in MESH_AXES)
    me = 4 * x + 2 * y + c

    def copies():
        out = []
        for w in range(n):
            out.append(pltpu.make_async_copy(srcs[w].at[me], outs[w].at[me], local_sems.at[w]))
            for k in range(1, N_DEV):
                px, py, pc = x ^ (k >> 2), y ^ ((k >> 1) & 1), c ^ (k & 1)
                out.append(pltpu.make_async_remote_copy(
                    src_ref=srcs[w].at[4 * px + 2 * py + pc], dst_ref=outs[w].at[me],
                    send_sem=send_sems.at[w, k - 1], recv_sem=recv_sems.at[w, k - 1],
                    device_id=(px, py, pc), device_id_type=pl.DeviceIdType.MESH))
        return out

    def start():
        for cp in copies():
            cp.start()

    def finish():
        for cp in copies():
            cp.wait()

    return start, finish


def _comm_swap(srcs, outs, send_sems, recv_sems, local_sems):
    x, y, c = (lax.axis_index(a) for a in MESH_AXES)

    def copies():
        return [pltpu.make_async_remote_copy(
            src_ref=srcs[w].at[1 - c], dst_ref=outs[w], send_sem=send_sems.at[w, 0], recv_sem=recv_sems.at[w, 0],
            device_id=(x, y, 1 - c), device_id_type=pl.DeviceIdType.MESH) for w in range(len(srcs))]

    def start():
        for cp in copies():
            cp.start()

    def finish():
        for cp in copies():
            cp.wait()

    return start, finish


def _comm_chips(srcs, outs, send_sems, recv_sems, local_sems):
    n = len(srcs)
    x, y, c = (lax.axis_index(a) for a in MESH_AXES)
    mine = 2 * x + y

    def copies():
        out = []
        for w in range(n):
            out.append(pltpu.make_async_copy(srcs[w].at[mine], outs[w].at[mine], local_sems.at[w]))
            for j, (fx, fy) in enumerate(CHIP_FLIPS):
                px, py = x ^ fx, y ^ fy
                out.append(pltpu.make_async_remote_copy(
                    src_ref=srcs[w].at[2 * px + py], dst_ref=outs[w].at[mine],
                    send_sem=send_sems.at[w, j], recv_sem=recv_sems.at[w, j],
                    device_id=(px, py, c), device_id_type=pl.DeviceIdType.MESH))
        return out

    def start():
        for cp in copies():
            cp.start()

    def finish():
        for cp in copies():
            cp.wait()

    return start, finish


def _comm_parts(comm):
    kind, arrays = comm
    n = len(arrays)
    lead = {"gather": lambda a: (N_DEV,) + a.shape, "scatter": lambda a: (N_DEV,) + a.shape[1:],
            "swap": lambda a: a.shape[1:], "chips": lambda a: a.shape}[kind]
    shapes = [jax.ShapeDtypeStruct(lead(a), a.dtype) for a in arrays]
    sems = [pltpu.SemaphoreType.DMA((n, N_PEER)), pltpu.SemaphoreType.DMA((n, N_PEER)), pltpu.SemaphoreType.DMA((n,))]
    make = {"gather": _comm_gather, "scatter": _comm_scatter, "swap": _comm_swap, "chips": _comm_chips}[kind]
    return n, shapes, sems, make


def _exchange(name, kind, arrays):
    n, shapes, sems, make = _comm_parts((kind, arrays))

    def body(*refs):
        start, finish = make(refs[:n], refs[n:2 * n], *refs[2 * n:])
        start()
        finish()

    any_spec = pl.BlockSpec(memory_space=pl.ANY)
    return pl.pallas_call(
        body, name=name, in_specs=[any_spec] * n, out_specs=[any_spec] * n, out_shape=shapes, scratch_shapes=sems,
        compiler_params=pltpu.CompilerParams(has_side_effects=True),
    )(*arrays)


def _call(body, *, name, grid, in_specs, out_specs, out_shape, scratch, sem, args, comm=None, into=None):
    any_spec = pl.BlockSpec(memory_space=pl.ANY)
    in_specs, args, aliases, n_body_in = list(in_specs), list(args), {}, len(in_specs)
    if into is not None:
        in_specs.append(any_spec)
        args.append(into[0])
        aliases = {n_body_in: into[1]}
    n_in, n_out, n_scr = len(in_specs), len(out_specs), len(scratch)
    if comm is None:
        def plain(*refs):
            body(*refs[:n_body_in], *refs[n_in:])

        return pl.pallas_call(plain, name=name, grid=grid, in_specs=in_specs, out_specs=out_specs, out_shape=out_shape,
                              scratch_shapes=scratch, input_output_aliases=aliases, compiler_params=_params(sem))(*args)
    n, shapes, sems, make = _comm_parts(comm)

    def carrier(*refs):
        ins, csrc = refs[:n_body_in], refs[n_in:n_in + n]
        outs, cout = refs[n_in + n:n_in + n + n_out], refs[n_in + n + n_out:n_in + 2 * n + n_out]
        rest = refs[n_in + 2 * n + n_out:]
        start, finish = make(csrc, cout, *rest[n_scr:])
        ids = [pl.program_id(a) for a in range(len(grid))]
        first = functools.reduce(jnp.logical_and, [i == 0 for i in ids])
        last = functools.reduce(jnp.logical_and, [i == g - 1 for i, g in zip(ids, grid)])
        pl.when(first)(start)
        body(*ins, *outs, *rest[:n_scr])
        pl.when(last)(finish)

    return pl.pallas_call(
        carrier, name=name, grid=grid, in_specs=in_specs + [any_spec] * n,
        out_specs=list(out_specs) + [any_spec] * n, out_shape=list(out_shape) + shapes,
        scratch_shapes=list(scratch) + sems, input_output_aliases=aliases,
        compiler_params=pltpu.CompilerParams(dimension_semantics=("arbitrary",) * len(grid),
                                             vmem_limit_bytes=VMEM_LIMIT, has_side_effects=True),
    )(*args, *comm[1])


def _fused_matmul(name, M, N, K, pairs, extras, epilogue, out_dtypes, n_acc, tm, tn, tk, outer="i", comm=None,
                  stack=False, vecs=(), row_sums=0, wide=None, sub=None):
    nk = K // tk
    n_pairs, n_ex, n_out = len(pairs), len(extras), len(out_dtypes)
    assert not row_sums or (outer == "i" and N == tn)

    def ij(g0, g1):
        return (g0, g1) if outer == "i" else (g1, g0)

    in_specs, args = [], []
    for p in pairs:
        ao, bk, bn = p.get("a_off", 0), p.get("bk_off", 0), p.get("bn_off", 0)
        mode = dict(pipeline_mode=pl.Buffered(1)) if p.get("resident") else {}
        if "a_lead" in p:
            in_specs.append(pl.BlockSpec((None, tm, tk),
                                         lambda g0, g1, k, ao=ao, ld=p["a_lead"]: (ld, ij(g0, g1)[0], k + ao)))
        else:
            in_specs.append(pl.BlockSpec((tm, tk), lambda g0, g1, k, ao=ao: (ij(g0, g1)[0], k + ao)))
        if "b_shift" in p:
            first, shift = p["b_shift"]
            if p.get("trans_b"):
                in_specs.append(pl.BlockSpec(
                    (pl.Element(tn), pl.Element(tk)),
                    lambda g0, g1, k, bk=bk: (
                        pl.multiple_of(ij(g0, g1)[1] * tn + jnp.where(ij(g0, g1)[1] >= first, shift, 0), 16),
                        (k + bk) * tk)))
            else:
                in_specs.append(pl.BlockSpec(
                    (pl.Element(tk), pl.Element(tn)),
                    lambda g0, g1, k, bn=bn: (pl.multiple_of(k * tk + jnp.where(k >= first, shift, 0), 16),
                                              (ij(g0, g1)[1] + bn) * tn)))
        elif p.get("trans_b"):
            in_specs.append(pl.BlockSpec((tn, tk), lambda g0, g1, k, bk=bk, bn=bn: (ij(g0, g1)[1] + bn, k + bk), **mode))
        else:
            in_specs.append(pl.BlockSpec((tk, tn), lambda g0, g1, k, bk=bk, bn=bn: (k + bk, ij(g0, g1)[1] + bn), **mode))
        args += [p["a"], p["b"]]
    for arr, off in extras:
        in_specs.append(pl.BlockSpec((tm, tn), lambda g0, g1, k, off=off: (ij(g0, g1)[0], ij(g0, g1)[1] + off)))
        args.append(arr)
    for arr in vecs:
        in_specs.append(pl.BlockSpec((1, tn), lambda g0, g1, k: (0, ij(g0, g1)[1])))
        args.append(arr)
    if stack:
        out_specs = [pl.BlockSpec((n_out, tm, tn), lambda g0, g1, k: (0,) + ij(g0, g1))]
        out_shape = [jax.ShapeDtypeStruct((n_out, M, N), out_dtypes[0])]
    else:
        out_specs = [pl.BlockSpec((tm, tn), lambda g0, g1, k: ij(g0, g1)) for _ in out_dtypes]
        out_shape = [jax.ShapeDtypeStruct((M, N), dt) for dt in out_dtypes]
    if wide:
        out_specs.append(pl.BlockSpec((pl.Element(tm), pl.Element(wide["width"])),
                                      lambda g0, g1, k: (pl.multiple_of(ij(g0, g1)[0] * tm, 16), wide["col"])))
        out_shape.append(jax.ShapeDtypeStruct((M, wide["total"]), wide["dtype"]))
    n_tile_out = len(out_specs)
    out_specs += [pl.BlockSpec((1, tn), lambda g0, g1, k: (0, 0)) for _ in range(row_sums)]
    out_shape += [jax.ShapeDtypeStruct((1, N), F32) for _ in range(row_sums)]
    grid = (M // tm, N // tn, nk) if outer == "i" else (N // tn, M // tm, nk)
    n_in = 2 * n_pairs + n_ex + len(vecs)

    def partials(refs, cs=slice(None)):
        accs = [None] * n_acc
        for idx, p in enumerate(pairs):
            b_ref = refs[2 * idx + 1]
            d = (_dot(refs[2 * idx][...], b_ref[cs, :], NT) if p.get("trans_b")
                 else _dot(refs[2 * idx][...], b_ref[:, cs], NN))
            accs[p["acc"]] = d if accs[p["acc"]] is None else accs[p["acc"]] + d
        return accs

    def finish(accs, refs, first_rows, cs=slice(None)):
        res = epilogue(accs, [r[:, cs] for r in refs[2 * n_pairs:n_in]])
        if stack:
            o = refs[n_in]
            for idx in range(n_out):
                o[idx, :, cs] = res[idx].astype(o.dtype)
        else:
            for o, r in zip(refs[n_in:n_in + n_out], res):
                o[:, cs] = r.astype(o.dtype)
        if wide:
            o = refs[n_in + n_tile_out - 1]
            o[...] = res[n_out].astype(o.dtype)
        for o, r in zip(refs[n_in + n_tile_out:n_in + n_tile_out + row_sums], res[n_out + bool(wide):]):
            @pl.when(first_rows)
            def _(o=o, r=r):
                o[...] = r

            @pl.when(jnp.logical_not(first_rows))
            def _(o=o, r=r):
                o[...] += r

    if nk == 1 and sub:
        assert not wide and not row_sums and tn % sub == 0

        def body(*refs):
            for c in range(tn // sub):
                cs = slice(c * sub, (c + 1) * sub)
                finish(partials(refs, cs), refs, None, cs)
        scratch = []
    elif nk == 1:
        def body(*refs):
            finish(partials(refs), refs, pl.program_id(0) == 0)
        scratch = []
    else:
        def body(*refs):
            acc_refs = refs[-n_acc:]
            k = pl.program_id(2)
            first_rows = pl.program_id(0) == 0
            new = partials(refs)

            @pl.when(k == 0)
            def _():
                for a, v in zip(acc_refs, new):
                    a[...] = v

            @pl.when(k > 0)
            def _():
                for a, v in zip(acc_refs, new):
                    a[...] += v

            @pl.when(k == nk - 1)
            def _():
                finish([a[...] for a in acc_refs], refs, first_rows)
        scratch = [pltpu.VMEM((tm, tn), F32) for _ in range(n_acc)]

    return _call(body, name=name, grid=grid, in_specs=in_specs, out_specs=out_specs, out_shape=out_shape,
                 scratch=scratch, sem=("parallel", "parallel", "arbitrary"), args=args, comm=comm)


def _matmul_tn(name, x, y, t1, t2, tr, scale=1.0, comm=None, out_dtype=BF16, out_skip=None):
    L = x.shape[0] if x.ndim == 3 else 1
    R, K1 = x.shape[-2:]
    N1 = y.shape[1]
    nr, n1 = R // tr, K1 // t1
    if x.ndim == 3:
        x_spec = pl.BlockSpec((None, tr, t1), lambda i, j, r: (i // n1, r, i % n1))
    else:
        x_spec = pl.BlockSpec((tr, t1), lambda i, j, r: (r, i))
    rows_out = L * K1
    o_spec = pl.BlockSpec((t1, t2), lambda i, j, r: (i, j))
    if out_skip:
        row, count = out_skip
        rows_out += count
        o_spec = pl.BlockSpec(
            (pl.Element(t1), pl.Element(t2)),
            lambda i, j, r: (pl.multiple_of(i * t1 + jnp.where(i * t1 >= row, count, 0), 16), j * t2))

    def body(x_ref, y_ref, o_ref, *acc):
        d = _dot(x_ref[...], y_ref[...], TN)
        if nr == 1:
            o_ref[...] = (d * scale).astype(o_ref.dtype)
            return
        r = pl.program_id(2)

        @pl.when(r == 0)
        def _():
            acc[0][...] = d

        @pl.when(jnp.logical_and(r > 0, r < nr - 1))
        def _():
            acc[0][...] += d

        @pl.when(r == nr - 1)
        def _():
            o_ref[...] = ((acc[0][...] + d) * scale).astype(o_ref.dtype)

    return _call(
        body, name=name, grid=(L * n1, N1 // t2, nr),
        in_specs=[x_spec, pl.BlockSpec((tr, t2), lambda i, j, r: (r, j))], out_specs=[o_spec],
        out_shape=[jax.ShapeDtypeStruct((rows_out, N1), out_dtype)],
        scratch=[pltpu.VMEM((t1, t2), F32)] if nr > 1 else [],
        sem=("parallel", "parallel", "arbitrary"), args=(x, y), comm=comm)


def _rmsnorm_fwd(name, h, w):
    M, D = h.shape
    tm = _tile(M, (544, 256, 128))

    def body(h_ref, w_ref, o_ref):
        x = h_ref[...]
        r = lax.rsqrt(jnp.mean(x * x, axis=-1, keepdims=True) + EPS)
        o_ref[...] = (x * r * w_ref[...]).astype(o_ref.dtype)

    return pl.pallas_call(
        body, name=name, grid=(M // tm,),
        in_specs=[pl.BlockSpec((tm, D), lambda i: (i, 0)), pl.BlockSpec((1, D), lambda i: (0, 0))],
        out_specs=pl.BlockSpec((tm, D), lambda i: (i, 0)),
        out_shape=jax.ShapeDtypeStruct((M, D), BF16), compiler_params=_params(("parallel",)),
    )(h, w)


def _rmsnorm_bwd_tile(dn, h, w, dh_in):
    r = lax.rsqrt(jnp.mean(h * h, axis=-1, keepdims=True) + EPS)
    xhat = h * r
    gw = dn * w
    dh = dh_in + r * (gw - xhat * jnp.mean(gw * xhat, axis=-1, keepdims=True))
    return dh, jnp.sum(dn * xhat, axis=0, keepdims=True)


def _loss_head(h, w, target, Bl, nb):
    M, D = h.shape

    def body(h_ref, w_ref, t_ref, dh_ref, dhb_ref, dw_ref, loss_ref):
        b, t = pl.program_id(0), pl.program_id(1)
        live = (t > 0).astype(F32)
        x = h_ref[...]
        r = lax.rsqrt(jnp.mean(x * x, axis=-1, keepdims=True) + EPS)
        xhat = x * r
        wv = w_ref[...]
        err = (xhat * wv - t_ref[0]) * live
        dy = err * (1.0 / D)
        gw = dy * wv
        dx = r * (gw - xhat * jnp.mean(gw * xhat, axis=-1, keepdims=True))
        dh_ref[...] = dx
        dhb_ref[...] = dx.astype(BF16)
        dw = jnp.sum(dy * xhat, axis=0, keepdims=True)
        part = 0.5 * jnp.sum(jnp.sum(err * err, axis=-1, keepdims=True) * (1.0 / D), axis=0, keepdims=True)
        first = jnp.logical_and(b == 0, t == 0)

        @pl.when(first)
        def _():
            dw_ref[...] = dw
            loss_ref[...] = jnp.broadcast_to(part, loss_ref.shape)

        @pl.when(jnp.logical_not(first))
        def _():
            dw_ref[...] += dw
            loss_ref[...] += jnp.broadcast_to(part, loss_ref.shape)

    row = pl.BlockSpec((Q, D), lambda b, t: (b * nb + t, 0))
    vec = pl.BlockSpec((1, D), lambda b, t: (0, 0))
    return pl.pallas_call(
        body, name="loss_head", grid=(Bl, nb),
        in_specs=[row, vec, pl.BlockSpec((1, Q, D), lambda b, t: (b, jnp.maximum(t - 1, 0), 0))],
        out_specs=[row, row, vec, pl.BlockSpec((8, 128), lambda b, t: (0, 0))],
        out_shape=[jax.ShapeDtypeStruct((M, D), F32), jax.ShapeDtypeStruct((M, D), BF16),
                   jax.ShapeDtypeStruct((1, D), F32), jax.ShapeDtypeStruct((8, 128), F32)],
        compiler_params=_params(("arbitrary", "arbitrary")),
    )(h, w, target)


CONV_TC = 256


def _conv_pre(xr_ref, w_ref, b_ref):
    x = xr_ref[...].astype(F32)
    acc = b_ref[...] + w_ref[SSD_CONV - 1:SSD_CONV, :] * x
    for k in range(1, SSD_CONV):
        acc = acc + w_ref[SSD_CONV - 1 - k:SSD_CONV - k, :] * pltpu.roll(x, k, 0)
    return x, acc


def _conv_fwd(proj, w, b, Bl, T):
    M = proj.shape[0]
    off = 1024 // CONV_TC

    def body(xr_ref, w_ref, b_ref, o_ref):
        _, acc = _conv_pre(xr_ref, w_ref, b_ref)
        row = lax.broadcasted_iota(jnp.int32, acc.shape, 0)
        o_ref[...] = jnp.where(row >= PAD, acc * _sigmoid(acc), 0.0).astype(o_ref.dtype)

    return pl.pallas_call(
        body, name="conv_fwd", grid=(Bl, SSD_CONV_CH // CONV_TC),
        in_specs=[pl.BlockSpec((T, CONV_TC), lambda bb, j: (bb, j + off)),
                  pl.BlockSpec((SSD_CONV, CONV_TC), lambda bb, j: (0, j)), pl.BlockSpec((1, CONV_TC), lambda bb, j: (0, j))],
        out_specs=pl.BlockSpec((T, CONV_TC), lambda bb, j: (bb, j)),
        out_shape=jax.ShapeDtypeStruct((M, SSD_CONV_CH), BF16), compiler_params=_params(("parallel", "parallel")),
    )(proj, w, b)


def _conv_bwd(proj, w, b, dxc, dproj, Bl, T):
    M = proj.shape[0]
    off = 1024 // CONV_TC

    def body(xr_ref, w_ref, b_ref, d_ref, dx_ref, dw_ref, db_ref):
        x, acc = _conv_pre(xr_ref, w_ref, b_ref)
        row = lax.broadcasted_iota(jnp.int32, acc.shape, 0)
        s = _sigmoid(acc)
        dpre = jnp.where(row >= PAD, d_ref[...].astype(F32) * _dsilu(acc, s), 0.0)
        dx = w_ref[SSD_CONV - 1:SSD_CONV, :] * dpre
        dws = [jnp.sum(dpre * x, axis=0, keepdims=True)]
        for k in range(1, SSD_CONV):
            dx = dx + w_ref[SSD_CONV - 1 - k:SSD_CONV - k, :] * pltpu.roll(dpre, T - k, 0)
            dws.append(jnp.sum(dpre * pltpu.roll(x, k, 0), axis=0, keepdims=True))
        dx_ref[...] = dx.astype(dx_ref.dtype)
        dw = jnp.concatenate(dws[::-1], axis=0)
        db = jnp.sum(dpre, axis=0, keepdims=True)

        @pl.when(pl.program_id(1) == 0)
        def _():
            dw_ref[...] = dw
            db_ref[...] = db

        @pl.when(pl.program_id(1) > 0)
        def _():
            dw_ref[...] += dw
            db_ref[...] += db

    return _call(
        body, name="conv_bwd", grid=(SSD_CONV_CH // CONV_TC, Bl),
        in_specs=[pl.BlockSpec((T, CONV_TC), lambda j, bb: (bb, j + off)),
                  pl.BlockSpec((SSD_CONV, CONV_TC), lambda j, bb: (0, j)), pl.BlockSpec((1, CONV_TC), lambda j, bb: (0, j)),
                  pl.BlockSpec((T, CONV_TC), lambda j, bb: (bb, j))],
        out_specs=[pl.BlockSpec((T, CONV_TC), lambda j, bb: (bb, j + off)),
                   pl.BlockSpec((SSD_CONV, CONV_TC), lambda j, bb: (0, j)), pl.BlockSpec((1, CONV_TC), lambda j, bb: (0, j))],
        out_shape=[jax.ShapeDtypeStruct(dproj.shape, BF16), jax.ShapeDtypeStruct((SSD_CONV, SSD_CONV_CH), F32),
                   jax.ShapeDtypeStruct((1, SSD_CONV_CH), F32)],
        scratch=[], sem=("parallel", "arbitrary"), args=(proj, w, b, dxc), into=(dproj, 0))


N_PAIR = SSD_HEADS // 2
HPG = SSD_HEADS // SSD_GROUPS
GW = SSD_INNER // SSD_GROUPS


def _per_group(fn, *arrs):
    return jnp.concatenate([jnp.broadcast_to(fn(*(a[:, GW * g:GW * (g + 1)] for a in arrs)), (arrs[0].shape[0], GW))
                            for g in range(SSD_GROUPS)], axis=1)


def _ssd_prep(c, dtr_ref, bias_ref, alog_ref, d_ref):
    row = lax.broadcasted_iota(jnp.int32, (Q, 128), 0)
    col = lax.broadcasted_iota(jnp.int32, (Q, 128), 1)
    live = col < SSD_HEADS
    valid = jnp.logical_and(jnp.logical_or(c > 0, row >= PAD), live)
    pre = dtr_ref[...] + bias_ref[...]
    dt = jnp.where(valid, _softplus(pre), 0.0)
    A = jnp.where(live[0:1], -jnp.exp(alog_ref[...]), 0.0)
    tri = row >= col
    eye = (row == col).astype(BF16)
    cs = _dot01(tri, dt * A, NN, "a")
    cst = _dot01(eye, cs, NT, "a")
    spread = (lax.broadcasted_iota(jnp.int32, (128, SSD_INNER), 0)
              == lax.broadcasted_iota(jnp.int32, (128, SSD_INNER), 1) // SSD_HEAD_DIM).astype(BF16)
    dt_w = _dot01(dt, spread, NN, "b")
    cs_w = _dot01(cs, spread, NN, "b")
    d_w = _dot01(jnp.broadcast_to(d_ref[...], (8, 128)), spread, NN, "b")[0:1]
    lane = lax.broadcasted_iota(jnp.int32, (Q, SSD_INNER), 1)
    first = (lane % 128) < SSD_HEAD_DIM
    return dict(row=row, col=col, valid=valid, pre=pre, dt=dt, A=A, tri=tri, eye=eye, cs=cs, cst=cst, spread=spread,
                dt_w=dt_w, cs_w=cs_w, d_w=d_w, ecs_w=jnp.exp(cs_w), decay_w=jnp.exp(cs_w[Q - 1:Q] - cs_w), first=first)


def _ssd_chunk(xc_ref, s, states):
    xv = xc_ref[:, 0:SSD_INNER].astype(F32)
    Bs = [xc_ref[:, SSD_INNER + 128 * g:SSD_INNER + 128 * (g + 1)] for g in range(SSD_GROUPS)]
    Cs = [xc_ref[:, SSD_INNER + 512 + 128 * g:SSD_INNER + 512 + 128 * (g + 1)] for g in range(SSD_GROUPS)]
    X = xv * s["dt_w"]
    X0 = jnp.where(s["first"], X, 0.0)
    Xb = (X0.astype(BF16), (X - X0).astype(BF16))
    Xd = (X * s["decay_w"]).astype(BF16)
    CB = [_dot(Cs[g], Bs[g], NT) for g in range(SSD_GROUPS)]
    Lms = [jnp.exp(jnp.where(s["tri"], s["cs"][:, h:h + 1] - s["cst"][h:h + 1, :], -jnp.inf)) for h in range(SSD_HEADS)]
    Ms = [CB[h // HPG] * Lms[h] for h in range(SSD_HEADS)]
    Mb = [m.astype(BF16) for m in Ms]
    prev_b = [st.astype(BF16) for st in states]
    yds, yos, sts = [], [], []
    for p in range(N_PAIR):
        g, ln = p // 2, slice(128 * p, 128 * (p + 1))
        yds.append(_dot(Mb[2 * p], Xb[0][:, ln], NN) + _dot(Mb[2 * p + 1], Xb[1][:, ln], NN))
        yos.append(_dot(Cs[g], prev_b[p], NT))
        sts.append(_dot(Xd[:, ln], Bs[g], TN))
    yo = jnp.concatenate(yos, axis=1)
    y = jnp.concatenate(yds, axis=1) + yo * s["ecs_w"] + xv * s["d_w"]
    upper = s["row"] < SSD_HEAD_DIM
    cl = s["cs"][Q - 1:Q, :]
    ecl_rows = [jnp.where(upper, jnp.exp(cl[:, 2 * p:2 * p + 1]), jnp.exp(cl[:, 2 * p + 1:2 * p + 2])) for p in range(N_PAIR)]
    new_states = [states[p] * ecl_rows[p] + sts[p] for p in range(N_PAIR)]
    return y, new_states, dict(xv=xv, Bs=Bs, Cs=Cs, X=X, Xb=Xb, CB=CB, Lms=Lms, Ms=Ms, Mb=Mb, prev_b=prev_b, yo=yo,
                               ecl_rows=ecl_rows)


def _ssd_in_specs(nc, rev=False):
    rb = (lambda b, c: b * nc + nc - 1 - c) if rev else (lambda b, c: b * nc + c)
    vec = pl.BlockSpec((1, 128), lambda b, c: (0, 0))
    return [pl.BlockSpec((Q, SSD_CONV_CH), lambda b, c: (rb(b, c), 0)),
            pl.BlockSpec((Q, 128), lambda b, c: (rb(b, c), 0)),
            pl.BlockSpec((Q, SSD_INNER), lambda b, c: (rb(b, c), 0)),
            vec, vec, vec, pl.BlockSpec((1, SSD_INNER), lambda b, c: (0, 0))]


def _ssd_fwd(xc, dtr, proj, bias_p, alog_p, d_p, nw, Bl, nc):
    M = xc.shape[0]

    def body(xc_ref, dtr_ref, z_ref, bias_ref, alog_ref, d_ref, nw_ref, y_ref, prev_ref, state):
        c = pl.program_id(1)

        @pl.when(c == 0)
        def _():
            state[...] = jnp.zeros_like(state)

        s = _ssd_prep(c, dtr_ref, bias_ref, alog_ref, d_ref)
        states = [state[p] for p in range(N_PAIR)]
        y, new_states, _ = _ssd_chunk(xc_ref, s, states)
        for p in range(N_PAIR):
            prev_ref[0, 0, p] = states[p]
            state[p] = new_states[p]
        zz = z_ref[...].astype(F32)
        yg = y * zz * _sigmoid(zz)
        r = _per_group(lambda a: lax.rsqrt(jnp.mean(a * a, axis=-1, keepdims=True) + EPS), yg)
        y_ref[...] = (yg * r * nw_ref[...]).astype(y_ref.dtype)

    return pl.pallas_call(
        body, name="ssd_fwd", grid=(Bl, nc), in_specs=_ssd_in_specs(nc),
        out_specs=[pl.BlockSpec((Q, SSD_INNER), lambda b, c: (b * nc + c, 0)),
                   pl.BlockSpec((1, 1, N_PAIR, 128, 128), lambda b, c: (b, c, 0, 0, 0))],
        out_shape=[jax.ShapeDtypeStruct((M, SSD_INNER), BF16), jax.ShapeDtypeStruct((Bl, nc, N_PAIR, 128, 128), F32)],
        scratch_shapes=[pltpu.VMEM((N_PAIR, 128, 128), F32)],
        compiler_params=_params(("arbitrary", "arbitrary")),
    )(xc, dtr, proj, bias_p, alog_p, d_p, nw)


def _ssd_bwd(xc, dtr, proj, bias_p, alog_p, d_p, nw, prev, dya, dproj, Bl, nc, comm=None):
    M = xc.shape[0]

    def body(xc_ref, dtr_ref, z_ref, bias_ref, alog_ref, d_ref, nw_ref, prev_ref, dy_ref,
             dxc_ref, dz_ref, ddtr_ref, dbias_ref, dalog_ref, dd_ref, dnw_ref, dS):
        b, t = pl.program_id(0), pl.program_id(1)

        @pl.when(t == 0)
        def _():
            dS[...] = jnp.zeros_like(dS)

        s = _ssd_prep(nc - 1 - t, dtr_ref, bias_ref, alog_ref, d_ref)
        states = [prev_ref[0, 0, p] for p in range(N_PAIR)]
        y, _, k = _ssd_chunk(xc_ref, s, states)
        xv, Bs, Cs, Xb = k["xv"], k["Bs"], k["Cs"], k["Xb"]

        zz = z_ref[...].astype(F32)
        sz = _sigmoid(zz)
        silu_z = zz * sz
        yg = y * silu_z
        r = _per_group(lambda a: lax.rsqrt(jnp.mean(a * a, axis=-1, keepdims=True) + EPS), yg)
        xhat = yg * r
        dout = dy_ref[...].astype(F32)
        gw = dout * nw_ref[...]
        dyg = r * (gw - xhat * _per_group(lambda a, c2: jnp.mean(a * c2, axis=-1, keepdims=True), gw, xhat))
        dnw = jnp.sum(dout * xhat, axis=0, keepdims=True)
        dz_ref[...] = (dyg * y * _dsilu(zz, sz)).astype(dz_ref.dtype)
        dy = dyg * silu_z
        dy0 = jnp.where(s["first"], dy, 0.0)
        dyb = (dy0.astype(BF16), (dy - dy0).astype(BF16))
        dYo = (dy * s["ecs_w"]).astype(BF16)

        dS_f = [dS[p] for p in range(N_PAIR)]
        dS_b = [d.astype(BF16) for d in dS_f]
        BdS, dXm, dprev, dCs, dMs, XdS = [], [], [], [[] for _ in range(SSD_GROUPS)], [], []
        for p in range(N_PAIR):
            g, ln = p // 2, slice(128 * p, 128 * (p + 1))
            BdS.append(_dot(Bs[g], dS_b[p], NT))
            dXm.append(_dot(k["Mb"][2 * p], dyb[0][:, ln], TN) + _dot(k["Mb"][2 * p + 1], dyb[1][:, ln], TN))
            dprev.append(_dot(dYo[:, ln], Cs[g], TN))
            dCs[g].append(_dot(dYo[:, ln], k["prev_b"][p], NN))
            for hh in range(2):
                dMs.append(_dot(dyb[hh][:, ln], Xb[hh][:, ln], NT))
                XdS.append(_dot(Xb[hh][:, ln], dS_b[p], NN))
        dX = jnp.concatenate(dXm, axis=1) + s["decay_w"] * jnp.concatenate(BdS, axis=1)
        dxs = dy * s["d_w"] + dX * s["dt_w"]

        heads = lambda a: _dot01(a, s["spread"], NT, "b")
        ddt = heads(dX * xv)
        dcs = heads(dy * k["yo"] * s["ecs_w"])
        dD = jnp.sum(heads(dy * xv), axis=0, keepdims=True)

        col, row = s["col"], s["row"]
        lane1 = col[0:1]
        rowsT = lax.broadcasted_iota(jnp.int32, (128, Q), 0)
        dcs_t = jnp.zeros((128, Q), F32)
        dcl = jnp.zeros((1, 128), F32)
        dB_out, dC_out = [], []
        for g in range(SSD_GROUPS):
            Bf = Bs[g].astype(F32)
            dCB = jnp.zeros((Q, Q), F32)
            dBacc = jnp.zeros((Q, 128), F32)
            for r4 in range(HPG):
                h = HPG * g + r4
                p, hh = h // 2, h % 2
                W = dMs[h] * k["Ms"][h]
                dCB = dCB + dMs[h] * k["Lms"][h]
                decay_h = s["decay_w"][:, SSD_HEAD_DIM * h:SSD_HEAD_DIM * h + 1]
                dBacc = dBacc + decay_h * XdS[h]
                tdec = jnp.sum(XdS[h] * Bf, axis=1, keepdims=True) * decay_h
                dcs = dcs + jnp.where(col == h, jnp.sum(W, axis=1, keepdims=True) - tdec, 0.0)
                dcs_t = dcs_t - jnp.where(rowsT == h, jnp.sum(W, axis=0, keepdims=True), 0.0)
                rows_h = (row < SSD_HEAD_DIM) if hh == 0 else (row >= SSD_HEAD_DIM)
                sprev = jnp.sum(jnp.sum(jnp.where(rows_h, dS_f[p] * states[p], 0.0), axis=1, keepdims=True),
                                axis=0, keepdims=True)
                ecl = jnp.exp(s["cs"][Q - 1:Q, h:h + 1])
                dcl = dcl + jnp.where(lane1 == h, jnp.sum(tdec, axis=0, keepdims=True) + ecl * sprev, 0.0)
            dCB_b = dCB.astype(BF16)
            dC_out.append(dCs[g][0] + dCs[g][1] + _dot(dCB_b, Bs[g], NN))
            dB_out.append(dBacc + _dot(dCB_b, Cs[g], TN))
        for p in range(N_PAIR):
            dS[p] = dS_f[p] * k["ecl_rows"][p] + dprev[p]
        dxc_ref[...] = jnp.concatenate([dxs] + dB_out + dC_out, axis=1).astype(dxc_ref.dtype)

        dcs = dcs + _dot01(s["eye"], dcs_t, NT, "a") + jnp.where(row == Q - 1, dcl, 0.0)
        da = _dot01(row <= col, dcs, NN, "a")
        ddt = ddt + da * s["A"]
        dpre = jnp.where(s["valid"], ddt * _sigmoid(s["pre"]), 0.0)
        ddtr_ref[...] = dpre
        dbias = jnp.sum(dpre, axis=0, keepdims=True)
        dalog = jnp.sum(da * s["dt"], axis=0, keepdims=True) * s["A"]
        first_step = jnp.logical_and(b == 0, t == 0)

        @pl.when(first_step)
        def _():
            dbias_ref[...] = dbias
            dalog_ref[...] = dalog
            dd_ref[...] = dD
            dnw_ref[...] = dnw

        @pl.when(jnp.logical_not(first_step))
        def _():
            dbias_ref[...] += dbias
            dalog_ref[...] += dalog
            dd_ref[...] += dD
            dnw_ref[...] += dnw

    rb = lambda b, c: b * nc + nc - 1 - c
    rowblk = lambda w: pl.BlockSpec((Q, w), lambda b, c: (rb(b, c), 0))
    vec = lambda w: pl.BlockSpec((1, w), lambda b, c: (0, 0))
    return _call(
        body, name="ssd_bwd", grid=(Bl, nc),
        in_specs=_ssd_in_specs(nc, rev=True) + [
            pl.BlockSpec((1, 1, N_PAIR, 128, 128), lambda b, c: (b, nc - 1 - c, 0, 0, 0)), rowblk(SSD_INNER)],
        out_specs=[rowblk(SSD_CONV_CH), rowblk(SSD_INNER), rowblk(128), vec(128), vec(128), vec(128), vec(SSD_INNER)],
        out_shape=[jax.ShapeDtypeStruct((M, SSD_CONV_CH), BF16), jax.ShapeDtypeStruct(dproj.shape, BF16),
                   jax.ShapeDtypeStruct((M, 128), F32), jax.ShapeDtypeStruct((1, 128), F32),
                   jax.ShapeDtypeStruct((1, 128), F32), jax.ShapeDtypeStruct((1, 128), F32),
                   jax.ShapeDtypeStruct((1, SSD_INNER), F32)],
        scratch=[pltpu.VMEM((N_PAIR, 128, 128), F32)], sem=("arbitrary", "arbitrary"),
        args=(xc, dtr, proj, bias_p, alog_p, d_p, nw, prev, dya), comm=comm, into=(dproj, 1))


NSUB = Q // HG_CHUNK
HG_HP = 8
EXP_CAP = 80.0


def _hg_setup(blk, q_ref, f_ref, hb_ref):
    row = lax.broadcasted_iota(jnp.int32, (Q, Q), 0)
    col = lax.broadcasted_iota(jnp.int32, (Q, Q), 1)
    same = (row // HG_CHUNK) == (col // HG_CHUNK)
    causal = jnp.logical_and(same, col <= row)
    lb = _sigmoid(hb_ref[0:1, :] - hb_ref[1:2, :])
    fl = f_ref[...].astype(F32)
    sg = _sigmoid(fl)
    fg = lb + (1.0 - lb) * sg
    k = (1.0 - lb) * (1.0 - sg)
    gl = jnp.log(fg)
    G = _dot01(causal, gl, NN, "a")
    T = _dot01(same, gl, NN, "a")
    qv = q_ref[...].astype(F32)
    sq = _sigmoid(qv)
    eG = jnp.exp(G)
    eGn = jnp.exp(jnp.minimum(-G, EXP_CAP))
    eTG = jnp.exp(T - G)
    qt = qv * sq * eG
    kt = k * eGn
    kh = k * eTG
    valid = jnp.logical_or(blk > 0, row[:, :1] >= PAD)
    return dict(row=row, col=col, same=same, causal=causal, lb=lb, sg=sg, fg=fg, k=k, T=T, qv=qv, sq=sq,
                eG=eG, eGn=eGn, eTG=eTG, qt=qt, kt=kt, kh=kh, valid=valid)


def _hg_specs(nb, rev=False):
    rb = (lambda h, b, t: b * nb + nb - 1 - t) if rev else (lambda h, b, t: b * nb + t)
    w = 128 * HG_HP
    blk = lambda off: pl.BlockSpec((Q, w), lambda h, b, t, off=off: (rb(h, b, t), off // HG_HP + h))
    return [blk(24), blk(32), blk(40), blk(48),
            pl.BlockSpec((2, w), lambda h, b, t: (0, h)), pl.BlockSpec((1, w), lambda h, b, t: (0, h))]


HEAD_LANES = tuple(slice(128 * hh, 128 * (hh + 1)) for hh in range(HG_HP))


def _per_head(fn, *arrs):
    return jnp.concatenate([jnp.broadcast_to(fn(*(a[:, ln] for a in arrs)), (arrs[0].shape[0], 128))
                            for ln in HEAD_LANES], axis=1)


def _hgrn_fwd(proj, hb, nw, Bl, nb, comm=None):
    M = proj.shape[0]

    def body(q_ref, f_ref, i_ref, g_ref, hb_ref, nw_ref, y_ref, o_ref, st_ref, S):
        blk = pl.program_id(2)

        @pl.when(blk == 0)
        def _():
            S[...] = jnp.zeros_like(S)

        s = _hg_setup(blk, q_ref, f_ref, hb_ref)
        v = i_ref[...]
        qt_b, kt_b, kh_b = s["qt"].astype(BF16), s["kt"].astype(BF16), s["kh"].astype(BF16)
        eT = jnp.exp(s["T"])
        att = [jnp.where(s["causal"], _dot(qt_b[:, ln], kt_b[:, ln], NT), 0.0).astype(BF16) for ln in HEAD_LANES]
        o_intra = [_dot(att[hh], v[:, ln], NN) for hh, ln in enumerate(HEAD_LANES)]
        for j in range(NSUB):
            sl = slice(HG_CHUNK * j, HG_CHUNK * (j + 1))
            for hh, ln in enumerate(HEAD_LANES):
                St = S[hh]
                st_ref[0, hh, 0, j] = St
                o_ref[sl, ln] = o_intra[hh][sl] + _dot(qt_b[sl, ln], St.astype(BF16), NT)
                S[hh] = St * eT[HG_CHUNK * j:HG_CHUNK * j + 1, ln] + _dot(v[sl, ln], kh_b[sl, ln], TN)
        o = o_ref[...]
        r = _per_head(lambda a: lax.rsqrt(jnp.mean(a * a, axis=-1, keepdims=True) + EPS), o)
        gv = g_ref[...].astype(F32)
        y_ref[...] = (o * r * nw_ref[...] * gv * _sigmoid(gv)).astype(y_ref.dtype)

    rowblk = pl.BlockSpec((Q, 128 * HG_HP), lambda h, b, t: (b * nb + t, h))
    return _call(
        body, name="hgrn_fwd", grid=(HG_HEADS // HG_HP, Bl, nb), in_specs=_hg_specs(nb),
        out_specs=[rowblk, rowblk,
                   pl.BlockSpec((1, HG_HP, 1, NSUB, 128, 128), lambda h, b, t: (b, h, t, 0, 0, 0))],
        out_shape=[jax.ShapeDtypeStruct((M, HG_WIDTH), BF16), jax.ShapeDtypeStruct((M, HG_WIDTH), F32),
                   jax.ShapeDtypeStruct((Bl, HG_HEADS, nb, NSUB, 128, 128), F32)],
        scratch=[pltpu.VMEM((HG_HP, 128, 128), F32)], sem=("parallel", "arbitrary", "arbitrary"),
        args=(proj, proj, proj, proj, hb, nw), comm=comm)


def _hgrn_bwd(proj, hb, nw, o_saved, st_saved, dyb, dproj, Bl, nb, comm=None):
    assert HG_HP == HG_HEADS

    def body(q_ref, f_ref, i_ref, g_ref, hb_ref, nw_ref, o_ref, st_ref, dy_ref,
             d_ref, dhb_ref, dnw_ref, dS, a_dqt, a_dv, a_dkh, a_dgl):
        b, t = pl.program_id(1), pl.program_id(2)

        @pl.when(t == 0)
        def _():
            dS[...] = jnp.zeros_like(dS)

        first_step = jnp.logical_and(b == 0, t == 0)
        s = _hg_setup(nb - 1 - t, q_ref, f_ref, hb_ref)
        v = i_ref[...]
        qt_b, kt_b, kh_b = s["qt"].astype(BF16), s["kt"].astype(BF16), s["kh"].astype(BF16)
        eT = jnp.exp(s["T"])
        att = [jnp.where(s["causal"], _dot(qt_b[:, ln], kt_b[:, ln], NT), 0.0).astype(BF16) for ln in HEAD_LANES]

        o = o_ref[...]
        r = _per_head(lambda a: lax.rsqrt(jnp.mean(a * a, axis=-1, keepdims=True) + EPS), o)
        xhat = o * r
        gv = g_ref[...].astype(F32)
        sgv = _sigmoid(gv)
        dyv = dy_ref[...].astype(F32)
        d_on = dyv * gv * sgv
        dg_out = dyv * xhat * nw_ref[...] * _dsilu(gv, sgv)
        gw = d_on * nw_ref[...]
        do = r * (gw - xhat * _per_head(lambda a, c: jnp.mean(a * c, axis=-1, keepdims=True), gw, xhat))
        dnw = jnp.sum(d_on * xhat, axis=0, keepdims=True)
        do_b = do.astype(BF16)

        datt = [jnp.where(s["causal"], _dot(do_b[:, ln], v[:, ln], NT), 0.0).astype(BF16) for ln in HEAD_LANES]
        dqt = jnp.concatenate([_dot(datt[hh], kt_b[:, ln], NN) for hh, ln in enumerate(HEAD_LANES)], axis=1)
        dkt = jnp.concatenate([_dot(datt[hh], qt_b[:, ln], TN) for hh, ln in enumerate(HEAD_LANES)], axis=1)
        dv = jnp.concatenate([_dot(att[hh], do_b[:, ln], TN) for hh, ln in enumerate(HEAD_LANES)], axis=1)
        last_row = (lax.broadcasted_iota(jnp.int32, (HG_CHUNK, 128), 0) == HG_CHUNK - 1)
        for j in reversed(range(NSUB)):
            sl = slice(HG_CHUNK * j, HG_CHUNK * (j + 1))
            for hh, ln in enumerate(HEAD_LANES):
                St = st_ref[0, hh, 0, j]
                dSt = dS[hh]
                St_b, dSt_b = St.astype(BF16), dSt.astype(BF16)
                eT_j = eT[HG_CHUNK * j:HG_CHUNK * j + 1, ln]
                dkh_j = _dot(v[sl, ln], dSt_b, NN)
                a_dqt[sl, ln] = _dot(do_b[sl, ln], St_b, NN)
                a_dv[sl, ln] = _dot(kh_b[sl, ln], dSt_b, NT)
                a_dkh[sl, ln] = dkh_j
                dlast = (jnp.sum(St * dSt, axis=0, keepdims=True) * eT_j
                         + jnp.sum(dkh_j * s["kh"][sl, ln], axis=0, keepdims=True))
                a_dgl[sl, ln] = jnp.where(last_row, dlast, 0.0)
                dS[hh] = dSt * eT_j + _dot(do_b[sl, ln], qt_b[sl, ln], TN)
        dqt = dqt + a_dqt[...]
        dv = dv + a_dv[...]
        dkh = a_dkh[...]
        dG = dqt * s["qt"] - dkt * s["kt"] - dkh * s["kh"] + a_dgl[...]
        rev_causal = jnp.logical_and(s["same"], s["col"] >= s["row"])
        dgl = _dot01(rev_causal, dG, NN, "a")
        dk = dkt * s["eGn"] + dkh * s["eTG"]
        dfg = dgl / s["fg"] - dk
        lb, sg = s["lb"], s["sg"]
        keep = s["valid"].astype(F32)
        d_ref[:, 0:w] = (dqt * s["eG"] * _dsilu(s["qv"], s["sq"]) * keep).astype(d_ref.dtype)
        d_ref[:, w:2 * w] = (dfg * (1.0 - lb) * sg * (1.0 - sg) * keep).astype(d_ref.dtype)
        d_ref[:, 2 * w:3 * w] = (dv * keep).astype(d_ref.dtype)
        d_ref[:, 3 * w:4 * w] = (dg_out * keep).astype(d_ref.dtype)
        dlb = jnp.sum(dfg * (1.0 - sg) * keep, axis=0, keepdims=True) * lb * (1.0 - lb)
        dhb = jnp.concatenate([dlb, -dlb], axis=0)

        @pl.when(first_step)
        def _():
            dhb_ref[...] = dhb
            dnw_ref[...] = dnw

        @pl.when(jnp.logical_not(first_step))
        def _():
            dhb_ref[...] += dhb
            dnw_ref[...] += dnw

    w = 128 * HG_HP
    rowblk = pl.BlockSpec((Q, w), lambda h, b, t: (b * nb + nb - 1 - t, h))
    return _call(
        body, name="hgrn_bwd", grid=(HG_HEADS // HG_HP, Bl, nb),
        in_specs=_hg_specs(nb, rev=True) + [
            rowblk, pl.BlockSpec((1, HG_HP, 1, NSUB, 128, 128), lambda h, b, t: (b, h, nb - 1 - t, 0, 0, 0)), rowblk],
        out_specs=[pl.BlockSpec((pl.Element(Q), pl.Element(4 * w)),
                                lambda h, b, t: (pl.multiple_of((b * nb + nb - 1 - t) * Q, Q), 3 * HG_WIDTH)),
                   pl.BlockSpec((2, w), lambda h, b, t: (0, h)), pl.BlockSpec((1, w), lambda h, b, t: (0, h))],
        out_shape=[jax.ShapeDtypeStruct(dproj.shape, BF16),
                   jax.ShapeDtypeStruct((2, HG_WIDTH), F32), jax.ShapeDtypeStruct((1, HG_WIDTH), F32)],
        scratch=[pltpu.VMEM((HG_HP, 128, 128), F32)] + [pltpu.VMEM((Q, w), F32)] * 4,
        sem=("parallel", "arbitrary", "arbitrary"),
        args=(proj, proj, proj, proj, hb, nw, o_saved, st_saved, dyb), comm=comm, into=(dproj, 0))


def _adamw(name, parts, w, m, v):
    R, C = w.shape
    S = parts.shape[0]
    tr, tc = (_tile(R, (256, 176, 128, 64, 8)), C) if R % 8 == 0 else (R, 256)
    c1, c2 = 1.0 - ADAM_B1 ** ADAM_STEP, 1.0 - ADAM_B2 ** ADAM_STEP

    def body(p_ref, w_ref, m_ref, v_ref, g_ref, d_ref, nm_ref, nv_ref):
        g = p_ref[0].astype(F32)
        for s in range(1, S):
            g = g + p_ref[s].astype(F32)
        nm = ADAM_B1 * m_ref[...] + (1.0 - ADAM_B1) * g
        nv = ADAM_B2 * v_ref[...] + (1.0 - ADAM_B2) * (g * g)
        g_ref[...] = g
        nm_ref[...] = nm
        nv_ref[...] = nv
        d_ref[...] = -ADAM_LR * ((nm / c1) / (jnp.sqrt(nv / c2) + ADAM_EPS) + ADAM_WD * w_ref[...])

    blk = pl.BlockSpec((tr, tc), lambda i, j: (i, j))
    return pl.pallas_call(
        body, name=name, grid=(R // tr, C // tc),
        in_specs=[pl.BlockSpec((S, tr, tc), lambda i, j: (0, i, j)), blk, blk, blk], out_specs=[blk] * 4,
        out_shape=[jax.ShapeDtypeStruct((R, C), F32)] * 4, compiler_params=_params(("parallel", "parallel")),
    )(parts, w, m, v)


def _pair_sum(name, by_core, arrived):
    _, J, R, C = by_core.shape
    tc = _tile(C, (512, 256, 128))

    def body(c_ref, a_ref, b_ref, o_ref):
        o_ref[...] = (a_ref[0].astype(F32) + b_ref[...].astype(F32)).astype(o_ref.dtype)

    blk = pl.BlockSpec((1, R, tc), lambda j, k, c_ref: (j, 0, k))
    return pl.pallas_call(
        body, name=name,
        grid_spec=pltpu.PrefetchScalarGridSpec(
            num_scalar_prefetch=1, grid=(J, C // tc),
            in_specs=[pl.BlockSpec((1, 1, R, tc), lambda j, k, c_ref: (c_ref[0], j, 0, k)), blk], out_specs=blk),
        out_shape=jax.ShapeDtypeStruct(arrived.shape, arrived.dtype), compiler_params=_params(("parallel", "parallel")),
    )(lax.axis_index("c").astype(jnp.int32).reshape(1), by_core, arrived)


def _sum_parts(name, parts):
    S, R, C = parts.shape

    def body(p_ref, o_ref):
        g = p_ref[0]
        for s in range(1, S):
            g = g + p_ref[s]
        o_ref[...] = g

    return pl.pallas_call(
        body, name=name, out_shape=jax.ShapeDtypeStruct((R, C), F32),
        in_specs=[pl.BlockSpec(memory_space=pltpu.VMEM)], out_specs=pl.BlockSpec(memory_space=pltpu.VMEM),
    )(parts)


def _heads_to_lanes(p):
    return jnp.pad(p, [(0, 0)] * (p.ndim - 1) + [(0, 128 - SSD_HEADS)])


def _lanes_to_heads(p):
    return p[..., :SSD_HEADS]


def _pack_rows(arrs):
    rows = []
    for a in arrs:
        f = a.reshape(-1).astype(F32)
        n = -(-f.shape[0] // D_MODEL) * D_MODEL
        rows.append(jnp.pad(f, (0, n - f.shape[0])).reshape(-1, D_MODEL))
    out = jnp.concatenate(rows, axis=0)
    return jnp.pad(out, ((0, (-out.shape[0]) % 8), (0, 0)))


def _unpack_rows(packed, like):
    outs, r = [], 0
    for a in like:
        n = 1
        for s in a.shape:
            n *= s
        nr = -(-n // D_MODEL)
        outs.append(packed[r:r + nr].reshape(-1)[:n].reshape(a.shape))
        r += nr
    return outs


def _cols(gth):
    return jnp.transpose(gth, (1, 0, 2)).reshape(gth.shape[1], -1)


def _rows(gth):
    return gth.reshape(-1, gth.shape[2])


def _to_rows(g):
    return g.reshape(N_DEV, -1, g.shape[1]).astype(BF16)


def _by_core(g):
    return jnp.transpose(g.reshape(N_DEV // 2, 2, -1, g.shape[1]), (1, 0, 2, 3)).astype(BF16)


DT_ROW = 3072


def _chip_sums(tag, by_core, swap_in=None):
    arrived = swap_in(by_core) if swap_in else _exchange(tag + "_swap", "swap", by_core)
    return [_pair_sum(f"{tag}_chipsum{i}", m, a) for i, (m, a) in enumerate(zip(by_core, arrived))]


def _ffn_fwd_gu(tag, h, norm_w, w_gu_t, comm=None, n=None):
    M = h.shape[0]
    F = w_gu_t.shape[0] // 2
    tm = _tile(M, (544, 256))
    if n is None:
        n = _rmsnorm_fwd(tag + "_norm", h, norm_w)
    outs = _fused_matmul(
        tag + "_gu", M, F, D_MODEL,
        [dict(a=n, b=w_gu_t, trans_b=True, acc=0, resident=True),
         dict(a=n, b=w_gu_t, trans_b=True, bn_off=1, acc=1, resident=True)], [],
        lambda accs, ex: (accs[0], accs[1], accs[0] * _sigmoid(accs[0]) * accs[1]),
        [BF16, BF16, BF16], 2, tm, F, D_MODEL, outer="i", comm=comm, sub=256)
    return (n, *outs[:3]), outs[3:]


def _rmsnorm_tile(x, w):
    return x * lax.rsqrt(jnp.mean(x * x, axis=-1, keepdims=True) + EPS) * w


def _ffn_fwd_down(tag, h, a, w_down, next_norm=None):
    M = h.shape[0]
    F = w_down.shape[0]
    tm = _tile(M, (1088, 544, 256))
    if next_norm is None:
        (h_out,) = _fused_matmul(
            tag + "_down", M, D_MODEL, F, [dict(a=a, b=w_down, acc=0)], [(h, 0)],
            lambda accs, ex: (ex[0] + 0.5 * accs[0],), [F32], 1, tm, D_MODEL, F, outer="j", sub=256)
        return h_out

    def with_norm(accs, ex):
        h_new = ex[0] + 0.5 * accs[0]
        return h_new, _rmsnorm_tile(h_new, ex[1])

    return _fused_matmul(tag + "_down", M, D_MODEL, F, [dict(a=a, b=w_down, acc=0, resident=True)], [(h, 0)], with_norm,
                         [F32, BF16], 1, tm, D_MODEL, F, outer="j", vecs=[next_norm])


def _ffn_bwd(tag, dh, dh_b, h, norm_w, w_gu_t, w_down, saved, scatter=False):
    n, g, u, a = saved
    M = h.shape[0]
    F = w_down.shape[0]
    tm = _tile(M, (544, 256))
    tn = _tile(F, (1408, 704, 256))

    def swiglu_bwd(accs, ex):
        da, gv, uv = 0.5 * accs[0], ex[0].astype(F32), ex[1].astype(F32)
        s = _sigmoid(gv)
        return da * uv * _dsilu(gv, s), da * gv * s

    (dgu,) = _fused_matmul(
        tag + "_dact", M, F, D_MODEL, [dict(a=dh_b, b=w_down, trans_b=True, acc=0, resident=True)], [(g, 0), (u, 0)],
        swiglu_bwd, [BF16, BF16], 1, tm, F, D_MODEL, outer="i", stack=True, sub=256)
    tr = _tile(M, (2176, 256))
    (dw_down,) = _matmul_tn(tag + "_dwd", a, dh_b, tn, D_MODEL, tr, scale=0.5)
    dw_gu_t, *p_down = _matmul_tn(tag + "_dwgu", dgu, n, tn, D_MODEL, tr,
                                  comm=("scatter", [_to_rows(dw_down)]) if scatter else None)
    comm = None
    if scatter:
        comm = ("chips", _chip_sums(tag + "_wgu", [_by_core(dw_gu_t)]))
    def norm_bwd(accs, ex):
        dh_prev, dw = _rmsnorm_bwd_tile(accs[0], ex[0], ex[2], ex[1])
        return dh_prev, dh_prev, dw

    dh_prev, dh_prev_b, dnorm, *p_gu = _fused_matmul(
        tag + "_dn", M, D_MODEL, F,
        [dict(a=dgu, a_lead=0, b=w_gu_t, acc=0, resident=True),
         dict(a=dgu, a_lead=1, b=w_gu_t, bk_off=1, acc=0, resident=True)], [(h, 0), (dh, 0)],
        norm_bwd, [F32, BF16], 1, tm, D_MODEL, F, outer="i", comm=comm, vecs=[norm_w], row_sums=1)
    return (dh_prev, dh_prev_b, dnorm, *((p_gu[0], p_down[0]) if scatter else (dw_gu_t, dw_down)))


def kernel(x, meta_tokens, ffn1_norm, ffn1_w_gu, ffn1_w_down, mix_norm, w_in, ssd_conv_w, ssd_conv_b, ssd_dt_bias, ssd_a_log, ssd_d, ssd_norm, hg_lower_bound, hg_norm, w_branch_a, w_branch_b, w_out, ffn2_norm, ffn2_w_gu, ffn2_w_down, final_norm, loss_target, m_meta_tokens, m_ffn1_norm, m_ffn1_w_gu, m_ffn1_w_down, m_mix_norm, m_w_in, m_ssd_conv_w, m_ssd_conv_b, m_ssd_dt_bias, m_ssd_a_log, m_ssd_d, m_ssd_norm, m_hg_lower_bound, m_hg_norm, m_w_branch_a, m_w_branch_b, m_w_out, m_ffn2_norm, m_ffn2_w_gu, m_ffn2_w_down, m_final_norm, v_meta_tokens, v_ffn1_norm, v_ffn1_w_gu, v_ffn1_w_down, v_mix_norm, v_w_in, v_ssd_conv_w, v_ssd_conv_b, v_ssd_dt_bias, v_ssd_a_log, v_ssd_d, v_ssd_norm, v_hg_lower_bound, v_hg_norm, v_w_branch_a, v_w_branch_b, v_w_out, v_ffn2_norm, v_ffn2_w_gu, v_ffn2_w_down, v_final_norm):
    Bl, S, D = x.shape
    T = PAD + N_META + S
    nc = T // Q
    M = Bl * T
    me = 4 * lax.axis_index("x") + 2 * lax.axis_index("y") + lax.axis_index("c")

    bf = lambda a: a[0].astype(BF16)
    bft = lambda a: a[0].T.astype(BF16)
    g_wgu1, g_meta, g_conv_w = _exchange("gather_first", "gather", [bft(ffn1_w_gu), meta_tokens, ssd_conv_w[0]])
    wgu1, meta_full, conv_w_full = _rows(g_wgu1), _cols(g_meta), _cols(g_conv_w)
    bias_p, alog_p, d_p = _heads_to_lanes(ssd_dt_bias), _heads_to_lanes(ssd_a_log), _heads_to_lanes(ssd_d)
    final_w = final_norm.reshape(1, D)

    h0 = jnp.concatenate([jnp.zeros((Bl, PAD, D), F32), jnp.broadcast_to(meta_full[None], (Bl, N_META, D)), x],
                         axis=1).reshape(M, D)
    tm = _tile(M, (1088, 544, 256))
    ffn1_saved, (g_wd1, g_win) = _ffn_fwd_gu("ffn1", h0, ffn1_norm, wgu1, comm=("gather", [bf(ffn1_w_down), bft(w_in)]))
    wd1 = _rows(g_wd1)
    win_t = _rows(g_win)
    win_dt = jnp.pad(win_t[DT_ROW:DT_ROW + SSD_HEADS], ((0, 128 - SSD_HEADS), (0, 0)))
    h1, un = _ffn_fwd_down("ffn1", h0, ffn1_saved[3], wd1, next_norm=mix_norm)
    plain = lambda accs, ex: (accs[0],)
    proj, g_wa, g_wb, g_wo = _fused_matmul(
        "in_proj", M, N_MAIN, D, [dict(a=un, b=win_t, trans_b=True, acc=0, b_shift=(DT_ROW // 1536, SSD_HEADS))], [],
        plain, [BF16], 1, tm, 1536, D,
        outer="j", comm=("gather", [bf(w_branch_a), bf(w_branch_b), bf(w_out)]), sub=512)
    wa, wb, wo = _rows(g_wa), _rows(g_wb), _rows(g_wo)
    (dtr,) = _fused_matmul("in_proj_dt", M, 128, D, [dict(a=un, b=win_dt, trans_b=True, acc=0)], [], plain, [F32], 1,
                           tm, 128, D, outer="j")
    xc = _conv_fwd(proj, conv_w_full, ssd_conv_b, Bl, T)
    ya, ssd_prev = _ssd_fwd(xc, dtr, proj, bias_p, alog_p, d_p, ssd_norm, Bl, nc)
    yb, hg_o, hg_st, g_wgu2, g_wd2 = _hgrn_fwd(proj, hg_lower_bound, hg_norm, Bl, nc,
                                               comm=("gather", [bft(ffn2_w_gu), bf(ffn2_w_down)]))
    wgu2, wd2 = _rows(g_wgu2), _rows(g_wd2)

    def branch_fwd(accs, ex):
        pa, pb = accs
        return pa, pb, _sigmoid(ex[0].astype(F32)) * pa + _sigmoid(ex[1].astype(F32)) * pb

    pa, pb, merged = _fused_matmul(
        "branches", M, D, D, [dict(a=ya, b=wa, acc=0), dict(a=yb, b=wb, acc=1)], [(proj, 7), (proj, 8)],
        branch_fwd, [BF16, BF16, BF16], 2, tm, D, D, outer="j", sub=256)
    def out_with_norm(accs, ex):
        h_new = ex[0] + accs[0]
        return h_new, _rmsnorm_tile(h_new, ex[1])

    h2, n2 = _fused_matmul("out_proj", M, D, D, [dict(a=merged, b=wo, acc=0)], [(h1, 0)], out_with_norm,
                           [F32, BF16], 1, tm, D, D, outer="j", vecs=[ffn2_norm])
    ffn2_saved, _ = _ffn_fwd_gu("ffn2", h2, ffn2_norm, wgu2, n=n2)
    h3 = _ffn_fwd_down("ffn2", h2, ffn2_saved[3], wd2)

    dh3, dh3_b, d_final, loss_part = _loss_head(h3, final_w, loss_target, Bl, nc)
    dh2, dh2_b, d_ffn2_norm, d_wgu2, d_wd2 = _ffn_bwd("ffn2", dh3, dh3_b, h2, ffn2_norm, wgu2, wd2, ffn2_saved)

    def branch_bwd(accs, ex):
        dm = accs[0]
        ga, gb, pav, pbv = (e.astype(F32) for e in ex)
        sa, sb = _sigmoid(ga), _sigmoid(gb)
        return (dm * sa, dm * sb,
                jnp.concatenate([dm * pav * sa * (1.0 - sa), dm * pbv * sb * (1.0 - sb)], axis=1))

    d_merged_outs = []

    def d_merged_with_swap(theirs):
        d_merged_outs.extend(_fused_matmul(
            "d_merged", M, D, D, [dict(a=dh2_b, b=wo, trans_b=True, acc=0)], [(proj, 7), (proj, 8), (pa, 0), (pb, 0)],
            branch_bwd, [BF16] * 2, 1, tm, D, D, outer="j", comm=("swap", theirs),
            wide=dict(width=2 * D, col=7 * D, total=N_MAIN, dtype=BF16)))
        return d_merged_outs[3:]

    s_ffn2 = _chip_sums("ffn2", [_by_core(d_wgu2), _by_core(d_wd2)], swap_in=d_merged_with_swap)
    dpa, dpb, dproj = d_merged_outs[:3]
    (d_wo,) = _matmul_tn("d_w_out", merged, dh2_b, 512, D, M)
    (d_wa,) = _matmul_tn("d_w_a", ya, dpa, 512, D, M)
    (d_wb,) = _matmul_tn("d_w_b", yb, dpb, 512, D, M)
    dya, dyb = _fused_matmul(
        "d_branches", M, D, D, [dict(a=dpa, b=wa, trans_b=True, acc=0), dict(a=dpb, b=wb, trans_b=True, acc=1)], [],
        lambda accs, ex: (accs[0], accs[1]), [BF16, BF16], 2, tm, D, D, outer="j")
    *ssd_grads, p_wgu2, p_wd2 = _ssd_bwd(xc, dtr, proj, bias_p, alog_p, d_p, ssd_norm, ssd_prev, dya, dproj, Bl, nc,
                                         comm=("chips", s_ffn2))
    dxc, dproj, ddtr, d_bias_p, d_alog_p, d_d_p, d_ssd_norm = ssd_grads
    dproj, d_conv_w, d_conv_b = _conv_bwd(proj, conv_w_full, ssd_conv_b, dxc, dproj, Bl, T)
    dproj, d_hb, d_hg_norm, p_wa, p_wb, p_wo = _hgrn_bwd(
        proj, hg_lower_bound, hg_norm, hg_o, hg_st, dyb, dproj, Bl, nc,
        comm=("scatter", [_to_rows(d_wa), _to_rows(d_wb), _to_rows(d_wo)]))
    ddtr_b = ddtr.astype(BF16)
    (d_win_t,) = _matmul_tn("d_w_in", dproj, un, 768, D, M, out_skip=(DT_ROW, SSD_HEADS))
    (d_win_dt,) = _matmul_tn("d_w_in_dt", ddtr_b, un, 128, D, M)
    d_win_t = lax.dynamic_update_slice(d_win_t, d_win_dt[:SSD_HEADS], (DT_ROW, 0))
    d_un_dt_outs = []

    def d_un_dt_with_swap(theirs):
        d_un_dt_outs.extend(_fused_matmul("d_un_dt", M, D, 128, [dict(a=ddtr_b, b=win_dt, acc=0)], [], plain, [F32], 1,
                                          tm, D, 128, outer="j", comm=("swap", theirs)))
        return d_un_dt_outs[1:]

    s_win = _chip_sums("w_in", [_by_core(d_win_t)], swap_in=d_un_dt_with_swap)
    def mix_norm_bwd(accs, ex):
        dh, dw = _rmsnorm_bwd_tile(accs[0] + ex[0], ex[1], ex[3], ex[2])
        return dh, dh, dw

    dh1, dh1_b, d_mix_norm, p_win = _fused_matmul(
        "d_un", M, D, N_MAIN, [dict(a=dproj, b=win_t, acc=0, b_shift=(DT_ROW // 3072, SSD_HEADS))],
        [(d_un_dt_outs[0], 0), (h1, 0), (dh2, 0)],
        mix_norm_bwd, [F32, BF16], 1, _tile(M, (544, 256)), D, 3072, outer="i", comm=("chips", s_win),
        vecs=[mix_norm], row_sums=1)
    dh0, _, d_ffn1_norm, p_wgu1, p_wd1 = _ffn_bwd("ffn1", dh1, dh1_b, h0, ffn1_norm, wgu1, wd1, ffn1_saved, scatter=True)

    dh0 = dh0.reshape(Bl, T, D)
    grad_x = dh0[:, PAD + N_META:]
    d_meta = dh0[:, PAD:PAD + N_META]

    small_grads = [d_ffn1_norm, d_mix_norm, d_conv_b, _lanes_to_heads(d_bias_p), _lanes_to_heads(d_alog_p),
                   _lanes_to_heads(d_d_p), d_ssd_norm, d_hb, d_hg_norm, d_ffn2_norm, d_final.reshape(D), d_conv_w]
    small_packed = _pack_rows(small_grads + [d_meta[b] for b in range(Bl)])
    parts = [p_wgu1, p_wd1, p_win, p_wa, p_wb, p_wo, p_wgu2, p_wd2]
    (small_all,) = _exchange("gather_small_grads", "gather", [small_packed])
    small_sum = _sum_parts("sum_small_grads", small_all)
    unpacked = _unpack_rows(small_sum, small_grads + [d_meta[b] for b in range(Bl)])
    g_small = unpacked[:len(small_grads)]
    g_meta_full = unpacked[len(small_grads)]
    for b in range(1, Bl):
        g_meta_full = g_meta_full + unpacked[len(small_grads) + b]
    g_meta = lax.dynamic_slice_in_dim(g_meta_full, me * (D // N_DEV), D // N_DEV, axis=1)
    g_conv_w = lax.dynamic_slice_in_dim(g_small[11], me * (SSD_CONV_CH // N_DEV), SSD_CONV_CH // N_DEV, axis=1)

    names = ["meta_tokens", "ffn1_norm", "ffn1_w_gu", "ffn1_w_down", "mix_norm", "w_in", "ssd_conv_w", "ssd_conv_b",
             "ssd_dt_bias", "ssd_a_log", "ssd_d", "ssd_norm", "hg_lower_bound", "hg_norm", "w_branch_a", "w_branch_b",
             "w_out", "ffn2_norm", "ffn2_w_gu", "ffn2_w_down", "final_norm"]
    W = dict(meta_tokens=meta_tokens, ffn1_norm=ffn1_norm, ffn1_w_gu=ffn1_w_gu, ffn1_w_down=ffn1_w_down, mix_norm=mix_norm,
             w_in=w_in, ssd_conv_w=ssd_conv_w, ssd_conv_b=ssd_conv_b, ssd_dt_bias=ssd_dt_bias, ssd_a_log=ssd_a_log,
             ssd_d=ssd_d, ssd_norm=ssd_norm, hg_lower_bound=hg_lower_bound, hg_norm=hg_norm, w_branch_a=w_branch_a,
             w_branch_b=w_branch_b, w_out=w_out, ffn2_norm=ffn2_norm, ffn2_w_gu=ffn2_w_gu, ffn2_w_down=ffn2_w_down,
             final_norm=final_norm)
    Mo = dict(meta_tokens=m_meta_tokens, ffn1_norm=m_ffn1_norm, ffn1_w_gu=m_ffn1_w_gu, ffn1_w_down=m_ffn1_w_down,
              mix_norm=m_mix_norm, w_in=m_w_in, ssd_conv_w=m_ssd_conv_w, ssd_conv_b=m_ssd_conv_b, ssd_dt_bias=m_ssd_dt_bias,
              ssd_a_log=m_ssd_a_log, ssd_d=m_ssd_d, ssd_norm=m_ssd_norm, hg_lower_bound=m_hg_lower_bound, hg_norm=m_hg_norm,
              w_branch_a=m_w_branch_a, w_branch_b=m_w_branch_b, w_out=m_w_out, ffn2_norm=m_ffn2_norm, ffn2_w_gu=m_ffn2_w_gu,
              ffn2_w_down=m_ffn2_w_down, final_norm=m_final_norm)
    Vo = dict(meta_tokens=v_meta_tokens, ffn1_norm=v_ffn1_norm, ffn1_w_gu=v_ffn1_w_gu, ffn1_w_down=v_ffn1_w_down,
              mix_norm=v_mix_norm, w_in=v_w_in, ssd_conv_w=v_ssd_conv_w, ssd_conv_b=v_ssd_conv_b, ssd_dt_bias=v_ssd_dt_bias,
              ssd_a_log=v_ssd_a_log, ssd_d=v_ssd_d, ssd_norm=v_ssd_norm, hg_lower_bound=v_hg_lower_bound, hg_norm=v_hg_norm,
              w_branch_a=v_w_branch_a, w_branch_b=v_w_branch_b, w_out=v_w_out, ffn2_norm=v_ffn2_norm, ffn2_w_gu=v_ffn2_w_gu,
              ffn2_w_down=v_ffn2_w_down, final_norm=v_final_norm)
    grads, deltas, new_m, new_v = {}, {}, {}, {}
    big_names = ["ffn1_w_gu", "ffn1_w_down", "w_in", "w_branch_a", "w_branch_b", "w_out", "ffn2_w_gu", "ffn2_w_down"]
    transposed = ("ffn1_w_gu", "ffn2_w_gu", "w_in")
    for nm, part in zip(big_names, parts):
        view = (lambda a: a[0].T) if nm in transposed else (lambda a: a[0])
        back = (lambda o: o.T[None]) if nm in transposed else (lambda o: o[None])
        outs = _adamw("adamw_" + nm, part, view(W[nm]), view(Mo[nm]), view(Vo[nm]))
        grads[nm], deltas[nm], new_m[nm], new_v[nm] = (back(o) for o in outs)
    small_names = ["ffn1_norm", "mix_norm", "ssd_conv_b", "ssd_dt_bias", "ssd_a_log", "ssd_d", "ssd_norm", "hg_lower_bound",
                   "hg_norm", "ffn2_norm", "final_norm", "ssd_conv_w", "meta_tokens"]
    small_g = g_small[:11] + [g_conv_w.reshape(ssd_conv_w.shape), g_meta]
    pk = lambda d: _pack_rows([d[nm] for nm in small_names])
    outs = _adamw("adamw_small", _pack_rows(small_g)[None], pk(W), pk(Mo), pk(Vo))
    like = [W[nm] for nm in small_names]
    for dst, o in zip((grads, deltas, new_m, new_v), outs):
        for nm, val in zip(small_names, _unpack_rows(o, like)):
            dst[nm] = val

    loss = lax.psum(loss_part[0, 0], MESH_AXES)
    return (loss, grad_x, *[grads[nm] for nm in names], *[deltas[nm] for nm in names],
            *[new_m[nm] for nm in names], *[new_v[nm] for nm in names])
```

```python
import functools

import jax
import jax.numpy as jnp
from jax import lax
from jax.experimental import pallas as pl
from jax.experimental.pallas import tpu as pltpu

F32, BF16 = jnp.float32, jnp.bfloat16
NN, NT, TN = ((1,), (0,)), ((1,), (1,)), ((0,), (0,))
MESH_AXES = ("x", "y", "c")
N_DEV = 8

D_MODEL = 1024
N_META = 16
EPS = 1e-6
SSD_HEADS, SSD_HEAD_DIM, SSD_GROUPS, SSD_STATE, SSD_CONV, Q = 16, 64, 4, 128, 4, 128
SSD_INNER = SSD_HEADS * SSD_HEAD_DIM
SSD_CONV_CH = SSD_INNER + 2 * SSD_GROUPS * SSD_STATE
HG_WIDTH, HG_HEADS, HG_CHUNK = 1024, 8, 16
PAD = Q - N_META
N_MAIN = 9 * 1024
ADAM_LR, ADAM_B1, ADAM_B2, ADAM_EPS, ADAM_WD, ADAM_STEP = 0.001, 0.9, 0.999, 1e-08, 0.01, 10
VMEM_LIMIT = 52 * 1024 * 1024


def _dot(a, b, dims, prec=None):
    return lax.dot_general(a, b, (dims, ((), ())), precision=prec, preferred_element_type=F32)


def _dot01(a, b, dims, sel):
    x = b if sel == "a" else a
    hi = x.astype(BF16)
    r1 = x - hi.astype(F32)
    mid = r1.astype(BF16)
    lo = (r1 - mid.astype(F32)).astype(BF16)
    s = (a if sel == "a" else b).astype(BF16)
    parts = [_dot(s, p, dims) if sel == "a" else _dot(p, s, dims) for p in (hi, mid, lo)]
    return parts[0] + parts[1] + parts[2]


def _sigmoid(x):
    return 1.0 / (1.0 + jnp.exp(-x))


def _dsilu(x, s):
    return s * (1.0 + x * (1.0 - s))


def _softplus(x):
    e = jnp.exp(-jnp.abs(x))
    u = 1.0 + e
    log1p_e = jnp.where(u == 1.0, e, jnp.log(u) * e / (u - 1.0))
    return jnp.maximum(x, 0.0) + log1p_e


def _params(sem):
    return pltpu.CompilerParams(dimension_semantics=sem, vmem_limit_bytes=VMEM_LIMIT)


def _tile(n, prefs):
    for p in prefs:
        if n % p == 0:
            return p
    return n


CHIP_FLIPS = ((1, 0), (0, 1), (1, 1))
N_PEER = N_DEV - 1


def _comm_gather(srcs, outs, send_sems, recv_sems, local_sems):
    n = len(srcs)
    x, y, c = (lax.axis_index(a) for a in MESH_AXES)
    dev = lambda px, py, pc: 4 * px + 2 * py + pc
    me, sib = dev(x, y, c), (x, y, 1 - c)

    def rc(w, k, slot, to, src=None):
        return pltpu.make_async_remote_copy(
            src_ref=outs[w].at[slot] if src is None else src, dst_ref=outs[w].at[slot],
            send_sem=send_sems.at[w, k], recv_sem=recv_sems.at[w, k], device_id=to, device_id_type=pl.DeviceIdType.MESH)

    def local(w):
        return pltpu.make_async_copy(srcs[w], outs[w].at[me], local_sems.at[w])

    def start():
        for w in range(n):
            local(w).start()
            rc(w, 0, me, sib, src=srcs[w]).start()
            for j, (fx, fy) in enumerate(CHIP_FLIPS):
                rc(w, 1 + j, me, (x ^ fx, y ^ fy, c), src=srcs[w]).start()

    def pass_on():
        for w in range(n):
            for j, (fx, fy) in enumerate(CHIP_FLIPS):
                slot = dev(x ^ fx, y ^ fy, c)
                rc(w, 1 + j, slot, sib).wait_recv()
                rc(w, 4 + j, slot, sib).start()

    def finish():
        for w in range(n):
            rc(w, 0, dev(x, y, 1 - c), sib).wait_recv()
            rc(w, 0, me, sib, src=srcs[w]).wait_send()
            for j, (fx, fy) in enumerate(CHIP_FLIPS):
                rc(w, 4 + j, dev(x ^ fx, y ^ fy, 1 - c), sib).wait_recv()
                rc(w, 1 + j, me, sib, src=srcs[w]).wait_send()
                rc(w, 4 + j, dev(x ^ fx, y ^ fy, c), sib).wait_send()
            local(w).wait()

    return start, pass_on, finish


def _comm_scatter(srcs, outs, send_sems, recv_sems, local_sems):
    n = len(srcs)
    x, y, c = (lax.axis_index(a) for a in MESH_AXES)
    me = 4 * x + 2 * y + c

    def copies():
        out = []
        for w in range(n):
            out.append(pltpu.make_async_copy(srcs[w].at[me], outs[w].at[me], local_sems.at[w]))
            for k in range(1, N_DEV):
                px, py, pc = x ^ (k >> 2), y ^ ((k >> 1) & 1), c ^ (k & 1)
                out.append(pltpu.make_async_remote_copy(
                    src_ref=srcs[w].at[4 * px + 2 * py + pc], dst_ref=outs[w].at[me],
                    send_sem=send_sems.at[w, k - 1], recv_sem=recv_sems.at[w, k - 1],
                    device_id=(px, py, pc), device_id_type=pl.DeviceIdType.MESH))
        return out

    def start():
        for cp in copies():
            cp.start()

    def finish():
        for cp in copies():
            cp.wait()

    return start, None, finish


def _comm_swap(srcs, outs, send_sems, recv_sems, local_sems):
    x, y, c = (lax.axis_index(a) for a in MESH_AXES)

    def copies():
        return [pltpu.make_async_remote_copy(
            src_ref=srcs[w].at[1 - c], dst_ref=outs[w], send_sem=send_sems.at[w, 0], recv_sem=recv_sems.at[w, 0],
            device_id=(x, y, 1 - c), device_id_type=pl.DeviceIdType.MESH) for w in range(len(srcs))]

    def start():
        for cp in copies():
            cp.start()

    def finish():
        for cp in copies():
            cp.wait()

    return start, None, finish


def _comm_chips(srcs, outs, send_sems, recv_sems, local_sems):
    n = len(srcs)
    x, y, c = (lax.axis_index(a) for a in MESH_AXES)
    mine = 2 * x + y

    def copies():
        out = []
        for w in range(n):
            out.append(pltpu.make_async_copy(srcs[w].at[mine], outs[w].at[mine], local_sems.at[w]))
            for j, (fx, fy) in enumerate(CHIP_FLIPS):
                px, py = x ^ fx, y ^ fy
                out.append(pltpu.make_async_remote_copy(
                    src_ref=srcs[w].at[2 * px + py], dst_ref=outs[w].at[mine],
                    send_sem=send_sems.at[w, j], recv_sem=recv_sems.at[w, j],
                    device_id=(px, py, c), device_id_type=pl.DeviceIdType.MESH))
        return out

    def start():
        for cp in copies():
            cp.start()

    def finish():
        for cp in copies():
            cp.wait()

    return start, None, finish


def _comm_parts(comm):
    kind, arrays = comm[:2]
    n = len(arrays)
    lead = {"gather": lambda a: (N_DEV,) + a.shape, "scatter": lambda a: (N_DEV,) + a.shape[1:],
            "swap": lambda a: a.shape[1:], "chips": lambda a: a.shape}[kind]
    shapes = [jax.ShapeDtypeStruct(lead(a), a.dtype) for a in arrays]
    sems = [pltpu.SemaphoreType.DMA((n, N_PEER)), pltpu.SemaphoreType.DMA((n, N_PEER)), pltpu.SemaphoreType.DMA((n,))]
    make = {"gather": _comm_gather, "scatter": _comm_scatter, "swap": _comm_swap, "chips": _comm_chips}[kind]
    return n, shapes, sems, make


def _exchange(name, kind, arrays):
    n, shapes, sems, make = _comm_parts((kind, arrays))

    def body(*refs):
        start, middle, finish = make(refs[:n], refs[n:2 * n], *refs[2 * n:])
        start()
        if middle:
            middle()
        finish()

    any_spec = pl.BlockSpec(memory_space=pl.ANY)
    return pl.pallas_call(
        body, name=name, in_specs=[any_spec] * n, out_specs=[any_spec] * n, out_shape=shapes, scratch_shapes=sems,
        compiler_params=pltpu.CompilerParams(has_side_effects=True),
    )(*arrays)


def _call(body, *, name, grid, in_specs, out_specs, out_shape, scratch, sem, args, comm=None, into=None):
    any_spec = pl.BlockSpec(memory_space=pl.ANY)
    in_specs, args, aliases, n_body_in = list(in_specs), list(args), {}, len(in_specs)
    if into is not None:
        in_specs.append(any_spec)
        args.append(into[0])
        aliases = {n_body_in: into[1]}
    n_in, n_out, n_scr = len(in_specs), len(out_specs), len(scratch)
    if comm is None:
        def plain(*refs):
            body(*refs[:n_body_in], *refs[n_in:])

        return pl.pallas_call(plain, name=name, grid=grid, in_specs=in_specs, out_specs=out_specs, out_shape=out_shape,
                              scratch_shapes=scratch, input_output_aliases=aliases, compiler_params=_params(sem))(*args)
    n, shapes, sems, make = _comm_parts(comm)

    def carrier(*refs):
        ins, csrc = refs[:n_body_in], refs[n_in:n_in + n]
        outs, cout = refs[n_in + n:n_in + n + n_out], refs[n_in + n + n_out:n_in + 2 * n + n_out]
        rest = refs[n_in + 2 * n + n_out:]
        start, middle, finish = make(csrc, cout, *rest[n_scr:])
        ids = [pl.program_id(a) for a in range(len(grid))]
        step = functools.reduce(lambda acc, ig: acc * ig[1] + ig[0], zip(ids, grid), 0)
        n_steps = functools.reduce(lambda a, b: a * b, grid, 1)
        pl.when(step == 0)(start)
        body(*ins, *outs, *rest[:n_scr])
        if middle:
            early = len(comm) > 2 and comm[2] == "early"
            pl.when(step == ((3 * n_steps) // 4 if early else n_steps - 1))(middle)
        pl.when(step == n_steps - 1)(finish)

    return pl.pallas_call(
        carrier, name=name, grid=grid, in_specs=in_specs + [any_spec] * n,
        out_specs=list(out_specs) + [any_spec] * n, out_shape=list(out_shape) + shapes,
        scratch_shapes=list(scratch) + sems, input_output_aliases=aliases,
        compiler_params=pltpu.CompilerParams(dimension_semantics=("arbitrary",) * len(grid),
                                             vmem_limit_bytes=VMEM_LIMIT, has_side_effects=True),
    )(*args, *comm[1])


def _fused_matmul(name, M, N, K, pairs, extras, epilogue, out_dtypes, n_acc, tm, tn, tk, outer="i", comm=None,
                  stack=False, vecs=(), row_sums=0, wide=None, sub=None):
    nk = K // tk
    n_pairs, n_ex, n_out = len(pairs), len(extras), len(out_dtypes)
    assert not row_sums or (outer == "i" and N == tn)

    def ij(g0, g1):
        return (g0, g1) if outer == "i" else (g1, g0)

    in_specs, args = [], []
    for p in pairs:
        ao, bk, bn = p.get("a_off", 0), p.get("bk_off", 0), p.get("bn_off", 0)
        mode = dict(pipeline_mode=pl.Buffered(1)) if p.get("resident") else {}
        if "a_lead" in p:
            in_specs.append(pl.BlockSpec((None, tm, tk),
                                         lambda g0, g1, k, ao=ao, ld=p["a_lead"]: (ld, ij(g0, g1)[0], k + ao)))
        else:
            in_specs.append(pl.BlockSpec((tm, tk), lambda g0, g1, k, ao=ao: (ij(g0, g1)[0], k + ao)))
        if "b_shift" in p:
            first, shift = p["b_shift"]
            if p.get("trans_b"):
                in_specs.append(pl.BlockSpec(
                    (pl.Element(tn), pl.Element(tk)),
                    lambda g0, g1, k, bk=bk: (
                        pl.multiple_of(ij(g0, g1)[1] * tn + jnp.where(ij(g0, g1)[1] >= first, shift, 0), 16),
                        (k + bk) * tk)))
            else:
                in_specs.append(pl.BlockSpec(
                    (pl.Element(tk), pl.Element(tn)),
                    lambda g0, g1, k, bn=bn: (pl.multiple_of(k * tk + jnp.where(k >= first, shift, 0), 16),
                                              (ij(g0, g1)[1] + bn) * tn)))
        elif p.get("trans_b"):
            in_specs.append(pl.BlockSpec((tn, tk), lambda g0, g1, k, bk=bk, bn=bn: (ij(g0, g1)[1] + bn, k + bk), **mode))
        else:
            in_specs.append(pl.BlockSpec((tk, tn), lambda g0, g1, k, bk=bk, bn=bn: (k + bk, ij(g0, g1)[1] + bn), **mode))
        args += [p["a"], p["b"]]
    for arr, off in extras:
        in_specs.append(pl.BlockSpec((tm, tn), lambda g0, g1, k, off=off: (ij(g0, g1)[0], ij(g0, g1)[1] + off)))
        args.append(arr)
    for arr in vecs:
        in_specs.append(pl.BlockSpec((1, tn), lambda g0, g1, k: (0, ij(g0, g1)[1])))
        args.append(arr)
    if stack:
        out_specs = [pl.BlockSpec((n_out, tm, tn), lambda g0, g1, k: (0,) + ij(g0, g1))]
        out_shape = [jax.ShapeDtypeStruct((n_out, M, N), out_dtypes[0])]
    else:
        out_specs = [pl.BlockSpec((tm, tn), lambda g0, g1, k: ij(g0, g1)) for _ in out_dtypes]
        out_shape = [jax.ShapeDtypeStruct((M, N), dt) for dt in out_dtypes]
    if wide:
        out_specs.append(pl.BlockSpec((pl.Element(tm), pl.Element(wide["width"])),
                                      lambda g0, g1, k: (pl.multiple_of(ij(g0, g1)[0] * tm, 16), wide["col"])))
        out_shape.append(jax.ShapeDtypeStruct((M, wide["total"]), wide["dtype"]))
    n_tile_out = len(out_specs)
    out_specs += [pl.BlockSpec((1, tn), lambda g0, g1, k: (0, 0)) for _ in range(row_sums)]
    out_shape += [jax.ShapeDtypeStruct((1, N), F32) for _ in range(row_sums)]
    grid = (M // tm, N // tn, nk) if outer == "i" else (N // tn, M // tm, nk)
    n_in = 2 * n_pairs + n_ex + len(vecs)

    def partials(refs, cs=slice(None)):
        accs = [None] * n_acc
        for idx, p in enumerate(pairs):
            b_ref = refs[2 * idx + 1]
            d = (_dot(refs[2 * idx][...], b_ref[cs, :], NT) if p.get("trans_b")
                 else _dot(refs[2 * idx][...], b_ref[:, cs], NN))
            accs[p["acc"]] = d if accs[p["acc"]] is None else accs[p["acc"]] + d
        return accs

    def finish(accs, refs, first_rows, cs=slice(None)):
        res = epilogue(accs, [r[:, cs] for r in refs[2 * n_pairs:n_in]])
        if stack:
            o = refs[n_in]
            for idx in range(n_out):
                o[idx, :, cs] = res[idx].astype(o.dtype)
        else:
            for o, r in zip(refs[n_in:n_in + n_out], res):
                o[:, cs] = r.astype(o.dtype)
        if wide:
            o = refs[n_in + n_tile_out - 1]
            o[...] = res[n_out].astype(o.dtype)
        for o, r in zip(refs[n_in + n_tile_out:n_in + n_tile_out + row_sums], res[n_out + bool(wide):]):
            @pl.when(first_rows)
            def _(o=o, r=r):
                o[...] = r

            @pl.when(jnp.logical_not(first_rows))
            def _(o=o, r=r):
                o[...] += r

    if nk == 1 and sub:
        assert not wide and not row_sums and tn % sub == 0

        def body(*refs):
            for c in range(tn // sub):
                cs = slice(c * sub, (c + 1) * sub)
                finish(partials(refs, cs), refs, None, cs)
        scratch = []
    elif nk == 1:
        def body(*refs):
            finish(partials(refs), refs, pl.program_id(0) == 0)
        scratch = []
    else:
        def body(*refs):
            acc_refs = refs[-n_acc:]
            k = pl.program_id(2)
            first_rows = pl.program_id(0) == 0
            new = partials(refs)

            @pl.when(k == 0)
            def _():
                for a, v in zip(acc_refs, new):
                    a[...] = v

            @pl.when(k > 0)
            def _():
                for a, v in zip(acc_refs, new):
                    a[...] += v

            @pl.when(k == nk - 1)
            def _():
                finish([a[...] for a in acc_refs], refs, first_rows)
        scratch = [pltpu.VMEM((tm, tn), F32) for _ in range(n_acc)]

    return _call(body, name=name, grid=grid, in_specs=in_specs, out_specs=out_specs, out_shape=out_shape,
                 scratch=scratch, sem=("parallel", "parallel", "arbitrary"), args=args, comm=comm)


def _matmul_tn(name, x, y, t1, t2, tr, scale=1.0, comm=None, out_dtype=BF16, out_skip=None):
    L = x.shape[0] if x.ndim == 3 else 1
    R, K1 = x.shape[-2:]
    N1 = y.shape[1]
    nr, n1 = R // tr, K1 // t1
    if x.ndim == 3:
        x_spec = pl.BlockSpec((None, tr, t1), lambda i, j, r: (i // n1, r, i % n1))
    else:
        x_spec = pl.BlockSpec((tr, t1), lambda i, j, r: (r, i))
    rows_out = L * K1
    o_spec = pl.BlockSpec((t1, t2), lambda i, j, r: (i, j))
    if out_skip:
        row, count = out_skip
        rows_out += count
        o_spec = pl.BlockSpec(
            (pl.Element(t1), pl.Element(t2)),
            lambda i, j, r: (pl.multiple_of(i * t1 + jnp.where(i * t1 >= row, count, 0), 16), j * t2))

    def body(x_ref, y_ref, o_ref, *acc):
        d = _dot(x_ref[...], y_ref[...], TN)
        if nr == 1:
            o_ref[...] = (d * scale).astype(o_ref.dtype)
            return
        r = pl.program_id(2)

        @pl.when(r == 0)
        def _():
            acc[0][...] = d

        @pl.when(jnp.logical_and(r > 0, r < nr - 1))
        def _():
            acc[0][...] += d

        @pl.when(r == nr - 1)
        def _():
            o_ref[...] = ((acc[0][...] + d) * scale).astype(o_ref.dtype)

    return _call(
        body, name=name, grid=(L * n1, N1 // t2, nr),
        in_specs=[x_spec, pl.BlockSpec((tr, t2), lambda i, j, r: (r, j))], out_specs=[o_spec],
        out_shape=[jax.ShapeDtypeStruct((rows_out, N1), out_dtype)],
        scratch=[pltpu.VMEM((t1, t2), F32)] if nr > 1 else [],
        sem=("parallel", "parallel", "arbitrary"), args=(x, y), comm=comm)


def _rmsnorm_fwd(name, h, w):
    M, D = h.shape
    tm = _tile(M, (544, 256, 128))

    def body(h_ref, w_ref, o_ref):
        x = h_ref[...]
        r = lax.rsqrt(jnp.mean(x * x, axis=-1, keepdims=True) + EPS)
        o_ref[...] = (x * r * w_ref[...]).astype(o_ref.dtype)

    return pl.pallas_call(
        body, name=name, grid=(M // tm,),
        in_specs=[pl.BlockSpec((tm, D), lambda i: (i, 0)), pl.BlockSpec((1, D), lambda i: (0, 0))],
        out_specs=pl.BlockSpec((tm, D), lambda i: (i, 0)),
        out_shape=jax.ShapeDtypeStruct((M, D), BF16), compiler_params=_params(("parallel",)),
    )(h, w)


def _rmsnorm_bwd_tile(dn, h, w, dh_in):
    r = lax.rsqrt(jnp.mean(h * h, axis=-1, keepdims=True) + EPS)
    xhat = h * r
    gw = dn * w
    dh = dh_in + r * (gw - xhat * jnp.mean(gw * xhat, axis=-1, keepdims=True))
    return dh, jnp.sum(dn * xhat, axis=0, keepdims=True)


def _loss_head(h, w, target, Bl, nb):
    M, D = h.shape

    def body(h_ref, w_ref, t_ref, dh_ref, dhb_ref, dw_ref, loss_ref):
        b, t = pl.program_id(0), pl.program_id(1)
        live = (t > 0).astype(F32)
        x = h_ref[...]
        r = lax.rsqrt(jnp.mean(x * x, axis=-1, keepdims=True) + EPS)
        xhat = x * r
        wv = w_ref[...]
        err = (xhat * wv - t_ref[0]) * live
        dy = err * (1.0 / D)
        gw = dy * wv
        dx = r * (gw - xhat * jnp.mean(gw * xhat, axis=-1, keepdims=True))
        dh_ref[...] = dx
        dhb_ref[...] = dx.astype(BF16)
        dw = jnp.sum(dy * xhat, axis=0, keepdims=True)
        part = 0.5 * jnp.sum(jnp.sum(err * err, axis=-1, keepdims=True) * (1.0 / D), axis=0, keepdims=True)
        first = jnp.logical_and(b == 0, t == 0)

        @pl.when(first)
        def _():
            dw_ref[...] = dw
            loss_ref[...] = jnp.broadcast_to(part, loss_ref.shape)

        @pl.when(jnp.logical_not(first))
        def _():
            dw_ref[...] += dw
            loss_ref[...] += jnp.broadcast_to(part, loss_ref.shape)

    row = pl.BlockSpec((Q, D), lambda b, t: (b * nb + t, 0))
    vec = pl.BlockSpec((1, D), lambda b, t: (0, 0))
    return pl.pallas_call(
        body, name="loss_head", grid=(Bl, nb),
        in_specs=[row, vec, pl.BlockSpec((1, Q, D), lambda b, t: (b, jnp.maximum(t - 1, 0), 0))],
        out_specs=[row, row, vec, pl.BlockSpec((8, 128), lambda b, t: (0, 0))],
        out_shape=[jax.ShapeDtypeStruct((M, D), F32), jax.ShapeDtypeStruct((M, D), BF16),
                   jax.ShapeDtypeStruct((1, D), F32), jax.ShapeDtypeStruct((8, 128), F32)],
        compiler_params=_params(("arbitrary", "arbitrary")),
    )(h, w, target)


CONV_TC = 256


def _conv_pre(xr_ref, w_ref, b_ref):
    x = xr_ref[...].astype(F32)
    acc = b_ref[...] + w_ref[SSD_CONV - 1:SSD_CONV, :] * x
    for k in range(1, SSD_CONV):
        acc = acc + w_ref[SSD_CONV - 1 - k:SSD_CONV - k, :] * pltpu.roll(x, k, 0)
    return x, acc


def _conv_fwd(proj, w, b, Bl, T):
    M = proj.shape[0]
    off = 1024 // CONV_TC

    def body(xr_ref, w_ref, b_ref, o_ref):
        _, acc = _conv_pre(xr_ref, w_ref, b_ref)
        row = lax.broadcasted_iota(jnp.int32, acc.shape, 0)
        o_ref[...] = jnp.where(row >= PAD, acc * _sigmoid(acc), 0.0).astype(o_ref.dtype)

    return pl.pallas_call(
        body, name="conv_fwd", grid=(Bl, SSD_CONV_CH // CONV_TC),
        in_specs=[pl.BlockSpec((T, CONV_TC), lambda bb, j: (bb, j + off)),
                  pl.BlockSpec((SSD_CONV, CONV_TC), lambda bb, j: (0, j)), pl.BlockSpec((1, CONV_TC), lambda bb, j: (0, j))],
        out_specs=pl.BlockSpec((T, CONV_TC), lambda bb, j: (bb, j)),
        out_shape=jax.ShapeDtypeStruct((M, SSD_CONV_CH), BF16), compiler_params=_params(("parallel", "parallel")),
    )(proj, w, b)


def _conv_bwd(proj, w, b, dxc, dproj, Bl, T):
    M = proj.shape[0]
    off = 1024 // CONV_TC

    def body(xr_ref, w_ref, b_ref, d_ref, dx_ref, dw_ref, db_ref):
        x, acc = _conv_pre(xr_ref, w_ref, b_ref)
        row = lax.broadcasted_iota(jnp.int32, acc.shape, 0)
        s = _sigmoid(acc)
        dpre = jnp.where(row >= PAD, d_ref[...].astype(F32) * _dsilu(acc, s), 0.0)
        dx = w_ref[SSD_CONV - 1:SSD_CONV, :] * dpre
        dws = [jnp.sum(dpre * x, axis=0, keepdims=True)]
        for k in range(1, SSD_CONV):
            dx = dx + w_ref[SSD_CONV - 1 - k:SSD_CONV - k, :] * pltpu.roll(dpre, T - k, 0)
            dws.append(jnp.sum(dpre * pltpu.roll(x, k, 0), axis=0, keepdims=True))
        dx_ref[...] = dx.astype(dx_ref.dtype)
        dw = jnp.concatenate(dws[::-1], axis=0)
        db = jnp.sum(dpre, axis=0, keepdims=True)

        @pl.when(pl.program_id(1) == 0)
        def _():
            dw_ref[...] = dw
            db_ref[...] = db

        @pl.when(pl.program_id(1) > 0)
        def _():
            dw_ref[...] += dw
            db_ref[...] += db

    return _call(
        body, name="conv_bwd", grid=(SSD_CONV_CH // CONV_TC, Bl),
        in_specs=[pl.BlockSpec((T, CONV_TC), lambda j, bb: (bb, j + off)),
                  pl.BlockSpec((SSD_CONV, CONV_TC), lambda j, bb: (0, j)), pl.BlockSpec((1, CONV_TC), lambda j, bb: (0, j)),
                  pl.BlockSpec((T, CONV_TC), lambda j, bb: (bb, j))],
        out_specs=[pl.BlockSpec((T, CONV_TC), lambda j, bb: (bb, j + off)),
                   pl.BlockSpec((SSD_CONV, CONV_TC), lambda j, bb: (0, j)), pl.BlockSpec((1, CONV_TC), lambda j, bb: (0, j))],
        out_shape=[jax.ShapeDtypeStruct(dproj.shape, BF16), jax.ShapeDtypeStruct((SSD_CONV, SSD_CONV_CH), F32),
                   jax.ShapeDtypeStruct((1, SSD_CONV_CH), F32)],
        scratch=[], sem=("parallel", "arbitrary"), args=(proj, w, b, dxc), into=(dproj, 0))


N_PAIR = SSD_HEADS // 2
HPG = SSD_HEADS // SSD_GROUPS
GW = SSD_INNER // SSD_GROUPS


def _per_group(fn, *arrs):
    return jnp.concatenate([jnp.broadcast_to(fn(*(a[:, GW * g:GW * (g + 1)] for a in arrs)), (arrs[0].shape[0], GW))
                            for g in range(SSD_GROUPS)], axis=1)


def _ssd_prep(c, dtr_ref, bias_ref, alog_ref, d_ref):
    row = lax.broadcasted_iota(jnp.int32, (Q, 128), 0)
    col = lax.broadcasted_iota(jnp.int32, (Q, 128), 1)
    live = col < SSD_HEADS
    valid = jnp.logical_and(jnp.logical_or(c > 0, row >= PAD), live)
    pre = dtr_ref[...] + bias_ref[...]
    dt = jnp.where(valid, _softplus(pre), 0.0)
    A = jnp.where(live[0:1], -jnp.exp(alog_ref[...]), 0.0)
    tri = row >= col
    eye = (row == col).astype(BF16)
    cs = _dot01(tri, dt * A, NN, "a")
    cst = _dot01(eye, cs, NT, "a")
    spread = (lax.broadcasted_iota(jnp.int32, (128, SSD_INNER), 0)
              == lax.broadcasted_iota(jnp.int32, (128, SSD_INNER), 1) // SSD_HEAD_DIM).astype(BF16)
    dt_w = _dot01(dt, spread, NN, "b")
    cs_w = _dot01(cs, spread, NN, "b")
    d_w = _dot01(jnp.broadcast_to(d_ref[...], (8, 128)), spread, NN, "b")[0:1]
    lane = lax.broadcasted_iota(jnp.int32, (Q, SSD_INNER), 1)
    first = (lane % 128) < SSD_HEAD_DIM
    return dict(row=row, col=col, valid=valid, pre=pre, dt=dt, A=A, tri=tri, eye=eye, cs=cs, cst=cst, spread=spread,
                dt_w=dt_w, cs_w=cs_w, d_w=d_w, ecs_w=jnp.exp(cs_w), decay_w=jnp.exp(cs_w[Q - 1:Q] - cs_w), first=first)


def _ssd_chunk(xc_ref, s, states):
    xv = xc_ref[:, 0:SSD_INNER].astype(F32)
    Bs = [xc_ref[:, SSD_INNER + 128 * g:SSD_INNER + 128 * (g + 1)] for g in range(SSD_GROUPS)]
    Cs = [xc_ref[:, SSD_INNER + 512 + 128 * g:SSD_INNER + 512 + 128 * (g + 1)] for g in range(SSD_GROUPS)]
    X = xv * s["dt_w"]
    X0 = jnp.where(s["first"], X, 0.0)
    Xb = (X0.astype(BF16), (X - X0).astype(BF16))
    Xd = (X * s["decay_w"]).astype(BF16)
    CB = [_dot(Cs[g], Bs[g], NT) for g in range(SSD_GROUPS)]
    Lms = [jnp.exp(jnp.where(s["tri"], s["cs"][:, h:h + 1] - s["cst"][h:h + 1, :], -jnp.inf)) for h in range(SSD_HEADS)]
    Ms = [CB[h // HPG] * Lms[h] for h in range(SSD_HEADS)]
    Mb = [m.astype(BF16) for m in Ms]
    prev_b = [st.astype(BF16) for st in states]
    yds, yos, sts = [], [], []
    for p in range(N_PAIR):
        g, ln = p // 2, slice(128 * p, 128 * (p + 1))
        yds.append(_dot(Mb[2 * p], Xb[0][:, ln], NN) + _dot(Mb[2 * p + 1], Xb[1][:, ln], NN))
        yos.append(_dot(Cs[g], prev_b[p], NT))
        sts.append(_dot(Xd[:, ln], Bs[g], TN))
    yo = jnp.concatenate(yos, axis=1)
    y = jnp.concatenate(yds, axis=1) + yo * s["ecs_w"] + xv * s["d_w"]
    upper = s["row"] < SSD_HEAD_DIM
    cl = s["cs"][Q - 1:Q, :]
    ecl_rows = [jnp.where(upper, jnp.exp(cl[:, 2 * p:2 * p + 1]), jnp.exp(cl[:, 2 * p + 1:2 * p + 2])) for p in range(N_PAIR)]
    new_states = [states[p] * ecl_rows[p] + sts[p] for p in range(N_PAIR)]
    return y, new_states, dict(xv=xv, Bs=Bs, Cs=Cs, X=X, Xb=Xb, CB=CB, Lms=Lms, Ms=Ms, Mb=Mb, prev_b=prev_b, yo=yo,
                               ecl_rows=ecl_rows)


def _ssd_in_specs(nc, rev=False):
    rb = (lambda b, c: b * nc + nc - 1 - c) if rev else (lambda b, c: b * nc + c)
    vec = pl.BlockSpec((1, 128), lambda b, c: (0, 0))
    return [pl.BlockSpec((Q, SSD_CONV_CH), lambda b, c: (rb(b, c), 0)),
            pl.BlockSpec((Q, 128), lambda b, c: (rb(b, c), 0)),
            pl.BlockSpec((Q, SSD_INNER), lambda b, c: (rb(b, c), 0)),
            vec, vec, vec, pl.BlockSpec((1, SSD_INNER), lambda b, c: (0, 0))]


def _ssd_fwd(xc, dtr, proj, bias_p, alog_p, d_p, nw, Bl, nc):
    M = xc.shape[0]

    def body(xc_ref, dtr_ref, z_ref, bias_ref, alog_ref, d_ref, nw_ref, y_ref, prev_ref, state):
        c = pl.program_id(1)

        @pl.when(c == 0)
        def _():
            state[...] = jnp.zeros_like(state)

        s = _ssd_prep(c, dtr_ref, bias_ref, alog_ref, d_ref)
        states = [state[p] for p in range(N_PAIR)]
        y, new_states, _ = _ssd_chunk(xc_ref, s, states)
        for p in range(N_PAIR):
            prev_ref[0, 0, p] = states[p]
            state[p] = new_states[p]
        zz = z_ref[...].astype(F32)
        yg = y * zz * _sigmoid(zz)
        r = _per_group(lambda a: lax.rsqrt(jnp.mean(a * a, axis=-1, keepdims=True) + EPS), yg)
        y_ref[...] = (yg * r * nw_ref[...]).astype(y_ref.dtype)

    return pl.pallas_call(
        body, name="ssd_fwd", grid=(Bl, nc), in_specs=_ssd_in_specs(nc),
        out_specs=[pl.BlockSpec((Q, SSD_INNER), lambda b, c: (b * nc + c, 0)),
                   pl.BlockSpec((1, 1, N_PAIR, 128, 128), lambda b, c: (b, c, 0, 0, 0))],
        out_shape=[jax.ShapeDtypeStruct((M, SSD_INNER), BF16), jax.ShapeDtypeStruct((Bl, nc, N_PAIR, 128, 128), F32)],
        scratch_shapes=[pltpu.VMEM((N_PAIR, 128, 128), F32)],
        compiler_params=_params(("arbitrary", "arbitrary")),
    )(xc, dtr, proj, bias_p, alog_p, d_p, nw)


def _ssd_bwd(xc, dtr, proj, bias_p, alog_p, d_p, nw, prev, dya, dproj, Bl, nc, comm=None):
    M = xc.shape[0]

    def body(xc_ref, dtr_ref, z_ref, bias_ref, alog_ref, d_ref, nw_ref, prev_ref, dy_ref,
             dxc_ref, dz_ref, ddtr_ref, dbias_ref, dalog_ref, dd_ref, dnw_ref, dS):
        b, t = pl.program_id(0), pl.program_id(1)

        @pl.when(t == 0)
        def _():
            dS[...] = jnp.zeros_like(dS)

        s = _ssd_prep(nc - 1 - t, dtr_ref, bias_ref, alog_ref, d_ref)
        states = [prev_ref[0, 0, p] for p in range(N_PAIR)]
        y, _, k = _ssd_chunk(xc_ref, s, states)
        xv, Bs, Cs, Xb = k["xv"], k["Bs"], k["Cs"], k["Xb"]

        zz = z_ref[...].astype(F32)
        sz = _sigmoid(zz)
        silu_z = zz * sz
        yg = y * silu_z
        r = _per_group(lambda a: lax.rsqrt(jnp.mean(a * a, axis=-1, keepdims=True) + EPS), yg)
        xhat = yg * r
        dout = dy_ref[...].astype(F32)
        gw = dout * nw_ref[...]
        dyg = r * (gw - xhat * _per_group(lambda a, c2: jnp.mean(a * c2, axis=-1, keepdims=True), gw, xhat))
        dnw = jnp.sum(dout * xhat, axis=0, keepdims=True)
        dz_ref[...] = (dyg * y * _dsilu(zz, sz)).astype(dz_ref.dtype)
        dy = dyg * silu_z
        dy0 = jnp.where(s["first"], dy, 0.0)
        dyb = (dy0.astype(BF16), (dy - dy0).astype(BF16))
        dYo = (dy * s["ecs_w"]).astype(BF16)

        dS_f = [dS[p] for p in range(N_PAIR)]
        dS_b = [d.astype(BF16) for d in dS_f]
        BdS, dXm, dprev, dCs, dMs, XdS = [], [], [], [[] for _ in range(SSD_GROUPS)], [], []
        for p in range(N_PAIR):
            g, ln = p // 2, slice(128 * p, 128 * (p + 1))
            BdS.append(_dot(Bs[g], dS_b[p], NT))
            dXm.append(_dot(k["Mb"][2 * p], dyb[0][:, ln], TN) + _dot(k["Mb"][2 * p + 1], dyb[1][:, ln], TN))
            dprev.append(_dot(dYo[:, ln], Cs[g], TN))
            dCs[g].append(_dot(dYo[:, ln], k["prev_b"][p], NN))
            for hh in range(2):
                dMs.append(_dot(dyb[hh][:, ln], Xb[hh][:, ln], NT))
                XdS.append(_dot(Xb[hh][:, ln], dS_b[p], NN))
        dX = jnp.concatenate(dXm, axis=1) + s["decay_w"] * jnp.concatenate(BdS, axis=1)
        dxs = dy * s["d_w"] + dX * s["dt_w"]

        sums = _dot01(jnp.concatenate([dX * xv, dy * k["yo"] * s["ecs_w"], dy * xv], axis=0), s["spread"], NT, "b")
        ddt, dcs = sums[0:Q], sums[Q:2 * Q]
        dD = jnp.sum(sums[2 * Q:3 * Q], axis=0, keepdims=True)

        col, row = s["col"], s["row"]
        lane1 = col[0:1]
        rowsT = lax.broadcasted_iota(jnp.int32, (128, Q), 0)
        dcs_t = jnp.zeros((128, Q), F32)
        dcl = jnp.zeros((1, 128), F32)
        dB_out, dC_out = [], []
        for g in range(SSD_GROUPS):
            Bf = Bs[g].astype(F32)
            dCB = jnp.zeros((Q, Q), F32)
            dBacc = jnp.zeros((Q, 128), F32)
            for r4 in range(HPG):
                h = HPG * g + r4
                p, hh = h // 2, h % 2
                W = dMs[h] * k["Ms"][h]
                dCB = dCB + dMs[h] * k["Lms"][h]
                decay_h = s["decay_w"][:, SSD_HEAD_DIM * h:SSD_HEAD_DIM * h + 1]
                dBacc = dBacc + decay_h * XdS[h]
                tdec = jnp.sum(XdS[h] * Bf, axis=1, keepdims=True) * decay_h
                dcs = dcs + jnp.where(col == h, jnp.sum(W, axis=1, keepdims=True) - tdec, 0.0)
                dcs_t = dcs_t - jnp.where(rowsT == h, jnp.sum(W, axis=0, keepdims=True), 0.0)
                rows_h = (row < SSD_HEAD_DIM) if hh == 0 else (row >= SSD_HEAD_DIM)
                sprev = jnp.sum(jnp.sum(jnp.where(rows_h, dS_f[p] * states[p], 0.0), axis=1, keepdims=True),
                                axis=0, keepdims=True)
                ecl = jnp.exp(s["cs"][Q - 1:Q, h:h + 1])
                dcl = dcl + jnp.where(lane1 == h, jnp.sum(tdec, axis=0, keepdims=True) + ecl * sprev, 0.0)
            dCB_b = dCB.astype(BF16)
            dC_out.append(dCs[g][0] + dCs[g][1] + _dot(dCB_b, Bs[g], NN))
            dB_out.append(dBacc + _dot(dCB_b, Cs[g], TN))
        for p in range(N_PAIR):
            dS[p] = dS_f[p] * k["ecl_rows"][p] + dprev[p]
        dxc_ref[...] = jnp.concatenate([dxs] + dB_out + dC_out, axis=1).astype(dxc_ref.dtype)

        dcs = dcs + _dot01(s["eye"], dcs_t, NT, "a") + jnp.where(row == Q - 1, dcl, 0.0)
        da = _dot01(row <= col, dcs, NN, "a")
        ddt = ddt + da * s["A"]
        dpre = jnp.where(s["valid"], ddt * _sigmoid(s["pre"]), 0.0)
        ddtr_ref[...] = dpre
        dbias = jnp.sum(dpre, axis=0, keepdims=True)
        dalog = jnp.sum(da * s["dt"], axis=0, keepdims=True) * s["A"]
        first_step = jnp.logical_and(b == 0, t == 0)

        @pl.when(first_step)
        def _():
            dbias_ref[...] = dbias
            dalog_ref[...] = dalog
            dd_ref[...] = dD
            dnw_ref[...] = dnw

        @pl.when(jnp.logical_not(first_step))
        def _():
            dbias_ref[...] += dbias
            dalog_ref[...] += dalog
            dd_ref[...] += dD
            dnw_ref[...] += dnw

    rb = lambda b, c: b * nc + nc - 1 - c
    rowblk = lambda w: pl.BlockSpec((Q, w), lambda b, c: (rb(b, c), 0))
    vec = lambda w: pl.BlockSpec((1, w), lambda b, c: (0, 0))
    return _call(
        body, name="ssd_bwd", grid=(Bl, nc),
        in_specs=_ssd_in_specs(nc, rev=True) + [
            pl.BlockSpec((1, 1, N_PAIR, 128, 128), lambda b, c: (b, nc - 1 - c, 0, 0, 0)), rowblk(SSD_INNER)],
        out_specs=[rowblk(SSD_CONV_CH), rowblk(SSD_INNER), rowblk(128), vec(128), vec(128), vec(128), vec(SSD_INNER)],
        out_shape=[jax.ShapeDtypeStruct((M, SSD_CONV_CH), BF16), jax.ShapeDtypeStruct(dproj.shape, BF16),
                   jax.ShapeDtypeStruct((M, 128), F32), jax.ShapeDtypeStruct((1, 128), F32),
                   jax.ShapeDtypeStruct((1, 128), F32), jax.ShapeDtypeStruct((1, 128), F32),
                   jax.ShapeDtypeStruct((1, SSD_INNER), F32)],
        scratch=[pltpu.VMEM((N_PAIR, 128, 128), F32)], sem=("arbitrary", "arbitrary"),
        args=(xc, dtr, proj, bias_p, alog_p, d_p, nw, prev, dya), comm=comm, into=(dproj, 1))


NSUB = Q // HG_CHUNK
HG_HP = 8
EXP_CAP = 80.0


def _hg_setup(blk, q_ref, f_ref, hb_ref):
    row = lax.broadcasted_iota(jnp.int32, (Q, Q), 0)
    col = lax.broadcasted_iota(jnp.int32, (Q, Q), 1)
    same = (row // HG_CHUNK) == (col // HG_CHUNK)
    causal = jnp.logical_and(same, col <= row)
    lb = _sigmoid(hb_ref[0:1, :] - hb_ref[1:2, :])
    fl = f_ref[...].astype(F32)
    sg = _sigmoid(fl)
    fg = lb + (1.0 - lb) * sg
    k = (1.0 - lb) * (1.0 - sg)
    gl = jnp.log(fg)
    G = _dot01(causal, gl, NN, "a")
    T = _dot01(same, gl, NN, "a")
    qv = q_ref[...].astype(F32)
    sq = _sigmoid(qv)
    eG = jnp.exp(G)
    eGn = jnp.exp(jnp.minimum(-G, EXP_CAP))
    eTG = jnp.exp(T - G)
    qt = qv * sq * eG
    kt = k * eGn
    kh = k * eTG
    valid = jnp.logical_or(blk > 0, row[:, :1] >= PAD)
    return dict(row=row, col=col, same=same, causal=causal, lb=lb, sg=sg, fg=fg, k=k, T=T, qv=qv, sq=sq,
                eG=eG, eGn=eGn, eTG=eTG, qt=qt, kt=kt, kh=kh, valid=valid)


def _hg_specs(nb, rev=False):
    rb = (lambda h, b, t: b * nb + nb - 1 - t) if rev else (lambda h, b, t: b * nb + t)
    w = 128 * HG_HP
    blk = lambda off: pl.BlockSpec((Q, w), lambda h, b, t, off=off: (rb(h, b, t), off // HG_HP + h))
    return [blk(24), blk(32), blk(40), blk(48),
            pl.BlockSpec((2, w), lambda h, b, t: (0, h)), pl.BlockSpec((1, w), lambda h, b, t: (0, h))]


HEAD_LANES = tuple(slice(128 * hh, 128 * (hh + 1)) for hh in range(HG_HP))


def _per_head(fn, *arrs):
    return jnp.concatenate([jnp.broadcast_to(fn(*(a[:, ln] for a in arrs)), (arrs[0].shape[0], 128))
                            for ln in HEAD_LANES], axis=1)


def _hgrn_fwd(proj, hb, nw, Bl, nb, comm=None):
    M = proj.shape[0]

    def body(q_ref, f_ref, i_ref, g_ref, hb_ref, nw_ref, y_ref, o_ref, st_ref, S):
        blk = pl.program_id(2)

        @pl.when(blk == 0)
        def _():
            S[...] = jnp.zeros_like(S)

        s = _hg_setup(blk, q_ref, f_ref, hb_ref)
        v = i_ref[...]
        qt_b, kt_b, kh_b = s["qt"].astype(BF16), s["kt"].astype(BF16), s["kh"].astype(BF16)
        eT = jnp.exp(s["T"])
        att = [jnp.where(s["causal"], _dot(qt_b[:, ln], kt_b[:, ln], NT), 0.0).astype(BF16) for ln in HEAD_LANES]
        o_intra = [_dot(att[hh], v[:, ln], NN) for hh, ln in enumerate(HEAD_LANES)]
        for j in range(NSUB):
            sl = slice(HG_CHUNK * j, HG_CHUNK * (j + 1))
            for hh, ln in enumerate(HEAD_LANES):
                St = S[hh]
                st_ref[0, hh, 0, j] = St
                o_ref[sl, ln] = o_intra[hh][sl] + _dot(qt_b[sl, ln], St.astype(BF16), NT)
                S[hh] = St * eT[HG_CHUNK * j:HG_CHUNK * j + 1, ln] + _dot(v[sl, ln], kh_b[sl, ln], TN)
        o = o_ref[...]
        r = _per_head(lambda a: lax.rsqrt(jnp.mean(a * a, axis=-1, keepdims=True) + EPS), o)
        gv = g_ref[...].astype(F32)
        y_ref[...] = (o * r * nw_ref[...] * gv * _sigmoid(gv)).astype(y_ref.dtype)

    rowblk = pl.BlockSpec((Q, 128 * HG_HP), lambda h, b, t: (b * nb + t, h))
    return _call(
        body, name="hgrn_fwd", grid=(HG_HEADS // HG_HP, Bl, nb), in_specs=_hg_specs(nb),
        out_specs=[rowblk, rowblk,
                   pl.BlockSpec((1, HG_HP, 1, NSUB, 128, 128), lambda h, b, t: (b, h, t, 0, 0, 0))],
        out_shape=[jax.ShapeDtypeStruct((M, HG_WIDTH), BF16), jax.ShapeDtypeStruct((M, HG_WIDTH), F32),
                   jax.ShapeDtypeStruct((Bl, HG_HEADS, nb, NSUB, 128, 128), F32)],
        scratch=[pltpu.VMEM((HG_HP, 128, 128), F32)], sem=("parallel", "arbitrary", "arbitrary"),
        args=(proj, proj, proj, proj, hb, nw), comm=comm)


def _hgrn_bwd(proj, hb, nw, o_saved, st_saved, dyb, dproj, Bl, nb, comm=None):
    assert HG_HP == HG_HEADS

    def body(q_ref, f_ref, i_ref, g_ref, hb_ref, nw_ref, o_ref, st_ref, dy_ref,
             d_ref, dhb_ref, dnw_ref, dS, a_dqt, a_dv, a_dkh, a_dgl):
        b, t = pl.program_id(1), pl.program_id(2)

        @pl.when(t == 0)
        def _():
            dS[...] = jnp.zeros_like(dS)

        first_step = jnp.logical_and(b == 0, t == 0)
        s = _hg_setup(nb - 1 - t, q_ref, f_ref, hb_ref)
        v = i_ref[...]
        qt_b, kt_b, kh_b = s["qt"].astype(BF16), s["kt"].astype(BF16), s["kh"].astype(BF16)
        eT = jnp.exp(s["T"])
        att = [jnp.where(s["causal"], _dot(qt_b[:, ln], kt_b[:, ln], NT), 0.0).astype(BF16) for ln in HEAD_LANES]

        o = o_ref[...]
        r = _per_head(lambda a: lax.rsqrt(jnp.mean(a * a, axis=-1, keepdims=True) + EPS), o)
        xhat = o * r
        gv = g_ref[...].astype(F32)
        sgv = _sigmoid(gv)
        dyv = dy_ref[...].astype(F32)
        d_on = dyv * gv * sgv
        dg_out = dyv * xhat * nw_ref[...] * _dsilu(gv, sgv)
        gw = d_on * nw_ref[...]
        do = r * (gw - xhat * _per_head(lambda a, c: jnp.mean(a * c, axis=-1, keepdims=True), gw, xhat))
        dnw = jnp.sum(d_on * xhat, axis=0, keepdims=True)
        do_b = do.astype(BF16)

        datt = [jnp.where(s["causal"], _dot(do_b[:, ln], v[:, ln], NT), 0.0).astype(BF16) for ln in HEAD_LANES]
        dqt = jnp.concatenate([_dot(datt[hh], kt_b[:, ln], NN) for hh, ln in enumerate(HEAD_LANES)], axis=1)
        dkt = jnp.concatenate([_dot(datt[hh], qt_b[:, ln], TN) for hh, ln in enumerate(HEAD_LANES)], axis=1)
        dv = jnp.concatenate([_dot(att[hh], do_b[:, ln], TN) for hh, ln in enumerate(HEAD_LANES)], axis=1)
        last_row = (lax.broadcasted_iota(jnp.int32, (HG_CHUNK, 128), 0) == HG_CHUNK - 1)
        for j in reversed(range(NSUB)):
            sl = slice(HG_CHUNK * j, HG_CHUNK * (j + 1))
            for hh, ln in enumerate(HEAD_LANES):
                St = st_ref[0, hh, 0, j]
                dSt = dS[hh]
                St_b, dSt_b = St.astype(BF16), dSt.astype(BF16)
                eT_j = eT[HG_CHUNK * j:HG_CHUNK * j + 1, ln]
                dkh_j = _dot(v[sl, ln], dSt_b, NN)
                a_dqt[sl, ln] = _dot(do_b[sl, ln], St_b, NN)
                a_dv[sl, ln] = _dot(kh_b[sl, ln], dSt_b, NT)
                a_dkh[sl, ln] = dkh_j
                dlast = (jnp.sum(St * dSt, axis=0, keepdims=True) * eT_j
                         + jnp.sum(dkh_j * s["kh"][sl, ln], axis=0, keepdims=True))
                a_dgl[sl, ln] = jnp.where(last_row, dlast, 0.0)
                dS[hh] = dSt * eT_j + _dot(do_b[sl, ln], qt_b[sl, ln], TN)
        dqt = dqt + a_dqt[...]
        dv = dv + a_dv[...]
        dkh = a_dkh[...]
        dG = dqt * s["qt"] - dkt * s["kt"] - dkh * s["kh"] + a_dgl[...]
        rev_causal = jnp.logical_and(s["same"], s["col"] >= s["row"])
        dgl = _dot01(rev_causal, dG, NN, "a")
        dk = dkt * s["eGn"] + dkh * s["eTG"]
        dfg = dgl / s["fg"] - dk
        lb, sg = s["lb"], s["sg"]
        keep = s["valid"].astype(F32)
        d_ref[:, 0:w] = (dqt * s["eG"] * _dsilu(s["qv"], s["sq"]) * keep).astype(d_ref.dtype)
        d_ref[:, w:2 * w] = (dfg * (1.0 - lb) * sg * (1.0 - sg) * keep).astype(d_ref.dtype)
        d_ref[:, 2 * w:3 * w] = (dv * keep).astype(d_ref.dtype)
        d_ref[:, 3 * w:4 * w] = (dg_out * keep).astype(d_ref.dtype)
        dlb = jnp.sum(dfg * (1.0 - sg) * keep, axis=0, keepdims=True) * lb * (1.0 - lb)
        dhb = jnp.concatenate([dlb, -dlb], axis=0)

        @pl.when(first_step)
        def _():
            dhb_ref[...] = dhb
            dnw_ref[...] = dnw

        @pl.when(jnp.logical_not(first_step))
        def _():
            dhb_ref[...] += dhb
            dnw_ref[...] += dnw

    w = 128 * HG_HP
    rowblk = pl.BlockSpec((Q, w), lambda h, b, t: (b * nb + nb - 1 - t, h))
    return _call(
        body, name="hgrn_bwd", grid=(HG_HEADS // HG_HP, Bl, nb),
        in_specs=_hg_specs(nb, rev=True) + [
            rowblk, pl.BlockSpec((1, HG_HP, 1, NSUB, 128, 128), lambda h, b, t: (b, h, nb - 1 - t, 0, 0, 0)), rowblk],
        out_specs=[pl.BlockSpec((pl.Element(Q), pl.Element(4 * w)),
                                lambda h, b, t: (pl.multiple_of((b * nb + nb - 1 - t) * Q, Q), 3 * HG_WIDTH)),
                   pl.BlockSpec((2, w), lambda h, b, t: (0, h)), pl.BlockSpec((1, w), lambda h, b, t: (0, h))],
        out_shape=[jax.ShapeDtypeStruct(dproj.shape, BF16),
                   jax.ShapeDtypeStruct((2, HG_WIDTH), F32), jax.ShapeDtypeStruct((1, HG_WIDTH), F32)],
        scratch=[pltpu.VMEM((HG_HP, 128, 128), F32)] + [pltpu.VMEM((Q, w), F32)] * 4,
        sem=("parallel", "arbitrary", "arbitrary"),
        args=(proj, proj, proj, proj, hb, nw, o_saved, st_saved, dyb), comm=comm, into=(dproj, 0))


def _adamw(name, parts, w, m, v):
    R, C = w.shape
    S = parts.shape[0]
    tr, tc = (_tile(R, (256, 176, 128, 64, 8)), C) if R % 8 == 0 else (R, 256)
    c1, c2 = 1.0 - ADAM_B1 ** ADAM_STEP, 1.0 - ADAM_B2 ** ADAM_STEP

    def body(p_ref, w_ref, m_ref, v_ref, g_ref, d_ref, nm_ref, nv_ref):
        g = p_ref[0].astype(F32)
        for s in range(1, S):
            g = g + p_ref[s].astype(F32)
        nm = ADAM_B1 * m_ref[...] + (1.0 - ADAM_B1) * g
        nv = ADAM_B2 * v_ref[...] + (1.0 - ADAM_B2) * (g * g)
        g_ref[...] = g
        nm_ref[...] = nm
        nv_ref[...] = nv
        d_ref[...] = -ADAM_LR * ((nm / c1) / (jnp.sqrt(nv / c2) + ADAM_EPS) + ADAM_WD * w_ref[...])

    blk = pl.BlockSpec((tr, tc), lambda i, j: (i, j))
    return pl.pallas_call(
        body, name=name, grid=(R // tr, C // tc),
        in_specs=[pl.BlockSpec((S, tr, tc), lambda i, j: (0, i, j)), blk, blk, blk], out_specs=[blk] * 4,
        out_shape=[jax.ShapeDtypeStruct((R, C), F32)] * 4, compiler_params=_params(("parallel", "parallel")),
    )(parts, w, m, v)


def _pair_sum(name, by_core, arrived):
    _, J, R, C = by_core.shape
    tc = _tile(C, (512, 256, 128))

    def body(c_ref, a_ref, b_ref, o_ref):
        o_ref[...] = (a_ref[0].astype(F32) + b_ref[...].astype(F32)).astype(o_ref.dtype)

    blk = pl.BlockSpec((1, R, tc), lambda j, k, c_ref: (j, 0, k))
    return pl.pallas_call(
        body, name=name,
        grid_spec=pltpu.PrefetchScalarGridSpec(
            num_scalar_prefetch=1, grid=(J, C // tc),
            in_specs=[pl.BlockSpec((1, 1, R, tc), lambda j, k, c_ref: (c_ref[0], j, 0, k)), blk], out_specs=blk),
        out_shape=jax.ShapeDtypeStruct(arrived.shape, arrived.dtype), compiler_params=_params(("parallel", "parallel")),
    )(lax.axis_index("c").astype(jnp.int32).reshape(1), by_core, arrived)


def _sum_parts(name, parts):
    S, R, C = parts.shape

    def body(p_ref, o_ref):
        g = p_ref[0]
        for s in range(1, S):
            g = g + p_ref[s]
        o_ref[...] = g

    return pl.pallas_call(
        body, name=name, out_shape=jax.ShapeDtypeStruct((R, C), F32),
        in_specs=[pl.BlockSpec(memory_space=pltpu.VMEM)], out_specs=pl.BlockSpec(memory_space=pltpu.VMEM),
    )(parts)


def _heads_to_lanes(p):
    return jnp.pad(p, [(0, 0)] * (p.ndim - 1) + [(0, 128 - SSD_HEADS)])


def _lanes_to_heads(p):
    return p[..., :SSD_HEADS]


def _pack_rows(arrs):
    rows = []
    for a in arrs:
        f = a.reshape(-1).astype(F32)
        n = -(-f.shape[0] // D_MODEL) * D_MODEL
        rows.append(jnp.pad(f, (0, n - f.shape[0])).reshape(-1, D_MODEL))
    out = jnp.concatenate(rows, axis=0)
    return jnp.pad(out, ((0, (-out.shape[0]) % 8), (0, 0)))


def _unpack_rows(packed, like):
    outs, r = [], 0
    for a in like:
        n = 1
        for s in a.shape:
            n *= s
        nr = -(-n // D_MODEL)
        outs.append(packed[r:r + nr].reshape(-1)[:n].reshape(a.shape))
        r += nr
    return outs


def _cols(gth):
    return jnp.transpose(gth, (1, 0, 2)).reshape(gth.shape[1], -1)


def _rows(gth):
    return gth.reshape(-1, gth.shape[2])


def _to_rows(g):
    return g.reshape(N_DEV, -1, g.shape[1]).astype(BF16)


def _by_core(g):
    return jnp.transpose(g.reshape(N_DEV // 2, 2, -1, g.shape[1]), (1, 0, 2, 3)).astype(BF16)


DT_ROW = 3072


def _chip_sums(tag, by_core, swap_in=None):
    arrived = swap_in(by_core) if swap_in else _exchange(tag + "_swap", "swap", by_core)
    return [_pair_sum(f"{tag}_chipsum{i}", m, a) for i, (m, a) in enumerate(zip(by_core, arrived))]


def _ffn_fwd_gu(tag, h, norm_w, w_gu_t, comm=None, n=None):
    M = h.shape[0]
    F = w_gu_t.shape[0] // 2
    tm = _tile(M, (544, 256))
    if n is None:
        n = _rmsnorm_fwd(tag + "_norm", h, norm_w)
    outs = _fused_matmul(
        tag + "_gu", M, F, D_MODEL,
        [dict(a=n, b=w_gu_t, trans_b=True, acc=0, resident=True),
         dict(a=n, b=w_gu_t, trans_b=True, bn_off=1, acc=1, resident=True)], [],
        lambda accs, ex: (accs[0], accs[1], accs[0] * _sigmoid(accs[0]) * accs[1]),
        [BF16, BF16, BF16], 2, tm, F, D_MODEL, outer="i", comm=comm, sub=256)
    return (n, *outs[:3]), outs[3:]


def _rmsnorm_tile(x, w):
    return x * lax.rsqrt(jnp.mean(x * x, axis=-1, keepdims=True) + EPS) * w


def _ffn_fwd_down(tag, h, a, w_down, next_norm=None):
    M = h.shape[0]
    F = w_down.shape[0]
    tm = _tile(M, (1088, 544, 256))
    if next_norm is None:
        (h_out,) = _fused_matmul(
            tag + "_down", M, D_MODEL, F, [dict(a=a, b=w_down, acc=0)], [(h, 0)],
            lambda accs, ex: (ex[0] + 0.5 * accs[0],), [F32], 1, tm, D_MODEL, F, outer="j", sub=256)
        return h_out

    def with_norm(accs, ex):
        h_new = ex[0] + 0.5 * accs[0]
        return h_new, _rmsnorm_tile(h_new, ex[1])

    return _fused_matmul(tag + "_down", M, D_MODEL, F, [dict(a=a, b=w_down, acc=0, resident=True)], [(h, 0)], with_norm,
                         [F32, BF16], 1, tm, D_MODEL, F, outer="j", vecs=[next_norm])


def _ffn_bwd(tag, dh, dh_b, h, norm_w, w_gu_t, w_down, saved, scatter=False):
    n, g, u, a = saved
    M = h.shape[0]
    F = w_down.shape[0]
    tm = _tile(M, (544, 256))
    tn = _tile(F, (1408, 704, 256))

    def swiglu_bwd(accs, ex):
        da, gv, uv = 0.5 * accs[0], ex[0].astype(F32), ex[1].astype(F32)
        s = _sigmoid(gv)
        return da * uv * _dsilu(gv, s), da * gv * s

    (dgu,) = _fused_matmul(
        tag + "_dact", M, F, D_MODEL, [dict(a=dh_b, b=w_down, trans_b=True, acc=0, resident=True)], [(g, 0), (u, 0)],
        swiglu_bwd, [BF16, BF16], 1, tm, F, D_MODEL, outer="i", stack=True, sub=256)
    tr = _tile(M, (2176, 256))
    (dw_down,) = _matmul_tn(tag + "_dwd", a, dh_b, tn, D_MODEL, tr, scale=0.5)
    dw_gu_t, *p_down = _matmul_tn(tag + "_dwgu", dgu, n, tn, D_MODEL, tr,
                                  comm=("scatter", [_to_rows(dw_down)]) if scatter else None)
    comm = None
    if scatter:
        comm = ("chips", _chip_sums(tag + "_wgu", [_by_core(dw_gu_t)]))
    def norm_bwd(accs, ex):
        dh_prev, dw = _rmsnorm_bwd_tile(accs[0], ex[0], ex[2], ex[1])
        return dh_prev, dh_prev, dw

    dh_prev, dh_prev_b, dnorm, *p_gu = _fused_matmul(
        tag + "_dn", M, D_MODEL, F,
        [dict(a=dgu, a_lead=0, b=w_gu_t, acc=0, resident=True),
         dict(a=dgu, a_lead=1, b=w_gu_t, bk_off=1, acc=0, resident=True)], [(h, 0), (dh, 0)],
        norm_bwd, [F32, BF16], 1, tm, D_MODEL, F, outer="i", comm=comm, vecs=[norm_w], row_sums=1)
    return (dh_prev, dh_prev_b, dnorm, *((p_gu[0], p_down[0]) if scatter else (dw_gu_t, dw_down)))


def kernel(x, meta_tokens, ffn1_norm, ffn1_w_gu, ffn1_w_down, mix_norm, w_in, ssd_conv_w, ssd_conv_b, ssd_dt_bias, ssd_a_log, ssd_d, ssd_norm, hg_lower_bound, hg_norm, w_branch_a, w_branch_b, w_out, ffn2_norm, ffn2_w_gu, ffn2_w_down, final_norm, loss_target, m_meta_tokens, m_ffn1_norm, m_ffn1_w_gu, m_ffn1_w_down, m_mix_norm, m_w_in, m_ssd_conv_w, m_ssd_conv_b, m_ssd_dt_bias, m_ssd_a_log, m_ssd_d, m_ssd_norm, m_hg_lower_bound, m_hg_norm, m_w_branch_a, m_w_branch_b, m_w_out, m_ffn2_norm, m_ffn2_w_gu, m_ffn2_w_down, m_final_norm, v_meta_tokens, v_ffn1_norm, v_ffn1_w_gu, v_ffn1_w_down, v_mix_norm, v_w_in, v_ssd_conv_w, v_ssd_conv_b, v_ssd_dt_bias, v_ssd_a_log, v_ssd_d, v_ssd_norm, v_hg_lower_bound, v_hg_norm, v_w_branch_a, v_w_branch_b, v_w_out, v_ffn2_norm, v_ffn2_w_gu, v_ffn2_w_down, v_final_norm):
    Bl, S, D = x.shape
    T = PAD + N_META + S
    nc = T // Q
    M = Bl * T
    me = 4 * lax.axis_index("x") + 2 * lax.axis_index("y") + lax.axis_index("c")

    bf = lambda a: a[0].astype(BF16)
    bft = lambda a: a[0].T.astype(BF16)
    g_wgu1, g_meta, g_conv_w = _exchange("gather_first", "gather", [bft(ffn1_w_gu), meta_tokens, ssd_conv_w[0]])
    wgu1, meta_full, conv_w_full = _rows(g_wgu1), _cols(g_meta), _cols(g_conv_w)
    bias_p, alog_p, d_p = _heads_to_lanes(ssd_dt_bias), _heads_to_lanes(ssd_a_log), _heads_to_lanes(ssd_d)
    final_w = final_norm.reshape(1, D)

    h0 = jnp.concatenate([jnp.zeros((Bl, PAD, D), F32), jnp.broadcast_to(meta_full[None], (Bl, N_META, D)), x],
                         axis=1).reshape(M, D)
    tm = _tile(M, (1088, 544, 256))
    ffn1_saved, (g_wd1, g_win) = _ffn_fwd_gu("ffn1", h0, ffn1_norm, wgu1, comm=("gather", [bf(ffn1_w_down), bft(w_in)]))
    wd1 = _rows(g_wd1)
    win_t = _rows(g_win)
    win_dt = jnp.pad(win_t[DT_ROW:DT_ROW + SSD_HEADS], ((0, 128 - SSD_HEADS), (0, 0)))
    h1, un = _ffn_fwd_down("ffn1", h0, ffn1_saved[3], wd1, next_norm=mix_norm)
    plain = lambda accs, ex: (accs[0],)
    proj, g_wa, g_wb, g_wo = _fused_matmul(
        "in_proj", M, N_MAIN, D, [dict(a=un, b=win_t, trans_b=True, acc=0, b_shift=(DT_ROW // 1536, SSD_HEADS))], [],
        plain, [BF16], 1, tm, 1536, D,
        outer="j", comm=("gather", [bf(w_branch_a), bf(w_branch_b), bf(w_out)], "early"), sub=512)
    wa, wb, wo = _rows(g_wa), _rows(g_wb), _rows(g_wo)
    (dtr,) = _fused_matmul("in_proj_dt", M, 128, D, [dict(a=un, b=win_dt, trans_b=True, acc=0)], [], plain, [F32], 1,
                           tm, 128, D, outer="j")
    xc = _conv_fwd(proj, conv_w_full, ssd_conv_b, Bl, T)
    ya, ssd_prev = _ssd_fwd(xc, dtr, proj, bias_p, alog_p, d_p, ssd_norm, Bl, nc)
    yb, hg_o, hg_st, g_wgu2, g_wd2 = _hgrn_fwd(proj, hg_lower_bound, hg_norm, Bl, nc,
                                               comm=("gather", [bft(ffn2_w_gu), bf(ffn2_w_down)], "early"))
    wgu2, wd2 = _rows(g_wgu2), _rows(g_wd2)

    def branch_fwd(accs, ex):
        pa, pb = accs
        return pa, pb, _sigmoid(ex[0].astype(F32)) * pa + _sigmoid(ex[1].astype(F32)) * pb

    pa, pb, merged = _fused_matmul(
        "branches", M, D, D, [dict(a=ya, b=wa, acc=0), dict(a=yb, b=wb, acc=1)], [(proj, 7), (proj, 8)],
        branch_fwd, [BF16, BF16, BF16], 2, tm, D, D, outer="j", sub=256)
    def out_with_norm(accs, ex):
        h_new = ex[0] + accs[0]
        return h_new, _rmsnorm_tile(h_new, ex[1])

    h2, n2 = _fused_matmul("out_proj", M, D, D, [dict(a=merged, b=wo, acc=0)], [(h1, 0)], out_with_norm,
                           [F32, BF16], 1, tm, D, D, outer="j", vecs=[ffn2_norm])
    ffn2_saved, _ = _ffn_fwd_gu("ffn2", h2, ffn2_norm, wgu2, n=n2)
    h3 = _ffn_fwd_down("ffn2", h2, ffn2_saved[3], wd2)

    dh3, dh3_b, d_final, loss_part = _loss_head(h3, final_w, loss_target, Bl, nc)
    dh2, dh2_b, d_ffn2_norm, d_wgu2, d_wd2 = _ffn_bwd("ffn2", dh3, dh3_b, h2, ffn2_norm, wgu2, wd2, ffn2_saved)

    def branch_bwd(accs, ex):
        dm = accs[0]
        ga, gb, pav, pbv = (e.astype(F32) for e in ex)
        sa, sb = _sigmoid(ga), _sigmoid(gb)
        return (dm * sa, dm * sb,
                jnp.concatenate([dm * pav * sa * (1.0 - sa), dm * pbv * sb * (1.0 - sb)], axis=1))

    d_merged_outs = []

    def d_merged_with_swap(theirs):
        d_merged_outs.extend(_fused_matmul(
            "d_merged", M, D, D, [dict(a=dh2_b, b=wo, trans_b=True, acc=0)], [(proj, 7), (proj, 8), (pa, 0), (pb, 0)],
            branch_bwd, [BF16] * 2, 1, tm, D, D, outer="j", comm=("swap", theirs),
            wide=dict(width=2 * D, col=7 * D, total=N_MAIN, dtype=BF16)))
        return d_merged_outs[3:]

    s_ffn2 = _chip_sums("ffn2", [_by_core(d_wgu2), _by_core(d_wd2)], swap_in=d_merged_with_swap)
    dpa, dpb, dproj = d_merged_outs[:3]
    (d_wo,) = _matmul_tn("d_w_out", merged, dh2_b, 512, D, M)
    (d_wa,) = _matmul_tn("d_w_a", ya, dpa, 512, D, M)
    (d_wb,) = _matmul_tn("d_w_b", yb, dpb, 512, D, M)
    dya, dyb = _fused_matmul(
        "d_branches", M, D, D, [dict(a=dpa, b=wa, trans_b=True, acc=0), dict(a=dpb, b=wb, trans_b=True, acc=1)], [],
        lambda accs, ex: (accs[0], accs[1]), [BF16, BF16], 2, tm, D, D, outer="j")
    *ssd_grads, p_wgu2, p_wd2 = _ssd_bwd(xc, dtr, proj, bias_p, alog_p, d_p, ssd_norm, ssd_prev, dya, dproj, Bl, nc,
                                         comm=("chips", s_ffn2))
    dxc, dproj, ddtr, d_bias_p, d_alog_p, d_d_p, d_ssd_norm = ssd_grads
    dproj, d_conv_w, d_conv_b = _conv_bwd(proj, conv_w_full, ssd_conv_b, dxc, dproj, Bl, T)
    dproj, d_hb, d_hg_norm, p_wa, p_wb, p_wo = _hgrn_bwd(
        proj, hg_lower_bound, hg_norm, hg_o, hg_st, dyb, dproj, Bl, nc,
        comm=("scatter", [_to_rows(d_wa), _to_rows(d_wb), _to_rows(d_wo)]))
    ddtr_b = ddtr.astype(BF16)
    (d_win_t,) = _matmul_tn("d_w_in", dproj, un, 768, D, M, out_skip=(DT_ROW, SSD_HEADS))
    (d_win_dt,) = _matmul_tn("d_w_in_dt", ddtr_b, un, 128, D, M)
    d_win_t = lax.dynamic_update_slice(d_win_t, d_win_dt[:SSD_HEADS], (DT_ROW, 0))
    d_un_dt_outs = []

    def d_un_dt_with_swap(theirs):
        d_un_dt_outs.extend(_fused_matmul("d_un_dt", M, D, 128, [dict(a=ddtr_b, b=win_dt, acc=0)], [], plain, [F32], 1,
                                          tm, D, 128, outer="j", comm=("swap", theirs)))
        return d_un_dt_outs[1:]

    s_win = _chip_sums("w_in", [_by_core(d_win_t)], swap_in=d_un_dt_with_swap)
    def mix_norm_bwd(accs, ex):
        dh, dw = _rmsnorm_bwd_tile(accs[0] + ex[0], ex[1], ex[3], ex[2])
        return dh, dh, dw

    dh1, dh1_b, d_mix_norm, p_win = _fused_matmul(
        "d_un", M, D, N_MAIN, [dict(a=dproj, b=win_t, acc=0, b_shift=(DT_ROW // 3072, SSD_HEADS))],
        [(d_un_dt_outs[0], 0), (h1, 0), (dh2, 0)],
        mix_norm_bwd, [F32, BF16], 1, _tile(M, (544, 256)), D, 3072, outer="i", comm=("chips", s_win),
        vecs=[mix_norm], row_sums=1)
    dh0, _, d_ffn1_norm, p_wgu1, p_wd1 = _ffn_bwd("ffn1", dh1, dh1_b, h0, ffn1_norm, wgu1, wd1, ffn1_saved, scatter=True)

    dh0 = dh0.reshape(Bl, T, D)
    grad_x = dh0[:, PAD + N_META:]
    d_meta = dh0[:, PAD:PAD + N_META]

    small_grads = [d_ffn1_norm, d_mix_norm, d_conv_b, _lanes_to_heads(d_bias_p), _lanes_to_heads(d_alog_p),
                   _lanes_to_heads(d_d_p), d_ssd_norm, d_hb, d_hg_norm, d_ffn2_norm, d_final.reshape(D), d_conv_w]
    small_packed = _pack_rows(small_grads + [d_meta[b] for b in range(Bl)])
    parts = [p_wgu1, p_wd1, p_win, p_wa, p_wb, p_wo, p_wgu2, p_wd2]
    (small_all,) = _exchange("gather_small_grads", "gather", [small_packed])
    small_sum = _sum_parts("sum_small_grads", small_all)
    unpacked = _unpack_rows(small_sum, small_grads + [d_meta[b] for b in range(Bl)])
    g_small = unpacked[:len(small_grads)]
    g_meta_full = unpacked[len(small_grads)]
    for b in range(1, Bl):
        g_meta_full = g_meta_full + unpacked[len(small_grads) + b]
    g_meta = lax.dynamic_slice_in_dim(g_meta_full, me * (D // N_DEV), D // N_DEV, axis=1)
    g_conv_w = lax.dynamic_slice_in_dim(g_small[11], me * (SSD_CONV_CH // N_DEV), SSD_CONV_CH // N_DEV, axis=1)

    names = ["meta_tokens", "ffn1_norm", "ffn1_w_gu", "ffn1_w_down", "mix_norm", "w_in", "ssd_conv_w", "ssd_conv_b",
             "ssd_dt_bias", "ssd_a_log", "ssd_d", "ssd_norm", "hg_lower_bound", "hg_norm", "w_branch_a", "w_branch_b",
             "w_out", "ffn2_norm", "ffn2_w_gu", "ffn2_w_down", "final_norm"]
    W = dict(meta_tokens=meta_tokens, ffn1_norm=ffn1_norm, ffn1_w_gu=ffn1_w_gu, ffn1_w_down=ffn1_w_down, mix_norm=mix_norm,
             w_in=w_in, ssd_conv_w=ssd_conv_w, ssd_conv_b=ssd_conv_b, ssd_dt_bias=ssd_dt_bias, ssd_a_log=ssd_a_log,
             ssd_d=ssd_d, ssd_norm=ssd_norm, hg_lower_bound=hg_lower_bound, hg_norm=hg_norm, w_branch_a=w_branch_a,
             w_branch_b=w_branch_b, w_out=w_out, ffn2_norm=ffn2_norm, ffn2_w_gu=ffn2_w_gu, ffn2_w_down=ffn2_w_down,
             final_norm=final_norm)
    Mo = dict(meta_tokens=m_meta_tokens, ffn1_norm=m_ffn1_norm, ffn1_w_gu=m_ffn1_w_gu, ffn1_w_down=m_ffn1_w_down,
              mix_norm=m_mix_norm, w_in=m_w_in, ssd_conv_w=m_ssd_conv_w, ssd_conv_b=m_ssd_conv_b, ssd_dt_bias=m_ssd_dt_bias,
              ssd_a_log=m_ssd_a_log, ssd_d=m_ssd_d, ssd_norm=m_ssd_norm, hg_lower_bound=m_hg_lower_bound, hg_norm=m_hg_norm,
              w_branch_a=m_w_branch_a, w_branch_b=m_w_branch_b, w_out=m_w_out, ffn2_norm=m_ffn2_norm, ffn2_w_gu=m_ffn2_w_gu,
              ffn2_w_down=m_ffn2_w_down, final_norm=m_final_norm)
    Vo = dict(meta_tokens=v_meta_tokens, ffn1_norm=v_ffn1_norm, ffn1_w_gu=v_ffn1_w_gu, ffn1_w_down=v_ffn1_w_down,
              mix_norm=v_mix_norm, w_in=v_w_in, ssd_conv_w=v_ssd_conv_w, ssd_conv_b=v_ssd_conv_b, ssd_dt_bias=v_ssd_dt_bias,
              ssd_a_log=v_ssd_a_log, ssd_d=v_ssd_d, ssd_norm=v_ssd_norm, hg_lower_bound=v_hg_lower_bound, hg_norm=v_hg_norm,
              w_branch_a=v_w_branch_a, w_branch_b=v_w_branch_b, w_out=v_w_out, ffn2_norm=v_ffn2_norm, ffn2_w_gu=v_ffn2_w_gu,
              ffn2_w_down=v_ffn2_w_down, final_norm=v_final_norm)
    grads, deltas, new_m, new_v = {}, {}, {}, {}
    big_names = ["ffn1_w_gu", "ffn1_w_down", "w_in", "w_branch_a", "w_branch_b", "w_out", "ffn2_w_gu", "ffn2_w_down"]
    transposed = ("ffn1_w_gu", "ffn2_w_gu", "w_in")
    for nm, part in zip(big_names, parts):
        view = (lambda a: a[0].T) if nm in transposed else (lambda a: a[0])
        back = (lambda o: o.T[None]) if nm in transposed else (lambda o: o[None])
        outs = _adamw("adamw_" + nm, part, view(W[nm]), view(Mo[nm]), view(Vo[nm]))
        grads[nm], deltas[nm], new_m[nm], new_v[nm] = (back(o) for o in outs)
    small_names = ["ffn1_norm", "mix_norm", "ssd_conv_b", "ssd_dt_bias", "ssd_a_log", "ssd_d", "ssd_norm", "hg_lower_bound",
                   "hg_norm", "ffn2_norm", "final_norm", "ssd_conv_w", "meta_tokens"]
    small_g = g_small[:11] + [g_conv_w.reshape(ssd_conv_w.shape), g_meta]
    pk = lambda d: _pack_rows([d[nm] for nm in small_names])
    outs = _adamw("adamw_small", _pack_rows(small_g)[None], pk(W), pk(Mo), pk(Vo))
    like = [W[nm] for nm in small_names]
    for dst, o in zip((grads, deltas, new_m, new_v), outs):
        for nm, val in zip(small_names, _unpack_rows(o, like)):
            dst[nm] = val

    loss = lax.psum(loss_part[0, 0], MESH_AXES)
    return (loss, grad_x, *[grads[nm] for nm in names], *[deltas[nm] for nm in names],
            *[new_m[nm] for nm in names], *[new_v[nm] for nm in names])
```

```python
import functools

import jax
import jax.numpy as jnp
from jax import lax
from jax.experimental import pallas as pl
from jax.experimental.pallas import tpu as pltpu

F32, BF16 = jnp.float32, jnp.bfloat16
NN, NT, TN = ((1,), (0,)), ((1,), (1,)), ((0,), (0,))
MESH_AXES = ("x", "y", "c")
N_DEV = 8

D_MODEL = 1024
N_META = 16
EPS = 1e-6
SSD_HEADS, SSD_HEAD_DIM, SSD_GROUPS, SSD_STATE, SSD_CONV, Q = 16, 64, 4, 128, 4, 128
SSD_INNER = SSD_HEADS * SSD_HEAD_DIM
SSD_CONV_CH = SSD_INNER + 2 * SSD_GROUPS * SSD_STATE
HG_WIDTH, HG_HEADS, HG_CHUNK = 1024, 8, 16
PAD = Q - N_META
N_MAIN = 9 * 1024
ADAM_LR, ADAM_B1, ADAM_B2, ADAM_EPS, ADAM_WD, ADAM_STEP = 0.001, 0.9, 0.999, 1e-08, 0.01, 10
VMEM_LIMIT = 52 * 1024 * 1024


def _dot(a, b, dims):
    return lax.dot_general(a, b, (dims, ((), ())), preferred_element_type=F32)


def _dot01(a, b, dims, sel):
    x = b if sel == "a" else a
    hi = x.astype(BF16)
    r1 = x - hi.astype(F32)
    mid = r1.astype(BF16)
    lo = (r1 - mid.astype(F32)).astype(BF16)
    s = (a if sel == "a" else b).astype(BF16)
    parts = [_dot(s, p, dims) if sel == "a" else _dot(p, s, dims) for p in (hi, mid, lo)]
    return parts[0] + parts[1] + parts[2]


def _sigmoid(x):
    return 1.0 / (1.0 + jnp.exp(-x))


def _dsilu(x, s):
    return s * (1.0 + x * (1.0 - s))


def _softplus(x):
    e = jnp.exp(-jnp.abs(x))
    u = 1.0 + e
    log1p_e = jnp.where(u == 1.0, e, jnp.log(u) * e / (u - 1.0))
    return jnp.maximum(x, 0.0) + log1p_e


def _params(sem):
    return pltpu.CompilerParams(dimension_semantics=sem, vmem_limit_bytes=VMEM_LIMIT)


def _tile(n, prefs):
    for p in prefs:
        if n % p == 0:
            return p
    return n


CHIP_FLIPS = ((1, 0), (0, 1), (1, 1))
N_PEER = N_DEV - 1


def _comm_gather(srcs, outs, send_sems, recv_sems, local_sems):
    n = len(srcs)
    x, y, c = (lax.axis_index(a) for a in MESH_AXES)
    dev = lambda px, py, pc: 4 * px + 2 * py + pc
    me, sib = dev(x, y, c), (x, y, 1 - c)

    def rc(w, k, slot, to, src=None):
        return pltpu.make_async_remote_copy(
            src_ref=outs[w].at[slot] if src is None else src, dst_ref=outs[w].at[slot],
            send_sem=send_sems.at[w, k], recv_sem=recv_sems.at[w, k], device_id=to, device_id_type=pl.DeviceIdType.MESH)

    def local(w):
        return pltpu.make_async_copy(srcs[w], outs[w].at[me], local_sems.at[w])

    def start():
        for w in range(n):
            local(w).start()
            rc(w, 0, me, sib, src=srcs[w]).start()
            for j, (fx, fy) in enumerate(CHIP_FLIPS):
                rc(w, 1 + j, me, (x ^ fx, y ^ fy, c), src=srcs[w]).start()

    def pass_on():
        for w in range(n):
            for j, (fx, fy) in enumerate(CHIP_FLIPS):
                slot = dev(x ^ fx, y ^ fy, c)
                rc(w, 1 + j, slot, sib).wait_recv()
                rc(w, 4 + j, slot, sib).start()

    def finish():
        for w in range(n):
            rc(w, 0, dev(x, y, 1 - c), sib).wait_recv()
            rc(w, 0, me, sib, src=srcs[w]).wait_send()
            for j, (fx, fy) in enumerate(CHIP_FLIPS):
                rc(w, 4 + j, dev(x ^ fx, y ^ fy, 1 - c), sib).wait_recv()
                rc(w, 1 + j, me, sib, src=srcs[w]).wait_send()
                rc(w, 4 + j, dev(x ^ fx, y ^ fy, c), sib).wait_send()
            local(w).wait()

    return start, pass_on, finish


def _comm_scatter(srcs, outs, send_sems, recv_sems, local_sems):
    n = len(srcs)
    x, y, c = (lax.axis_index(a) for a in MESH_AXES)
    me = 4 * x + 2 * y + c

    def copies():
        out = []
        for w in range(n):
            out.append(pltpu.make_async_copy(srcs[w].at[me], outs[w].at[me], local_sems.at[w]))
            for k in range(1, N_DEV):
                px, py, pc = x ^ (k >> 2), y ^ ((k >> 1) & 1), c ^ (k & 1)
                out.append(pltpu.make_async_remote_copy(
                    src_ref=srcs[w].at[4 * px + 2 * py + pc], dst_ref=outs[w].at[me],
                    send_sem=send_sems.at[w, k - 1], recv_sem=recv_sems.at[w, k - 1],
                    device_id=(px, py, pc), device_id_type=pl.DeviceIdType.MESH))
        return out

    def start():
        for cp in copies():
            cp.start()

    def finish():
        for cp in copies():
            cp.wait()

    return start, None, finish


def _comm_swap(srcs, outs, send_sems, recv_sems, local_sems):
    x, y, c = (lax.axis_index(a) for a in MESH_AXES)

    def copies():
        return [pltpu.make_async_remote_copy(
            src_ref=srcs[w].at[1 - c], dst_ref=outs[w], send_sem=send_sems.at[w, 0], recv_sem=recv_sems.at[w, 0],
            device_id=(x, y, 1 - c), device_id_type=pl.DeviceIdType.MESH) for w in range(len(srcs))]

    def start():
        for cp in copies():
            cp.start()

    def finish():
        for cp in copies():
            cp.wait()

    return start, None, finish


def _comm_chips(srcs, outs, send_sems, recv_sems, local_sems):
    n = len(srcs)
    x, y, c = (lax.axis_index(a) for a in MESH_AXES)
    mine = 2 * x + y

    def copies():
        out = []
        for w in range(n):
            out.append(pltpu.make_async_copy(srcs[w].at[mine], outs[w].at[mine], local_sems.at[w]))
            for j, (fx, fy) in enumerate(CHIP_FLIPS):
                px, py = x ^ fx, y ^ fy
                out.append(pltpu.make_async_remote_copy(
                    src_ref=srcs[w].at[2 * px + py], dst_ref=outs[w].at[mine],
                    send_sem=send_sems.at[w, j], recv_sem=recv_sems.at[w, j],
                    device_id=(px, py, c), device_id_type=pl.DeviceIdType.MESH))
        return out

    def start():
        for cp in copies():
            cp.start()

    def finish():
        for cp in copies():
            cp.wait()

    return start, None, finish


def _comm_parts(comm):
    kind, arrays = comm[:2]
    n = len(arrays)
    lead = {"gather": lambda a: (N_DEV,) + a.shape, "scatter": lambda a: (N_DEV,) + a.shape[1:],
            "swap": lambda a: a.shape[1:], "chips": lambda a: a.shape}[kind]
    shapes = [jax.ShapeDtypeStruct(lead(a), a.dtype) for a in arrays]
    sems = [pltpu.SemaphoreType.DMA((n, N_PEER)), pltpu.SemaphoreType.DMA((n, N_PEER)), pltpu.SemaphoreType.DMA((n,))]
    make = {"gather": _comm_gather, "scatter": _comm_scatter, "swap": _comm_swap, "chips": _comm_chips}[kind]
    return n, shapes, sems, make


def _exchange(name, kind, arrays):
    n, shapes, sems, make = _comm_parts((kind, arrays))

    def body(*refs):
        start, middle, finish = make(refs[:n], refs[n:2 * n], *refs[2 * n:])
        start()
        if middle:
            middle()
        finish()

    any_spec = pl.BlockSpec(memory_space=pl.ANY)
    return pl.pallas_call(
        body, name=name, in_specs=[any_spec] * n, out_specs=[any_spec] * n, out_shape=shapes, scratch_shapes=sems,
        compiler_params=pltpu.CompilerParams(has_side_effects=True),
    )(*arrays)


def _call(body, *, name, grid, in_specs, out_specs, out_shape, scratch, sem, args, comm=None, into=None):
    any_spec = pl.BlockSpec(memory_space=pl.ANY)
    in_specs, args, aliases, n_body_in = list(in_specs), list(args), {}, len(in_specs)
    if into is not None:
        in_specs.append(any_spec)
        args.append(into[0])
        aliases = {n_body_in: into[1]}
    n_in, n_out, n_scr = len(in_specs), len(out_specs), len(scratch)
    if comm is None:
        def plain(*refs):
            body(*refs[:n_body_in], *refs[n_in:])

        return pl.pallas_call(plain, name=name, grid=grid, in_specs=in_specs, out_specs=out_specs, out_shape=out_shape,
                              scratch_shapes=scratch, input_output_aliases=aliases, compiler_params=_params(sem))(*args)
    n, shapes, sems, make = _comm_parts(comm)

    def carrier(*refs):
        ins, csrc = refs[:n_body_in], refs[n_in:n_in + n]
        outs, cout = refs[n_in + n:n_in + n + n_out], refs[n_in + n + n_out:n_in + 2 * n + n_out]
        rest = refs[n_in + 2 * n + n_out:]
        start, middle, finish = make(csrc, cout, *rest[n_scr:])
        ids = [pl.program_id(a) for a in range(len(grid))]
        step = functools.reduce(lambda acc, ig: acc * ig[1] + ig[0], zip(ids, grid), 0)
        n_steps = functools.reduce(lambda a, b: a * b, grid, 1)
        pl.when(step == 0)(start)
        body(*ins, *outs, *rest[:n_scr])
        if middle:
            early = len(comm) > 2 and comm[2] == "early"
            pl.when(step == ((3 * n_steps) // 4 if early else n_steps - 1))(middle)
        pl.when(step == n_steps - 1)(finish)

    return pl.pallas_call(
        carrier, name=name, grid=grid, in_specs=in_specs + [any_spec] * n,
        out_specs=list(out_specs) + [any_spec] * n, out_shape=list(out_shape) + shapes,
        scratch_shapes=list(scratch) + sems, input_output_aliases=aliases,
        compiler_params=pltpu.CompilerParams(dimension_semantics=("arbitrary",) * len(grid),
                                             vmem_limit_bytes=VMEM_LIMIT, has_side_effects=True),
    )(*args, *comm[1])


def _fused_matmul(name, M, N, K, pairs, extras, epilogue, out_dtypes, n_acc, tm, tn, tk, outer="i", comm=None,
                  stack=False, vecs=(), row_sums=0, wide=None, sub=None):
    nk = K // tk
    n_pairs, n_ex, n_out = len(pairs), len(extras), len(out_dtypes)
    assert not row_sums or (outer == "i" and N == tn)

    def ij(g0, g1):
        return (g0, g1) if outer == "i" else (g1, g0)

    in_specs, args = [], []
    for p in pairs:
        ao, bk, bn = p.get("a_off", 0), p.get("bk_off", 0), p.get("bn_off", 0)
        mode = dict(pipeline_mode=pl.Buffered(1)) if p.get("resident") else {}
        if "a_lead" in p:
            in_specs.append(pl.BlockSpec((None, tm, tk),
                                         lambda g0, g1, k, ao=ao, ld=p["a_lead"]: (ld, ij(g0, g1)[0], k + ao)))
        else:
            in_specs.append(pl.BlockSpec((tm, tk), lambda g0, g1, k, ao=ao: (ij(g0, g1)[0], k + ao)))
        if "b_shift" in p:
            first, shift = p["b_shift"]
            if p.get("trans_b"):
                in_specs.append(pl.BlockSpec(
                    (pl.Element(tn), pl.Element(tk)),
                    lambda g0, g1, k, bk=bk: (
                        pl.multiple_of(ij(g0, g1)[1] * tn + jnp.where(ij(g0, g1)[1] >= first, shift, 0), 16),
                        (k + bk) * tk)))
            else:
                in_specs.append(pl.BlockSpec(
                    (pl.Element(tk), pl.Element(tn)),
                    lambda g0, g1, k, bn=bn: (pl.multiple_of(k * tk + jnp.where(k >= first, shift, 0), 16),
                                              (ij(g0, g1)[1] + bn) * tn)))
        elif p.get("trans_b"):
            in_specs.append(pl.BlockSpec((tn, tk), lambda g0, g1, k, bk=bk, bn=bn: (ij(g0, g1)[1] + bn, k + bk), **mode))
        else:
            in_specs.append(pl.BlockSpec((tk, tn), lambda g0, g1, k, bk=bk, bn=bn: (k + bk, ij(g0, g1)[1] + bn), **mode))
        args += [p["a"], p["b"]]
    for arr, off in extras:
        in_specs.append(pl.BlockSpec((tm, tn), lambda g0, g1, k, off=off: (ij(g0, g1)[0], ij(g0, g1)[1] + off)))
        args.append(arr)
    for arr in vecs:
        in_specs.append(pl.BlockSpec((1, tn), lambda g0, g1, k: (0, ij(g0, g1)[1])))
        args.append(arr)
    if stack:
        out_specs = [pl.BlockSpec((n_out, tm, tn), lambda g0, g1, k: (0,) + ij(g0, g1))]
        out_shape = [jax.ShapeDtypeStruct((n_out, M, N), out_dtypes[0])]
    else:
        out_specs = [pl.BlockSpec((tm, tn), lambda g0, g1, k: ij(g0, g1)) for _ in out_dtypes]
        out_shape = [jax.ShapeDtypeStruct((M, N), dt) for dt in out_dtypes]
    if wide:
        out_specs.append(pl.BlockSpec((pl.Element(tm), pl.Element(wide["width"])),
                                      lambda g0, g1, k: (pl.multiple_of(ij(g0, g1)[0] * tm, 16), wide["col"])))
        out_shape.append(jax.ShapeDtypeStruct((M, wide["total"]), wide["dtype"]))
    n_tile_out = len(out_specs)
    out_specs += [pl.BlockSpec((1, tn), lambda g0, g1, k: (0, 0)) for _ in range(row_sums)]
    out_shape += [jax.ShapeDtypeStruct((1, N), F32) for _ in range(row_sums)]
    grid = (M // tm, N // tn, nk) if outer == "i" else (N // tn, M // tm, nk)
    n_in = 2 * n_pairs + n_ex + len(vecs)

    def partials(refs, cs=slice(None)):
        accs = [None] * n_acc
        for idx, p in enumerate(pairs):
            b_ref = refs[2 * idx + 1]
            d = (_dot(refs[2 * idx][...], b_ref[cs, :], NT) if p.get("trans_b")
                 else _dot(refs[2 * idx][...], b_ref[:, cs], NN))
            accs[p["acc"]] = d if accs[p["acc"]] is None else accs[p["acc"]] + d
        return accs

    def finish(accs, refs, first_rows, cs=slice(None)):
        res = epilogue(accs, [r[:, cs] for r in refs[2 * n_pairs:n_in]])
        if stack:
            o = refs[n_in]
            for idx in range(n_out):
                o[idx, :, cs] = res[idx].astype(o.dtype)
        else:
            for o, r in zip(refs[n_in:n_in + n_out], res):
                o[:, cs] = r.astype(o.dtype)
        if wide:
            o = refs[n_in + n_tile_out - 1]
            o[...] = res[n_out].astype(o.dtype)
        for o, r in zip(refs[n_in + n_tile_out:n_in + n_tile_out + row_sums], res[n_out + bool(wide):]):
            @pl.when(first_rows)
            def _(o=o, r=r):
                o[...] = r

            @pl.when(jnp.logical_not(first_rows))
            def _(o=o, r=r):
                o[...] += r

    if nk == 1 and sub:
        assert not wide and not row_sums and tn % sub == 0

        def body(*refs):
            for c in range(tn // sub):
                cs = slice(c * sub, (c + 1) * sub)
                finish(partials(refs, cs), refs, None, cs)
        scratch = []
    elif nk == 1:
        def body(*refs):
            finish(partials(refs), refs, pl.program_id(0) == 0)
        scratch = []
    else:
        def body(*refs):
            acc_refs = refs[-n_acc:]
            k = pl.program_id(2)
            first_rows = pl.program_id(0) == 0
            new = partials(refs)

            @pl.when(k == 0)
            def _():
                for a, v in zip(acc_refs, new):
                    a[...] = v

            @pl.when(k > 0)
            def _():
                for a, v in zip(acc_refs, new):
                    a[...] += v

            @pl.when(k == nk - 1)
            def _():
                finish([a[...] for a in acc_refs], refs, first_rows)
        scratch = [pltpu.VMEM((tm, tn), F32) for _ in range(n_acc)]

    return _call(body, name=name, grid=grid, in_specs=in_specs, out_specs=out_specs, out_shape=out_shape,
                 scratch=scratch, sem=("parallel", "parallel", "arbitrary"), args=args, comm=comm)


def _matmul_tn(name, x, y, t1, t2, tr, scale=1.0, comm=None, out_dtype=BF16, out_skip=None):
    L = x.shape[0] if x.ndim == 3 else 1
    R, K1 = x.shape[-2:]
    N1 = y.shape[1]
    nr, n1 = R // tr, K1 // t1
    if x.ndim == 3:
        x_spec = pl.BlockSpec((None, tr, t1), lambda i, j, r: (i // n1, r, i % n1))
    else:
        x_spec = pl.BlockSpec((tr, t1), lambda i, j, r: (r, i))
    rows_out = L * K1
    o_spec = pl.BlockSpec((t1, t2), lambda i, j, r: (i, j))
    if out_skip:
        row, count = out_skip
        rows_out += count
        o_spec = pl.BlockSpec(
            (pl.Element(t1), pl.Element(t2)),
            lambda i, j, r: (pl.multiple_of(i * t1 + jnp.where(i * t1 >= row, count, 0), 16), j * t2))

    def body(x_ref, y_ref, o_ref, *acc):
        d = _dot(x_ref[...], y_ref[...], TN)
        if nr == 1:
            o_ref[...] = (d * scale).astype(o_ref.dtype)
            return
        r = pl.program_id(2)

        @pl.when(r == 0)
        def _():
            acc[0][...] = d

        @pl.when(jnp.logical_and(r > 0, r < nr - 1))
        def _():
            acc[0][...] += d

        @pl.when(r == nr - 1)
        def _():
            o_ref[...] = ((acc[0][...] + d) * scale).astype(o_ref.dtype)

    return _call(
        body, name=name, grid=(L * n1, N1 // t2, nr),
        in_specs=[x_spec, pl.BlockSpec((tr, t2), lambda i, j, r: (r, j))], out_specs=[o_spec],
        out_shape=[jax.ShapeDtypeStruct((rows_out, N1), out_dtype)],
        scratch=[pltpu.VMEM((t1, t2), F32)] if nr > 1 else [],
        sem=("parallel", "parallel", "arbitrary"), args=(x, y), comm=comm)


def _embed_norm(x, meta, w, comm=None):
    Bl, S, D = x.shape
    nb = (PAD + N_META + S) // Q
    M = Bl * nb * Q

    def body(x_ref, meta_ref, w_ref, h_ref, n_ref):
        head = jnp.concatenate([jnp.zeros((PAD, D), F32), meta_ref[...]], axis=0)
        h = jnp.where(pl.program_id(1) == 0, head, x_ref[0])
        h_ref[...] = h
        n_ref[...] = _rmsnorm_tile(h, w_ref[...]).astype(n_ref.dtype)

    row = pl.BlockSpec((Q, D), lambda b, t: (b * nb + t, 0))
    return _call(
        body, name="embed_norm", grid=(Bl, nb),
        in_specs=[pl.BlockSpec((1, Q, D), lambda b, t: (b, jnp.maximum(t - 1, 0), 0)),
                  pl.BlockSpec((N_META, D), lambda b, t: (0, 0)), pl.BlockSpec((1, D), lambda b, t: (0, 0))],
        out_specs=[row, row], out_shape=[jax.ShapeDtypeStruct((M, D), F32), jax.ShapeDtypeStruct((M, D), BF16)],
        scratch=[], sem=("parallel", "parallel"), args=(x, meta, w), comm=comm)


def _rmsnorm_bwd_tile(dn, h, w, dh_in):
    r = lax.rsqrt(jnp.mean(h * h, axis=-1, keepdims=True) + EPS)
    xhat = h * r
    gw = dn * w
    dh = dh_in + r * (gw - xhat * jnp.mean(gw * xhat, axis=-1, keepdims=True))
    return dh, jnp.sum(dn * xhat, axis=0, keepdims=True)


def _loss_head(h, w, target, Bl, nb):
    M, D = h.shape

    def body(h_ref, w_ref, t_ref, dh_ref, dhb_ref, dw_ref, loss_ref):
        b, t = pl.program_id(0), pl.program_id(1)
        live = (t > 0).astype(F32)
        x = h_ref[...]
        r = lax.rsqrt(jnp.mean(x * x, axis=-1, keepdims=True) + EPS)
        xhat = x * r
        wv = w_ref[...]
        err = (xhat * wv - t_ref[0]) * live
        dy = err * (1.0 / D)
        gw = dy * wv
        dx = r * (gw - xhat * jnp.mean(gw * xhat, axis=-1, keepdims=True))
        dh_ref[...] = dx
        dhb_ref[...] = dx.astype(BF16)
        dw = jnp.sum(dy * xhat, axis=0, keepdims=True)
        part = 0.5 * jnp.sum(jnp.sum(err * err, axis=-1, keepdims=True) * (1.0 / D), axis=0, keepdims=True)
        first = jnp.logical_and(b == 0, t == 0)

        @pl.when(first)
        def _():
            dw_ref[...] = dw
            loss_ref[...] = jnp.broadcast_to(part, loss_ref.shape)

        @pl.when(jnp.logical_not(first))
        def _():
            dw_ref[...] += dw
            loss_ref[...] += jnp.broadcast_to(part, loss_ref.shape)

    row = pl.BlockSpec((Q, D), lambda b, t: (b * nb + t, 0))
    vec = pl.BlockSpec((1, D), lambda b, t: (0, 0))
    return pl.pallas_call(
        body, name="loss_head", grid=(Bl, nb),
        in_specs=[row, vec, pl.BlockSpec((1, Q, D), lambda b, t: (b, jnp.maximum(t - 1, 0), 0))],
        out_specs=[row, row, vec, pl.BlockSpec((8, 128), lambda b, t: (0, 0))],
        out_shape=[jax.ShapeDtypeStruct((M, D), F32), jax.ShapeDtypeStruct((M, D), BF16),
                   jax.ShapeDtypeStruct((1, D), F32), jax.ShapeDtypeStruct((8, 128), F32)],
        compiler_params=_params(("arbitrary", "arbitrary")),
    )(h, w, target)


CONV_TC = 256


def _conv_pre(xr_ref, w_ref, b_ref):
    x = xr_ref[...].astype(F32)
    acc = b_ref[...] + w_ref[SSD_CONV - 1:SSD_CONV, :] * x
    for k in range(1, SSD_CONV):
        acc = acc + w_ref[SSD_CONV - 1 - k:SSD_CONV - k, :] * pltpu.roll(x, k, 0)
    return x, acc


def _conv_fwd(proj, w, b, Bl, T):
    M = proj.shape[0]
    off = 1024 // CONV_TC

    def body(xr_ref, w_ref, b_ref, o_ref):
        _, acc = _conv_pre(xr_ref, w_ref, b_ref)
        row = lax.broadcasted_iota(jnp.int32, acc.shape, 0)
        o_ref[...] = jnp.where(row >= PAD, acc * _sigmoid(acc), 0.0).astype(o_ref.dtype)

    return pl.pallas_call(
        body, name="conv_fwd", grid=(Bl, SSD_CONV_CH // CONV_TC),
        in_specs=[pl.BlockSpec((T, CONV_TC), lambda bb, j: (bb, j + off)),
                  pl.BlockSpec((SSD_CONV, CONV_TC), lambda bb, j: (0, j)), pl.BlockSpec((1, CONV_TC), lambda bb, j: (0, j))],
        out_specs=pl.BlockSpec((T, CONV_TC), lambda bb, j: (bb, j)),
        out_shape=jax.ShapeDtypeStruct((M, SSD_CONV_CH), BF16), compiler_params=_params(("parallel", "parallel")),
    )(proj, w, b)


def _conv_bwd(proj, w, b, dxc, dproj, Bl, T):
    M = proj.shape[0]
    off = 1024 // CONV_TC

    def body(xr_ref, w_ref, b_ref, d_ref, dx_ref, dw_ref, db_ref):
        x, acc = _conv_pre(xr_ref, w_ref, b_ref)
        row = lax.broadcasted_iota(jnp.int32, acc.shape, 0)
        s = _sigmoid(acc)
        dpre = jnp.where(row >= PAD, d_ref[...].astype(F32) * _dsilu(acc, s), 0.0)
        dx = w_ref[SSD_CONV - 1:SSD_CONV, :] * dpre
        dws = [jnp.sum(dpre * x, axis=0, keepdims=True)]
        for k in range(1, SSD_CONV):
            dx = dx + w_ref[SSD_CONV - 1 - k:SSD_CONV - k, :] * pltpu.roll(dpre, T - k, 0)
            dws.append(jnp.sum(dpre * pltpu.roll(x, k, 0), axis=0, keepdims=True))
        dx_ref[...] = dx.astype(dx_ref.dtype)
        dw = jnp.concatenate(dws[::-1], axis=0)
        db = jnp.sum(dpre, axis=0, keepdims=True)

        @pl.when(pl.program_id(1) == 0)
        def _():
            dw_ref[...] = dw
            db_ref[...] = db

        @pl.when(pl.program_id(1) > 0)
        def _():
            dw_ref[...] += dw
            db_ref[...] += db

    return _call(
        body, name="conv_bwd", grid=(SSD_CONV_CH // CONV_TC, Bl),
        in_specs=[pl.BlockSpec((T, CONV_TC), lambda j, bb: (bb, j + off)),
                  pl.BlockSpec((SSD_CONV, CONV_TC), lambda j, bb: (0, j)), pl.BlockSpec((1, CONV_TC), lambda j, bb: (0, j)),
                  pl.BlockSpec((T, CONV_TC), lambda j, bb: (bb, j))],
        out_specs=[pl.BlockSpec((T, CONV_TC), lambda j, bb: (bb, j + off)),
                   pl.BlockSpec((SSD_CONV, CONV_TC), lambda j, bb: (0, j)), pl.BlockSpec((1, CONV_TC), lambda j, bb: (0, j))],
        out_shape=[jax.ShapeDtypeStruct(dproj.shape, BF16), jax.ShapeDtypeStruct((SSD_CONV, SSD_CONV_CH), F32),
                   jax.ShapeDtypeStruct((1, SSD_CONV_CH), F32)],
        scratch=[], sem=("parallel", "arbitrary"), args=(proj, w, b, dxc), into=(dproj, 0))


N_PAIR = SSD_HEADS // 2
HPG = SSD_HEADS // SSD_GROUPS
GW = SSD_INNER // SSD_GROUPS


def _per_group(fn, *arrs):
    return jnp.concatenate([jnp.broadcast_to(fn(*(a[:, GW * g:GW * (g + 1)] for a in arrs)), (arrs[0].shape[0], GW))
                            for g in range(SSD_GROUPS)], axis=1)


def _ssd_prep(c, dtr_ref, bias_ref, alog_ref, d_ref):
    row = lax.broadcasted_iota(jnp.int32, (Q, 128), 0)
    col = lax.broadcasted_iota(jnp.int32, (Q, 128), 1)
    live = col < SSD_HEADS
    valid = jnp.logical_and(jnp.logical_or(c > 0, row >= PAD), live)
    pre = dtr_ref[...] + bias_ref[...]
    dt = jnp.where(valid, _softplus(pre), 0.0)
    A = jnp.where(live[0:1], -jnp.exp(alog_ref[...]), 0.0)
    tri = row >= col
    eye = (row == col).astype(BF16)
    cs = _dot01(tri, dt * A, NN, "a")
    cst = _dot01(eye, cs, NT, "a")
    spread = (lax.broadcasted_iota(jnp.int32, (128, SSD_INNER), 0)
              == lax.broadcasted_iota(jnp.int32, (128, SSD_INNER), 1) // SSD_HEAD_DIM).astype(BF16)
    dt_w = _dot01(dt, spread, NN, "b")
    cs_w = _dot01(cs, spread, NN, "b")
    d_w = _dot01(jnp.broadcast_to(d_ref[...], (8, 128)), spread, NN, "b")[0:1]
    lane = lax.broadcasted_iota(jnp.int32, (Q, SSD_INNER), 1)
    first = (lane % 128) < SSD_HEAD_DIM
    return dict(row=row, col=col, valid=valid, pre=pre, dt=dt, A=A, tri=tri, eye=eye, cs=cs, cst=cst, spread=spread,
                dt_w=dt_w, cs_w=cs_w, d_w=d_w, ecs_w=jnp.exp(cs_w), decay_w=jnp.exp(cs_w[Q - 1:Q] - cs_w), first=first)


def _ssd_chunk(xc_ref, s, states):
    xv = xc_ref[:, 0:SSD_INNER].astype(F32)
    Bs = [xc_ref[:, SSD_INNER + 128 * g:SSD_INNER + 128 * (g + 1)] for g in range(SSD_GROUPS)]
    Cs = [xc_ref[:, SSD_INNER + 512 + 128 * g:SSD_INNER + 512 + 128 * (g + 1)] for g in range(SSD_GROUPS)]
    X = xv * s["dt_w"]
    X0 = jnp.where(s["first"], X, 0.0)
    Xb = (X0.astype(BF16), (X - X0).astype(BF16))
    Xd = (X * s["decay_w"]).astype(BF16)
    CB = [_dot(Cs[g], Bs[g], NT) for g in range(SSD_GROUPS)]
    Lms = [jnp.exp(jnp.where(s["tri"], s["cs"][:, h:h + 1] - s["cst"][h:h + 1, :], -jnp.inf)) for h in range(SSD_HEADS)]
    Ms = [CB[h // HPG] * Lms[h] for h in range(SSD_HEADS)]
    Mb = [m.astype(BF16) for m in Ms]
    prev_b = [st.astype(BF16) for st in states]
    yds, yos, sts = [], [], []
    for p in range(N_PAIR):
        g, ln = p // 2, slice(128 * p, 128 * (p + 1))
        yds.append(_dot(Mb[2 * p], Xb[0][:, ln], NN) + _dot(Mb[2 * p + 1], Xb[1][:, ln], NN))
        yos.append(_dot(Cs[g], prev_b[p], NT))
        sts.append(_dot(Xd[:, ln], Bs[g], TN))
    yo = jnp.concatenate(yos, axis=1)
    y = jnp.concatenate(yds, axis=1) + yo * s["ecs_w"] + xv * s["d_w"]
    upper = s["row"] < SSD_HEAD_DIM
    cl = s["cs"][Q - 1:Q, :]
    ecl_rows = [jnp.where(upper, jnp.exp(cl[:, 2 * p:2 * p + 1]), jnp.exp(cl[:, 2 * p + 1:2 * p + 2])) for p in range(N_PAIR)]
    new_states = [states[p] * ecl_rows[p] + sts[p] for p in range(N_PAIR)]
    return y, new_states, dict(xv=xv, Bs=Bs, Cs=Cs, X=X, Xb=Xb, CB=CB, Lms=Lms, Ms=Ms, Mb=Mb, prev_b=prev_b, yo=yo,
                               ecl_rows=ecl_rows)


def _ssd_in_specs(nc, rev=False):
    rb = (lambda b, c: b * nc + nc - 1 - c) if rev else (lambda b, c: b * nc + c)
    vec = pl.BlockSpec((1, 128), lambda b, c: (0, 0))
    return [pl.BlockSpec((Q, SSD_CONV_CH), lambda b, c: (rb(b, c), 0)),
            pl.BlockSpec((Q, 128), lambda b, c: (rb(b, c), 0)),
            pl.BlockSpec((Q, SSD_INNER), lambda b, c: (rb(b, c), 0)),
            vec, vec, vec, pl.BlockSpec((1, SSD_INNER), lambda b, c: (0, 0))]


def _ssd_fwd(xc, dtr, proj, bias_p, alog_p, d_p, nw, Bl, nc):
    M = xc.shape[0]

    def body(xc_ref, dtr_ref, z_ref, bias_ref, alog_ref, d_ref, nw_ref, y_ref, prev_ref, state):
        c = pl.program_id(1)

        @pl.when(c == 0)
        def _():
            state[...] = jnp.zeros_like(state)

        s = _ssd_prep(c, dtr_ref, bias_ref, alog_ref, d_ref)
        states = [state[p] for p in range(N_PAIR)]
        y, new_states, _ = _ssd_chunk(xc_ref, s, states)
        for p in range(N_PAIR):
            prev_ref[0, 0, p] = states[p]
            state[p] = new_states[p]
        zz = z_ref[...].astype(F32)
        yg = y * zz * _sigmoid(zz)
        r = _per_group(lambda a: lax.rsqrt(jnp.mean(a * a, axis=-1, keepdims=True) + EPS), yg)
        y_ref[...] = (yg * r * nw_ref[...]).astype(y_ref.dtype)

    return pl.pallas_call(
        body, name="ssd_fwd", grid=(Bl, nc), in_specs=_ssd_in_specs(nc),
        out_specs=[pl.BlockSpec((Q, SSD_INNER), lambda b, c: (b * nc + c, 0)),
                   pl.BlockSpec((1, 1, N_PAIR, 128, 128), lambda b, c: (b, c, 0, 0, 0))],
        out_shape=[jax.ShapeDtypeStruct((M, SSD_INNER), BF16), jax.ShapeDtypeStruct((Bl, nc, N_PAIR, 128, 128), F32)],
        scratch_shapes=[pltpu.VMEM((N_PAIR, 128, 128), F32)],
        compiler_params=_params(("arbitrary", "arbitrary")),
    )(xc, dtr, proj, bias_p, alog_p, d_p, nw)


def _ssd_bwd(xc, dtr, proj, bias_p, alog_p, d_p, nw, prev, dya, dproj, Bl, nc, comm=None):
    M = xc.shape[0]

    def body(xc_ref, dtr_ref, z_ref, bias_ref, alog_ref, d_ref, nw_ref, prev_ref, dy_ref,
             dxc_ref, dz_ref, ddtr_ref, dbias_ref, dalog_ref, dd_ref, dnw_ref, dS):
        b, t = pl.program_id(0), pl.program_id(1)

        @pl.when(t == 0)
        def _():
            dS[...] = jnp.zeros_like(dS)

        s = _ssd_prep(nc - 1 - t, dtr_ref, bias_ref, alog_ref, d_ref)
        states = [prev_ref[0, 0, p] for p in range(N_PAIR)]
        y, _, k = _ssd_chunk(xc_ref, s, states)
        xv, Bs, Cs, Xb = k["xv"], k["Bs"], k["Cs"], k["Xb"]

        zz = z_ref[...].astype(F32)
        sz = _sigmoid(zz)
        silu_z = zz * sz
        yg = y * silu_z
        r = _per_group(lambda a: lax.rsqrt(jnp.mean(a * a, axis=-1, keepdims=True) + EPS), yg)
        xhat = yg * r
        dout = dy_ref[...].astype(F32)
        gw = dout * nw_ref[...]
        dyg = r * (gw - xhat * _per_group(lambda a, c2: jnp.mean(a * c2, axis=-1, keepdims=True), gw, xhat))
        dnw = jnp.sum(dout * xhat, axis=0, keepdims=True)
        dz_ref[...] = (dyg * y * _dsilu(zz, sz)).astype(dz_ref.dtype)
        dy = dyg * silu_z
        dy0 = jnp.where(s["first"], dy, 0.0)
        dyb = (dy0.astype(BF16), (dy - dy0).astype(BF16))
        dYo = (dy * s["ecs_w"]).astype(BF16)

        dS_f = [dS[p] for p in range(N_PAIR)]
        dS_b = [d.astype(BF16) for d in dS_f]
        BdS, dXm, dprev, dCs, dMs, XdS = [], [], [], [[] for _ in range(SSD_GROUPS)], [], []
        for p in range(N_PAIR):
            g, ln = p // 2, slice(128 * p, 128 * (p + 1))
            BdS.append(_dot(Bs[g], dS_b[p], NT))
            dXm.append(_dot(k["Mb"][2 * p], dyb[0][:, ln], TN) + _dot(k["Mb"][2 * p + 1], dyb[1][:, ln], TN))
            dprev.append(_dot(dYo[:, ln], Cs[g], TN))
            dCs[g].append(_dot(dYo[:, ln], k["prev_b"][p], NN))
            for hh in range(2):
                dMs.append(_dot(dyb[hh][:, ln], Xb[hh][:, ln], NT))
                XdS.append(_dot(Xb[hh][:, ln], dS_b[p], NN))
        dX = jnp.concatenate(dXm, axis=1) + s["decay_w"] * jnp.concatenate(BdS, axis=1)
        dxs = dy * s["d_w"] + dX * s["dt_w"]

        sums = _dot01(jnp.concatenate([dX * xv, dy * k["yo"] * s["ecs_w"], dy * xv], axis=0), s["spread"], NT, "b")
        ddt, dcs = sums[0:Q], sums[Q:2 * Q]
        dD = jnp.sum(sums[2 * Q:3 * Q], axis=0, keepdims=True)

        col, row = s["col"], s["row"]
        lane1 = col[0:1]
        rowsT = lax.broadcasted_iota(jnp.int32, (128, Q), 0)
        dcs_t = jnp.zeros((128, Q), F32)
        dcl = jnp.zeros((1, 128), F32)
        dB_out, dC_out = [], []
        for g in range(SSD_GROUPS):
            Bf = Bs[g].astype(F32)
            dCB = jnp.zeros((Q, Q), F32)
            dBacc = jnp.zeros((Q, 128), F32)
            for r4 in range(HPG):
                h = HPG * g + r4
                p, hh = h // 2, h % 2
                W = dMs[h] * k["Ms"][h]
                dCB = dCB + dMs[h] * k["Lms"][h]
                decay_h = s["decay_w"][:, SSD_HEAD_DIM * h:SSD_HEAD_DIM * h + 1]
                dBacc = dBacc + decay_h * XdS[h]
                tdec = jnp.sum(XdS[h] * Bf, axis=1, keepdims=True) * decay_h
                dcs = dcs + jnp.where(col == h, jnp.sum(W, axis=1, keepdims=True) - tdec, 0.0)
                dcs_t = dcs_t - jnp.where(rowsT == h, jnp.sum(W, axis=0, keepdims=True), 0.0)
                rows_h = (row < SSD_HEAD_DIM) if hh == 0 else (row >= SSD_HEAD_DIM)
                sprev = jnp.sum(jnp.sum(jnp.where(rows_h, dS_f[p] * states[p], 0.0), axis=1, keepdims=True),
                                axis=0, keepdims=True)
                ecl = jnp.exp(s["cs"][Q - 1:Q, h:h + 1])
                dcl = dcl + jnp.where(lane1 == h, jnp.sum(tdec, axis=0, keepdims=True) + ecl * sprev, 0.0)
            dCB_b = dCB.astype(BF16)
            dC_out.append(dCs[g][0] + dCs[g][1] + _dot(dCB_b, Bs[g], NN))
            dB_out.append(dBacc + _dot(dCB_b, Cs[g], TN))
        for p in range(N_PAIR):
            dS[p] = dS_f[p] * k["ecl_rows"][p] + dprev[p]
        dxc_ref[...] = jnp.concatenate([dxs] + dB_out + dC_out, axis=1).astype(dxc_ref.dtype)

        dcs = dcs + _dot01(s["eye"], dcs_t, NT, "a") + jnp.where(row == Q - 1, dcl, 0.0)
        da = _dot01(row <= col, dcs, NN, "a")
        ddt = ddt + da * s["A"]
        dpre = jnp.where(s["valid"], ddt * _sigmoid(s["pre"]), 0.0)
        ddtr_ref[...] = dpre
        dbias = jnp.sum(dpre, axis=0, keepdims=True)
        dalog = jnp.sum(da * s["dt"], axis=0, keepdims=True) * s["A"]
        first_step = jnp.logical_and(b == 0, t == 0)

        @pl.when(first_step)
        def _():
            dbias_ref[...] = dbias
            dalog_ref[...] = dalog
            dd_ref[...] = dD
            dnw_ref[...] = dnw

        @pl.when(jnp.logical_not(first_step))
        def _():
            dbias_ref[...] += dbias
            dalog_ref[...] += dalog
            dd_ref[...] += dD
            dnw_ref[...] += dnw

    rb = lambda b, c: b * nc + nc - 1 - c
    rowblk = lambda w: pl.BlockSpec((Q, w), lambda b, c: (rb(b, c), 0))
    vec = lambda w: pl.BlockSpec((1, w), lambda b, c: (0, 0))
    return _call(
        body, name="ssd_bwd", grid=(Bl, nc),
        in_specs=_ssd_in_specs(nc, rev=True) + [
            pl.BlockSpec((1, 1, N_PAIR, 128, 128), lambda b, c: (b, nc - 1 - c, 0, 0, 0)), rowblk(SSD_INNER)],
        out_specs=[rowblk(SSD_CONV_CH), rowblk(SSD_INNER), rowblk(128), vec(128), vec(128), vec(128), vec(SSD_INNER)],
        out_shape=[jax.ShapeDtypeStruct((M, SSD_CONV_CH), BF16), jax.ShapeDtypeStruct(dproj.shape, BF16),
                   jax.ShapeDtypeStruct((M, 128), F32), jax.ShapeDtypeStruct((1, 128), F32),
                   jax.ShapeDtypeStruct((1, 128), F32), jax.ShapeDtypeStruct((1, 128), F32),
                   jax.ShapeDtypeStruct((1, SSD_INNER), F32)],
        scratch=[pltpu.VMEM((N_PAIR, 128, 128), F32)], sem=("arbitrary", "arbitrary"),
        args=(xc, dtr, proj, bias_p, alog_p, d_p, nw, prev, dya), comm=comm, into=(dproj, 1))


NSUB = Q // HG_CHUNK
HG_HP = 8
EXP_CAP = 80.0


def _hg_setup(blk, q_ref, f_ref, hb_ref):
    row = lax.broadcasted_iota(jnp.int32, (Q, Q), 0)
    col = lax.broadcasted_iota(jnp.int32, (Q, Q), 1)
    same = (row // HG_CHUNK) == (col // HG_CHUNK)
    causal = jnp.logical_and(same, col <= row)
    lb = _sigmoid(hb_ref[0:1, :] - hb_ref[1:2, :])
    fl = f_ref[...].astype(F32)
    sg = _sigmoid(fl)
    fg = lb + (1.0 - lb) * sg
    k = (1.0 - lb) * (1.0 - sg)
    gl = jnp.log(fg)
    G = _dot01(causal, gl, NN, "a")
    T = _dot01(same, gl, NN, "a")
    qv = q_ref[...].astype(F32)
    sq = _sigmoid(qv)
    eG = jnp.exp(G)
    eGn = jnp.exp(jnp.minimum(-G, EXP_CAP))
    eTG = jnp.exp(T - G)
    qt = qv * sq * eG
    kt = k * eGn
    kh = k * eTG
    valid = jnp.logical_or(blk > 0, row[:, :1] >= PAD)
    return dict(row=row, col=col, same=same, causal=causal, lb=lb, sg=sg, fg=fg, k=k, T=T, qv=qv, sq=sq,
                eG=eG, eGn=eGn, eTG=eTG, qt=qt, kt=kt, kh=kh, valid=valid)


def _hg_specs(nb, rev=False):
    rb = (lambda h, b, t: b * nb + nb - 1 - t) if rev else (lambda h, b, t: b * nb + t)
    w = 128 * HG_HP
    blk = lambda off: pl.BlockSpec((Q, w), lambda h, b, t, off=off: (rb(h, b, t), off // HG_HP + h))
    return [blk(24), blk(32), blk(40), blk(48),
            pl.BlockSpec((2, w), lambda h, b, t: (0, h)), pl.BlockSpec((1, w), lambda h, b, t: (0, h))]


HEAD_LANES = tuple(slice(128 * hh, 128 * (hh + 1)) for hh in range(HG_HP))


def _per_head(fn, *arrs):
    return jnp.concatenate([jnp.broadcast_to(fn(*(a[:, ln] for a in arrs)), (arrs[0].shape[0], 128))
                            for ln in HEAD_LANES], axis=1)


def _hgrn_fwd(proj, hb, nw, Bl, nb, comm=None):
    M = proj.shape[0]

    def body(q_ref, f_ref, i_ref, g_ref, hb_ref, nw_ref, y_ref, o_ref, st_ref, S):
        blk = pl.program_id(2)

        @pl.when(blk == 0)
        def _():
            S[...] = jnp.zeros_like(S)

        s = _hg_setup(blk, q_ref, f_ref, hb_ref)
        v = i_ref[...]
        qt_b, kt_b, kh_b = s["qt"].astype(BF16), s["kt"].astype(BF16), s["kh"].astype(BF16)
        eT = jnp.exp(s["T"])
        att = [jnp.where(s["causal"], _dot(qt_b[:, ln], kt_b[:, ln], NT), 0.0).astype(BF16) for ln in HEAD_LANES]
        o_intra = [_dot(att[hh], v[:, ln], NN) for hh, ln in enumerate(HEAD_LANES)]
        for j in range(NSUB):
            sl = slice(HG_CHUNK * j, HG_CHUNK * (j + 1))
            for hh, ln in enumerate(HEAD_LANES):
                St = S[hh]
                st_ref[0, hh, 0, j] = St
                o_ref[sl, ln] = o_intra[hh][sl] + _dot(qt_b[sl, ln], St.astype(BF16), NT)
                S[hh] = St * eT[HG_CHUNK * j:HG_CHUNK * j + 1, ln] + _dot(v[sl, ln], kh_b[sl, ln], TN)
        o = o_ref[...]
        r = _per_head(lambda a: lax.rsqrt(jnp.mean(a * a, axis=-1, keepdims=True) + EPS), o)
        gv = g_ref[...].astype(F32)
        y_ref[...] = (o * r * nw_ref[...] * gv * _sigmoid(gv)).astype(y_ref.dtype)

    rowblk = pl.BlockSpec((Q, 128 * HG_HP), lambda h, b, t: (b * nb + t, h))
    return _call(
        body, name="hgrn_fwd", grid=(HG_HEADS // HG_HP, Bl, nb), in_specs=_hg_specs(nb),
        out_specs=[rowblk, rowblk,
                   pl.BlockSpec((1, HG_HP, 1, NSUB, 128, 128), lambda h, b, t: (b, h, t, 0, 0, 0))],
        out_shape=[jax.ShapeDtypeStruct((M, HG_WIDTH), BF16), jax.ShapeDtypeStruct((M, HG_WIDTH), F32),
                   jax.ShapeDtypeStruct((Bl, HG_HEADS, nb, NSUB, 128, 128), F32)],
        scratch=[pltpu.VMEM((HG_HP, 128, 128), F32)], sem=("parallel", "arbitrary", "arbitrary"),
        args=(proj, proj, proj, proj, hb, nw), comm=comm)


def _hgrn_bwd(proj, hb, nw, o_saved, st_saved, dyb, dproj, Bl, nb, comm=None):
    assert HG_HP == HG_HEADS

    def body(q_ref, f_ref, i_ref, g_ref, hb_ref, nw_ref, o_ref, st_ref, dy_ref,
             d_ref, dhb_ref, dnw_ref, dS, a_dqt, a_dv, a_dkh, a_dgl):
        b, t = pl.program_id(1), pl.program_id(2)

        @pl.when(t == 0)
        def _():
            dS[...] = jnp.zeros_like(dS)

        first_step = jnp.logical_and(b == 0, t == 0)
        s = _hg_setup(nb - 1 - t, q_ref, f_ref, hb_ref)
        v = i_ref[...]
        qt_b, kt_b, kh_b = s["qt"].astype(BF16), s["kt"].astype(BF16), s["kh"].astype(BF16)
        eT = jnp.exp(s["T"])
        att = [jnp.where(s["causal"], _dot(qt_b[:, ln], kt_b[:, ln], NT), 0.0).astype(BF16) for ln in HEAD_LANES]

        o = o_ref[...]
        r = _per_head(lambda a: lax.rsqrt(jnp.mean(a * a, axis=-1, keepdims=True) + EPS), o)
        xhat = o * r
        gv = g_ref[...].astype(F32)
        sgv = _sigmoid(gv)
        dyv = dy_ref[...].astype(F32)
        d_on = dyv * gv * sgv
        dg_out = dyv * xhat * nw_ref[...] * _dsilu(gv, sgv)
        gw = d_on * nw_ref[...]
        do = r * (gw - xhat * _per_head(lambda a, c: jnp.mean(a * c, axis=-1, keepdims=True), gw, xhat))
        dnw = jnp.sum(d_on * xhat, axis=0, keepdims=True)
        do_b = do.astype(BF16)

        datt = [jnp.where(s["causal"], _dot(do_b[:, ln], v[:, ln], NT), 0.0).astype(BF16) for ln in HEAD_LANES]
        dqt = jnp.concatenate([_dot(datt[hh], kt_b[:, ln], NN) for hh, ln in enumerate(HEAD_LANES)], axis=1)
        dkt = jnp.concatenate([_dot(datt[hh], qt_b[:, ln], TN) for hh, ln in enumerate(HEAD_LANES)], axis=1)
        dv = jnp.concatenate([_dot(att[hh], do_b[:, ln], TN) for hh, ln in enumerate(HEAD_LANES)], axis=1)
        last_row = (lax.broadcasted_iota(jnp.int32, (HG_CHUNK, 128), 0) == HG_CHUNK - 1)
        for j in reversed(range(NSUB)):
            sl = slice(HG_CHUNK * j, HG_CHUNK * (j + 1))
            for hh, ln in enumerate(HEAD_LANES):
                St = st_ref[0, hh, 0, j]
                dSt = dS[hh]
                St_b, dSt_b = St.astype(BF16), dSt.astype(BF16)
                eT_j = eT[HG_CHUNK * j:HG_CHUNK * j + 1, ln]
                dkh_j = _dot(v[sl, ln], dSt_b, NN)
                a_dqt[sl, ln] = _dot(do_b[sl, ln], St_b, NN)
                a_dv[sl, ln] = _dot(kh_b[sl, ln], dSt_b, NT)
                a_dkh[sl, ln] = dkh_j
                dlast = (jnp.sum(St * dSt, axis=0, keepdims=True) * eT_j
                         + jnp.sum(dkh_j * s["kh"][sl, ln], axis=0, keepdims=True))
                a_dgl[sl, ln] = jnp.where(last_row, dlast, 0.0)
                dS[hh] = dSt * eT_j + _dot(do_b[sl, ln], qt_b[sl, ln], TN)
        dqt = dqt + a_dqt[...]
        dv = dv + a_dv[...]
        dkh = a_dkh[...]
        dG = dqt * s["qt"] - dkt * s["kt"] - dkh * s["kh"] + a_dgl[...]
        rev_causal = jnp.logical_and(s["same"], s["col"] >= s["row"])
        dgl = _dot01(rev_causal, dG, NN, "a")
        dk = dkt * s["eGn"] + dkh * s["eTG"]
        dfg = dgl / s["fg"] - dk
        lb, sg = s["lb"], s["sg"]
        keep = s["valid"].astype(F32)
        d_ref[:, 0:w] = (dqt * s["eG"] * _dsilu(s["qv"], s["sq"]) * keep).astype(d_ref.dtype)
        d_ref[:, w:2 * w] = (dfg * (1.0 - lb) * sg * (1.0 - sg) * keep).astype(d_ref.dtype)
        d_ref[:, 2 * w:3 * w] = (dv * keep).astype(d_ref.dtype)
        d_ref[:, 3 * w:4 * w] = (dg_out * keep).astype(d_ref.dtype)
        dlb = jnp.sum(dfg * (1.0 - sg) * keep, axis=0, keepdims=True) * lb * (1.0 - lb)
        dhb = jnp.concatenate([dlb, -dlb], axis=0)

        @pl.when(first_step)
        def _():
            dhb_ref[...] = dhb
            dnw_ref[...] = dnw

        @pl.when(jnp.logical_not(first_step))
        def _():
            dhb_ref[...] += dhb
            dnw_ref[...] += dnw

    w = 128 * HG_HP
    rowblk = pl.BlockSpec((Q, w), lambda h, b, t: (b * nb + nb - 1 - t, h))
    return _call(
        body, name="hgrn_bwd", grid=(HG_HEADS // HG_HP, Bl, nb),
        in_specs=_hg_specs(nb, rev=True) + [
            rowblk, pl.BlockSpec((1, HG_HP, 1, NSUB, 128, 128), lambda h, b, t: (b, h, nb - 1 - t, 0, 0, 0)), rowblk],
        out_specs=[pl.BlockSpec((pl.Element(Q), pl.Element(4 * w)),
                                lambda h, b, t: (pl.multiple_of((b * nb + nb - 1 - t) * Q, Q), 3 * HG_WIDTH)),
                   pl.BlockSpec((2, w), lambda h, b, t: (0, h)), pl.BlockSpec((1, w), lambda h, b, t: (0, h))],
        out_shape=[jax.ShapeDtypeStruct(dproj.shape, BF16),
                   jax.ShapeDtypeStruct((2, HG_WIDTH), F32), jax.ShapeDtypeStruct((1, HG_WIDTH), F32)],
        scratch=[pltpu.VMEM((HG_HP, 128, 128), F32)] + [pltpu.VMEM((Q, w), F32)] * 4,
        sem=("parallel", "arbitrary", "arbitrary"),
        args=(proj, proj, proj, proj, hb, nw, o_saved, st_saved, dyb), comm=comm, into=(dproj, 0))


def _adamw(name, parts, w, m, v):
    R, C = w.shape
    S = parts.shape[0]
    tr, tc = (_tile(R, (256, 176, 128, 64, 8)), C) if R % 8 == 0 else (R, 256)
    c1, c2 = 1.0 - ADAM_B1 ** ADAM_STEP, 1.0 - ADAM_B2 ** ADAM_STEP

    def body(p_ref, w_ref, m_ref, v_ref, g_ref, d_ref, nm_ref, nv_ref):
        g = p_ref[0].astype(F32)
        for s in range(1, S):
            g = g + p_ref[s].astype(F32)
        nm = ADAM_B1 * m_ref[...] + (1.0 - ADAM_B1) * g
        nv = ADAM_B2 * v_ref[...] + (1.0 - ADAM_B2) * (g * g)
        g_ref[...] = g
        nm_ref[...] = nm
        nv_ref[...] = nv
        d_ref[...] = -ADAM_LR * ((nm / c1) / (jnp.sqrt(nv / c2) + ADAM_EPS) + ADAM_WD * w_ref[...])

    blk = pl.BlockSpec((tr, tc), lambda i, j: (i, j))
    return pl.pallas_call(
        body, name=name, grid=(R // tr, C // tc),
        in_specs=[pl.BlockSpec((S, tr, tc), lambda i, j: (0, i, j)), blk, blk, blk], out_specs=[blk] * 4,
        out_shape=[jax.ShapeDtypeStruct((R, C), F32)] * 4, compiler_params=_params(("parallel", "parallel")),
    )(parts, w, m, v)


def _pair_sum(name, by_core, arrived):
    _, J, R, C = by_core.shape
    tc = _tile(C, (512, 256, 128))

    def body(c_ref, a_ref, b_ref, o_ref):
        o_ref[...] = (a_ref[0].astype(F32) + b_ref[...].astype(F32)).astype(o_ref.dtype)

    blk = pl.BlockSpec((1, R, tc), lambda j, k, c_ref: (j, 0, k))
    return pl.pallas_call(
        body, name=name,
        grid_spec=pltpu.PrefetchScalarGridSpec(
            num_scalar_prefetch=1, grid=(J, C // tc),
            in_specs=[pl.BlockSpec((1, 1, R, tc), lambda j, k, c_ref: (c_ref[0], j, 0, k)), blk], out_specs=blk),
        out_shape=jax.ShapeDtypeStruct(arrived.shape, arrived.dtype), compiler_params=_params(("parallel", "parallel")),
    )(lax.axis_index("c").astype(jnp.int32).reshape(1), by_core, arrived)


def _sum_parts(name, parts):
    S, R, C = parts.shape

    def body(p_ref, o_ref):
        g = p_ref[0]
        for s in range(1, S):
            g = g + p_ref[s]
        o_ref[...] = g

    return pl.pallas_call(
        body, name=name, out_shape=jax.ShapeDtypeStruct((R, C), F32),
        in_specs=[pl.BlockSpec(memory_space=pltpu.VMEM)], out_specs=pl.BlockSpec(memory_space=pltpu.VMEM),
    )(parts)


def _heads_to_lanes(p):
    return jnp.pad(p, [(0, 0)] * (p.ndim - 1) + [(0, 128 - SSD_HEADS)])


def _lanes_to_heads(p):
    return p[..., :SSD_HEADS]


def _pack_rows(arrs):
    rows = []
    for a in arrs:
        f = a.reshape(-1).astype(F32)
        n = -(-f.shape[0] // D_MODEL) * D_MODEL
        rows.append(jnp.pad(f, (0, n - f.shape[0])).reshape(-1, D_MODEL))
    out = jnp.concatenate(rows, axis=0)
    return jnp.pad(out, ((0, (-out.shape[0]) % 8), (0, 0)))


def _unpack_rows(packed, like):
    outs, r = [], 0
    for a in like:
        n = 1
        for s in a.shape:
            n *= s
        nr = -(-n // D_MODEL)
        outs.append(packed[r:r + nr].reshape(-1)[:n].reshape(a.shape))
        r += nr
    return outs


def _cols(gth):
    return jnp.transpose(gth, (1, 0, 2)).reshape(gth.shape[1], -1)


def _rows(gth):
    return gth.reshape(-1, gth.shape[2])


def _to_rows(g):
    return g.reshape(N_DEV, -1, g.shape[1]).astype(BF16)


def _by_core(g):
    return jnp.transpose(g.reshape(N_DEV // 2, 2, -1, g.shape[1]), (1, 0, 2, 3)).astype(BF16)


DT_ROW = 3072


def _chip_sums(tag, by_core, swap_in=None):
    arrived = swap_in(by_core) if swap_in else _exchange(tag + "_swap", "swap", by_core)
    return [_pair_sum(f"{tag}_chipsum{i}", m, a) for i, (m, a) in enumerate(zip(by_core, arrived))]


def _ffn_fwd_gu(tag, n, w_gu_t, comm=None):
    M = n.shape[0]
    F = w_gu_t.shape[0] // 2
    tm = _tile(M, (544, 256))
    outs = _fused_matmul(
        tag + "_gu", M, F, D_MODEL,
        [dict(a=n, b=w_gu_t, trans_b=True, acc=0, resident=True),
         dict(a=n, b=w_gu_t, trans_b=True, bn_off=1, acc=1, resident=True)], [],
        lambda accs, ex: (accs[0], accs[1], accs[0] * _sigmoid(accs[0]) * accs[1]),
        [BF16, BF16, BF16], 2, tm, F, D_MODEL, outer="i", comm=comm, sub=256)
    return (n, *outs[:3]), outs[3:]


def _rmsnorm_tile(x, w):
    return x * lax.rsqrt(jnp.mean(x * x, axis=-1, keepdims=True) + EPS) * w


def _ffn_fwd_down(tag, h, a, w_down, next_norm=None):
    M = h.shape[0]
    F = w_down.shape[0]
    tm = _tile(M, (1088, 544, 256))
    if next_norm is None:
        (h_out,) = _fused_matmul(
            tag + "_down", M, D_MODEL, F, [dict(a=a, b=w_down, acc=0)], [(h, 0)],
            lambda accs, ex: (ex[0] + 0.5 * accs[0],), [F32], 1, tm, D_MODEL, F, outer="j", sub=256)
        return h_out

    def with_norm(accs, ex):
        h_new = ex[0] + 0.5 * accs[0]
        return h_new, _rmsnorm_tile(h_new, ex[1])

    return _fused_matmul(tag + "_down", M, D_MODEL, F, [dict(a=a, b=w_down, acc=0, resident=True)], [(h, 0)], with_norm,
                         [F32, BF16], 1, tm, D_MODEL, F, outer="j", vecs=[next_norm])


def _ffn_bwd(tag, dh, dh_b, h, norm_w, w_gu_t, w_down, saved, scatter=False):
    n, g, u, a = saved
    M = h.shape[0]
    F = w_down.shape[0]
    tm = _tile(M, (544, 256))
    tn = _tile(F, (1408, 704, 256))

    def swiglu_bwd(accs, ex):
        da, gv, uv = 0.5 * accs[0], ex[0].astype(F32), ex[1].astype(F32)
        s = _sigmoid(gv)
        return da * uv * _dsilu(gv, s), da * gv * s

    (dgu,) = _fused_matmul(
        tag + "_dact", M, F, D_MODEL, [dict(a=dh_b, b=w_down, trans_b=True, acc=0, resident=True)], [(g, 0), (u, 0)],
        swiglu_bwd, [BF16, BF16], 1, tm, F, D_MODEL, outer="i", stack=True, sub=256)
    tr = _tile(M, (2176, 256))
    (dw_down,) = _matmul_tn(tag + "_dwd", a, dh_b, tn, D_MODEL, tr, scale=0.5)
    dw_gu_t, *p_down = _matmul_tn(tag + "_dwgu", dgu, n, tn, D_MODEL, tr,
                                  comm=("scatter", [_to_rows(dw_down)]) if scatter else None)
    comm = None
    if scatter:
        comm = ("chips", _chip_sums(tag + "_wgu", [_by_core(dw_gu_t)]))
    def norm_bwd(accs, ex):
        dh_prev, dw = _rmsnorm_bwd_tile(accs[0], ex[0], ex[2], ex[1])
        return dh_prev, dh_prev, dw

    dh_prev, dh_prev_b, dnorm, *p_gu = _fused_matmul(
        tag + "_dn", M, D_MODEL, F,
        [dict(a=dgu, a_lead=0, b=w_gu_t, acc=0, resident=True),
         dict(a=dgu, a_lead=1, b=w_gu_t, bk_off=1, acc=0, resident=True)], [(h, 0), (dh, 0)],
        norm_bwd, [F32, BF16], 1, tm, D_MODEL, F, outer="i", comm=comm, vecs=[norm_w], row_sums=1)
    return (dh_prev, dh_prev_b, dnorm, *((p_gu[0], p_down[0]) if scatter else (dw_gu_t, dw_down)))


def kernel(x, meta_tokens, ffn1_norm, ffn1_w_gu, ffn1_w_down, mix_norm, w_in, ssd_conv_w, ssd_conv_b, ssd_dt_bias, ssd_a_log, ssd_d, ssd_norm, hg_lower_bound, hg_norm, w_branch_a, w_branch_b, w_out, ffn2_norm, ffn2_w_gu, ffn2_w_down, final_norm, loss_target, m_meta_tokens, m_ffn1_norm, m_ffn1_w_gu, m_ffn1_w_down, m_mix_norm, m_w_in, m_ssd_conv_w, m_ssd_conv_b, m_ssd_dt_bias, m_ssd_a_log, m_ssd_d, m_ssd_norm, m_hg_lower_bound, m_hg_norm, m_w_branch_a, m_w_branch_b, m_w_out, m_ffn2_norm, m_ffn2_w_gu, m_ffn2_w_down, m_final_norm, v_meta_tokens, v_ffn1_norm, v_ffn1_w_gu, v_ffn1_w_down, v_mix_norm, v_w_in, v_ssd_conv_w, v_ssd_conv_b, v_ssd_dt_bias, v_ssd_a_log, v_ssd_d, v_ssd_norm, v_hg_lower_bound, v_hg_norm, v_w_branch_a, v_w_branch_b, v_w_out, v_ffn2_norm, v_ffn2_w_gu, v_ffn2_w_down, v_final_norm):
    Bl, S, D = x.shape
    T = PAD + N_META + S
    nc = T // Q
    M = Bl * T
    me = 4 * lax.axis_index("x") + 2 * lax.axis_index("y") + lax.axis_index("c")

    bf = lambda a: a[0].astype(BF16)
    bft = lambda a: a[0].T.astype(BF16)
    g_meta, g_conv_w = _exchange("gather_small", "gather", [meta_tokens, ssd_conv_w[0]])
    meta_full, conv_w_full = _cols(g_meta), _cols(g_conv_w)
    bias_p, alog_p, d_p = _heads_to_lanes(ssd_dt_bias), _heads_to_lanes(ssd_a_log), _heads_to_lanes(ssd_d)
    final_w = final_norm.reshape(1, D)

    h0, n1, g_wgu1 = _embed_norm(x, meta_full, ffn1_norm, comm=("gather", [bft(ffn1_w_gu)]))
    wgu1 = _rows(g_wgu1)
    tm = _tile(M, (1088, 544, 256))
    ffn1_saved, (g_wd1, g_win) = _ffn_fwd_gu("ffn1", n1, wgu1, comm=("gather", [bf(ffn1_w_down), bft(w_in)]))
    wd1 = _rows(g_wd1)
    win_t = _rows(g_win)
    win_dt = jnp.pad(win_t[DT_ROW:DT_ROW + SSD_HEADS], ((0, 128 - SSD_HEADS), (0, 0)))
    h1, un = _ffn_fwd_down("ffn1", h0, ffn1_saved[3], wd1, next_norm=mix_norm)
    plain = lambda accs, ex: (accs[0],)
    proj, g_wa, g_wb, g_wo = _fused_matmul(
        "in_proj", M, N_MAIN, D, [dict(a=un, b=win_t, trans_b=True, acc=0, b_shift=(DT_ROW // 1536, SSD_HEADS))], [],
        plain, [BF16], 1, tm, 1536, D,
        outer="j", comm=("gather", [bf(w_branch_a), bf(w_branch_b), bf(w_out)], "early"), sub=512)
    wa, wb, wo = _rows(g_wa), _rows(g_wb), _rows(g_wo)
    (dtr,) = _fused_matmul("in_proj_dt", M, 128, D, [dict(a=un, b=win_dt, trans_b=True, acc=0)], [], plain, [F32], 1,
                           tm, 128, D, outer="j")
    xc = _conv_fwd(proj, conv_w_full, ssd_conv_b, Bl, T)
    ya, ssd_prev = _ssd_fwd(xc, dtr, proj, bias_p, alog_p, d_p, ssd_norm, Bl, nc)
    yb, hg_o, hg_st, g_wgu2, g_wd2 = _hgrn_fwd(proj, hg_lower_bound, hg_norm, Bl, nc,
                                               comm=("gather", [bft(ffn2_w_gu), bf(ffn2_w_down)]))
    wgu2, wd2 = _rows(g_wgu2), _rows(g_wd2)

    def branch_fwd(accs, ex):
        pa, pb = accs
        return pa, pb, _sigmoid(ex[0].astype(F32)) * pa + _sigmoid(ex[1].astype(F32)) * pb

    pa, pb, merged = _fused_matmul(
        "branches", M, D, D, [dict(a=ya, b=wa, acc=0), dict(a=yb, b=wb, acc=1)], [(proj, 7), (proj, 8)],
        branch_fwd, [BF16, BF16, BF16], 2, tm, D, D, outer="j", sub=256)
    def out_with_norm(accs, ex):
        h_new = ex[0] + accs[0]
        return h_new, _rmsnorm_tile(h_new, ex[1])

    h2, n2 = _fused_matmul("out_proj", M, D, D, [dict(a=merged, b=wo, acc=0)], [(h1, 0)], out_with_norm,
                           [F32, BF16], 1, tm, D, D, outer="j", vecs=[ffn2_norm])
    ffn2_saved, _ = _ffn_fwd_gu("ffn2", n2, wgu2)
    h3 = _ffn_fwd_down("ffn2", h2, ffn2_saved[3], wd2)

    dh3, dh3_b, d_final, loss_part = _loss_head(h3, final_w, loss_target, Bl, nc)
    dh2, dh2_b, d_ffn2_norm, d_wgu2, d_wd2 = _ffn_bwd("ffn2", dh3, dh3_b, h2, ffn2_norm, wgu2, wd2, ffn2_saved)

    def branch_bwd(accs, ex):
        dm = accs[0]
        ga, gb, pav, pbv = (e.astype(F32) for e in ex)
        sa, sb = _sigmoid(ga), _sigmoid(gb)
        return (dm * sa, dm * sb,
                jnp.concatenate([dm * pav * sa * (1.0 - sa), dm * pbv * sb * (1.0 - sb)], axis=1))

    d_merged_outs = []

    def d_merged_with_swap(theirs):
        d_merged_outs.extend(_fused_matmul(
            "d_merged", M, D, D, [dict(a=dh2_b, b=wo, trans_b=True, acc=0)], [(proj, 7), (proj, 8), (pa, 0), (pb, 0)],
            branch_bwd, [BF16] * 2, 1, tm, D, D, outer="j", comm=("swap", theirs),
            wide=dict(width=2 * D, col=7 * D, total=N_MAIN, dtype=BF16)))
        return d_merged_outs[3:]

    s_ffn2 = _chip_sums("ffn2", [_by_core(d_wgu2), _by_core(d_wd2)], swap_in=d_merged_with_swap)
    dpa, dpb, dproj = d_merged_outs[:3]
    (d_wo,) = _matmul_tn("d_w_out", merged, dh2_b, 512, D, M)
    (d_wa,) = _matmul_tn("d_w_a", ya, dpa, 512, D, M)
    (d_wb,) = _matmul_tn("d_w_b", yb, dpb, 512, D, M)
    dya, dyb = _fused_matmul(
        "d_branches", M, D, D, [dict(a=dpa, b=wa, trans_b=True, acc=0), dict(a=dpb, b=wb, trans_b=True, acc=1)], [],
        lambda accs, ex: (accs[0], accs[1]), [BF16, BF16], 2, tm, D, D, outer="j")
    *ssd_grads, p_wgu2, p_wd2 = _ssd_bwd(xc, dtr, proj, bias_p, alog_p, d_p, ssd_norm, ssd_prev, dya, dproj, Bl, nc,
                                         comm=("chips", s_ffn2))
    dxc, dproj, ddtr, d_bias_p, d_alog_p, d_d_p, d_ssd_norm = ssd_grads
    dproj, d_conv_w, d_conv_b = _conv_bwd(proj, conv_w_full, ssd_conv_b, dxc, dproj, Bl, T)
    dproj, d_hb, d_hg_norm, p_wa, p_wb, p_wo = _hgrn_bwd(
        proj, hg_lower_bound, hg_norm, hg_o, hg_st, dyb, dproj, Bl, nc,
        comm=("scatter", [_to_rows(d_wa), _to_rows(d_wb), _to_rows(d_wo)]))
    ddtr_b = ddtr.astype(BF16)
    (d_win_t,) = _matmul_tn("d_w_in", dproj, un, 768, D, M, out_skip=(DT_ROW, SSD_HEADS))
    (d_win_dt,) = _matmul_tn("d_w_in_dt", ddtr_b, un, 128, D, M)
    d_win_t = lax.dynamic_update_slice(d_win_t, d_win_dt[:SSD_HEADS], (DT_ROW, 0))
    d_un_dt_outs = []

    def d_un_dt_with_swap(theirs):
        d_un_dt_outs.extend(_fused_matmul("d_un_dt", M, D, 128, [dict(a=ddtr_b, b=win_dt, acc=0)], [], plain, [F32], 1,
                                          tm, D, 128, outer="j", comm=("swap", theirs)))
        return d_un_dt_outs[1:]

    s_win = _chip_sums("w_in", [_by_core(d_win_t)], swap_in=d_un_dt_with_swap)
    def mix_norm_bwd(accs, ex):
        dh, dw = _rmsnorm_bwd_tile(accs[0] + ex[0], ex[1], ex[3], ex[2])
        return dh, dh, dw

    dh1, dh1_b, d_mix_norm, p_win = _fused_matmul(
        "d_un", M, D, N_MAIN, [dict(a=dproj, b=win_t, acc=0, b_shift=(DT_ROW // 3072, SSD_HEADS))],
        [(d_un_dt_outs[0], 0), (h1, 0), (dh2, 0)],
        mix_norm_bwd, [F32, BF16], 1, _tile(M, (544, 256)), D, 3072, outer="i", comm=("chips", s_win),
        vecs=[mix_norm], row_sums=1)
    dh0, _, d_ffn1_norm, p_wgu1, p_wd1 = _ffn_bwd("ffn1", dh1, dh1_b, h0, ffn1_norm, wgu1, wd1, ffn1_saved, scatter=True)

    dh0 = dh0.reshape(Bl, T, D)
    grad_x = dh0[:, PAD + N_META:]
    d_meta = dh0[:, PAD:PAD + N_META]

    small_grads = [d_ffn1_norm, d_mix_norm, d_conv_b, _lanes_to_heads(d_bias_p), _lanes_to_heads(d_alog_p),
                   _lanes_to_heads(d_d_p), d_ssd_norm, d_hb, d_hg_norm, d_ffn2_norm, d_final.reshape(D), d_conv_w]
    small_packed = _pack_rows(small_grads + [d_meta[b] for b in range(Bl)])
    parts = [p_wgu1, p_wd1, p_win, p_wa, p_wb, p_wo, p_wgu2, p_wd2]
    (small_all,) = _exchange("gather_small_grads", "gather", [small_packed])
    small_sum = _sum_parts("sum_small_grads", small_all)
    unpacked = _unpack_rows(small_sum, small_grads + [d_meta[b] for b in range(Bl)])
    g_small = unpacked[:len(small_grads)]
    g_meta_full = unpacked[len(small_grads)]
    for b in range(1, Bl):
        g_meta_full = g_meta_full + unpacked[len(small_grads) + b]
    g_meta = lax.dynamic_slice_in_dim(g_meta_full, me * (D // N_DEV), D // N_DEV, axis=1)
    g_conv_w = lax.dynamic_slice_in_dim(g_small[11], me * (SSD_CONV_CH // N_DEV), SSD_CONV_CH // N_DEV, axis=1)

    names = ["meta_tokens", "ffn1_norm", "ffn1_w_gu", "ffn1_w_down", "mix_norm", "w_in", "ssd_conv_w", "ssd_conv_b",
             "ssd_dt_bias", "ssd_a_log", "ssd_d", "ssd_norm", "hg_lower_bound", "hg_norm", "w_branch_a", "w_branch_b",
             "w_out", "ffn2_norm", "ffn2_w_gu", "ffn2_w_down", "final_norm"]
    W = dict(meta_tokens=meta_tokens, ffn1_norm=ffn1_norm, ffn1_w_gu=ffn1_w_gu, ffn1_w_down=ffn1_w_down, mix_norm=mix_norm,
             w_in=w_in, ssd_conv_w=ssd_conv_w, ssd_conv_b=ssd_conv_b, ssd_dt_bias=ssd_dt_bias, ssd_a_log=ssd_a_log,
             ssd_d=ssd_d, ssd_norm=ssd_norm, hg_lower_bound=hg_lower_bound, hg_norm=hg_norm, w_branch_a=w_branch_a,
             w_branch_b=w_branch_b, w_out=w_out, ffn2_norm=ffn2_norm, ffn2_w_gu=ffn2_w_gu, ffn2_w_down=ffn2_w_down,
             final_norm=final_norm)
    Mo = dict(meta_tokens=m_meta_tokens, ffn1_norm=m_ffn1_norm, ffn1_w_gu=m_ffn1_w_gu, ffn1_w_down=m_ffn1_w_down,
              mix_norm=m_mix_norm, w_in=m_w_in, ssd_conv_w=m_ssd_conv_w, ssd_conv_b=m_ssd_conv_b, ssd_dt_bias=m_ssd_dt_bias,
              ssd_a_log=m_ssd_a_log, ssd_d=m_ssd_d, ssd_norm=m_ssd_norm, hg_lower_bound=m_hg_lower_bound, hg_norm=m_hg_norm,
              w_branch_a=m_w_branch_a, w_branch_b=m_w_branch_b, w_out=m_w_out, ffn2_norm=m_ffn2_norm, ffn2_w_gu=m_ffn2_w_gu,
              ffn2_w_down=m_ffn2_w_down, final_norm=m_final_norm)
    Vo = dict(meta_tokens=v_meta_tokens, ffn1_norm=v_ffn1_norm, ffn1_w_gu=v_ffn1_w_gu, ffn1_w_down=v_ffn1_w_down,
              mix_norm=v_mix_norm, w_in=v_w_in, ssd_conv_w=v_ssd_conv_w, ssd_conv_b=v_ssd_conv_b, ssd_dt_bias=v_ssd_dt_bias,
              ssd_a_log=v_ssd_a_log, ssd_d=v_ssd_d, ssd_norm=v_ssd_norm, hg_lower_bound=v_hg_lower_bound, hg_norm=v_hg_norm,
              w_branch_a=v_w_branch_a, w_branch_b=v_w_branch_b, w_out=v_w_out, ffn2_norm=v_ffn2_norm, ffn2_w_gu=v_ffn2_w_gu,
              ffn2_w_down=v_ffn2_w_down, final_norm=v_final_norm)
    grads, deltas, new_m, new_v = {}, {}, {}, {}
    big_names = ["ffn1_w_gu", "ffn1_w_down", "w_in", "w_branch_a", "w_branch_b", "w_out", "ffn2_w_gu", "ffn2_w_down"]
    transposed = ("ffn1_w_gu", "ffn2_w_gu", "w_in")
    for nm, part in zip(big_names, parts):
        view = (lambda a: a[0].T) if nm in transposed else (lambda a: a[0])
        back = (lambda o: o.T[None]) if nm in transposed else (lambda o: o[None])
        outs = _adamw("adamw_" + nm, part, view(W[nm]), view(Mo[nm]), view(Vo[nm]))
        grads[nm], deltas[nm], new_m[nm], new_v[nm] = (back(o) for o in outs)
    small_names = ["ffn1_norm", "mix_norm", "ssd_conv_b", "ssd_dt_bias", "ssd_a_log", "ssd_d", "ssd_norm", "hg_lower_bound",
                   "hg_norm", "ffn2_norm", "final_norm", "ssd_conv_w", "meta_tokens"]
    small_g = g_small[:11] + [g_conv_w.reshape(ssd_conv_w.shape), g_meta]
    pk = lambda d: _pack_rows([d[nm] for nm in small_names])
    outs = _adamw("adamw_small", _pack_rows(small_g)[None], pk(W), pk(Mo), pk(Vo))
    like = [W[nm] for nm in small_names]
    for dst, o in zip((grads, deltas, new_m, new_v), outs):
        for nm, val in zip(small_names, _unpack_rows(o, like)):
            dst[nm] = val

    loss = lax.psum(loss_part[0, 0], MESH_AXES)
    return (loss, grad_x, *[grads[nm] for nm in names], *[deltas[nm] for nm in names],
            *[new_m[nm] for nm in names], *[new_v[nm] for nm in names])
```

```python
import functools

import jax
import jax.numpy as jnp
from jax import lax
from jax.experimental import pallas as pl
from jax.experimental.pallas import tpu as pltpu

F32, BF16 = jnp.float32, jnp.bfloat16
NN, NT, TN = ((1,), (0,)), ((1,), (1,)), ((0,), (0,))
MESH_AXES = ("x", "y", "c")
N_DEV = 8

D_MODEL = 1024
N_META = 16
EPS = 1e-6
SSD_HEADS, SSD_HEAD_DIM, SSD_GROUPS, SSD_STATE, SSD_CONV, Q = 16, 64, 4, 128, 4, 128
SSD_INNER = SSD_HEADS * SSD_HEAD_DIM
SSD_CONV_CH = SSD_INNER + 2 * SSD_GROUPS * SSD_STATE
HG_WIDTH, HG_HEADS, HG_CHUNK = 1024, 8, 16
PAD = Q - N_META
N_MAIN = 9 * 1024
ADAM_LR, ADAM_B1, ADAM_B2, ADAM_EPS, ADAM_WD, ADAM_STEP = 0.001, 0.9, 0.999, 1e-08, 0.01, 10
VMEM_LIMIT = 52 * 1024 * 1024


def _dot(a, b, dims):
    return lax.dot_general(a, b, (dims, ((), ())), preferred_element_type=F32)


def _dot01(a, b, dims, sel):
    x = b if sel == "a" else a
    hi = x.astype(BF16)
    r1 = x - hi.astype(F32)
    mid = r1.astype(BF16)
    lo = (r1 - mid.astype(F32)).astype(BF16)
    s = (a if sel == "a" else b).astype(BF16)
    parts = [_dot(s, p, dims) if sel == "a" else _dot(p, s, dims) for p in (hi, mid, lo)]
    return parts[0] + parts[1] + parts[2]


def _sigmoid(x):
    return 1.0 / (1.0 + jnp.exp(-x))


def _dsilu(x, s):
    return s * (1.0 + x * (1.0 - s))


def _softplus(x):
    e = jnp.exp(-jnp.abs(x))
    u = 1.0 + e
    log1p_e = jnp.where(u == 1.0, e, jnp.log(u) * e / (u - 1.0))
    return jnp.maximum(x, 0.0) + log1p_e


def _params(sem):
    return pltpu.CompilerParams(dimension_semantics=sem, vmem_limit_bytes=VMEM_LIMIT)


def _tile(n, prefs):
    for p in prefs:
        if n % p == 0:
            return p
    return n


CHIP_FLIPS = ((1, 0), (0, 1), (1, 1))
N_PEER = N_DEV - 1


def _comm_gather(srcs, outs, send_sems, recv_sems, local_sems):
    n = len(srcs)
    x, y, c = (lax.axis_index(a) for a in MESH_AXES)
    dev = lambda px, py, pc: 4 * px + 2 * py + pc
    me, sib = dev(x, y, c), (x, y, 1 - c)

    def rc(w, k, slot, to, src=None):
        return pltpu.make_async_remote_copy(
            src_ref=outs[w].at[slot] if src is None else src, dst_ref=outs[w].at[slot],
            send_sem=send_sems.at[w, k], recv_sem=recv_sems.at[w, k], device_id=to, device_id_type=pl.DeviceIdType.MESH)

    def local(w):
        return pltpu.make_async_copy(srcs[w], outs[w].at[me], local_sems.at[w])

    def start():
        for w in range(n):
            local(w).start()
            rc(w, 0, me, sib, src=srcs[w]).start()
            for j, (fx, fy) in enumerate(CHIP_FLIPS):
                rc(w, 1 + j, me, (x ^ fx, y ^ fy, c), src=srcs[w]).start()

    def pass_on():
        for w in range(n):
            for j, (fx, fy) in enumerate(CHIP_FLIPS):
                slot = dev(x ^ fx, y ^ fy, c)
                rc(w, 1 + j, slot, sib).wait_recv()
                rc(w, 4 + j, slot, sib).start()

    def finish():
        for w in range(n):
            rc(w, 0, dev(x, y, 1 - c), sib).wait_recv()
            rc(w, 0, me, sib, src=srcs[w]).wait_send()
            for j, (fx, fy) in enumerate(CHIP_FLIPS):
                rc(w, 4 + j, dev(x ^ fx, y ^ fy, 1 - c), sib).wait_recv()
                rc(w, 1 + j, me, sib, src=srcs[w]).wait_send()
                rc(w, 4 + j, dev(x ^ fx, y ^ fy, c), sib).wait_send()
            local(w).wait()

    return start, pass_on, finish


def _comm_scatter(srcs, outs, send_sems, recv_sems, local_sems):
    n = len(srcs)
    x, y, c = (lax.axis_index(a) for a in MESH_AXES)
    me = 4 * x + 2 * y + c

    def copies():
        out = []
        for w in range(n):
            out.append(pltpu.make_async_copy(srcs[w].at[me], outs[w].at[me], local_sems.at[w]))
            for k in range(1, N_DEV):
                px, py, pc = x ^ (k >> 2), y ^ ((k >> 1) & 1), c ^ (k & 1)
                out.append(pltpu.make_async_remote_copy(
                    src_ref=srcs[w].at[4 * px + 2 * py + pc], dst_ref=outs[w].at[me],
                    send_sem=send_sems.at[w, k - 1], recv_sem=recv_sems.at[w, k - 1],
                    device_id=(px, py, pc), device_id_type=pl.DeviceIdType.MESH))
        return out

    def start():
        for cp in copies():
            cp.start()

    def finish():
        for cp in copies():
            cp.wait()

    return start, None, finish


def _comm_swap(srcs, outs, send_sems, recv_sems, local_sems):
    x, y, c = (lax.axis_index(a) for a in MESH_AXES)

    def copies():
        return [pltpu.make_async_remote_copy(
            src_ref=srcs[w].at[1 - c], dst_ref=outs[w], send_sem=send_sems.at[w, 0], recv_sem=recv_sems.at[w, 0],
            device_id=(x, y, 1 - c), device_id_type=pl.DeviceIdType.MESH) for w in range(len(srcs))]

    def start():
        for cp in copies():
            cp.start()

    def finish():
        for cp in copies():
            cp.wait()

    return start, None, finish


def _comm_chips(srcs, outs, send_sems, recv_sems, local_sems):
    n = len(srcs)
    x, y, c = (lax.axis_index(a) for a in MESH_AXES)
    mine = 2 * x + y

    def copies():
        out = []
        for w in range(n):
            out.append(pltpu.make_async_copy(srcs[w].at[mine], outs[w].at[mine], local_sems.at[w]))
            for j, (fx, fy) in enumerate(CHIP_FLIPS):
                px, py = x ^ fx, y ^ fy
                out.append(pltpu.make_async_remote_copy(
                    src_ref=srcs[w].at[2 * px + py], dst_ref=outs[w].at[mine],
                    send_sem=send_sems.at[w, j], recv_sem=recv_sems.at[w, j],
                    device_id=(px, py, c), device_id_type=pl.DeviceIdType.MESH))
        return out

    def start():
        for cp in copies():
            cp.start()

    def finish():
        for cp in copies():
            cp.wait()

    return start, None, finish


def _comm_parts(comm):
    kind, arrays = comm[:2]
    n = len(arrays)
    lead = {"gather": lambda a: (N_DEV,) + a.shape, "scatter": lambda a: (N_DEV,) + a.shape[1:],
            "swap": lambda a: a.shape[1:], "chips": lambda a: a.shape}[kind]
    shapes = [jax.ShapeDtypeStruct(lead(a), a.dtype) for a in arrays]
    sems = [pltpu.SemaphoreType.DMA((n, N_PEER)), pltpu.SemaphoreType.DMA((n, N_PEER)), pltpu.SemaphoreType.DMA((n,))]
    make = {"gather": _comm_gather, "scatter": _comm_scatter, "swap": _comm_swap, "chips": _comm_chips}[kind]
    return n, shapes, sems, make


def _exchange(name, kind, arrays):
    n, shapes, sems, make = _comm_parts((kind, arrays))

    def body(*refs):
        start, middle, finish = make(refs[:n], refs[n:2 * n], *refs[2 * n:])
        start()
        if middle:
            middle()
        finish()

    any_spec = pl.BlockSpec(memory_space=pl.ANY)
    return pl.pallas_call(
        body, name=name, in_specs=[any_spec] * n, out_specs=[any_spec] * n, out_shape=shapes, scratch_shapes=sems,
        compiler_params=pltpu.CompilerParams(has_side_effects=True),
    )(*arrays)


def _call(body, *, name, grid, in_specs, out_specs, out_shape, scratch, sem, args, comm=None, into=None):
    any_spec = pl.BlockSpec(memory_space=pl.ANY)
    in_specs, args, aliases, n_body_in = list(in_specs), list(args), {}, len(in_specs)
    if into is not None:
        in_specs.append(any_spec)
        args.append(into[0])
        aliases = {n_body_in: into[1]}
    n_in, n_out, n_scr = len(in_specs), len(out_specs), len(scratch)
    if comm is None:
        def plain(*refs):
            body(*refs[:n_body_in], *refs[n_in:])

        return pl.pallas_call(plain, name=name, grid=grid, in_specs=in_specs, out_specs=out_specs, out_shape=out_shape,
                              scratch_shapes=scratch, input_output_aliases=aliases, compiler_params=_params(sem))(*args)
    n, shapes, sems, make = _comm_parts(comm)

    def carrier(*refs):
        ins, csrc = refs[:n_body_in], refs[n_in:n_in + n]
        outs, cout = refs[n_in + n:n_in + n + n_out], refs[n_in + n + n_out:n_in + 2 * n + n_out]
        rest = refs[n_in + 2 * n + n_out:]
        start, middle, finish = make(csrc, cout, *rest[n_scr:])
        ids = [pl.program_id(a) for a in range(len(grid))]
        step = functools.reduce(lambda acc, ig: acc * ig[1] + ig[0], zip(ids, grid), 0)
        n_steps = functools.reduce(lambda a, b: a * b, grid, 1)
        pl.when(step == 0)(start)
        body(*ins, *outs, *rest[:n_scr])
        if middle:
            early = len(comm) > 2 and comm[2] == "early"
            pl.when(step == ((3 * n_steps) // 4 if early else n_steps - 1))(middle)
        pl.when(step == n_steps - 1)(finish)

    return pl.pallas_call(
        carrier, name=name, grid=grid, in_specs=in_specs + [any_spec] * n,
        out_specs=list(out_specs) + [any_spec] * n, out_shape=list(out_shape) + shapes,
        scratch_shapes=list(scratch) + sems, input_output_aliases=aliases,
        compiler_params=pltpu.CompilerParams(dimension_semantics=("arbitrary",) * len(grid),
                                             vmem_limit_bytes=VMEM_LIMIT, has_side_effects=True),
    )(*args, *comm[1])


def _fused_matmul(name, M, N, K, pairs, extras, epilogue, out_dtypes, n_acc, tm, tn, tk, outer="i", comm=None,
                  stack=False, vecs=(), row_sums=0, wide=None, sub=None):
    nk = K // tk
    n_pairs, n_ex, n_out = len(pairs), len(extras), len(out_dtypes)
    assert not row_sums or (outer == "i" and N == tn)

    def ij(g0, g1):
        return (g0, g1) if outer == "i" else (g1, g0)

    in_specs, args = [], []
    for p in pairs:
        ao, bk, bn = p.get("a_off", 0), p.get("bk_off", 0), p.get("bn_off", 0)
        mode = dict(pipeline_mode=pl.Buffered(1)) if p.get("resident") else {}
        if "a_lead" in p:
            in_specs.append(pl.BlockSpec((None, tm, tk),
                                         lambda g0, g1, k, ao=ao, ld=p["a_lead"]: (ld, ij(g0, g1)[0], k + ao)))
        else:
            in_specs.append(pl.BlockSpec((tm, tk), lambda g0, g1, k, ao=ao: (ij(g0, g1)[0], k + ao)))
        if "b_shift" in p:
            first, shift = p["b_shift"]
            if p.get("trans_b"):
                in_specs.append(pl.BlockSpec(
                    (pl.Element(tn), pl.Element(tk)),
                    lambda g0, g1, k, bk=bk: (
                        pl.multiple_of(ij(g0, g1)[1] * tn + jnp.where(ij(g0, g1)[1] >= first, shift, 0), 16),
                        (k + bk) * tk)))
            else:
                in_specs.append(pl.BlockSpec(
                    (pl.Element(tk), pl.Element(tn)),
                    lambda g0, g1, k, bn=bn: (pl.multiple_of(k * tk + jnp.where(k >= first, shift, 0), 16),
                                              (ij(g0, g1)[1] + bn) * tn)))
        elif p.get("trans_b"):
            in_specs.append(pl.BlockSpec((tn, tk), lambda g0, g1, k, bk=bk, bn=bn: (ij(g0, g1)[1] + bn, k + bk), **mode))
        else:
            in_specs.append(pl.BlockSpec((tk, tn), lambda g0, g1, k, bk=bk, bn=bn: (k + bk, ij(g0, g1)[1] + bn), **mode))
        args += [p["a"], p["b"]]
    for arr, off in extras:
        in_specs.append(pl.BlockSpec((tm, tn), lambda g0, g1, k, off=off: (ij(g0, g1)[0], ij(g0, g1)[1] + off)))
        args.append(arr)
    for arr in vecs:
        in_specs.append(pl.BlockSpec((1, tn), lambda g0, g1, k: (0, ij(g0, g1)[1])))
        args.append(arr)
    if stack:
        out_specs = [pl.BlockSpec((n_out, tm, tn), lambda g0, g1, k: (0,) + ij(g0, g1))]
        out_shape = [jax.ShapeDtypeStruct((n_out, M, N), out_dtypes[0])]
    else:
        out_specs = [pl.BlockSpec((tm, tn), lambda g0, g1, k: ij(g0, g1)) for _ in out_dtypes]
        out_shape = [jax.ShapeDtypeStruct((M, N), dt) for dt in out_dtypes]
    if wide:
        out_specs.append(pl.BlockSpec((pl.Element(tm), pl.Element(wide["width"])),
                                      lambda g0, g1, k: (pl.multiple_of(ij(g0, g1)[0] * tm, 16), wide["col"])))
        out_shape.append(jax.ShapeDtypeStruct((M, wide["total"]), wide["dtype"]))
    n_tile_out = len(out_specs)
    out_specs += [pl.BlockSpec((1, tn), lambda g0, g1, k: (0, 0)) for _ in range(row_sums)]
    out_shape += [jax.ShapeDtypeStruct((1, N), F32) for _ in range(row_sums)]
    grid = (M // tm, N // tn, nk) if outer == "i" else (N // tn, M // tm, nk)
    n_in = 2 * n_pairs + n_ex + len(vecs)

    def partials(refs, cs=slice(None)):
        accs = [None] * n_acc
        for idx, p in enumerate(pairs):
            b_ref = refs[2 * idx + 1]
            d = (_dot(refs[2 * idx][...], b_ref[cs, :], NT) if p.get("trans_b")
                 else _dot(refs[2 * idx][...], b_ref[:, cs], NN))
            accs[p["acc"]] = d if accs[p["acc"]] is None else accs[p["acc"]] + d
        return accs

    def finish(accs, refs, first_rows, cs=slice(None)):
        res = epilogue(accs, [r[:, cs] for r in refs[2 * n_pairs:n_in]])
        if stack:
            o = refs[n_in]
            for idx in range(n_out):
                o[idx, :, cs] = res[idx].astype(o.dtype)
        else:
            for o, r in zip(refs[n_in:n_in + n_out], res):
                o[:, cs] = r.astype(o.dtype)
        if wide:
            o = refs[n_in + n_tile_out - 1]
            o[...] = res[n_out].astype(o.dtype)
        for o, r in zip(refs[n_in + n_tile_out:n_in + n_tile_out + row_sums], res[n_out + bool(wide):]):
            @pl.when(first_rows)
            def _(o=o, r=r):
                o[...] = r

            @pl.when(jnp.logical_not(first_rows))
            def _(o=o, r=r):
                o[...] += r

    if nk == 1 and sub:
        assert not wide and not row_sums and tn % sub == 0

        def body(*refs):
            for c in range(tn // sub):
                cs = slice(c * sub, (c + 1) * sub)
                finish(partials(refs, cs), refs, None, cs)
        scratch = []
    elif nk == 1:
        def body(*refs):
            finish(partials(refs), refs, pl.program_id(0) == 0)
        scratch = []
    else:
        def body(*refs):
            acc_refs = refs[-n_acc:]
            k = pl.program_id(2)
            first_rows = pl.program_id(0) == 0
            new = partials(refs)

            @pl.when(k == 0)
            def _():
                for a, v in zip(acc_refs, new):
                    a[...] = v

            @pl.when(k > 0)
            def _():
                for a, v in zip(acc_refs, new):
                    a[...] += v

            @pl.when(k == nk - 1)
            def _():
                finish([a[...] for a in acc_refs], refs, first_rows)
        scratch = [pltpu.VMEM((tm, tn), F32) for _ in range(n_acc)]

    return _call(body, name=name, grid=grid, in_specs=in_specs, out_specs=out_specs, out_shape=out_shape,
                 scratch=scratch, sem=("parallel", "parallel", "arbitrary"), args=args, comm=comm)


def _matmul_tn(name, x, y, t1, t2, tr, scale=1.0, comm=None, out_dtype=BF16, out_skip=None):
    L = x.shape[0] if x.ndim == 3 else 1
    R, K1 = x.shape[-2:]
    N1 = y.shape[1]
    nr, n1 = R // tr, K1 // t1
    if x.ndim == 3:
        x_spec = pl.BlockSpec((None, tr, t1), lambda i, j, r: (i // n1, r, i % n1))
    else:
        x_spec = pl.BlockSpec((tr, t1), lambda i, j, r: (r, i))
    rows_out = L * K1
    o_spec = pl.BlockSpec((t1, t2), lambda i, j, r: (i, j))
    if out_skip:
        row, count = out_skip
        rows_out += count
        o_spec = pl.BlockSpec(
            (pl.Element(t1), pl.Element(t2)),
            lambda i, j, r: (pl.multiple_of(i * t1 + jnp.where(i * t1 >= row, count, 0), 16), j * t2))

    def body(x_ref, y_ref, o_ref, *acc):
        d = _dot(x_ref[...], y_ref[...], TN)
        if nr == 1:
            o_ref[...] = (d * scale).astype(o_ref.dtype)
            return
        r = pl.program_id(2)

        @pl.when(r == 0)
        def _():
            acc[0][...] = d

        @pl.when(jnp.logical_and(r > 0, r < nr - 1))
        def _():
            acc[0][...] += d

        @pl.when(r == nr - 1)
        def _():
            o_ref[...] = ((acc[0][...] + d) * scale).astype(o_ref.dtype)

    return _call(
        body, name=name, grid=(L * n1, N1 // t2, nr),
        in_specs=[x_spec, pl.BlockSpec((tr, t2), lambda i, j, r: (r, j))], out_specs=[o_spec],
        out_shape=[jax.ShapeDtypeStruct((rows_out, N1), out_dtype)],
        scratch=[pltpu.VMEM((t1, t2), F32)] if nr > 1 else [],
        sem=("parallel", "parallel", "arbitrary"), args=(x, y), comm=comm)


def _embed_norm(x, meta, w, comm=None):
    Bl, S, D = x.shape
    nb = (PAD + N_META + S) // Q
    M = Bl * nb * Q

    def body(x_ref, meta_ref, w_ref, h_ref, n_ref):
        head = jnp.concatenate([jnp.zeros((PAD, D), F32), meta_ref[...]], axis=0)
        h = jnp.where(pl.program_id(1) == 0, head, x_ref[0])
        h_ref[...] = h
        n_ref[...] = _rmsnorm_tile(h, w_ref[...]).astype(n_ref.dtype)

    row = pl.BlockSpec((Q, D), lambda b, t: (b * nb + t, 0))
    return _call(
        body, name="embed_norm", grid=(Bl, nb),
        in_specs=[pl.BlockSpec((1, Q, D), lambda b, t: (b, jnp.maximum(t - 1, 0), 0)),
                  pl.BlockSpec((N_META, D), lambda b, t: (0, 0)), pl.BlockSpec((1, D), lambda b, t: (0, 0))],
        out_specs=[row, row], out_shape=[jax.ShapeDtypeStruct((M, D), F32), jax.ShapeDtypeStruct((M, D), BF16)],
        scratch=[], sem=("parallel", "parallel"), args=(x, meta, w), comm=comm)


def _rmsnorm_bwd_tile(dn, h, w, dh_in):
    r = lax.rsqrt(jnp.mean(h * h, axis=-1, keepdims=True) + EPS)
    xhat = h * r
    gw = dn * w
    dh = dh_in + r * (gw - xhat * jnp.mean(gw * xhat, axis=-1, keepdims=True))
    return dh, jnp.sum(dn * xhat, axis=0, keepdims=True)


def _loss_head(h, w, target, Bl, nb):
    M, D = h.shape

    def body(h_ref, w_ref, t_ref, dh_ref, dhb_ref, dw_ref, loss_ref):
        b, t = pl.program_id(0), pl.program_id(1)
        live = (t > 0).astype(F32)
        x = h_ref[...]
        r = lax.rsqrt(jnp.mean(x * x, axis=-1, keepdims=True) + EPS)
        xhat = x * r
        wv = w_ref[...]
        err = (xhat * wv - t_ref[0]) * live
        dy = err * (1.0 / D)
        gw = dy * wv
        dx = r * (gw - xhat * jnp.mean(gw * xhat, axis=-1, keepdims=True))
        dh_ref[...] = dx
        dhb_ref[...] = dx.astype(BF16)
        dw = jnp.sum(dy * xhat, axis=0, keepdims=True)
        part = 0.5 * jnp.sum(jnp.sum(err * err, axis=-1, keepdims=True) * (1.0 / D), axis=0, keepdims=True)
        first = jnp.logical_and(b == 0, t == 0)

        @pl.when(first)
        def _():
            dw_ref[...] = dw
            loss_ref[...] = jnp.broadcast_to(part, loss_ref.shape)

        @pl.when(jnp.logical_not(first))
        def _():
            dw_ref[...] += dw
            loss_ref[...] += jnp.broadcast_to(part, loss_ref.shape)

    row = pl.BlockSpec((Q, D), lambda b, t: (b * nb + t, 0))
    vec = pl.BlockSpec((1, D), lambda b, t: (0, 0))
    return pl.pallas_call(
        body, name="loss_head", grid=(Bl, nb),
        in_specs=[row, vec, pl.BlockSpec((1, Q, D), lambda b, t: (b, jnp.maximum(t - 1, 0), 0))],
        out_specs=[row, row, vec, pl.BlockSpec((8, 128), lambda b, t: (0, 0))],
        out_shape=[jax.ShapeDtypeStruct((M, D), F32), jax.ShapeDtypeStruct((M, D), BF16),
                   jax.ShapeDtypeStruct((1, D), F32), jax.ShapeDtypeStruct((8, 128), F32)],
        compiler_params=_params(("arbitrary", "arbitrary")),
    )(h, w, target)


CONV_TC = 256


def _conv_pre(xr_ref, w_ref, b_ref):
    x = xr_ref[...].astype(F32)
    acc = b_ref[...] + w_ref[SSD_CONV - 1:SSD_CONV, :] * x
    for k in range(1, SSD_CONV):
        acc = acc + w_ref[SSD_CONV - 1 - k:SSD_CONV - k, :] * pltpu.roll(x, k, 0)
    return x, acc


def _conv_fwd(proj, w, b, Bl, T):
    M = proj.shape[0]
    off = 1024 // CONV_TC

    def body(xr_ref, w_ref, b_ref, o_ref):
        _, acc = _conv_pre(xr_ref, w_ref, b_ref)
        row = lax.broadcasted_iota(jnp.int32, acc.shape, 0)
        o_ref[...] = jnp.where(row >= PAD, acc * _sigmoid(acc), 0.0).astype(o_ref.dtype)

    return pl.pallas_call(
        body, name="conv_fwd", grid=(Bl, SSD_CONV_CH // CONV_TC),
        in_specs=[pl.BlockSpec((T, CONV_TC), lambda bb, j: (bb, j + off)),
                  pl.BlockSpec((SSD_CONV, CONV_TC), lambda bb, j: (0, j)), pl.BlockSpec((1, CONV_TC), lambda bb, j: (0, j))],
        out_specs=pl.BlockSpec((T, CONV_TC), lambda bb, j: (bb, j)),
        out_shape=jax.ShapeDtypeStruct((M, SSD_CONV_CH), BF16), compiler_params=_params(("parallel", "parallel")),
    )(proj, w, b)


def _conv_bwd(proj, w, b, dxc, dproj, Bl, T):
    M = proj.shape[0]
    off = 1024 // CONV_TC

    def body(xr_ref, w_ref, b_ref, d_ref, dx_ref, dw_ref, db_ref):
        x, acc = _conv_pre(xr_ref, w_ref, b_ref)
        row = lax.broadcasted_iota(jnp.int32, acc.shape, 0)
        s = _sigmoid(acc)
        dpre = jnp.where(row >= PAD, d_ref[...].astype(F32) * _dsilu(acc, s), 0.0)
        dx = w_ref[SSD_CONV - 1:SSD_CONV, :] * dpre
        dws = [jnp.sum(dpre * x, axis=0, keepdims=True)]
        for k in range(1, SSD_CONV):
            dx = dx + w_ref[SSD_CONV - 1 - k:SSD_CONV - k, :] * pltpu.roll(dpre, T - k, 0)
            dws.append(jnp.sum(dpre * pltpu.roll(x, k, 0), axis=0, keepdims=True))
        dx_ref[...] = dx.astype(dx_ref.dtype)
        dw = jnp.concatenate(dws[::-1], axis=0)
        db = jnp.sum(dpre, axis=0, keepdims=True)

        @pl.when(pl.program_id(1) == 0)
        def _():
            dw_ref[...] = dw
            db_ref[...] = db

        @pl.when(pl.program_id(1) > 0)
        def _():
            dw_ref[...] += dw
            db_ref[...] += db

    return _call(
        body, name="conv_bwd", grid=(SSD_CONV_CH // CONV_TC, Bl),
        in_specs=[pl.BlockSpec((T, CONV_TC), lambda j, bb: (bb, j + off)),
                  pl.BlockSpec((SSD_CONV, CONV_TC), lambda j, bb: (0, j)), pl.BlockSpec((1, CONV_TC), lambda j, bb: (0, j)),
                  pl.BlockSpec((T, CONV_TC), lambda j, bb: (bb, j))],
        out_specs=[pl.BlockSpec((T, CONV_TC), lambda j, bb: (bb, j + off)),
                   pl.BlockSpec((SSD_CONV, CONV_TC), lambda j, bb: (0, j)), pl.BlockSpec((1, CONV_TC), lambda j, bb: (0, j))],
        out_shape=[jax.ShapeDtypeStruct(dproj.shape, BF16), jax.ShapeDtypeStruct((SSD_CONV, SSD_CONV_CH), F32),
                   jax.ShapeDtypeStruct((1, SSD_CONV_CH), F32)],
        scratch=[], sem=("parallel", "arbitrary"), args=(proj, w, b, dxc), into=(dproj, 0))


N_PAIR = SSD_HEADS // 2
HPG = SSD_HEADS // SSD_GROUPS
GW = SSD_INNER // SSD_GROUPS


def _per_group(fn, *arrs):
    return jnp.concatenate([jnp.broadcast_to(fn(*(a[:, GW * g:GW * (g + 1)] for a in arrs)), (arrs[0].shape[0], GW))
                            for g in range(SSD_GROUPS)], axis=1)


def _ssd_prep(c, dtr_ref, bias_ref, alog_ref, d_ref):
    row = lax.broadcasted_iota(jnp.int32, (Q, 128), 0)
    col = lax.broadcasted_iota(jnp.int32, (Q, 128), 1)
    live = col < SSD_HEADS
    valid = jnp.logical_and(jnp.logical_or(c > 0, row >= PAD), live)
    pre = dtr_ref[...] + bias_ref[...]
    dt = jnp.where(valid, _softplus(pre), 0.0)
    A = jnp.where(live[0:1], -jnp.exp(alog_ref[...]), 0.0)
    tri = row >= col
    eye = (row == col).astype(BF16)
    cs = _dot01(tri, dt * A, NN, "a")
    cst = _dot01(eye, cs, NT, "a")
    spread = (lax.broadcasted_iota(jnp.int32, (128, SSD_INNER), 0)
              == lax.broadcasted_iota(jnp.int32, (128, SSD_INNER), 1) // SSD_HEAD_DIM).astype(BF16)
    dt_w = _dot01(dt, spread, NN, "b")
    cs_w = _dot01(cs, spread, NN, "b")
    d_w = _dot01(jnp.broadcast_to(d_ref[...], (8, 128)), spread, NN, "b")[0:1]
    lane = lax.broadcasted_iota(jnp.int32, (Q, SSD_INNER), 1)
    first = (lane % 128) < SSD_HEAD_DIM
    return dict(row=row, col=col, valid=valid, pre=pre, dt=dt, A=A, tri=tri, eye=eye, cs=cs, cst=cst, spread=spread,
                dt_w=dt_w, cs_w=cs_w, d_w=d_w, ecs_w=jnp.exp(cs_w), decay_w=jnp.exp(cs_w[Q - 1:Q] - cs_w), first=first)


def _ssd_chunk(xc_ref, s, states):
    xv = xc_ref[:, 0:SSD_INNER].astype(F32)
    Bs = [xc_ref[:, SSD_INNER + 128 * g:SSD_INNER + 128 * (g + 1)] for g in range(SSD_GROUPS)]
    Cs = [xc_ref[:, SSD_INNER + 512 + 128 * g:SSD_INNER + 512 + 128 * (g + 1)] for g in range(SSD_GROUPS)]
    X = xv * s["dt_w"]
    X0 = jnp.where(s["first"], X, 0.0)
    Xb = (X0.astype(BF16), (X - X0).astype(BF16))
    Xd = (X * s["decay_w"]).astype(BF16)
    CB = [_dot(Cs[g], Bs[g], NT) for g in range(SSD_GROUPS)]
    Lms = [jnp.exp(jnp.where(s["tri"], s["cs"][:, h:h + 1] - s["cst"][h:h + 1, :], -jnp.inf)) for h in range(SSD_HEADS)]
    Ms = [CB[h // HPG] * Lms[h] for h in range(SSD_HEADS)]
    Mb = [m.astype(BF16) for m in Ms]
    prev_b = [st.astype(BF16) for st in states]
    yds, yos, sts = [], [], []
    for p in range(N_PAIR):
        g, ln = p // 2, slice(128 * p, 128 * (p + 1))
        yds.append(_dot(Mb[2 * p], Xb[0][:, ln], NN) + _dot(Mb[2 * p + 1], Xb[1][:, ln], NN))
        yos.append(_dot(Cs[g], prev_b[p], NT))
        sts.append(_dot(Xd[:, ln], Bs[g], TN))
    yo = jnp.concatenate(yos, axis=1)
    y = jnp.concatenate(yds, axis=1) + yo * s["ecs_w"] + xv * s["d_w"]
    upper = s["row"] < SSD_HEAD_DIM
    cl = s["cs"][Q - 1:Q, :]
    ecl_rows = [jnp.where(upper, jnp.exp(cl[:, 2 * p:2 * p + 1]), jnp.exp(cl[:, 2 * p + 1:2 * p + 2])) for p in range(N_PAIR)]
    new_states = [states[p] * ecl_rows[p] + sts[p] for p in range(N_PAIR)]
    return y, new_states, dict(xv=xv, Bs=Bs, Cs=Cs, X=X, Xb=Xb, CB=CB, Lms=Lms, Ms=Ms, Mb=Mb, prev_b=prev_b, yo=yo,
                               ecl_rows=ecl_rows)


def _ssd_in_specs(nc, rev=False):
    rb = (lambda b, c: b * nc + nc - 1 - c) if rev else (lambda b, c: b * nc + c)
    vec = pl.BlockSpec((1, 128), lambda b, c: (0, 0))
    return [pl.BlockSpec((Q, SSD_CONV_CH), lambda b, c: (rb(b, c), 0)),
            pl.BlockSpec((Q, 128), lambda b, c: (rb(b, c), 0)),
            pl.BlockSpec((Q, SSD_INNER), lambda b, c: (rb(b, c), 0)),
            vec, vec, vec, pl.BlockSpec((1, SSD_INNER), lambda b, c: (0, 0))]


def _ssd_fwd(xc, dtr, proj, bias_p, alog_p, d_p, nw, Bl, nc):
    M = xc.shape[0]

    def body(xc_ref, dtr_ref, z_ref, bias_ref, alog_ref, d_ref, nw_ref, y_ref, prev_ref, state):
        c = pl.program_id(1)

        @pl.when(c == 0)
        def _():
            state[...] = jnp.zeros_like(state)

        s = _ssd_prep(c, dtr_ref, bias_ref, alog_ref, d_ref)
        states = [state[p] for p in range(N_PAIR)]
        y, new_states, _ = _ssd_chunk(xc_ref, s, states)
        for p in range(N_PAIR):
            prev_ref[0, 0, p] = states[p]
            state[p] = new_states[p]
        zz = z_ref[...].astype(F32)
        yg = y * zz * _sigmoid(zz)
        r = _per_group(lambda a: lax.rsqrt(jnp.mean(a * a, axis=-1, keepdims=True) + EPS), yg)
        y_ref[...] = (yg * r * nw_ref[...]).astype(y_ref.dtype)

    return pl.pallas_call(
        body, name="ssd_fwd", grid=(Bl, nc), in_specs=_ssd_in_specs(nc),
        out_specs=[pl.BlockSpec((Q, SSD_INNER), lambda b, c: (b * nc + c, 0)),
                   pl.BlockSpec((1, 1, N_PAIR, 128, 128), lambda b, c: (b, c, 0, 0, 0))],
        out_shape=[jax.ShapeDtypeStruct((M, SSD_INNER), BF16), jax.ShapeDtypeStruct((Bl, nc, N_PAIR, 128, 128), F32)],
        scratch_shapes=[pltpu.VMEM((N_PAIR, 128, 128), F32)],
        compiler_params=_params(("arbitrary", "arbitrary")),
    )(xc, dtr, proj, bias_p, alog_p, d_p, nw)


def _ssd_bwd(xc, dtr, proj, bias_p, alog_p, d_p, nw, prev, dya, dproj, Bl, nc, comm=None):
    M = xc.shape[0]

    def body(xc_ref, dtr_ref, z_ref, bias_ref, alog_ref, d_ref, nw_ref, prev_ref, dy_ref,
             dxc_ref, dz_ref, ddtr_ref, dbias_ref, dalog_ref, dd_ref, dnw_ref, dS):
        b, t = pl.program_id(0), pl.program_id(1)

        @pl.when(t == 0)
        def _():
            dS[...] = jnp.zeros_like(dS)

        s = _ssd_prep(nc - 1 - t, dtr_ref, bias_ref, alog_ref, d_ref)
        states = [prev_ref[0, 0, p] for p in range(N_PAIR)]
        y, _, k = _ssd_chunk(xc_ref, s, states)
        xv, Bs, Cs, Xb = k["xv"], k["Bs"], k["Cs"], k["Xb"]

        zz = z_ref[...].astype(F32)
        sz = _sigmoid(zz)
        silu_z = zz * sz
        yg = y * silu_z
        r = _per_group(lambda a: lax.rsqrt(jnp.mean(a * a, axis=-1, keepdims=True) + EPS), yg)
        xhat = yg * r
        dout = dy_ref[...].astype(F32)
        gw = dout * nw_ref[...]
        dyg = r * (gw - xhat * _per_group(lambda a, c2: jnp.mean(a * c2, axis=-1, keepdims=True), gw, xhat))
        dnw = jnp.sum(dout * xhat, axis=0, keepdims=True)
        dz_ref[...] = (dyg * y * _dsilu(zz, sz)).astype(dz_ref.dtype)
        dy = dyg * silu_z
        dy0 = jnp.where(s["first"], dy, 0.0)
        dyb = (dy0.astype(BF16), (dy - dy0).astype(BF16))
        dYo = (dy * s["ecs_w"]).astype(BF16)

        dS_f = [dS[p] for p in range(N_PAIR)]
        dS_b = [d.astype(BF16) for d in dS_f]
        BdS, dXm, dprev, dCs, dMs, XdS = [], [], [], [[] for _ in range(SSD_GROUPS)], [], []
        for p in range(N_PAIR):
            g, ln = p // 2, slice(128 * p, 128 * (p + 1))
            BdS.append(_dot(Bs[g], dS_b[p], NT))
            dXm.append(_dot(k["Mb"][2 * p], dyb[0][:, ln], TN) + _dot(k["Mb"][2 * p + 1], dyb[1][:, ln], TN))
            dprev.append(_dot(dYo[:, ln], Cs[g], TN))
            dCs[g].append(_dot(dYo[:, ln], k["prev_b"][p], NN))
            for hh in range(2):
                dMs.append(_dot(dyb[hh][:, ln], Xb[hh][:, ln], NT))
                XdS.append(_dot(Xb[hh][:, ln], dS_b[p], NN))
        dX = jnp.concatenate(dXm, axis=1) + s["decay_w"] * jnp.concatenate(BdS, axis=1)
        dxs = dy * s["d_w"] + dX * s["dt_w"]

        sums = _dot01(jnp.concatenate([dX * xv, dy * k["yo"] * s["ecs_w"], dy * xv], axis=0), s["spread"], NT, "b")
        ddt, dcs = sums[0:Q], sums[Q:2 * Q]
        dD = jnp.sum(sums[2 * Q:3 * Q], axis=0, keepdims=True)

        col, row = s["col"], s["row"]
        lane1 = col[0:1]
        rowsT = lax.broadcasted_iota(jnp.int32, (128, Q), 0)
        dcs_t = jnp.zeros((128, Q), F32)
        dcl = jnp.zeros((1, 128), F32)
        dB_out, dC_out = [], []
        for g in range(SSD_GROUPS):
            Bf = Bs[g].astype(F32)
            dCB = jnp.zeros((Q, Q), F32)
            dBacc = jnp.zeros((Q, 128), F32)
            for r4 in range(HPG):
                h = HPG * g + r4
                p, hh = h // 2, h % 2
                W = dMs[h] * k["Ms"][h]
                dCB = dCB + dMs[h] * k["Lms"][h]
                decay_h = s["decay_w"][:, SSD_HEAD_DIM * h:SSD_HEAD_DIM * h + 1]
                dBacc = dBacc + decay_h * XdS[h]
                tdec = jnp.sum(XdS[h] * Bf, axis=1, keepdims=True) * decay_h
                dcs = dcs + jnp.where(col == h, jnp.sum(W, axis=1, keepdims=True) - tdec, 0.0)
                dcs_t = dcs_t - jnp.where(rowsT == h, jnp.sum(W, axis=0, keepdims=True), 0.0)
                rows_h = (row < SSD_HEAD_DIM) if hh == 0 else (row >= SSD_HEAD_DIM)
                sprev = jnp.sum(jnp.sum(jnp.where(rows_h, dS_f[p] * states[p], 0.0), axis=1, keepdims=True),
                                axis=0, keepdims=True)
                ecl = jnp.exp(s["cs"][Q - 1:Q, h:h + 1])
                dcl = dcl + jnp.where(lane1 == h, jnp.sum(tdec, axis=0, keepdims=True) + ecl * sprev, 0.0)
            dCB_b = dCB.astype(BF16)
            dC_out.append(dCs[g][0] + dCs[g][1] + _dot(dCB_b, Bs[g], NN))
            dB_out.append(dBacc + _dot(dCB_b, Cs[g], TN))
        for p in range(N_PAIR):
            dS[p] = dS_f[p] * k["ecl_rows"][p] + dprev[p]
        dxc_ref[...] = jnp.concatenate([dxs] + dB_out + dC_out, axis=1).astype(dxc_ref.dtype)

        dcs = dcs + _dot01(s["eye"], dcs_t, NT, "a") + jnp.where(row == Q - 1, dcl, 0.0)
        da = _dot01(row <= col, dcs, NN, "a")
        ddt = ddt + da * s["A"]
        dpre = jnp.where(s["valid"], ddt * _sigmoid(s["pre"]), 0.0)
        ddtr_ref[...] = dpre
        dbias = jnp.sum(dpre, axis=0, keepdims=True)
        dalog = jnp.sum(da * s["dt"], axis=0, keepdims=True) * s["A"]
        first_step = jnp.logical_and(b == 0, t == 0)

        @pl.when(first_step)
        def _():
            dbias_ref[...] = dbias
            dalog_ref[...] = dalog
            dd_ref[...] = dD
            dnw_ref[...] = dnw

        @pl.when(jnp.logical_not(first_step))
        def _():
            dbias_ref[...] += dbias
            dalog_ref[...] += dalog
            dd_ref[...] += dD
            dnw_ref[...] += dnw

    rb = lambda b, c: b * nc + nc - 1 - c
    rowblk = lambda w: pl.BlockSpec((Q, w), lambda b, c: (rb(b, c), 0))
    vec = lambda w: pl.BlockSpec((1, w), lambda b, c: (0, 0))
    return _call(
        body, name="ssd_bwd", grid=(Bl, nc),
        in_specs=_ssd_in_specs(nc, rev=True) + [
            pl.BlockSpec((1, 1, N_PAIR, 128, 128), lambda b, c: (b, nc - 1 - c, 0, 0, 0)), rowblk(SSD_INNER)],
        out_specs=[rowblk(SSD_CONV_CH), rowblk(SSD_INNER), rowblk(128), vec(128), vec(128), vec(128), vec(SSD_INNER)],
        out_shape=[jax.ShapeDtypeStruct((M, SSD_CONV_CH), BF16), jax.ShapeDtypeStruct(dproj.shape, BF16),
                   jax.ShapeDtypeStruct((M, 128), F32), jax.ShapeDtypeStruct((1, 128), F32),
                   jax.ShapeDtypeStruct((1, 128), F32), jax.ShapeDtypeStruct((1, 128), F32),
                   jax.ShapeDtypeStruct((1, SSD_INNER), F32)],
        scratch=[pltpu.VMEM((N_PAIR, 128, 128), F32)], sem=("arbitrary", "arbitrary"),
        args=(xc, dtr, proj, bias_p, alog_p, d_p, nw, prev, dya), comm=comm, into=(dproj, 1))


NSUB = Q // HG_CHUNK
HG_HP = 8
EXP_CAP = 80.0


def _hg_setup(blk, q_ref, f_ref, hb_ref):
    row = lax.broadcasted_iota(jnp.int32, (Q, Q), 0)
    col = lax.broadcasted_iota(jnp.int32, (Q, Q), 1)
    same = (row // HG_CHUNK) == (col // HG_CHUNK)
    causal = jnp.logical_and(same, col <= row)
    lb = _sigmoid(hb_ref[0:1, :] - hb_ref[1:2, :])
    fl = f_ref[...].astype(F32)
    sg = _sigmoid(fl)
    fg = lb + (1.0 - lb) * sg
    k = (1.0 - lb) * (1.0 - sg)
    gl = jnp.log(fg)
    G = _dot01(causal, gl, NN, "a")
    T = _dot01(same, gl, NN, "a")
    qv = q_ref[...].astype(F32)
    sq = _sigmoid(qv)
    eG = jnp.exp(G)
    eGn = jnp.exp(jnp.minimum(-G, EXP_CAP))
    eTG = jnp.exp(T - G)
    qt = qv * sq * eG
    kt = k * eGn
    kh = k * eTG
    valid = jnp.logical_or(blk > 0, row[:, :1] >= PAD)
    return dict(row=row, col=col, same=same, causal=causal, lb=lb, sg=sg, fg=fg, k=k, T=T, qv=qv, sq=sq,
                eG=eG, eGn=eGn, eTG=eTG, qt=qt, kt=kt, kh=kh, valid=valid)


def _hg_specs(nb, rev=False):
    rb = (lambda h, b, t: b * nb + nb - 1 - t) if rev else (lambda h, b, t: b * nb + t)
    w = 128 * HG_HP
    blk = lambda off: pl.BlockSpec((Q, w), lambda h, b, t, off=off: (rb(h, b, t), off // HG_HP + h))
    return [blk(24), blk(32), blk(40), blk(48),
            pl.BlockSpec((2, w), lambda h, b, t: (0, h)), pl.BlockSpec((1, w), lambda h, b, t: (0, h))]


HEAD_LANES = tuple(slice(128 * hh, 128 * (hh + 1)) for hh in range(HG_HP))


def _per_head(fn, *arrs):
    return jnp.concatenate([jnp.broadcast_to(fn(*(a[:, ln] for a in arrs)), (arrs[0].shape[0], 128))
                            for ln in HEAD_LANES], axis=1)


def _hgrn_fwd(proj, hb, nw, Bl, nb, comm=None):
    M = proj.shape[0]

    def body(q_ref, f_ref, i_ref, g_ref, hb_ref, nw_ref, y_ref, o_ref, st_ref, S):
        blk = pl.program_id(2)

        @pl.when(blk == 0)
        def _():
            S[...] = jnp.zeros_like(S)

        s = _hg_setup(blk, q_ref, f_ref, hb_ref)
        v = i_ref[...]
        qt_b, kt_b, kh_b = s["qt"].astype(BF16), s["kt"].astype(BF16), s["kh"].astype(BF16)
        eT = jnp.exp(s["T"])
        att = [jnp.where(s["causal"], _dot(qt_b[:, ln], kt_b[:, ln], NT), 0.0).astype(BF16) for ln in HEAD_LANES]
        o_intra = [_dot(att[hh], v[:, ln], NN) for hh, ln in enumerate(HEAD_LANES)]
        for j in range(NSUB):
            sl = slice(HG_CHUNK * j, HG_CHUNK * (j + 1))
            for hh, ln in enumerate(HEAD_LANES):
                St = S[hh]
                st_ref[0, hh, 0, j] = St
                o_ref[sl, ln] = o_intra[hh][sl] + _dot(qt_b[sl, ln], St.astype(BF16), NT)
                S[hh] = St * eT[HG_CHUNK * j:HG_CHUNK * j + 1, ln] + _dot(v[sl, ln], kh_b[sl, ln], TN)
        o = o_ref[...]
        r = _per_head(lambda a: lax.rsqrt(jnp.mean(a * a, axis=-1, keepdims=True) + EPS), o)
        gv = g_ref[...].astype(F32)
        y_ref[...] = (o * r * nw_ref[...] * gv * _sigmoid(gv)).astype(y_ref.dtype)

    rowblk = pl.BlockSpec((Q, 128 * HG_HP), lambda h, b, t: (b * nb + t, h))
    return _call(
        body, name="hgrn_fwd", grid=(HG_HEADS // HG_HP, Bl, nb), in_specs=_hg_specs(nb),
        out_specs=[rowblk, rowblk,
                   pl.BlockSpec((1, HG_HP, 1, NSUB, 128, 128), lambda h, b, t: (b, h, t, 0, 0, 0))],
        out_shape=[jax.ShapeDtypeStruct((M, HG_WIDTH), BF16), jax.ShapeDtypeStruct((M, HG_WIDTH), F32),
                   jax.ShapeDtypeStruct((Bl, HG_HEADS, nb, NSUB, 128, 128), F32)],
        scratch=[pltpu.VMEM((HG_HP, 128, 128), F32)], sem=("parallel", "arbitrary", "arbitrary"),
        args=(proj, proj, proj, proj, hb, nw), comm=comm)


def _hgrn_bwd(proj, hb, nw, o_saved, st_saved, dyb, dproj, Bl, nb, comm=None):
    assert HG_HP == HG_HEADS

    def body(q_ref, f_ref, i_ref, g_ref, hb_ref, nw_ref, o_ref, st_ref, dy_ref,
             d_ref, dhb_ref, dnw_ref, dS, a_dqt, a_dv, a_dkh, a_dgl):
        b, t = pl.program_id(1), pl.program_id(2)

        @pl.when(t == 0)
        def _():
            dS[...] = jnp.zeros_like(dS)

        first_step = jnp.logical_and(b == 0, t == 0)
        s = _hg_setup(nb - 1 - t, q_ref, f_ref, hb_ref)
        v = i_ref[...]
        qt_b, kt_b, kh_b = s["qt"].astype(BF16), s["kt"].astype(BF16), s["kh"].astype(BF16)
        eT = jnp.exp(s["T"])
        att = [jnp.where(s["causal"], _dot(qt_b[:, ln], kt_b[:, ln], NT), 0.0).astype(BF16) for ln in HEAD_LANES]

        o = o_ref[...]
        r = _per_head(lambda a: lax.rsqrt(jnp.mean(a * a, axis=-1, keepdims=True) + EPS), o)
        xhat = o * r
        gv = g_ref[...].astype(F32)
        sgv = _sigmoid(gv)
        dyv = dy_ref[...].astype(F32)
        d_on = dyv * gv * sgv
        dg_out = dyv * xhat * nw_ref[...] * _dsilu(gv, sgv)
        gw = d_on * nw_ref[...]
        do = r * (gw - xhat * _per_head(lambda a, c: jnp.mean(a * c, axis=-1, keepdims=True), gw, xhat))
        dnw = jnp.sum(d_on * xhat, axis=0, keepdims=True)
        do_b = do.astype(BF16)

        datt = [jnp.where(s["causal"], _dot(do_b[:, ln], v[:, ln], NT), 0.0).astype(BF16) for ln in HEAD_LANES]
        dqt = jnp.concatenate([_dot(datt[hh], kt_b[:, ln], NN) for hh, ln in enumerate(HEAD_LANES)], axis=1)
        dkt = jnp.concatenate([_dot(datt[hh], qt_b[:, ln], TN) for hh, ln in enumerate(HEAD_LANES)], axis=1)
        dv = jnp.concatenate([_dot(att[hh], do_b[:, ln], TN) for hh, ln in enumerate(HEAD_LANES)], axis=1)
        last_row = (lax.broadcasted_iota(jnp.int32, (HG_CHUNK, 128), 0) == HG_CHUNK - 1)
        for j in reversed(range(NSUB)):
            sl = slice(HG_CHUNK * j, HG_CHUNK * (j + 1))
            for hh, ln in enumerate(HEAD_LANES):
                St = st_ref[0, hh, 0, j]
                dSt = dS[hh]
                St_b, dSt_b = St.astype(BF16), dSt.astype(BF16)
                eT_j = eT[HG_CHUNK * j:HG_CHUNK * j + 1, ln]
                dkh_j = _dot(v[sl, ln], dSt_b, NN)
                a_dqt[sl, ln] = _dot(do_b[sl, ln], St_b, NN)
                a_dv[sl, ln] = _dot(kh_b[sl, ln], dSt_b, NT)
                a_dkh[sl, ln] = dkh_j
                dlast = (jnp.sum(St * dSt, axis=0, keepdims=True) * eT_j
                         + jnp.sum(dkh_j * s["kh"][sl, ln], axis=0, keepdims=True))
                a_dgl[sl, ln] = jnp.where(last_row, dlast, 0.0)
                dS[hh] = dSt * eT_j + _dot(do_b[sl, ln], qt_b[sl, ln], TN)
        dqt = dqt + a_dqt[...]
        dv = dv + a_dv[...]
        dkh = a_dkh[...]
        dG = dqt * s["qt"] - dkt * s["kt"] - dkh * s["kh"] + a_dgl[...]
        rev_causal = jnp.logical_and(s["same"], s["col"] >= s["row"])
        dgl = _dot01(rev_causal, dG, NN, "a")
        dk = dkt * s["eGn"] + dkh * s["eTG"]
        dfg = dgl / s["fg"] - dk
        lb, sg = s["lb"], s["sg"]
        keep = s["valid"].astype(F32)
        d_ref[:, 0:w] = (dqt * s["eG"] * _dsilu(s["qv"], s["sq"]) * keep).astype(d_ref.dtype)
        d_ref[:, w:2 * w] = (dfg * (1.0 - lb) * sg * (1.0 - sg) * keep).astype(d_ref.dtype)
        d_ref[:, 2 * w:3 * w] = (dv * keep).astype(d_ref.dtype)
        d_ref[:, 3 * w:4 * w] = (dg_out * keep).astype(d_ref.dtype)
        dlb = jnp.sum(dfg * (1.0 - sg) * keep, axis=0, keepdims=True) * lb * (1.0 - lb)
        dhb = jnp.concatenate([dlb, -dlb], axis=0)

        @pl.when(first_step)
        def _():
            dhb_ref[...] = dhb
            dnw_ref[...] = dnw

        @pl.when(jnp.logical_not(first_step))
        def _():
            dhb_ref[...] += dhb
            dnw_ref[...] += dnw

    w = 128 * HG_HP
    rowblk = pl.BlockSpec((Q, w), lambda h, b, t: (b * nb + nb - 1 - t, h))
    return _call(
        body, name="hgrn_bwd", grid=(HG_HEADS // HG_HP, Bl, nb),
        in_specs=_hg_specs(nb, rev=True) + [
            rowblk, pl.BlockSpec((1, HG_HP, 1, NSUB, 128, 128), lambda h, b, t: (b, h, nb - 1 - t, 0, 0, 0)), rowblk],
        out_specs=[pl.BlockSpec((pl.Element(Q), pl.Element(4 * w)),
                                lambda h, b, t: (pl.multiple_of((b * nb + nb - 1 - t) * Q, Q), 3 * HG_WIDTH)),
                   pl.BlockSpec((2, w), lambda h, b, t: (0, h)), pl.BlockSpec((1, w), lambda h, b, t: (0, h))],
        out_shape=[jax.ShapeDtypeStruct(dproj.shape, BF16),
                   jax.ShapeDtypeStruct((2, HG_WIDTH), F32), jax.ShapeDtypeStruct((1, HG_WIDTH), F32)],
        scratch=[pltpu.VMEM((HG_HP, 128, 128), F32)] + [pltpu.VMEM((Q, w), F32)] * 4,
        sem=("parallel", "arbitrary", "arbitrary"),
        args=(proj, proj, proj, proj, hb, nw, o_saved, st_saved, dyb), comm=comm, into=(dproj, 0))


def _adamw(name, parts, w, m, v):
    R, C = w.shape
    S = parts.shape[0]
    tr, tc = (_tile(R, (256, 176, 128, 64, 8)), C) if R % 8 == 0 else (R, 256)
    c1, c2 = 1.0 - ADAM_B1 ** ADAM_STEP, 1.0 - ADAM_B2 ** ADAM_STEP

    def body(p_ref, w_ref, m_ref, v_ref, g_ref, d_ref, nm_ref, nv_ref):
        g = p_ref[0].astype(F32)
        for s in range(1, S):
            g = g + p_ref[s].astype(F32)
        nm = ADAM_B1 * m_ref[...] + (1.0 - ADAM_B1) * g
        nv = ADAM_B2 * v_ref[...] + (1.0 - ADAM_B2) * (g * g)
        g_ref[...] = g
        nm_ref[...] = nm
        nv_ref[...] = nv
        d_ref[...] = -ADAM_LR * ((nm / c1) / (jnp.sqrt(nv / c2) + ADAM_EPS) + ADAM_WD * w_ref[...])

    blk = pl.BlockSpec((tr, tc), lambda i, j: (i, j))
    return pl.pallas_call(
        body, name=name, grid=(R // tr, C // tc),
        in_specs=[pl.BlockSpec((S, tr, tc), lambda i, j: (0, i, j)), blk, blk, blk], out_specs=[blk] * 4,
        out_shape=[jax.ShapeDtypeStruct((R, C), F32)] * 4, compiler_params=_params(("parallel", "parallel")),
    )(parts, w, m, v)


def _pair_sum(name, by_core, arrived):
    _, J, R, C = by_core.shape
    tc = _tile(C, (512, 256, 128))

    def body(c_ref, a_ref, b_ref, o_ref):
        o_ref[...] = (a_ref[0].astype(F32) + b_ref[...].astype(F32)).astype(o_ref.dtype)

    blk = pl.BlockSpec((1, R, tc), lambda j, k, c_ref: (j, 0, k))
    return pl.pallas_call(
        body, name=name,
        grid_spec=pltpu.PrefetchScalarGridSpec(
            num_scalar_prefetch=1, grid=(J, C // tc),
            in_specs=[pl.BlockSpec((1, 1, R, tc), lambda j, k, c_ref: (c_ref[0], j, 0, k)), blk], out_specs=blk),
        out_shape=jax.ShapeDtypeStruct(arrived.shape, arrived.dtype), compiler_params=_params(("parallel", "parallel")),
    )(lax.axis_index("c").astype(jnp.int32).reshape(1), by_core, arrived)


def _sum_parts(name, parts):
    S, R, C = parts.shape

    def body(p_ref, o_ref):
        g = p_ref[0]
        for s in range(1, S):
            g = g + p_ref[s]
        o_ref[...] = g

    return pl.pallas_call(
        body, name=name, out_shape=jax.ShapeDtypeStruct((R, C), F32),
        in_specs=[pl.BlockSpec(memory_space=pltpu.VMEM)], out_specs=pl.BlockSpec(memory_space=pltpu.VMEM),
    )(parts)


def _heads_to_lanes(p):
    return jnp.pad(p, [(0, 0)] * (p.ndim - 1) + [(0, 128 - SSD_HEADS)])


def _lanes_to_heads(p):
    return p[..., :SSD_HEADS]


def _pack_rows(arrs):
    rows = []
    for a in arrs:
        f = a.reshape(-1).astype(F32)
        n = -(-f.shape[0] // D_MODEL) * D_MODEL
        rows.append(jnp.pad(f, (0, n - f.shape[0])).reshape(-1, D_MODEL))
    out = jnp.concatenate(rows, axis=0)
    return jnp.pad(out, ((0, (-out.shape[0]) % 8), (0, 0)))


def _unpack_rows(packed, like):
    outs, r = [], 0
    for a in like:
        n = 1
        for s in a.shape:
            n *= s
        nr = -(-n // D_MODEL)
        outs.append(packed[r:r + nr].reshape(-1)[:n].reshape(a.shape))
        r += nr
    return outs


def _cols(gth):
    return jnp.transpose(gth, (1, 0, 2)).reshape(gth.shape[1], -1)


def _rows(gth):
    return gth.reshape(-1, gth.shape[2])


def _to_rows(g):
    return g.reshape(N_DEV, -1, g.shape[1]).astype(BF16)


def _by_core(g):
    return jnp.transpose(g.reshape(N_DEV // 2, 2, -1, g.shape[1]), (1, 0, 2, 3)).astype(BF16)


DT_ROW = 3072


def _chip_sums(tag, by_core, swap_in=None):
    arrived = swap_in(by_core) if swap_in else _exchange(tag + "_swap", "swap", by_core)
    return [_pair_sum(f"{tag}_chipsum{i}", m, a) for i, (m, a) in enumerate(zip(by_core, arrived))]


def _ffn_fwd_gu(tag, n, w_gu_t, comm=None):
    M = n.shape[0]
    F = w_gu_t.shape[0] // 2
    tm = _tile(M, (544, 256))
    outs = _fused_matmul(
        tag + "_gu", M, F, D_MODEL,
        [dict(a=n, b=w_gu_t, trans_b=True, acc=0, resident=True),
         dict(a=n, b=w_gu_t, trans_b=True, bn_off=1, acc=1, resident=True)], [],
        lambda accs, ex: (accs[0], accs[1], accs[0] * _sigmoid(accs[0]) * accs[1]),
        [BF16, BF16, BF16], 2, tm, F, D_MODEL, outer="i", comm=comm, sub=256)
    return (n, *outs[:3]), outs[3:]


def _rmsnorm_tile(x, w):
    return x * lax.rsqrt(jnp.mean(x * x, axis=-1, keepdims=True) + EPS) * w


def _ffn_fwd_down(tag, h, a, w_down, next_norm=None, comm=None):
    M = h.shape[0]
    F = w_down.shape[0]
    tm = _tile(M, (1088, 544, 256))
    if next_norm is None:
        (h_out,) = _fused_matmul(
            tag + "_down", M, D_MODEL, F, [dict(a=a, b=w_down, acc=0)], [(h, 0)],
            lambda accs, ex: (ex[0] + 0.5 * accs[0],), [F32], 1, tm, D_MODEL, F, outer="j", sub=256)
        return h_out

    def with_norm(accs, ex):
        h_new = ex[0] + 0.5 * accs[0]
        return h_new, _rmsnorm_tile(h_new, ex[1])

    return _fused_matmul(tag + "_down", M, D_MODEL, F, [dict(a=a, b=w_down, acc=0, resident=True)], [(h, 0)], with_norm,
                         [F32, BF16], 1, tm, D_MODEL, F, outer="j", vecs=[next_norm], comm=comm)


def _ffn_bwd(tag, dh, dh_b, h, norm_w, w_gu_t, w_down, saved, scatter=False):
    n, g, u, a = saved
    M = h.shape[0]
    F = w_down.shape[0]
    tm = _tile(M, (544, 256))
    tn = _tile(F, (1408, 704, 256))

    def swiglu_bwd(accs, ex):
        da, gv, uv = 0.5 * accs[0], ex[0].astype(F32), ex[1].astype(F32)
        s = _sigmoid(gv)
        return da * uv * _dsilu(gv, s), da * gv * s

    (dgu,) = _fused_matmul(
        tag + "_dact", M, F, D_MODEL, [dict(a=dh_b, b=w_down, trans_b=True, acc=0, resident=True)], [(g, 0), (u, 0)],
        swiglu_bwd, [BF16, BF16], 1, tm, F, D_MODEL, outer="i", stack=True, sub=256)
    tr = _tile(M, (2176, 256))
    (dw_down,) = _matmul_tn(tag + "_dwd", a, dh_b, tn, D_MODEL, tr, scale=0.5)
    dw_gu_t, *p_down = _matmul_tn(tag + "_dwgu", dgu, n, tn, D_MODEL, tr,
                                  comm=("scatter", [_to_rows(dw_down)]) if scatter else None)
    comm = None
    if scatter:
        comm = ("chips", _chip_sums(tag + "_wgu", [_by_core(dw_gu_t)]))
    def norm_bwd(accs, ex):
        dh_prev, dw = _rmsnorm_bwd_tile(accs[0], ex[0], ex[2], ex[1])
        return dh_prev, dh_prev, dw

    dh_prev, dh_prev_b, dnorm, *p_gu = _fused_matmul(
        tag + "_dn", M, D_MODEL, F,
        [dict(a=dgu, a_lead=0, b=w_gu_t, acc=0, resident=True),
         dict(a=dgu, a_lead=1, b=w_gu_t, bk_off=1, acc=0, resident=True)], [(h, 0), (dh, 0)],
        norm_bwd, [F32, BF16], 1, tm, D_MODEL, F, outer="i", comm=comm, vecs=[norm_w], row_sums=1)
    return (dh_prev, dh_prev_b, dnorm, *((p_gu[0], p_down[0]) if scatter else (dw_gu_t, dw_down)))


def kernel(x, meta_tokens, ffn1_norm, ffn1_w_gu, ffn1_w_down, mix_norm, w_in, ssd_conv_w, ssd_conv_b, ssd_dt_bias, ssd_a_log, ssd_d, ssd_norm, hg_lower_bound, hg_norm, w_branch_a, w_branch_b, w_out, ffn2_norm, ffn2_w_gu, ffn2_w_down, final_norm, loss_target, m_meta_tokens, m_ffn1_norm, m_ffn1_w_gu, m_ffn1_w_down, m_mix_norm, m_w_in, m_ssd_conv_w, m_ssd_conv_b, m_ssd_dt_bias, m_ssd_a_log, m_ssd_d, m_ssd_norm, m_hg_lower_bound, m_hg_norm, m_w_branch_a, m_w_branch_b, m_w_out, m_ffn2_norm, m_ffn2_w_gu, m_ffn2_w_down, m_final_norm, v_meta_tokens, v_ffn1_norm, v_ffn1_w_gu, v_ffn1_w_down, v_mix_norm, v_w_in, v_ssd_conv_w, v_ssd_conv_b, v_ssd_dt_bias, v_ssd_a_log, v_ssd_d, v_ssd_norm, v_hg_lower_bound, v_hg_norm, v_w_branch_a, v_w_branch_b, v_w_out, v_ffn2_norm, v_ffn2_w_gu, v_ffn2_w_down, v_final_norm):
    Bl, S, D = x.shape
    T = PAD + N_META + S
    nc = T // Q
    M = Bl * T
    me = 4 * lax.axis_index("x") + 2 * lax.axis_index("y") + lax.axis_index("c")

    bf = lambda a: a[0].astype(BF16)
    bft = lambda a: a[0].T.astype(BF16)
    g_meta, g_conv_w = _exchange("gather_small", "gather", [meta_tokens, ssd_conv_w[0]])
    meta_full, conv_w_full = _cols(g_meta), _cols(g_conv_w)
    bias_p, alog_p, d_p = _heads_to_lanes(ssd_dt_bias), _heads_to_lanes(ssd_a_log), _heads_to_lanes(ssd_d)
    final_w = final_norm.reshape(1, D)

    h0, n1, g_wgu1 = _embed_norm(x, meta_full, ffn1_norm, comm=("gather", [bft(ffn1_w_gu)]))
    wgu1 = _rows(g_wgu1)
    tm = _tile(M, (1088, 544, 256))
    win_shard = bft(w_in)
    cut = (win_shard.shape[0] // 32) * 16
    ffn1_saved, (g_wd1, g_win_a) = _ffn_fwd_gu("ffn1", n1, wgu1, comm=("gather", [bf(ffn1_w_down), win_shard[:cut]]))
    wd1 = _rows(g_wd1)
    h1, un, g_win_b = _ffn_fwd_down("ffn1", h0, ffn1_saved[3], wd1, next_norm=mix_norm,
                                    comm=("gather", [win_shard[cut:]]))
    win_t = _rows(jnp.concatenate([g_win_a, g_win_b], axis=1))
    win_dt = jnp.pad(win_t[DT_ROW:DT_ROW + SSD_HEADS], ((0, 128 - SSD_HEADS), (0, 0)))
    plain = lambda accs, ex: (accs[0],)
    proj, g_wa, g_wb, g_wo = _fused_matmul(
        "in_proj", M, N_MAIN, D, [dict(a=un, b=win_t, trans_b=True, acc=0, b_shift=(DT_ROW // 1536, SSD_HEADS))], [],
        plain, [BF16], 1, tm, 1536, D,
        outer="j", comm=("gather", [bf(w_branch_a), bf(w_branch_b), bf(w_out)], "early"), sub=512)
    wa, wb, wo = _rows(g_wa), _rows(g_wb), _rows(g_wo)
    (dtr,) = _fused_matmul("in_proj_dt", M, 128, D, [dict(a=un, b=win_dt, trans_b=True, acc=0)], [], plain, [F32], 1,
                           tm, 128, D, outer="j")
    xc = _conv_fwd(proj, conv_w_full, ssd_conv_b, Bl, T)
    ya, ssd_prev = _ssd_fwd(xc, dtr, proj, bias_p, alog_p, d_p, ssd_norm, Bl, nc)
    yb, hg_o, hg_st, g_wgu2, g_wd2 = _hgrn_fwd(proj, hg_lower_bound, hg_norm, Bl, nc,
                                               comm=("gather", [bft(ffn2_w_gu), bf(ffn2_w_down)]))
    wgu2, wd2 = _rows(g_wgu2), _rows(g_wd2)

    def branch_fwd(accs, ex):
        pa, pb = accs
        return pa, pb, _sigmoid(ex[0].astype(F32)) * pa + _sigmoid(ex[1].astype(F32)) * pb

    pa, pb, merged = _fused_matmul(
        "branches", M, D, D, [dict(a=ya, b=wa, acc=0), dict(a=yb, b=wb, acc=1)], [(proj, 7), (proj, 8)],
        branch_fwd, [BF16, BF16, BF16], 2, tm, D, D, outer="j", sub=256)
    def out_with_norm(accs, ex):
        h_new = ex[0] + accs[0]
        return h_new, _rmsnorm_tile(h_new, ex[1])

    h2, n2 = _fused_matmul("out_proj", M, D, D, [dict(a=merged, b=wo, acc=0)], [(h1, 0)], out_with_norm,
                           [F32, BF16], 1, tm, D, D, outer="j", vecs=[ffn2_norm])
    ffn2_saved, _ = _ffn_fwd_gu("ffn2", n2, wgu2)
    h3 = _ffn_fwd_down("ffn2", h2, ffn2_saved[3], wd2)

    dh3, dh3_b, d_final, loss_part = _loss_head(h3, final_w, loss_target, Bl, nc)
    dh2, dh2_b, d_ffn2_norm, d_wgu2, d_wd2 = _ffn_bwd("ffn2", dh3, dh3_b, h2, ffn2_norm, wgu2, wd2, ffn2_saved)

    def branch_bwd(accs, ex):
        dm = accs[0]
        ga, gb, pav, pbv = (e.astype(F32) for e in ex)
        sa, sb = _sigmoid(ga), _sigmoid(gb)
        return (dm * sa, dm * sb,
                jnp.concatenate([dm * pav * sa * (1.0 - sa), dm * pbv * sb * (1.0 - sb)], axis=1))

    d_merged_outs = []

    def d_merged_with_swap(theirs):
        d_merged_outs.extend(_fused_matmul(
            "d_merged", M, D, D, [dict(a=dh2_b, b=wo, trans_b=True, acc=0)], [(proj, 7), (proj, 8), (pa, 0), (pb, 0)],
            branch_bwd, [BF16] * 2, 1, tm, D, D, outer="j", comm=("swap", theirs),
            wide=dict(width=2 * D, col=7 * D, total=N_MAIN, dtype=BF16)))
        return d_merged_outs[3:]

    s_ffn2 = _chip_sums("ffn2", [_by_core(d_wgu2), _by_core(d_wd2)], swap_in=d_merged_with_swap)
    dpa, dpb, dproj = d_merged_outs[:3]
    (d_wo,) = _matmul_tn("d_w_out", merged, dh2_b, 512, D, M)
    (d_wa,) = _matmul_tn("d_w_a", ya, dpa, 512, D, M)
    (d_wb,) = _matmul_tn("d_w_b", yb, dpb, 512, D, M)
    dya, dyb = _fused_matmul(
        "d_branches", M, D, D, [dict(a=dpa, b=wa, trans_b=True, acc=0), dict(a=dpb, b=wb, trans_b=True, acc=1)], [],
        lambda accs, ex: (accs[0], accs[1]), [BF16, BF16], 2, tm, D, D, outer="j")
    *ssd_grads, p_wgu2, p_wd2 = _ssd_bwd(xc, dtr, proj, bias_p, alog_p, d_p, ssd_norm, ssd_prev, dya, dproj, Bl, nc,
                                         comm=("chips", s_ffn2))
    dxc, dproj, ddtr, d_bias_p, d_alog_p, d_d_p, d_ssd_norm = ssd_grads
    dproj, d_conv_w, d_conv_b = _conv_bwd(proj, conv_w_full, ssd_conv_b, dxc, dproj, Bl, T)
    dproj, d_hb, d_hg_norm, p_wa, p_wb, p_wo = _hgrn_bwd(
        proj, hg_lower_bound, hg_norm, hg_o, hg_st, dyb, dproj, Bl, nc,
        comm=("scatter", [_to_rows(d_wa), _to_rows(d_wb), _to_rows(d_wo)]))
    ddtr_b = ddtr.astype(BF16)
    (d_win_t,) = _matmul_tn("d_w_in", dproj, un, 768, D, M, out_skip=(DT_ROW, SSD_HEADS))
    (d_win_dt,) = _matmul_tn("d_w_in_dt", ddtr_b, un, 128, D, M)
    d_win_t = lax.dynamic_update_slice(d_win_t, d_win_dt[:SSD_HEADS], (DT_ROW, 0))
    d_un_dt_outs = []

    def d_un_dt_with_swap(theirs):
        d_un_dt_outs.extend(_fused_matmul("d_un_dt", M, D, 128, [dict(a=ddtr_b, b=win_dt, acc=0)], [], plain, [F32], 1,
                                          tm, D, 128, outer="j", comm=("swap", theirs)))
        return d_un_dt_outs[1:]

    s_win = _chip_sums("w_in", [_by_core(d_win_t)], swap_in=d_un_dt_with_swap)
    def mix_norm_bwd(accs, ex):
        dh, dw = _rmsnorm_bwd_tile(accs[0] + ex[0], ex[1], ex[3], ex[2])
        return dh, dh, dw

    dh1, dh1_b, d_mix_norm, p_win = _fused_matmul(
        "d_un", M, D, N_MAIN, [dict(a=dproj, b=win_t, acc=0, b_shift=(DT_ROW // 3072, SSD_HEADS))],
        [(d_un_dt_outs[0], 0), (h1, 0), (dh2, 0)],
        mix_norm_bwd, [F32, BF16], 1, _tile(M, (544, 256)), D, 3072, outer="i", comm=("chips", s_win),
        vecs=[mix_norm], row_sums=1)
    dh0, _, d_ffn1_norm, p_wgu1, p_wd1 = _ffn_bwd("ffn1", dh1, dh1_b, h0, ffn1_norm, wgu1, wd1, ffn1_saved, scatter=True)

    dh0 = dh0.reshape(Bl, T, D)
    grad_x = dh0[:, PAD + N_META:]
    d_meta = dh0[:, PAD:PAD + N_META]

    small_grads = [d_ffn1_norm, d_mix_norm, d_conv_b, _lanes_to_heads(d_bias_p), _lanes_to_heads(d_alog_p),
                   _lanes_to_heads(d_d_p), d_ssd_norm, d_hb, d_hg_norm, d_ffn2_norm, d_final.reshape(D), d_conv_w]
    small_packed = _pack_rows(small_grads + [d_meta[b] for b in range(Bl)])
    parts = [p_wgu1, p_wd1, p_win, p_wa, p_wb, p_wo, p_wgu2, p_wd2]
    (small_all,) = _exchange("gather_small_grads", "gather", [small_packed])
    small_sum = _sum_parts("sum_small_grads", small_all)
    unpacked = _unpack_rows(small_sum, small_grads + [d_meta[b] for b in range(Bl)])
    g_small = unpacked[:len(small_grads)]
    g_meta_full = unpacked[len(small_grads)]
    for b in range(1, Bl):
        g_meta_full = g_meta_full + unpacked[len(small_grads) + b]
    g_meta = lax.dynamic_slice_in_dim(g_meta_full, me * (D // N_DEV), D // N_DEV, axis=1)
    g_conv_w = lax.dynamic_slice_in_dim(g_small[11], me * (SSD_CONV_CH // N_DEV), SSD_CONV_CH // N_DEV, axis=1)

    names = ["meta_tokens", "ffn1_norm", "ffn1_w_gu", "ffn1_w_down", "mix_norm", "w_in", "ssd_conv_w", "ssd_conv_b",
             "ssd_dt_bias", "ssd_a_log", "ssd_d", "ssd_norm", "hg_lower_bound", "hg_norm", "w_branch_a", "w_branch_b",
             "w_out", "ffn2_norm", "ffn2_w_gu", "ffn2_w_down", "final_norm"]
    W = dict(meta_tokens=meta_tokens, ffn1_norm=ffn1_norm, ffn1_w_gu=ffn1_w_gu, ffn1_w_down=ffn1_w_down, mix_norm=mix_norm,
             w_in=w_in, ssd_conv_w=ssd_conv_w, ssd_conv_b=ssd_conv_b, ssd_dt_bias=ssd_dt_bias, ssd_a_log=ssd_a_log,
             ssd_d=ssd_d, ssd_norm=ssd_norm, hg_lower_bound=hg_lower_bound, hg_norm=hg_norm, w_branch_a=w_branch_a,
             w_branch_b=w_branch_b, w_out=w_out, ffn2_norm=ffn2_norm, ffn2_w_gu=ffn2_w_gu, ffn2_w_down=ffn2_w_down,
             final_norm=final_norm)
    Mo = dict(meta_tokens=m_meta_tokens, ffn1_norm=m_ffn1_norm, ffn1_w_gu=m_ffn1_w_gu, ffn1_w_down=m_ffn1_w_down,
              mix_norm=m_mix_norm, w_in=m_w_in, ssd_conv_w=m_ssd_conv_w, ssd_conv_b=m_ssd_conv_b, ssd_dt_bias=m_ssd_dt_bias,
              ssd_a_log=m_ssd_a_log, ssd_d=m_ssd_d, ssd_norm=m_ssd_norm, hg_lower_bound=m_hg_lower_bound, hg_norm=m_hg_norm,
              w_branch_a=m_w_branch_a, w_branch_b=m_w_branch_b, w_out=m_w_out, ffn2_norm=m_ffn2_norm, ffn2_w_gu=m_ffn2_w_gu,
              ffn2_w_down=m_ffn2_w_down, final_norm=m_final_norm)
    Vo = dict(meta_tokens=v_meta_tokens, ffn1_norm=v_ffn1_norm, ffn1_w_gu=v_ffn1_w_gu, ffn1_w_down=v_ffn1_w_down,
              mix_norm=v_mix_norm, w_in=v_w_in, ssd_conv_w=v_ssd_conv_w, ssd_conv_b=v_ssd_conv_b, ssd_dt_bias=v_ssd_dt_bias,
              ssd_a_log=v_ssd_a_log, ssd_d=v_ssd_d, ssd_norm=v_ssd_norm, hg_lower_bound=v_hg_lower_bound, hg_norm=v_hg_norm,
              w_branch_a=v_w_branch_a, w_branch_b=v_w_branch_b, w_out=v_w_out, ffn2_norm=v_ffn2_norm, ffn2_w_gu=v_ffn2_w_gu,
              ffn2_w_down=v_ffn2_w_down, final_norm=v_final_norm)
    grads, deltas, new_m, new_v = {}, {}, {}, {}
    big_names = ["ffn1_w_gu", "ffn1_w_down", "w_in", "w_branch_a", "w_branch_b", "w_out", "ffn2_w_gu", "ffn2_w_down"]
    transposed = ("ffn1_w_gu", "ffn2_w_gu", "w_in")
    for nm, part in zip(big_names, parts):
        view = (lambda a: a[0].T) if nm in transposed else (lambda a: a[0])
        back = (lambda o: o.T[None]) if nm in transposed else (lambda o: o[None])
        outs = _adamw("adamw_" + nm, part, view(W[nm]), view(Mo[nm]), view(Vo[nm]))
        grads[nm], deltas[nm], new_m[nm], new_v[nm] = (back(o) for o in outs)
    small_names = ["ffn1_norm", "mix_norm", "ssd_conv_b", "ssd_dt_bias", "ssd_a_log", "ssd_d", "ssd_norm", "hg_lower_bound",
                   "hg_norm", "ffn2_norm", "final_norm", "ssd_conv_w", "meta_tokens"]
    small_g = g_small[:11] + [g_conv_w.reshape(ssd_conv_w.shape), g_meta]
    pk = lambda d: _pack_rows([d[nm] for nm in small_names])
    outs = _adamw("adamw_small", _pack_rows(small_g)[None], pk(W), pk(Mo), pk(Vo))
    like = [W[nm] for nm in small_names]
    for dst, o in zip((grads, deltas, new_m, new_v), outs):
        for nm, val in zip(small_names, _unpack_rows(o, like)):
            dst[nm] = val

    loss = lax.psum(loss_part[0, 0], MESH_AXES)
    return (loss, grad_x, *[grads[nm] for nm in names], *[deltas[nm] for nm in names],
            *[new_m[nm] for nm in names], *[new_v[nm] for nm in names])
```

```python
import functools

import jax
import jax.numpy as jnp
from jax import lax
from jax.experimental import pallas as pl
from jax.experimental.pallas import tpu as pltpu

F32, BF16 = jnp.float32, jnp.bfloat16
NN, NT, TN = ((1,), (0,)), ((1,), (1,)), ((0,), (0,))
MESH_AXES = ("x", "y", "c")
N_DEV = 8

D_MODEL = 1024
N_META = 16
EPS = 1e-6
SSD_HEADS, SSD_HEAD_DIM, SSD_GROUPS, SSD_STATE, SSD_CONV, Q = 16, 64, 4, 128, 4, 128
SSD_INNER = SSD_HEADS * SSD_HEAD_DIM
SSD_CONV_CH = SSD_INNER + 2 * SSD_GROUPS * SSD_STATE
HG_WIDTH, HG_HEADS, HG_CHUNK = 1024, 8, 16
PAD = Q - N_META
N_MAIN = 9 * 1024
ADAM_LR, ADAM_B1, ADAM_B2, ADAM_EPS, ADAM_WD, ADAM_STEP = 0.001, 0.9, 0.999, 1e-08, 0.01, 10
VMEM_LIMIT = 52 * 1024 * 1024


def _dot(a, b, dims):
    return lax.dot_general(a, b, (dims, ((), ())), preferred_element_type=F32)


def _dot01(a, b, dims, sel):
    x = b if sel == "a" else a
    hi = x.astype(BF16)
    r1 = x - hi.astype(F32)
    mid = r1.astype(BF16)
    lo = (r1 - mid.astype(F32)).astype(BF16)
    s = (a if sel == "a" else b).astype(BF16)
    parts = [_dot(s, p, dims) if sel == "a" else _dot(p, s, dims) for p in (hi, mid, lo)]
    return parts[0] + parts[1] + parts[2]


def _sigmoid(x):
    return 1.0 / (1.0 + jnp.exp(-x))


def _dsilu(x, s):
    return s * (1.0 + x * (1.0 - s))


def _softplus(x):
    e = jnp.exp(-jnp.abs(x))
    u = 1.0 + e
    log1p_e = jnp.where(u == 1.0, e, jnp.log(u) * e / (u - 1.0))
    return jnp.maximum(x, 0.0) + log1p_e


def _params(sem):
    return pltpu.CompilerParams(dimension_semantics=sem, vmem_limit_bytes=VMEM_LIMIT)


def _tile(n, prefs):
    for p in prefs:
        if n % p == 0:
            return p
    return n


CHIP_FLIPS = ((1, 0), (0, 1), (1, 1))
N_PEER = N_DEV - 1


def _comm_gather(srcs, outs, send_sems, recv_sems, local_sems):
    n = len(srcs)
    x, y, c = (lax.axis_index(a) for a in MESH_AXES)
    dev = lambda px, py, pc: 4 * px + 2 * py + pc
    me, sib = dev(x, y, c), (x, y, 1 - c)

    def rc(w, k, slot, to, src=None):
        return pltpu.make_async_remote_copy(
            src_ref=outs[w].at[slot] if src is None else src, dst_ref=outs[w].at[slot],
            send_sem=send_sems.at[w, k], recv_sem=recv_sems.at[w, k], device_id=to, device_id_type=pl.DeviceIdType.MESH)

    def local(w):
        return pltpu.make_async_copy(srcs[w], outs[w].at[me], local_sems.at[w])

    def start():
        for w in range(n):
            local(w).start()
            rc(w, 0, me, sib, src=srcs[w]).start()
            for j, (fx, fy) in enumerate(CHIP_FLIPS):
                rc(w, 1 + j, me, (x ^ fx, y ^ fy, c), src=srcs[w]).start()

    def pass_on():
        for w in range(n):
            for j, (fx, fy) in enumerate(CHIP_FLIPS):
                slot = dev(x ^ fx, y ^ fy, c)
                rc(w, 1 + j, slot, sib).wait_recv()
                rc(w, 4 + j, slot, sib).start()

    def finish():
        for w in range(n):
            rc(w, 0, dev(x, y, 1 - c), sib).wait_recv()
            rc(w, 0, me, sib, src=srcs[w]).wait_send()
            for j, (fx, fy) in enumerate(CHIP_FLIPS):
                rc(w, 4 + j, dev(x ^ fx, y ^ fy, 1 - c), sib).wait_recv()
                rc(w, 1 + j, me, sib, src=srcs[w]).wait_send()
                rc(w, 4 + j, dev(x ^ fx, y ^ fy, c), sib).wait_send()
            local(w).wait()

    return start, pass_on, finish


def _comm_scatter(srcs, outs, send_sems, recv_sems, local_sems):
    n = len(srcs)
    x, y, c = (lax.axis_index(a) for a in MESH_AXES)
    me = 4 * x + 2 * y + c

    def copies():
        out = []
        for w in range(n):
            out.append(pltpu.make_async_copy(srcs[w].at[me], outs[w].at[me], local_sems.at[w]))
            for k in range(1, N_DEV):
                px, py, pc = x ^ (k >> 2), y ^ ((k >> 1) & 1), c ^ (k & 1)
                out.append(pltpu.make_async_remote_copy(
                    src_ref=srcs[w].at[4 * px + 2 * py + pc], dst_ref=outs[w].at[me],
                    send_sem=send_sems.at[w, k - 1], recv_sem=recv_sems.at[w, k - 1],
                    device_id=(px, py, pc), device_id_type=pl.DeviceIdType.MESH))
        return out

    def start():
        for cp in copies():
            cp.start()

    def finish():
        for cp in copies():
            cp.wait()

    return start, None, finish


def _comm_swap(srcs, outs, send_sems, recv_sems, local_sems):
    x, y, c = (lax.axis_index(a) for a in MESH_AXES)

    def copies():
        return [pltpu.make_async_remote_copy(
            src_ref=srcs[w].at[1 - c], dst_ref=outs[w], send_sem=send_sems.at[w, 0], recv_sem=recv_sems.at[w, 0],
            device_id=(x, y, 1 - c), device_id_type=pl.DeviceIdType.MESH) for w in range(len(srcs))]

    def start():
        for cp in copies():
            cp.start()

    def finish():
        for cp in copies():
            cp.wait()

    return start, None, finish


def _comm_chips(srcs, outs, send_sems, recv_sems, local_sems):
    n = len(srcs)
    x, y, c = (lax.axis_index(a) for a in MESH_AXES)
    mine = 2 * x + y

    def copies():
        out = []
        for w in range(n):
            out.append(pltpu.make_async_copy(srcs[w].at[mine], outs[w].at[mine], local_sems.at[w]))
            for j, (fx, fy) in enumerate(CHIP_FLIPS):
                px, py = x ^ fx, y ^ fy
                out.append(pltpu.make_async_remote_copy(
                    src_ref=srcs[w].at[2 * px + py], dst_ref=outs[w].at[mine],
                    send_sem=send_sems.at[w, j], recv_sem=recv_sems.at[w, j],
                    device_id=(px, py, c), device_id_type=pl.DeviceIdType.MESH))
        return out

    def start():
        for cp in copies():
            cp.start()

    def finish():
        for cp in copies():
            cp.wait()

    return start, None, finish


def _comm_parts(comm):
    kind, arrays = comm[:2]
    n = len(arrays)
    lead = {"gather": lambda a: (N_DEV,) + a.shape, "scatter": lambda a: (N_DEV,) + a.shape[1:],
            "swap": lambda a: a.shape[1:], "chips": lambda a: a.shape}[kind]
    shapes = [jax.ShapeDtypeStruct(lead(a), a.dtype) for a in arrays]
    sems = [pltpu.SemaphoreType.DMA((n, N_PEER)), pltpu.SemaphoreType.DMA((n, N_PEER)), pltpu.SemaphoreType.DMA((n,))]
    make = {"gather": _comm_gather, "scatter": _comm_scatter, "swap": _comm_swap, "chips": _comm_chips}[kind]
    return n, shapes, sems, make


def _exchange(name, kind, arrays):
    n, shapes, sems, make = _comm_parts((kind, arrays))

    def body(*refs):
        start, middle, finish = make(refs[:n], refs[n:2 * n], *refs[2 * n:])
        start()
        if middle:
            middle()
        finish()

    any_spec = pl.BlockSpec(memory_space=pl.ANY)
    return pl.pallas_call(
        body, name=name, in_specs=[any_spec] * n, out_specs=[any_spec] * n, out_shape=shapes, scratch_shapes=sems,
        compiler_params=pltpu.CompilerParams(has_side_effects=True),
    )(*arrays)


def _call(body, *, name, grid, in_specs, out_specs, out_shape, scratch, sem, args, comm=None, into=None):
    any_spec = pl.BlockSpec(memory_space=pl.ANY)
    in_specs, args, aliases, n_body_in = list(in_specs), list(args), {}, len(in_specs)
    if into is not None:
        in_specs.append(any_spec)
        args.append(into[0])
        aliases = {n_body_in: into[1]}
    n_in, n_out, n_scr = len(in_specs), len(out_specs), len(scratch)
    if comm is None:
        def plain(*refs):
            body(*refs[:n_body_in], *refs[n_in:])

        return pl.pallas_call(plain, name=name, grid=grid, in_specs=in_specs, out_specs=out_specs, out_shape=out_shape,
                              scratch_shapes=scratch, input_output_aliases=aliases, compiler_params=_params(sem))(*args)
    n, shapes, sems, make = _comm_parts(comm)

    def carrier(*refs):
        ins, csrc = refs[:n_body_in], refs[n_in:n_in + n]
        outs, cout = refs[n_in + n:n_in + n + n_out], refs[n_in + n + n_out:n_in + 2 * n + n_out]
        rest = refs[n_in + 2 * n + n_out:]
        start, middle, finish = make(csrc, cout, *rest[n_scr:])
        ids = [pl.program_id(a) for a in range(len(grid))]
        step = functools.reduce(lambda acc, ig: acc * ig[1] + ig[0], zip(ids, grid), 0)
        n_steps = functools.reduce(lambda a, b: a * b, grid, 1)
        pl.when(step == 0)(start)
        body(*ins, *outs, *rest[:n_scr])
        if middle:
            early = len(comm) > 2 and comm[2] == "early"
            pl.when(step == ((3 * n_steps) // 4 if early else n_steps - 1))(middle)
        pl.when(step == n_steps - 1)(finish)

    return pl.pallas_call(
        carrier, name=name, grid=grid, in_specs=in_specs + [any_spec] * n,
        out_specs=list(out_specs) + [any_spec] * n, out_shape=list(out_shape) + shapes,
        scratch_shapes=list(scratch) + sems, input_output_aliases=aliases,
        compiler_params=pltpu.CompilerParams(dimension_semantics=("arbitrary",) * len(grid),
                                             vmem_limit_bytes=VMEM_LIMIT, has_side_effects=True),
    )(*args, *comm[1])


def _fused_matmul(name, M, N, K, pairs, extras, epilogue, out_dtypes, n_acc, tm, tn, tk, outer="i", comm=None,
                  stack=False, vecs=(), row_sums=0, wide=None, sub=None):
    nk = K // tk
    n_pairs, n_ex, n_out = len(pairs), len(extras), len(out_dtypes)
    assert not row_sums or (outer == "i" and N == tn)

    def ij(g0, g1):
        return (g0, g1) if outer == "i" else (g1, g0)

    in_specs, args = [], []
    for p in pairs:
        ao, bk, bn = p.get("a_off", 0), p.get("bk_off", 0), p.get("bn_off", 0)
        mode = dict(pipeline_mode=pl.Buffered(1)) if p.get("resident") else {}
        if "a_lead" in p:
            in_specs.append(pl.BlockSpec((None, tm, tk),
                                         lambda g0, g1, k, ao=ao, ld=p["a_lead"]: (ld, ij(g0, g1)[0], k + ao)))
        else:
            in_specs.append(pl.BlockSpec((tm, tk), lambda g0, g1, k, ao=ao: (ij(g0, g1)[0], k + ao)))
        if "b_shift" in p:
            first, shift = p["b_shift"]
            if p.get("trans_b"):
                in_specs.append(pl.BlockSpec(
                    (pl.Element(tn), pl.Element(tk)),
                    lambda g0, g1, k, bk=bk: (
                        pl.multiple_of(ij(g0, g1)[1] * tn + jnp.where(ij(g0, g1)[1] >= first, shift, 0), 16),
                        (k + bk) * tk)))
            else:
                in_specs.append(pl.BlockSpec(
                    (pl.Element(tk), pl.Element(tn)),
                    lambda g0, g1, k, bn=bn: (pl.multiple_of(k * tk + jnp.where(k >= first, shift, 0), 16),
                                              (ij(g0, g1)[1] + bn) * tn)))
        elif p.get("trans_b"):
            in_specs.append(pl.BlockSpec((tn, tk), lambda g0, g1, k, bk=bk, bn=bn: (ij(g0, g1)[1] + bn, k + bk), **mode))
        else:
            in_specs.append(pl.BlockSpec((tk, tn), lambda g0, g1, k, bk=bk, bn=bn: (k + bk, ij(g0, g1)[1] + bn), **mode))
        args += [p["a"], p["b"]]
    for arr, off in extras:
        in_specs.append(pl.BlockSpec((tm, tn), lambda g0, g1, k, off=off: (ij(g0, g1)[0], ij(g0, g1)[1] + off)))
        args.append(arr)
    for arr in vecs:
        in_specs.append(pl.BlockSpec((1, tn), lambda g0, g1, k: (0, ij(g0, g1)[1])))
        args.append(arr)
    if stack:
        out_specs = [pl.BlockSpec((n_out, tm, tn), lambda g0, g1, k: (0,) + ij(g0, g1))]
        out_shape = [jax.ShapeDtypeStruct((n_out, M, N), out_dtypes[0])]
    else:
        out_specs = [pl.BlockSpec((tm, tn), lambda g0, g1, k: ij(g0, g1)) for _ in out_dtypes]
        out_shape = [jax.ShapeDtypeStruct((M, N), dt) for dt in out_dtypes]
    if wide:
        out_specs.append(pl.BlockSpec((pl.Element(tm), pl.Element(wide["width"])),
                                      lambda g0, g1, k: (pl.multiple_of(ij(g0, g1)[0] * tm, 16), wide["col"])))
        out_shape.append(jax.ShapeDtypeStruct((M, wide["total"]), wide["dtype"]))
    n_tile_out = len(out_specs)
    out_specs += [pl.BlockSpec((1, tn), lambda g0, g1, k: (0, 0)) for _ in range(row_sums)]
    out_shape += [jax.ShapeDtypeStruct((1, N), F32) for _ in range(row_sums)]
    grid = (M // tm, N // tn, nk) if outer == "i" else (N // tn, M // tm, nk)
    n_in = 2 * n_pairs + n_ex + len(vecs)

    def partials(refs, cs=slice(None)):
        accs = [None] * n_acc
        for idx, p in enumerate(pairs):
            b_ref = refs[2 * idx + 1]
            d = (_dot(refs[2 * idx][...], b_ref[cs, :], NT) if p.get("trans_b")
                 else _dot(refs[2 * idx][...], b_ref[:, cs], NN))
            accs[p["acc"]] = d if accs[p["acc"]] is None else accs[p["acc"]] + d
        return accs

    def finish(accs, refs, first_rows, cs=slice(None)):
        res = epilogue(accs, [r[:, cs] for r in refs[2 * n_pairs:n_in]])
        if stack:
            o = refs[n_in]
            for idx in range(n_out):
                o[idx, :, cs] = res[idx].astype(o.dtype)
        else:
            for o, r in zip(refs[n_in:n_in + n_out], res):
                o[:, cs] = r.astype(o.dtype)
        if wide:
            o = refs[n_in + n_tile_out - 1]
            o[...] = res[n_out].astype(o.dtype)
        for o, r in zip(refs[n_in + n_tile_out:n_in + n_tile_out + row_sums], res[n_out + bool(wide):]):
            @pl.when(first_rows)
            def _(o=o, r=r):
                o[...] = r

            @pl.when(jnp.logical_not(first_rows))
            def _(o=o, r=r):
                o[...] += r

    if nk == 1 and sub:
        assert not wide and not row_sums and tn % sub == 0

        def body(*refs):
            for c in range(tn // sub):
                cs = slice(c * sub, (c + 1) * sub)
                finish(partials(refs, cs), refs, None, cs)
        scratch = []
    elif nk == 1:
        def body(*refs):
            finish(partials(refs), refs, pl.program_id(0) == 0)
        scratch = []
    else:
        def body(*refs):
            acc_refs = refs[-n_acc:]
            k = pl.program_id(2)
            first_rows = pl.program_id(0) == 0
            new = partials(refs)

            @pl.when(k == 0)
            def _():
                for a, v in zip(acc_refs, new):
                    a[...] = v

            @pl.when(k > 0)
            def _():
                for a, v in zip(acc_refs, new):
                    a[...] += v

            @pl.when(k == nk - 1)
            def _():
                finish([a[...] for a in acc_refs], refs, first_rows)
        scratch = [pltpu.VMEM((tm, tn), F32) for _ in range(n_acc)]

    return _call(body, name=name, grid=grid, in_specs=in_specs, out_specs=out_specs, out_shape=out_shape,
                 scratch=scratch, sem=("parallel", "parallel", "arbitrary"), args=args, comm=comm)


def _matmul_tn(name, x, y, t1, t2, tr, scale=1.0, comm=None, out_dtype=BF16, out_skip=None):
    L = x.shape[0] if x.ndim == 3 else 1
    R, K1 = x.shape[-2:]
    N1 = y.shape[1]
    nr, n1 = R // tr, K1 // t1
    if x.ndim == 3:
        x_spec = pl.BlockSpec((None, tr, t1), lambda i, j, r: (i // n1, r, i % n1))
    else:
        x_spec = pl.BlockSpec((tr, t1), lambda i, j, r: (r, i))
    rows_out = L * K1
    o_spec = pl.BlockSpec((t1, t2), lambda i, j, r: (i, j))
    if out_skip:
        row, count = out_skip
        rows_out += count
        o_spec = pl.BlockSpec(
            (pl.Element(t1), pl.Element(t2)),
            lambda i, j, r: (pl.multiple_of(i * t1 + jnp.where(i * t1 >= row, count, 0), 16), j * t2))

    def body(x_ref, y_ref, o_ref, *acc):
        d = _dot(x_ref[...], y_ref[...], TN)
        if nr == 1:
            o_ref[...] = (d * scale).astype(o_ref.dtype)
            return
        r = pl.program_id(2)

        @pl.when(r == 0)
        def _():
            acc[0][...] = d

        @pl.when(jnp.logical_and(r > 0, r < nr - 1))
        def _():
            acc[0][...] += d

        @pl.when(r == nr - 1)
        def _():
            o_ref[...] = ((acc[0][...] + d) * scale).astype(o_ref.dtype)

    return _call(
        body, name=name, grid=(L * n1, N1 // t2, nr),
        in_specs=[x_spec, pl.BlockSpec((tr, t2), lambda i, j, r: (r, j))], out_specs=[o_spec],
        out_shape=[jax.ShapeDtypeStruct((rows_out, N1), out_dtype)],
        scratch=[pltpu.VMEM((t1, t2), F32)] if nr > 1 else [],
        sem=("parallel", "parallel", "arbitrary"), args=(x, y), comm=comm)


def _embed_norm(x, meta, w, comm=None):
    Bl, S, D = x.shape
    nb = (PAD + N_META + S) // Q
    M = Bl * nb * Q

    def body(x_ref, meta_ref, w_ref, h_ref, n_ref):
        head = jnp.concatenate([jnp.zeros((PAD, D), F32), meta_ref[...]], axis=0)
        h = jnp.where(pl.program_id(1) == 0, head, x_ref[0])
        h_ref[...] = h
        n_ref[...] = _rmsnorm_tile(h, w_ref[...]).astype(n_ref.dtype)

    row = pl.BlockSpec((Q, D), lambda b, t: (b * nb + t, 0))
    return _call(
        body, name="embed_norm", grid=(Bl, nb),
        in_specs=[pl.BlockSpec((1, Q, D), lambda b, t: (b, jnp.maximum(t - 1, 0), 0)),
                  pl.BlockSpec((N_META, D), lambda b, t: (0, 0)), pl.BlockSpec((1, D), lambda b, t: (0, 0))],
        out_specs=[row, row], out_shape=[jax.ShapeDtypeStruct((M, D), F32), jax.ShapeDtypeStruct((M, D), BF16)],
        scratch=[], sem=("parallel", "parallel"), args=(x, meta, w), comm=comm)


def _rmsnorm_bwd_tile(dn, h, w, dh_in):
    r = lax.rsqrt(jnp.mean(h * h, axis=-1, keepdims=True) + EPS)
    xhat = h * r
    gw = dn * w
    dh = dh_in + r * (gw - xhat * jnp.mean(gw * xhat, axis=-1, keepdims=True))
    return dh, jnp.sum(dn * xhat, axis=0, keepdims=True)


def _loss_head(h, w, target, Bl, nb):
    M, D = h.shape

    def body(h_ref, w_ref, t_ref, dh_ref, dhb_ref, dw_ref, loss_ref):
        b, t = pl.program_id(0), pl.program_id(1)
        live = (t > 0).astype(F32)
        x = h_ref[...]
        r = lax.rsqrt(jnp.mean(x * x, axis=-1, keepdims=True) + EPS)
        xhat = x * r
        wv = w_ref[...]
        err = (xhat * wv - t_ref[0]) * live
        dy = err * (1.0 / D)
        gw = dy * wv
        dx = r * (gw - xhat * jnp.mean(gw * xhat, axis=-1, keepdims=True))
        dh_ref[...] = dx
        dhb_ref[...] = dx.astype(BF16)
        dw = jnp.sum(dy * xhat, axis=0, keepdims=True)
        part = 0.5 * jnp.sum(jnp.sum(err * err, axis=-1, keepdims=True) * (1.0 / D), axis=0, keepdims=True)
        first = jnp.logical_and(b == 0, t == 0)

        @pl.when(first)
        def _():
            dw_ref[...] = dw
            loss_ref[...] = jnp.broadcast_to(part, loss_ref.shape)

        @pl.when(jnp.logical_not(first))
        def _():
            dw_ref[...] += dw
            loss_ref[...] += jnp.broadcast_to(part, loss_ref.shape)

    row = pl.BlockSpec((Q, D), lambda b, t: (b * nb + t, 0))
    vec = pl.BlockSpec((1, D), lambda b, t: (0, 0))
    return pl.pallas_call(
        body, name="loss_head", grid=(Bl, nb),
        in_specs=[row, vec, pl.BlockSpec((1, Q, D), lambda b, t: (b, jnp.maximum(t - 1, 0), 0))],
        out_specs=[row, row, vec, pl.BlockSpec((8, 128), lambda b, t: (0, 0))],
        out_shape=[jax.ShapeDtypeStruct((M, D), F32), jax.ShapeDtypeStruct((M, D), BF16),
                   jax.ShapeDtypeStruct((1, D), F32), jax.ShapeDtypeStruct((8, 128), F32)],
        compiler_params=_params(("arbitrary", "arbitrary")),
    )(h, w, target)


CONV_TC = 256


def _conv_pre(xr_ref, w_ref, b_ref):
    x = xr_ref[...].astype(F32)
    acc = b_ref[...] + w_ref[SSD_CONV - 1:SSD_CONV, :] * x
    for k in range(1, SSD_CONV):
        acc = acc + w_ref[SSD_CONV - 1 - k:SSD_CONV - k, :] * pltpu.roll(x, k, 0)
    return x, acc


def _conv_fwd(proj, w, b, Bl, T):
    M = proj.shape[0]
    off = 1024 // CONV_TC

    def body(xr_ref, w_ref, b_ref, o_ref):
        _, acc = _conv_pre(xr_ref, w_ref, b_ref)
        row = lax.broadcasted_iota(jnp.int32, acc.shape, 0)
        o_ref[...] = jnp.where(row >= PAD, acc * _sigmoid(acc), 0.0).astype(o_ref.dtype)

    return pl.pallas_call(
        body, name="conv_fwd", grid=(Bl, SSD_CONV_CH // CONV_TC),
        in_specs=[pl.BlockSpec((T, CONV_TC), lambda bb, j: (bb, j + off)),
                  pl.BlockSpec((SSD_CONV, CONV_TC), lambda bb, j: (0, j)), pl.BlockSpec((1, CONV_TC), lambda bb, j: (0, j))],
        out_specs=pl.BlockSpec((T, CONV_TC), lambda bb, j: (bb, j)),
        out_shape=jax.ShapeDtypeStruct((M, SSD_CONV_CH), BF16), compiler_params=_params(("parallel", "parallel")),
    )(proj, w, b)


def _conv_bwd(proj, w, b, dxc, dproj, Bl, T):
    M = proj.shape[0]
    off = 1024 // CONV_TC

    def body(xr_ref, w_ref, b_ref, d_ref, dx_ref, dw_ref, db_ref):
        x, acc = _conv_pre(xr_ref, w_ref, b_ref)
        row = lax.broadcasted_iota(jnp.int32, acc.shape, 0)
        s = _sigmoid(acc)
        dpre = jnp.where(row >= PAD, d_ref[...].astype(F32) * _dsilu(acc, s), 0.0)
        dx = w_ref[SSD_CONV - 1:SSD_CONV, :] * dpre
        dws = [jnp.sum(dpre * x, axis=0, keepdims=True)]
        for k in range(1, SSD_CONV):
            dx = dx + w_ref[SSD_CONV - 1 - k:SSD_CONV - k, :] * pltpu.roll(dpre, T - k, 0)
            dws.append(jnp.sum(dpre * pltpu.roll(x, k, 0), axis=0, keepdims=True))
        dx_ref[...] = dx.astype(dx_ref.dtype)
        dw = jnp.concatenate(dws[::-1], axis=0)
        db = jnp.sum(dpre, axis=0, keepdims=True)

        @pl.when(pl.program_id(1) == 0)
        def _():
            dw_ref[...] = dw
            db_ref[...] = db

        @pl.when(pl.program_id(1) > 0)
        def _():
            dw_ref[...] += dw
            db_ref[...] += db

    return _call(
        body, name="conv_bwd", grid=(SSD_CONV_CH // CONV_TC, Bl),
        in_specs=[pl.BlockSpec((T, CONV_TC), lambda j, bb: (bb, j + off)),
                  pl.BlockSpec((SSD_CONV, CONV_TC), lambda j, bb: (0, j)), pl.BlockSpec((1, CONV_TC), lambda j, bb: (0, j)),
                  pl.BlockSpec((T, CONV_TC), lambda j, bb: (bb, j))],
        out_specs=[pl.BlockSpec((T, CONV_TC), lambda j, bb: (bb, j + off)),
                   pl.BlockSpec((SSD_CONV, CONV_TC), lambda j, bb: (0, j)), pl.BlockSpec((1, CONV_TC), lambda j, bb: (0, j))],
        out_shape=[jax.ShapeDtypeStruct(dproj.shape, BF16), jax.ShapeDtypeStruct((SSD_CONV, SSD_CONV_CH), F32),
                   jax.ShapeDtypeStruct((1, SSD_CONV_CH), F32)],
        scratch=[], sem=("parallel", "arbitrary"), args=(proj, w, b, dxc), into=(dproj, 0))


N_PAIR = SSD_HEADS // 2
HPG = SSD_HEADS // SSD_GROUPS
GW = SSD_INNER // SSD_GROUPS


def _per_group(fn, *arrs):
    return jnp.concatenate([jnp.broadcast_to(fn(*(a[:, GW * g:GW * (g + 1)] for a in arrs)), (arrs[0].shape[0], GW))
                            for g in range(SSD_GROUPS)], axis=1)


def _ssd_prep(c, dtr_ref, bias_ref, alog_ref, d_ref):
    row = lax.broadcasted_iota(jnp.int32, (Q, 128), 0)
    col = lax.broadcasted_iota(jnp.int32, (Q, 128), 1)
    live = col < SSD_HEADS
    valid = jnp.logical_and(jnp.logical_or(c > 0, row >= PAD), live)
    pre = dtr_ref[...] + bias_ref[...]
    dt = jnp.where(valid, _softplus(pre), 0.0)
    A = jnp.where(live[0:1], -jnp.exp(alog_ref[...]), 0.0)
    tri = row >= col
    eye = (row == col).astype(BF16)
    cs = _dot01(tri, dt * A, NN, "a")
    cst = _dot01(eye, cs, NT, "a")
    spread = (lax.broadcasted_iota(jnp.int32, (128, SSD_INNER), 0)
              == lax.broadcasted_iota(jnp.int32, (128, SSD_INNER), 1) // SSD_HEAD_DIM).astype(BF16)
    dt_w = _dot01(dt, spread, NN, "b")
    cs_w = _dot01(cs, spread, NN, "b")
    d_w = _dot01(jnp.broadcast_to(d_ref[...], (8, 128)), spread, NN, "b")[0:1]
    lane = lax.broadcasted_iota(jnp.int32, (Q, SSD_INNER), 1)
    first = (lane % 128) < SSD_HEAD_DIM
    return dict(row=row, col=col, valid=valid, pre=pre, dt=dt, A=A, tri=tri, eye=eye, cs=cs, cst=cst, spread=spread,
                dt_w=dt_w, cs_w=cs_w, d_w=d_w, ecs_w=jnp.exp(cs_w), decay_w=jnp.exp(cs_w[Q - 1:Q] - cs_w), first=first)


def _ssd_chunk(xc_ref, s, states):
    xv = xc_ref[:, 0:SSD_INNER].astype(F32)
    Bs = [xc_ref[:, SSD_INNER + 128 * g:SSD_INNER + 128 * (g + 1)] for g in range(SSD_GROUPS)]
    Cs = [xc_ref[:, SSD_INNER + 512 + 128 * g:SSD_INNER + 512 + 128 * (g + 1)] for g in range(SSD_GROUPS)]
    X = xv * s["dt_w"]
    X0 = jnp.where(s["first"], X, 0.0)
    Xb = (X0.astype(BF16), (X - X0).astype(BF16))
    Xd = (X * s["decay_w"]).astype(BF16)
    CB = [_dot(Cs[g], Bs[g], NT) for g in range(SSD_GROUPS)]
    Lms = [jnp.exp(jnp.where(s["tri"], s["cs"][:, h:h + 1] - s["cst"][h:h + 1, :], -jnp.inf)) for h in range(SSD_HEADS)]
    Ms = [CB[h // HPG] * Lms[h] for h in range(SSD_HEADS)]
    Mb = [m.astype(BF16) for m in Ms]
    prev_b = [st.astype(BF16) for st in states]
    yds, yos, sts = [], [], []
    for p in range(N_PAIR):
        g, ln = p // 2, slice(128 * p, 128 * (p + 1))
        yds.append(_dot(Mb[2 * p], Xb[0][:, ln], NN) + _dot(Mb[2 * p + 1], Xb[1][:, ln], NN))
        yos.append(_dot(Cs[g], prev_b[p], NT))
        sts.append(_dot(Xd[:, ln], Bs[g], TN))
    yo = jnp.concatenate(yos, axis=1)
    y = jnp.concatenate(yds, axis=1) + yo * s["ecs_w"] + xv * s["d_w"]
    upper = s["row"] < SSD_HEAD_DIM
    cl = s["cs"][Q - 1:Q, :]
    ecl_rows = [jnp.where(upper, jnp.exp(cl[:, 2 * p:2 * p + 1]), jnp.exp(cl[:, 2 * p + 1:2 * p + 2])) for p in range(N_PAIR)]
    new_states = [states[p] * ecl_rows[p] + sts[p] for p in range(N_PAIR)]
    return y, new_states, dict(xv=xv, Bs=Bs, Cs=Cs, X=X, Xb=Xb, CB=CB, Lms=Lms, Ms=Ms, Mb=Mb, prev_b=prev_b, yo=yo,
                               ecl_rows=ecl_rows)


def _ssd_in_specs(nc, rev=False):
    rb = (lambda b, c: b * nc + nc - 1 - c) if rev else (lambda b, c: b * nc + c)
    vec = pl.BlockSpec((1, 128), lambda b, c: (0, 0))
    return [pl.BlockSpec((Q, SSD_CONV_CH), lambda b, c: (rb(b, c), 0)),
            pl.BlockSpec((Q, 128), lambda b, c: (rb(b, c), 0)),
            pl.BlockSpec((Q, SSD_INNER), lambda b, c: (rb(b, c), 0)),
            vec, vec, vec, pl.BlockSpec((1, SSD_INNER), lambda b, c: (0, 0))]


def _ssd_fwd(xc, dtr, proj, bias_p, alog_p, d_p, nw, Bl, nc):
    M = xc.shape[0]

    def body(xc_ref, dtr_ref, z_ref, bias_ref, alog_ref, d_ref, nw_ref, y_ref, prev_ref, state):
        c = pl.program_id(1)

        @pl.when(c == 0)
        def _():
            state[...] = jnp.zeros_like(state)

        s = _ssd_prep(c, dtr_ref, bias_ref, alog_ref, d_ref)
        states = [state[p] for p in range(N_PAIR)]
        y, new_states, _ = _ssd_chunk(xc_ref, s, states)
        for p in range(N_PAIR):
            prev_ref[0, 0, p] = states[p]
            state[p] = new_states[p]
        zz = z_ref[...].astype(F32)
        yg = y * zz * _sigmoid(zz)
        r = _per_group(lambda a: lax.rsqrt(jnp.mean(a * a, axis=-1, keepdims=True) + EPS), yg)
        y_ref[...] = (yg * r * nw_ref[...]).astype(y_ref.dtype)

    return pl.pallas_call(
        body, name="ssd_fwd", grid=(Bl, nc), in_specs=_ssd_in_specs(nc),
        out_specs=[pl.BlockSpec((Q, SSD_INNER), lambda b, c: (b * nc + c, 0)),
                   pl.BlockSpec((1, 1, N_PAIR, 128, 128), lambda b, c: (b, c, 0, 0, 0))],
        out_shape=[jax.ShapeDtypeStruct((M, SSD_INNER), BF16), jax.ShapeDtypeStruct((Bl, nc, N_PAIR, 128, 128), F32)],
        scratch_shapes=[pltpu.VMEM((N_PAIR, 128, 128), F32)],
        compiler_params=_params(("arbitrary", "arbitrary")),
    )(xc, dtr, proj, bias_p, alog_p, d_p, nw)


def _ssd_bwd(xc, dtr, proj, bias_p, alog_p, d_p, nw, prev, dya, dproj, Bl, nc, comm=None):
    M = xc.shape[0]

    def body(xc_ref, dtr_ref, z_ref, bias_ref, alog_ref, d_ref, nw_ref, prev_ref, dy_ref,
             dxc_ref, dz_ref, ddtr_ref, dbias_ref, dalog_ref, dd_ref, dnw_ref, dS):
        b, t = pl.program_id(0), pl.program_id(1)

        @pl.when(t == 0)
        def _():
            dS[...] = jnp.zeros_like(dS)

        s = _ssd_prep(nc - 1 - t, dtr_ref, bias_ref, alog_ref, d_ref)
        states = [prev_ref[0, 0, p] for p in range(N_PAIR)]
        y, _, k = _ssd_chunk(xc_ref, s, states)
        xv, Bs, Cs, Xb = k["xv"], k["Bs"], k["Cs"], k["Xb"]

        zz = z_ref[...].astype(F32)
        sz = _sigmoid(zz)
        silu_z = zz * sz
        yg = y * silu_z
        r = _per_group(lambda a: lax.rsqrt(jnp.mean(a * a, axis=-1, keepdims=True) + EPS), yg)
        xhat = yg * r
        dout = dy_ref[...].astype(F32)
        gw = dout * nw_ref[...]
        dyg = r * (gw - xhat * _per_group(lambda a, c2: jnp.mean(a * c2, axis=-1, keepdims=True), gw, xhat))
        dnw = jnp.sum(dout * xhat, axis=0, keepdims=True)
        dz_ref[...] = (dyg * y * _dsilu(zz, sz)).astype(dz_ref.dtype)
        dy = dyg * silu_z
        dy0 = jnp.where(s["first"], dy, 0.0)
        dyb = (dy0.astype(BF16), (dy - dy0).astype(BF16))
        dYo = (dy * s["ecs_w"]).astype(BF16)

        dS_f = [dS[p] for p in range(N_PAIR)]
        dS_b = [d.astype(BF16) for d in dS_f]
        BdS, dXm, dprev, dCs, dMs, XdS = [], [], [], [[] for _ in range(SSD_GROUPS)], [], []
        for p in range(N_PAIR):
            g, ln = p // 2, slice(128 * p, 128 * (p + 1))
            BdS.append(_dot(Bs[g], dS_b[p], NT))
            dXm.append(_dot(k["Mb"][2 * p], dyb[0][:, ln], TN) + _dot(k["Mb"][2 * p + 1], dyb[1][:, ln], TN))
            dprev.append(_dot(dYo[:, ln], Cs[g], TN))
            dCs[g].append(_dot(dYo[:, ln], k["prev_b"][p], NN))
            for hh in range(2):
                dMs.append(_dot(dyb[hh][:, ln], Xb[hh][:, ln], NT))
                XdS.append(_dot(Xb[hh][:, ln], dS_b[p], NN))
        dX = jnp.concatenate(dXm, axis=1) + s["decay_w"] * jnp.concatenate(BdS, axis=1)
        dxs = dy * s["d_w"] + dX * s["dt_w"]

        sums = _dot01(jnp.concatenate([dX * xv, dy * k["yo"] * s["ecs_w"], dy * xv], axis=0), s["spread"], NT, "b")
        ddt, dcs = sums[0:Q], sums[Q:2 * Q]
        dD = jnp.sum(sums[2 * Q:3 * Q], axis=0, keepdims=True)

        col, row = s["col"], s["row"]
        lane1 = col[0:1]
        rowsT = lax.broadcasted_iota(jnp.int32, (128, Q), 0)
        dcs_t = jnp.zeros((128, Q), F32)
        dcl = jnp.zeros((1, 128), F32)
        dB_out, dC_out = [], []
        for g in range(SSD_GROUPS):
            Bf = Bs[g].astype(F32)
            dCB = jnp.zeros((Q, Q), F32)
            dBacc = jnp.zeros((Q, 128), F32)
            for r4 in range(HPG):
                h = HPG * g + r4
                p, hh = h // 2, h % 2
                W = dMs[h] * k["Ms"][h]
                dCB = dCB + dMs[h] * k["Lms"][h]
                decay_h = s["decay_w"][:, SSD_HEAD_DIM * h:SSD_HEAD_DIM * h + 1]
                dBacc = dBacc + decay_h * XdS[h]
                tdec = jnp.sum(XdS[h] * Bf, axis=1, keepdims=True) * decay_h
                dcs = dcs + jnp.where(col == h, jnp.sum(W, axis=1, keepdims=True) - tdec, 0.0)
                dcs_t = dcs_t - jnp.where(rowsT == h, jnp.sum(W, axis=0, keepdims=True), 0.0)
                rows_h = (row < SSD_HEAD_DIM) if hh == 0 else (row >= SSD_HEAD_DIM)
                sprev = jnp.sum(jnp.sum(jnp.where(rows_h, dS_f[p] * states[p], 0.0), axis=1, keepdims=True),
                                axis=0, keepdims=True)
                ecl = jnp.exp(s["cs"][Q - 1:Q, h:h + 1])
                dcl = dcl + jnp.where(lane1 == h, jnp.sum(tdec, axis=0, keepdims=True) + ecl * sprev, 0.0)
            dCB_b = dCB.astype(BF16)
            dC_out.append(dCs[g][0] + dCs[g][1] + _dot(dCB_b, Bs[g], NN))
            dB_out.append(dBacc + _dot(dCB_b, Cs[g], TN))
        for p in range(N_PAIR):
            dS[p] = dS_f[p] * k["ecl_rows"][p] + dprev[p]
        dxc_ref[...] = jnp.concatenate([dxs] + dB_out + dC_out, axis=1).astype(dxc_ref.dtype)

        dcs = dcs + _dot01(s["eye"], dcs_t, NT, "a") + jnp.where(row == Q - 1, dcl, 0.0)
        da = _dot01(row <= col, dcs, NN, "a")
        ddt = ddt + da * s["A"]
        dpre = jnp.where(s["valid"], ddt * _sigmoid(s["pre"]), 0.0)
        ddtr_ref[...] = dpre
        dbias = jnp.sum(dpre, axis=0, keepdims=True)
        dalog = jnp.sum(da * s["dt"], axis=0, keepdims=True) * s["A"]
        first_step = jnp.logical_and(b == 0, t == 0)

        @pl.when(first_step)
        def _():
            dbias_ref[...] = dbias
            dalog_ref[...] = dalog
            dd_ref[...] = dD
            dnw_ref[...] = dnw

        @pl.when(jnp.logical_not(first_step))
        def _():
            dbias_ref[...] += dbias
            dalog_ref[...] += dalog
            dd_ref[...] += dD
            dnw_ref[...] += dnw

    rb = lambda b, c: b * nc + nc - 1 - c
    rowblk = lambda w: pl.BlockSpec((Q, w), lambda b, c: (rb(b, c), 0))
    vec = lambda w: pl.BlockSpec((1, w), lambda b, c: (0, 0))
    return _call(
        body, name="ssd_bwd", grid=(Bl, nc),
        in_specs=_ssd_in_specs(nc, rev=True) + [
            pl.BlockSpec((1, 1, N_PAIR, 128, 128), lambda b, c: (b, nc - 1 - c, 0, 0, 0)), rowblk(SSD_INNER)],
        out_specs=[rowblk(SSD_CONV_CH), rowblk(SSD_INNER), rowblk(128), vec(128), vec(128), vec(128), vec(SSD_INNER)],
        out_shape=[jax.ShapeDtypeStruct((M, SSD_CONV_CH), BF16), jax.ShapeDtypeStruct(dproj.shape, BF16),
                   jax.ShapeDtypeStruct((M, 128), F32), jax.ShapeDtypeStruct((1, 128), F32),
                   jax.ShapeDtypeStruct((1, 128), F32), jax.ShapeDtypeStruct((1, 128), F32),
                   jax.ShapeDtypeStruct((1, SSD_INNER), F32)],
        scratch=[pltpu.VMEM((N_PAIR, 128, 128), F32)], sem=("arbitrary", "arbitrary"),
        args=(xc, dtr, proj, bias_p, alog_p, d_p, nw, prev, dya), comm=comm, into=(dproj, 1))


NSUB = Q // HG_CHUNK
HG_HP = 8
EXP_CAP = 80.0


def _hg_setup(blk, q_ref, f_ref, hb_ref):
    row = lax.broadcasted_iota(jnp.int32, (Q, Q), 0)
    col = lax.broadcasted_iota(jnp.int32, (Q, Q), 1)
    same = (row // HG_CHUNK) == (col // HG_CHUNK)
    causal = jnp.logical_and(same, col <= row)
    lb = _sigmoid(hb_ref[0:1, :] - hb_ref[1:2, :])
    fl = f_ref[...].astype(F32)
    sg = _sigmoid(fl)
    fg = lb + (1.0 - lb) * sg
    k = (1.0 - lb) * (1.0 - sg)
    gl = jnp.log(fg)
    G = _dot01(causal, gl, NN, "a")
    T = _dot01(same, gl, NN, "a")
    qv = q_ref[...].astype(F32)
    sq = _sigmoid(qv)
    eG = jnp.exp(G)
    eGn = jnp.exp(jnp.minimum(-G, EXP_CAP))
    eTG = jnp.exp(T - G)
    qt = qv * sq * eG
    kt = k * eGn
    kh = k * eTG
    valid = jnp.logical_or(blk > 0, row[:, :1] >= PAD)
    return dict(row=row, col=col, same=same, causal=causal, lb=lb, sg=sg, fg=fg, k=k, T=T, qv=qv, sq=sq,
                eG=eG, eGn=eGn, eTG=eTG, qt=qt, kt=kt, kh=kh, valid=valid)


def _hg_specs(nb, rev=False):
    rb = (lambda h, b, t: b * nb + nb - 1 - t) if rev else (lambda h, b, t: b * nb + t)
    w = 128 * HG_HP
    blk = lambda off: pl.BlockSpec((Q, w), lambda h, b, t, off=off: (rb(h, b, t), off // HG_HP + h))
    return [blk(24), blk(32), blk(40), blk(48),
            pl.BlockSpec((2, w), lambda h, b, t: (0, h)), pl.BlockSpec((1, w), lambda h, b, t: (0, h))]


HEAD_LANES = tuple(slice(128 * hh, 128 * (hh + 1)) for hh in range(HG_HP))


def _per_head(fn, *arrs):
    return jnp.concatenate([jnp.broadcast_to(fn(*(a[:, ln] for a in arrs)), (arrs[0].shape[0], 128))
                            for ln in HEAD_LANES], axis=1)


def _hgrn_fwd(proj, hb, nw, Bl, nb, comm=None):
    M = proj.shape[0]

    def body(q_ref, f_ref, i_ref, g_ref, hb_ref, nw_ref, y_ref, o_ref, st_ref, S):
        blk = pl.program_id(2)

        @pl.when(blk == 0)
        def _():
            S[...] = jnp.zeros_like(S)

        s = _hg_setup(blk, q_ref, f_ref, hb_ref)
        v = i_ref[...]
        qt_b, kt_b, kh_b = s["qt"].astype(BF16), s["kt"].astype(BF16), s["kh"].astype(BF16)
        eT = jnp.exp(s["T"])
        att = [jnp.where(s["causal"], _dot(qt_b[:, ln], kt_b[:, ln], NT), 0.0).astype(BF16) for ln in HEAD_LANES]
        o_intra = [_dot(att[hh], v[:, ln], NN) for hh, ln in enumerate(HEAD_LANES)]
        for j in range(NSUB):
            sl = slice(HG_CHUNK * j, HG_CHUNK * (j + 1))
            for hh, ln in enumerate(HEAD_LANES):
                St = S[hh]
                st_ref[0, hh, 0, j] = St
                o_ref[sl, ln] = o_intra[hh][sl] + _dot(qt_b[sl, ln], St.astype(BF16), NT)
                S[hh] = St * eT[HG_CHUNK * j:HG_CHUNK * j + 1, ln] + _dot(v[sl, ln], kh_b[sl, ln], TN)
        o = o_ref[...]
        r = _per_head(lambda a: lax.rsqrt(jnp.mean(a * a, axis=-1, keepdims=True) + EPS), o)
        gv = g_ref[...].astype(F32)
        y_ref[...] = (o * r * nw_ref[...] * gv * _sigmoid(gv)).astype(y_ref.dtype)

    rowblk = pl.BlockSpec((Q, 128 * HG_HP), lambda h, b, t: (b * nb + t, h))
    return _call(
        body, name="hgrn_fwd", grid=(HG_HEADS // HG_HP, Bl, nb), in_specs=_hg_specs(nb),
        out_specs=[rowblk, rowblk,
                   pl.BlockSpec((1, HG_HP, 1, NSUB, 128, 128), lambda h, b, t: (b, h, t, 0, 0, 0))],
        out_shape=[jax.ShapeDtypeStruct((M, HG_WIDTH), BF16), jax.ShapeDtypeStruct((M, HG_WIDTH), F32),
                   jax.ShapeDtypeStruct((Bl, HG_HEADS, nb, NSUB, 128, 128), F32)],
        scratch=[pltpu.VMEM((HG_HP, 128, 128), F32)], sem=("parallel", "arbitrary", "arbitrary"),
        args=(proj, proj, proj, proj, hb, nw), comm=comm)


def _hgrn_bwd(proj, hb, nw, o_saved, st_saved, dyb, dproj, Bl, nb, comm=None):
    assert HG_HP == HG_HEADS

    def body(q_ref, f_ref, i_ref, g_ref, hb_ref, nw_ref, o_ref, st_ref, dy_ref,
             d_ref, dhb_ref, dnw_ref, dS, a_dqt, a_dv, a_dkh, a_dgl):
        b, t = pl.program_id(1), pl.program_id(2)

        @pl.when(t == 0)
        def _():
            dS[...] = jnp.zeros_like(dS)

        first_step = jnp.logical_and(b == 0, t == 0)
        s = _hg_setup(nb - 1 - t, q_ref, f_ref, hb_ref)
        v = i_ref[...]
        qt_b, kt_b, kh_b = s["qt"].astype(BF16), s["kt"].astype(BF16), s["kh"].astype(BF16)
        eT = jnp.exp(s["T"])
        att = [jnp.where(s["causal"], _dot(qt_b[:, ln], kt_b[:, ln], NT), 0.0).astype(BF16) for ln in HEAD_LANES]

        o = o_ref[...]
        r = _per_head(lambda a: lax.rsqrt(jnp.mean(a * a, axis=-1, keepdims=True) + EPS), o)
        xhat = o * r
        gv = g_ref[...].astype(F32)
        sgv = _sigmoid(gv)
        dyv = dy_ref[...].astype(F32)
        d_on = dyv * gv * sgv
        dg_out = dyv * xhat * nw_ref[...] * _dsilu(gv, sgv)
        gw = d_on * nw_ref[...]
        do = r * (gw - xhat * _per_head(lambda a, c: jnp.mean(a * c, axis=-1, keepdims=True), gw, xhat))
        dnw = jnp.sum(d_on * xhat, axis=0, keepdims=True)
        do_b = do.astype(BF16)

        datt = [jnp.where(s["causal"], _dot(do_b[:, ln], v[:, ln], NT), 0.0).astype(BF16) for ln in HEAD_LANES]
        dqt = jnp.concatenate([_dot(datt[hh], kt_b[:, ln], NN) for hh, ln in enumerate(HEAD_LANES)], axis=1)
        dkt = jnp.concatenate([_dot(datt[hh], qt_b[:, ln], TN) for hh, ln in enumerate(HEAD_LANES)], axis=1)
        dv = jnp.concatenate([_dot(att[hh], do_b[:, ln], TN) for hh, ln in enumerate(HEAD_LANES)], axis=1)
        last_row = (lax.broadcasted_iota(jnp.int32, (HG_CHUNK, 128), 0) == HG_CHUNK - 1)
        for j in reversed(range(NSUB)):
            sl = slice(HG_CHUNK * j, HG_CHUNK * (j + 1))
            for hh, ln in enumerate(HEAD_LANES):
                St = st_ref[0, hh, 0, j]
                dSt = dS[hh]
                St_b, dSt_b = St.astype(BF16), dSt.astype(BF16)
                eT_j = eT[HG_CHUNK * j:HG_CHUNK * j + 1, ln]
                dkh_j = _dot(v[sl, ln], dSt_b, NN)
                a_dqt[sl, ln] = _dot(do_b[sl, ln], St_b, NN)
                a_dv[sl, ln] = _dot(kh_b[sl, ln], dSt_b, NT)
                a_dkh[sl, ln] = dkh_j
                dlast = (jnp.sum(St * dSt, axis=0, keepdims=True) * eT_j
                         + jnp.sum(dkh_j * s["kh"][sl, ln], axis=0, keepdims=True))
                a_dgl[sl, ln] = jnp.where(last_row, dlast, 0.0)
                dS[hh] = dSt * eT_j + _dot(do_b[sl, ln], qt_b[sl, ln], TN)
        dqt = dqt + a_dqt[...]
        dv = dv + a_dv[...]
        dkh = a_dkh[...]
        dG = dqt * s["qt"] - dkt * s["kt"] - dkh * s["kh"] + a_dgl[...]
        rev_causal = jnp.logical_and(s["same"], s["col"] >= s["row"])
        dgl = _dot01(rev_causal, dG, NN, "a")
        dk = dkt * s["eGn"] + dkh * s["eTG"]
        dfg = dgl / s["fg"] - dk
        lb, sg = s["lb"], s["sg"]
        keep = s["valid"].astype(F32)
        d_ref[:, 0:w] = (dqt * s["eG"] * _dsilu(s["qv"], s["sq"]) * keep).astype(d_ref.dtype)
        d_ref[:, w:2 * w] = (dfg * (1.0 - lb) * sg * (1.0 - sg) * keep).astype(d_ref.dtype)
        d_ref[:, 2 * w:3 * w] = (dv * keep).astype(d_ref.dtype)
        d_ref[:, 3 * w:4 * w] = (dg_out * keep).astype(d_ref.dtype)
        dlb = jnp.sum(dfg * (1.0 - sg) * keep, axis=0, keepdims=True) * lb * (1.0 - lb)
        dhb = jnp.concatenate([dlb, -dlb], axis=0)

        @pl.when(first_step)
        def _():
            dhb_ref[...] = dhb
            dnw_ref[...] = dnw

        @pl.when(jnp.logical_not(first_step))
        def _():
            dhb_ref[...] += dhb
            dnw_ref[...] += dnw

    w = 128 * HG_HP
    rowblk = pl.BlockSpec((Q, w), lambda h, b, t: (b * nb + nb - 1 - t, h))
    return _call(
        body, name="hgrn_bwd", grid=(HG_HEADS // HG_HP, Bl, nb),
        in_specs=_hg_specs(nb, rev=True) + [
            rowblk, pl.BlockSpec((1, HG_HP, 1, NSUB, 128, 128), lambda h, b, t: (b, h, nb - 1 - t, 0, 0, 0)), rowblk],
        out_specs=[pl.BlockSpec((pl.Element(Q), pl.Element(4 * w)),
                                lambda h, b, t: (pl.multiple_of((b * nb + nb - 1 - t) * Q, Q), 3 * HG_WIDTH)),
                   pl.BlockSpec((2, w), lambda h, b, t: (0, h)), pl.BlockSpec((1, w), lambda h, b, t: (0, h))],
        out_shape=[jax.ShapeDtypeStruct(dproj.shape, BF16),
                   jax.ShapeDtypeStruct((2, HG_WIDTH), F32), jax.ShapeDtypeStruct((1, HG_WIDTH), F32)],
        scratch=[pltpu.VMEM((HG_HP, 128, 128), F32)] + [pltpu.VMEM((Q, w), F32)] * 4,
        sem=("parallel", "arbitrary", "arbitrary"),
        args=(proj, proj, proj, proj, hb, nw, o_saved, st_saved, dyb), comm=comm, into=(dproj, 0))


def _adamw(name, parts, w, m, v, comm=None):
    R, C = w.shape
    S = parts.shape[0]
    tr, tc = (_tile(R, (256, 176, 128, 64, 8)), C) if R % 8 == 0 else (R, 256)
    c1, c2 = 1.0 - ADAM_B1 ** ADAM_STEP, 1.0 - ADAM_B2 ** ADAM_STEP

    def body(p_ref, w_ref, m_ref, v_ref, g_ref, d_ref, nm_ref, nv_ref):
        g = p_ref[0].astype(F32)
        for s in range(1, S):
            g = g + p_ref[s].astype(F32)
        nm = ADAM_B1 * m_ref[...] + (1.0 - ADAM_B1) * g
        nv = ADAM_B2 * v_ref[...] + (1.0 - ADAM_B2) * (g * g)
        g_ref[...] = g
        nm_ref[...] = nm
        nv_ref[...] = nv
        d_ref[...] = -ADAM_LR * ((nm / c1) / (jnp.sqrt(nv / c2) + ADAM_EPS) + ADAM_WD * w_ref[...])

    blk = pl.BlockSpec((tr, tc), lambda i, j: (i, j))
    return _call(
        body, name=name, grid=(R // tr, C // tc),
        in_specs=[pl.BlockSpec((S, tr, tc), lambda i, j: (0, i, j)), blk, blk, blk], out_specs=[blk] * 4,
        out_shape=[jax.ShapeDtypeStruct((R, C), F32)] * 4, scratch=[], sem=("parallel", "parallel"),
        args=(parts, w, m, v), comm=comm)


def _pair_sum(name, by_core, arrived):
    _, J, R, C = by_core.shape
    tc = _tile(C, (512, 256, 128))

    def body(c_ref, a_ref, b_ref, o_ref):
        o_ref[...] = (a_ref[0].astype(F32) + b_ref[...].astype(F32)).astype(o_ref.dtype)

    blk = pl.BlockSpec((1, R, tc), lambda j, k, c_ref: (j, 0, k))
    return pl.pallas_call(
        body, name=name,
        grid_spec=pltpu.PrefetchScalarGridSpec(
            num_scalar_prefetch=1, grid=(J, C // tc),
            in_specs=[pl.BlockSpec((1, 1, R, tc), lambda j, k, c_ref: (c_ref[0], j, 0, k)), blk], out_specs=blk),
        out_shape=jax.ShapeDtypeStruct(arrived.shape, arrived.dtype), compiler_params=_params(("parallel", "parallel")),
    )(lax.axis_index("c").astype(jnp.int32).reshape(1), by_core, arrived)


def _sum_parts(name, parts):
    S, R, C = parts.shape

    def body(p_ref, o_ref):
        g = p_ref[0]
        for s in range(1, S):
            g = g + p_ref[s]
        o_ref[...] = g

    return pl.pallas_call(
        body, name=name, out_shape=jax.ShapeDtypeStruct((R, C), F32),
        in_specs=[pl.BlockSpec(memory_space=pltpu.VMEM)], out_specs=pl.BlockSpec(memory_space=pltpu.VMEM),
    )(parts)


def _heads_to_lanes(p):
    return jnp.pad(p, [(0, 0)] * (p.ndim - 1) + [(0, 128 - SSD_HEADS)])


def _lanes_to_heads(p):
    return p[..., :SSD_HEADS]


def _pack_rows(arrs):
    flat = jnp.concatenate([a.reshape(-1).astype(F32) for a in arrs])
    return jnp.pad(flat, (0, (-flat.shape[0]) % (8 * D_MODEL))).reshape(-1, D_MODEL)


def _unpack_rows(packed, like):
    flat, outs, at = packed.reshape(-1), [], 0
    for a in like:
        outs.append(flat[at:at + a.size].reshape(a.shape))
        at += a.size
    return outs


def _cols(gth):
    return jnp.transpose(gth, (1, 0, 2)).reshape(gth.shape[1], -1)


def _rows(gth):
    return gth.reshape(-1, gth.shape[2])


def _to_rows(g):
    return g.reshape(N_DEV, -1, g.shape[1]).astype(BF16)


def _by_core(g):
    return jnp.transpose(g.reshape(N_DEV // 2, 2, -1, g.shape[1]), (1, 0, 2, 3)).astype(BF16)


DT_ROW = 3072


def _chip_sums(tag, by_core, swap_in=None):
    arrived = swap_in(by_core) if swap_in else _exchange(tag + "_swap", "swap", by_core)
    return [_pair_sum(f"{tag}_chipsum{i}", m, a) for i, (m, a) in enumerate(zip(by_core, arrived))]


def _ffn_fwd_gu(tag, n, w_gu_t, comm=None):
    M = n.shape[0]
    F = w_gu_t.shape[0] // 2
    tm = _tile(M, (544, 256))
    outs = _fused_matmul(
        tag + "_gu", M, F, D_MODEL,
        [dict(a=n, b=w_gu_t, trans_b=True, acc=0, resident=True),
         dict(a=n, b=w_gu_t, trans_b=True, bn_off=1, acc=1, resident=True)], [],
        lambda accs, ex: (accs[0], accs[1], accs[0] * _sigmoid(accs[0]) * accs[1]),
        [BF16, BF16, BF16], 2, tm, F, D_MODEL, outer="i", comm=comm, sub=256)
    return (n, *outs[:3]), outs[3:]


def _rmsnorm_tile(x, w):
    return x * lax.rsqrt(jnp.mean(x * x, axis=-1, keepdims=True) + EPS) * w


def _ffn_fwd_down(tag, h, a, w_down, next_norm=None, comm=None):
    M = h.shape[0]
    F = w_down.shape[0]
    tm = _tile(M, (1088, 544, 256))
    if next_norm is None:
        (h_out,) = _fused_matmul(
            tag + "_down", M, D_MODEL, F, [dict(a=a, b=w_down, acc=0)], [(h, 0)],
            lambda accs, ex: (ex[0] + 0.5 * accs[0],), [F32], 1, tm, D_MODEL, F, outer="j", sub=256)
        return h_out

    def with_norm(accs, ex):
        h_new = ex[0] + 0.5 * accs[0]
        return h_new, _rmsnorm_tile(h_new, ex[1])

    return _fused_matmul(tag + "_down", M, D_MODEL, F, [dict(a=a, b=w_down, acc=0, resident=True)], [(h, 0)], with_norm,
                         [F32, BF16], 1, tm, D_MODEL, F, outer="j", vecs=[next_norm], comm=comm)


def _ffn_bwd(tag, dh, dh_b, h, norm_w, w_gu_t, w_down, saved, scatter=False):
    n, g, u, a = saved
    M = h.shape[0]
    F = w_down.shape[0]
    tm = _tile(M, (544, 256))
    tn = _tile(F, (1408, 704, 256))

    def swiglu_bwd(accs, ex):
        da, gv, uv = 0.5 * accs[0], ex[0].astype(F32), ex[1].astype(F32)
        s = _sigmoid(gv)
        return da * uv * _dsilu(gv, s), da * gv * s

    (dgu,) = _fused_matmul(
        tag + "_dact", M, F, D_MODEL, [dict(a=dh_b, b=w_down, trans_b=True, acc=0, resident=True)], [(g, 0), (u, 0)],
        swiglu_bwd, [BF16, BF16], 1, tm, F, D_MODEL, outer="i", stack=True, sub=256)
    tr = _tile(M, (2176, 256))
    (dw_down,) = _matmul_tn(tag + "_dwd", a, dh_b, tn, D_MODEL, tr, scale=0.5)
    dw_gu_t, *p_down = _matmul_tn(tag + "_dwgu", dgu, n, tn, D_MODEL, tr,
                                  comm=("scatter", [_to_rows(dw_down)]) if scatter else None)
    comm = None
    if scatter:
        comm = ("chips", _chip_sums(tag + "_wgu", [_by_core(dw_gu_t)]))
    def norm_bwd(accs, ex):
        dh_prev, dw = _rmsnorm_bwd_tile(accs[0], ex[0], ex[2], ex[1])
        return dh_prev, dh_prev, dw

    dh_prev, dh_prev_b, dnorm, *p_gu = _fused_matmul(
        tag + "_dn", M, D_MODEL, F,
        [dict(a=dgu, a_lead=0, b=w_gu_t, acc=0, resident=True),
         dict(a=dgu, a_lead=1, b=w_gu_t, bk_off=1, acc=0, resident=True)], [(h, 0), (dh, 0)],
        norm_bwd, [F32, BF16], 1, tm, D_MODEL, F, outer="i", comm=comm, vecs=[norm_w], row_sums=1)
    return (dh_prev, dh_prev_b, dnorm, *((p_gu[0], p_down[0]) if scatter else (dw_gu_t, dw_down)))


def kernel(x, meta_tokens, ffn1_norm, ffn1_w_gu, ffn1_w_down, mix_norm, w_in, ssd_conv_w, ssd_conv_b, ssd_dt_bias, ssd_a_log, ssd_d, ssd_norm, hg_lower_bound, hg_norm, w_branch_a, w_branch_b, w_out, ffn2_norm, ffn2_w_gu, ffn2_w_down, final_norm, loss_target, m_meta_tokens, m_ffn1_norm, m_ffn1_w_gu, m_ffn1_w_down, m_mix_norm, m_w_in, m_ssd_conv_w, m_ssd_conv_b, m_ssd_dt_bias, m_ssd_a_log, m_ssd_d, m_ssd_norm, m_hg_lower_bound, m_hg_norm, m_w_branch_a, m_w_branch_b, m_w_out, m_ffn2_norm, m_ffn2_w_gu, m_ffn2_w_down, m_final_norm, v_meta_tokens, v_ffn1_norm, v_ffn1_w_gu, v_ffn1_w_down, v_mix_norm, v_w_in, v_ssd_conv_w, v_ssd_conv_b, v_ssd_dt_bias, v_ssd_a_log, v_ssd_d, v_ssd_norm, v_hg_lower_bound, v_hg_norm, v_w_branch_a, v_w_branch_b, v_w_out, v_ffn2_norm, v_ffn2_w_gu, v_ffn2_w_down, v_final_norm):
    Bl, S, D = x.shape
    T = PAD + N_META + S
    nc = T // Q
    M = Bl * T
    me = 4 * lax.axis_index("x") + 2 * lax.axis_index("y") + lax.axis_index("c")

    bf = lambda a: a[0].astype(BF16)
    bft = lambda a: a[0].T.astype(BF16)
    g_meta, g_conv_w = _exchange("gather_small", "gather", [meta_tokens, ssd_conv_w[0]])
    meta_full, conv_w_full = _cols(g_meta), _cols(g_conv_w)
    bias_p, alog_p, d_p = _heads_to_lanes(ssd_dt_bias), _heads_to_lanes(ssd_a_log), _heads_to_lanes(ssd_d)
    final_w = final_norm.reshape(1, D)

    h0, n1, g_wgu1 = _embed_norm(x, meta_full, ffn1_norm, comm=("gather", [bft(ffn1_w_gu)]))
    wgu1 = _rows(g_wgu1)
    tm = _tile(M, (1088, 544, 256))
    win_shard = bft(w_in)
    cut = (win_shard.shape[0] // 32) * 16
    ffn1_saved, (g_wd1, g_win_a) = _ffn_fwd_gu("ffn1", n1, wgu1, comm=("gather", [bf(ffn1_w_down), win_shard[:cut]]))
    wd1 = _rows(g_wd1)
    h1, un, g_win_b = _ffn_fwd_down("ffn1", h0, ffn1_saved[3], wd1, next_norm=mix_norm,
                                    comm=("gather", [win_shard[cut:]]))
    win_t = _rows(jnp.concatenate([g_win_a, g_win_b], axis=1))
    win_dt = jnp.pad(win_t[DT_ROW:DT_ROW + SSD_HEADS], ((0, 128 - SSD_HEADS), (0, 0)))
    plain = lambda accs, ex: (accs[0],)
    proj, g_wa, g_wb, g_wo = _fused_matmul(
        "in_proj", M, N_MAIN, D, [dict(a=un, b=win_t, trans_b=True, acc=0, b_shift=(DT_ROW // 1536, SSD_HEADS))], [],
        plain, [BF16], 1, tm, 1536, D,
        outer="j", comm=("gather", [bf(w_branch_a), bf(w_branch_b), bf(w_out)], "early"), sub=512)
    wa, wb, wo = _rows(g_wa), _rows(g_wb), _rows(g_wo)
    (dtr,) = _fused_matmul("in_proj_dt", M, 128, D, [dict(a=un, b=win_dt, trans_b=True, acc=0)], [], plain, [F32], 1,
                           tm, 128, D, outer="j")
    xc = _conv_fwd(proj, conv_w_full, ssd_conv_b, Bl, T)
    ya, ssd_prev = _ssd_fwd(xc, dtr, proj, bias_p, alog_p, d_p, ssd_norm, Bl, nc)
    yb, hg_o, hg_st, g_wgu2, g_wd2 = _hgrn_fwd(proj, hg_lower_bound, hg_norm, Bl, nc,
                                               comm=("gather", [bft(ffn2_w_gu), bf(ffn2_w_down)]))
    wgu2, wd2 = _rows(g_wgu2), _rows(g_wd2)

    def branch_fwd(accs, ex):
        pa, pb = accs
        return pa, pb, _sigmoid(ex[0].astype(F32)) * pa + _sigmoid(ex[1].astype(F32)) * pb

    pa, pb, merged = _fused_matmul(
        "branches", M, D, D, [dict(a=ya, b=wa, acc=0), dict(a=yb, b=wb, acc=1)], [(proj, 7), (proj, 8)],
        branch_fwd, [BF16, BF16, BF16], 2, tm, D, D, outer="j", sub=256)
    def out_with_norm(accs, ex):
        h_new = ex[0] + accs[0]
        return h_new, _rmsnorm_tile(h_new, ex[1])

    h2, n2 = _fused_matmul("out_proj", M, D, D, [dict(a=merged, b=wo, acc=0)], [(h1, 0)], out_with_norm,
                           [F32, BF16], 1, tm, D, D, outer="j", vecs=[ffn2_norm])
    ffn2_saved, _ = _ffn_fwd_gu("ffn2", n2, wgu2)
    h3 = _ffn_fwd_down("ffn2", h2, ffn2_saved[3], wd2)

    dh3, dh3_b, d_final, loss_part = _loss_head(h3, final_w, loss_target, Bl, nc)
    dh2, dh2_b, d_ffn2_norm, d_wgu2, d_wd2 = _ffn_bwd("ffn2", dh3, dh3_b, h2, ffn2_norm, wgu2, wd2, ffn2_saved)

    def branch_bwd(accs, ex):
        dm = accs[0]
        ga, gb, pav, pbv = (e.astype(F32) for e in ex)
        sa, sb = _sigmoid(ga), _sigmoid(gb)
        return (dm * sa, dm * sb,
                jnp.concatenate([dm * pav * sa * (1.0 - sa), dm * pbv * sb * (1.0 - sb)], axis=1))

    d_merged_outs = []

    def d_merged_with_swap(theirs):
        d_merged_outs.extend(_fused_matmul(
            "d_merged", M, D, D, [dict(a=dh2_b, b=wo, trans_b=True, acc=0)], [(proj, 7), (proj, 8), (pa, 0), (pb, 0)],
            branch_bwd, [BF16] * 2, 1, tm, D, D, outer="j", comm=("swap", theirs),
            wide=dict(width=2 * D, col=7 * D, total=N_MAIN, dtype=BF16)))
        return d_merged_outs[3:]

    s_ffn2 = _chip_sums("ffn2", [_by_core(d_wgu2), _by_core(d_wd2)], swap_in=d_merged_with_swap)
    dpa, dpb, dproj = d_merged_outs[:3]
    (d_wo,) = _matmul_tn("d_w_out", merged, dh2_b, 512, D, M)
    (d_wa,) = _matmul_tn("d_w_a", ya, dpa, 512, D, M)
    (d_wb,) = _matmul_tn("d_w_b", yb, dpb, 512, D, M)
    dya, dyb = _fused_matmul(
        "d_branches", M, D, D, [dict(a=dpa, b=wa, trans_b=True, acc=0), dict(a=dpb, b=wb, trans_b=True, acc=1)], [],
        lambda accs, ex: (accs[0], accs[1]), [BF16, BF16], 2, tm, D, D, outer="j")
    *ssd_grads, p_wgu2, p_wd2 = _ssd_bwd(xc, dtr, proj, bias_p, alog_p, d_p, ssd_norm, ssd_prev, dya, dproj, Bl, nc,
                                         comm=("chips", s_ffn2))
    dxc, dproj, ddtr, d_bias_p, d_alog_p, d_d_p, d_ssd_norm = ssd_grads
    dproj, d_conv_w, d_conv_b = _conv_bwd(proj, conv_w_full, ssd_conv_b, dxc, dproj, Bl, T)
    dproj, d_hb, d_hg_norm, p_wa, p_wb, p_wo = _hgrn_bwd(
        proj, hg_lower_bound, hg_norm, hg_o, hg_st, dyb, dproj, Bl, nc,
        comm=("scatter", [_to_rows(d_wa), _to_rows(d_wb), _to_rows(d_wo)]))
    ddtr_b = ddtr.astype(BF16)
    (d_win_t,) = _matmul_tn("d_w_in", dproj, un, 768, D, M, out_skip=(DT_ROW, SSD_HEADS))
    (d_win_dt,) = _matmul_tn("d_w_in_dt", ddtr_b, un, 128, D, M)
    d_win_t = lax.dynamic_update_slice(d_win_t, d_win_dt[:SSD_HEADS], (DT_ROW, 0))
    d_un_dt_outs = []

    def d_un_dt_with_swap(theirs):
        d_un_dt_outs.extend(_fused_matmul("d_un_dt", M, D, 128, [dict(a=ddtr_b, b=win_dt, acc=0)], [], plain, [F32], 1,
                                          tm, D, 128, outer="j", comm=("swap", theirs)))
        return d_un_dt_outs[1:]

    s_win = _chip_sums("w_in", [_by_core(d_win_t)], swap_in=d_un_dt_with_swap)
    def mix_norm_bwd(accs, ex):
        dh, dw = _rmsnorm_bwd_tile(accs[0] + ex[0], ex[1], ex[3], ex[2])
        return dh, dh, dw

    dh1, dh1_b, d_mix_norm, p_win = _fused_matmul(
        "d_un", M, D, N_MAIN, [dict(a=dproj, b=win_t, acc=0, b_shift=(DT_ROW // 3072, SSD_HEADS))],
        [(d_un_dt_outs[0], 0), (h1, 0), (dh2, 0)],
        mix_norm_bwd, [F32, BF16], 1, _tile(M, (544, 256)), D, 3072, outer="i", comm=("chips", s_win),
        vecs=[mix_norm], row_sums=1)
    dh0, _, d_ffn1_norm, p_wgu1, p_wd1 = _ffn_bwd("ffn1", dh1, dh1_b, h0, ffn1_norm, wgu1, wd1, ffn1_saved, scatter=True)

    dh0 = dh0.reshape(Bl, T, D)
    grad_x = dh0[:, PAD + N_META:]
    d_meta = dh0[:, PAD:PAD + N_META]

    small_grads = [d_ffn1_norm, d_mix_norm, d_conv_b, _lanes_to_heads(d_bias_p), _lanes_to_heads(d_alog_p),
                   _lanes_to_heads(d_d_p), d_ssd_norm, d_hb, d_hg_norm, d_ffn2_norm, d_final.reshape(D), d_conv_w]
    small_like = small_grads + [d_meta[b] for b in range(Bl)] + [loss_part[0, 0:1]]
    small_packed = _pack_rows(small_like)
    parts = [p_wgu1, p_wd1, p_win, p_wa, p_wb, p_wo, p_wgu2, p_wd2]

    names = ["meta_tokens", "ffn1_norm", "ffn1_w_gu", "ffn1_w_down", "mix_norm", "w_in", "ssd_conv_w", "ssd_conv_b",
             "ssd_dt_bias", "ssd_a_log", "ssd_d", "ssd_norm", "hg_lower_bound", "hg_norm", "w_branch_a", "w_branch_b",
             "w_out", "ffn2_norm", "ffn2_w_gu", "ffn2_w_down", "final_norm"]
    W = dict(meta_tokens=meta_tokens, ffn1_norm=ffn1_norm, ffn1_w_gu=ffn1_w_gu, ffn1_w_down=ffn1_w_down, mix_norm=mix_norm,
             w_in=w_in, ssd_conv_w=ssd_conv_w, ssd_conv_b=ssd_conv_b, ssd_dt_bias=ssd_dt_bias, ssd_a_log=ssd_a_log,
             ssd_d=ssd_d, ssd_norm=ssd_norm, hg_lower_bound=hg_lower_bound, hg_norm=hg_norm, w_branch_a=w_branch_a,
             w_branch_b=w_branch_b, w_out=w_out, ffn2_norm=ffn2_norm, ffn2_w_gu=ffn2_w_gu, ffn2_w_down=ffn2_w_down,
             final_norm=final_norm)
    Mo = dict(meta_tokens=m_meta_tokens, ffn1_norm=m_ffn1_norm, ffn1_w_gu=m_ffn1_w_gu, ffn1_w_down=m_ffn1_w_down,
              mix_norm=m_mix_norm, w_in=m_w_in, ssd_conv_w=m_ssd_conv_w, ssd_conv_b=m_ssd_conv_b, ssd_dt_bias=m_ssd_dt_bias,
              ssd_a_log=m_ssd_a_log, ssd_d=m_ssd_d, ssd_norm=m_ssd_norm, hg_lower_bound=m_hg_lower_bound, hg_norm=m_hg_norm,
              w_branch_a=m_w_branch_a, w_branch_b=m_w_branch_b, w_out=m_w_out, ffn2_norm=m_ffn2_norm, ffn2_w_gu=m_ffn2_w_gu,
              ffn2_w_down=m_ffn2_w_down, final_norm=m_final_norm)
    Vo = dict(meta_tokens=v_meta_tokens, ffn1_norm=v_ffn1_norm, ffn1_w_gu=v_ffn1_w_gu, ffn1_w_down=v_ffn1_w_down,
              mix_norm=v_mix_norm, w_in=v_w_in, ssd_conv_w=v_ssd_conv_w, ssd_conv_b=v_ssd_conv_b, ssd_dt_bias=v_ssd_dt_bias,
              ssd_a_log=v_ssd_a_log, ssd_d=v_ssd_d, ssd_norm=v_ssd_norm, hg_lower_bound=v_hg_lower_bound, hg_norm=v_hg_norm,
              w_branch_a=v_w_branch_a, w_branch_b=v_w_branch_b, w_out=v_w_out, ffn2_norm=v_ffn2_norm, ffn2_w_gu=v_ffn2_w_gu,
              ffn2_w_down=v_ffn2_w_down, final_norm=v_final_norm)
    grads, deltas, new_m, new_v = {}, {}, {}, {}
    big_names = ["ffn1_w_gu", "ffn1_w_down", "w_in", "w_branch_a", "w_branch_b", "w_out", "ffn2_w_gu", "ffn2_w_down"]
    transposed = ("ffn1_w_gu", "ffn2_w_gu", "w_in")
    small_all = None
    for nm, part in zip(big_names, parts):
        view = (lambda a: a[0].T) if nm in transposed else (lambda a: a[0])
        back = (lambda o: o.T[None]) if nm in transposed else (lambda o: o[None])
        outs = _adamw("adamw_" + nm, part, view(W[nm]), view(Mo[nm]), view(Vo[nm]),
                      comm=("gather", [small_packed]) if small_all is None else None)
        if small_all is None:
            small_all = outs[4]
        grads[nm], deltas[nm], new_m[nm], new_v[nm] = (back(o) for o in outs[:4])
    unpacked = _unpack_rows(_sum_parts("sum_small_grads", small_all), small_like)
    g_small = unpacked[:len(small_grads)]
    g_meta_full = unpacked[len(small_grads)]
    for b in range(1, Bl):
        g_meta_full = g_meta_full + unpacked[len(small_grads) + b]
    g_meta = lax.dynamic_slice_in_dim(g_meta_full, me * (D // N_DEV), D // N_DEV, axis=1)
    g_conv_w = lax.dynamic_slice_in_dim(g_small[11], me * (SSD_CONV_CH // N_DEV), SSD_CONV_CH // N_DEV, axis=1)
    loss = unpacked[-1].reshape(())
    small_names = ["ffn1_norm", "mix_norm", "ssd_conv_b", "ssd_dt_bias", "ssd_a_log", "ssd_d", "ssd_norm", "hg_lower_bound",
                   "hg_norm", "ffn2_norm", "final_norm", "ssd_conv_w", "meta_tokens"]
    small_g = g_small[:11] + [g_conv_w.reshape(ssd_conv_w.shape), g_meta]
    pk = lambda d: _pack_rows([d[nm] for nm in small_names])
    outs = _adamw("adamw_small", _pack_rows(small_g)[None], pk(W), pk(Mo), pk(Vo))
    like = [W[nm] for nm in small_names]
    for dst, o in zip((grads, deltas, new_m, new_v), outs):
        for nm, val in zip(small_names, _unpack_rows(o, like)):
            dst[nm] = val

    return (loss, grad_x, *[grads[nm] for nm in names], *[deltas[nm] for nm in names],
            *[new_m[nm] for nm in names], *[new_v[nm] for nm in names])
```

```python
import functools

import jax
import jax.numpy as jnp
from jax import lax
from jax.experimental import pallas as pl
from jax.experimental.pallas import tpu as pltpu

F32, BF16 = jnp.float32, jnp.bfloat16
NN, NT, TN = ((1,), (0,)), ((1,), (1,)), ((0,), (0,))
MESH_AXES = ("x", "y", "c")
N_DEV = 8

D_MODEL = 1024
N_META = 16
EPS = 1e-6
SSD_HEADS, SSD_HEAD_DIM, SSD_GROUPS, SSD_STATE, SSD_CONV, Q = 16, 64, 4, 128, 4, 128
SSD_INNER = SSD_HEADS * SSD_HEAD_DIM
SSD_CONV_CH = SSD_INNER + 2 * SSD_GROUPS * SSD_STATE
HG_WIDTH, HG_HEADS, HG_CHUNK = 1024, 8, 16
PAD = Q - N_META
N_MAIN = 9 * 1024
ADAM_LR, ADAM_B1, ADAM_B2, ADAM_EPS, ADAM_WD, ADAM_STEP = 0.001, 0.9, 0.999, 1e-08, 0.01, 10
VMEM_LIMIT = 52 * 1024 * 1024


def _dot(a, b, dims):
    return lax.dot_general(a, b, (dims, ((), ())), preferred_element_type=F32)


def _dot01(a, b, dims, sel):
    x = b if sel == "a" else a
    hi = x.astype(BF16)
    r1 = x - hi.astype(F32)
    mid = r1.astype(BF16)
    lo = (r1 - mid.astype(F32)).astype(BF16)
    s = (a if sel == "a" else b).astype(BF16)
    parts = [_dot(s, p, dims) if sel == "a" else _dot(p, s, dims) for p in (hi, mid, lo)]
    return parts[0] + parts[1] + parts[2]


def _sigmoid(x):
    return 1.0 / (1.0 + jnp.exp(-x))


def _dsilu(x, s):
    return s * (1.0 + x * (1.0 - s))


def _softplus(x):
    e = jnp.exp(-jnp.abs(x))
    u = 1.0 + e
    log1p_e = jnp.where(u == 1.0, e, jnp.log(u) * e / (u - 1.0))
    return jnp.maximum(x, 0.0) + log1p_e


def _params(sem):
    return pltpu.CompilerParams(dimension_semantics=sem, vmem_limit_bytes=VMEM_LIMIT)


def _tile(n, prefs):
    for p in prefs:
        if n % p == 0:
            return p
    return n


CHIP_FLIPS = ((1, 0), (0, 1), (1, 1))
N_PEER = N_DEV - 1


def _comm_gather(srcs, outs, send_sems, recv_sems, local_sems):
    n = len(srcs)
    x, y, c = (lax.axis_index(a) for a in MESH_AXES)
    dev = lambda px, py, pc: 4 * px + 2 * py + pc
    me, sib = dev(x, y, c), (x, y, 1 - c)

    def rc(w, k, slot, to, src=None):
        return pltpu.make_async_remote_copy(
            src_ref=outs[w].at[slot] if src is None else src, dst_ref=outs[w].at[slot],
            send_sem=send_sems.at[w, k], recv_sem=recv_sems.at[w, k], device_id=to, device_id_type=pl.DeviceIdType.MESH)

    def local(w):
        return pltpu.make_async_copy(srcs[w], outs[w].at[me], local_sems.at[w])

    def start():
        for w in range(n):
            local(w).start()
            rc(w, 0, me, sib, src=srcs[w]).start()
            for j, (fx, fy) in enumerate(CHIP_FLIPS):
                rc(w, 1 + j, me, (x ^ fx, y ^ fy, c), src=srcs[w]).start()

    def pass_on():
        for w in range(n):
            for j, (fx, fy) in enumerate(CHIP_FLIPS):
                slot = dev(x ^ fx, y ^ fy, c)
                rc(w, 1 + j, slot, sib).wait_recv()
                rc(w, 4 + j, slot, sib).start()

    def finish():
        for w in range(n):
            rc(w, 0, dev(x, y, 1 - c), sib).wait_recv()
            rc(w, 0, me, sib, src=srcs[w]).wait_send()
            for j, (fx, fy) in enumerate(CHIP_FLIPS):
                rc(w, 4 + j, dev(x ^ fx, y ^ fy, 1 - c), sib).wait_recv()
                rc(w, 1 + j, me, sib, src=srcs[w]).wait_send()
                rc(w, 4 + j, dev(x ^ fx, y ^ fy, c), sib).wait_send()
            local(w).wait()

    return start, pass_on, finish


def _comm_scatter(srcs, outs, send_sems, recv_sems, local_sems):
    n = len(srcs)
    x, y, c = (lax.axis_index(a) for a in MESH_AXES)
    me = 4 * x + 2 * y + c

    def copies():
        out = []
        for w in range(n):
            out.append(pltpu.make_async_copy(srcs[w].at[me], outs[w].at[me], local_sems.at[w]))
            for k in range(1, N_DEV):
                px, py, pc = x ^ (k >> 2), y ^ ((k >> 1) & 1), c ^ (k & 1)
                out.append(pltpu.make_async_remote_copy(
                    src_ref=srcs[w].at[4 * px + 2 * py + pc], dst_ref=outs[w].at[me],
                    send_sem=send_sems.at[w, k - 1], recv_sem=recv_sems.at[w, k - 1],
                    device_id=(px, py, pc), device_id_type=pl.DeviceIdType.MESH))
        return out

    def start():
        for cp in copies():
            cp.start()

    def finish():
        for cp in copies():
            cp.wait()

    return start, None, finish


def _comm_swap(srcs, outs, send_sems, recv_sems, local_sems):
    x, y, c = (lax.axis_index(a) for a in MESH_AXES)

    def copies():
        return [pltpu.make_async_remote_copy(
            src_ref=srcs[w].at[1 - c], dst_ref=outs[w], send_sem=send_sems.at[w, 0], recv_sem=recv_sems.at[w, 0],
            device_id=(x, y, 1 - c), device_id_type=pl.DeviceIdType.MESH) for w in range(len(srcs))]

    def start():
        for cp in copies():
            cp.start()

    def finish():
        for cp in copies():
            cp.wait()

    return start, None, finish


def _comm_chips(srcs, outs, send_sems, recv_sems, local_sems):
    n = len(srcs)
    x, y, c = (lax.axis_index(a) for a in MESH_AXES)
    mine = 2 * x + y

    def copies():
        out = []
        for w in range(n):
            out.append(pltpu.make_async_copy(srcs[w].at[mine], outs[w].at[mine], local_sems.at[w]))
            for j, (fx, fy) in enumerate(CHIP_FLIPS):
                px, py = x ^ fx, y ^ fy
                out.append(pltpu.make_async_remote_copy(
                    src_ref=srcs[w].at[2 * px + py], dst_ref=outs[w].at[mine],
                    send_sem=send_sems.at[w, j], recv_sem=recv_sems.at[w, j],
                    device_id=(px, py, c), device_id_type=pl.DeviceIdType.MESH))
        return out

    def start():
        for cp in copies():
            cp.start()

    def finish():
        for cp in copies():
            cp.wait()

    return start, None, finish


def _comm_parts(comm):
    kind, arrays = comm[:2]
    n = len(arrays)
    lead = {"gather": lambda a: (N_DEV,) + a.shape, "scatter": lambda a: (N_DEV,) + a.shape[1:],
            "swap": lambda a: a.shape[1:], "chips": lambda a: a.shape}[kind]
    shapes = [jax.ShapeDtypeStruct(lead(a), a.dtype) for a in arrays]
    sems = [pltpu.SemaphoreType.DMA((n, N_PEER)), pltpu.SemaphoreType.DMA((n, N_PEER)), pltpu.SemaphoreType.DMA((n,))]
    make = {"gather": _comm_gather, "scatter": _comm_scatter, "swap": _comm_swap, "chips": _comm_chips}[kind]
    return n, shapes, sems, make


def _exchange(name, kind, arrays):
    n, shapes, sems, make = _comm_parts((kind, arrays))

    def body(*refs):
        start, middle, finish = make(refs[:n], refs[n:2 * n], *refs[2 * n:])
        start()
        if middle:
            middle()
        finish()

    any_spec = pl.BlockSpec(memory_space=pl.ANY)
    return pl.pallas_call(
        body, name=name, in_specs=[any_spec] * n, out_specs=[any_spec] * n, out_shape=shapes, scratch_shapes=sems,
        compiler_params=pltpu.CompilerParams(has_side_effects=True),
    )(*arrays)


def _call(body, *, name, grid, in_specs, out_specs, out_shape, scratch, sem, args, comm=None, into=None):
    any_spec = pl.BlockSpec(memory_space=pl.ANY)
    in_specs, args, aliases, n_body_in = list(in_specs), list(args), {}, len(in_specs)
    if into is not None:
        in_specs.append(any_spec)
        args.append(into[0])
        aliases = {n_body_in: into[1]}
    n_in, n_out, n_scr = len(in_specs), len(out_specs), len(scratch)
    if comm is None:
        def plain(*refs):
            body(*refs[:n_body_in], *refs[n_in:])

        return pl.pallas_call(plain, name=name, grid=grid, in_specs=in_specs, out_specs=out_specs, out_shape=out_shape,
                              scratch_shapes=scratch, input_output_aliases=aliases, compiler_params=_params(sem))(*args)
    n, shapes, sems, make = _comm_parts(comm)

    def carrier(*refs):
        ins, csrc = refs[:n_body_in], refs[n_in:n_in + n]
        outs, cout = refs[n_in + n:n_in + n + n_out], refs[n_in + n + n_out:n_in + 2 * n + n_out]
        rest = refs[n_in + 2 * n + n_out:]
        start, middle, finish = make(csrc, cout, *rest[n_scr:])
        ids = [pl.program_id(a) for a in range(len(grid))]
        step = functools.reduce(lambda acc, ig: acc * ig[1] + ig[0], zip(ids, grid), 0)
        n_steps = functools.reduce(lambda a, b: a * b, grid, 1)
        pl.when(step == 0)(start)
        body(*ins, *outs, *rest[:n_scr])
        if middle:
            early = len(comm) > 2 and comm[2] == "early"
            pl.when(step == ((3 * n_steps) // 4 if early else n_steps - 1))(middle)
        pl.when(step == n_steps - 1)(finish)

    return pl.pallas_call(
        carrier, name=name, grid=grid, in_specs=in_specs + [any_spec] * n,
        out_specs=list(out_specs) + [any_spec] * n, out_shape=list(out_shape) + shapes,
        scratch_shapes=list(scratch) + sems, input_output_aliases=aliases,
        compiler_params=pltpu.CompilerParams(dimension_semantics=("arbitrary",) * len(grid),
                                             vmem_limit_bytes=VMEM_LIMIT, has_side_effects=True),
    )(*args, *comm[1])


def _fused_matmul(name, M, N, K, pairs, extras, epilogue, out_dtypes, n_acc, tm, tn, tk, outer="i", comm=None,
                  stack=False, vecs=(), row_sums=0, wide=None, sub=None):
    nk = K // tk
    n_pairs, n_ex, n_out = len(pairs), len(extras), len(out_dtypes)
    assert not row_sums or (outer == "i" and N == tn)

    def ij(g0, g1):
        return (g0, g1) if outer == "i" else (g1, g0)

    in_specs, args = [], []
    for p in pairs:
        ao, bk, bn = p.get("a_off", 0), p.get("bk_off", 0), p.get("bn_off", 0)
        mode = dict(pipeline_mode=pl.Buffered(1)) if p.get("resident") else {}
        in_specs.append(pl.BlockSpec((tm, tk), lambda g0, g1, k, ao=ao: (ij(g0, g1)[0], k + ao)))
        if "b_shift" in p:
            first, shift = p["b_shift"]
            if p.get("trans_b"):
                in_specs.append(pl.BlockSpec(
                    (pl.Element(tn), pl.Element(tk)),
                    lambda g0, g1, k, bk=bk: (
                        pl.multiple_of(ij(g0, g1)[1] * tn + jnp.where(ij(g0, g1)[1] >= first, shift, 0), 16),
                        (k + bk) * tk)))
            else:
                in_specs.append(pl.BlockSpec(
                    (pl.Element(tk), pl.Element(tn)),
                    lambda g0, g1, k, bn=bn: (pl.multiple_of(k * tk + jnp.where(k >= first, shift, 0), 16),
                                              (ij(g0, g1)[1] + bn) * tn)))
        elif p.get("trans_b"):
            in_specs.append(pl.BlockSpec((tn, tk), lambda g0, g1, k, bk=bk, bn=bn: (ij(g0, g1)[1] + bn, k + bk), **mode))
        else:
            in_specs.append(pl.BlockSpec((tk, tn), lambda g0, g1, k, bk=bk, bn=bn: (k + bk, ij(g0, g1)[1] + bn), **mode))
        args += [p["a"], p["b"]]
    for arr, off in extras:
        in_specs.append(pl.BlockSpec((tm, tn), lambda g0, g1, k, off=off: (ij(g0, g1)[0], ij(g0, g1)[1] + off)))
        args.append(arr)
    for arr in vecs:
        in_specs.append(pl.BlockSpec((1, tn), lambda g0, g1, k: (0, ij(g0, g1)[1])))
        args.append(arr)
    if stack:
        assert N == tn
        out_specs = [pl.BlockSpec((tm, n_out * tn), lambda g0, g1, k: (ij(g0, g1)[0], 0))]
        out_shape = [jax.ShapeDtypeStruct((M, n_out * N), out_dtypes[0])]
    else:
        out_specs = [pl.BlockSpec((tm, tn), lambda g0, g1, k: ij(g0, g1)) for _ in out_dtypes]
        out_shape = [jax.ShapeDtypeStruct((M, N), dt) for dt in out_dtypes]
    if wide:
        out_specs.append(pl.BlockSpec((pl.Element(tm), pl.Element(wide["width"])),
                                      lambda g0, g1, k: (pl.multiple_of(ij(g0, g1)[0] * tm, 16), wide["col"])))
        out_shape.append(jax.ShapeDtypeStruct((M, wide["total"]), wide["dtype"]))
    n_tile_out = len(out_specs)
    out_specs += [pl.BlockSpec((1, tn), lambda g0, g1, k: (0, 0)) for _ in range(row_sums)]
    out_shape += [jax.ShapeDtypeStruct((1, N), F32) for _ in range(row_sums)]
    grid = (M // tm, N // tn, nk) if outer == "i" else (N // tn, M // tm, nk)
    n_in = 2 * n_pairs + n_ex + len(vecs)

    def partials(refs, cs=slice(None)):
        accs = [None] * n_acc
        for idx, p in enumerate(pairs):
            b_ref = refs[2 * idx + 1]
            d = (_dot(refs[2 * idx][...], b_ref[cs, :], NT) if p.get("trans_b")
                 else _dot(refs[2 * idx][...], b_ref[:, cs], NN))
            accs[p["acc"]] = d if accs[p["acc"]] is None else accs[p["acc"]] + d
        return accs

    def finish(accs, refs, first_rows, cs=slice(None)):
        res = epilogue(accs, [r[:, cs] for r in refs[2 * n_pairs:n_in]])
        if stack:
            o = refs[n_in]
            for idx in range(n_out):
                lo = idx * tn + (cs.start or 0)
                o[:, lo:lo + (tn if cs.stop is None else cs.stop - cs.start)] = res[idx].astype(o.dtype)
        else:
            for o, r in zip(refs[n_in:n_in + n_out], res):
                o[:, cs] = r.astype(o.dtype)
        if wide:
            o = refs[n_in + n_tile_out - 1]
            o[...] = res[n_out].astype(o.dtype)
        for o, r in zip(refs[n_in + n_tile_out:n_in + n_tile_out + row_sums], res[n_out + bool(wide):]):
            @pl.when(first_rows)
            def _(o=o, r=r):
                o[...] = r

            @pl.when(jnp.logical_not(first_rows))
            def _(o=o, r=r):
                o[...] += r

    if nk == 1 and sub:
        assert not wide and not row_sums and tn % sub == 0

        def body(*refs):
            for c in range(tn // sub):
                cs = slice(c * sub, (c + 1) * sub)
                finish(partials(refs, cs), refs, None, cs)
        scratch = []
    elif nk == 1:
        def body(*refs):
            finish(partials(refs), refs, pl.program_id(0) == 0)
        scratch = []
    else:
        def body(*refs):
            acc_refs = refs[-n_acc:]
            k = pl.program_id(2)
            first_rows = pl.program_id(0) == 0
            new = partials(refs)

            @pl.when(k == 0)
            def _():
                for a, v in zip(acc_refs, new):
                    a[...] = v

            @pl.when(k > 0)
            def _():
                for a, v in zip(acc_refs, new):
                    a[...] += v

            @pl.when(k == nk - 1)
            def _():
                finish([a[...] for a in acc_refs], refs, first_rows)
        scratch = [pltpu.VMEM((tm, tn), F32) for _ in range(n_acc)]

    return _call(body, name=name, grid=grid, in_specs=in_specs, out_specs=out_specs, out_shape=out_shape,
                 scratch=scratch, sem=("parallel", "parallel", "arbitrary"), args=args, comm=comm)


def _matmul_tn(name, x, y, t1, t2, tr, scale=1.0, comm=None, out_dtype=BF16, out_skip=None):
    R, K1 = x.shape
    N1 = y.shape[1]
    nr, n1 = R // tr, K1 // t1
    x_spec = pl.BlockSpec((tr, t1), lambda i, j, r: (r, i))
    rows_out = K1
    o_spec = pl.BlockSpec((t1, t2), lambda i, j, r: (i, j))
    if out_skip:
        row, count = out_skip
        rows_out += count
        o_spec = pl.BlockSpec(
            (pl.Element(t1), pl.Element(t2)),
            lambda i, j, r: (pl.multiple_of(i * t1 + jnp.where(i * t1 >= row, count, 0), 16), j * t2))

    def body(x_ref, y_ref, o_ref, *acc):
        d = _dot(x_ref[...], y_ref[...], TN)
        if nr == 1:
            o_ref[...] = (d * scale).astype(o_ref.dtype)
            return
        r = pl.program_id(2)

        @pl.when(r == 0)
        def _():
            acc[0][...] = d

        @pl.when(jnp.logical_and(r > 0, r < nr - 1))
        def _():
            acc[0][...] += d

        @pl.when(r == nr - 1)
        def _():
            o_ref[...] = ((acc[0][...] + d) * scale).astype(o_ref.dtype)

    return _call(
        body, name=name, grid=(n1, N1 // t2, nr),
        in_specs=[x_spec, pl.BlockSpec((tr, t2), lambda i, j, r: (r, j))], out_specs=[o_spec],
        out_shape=[jax.ShapeDtypeStruct((rows_out, N1), out_dtype)],
        scratch=[pltpu.VMEM((t1, t2), F32)] if nr > 1 else [],
        sem=("parallel", "parallel", "arbitrary"), args=(x, y), comm=comm)


def _embed_norm(x, meta, w, comm=None):
    Bl, S, D = x.shape
    nb = (PAD + N_META + S) // Q
    M = Bl * nb * Q

    def body(x_ref, meta_ref, w_ref, h_ref, n_ref):
        head = jnp.concatenate([jnp.zeros((PAD, D), F32), meta_ref[...]], axis=0)
        h = jnp.where(pl.program_id(1) == 0, head, x_ref[0])
        h_ref[...] = h
        n_ref[...] = _rmsnorm_tile(h, w_ref[...]).astype(n_ref.dtype)

    row = pl.BlockSpec((Q, D), lambda b, t: (b * nb + t, 0))
    return _call(
        body, name="embed_norm", grid=(Bl, nb),
        in_specs=[pl.BlockSpec((1, Q, D), lambda b, t: (b, jnp.maximum(t - 1, 0), 0)),
                  pl.BlockSpec((N_META, D), lambda b, t: (0, 0)), pl.BlockSpec((1, D), lambda b, t: (0, 0))],
        out_specs=[row, row], out_shape=[jax.ShapeDtypeStruct((M, D), F32), jax.ShapeDtypeStruct((M, D), BF16)],
        scratch=[], sem=("parallel", "parallel"), args=(x, meta, w), comm=comm)


def _rmsnorm_bwd_tile(dn, h, w, dh_in):
    r = lax.rsqrt(jnp.mean(h * h, axis=-1, keepdims=True) + EPS)
    xhat = h * r
    gw = dn * w
    dh = dh_in + r * (gw - xhat * jnp.mean(gw * xhat, axis=-1, keepdims=True))
    return dh, jnp.sum(dn * xhat, axis=0, keepdims=True)


def _loss_head(h, w, target, Bl, nb):
    M, D = h.shape

    def body(h_ref, w_ref, t_ref, dh_ref, dhb_ref, dw_ref, loss_ref):
        b, t = pl.program_id(0), pl.program_id(1)
        live = (t > 0).astype(F32)
        x = h_ref[...]
        r = lax.rsqrt(jnp.mean(x * x, axis=-1, keepdims=True) + EPS)
        xhat = x * r
        wv = w_ref[...]
        err = (xhat * wv - t_ref[0]) * live
        dy = err * (1.0 / D)
        gw = dy * wv
        dx = r * (gw - xhat * jnp.mean(gw * xhat, axis=-1, keepdims=True))
        dh_ref[...] = dx
        dhb_ref[...] = dx.astype(BF16)
        dw = jnp.sum(dy * xhat, axis=0, keepdims=True)
        part = 0.5 * jnp.sum(jnp.sum(err * err, axis=-1, keepdims=True) * (1.0 / D), axis=0, keepdims=True)
        first = jnp.logical_and(b == 0, t == 0)

        @pl.when(first)
        def _():
            dw_ref[...] = dw
            loss_ref[...] = jnp.broadcast_to(part, loss_ref.shape)

        @pl.when(jnp.logical_not(first))
        def _():
            dw_ref[...] += dw
            loss_ref[...] += jnp.broadcast_to(part, loss_ref.shape)

    row = pl.BlockSpec((Q, D), lambda b, t: (b * nb + t, 0))
    vec = pl.BlockSpec((1, D), lambda b, t: (0, 0))
    return pl.pallas_call(
        body, name="loss_head", grid=(Bl, nb),
        in_specs=[row, vec, pl.BlockSpec((1, Q, D), lambda b, t: (b, jnp.maximum(t - 1, 0), 0))],
        out_specs=[row, row, vec, pl.BlockSpec((8, 128), lambda b, t: (0, 0))],
        out_shape=[jax.ShapeDtypeStruct((M, D), F32), jax.ShapeDtypeStruct((M, D), BF16),
                   jax.ShapeDtypeStruct((1, D), F32), jax.ShapeDtypeStruct((8, 128), F32)],
        compiler_params=_params(("arbitrary", "arbitrary")),
    )(h, w, target)


CONV_TC = 256


def _conv_pre(xr_ref, w_ref, b_ref):
    x = xr_ref[...].astype(F32)
    acc = b_ref[...] + w_ref[SSD_CONV - 1:SSD_CONV, :] * x
    for k in range(1, SSD_CONV):
        acc = acc + w_ref[SSD_CONV - 1 - k:SSD_CONV - k, :] * pltpu.roll(x, k, 0)
    return x, acc


def _conv_fwd(proj, w, b, Bl, T):
    M = proj.shape[0]
    off = 1024 // CONV_TC

    def body(xr_ref, w_ref, b_ref, o_ref):
        _, acc = _conv_pre(xr_ref, w_ref, b_ref)
        row = lax.broadcasted_iota(jnp.int32, acc.shape, 0)
        o_ref[...] = jnp.where(row >= PAD, acc * _sigmoid(acc), 0.0).astype(o_ref.dtype)

    return pl.pallas_call(
        body, name="conv_fwd", grid=(Bl, SSD_CONV_CH // CONV_TC),
        in_specs=[pl.BlockSpec((T, CONV_TC), lambda bb, j: (bb, j + off)),
                  pl.BlockSpec((SSD_CONV, CONV_TC), lambda bb, j: (0, j)), pl.BlockSpec((1, CONV_TC), lambda bb, j: (0, j))],
        out_specs=pl.BlockSpec((T, CONV_TC), lambda bb, j: (bb, j)),
        out_shape=jax.ShapeDtypeStruct((M, SSD_CONV_CH), BF16), compiler_params=_params(("parallel", "parallel")),
    )(proj, w, b)


def _conv_bwd(proj, w, b, dxc, dproj, Bl, T):
    M = proj.shape[0]
    off = 1024 // CONV_TC

    def body(xr_ref, w_ref, b_ref, d_ref, dx_ref, dw_ref, db_ref):
        x, acc = _conv_pre(xr_ref, w_ref, b_ref)
        row = lax.broadcasted_iota(jnp.int32, acc.shape, 0)
        s = _sigmoid(acc)
        dpre = jnp.where(row >= PAD, d_ref[...].astype(F32) * _dsilu(acc, s), 0.0)
        dx = w_ref[SSD_CONV - 1:SSD_CONV, :] * dpre
        dws = [jnp.sum(dpre * x, axis=0, keepdims=True)]
        for k in range(1, SSD_CONV):
            dx = dx + w_ref[SSD_CONV - 1 - k:SSD_CONV - k, :] * pltpu.roll(dpre, T - k, 0)
            dws.append(jnp.sum(dpre * pltpu.roll(x, k, 0), axis=0, keepdims=True))
        dx_ref[...] = dx.astype(dx_ref.dtype)
        dw = jnp.concatenate(dws[::-1], axis=0)
        db = jnp.sum(dpre, axis=0, keepdims=True)

        @pl.when(pl.program_id(1) == 0)
        def _():
            dw_ref[...] = dw
            db_ref[...] = db

        @pl.when(pl.program_id(1) > 0)
        def _():
            dw_ref[...] += dw
            db_ref[...] += db

    return _call(
        body, name="conv_bwd", grid=(SSD_CONV_CH // CONV_TC, Bl),
        in_specs=[pl.BlockSpec((T, CONV_TC), lambda j, bb: (bb, j + off)),
                  pl.BlockSpec((SSD_CONV, CONV_TC), lambda j, bb: (0, j)), pl.BlockSpec((1, CONV_TC), lambda j, bb: (0, j)),
                  pl.BlockSpec((T, CONV_TC), lambda j, bb: (bb, j))],
        out_specs=[pl.BlockSpec((T, CONV_TC), lambda j, bb: (bb, j + off)),
                   pl.BlockSpec((SSD_CONV, CONV_TC), lambda j, bb: (0, j)), pl.BlockSpec((1, CONV_TC), lambda j, bb: (0, j))],
        out_shape=[jax.ShapeDtypeStruct(dproj.shape, BF16), jax.ShapeDtypeStruct((SSD_CONV, SSD_CONV_CH), F32),
                   jax.ShapeDtypeStruct((1, SSD_CONV_CH), F32)],
        scratch=[], sem=("parallel", "arbitrary"), args=(proj, w, b, dxc), into=(dproj, 0))


N_PAIR = SSD_HEADS // 2
HPG = SSD_HEADS // SSD_GROUPS
GW = SSD_INNER // SSD_GROUPS


def _per_group(fn, *arrs):
    return jnp.concatenate([jnp.broadcast_to(fn(*(a[:, GW * g:GW * (g + 1)] for a in arrs)), (arrs[0].shape[0], GW))
                            for g in range(SSD_GROUPS)], axis=1)


def _ssd_prep(c, dtr_ref, bias_ref, alog_ref, d_ref):
    row = lax.broadcasted_iota(jnp.int32, (Q, 128), 0)
    col = lax.broadcasted_iota(jnp.int32, (Q, 128), 1)
    live = col < SSD_HEADS
    valid = jnp.logical_and(jnp.logical_or(c > 0, row >= PAD), live)
    pre = dtr_ref[...] + bias_ref[...]
    dt = jnp.where(valid, _softplus(pre), 0.0)
    A = jnp.where(live[0:1], -jnp.exp(alog_ref[...]), 0.0)
    tri = row >= col
    eye = (row == col).astype(BF16)
    cs = _dot01(tri, dt * A, NN, "a")
    cst = _dot01(eye, cs, NT, "a")
    spread = (lax.broadcasted_iota(jnp.int32, (128, SSD_INNER), 0)
              == lax.broadcasted_iota(jnp.int32, (128, SSD_INNER), 1) // SSD_HEAD_DIM).astype(BF16)
    dt_w = _dot01(dt, spread, NN, "b")
    cs_w = _dot01(cs, spread, NN, "b")
    d_w = _dot01(jnp.broadcast_to(d_ref[...], (8, 128)), spread, NN, "b")[0:1]
    lane = lax.broadcasted_iota(jnp.int32, (Q, SSD_INNER), 1)
    first = (lane % 128) < SSD_HEAD_DIM
    return dict(row=row, col=col, valid=valid, pre=pre, dt=dt, A=A, tri=tri, eye=eye, cs=cs, cst=cst, spread=spread,
                dt_w=dt_w, cs_w=cs_w, d_w=d_w, ecs_w=jnp.exp(cs_w), decay_w=jnp.exp(cs_w[Q - 1:Q] - cs_w), first=first)


def _ssd_chunk(xc_ref, s, states):
    xv = xc_ref[:, 0:SSD_INNER].astype(F32)
    Bs = [xc_ref[:, SSD_INNER + 128 * g:SSD_INNER + 128 * (g + 1)] for g in range(SSD_GROUPS)]
    Cs = [xc_ref[:, SSD_INNER + 512 + 128 * g:SSD_INNER + 512 + 128 * (g + 1)] for g in range(SSD_GROUPS)]
    X = xv * s["dt_w"]
    X0 = jnp.where(s["first"], X, 0.0)
    Xb = (X0.astype(BF16), (X - X0).astype(BF16))
    Xd = (X * s["decay_w"]).astype(BF16)
    CB = [_dot(Cs[g], Bs[g], NT) for g in range(SSD_GROUPS)]
    Lms = [jnp.exp(jnp.where(s["tri"], s["cs"][:, h:h + 1] - s["cst"][h:h + 1, :], -jnp.inf)) for h in range(SSD_HEADS)]
    Ms = [CB[h // HPG] * Lms[h] for h in range(SSD_HEADS)]
    Mb = [m.astype(BF16) for m in Ms]
    prev_b = [st.astype(BF16) for st in states]
    yds, yos, sts = [], [], []
    for p in range(N_PAIR):
        g, ln = p // 2, slice(128 * p, 128 * (p + 1))
        yds.append(_dot(Mb[2 * p], Xb[0][:, ln], NN) + _dot(Mb[2 * p + 1], Xb[1][:, ln], NN))
        yos.append(_dot(Cs[g], prev_b[p], NT))
        sts.append(_dot(Xd[:, ln], Bs[g], TN))
    yo = jnp.concatenate(yos, axis=1)
    y = jnp.concatenate(yds, axis=1) + yo * s["ecs_w"] + xv * s["d_w"]
    upper = s["row"] < SSD_HEAD_DIM
    cl = s["cs"][Q - 1:Q, :]
    ecl_rows = [jnp.where(upper, jnp.exp(cl[:, 2 * p:2 * p + 1]), jnp.exp(cl[:, 2 * p + 1:2 * p + 2])) for p in range(N_PAIR)]
    new_states = [states[p] * ecl_rows[p] + sts[p] for p in range(N_PAIR)]
    return y, new_states, dict(xv=xv, Bs=Bs, Cs=Cs, X=X, Xb=Xb, CB=CB, Lms=Lms, Ms=Ms, Mb=Mb, prev_b=prev_b, yo=yo,
                               ecl_rows=ecl_rows)


def _ssd_in_specs(nc, rev=False):
    rb = (lambda b, c: b * nc + nc - 1 - c) if rev else (lambda b, c: b * nc + c)
    vec = pl.BlockSpec((1, 128), lambda b, c: (0, 0))
    return [pl.BlockSpec((Q, SSD_CONV_CH), lambda b, c: (rb(b, c), 0)),
            pl.BlockSpec((Q, 128), lambda b, c: (rb(b, c), 0)),
            pl.BlockSpec((Q, SSD_INNER), lambda b, c: (rb(b, c), 0)),
            vec, vec, vec, pl.BlockSpec((1, SSD_INNER), lambda b, c: (0, 0))]


def _ssd_fwd(xc, dtr, proj, bias_p, alog_p, d_p, nw, Bl, nc):
    M = xc.shape[0]

    def body(xc_ref, dtr_ref, z_ref, bias_ref, alog_ref, d_ref, nw_ref, y_ref, prev_ref, state):
        c = pl.program_id(1)

        @pl.when(c == 0)
        def _():
            state[...] = jnp.zeros_like(state)

        s = _ssd_prep(c, dtr_ref, bias_ref, alog_ref, d_ref)
        states = [state[p] for p in range(N_PAIR)]
        y, new_states, _ = _ssd_chunk(xc_ref, s, states)
        for p in range(N_PAIR):
            prev_ref[0, 0, p] = states[p]
            state[p] = new_states[p]
        zz = z_ref[...].astype(F32)
        yg = y * zz * _sigmoid(zz)
        r = _per_group(lambda a: lax.rsqrt(jnp.mean(a * a, axis=-1, keepdims=True) + EPS), yg)
        y_ref[...] = (yg * r * nw_ref[...]).astype(y_ref.dtype)

    return pl.pallas_call(
        body, name="ssd_fwd", grid=(Bl, nc), in_specs=_ssd_in_specs(nc),
        out_specs=[pl.BlockSpec((Q, SSD_INNER), lambda b, c: (b * nc + c, 0)),
                   pl.BlockSpec((1, 1, N_PAIR, 128, 128), lambda b, c: (b, c, 0, 0, 0))],
        out_shape=[jax.ShapeDtypeStruct((M, SSD_INNER), BF16), jax.ShapeDtypeStruct((Bl, nc, N_PAIR, 128, 128), F32)],
        scratch_shapes=[pltpu.VMEM((N_PAIR, 128, 128), F32)],
        compiler_params=_params(("arbitrary", "arbitrary")),
    )(xc, dtr, proj, bias_p, alog_p, d_p, nw)


def _ssd_bwd(xc, dtr, proj, bias_p, alog_p, d_p, nw, prev, dya, dproj, Bl, nc, comm=None):
    M = xc.shape[0]

    def body(xc_ref, dtr_ref, z_ref, bias_ref, alog_ref, d_ref, nw_ref, prev_ref, dy_ref,
             dxc_ref, dz_ref, ddtr_ref, dbias_ref, dalog_ref, dd_ref, dnw_ref, dS):
        b, t = pl.program_id(0), pl.program_id(1)

        @pl.when(t == 0)
        def _():
            dS[...] = jnp.zeros_like(dS)

        s = _ssd_prep(nc - 1 - t, dtr_ref, bias_ref, alog_ref, d_ref)
        states = [prev_ref[0, 0, p] for p in range(N_PAIR)]
        y, _, k = _ssd_chunk(xc_ref, s, states)
        xv, Bs, Cs, Xb = k["xv"], k["Bs"], k["Cs"], k["Xb"]

        zz = z_ref[...].astype(F32)
        sz = _sigmoid(zz)
        silu_z = zz * sz
        yg = y * silu_z
        r = _per_group(lambda a: lax.rsqrt(jnp.mean(a * a, axis=-1, keepdims=True) + EPS), yg)
        xhat = yg * r
        dout = dy_ref[...].astype(F32)
        gw = dout * nw_ref[...]
        dyg = r * (gw - xhat * _per_group(lambda a, c2: jnp.mean(a * c2, axis=-1, keepdims=True), gw, xhat))
        dnw = jnp.sum(dout * xhat, axis=0, keepdims=True)
        dz_ref[...] = (dyg * y * _dsilu(zz, sz)).astype(dz_ref.dtype)
        dy = dyg * silu_z
        dy0 = jnp.where(s["first"], dy, 0.0)
        dyb = (dy0.astype(BF16), (dy - dy0).astype(BF16))
        dYo = (dy * s["ecs_w"]).astype(BF16)

        dS_f = [dS[p] for p in range(N_PAIR)]
        dS_b = [d.astype(BF16) for d in dS_f]
        BdS, dXm, dprev, dCs, dMs, XdS = [], [], [], [[] for _ in range(SSD_GROUPS)], [], []
        for p in range(N_PAIR):
            g, ln = p // 2, slice(128 * p, 128 * (p + 1))
            BdS.append(_dot(Bs[g], dS_b[p], NT))
            dXm.append(_dot(k["Mb"][2 * p], dyb[0][:, ln], TN) + _dot(k["Mb"][2 * p + 1], dyb[1][:, ln], TN))
            dprev.append(_dot(dYo[:, ln], Cs[g], TN))
            dCs[g].append(_dot(dYo[:, ln], k["prev_b"][p], NN))
            for hh in range(2):
                dMs.append(_dot(dyb[hh][:, ln], Xb[hh][:, ln], NT))
                XdS.append(_dot(Xb[hh][:, ln], dS_b[p], NN))
        dX = jnp.concatenate(dXm, axis=1) + s["decay_w"] * jnp.concatenate(BdS, axis=1)
        dxs = dy * s["d_w"] + dX * s["dt_w"]

        sums = _dot01(jnp.concatenate([dX * xv, dy * k["yo"] * s["ecs_w"], dy * xv], axis=0), s["spread"], NT, "b")
        ddt, dcs = sums[0:Q], sums[Q:2 * Q]
        dD = jnp.sum(sums[2 * Q:3 * Q], axis=0, keepdims=True)

        col, row = s["col"], s["row"]
        lane1 = col[0:1]
        rowsT = lax.broadcasted_iota(jnp.int32, (128, Q), 0)
        dcs_t = jnp.zeros((128, Q), F32)
        dcl = jnp.zeros((1, 128), F32)
        dB_out, dC_out = [], []
        for g in range(SSD_GROUPS):
            Bf = Bs[g].astype(F32)
            dCB = jnp.zeros((Q, Q), F32)
            dBacc = jnp.zeros((Q, 128), F32)
            for r4 in range(HPG):
                h = HPG * g + r4
                p, hh = h // 2, h % 2
                W = dMs[h] * k["Ms"][h]
                dCB = dCB + dMs[h] * k["Lms"][h]
                decay_h = s["decay_w"][:, SSD_HEAD_DIM * h:SSD_HEAD_DIM * h + 1]
                dBacc = dBacc + decay_h * XdS[h]
                tdec = jnp.sum(XdS[h] * Bf, axis=1, keepdims=True) * decay_h
                dcs = dcs + jnp.where(col == h, jnp.sum(W, axis=1, keepdims=True) - tdec, 0.0)
                dcs_t = dcs_t - jnp.where(rowsT == h, jnp.sum(W, axis=0, keepdims=True), 0.0)
                rows_h = (row < SSD_HEAD_DIM) if hh == 0 else (row >= SSD_HEAD_DIM)
                sprev = jnp.sum(jnp.sum(jnp.where(rows_h, dS_f[p] * states[p], 0.0), axis=1, keepdims=True),
                                axis=0, keepdims=True)
                ecl = jnp.exp(s["cs"][Q - 1:Q, h:h + 1])
                dcl = dcl + jnp.where(lane1 == h, jnp.sum(tdec, axis=0, keepdims=True) + ecl * sprev, 0.0)
            dCB_b = dCB.astype(BF16)
            dC_out.append(dCs[g][0] + dCs[g][1] + _dot(dCB_b, Bs[g], NN))
            dB_out.append(dBacc + _dot(dCB_b, Cs[g], TN))
        for p in range(N_PAIR):
            dS[p] = dS_f[p] * k["ecl_rows"][p] + dprev[p]
        dxc_ref[...] = jnp.concatenate([dxs] + dB_out + dC_out, axis=1).astype(dxc_ref.dtype)

        dcs = dcs + _dot01(s["eye"], dcs_t, NT, "a") + jnp.where(row == Q - 1, dcl, 0.0)
        da = _dot01(row <= col, dcs, NN, "a")
        ddt = ddt + da * s["A"]
        dpre = jnp.where(s["valid"], ddt * _sigmoid(s["pre"]), 0.0)
        ddtr_ref[...] = dpre
        dbias = jnp.sum(dpre, axis=0, keepdims=True)
        dalog = jnp.sum(da * s["dt"], axis=0, keepdims=True) * s["A"]
        first_step = jnp.logical_and(b == 0, t == 0)

        @pl.when(first_step)
        def _():
            dbias_ref[...] = dbias
            dalog_ref[...] = dalog
            dd_ref[...] = dD
            dnw_ref[...] = dnw

        @pl.when(jnp.logical_not(first_step))
        def _():
            dbias_ref[...] += dbias
            dalog_ref[...] += dalog
            dd_ref[...] += dD
            dnw_ref[...] += dnw

    rb = lambda b, c: b * nc + nc - 1 - c
    rowblk = lambda w: pl.BlockSpec((Q, w), lambda b, c: (rb(b, c), 0))
    vec = lambda w: pl.BlockSpec((1, w), lambda b, c: (0, 0))
    return _call(
        body, name="ssd_bwd", grid=(Bl, nc),
        in_specs=_ssd_in_specs(nc, rev=True) + [
            pl.BlockSpec((1, 1, N_PAIR, 128, 128), lambda b, c: (b, nc - 1 - c, 0, 0, 0)), rowblk(SSD_INNER)],
        out_specs=[rowblk(SSD_CONV_CH), rowblk(SSD_INNER), rowblk(128), vec(128), vec(128), vec(128), vec(SSD_INNER)],
        out_shape=[jax.ShapeDtypeStruct((M, SSD_CONV_CH), BF16), jax.ShapeDtypeStruct(dproj.shape, BF16),
                   jax.ShapeDtypeStruct((M, 128), F32), jax.ShapeDtypeStruct((1, 128), F32),
                   jax.ShapeDtypeStruct((1, 128), F32), jax.ShapeDtypeStruct((1, 128), F32),
                   jax.ShapeDtypeStruct((1, SSD_INNER), F32)],
        scratch=[pltpu.VMEM((N_PAIR, 128, 128), F32)], sem=("arbitrary", "arbitrary"),
        args=(xc, dtr, proj, bias_p, alog_p, d_p, nw, prev, dya), comm=comm, into=(dproj, 1))


NSUB = Q // HG_CHUNK
HG_HP = 8
EXP_CAP = 80.0


def _hg_setup(blk, q_ref, f_ref, hb_ref):
    row = lax.broadcasted_iota(jnp.int32, (Q, Q), 0)
    col = lax.broadcasted_iota(jnp.int32, (Q, Q), 1)
    same = (row // HG_CHUNK) == (col // HG_CHUNK)
    causal = jnp.logical_and(same, col <= row)
    lb = _sigmoid(hb_ref[0:1, :] - hb_ref[1:2, :])
    fl = f_ref[...].astype(F32)
    sg = _sigmoid(fl)
    fg = lb + (1.0 - lb) * sg
    k = (1.0 - lb) * (1.0 - sg)
    gl = jnp.log(fg)
    G = _dot01(causal, gl, NN, "a")
    T = _dot01(same, gl, NN, "a")
    qv = q_ref[...].astype(F32)
    sq = _sigmoid(qv)
    eG = jnp.exp(G)
    eGn = jnp.exp(jnp.minimum(-G, EXP_CAP))
    eTG = jnp.exp(T - G)
    qt = qv * sq * eG
    kt = k * eGn
    kh = k * eTG
    valid = jnp.logical_or(blk > 0, row[:, :1] >= PAD)
    return dict(row=row, col=col, same=same, causal=causal, lb=lb, sg=sg, fg=fg, k=k, T=T, qv=qv, sq=sq,
                eG=eG, eGn=eGn, eTG=eTG, qt=qt, kt=kt, kh=kh, valid=valid)


def _hg_specs(nb, rev=False):
    rb = (lambda h, b, t: b * nb + nb - 1 - t) if rev else (lambda h, b, t: b * nb + t)
    w = 128 * HG_HP
    blk = lambda off: pl.BlockSpec((Q, w), lambda h, b, t, off=off: (rb(h, b, t), off // HG_HP + h))
    return [blk(24), blk(32), blk(40), blk(48),
            pl.BlockSpec((2, w), lambda h, b, t: (0, h)), pl.BlockSpec((1, w), lambda h, b, t: (0, h))]


HEAD_LANES = tuple(slice(128 * hh, 128 * (hh + 1)) for hh in range(HG_HP))


def _per_head(fn, *arrs):
    return jnp.concatenate([jnp.broadcast_to(fn(*(a[:, ln] for a in arrs)), (arrs[0].shape[0], 128))
                            for ln in HEAD_LANES], axis=1)


def _hgrn_fwd(proj, hb, nw, Bl, nb, comm=None):
    M = proj.shape[0]

    def body(q_ref, f_ref, i_ref, g_ref, hb_ref, nw_ref, y_ref, o_ref, st_ref, S):
        blk = pl.program_id(2)

        @pl.when(blk == 0)
        def _():
            S[...] = jnp.zeros_like(S)

        s = _hg_setup(blk, q_ref, f_ref, hb_ref)
        v = i_ref[...]
        qt_b, kt_b, kh_b = s["qt"].astype(BF16), s["kt"].astype(BF16), s["kh"].astype(BF16)
        eT = jnp.exp(s["T"])
        att = [jnp.where(s["causal"], _dot(qt_b[:, ln], kt_b[:, ln], NT), 0.0).astype(BF16) for ln in HEAD_LANES]
        o_intra = [_dot(att[hh], v[:, ln], NN) for hh, ln in enumerate(HEAD_LANES)]
        for j in range(NSUB):
            sl = slice(HG_CHUNK * j, HG_CHUNK * (j + 1))
            for hh, ln in enumerate(HEAD_LANES):
                St = S[hh]
                st_ref[0, hh, 0, j] = St
                o_ref[sl, ln] = o_intra[hh][sl] + _dot(qt_b[sl, ln], St.astype(BF16), NT)
                S[hh] = St * eT[HG_CHUNK * j:HG_CHUNK * j + 1, ln] + _dot(v[sl, ln], kh_b[sl, ln], TN)
        o = o_ref[...]
        r = _per_head(lambda a: lax.rsqrt(jnp.mean(a * a, axis=-1, keepdims=True) + EPS), o)
        gv = g_ref[...].astype(F32)
        y_ref[...] = (o * r * nw_ref[...] * gv * _sigmoid(gv)).astype(y_ref.dtype)

    rowblk = pl.BlockSpec((Q, 128 * HG_HP), lambda h, b, t: (b * nb + t, h))
    return _call(
        body, name="hgrn_fwd", grid=(HG_HEADS // HG_HP, Bl, nb), in_specs=_hg_specs(nb),
        out_specs=[rowblk, rowblk,
                   pl.BlockSpec((1, HG_HP, 1, NSUB, 128, 128), lambda h, b, t: (b, h, t, 0, 0, 0))],
        out_shape=[jax.ShapeDtypeStruct((M, HG_WIDTH), BF16), jax.ShapeDtypeStruct((M, HG_WIDTH), F32),
                   jax.ShapeDtypeStruct((Bl, HG_HEADS, nb, NSUB, 128, 128), F32)],
        scratch=[pltpu.VMEM((HG_HP, 128, 128), F32)], sem=("parallel", "arbitrary", "arbitrary"),
        args=(proj, proj, proj, proj, hb, nw), comm=comm)


def _hgrn_bwd(proj, hb, nw, o_saved, st_saved, dyb, dproj, Bl, nb, comm=None):
    assert HG_HP == HG_HEADS

    def body(q_ref, f_ref, i_ref, g_ref, hb_ref, nw_ref, o_ref, st_ref, dy_ref,
             d_ref, dhb_ref, dnw_ref, dS, a_dqt, a_dv, a_dkh, a_dgl):
        b, t = pl.program_id(1), pl.program_id(2)

        @pl.when(t == 0)
        def _():
            dS[...] = jnp.zeros_like(dS)

        first_step = jnp.logical_and(b == 0, t == 0)
        s = _hg_setup(nb - 1 - t, q_ref, f_ref, hb_ref)
        v = i_ref[...]
        qt_b, kt_b, kh_b = s["qt"].astype(BF16), s["kt"].astype(BF16), s["kh"].astype(BF16)
        eT = jnp.exp(s["T"])
        att = [jnp.where(s["causal"], _dot(qt_b[:, ln], kt_b[:, ln], NT), 0.0).astype(BF16) for ln in HEAD_LANES]

        o = o_ref[...]
        r = _per_head(lambda a: lax.rsqrt(jnp.mean(a * a, axis=-1, keepdims=True) + EPS), o)
        xhat = o * r
        gv = g_ref[...].astype(F32)
        sgv = _sigmoid(gv)
        dyv = dy_ref[...].astype(F32)
        d_on = dyv * gv * sgv
        dg_out = dyv * xhat * nw_ref[...] * _dsilu(gv, sgv)
        gw = d_on * nw_ref[...]
        do = r * (gw - xhat * _per_head(lambda a, c: jnp.mean(a * c, axis=-1, keepdims=True), gw, xhat))
        dnw = jnp.sum(d_on * xhat, axis=0, keepdims=True)
        do_b = do.astype(BF16)

        datt = [jnp.where(s["causal"], _dot(do_b[:, ln], v[:, ln], NT), 0.0).astype(BF16) for ln in HEAD_LANES]
        dqt = jnp.concatenate([_dot(datt[hh], kt_b[:, ln], NN) for hh, ln in enumerate(HEAD_LANES)], axis=1)
        dkt = jnp.concatenate([_dot(datt[hh], qt_b[:, ln], TN) for hh, ln in enumerate(HEAD_LANES)], axis=1)
        dv = jnp.concatenate([_dot(att[hh], do_b[:, ln], TN) for hh, ln in enumerate(HEAD_LANES)], axis=1)
        last_row = (lax.broadcasted_iota(jnp.int32, (HG_CHUNK, 128), 0) == HG_CHUNK - 1)
        for j in reversed(range(NSUB)):
            sl = slice(HG_CHUNK * j, HG_CHUNK * (j + 1))
            for hh, ln in enumerate(HEAD_LANES):
                St = st_ref[0, hh, 0, j]
                dSt = dS[hh]
                St_b, dSt_b = St.astype(BF16), dSt.astype(BF16)
                eT_j = eT[HG_CHUNK * j:HG_CHUNK * j + 1, ln]
                dkh_j = _dot(v[sl, ln], dSt_b, NN)
                a_dqt[sl, ln] = _dot(do_b[sl, ln], St_b, NN)
                a_dv[sl, ln] = _dot(kh_b[sl, ln], dSt_b, NT)
                a_dkh[sl, ln] = dkh_j
                dlast = (jnp.sum(St * dSt, axis=0, keepdims=True) * eT_j
                         + jnp.sum(dkh_j * s["kh"][sl, ln], axis=0, keepdims=True))
                a_dgl[sl, ln] = jnp.where(last_row, dlast, 0.0)
                dS[hh] = dSt * eT_j + _dot(do_b[sl, ln], qt_b[sl, ln], TN)
        dqt = dqt + a_dqt[...]
        dv = dv + a_dv[...]
        dkh = a_dkh[...]
        dG = dqt * s["qt"] - dkt * s["kt"] - dkh * s["kh"] + a_dgl[...]
        rev_causal = jnp.logical_and(s["same"], s["col"] >= s["row"])
        dgl = _dot01(rev_causal, dG, NN, "a")
        dk = dkt * s["eGn"] + dkh * s["eTG"]
        dfg = dgl / s["fg"] - dk
        lb, sg = s["lb"], s["sg"]
        keep = s["valid"].astype(F32)
        d_ref[:, 0:w] = (dqt * s["eG"] * _dsilu(s["qv"], s["sq"]) * keep).astype(d_ref.dtype)
        d_ref[:, w:2 * w] = (dfg * (1.0 - lb) * sg * (1.0 - sg) * keep).astype(d_ref.dtype)
        d_ref[:, 2 * w:3 * w] = (dv * keep).astype(d_ref.dtype)
        d_ref[:, 3 * w:4 * w] = (dg_out * keep).astype(d_ref.dtype)
        dlb = jnp.sum(dfg * (1.0 - sg) * keep, axis=0, keepdims=True) * lb * (1.0 - lb)
        dhb = jnp.concatenate([dlb, -dlb], axis=0)

        @pl.when(first_step)
        def _():
            dhb_ref[...] = dhb
            dnw_ref[...] = dnw

        @pl.when(jnp.logical_not(first_step))
        def _():
            dhb_ref[...] += dhb
            dnw_ref[...] += dnw

    w = 128 * HG_HP
    rowblk = pl.BlockSpec((Q, w), lambda h, b, t: (b * nb + nb - 1 - t, h))
    return _call(
        body, name="hgrn_bwd", grid=(HG_HEADS // HG_HP, Bl, nb),
        in_specs=_hg_specs(nb, rev=True) + [
            rowblk, pl.BlockSpec((1, HG_HP, 1, NSUB, 128, 128), lambda h, b, t: (b, h, nb - 1 - t, 0, 0, 0)), rowblk],
        out_specs=[pl.BlockSpec((pl.Element(Q), pl.Element(4 * w)),
                                lambda h, b, t: (pl.multiple_of((b * nb + nb - 1 - t) * Q, Q), 3 * HG_WIDTH)),
                   pl.BlockSpec((2, w), lambda h, b, t: (0, h)), pl.BlockSpec((1, w), lambda h, b, t: (0, h))],
        out_shape=[jax.ShapeDtypeStruct(dproj.shape, BF16),
                   jax.ShapeDtypeStruct((2, HG_WIDTH), F32), jax.ShapeDtypeStruct((1, HG_WIDTH), F32)],
        scratch=[pltpu.VMEM((HG_HP, 128, 128), F32)] + [pltpu.VMEM((Q, w), F32)] * 4,
        sem=("parallel", "arbitrary", "arbitrary"),
        args=(proj, proj, proj, proj, hb, nw, o_saved, st_saved, dyb), comm=comm, into=(dproj, 0))


def _adamw(name, parts, w, m, v, comm=None):
    R, C = w.shape
    S = parts.shape[0]
    tr, tc = (_tile(R, (256, 176, 128, 64, 8)), C) if R % 8 == 0 else (R, 256)
    c1, c2 = 1.0 - ADAM_B1 ** ADAM_STEP, 1.0 - ADAM_B2 ** ADAM_STEP

    def body(p_ref, w_ref, m_ref, v_ref, g_ref, d_ref, nm_ref, nv_ref):
        g = p_ref[0].astype(F32)
        for s in range(1, S):
            g = g + p_ref[s].astype(F32)
        nm = ADAM_B1 * m_ref[...] + (1.0 - ADAM_B1) * g
        nv = ADAM_B2 * v_ref[...] + (1.0 - ADAM_B2) * (g * g)
        g_ref[...] = g
        nm_ref[...] = nm
        nv_ref[...] = nv
        d_ref[...] = -ADAM_LR * ((nm / c1) / (jnp.sqrt(nv / c2) + ADAM_EPS) + ADAM_WD * w_ref[...])

    blk = pl.BlockSpec((tr, tc), lambda i, j: (i, j))
    return _call(
        body, name=name, grid=(R // tr, C // tc),
        in_specs=[pl.BlockSpec((S, tr, tc), lambda i, j: (0, i, j)), blk, blk, blk], out_specs=[blk] * 4,
        out_shape=[jax.ShapeDtypeStruct((R, C), F32)] * 4, scratch=[], sem=("parallel", "parallel"),
        args=(parts, w, m, v), comm=comm)


def _pair_sum(name, by_core, arrived):
    _, J, R, C = by_core.shape
    tc = _tile(C, (512, 256, 128))

    def body(c_ref, a_ref, b_ref, o_ref):
        o_ref[...] = (a_ref[0].astype(F32) + b_ref[...].astype(F32)).astype(o_ref.dtype)

    blk = pl.BlockSpec((1, R, tc), lambda j, k, c_ref: (j, 0, k))
    return pl.pallas_call(
        body, name=name,
        grid_spec=pltpu.PrefetchScalarGridSpec(
            num_scalar_prefetch=1, grid=(J, C // tc),
            in_specs=[pl.BlockSpec((1, 1, R, tc), lambda j, k, c_ref: (c_ref[0], j, 0, k)), blk], out_specs=blk),
        out_shape=jax.ShapeDtypeStruct(arrived.shape, arrived.dtype), compiler_params=_params(("parallel", "parallel")),
    )(lax.axis_index("c").astype(jnp.int32).reshape(1), by_core, arrived)


def _sum_parts(name, parts):
    S, R, C = parts.shape

    def body(p_ref, o_ref):
        g = p_ref[0]
        for s in range(1, S):
            g = g + p_ref[s]
        o_ref[...] = g

    return pl.pallas_call(
        body, name=name, out_shape=jax.ShapeDtypeStruct((R, C), F32),
        in_specs=[pl.BlockSpec(memory_space=pltpu.VMEM)], out_specs=pl.BlockSpec(memory_space=pltpu.VMEM),
    )(parts)


def _heads_to_lanes(p):
    return jnp.pad(p, [(0, 0)] * (p.ndim - 1) + [(0, 128 - SSD_HEADS)])


def _lanes_to_heads(p):
    return p[..., :SSD_HEADS]


def _pack_rows(arrs):
    flat = jnp.concatenate([a.reshape(-1).astype(F32) for a in arrs])
    return jnp.pad(flat, (0, (-flat.shape[0]) % (8 * D_MODEL))).reshape(-1, D_MODEL)


def _unpack_rows(packed, like):
    flat, outs, at = packed.reshape(-1), [], 0
    for a in like:
        outs.append(flat[at:at + a.size].reshape(a.shape))
        at += a.size
    return outs


def _cols(gth):
    return jnp.transpose(gth, (1, 0, 2)).reshape(gth.shape[1], -1)


def _rows(gth):
    return gth.reshape(-1, gth.shape[2])


def _to_rows(g):
    return g.reshape(N_DEV, -1, g.shape[1]).astype(BF16)


def _by_core(g):
    return jnp.transpose(g.reshape(N_DEV // 2, 2, -1, g.shape[1]), (1, 0, 2, 3)).astype(BF16)


DT_ROW = 3072


def _chip_sums(tag, by_core, swap_in=None):
    arrived = swap_in(by_core) if swap_in else _exchange(tag + "_swap", "swap", by_core)
    return [_pair_sum(f"{tag}_chipsum{i}", m, a) for i, (m, a) in enumerate(zip(by_core, arrived))]


def _ffn_fwd_gu(tag, n, w_gu_t, comm=None):
    M = n.shape[0]
    F = w_gu_t.shape[0] // 2
    tm = _tile(M, (544, 256))
    outs = _fused_matmul(
        tag + "_gu", M, F, D_MODEL,
        [dict(a=n, b=w_gu_t, trans_b=True, acc=0, resident=True),
         dict(a=n, b=w_gu_t, trans_b=True, bn_off=1, acc=1, resident=True)], [],
        lambda accs, ex: (accs[0], accs[1], accs[0] * _sigmoid(accs[0]) * accs[1]),
        [BF16, BF16, BF16], 2, tm, F, D_MODEL, outer="i", comm=comm, sub=256)
    return (n, *outs[:3]), outs[3:]


def _rmsnorm_tile(x, w):
    return x * lax.rsqrt(jnp.mean(x * x, axis=-1, keepdims=True) + EPS) * w


def _ffn_fwd_down(tag, h, a, w_down, next_norm=None, comm=None):
    M = h.shape[0]
    F = w_down.shape[0]
    tm = _tile(M, (1088, 544, 256))
    if next_norm is None:
        (h_out,) = _fused_matmul(
            tag + "_down", M, D_MODEL, F, [dict(a=a, b=w_down, acc=0)], [(h, 0)],
            lambda accs, ex: (ex[0] + 0.5 * accs[0],), [F32], 1, tm, D_MODEL, F, outer="j", sub=256)
        return h_out

    def with_norm(accs, ex):
        h_new = ex[0] + 0.5 * accs[0]
        return h_new, _rmsnorm_tile(h_new, ex[1])

    return _fused_matmul(tag + "_down", M, D_MODEL, F, [dict(a=a, b=w_down, acc=0, resident=True)], [(h, 0)], with_norm,
                         [F32, BF16], 1, tm, D_MODEL, F, outer="j", vecs=[next_norm], comm=comm)


def _ffn_bwd(tag, dh, dh_b, h, norm_w, w_gu_t, w_down, saved, scatter=False):
    n, g, u, a = saved
    M = h.shape[0]
    F = w_down.shape[0]
    tm = _tile(M, (544, 256))
    tn = _tile(F, (1408, 704, 256))

    def swiglu_bwd(accs, ex):
        da, gv, uv = 0.5 * accs[0], ex[0].astype(F32), ex[1].astype(F32)
        s = _sigmoid(gv)
        return da * uv * _dsilu(gv, s), da * gv * s

    (dgu,) = _fused_matmul(
        tag + "_dact", M, F, D_MODEL, [dict(a=dh_b, b=w_down, trans_b=True, acc=0, resident=True)], [(g, 0), (u, 0)],
        swiglu_bwd, [BF16, BF16], 1, tm, F, D_MODEL, outer="i", stack=True, sub=256)
    tr = _tile(M, (2176, 256))
    (dw_down,) = _matmul_tn(tag + "_dwd", a, dh_b, tn, D_MODEL, tr, scale=0.5)
    dw_gu_t, *p_down = _matmul_tn(tag + "_dwgu", dgu, n, tn, D_MODEL, tr,
                                  comm=("scatter", [_to_rows(dw_down)]) if scatter else None)
    comm = None
    if scatter:
        comm = ("chips", _chip_sums(tag + "_wgu", [_by_core(dw_gu_t)]))
    def norm_bwd(accs, ex):
        dh_prev, dw = _rmsnorm_bwd_tile(accs[0], ex[0], ex[2], ex[1])
        return dh_prev, dh_prev, dw

    dh_prev, dh_prev_b, dnorm, *p_gu = _fused_matmul(
        tag + "_dn", M, D_MODEL, 2 * F,
        [dict(a=dgu, b=w_gu_t, acc=0, resident=True)], [(h, 0), (dh, 0)],
        norm_bwd, [F32, BF16], 1, tm, D_MODEL, 2 * F, outer="i", comm=comm, vecs=[norm_w], row_sums=1)
    return (dh_prev, dh_prev_b, dnorm, *((p_gu[0], p_down[0]) if scatter else (dw_gu_t, dw_down)))


def kernel(x, meta_tokens, ffn1_norm, ffn1_w_gu, ffn1_w_down, mix_norm, w_in, ssd_conv_w, ssd_conv_b, ssd_dt_bias, ssd_a_log, ssd_d, ssd_norm, hg_lower_bound, hg_norm, w_branch_a, w_branch_b, w_out, ffn2_norm, ffn2_w_gu, ffn2_w_down, final_norm, loss_target, m_meta_tokens, m_ffn1_norm, m_ffn1_w_gu, m_ffn1_w_down, m_mix_norm, m_w_in, m_ssd_conv_w, m_ssd_conv_b, m_ssd_dt_bias, m_ssd_a_log, m_ssd_d, m_ssd_norm, m_hg_lower_bound, m_hg_norm, m_w_branch_a, m_w_branch_b, m_w_out, m_ffn2_norm, m_ffn2_w_gu, m_ffn2_w_down, m_final_norm, v_meta_tokens, v_ffn1_norm, v_ffn1_w_gu, v_ffn1_w_down, v_mix_norm, v_w_in, v_ssd_conv_w, v_ssd_conv_b, v_ssd_dt_bias, v_ssd_a_log, v_ssd_d, v_ssd_norm, v_hg_lower_bound, v_hg_norm, v_w_branch_a, v_w_branch_b, v_w_out, v_ffn2_norm, v_ffn2_w_gu, v_ffn2_w_down, v_final_norm):
    Bl, S, D = x.shape
    T = PAD + N_META + S
    nc = T // Q
    M = Bl * T
    me = 4 * lax.axis_index("x") + 2 * lax.axis_index("y") + lax.axis_index("c")

    bf = lambda a: a[0].astype(BF16)
    bft = lambda a: a[0].T.astype(BF16)
    g_meta, g_conv_w = _exchange("gather_small", "gather", [meta_tokens, ssd_conv_w[0]])
    meta_full, conv_w_full = _cols(g_meta), _cols(g_conv_w)
    bias_p, alog_p, d_p = _heads_to_lanes(ssd_dt_bias), _heads_to_lanes(ssd_a_log), _heads_to_lanes(ssd_d)
    final_w = final_norm.reshape(1, D)

    h0, n1, g_wgu1 = _embed_norm(x, meta_full, ffn1_norm, comm=("gather", [bft(ffn1_w_gu)]))
    wgu1 = _rows(g_wgu1)
    tm = _tile(M, (1088, 544, 256))
    win_shard = bft(w_in)
    cut = (win_shard.shape[0] // 32) * 16
    ffn1_saved, (g_wd1, g_win_a) = _ffn_fwd_gu("ffn1", n1, wgu1, comm=("gather", [bf(ffn1_w_down), win_shard[:cut]]))
    wd1 = _rows(g_wd1)
    h1, un, g_win_b = _ffn_fwd_down("ffn1", h0, ffn1_saved[3], wd1, next_norm=mix_norm,
                                    comm=("gather", [win_shard[cut:]]))
    win_t = _rows(jnp.concatenate([g_win_a, g_win_b], axis=1))
    win_dt = jnp.pad(win_t[DT_ROW:DT_ROW + SSD_HEADS], ((0, 128 - SSD_HEADS), (0, 0)))
    plain = lambda accs, ex: (accs[0],)
    proj, g_wa, g_wb, g_wo = _fused_matmul(
        "in_proj", M, N_MAIN, D, [dict(a=un, b=win_t, trans_b=True, acc=0, b_shift=(DT_ROW // 1536, SSD_HEADS))], [],
        plain, [BF16], 1, tm, 1536, D,
        outer="j", comm=("gather", [bf(w_branch_a), bf(w_branch_b), bf(w_out)], "early"), sub=512)
    wa, wb, wo = _rows(g_wa), _rows(g_wb), _rows(g_wo)
    (dtr,) = _fused_matmul("in_proj_dt", M, 128, D, [dict(a=un, b=win_dt, trans_b=True, acc=0)], [], plain, [F32], 1,
                           tm, 128, D, outer="j")
    xc = _conv_fwd(proj, conv_w_full, ssd_conv_b, Bl, T)
    ya, ssd_prev = _ssd_fwd(xc, dtr, proj, bias_p, alog_p, d_p, ssd_norm, Bl, nc)
    yb, hg_o, hg_st, g_wgu2, g_wd2 = _hgrn_fwd(proj, hg_lower_bound, hg_norm, Bl, nc,
                                               comm=("gather", [bft(ffn2_w_gu), bf(ffn2_w_down)]))
    wgu2, wd2 = _rows(g_wgu2), _rows(g_wd2)

    def branch_fwd(accs, ex):
        pa, pb = accs
        return pa, pb, _sigmoid(ex[0].astype(F32)) * pa + _sigmoid(ex[1].astype(F32)) * pb

    pa, pb, merged = _fused_matmul(
        "branches", M, D, D, [dict(a=ya, b=wa, acc=0), dict(a=yb, b=wb, acc=1)], [(proj, 7), (proj, 8)],
        branch_fwd, [BF16, BF16, BF16], 2, tm, D, D, outer="j")
    def out_with_norm(accs, ex):
        h_new = ex[0] + accs[0]
        return h_new, _rmsnorm_tile(h_new, ex[1])

    h2, n2 = _fused_matmul("out_proj", M, D, D, [dict(a=merged, b=wo, acc=0)], [(h1, 0)], out_with_norm,
                           [F32, BF16], 1, tm, D, D, outer="j", vecs=[ffn2_norm])
    ffn2_saved, _ = _ffn_fwd_gu("ffn2", n2, wgu2)
    h3 = _ffn_fwd_down("ffn2", h2, ffn2_saved[3], wd2)

    dh3, dh3_b, d_final, loss_part = _loss_head(h3, final_w, loss_target, Bl, nc)
    dh2, dh2_b, d_ffn2_norm, d_wgu2, d_wd2 = _ffn_bwd("ffn2", dh3, dh3_b, h2, ffn2_norm, wgu2, wd2, ffn2_saved)

    def branch_bwd(accs, ex):
        dm = accs[0]
        ga, gb, pav, pbv = (e.astype(F32) for e in ex)
        sa, sb = _sigmoid(ga), _sigmoid(gb)
        return (dm * sa, dm * sb,
                jnp.concatenate([dm * pav * sa * (1.0 - sa), dm * pbv * sb * (1.0 - sb)], axis=1))

    d_merged_outs = []

    def d_merged_with_swap(theirs):
        d_merged_outs.extend(_fused_matmul(
            "d_merged", M, D, D, [dict(a=dh2_b, b=wo, trans_b=True, acc=0)], [(proj, 7), (proj, 8), (pa, 0), (pb, 0)],
            branch_bwd, [BF16] * 2, 1, tm, D, D, outer="j", comm=("swap", theirs),
            wide=dict(width=2 * D, col=7 * D, total=N_MAIN, dtype=BF16)))
        return d_merged_outs[3:]

    s_ffn2 = _chip_sums("ffn2", [_by_core(d_wgu2), _by_core(d_wd2)], swap_in=d_merged_with_swap)
    dpa, dpb, dproj = d_merged_outs[:3]
    (d_wo,) = _matmul_tn("d_w_out", merged, dh2_b, 512, D, M)
    (d_wa,) = _matmul_tn("d_w_a", ya, dpa, 512, D, M)
    (d_wb,) = _matmul_tn("d_w_b", yb, dpb, 512, D, M)
    dya, dyb = _fused_matmul(
        "d_branches", M, D, D, [dict(a=dpa, b=wa, trans_b=True, acc=0), dict(a=dpb, b=wb, trans_b=True, acc=1)], [],
        lambda accs, ex: (accs[0], accs[1]), [BF16, BF16], 2, tm, D, D, outer="j")
    *ssd_grads, p_wgu2, p_wd2 = _ssd_bwd(xc, dtr, proj, bias_p, alog_p, d_p, ssd_norm, ssd_prev, dya, dproj, Bl, nc,
                                         comm=("chips", s_ffn2))
    dxc, dproj, ddtr, d_bias_p, d_alog_p, d_d_p, d_ssd_norm = ssd_grads
    dproj, d_conv_w, d_conv_b = _conv_bwd(proj, conv_w_full, ssd_conv_b, dxc, dproj, Bl, T)
    dproj, d_hb, d_hg_norm, p_wa, p_wb, p_wo = _hgrn_bwd(
        proj, hg_lower_bound, hg_norm, hg_o, hg_st, dyb, dproj, Bl, nc,
        comm=("scatter", [_to_rows(d_wa), _to_rows(d_wb), _to_rows(d_wo)]))
    ddtr_b = ddtr.astype(BF16)
    (d_win_t,) = _matmul_tn("d_w_in", dproj, un, 768, D, M, out_skip=(DT_ROW, SSD_HEADS))
    (d_win_dt,) = _matmul_tn("d_w_in_dt", ddtr_b, un, 128, D, M)
    d_win_t = lax.dynamic_update_slice(d_win_t, d_win_dt[:SSD_HEADS], (DT_ROW, 0))
    d_un_dt_outs = []

    def d_un_dt_with_swap(theirs):
        d_un_dt_outs.extend(_fused_matmul("d_un_dt", M, D, 128, [dict(a=ddtr_b, b=win_dt, acc=0)], [], plain, [F32], 1,
                                          tm, D, 128, outer="j", comm=("swap", theirs)))
        return d_un_dt_outs[1:]

    s_win = _chip_sums("w_in", [_by_core(d_win_t)], swap_in=d_un_dt_with_swap)
    def mix_norm_bwd(accs, ex):
        dh, dw = _rmsnorm_bwd_tile(accs[0] + ex[0], ex[1], ex[3], ex[2])
        return dh, dh, dw

    dh1, dh1_b, d_mix_norm, p_win = _fused_matmul(
        "d_un", M, D, N_MAIN, [dict(a=dproj, b=win_t, acc=0, b_shift=(DT_ROW // 3072, SSD_HEADS))],
        [(d_un_dt_outs[0], 0), (h1, 0), (dh2, 0)],
        mix_norm_bwd, [F32, BF16], 1, _tile(M, (544, 256)), D, 3072, outer="i", comm=("chips", s_win),
        vecs=[mix_norm], row_sums=1)
    dh0, _, d_ffn1_norm, p_wgu1, p_wd1 = _ffn_bwd("ffn1", dh1, dh1_b, h0, ffn1_norm, wgu1, wd1, ffn1_saved, scatter=True)

    dh0 = dh0.reshape(Bl, T, D)
    grad_x = dh0[:, PAD + N_META:]
    d_meta = dh0[:, PAD:PAD + N_META]

    small_grads = [d_ffn1_norm, d_mix_norm, d_conv_b, _lanes_to_heads(d_bias_p), _lanes_to_heads(d_alog_p),
                   _lanes_to_heads(d_d_p), d_ssd_norm, d_hb, d_hg_norm, d_ffn2_norm, d_final.reshape(D), d_conv_w]
    small_like = small_grads + [d_meta[b] for b in range(Bl)] + [loss_part[0, 0:1]]
    small_packed = _pack_rows(small_like)
    parts = [p_wgu1, p_wd1, p_win, p_wa, p_wb, p_wo, p_wgu2, p_wd2]

    names = ["meta_tokens", "ffn1_norm", "ffn1_w_gu", "ffn1_w_down", "mix_norm", "w_in", "ssd_conv_w", "ssd_conv_b",
             "ssd_dt_bias", "ssd_a_log", "ssd_d", "ssd_norm", "hg_lower_bound", "hg_norm", "w_branch_a", "w_branch_b",
             "w_out", "ffn2_norm", "ffn2_w_gu", "ffn2_w_down", "final_norm"]
    W = dict(meta_tokens=meta_tokens, ffn1_norm=ffn1_norm, ffn1_w_gu=ffn1_w_gu, ffn1_w_down=ffn1_w_down, mix_norm=mix_norm,
             w_in=w_in, ssd_conv_w=ssd_conv_w, ssd_conv_b=ssd_conv_b, ssd_dt_bias=ssd_dt_bias, ssd_a_log=ssd_a_log,
             ssd_d=ssd_d, ssd_norm=ssd_norm, hg_lower_bound=hg_lower_bound, hg_norm=hg_norm, w_branch_a=w_branch_a,
             w_branch_b=w_branch_b, w_out=w_out, ffn2_norm=ffn2_norm, ffn2_w_gu=ffn2_w_gu, ffn2_w_down=ffn2_w_down,
             final_norm=final_norm)
    Mo = dict(meta_tokens=m_meta_tokens, ffn1_norm=m_ffn1_norm, ffn1_w_gu=m_ffn1_w_gu, ffn1_w_down=m_ffn1_w_down,
              mix_norm=m_mix_norm, w_in=m_w_in, ssd_conv_w=m_ssd_conv_w, ssd_conv_b=m_ssd_conv_b, ssd_dt_bias=m_ssd_dt_bias,
              ssd_a_log=m_ssd_a_log, ssd_d=m_ssd_d, ssd_norm=m_ssd_norm, hg_lower_bound=m_hg_lower_bound, hg_norm=m_hg_norm,
              w_branch_a=m_w_branch_a, w_branch_b=m_w_branch_b, w_out=m_w_out, ffn2_norm=m_ffn2_norm, ffn2_w_gu=m_ffn2_w_gu,
              ffn2_w_down=m_ffn2_w_down, final_norm=m_final_norm)
    Vo = dict(meta_tokens=v_meta_tokens, ffn1_norm=v_ffn1_norm, ffn1_w_gu=v_ffn1_w_gu, ffn1_w_down=v_ffn1_w_down,
              mix_norm=v_mix_norm, w_in=v_w_in, ssd_conv_w=v_ssd_conv_w, ssd_conv_b=v_ssd_conv_b, ssd_dt_bias=v_ssd_dt_bias,
              ssd_a_log=v_ssd_a_log, ssd_d=v_ssd_d, ssd_norm=v_ssd_norm, hg_lower_bound=v_hg_lower_bound, hg_norm=v_hg_norm,
              w_branch_a=v_w_branch_a, w_branch_b=v_w_branch_b, w_out=v_w_out, ffn2_norm=v_ffn2_norm, ffn2_w_gu=v_ffn2_w_gu,
              ffn2_w_down=v_ffn2_w_down, final_norm=v_final_norm)
    grads, deltas, new_m, new_v = {}, {}, {}, {}
    big_names = ["ffn1_w_gu", "ffn1_w_down", "w_in", "w_branch_a", "w_branch_b", "w_out", "ffn2_w_gu", "ffn2_w_down"]
    transposed = ("ffn1_w_gu", "ffn2_w_gu", "w_in")
    small_all = None
    for nm, part in zip(big_names, parts):
        view = (lambda a: a[0].T) if nm in transposed else (lambda a: a[0])
        back = (lambda o: o.T[None]) if nm in transposed else (lambda o: o[None])
        outs = _adamw("adamw_" + nm, part, view(W[nm]), view(Mo[nm]), view(Vo[nm]),
                      comm=("gather", [small_packed]) if small_all is None else None)
        if small_all is None:
            small_all = outs[4]
        grads[nm], deltas[nm], new_m[nm], new_v[nm] = (back(o) for o in outs[:4])
    unpacked = _unpack_rows(_sum_parts("sum_small_grads", small_all), small_like)
    g_small = unpacked[:len(small_grads)]
    g_meta_full = unpacked[len(small_grads)]
    for b in range(1, Bl):
        g_meta_full = g_meta_full + unpacked[len(small_grads) + b]
    g_meta = lax.dynamic_slice_in_dim(g_meta_full, me * (D // N_DEV), D // N_DEV, axis=1)
    g_conv_w = lax.dynamic_slice_in_dim(g_small[11], me * (SSD_CONV_CH // N_DEV), SSD_CONV_CH // N_DEV, axis=1)
    loss = unpacked[-1].reshape(())
    small_names = ["ffn1_norm", "mix_norm", "ssd_conv_b", "ssd_dt_bias", "ssd_a_log", "ssd_d", "ssd_norm", "hg_lower_bound",
                   "hg_norm", "ffn2_norm", "final_norm", "ssd_conv_w", "meta_tokens"]
    small_g = g_small[:11] + [g_conv_w.reshape(ssd_conv_w.shape), g_meta]
    pk = lambda d: _pack_rows([d[nm] for nm in small_names])
    outs = _adamw("adamw_small", _pack_rows(small_g)[None], pk(W), pk(Mo), pk(Vo))
    like = [W[nm] for nm in small_names]
    for dst, o in zip((grads, deltas, new_m, new_v), outs):
        for nm, val in zip(small_names, _unpack_rows(o, like)):
            dst[nm] = val

    return (loss, grad_x, *[grads[nm] for nm in names], *[deltas[nm] for nm in names],
            *[new_m[nm] for nm in names], *[new_v[nm] for nm in names])
```

```python
import functools

import jax
import jax.numpy as jnp
from jax import lax
from jax.experimental import pallas as pl
from jax.experimental.pallas import tpu as pltpu

F32, BF16 = jnp.float32, jnp.bfloat16
NN, NT, TN = ((1,), (0,)), ((1,), (1,)), ((0,), (0,))
MESH_AXES = ("x", "y", "c")
N_DEV = 8

D_MODEL = 1024
N_META = 16
EPS = 1e-6
SSD_HEADS, SSD_HEAD_DIM, SSD_GROUPS, SSD_STATE, SSD_CONV, Q = 16, 64, 4, 128, 4, 128
SSD_INNER = SSD_HEADS * SSD_HEAD_DIM
SSD_CONV_CH = SSD_INNER + 2 * SSD_GROUPS * SSD_STATE
HG_WIDTH, HG_HEADS, HG_CHUNK = 1024, 8, 16
PAD = Q - N_META
N_MAIN = 9 * 1024
ADAM_LR, ADAM_B1, ADAM_B2, ADAM_EPS, ADAM_WD, ADAM_STEP = 0.001, 0.9, 0.999, 1e-08, 0.01, 10
VMEM_LIMIT = 52 * 1024 * 1024


def _dot(a, b, dims):
    return lax.dot_general(a, b, (dims, ((), ())), preferred_element_type=F32)


def _dot01(a, b, dims, sel):
    x = b if sel == "a" else a
    hi = x.astype(BF16)
    r1 = x - hi.astype(F32)
    mid = r1.astype(BF16)
    lo = (r1 - mid.astype(F32)).astype(BF16)
    s = (a if sel == "a" else b).astype(BF16)
    parts = [_dot(s, p, dims) if sel == "a" else _dot(p, s, dims) for p in (hi, mid, lo)]
    return parts[0] + parts[1] + parts[2]


def _sigmoid(x):
    return 1.0 / (1.0 + jnp.exp(-x))


def _dsilu(x, s):
    return s * (1.0 + x * (1.0 - s))


def _softplus(x):
    e = jnp.exp(-jnp.abs(x))
    u = 1.0 + e
    log1p_e = jnp.where(u == 1.0, e, jnp.log(u) * e / (u - 1.0))
    return jnp.maximum(x, 0.0) + log1p_e


def _params(sem):
    return pltpu.CompilerParams(dimension_semantics=sem, vmem_limit_bytes=VMEM_LIMIT)


def _tile(n, prefs):
    for p in prefs:
        if n % p == 0:
            return p
    return n


CHIP_FLIPS = ((1, 0), (0, 1), (1, 1))
N_PEER = N_DEV - 1


def _comm_gather(srcs, outs, send_sems, recv_sems, local_sems):
    n = len(srcs)
    x, y, c = (lax.axis_index(a) for a in MESH_AXES)
    dev = lambda px, py, pc: 4 * px + 2 * py + pc
    me, sib = dev(x, y, c), (x, y, 1 - c)
    nbr_x, nbr_y, diag = (1 - x, y), (x, 1 - y), (1 - x, 1 - y)
    via = (x ^ c, y ^ (1 - c), c)
    sent_on = dev(x ^ (1 - c), y ^ c, c)

    def rc(w, k, slot, to, src=None):
        return pltpu.make_async_remote_copy(
            src_ref=outs[w].at[slot] if src is None else src, dst_ref=outs[w].at[slot],
            send_sem=send_sems.at[w, k], recv_sem=recv_sems.at[w, k], device_id=to, device_id_type=pl.DeviceIdType.MESH)

    def local(w):
        return pltpu.make_async_copy(srcs[w], outs[w].at[me], local_sems.at[w])

    def start():
        for w in range(n):
            local(w).start()
            rc(w, 0, me, sib, src=srcs[w]).start()
            rc(w, 1, me, (*nbr_x, c), src=srcs[w]).start()
            rc(w, 2, me, (*nbr_y, c), src=srcs[w]).start()

    def pass_on():
        for w in range(n):
            rc(w, 1, dev(*nbr_x, c), sib).wait_recv()
            rc(w, 2, dev(*nbr_y, c), sib).wait_recv()
            rc(w, 3, sent_on, via).start()
            rc(w, 4, dev(*nbr_x, c), sib).start()
            rc(w, 5, dev(*nbr_y, c), sib).start()

    def pass_on_diagonal():
        for w in range(n):
            rc(w, 3, dev(*diag, c), sib).wait_recv()
            rc(w, 6, dev(*diag, c), sib).start()

    def finish():
        for w in range(n):
            rc(w, 0, dev(x, y, 1 - c), sib).wait_recv()
            for k, chip in ((4, nbr_x), (5, nbr_y), (6, diag)):
                rc(w, k, dev(*chip, 1 - c), sib).wait_recv()
            for k in range(N_PEER):
                rc(w, k, me, sib, src=srcs[w]).wait_send()
            local(w).wait()

    return start, (pass_on, pass_on_diagonal), finish


def _comm_scatter(srcs, outs, send_sems, recv_sems, local_sems):
    n = len(srcs)
    x, y, c = (lax.axis_index(a) for a in MESH_AXES)
    me = 4 * x + 2 * y + c

    def copies():
        out = []
        for w in range(n):
            out.append(pltpu.make_async_copy(srcs[w].at[me], outs[w].at[me], local_sems.at[w]))
            for k in range(1, N_DEV):
                px, py, pc = x ^ (k >> 2), y ^ ((k >> 1) & 1), c ^ (k & 1)
                out.append(pltpu.make_async_remote_copy(
                    src_ref=srcs[w].at[4 * px + 2 * py + pc], dst_ref=outs[w].at[me],
                    send_sem=send_sems.at[w, k - 1], recv_sem=recv_sems.at[w, k - 1],
                    device_id=(px, py, pc), device_id_type=pl.DeviceIdType.MESH))
        return out

    def start():
        for cp in copies():
            cp.start()

    def finish():
        for cp in copies():
            cp.wait()

    return start, None, finish


def _comm_swap(srcs, outs, send_sems, recv_sems, local_sems):
    x, y, c = (lax.axis_index(a) for a in MESH_AXES)

    def copies():
        return [pltpu.make_async_remote_copy(
            src_ref=srcs[w].at[1 - c], dst_ref=outs[w], send_sem=send_sems.at[w, 0], recv_sem=recv_sems.at[w, 0],
            device_id=(x, y, 1 - c), device_id_type=pl.DeviceIdType.MESH) for w in range(len(srcs))]

    def start():
        for cp in copies():
            cp.start()

    def finish():
        for cp in copies():
            cp.wait()

    return start, None, finish


def _comm_chips(srcs, outs, send_sems, recv_sems, local_sems):
    n = len(srcs)
    x, y, c = (lax.axis_index(a) for a in MESH_AXES)
    mine = 2 * x + y

    def copies():
        out = []
        for w in range(n):
            out.append(pltpu.make_async_copy(srcs[w].at[mine], outs[w].at[mine], local_sems.at[w]))
            for j, (fx, fy) in enumerate(CHIP_FLIPS):
                px, py = x ^ fx, y ^ fy
                out.append(pltpu.make_async_remote_copy(
                    src_ref=srcs[w].at[2 * px + py], dst_ref=outs[w].at[mine],
                    send_sem=send_sems.at[w, j], recv_sem=recv_sems.at[w, j],
                    device_id=(px, py, c), device_id_type=pl.DeviceIdType.MESH))
        return out

    def start():
        for cp in copies():
            cp.start()

    def finish():
        for cp in copies():
            cp.wait()

    return start, None, finish


def _comm_parts(comm):
    kind, arrays = comm[:2]
    n = len(arrays)
    lead = {"gather": lambda a: (N_DEV,) + a.shape, "scatter": lambda a: (N_DEV,) + a.shape[1:],
            "swap": lambda a: a.shape[1:], "chips": lambda a: a.shape}[kind]
    shapes = [jax.ShapeDtypeStruct(lead(a), a.dtype) for a in arrays]
    sems = [pltpu.SemaphoreType.DMA((n, N_PEER)), pltpu.SemaphoreType.DMA((n, N_PEER)), pltpu.SemaphoreType.DMA((n,))]
    make = {"gather": _comm_gather, "scatter": _comm_scatter, "swap": _comm_swap, "chips": _comm_chips}[kind]
    return n, shapes, sems, make


def _exchange(name, kind, arrays):
    n, shapes, sems, make = _comm_parts((kind, arrays))

    def body(*refs):
        start, middle, finish = make(refs[:n], refs[n:2 * n], *refs[2 * n:])
        start()
        for stage in middle or ():
            stage()
        finish()

    any_spec = pl.BlockSpec(memory_space=pl.ANY)
    return pl.pallas_call(
        body, name=name, in_specs=[any_spec] * n, out_specs=[any_spec] * n, out_shape=shapes, scratch_shapes=sems,
        compiler_params=pltpu.CompilerParams(has_side_effects=True),
    )(*arrays)


def _call(body, *, name, grid, in_specs, out_specs, out_shape, scratch, sem, args, comm=None, into=None):
    any_spec = pl.BlockSpec(memory_space=pl.ANY)
    in_specs, args, aliases, n_body_in = list(in_specs), list(args), {}, len(in_specs)
    if into is not None:
        in_specs.append(any_spec)
        args.append(into[0])
        aliases = {n_body_in: into[1]}
    n_in, n_out, n_scr = len(in_specs), len(out_specs), len(scratch)
    if comm is None:
        def plain(*refs):
            body(*refs[:n_body_in], *refs[n_in:])

        return pl.pallas_call(plain, name=name, grid=grid, in_specs=in_specs, out_specs=out_specs, out_shape=out_shape,
                              scratch_shapes=scratch, input_output_aliases=aliases, compiler_params=_params(sem))(*args)
    n, shapes, sems, make = _comm_parts(comm)

    def carrier(*refs):
        ins, csrc = refs[:n_body_in], refs[n_in:n_in + n]
        outs, cout = refs[n_in + n:n_in + n + n_out], refs[n_in + n + n_out:n_in + 2 * n + n_out]
        rest = refs[n_in + 2 * n + n_out:]
        start, middle, finish = make(csrc, cout, *rest[n_scr:])
        ids = [pl.program_id(a) for a in range(len(grid))]
        step = functools.reduce(lambda acc, ig: acc * ig[1] + ig[0], zip(ids, grid), 0)
        n_steps = functools.reduce(lambda a, b: a * b, grid, 1)
        pl.when(step == 0)(start)
        body(*ins, *outs, *rest[:n_scr])
        if middle:
            early = len(comm) > 2 and comm[2] == "early"
            pl.when(step == ((3 * n_steps) // 4 if early else n_steps - 1))(middle[0])
            pl.when(step == n_steps - 1)(middle[1])
        pl.when(step == n_steps - 1)(finish)

    return pl.pallas_call(
        carrier, name=name, grid=grid, in_specs=in_specs + [any_spec] * n,
        out_specs=list(out_specs) + [any_spec] * n, out_shape=list(out_shape) + shapes,
        scratch_shapes=list(scratch) + sems, input_output_aliases=aliases,
        compiler_params=pltpu.CompilerParams(dimension_semantics=("arbitrary",) * len(grid),
                                             vmem_limit_bytes=VMEM_LIMIT, has_side_effects=True),
    )(*args, *comm[1])


def _fused_matmul(name, M, N, K, pairs, extras, epilogue, out_dtypes, n_acc, tm, tn, tk, outer="i", comm=None,
                  stack=False, vecs=(), row_sums=0, wide=None, sub=None):
    nk = K // tk
    n_pairs, n_ex, n_out = len(pairs), len(extras), len(out_dtypes)
    assert not row_sums or (outer == "i" and N == tn)

    def ij(g0, g1):
        return (g0, g1) if outer == "i" else (g1, g0)

    in_specs, args = [], []
    for p in pairs:
        ao, bk, bn = p.get("a_off", 0), p.get("bk_off", 0), p.get("bn_off", 0)
        mode = dict(pipeline_mode=pl.Buffered(1)) if p.get("resident") else {}
        in_specs.append(pl.BlockSpec((tm, tk), lambda g0, g1, k, ao=ao: (ij(g0, g1)[0], k + ao)))
        if "b_shift" in p:
            first, shift = p["b_shift"]
            if p.get("trans_b"):
                in_specs.append(pl.BlockSpec(
                    (pl.Element(tn), pl.Element(tk)),
                    lambda g0, g1, k, bk=bk: (
                        pl.multiple_of(ij(g0, g1)[1] * tn + jnp.where(ij(g0, g1)[1] >= first, shift, 0), 16),
                        (k + bk) * tk)))
            else:
                in_specs.append(pl.BlockSpec(
                    (pl.Element(tk), pl.Element(tn)),
                    lambda g0, g1, k, bn=bn: (pl.multiple_of(k * tk + jnp.where(k >= first, shift, 0), 16),
                                              (ij(g0, g1)[1] + bn) * tn)))
        elif p.get("trans_b"):
            in_specs.append(pl.BlockSpec((tn, tk), lambda g0, g1, k, bk=bk, bn=bn: (ij(g0, g1)[1] + bn, k + bk), **mode))
        else:
            in_specs.append(pl.BlockSpec((tk, tn), lambda g0, g1, k, bk=bk, bn=bn: (k + bk, ij(g0, g1)[1] + bn), **mode))
        args += [p["a"], p["b"]]
    for arr, off in extras:
        in_specs.append(pl.BlockSpec((tm, tn), lambda g0, g1, k, off=off: (ij(g0, g1)[0], ij(g0, g1)[1] + off)))
        args.append(arr)
    for arr in vecs:
        in_specs.append(pl.BlockSpec((1, tn), lambda g0, g1, k: (0, ij(g0, g1)[1])))
        args.append(arr)
    if stack:
        assert N == tn
        out_specs = [pl.BlockSpec((tm, n_out * tn), lambda g0, g1, k: (ij(g0, g1)[0], 0))]
        out_shape = [jax.ShapeDtypeStruct((M, n_out * N), out_dtypes[0])]
    else:
        out_specs = [pl.BlockSpec((tm, tn), lambda g0, g1, k: ij(g0, g1)) for _ in out_dtypes]
        out_shape = [jax.ShapeDtypeStruct((M, N), dt) for dt in out_dtypes]
    if wide:
        out_specs.append(pl.BlockSpec((pl.Element(tm), pl.Element(wide["width"])),
                                      lambda g0, g1, k: (pl.multiple_of(ij(g0, g1)[0] * tm, 16), wide["col"])))
        out_shape.append(jax.ShapeDtypeStruct((M, wide["total"]), wide["dtype"]))
    n_tile_out = len(out_specs)
    out_specs += [pl.BlockSpec((1, tn), lambda g0, g1, k: (0, 0)) for _ in range(row_sums)]
    out_shape += [jax.ShapeDtypeStruct((1, N), F32) for _ in range(row_sums)]
    grid = (M // tm, N // tn, nk) if outer == "i" else (N // tn, M // tm, nk)
    n_in = 2 * n_pairs + n_ex + len(vecs)

    def partials(refs, cs=slice(None)):
        accs = [None] * n_acc
        for idx, p in enumerate(pairs):
            b_ref = refs[2 * idx + 1]
            d = (_dot(refs[2 * idx][...], b_ref[cs, :], NT) if p.get("trans_b")
                 else _dot(refs[2 * idx][...], b_ref[:, cs], NN))
            accs[p["acc"]] = d if accs[p["acc"]] is None else accs[p["acc"]] + d
        return accs

    def finish(accs, refs, first_rows, cs=slice(None)):
        res = epilogue(accs, [r[:, cs] for r in refs[2 * n_pairs:n_in]])
        if stack:
            o = refs[n_in]
            for idx in range(n_out):
                lo = idx * tn + (cs.start or 0)
                o[:, lo:lo + (tn if cs.stop is None else cs.stop - cs.start)] = res[idx].astype(o.dtype)
        else:
            for o, r in zip(refs[n_in:n_in + n_out], res):
                o[:, cs] = r.astype(o.dtype)
        if wide:
            o = refs[n_in + n_tile_out - 1]
            o[...] = res[n_out].astype(o.dtype)
        for o, r in zip(refs[n_in + n_tile_out:n_in + n_tile_out + row_sums], res[n_out + bool(wide):]):
            @pl.when(first_rows)
            def _(o=o, r=r):
                o[...] = r

            @pl.when(jnp.logical_not(first_rows))
            def _(o=o, r=r):
                o[...] += r

    if nk == 1 and sub:
        assert not wide and not row_sums and tn % sub == 0

        def body(*refs):
            for c in range(tn // sub):
                cs = slice(c * sub, (c + 1) * sub)
                finish(partials(refs, cs), refs, None, cs)
        scratch = []
    elif nk == 1:
        def body(*refs):
            finish(partials(refs), refs, pl.program_id(0) == 0)
        scratch = []
    else:
        def body(*refs):
            acc_refs = refs[-n_acc:]
            k = pl.program_id(2)
            first_rows = pl.program_id(0) == 0
            new = partials(refs)

            @pl.when(k == 0)
            def _():
                for a, v in zip(acc_refs, new):
                    a[...] = v

            @pl.when(k > 0)
            def _():
                for a, v in zip(acc_refs, new):
                    a[...] += v

            @pl.when(k == nk - 1)
            def _():
                finish([a[...] for a in acc_refs], refs, first_rows)
        scratch = [pltpu.VMEM((tm, tn), F32) for _ in range(n_acc)]

    return _call(body, name=name, grid=grid, in_specs=in_specs, out_specs=out_specs, out_shape=out_shape,
                 scratch=scratch, sem=("parallel", "parallel", "arbitrary"), args=args, comm=comm)


def _matmul_tn(name, x, y, t1, t2, tr, scale=1.0, comm=None, out_dtype=BF16, out_skip=None):
    R, K1 = x.shape
    N1 = y.shape[1]
    nr, n1 = R // tr, K1 // t1
    x_spec = pl.BlockSpec((tr, t1), lambda i, j, r: (r, i))
    rows_out = K1
    o_spec = pl.BlockSpec((t1, t2), lambda i, j, r: (i, j))
    if out_skip:
        row, count = out_skip
        rows_out += count
        o_spec = pl.BlockSpec(
            (pl.Element(t1), pl.Element(t2)),
            lambda i, j, r: (pl.multiple_of(i * t1 + jnp.where(i * t1 >= row, count, 0), 16), j * t2))

    def body(x_ref, y_ref, o_ref, *acc):
        d = _dot(x_ref[...], y_ref[...], TN)
        if nr == 1:
            o_ref[...] = (d * scale).astype(o_ref.dtype)
            return
        r = pl.program_id(2)

        @pl.when(r == 0)
        def _():
            acc[0][...] = d

        @pl.when(jnp.logical_and(r > 0, r < nr - 1))
        def _():
            acc[0][...] += d

        @pl.when(r == nr - 1)
        def _():
            o_ref[...] = ((acc[0][...] + d) * scale).astype(o_ref.dtype)

    return _call(
        body, name=name, grid=(n1, N1 // t2, nr),
        in_specs=[x_spec, pl.BlockSpec((tr, t2), lambda i, j, r: (r, j))], out_specs=[o_spec],
        out_shape=[jax.ShapeDtypeStruct((rows_out, N1), out_dtype)],
        scratch=[pltpu.VMEM((t1, t2), F32)] if nr > 1 else [],
        sem=("parallel", "parallel", "arbitrary"), args=(x, y), comm=comm)


def _embed_norm(x, meta, w, comm=None):
    Bl, S, D = x.shape
    nb = (PAD + N_META + S) // Q
    M = Bl * nb * Q

    def body(x_ref, meta_ref, w_ref, h_ref, n_ref):
        head = jnp.concatenate([jnp.zeros((PAD, D), F32), meta_ref[...]], axis=0)
        h = jnp.where(pl.program_id(1) == 0, head, x_ref[0])
        h_ref[...] = h
        n_ref[...] = _rmsnorm_tile(h, w_ref[...]).astype(n_ref.dtype)

    row = pl.BlockSpec((Q, D), lambda b, t: (b * nb + t, 0))
    return _call(
        body, name="embed_norm", grid=(Bl, nb),
        in_specs=[pl.BlockSpec((1, Q, D), lambda b, t: (b, jnp.maximum(t - 1, 0), 0)),
                  pl.BlockSpec((N_META, D), lambda b, t: (0, 0)), pl.BlockSpec((1, D), lambda b, t: (0, 0))],
        out_specs=[row, row], out_shape=[jax.ShapeDtypeStruct((M, D), F32), jax.ShapeDtypeStruct((M, D), BF16)],
        scratch=[], sem=("parallel", "parallel"), args=(x, meta, w), comm=comm)


def _rmsnorm_bwd_tile(dn, h, w, dh_in):
    r = lax.rsqrt(jnp.mean(h * h, axis=-1, keepdims=True) + EPS)
    xhat = h * r
    gw = dn * w
    dh = dh_in + r * (gw - xhat * jnp.mean(gw * xhat, axis=-1, keepdims=True))
    return dh, jnp.sum(dn * xhat, axis=0, keepdims=True)


def _loss_head(h, w, target, Bl, nb):
    M, D = h.shape

    def body(h_ref, w_ref, t_ref, dh_ref, dhb_ref, dw_ref, loss_ref):
        b, t = pl.program_id(0), pl.program_id(1)
        live = (t > 0).astype(F32)
        x = h_ref[...]
        r = lax.rsqrt(jnp.mean(x * x, axis=-1, keepdims=True) + EPS)
        xhat = x * r
        wv = w_ref[...]
        err = (xhat * wv - t_ref[0]) * live
        dy = err * (1.0 / D)
        gw = dy * wv
        dx = r * (gw - xhat * jnp.mean(gw * xhat, axis=-1, keepdims=True))
        dh_ref[...] = dx
        dhb_ref[...] = dx.astype(BF16)
        dw = jnp.sum(dy * xhat, axis=0, keepdims=True)
        part = 0.5 * jnp.sum(jnp.sum(err * err, axis=-1, keepdims=True) * (1.0 / D), axis=0, keepdims=True)
        first = jnp.logical_and(b == 0, t == 0)

        @pl.when(first)
        def _():
            dw_ref[...] = dw
            loss_ref[...] = jnp.broadcast_to(part, loss_ref.shape)

        @pl.when(jnp.logical_not(first))
        def _():
            dw_ref[...] += dw
            loss_ref[...] += jnp.broadcast_to(part, loss_ref.shape)

    row = pl.BlockSpec((Q, D), lambda b, t: (b * nb + t, 0))
    vec = pl.BlockSpec((1, D), lambda b, t: (0, 0))
    return pl.pallas_call(
        body, name="loss_head", grid=(Bl, nb),
        in_specs=[row, vec, pl.BlockSpec((1, Q, D), lambda b, t: (b, jnp.maximum(t - 1, 0), 0))],
        out_specs=[row, row, vec, pl.BlockSpec((8, 128), lambda b, t: (0, 0))],
        out_shape=[jax.ShapeDtypeStruct((M, D), F32), jax.ShapeDtypeStruct((M, D), BF16),
                   jax.ShapeDtypeStruct((1, D), F32), jax.ShapeDtypeStruct((8, 128), F32)],
        compiler_params=_params(("arbitrary", "arbitrary")),
    )(h, w, target)


CONV_TC = 256


def _conv_pre(xr_ref, w_ref, b_ref):
    x = xr_ref[...].astype(F32)
    acc = b_ref[...] + w_ref[SSD_CONV - 1:SSD_CONV, :] * x
    for k in range(1, SSD_CONV):
        acc = acc + w_ref[SSD_CONV - 1 - k:SSD_CONV - k, :] * pltpu.roll(x, k, 0)
    return x, acc


def _conv_fwd(proj, w, b, Bl, T):
    M = proj.shape[0]
    off = 1024 // CONV_TC

    def body(xr_ref, w_ref, b_ref, o_ref):
        _, acc = _conv_pre(xr_ref, w_ref, b_ref)
        row = lax.broadcasted_iota(jnp.int32, acc.shape, 0)
        o_ref[...] = jnp.where(row >= PAD, acc * _sigmoid(acc), 0.0).astype(o_ref.dtype)

    return pl.pallas_call(
        body, name="conv_fwd", grid=(Bl, SSD_CONV_CH // CONV_TC),
        in_specs=[pl.BlockSpec((T, CONV_TC), lambda bb, j: (bb, j + off)),
                  pl.BlockSpec((SSD_CONV, CONV_TC), lambda bb, j: (0, j)), pl.BlockSpec((1, CONV_TC), lambda bb, j: (0, j))],
        out_specs=pl.BlockSpec((T, CONV_TC), lambda bb, j: (bb, j)),
        out_shape=jax.ShapeDtypeStruct((M, SSD_CONV_CH), BF16), compiler_params=_params(("parallel", "parallel")),
    )(proj, w, b)


def _conv_bwd(proj, w, b, dxc, dproj, Bl, T):
    M = proj.shape[0]
    off = 1024 // CONV_TC

    def body(xr_ref, w_ref, b_ref, d_ref, dx_ref, dw_ref, db_ref):
        x, acc = _conv_pre(xr_ref, w_ref, b_ref)
        row = lax.broadcasted_iota(jnp.int32, acc.shape, 0)
        s = _sigmoid(acc)
        dpre = jnp.where(row >= PAD, d_ref[...].astype(F32) * _dsilu(acc, s), 0.0)
        dx = w_ref[SSD_CONV - 1:SSD_CONV, :] * dpre
        dws = [jnp.sum(dpre * x, axis=0, keepdims=True)]
        for k in range(1, SSD_CONV):
            dx = dx + w_ref[SSD_CONV - 1 - k:SSD_CONV - k, :] * pltpu.roll(dpre, T - k, 0)
            dws.append(jnp.sum(dpre * pltpu.roll(x, k, 0), axis=0, keepdims=True))
        dx_ref[...] = dx.astype(dx_ref.dtype)
        dw = jnp.concatenate(dws[::-1], axis=0)
        db = jnp.sum(dpre, axis=0, keepdims=True)

        @pl.when(pl.program_id(1) == 0)
        def _():
            dw_ref[...] = dw
            db_ref[...] = db

        @pl.when(pl.program_id(1) > 0)
        def _():
            dw_ref[...] += dw
            db_ref[...] += db

    return _call(
        body, name="conv_bwd", grid=(SSD_CONV_CH // CONV_TC, Bl),
        in_specs=[pl.BlockSpec((T, CONV_TC), lambda j, bb: (bb, j + off)),
                  pl.BlockSpec((SSD_CONV, CONV_TC), lambda j, bb: (0, j)), pl.BlockSpec((1, CONV_TC), lambda j, bb: (0, j)),
                  pl.BlockSpec((T, CONV_TC), lambda j, bb: (bb, j))],
        out_specs=[pl.BlockSpec((T, CONV_TC), lambda j, bb: (bb, j + off)),
                   pl.BlockSpec((SSD_CONV, CONV_TC), lambda j, bb: (0, j)), pl.BlockSpec((1, CONV_TC), lambda j, bb: (0, j))],
        out_shape=[jax.ShapeDtypeStruct(dproj.shape, BF16), jax.ShapeDtypeStruct((SSD_CONV, SSD_CONV_CH), F32),
                   jax.ShapeDtypeStruct((1, SSD_CONV_CH), F32)],
        scratch=[], sem=("parallel", "arbitrary"), args=(proj, w, b, dxc), into=(dproj, 0))


N_PAIR = SSD_HEADS // 2
HPG = SSD_HEADS // SSD_GROUPS
GW = SSD_INNER // SSD_GROUPS


def _per_group(fn, *arrs):
    return jnp.concatenate([jnp.broadcast_to(fn(*(a[:, GW * g:GW * (g + 1)] for a in arrs)), (arrs[0].shape[0], GW))
                            for g in range(SSD_GROUPS)], axis=1)


def _ssd_prep(c, dtr_ref, bias_ref, alog_ref, d_ref):
    row = lax.broadcasted_iota(jnp.int32, (Q, 128), 0)
    col = lax.broadcasted_iota(jnp.int32, (Q, 128), 1)
    live = col < SSD_HEADS
    valid = jnp.logical_and(jnp.logical_or(c > 0, row >= PAD), live)
    pre = dtr_ref[...] + bias_ref[...]
    dt = jnp.where(valid, _softplus(pre), 0.0)
    A = jnp.where(live[0:1], -jnp.exp(alog_ref[...]), 0.0)
    tri = row >= col
    eye = (row == col).astype(BF16)
    cs = _dot01(tri, dt * A, NN, "a")
    cst = _dot01(eye, cs, NT, "a")
    spread = (lax.broadcasted_iota(jnp.int32, (128, SSD_INNER), 0)
              == lax.broadcasted_iota(jnp.int32, (128, SSD_INNER), 1) // SSD_HEAD_DIM).astype(BF16)
    dt_w = _dot01(dt, spread, NN, "b")
    cs_w = _dot01(cs, spread, NN, "b")
    d_w = _dot01(jnp.broadcast_to(d_ref[...], (8, 128)), spread, NN, "b")[0:1]
    lane = lax.broadcasted_iota(jnp.int32, (Q, SSD_INNER), 1)
    first = (lane % 128) < SSD_HEAD_DIM
    return dict(row=row, col=col, valid=valid, pre=pre, dt=dt, A=A, tri=tri, eye=eye, cs=cs, cst=cst, spread=spread,
                dt_w=dt_w, cs_w=cs_w, d_w=d_w, ecs_w=jnp.exp(cs_w), decay_w=jnp.exp(cs_w[Q - 1:Q] - cs_w), first=first)


def _ssd_chunk(xc_ref, s, states):
    xv = xc_ref[:, 0:SSD_INNER].astype(F32)
    Bs = [xc_ref[:, SSD_INNER + 128 * g:SSD_INNER + 128 * (g + 1)] for g in range(SSD_GROUPS)]
    Cs = [xc_ref[:, SSD_INNER + 512 + 128 * g:SSD_INNER + 512 + 128 * (g + 1)] for g in range(SSD_GROUPS)]
    X = xv * s["dt_w"]
    X0 = jnp.where(s["first"], X, 0.0)
    Xb = (X0.astype(BF16), (X - X0).astype(BF16))
    Xd = (X * s["decay_w"]).astype(BF16)
    CB = [_dot(Cs[g], Bs[g], NT) for g in range(SSD_GROUPS)]
    Lms = [jnp.exp(jnp.where(s["tri"], s["cs"][:, h:h + 1] - s["cst"][h:h + 1, :], -jnp.inf)) for h in range(SSD_HEADS)]
    Ms = [CB[h // HPG] * Lms[h] for h in range(SSD_HEADS)]
    Mb = [m.astype(BF16) for m in Ms]
    prev_b = [st.astype(BF16) for st in states]
    yds, yos, sts = [], [], []
    for p in range(N_PAIR):
        g, ln = p // 2, slice(128 * p, 128 * (p + 1))
        yds.append(_dot(Mb[2 * p], Xb[0][:, ln], NN) + _dot(Mb[2 * p + 1], Xb[1][:, ln], NN))
        yos.append(_dot(Cs[g], prev_b[p], NT))
        sts.append(_dot(Xd[:, ln], Bs[g], TN))
    yo = jnp.concatenate(yos, axis=1)
    y = jnp.concatenate(yds, axis=1) + yo * s["ecs_w"] + xv * s["d_w"]
    upper = s["row"] < SSD_HEAD_DIM
    cl = s["cs"][Q - 1:Q, :]
    ecl_rows = [jnp.where(upper, jnp.exp(cl[:, 2 * p:2 * p + 1]), jnp.exp(cl[:, 2 * p + 1:2 * p + 2])) for p in range(N_PAIR)]
    new_states = [states[p] * ecl_rows[p] + sts[p] for p in range(N_PAIR)]
    return y, new_states, dict(xv=xv, Bs=Bs, Cs=Cs, X=X, Xb=Xb, CB=CB, Lms=Lms, Ms=Ms, Mb=Mb, prev_b=prev_b, yo=yo,
                               ecl_rows=ecl_rows)


def _ssd_in_specs(nc, rev=False):
    rb = (lambda b, c: b * nc + nc - 1 - c) if rev else (lambda b, c: b * nc + c)
    vec = pl.BlockSpec((1, 128), lambda b, c: (0, 0))
    return [pl.BlockSpec((Q, SSD_CONV_CH), lambda b, c: (rb(b, c), 0)),
            pl.BlockSpec((Q, 128), lambda b, c: (rb(b, c), 0)),
            pl.BlockSpec((Q, SSD_INNER), lambda b, c: (rb(b, c), 0)),
            vec, vec, vec, pl.BlockSpec((1, SSD_INNER), lambda b, c: (0, 0))]


def _ssd_fwd(xc, dtr, proj, bias_p, alog_p, d_p, nw, Bl, nc):
    M = xc.shape[0]

    def body(xc_ref, dtr_ref, z_ref, bias_ref, alog_ref, d_ref, nw_ref, y_ref, prev_ref, state):
        c = pl.program_id(1)

        @pl.when(c == 0)
        def _():
            state[...] = jnp.zeros_like(state)

        s = _ssd_prep(c, dtr_ref, bias_ref, alog_ref, d_ref)
        states = [state[p] for p in range(N_PAIR)]
        y, new_states, _ = _ssd_chunk(xc_ref, s, states)
        for p in range(N_PAIR):
            prev_ref[0, 0, p] = states[p]
            state[p] = new_states[p]
        zz = z_ref[...].astype(F32)
        yg = y * zz * _sigmoid(zz)
        r = _per_group(lambda a: lax.rsqrt(jnp.mean(a * a, axis=-1, keepdims=True) + EPS), yg)
        y_ref[...] = (yg * r * nw_ref[...]).astype(y_ref.dtype)

    return pl.pallas_call(
        body, name="ssd_fwd", grid=(Bl, nc), in_specs=_ssd_in_specs(nc),
        out_specs=[pl.BlockSpec((Q, SSD_INNER), lambda b, c: (b * nc + c, 0)),
                   pl.BlockSpec((1, 1, N_PAIR, 128, 128), lambda b, c: (b, c, 0, 0, 0))],
        out_shape=[jax.ShapeDtypeStruct((M, SSD_INNER), BF16), jax.ShapeDtypeStruct((Bl, nc, N_PAIR, 128, 128), F32)],
        scratch_shapes=[pltpu.VMEM((N_PAIR, 128, 128), F32)],
        compiler_params=_params(("arbitrary", "arbitrary")),
    )(xc, dtr, proj, bias_p, alog_p, d_p, nw)


def _ssd_bwd(xc, dtr, proj, bias_p, alog_p, d_p, nw, prev, dya, dproj, Bl, nc, comm=None):
    M = xc.shape[0]

    def body(xc_ref, dtr_ref, z_ref, bias_ref, alog_ref, d_ref, nw_ref, prev_ref, dy_ref,
             dxc_ref, dz_ref, ddtr_ref, dbias_ref, dalog_ref, dd_ref, dnw_ref, dS):
        b, t = pl.program_id(0), pl.program_id(1)

        @pl.when(t == 0)
        def _():
            dS[...] = jnp.zeros_like(dS)

        s = _ssd_prep(nc - 1 - t, dtr_ref, bias_ref, alog_ref, d_ref)
        states = [prev_ref[0, 0, p] for p in range(N_PAIR)]
        y, _, k = _ssd_chunk(xc_ref, s, states)
        xv, Bs, Cs, Xb = k["xv"], k["Bs"], k["Cs"], k["Xb"]

        zz = z_ref[...].astype(F32)
        sz = _sigmoid(zz)
        silu_z = zz * sz
        yg = y * silu_z
        r = _per_group(lambda a: lax.rsqrt(jnp.mean(a * a, axis=-1, keepdims=True) + EPS), yg)
        xhat = yg * r
        dout = dy_ref[...].astype(F32)
        gw = dout * nw_ref[...]
        dyg = r * (gw - xhat * _per_group(lambda a, c2: jnp.mean(a * c2, axis=-1, keepdims=True), gw, xhat))
        dnw = jnp.sum(dout * xhat, axis=0, keepdims=True)
        dz_ref[...] = (dyg * y * _dsilu(zz, sz)).astype(dz_ref.dtype)
        dy = dyg * silu_z
        dy0 = jnp.where(s["first"], dy, 0.0)
        dyb = (dy0.astype(BF16), (dy - dy0).astype(BF16))
        dYo = (dy * s["ecs_w"]).astype(BF16)

        dS_f = [dS[p] for p in range(N_PAIR)]
        dS_b = [d.astype(BF16) for d in dS_f]
        BdS, dXm, dprev, dCs, dMs, XdS = [], [], [], [[] for _ in range(SSD_GROUPS)], [], []
        for p in range(N_PAIR):
            g, ln = p // 2, slice(128 * p, 128 * (p + 1))
            BdS.append(_dot(Bs[g], dS_b[p], NT))
            dXm.append(_dot(k["Mb"][2 * p], dyb[0][:, ln], TN) + _dot(k["Mb"][2 * p + 1], dyb[1][:, ln], TN))
            dprev.append(_dot(dYo[:, ln], Cs[g], TN))
            dCs[g].append(_dot(dYo[:, ln], k["prev_b"][p], NN))
            for hh in range(2):
                dMs.append(_dot(dyb[hh][:, ln], Xb[hh][:, ln], NT))
                XdS.append(_dot(Xb[hh][:, ln], dS_b[p], NN))
        dX = jnp.concatenate(dXm, axis=1) + s["decay_w"] * jnp.concatenate(BdS, axis=1)
        dxs = dy * s["d_w"] + dX * s["dt_w"]

        sums = _dot01(jnp.concatenate([dX * xv, dy * k["yo"] * s["ecs_w"], dy * xv], axis=0), s["spread"], NT, "b")
        ddt, dcs = sums[0:Q], sums[Q:2 * Q]
        dD = jnp.sum(sums[2 * Q:3 * Q], axis=0, keepdims=True)

        col, row = s["col"], s["row"]
        lane1 = col[0:1]
        rowsT = lax.broadcasted_iota(jnp.int32, (128, Q), 0)
        dcs_t = jnp.zeros((128, Q), F32)
        dcl = jnp.zeros((1, 128), F32)
        dB_out, dC_out = [], []
        for g in range(SSD_GROUPS):
            Bf = Bs[g].astype(F32)
            dCB = jnp.zeros((Q, Q), F32)
            dBacc = jnp.zeros((Q, 128), F32)
            for r4 in range(HPG):
                h = HPG * g + r4
                p, hh = h // 2, h % 2
                W = dMs[h] * k["Ms"][h]
                dCB = dCB + dMs[h] * k["Lms"][h]
                decay_h = s["decay_w"][:, SSD_HEAD_DIM * h:SSD_HEAD_DIM * h + 1]
                dBacc = dBacc + decay_h * XdS[h]
                tdec = jnp.sum(XdS[h] * Bf, axis=1, keepdims=True) * decay_h
                dcs = dcs + jnp.where(col == h, jnp.sum(W, axis=1, keepdims=True) - tdec, 0.0)
                dcs_t = dcs_t - jnp.where(rowsT == h, jnp.sum(W, axis=0, keepdims=True), 0.0)
                rows_h = (row < SSD_HEAD_DIM) if hh == 0 else (row >= SSD_HEAD_DIM)
                sprev = jnp.sum(jnp.sum(jnp.where(rows_h, dS_f[p] * states[p], 0.0), axis=1, keepdims=True),
                                axis=0, keepdims=True)
                ecl = jnp.exp(s["cs"][Q - 1:Q, h:h + 1])
                dcl = dcl + jnp.where(lane1 == h, jnp.sum(tdec, axis=0, keepdims=True) + ecl * sprev, 0.0)
            dCB_b = dCB.astype(BF16)
            dC_out.append(dCs[g][0] + dCs[g][1] + _dot(dCB_b, Bs[g], NN))
            dB_out.append(dBacc + _dot(dCB_b, Cs[g], TN))
        for p in range(N_PAIR):
            dS[p] = dS_f[p] * k["ecl_rows"][p] + dprev[p]
        dxc_ref[...] = jnp.concatenate([dxs] + dB_out + dC_out, axis=1).astype(dxc_ref.dtype)

        dcs = dcs + _dot01(s["eye"], dcs_t, NT, "a") + jnp.where(row == Q - 1, dcl, 0.0)
        da = _dot01(row <= col, dcs, NN, "a")
        ddt = ddt + da * s["A"]
        dpre = jnp.where(s["valid"], ddt * _sigmoid(s["pre"]), 0.0)
        ddtr_ref[...] = dpre
        dbias = jnp.sum(dpre, axis=0, keepdims=True)
        dalog = jnp.sum(da * s["dt"], axis=0, keepdims=True) * s["A"]
        first_step = jnp.logical_and(b == 0, t == 0)

        @pl.when(first_step)
        def _():
            dbias_ref[...] = dbias
            dalog_ref[...] = dalog
            dd_ref[...] = dD
            dnw_ref[...] = dnw

        @pl.when(jnp.logical_not(first_step))
        def _():
            dbias_ref[...] += dbias
            dalog_ref[...] += dalog
            dd_ref[...] += dD
            dnw_ref[...] += dnw

    rb = lambda b, c: b * nc + nc - 1 - c
    rowblk = lambda w: pl.BlockSpec((Q, w), lambda b, c: (rb(b, c), 0))
    vec = lambda w: pl.BlockSpec((1, w), lambda b, c: (0, 0))
    return _call(
        body, name="ssd_bwd", grid=(Bl, nc),
        in_specs=_ssd_in_specs(nc, rev=True) + [
            pl.BlockSpec((1, 1, N_PAIR, 128, 128), lambda b, c: (b, nc - 1 - c, 0, 0, 0)), rowblk(SSD_INNER)],
        out_specs=[rowblk(SSD_CONV_CH), rowblk(SSD_INNER), rowblk(128), vec(128), vec(128), vec(128), vec(SSD_INNER)],
        out_shape=[jax.ShapeDtypeStruct((M, SSD_CONV_CH), BF16), jax.ShapeDtypeStruct(dproj.shape, BF16),
                   jax.ShapeDtypeStruct((M, 128), F32), jax.ShapeDtypeStruct((1, 128), F32),
                   jax.ShapeDtypeStruct((1, 128), F32), jax.ShapeDtypeStruct((1, 128), F32),
                   jax.ShapeDtypeStruct((1, SSD_INNER), F32)],
        scratch=[pltpu.VMEM((N_PAIR, 128, 128), F32)], sem=("arbitrary", "arbitrary"),
        args=(xc, dtr, proj, bias_p, alog_p, d_p, nw, prev, dya), comm=comm, into=(dproj, 1))


NSUB = Q // HG_CHUNK
HG_HP = 8
EXP_CAP = 80.0


def _hg_setup(blk, q_ref, f_ref, hb_ref):
    row = lax.broadcasted_iota(jnp.int32, (Q, Q), 0)
    col = lax.broadcasted_iota(jnp.int32, (Q, Q), 1)
    same = (row // HG_CHUNK) == (col // HG_CHUNK)
    causal = jnp.logical_and(same, col <= row)
    lb = _sigmoid(hb_ref[0:1, :] - hb_ref[1:2, :])
    fl = f_ref[...].astype(F32)
    sg = _sigmoid(fl)
    fg = lb + (1.0 - lb) * sg
    k = (1.0 - lb) * (1.0 - sg)
    gl = jnp.log(fg)
    G = _dot01(causal, gl, NN, "a")
    T = _dot01(same, gl, NN, "a")
    qv = q_ref[...].astype(F32)
    sq = _sigmoid(qv)
    eG = jnp.exp(G)
    eGn = jnp.exp(jnp.minimum(-G, EXP_CAP))
    eTG = jnp.exp(T - G)
    qt = qv * sq * eG
    kt = k * eGn
    kh = k * eTG
    valid = jnp.logical_or(blk > 0, row[:, :1] >= PAD)
    return dict(row=row, col=col, same=same, causal=causal, lb=lb, sg=sg, fg=fg, k=k, T=T, qv=qv, sq=sq,
                eG=eG, eGn=eGn, eTG=eTG, qt=qt, kt=kt, kh=kh, valid=valid)


def _hg_specs(nb, rev=False):
    rb = (lambda h, b, t: b * nb + nb - 1 - t) if rev else (lambda h, b, t: b * nb + t)
    w = 128 * HG_HP
    blk = lambda off: pl.BlockSpec((Q, w), lambda h, b, t, off=off: (rb(h, b, t), off // HG_HP + h))
    return [blk(24), blk(32), blk(40), blk(48),
            pl.BlockSpec((2, w), lambda h, b, t: (0, h)), pl.BlockSpec((1, w), lambda h, b, t: (0, h))]


HEAD_LANES = tuple(slice(128 * hh, 128 * (hh + 1)) for hh in range(HG_HP))


def _per_head(fn, *arrs):
    return jnp.concatenate([jnp.broadcast_to(fn(*(a[:, ln] for a in arrs)), (arrs[0].shape[0], 128))
                            for ln in HEAD_LANES], axis=1)


def _hgrn_fwd(proj, hb, nw, Bl, nb, comm=None):
    M = proj.shape[0]

    def body(q_ref, f_ref, i_ref, g_ref, hb_ref, nw_ref, y_ref, o_ref, st_ref, S):
        blk = pl.program_id(2)

        @pl.when(blk == 0)
        def _():
            S[...] = jnp.zeros_like(S)

        s = _hg_setup(blk, q_ref, f_ref, hb_ref)
        v = i_ref[...]
        qt_b, kt_b, kh_b = s["qt"].astype(BF16), s["kt"].astype(BF16), s["kh"].astype(BF16)
        eT = jnp.exp(s["T"])
        att = [jnp.where(s["causal"], _dot(qt_b[:, ln], kt_b[:, ln], NT), 0.0).astype(BF16) for ln in HEAD_LANES]
        o_intra = [_dot(att[hh], v[:, ln], NN) for hh, ln in enumerate(HEAD_LANES)]
        for j in range(NSUB):
            sl = slice(HG_CHUNK * j, HG_CHUNK * (j + 1))
            for hh, ln in enumerate(HEAD_LANES):
                St = S[hh]
                st_ref[0, hh, 0, j] = St
                o_ref[sl, ln] = o_intra[hh][sl] + _dot(qt_b[sl, ln], St.astype(BF16), NT)
                S[hh] = St * eT[HG_CHUNK * j:HG_CHUNK * j + 1, ln] + _dot(v[sl, ln], kh_b[sl, ln], TN)
        o = o_ref[...]
        r = _per_head(lambda a: lax.rsqrt(jnp.mean(a * a, axis=-1, keepdims=True) + EPS), o)
        gv = g_ref[...].astype(F32)
        y_ref[...] = (o * r * nw_ref[...] * gv * _sigmoid(gv)).astype(y_ref.dtype)

    rowblk = pl.BlockSpec((Q, 128 * HG_HP), lambda h, b, t: (b * nb + t, h))
    return _call(
        body, name="hgrn_fwd", grid=(HG_HEADS // HG_HP, Bl, nb), in_specs=_hg_specs(nb),
        out_specs=[rowblk, rowblk,
                   pl.BlockSpec((1, HG_HP, 1, NSUB, 128, 128), lambda h, b, t: (b, h, t, 0, 0, 0))],
        out_shape=[jax.ShapeDtypeStruct((M, HG_WIDTH), BF16), jax.ShapeDtypeStruct((M, HG_WIDTH), F32),
                   jax.ShapeDtypeStruct((Bl, HG_HEADS, nb, NSUB, 128, 128), F32)],
        scratch=[pltpu.VMEM((HG_HP, 128, 128), F32)], sem=("parallel", "arbitrary", "arbitrary"),
        args=(proj, proj, proj, proj, hb, nw), comm=comm)


def _hgrn_bwd(proj, hb, nw, o_saved, st_saved, dyb, dproj, Bl, nb, comm=None):
    assert HG_HP == HG_HEADS

    def body(q_ref, f_ref, i_ref, g_ref, hb_ref, nw_ref, o_ref, st_ref, dy_ref,
             d_ref, dhb_ref, dnw_ref, dS, a_dqt, a_dv, a_dkh, a_dgl):
        b, t = pl.program_id(1), pl.program_id(2)

        @pl.when(t == 0)
        def _():
            dS[...] = jnp.zeros_like(dS)

        first_step = jnp.logical_and(b == 0, t == 0)
        s = _hg_setup(nb - 1 - t, q_ref, f_ref, hb_ref)
        v = i_ref[...]
        qt_b, kt_b, kh_b = s["qt"].astype(BF16), s["kt"].astype(BF16), s["kh"].astype(BF16)
        eT = jnp.exp(s["T"])
        att = [jnp.where(s["causal"], _dot(qt_b[:, ln], kt_b[:, ln], NT), 0.0).astype(BF16) for ln in HEAD_LANES]

        o = o_ref[...]
        r = _per_head(lambda a: lax.rsqrt(jnp.mean(a * a, axis=-1, keepdims=True) + EPS), o)
        xhat = o * r
        gv = g_ref[...].astype(F32)
        sgv = _sigmoid(gv)
        dyv = dy_ref[...].astype(F32)
        d_on = dyv * gv * sgv
        dg_out = dyv * xhat * nw_ref[...] * _dsilu(gv, sgv)
        gw = d_on * nw_ref[...]
        do = r * (gw - xhat * _per_head(lambda a, c: jnp.mean(a * c, axis=-1, keepdims=True), gw, xhat))
        dnw = jnp.sum(d_on * xhat, axis=0, keepdims=True)
        do_b = do.astype(BF16)

        datt = [jnp.where(s["causal"], _dot(do_b[:, ln], v[:, ln], NT), 0.0).astype(BF16) for ln in HEAD_LANES]
        dqt = jnp.concatenate([_dot(datt[hh], kt_b[:, ln], NN) for hh, ln in enumerate(HEAD_LANES)], axis=1)
        dkt = jnp.concatenate([_dot(datt[hh], qt_b[:, ln], TN) for hh, ln in enumerate(HEAD_LANES)], axis=1)
        dv = jnp.concatenate([_dot(att[hh], do_b[:, ln], TN) for hh, ln in enumerate(HEAD_LANES)], axis=1)
        last_row = (lax.broadcasted_iota(jnp.int32, (HG_CHUNK, 128), 0) == HG_CHUNK - 1)
        for j in reversed(range(NSUB)):
            sl = slice(HG_CHUNK * j, HG_CHUNK * (j + 1))
            for hh, ln in enumerate(HEAD_LANES):
                St = st_ref[0, hh, 0, j]
                dSt = dS[hh]
                St_b, dSt_b = St.astype(BF16), dSt.astype(BF16)
                eT_j = eT[HG_CHUNK * j:HG_CHUNK * j + 1, ln]
                dkh_j = _dot(v[sl, ln], dSt_b, NN)
                a_dqt[sl, ln] = _dot(do_b[sl, ln], St_b, NN)
                a_dv[sl, ln] = _dot(kh_b[sl, ln], dSt_b, NT)
                a_dkh[sl, ln] = dkh_j
                dlast = (jnp.sum(St * dSt, axis=0, keepdims=True) * eT_j
                         + jnp.sum(dkh_j * s["kh"][sl, ln], axis=0, keepdims=True))
                a_dgl[sl, ln] = jnp.where(last_row, dlast, 0.0)
                dS[hh] = dSt * eT_j + _dot(do_b[sl, ln], qt_b[sl, ln], TN)
        dqt = dqt + a_dqt[...]
        dv = dv + a_dv[...]
        dkh = a_dkh[...]
        dG = dqt * s["qt"] - dkt * s["kt"] - dkh * s["kh"] + a_dgl[...]
        rev_causal = jnp.logical_and(s["same"], s["col"] >= s["row"])
        dgl = _dot01(rev_causal, dG, NN, "a")
        dk = dkt * s["eGn"] + dkh * s["eTG"]
        dfg = dgl / s["fg"] - dk
        lb, sg = s["lb"], s["sg"]
        keep = s["valid"].astype(F32)
        d_ref[:, 0:w] = (dqt * s["eG"] * _dsilu(s["qv"], s["sq"]) * keep).astype(d_ref.dtype)
        d_ref[:, w:2 * w] = (dfg * (1.0 - lb) * sg * (1.0 - sg) * keep).astype(d_ref.dtype)
        d_ref[:, 2 * w:3 * w] = (dv * keep).astype(d_ref.dtype)
        d_ref[:, 3 * w:4 * w] = (dg_out * keep).astype(d_ref.dtype)
        dlb = jnp.sum(dfg * (1.0 - sg) * keep, axis=0, keepdims=True) * lb * (1.0 - lb)
        dhb = jnp.concatenate([dlb, -dlb], axis=0)

        @pl.when(first_step)
        def _():
            dhb_ref[...] = dhb
            dnw_ref[...] = dnw

        @pl.when(jnp.logical_not(first_step))
        def _():
            dhb_ref[...] += dhb
            dnw_ref[...] += dnw

    w = 128 * HG_HP
    rowblk = pl.BlockSpec((Q, w), lambda h, b, t: (b * nb + nb - 1 - t, h))
    return _call(
        body, name="hgrn_bwd", grid=(HG_HEADS // HG_HP, Bl, nb),
        in_specs=_hg_specs(nb, rev=True) + [
            rowblk, pl.BlockSpec((1, HG_HP, 1, NSUB, 128, 128), lambda h, b, t: (b, h, nb - 1 - t, 0, 0, 0)), rowblk],
        out_specs=[pl.BlockSpec((pl.Element(Q), pl.Element(4 * w)),
                                lambda h, b, t: (pl.multiple_of((b * nb + nb - 1 - t) * Q, Q), 3 * HG_WIDTH)),
                   pl.BlockSpec((2, w), lambda h, b, t: (0, h)), pl.BlockSpec((1, w), lambda h, b, t: (0, h))],
        out_shape=[jax.ShapeDtypeStruct(dproj.shape, BF16),
                   jax.ShapeDtypeStruct((2, HG_WIDTH), F32), jax.ShapeDtypeStruct((1, HG_WIDTH), F32)],
        scratch=[pltpu.VMEM((HG_HP, 128, 128), F32)] + [pltpu.VMEM((Q, w), F32)] * 4,
        sem=("parallel", "arbitrary", "arbitrary"),
        args=(proj, proj, proj, proj, hb, nw, o_saved, st_saved, dyb), comm=comm, into=(dproj, 0))


def _adamw(name, parts, w, m, v, comm=None):
    R, C = w.shape
    S = parts.shape[0]
    tr, tc = (_tile(R, (256, 176, 128, 64, 8)), C) if R % 8 == 0 else (R, 256)
    c1, c2 = 1.0 - ADAM_B1 ** ADAM_STEP, 1.0 - ADAM_B2 ** ADAM_STEP

    def body(p_ref, w_ref, m_ref, v_ref, g_ref, d_ref, nm_ref, nv_ref):
        g = p_ref[0].astype(F32)
        for s in range(1, S):
            g = g + p_ref[s].astype(F32)
        nm = ADAM_B1 * m_ref[...] + (1.0 - ADAM_B1) * g
        nv = ADAM_B2 * v_ref[...] + (1.0 - ADAM_B2) * (g * g)
        g_ref[...] = g
        nm_ref[...] = nm
        nv_ref[...] = nv
        d_ref[...] = -ADAM_LR * ((nm / c1) / (jnp.sqrt(nv / c2) + ADAM_EPS) + ADAM_WD * w_ref[...])

    blk = pl.BlockSpec((tr, tc), lambda i, j: (i, j))
    return _call(
        body, name=name, grid=(R // tr, C // tc),
        in_specs=[pl.BlockSpec((S, tr, tc), lambda i, j: (0, i, j)), blk, blk, blk], out_specs=[blk] * 4,
        out_shape=[jax.ShapeDtypeStruct((R, C), F32)] * 4, scratch=[], sem=("parallel", "parallel"),
        args=(parts, w, m, v), comm=comm)


def _pair_sum(name, by_core, arrived):
    _, J, R, C = by_core.shape
    tc = _tile(C, (512, 256, 128))

    def body(c_ref, a_ref, b_ref, o_ref):
        o_ref[...] = (a_ref[0].astype(F32) + b_ref[...].astype(F32)).astype(o_ref.dtype)

    blk = pl.BlockSpec((1, R, tc), lambda j, k, c_ref: (j, 0, k))
    return pl.pallas_call(
        body, name=name,
        grid_spec=pltpu.PrefetchScalarGridSpec(
            num_scalar_prefetch=1, grid=(J, C // tc),
            in_specs=[pl.BlockSpec((1, 1, R, tc), lambda j, k, c_ref: (c_ref[0], j, 0, k)), blk], out_specs=blk),
        out_shape=jax.ShapeDtypeStruct(arrived.shape, arrived.dtype), compiler_params=_params(("parallel", "parallel")),
    )(lax.axis_index("c").astype(jnp.int32).reshape(1), by_core, arrived)


def _sum_parts(name, parts):
    S, R, C = parts.shape

    def body(p_ref, o_ref):
        g = p_ref[0]
        for s in range(1, S):
            g = g + p_ref[s]
        o_ref[...] = g

    return pl.pallas_call(
        body, name=name, out_shape=jax.ShapeDtypeStruct((R, C), F32),
        in_specs=[pl.BlockSpec(memory_space=pltpu.VMEM)], out_specs=pl.BlockSpec(memory_space=pltpu.VMEM),
    )(parts)


def _heads_to_lanes(p):
    return jnp.pad(p, [(0, 0)] * (p.ndim - 1) + [(0, 128 - SSD_HEADS)])


def _lanes_to_heads(p):
    return p[..., :SSD_HEADS]


def _pack_rows(arrs):
    flat = jnp.concatenate([a.reshape(-1).astype(F32) for a in arrs])
    return jnp.pad(flat, (0, (-flat.shape[0]) % (8 * D_MODEL))).reshape(-1, D_MODEL)


def _unpack_rows(packed, like):
    flat, outs, at = packed.reshape(-1), [], 0
    for a in like:
        outs.append(flat[at:at + a.size].reshape(a.shape))
        at += a.size
    return outs


def _cols(gth):
    return jnp.transpose(gth, (1, 0, 2)).reshape(gth.shape[1], -1)


def _rows(gth):
    return gth.reshape(-1, gth.shape[2])


def _to_rows(g):
    return g.reshape(N_DEV, -1, g.shape[1]).astype(BF16)


def _by_core(g):
    return jnp.transpose(g.reshape(N_DEV // 2, 2, -1, g.shape[1]), (1, 0, 2, 3)).astype(BF16)


DT_ROW = 3072


def _chip_sums(tag, by_core, swap_in=None):
    arrived = swap_in(by_core) if swap_in else _exchange(tag + "_swap", "swap", by_core)
    return [_pair_sum(f"{tag}_chipsum{i}", m, a) for i, (m, a) in enumerate(zip(by_core, arrived))]


def _ffn_fwd_gu(tag, n, w_gu_t, comm=None):
    M = n.shape[0]
    F = w_gu_t.shape[0] // 2
    tm = _tile(M, (544, 256))
    outs = _fused_matmul(
        tag + "_gu", M, F, D_MODEL,
        [dict(a=n, b=w_gu_t, trans_b=True, acc=0, resident=True),
         dict(a=n, b=w_gu_t, trans_b=True, bn_off=1, acc=1, resident=True)], [],
        lambda accs, ex: (accs[0], accs[1], accs[0] * _sigmoid(accs[0]) * accs[1]),
        [BF16, BF16, BF16], 2, tm, F, D_MODEL, outer="i", comm=comm, sub=256)
    return (n, *outs[:3]), outs[3:]


def _rmsnorm_tile(x, w):
    return x * lax.rsqrt(jnp.mean(x * x, axis=-1, keepdims=True) + EPS) * w


def _ffn_fwd_down(tag, h, a, w_down, next_norm=None, comm=None):
    M = h.shape[0]
    F = w_down.shape[0]
    tm = _tile(M, (1088, 544, 256))
    if next_norm is None:
        (h_out,) = _fused_matmul(
            tag + "_down", M, D_MODEL, F, [dict(a=a, b=w_down, acc=0)], [(h, 0)],
            lambda accs, ex: (ex[0] + 0.5 * accs[0],), [F32], 1, tm, D_MODEL, F, outer="j", sub=256)
        return h_out

    def with_norm(accs, ex):
        h_new = ex[0] + 0.5 * accs[0]
        return h_new, _rmsnorm_tile(h_new, ex[1])

    return _fused_matmul(tag + "_down", M, D_MODEL, F, [dict(a=a, b=w_down, acc=0, resident=True)], [(h, 0)], with_norm,
                         [F32, BF16], 1, tm, D_MODEL, F, outer="j", vecs=[next_norm], comm=comm)


def _ffn_bwd(tag, dh, dh_b, h, norm_w, w_gu_t, w_down, saved, scatter=False):
    n, g, u, a = saved
    M = h.shape[0]
    F = w_down.shape[0]
    tm = _tile(M, (544, 256))
    tn = _tile(F, (1408, 704, 256))

    def swiglu_bwd(accs, ex):
        da, gv, uv = 0.5 * accs[0], ex[0].astype(F32), ex[1].astype(F32)
        s = _sigmoid(gv)
        return da * uv * _dsilu(gv, s), da * gv * s

    (dgu,) = _fused_matmul(
        tag + "_dact", M, F, D_MODEL, [dict(a=dh_b, b=w_down, trans_b=True, acc=0, resident=True)], [(g, 0), (u, 0)],
        swiglu_bwd, [BF16, BF16], 1, tm, F, D_MODEL, outer="i", stack=True, sub=256)
    tr = _tile(M, (2176, 256))
    (dw_down,) = _matmul_tn(tag + "_dwd", a, dh_b, tn, D_MODEL, tr, scale=0.5)
    dw_gu_t, *p_down = _matmul_tn(tag + "_dwgu", dgu, n, tn, D_MODEL, tr,
                                  comm=("scatter", [_to_rows(dw_down)]) if scatter else None)
    comm = None
    if scatter:
        comm = ("chips", _chip_sums(tag + "_wgu", [_by_core(dw_gu_t)]))
    def norm_bwd(accs, ex):
        dh_prev, dw = _rmsnorm_bwd_tile(accs[0], ex[0], ex[2], ex[1])
        return dh_prev, dh_prev, dw

    dh_prev, dh_prev_b, dnorm, *p_gu = _fused_matmul(
        tag + "_dn", M, D_MODEL, 2 * F,
        [dict(a=dgu, b=w_gu_t, acc=0, resident=True)], [(h, 0), (dh, 0)],
        norm_bwd, [F32, BF16], 1, tm, D_MODEL, 2 * F, outer="i", comm=comm, vecs=[norm_w], row_sums=1)
    return (dh_prev, dh_prev_b, dnorm, *((p_gu[0], p_down[0]) if scatter else (dw_gu_t, dw_down)))


def kernel(x, meta_tokens, ffn1_norm, ffn1_w_gu, ffn1_w_down, mix_norm, w_in, ssd_conv_w, ssd_conv_b, ssd_dt_bias, ssd_a_log, ssd_d, ssd_norm, hg_lower_bound, hg_norm, w_branch_a, w_branch_b, w_out, ffn2_norm, ffn2_w_gu, ffn2_w_down, final_norm, loss_target, m_meta_tokens, m_ffn1_norm, m_ffn1_w_gu, m_ffn1_w_down, m_mix_norm, m_w_in, m_ssd_conv_w, m_ssd_conv_b, m_ssd_dt_bias, m_ssd_a_log, m_ssd_d, m_ssd_norm, m_hg_lower_bound, m_hg_norm, m_w_branch_a, m_w_branch_b, m_w_out, m_ffn2_norm, m_ffn2_w_gu, m_ffn2_w_down, m_final_norm, v_meta_tokens, v_ffn1_norm, v_ffn1_w_gu, v_ffn1_w_down, v_mix_norm, v_w_in, v_ssd_conv_w, v_ssd_conv_b, v_ssd_dt_bias, v_ssd_a_log, v_ssd_d, v_ssd_norm, v_hg_lower_bound, v_hg_norm, v_w_branch_a, v_w_branch_b, v_w_out, v_ffn2_norm, v_ffn2_w_gu, v_ffn2_w_down, v_final_norm):
    Bl, S, D = x.shape
    T = PAD + N_META + S
    nc = T // Q
    M = Bl * T
    me = 4 * lax.axis_index("x") + 2 * lax.axis_index("y") + lax.axis_index("c")

    bf = lambda a: a[0].astype(BF16)
    bft = lambda a: a[0].T.astype(BF16)
    g_meta, g_conv_w = _exchange("gather_small", "gather", [meta_tokens, ssd_conv_w[0]])
    meta_full, conv_w_full = _cols(g_meta), _cols(g_conv_w)
    bias_p, alog_p, d_p = _heads_to_lanes(ssd_dt_bias), _heads_to_lanes(ssd_a_log), _heads_to_lanes(ssd_d)
    final_w = final_norm.reshape(1, D)

    h0, n1, g_wgu1 = _embed_norm(x, meta_full, ffn1_norm, comm=("gather", [bft(ffn1_w_gu)]))
    wgu1 = _rows(g_wgu1)
    tm = _tile(M, (1088, 544, 256))
    win_shard = bft(w_in)
    cut = (win_shard.shape[0] // 32) * 16
    ffn1_saved, (g_wd1, g_win_a) = _ffn_fwd_gu("ffn1", n1, wgu1, comm=("gather", [bf(ffn1_w_down), win_shard[:cut]]))
    wd1 = _rows(g_wd1)
    h1, un, g_win_b = _ffn_fwd_down("ffn1", h0, ffn1_saved[3], wd1, next_norm=mix_norm,
                                    comm=("gather", [win_shard[cut:]]))
    win_t = _rows(jnp.concatenate([g_win_a, g_win_b], axis=1))
    win_dt = jnp.pad(win_t[DT_ROW:DT_ROW + SSD_HEADS], ((0, 128 - SSD_HEADS), (0, 0)))
    plain = lambda accs, ex: (accs[0],)
    proj, g_wa, g_wb, g_wo = _fused_matmul(
        "in_proj", M, N_MAIN, D, [dict(a=un, b=win_t, trans_b=True, acc=0, b_shift=(DT_ROW // 1536, SSD_HEADS))], [],
        plain, [BF16], 1, tm, 1536, D,
        outer="j", comm=("gather", [bf(w_branch_a), bf(w_branch_b), bf(w_out)], "early"), sub=512)
    wa, wb, wo = _rows(g_wa), _rows(g_wb), _rows(g_wo)
    (dtr,) = _fused_matmul("in_proj_dt", M, 128, D, [dict(a=un, b=win_dt, trans_b=True, acc=0)], [], plain, [F32], 1,
                           tm, 128, D, outer="j")
    xc = _conv_fwd(proj, conv_w_full, ssd_conv_b, Bl, T)
    ya, ssd_prev = _ssd_fwd(xc, dtr, proj, bias_p, alog_p, d_p, ssd_norm, Bl, nc)
    yb, hg_o, hg_st, g_wgu2, g_wd2 = _hgrn_fwd(proj, hg_lower_bound, hg_norm, Bl, nc,
                                               comm=("gather", [bft(ffn2_w_gu), bf(ffn2_w_down)], "early"))
    wgu2, wd2 = _rows(g_wgu2), _rows(g_wd2)

    def branch_fwd(accs, ex):
        pa, pb = accs
        return pa, pb, _sigmoid(ex[0].astype(F32)) * pa + _sigmoid(ex[1].astype(F32)) * pb

    pa, pb, merged = _fused_matmul(
        "branches", M, D, D, [dict(a=ya, b=wa, acc=0), dict(a=yb, b=wb, acc=1)], [(proj, 7), (proj, 8)],
        branch_fwd, [BF16, BF16, BF16], 2, tm, D, D, outer="j")
    def out_with_norm(accs, ex):
        h_new = ex[0] + accs[0]
        return h_new, _rmsnorm_tile(h_new, ex[1])

    h2, n2 = _fused_matmul("out_proj", M, D, D, [dict(a=merged, b=wo, acc=0)], [(h1, 0)], out_with_norm,
                           [F32, BF16], 1, tm, D, D, outer="j", vecs=[ffn2_norm])
    ffn2_saved, _ = _ffn_fwd_gu("ffn2", n2, wgu2)
    h3 = _ffn_fwd_down("ffn2", h2, ffn2_saved[3], wd2)

    dh3, dh3_b, d_final, loss_part = _loss_head(h3, final_w, loss_target, Bl, nc)
    dh2, dh2_b, d_ffn2_norm, d_wgu2, d_wd2 = _ffn_bwd("ffn2", dh3, dh3_b, h2, ffn2_norm, wgu2, wd2, ffn2_saved)

    def branch_bwd(accs, ex):
        dm = accs[0]
        ga, gb, pav, pbv = (e.astype(F32) for e in ex)
        sa, sb = _sigmoid(ga), _sigmoid(gb)
        return (dm * sa, dm * sb,
                jnp.concatenate([dm * pav * sa * (1.0 - sa), dm * pbv * sb * (1.0 - sb)], axis=1))

    d_merged_outs = []

    def d_merged_with_swap(theirs):
        d_merged_outs.extend(_fused_matmul(
            "d_merged", M, D, D, [dict(a=dh2_b, b=wo, trans_b=True, acc=0)], [(proj, 7), (proj, 8), (pa, 0), (pb, 0)],
            branch_bwd, [BF16] * 2, 1, tm, D, D, outer="j", comm=("swap", theirs),
            wide=dict(width=2 * D, col=7 * D, total=N_MAIN, dtype=BF16)))
        return d_merged_outs[3:]

    s_ffn2 = _chip_sums("ffn2", [_by_core(d_wgu2), _by_core(d_wd2)], swap_in=d_merged_with_swap)
    dpa, dpb, dproj = d_merged_outs[:3]
    (d_wo,) = _matmul_tn("d_w_out", merged, dh2_b, 512, D, M)
    (d_wa,) = _matmul_tn("d_w_a", ya, dpa, 512, D, M)
    (d_wb,) = _matmul_tn("d_w_b", yb, dpb, 512, D, M)
    dya, dyb = _fused_matmul(
        "d_branches", M, D, D, [dict(a=dpa, b=wa, trans_b=True, acc=0), dict(a=dpb, b=wb, trans_b=True, acc=1)], [],
        lambda accs, ex: (accs[0], accs[1]), [BF16, BF16], 2, tm, D, D, outer="j")
    *ssd_grads, p_wgu2, p_wd2 = _ssd_bwd(xc, dtr, proj, bias_p, alog_p, d_p, ssd_norm, ssd_prev, dya, dproj, Bl, nc,
                                         comm=("chips", s_ffn2))
    dxc, dproj, ddtr, d_bias_p, d_alog_p, d_d_p, d_ssd_norm = ssd_grads
    dproj, d_conv_w, d_conv_b = _conv_bwd(proj, conv_w_full, ssd_conv_b, dxc, dproj, Bl, T)
    dproj, d_hb, d_hg_norm, p_wa, p_wb, p_wo = _hgrn_bwd(
        proj, hg_lower_bound, hg_norm, hg_o, hg_st, dyb, dproj, Bl, nc,
        comm=("scatter", [_to_rows(d_wa), _to_rows(d_wb), _to_rows(d_wo)]))
    ddtr_b = ddtr.astype(BF16)
    (d_win_t,) = _matmul_tn("d_w_in", dproj, un, 768, D, M, out_skip=(DT_ROW, SSD_HEADS))
    (d_win_dt,) = _matmul_tn("d_w_in_dt", ddtr_b, un, 128, D, M)
    d_win_t = lax.dynamic_update_slice(d_win_t, d_win_dt[:SSD_HEADS], (DT_ROW, 0))
    d_un_dt_outs = []

    def d_un_dt_with_swap(theirs):
        d_un_dt_outs.extend(_fused_matmul("d_un_dt", M, D, 128, [dict(a=ddtr_b, b=win_dt, acc=0)], [], plain, [F32], 1,
                                          tm, D, 128, outer="j", comm=("swap", theirs)))
        return d_un_dt_outs[1:]

    s_win = _chip_sums("w_in", [_by_core(d_win_t)], swap_in=d_un_dt_with_swap)
    def mix_norm_bwd(accs, ex):
        dh, dw = _rmsnorm_bwd_tile(accs[0] + ex[0], ex[1], ex[3], ex[2])
        return dh, dh, dw

    dh1, dh1_b, d_mix_norm, p_win = _fused_matmul(
        "d_un", M, D, N_MAIN, [dict(a=dproj, b=win_t, acc=0, b_shift=(DT_ROW // 3072, SSD_HEADS))],
        [(d_un_dt_outs[0], 0), (h1, 0), (dh2, 0)],
        mix_norm_bwd, [F32, BF16], 1, _tile(M, (544, 256)), D, 3072, outer="i", comm=("chips", s_win),
        vecs=[mix_norm], row_sums=1)
    dh0, _, d_ffn1_norm, p_wgu1, p_wd1 = _ffn_bwd("ffn1", dh1, dh1_b, h0, ffn1_norm, wgu1, wd1, ffn1_saved, scatter=True)

    dh0 = dh0.reshape(Bl, T, D)
    grad_x = dh0[:, PAD + N_META:]
    d_meta = dh0[:, PAD:PAD + N_META]

    small_grads = [d_ffn1_norm, d_mix_norm, d_conv_b, _lanes_to_heads(d_bias_p), _lanes_to_heads(d_alog_p),
                   _lanes_to_heads(d_d_p), d_ssd_norm, d_hb, d_hg_norm, d_ffn2_norm, d_final.reshape(D), d_conv_w]
    small_like = small_grads + [d_meta[b] for b in range(Bl)] + [loss_part[0, 0:1]]
    small_packed = _pack_rows(small_like)
    parts = [p_wgu1, p_wd1, p_win, p_wa, p_wb, p_wo, p_wgu2, p_wd2]

    names = ["meta_tokens", "ffn1_norm", "ffn1_w_gu", "ffn1_w_down", "mix_norm", "w_in", "ssd_conv_w", "ssd_conv_b",
             "ssd_dt_bias", "ssd_a_log", "ssd_d", "ssd_norm", "hg_lower_bound", "hg_norm", "w_branch_a", "w_branch_b",
             "w_out", "ffn2_norm", "ffn2_w_gu", "ffn2_w_down", "final_norm"]
    W = dict(meta_tokens=meta_tokens, ffn1_norm=ffn1_norm, ffn1_w_gu=ffn1_w_gu, ffn1_w_down=ffn1_w_down, mix_norm=mix_norm,
             w_in=w_in, ssd_conv_w=ssd_conv_w, ssd_conv_b=ssd_conv_b, ssd_dt_bias=ssd_dt_bias, ssd_a_log=ssd_a_log,
             ssd_d=ssd_d, ssd_norm=ssd_norm, hg_lower_bound=hg_lower_bound, hg_norm=hg_norm, w_branch_a=w_branch_a,
             w_branch_b=w_branch_b, w_out=w_out, ffn2_norm=ffn2_norm, ffn2_w_gu=ffn2_w_gu, ffn2_w_down=ffn2_w_down,
             final_norm=final_norm)
    Mo = dict(meta_tokens=m_meta_tokens, ffn1_norm=m_ffn1_norm, ffn1_w_gu=m_ffn1_w_gu, ffn1_w_down=m_ffn1_w_down,
              mix_norm=m_mix_norm, w_in=m_w_in, ssd_conv_w=m_ssd_conv_w, ssd_conv_b=m_ssd_conv_b, ssd_dt_bias=m_ssd_dt_bias,
              ssd_a_log=m_ssd_a_log, ssd_d=m_ssd_d, ssd_norm=m_ssd_norm, hg_lower_bound=m_hg_lower_bound, hg_norm=m_hg_norm,
              w_branch_a=m_w_branch_a, w_branch_b=m_w_branch_b, w_out=m_w_out, ffn2_norm=m_ffn2_norm, ffn2_w_gu=m_ffn2_w_gu,
              ffn2_w_down=m_ffn2_w_down, final_norm=m_final_norm)
    Vo = dict(meta_tokens=v_meta_tokens, ffn1_norm=v_ffn1_norm, ffn1_w_gu=v_ffn1_w_gu, ffn1_w_down=v_ffn1_w_down,
              mix_norm=v_mix_norm, w_in=v_w_in, ssd_conv_w=v_ssd_conv_w, ssd_conv_b=v_ssd_conv_b, ssd_dt_bias=v_ssd_dt_bias,
              ssd_a_log=v_ssd_a_log, ssd_d=v_ssd_d, ssd_norm=v_ssd_norm, hg_lower_bound=v_hg_lower_bound, hg_norm=v_hg_norm,
              w_branch_a=v_w_branch_a, w_branch_b=v_w_branch_b, w_out=v_w_out, ffn2_norm=v_ffn2_norm, ffn2_w_gu=v_ffn2_w_gu,
              ffn2_w_down=v_ffn2_w_down, final_norm=v_final_norm)
    grads, deltas, new_m, new_v = {}, {}, {}, {}
    big_names = ["ffn1_w_gu", "ffn1_w_down", "w_in", "w_branch_a", "w_branch_b", "w_out", "ffn2_w_gu", "ffn2_w_down"]
    transposed = ("ffn1_w_gu", "ffn2_w_gu", "w_in")
    small_all = None
    for nm, part in zip(big_names, parts):
        view = (lambda a: a[0].T) if nm in transposed else (lambda a: a[0])
        back = (lambda o: o.T[None]) if nm in transposed else (lambda o: o[None])
        outs = _adamw("adamw_" + nm, part, view(W[nm]), view(Mo[nm]), view(Vo[nm]),
                      comm=("gather", [small_packed]) if small_all is None else None)
        if small_all is None:
            small_all = outs[4]
        grads[nm], deltas[nm], new_m[nm], new_v[nm] = (back(o) for o in outs[:4])
    unpacked = _unpack_rows(_sum_parts("sum_small_grads", small_all), small_like)
    g_small = unpacked[:len(small_grads)]
    g_meta_full = unpacked[len(small_grads)]
    for b in range(1, Bl):
        g_meta_full = g_meta_full + unpacked[len(small_grads) + b]
    g_meta = lax.dynamic_slice_in_dim(g_meta_full, me * (D // N_DEV), D // N_DEV, axis=1)
    g_conv_w = lax.dynamic_slice_in_dim(g_small[11], me * (SSD_CONV_CH // N_DEV), SSD_CONV_CH // N_DEV, axis=1)
    loss = unpacked[-1].reshape(())
    small_names = ["ffn1_norm", "mix_norm", "ssd_conv_b", "ssd_dt_bias", "ssd_a_log", "ssd_d", "ssd_norm", "hg_lower_bound",
                   "hg_norm", "ffn2_norm", "final_norm", "ssd_conv_w", "meta_tokens"]
    small_g = g_small[:11] + [g_conv_w.reshape(ssd_conv_w.shape), g_meta]
    pk = lambda d: _pack_rows([d[nm] for nm in small_names])
    outs = _adamw("adamw_small", _pack_rows(small_g)[None], pk(W), pk(Mo), pk(Vo))
    like = [W[nm] for nm in small_names]
    for dst, o in zip((grads, deltas, new_m, new_v), outs):
        for nm, val in zip(small_names, _unpack_rows(o, like)):
            dst[nm] = val

    return (loss, grad_x, *[grads[nm] for nm in names], *[deltas[nm] for nm in names],
            *[new_m[nm] for nm in names], *[new_v[nm] for nm in names])
```

```python
import functools

import jax
import jax.numpy as jnp
from jax import lax
from jax.experimental import pallas as pl
from jax.experimental.pallas import tpu as pltpu

F32, BF16 = jnp.float32, jnp.bfloat16
NN, NT, TN = ((1,), (0,)), ((1,), (1,)), ((0,), (0,))
MESH_AXES = ("x", "y", "c")
N_DEV = 8

D_MODEL = 1024
N_META = 16
EPS = 1e-6
SSD_HEADS, SSD_HEAD_DIM, SSD_GROUPS, SSD_STATE, SSD_CONV, Q = 16, 64, 4, 128, 4, 128
SSD_INNER = SSD_HEADS * SSD_HEAD_DIM
SSD_CONV_CH = SSD_INNER + 2 * SSD_GROUPS * SSD_STATE
HG_WIDTH, HG_HEADS, HG_CHUNK = 1024, 8, 16
PAD = Q - N_META
N_MAIN = 9 * 1024
ADAM_LR, ADAM_B1, ADAM_B2, ADAM_EPS, ADAM_WD, ADAM_STEP = 0.001, 0.9, 0.999, 1e-08, 0.01, 10
VMEM_LIMIT = 52 * 1024 * 1024


def _dot(a, b, dims):
    return lax.dot_general(a, b, (dims, ((), ())), preferred_element_type=F32)


def _dot01(a, b, dims, sel):
    x = b if sel == "a" else a
    hi = x.astype(BF16)
    r1 = x - hi.astype(F32)
    mid = r1.astype(BF16)
    lo = (r1 - mid.astype(F32)).astype(BF16)
    s = (a if sel == "a" else b).astype(BF16)
    parts = [_dot(s, p, dims) if sel == "a" else _dot(p, s, dims) for p in (hi, mid, lo)]
    return parts[0] + parts[1] + parts[2]


def _sigmoid(x):
    return 1.0 / (1.0 + jnp.exp(-x))


def _dsilu(x, s):
    return s * (1.0 + x * (1.0 - s))


def _softplus(x):
    e = jnp.exp(-jnp.abs(x))
    u = 1.0 + e
    log1p_e = jnp.where(u == 1.0, e, jnp.log(u) * e / (u - 1.0))
    return jnp.maximum(x, 0.0) + log1p_e


def _params(sem):
    return pltpu.CompilerParams(dimension_semantics=sem, vmem_limit_bytes=VMEM_LIMIT)


def _tile(n, prefs):
    for p in prefs:
        if n % p == 0:
            return p
    return n


CHIP_FLIPS = ((1, 0), (0, 1), (1, 1))
N_PEER = N_DEV - 1


def _comm_gather(srcs, outs, send_sems, recv_sems, local_sems):
    n = len(srcs)
    x, y, c = (lax.axis_index(a) for a in MESH_AXES)
    dev = lambda px, py, pc: 4 * px + 2 * py + pc
    me, sib = dev(x, y, c), (x, y, 1 - c)
    nbr_x, nbr_y, diag = (1 - x, y), (x, 1 - y), (1 - x, 1 - y)
    via = (x ^ c, y ^ (1 - c), c)
    sent_on = dev(x ^ (1 - c), y ^ c, c)

    def rc(w, k, slot, to, src=None):
        return pltpu.make_async_remote_copy(
            src_ref=outs[w].at[slot] if src is None else src, dst_ref=outs[w].at[slot],
            send_sem=send_sems.at[w, k], recv_sem=recv_sems.at[w, k], device_id=to, device_id_type=pl.DeviceIdType.MESH)

    def local(w):
        return pltpu.make_async_copy(srcs[w], outs[w].at[me], local_sems.at[w])

    def start():
        for w in range(n):
            local(w).start()
            rc(w, 0, me, sib, src=srcs[w]).start()
            rc(w, 1, me, (*nbr_x, c), src=srcs[w]).start()
            rc(w, 2, me, (*nbr_y, c), src=srcs[w]).start()

    def pass_on():
        for w in range(n):
            rc(w, 1, dev(*nbr_x, c), sib).wait_recv()
            rc(w, 2, dev(*nbr_y, c), sib).wait_recv()
            rc(w, 3, sent_on, via).start()
            rc(w, 4, dev(*nbr_x, c), sib).start()
            rc(w, 5, dev(*nbr_y, c), sib).start()

    def pass_on_diagonal():
        for w in range(n):
            rc(w, 3, dev(*diag, c), sib).wait_recv()
            rc(w, 6, dev(*diag, c), sib).start()

    def finish():
        for w in range(n):
            rc(w, 0, dev(x, y, 1 - c), sib).wait_recv()
            for k, chip in ((4, nbr_x), (5, nbr_y), (6, diag)):
                rc(w, k, dev(*chip, 1 - c), sib).wait_recv()
            for k in range(N_PEER):
                rc(w, k, me, sib, src=srcs[w]).wait_send()
            local(w).wait()

    return start, (pass_on, pass_on_diagonal), finish


def _comm_scatter(srcs, outs, send_sems, recv_sems, local_sems):
    n = len(srcs)
    x, y, c = (lax.axis_index(a) for a in MESH_AXES)
    me = 4 * x + 2 * y + c

    def copies():
        out = []
        for w in range(n):
            out.append(pltpu.make_async_copy(srcs[w].at[me], outs[w].at[me], local_sems.at[w]))
            for k in range(1, N_DEV):
                px, py, pc = x ^ (k >> 2), y ^ ((k >> 1) & 1), c ^ (k & 1)
                out.append(pltpu.make_async_remote_copy(
                    src_ref=srcs[w].at[4 * px + 2 * py + pc], dst_ref=outs[w].at[me],
                    send_sem=send_sems.at[w, k - 1], recv_sem=recv_sems.at[w, k - 1],
                    device_id=(px, py, pc), device_id_type=pl.DeviceIdType.MESH))
        return out

    def start():
        for cp in copies():
            cp.start()

    def finish():
        for cp in copies():
            cp.wait()

    return start, None, finish


def _comm_swap(srcs, outs, send_sems, recv_sems, local_sems):
    x, y, c = (lax.axis_index(a) for a in MESH_AXES)

    def copies():
        return [pltpu.make_async_remote_copy(
            src_ref=srcs[w].at[1 - c], dst_ref=outs[w], send_sem=send_sems.at[w, 0], recv_sem=recv_sems.at[w, 0],
            device_id=(x, y, 1 - c), device_id_type=pl.DeviceIdType.MESH) for w in range(len(srcs))]

    def start():
        for cp in copies():
            cp.start()

    def finish():
        for cp in copies():
            cp.wait()

    return start, None, finish


def _comm_chips(srcs, outs, send_sems, recv_sems, local_sems):
    n = len(srcs)
    x, y, c = (lax.axis_index(a) for a in MESH_AXES)
    mine = 2 * x + y

    def copies():
        out = []
        for w in range(n):
            out.append(pltpu.make_async_copy(srcs[w].at[mine], outs[w].at[mine], local_sems.at[w]))
            for j, (fx, fy) in enumerate(CHIP_FLIPS):
                px, py = x ^ fx, y ^ fy
                out.append(pltpu.make_async_remote_copy(
                    src_ref=srcs[w].at[2 * px + py], dst_ref=outs[w].at[mine],
                    send_sem=send_sems.at[w, j], recv_sem=recv_sems.at[w, j],
                    device_id=(px, py, c), device_id_type=pl.DeviceIdType.MESH))
        return out

    def start():
        for cp in copies():
            cp.start()

    def finish():
        for cp in copies():
            cp.wait()

    return start, None, finish


def _comm_parts(comm):
    kind, arrays = comm[:2]
    n = len(arrays)
    lead = {"gather": lambda a: (N_DEV,) + a.shape, "scatter": lambda a: (N_DEV,) + a.shape[1:],
            "swap": lambda a: a.shape[1:], "chips": lambda a: a.shape}[kind]
    shapes = [jax.ShapeDtypeStruct(lead(a), a.dtype) for a in arrays]
    sems = [pltpu.SemaphoreType.DMA((n, N_PEER)), pltpu.SemaphoreType.DMA((n, N_PEER)), pltpu.SemaphoreType.DMA((n,))]
    make = {"gather": _comm_gather, "scatter": _comm_scatter, "swap": _comm_swap, "chips": _comm_chips}[kind]
    return n, shapes, sems, make


def _exchange(name, kind, arrays):
    n, shapes, sems, make = _comm_parts((kind, arrays))

    def body(*refs):
        start, middle, finish = make(refs[:n], refs[n:2 * n], *refs[2 * n:])
        start()
        for stage in middle or ():
            stage()
        finish()

    any_spec = pl.BlockSpec(memory_space=pl.ANY)
    return pl.pallas_call(
        body, name=name, in_specs=[any_spec] * n, out_specs=[any_spec] * n, out_shape=shapes, scratch_shapes=sems,
        compiler_params=pltpu.CompilerParams(has_side_effects=True),
    )(*arrays)


def _call(body, *, name, grid, in_specs, out_specs, out_shape, scratch, sem, args, comm=None, into=None):
    any_spec = pl.BlockSpec(memory_space=pl.ANY)
    in_specs, args, aliases, n_body_in = list(in_specs), list(args), {}, len(in_specs)
    if into is not None:
        in_specs.append(any_spec)
        args.append(into[0])
        aliases = {n_body_in: into[1]}
    n_in, n_out, n_scr = len(in_specs), len(out_specs), len(scratch)
    if comm is None:
        def plain(*refs):
            body(*refs[:n_body_in], *refs[n_in:])

        return pl.pallas_call(plain, name=name, grid=grid, in_specs=in_specs, out_specs=out_specs, out_shape=out_shape,
                              scratch_shapes=scratch, input_output_aliases=aliases, compiler_params=_params(sem))(*args)
    n, shapes, sems, make = _comm_parts(comm)

    def carrier(*refs):
        ins, csrc = refs[:n_body_in], refs[n_in:n_in + n]
        outs, cout = refs[n_in + n:n_in + n + n_out], refs[n_in + n + n_out:n_in + 2 * n + n_out]
        rest = refs[n_in + 2 * n + n_out:]
        start, middle, finish = make(csrc, cout, *rest[n_scr:])
        ids = [pl.program_id(a) for a in range(len(grid))]
        step = functools.reduce(lambda acc, ig: acc * ig[1] + ig[0], zip(ids, grid), 0)
        n_steps = functools.reduce(lambda a, b: a * b, grid, 1)
        pl.when(step == 0)(start)
        body(*ins, *outs, *rest[:n_scr])
        if middle:
            pl.when(step == max(0, (3 * n_steps) // 4 - 1))(middle[0])
            pl.when(step == n_steps - 1)(middle[1])
        pl.when(step == n_steps - 1)(finish)

    return pl.pallas_call(
        carrier, name=name, grid=grid, in_specs=in_specs + [any_spec] * n,
        out_specs=list(out_specs) + [any_spec] * n, out_shape=list(out_shape) + shapes,
        scratch_shapes=list(scratch) + sems, input_output_aliases=aliases,
        compiler_params=pltpu.CompilerParams(dimension_semantics=("arbitrary",) * len(grid),
                                             vmem_limit_bytes=VMEM_LIMIT, has_side_effects=True),
    )(*args, *comm[1])


def _fused_matmul(name, M, N, K, pairs, extras, epilogue, out_dtypes, n_acc, tm, tn, tk, outer="i", comm=None,
                  stack=False, vecs=(), row_sums=0, wide=None, sub=None):
    nk = K // tk
    n_pairs, n_ex, n_out = len(pairs), len(extras), len(out_dtypes)
    assert not row_sums or (outer == "i" and N == tn)

    def ij(g0, g1):
        return (g0, g1) if outer == "i" else (g1, g0)

    in_specs, args = [], []
    for p in pairs:
        ao, bk, bn = p.get("a_off", 0), p.get("bk_off", 0), p.get("bn_off", 0)
        mode = dict(pipeline_mode=pl.Buffered(1)) if p.get("resident") else {}
        in_specs.append(pl.BlockSpec((tm, tk), lambda g0, g1, k, ao=ao: (ij(g0, g1)[0], k + ao)))
        if "b_shift" in p:
            first, shift = p["b_shift"]
            if p.get("trans_b"):
                in_specs.append(pl.BlockSpec(
                    (pl.Element(tn), pl.Element(tk)),
                    lambda g0, g1, k, bk=bk: (
                        pl.multiple_of(ij(g0, g1)[1] * tn + jnp.where(ij(g0, g1)[1] >= first, shift, 0), 16),
                        (k + bk) * tk)))
            else:
                in_specs.append(pl.BlockSpec(
                    (pl.Element(tk), pl.Element(tn)),
                    lambda g0, g1, k, bn=bn: (pl.multiple_of(k * tk + jnp.where(k >= first, shift, 0), 16),
                                              (ij(g0, g1)[1] + bn) * tn)))
        elif p.get("trans_b"):
            in_specs.append(pl.BlockSpec((tn, tk), lambda g0, g1, k, bk=bk, bn=bn: (ij(g0, g1)[1] + bn, k + bk), **mode))
        else:
            in_specs.append(pl.BlockSpec((tk, tn), lambda g0, g1, k, bk=bk, bn=bn: (k + bk, ij(g0, g1)[1] + bn), **mode))
        args += [p["a"], p["b"]]
    for arr, off in extras:
        in_specs.append(pl.BlockSpec((tm, tn), lambda g0, g1, k, off=off: (ij(g0, g1)[0], ij(g0, g1)[1] + off)))
        args.append(arr)
    for arr in vecs:
        in_specs.append(pl.BlockSpec((1, tn), lambda g0, g1, k: (0, ij(g0, g1)[1])))
        args.append(arr)
    if stack:
        assert N == tn
        out_specs = [pl.BlockSpec((tm, n_out * tn), lambda g0, g1, k: (ij(g0, g1)[0], 0))]
        out_shape = [jax.ShapeDtypeStruct((M, n_out * N), out_dtypes[0])]
    else:
        out_specs = [pl.BlockSpec((tm, tn), lambda g0, g1, k: ij(g0, g1)) for _ in out_dtypes]
        out_shape = [jax.ShapeDtypeStruct((M, N), dt) for dt in out_dtypes]
    if wide:
        out_specs.append(pl.BlockSpec((pl.Element(tm), pl.Element(wide["width"])),
                                      lambda g0, g1, k: (pl.multiple_of(ij(g0, g1)[0] * tm, 16), wide["col"])))
        out_shape.append(jax.ShapeDtypeStruct((M, wide["total"]), wide["dtype"]))
    n_tile_out = len(out_specs)
    out_specs += [pl.BlockSpec((1, tn), lambda g0, g1, k: (0, 0)) for _ in range(row_sums)]
    out_shape += [jax.ShapeDtypeStruct((1, N), F32) for _ in range(row_sums)]
    grid = (M // tm, N // tn, nk) if outer == "i" else (N // tn, M // tm, nk)
    n_in = 2 * n_pairs + n_ex + len(vecs)

    def partials(refs, cs=slice(None)):
        accs = [None] * n_acc
        for idx, p in enumerate(pairs):
            b_ref = refs[2 * idx + 1]
            d = (_dot(refs[2 * idx][...], b_ref[cs, :], NT) if p.get("trans_b")
                 else _dot(refs[2 * idx][...], b_ref[:, cs], NN))
            accs[p["acc"]] = d if accs[p["acc"]] is None else accs[p["acc"]] + d
        return accs

    def finish(accs, refs, first_rows, cs=slice(None)):
        res = epilogue(accs, [r[:, cs] for r in refs[2 * n_pairs:n_in]])
        if stack:
            o = refs[n_in]
            for idx in range(n_out):
                lo = idx * tn + (cs.start or 0)
                o[:, lo:lo + (tn if cs.stop is None else cs.stop - cs.start)] = res[idx].astype(o.dtype)
        else:
            for o, r in zip(refs[n_in:n_in + n_out], res):
                o[:, cs] = r.astype(o.dtype)
        if wide:
            o = refs[n_in + n_tile_out - 1]
            o[...] = res[n_out].astype(o.dtype)
        for o, r in zip(refs[n_in + n_tile_out:n_in + n_tile_out + row_sums], res[n_out + bool(wide):]):
            @pl.when(first_rows)
            def _(o=o, r=r):
                o[...] = r

            @pl.when(jnp.logical_not(first_rows))
            def _(o=o, r=r):
                o[...] += r

    if nk == 1 and sub:
        assert not wide and not row_sums and tn % sub == 0

        def body(*refs):
            for c in range(tn // sub):
                cs = slice(c * sub, (c + 1) * sub)
                finish(partials(refs, cs), refs, None, cs)
        scratch = []
    elif nk == 1:
        def body(*refs):
            finish(partials(refs), refs, pl.program_id(0) == 0)
        scratch = []
    else:
        def body(*refs):
            acc_refs = refs[-n_acc:]
            k = pl.program_id(2)
            first_rows = pl.program_id(0) == 0
            new = partials(refs)

            @pl.when(k == 0)
            def _():
                for a, v in zip(acc_refs, new):
                    a[...] = v

            @pl.when(k > 0)
            def _():
                for a, v in zip(acc_refs, new):
                    a[...] += v

            @pl.when(k == nk - 1)
            def _():
                finish([a[...] for a in acc_refs], refs, first_rows)
        scratch = [pltpu.VMEM((tm, tn), F32) for _ in range(n_acc)]

    return _call(body, name=name, grid=grid, in_specs=in_specs, out_specs=out_specs, out_shape=out_shape,
                 scratch=scratch, sem=("parallel", "parallel", "arbitrary"), args=args, comm=comm)


def _matmul_tn(name, x, y, t1, t2, tr, scale=1.0, comm=None, out_dtype=BF16, out_skip=None):
    R, K1 = x.shape
    N1 = y.shape[1]
    nr, n1 = R // tr, K1 // t1
    x_spec = pl.BlockSpec((tr, t1), lambda i, j, r: (r, i))
    rows_out = K1
    o_spec = pl.BlockSpec((t1, t2), lambda i, j, r: (i, j))
    if out_skip:
        row, count = out_skip
        rows_out += count
        o_spec = pl.BlockSpec(
            (pl.Element(t1), pl.Element(t2)),
            lambda i, j, r: (pl.multiple_of(i * t1 + jnp.where(i * t1 >= row, count, 0), 16), j * t2))

    def body(x_ref, y_ref, o_ref, *acc):
        d = _dot(x_ref[...], y_ref[...], TN)
        if nr == 1:
            o_ref[...] = (d * scale).astype(o_ref.dtype)
            return
        r = pl.program_id(2)

        @pl.when(r == 0)
        def _():
            acc[0][...] = d

        @pl.when(jnp.logical_and(r > 0, r < nr - 1))
        def _():
            acc[0][...] += d

        @pl.when(r == nr - 1)
        def _():
            o_ref[...] = ((acc[0][...] + d) * scale).astype(o_ref.dtype)

    return _call(
        body, name=name, grid=(n1, N1 // t2, nr),
        in_specs=[x_spec, pl.BlockSpec((tr, t2), lambda i, j, r: (r, j))], out_specs=[o_spec],
        out_shape=[jax.ShapeDtypeStruct((rows_out, N1), out_dtype)],
        scratch=[pltpu.VMEM((t1, t2), F32)] if nr > 1 else [],
        sem=("parallel", "parallel", "arbitrary"), args=(x, y), comm=comm)


def _embed_norm(x, meta, w, comm=None):
    Bl, S, D = x.shape
    nb = (PAD + N_META + S) // Q
    M = Bl * nb * Q

    def body(x_ref, meta_ref, w_ref, h_ref, n_ref):
        head = jnp.concatenate([jnp.zeros((PAD, D), F32), meta_ref[...]], axis=0)
        h = jnp.where(pl.program_id(1) == 0, head, x_ref[0])
        h_ref[...] = h
        n_ref[...] = _rmsnorm_tile(h, w_ref[...]).astype(n_ref.dtype)

    row = pl.BlockSpec((Q, D), lambda b, t: (b * nb + t, 0))
    return _call(
        body, name="embed_norm", grid=(Bl, nb),
        in_specs=[pl.BlockSpec((1, Q, D), lambda b, t: (b, jnp.maximum(t - 1, 0), 0)),
                  pl.BlockSpec((N_META, D), lambda b, t: (0, 0)), pl.BlockSpec((1, D), lambda b, t: (0, 0))],
        out_specs=[row, row], out_shape=[jax.ShapeDtypeStruct((M, D), F32), jax.ShapeDtypeStruct((M, D), BF16)],
        scratch=[], sem=("parallel", "parallel"), args=(x, meta, w), comm=comm)


def _rmsnorm_bwd_tile(dn, h, w, dh_in):
    r = lax.rsqrt(jnp.mean(h * h, axis=-1, keepdims=True) + EPS)
    xhat = h * r
    gw = dn * w
    dh = dh_in + r * (gw - xhat * jnp.mean(gw * xhat, axis=-1, keepdims=True))
    return dh, jnp.sum(dn * xhat, axis=0, keepdims=True)


def _loss_head(h, w, target, Bl, nb):
    M, D = h.shape

    def body(h_ref, w_ref, t_ref, dh_ref, dhb_ref, dw_ref, loss_ref):
        b, t = pl.program_id(0), pl.program_id(1)
        live = (t > 0).astype(F32)
        x = h_ref[...]
        r = lax.rsqrt(jnp.mean(x * x, axis=-1, keepdims=True) + EPS)
        xhat = x * r
        wv = w_ref[...]
        err = (xhat * wv - t_ref[0]) * live
        dy = err * (1.0 / D)
        gw = dy * wv
        dx = r * (gw - xhat * jnp.mean(gw * xhat, axis=-1, keepdims=True))
        dh_ref[...] = dx
        dhb_ref[...] = dx.astype(BF16)
        dw = jnp.sum(dy * xhat, axis=0, keepdims=True)
        part = 0.5 * jnp.sum(jnp.sum(err * err, axis=-1, keepdims=True) * (1.0 / D), axis=0, keepdims=True)
        first = jnp.logical_and(b == 0, t == 0)

        @pl.when(first)
        def _():
            dw_ref[...] = dw
            loss_ref[...] = jnp.broadcast_to(part, loss_ref.shape)

        @pl.when(jnp.logical_not(first))
        def _():
            dw_ref[...] += dw
            loss_ref[...] += jnp.broadcast_to(part, loss_ref.shape)

    row = pl.BlockSpec((Q, D), lambda b, t: (b * nb + t, 0))
    vec = pl.BlockSpec((1, D), lambda b, t: (0, 0))
    return pl.pallas_call(
        body, name="loss_head", grid=(Bl, nb),
        in_specs=[row, vec, pl.BlockSpec((1, Q, D), lambda b, t: (b, jnp.maximum(t - 1, 0), 0))],
        out_specs=[row, row, vec, pl.BlockSpec((8, 128), lambda b, t: (0, 0))],
        out_shape=[jax.ShapeDtypeStruct((M, D), F32), jax.ShapeDtypeStruct((M, D), BF16),
                   jax.ShapeDtypeStruct((1, D), F32), jax.ShapeDtypeStruct((8, 128), F32)],
        compiler_params=_params(("arbitrary", "arbitrary")),
    )(h, w, target)


CONV_TC = 256


def _conv_pre(xr_ref, w_ref, b_ref):
    x = xr_ref[...].astype(F32)
    acc = b_ref[...] + w_ref[SSD_CONV - 1:SSD_CONV, :] * x
    for k in range(1, SSD_CONV):
        acc = acc + w_ref[SSD_CONV - 1 - k:SSD_CONV - k, :] * pltpu.roll(x, k, 0)
    return x, acc


def _conv_fwd(proj, w, b, Bl, T):
    M = proj.shape[0]
    off = 1024 // CONV_TC

    def body(xr_ref, w_ref, b_ref, o_ref):
        _, acc = _conv_pre(xr_ref, w_ref, b_ref)
        row = lax.broadcasted_iota(jnp.int32, acc.shape, 0)
        o_ref[...] = jnp.where(row >= PAD, acc * _sigmoid(acc), 0.0).astype(o_ref.dtype)

    return pl.pallas_call(
        body, name="conv_fwd", grid=(Bl, SSD_CONV_CH // CONV_TC),
        in_specs=[pl.BlockSpec((T, CONV_TC), lambda bb, j: (bb, j + off)),
                  pl.BlockSpec((SSD_CONV, CONV_TC), lambda bb, j: (0, j)), pl.BlockSpec((1, CONV_TC), lambda bb, j: (0, j))],
        out_specs=pl.BlockSpec((T, CONV_TC), lambda bb, j: (bb, j)),
        out_shape=jax.ShapeDtypeStruct((M, SSD_CONV_CH), BF16), compiler_params=_params(("parallel", "parallel")),
    )(proj, w, b)


def _conv_bwd(proj, w, b, dxc, dproj, Bl, T):
    M = proj.shape[0]
    off = 1024 // CONV_TC

    def body(xr_ref, w_ref, b_ref, d_ref, dx_ref, dw_ref, db_ref):
        x, acc = _conv_pre(xr_ref, w_ref, b_ref)
        row = lax.broadcasted_iota(jnp.int32, acc.shape, 0)
        s = _sigmoid(acc)
        dpre = jnp.where(row >= PAD, d_ref[...].astype(F32) * _dsilu(acc, s), 0.0)
        dx = w_ref[SSD_CONV - 1:SSD_CONV, :] * dpre
        dws = [jnp.sum(dpre * x, axis=0, keepdims=True)]
        for k in range(1, SSD_CONV):
            dx = dx + w_ref[SSD_CONV - 1 - k:SSD_CONV - k, :] * pltpu.roll(dpre, T - k, 0)
            dws.append(jnp.sum(dpre * pltpu.roll(x, k, 0), axis=0, keepdims=True))
        dx_ref[...] = dx.astype(dx_ref.dtype)
        dw = jnp.concatenate(dws[::-1], axis=0)
        db = jnp.sum(dpre, axis=0, keepdims=True)

        @pl.when(pl.program_id(1) == 0)
        def _():
            dw_ref[...] = dw
            db_ref[...] = db

        @pl.when(pl.program_id(1) > 0)
        def _():
            dw_ref[...] += dw
            db_ref[...] += db

    return _call(
        body, name="conv_bwd", grid=(SSD_CONV_CH // CONV_TC, Bl),
        in_specs=[pl.BlockSpec((T, CONV_TC), lambda j, bb: (bb, j + off)),
                  pl.BlockSpec((SSD_CONV, CONV_TC), lambda j, bb: (0, j)), pl.BlockSpec((1, CONV_TC), lambda j, bb: (0, j)),
                  pl.BlockSpec((T, CONV_TC), lambda j, bb: (bb, j))],
        out_specs=[pl.BlockSpec((T, CONV_TC), lambda j, bb: (bb, j + off)),
                   pl.BlockSpec((SSD_CONV, CONV_TC), lambda j, bb: (0, j)), pl.BlockSpec((1, CONV_TC), lambda j, bb: (0, j))],
        out_shape=[jax.ShapeDtypeStruct(dproj.shape, BF16), jax.ShapeDtypeStruct((SSD_CONV, SSD_CONV_CH), F32),
                   jax.ShapeDtypeStruct((1, SSD_CONV_CH), F32)],
        scratch=[], sem=("parallel", "arbitrary"), args=(proj, w, b, dxc), into=(dproj, 0))


N_PAIR = SSD_HEADS // 2
HPG = SSD_HEADS // SSD_GROUPS
GW = SSD_INNER // SSD_GROUPS


def _per_group(fn, *arrs):
    return jnp.concatenate([jnp.broadcast_to(fn(*(a[:, GW * g:GW * (g + 1)] for a in arrs)), (arrs[0].shape[0], GW))
                            for g in range(SSD_GROUPS)], axis=1)


def _ssd_prep(c, dtr_ref, bias_ref, alog_ref, d_ref):
    row = lax.broadcasted_iota(jnp.int32, (Q, 128), 0)
    col = lax.broadcasted_iota(jnp.int32, (Q, 128), 1)
    live = col < SSD_HEADS
    valid = jnp.logical_and(jnp.logical_or(c > 0, row >= PAD), live)
    pre = dtr_ref[...] + bias_ref[...]
    dt = jnp.where(valid, _softplus(pre), 0.0)
    A = jnp.where(live[0:1], -jnp.exp(alog_ref[...]), 0.0)
    tri = row >= col
    eye = (row == col).astype(BF16)
    cs = _dot01(tri, dt * A, NN, "a")
    cst = _dot01(eye, cs, NT, "a")
    spread = (lax.broadcasted_iota(jnp.int32, (128, SSD_INNER), 0)
              == lax.broadcasted_iota(jnp.int32, (128, SSD_INNER), 1) // SSD_HEAD_DIM).astype(BF16)
    dt_w = _dot01(dt, spread, NN, "b")
    cs_w = _dot01(cs, spread, NN, "b")
    d_w = _dot01(jnp.broadcast_to(d_ref[...], (8, 128)), spread, NN, "b")[0:1]
    lane = lax.broadcasted_iota(jnp.int32, (Q, SSD_INNER), 1)
    first = (lane % 128) < SSD_HEAD_DIM
    return dict(row=row, col=col, valid=valid, pre=pre, dt=dt, A=A, tri=tri, eye=eye, cs=cs, cst=cst, spread=spread,
                dt_w=dt_w, cs_w=cs_w, d_w=d_w, ecs_w=jnp.exp(cs_w), decay_w=jnp.exp(cs_w[Q - 1:Q] - cs_w), first=first)


def _ssd_chunk(xc_ref, s, states):
    xv = xc_ref[:, 0:SSD_INNER].astype(F32)
    Bs = [xc_ref[:, SSD_INNER + 128 * g:SSD_INNER + 128 * (g + 1)] for g in range(SSD_GROUPS)]
    Cs = [xc_ref[:, SSD_INNER + 512 + 128 * g:SSD_INNER + 512 + 128 * (g + 1)] for g in range(SSD_GROUPS)]
    X = xv * s["dt_w"]
    X0 = jnp.where(s["first"], X, 0.0)
    Xb = (X0.astype(BF16), (X - X0).astype(BF16))
    Xd = (X * s["decay_w"]).astype(BF16)
    CB = [_dot(Cs[g], Bs[g], NT) for g in range(SSD_GROUPS)]
    Lms = [jnp.exp(jnp.where(s["tri"], s["cs"][:, h:h + 1] - s["cst"][h:h + 1, :], -jnp.inf)) for h in range(SSD_HEADS)]
    Ms = [CB[h // HPG] * Lms[h] for h in range(SSD_HEADS)]
    Mb = [m.astype(BF16) for m in Ms]
    prev_b = [st.astype(BF16) for st in states]
    yds, yos, sts = [], [], []
    for p in range(N_PAIR):
        g, ln = p // 2, slice(128 * p, 128 * (p + 1))
        yds.append(_dot(Mb[2 * p], Xb[0][:, ln], NN) + _dot(Mb[2 * p + 1], Xb[1][:, ln], NN))
        yos.append(_dot(Cs[g], prev_b[p], NT))
        sts.append(_dot(Xd[:, ln], Bs[g], TN))
    yo = jnp.concatenate(yos, axis=1)
    y = jnp.concatenate(yds, axis=1) + yo * s["ecs_w"] + xv * s["d_w"]
    upper = s["row"] < SSD_HEAD_DIM
    cl = s["cs"][Q - 1:Q, :]
    ecl_rows = [jnp.where(upper, jnp.exp(cl[:, 2 * p:2 * p + 1]), jnp.exp(cl[:, 2 * p + 1:2 * p + 2])) for p in range(N_PAIR)]
    new_states = [states[p] * ecl_rows[p] + sts[p] for p in range(N_PAIR)]
    return y, new_states, dict(xv=xv, Bs=Bs, Cs=Cs, X=X, Xb=Xb, CB=CB, Lms=Lms, Ms=Ms, Mb=Mb, prev_b=prev_b, yo=yo,
                               ecl_rows=ecl_rows)


def _ssd_in_specs(nc, rev=False):
    rb = (lambda b, c: b * nc + nc - 1 - c) if rev else (lambda b, c: b * nc + c)
    vec = pl.BlockSpec((1, 128), lambda b, c: (0, 0))
    return [pl.BlockSpec((Q, SSD_CONV_CH), lambda b, c: (rb(b, c), 0)),
            pl.BlockSpec((Q, 128), lambda b, c: (rb(b, c), 0)),
            pl.BlockSpec((Q, SSD_INNER), lambda b, c: (rb(b, c), 0)),
            vec, vec, vec, pl.BlockSpec((1, SSD_INNER), lambda b, c: (0, 0))]


def _ssd_fwd(xc, dtr, proj, bias_p, alog_p, d_p, nw, Bl, nc):
    M = xc.shape[0]

    def body(xc_ref, dtr_ref, z_ref, bias_ref, alog_ref, d_ref, nw_ref, y_ref, prev_ref, state):
        c = pl.program_id(1)

        @pl.when(c == 0)
        def _():
            state[...] = jnp.zeros_like(state)

        s = _ssd_prep(c, dtr_ref, bias_ref, alog_ref, d_ref)
        states = [state[p] for p in range(N_PAIR)]
        y, new_states, _ = _ssd_chunk(xc_ref, s, states)
        for p in range(N_PAIR):
            prev_ref[0, 0, p] = states[p]
            state[p] = new_states[p]
        zz = z_ref[...].astype(F32)
        yg = y * zz * _sigmoid(zz)
        r = _per_group(lambda a: lax.rsqrt(jnp.mean(a * a, axis=-1, keepdims=True) + EPS), yg)
        y_ref[...] = (yg * r * nw_ref[...]).astype(y_ref.dtype)

    return pl.pallas_call(
        body, name="ssd_fwd", grid=(Bl, nc), in_specs=_ssd_in_specs(nc),
        out_specs=[pl.BlockSpec((Q, SSD_INNER), lambda b, c: (b * nc + c, 0)),
                   pl.BlockSpec((1, 1, N_PAIR, 128, 128), lambda b, c: (b, c, 0, 0, 0))],
        out_shape=[jax.ShapeDtypeStruct((M, SSD_INNER), BF16), jax.ShapeDtypeStruct((Bl, nc, N_PAIR, 128, 128), F32)],
        scratch_shapes=[pltpu.VMEM((N_PAIR, 128, 128), F32)],
        compiler_params=_params(("arbitrary", "arbitrary")),
    )(xc, dtr, proj, bias_p, alog_p, d_p, nw)


def _ssd_bwd(xc, dtr, proj, bias_p, alog_p, d_p, nw, prev, dya, dproj, Bl, nc, comm=None):
    M = xc.shape[0]

    def body(xc_ref, dtr_ref, z_ref, bias_ref, alog_ref, d_ref, nw_ref, prev_ref, dy_ref,
             dxc_ref, dz_ref, ddtr_ref, dbias_ref, dalog_ref, dd_ref, dnw_ref, dS):
        b, t = pl.program_id(0), pl.program_id(1)

        @pl.when(t == 0)
        def _():
            dS[...] = jnp.zeros_like(dS)

        s = _ssd_prep(nc - 1 - t, dtr_ref, bias_ref, alog_ref, d_ref)
        states = [prev_ref[0, 0, p] for p in range(N_PAIR)]
        y, _, k = _ssd_chunk(xc_ref, s, states)
        xv, Bs, Cs, Xb = k["xv"], k["Bs"], k["Cs"], k["Xb"]

        zz = z_ref[...].astype(F32)
        sz = _sigmoid(zz)
        silu_z = zz * sz
        yg = y * silu_z
        r = _per_group(lambda a: lax.rsqrt(jnp.mean(a * a, axis=-1, keepdims=True) + EPS), yg)
        xhat = yg * r
        dout = dy_ref[...].astype(F32)
        gw = dout * nw_ref[...]
        dyg = r * (gw - xhat * _per_group(lambda a, c2: jnp.mean(a * c2, axis=-1, keepdims=True), gw, xhat))
        dnw = jnp.sum(dout * xhat, axis=0, keepdims=True)
        dz_ref[...] = (dyg * y * _dsilu(zz, sz)).astype(dz_ref.dtype)
        dy = dyg * silu_z
        dy0 = jnp.where(s["first"], dy, 0.0)
        dyb = (dy0.astype(BF16), (dy - dy0).astype(BF16))
        dYo = (dy * s["ecs_w"]).astype(BF16)

        dS_f = [dS[p] for p in range(N_PAIR)]
        dS_b = [d.astype(BF16) for d in dS_f]
        BdS, dXm, dprev, dCs, dMs, XdS = [], [], [], [[] for _ in range(SSD_GROUPS)], [], []
        for p in range(N_PAIR):
            g, ln = p // 2, slice(128 * p, 128 * (p + 1))
            BdS.append(_dot(Bs[g], dS_b[p], NT))
            dXm.append(_dot(k["Mb"][2 * p], dyb[0][:, ln], TN) + _dot(k["Mb"][2 * p + 1], dyb[1][:, ln], TN))
            dprev.append(_dot(dYo[:, ln], Cs[g], TN))
            dCs[g].append(_dot(dYo[:, ln], k["prev_b"][p], NN))
            for hh in range(2):
                dMs.append(_dot(dyb[hh][:, ln], Xb[hh][:, ln], NT))
                XdS.append(_dot(Xb[hh][:, ln], dS_b[p], NN))
        dX = jnp.concatenate(dXm, axis=1) + s["decay_w"] * jnp.concatenate(BdS, axis=1)
        dxs = dy * s["d_w"] + dX * s["dt_w"]

        sums = _dot01(jnp.concatenate([dX * xv, dy * k["yo"] * s["ecs_w"], dy * xv], axis=0), s["spread"], NT, "b")
        ddt, dcs = sums[0:Q], sums[Q:2 * Q]
        dD = jnp.sum(sums[2 * Q:3 * Q], axis=0, keepdims=True)

        col, row = s["col"], s["row"]
        lane1 = col[0:1]
        rowsT = lax.broadcasted_iota(jnp.int32, (128, Q), 0)
        dcs_t = jnp.zeros((128, Q), F32)
        dcl = jnp.zeros((1, 128), F32)
        dB_out, dC_out = [], []
        for g in range(SSD_GROUPS):
            Bf = Bs[g].astype(F32)
            dCB = jnp.zeros((Q, Q), F32)
            dBacc = jnp.zeros((Q, 128), F32)
            for r4 in range(HPG):
                h = HPG * g + r4
                p, hh = h // 2, h % 2
                W = dMs[h] * k["Ms"][h]
                dCB = dCB + dMs[h] * k["Lms"][h]
                decay_h = s["decay_w"][:, SSD_HEAD_DIM * h:SSD_HEAD_DIM * h + 1]
                dBacc = dBacc + decay_h * XdS[h]
                tdec = jnp.sum(XdS[h] * Bf, axis=1, keepdims=True) * decay_h
                dcs = dcs + jnp.where(col == h, jnp.sum(W, axis=1, keepdims=True) - tdec, 0.0)
                dcs_t = dcs_t - jnp.where(rowsT == h, jnp.sum(W, axis=0, keepdims=True), 0.0)
                rows_h = (row < SSD_HEAD_DIM) if hh == 0 else (row >= SSD_HEAD_DIM)
                sprev = jnp.sum(jnp.sum(jnp.where(rows_h, dS_f[p] * states[p], 0.0), axis=1, keepdims=True),
                                axis=0, keepdims=True)
                ecl = jnp.exp(s["cs"][Q - 1:Q, h:h + 1])
                dcl = dcl + jnp.where(lane1 == h, jnp.sum(tdec, axis=0, keepdims=True) + ecl * sprev, 0.0)
            dCB_b = dCB.astype(BF16)
            dC_out.append(dCs[g][0] + dCs[g][1] + _dot(dCB_b, Bs[g], NN))
            dB_out.append(dBacc + _dot(dCB_b, Cs[g], TN))
        for p in range(N_PAIR):
            dS[p] = dS_f[p] * k["ecl_rows"][p] + dprev[p]
        dxc_ref[...] = jnp.concatenate([dxs] + dB_out + dC_out, axis=1).astype(dxc_ref.dtype)

        dcs = dcs + _dot01(s["eye"], dcs_t, NT, "a") + jnp.where(row == Q - 1, dcl, 0.0)
        da = _dot01(row <= col, dcs, NN, "a")
        ddt = ddt + da * s["A"]
        dpre = jnp.where(s["valid"], ddt * _sigmoid(s["pre"]), 0.0)
        ddtr_ref[...] = dpre
        dbias = jnp.sum(dpre, axis=0, keepdims=True)
        dalog = jnp.sum(da * s["dt"], axis=0, keepdims=True) * s["A"]
        first_step = jnp.logical_and(b == 0, t == 0)

        @pl.when(first_step)
        def _():
            dbias_ref[...] = dbias
            dalog_ref[...] = dalog
            dd_ref[...] = dD
            dnw_ref[...] = dnw

        @pl.when(jnp.logical_not(first_step))
        def _():
            dbias_ref[...] += dbias
            dalog_ref[...] += dalog
            dd_ref[...] += dD
            dnw_ref[...] += dnw

    rb = lambda b, c: b * nc + nc - 1 - c
    rowblk = lambda w: pl.BlockSpec((Q, w), lambda b, c: (rb(b, c), 0))
    vec = lambda w: pl.BlockSpec((1, w), lambda b, c: (0, 0))
    return _call(
        body, name="ssd_bwd", grid=(Bl, nc),
        in_specs=_ssd_in_specs(nc, rev=True) + [
            pl.BlockSpec((1, 1, N_PAIR, 128, 128), lambda b, c: (b, nc - 1 - c, 0, 0, 0)), rowblk(SSD_INNER)],
        out_specs=[rowblk(SSD_CONV_CH), rowblk(SSD_INNER), rowblk(128), vec(128), vec(128), vec(128), vec(SSD_INNER)],
        out_shape=[jax.ShapeDtypeStruct((M, SSD_CONV_CH), BF16), jax.ShapeDtypeStruct(dproj.shape, BF16),
                   jax.ShapeDtypeStruct((M, 128), F32), jax.ShapeDtypeStruct((1, 128), F32),
                   jax.ShapeDtypeStruct((1, 128), F32), jax.ShapeDtypeStruct((1, 128), F32),
                   jax.ShapeDtypeStruct((1, SSD_INNER), F32)],
        scratch=[pltpu.VMEM((N_PAIR, 128, 128), F32)], sem=("arbitrary", "arbitrary"),
        args=(xc, dtr, proj, bias_p, alog_p, d_p, nw, prev, dya), comm=comm, into=(dproj, 1))


NSUB = Q // HG_CHUNK
HG_HP = 8
EXP_CAP = 80.0


def _hg_setup(blk, q_ref, f_ref, hb_ref):
    row = lax.broadcasted_iota(jnp.int32, (Q, Q), 0)
    col = lax.broadcasted_iota(jnp.int32, (Q, Q), 1)
    same = (row // HG_CHUNK) == (col // HG_CHUNK)
    causal = jnp.logical_and(same, col <= row)
    lb = _sigmoid(hb_ref[0:1, :] - hb_ref[1:2, :])
    fl = f_ref[...].astype(F32)
    sg = _sigmoid(fl)
    fg = lb + (1.0 - lb) * sg
    k = (1.0 - lb) * (1.0 - sg)
    gl = jnp.log(fg)
    G = _dot01(causal, gl, NN, "a")
    T = _dot01(same, gl, NN, "a")
    qv = q_ref[...].astype(F32)
    sq = _sigmoid(qv)
    eG = jnp.exp(G)
    eGn = jnp.exp(jnp.minimum(-G, EXP_CAP))
    eTG = jnp.exp(T - G)
    qt = qv * sq * eG
    kt = k * eGn
    kh = k * eTG
    valid = jnp.logical_or(blk > 0, row[:, :1] >= PAD)
    return dict(row=row, col=col, same=same, causal=causal, lb=lb, sg=sg, fg=fg, k=k, T=T, qv=qv, sq=sq,
                eG=eG, eGn=eGn, eTG=eTG, qt=qt, kt=kt, kh=kh, valid=valid)


def _hg_specs(nb, rev=False):
    rb = (lambda h, b, t: b * nb + nb - 1 - t) if rev else (lambda h, b, t: b * nb + t)
    w = 128 * HG_HP
    blk = lambda off: pl.BlockSpec((Q, w), lambda h, b, t, off=off: (rb(h, b, t), off // HG_HP + h))
    return [blk(24), blk(32), blk(40), blk(48),
            pl.BlockSpec((2, w), lambda h, b, t: (0, h)), pl.BlockSpec((1, w), lambda h, b, t: (0, h))]


HEAD_LANES = tuple(slice(128 * hh, 128 * (hh + 1)) for hh in range(HG_HP))


def _per_head(fn, *arrs):
    return jnp.concatenate([jnp.broadcast_to(fn(*(a[:, ln] for a in arrs)), (arrs[0].shape[0], 128))
                            for ln in HEAD_LANES], axis=1)


def _hgrn_fwd(proj, hb, nw, Bl, nb, comm=None):
    M = proj.shape[0]

    def body(q_ref, f_ref, i_ref, g_ref, hb_ref, nw_ref, y_ref, o_ref, st_ref, S):
        blk = pl.program_id(2)

        @pl.when(blk == 0)
        def _():
            S[...] = jnp.zeros_like(S)

        s = _hg_setup(blk, q_ref, f_ref, hb_ref)
        v = i_ref[...]
        qt_b, kt_b, kh_b = s["qt"].astype(BF16), s["kt"].astype(BF16), s["kh"].astype(BF16)
        eT = jnp.exp(s["T"])
        att = [jnp.where(s["causal"], _dot(qt_b[:, ln], kt_b[:, ln], NT), 0.0).astype(BF16) for ln in HEAD_LANES]
        o_intra = [_dot(att[hh], v[:, ln], NN) for hh, ln in enumerate(HEAD_LANES)]
        for j in range(NSUB):
            sl = slice(HG_CHUNK * j, HG_CHUNK * (j + 1))
            for hh, ln in enumerate(HEAD_LANES):
                St = S[hh]
                st_ref[0, hh, 0, j] = St
                o_ref[sl, ln] = o_intra[hh][sl] + _dot(qt_b[sl, ln], St.astype(BF16), NT)
                S[hh] = St * eT[HG_CHUNK * j:HG_CHUNK * j + 1, ln] + _dot(v[sl, ln], kh_b[sl, ln], TN)
        o = o_ref[...]
        r = _per_head(lambda a: lax.rsqrt(jnp.mean(a * a, axis=-1, keepdims=True) + EPS), o)
        gv = g_ref[...].astype(F32)
        y_ref[...] = (o * r * nw_ref[...] * gv * _sigmoid(gv)).astype(y_ref.dtype)

    rowblk = pl.BlockSpec((Q, 128 * HG_HP), lambda h, b, t: (b * nb + t, h))
    return _call(
        body, name="hgrn_fwd", grid=(HG_HEADS // HG_HP, Bl, nb), in_specs=_hg_specs(nb),
        out_specs=[rowblk, rowblk,
                   pl.BlockSpec((1, HG_HP, 1, NSUB, 128, 128), lambda h, b, t: (b, h, t, 0, 0, 0))],
        out_shape=[jax.ShapeDtypeStruct((M, HG_WIDTH), BF16), jax.ShapeDtypeStruct((M, HG_WIDTH), F32),
                   jax.ShapeDtypeStruct((Bl, HG_HEADS, nb, NSUB, 128, 128), F32)],
        scratch=[pltpu.VMEM((HG_HP, 128, 128), F32)], sem=("parallel", "arbitrary", "arbitrary"),
        args=(proj, proj, proj, proj, hb, nw), comm=comm)


def _hgrn_bwd(proj, hb, nw, o_saved, st_saved, dyb, dproj, Bl, nb, comm=None):
    assert HG_HP == HG_HEADS

    def body(q_ref, f_ref, i_ref, g_ref, hb_ref, nw_ref, o_ref, st_ref, dy_ref,
             d_ref, dhb_ref, dnw_ref, dS, a_dqt, a_dv, a_dkh, a_dgl):
        b, t = pl.program_id(1), pl.program_id(2)

        @pl.when(t == 0)
        def _():
            dS[...] = jnp.zeros_like(dS)

        first_step = jnp.logical_and(b == 0, t == 0)
        s = _hg_setup(nb - 1 - t, q_ref, f_ref, hb_ref)
        v = i_ref[...]
        qt_b, kt_b, kh_b = s["qt"].astype(BF16), s["kt"].astype(BF16), s["kh"].astype(BF16)
        eT = jnp.exp(s["T"])
        att = [jnp.where(s["causal"], _dot(qt_b[:, ln], kt_b[:, ln], NT), 0.0).astype(BF16) for ln in HEAD_LANES]

        o = o_ref[...]
        r = _per_head(lambda a: lax.rsqrt(jnp.mean(a * a, axis=-1, keepdims=True) + EPS), o)
        xhat = o * r
        gv = g_ref[...].astype(F32)
        sgv = _sigmoid(gv)
        dyv = dy_ref[...].astype(F32)
        d_on = dyv * gv * sgv
        dg_out = dyv * xhat * nw_ref[...] * _dsilu(gv, sgv)
        gw = d_on * nw_ref[...]
        do = r * (gw - xhat * _per_head(lambda a, c: jnp.mean(a * c, axis=-1, keepdims=True), gw, xhat))
        dnw = jnp.sum(d_on * xhat, axis=0, keepdims=True)
        do_b = do.astype(BF16)

        datt = [jnp.where(s["causal"], _dot(do_b[:, ln], v[:, ln], NT), 0.0).astype(BF16) for ln in HEAD_LANES]
        dqt = jnp.concatenate([_dot(datt[hh], kt_b[:, ln], NN) for hh, ln in enumerate(HEAD_LANES)], axis=1)
        dkt = jnp.concatenate([_dot(datt[hh], qt_b[:, ln], TN) for hh, ln in enumerate(HEAD_LANES)], axis=1)
        dv = jnp.concatenate([_dot(att[hh], do_b[:, ln], TN) for hh, ln in enumerate(HEAD_LANES)], axis=1)
        last_row = (lax.broadcasted_iota(jnp.int32, (HG_CHUNK, 128), 0) == HG_CHUNK - 1)
        for j in reversed(range(NSUB)):
            sl = slice(HG_CHUNK * j, HG_CHUNK * (j + 1))
            for hh, ln in enumerate(HEAD_LANES):
                St = st_ref[0, hh, 0, j]
                dSt = dS[hh]
                St_b, dSt_b = St.astype(BF16), dSt.astype(BF16)
                eT_j = eT[HG_CHUNK * j:HG_CHUNK * j + 1, ln]
                dkh_j = _dot(v[sl, ln], dSt_b, NN)
                a_dqt[sl, ln] = _dot(do_b[sl, ln], St_b, NN)
                a_dv[sl, ln] = _dot(kh_b[sl, ln], dSt_b, NT)
                a_dkh[sl, ln] = dkh_j
                dlast = (jnp.sum(St * dSt, axis=0, keepdims=True) * eT_j
                         + jnp.sum(dkh_j * s["kh"][sl, ln], axis=0, keepdims=True))
                a_dgl[sl, ln] = jnp.where(last_row, dlast, 0.0)
                dS[hh] = dSt * eT_j + _dot(do_b[sl, ln], qt_b[sl, ln], TN)
        dqt = dqt + a_dqt[...]
        dv = dv + a_dv[...]
        dkh = a_dkh[...]
        dG = dqt * s["qt"] - dkt * s["kt"] - dkh * s["kh"] + a_dgl[...]
        rev_causal = jnp.logical_and(s["same"], s["col"] >= s["row"])
        dgl = _dot01(rev_causal, dG, NN, "a")
        dk = dkt * s["eGn"] + dkh * s["eTG"]
        dfg = dgl / s["fg"] - dk
        lb, sg = s["lb"], s["sg"]
        keep = s["valid"].astype(F32)
        d_ref[:, 0:w] = (dqt * s["eG"] * _dsilu(s["qv"], s["sq"]) * keep).astype(d_ref.dtype)
        d_ref[:, w:2 * w] = (dfg * (1.0 - lb) * sg * (1.0 - sg) * keep).astype(d_ref.dtype)
        d_ref[:, 2 * w:3 * w] = (dv * keep).astype(d_ref.dtype)
        d_ref[:, 3 * w:4 * w] = (dg_out * keep).astype(d_ref.dtype)
        dlb = jnp.sum(dfg * (1.0 - sg) * keep, axis=0, keepdims=True) * lb * (1.0 - lb)
        dhb = jnp.concatenate([dlb, -dlb], axis=0)

        @pl.when(first_step)
        def _():
            dhb_ref[...] = dhb
            dnw_ref[...] = dnw

        @pl.when(jnp.logical_not(first_step))
        def _():
            dhb_ref[...] += dhb
            dnw_ref[...] += dnw

    w = 128 * HG_HP
    rowblk = pl.BlockSpec((Q, w), lambda h, b, t: (b * nb + nb - 1 - t, h))
    return _call(
        body, name="hgrn_bwd", grid=(HG_HEADS // HG_HP, Bl, nb),
        in_specs=_hg_specs(nb, rev=True) + [
            rowblk, pl.BlockSpec((1, HG_HP, 1, NSUB, 128, 128), lambda h, b, t: (b, h, nb - 1 - t, 0, 0, 0)), rowblk],
        out_specs=[pl.BlockSpec((pl.Element(Q), pl.Element(4 * w)),
                                lambda h, b, t: (pl.multiple_of((b * nb + nb - 1 - t) * Q, Q), 3 * HG_WIDTH)),
                   pl.BlockSpec((2, w), lambda h, b, t: (0, h)), pl.BlockSpec((1, w), lambda h, b, t: (0, h))],
        out_shape=[jax.ShapeDtypeStruct(dproj.shape, BF16),
                   jax.ShapeDtypeStruct((2, HG_WIDTH), F32), jax.ShapeDtypeStruct((1, HG_WIDTH), F32)],
        scratch=[pltpu.VMEM((HG_HP, 128, 128), F32)] + [pltpu.VMEM((Q, w), F32)] * 4,
        sem=("parallel", "arbitrary", "arbitrary"),
        args=(proj, proj, proj, proj, hb, nw, o_saved, st_saved, dyb), comm=comm, into=(dproj, 0))


def _adamw(name, parts, w, m, v, comm=None):
    R, C = w.shape
    S = parts.shape[0]
    tr, tc = (_tile(R, (256, 176, 128, 64, 8)), C) if R % 8 == 0 else (R, 256)
    c1, c2 = 1.0 - ADAM_B1 ** ADAM_STEP, 1.0 - ADAM_B2 ** ADAM_STEP

    def body(p_ref, w_ref, m_ref, v_ref, g_ref, d_ref, nm_ref, nv_ref):
        g = p_ref[0].astype(F32)
        for s in range(1, S):
            g = g + p_ref[s].astype(F32)
        nm = ADAM_B1 * m_ref[...] + (1.0 - ADAM_B1) * g
        nv = ADAM_B2 * v_ref[...] + (1.0 - ADAM_B2) * (g * g)
        g_ref[...] = g
        nm_ref[...] = nm
        nv_ref[...] = nv
        d_ref[...] = -ADAM_LR * ((nm / c1) / (jnp.sqrt(nv / c2) + ADAM_EPS) + ADAM_WD * w_ref[...])

    blk = pl.BlockSpec((tr, tc), lambda i, j: (i, j))
    return _call(
        body, name=name, grid=(R // tr, C // tc),
        in_specs=[pl.BlockSpec((S, tr, tc), lambda i, j: (0, i, j)), blk, blk, blk], out_specs=[blk] * 4,
        out_shape=[jax.ShapeDtypeStruct((R, C), F32)] * 4, scratch=[], sem=("parallel", "parallel"),
        args=(parts, w, m, v), comm=comm)


def _pair_sum(name, by_core, arrived):
    _, J, R, C = by_core.shape
    tc = _tile(C, (512, 256, 128))

    def body(c_ref, a_ref, b_ref, o_ref):
        o_ref[...] = (a_ref[0].astype(F32) + b_ref[...].astype(F32)).astype(o_ref.dtype)

    blk = pl.BlockSpec((1, R, tc), lambda j, k, c_ref: (j, 0, k))
    return pl.pallas_call(
        body, name=name,
        grid_spec=pltpu.PrefetchScalarGridSpec(
            num_scalar_prefetch=1, grid=(J, C // tc),
            in_specs=[pl.BlockSpec((1, 1, R, tc), lambda j, k, c_ref: (c_ref[0], j, 0, k)), blk], out_specs=blk),
        out_shape=jax.ShapeDtypeStruct(arrived.shape, arrived.dtype), compiler_params=_params(("parallel", "parallel")),
    )(lax.axis_index("c").astype(jnp.int32).reshape(1), by_core, arrived)


def _sum_parts(name, parts):
    S, R, C = parts.shape

    def body(p_ref, o_ref):
        g = p_ref[0]
        for s in range(1, S):
            g = g + p_ref[s]
        o_ref[...] = g

    return pl.pallas_call(
        body, name=name, out_shape=jax.ShapeDtypeStruct((R, C), F32),
        in_specs=[pl.BlockSpec(memory_space=pltpu.VMEM)], out_specs=pl.BlockSpec(memory_space=pltpu.VMEM),
    )(parts)


def _heads_to_lanes(p):
    return jnp.pad(p, [(0, 0)] * (p.ndim - 1) + [(0, 128 - SSD_HEADS)])


def _lanes_to_heads(p):
    return p[..., :SSD_HEADS]


def _pack_rows(arrs):
    flat = jnp.concatenate([a.reshape(-1).astype(F32) for a in arrs])
    return jnp.pad(flat, (0, (-flat.shape[0]) % (8 * D_MODEL))).reshape(-1, D_MODEL)


def _unpack_rows(packed, like):
    flat, outs, at = packed.reshape(-1), [], 0
    for a in like:
        outs.append(flat[at:at + a.size].reshape(a.shape))
        at += a.size
    return outs


def _cols(gth):
    return jnp.transpose(gth, (1, 0, 2)).reshape(gth.shape[1], -1)


def _rows(gth):
    return gth.reshape(-1, gth.shape[2])


def _to_rows(g):
    return g.reshape(N_DEV, -1, g.shape[1]).astype(BF16)


def _by_core(g):
    return jnp.transpose(g.reshape(N_DEV // 2, 2, -1, g.shape[1]), (1, 0, 2, 3)).astype(BF16)


DT_ROW = 3072


def _chip_sums(tag, by_core, swap_in=None):
    arrived = swap_in(by_core) if swap_in else _exchange(tag + "_swap", "swap", by_core)
    return [_pair_sum(f"{tag}_chipsum{i}", m, a) for i, (m, a) in enumerate(zip(by_core, arrived))]


def _ffn_fwd_gu(tag, n, w_gu_t, comm=None):
    M = n.shape[0]
    F = w_gu_t.shape[0] // 2
    tm = _tile(M, (544, 256))
    outs = _fused_matmul(
        tag + "_gu", M, F, D_MODEL,
        [dict(a=n, b=w_gu_t, trans_b=True, acc=0, resident=True),
         dict(a=n, b=w_gu_t, trans_b=True, bn_off=1, acc=1, resident=True)], [],
        lambda accs, ex: (accs[0], accs[1], accs[0] * _sigmoid(accs[0]) * accs[1]),
        [BF16, BF16, BF16], 2, tm, F, D_MODEL, outer="i", comm=comm, sub=256)
    return (n, *outs[:3]), outs[3:]


def _rmsnorm_tile(x, w):
    return x * lax.rsqrt(jnp.mean(x * x, axis=-1, keepdims=True) + EPS) * w


def _ffn_fwd_down(tag, h, a, w_down, next_norm=None, comm=None):
    M = h.shape[0]
    F = w_down.shape[0]
    tm = _tile(M, (1088, 544, 256))
    if next_norm is None:
        (h_out,) = _fused_matmul(
            tag + "_down", M, D_MODEL, F, [dict(a=a, b=w_down, acc=0)], [(h, 0)],
            lambda accs, ex: (ex[0] + 0.5 * accs[0],), [F32], 1, tm, D_MODEL, F, outer="j", sub=256)
        return h_out

    def with_norm(accs, ex):
        h_new = ex[0] + 0.5 * accs[0]
        return h_new, _rmsnorm_tile(h_new, ex[1])

    return _fused_matmul(tag + "_down", M, D_MODEL, F, [dict(a=a, b=w_down, acc=0, resident=True)], [(h, 0)], with_norm,
                         [F32, BF16], 1, tm, D_MODEL, F, outer="j", vecs=[next_norm], comm=comm)


def _ffn_bwd(tag, dh, dh_b, h, norm_w, w_gu_t, w_down, saved, scatter=False):
    n, g, u, a = saved
    M = h.shape[0]
    F = w_down.shape[0]
    tm = _tile(M, (544, 256))
    tn = _tile(F, (1408, 704, 256))

    def swiglu_bwd(accs, ex):
        da, gv, uv = 0.5 * accs[0], ex[0].astype(F32), ex[1].astype(F32)
        s = _sigmoid(gv)
        return da * uv * _dsilu(gv, s), da * gv * s

    (dgu,) = _fused_matmul(
        tag + "_dact", M, F, D_MODEL, [dict(a=dh_b, b=w_down, trans_b=True, acc=0, resident=True)], [(g, 0), (u, 0)],
        swiglu_bwd, [BF16, BF16], 1, tm, F, D_MODEL, outer="i", stack=True, sub=256)
    tr = _tile(M, (2176, 256))
    (dw_down,) = _matmul_tn(tag + "_dwd", a, dh_b, tn, D_MODEL, tr, scale=0.5)
    dw_gu_t, *p_down = _matmul_tn(tag + "_dwgu", dgu, n, tn, D_MODEL, tr,
                                  comm=("scatter", [_to_rows(dw_down)]) if scatter else None)
    comm = None
    if scatter:
        comm = ("chips", _chip_sums(tag + "_wgu", [_by_core(dw_gu_t)]))
    def norm_bwd(accs, ex):
        dh_prev, dw = _rmsnorm_bwd_tile(accs[0], ex[0], ex[2], ex[1])
        return dh_prev, dh_prev, dw

    dh_prev, dh_prev_b, dnorm, *p_gu = _fused_matmul(
        tag + "_dn", M, D_MODEL, 2 * F,
        [dict(a=dgu, b=w_gu_t, acc=0, resident=True)], [(h, 0), (dh, 0)],
        norm_bwd, [F32, BF16], 1, tm, D_MODEL, 2 * F, outer="i", comm=comm, vecs=[norm_w], row_sums=1)
    return (dh_prev, dh_prev_b, dnorm, *((p_gu[0], p_down[0]) if scatter else (dw_gu_t, dw_down)))


def kernel(x, meta_tokens, ffn1_norm, ffn1_w_gu, ffn1_w_down, mix_norm, w_in, ssd_conv_w, ssd_conv_b, ssd_dt_bias, ssd_a_log, ssd_d, ssd_norm, hg_lower_bound, hg_norm, w_branch_a, w_branch_b, w_out, ffn2_norm, ffn2_w_gu, ffn2_w_down, final_norm, loss_target, m_meta_tokens, m_ffn1_norm, m_ffn1_w_gu, m_ffn1_w_down, m_mix_norm, m_w_in, m_ssd_conv_w, m_ssd_conv_b, m_ssd_dt_bias, m_ssd_a_log, m_ssd_d, m_ssd_norm, m_hg_lower_bound, m_hg_norm, m_w_branch_a, m_w_branch_b, m_w_out, m_ffn2_norm, m_ffn2_w_gu, m_ffn2_w_down, m_final_norm, v_meta_tokens, v_ffn1_norm, v_ffn1_w_gu, v_ffn1_w_down, v_mix_norm, v_w_in, v_ssd_conv_w, v_ssd_conv_b, v_ssd_dt_bias, v_ssd_a_log, v_ssd_d, v_ssd_norm, v_hg_lower_bound, v_hg_norm, v_w_branch_a, v_w_branch_b, v_w_out, v_ffn2_norm, v_ffn2_w_gu, v_ffn2_w_down, v_final_norm):
    Bl, S, D = x.shape
    T = PAD + N_META + S
    nc = T // Q
    M = Bl * T
    me = 4 * lax.axis_index("x") + 2 * lax.axis_index("y") + lax.axis_index("c")

    bf = lambda a: a[0].astype(BF16)
    bft = lambda a: a[0].T.astype(BF16)
    g_meta, g_conv_w = _exchange("gather_small", "gather", [meta_tokens, ssd_conv_w[0]])
    meta_full, conv_w_full = _cols(g_meta), _cols(g_conv_w)
    bias_p, alog_p, d_p = _heads_to_lanes(ssd_dt_bias), _heads_to_lanes(ssd_a_log), _heads_to_lanes(ssd_d)
    final_w = final_norm.reshape(1, D)

    h0, n1, g_wgu1 = _embed_norm(x, meta_full, ffn1_norm, comm=("gather", [bft(ffn1_w_gu)]))
    wgu1 = _rows(g_wgu1)
    tm = _tile(M, (1088, 544, 256))
    win_shard = bft(w_in)
    cut = (win_shard.shape[0] // 32) * 16
    ffn1_saved, (g_wd1, g_win_a) = _ffn_fwd_gu("ffn1", n1, wgu1, comm=("gather", [bf(ffn1_w_down), win_shard[:cut]]))
    wd1 = _rows(g_wd1)
    h1, un, g_win_b = _ffn_fwd_down("ffn1", h0, ffn1_saved[3], wd1, next_norm=mix_norm,
                                    comm=("gather", [win_shard[cut:]]))
    win_t = _rows(jnp.concatenate([g_win_a, g_win_b], axis=1))
    win_dt = jnp.pad(win_t[DT_ROW:DT_ROW + SSD_HEADS], ((0, 128 - SSD_HEADS), (0, 0)))
    plain = lambda accs, ex: (accs[0],)
    proj, g_wa, g_wb, g_wo = _fused_matmul(
        "in_proj", M, N_MAIN, D, [dict(a=un, b=win_t, trans_b=True, acc=0, b_shift=(DT_ROW // 1536, SSD_HEADS))], [],
        plain, [BF16], 1, tm, 1536, D,
        outer="j", comm=("gather", [bf(w_branch_a), bf(w_branch_b), bf(w_out)]), sub=512)
    wa, wb, wo = _rows(g_wa), _rows(g_wb), _rows(g_wo)
    (dtr,) = _fused_matmul("in_proj_dt", M, 128, D, [dict(a=un, b=win_dt, trans_b=True, acc=0)], [], plain, [F32], 1,
                           tm, 128, D, outer="j")
    xc = _conv_fwd(proj, conv_w_full, ssd_conv_b, Bl, T)
    ya, ssd_prev = _ssd_fwd(xc, dtr, proj, bias_p, alog_p, d_p, ssd_norm, Bl, nc)
    yb, hg_o, hg_st, g_wgu2, g_wd2 = _hgrn_fwd(proj, hg_lower_bound, hg_norm, Bl, nc,
                                               comm=("gather", [bft(ffn2_w_gu), bf(ffn2_w_down)]))
    wgu2, wd2 = _rows(g_wgu2), _rows(g_wd2)

    def branch_fwd(accs, ex):
        pa, pb = accs
        return pa, pb, _sigmoid(ex[0].astype(F32)) * pa + _sigmoid(ex[1].astype(F32)) * pb

    pa, pb, merged = _fused_matmul(
        "branches", M, D, D, [dict(a=ya, b=wa, acc=0), dict(a=yb, b=wb, acc=1)], [(proj, 7), (proj, 8)],
        branch_fwd, [BF16, BF16, BF16], 2, tm, D, D, outer="j")
    def out_with_norm(accs, ex):
        h_new = ex[0] + accs[0]
        return h_new, _rmsnorm_tile(h_new, ex[1])

    h2, n2 = _fused_matmul("out_proj", M, D, D, [dict(a=merged, b=wo, acc=0)], [(h1, 0)], out_with_norm,
                           [F32, BF16], 1, tm, D, D, outer="j", vecs=[ffn2_norm])
    ffn2_saved, _ = _ffn_fwd_gu("ffn2", n2, wgu2)
    h3 = _ffn_fwd_down("ffn2", h2, ffn2_saved[3], wd2)

    dh3, dh3_b, d_final, loss_part = _loss_head(h3, final_w, loss_target, Bl, nc)
    dh2, dh2_b, d_ffn2_norm, d_wgu2, d_wd2 = _ffn_bwd("ffn2", dh3, dh3_b, h2, ffn2_norm, wgu2, wd2, ffn2_saved)

    def branch_bwd(accs, ex):
        dm = accs[0]
        ga, gb, pav, pbv = (e.astype(F32) for e in ex)
        sa, sb = _sigmoid(ga), _sigmoid(gb)
        return (dm * sa, dm * sb,
                jnp.concatenate([dm * pav * sa * (1.0 - sa), dm * pbv * sb * (1.0 - sb)], axis=1))

    d_merged_outs = []

    def d_merged_with_swap(theirs):
        d_merged_outs.extend(_fused_matmul(
            "d_merged", M, D, D, [dict(a=dh2_b, b=wo, trans_b=True, acc=0)], [(proj, 7), (proj, 8), (pa, 0), (pb, 0)],
            branch_bwd, [BF16] * 2, 1, tm, D, D, outer="j", comm=("swap", theirs),
            wide=dict(width=2 * D, col=7 * D, total=N_MAIN, dtype=BF16)))
        return d_merged_outs[3:]

    s_ffn2 = _chip_sums("ffn2", [_by_core(d_wgu2), _by_core(d_wd2)], swap_in=d_merged_with_swap)
    dpa, dpb, dproj = d_merged_outs[:3]
    (d_wo,) = _matmul_tn("d_w_out", merged, dh2_b, 512, D, M)
    (d_wa,) = _matmul_tn("d_w_a", ya, dpa, 512, D, M)
    (d_wb,) = _matmul_tn("d_w_b", yb, dpb, 512, D, M)
    dya, dyb = _fused_matmul(
        "d_branches", M, D, D, [dict(a=dpa, b=wa, trans_b=True, acc=0), dict(a=dpb, b=wb, trans_b=True, acc=1)], [],
        lambda accs, ex: (accs[0], accs[1]), [BF16, BF16], 2, tm, D, D, outer="j")
    *ssd_grads, p_wgu2, p_wd2 = _ssd_bwd(xc, dtr, proj, bias_p, alog_p, d_p, ssd_norm, ssd_prev, dya, dproj, Bl, nc,
                                         comm=("chips", s_ffn2))
    dxc, dproj, ddtr, d_bias_p, d_alog_p, d_d_p, d_ssd_norm = ssd_grads
    dproj, d_conv_w, d_conv_b = _conv_bwd(proj, conv_w_full, ssd_conv_b, dxc, dproj, Bl, T)
    dproj, d_hb, d_hg_norm, p_wa, p_wb, p_wo = _hgrn_bwd(
        proj, hg_lower_bound, hg_norm, hg_o, hg_st, dyb, dproj, Bl, nc,
        comm=("scatter", [_to_rows(d_wa), _to_rows(d_wb), _to_rows(d_wo)]))
    ddtr_b = ddtr.astype(BF16)
    (d_win_t,) = _matmul_tn("d_w_in", dproj, un, 768, D, M, out_skip=(DT_ROW, SSD_HEADS))
    (d_win_dt,) = _matmul_tn("d_w_in_dt", ddtr_b, un, 128, D, M)
    d_win_t = lax.dynamic_update_slice(d_win_t, d_win_dt[:SSD_HEADS], (DT_ROW, 0))
    d_un_dt_outs = []

    def d_un_dt_with_swap(theirs):
        d_un_dt_outs.extend(_fused_matmul("d_un_dt", M, D, 128, [dict(a=ddtr_b, b=win_dt, acc=0)], [], plain, [F32], 1,
                                          tm, D, 128, outer="j", comm=("swap", theirs)))
        return d_un_dt_outs[1:]

    s_win = _chip_sums("w_in", [_by_core(d_win_t)], swap_in=d_un_dt_with_swap)
    def mix_norm_bwd(accs, ex):
        dh, dw = _rmsnorm_bwd_tile(accs[0] + ex[0], ex[1], ex[3], ex[2])
        return dh, dh, dw

    dh1, dh1_b, d_mix_norm, p_win = _fused_matmul(
        "d_un", M, D, N_MAIN, [dict(a=dproj, b=win_t, acc=0, b_shift=(DT_ROW // 3072, SSD_HEADS))],
        [(d_un_dt_outs[0], 0), (h1, 0), (dh2, 0)],
        mix_norm_bwd, [F32, BF16], 1, _tile(M, (544, 256)), D, 3072, outer="i", comm=("chips", s_win),
        vecs=[mix_norm], row_sums=1)
    dh0, _, d_ffn1_norm, p_wgu1, p_wd1 = _ffn_bwd("ffn1", dh1, dh1_b, h0, ffn1_norm, wgu1, wd1, ffn1_saved, scatter=True)

    dh0 = dh0.reshape(Bl, T, D)
    grad_x = dh0[:, PAD + N_META:]
    d_meta = dh0[:, PAD:PAD + N_META]

    small_grads = [d_ffn1_norm, d_mix_norm, d_conv_b, _lanes_to_heads(d_bias_p), _lanes_to_heads(d_alog_p),
                   _lanes_to_heads(d_d_p), d_ssd_norm, d_hb, d_hg_norm, d_ffn2_norm, d_final.reshape(D), d_conv_w]
    small_like = small_grads + [d_meta[b] for b in range(Bl)] + [loss_part[0, 0:1]]
    small_packed = _pack_rows(small_like)
    parts = [p_wgu1, p_wd1, p_win, p_wa, p_wb, p_wo, p_wgu2, p_wd2]

    names = ["meta_tokens", "ffn1_norm", "ffn1_w_gu", "ffn1_w_down", "mix_norm", "w_in", "ssd_conv_w", "ssd_conv_b",
             "ssd_dt_bias", "ssd_a_log", "ssd_d", "ssd_norm", "hg_lower_bound", "hg_norm", "w_branch_a", "w_branch_b",
             "w_out", "ffn2_norm", "ffn2_w_gu", "ffn2_w_down", "final_norm"]
    W = dict(meta_tokens=meta_tokens, ffn1_norm=ffn1_norm, ffn1_w_gu=ffn1_w_gu, ffn1_w_down=ffn1_w_down, mix_norm=mix_norm,
             w_in=w_in, ssd_conv_w=ssd_conv_w, ssd_conv_b=ssd_conv_b, ssd_dt_bias=ssd_dt_bias, ssd_a_log=ssd_a_log,
             ssd_d=ssd_d, ssd_norm=ssd_norm, hg_lower_bound=hg_lower_bound, hg_norm=hg_norm, w_branch_a=w_branch_a,
             w_branch_b=w_branch_b, w_out=w_out, ffn2_norm=ffn2_norm, ffn2_w_gu=ffn2_w_gu, ffn2_w_down=ffn2_w_down,
             final_norm=final_norm)
    Mo = dict(meta_tokens=m_meta_tokens, ffn1_norm=m_ffn1_norm, ffn1_w_gu=m_ffn1_w_gu, ffn1_w_down=m_ffn1_w_down,
              mix_norm=m_mix_norm, w_in=m_w_in, ssd_conv_w=m_ssd_conv_w, ssd_conv_b=m_ssd_conv_b, ssd_dt_bias=m_ssd_dt_bias,
              ssd_a_log=m_ssd_a_log, ssd_d=m_ssd_d, ssd_norm=m_ssd_norm, hg_lower_bound=m_hg_lower_bound, hg_norm=m_hg_norm,
              w_branch_a=m_w_branch_a, w_branch_b=m_w_branch_b, w_out=m_w_out, ffn2_norm=m_ffn2_norm, ffn2_w_gu=m_ffn2_w_gu,
              ffn2_w_down=m_ffn2_w_down, final_norm=m_final_norm)
    Vo = dict(meta_tokens=v_meta_tokens, ffn1_norm=v_ffn1_norm, ffn1_w_gu=v_ffn1_w_gu, ffn1_w_down=v_ffn1_w_down,
              mix_norm=v_mix_norm, w_in=v_w_in, ssd_conv_w=v_ssd_conv_w, ssd_conv_b=v_ssd_conv_b, ssd_dt_bias=v_ssd_dt_bias,
              ssd_a_log=v_ssd_a_log, ssd_d=v_ssd_d, ssd_norm=v_ssd_norm, hg_lower_bound=v_hg_lower_bound, hg_norm=v_hg_norm,
              w_branch_a=v_w_branch_a, w_branch_b=v_w_branch_b, w_out=v_w_out, ffn2_norm=v_ffn2_norm, ffn2_w_gu=v_ffn2_w_gu,
              ffn2_w_down=v_ffn2_w_down, final_norm=v_final_norm)
    grads, deltas, new_m, new_v = {}, {}, {}, {}
    big_names = ["ffn1_w_gu", "ffn1_w_down", "w_in", "w_branch_a", "w_branch_b", "w_out", "ffn2_w_gu", "ffn2_w_down"]
    transposed = ("ffn1_w_gu", "ffn2_w_gu", "w_in")
    small_all = None
    for nm, part in zip(big_names, parts):
        view = (lambda a: a[0].T) if nm in transposed else (lambda a: a[0])
        back = (lambda o: o.T[None]) if nm in transposed else (lambda o: o[None])
        outs = _adamw("adamw_" + nm, part, view(W[nm]), view(Mo[nm]), view(Vo[nm]),
                      comm=("gather", [small_packed]) if small_all is None else None)
        if small_all is None:
            small_all = outs[4]
        grads[nm], deltas[nm], new_m[nm], new_v[nm] = (back(o) for o in outs[:4])
    unpacked = _unpack_rows(_sum_parts("sum_small_grads", small_all), small_like)
    g_small = unpacked[:len(small_grads)]
    g_meta_full = unpacked[len(small_grads)]
    for b in range(1, Bl):
        g_meta_full = g_meta_full + unpacked[len(small_grads) + b]
    g_meta = lax.dynamic_slice_in_dim(g_meta_full, me * (D // N_DEV), D // N_DEV, axis=1)
    g_conv_w = lax.dynamic_slice_in_dim(g_small[11], me * (SSD_CONV_CH // N_DEV), SSD_CONV_CH // N_DEV, axis=1)
    loss = unpacked[-1].reshape(())
    small_names = ["ffn1_norm", "mix_norm", "ssd_conv_b", "ssd_dt_bias", "ssd_a_log", "ssd_d", "ssd_norm", "hg_lower_bound",
                   "hg_norm", "ffn2_norm", "final_norm", "ssd_conv_w", "meta_tokens"]
    small_g = g_small[:11] + [g_conv_w.reshape(ssd_conv_w.shape), g_meta]
    pk = lambda d: _pack_rows([d[nm] for nm in small_names])
    outs = _adamw("adamw_small", _pack_rows(small_g)[None], pk(W), pk(Mo), pk(Vo))
    like = [W[nm] for nm in small_names]
    for dst, o in zip((grads, deltas, new_m, new_v), outs):
        for nm, val in zip(small_names, _unpack_rows(o, like)):
            dst[nm] = val

    return (loss, grad_x, *[grads[nm] for nm in names], *[deltas[nm] for nm in names],
            *[new_m[nm] for nm in names], *[new_v[nm] for nm in names])
```

```python
import functools

import jax
import jax.numpy as jnp
from jax import lax
from jax.experimental import pallas as pl
from jax.experimental.pallas import tpu as pltpu

F32, BF16 = jnp.float32, jnp.bfloat16
NN, NT, TN = ((1,), (0,)), ((1,), (1,)), ((0,), (0,))
MESH_AXES = ("x", "y", "c")
N_DEV = 8

D_MODEL = 1024
N_META = 16
EPS = 1e-6
SSD_HEADS, SSD_HEAD_DIM, SSD_GROUPS, SSD_STATE, SSD_CONV, Q = 16, 64, 4, 128, 4, 128
SSD_INNER = SSD_HEADS * SSD_HEAD_DIM
SSD_CONV_CH = SSD_INNER + 2 * SSD_GROUPS * SSD_STATE
HG_WIDTH, HG_HEADS, HG_CHUNK = 1024, 8, 16
PAD = Q - N_META
N_MAIN = 9 * 1024
ADAM_LR, ADAM_B1, ADAM_B2, ADAM_EPS, ADAM_WD, ADAM_STEP = 0.001, 0.9, 0.999, 1e-08, 0.01, 10
VMEM_LIMIT = 52 * 1024 * 1024


def _dot(a, b, dims):
    return lax.dot_general(a, b, (dims, ((), ())), preferred_element_type=F32)


def _dot01(a, b, dims, sel):
    x = b if sel == "a" else a
    hi = x.astype(BF16)
    r1 = x - hi.astype(F32)
    mid = r1.astype(BF16)
    lo = (r1 - mid.astype(F32)).astype(BF16)
    s = (a if sel == "a" else b).astype(BF16)
    parts = [_dot(s, p, dims) if sel == "a" else _dot(p, s, dims) for p in (hi, mid, lo)]
    return parts[0] + parts[1] + parts[2]


def _sigmoid(x):
    return 1.0 / (1.0 + jnp.exp(-x))


def _dsilu(x, s):
    return s * (1.0 + x * (1.0 - s))


def _softplus(x):
    e = jnp.exp(-jnp.abs(x))
    u = 1.0 + e
    log1p_e = jnp.where(u == 1.0, e, jnp.log(u) * e / (u - 1.0))
    return jnp.maximum(x, 0.0) + log1p_e


def _params(sem):
    return pltpu.CompilerParams(dimension_semantics=sem, vmem_limit_bytes=VMEM_LIMIT)


def _tile(n, prefs):
    for p in prefs:
        if n % p == 0:
            return p
    return n


CHIP_FLIPS = ((1, 0), (0, 1), (1, 1))
N_PEER = N_DEV - 1


def _comm_gather(srcs, outs, send_sems, recv_sems, local_sems):
    n = len(srcs)
    x, y, c = (lax.axis_index(a) for a in MESH_AXES)
    dev = lambda px, py, pc: 4 * px + 2 * py + pc
    me, sib = dev(x, y, c), (x, y, 1 - c)
    nbr_x, nbr_y, diag = (1 - x, y), (x, 1 - y), (1 - x, 1 - y)
    via = (x ^ c, y ^ (1 - c), c)
    sent_on = dev(x ^ (1 - c), y ^ c, c)

    def rc(w, k, slot, to, src=None):
        return pltpu.make_async_remote_copy(
            src_ref=outs[w].at[slot] if src is None else src, dst_ref=outs[w].at[slot],
            send_sem=send_sems.at[w, k], recv_sem=recv_sems.at[w, k], device_id=to, device_id_type=pl.DeviceIdType.MESH)

    def local(w):
        return pltpu.make_async_copy(srcs[w], outs[w].at[me], local_sems.at[w])

    def start():
        for w in range(n):
            local(w).start()
            rc(w, 0, me, sib, src=srcs[w]).start()
            rc(w, 1, me, (*nbr_x, c), src=srcs[w]).start()
            rc(w, 2, me, (*nbr_y, c), src=srcs[w]).start()

    def pass_on():
        for w in range(n):
            rc(w, 1, dev(*nbr_x, c), sib).wait_recv()
            rc(w, 2, dev(*nbr_y, c), sib).wait_recv()
            rc(w, 3, sent_on, via).start()
            rc(w, 4, dev(*nbr_x, c), sib).start()
            rc(w, 5, dev(*nbr_y, c), sib).start()

    def pass_on_diagonal():
        for w in range(n):
            rc(w, 3, dev(*diag, c), sib).wait_recv()
            rc(w, 6, dev(*diag, c), sib).start()

    def finish():
        for w in range(n):
            rc(w, 0, dev(x, y, 1 - c), sib).wait_recv()
            for k, chip in ((4, nbr_x), (5, nbr_y), (6, diag)):
                rc(w, k, dev(*chip, 1 - c), sib).wait_recv()
            for k in range(N_PEER):
                rc(w, k, me, sib, src=srcs[w]).wait_send()
            local(w).wait()

    return start, (pass_on, pass_on_diagonal), finish


def _comm_scatter(srcs, outs, send_sems, recv_sems, local_sems):
    n = len(srcs)
    x, y, c = (lax.axis_index(a) for a in MESH_AXES)
    me = 4 * x + 2 * y + c

    def copies():
        out = []
        for w in range(n):
            out.append(pltpu.make_async_copy(srcs[w].at[me], outs[w].at[me], local_sems.at[w]))
            for k in range(1, N_DEV):
                px, py, pc = x ^ (k >> 2), y ^ ((k >> 1) & 1), c ^ (k & 1)
                out.append(pltpu.make_async_remote_copy(
                    src_ref=srcs[w].at[4 * px + 2 * py + pc], dst_ref=outs[w].at[me],
                    send_sem=send_sems.at[w, k - 1], recv_sem=recv_sems.at[w, k - 1],
                    device_id=(px, py, pc), device_id_type=pl.DeviceIdType.MESH))
        return out

    def start():
        for cp in copies():
            cp.start()

    def finish():
        for cp in copies():
            cp.wait()

    return start, None, finish


def _comm_swap(srcs, outs, send_sems, recv_sems, local_sems):
    x, y, c = (lax.axis_index(a) for a in MESH_AXES)

    def copies():
        return [pltpu.make_async_remote_copy(
            src_ref=srcs[w].at[1 - c], dst_ref=outs[w], send_sem=send_sems.at[w, 0], recv_sem=recv_sems.at[w, 0],
            device_id=(x, y, 1 - c), device_id_type=pl.DeviceIdType.MESH) for w in range(len(srcs))]

    def start():
        for cp in copies():
            cp.start()

    def finish():
        for cp in copies():
            cp.wait()

    return start, None, finish


def _comm_chips(srcs, outs, send_sems, recv_sems, local_sems):
    n = len(srcs)
    x, y, c = (lax.axis_index(a) for a in MESH_AXES)
    mine = 2 * x + y

    def copies():
        out = []
        for w in range(n):
            out.append(pltpu.make_async_copy(srcs[w].at[mine], outs[w].at[mine], local_sems.at[w]))
            for j, (fx, fy) in enumerate(CHIP_FLIPS):
                px, py = x ^ fx, y ^ fy
                out.append(pltpu.make_async_remote_copy(
                    src_ref=srcs[w].at[2 * px + py], dst_ref=outs[w].at[mine],
                    send_sem=send_sems.at[w, j], recv_sem=recv_sems.at[w, j],
                    device_id=(px, py, c), device_id_type=pl.DeviceIdType.MESH))
        return out

    def start():
        for cp in copies():
            cp.start()

    def finish():
        for cp in copies():
            cp.wait()

    return start, None, finish


def _comm_parts(comm):
    kind, arrays = comm[:2]
    n = len(arrays)
    lead = {"gather": lambda a: (N_DEV,) + a.shape, "scatter": lambda a: (N_DEV,) + a.shape[1:],
            "swap": lambda a: a.shape[1:], "chips": lambda a: a.shape}[kind]
    shapes = [jax.ShapeDtypeStruct(lead(a), a.dtype) for a in arrays]
    sems = [pltpu.SemaphoreType.DMA((n, N_PEER)), pltpu.SemaphoreType.DMA((n, N_PEER)), pltpu.SemaphoreType.DMA((n,))]
    make = {"gather": _comm_gather, "scatter": _comm_scatter, "swap": _comm_swap, "chips": _comm_chips}[kind]
    return n, shapes, sems, make


def _exchange(name, kind, arrays):
    n, shapes, sems, make = _comm_parts((kind, arrays))

    def body(*refs):
        start, middle, finish = make(refs[:n], refs[n:2 * n], *refs[2 * n:])
        start()
        for stage in middle or ():
            stage()
        finish()

    any_spec = pl.BlockSpec(memory_space=pl.ANY)
    return pl.pallas_call(
        body, name=name, in_specs=[any_spec] * n, out_specs=[any_spec] * n, out_shape=shapes, scratch_shapes=sems,
        compiler_params=pltpu.CompilerParams(has_side_effects=True),
    )(*arrays)


def _call(body, *, name, grid, in_specs, out_specs, out_shape, scratch, sem, args, comm=None, into=None):
    any_spec = pl.BlockSpec(memory_space=pl.ANY)
    in_specs, args, aliases, n_body_in = list(in_specs), list(args), {}, len(in_specs)
    if into is not None:
        in_specs.append(any_spec)
        args.append(into[0])
        aliases = {n_body_in: into[1]}
    n_in, n_out, n_scr = len(in_specs), len(out_specs), len(scratch)
    if comm is None:
        def plain(*refs):
            body(*refs[:n_body_in], *refs[n_in:])

        return pl.pallas_call(plain, name=name, grid=grid, in_specs=in_specs, out_specs=out_specs, out_shape=out_shape,
                              scratch_shapes=scratch, input_output_aliases=aliases, compiler_params=_params(sem))(*args)
    n, shapes, sems, make = _comm_parts(comm)

    def carrier(*refs):
        ins, csrc = refs[:n_body_in], refs[n_in:n_in + n]
        outs, cout = refs[n_in + n:n_in + n + n_out], refs[n_in + n + n_out:n_in + 2 * n + n_out]
        rest = refs[n_in + 2 * n + n_out:]
        start, middle, finish = make(csrc, cout, *rest[n_scr:])
        ids = [pl.program_id(a) for a in range(len(grid))]
        step = functools.reduce(lambda acc, ig: acc * ig[1] + ig[0], zip(ids, grid), 0)
        n_steps = functools.reduce(lambda a, b: a * b, grid, 1)
        pl.when(step == 0)(start)
        body(*ins, *outs, *rest[:n_scr])
        if middle:
            pl.when(step == max(0, (3 * n_steps) // 4 - 1))(middle[0])
            pl.when(step == n_steps - 1)(middle[1])
        pl.when(step == n_steps - 1)(finish)

    return pl.pallas_call(
        carrier, name=name, grid=grid, in_specs=in_specs + [any_spec] * n,
        out_specs=list(out_specs) + [any_spec] * n, out_shape=list(out_shape) + shapes,
        scratch_shapes=list(scratch) + sems, input_output_aliases=aliases,
        compiler_params=pltpu.CompilerParams(dimension_semantics=("arbitrary",) * len(grid),
                                             vmem_limit_bytes=VMEM_LIMIT, has_side_effects=True),
    )(*args, *comm[1])


def _fused_matmul(name, M, N, K, pairs, extras, epilogue, out_dtypes, n_acc, tm, tn, tk, outer="i", comm=None,
                  stack=False, vecs=(), row_sums=0, wide=None, sub=None):
    nk = K // tk
    n_pairs, n_ex, n_out = len(pairs), len(extras), len(out_dtypes)
    assert not row_sums or (outer == "i" and N == tn)

    def ij(g0, g1):
        return (g0, g1) if outer == "i" else (g1, g0)

    in_specs, args = [], []
    for p in pairs:
        ao, bk, bn = p.get("a_off", 0), p.get("bk_off", 0), p.get("bn_off", 0)
        mode = dict(pipeline_mode=pl.Buffered(1)) if p.get("resident") else {}
        in_specs.append(pl.BlockSpec((tm, tk), lambda g0, g1, k, ao=ao: (ij(g0, g1)[0], k + ao)))
        if "b_shift" in p:
            first, shift = p["b_shift"]
            if p.get("trans_b"):
                in_specs.append(pl.BlockSpec(
                    (pl.Element(tn), pl.Element(tk)),
                    lambda g0, g1, k, bk=bk: (
                        pl.multiple_of(ij(g0, g1)[1] * tn + jnp.where(ij(g0, g1)[1] >= first, shift, 0), 16),
                        (k + bk) * tk)))
            else:
                in_specs.append(pl.BlockSpec(
                    (pl.Element(tk), pl.Element(tn)),
                    lambda g0, g1, k, bn=bn: (pl.multiple_of(k * tk + jnp.where(k >= first, shift, 0), 16),
                                              (ij(g0, g1)[1] + bn) * tn)))
        elif p.get("trans_b"):
            in_specs.append(pl.BlockSpec((tn, tk), lambda g0, g1, k, bk=bk, bn=bn: (ij(g0, g1)[1] + bn, k + bk), **mode))
        else:
            in_specs.append(pl.BlockSpec((tk, tn), lambda g0, g1, k, bk=bk, bn=bn: (k + bk, ij(g0, g1)[1] + bn), **mode))
        args += [p["a"], p["b"]]
    for arr, off in extras:
        in_specs.append(pl.BlockSpec((tm, tn), lambda g0, g1, k, off=off: (ij(g0, g1)[0], ij(g0, g1)[1] + off)))
        args.append(arr)
    for arr in vecs:
        in_specs.append(pl.BlockSpec((1, tn), lambda g0, g1, k: (0, ij(g0, g1)[1])))
        args.append(arr)
    if stack:
        assert N == tn
        out_specs = [pl.BlockSpec((tm, n_out * tn), lambda g0, g1, k: (ij(g0, g1)[0], 0))]
        out_shape = [jax.ShapeDtypeStruct((M, n_out * N), out_dtypes[0])]
    else:
        out_specs = [pl.BlockSpec((tm, tn), lambda g0, g1, k: ij(g0, g1)) for _ in out_dtypes]
        out_shape = [jax.ShapeDtypeStruct((M, N), dt) for dt in out_dtypes]
    if wide:
        out_specs.append(pl.BlockSpec((pl.Element(tm), pl.Element(wide["width"])),
                                      lambda g0, g1, k: (pl.multiple_of(ij(g0, g1)[0] * tm, 16), wide["col"])))
        out_shape.append(jax.ShapeDtypeStruct((M, wide["total"]), wide["dtype"]))
    n_tile_out = len(out_specs)
    out_specs += [pl.BlockSpec((1, tn), lambda g0, g1, k: (0, 0)) for _ in range(row_sums)]
    out_shape += [jax.ShapeDtypeStruct((1, N), F32) for _ in range(row_sums)]
    grid = (M // tm, N // tn, nk) if outer == "i" else (N // tn, M // tm, nk)
    n_in = 2 * n_pairs + n_ex + len(vecs)

    def partials(refs, cs=slice(None)):
        accs = [None] * n_acc
        for idx, p in enumerate(pairs):
            b_ref = refs[2 * idx + 1]
            d = (_dot(refs[2 * idx][...], b_ref[cs, :], NT) if p.get("trans_b")
                 else _dot(refs[2 * idx][...], b_ref[:, cs], NN))
            accs[p["acc"]] = d if accs[p["acc"]] is None else accs[p["acc"]] + d
        return accs

    def finish(accs, refs, first_rows, cs=slice(None)):
        res = epilogue(accs, [r[:, cs] for r in refs[2 * n_pairs:n_in]])
        if stack:
            o = refs[n_in]
            for idx in range(n_out):
                lo = idx * tn + (cs.start or 0)
                o[:, lo:lo + (tn if cs.stop is None else cs.stop - cs.start)] = res[idx].astype(o.dtype)
        else:
            for o, r in zip(refs[n_in:n_in + n_out], res):
                o[:, cs] = r.astype(o.dtype)
        if wide:
            o = refs[n_in + n_tile_out - 1]
            o[...] = res[n_out].astype(o.dtype)
        for o, r in zip(refs[n_in + n_tile_out:n_in + n_tile_out + row_sums], res[n_out + bool(wide):]):
            @pl.when(first_rows)
            def _(o=o, r=r):
                o[...] = r

            @pl.when(jnp.logical_not(first_rows))
            def _(o=o, r=r):
                o[...] += r

    if nk == 1 and sub:
        assert not wide and not row_sums and tn % sub == 0

        def body(*refs):
            for c in range(tn // sub):
                cs = slice(c * sub, (c + 1) * sub)
                finish(partials(refs, cs), refs, None, cs)
        scratch = []
    elif nk == 1:
        def body(*refs):
            finish(partials(refs), refs, pl.program_id(0) == 0)
        scratch = []
    else:
        def body(*refs):
            acc_refs = refs[-n_acc:]
            k = pl.program_id(2)
            first_rows = pl.program_id(0) == 0
            new = partials(refs)

            @pl.when(k == 0)
            def _():
                for a, v in zip(acc_refs, new):
                    a[...] = v

            @pl.when(k > 0)
            def _():
                for a, v in zip(acc_refs, new):
                    a[...] += v

            @pl.when(k == nk - 1)
            def _():
                finish([a[...] for a in acc_refs], refs, first_rows)
        scratch = [pltpu.VMEM((tm, tn), F32) for _ in range(n_acc)]

    return _call(body, name=name, grid=grid, in_specs=in_specs, out_specs=out_specs, out_shape=out_shape,
                 scratch=scratch, sem=("parallel", "parallel", "arbitrary"), args=args, comm=comm)


def _matmul_tn(name, x, y, t1, t2, tr, scale=1.0, comm=None, out_dtype=BF16, out_skip=None):
    R, K1 = x.shape
    N1 = y.shape[1]
    nr, n1 = R // tr, K1 // t1
    x_spec = pl.BlockSpec((tr, t1), lambda i, j, r: (r, i))
    rows_out = K1
    o_spec = pl.BlockSpec((t1, t2), lambda i, j, r: (i, j))
    if out_skip:
        row, count = out_skip
        rows_out += count
        o_spec = pl.BlockSpec(
            (pl.Element(t1), pl.Element(t2)),
            lambda i, j, r: (pl.multiple_of(i * t1 + jnp.where(i * t1 >= row, count, 0), 16), j * t2))

    def body(x_ref, y_ref, o_ref, *acc):
        d = _dot(x_ref[...], y_ref[...], TN)
        if nr == 1:
            o_ref[...] = (d * scale).astype(o_ref.dtype)
            return
        r = pl.program_id(2)

        @pl.when(r == 0)
        def _():
            acc[0][...] = d

        @pl.when(jnp.logical_and(r > 0, r < nr - 1))
        def _():
            acc[0][...] += d

        @pl.when(r == nr - 1)
        def _():
            o_ref[...] = ((acc[0][...] + d) * scale).astype(o_ref.dtype)

    return _call(
        body, name=name, grid=(n1, N1 // t2, nr),
        in_specs=[x_spec, pl.BlockSpec((tr, t2), lambda i, j, r: (r, j))], out_specs=[o_spec],
        out_shape=[jax.ShapeDtypeStruct((rows_out, N1), out_dtype)],
        scratch=[pltpu.VMEM((t1, t2), F32)] if nr > 1 else [],
        sem=("parallel", "parallel", "arbitrary"), args=(x, y), comm=comm)


def _embed_norm(x, meta, w, comm=None):
    Bl, S, D = x.shape
    nb = (PAD + N_META + S) // Q
    M = Bl * nb * Q

    def body(x_ref, meta_ref, w_ref, h_ref, n_ref):
        head = jnp.concatenate([jnp.zeros((PAD, D), F32), meta_ref[...]], axis=0)
        h = jnp.where(pl.program_id(1) == 0, head, x_ref[0])
        h_ref[...] = h
        n_ref[...] = _rmsnorm_tile(h, w_ref[...]).astype(n_ref.dtype)

    row = pl.BlockSpec((Q, D), lambda b, t: (b * nb + t, 0))
    return _call(
        body, name="embed_norm", grid=(Bl, nb),
        in_specs=[pl.BlockSpec((1, Q, D), lambda b, t: (b, jnp.maximum(t - 1, 0), 0)),
                  pl.BlockSpec((N_META, D), lambda b, t: (0, 0)), pl.BlockSpec((1, D), lambda b, t: (0, 0))],
        out_specs=[row, row], out_shape=[jax.ShapeDtypeStruct((M, D), F32), jax.ShapeDtypeStruct((M, D), BF16)],
        scratch=[], sem=("parallel", "parallel"), args=(x, meta, w), comm=comm)


def _rmsnorm_bwd_tile(dn, h, w, dh_in):
    r = lax.rsqrt(jnp.mean(h * h, axis=-1, keepdims=True) + EPS)
    xhat = h * r
    gw = dn * w
    dh = dh_in + r * (gw - xhat * jnp.mean(gw * xhat, axis=-1, keepdims=True))
    return dh, jnp.sum(dn * xhat, axis=0, keepdims=True)


def _loss_head(h, w, target, Bl, nb):
    M, D = h.shape

    def body(h_ref, w_ref, t_ref, dh_ref, dhb_ref, dw_ref, loss_ref):
        b, t = pl.program_id(0), pl.program_id(1)
        live = (t > 0).astype(F32)
        x = h_ref[...]
        r = lax.rsqrt(jnp.mean(x * x, axis=-1, keepdims=True) + EPS)
        xhat = x * r
        wv = w_ref[...]
        err = (xhat * wv - t_ref[0]) * live
        dy = err * (1.0 / D)
        gw = dy * wv
        dx = r * (gw - xhat * jnp.mean(gw * xhat, axis=-1, keepdims=True))
        dh_ref[...] = dx
        dhb_ref[...] = dx.astype(BF16)
        dw = jnp.sum(dy * xhat, axis=0, keepdims=True)
        part = 0.5 * jnp.sum(jnp.sum(err * err, axis=-1, keepdims=True) * (1.0 / D), axis=0, keepdims=True)
        first = jnp.logical_and(b == 0, t == 0)

        @pl.when(first)
        def _():
            dw_ref[...] = dw
            loss_ref[...] = jnp.broadcast_to(part, loss_ref.shape)

        @pl.when(jnp.logical_not(first))
        def _():
            dw_ref[...] += dw
            loss_ref[...] += jnp.broadcast_to(part, loss_ref.shape)

    row = pl.BlockSpec((Q, D), lambda b, t: (b * nb + t, 0))
    vec = pl.BlockSpec((1, D), lambda b, t: (0, 0))
    return pl.pallas_call(
        body, name="loss_head", grid=(Bl, nb),
        in_specs=[row, vec, pl.BlockSpec((1, Q, D), lambda b, t: (b, jnp.maximum(t - 1, 0), 0))],
        out_specs=[row, row, vec, pl.BlockSpec((8, 128), lambda b, t: (0, 0))],
        out_shape=[jax.ShapeDtypeStruct((M, D), F32), jax.ShapeDtypeStruct((M, D), BF16),
                   jax.ShapeDtypeStruct((1, D), F32), jax.ShapeDtypeStruct((8, 128), F32)],
        compiler_params=_params(("arbitrary", "arbitrary")),
    )(h, w, target)


CONV_TC = 256


def _conv_pre(xr_ref, w_ref, b_ref):
    x = xr_ref[...].astype(F32)
    acc = b_ref[...] + w_ref[SSD_CONV - 1:SSD_CONV, :] * x
    for k in range(1, SSD_CONV):
        acc = acc + w_ref[SSD_CONV - 1 - k:SSD_CONV - k, :] * pltpu.roll(x, k, 0)
    return x, acc


def _conv_fwd(proj, w, b, Bl, T):
    M = proj.shape[0]
    off = 1024 // CONV_TC

    def body(xr_ref, w_ref, b_ref, o_ref):
        _, acc = _conv_pre(xr_ref, w_ref, b_ref)
        row = lax.broadcasted_iota(jnp.int32, acc.shape, 0)
        o_ref[...] = jnp.where(row >= PAD, acc * _sigmoid(acc), 0.0).astype(o_ref.dtype)

    return pl.pallas_call(
        body, name="conv_fwd", grid=(Bl, SSD_CONV_CH // CONV_TC),
        in_specs=[pl.BlockSpec((T, CONV_TC), lambda bb, j: (bb, j + off)),
                  pl.BlockSpec((SSD_CONV, CONV_TC), lambda bb, j: (0, j)), pl.BlockSpec((1, CONV_TC), lambda bb, j: (0, j))],
        out_specs=pl.BlockSpec((T, CONV_TC), lambda bb, j: (bb, j)),
        out_shape=jax.ShapeDtypeStruct((M, SSD_CONV_CH), BF16), compiler_params=_params(("parallel", "parallel")),
    )(proj, w, b)


def _conv_bwd(proj, w, b, dxc, dproj, Bl, T):
    M = proj.shape[0]
    off = 1024 // CONV_TC

    def body(xr_ref, w_ref, b_ref, d_ref, dx_ref, dw_ref, db_ref):
        x, acc = _conv_pre(xr_ref, w_ref, b_ref)
        row = lax.broadcasted_iota(jnp.int32, acc.shape, 0)
        s = _sigmoid(acc)
        dpre = jnp.where(row >= PAD, d_ref[...].astype(F32) * _dsilu(acc, s), 0.0)
        dx = w_ref[SSD_CONV - 1:SSD_CONV, :] * dpre
        dws = [jnp.sum(dpre * x, axis=0, keepdims=True)]
        for k in range(1, SSD_CONV):
            dx = dx + w_ref[SSD_CONV - 1 - k:SSD_CONV - k, :] * pltpu.roll(dpre, T - k, 0)
            dws.append(jnp.sum(dpre * pltpu.roll(x, k, 0), axis=0, keepdims=True))
        dx_ref[...] = dx.astype(dx_ref.dtype)
        dw = jnp.concatenate(dws[::-1], axis=0)
        db = jnp.sum(dpre, axis=0, keepdims=True)

        @pl.when(pl.program_id(1) == 0)
        def _():
            dw_ref[...] = dw
            db_ref[...] = db

        @pl.when(pl.program_id(1) > 0)
        def _():
            dw_ref[...] += dw
            db_ref[...] += db

    return _call(
        body, name="conv_bwd", grid=(SSD_CONV_CH // CONV_TC, Bl),
        in_specs=[pl.BlockSpec((T, CONV_TC), lambda j, bb: (bb, j + off)),
                  pl.BlockSpec((SSD_CONV, CONV_TC), lambda j, bb: (0, j)), pl.BlockSpec((1, CONV_TC), lambda j, bb: (0, j)),
                  pl.BlockSpec((T, CONV_TC), lambda j, bb: (bb, j))],
        out_specs=[pl.BlockSpec((T, CONV_TC), lambda j, bb: (bb, j + off)),
                   pl.BlockSpec((SSD_CONV, CONV_TC), lambda j, bb: (0, j)), pl.BlockSpec((1, CONV_TC), lambda j, bb: (0, j))],
        out_shape=[jax.ShapeDtypeStruct(dproj.shape, BF16), jax.ShapeDtypeStruct((SSD_CONV, SSD_CONV_CH), F32),
                   jax.ShapeDtypeStruct((1, SSD_CONV_CH), F32)],
        scratch=[], sem=("parallel", "arbitrary"), args=(proj, w, b, dxc), into=(dproj, 0))


N_PAIR = SSD_HEADS // 2
HPG = SSD_HEADS // SSD_GROUPS
GW = SSD_INNER // SSD_GROUPS


def _per_group(fn, *arrs):
    return jnp.concatenate([jnp.broadcast_to(fn(*(a[:, GW * g:GW * (g + 1)] for a in arrs)), (arrs[0].shape[0], GW))
                            for g in range(SSD_GROUPS)], axis=1)


def _ssd_prep(c, dtr_ref, bias_ref, alog_ref, d_ref):
    row = lax.broadcasted_iota(jnp.int32, (Q, 128), 0)
    col = lax.broadcasted_iota(jnp.int32, (Q, 128), 1)
    live = col < SSD_HEADS
    valid = jnp.logical_and(jnp.logical_or(c > 0, row >= PAD), live)
    pre = dtr_ref[...] + bias_ref[...]
    dt = jnp.where(valid, _softplus(pre), 0.0)
    A = jnp.where(live[0:1], -jnp.exp(alog_ref[...]), 0.0)
    tri = row >= col
    eye = (row == col).astype(BF16)
    cs = _dot01(tri, dt * A, NN, "a")
    cst = _dot01(eye, cs, NT, "a")
    spread = (lax.broadcasted_iota(jnp.int32, (128, SSD_INNER), 0)
              == lax.broadcasted_iota(jnp.int32, (128, SSD_INNER), 1) // SSD_HEAD_DIM).astype(BF16)
    dt_w = _dot01(dt, spread, NN, "b")
    cs_w = _dot01(cs, spread, NN, "b")
    d_w = _dot01(jnp.broadcast_to(d_ref[...], (8, 128)), spread, NN, "b")[0:1]
    lane = lax.broadcasted_iota(jnp.int32, (Q, SSD_INNER), 1)
    first = (lane % 128) < SSD_HEAD_DIM
    return dict(row=row, col=col, valid=valid, pre=pre, dt=dt, A=A, tri=tri, eye=eye, cs=cs, cst=cst, spread=spread,
                dt_w=dt_w, cs_w=cs_w, d_w=d_w, ecs_w=jnp.exp(cs_w), decay_w=jnp.exp(cs_w[Q - 1:Q] - cs_w), first=first)


def _ssd_chunk(xc_ref, s, states):
    xv = xc_ref[:, 0:SSD_INNER].astype(F32)
    Bs = [xc_ref[:, SSD_INNER + 128 * g:SSD_INNER + 128 * (g + 1)] for g in range(SSD_GROUPS)]
    Cs = [xc_ref[:, SSD_INNER + 512 + 128 * g:SSD_INNER + 512 + 128 * (g + 1)] for g in range(SSD_GROUPS)]
    X = xv * s["dt_w"]
    X0 = jnp.where(s["first"], X, 0.0)
    Xb = (X0.astype(BF16), (X - X0).astype(BF16))
    Xd = (X * s["decay_w"]).astype(BF16)
    CB = [_dot(Cs[g], Bs[g], NT) for g in range(SSD_GROUPS)]
    Lms = [jnp.exp(jnp.where(s["tri"], s["cs"][:, h:h + 1] - s["cst"][h:h + 1, :], -jnp.inf)) for h in range(SSD_HEADS)]
    Ms = [CB[h // HPG] * Lms[h] for h in range(SSD_HEADS)]
    Mb = [m.astype(BF16) for m in Ms]
    prev_b = [st.astype(BF16) for st in states]
    yds, yos, sts = [], [], []
    for p in range(N_PAIR):
        g, ln = p // 2, slice(128 * p, 128 * (p + 1))
        yds.append(_dot(Mb[2 * p], Xb[0][:, ln], NN) + _dot(Mb[2 * p + 1], Xb[1][:, ln], NN))
        yos.append(_dot(Cs[g], prev_b[p], NT))
        sts.append(_dot(Xd[:, ln], Bs[g], TN))
    yo = jnp.concatenate(yos, axis=1)
    y = jnp.concatenate(yds, axis=1) + yo * s["ecs_w"] + xv * s["d_w"]
    upper = s["row"] < SSD_HEAD_DIM
    cl = s["cs"][Q - 1:Q, :]
    ecl_rows = [jnp.where(upper, jnp.exp(cl[:, 2 * p:2 * p + 1]), jnp.exp(cl[:, 2 * p + 1:2 * p + 2])) for p in range(N_PAIR)]
    new_states = [states[p] * ecl_rows[p] + sts[p] for p in range(N_PAIR)]
    return y, new_states, dict(xv=xv, Bs=Bs, Cs=Cs, X=X, Xb=Xb, CB=CB, Lms=Lms, Ms=Ms, Mb=Mb, prev_b=prev_b, yo=yo,
                               ecl_rows=ecl_rows)


def _ssd_in_specs(nc, rev=False):
    rb = (lambda b, c: b * nc + nc - 1 - c) if rev else (lambda b, c: b * nc + c)
    vec = pl.BlockSpec((1, 128), lambda b, c: (0, 0))
    return [pl.BlockSpec((Q, SSD_CONV_CH), lambda b, c: (rb(b, c), 0)),
            pl.BlockSpec((Q, 128), lambda b, c: (rb(b, c), 0)),
            pl.BlockSpec((Q, SSD_INNER), lambda b, c: (rb(b, c), 0)),
            vec, vec, vec, pl.BlockSpec((1, SSD_INNER), lambda b, c: (0, 0))]


def _ssd_fwd(xc, dtr, proj, bias_p, alog_p, d_p, nw, Bl, nc):
    M = xc.shape[0]

    def body(xc_ref, dtr_ref, z_ref, bias_ref, alog_ref, d_ref, nw_ref, y_ref, prev_ref, state):
        c = pl.program_id(1)

        @pl.when(c == 0)
        def _():
            state[...] = jnp.zeros_like(state)

        s = _ssd_prep(c, dtr_ref, bias_ref, alog_ref, d_ref)
        states = [state[p] for p in range(N_PAIR)]
        y, new_states, _ = _ssd_chunk(xc_ref, s, states)
        for p in range(N_PAIR):
            prev_ref[0, 0, p] = states[p]
            state[p] = new_states[p]
        zz = z_ref[...].astype(F32)
        yg = y * zz * _sigmoid(zz)
        r = _per_group(lambda a: lax.rsqrt(jnp.mean(a * a, axis=-1, keepdims=True) + EPS), yg)
        y_ref[...] = (yg * r * nw_ref[...]).astype(y_ref.dtype)

    return pl.pallas_call(
        body, name="ssd_fwd", grid=(Bl, nc), in_specs=_ssd_in_specs(nc),
        out_specs=[pl.BlockSpec((Q, SSD_INNER), lambda b, c: (b * nc + c, 0)),
                   pl.BlockSpec((1, 1, N_PAIR, 128, 128), lambda b, c: (b, c, 0, 0, 0))],
        out_shape=[jax.ShapeDtypeStruct((M, SSD_INNER), BF16), jax.ShapeDtypeStruct((Bl, nc, N_PAIR, 128, 128), F32)],
        scratch_shapes=[pltpu.VMEM((N_PAIR, 128, 128), F32)],
        compiler_params=_params(("arbitrary", "arbitrary")),
    )(xc, dtr, proj, bias_p, alog_p, d_p, nw)


def _ssd_bwd(xc, dtr, proj, bias_p, alog_p, d_p, nw, prev, dya, dproj, Bl, nc, comm=None):
    M = xc.shape[0]

    def body(xc_ref, dtr_ref, z_ref, bias_ref, alog_ref, d_ref, nw_ref, prev_ref, dy_ref,
             dxc_ref, dz_ref, ddtr_ref, dbias_ref, dalog_ref, dd_ref, dnw_ref, dS):
        b, t = pl.program_id(0), pl.program_id(1)

        @pl.when(t == 0)
        def _():
            dS[...] = jnp.zeros_like(dS)

        s = _ssd_prep(nc - 1 - t, dtr_ref, bias_ref, alog_ref, d_ref)
        states = [prev_ref[0, 0, p] for p in range(N_PAIR)]
        y, _, k = _ssd_chunk(xc_ref, s, states)
        xv, Bs, Cs, Xb = k["xv"], k["Bs"], k["Cs"], k["Xb"]

        zz = z_ref[...].astype(F32)
        sz = _sigmoid(zz)
        silu_z = zz * sz
        yg = y * silu_z
        r = _per_group(lambda a: lax.rsqrt(jnp.mean(a * a, axis=-1, keepdims=True) + EPS), yg)
        xhat = yg * r
        dout = dy_ref[...].astype(F32)
        gw = dout * nw_ref[...]
        dyg = r * (gw - xhat * _per_group(lambda a, c2: jnp.mean(a * c2, axis=-1, keepdims=True), gw, xhat))
        dnw = jnp.sum(dout * xhat, axis=0, keepdims=True)
        dz_ref[...] = (dyg * y * _dsilu(zz, sz)).astype(dz_ref.dtype)
        dy = dyg * silu_z
        dy0 = jnp.where(s["first"], dy, 0.0)
        dyb = (dy0.astype(BF16), (dy - dy0).astype(BF16))
        dYo = (dy * s["ecs_w"]).astype(BF16)

        dS_f = [dS[p] for p in range(N_PAIR)]
        dS_b = [d.astype(BF16) for d in dS_f]
        BdS, dXm, dprev, dCs, dMs, XdS = [], [], [], [[] for _ in range(SSD_GROUPS)], [], []
        for p in range(N_PAIR):
            g, ln = p // 2, slice(128 * p, 128 * (p + 1))
            BdS.append(_dot(Bs[g], dS_b[p], NT))
            dXm.append(_dot(k["Mb"][2 * p], dyb[0][:, ln], TN) + _dot(k["Mb"][2 * p + 1], dyb[1][:, ln], TN))
            dprev.append(_dot(dYo[:, ln], Cs[g], TN))
            dCs[g].append(_dot(dYo[:, ln], k["prev_b"][p], NN))
            for hh in range(2):
                dMs.append(_dot(dyb[hh][:, ln], Xb[hh][:, ln], NT))
                XdS.append(_dot(Xb[hh][:, ln], dS_b[p], NN))
        dX = jnp.concatenate(dXm, axis=1) + s["decay_w"] * jnp.concatenate(BdS, axis=1)
        dxs = dy * s["d_w"] + dX * s["dt_w"]

        sums = _dot01(jnp.concatenate([dX * xv, dy * k["yo"] * s["ecs_w"], dy * xv], axis=0), s["spread"], NT, "b")
        ddt, dcs = sums[0:Q], sums[Q:2 * Q]
        dD = jnp.sum(sums[2 * Q:3 * Q], axis=0, keepdims=True)

        col, row = s["col"], s["row"]
        lane1 = col[0:1]
        rowsT = lax.broadcasted_iota(jnp.int32, (128, Q), 0)
        dcs_t = jnp.zeros((128, Q), F32)
        dcl = jnp.zeros((1, 128), F32)
        dB_out, dC_out = [], []
        for g in range(SSD_GROUPS):
            Bf = Bs[g].astype(F32)
            dCB = jnp.zeros((Q, Q), F32)
            dBacc = jnp.zeros((Q, 128), F32)
            for r4 in range(HPG):
                h = HPG * g + r4
                p, hh = h // 2, h % 2
                W = dMs[h] * k["Ms"][h]
                dCB = dCB + dMs[h] * k["Lms"][h]
                decay_h = s["decay_w"][:, SSD_HEAD_DIM * h:SSD_HEAD_DIM * h + 1]
                dBacc = dBacc + decay_h * XdS[h]
                tdec = jnp.sum(XdS[h] * Bf, axis=1, keepdims=True) * decay_h
                dcs = dcs + jnp.where(col == h, jnp.sum(W, axis=1, keepdims=True) - tdec, 0.0)
                dcs_t = dcs_t - jnp.where(rowsT == h, jnp.sum(W, axis=0, keepdims=True), 0.0)
                rows_h = (row < SSD_HEAD_DIM) if hh == 0 else (row >= SSD_HEAD_DIM)
                sprev = jnp.sum(jnp.sum(jnp.where(rows_h, dS_f[p] * states[p], 0.0), axis=1, keepdims=True),
                                axis=0, keepdims=True)
                ecl = jnp.exp(s["cs"][Q - 1:Q, h:h + 1])
                dcl = dcl + jnp.where(lane1 == h, jnp.sum(tdec, axis=0, keepdims=True) + ecl * sprev, 0.0)
            dCB_b = dCB.astype(BF16)
            dC_out.append(dCs[g][0] + dCs[g][1] + _dot(dCB_b, Bs[g], NN))
            dB_out.append(dBacc + _dot(dCB_b, Cs[g], TN))
        for p in range(N_PAIR):
            dS[p] = dS_f[p] * k["ecl_rows"][p] + dprev[p]
        dxc_ref[...] = jnp.concatenate([dxs] + dB_out + dC_out, axis=1).astype(dxc_ref.dtype)

        dcs = dcs + _dot01(s["eye"], dcs_t, NT, "a") + jnp.where(row == Q - 1, dcl, 0.0)
        da = _dot01(row <= col, dcs, NN, "a")
        ddt = ddt + da * s["A"]
        dpre = jnp.where(s["valid"], ddt * _sigmoid(s["pre"]), 0.0)
        ddtr_ref[...] = dpre
        dbias = jnp.sum(dpre, axis=0, keepdims=True)
        dalog = jnp.sum(da * s["dt"], axis=0, keepdims=True) * s["A"]
        first_step = jnp.logical_and(b == 0, t == 0)

        @pl.when(first_step)
        def _():
            dbias_ref[...] = dbias
            dalog_ref[...] = dalog
            dd_ref[...] = dD
            dnw_ref[...] = dnw

        @pl.when(jnp.logical_not(first_step))
        def _():
            dbias_ref[...] += dbias
            dalog_ref[...] += dalog
            dd_ref[...] += dD
            dnw_ref[...] += dnw

    rb = lambda b, c: b * nc + nc - 1 - c
    rowblk = lambda w: pl.BlockSpec((Q, w), lambda b, c: (rb(b, c), 0))
    vec = lambda w: pl.BlockSpec((1, w), lambda b, c: (0, 0))
    return _call(
        body, name="ssd_bwd", grid=(Bl, nc),
        in_specs=_ssd_in_specs(nc, rev=True) + [
            pl.BlockSpec((1, 1, N_PAIR, 128, 128), lambda b, c: (b, nc - 1 - c, 0, 0, 0)), rowblk(SSD_INNER)],
        out_specs=[rowblk(SSD_CONV_CH), rowblk(SSD_INNER), rowblk(128), vec(128), vec(128), vec(128), vec(SSD_INNER)],
        out_shape=[jax.ShapeDtypeStruct((M, SSD_CONV_CH), BF16), jax.ShapeDtypeStruct(dproj.shape, BF16),
                   jax.ShapeDtypeStruct((M, 128), F32), jax.ShapeDtypeStruct((1, 128), F32),
                   jax.ShapeDtypeStruct((1, 128), F32), jax.ShapeDtypeStruct((1, 128), F32),
                   jax.ShapeDtypeStruct((1, SSD_INNER), F32)],
        scratch=[pltpu.VMEM((N_PAIR, 128, 128), F32)], sem=("arbitrary", "arbitrary"),
        args=(xc, dtr, proj, bias_p, alog_p, d_p, nw, prev, dya), comm=comm, into=(dproj, 1))


NSUB = Q // HG_CHUNK
HG_HP = 8
EXP_CAP = 80.0


def _hg_setup(blk, q_ref, f_ref, hb_ref):
    row = lax.broadcasted_iota(jnp.int32, (Q, Q), 0)
    col = lax.broadcasted_iota(jnp.int32, (Q, Q), 1)
    same = (row // HG_CHUNK) == (col // HG_CHUNK)
    causal = jnp.logical_and(same, col <= row)
    lb = _sigmoid(hb_ref[0:1, :] - hb_ref[1:2, :])
    fl = f_ref[...].astype(F32)
    sg = _sigmoid(fl)
    fg = lb + (1.0 - lb) * sg
    k = (1.0 - lb) * (1.0 - sg)
    gl = jnp.log(fg)
    G = _dot01(causal, gl, NN, "a")
    T = _dot01(same, gl, NN, "a")
    qv = q_ref[...].astype(F32)
    sq = _sigmoid(qv)
    eG = jnp.exp(G)
    eGn = jnp.exp(jnp.minimum(-G, EXP_CAP))
    eTG = jnp.exp(T - G)
    qt = qv * sq * eG
    kt = k * eGn
    kh = k * eTG
    valid = jnp.logical_or(blk > 0, row[:, :1] >= PAD)
    return dict(row=row, col=col, same=same, causal=causal, lb=lb, sg=sg, fg=fg, k=k, T=T, qv=qv, sq=sq,
                eG=eG, eGn=eGn, eTG=eTG, qt=qt, kt=kt, kh=kh, valid=valid)


def _hg_specs(nb, rev=False):
    rb = (lambda h, b, t: b * nb + nb - 1 - t) if rev else (lambda h, b, t: b * nb + t)
    w = 128 * HG_HP
    blk = lambda off: pl.BlockSpec((Q, w), lambda h, b, t, off=off: (rb(h, b, t), off // HG_HP + h))
    return [blk(24), blk(32), blk(40), blk(48),
            pl.BlockSpec((2, w), lambda h, b, t: (0, h)), pl.BlockSpec((1, w), lambda h, b, t: (0, h))]


HEAD_LANES = tuple(slice(128 * hh, 128 * (hh + 1)) for hh in range(HG_HP))


def _per_head(fn, *arrs):
    return jnp.concatenate([jnp.broadcast_to(fn(*(a[:, ln] for a in arrs)), (arrs[0].shape[0], 128))
                            for ln in HEAD_LANES], axis=1)


def _hgrn_fwd(proj, hb, nw, Bl, nb, comm=None):
    M = proj.shape[0]

    def body(q_ref, f_ref, i_ref, g_ref, hb_ref, nw_ref, y_ref, o_ref, st_ref, S):
        blk = pl.program_id(2)

        @pl.when(blk == 0)
        def _():
            S[...] = jnp.zeros_like(S)

        s = _hg_setup(blk, q_ref, f_ref, hb_ref)
        v = i_ref[...]
        qt_b, kt_b, kh_b = s["qt"].astype(BF16), s["kt"].astype(BF16), s["kh"].astype(BF16)
        eT = jnp.exp(s["T"])
        att = [jnp.where(s["causal"], _dot(qt_b[:, ln], kt_b[:, ln], NT), 0.0).astype(BF16) for ln in HEAD_LANES]
        o_intra = [_dot(att[hh], v[:, ln], NN) for hh, ln in enumerate(HEAD_LANES)]
        for j in range(NSUB):
            sl = slice(HG_CHUNK * j, HG_CHUNK * (j + 1))
            for hh, ln in enumerate(HEAD_LANES):
                St = S[hh]
                st_ref[0, hh, 0, j] = St
                o_ref[sl, ln] = o_intra[hh][sl] + _dot(qt_b[sl, ln], St.astype(BF16), NT)
                S[hh] = St * eT[HG_CHUNK * j:HG_CHUNK * j + 1, ln] + _dot(v[sl, ln], kh_b[sl, ln], TN)
        o = o_ref[...]
        r = _per_head(lambda a: lax.rsqrt(jnp.mean(a * a, axis=-1, keepdims=True) + EPS), o)
        gv = g_ref[...].astype(F32)
        y_ref[...] = (o * r * nw_ref[...] * gv * _sigmoid(gv)).astype(y_ref.dtype)

    rowblk = pl.BlockSpec((Q, 128 * HG_HP), lambda h, b, t: (b * nb + t, h))
    return _call(
        body, name="hgrn_fwd", grid=(HG_HEADS // HG_HP, Bl, nb), in_specs=_hg_specs(nb),
        out_specs=[rowblk, rowblk,
                   pl.BlockSpec((1, HG_HP, 1, NSUB, 128, 128), lambda h, b, t: (b, h, t, 0, 0, 0))],
        out_shape=[jax.ShapeDtypeStruct((M, HG_WIDTH), BF16), jax.ShapeDtypeStruct((M, HG_WIDTH), F32),
                   jax.ShapeDtypeStruct((Bl, HG_HEADS, nb, NSUB, 128, 128), F32)],
        scratch=[pltpu.VMEM((HG_HP, 128, 128), F32)], sem=("parallel", "arbitrary", "arbitrary"),
        args=(proj, proj, proj, proj, hb, nw), comm=comm)


def _hgrn_bwd(proj, hb, nw, o_saved, st_saved, dyb, dproj, Bl, nb, comm=None):
    assert HG_HP == HG_HEADS

    def body(q_ref, f_ref, i_ref, g_ref, hb_ref, nw_ref, o_ref, st_ref, dy_ref,
             d_ref, dhb_ref, dnw_ref, dS, a_dqt, a_dv, a_dkh, a_dgl):
        b, t = pl.program_id(1), pl.program_id(2)

        @pl.when(t == 0)
        def _():
            dS[...] = jnp.zeros_like(dS)

        first_step = jnp.logical_and(b == 0, t == 0)
        s = _hg_setup(nb - 1 - t, q_ref, f_ref, hb_ref)
        v = i_ref[...]
        qt_b, kt_b, kh_b = s["qt"].astype(BF16), s["kt"].astype(BF16), s["kh"].astype(BF16)
        eT = jnp.exp(s["T"])
        att = [jnp.where(s["causal"], _dot(qt_b[:, ln], kt_b[:, ln], NT), 0.0).astype(BF16) for ln in HEAD_LANES]

        o = o_ref[...]
        r = _per_head(lambda a: lax.rsqrt(jnp.mean(a * a, axis=-1, keepdims=True) + EPS), o)
        xhat = o * r
        gv = g_ref[...].astype(F32)
        sgv = _sigmoid(gv)
        dyv = dy_ref[...].astype(F32)
        d_on = dyv * gv * sgv
        dg_out = dyv * xhat * nw_ref[...] * _dsilu(gv, sgv)
        gw = d_on * nw_ref[...]
        do = r * (gw - xhat * _per_head(lambda a, c: jnp.mean(a * c, axis=-1, keepdims=True), gw, xhat))
        dnw = jnp.sum(d_on * xhat, axis=0, keepdims=True)
        do_b = do.astype(BF16)

        datt = [jnp.where(s["causal"], _dot(do_b[:, ln], v[:, ln], NT), 0.0).astype(BF16) for ln in HEAD_LANES]
        dqt = jnp.concatenate([_dot(datt[hh], kt_b[:, ln], NN) for hh, ln in enumerate(HEAD_LANES)], axis=1)
        dkt = jnp.concatenate([_dot(datt[hh], qt_b[:, ln], TN) for hh, ln in enumerate(HEAD_LANES)], axis=1)
        dv = jnp.concatenate([_dot(att[hh], do_b[:, ln], TN) for hh, ln in enumerate(HEAD_LANES)], axis=1)
        last_row = (lax.broadcasted_iota(jnp.int32, (HG_CHUNK, 128), 0) == HG_CHUNK - 1)
        for j in reversed(range(NSUB)):
            sl = slice(HG_CHUNK * j, HG_CHUNK * (j + 1))
            for hh, ln in enumerate(HEAD_LANES):
                St = st_ref[0, hh, 0, j]
                dSt = dS[hh]
                St_b, dSt_b = St.astype(BF16), dSt.astype(BF16)
                eT_j = eT[HG_CHUNK * j:HG_CHUNK * j + 1, ln]
                dkh_j = _dot(v[sl, ln], dSt_b, NN)
                a_dqt[sl, ln] = _dot(do_b[sl, ln], St_b, NN)
                a_dv[sl, ln] = _dot(kh_b[sl, ln], dSt_b, NT)
                a_dkh[sl, ln] = dkh_j
                dlast = (jnp.sum(St * dSt, axis=0, keepdims=True) * eT_j
                         + jnp.sum(dkh_j * s["kh"][sl, ln], axis=0, keepdims=True))
                a_dgl[sl, ln] = jnp.where(last_row, dlast, 0.0)
                dS[hh] = dSt * eT_j + _dot(do_b[sl, ln], qt_b[sl, ln], TN)
        dqt = dqt + a_dqt[...]
        dv = dv + a_dv[...]
        dkh = a_dkh[...]
        dG = dqt * s["qt"] - dkt * s["kt"] - dkh * s["kh"] + a_dgl[...]
        rev_causal = jnp.logical_and(s["same"], s["col"] >= s["row"])
        dgl = _dot01(rev_causal, dG, NN, "a")
        dk = dkt * s["eGn"] + dkh * s["eTG"]
        dfg = dgl / s["fg"] - dk
        lb, sg = s["lb"], s["sg"]
        keep = s["valid"].astype(F32)
        d_ref[:, 0:w] = (dqt * s["eG"] * _dsilu(s["qv"], s["sq"]) * keep).astype(d_ref.dtype)
        d_ref[:, w:2 * w] = (dfg * (1.0 - lb) * sg * (1.0 - sg) * keep).astype(d_ref.dtype)
        d_ref[:, 2 * w:3 * w] = (dv * keep).astype(d_ref.dtype)
        d_ref[:, 3 * w:4 * w] = (dg_out * keep).astype(d_ref.dtype)
        dlb = jnp.sum(dfg * (1.0 - sg) * keep, axis=0, keepdims=True) * lb * (1.0 - lb)
        dhb = jnp.concatenate([dlb, -dlb], axis=0)

        @pl.when(first_step)
        def _():
            dhb_ref[...] = dhb
            dnw_ref[...] = dnw

        @pl.when(jnp.logical_not(first_step))
        def _():
            dhb_ref[...] += dhb
            dnw_ref[...] += dnw

    w = 128 * HG_HP
    rowblk = pl.BlockSpec((Q, w), lambda h, b, t: (b * nb + nb - 1 - t, h))
    return _call(
        body, name="hgrn_bwd", grid=(HG_HEADS // HG_HP, Bl, nb),
        in_specs=_hg_specs(nb, rev=True) + [
            rowblk, pl.BlockSpec((1, HG_HP, 1, NSUB, 128, 128), lambda h, b, t: (b, h, nb - 1 - t, 0, 0, 0)), rowblk],
        out_specs=[pl.BlockSpec((pl.Element(Q), pl.Element(4 * w)),
                                lambda h, b, t: (pl.multiple_of((b * nb + nb - 1 - t) * Q, Q), 3 * HG_WIDTH)),
                   pl.BlockSpec((2, w), lambda h, b, t: (0, h)), pl.BlockSpec((1, w), lambda h, b, t: (0, h))],
        out_shape=[jax.ShapeDtypeStruct(dproj.shape, BF16),
                   jax.ShapeDtypeStruct((2, HG_WIDTH), F32), jax.ShapeDtypeStruct((1, HG_WIDTH), F32)],
        scratch=[pltpu.VMEM((HG_HP, 128, 128), F32)] + [pltpu.VMEM((Q, w), F32)] * 4,
        sem=("parallel", "arbitrary", "arbitrary"),
        args=(proj, proj, proj, proj, hb, nw, o_saved, st_saved, dyb), comm=comm, into=(dproj, 0))


def _adamw(name, parts, w, m, v, comm=None):
    R, C = w.shape
    S = parts.shape[0]
    tr, tc = (_tile(R, (256, 176, 128, 64, 8)), C) if R % 8 == 0 else (R, 256)
    c1, c2 = 1.0 - ADAM_B1 ** ADAM_STEP, 1.0 - ADAM_B2 ** ADAM_STEP

    def body(p_ref, w_ref, m_ref, v_ref, g_ref, d_ref, nm_ref, nv_ref):
        g = p_ref[0].astype(F32)
        for s in range(1, S):
            g = g + p_ref[s].astype(F32)
        nm = ADAM_B1 * m_ref[...] + (1.0 - ADAM_B1) * g
        nv = ADAM_B2 * v_ref[...] + (1.0 - ADAM_B2) * (g * g)
        g_ref[...] = g
        nm_ref[...] = nm
        nv_ref[...] = nv
        d_ref[...] = -ADAM_LR * ((nm / c1) / (jnp.sqrt(nv / c2) + ADAM_EPS) + ADAM_WD * w_ref[...])

    blk = pl.BlockSpec((tr, tc), lambda i, j: (i, j))
    return _call(
        body, name=name, grid=(R // tr, C // tc),
        in_specs=[pl.BlockSpec((S, tr, tc), lambda i, j: (0, i, j)), blk, blk, blk], out_specs=[blk] * 4,
        out_shape=[jax.ShapeDtypeStruct((R, C), F32)] * 4, scratch=[], sem=("parallel", "parallel"),
        args=(parts, w, m, v), comm=comm)


def _pair_sum(name, by_core, arrived):
    _, J, R, C = by_core.shape
    tc = _tile(C, (512, 256, 128))

    def body(c_ref, a_ref, b_ref, o_ref):
        o_ref[...] = (a_ref[0].astype(F32) + b_ref[...].astype(F32)).astype(o_ref.dtype)

    blk = pl.BlockSpec((1, R, tc), lambda j, k, c_ref: (j, 0, k))
    return pl.pallas_call(
        body, name=name,
        grid_spec=pltpu.PrefetchScalarGridSpec(
            num_scalar_prefetch=1, grid=(J, C // tc),
            in_specs=[pl.BlockSpec((1, 1, R, tc), lambda j, k, c_ref: (c_ref[0], j, 0, k)), blk], out_specs=blk),
        out_shape=jax.ShapeDtypeStruct(arrived.shape, arrived.dtype), compiler_params=_params(("parallel", "parallel")),
    )(lax.axis_index("c").astype(jnp.int32).reshape(1), by_core, arrived)


def _sum_parts(name, parts):
    S, R, C = parts.shape

    def body(p_ref, o_ref):
        g = p_ref[0]
        for s in range(1, S):
            g = g + p_ref[s]
        o_ref[...] = g

    return pl.pallas_call(
        body, name=name, out_shape=jax.ShapeDtypeStruct((R, C), F32),
        in_specs=[pl.BlockSpec(memory_space=pltpu.VMEM)], out_specs=pl.BlockSpec(memory_space=pltpu.VMEM),
    )(parts)


def _heads_to_lanes(p):
    return jnp.pad(p, [(0, 0)] * (p.ndim - 1) + [(0, 128 - SSD_HEADS)])


def _lanes_to_heads(p):
    return p[..., :SSD_HEADS]


def _pack_rows(arrs):
    flat = jnp.concatenate([a.reshape(-1).astype(F32) for a in arrs])
    return jnp.pad(flat, (0, (-flat.shape[0]) % (8 * D_MODEL))).reshape(-1, D_MODEL)


def _unpack_rows(packed, like):
    flat, outs, at = packed.reshape(-1), [], 0
    for a in like:
        outs.append(flat[at:at + a.size].reshape(a.shape))
        at += a.size
    return outs


def _cols(gth):
    return jnp.transpose(gth, (1, 0, 2)).reshape(gth.shape[1], -1)


def _rows(gth):
    return gth.reshape(-1, gth.shape[2])


def _to_rows(g):
    return g.reshape(N_DEV, -1, g.shape[1]).astype(BF16)


def _by_core(g):
    return jnp.transpose(g.reshape(N_DEV // 2, 2, -1, g.shape[1]), (1, 0, 2, 3)).astype(BF16)


DT_ROW = 3072


def _chip_sums(tag, by_core, swap_in=None):
    arrived = swap_in(by_core) if swap_in else _exchange(tag + "_swap", "swap", by_core)
    return [_pair_sum(f"{tag}_chipsum{i}", m, a) for i, (m, a) in enumerate(zip(by_core, arrived))]


def _ffn_fwd_gu(tag, n, w_gu_t, comm=None):
    M = n.shape[0]
    F = w_gu_t.shape[0] // 2
    tm = _tile(M, (544, 256))
    outs = _fused_matmul(
        tag + "_gu", M, F, D_MODEL,
        [dict(a=n, b=w_gu_t, trans_b=True, acc=0, resident=True),
         dict(a=n, b=w_gu_t, trans_b=True, bn_off=1, acc=1, resident=True)], [],
        lambda accs, ex: (accs[0], accs[1], accs[0] * _sigmoid(accs[0]) * accs[1]),
        [BF16, BF16, BF16], 2, tm, F, D_MODEL, outer="i", comm=comm, sub=256)
    return (n, *outs[:3]), outs[3:]


def _rmsnorm_tile(x, w):
    return x * lax.rsqrt(jnp.mean(x * x, axis=-1, keepdims=True) + EPS) * w


def _ffn_fwd_down(tag, h, a, w_down, next_norm=None, comm=None):
    M = h.shape[0]
    F = w_down.shape[0]
    tm = _tile(M, (1088, 544, 256))
    if next_norm is None:
        (h_out,) = _fused_matmul(
            tag + "_down", M, D_MODEL, F, [dict(a=a, b=w_down, acc=0)], [(h, 0)],
            lambda accs, ex: (ex[0] + 0.5 * accs[0],), [F32], 1, tm, D_MODEL, F, outer="j", sub=256)
        return h_out

    def with_norm(accs, ex):
        h_new = ex[0] + 0.5 * accs[0]
        return h_new, _rmsnorm_tile(h_new, ex[1])

    return _fused_matmul(tag + "_down", M, D_MODEL, F, [dict(a=a, b=w_down, acc=0, resident=True)], [(h, 0)], with_norm,
                         [F32, BF16], 1, tm, D_MODEL, F, outer="j", vecs=[next_norm], comm=comm)


def _ffn_bwd(tag, dh, dh_b, h, norm_w, w_gu_t, w_down, saved, scatter=False):
    n, g, u, a = saved
    M = h.shape[0]
    F = w_down.shape[0]
    tm = _tile(M, (544, 256))
    tn = _tile(F, (1408, 704, 256))

    def swiglu_bwd(accs, ex):
        da, gv, uv = 0.5 * accs[0], ex[0].astype(F32), ex[1].astype(F32)
        s = _sigmoid(gv)
        return da * uv * _dsilu(gv, s), da * gv * s

    tr = _tile(M, (2176, 256))
    (dw_down,) = _matmul_tn(tag + "_dwd", a, dh_b, tn, D_MODEL, tr, scale=0.5)
    dgu, *p_down = _fused_matmul(
        tag + "_dact", M, F, D_MODEL, [dict(a=dh_b, b=w_down, trans_b=True, acc=0, resident=True)], [(g, 0), (u, 0)],
        swiglu_bwd, [BF16, BF16], 1, tm, F, D_MODEL, outer="i", stack=True, sub=256,
        comm=("scatter", [_to_rows(dw_down)]) if scatter else None)
    (dw_gu_t,) = _matmul_tn(tag + "_dwgu", dgu, n, tn, D_MODEL, tr)
    comm = None
    if scatter:
        comm = ("chips", _chip_sums(tag + "_wgu", [_by_core(dw_gu_t)]))
    def norm_bwd(accs, ex):
        dh_prev, dw = _rmsnorm_bwd_tile(accs[0], ex[0], ex[2], ex[1])
        return dh_prev, dh_prev, dw

    dh_prev, dh_prev_b, dnorm, *p_gu = _fused_matmul(
        tag + "_dn", M, D_MODEL, 2 * F,
        [dict(a=dgu, b=w_gu_t, acc=0, resident=True)], [(h, 0), (dh, 0)],
        norm_bwd, [F32, BF16], 1, tm, D_MODEL, 2 * F, outer="i", comm=comm, vecs=[norm_w], row_sums=1)
    return (dh_prev, dh_prev_b, dnorm, *((p_gu[0], p_down[0]) if scatter else (dw_gu_t, dw_down)))


def kernel(x, meta_tokens, ffn1_norm, ffn1_w_gu, ffn1_w_down, mix_norm, w_in, ssd_conv_w, ssd_conv_b, ssd_dt_bias, ssd_a_log, ssd_d, ssd_norm, hg_lower_bound, hg_norm, w_branch_a, w_branch_b, w_out, ffn2_norm, ffn2_w_gu, ffn2_w_down, final_norm, loss_target, m_meta_tokens, m_ffn1_norm, m_ffn1_w_gu, m_ffn1_w_down, m_mix_norm, m_w_in, m_ssd_conv_w, m_ssd_conv_b, m_ssd_dt_bias, m_ssd_a_log, m_ssd_d, m_ssd_norm, m_hg_lower_bound, m_hg_norm, m_w_branch_a, m_w_branch_b, m_w_out, m_ffn2_norm, m_ffn2_w_gu, m_ffn2_w_down, m_final_norm, v_meta_tokens, v_ffn1_norm, v_ffn1_w_gu, v_ffn1_w_down, v_mix_norm, v_w_in, v_ssd_conv_w, v_ssd_conv_b, v_ssd_dt_bias, v_ssd_a_log, v_ssd_d, v_ssd_norm, v_hg_lower_bound, v_hg_norm, v_w_branch_a, v_w_branch_b, v_w_out, v_ffn2_norm, v_ffn2_w_gu, v_ffn2_w_down, v_final_norm):
    Bl, S, D = x.shape
    T = PAD + N_META + S
    nc = T // Q
    M = Bl * T
    me = 4 * lax.axis_index("x") + 2 * lax.axis_index("y") + lax.axis_index("c")

    bf = lambda a: a[0].astype(BF16)
    bft = lambda a: a[0].T.astype(BF16)
    g_meta, g_conv_w = _exchange("gather_small", "gather", [meta_tokens, ssd_conv_w[0]])
    meta_full, conv_w_full = _cols(g_meta), _cols(g_conv_w)
    bias_p, alog_p, d_p = _heads_to_lanes(ssd_dt_bias), _heads_to_lanes(ssd_a_log), _heads_to_lanes(ssd_d)
    final_w = final_norm.reshape(1, D)

    h0, n1, g_wgu1 = _embed_norm(x, meta_full, ffn1_norm, comm=("gather", [bft(ffn1_w_gu)]))
    wgu1 = _rows(g_wgu1)
    tm = _tile(M, (1088, 544, 256))
    win_shard = bft(w_in)
    cut = (win_shard.shape[0] // 32) * 16
    ffn1_saved, (g_wd1, g_win_a) = _ffn_fwd_gu("ffn1", n1, wgu1, comm=("gather", [bf(ffn1_w_down), win_shard[:cut]]))
    wd1 = _rows(g_wd1)
    h1, un, g_win_b = _ffn_fwd_down("ffn1", h0, ffn1_saved[3], wd1, next_norm=mix_norm,
                                    comm=("gather", [win_shard[cut:]]))
    win_t = _rows(jnp.concatenate([g_win_a, g_win_b], axis=1))
    win_dt = jnp.pad(win_t[DT_ROW:DT_ROW + SSD_HEADS], ((0, 128 - SSD_HEADS), (0, 0)))
    plain = lambda accs, ex: (accs[0],)
    proj, g_wa, g_wb, g_wo = _fused_matmul(
        "in_proj", M, N_MAIN, D, [dict(a=un, b=win_t, trans_b=True, acc=0, b_shift=(DT_ROW // 1536, SSD_HEADS))], [],
        plain, [BF16], 1, tm, 1536, D,
        outer="j", comm=("gather", [bf(w_branch_a), bf(w_branch_b), bf(w_out)]), sub=512)
    wa, wb, wo = _rows(g_wa), _rows(g_wb), _rows(g_wo)
    (dtr,) = _fused_matmul("in_proj_dt", M, 128, D, [dict(a=un, b=win_dt, trans_b=True, acc=0)], [], plain, [F32], 1,
                           tm, 128, D, outer="j")
    xc = _conv_fwd(proj, conv_w_full, ssd_conv_b, Bl, T)
    ya, ssd_prev = _ssd_fwd(xc, dtr, proj, bias_p, alog_p, d_p, ssd_norm, Bl, nc)
    yb, hg_o, hg_st, g_wgu2, g_wd2 = _hgrn_fwd(proj, hg_lower_bound, hg_norm, Bl, nc,
                                               comm=("gather", [bft(ffn2_w_gu), bf(ffn2_w_down)]))
    wgu2, wd2 = _rows(g_wgu2), _rows(g_wd2)

    def branch_fwd(accs, ex):
        pa, pb = accs
        return pa, pb, _sigmoid(ex[0].astype(F32)) * pa + _sigmoid(ex[1].astype(F32)) * pb

    pa, pb, merged = _fused_matmul(
        "branches", M, D, D, [dict(a=ya, b=wa, acc=0), dict(a=yb, b=wb, acc=1)], [(proj, 7), (proj, 8)],
        branch_fwd, [BF16, BF16, BF16], 2, tm, D, D, outer="j")
    def out_with_norm(accs, ex):
        h_new = ex[0] + accs[0]
        return h_new, _rmsnorm_tile(h_new, ex[1])

    h2, n2 = _fused_matmul("out_proj", M, D, D, [dict(a=merged, b=wo, acc=0)], [(h1, 0)], out_with_norm,
                           [F32, BF16], 1, tm, D, D, outer="j", vecs=[ffn2_norm])
    ffn2_saved, _ = _ffn_fwd_gu("ffn2", n2, wgu2)
    h3 = _ffn_fwd_down("ffn2", h2, ffn2_saved[3], wd2)

    dh3, dh3_b, d_final, loss_part = _loss_head(h3, final_w, loss_target, Bl, nc)
    dh2, dh2_b, d_ffn2_norm, d_wgu2, d_wd2 = _ffn_bwd("ffn2", dh3, dh3_b, h2, ffn2_norm, wgu2, wd2, ffn2_saved)

    def branch_bwd(accs, ex):
        dm = accs[0]
        ga, gb, pav, pbv = (e.astype(F32) for e in ex)
        sa, sb = _sigmoid(ga), _sigmoid(gb)
        return (dm * sa, dm * sb,
                jnp.concatenate([dm * pav * sa * (1.0 - sa), dm * pbv * sb * (1.0 - sb)], axis=1))

    d_merged_outs = []

    def d_merged_with_swap(theirs):
        d_merged_outs.extend(_fused_matmul(
            "d_merged", M, D, D, [dict(a=dh2_b, b=wo, trans_b=True, acc=0)], [(proj, 7), (proj, 8), (pa, 0), (pb, 0)],
            branch_bwd, [BF16] * 2, 1, tm, D, D, outer="j", comm=("swap", theirs),
            wide=dict(width=2 * D, col=7 * D, total=N_MAIN, dtype=BF16)))
        return d_merged_outs[3:]

    s_ffn2 = _chip_sums("ffn2", [_by_core(d_wgu2), _by_core(d_wd2)], swap_in=d_merged_with_swap)
    dpa, dpb, dproj = d_merged_outs[:3]
    (d_wo,) = _matmul_tn("d_w_out", merged, dh2_b, 512, D, M)
    (d_wa,) = _matmul_tn("d_w_a", ya, dpa, 512, D, M)
    (d_wb,) = _matmul_tn("d_w_b", yb, dpb, 512, D, M)
    dya, dyb = _fused_matmul(
        "d_branches", M, D, D, [dict(a=dpa, b=wa, trans_b=True, acc=0), dict(a=dpb, b=wb, trans_b=True, acc=1)], [],
        lambda accs, ex: (accs[0], accs[1]), [BF16, BF16], 2, tm, D, D, outer="j")
    *ssd_grads, p_wgu2, p_wd2 = _ssd_bwd(xc, dtr, proj, bias_p, alog_p, d_p, ssd_norm, ssd_prev, dya, dproj, Bl, nc,
                                         comm=("chips", s_ffn2))
    dxc, dproj, ddtr, d_bias_p, d_alog_p, d_d_p, d_ssd_norm = ssd_grads
    dproj, d_conv_w, d_conv_b = _conv_bwd(proj, conv_w_full, ssd_conv_b, dxc, dproj, Bl, T)
    dproj, d_hb, d_hg_norm = _hgrn_bwd(proj, hg_lower_bound, hg_norm, hg_o, hg_st, dyb, dproj, Bl, nc)
    ddtr_b = ddtr.astype(BF16)
    d_win_t, p_wa, p_wb, p_wo = _matmul_tn(
        "d_w_in", dproj, un, 768, D, M, out_skip=(DT_ROW, SSD_HEADS),
        comm=("scatter", [_to_rows(d_wa), _to_rows(d_wb), _to_rows(d_wo)]))
    (d_win_dt,) = _matmul_tn("d_w_in_dt", ddtr_b, un, 128, D, M)
    d_win_t = lax.dynamic_update_slice(d_win_t, d_win_dt[:SSD_HEADS], (DT_ROW, 0))
    d_un_dt_outs = []

    def d_un_dt_with_swap(theirs):
        d_un_dt_outs.extend(_fused_matmul("d_un_dt", M, D, 128, [dict(a=ddtr_b, b=win_dt, acc=0)], [], plain, [F32], 1,
                                          tm, D, 128, outer="j", comm=("swap", theirs)))
        return d_un_dt_outs[1:]

    s_win = _chip_sums("w_in", [_by_core(d_win_t)], swap_in=d_un_dt_with_swap)
    def mix_norm_bwd(accs, ex):
        dh, dw = _rmsnorm_bwd_tile(accs[0] + ex[0], ex[1], ex[3], ex[2])
        return dh, dh, dw

    dh1, dh1_b, d_mix_norm, p_win = _fused_matmul(
        "d_un", M, D, N_MAIN, [dict(a=dproj, b=win_t, acc=0, b_shift=(DT_ROW // 3072, SSD_HEADS))],
        [(d_un_dt_outs[0], 0), (h1, 0), (dh2, 0)],
        mix_norm_bwd, [F32, BF16], 1, _tile(M, (544, 256)), D, 3072, outer="i", comm=("chips", s_win),
        vecs=[mix_norm], row_sums=1)
    dh0, _, d_ffn1_norm, p_wgu1, p_wd1 = _ffn_bwd("ffn1", dh1, dh1_b, h0, ffn1_norm, wgu1, wd1, ffn1_saved, scatter=True)

    dh0 = dh0.reshape(Bl, T, D)
    grad_x = dh0[:, PAD + N_META:]
    d_meta = dh0[:, PAD:PAD + N_META]

    small_grads = [d_ffn1_norm, d_mix_norm, d_conv_b, _lanes_to_heads(d_bias_p), _lanes_to_heads(d_alog_p),
                   _lanes_to_heads(d_d_p), d_ssd_norm, d_hb, d_hg_norm, d_ffn2_norm, d_final.reshape(D), d_conv_w]
    small_like = small_grads + [d_meta[b] for b in range(Bl)] + [loss_part[0, 0:1]]
    small_packed = _pack_rows(small_like)
    parts = [p_wgu1, p_wd1, p_win, p_wa, p_wb, p_wo, p_wgu2, p_wd2]

    names = ["meta_tokens", "ffn1_norm", "ffn1_w_gu", "ffn1_w_down", "mix_norm", "w_in", "ssd_conv_w", "ssd_conv_b",
             "ssd_dt_bias", "ssd_a_log", "ssd_d", "ssd_norm", "hg_lower_bound", "hg_norm", "w_branch_a", "w_branch_b",
             "w_out", "ffn2_norm", "ffn2_w_gu", "ffn2_w_down", "final_norm"]
    W = dict(meta_tokens=meta_tokens, ffn1_norm=ffn1_norm, ffn1_w_gu=ffn1_w_gu, ffn1_w_down=ffn1_w_down, mix_norm=mix_norm,
             w_in=w_in, ssd_conv_w=ssd_conv_w, ssd_conv_b=ssd_conv_b, ssd_dt_bias=ssd_dt_bias, ssd_a_log=ssd_a_log,
             ssd_d=ssd_d, ssd_norm=ssd_norm, hg_lower_bound=hg_lower_bound, hg_norm=hg_norm, w_branch_a=w_branch_a,
             w_branch_b=w_branch_b, w_out=w_out, ffn2_norm=ffn2_norm, ffn2_w_gu=ffn2_w_gu, ffn2_w_down=ffn2_w_down,
             final_norm=final_norm)
    Mo = dict(meta_tokens=m_meta_tokens, ffn1_norm=m_ffn1_norm, ffn1_w_gu=m_ffn1_w_gu, ffn1_w_down=m_ffn1_w_down,
              mix_norm=m_mix_norm, w_in=m_w_in, ssd_conv_w=m_ssd_conv_w, ssd_conv_b=m_ssd_conv_b, ssd_dt_bias=m_ssd_dt_bias,
              ssd_a_log=m_ssd_a_log, ssd_d=m_ssd_d, ssd_norm=m_ssd_norm, hg_lower_bound=m_hg_lower_bound, hg_norm=m_hg_norm,
              w_branch_a=m_w_branch_a, w_branch_b=m_w_branch_b, w_out=m_w_out, ffn2_norm=m_ffn2_norm, ffn2_w_gu=m_ffn2_w_gu,
              ffn2_w_down=m_ffn2_w_down, final_norm=m_final_norm)
    Vo = dict(meta_tokens=v_meta_tokens, ffn1_norm=v_ffn1_norm, ffn1_w_gu=v_ffn1_w_gu, ffn1_w_down=v_ffn1_w_down,
              mix_norm=v_mix_norm, w_in=v_w_in, ssd_conv_w=v_ssd_conv_w, ssd_conv_b=v_ssd_conv_b, ssd_dt_bias=v_ssd_dt_bias,
              ssd_a_log=v_ssd_a_log, ssd_d=v_ssd_d, ssd_norm=v_ssd_norm, hg_lower_bound=v_hg_lower_bound, hg_norm=v_hg_norm,
              w_branch_a=v_w_branch_a, w_branch_b=v_w_branch_b, w_out=v_w_out, ffn2_norm=v_ffn2_norm, ffn2_w_gu=v_ffn2_w_gu,
              ffn2_w_down=v_ffn2_w_down, final_norm=v_final_norm)
    grads, deltas, new_m, new_v = {}, {}, {}, {}
    big_names = ["ffn1_w_gu", "ffn1_w_down", "w_in", "w_branch_a", "w_branch_b", "w_out", "ffn2_w_gu", "ffn2_w_down"]
    transposed = ("ffn1_w_gu", "ffn2_w_gu", "w_in")
    small_all = None
    for nm, part in zip(big_names, parts):
        view = (lambda a: a[0].T) if nm in transposed else (lambda a: a[0])
        back = (lambda o: o.T[None]) if nm in transposed else (lambda o: o[None])
        outs = _adamw("adamw_" + nm, part, view(W[nm]), view(Mo[nm]), view(Vo[nm]),
                      comm=("gather", [small_packed]) if small_all is None else None)
        if small_all is None:
            small_all = outs[4]
        grads[nm], deltas[nm], new_m[nm], new_v[nm] = (back(o) for o in outs[:4])
    unpacked = _unpack_rows(_sum_parts("sum_small_grads", small_all), small_like)
    g_small = unpacked[:len(small_grads)]
    g_meta_full = unpacked[len(small_grads)]
    for b in range(1, Bl):
        g_meta_full = g_meta_full + unpacked[len(small_grads) + b]
    g_meta = lax.dynamic_slice_in_dim(g_meta_full, me * (D // N_DEV), D // N_DEV, axis=1)
    g_conv_w = lax.dynamic_slice_in_dim(g_small[11], me * (SSD_CONV_CH // N_DEV), SSD_CONV_CH // N_DEV, axis=1)
    loss = unpacked[-1].reshape(())
    small_names = ["ffn1_norm", "mix_norm", "ssd_conv_b", "ssd_dt_bias", "ssd_a_log", "ssd_d", "ssd_norm", "hg_lower_bound",
                   "hg_norm", "ffn2_norm", "final_norm", "ssd_conv_w", "meta_tokens"]
    small_g = g_small[:11] + [g_conv_w.reshape(ssd_conv_w.shape), g_meta]
    pk = lambda d: _pack_rows([d[nm] for nm in small_names])
    outs = _adamw("adamw_small", _pack_rows(small_g)[None], pk(W), pk(Mo), pk(Vo))
    like = [W[nm] for nm in small_names]
    for dst, o in zip((grads, deltas, new_m, new_v), outs):
        for nm, val in zip(small_names, _unpack_rows(o, like)):
            dst[nm] = val

    return (loss, grad_x, *[grads[nm] for nm in names], *[deltas[nm] for nm in names],
            *[new_m[nm] for nm in names], *[new_v[nm] for nm in names])
```

```python
import functools

import jax
import jax.numpy as jnp
from jax import lax
from jax.experimental import pallas as pl
from jax.experimental.pallas import tpu as pltpu

F32, BF16 = jnp.float32, jnp.bfloat16
NN, NT, TN = ((1,), (0,)), ((1,), (1,)), ((0,), (0,))
MESH_AXES = ("x", "y", "c")
N_DEV = 8

D_MODEL = 1024
N_META = 16
EPS = 1e-6
SSD_HEADS, SSD_HEAD_DIM, SSD_GROUPS, SSD_STATE, SSD_CONV, Q = 16, 64, 4, 128, 4, 128
SSD_INNER = SSD_HEADS * SSD_HEAD_DIM
SSD_CONV_CH = SSD_INNER + 2 * SSD_GROUPS * SSD_STATE
HG_WIDTH, HG_HEADS, HG_CHUNK = 1024, 8, 16
PAD = Q - N_META
N_MAIN = 9 * 1024
ADAM_LR, ADAM_B1, ADAM_B2, ADAM_EPS, ADAM_WD, ADAM_STEP = 0.001, 0.9, 0.999, 1e-08, 0.01, 10
VMEM_LIMIT = 52 * 1024 * 1024


def _dot(a, b, dims):
    return lax.dot_general(a, b, (dims, ((), ())), preferred_element_type=F32)


def _dot01(a, b, dims, sel):
    x = b if sel == "a" else a
    hi = x.astype(BF16)
    r1 = x - hi.astype(F32)
    mid = r1.astype(BF16)
    lo = (r1 - mid.astype(F32)).astype(BF16)
    s = (a if sel == "a" else b).astype(BF16)
    parts = [_dot(s, p, dims) if sel == "a" else _dot(p, s, dims) for p in (hi, mid, lo)]
    return parts[0] + parts[1] + parts[2]


def _sigmoid(x):
    return 1.0 / (1.0 + jnp.exp(-x))


def _dsilu(x, s):
    return s * (1.0 + x * (1.0 - s))


def _softplus(x):
    e = jnp.exp(-jnp.abs(x))
    u = 1.0 + e
    log1p_e = jnp.where(u == 1.0, e, jnp.log(u) * e / (u - 1.0))
    return jnp.maximum(x, 0.0) + log1p_e


def _params(sem):
    return pltpu.CompilerParams(dimension_semantics=sem, vmem_limit_bytes=VMEM_LIMIT)


def _tile(n, prefs):
    for p in prefs:
        if n % p == 0:
            return p
    return n


CHIP_FLIPS = ((1, 0), (0, 1), (1, 1))
N_PEER = N_DEV - 1


def _comm_gather(srcs, outs, send_sems, recv_sems, local_sems):
    n = len(srcs)
    x, y, c = (lax.axis_index(a) for a in MESH_AXES)
    dev = lambda px, py, pc: 4 * px + 2 * py + pc
    me, sib = dev(x, y, c), (x, y, 1 - c)
    nbr_x, nbr_y, diag = (1 - x, y), (x, 1 - y), (1 - x, 1 - y)
    via = (x ^ c, y ^ (1 - c), c)
    sent_on = dev(x ^ (1 - c), y ^ c, c)

    def rc(w, k, slot, to, src=None):
        return pltpu.make_async_remote_copy(
            src_ref=outs[w].at[slot] if src is None else src, dst_ref=outs[w].at[slot],
            send_sem=send_sems.at[w, k], recv_sem=recv_sems.at[w, k], device_id=to, device_id_type=pl.DeviceIdType.MESH)

    def local(w):
        return pltpu.make_async_copy(srcs[w], outs[w].at[me], local_sems.at[w])

    def start():
        for w in range(n):
            local(w).start()
            rc(w, 0, me, sib, src=srcs[w]).start()
            rc(w, 1, me, (*nbr_x, c), src=srcs[w]).start()
            rc(w, 2, me, (*nbr_y, c), src=srcs[w]).start()

    def pass_on():
        for w in range(n):
            rc(w, 1, dev(*nbr_x, c), sib).wait_recv()
            rc(w, 2, dev(*nbr_y, c), sib).wait_recv()
            rc(w, 3, sent_on, via).start()
            rc(w, 4, dev(*nbr_x, c), sib).start()
            rc(w, 5, dev(*nbr_y, c), sib).start()

    def pass_on_diagonal():
        for w in range(n):
            rc(w, 3, dev(*diag, c), sib).wait_recv()
            rc(w, 6, dev(*diag, c), sib).start()

    def finish():
        for w in range(n):
            rc(w, 0, dev(x, y, 1 - c), sib).wait_recv()
            for k, chip in ((4, nbr_x), (5, nbr_y), (6, diag)):
                rc(w, k, dev(*chip, 1 - c), sib).wait_recv()
            for k in range(N_PEER):
                rc(w, k, me, sib, src=srcs[w]).wait_send()
            local(w).wait()

    return start, (pass_on, pass_on_diagonal), finish


def _comm_scatter(srcs, outs, send_sems, recv_sems, local_sems):
    n = len(srcs)
    x, y, c = (lax.axis_index(a) for a in MESH_AXES)
    me = 4 * x + 2 * y + c

    def copies():
        out = []
        for w in range(n):
            out.append(pltpu.make_async_copy(srcs[w].at[me], outs[w].at[me], local_sems.at[w]))
            for k in range(1, N_DEV):
                px, py, pc = x ^ (k >> 2), y ^ ((k >> 1) & 1), c ^ (k & 1)
                out.append(pltpu.make_async_remote_copy(
                    src_ref=srcs[w].at[4 * px + 2 * py + pc], dst_ref=outs[w].at[me],
                    send_sem=send_sems.at[w, k - 1], recv_sem=recv_sems.at[w, k - 1],
                    device_id=(px, py, pc), device_id_type=pl.DeviceIdType.MESH))
        return out

    def start():
        for cp in copies():
            cp.start()

    def finish():
        for cp in copies():
            cp.wait()

    return start, None, finish


def _comm_swap(srcs, outs, send_sems, recv_sems, local_sems):
    x, y, c = (lax.axis_index(a) for a in MESH_AXES)

    def copies():
        return [pltpu.make_async_remote_copy(
            src_ref=srcs[w].at[1 - c], dst_ref=outs[w], send_sem=send_sems.at[w, 0], recv_sem=recv_sems.at[w, 0],
            device_id=(x, y, 1 - c), device_id_type=pl.DeviceIdType.MESH) for w in range(len(srcs))]

    def start():
        for cp in copies():
            cp.start()

    def finish():
        for cp in copies():
            cp.wait()

    return start, None, finish


def _comm_chips(srcs, outs, send_sems, recv_sems, local_sems):
    n = len(srcs)
    x, y, c = (lax.axis_index(a) for a in MESH_AXES)
    mine = 2 * x + y

    def copies():
        out = []
        for w in range(n):
            out.append(pltpu.make_async_copy(srcs[w].at[mine], outs[w].at[mine], local_sems.at[w]))
            for j, (fx, fy) in enumerate(CHIP_FLIPS):
                px, py = x ^ fx, y ^ fy
                out.append(pltpu.make_async_remote_copy(
                    src_ref=srcs[w].at[2 * px + py], dst_ref=outs[w].at[mine],
                    send_sem=send_sems.at[w, j], recv_sem=recv_sems.at[w, j],
                    device_id=(px, py, c), device_id_type=pl.DeviceIdType.MESH))
        return out

    def start():
        for cp in copies():
            cp.start()

    def finish():
        for cp in copies():
            cp.wait()

    return start, None, finish


def _comm_parts(comm):
    kind, arrays = comm[:2]
    n = len(arrays)
    lead = {"gather": lambda a: (N_DEV,) + a.shape, "scatter": lambda a: (N_DEV,) + a.shape[1:],
            "swap": lambda a: a.shape[1:], "chips": lambda a: a.shape}[kind]
    shapes = [jax.ShapeDtypeStruct(lead(a), a.dtype) for a in arrays]
    sems = [pltpu.SemaphoreType.DMA((n, N_PEER)), pltpu.SemaphoreType.DMA((n, N_PEER)), pltpu.SemaphoreType.DMA((n,))]
    make = {"gather": _comm_gather, "scatter": _comm_scatter, "swap": _comm_swap, "chips": _comm_chips}[kind]
    return n, shapes, sems, make


def _exchange(name, kind, arrays):
    n, shapes, sems, make = _comm_parts((kind, arrays))

    def body(*refs):
        start, middle, finish = make(refs[:n], refs[n:2 * n], *refs[2 * n:])
        start()
        for stage in middle or ():
            stage()
        finish()

    any_spec = pl.BlockSpec(memory_space=pl.ANY)
    return pl.pallas_call(
        body, name=name, in_specs=[any_spec] * n, out_specs=[any_spec] * n, out_shape=shapes, scratch_shapes=sems,
        compiler_params=pltpu.CompilerParams(has_side_effects=True),
    )(*arrays)


def _call(body, *, name, grid, in_specs, out_specs, out_shape, scratch, sem, args, comm=None, into=None):
    any_spec = pl.BlockSpec(memory_space=pl.ANY)
    in_specs, args, aliases, n_body_in = list(in_specs), list(args), {}, len(in_specs)
    if into is not None:
        in_specs.append(any_spec)
        args.append(into[0])
        aliases = {n_body_in: into[1]}
    n_in, n_out, n_scr = len(in_specs), len(out_specs), len(scratch)
    if comm is None:
        def plain(*refs):
            body(*refs[:n_body_in], *refs[n_in:])

        return pl.pallas_call(plain, name=name, grid=grid, in_specs=in_specs, out_specs=out_specs, out_shape=out_shape,
                              scratch_shapes=scratch, input_output_aliases=aliases, compiler_params=_params(sem))(*args)
    n, shapes, sems, make = _comm_parts(comm)

    def carrier(*refs):
        ins, csrc = refs[:n_body_in], refs[n_in:n_in + n]
        outs, cout = refs[n_in + n:n_in + n + n_out], refs[n_in + n + n_out:n_in + 2 * n + n_out]
        rest = refs[n_in + 2 * n + n_out:]
        start, middle, finish = make(csrc, cout, *rest[n_scr:])
        ids = [pl.program_id(a) for a in range(len(grid))]
        step = functools.reduce(lambda acc, ig: acc * ig[1] + ig[0], zip(ids, grid), 0)
        n_steps = functools.reduce(lambda a, b: a * b, grid, 1)
        pl.when(step == 0)(start)
        body(*ins, *outs, *rest[:n_scr])
        if middle:
            pl.when(step == max(0, (3 * n_steps) // 4 - 1))(middle[0])
            pl.when(step == n_steps - 1)(middle[1])
        pl.when(step == n_steps - 1)(finish)

    return pl.pallas_call(
        carrier, name=name, grid=grid, in_specs=in_specs + [any_spec] * n,
        out_specs=list(out_specs) + [any_spec] * n, out_shape=list(out_shape) + shapes,
        scratch_shapes=list(scratch) + sems, input_output_aliases=aliases,
        compiler_params=pltpu.CompilerParams(dimension_semantics=("arbitrary",) * len(grid),
                                             vmem_limit_bytes=VMEM_LIMIT, has_side_effects=True),
    )(*args, *comm[1])


def _fused_matmul(name, M, N, K, pairs, extras, epilogue, out_dtypes, n_acc, tm, tn, tk, outer="i", comm=None,
                  stack=False, vecs=(), row_sums=0, wide=None, sub=None):
    nk = K // tk
    n_pairs, n_ex, n_out = len(pairs), len(extras), len(out_dtypes)
    assert not row_sums or (outer == "i" and N == tn)

    def ij(g0, g1):
        return (g0, g1) if outer == "i" else (g1, g0)

    in_specs, args = [], []
    for p in pairs:
        ao, bk, bn = p.get("a_off", 0), p.get("bk_off", 0), p.get("bn_off", 0)
        mode = dict(pipeline_mode=pl.Buffered(1)) if p.get("resident") else {}
        in_specs.append(pl.BlockSpec((tm, tk), lambda g0, g1, k, ao=ao: (ij(g0, g1)[0], k + ao)))
        if "b_shift" in p:
            first, shift = p["b_shift"]
            if p.get("trans_b"):
                in_specs.append(pl.BlockSpec(
                    (pl.Element(tn), pl.Element(tk)),
                    lambda g0, g1, k, bk=bk: (
                        pl.multiple_of(ij(g0, g1)[1] * tn + jnp.where(ij(g0, g1)[1] >= first, shift, 0), 16),
                        (k + bk) * tk)))
            else:
                in_specs.append(pl.BlockSpec(
                    (pl.Element(tk), pl.Element(tn)),
                    lambda g0, g1, k, bn=bn: (pl.multiple_of(k * tk + jnp.where(k >= first, shift, 0), 16),
                                              (ij(g0, g1)[1] + bn) * tn)))
        elif p.get("trans_b"):
            in_specs.append(pl.BlockSpec((tn, tk), lambda g0, g1, k, bk=bk, bn=bn: (ij(g0, g1)[1] + bn, k + bk), **mode))
        else:
            in_specs.append(pl.BlockSpec((tk, tn), lambda g0, g1, k, bk=bk, bn=bn: (k + bk, ij(g0, g1)[1] + bn), **mode))
        args += [p["a"], p["b"]]
    for arr, off in extras:
        in_specs.append(pl.BlockSpec((tm, tn), lambda g0, g1, k, off=off: (ij(g0, g1)[0], ij(g0, g1)[1] + off)))
        args.append(arr)
    for arr in vecs:
        in_specs.append(pl.BlockSpec((1, tn), lambda g0, g1, k: (0, ij(g0, g1)[1])))
        args.append(arr)
    if stack:
        assert N == tn
        out_specs = [pl.BlockSpec((tm, n_out * tn), lambda g0, g1, k: (ij(g0, g1)[0], 0))]
        out_shape = [jax.ShapeDtypeStruct((M, n_out * N), out_dtypes[0])]
    else:
        out_specs = [pl.BlockSpec((tm, tn), lambda g0, g1, k: ij(g0, g1)) for _ in out_dtypes]
        out_shape = [jax.ShapeDtypeStruct((M, N), dt) for dt in out_dtypes]
    if wide:
        out_specs.append(pl.BlockSpec((pl.Element(tm), pl.Element(wide["width"])),
                                      lambda g0, g1, k: (pl.multiple_of(ij(g0, g1)[0] * tm, 16), wide["col"])))
        out_shape.append(jax.ShapeDtypeStruct((M, wide["total"]), wide["dtype"]))
    n_tile_out = len(out_specs)
    out_specs += [pl.BlockSpec((1, tn), lambda g0, g1, k: (0, 0)) for _ in range(row_sums)]
    out_shape += [jax.ShapeDtypeStruct((1, N), F32) for _ in range(row_sums)]
    grid = (M // tm, N // tn, nk) if outer == "i" else (N // tn, M // tm, nk)
    n_in = 2 * n_pairs + n_ex + len(vecs)

    def partials(refs, cs=slice(None)):
        accs = [None] * n_acc
        for idx, p in enumerate(pairs):
            b_ref = refs[2 * idx + 1]
            d = (_dot(refs[2 * idx][...], b_ref[cs, :], NT) if p.get("trans_b")
                 else _dot(refs[2 * idx][...], b_ref[:, cs], NN))
            accs[p["acc"]] = d if accs[p["acc"]] is None else accs[p["acc"]] + d
        return accs

    def finish(accs, refs, first_rows, cs=slice(None)):
        res = epilogue(accs, [r[:, cs] for r in refs[2 * n_pairs:n_in]])
        if stack:
            o = refs[n_in]
            for idx in range(n_out):
                lo = idx * tn + (cs.start or 0)
                o[:, lo:lo + (tn if cs.stop is None else cs.stop - cs.start)] = res[idx].astype(o.dtype)
        else:
            for o, r in zip(refs[n_in:n_in + n_out], res):
                o[:, cs] = r.astype(o.dtype)
        if wide:
            o = refs[n_in + n_tile_out - 1]
            o[...] = res[n_out].astype(o.dtype)
        for o, r in zip(refs[n_in + n_tile_out:n_in + n_tile_out + row_sums], res[n_out + bool(wide):]):
            @pl.when(first_rows)
            def _(o=o, r=r):
                o[...] = r

            @pl.when(jnp.logical_not(first_rows))
            def _(o=o, r=r):
                o[...] += r

    if nk == 1 and sub:
        assert not wide and not row_sums and tn % sub == 0

        def body(*refs):
            for c in range(tn // sub):
                cs = slice(c * sub, (c + 1) * sub)
                finish(partials(refs, cs), refs, None, cs)
        scratch = []
    elif nk == 1:
        def body(*refs):
            finish(partials(refs), refs, pl.program_id(0) == 0)
        scratch = []
    else:
        def body(*refs):
            acc_refs = refs[-n_acc:]
            k = pl.program_id(2)
            first_rows = pl.program_id(0) == 0
            new = partials(refs)

            @pl.when(k == 0)
            def _():
                for a, v in zip(acc_refs, new):
                    a[...] = v

            @pl.when(k > 0)
            def _():
                for a, v in zip(acc_refs, new):
                    a[...] += v

            @pl.when(k == nk - 1)
            def _():
                finish([a[...] for a in acc_refs], refs, first_rows)
        scratch = [pltpu.VMEM((tm, tn), F32) for _ in range(n_acc)]

    return _call(body, name=name, grid=grid, in_specs=in_specs, out_specs=out_specs, out_shape=out_shape,
                 scratch=scratch, sem=("parallel", "parallel", "arbitrary"), args=args, comm=comm)


def _matmul_tn(name, x, y, t1, t2, tr, scale=1.0, comm=None, out_dtype=BF16, out_skip=None):
    R, K1 = x.shape
    N1 = y.shape[1]
    nr, n1 = R // tr, K1 // t1
    x_spec = pl.BlockSpec((tr, t1), lambda i, j, r: (r, i))
    rows_out = K1
    o_spec = pl.BlockSpec((t1, t2), lambda i, j, r: (i, j))
    if out_skip:
        row, count = out_skip
        rows_out += count
        o_spec = pl.BlockSpec(
            (pl.Element(t1), pl.Element(t2)),
            lambda i, j, r: (pl.multiple_of(i * t1 + jnp.where(i * t1 >= row, count, 0), 16), j * t2))

    def body(x_ref, y_ref, o_ref, *acc):
        d = _dot(x_ref[...], y_ref[...], TN)
        if nr == 1:
            o_ref[...] = (d * scale).astype(o_ref.dtype)
            return
        r = pl.program_id(2)

        @pl.when(r == 0)
        def _():
            acc[0][...] = d

        @pl.when(jnp.logical_and(r > 0, r < nr - 1))
        def _():
            acc[0][...] += d

        @pl.when(r == nr - 1)
        def _():
            o_ref[...] = ((acc[0][...] + d) * scale).astype(o_ref.dtype)

    return _call(
        body, name=name, grid=(n1, N1 // t2, nr),
        in_specs=[x_spec, pl.BlockSpec((tr, t2), lambda i, j, r: (r, j))], out_specs=[o_spec],
        out_shape=[jax.ShapeDtypeStruct((rows_out, N1), out_dtype)],
        scratch=[pltpu.VMEM((t1, t2), F32)] if nr > 1 else [],
        sem=("parallel", "parallel", "arbitrary"), args=(x, y), comm=comm)


def _embed_norm(x, meta, w, comm=None):
    Bl, S, D = x.shape
    nb = (PAD + N_META + S) // Q
    M = Bl * nb * Q

    def body(x_ref, meta_ref, w_ref, h_ref, n_ref):
        head = jnp.concatenate([jnp.zeros((PAD, D), F32), meta_ref[...]], axis=0)
        h = jnp.where(pl.program_id(1) == 0, head, x_ref[0])
        h_ref[...] = h
        n_ref[...] = _rmsnorm_tile(h, w_ref[...]).astype(n_ref.dtype)

    row = pl.BlockSpec((Q, D), lambda b, t: (b * nb + t, 0))
    return _call(
        body, name="embed_norm", grid=(Bl, nb),
        in_specs=[pl.BlockSpec((1, Q, D), lambda b, t: (b, jnp.maximum(t - 1, 0), 0)),
                  pl.BlockSpec((N_META, D), lambda b, t: (0, 0)), pl.BlockSpec((1, D), lambda b, t: (0, 0))],
        out_specs=[row, row], out_shape=[jax.ShapeDtypeStruct((M, D), F32), jax.ShapeDtypeStruct((M, D), BF16)],
        scratch=[], sem=("parallel", "parallel"), args=(x, meta, w), comm=comm)


def _rmsnorm_bwd_tile(dn, h, w, dh_in):
    r = lax.rsqrt(jnp.mean(h * h, axis=-1, keepdims=True) + EPS)
    xhat = h * r
    gw = dn * w
    dh = dh_in + r * (gw - xhat * jnp.mean(gw * xhat, axis=-1, keepdims=True))
    return dh, jnp.sum(dn * xhat, axis=0, keepdims=True)


def _loss_head(h, w, target, Bl, nb):
    M, D = h.shape
    nt = 4 if (nb * Q) % 32 == 0 and nb * Q // 4 >= Q else nb
    half = nb * Q // nt

    def body(h_ref, w_ref, t_ref, dh_ref, dhb_ref, dw_ref, loss_ref):
        b, t = pl.program_id(0), pl.program_id(1)
        row = lax.broadcasted_iota(jnp.int32, (half, 1), 0)
        live = jnp.logical_or(t > 0, row >= Q).astype(F32)
        x = h_ref[...]
        r = lax.rsqrt(jnp.mean(x * x, axis=-1, keepdims=True) + EPS)
        xhat = x * r
        wv = w_ref[...]
        tgt = t_ref[0]
        tgt = jnp.where(t == 0, pltpu.roll(tgt, Q, 0), tgt)
        err = (xhat * wv - tgt) * live
        dy = err * (1.0 / D)
        gw = dy * wv
        dx = r * (gw - xhat * jnp.mean(gw * xhat, axis=-1, keepdims=True))
        dh_ref[...] = dx
        dhb_ref[...] = dx.astype(BF16)
        dw = jnp.sum(dy * xhat, axis=0, keepdims=True)
        part = 0.5 * jnp.sum(jnp.sum(err * err, axis=-1, keepdims=True) * (1.0 / D), axis=0, keepdims=True)
        first = jnp.logical_and(b == 0, t == 0)

        @pl.when(first)
        def _():
            dw_ref[...] = dw
            loss_ref[...] = jnp.broadcast_to(part, loss_ref.shape)

        @pl.when(jnp.logical_not(first))
        def _():
            dw_ref[...] += dw
            loss_ref[...] += jnp.broadcast_to(part, loss_ref.shape)

    row = pl.BlockSpec((half, D), lambda b, t: (b * nt + t, 0))
    vec = pl.BlockSpec((1, D), lambda b, t: (0, 0))
    return pl.pallas_call(
        body, name="loss_head", grid=(Bl, nt),
        in_specs=[row, vec, pl.BlockSpec((pl.Element(1), pl.Element(half), pl.Element(D)),
                                         lambda b, t: (b, pl.multiple_of(jnp.maximum(t * half - Q, 0), 8), 0))],
        out_specs=[row, row, vec, pl.BlockSpec((8, 128), lambda b, t: (0, 0))],
        out_shape=[jax.ShapeDtypeStruct((M, D), F32), jax.ShapeDtypeStruct((M, D), BF16),
                   jax.ShapeDtypeStruct((1, D), F32), jax.ShapeDtypeStruct((8, 128), F32)],
        compiler_params=_params(("arbitrary", "arbitrary")),
    )(h, w, target)


CONV_TC = 256


def _conv_pre(xr_ref, w_ref, b_ref):
    x = xr_ref[...].astype(F32)
    acc = b_ref[...] + w_ref[SSD_CONV - 1:SSD_CONV, :] * x
    for k in range(1, SSD_CONV):
        acc = acc + w_ref[SSD_CONV - 1 - k:SSD_CONV - k, :] * pltpu.roll(x, k, 0)
    return x, acc


def _conv_fwd(proj, w, b, Bl, T):
    M = proj.shape[0]
    off = 1024 // CONV_TC

    def body(xr_ref, w_ref, b_ref, o_ref):
        _, acc = _conv_pre(xr_ref, w_ref, b_ref)
        row = lax.broadcasted_iota(jnp.int32, acc.shape, 0)
        o_ref[...] = jnp.where(row >= PAD, acc * _sigmoid(acc), 0.0).astype(o_ref.dtype)

    return pl.pallas_call(
        body, name="conv_fwd", grid=(Bl, SSD_CONV_CH // CONV_TC),
        in_specs=[pl.BlockSpec((T, CONV_TC), lambda bb, j: (bb, j + off)),
                  pl.BlockSpec((SSD_CONV, CONV_TC), lambda bb, j: (0, j)), pl.BlockSpec((1, CONV_TC), lambda bb, j: (0, j))],
        out_specs=pl.BlockSpec((T, CONV_TC), lambda bb, j: (bb, j)),
        out_shape=jax.ShapeDtypeStruct((M, SSD_CONV_CH), BF16), compiler_params=_params(("parallel", "parallel")),
    )(proj, w, b)


def _conv_bwd(proj, w, b, dxc, dproj, Bl, T):
    M = proj.shape[0]
    off = 1024 // CONV_TC

    def body(xr_ref, w_ref, b_ref, d_ref, dx_ref, dw_ref, db_ref):
        x, acc = _conv_pre(xr_ref, w_ref, b_ref)
        row = lax.broadcasted_iota(jnp.int32, acc.shape, 0)
        s = _sigmoid(acc)
        dpre = jnp.where(row >= PAD, d_ref[...].astype(F32) * _dsilu(acc, s), 0.0)
        dx = w_ref[SSD_CONV - 1:SSD_CONV, :] * dpre
        dws = [jnp.sum(dpre * x, axis=0, keepdims=True)]
        for k in range(1, SSD_CONV):
            dx = dx + w_ref[SSD_CONV - 1 - k:SSD_CONV - k, :] * pltpu.roll(dpre, T - k, 0)
            dws.append(jnp.sum(dpre * pltpu.roll(x, k, 0), axis=0, keepdims=True))
        dx_ref[...] = dx.astype(dx_ref.dtype)
        dw = jnp.concatenate(dws[::-1], axis=0)
        db = jnp.sum(dpre, axis=0, keepdims=True)

        @pl.when(pl.program_id(1) == 0)
        def _():
            dw_ref[...] = dw
            db_ref[...] = db

        @pl.when(pl.program_id(1) > 0)
        def _():
            dw_ref[...] += dw
            db_ref[...] += db

    return _call(
        body, name="conv_bwd", grid=(SSD_CONV_CH // CONV_TC, Bl),
        in_specs=[pl.BlockSpec((T, CONV_TC), lambda j, bb: (bb, j + off)),
                  pl.BlockSpec((SSD_CONV, CONV_TC), lambda j, bb: (0, j)), pl.BlockSpec((1, CONV_TC), lambda j, bb: (0, j)),
                  pl.BlockSpec((T, CONV_TC), lambda j, bb: (bb, j))],
        out_specs=[pl.BlockSpec((T, CONV_TC), lambda j, bb: (bb, j + off)),
                   pl.BlockSpec((SSD_CONV, CONV_TC), lambda j, bb: (0, j)), pl.BlockSpec((1, CONV_TC), lambda j, bb: (0, j))],
        out_shape=[jax.ShapeDtypeStruct(dproj.shape, BF16), jax.ShapeDtypeStruct((SSD_CONV, SSD_CONV_CH), F32),
                   jax.ShapeDtypeStruct((1, SSD_CONV_CH), F32)],
        scratch=[], sem=("parallel", "arbitrary"), args=(proj, w, b, dxc), into=(dproj, 0))


N_PAIR = SSD_HEADS // 2
HPG = SSD_HEADS // SSD_GROUPS
GW = SSD_INNER // SSD_GROUPS


def _per_group(fn, *arrs):
    return jnp.concatenate([jnp.broadcast_to(fn(*(a[:, GW * g:GW * (g + 1)] for a in arrs)), (arrs[0].shape[0], GW))
                            for g in range(SSD_GROUPS)], axis=1)


def _ssd_prep(c, dtr_ref, bias_ref, alog_ref, d_ref):
    row = lax.broadcasted_iota(jnp.int32, (Q, 128), 0)
    col = lax.broadcasted_iota(jnp.int32, (Q, 128), 1)
    live = col < SSD_HEADS
    valid = jnp.logical_and(jnp.logical_or(c > 0, row >= PAD), live)
    pre = dtr_ref[...] + bias_ref[...]
    dt = jnp.where(valid, _softplus(pre), 0.0)
    A = jnp.where(live[0:1], -jnp.exp(alog_ref[...]), 0.0)
    tri = row >= col
    eye = (row == col).astype(BF16)
    cs = _dot01(tri, dt * A, NN, "a")
    cst = _dot01(eye, cs, NT, "a")
    spread = (lax.broadcasted_iota(jnp.int32, (128, SSD_INNER), 0)
              == lax.broadcasted_iota(jnp.int32, (128, SSD_INNER), 1) // SSD_HEAD_DIM).astype(BF16)
    dt_w = _dot01(dt, spread, NN, "b")
    cs_w = _dot01(cs, spread, NN, "b")
    d_w = _dot01(jnp.broadcast_to(d_ref[...], (8, 128)), spread, NN, "b")[0:1]
    lane = lax.broadcasted_iota(jnp.int32, (Q, SSD_INNER), 1)
    first = (lane % 128) < SSD_HEAD_DIM
    return dict(row=row, col=col, valid=valid, pre=pre, dt=dt, A=A, tri=tri, eye=eye, cs=cs, cst=cst, spread=spread,
                dt_w=dt_w, cs_w=cs_w, d_w=d_w, ecs_w=jnp.exp(cs_w), decay_w=jnp.exp(cs_w[Q - 1:Q] - cs_w), first=first)


def _ssd_chunk(xc_ref, s, states):
    xv = xc_ref[:, 0:SSD_INNER].astype(F32)
    Bs = [xc_ref[:, SSD_INNER + 128 * g:SSD_INNER + 128 * (g + 1)] for g in range(SSD_GROUPS)]
    Cs = [xc_ref[:, SSD_INNER + 512 + 128 * g:SSD_INNER + 512 + 128 * (g + 1)] for g in range(SSD_GROUPS)]
    X = xv * s["dt_w"]
    X0 = jnp.where(s["first"], X, 0.0)
    Xb = (X0.astype(BF16), (X - X0).astype(BF16))
    Xd = (X * s["decay_w"]).astype(BF16)
    CB = [_dot(Cs[g], Bs[g], NT) for g in range(SSD_GROUPS)]
    Lms = [jnp.exp(jnp.where(s["tri"], s["cs"][:, h:h + 1] - s["cst"][h:h + 1, :], -jnp.inf)) for h in range(SSD_HEADS)]
    Ms = [CB[h // HPG] * Lms[h] for h in range(SSD_HEADS)]
    Mb = [m.astype(BF16) for m in Ms]
    prev_b = [st.astype(BF16) for st in states]
    yds, yos, sts = [], [], []
    for p in range(N_PAIR):
        g, ln = p // 2, slice(128 * p, 128 * (p + 1))
        yds.append(_dot(Mb[2 * p], Xb[0][:, ln], NN) + _dot(Mb[2 * p + 1], Xb[1][:, ln], NN))
        yos.append(_dot(Cs[g], prev_b[p], NT))
        sts.append(_dot(Xd[:, ln], Bs[g], TN))
    yo = jnp.concatenate(yos, axis=1)
    y = jnp.concatenate(yds, axis=1) + yo * s["ecs_w"] + xv * s["d_w"]
    upper = s["row"] < SSD_HEAD_DIM
    cl = s["cs"][Q - 1:Q, :]
    ecl_rows = [jnp.where(upper, jnp.exp(cl[:, 2 * p:2 * p + 1]), jnp.exp(cl[:, 2 * p + 1:2 * p + 2])) for p in range(N_PAIR)]
    new_states = [states[p] * ecl_rows[p] + sts[p] for p in range(N_PAIR)]
    return y, new_states, dict(xv=xv, Bs=Bs, Cs=Cs, X=X, Xb=Xb, CB=CB, Lms=Lms, Ms=Ms, Mb=Mb, prev_b=prev_b, yo=yo,
                               ecl_rows=ecl_rows)


def _ssd_in_specs(nc, rev=False):
    rb = (lambda b, c: b * nc + nc - 1 - c) if rev else (lambda b, c: b * nc + c)
    vec = pl.BlockSpec((1, 128), lambda b, c: (0, 0))
    return [pl.BlockSpec((Q, SSD_CONV_CH), lambda b, c: (rb(b, c), 0)),
            pl.BlockSpec((Q, 128), lambda b, c: (rb(b, c), 0)),
            pl.BlockSpec((Q, SSD_INNER), lambda b, c: (rb(b, c), 0)),
            vec, vec, vec, pl.BlockSpec((1, SSD_INNER), lambda b, c: (0, 0))]


def _ssd_fwd(xc, dtr, proj, bias_p, alog_p, d_p, nw, Bl, nc):
    M = xc.shape[0]

    def body(xc_ref, dtr_ref, z_ref, bias_ref, alog_ref, d_ref, nw_ref, y_ref, prev_ref, state):
        c = pl.program_id(1)

        @pl.when(c == 0)
        def _():
            state[...] = jnp.zeros_like(state)

        s = _ssd_prep(c, dtr_ref, bias_ref, alog_ref, d_ref)
        states = [state[p] for p in range(N_PAIR)]
        y, new_states, _ = _ssd_chunk(xc_ref, s, states)
        for p in range(N_PAIR):
            prev_ref[0, 0, p] = states[p]
            state[p] = new_states[p]
        zz = z_ref[...].astype(F32)
        yg = y * zz * _sigmoid(zz)
        r = _per_group(lambda a: lax.rsqrt(jnp.mean(a * a, axis=-1, keepdims=True) + EPS), yg)
        y_ref[...] = (yg * r * nw_ref[...]).astype(y_ref.dtype)

    return pl.pallas_call(
        body, name="ssd_fwd", grid=(Bl, nc), in_specs=_ssd_in_specs(nc),
        out_specs=[pl.BlockSpec((Q, SSD_INNER), lambda b, c: (b * nc + c, 0)),
                   pl.BlockSpec((1, 1, N_PAIR, 128, 128), lambda b, c: (b, c, 0, 0, 0))],
        out_shape=[jax.ShapeDtypeStruct((M, SSD_INNER), BF16), jax.ShapeDtypeStruct((Bl, nc, N_PAIR, 128, 128), F32)],
        scratch_shapes=[pltpu.VMEM((N_PAIR, 128, 128), F32)],
        compiler_params=_params(("arbitrary", "arbitrary")),
    )(xc, dtr, proj, bias_p, alog_p, d_p, nw)


def _ssd_bwd(xc, dtr, proj, bias_p, alog_p, d_p, nw, prev, dya, dproj, Bl, nc, comm=None):
    M = xc.shape[0]

    def body(xc_ref, dtr_ref, z_ref, bias_ref, alog_ref, d_ref, nw_ref, prev_ref, dy_ref,
             dxc_ref, dz_ref, ddtr_ref, dbias_ref, dalog_ref, dd_ref, dnw_ref, dS):
        b, t = pl.program_id(0), pl.program_id(1)

        @pl.when(t == 0)
        def _():
            dS[...] = jnp.zeros_like(dS)

        s = _ssd_prep(nc - 1 - t, dtr_ref, bias_ref, alog_ref, d_ref)
        states = [prev_ref[0, 0, p] for p in range(N_PAIR)]
        y, _, k = _ssd_chunk(xc_ref, s, states)
        xv, Bs, Cs, Xb = k["xv"], k["Bs"], k["Cs"], k["Xb"]

        zz = z_ref[...].astype(F32)
        sz = _sigmoid(zz)
        silu_z = zz * sz
        yg = y * silu_z
        r = _per_group(lambda a: lax.rsqrt(jnp.mean(a * a, axis=-1, keepdims=True) + EPS), yg)
        xhat = yg * r
        dout = dy_ref[...].astype(F32)
        gw = dout * nw_ref[...]
        dyg = r * (gw - xhat * _per_group(lambda a, c2: jnp.mean(a * c2, axis=-1, keepdims=True), gw, xhat))
        dnw = jnp.sum(dout * xhat, axis=0, keepdims=True)
        dz_ref[...] = (dyg * y * _dsilu(zz, sz)).astype(dz_ref.dtype)
        dy = dyg * silu_z
        dy0 = jnp.where(s["first"], dy, 0.0)
        dyb = (dy0.astype(BF16), (dy - dy0).astype(BF16))
        dYo = (dy * s["ecs_w"]).astype(BF16)

        dS_f = [dS[p] for p in range(N_PAIR)]
        dS_b = [d.astype(BF16) for d in dS_f]
        BdS, dXm, dprev, dCs, dMs, XdS = [], [], [], [[] for _ in range(SSD_GROUPS)], [], []
        for p in range(N_PAIR):
            g, ln = p // 2, slice(128 * p, 128 * (p + 1))
            BdS.append(_dot(Bs[g], dS_b[p], NT))
            dXm.append(_dot(k["Mb"][2 * p], dyb[0][:, ln], TN) + _dot(k["Mb"][2 * p + 1], dyb[1][:, ln], TN))
            dprev.append(_dot(dYo[:, ln], Cs[g], TN))
            dCs[g].append(_dot(dYo[:, ln], k["prev_b"][p], NN))
            for hh in range(2):
                dMs.append(_dot(dyb[hh][:, ln], Xb[hh][:, ln], NT))
                XdS.append(_dot(Xb[hh][:, ln], dS_b[p], NN))
        dX = jnp.concatenate(dXm, axis=1) + s["decay_w"] * jnp.concatenate(BdS, axis=1)
        dxs = dy * s["d_w"] + dX * s["dt_w"]

        sums = _dot01(jnp.concatenate([dX * xv, dy * k["yo"] * s["ecs_w"], dy * xv], axis=0), s["spread"], NT, "b")
        ddt, dcs = sums[0:Q], sums[Q:2 * Q]
        dD = jnp.sum(sums[2 * Q:3 * Q], axis=0, keepdims=True)

        col, row = s["col"], s["row"]
        lane1 = col[0:1]
        rowsT = lax.broadcasted_iota(jnp.int32, (128, Q), 0)
        dcs_t = jnp.zeros((128, Q), F32)
        dcl = jnp.zeros((1, 128), F32)
        dB_out, dC_out = [], []
        for g in range(SSD_GROUPS):
            Bf = Bs[g].astype(F32)
            dCB = jnp.zeros((Q, Q), F32)
            dBacc = jnp.zeros((Q, 128), F32)
            for r4 in range(HPG):
                h = HPG * g + r4
                p, hh = h // 2, h % 2
                W = dMs[h] * k["Ms"][h]
                dCB = dCB + dMs[h] * k["Lms"][h]
                decay_h = s["decay_w"][:, SSD_HEAD_DIM * h:SSD_HEAD_DIM * h + 1]
                dBacc = dBacc + decay_h * XdS[h]
                tdec = jnp.sum(XdS[h] * Bf, axis=1, keepdims=True) * decay_h
                dcs = dcs + jnp.where(col == h, jnp.sum(W, axis=1, keepdims=True) - tdec, 0.0)
                dcs_t = dcs_t - jnp.where(rowsT == h, jnp.sum(W, axis=0, keepdims=True), 0.0)
                rows_h = (row < SSD_HEAD_DIM) if hh == 0 else (row >= SSD_HEAD_DIM)
                sprev = jnp.sum(jnp.sum(jnp.where(rows_h, dS_f[p] * states[p], 0.0), axis=1, keepdims=True),
                                axis=0, keepdims=True)
                ecl = jnp.exp(s["cs"][Q - 1:Q, h:h + 1])
                dcl = dcl + jnp.where(lane1 == h, jnp.sum(tdec, axis=0, keepdims=True) + ecl * sprev, 0.0)
            dCB_b = dCB.astype(BF16)
            dC_out.append(dCs[g][0] + dCs[g][1] + _dot(dCB_b, Bs[g], NN))
            dB_out.append(dBacc + _dot(dCB_b, Cs[g], TN))
        for p in range(N_PAIR):
            dS[p] = dS_f[p] * k["ecl_rows"][p] + dprev[p]
        dxc_ref[...] = jnp.concatenate([dxs] + dB_out + dC_out, axis=1).astype(dxc_ref.dtype)

        dcs = dcs + _dot01(s["eye"], dcs_t, NT, "a") + jnp.where(row == Q - 1, dcl, 0.0)
        da = _dot01(row <= col, dcs, NN, "a")
        ddt = ddt + da * s["A"]
        dpre = jnp.where(s["valid"], ddt * _sigmoid(s["pre"]), 0.0)
        ddtr_ref[...] = dpre
        dbias = jnp.sum(dpre, axis=0, keepdims=True)
        dalog = jnp.sum(da * s["dt"], axis=0, keepdims=True) * s["A"]
        first_step = jnp.logical_and(b == 0, t == 0)

        @pl.when(first_step)
        def _():
            dbias_ref[...] = dbias
            dalog_ref[...] = dalog
            dd_ref[...] = dD
            dnw_ref[...] = dnw

        @pl.when(jnp.logical_not(first_step))
        def _():
            dbias_ref[...] += dbias
            dalog_ref[...] += dalog
            dd_ref[...] += dD
            dnw_ref[...] += dnw

    rb = lambda b, c: b * nc + nc - 1 - c
    rowblk = lambda w: pl.BlockSpec((Q, w), lambda b, c: (rb(b, c), 0))
    vec = lambda w: pl.BlockSpec((1, w), lambda b, c: (0, 0))
    return _call(
        body, name="ssd_bwd", grid=(Bl, nc),
        in_specs=_ssd_in_specs(nc, rev=True) + [
            pl.BlockSpec((1, 1, N_PAIR, 128, 128), lambda b, c: (b, nc - 1 - c, 0, 0, 0)), rowblk(SSD_INNER)],
        out_specs=[rowblk(SSD_CONV_CH), rowblk(SSD_INNER), rowblk(128), vec(128), vec(128), vec(128), vec(SSD_INNER)],
        out_shape=[jax.ShapeDtypeStruct((M, SSD_CONV_CH), BF16), jax.ShapeDtypeStruct(dproj.shape, BF16),
                   jax.ShapeDtypeStruct((M, 128), F32), jax.ShapeDtypeStruct((1, 128), F32),
                   jax.ShapeDtypeStruct((1, 128), F32), jax.ShapeDtypeStruct((1, 128), F32),
                   jax.ShapeDtypeStruct((1, SSD_INNER), F32)],
        scratch=[pltpu.VMEM((N_PAIR, 128, 128), F32)], sem=("arbitrary", "arbitrary"),
        args=(xc, dtr, proj, bias_p, alog_p, d_p, nw, prev, dya), comm=comm, into=(dproj, 1))


NSUB = Q // HG_CHUNK
HG_HP = 8
EXP_CAP = 80.0


def _hg_setup(blk, q_ref, f_ref, hb_ref):
    row = lax.broadcasted_iota(jnp.int32, (Q, Q), 0)
    col = lax.broadcasted_iota(jnp.int32, (Q, Q), 1)
    same = (row // HG_CHUNK) == (col // HG_CHUNK)
    causal = jnp.logical_and(same, col <= row)
    lb = _sigmoid(hb_ref[0:1, :] - hb_ref[1:2, :])
    fl = f_ref[...].astype(F32)
    sg = _sigmoid(fl)
    fg = lb + (1.0 - lb) * sg
    k = (1.0 - lb) * (1.0 - sg)
    gl = jnp.log(fg)
    G = _dot01(causal, gl, NN, "a")
    T = _dot01(same, gl, NN, "a")
    qv = q_ref[...].astype(F32)
    sq = _sigmoid(qv)
    eG = jnp.exp(G)
    eGn = jnp.exp(jnp.minimum(-G, EXP_CAP))
    eTG = jnp.exp(T - G)
    qt = qv * sq * eG
    kt = k * eGn
    kh = k * eTG
    valid = jnp.logical_or(blk > 0, row[:, :1] >= PAD)
    return dict(row=row, col=col, same=same, causal=causal, lb=lb, sg=sg, fg=fg, k=k, T=T, qv=qv, sq=sq,
                eG=eG, eGn=eGn, eTG=eTG, qt=qt, kt=kt, kh=kh, valid=valid)


def _hg_specs(nb, rev=False):
    rb = (lambda h, b, t: b * nb + nb - 1 - t) if rev else (lambda h, b, t: b * nb + t)
    w = 128 * HG_HP
    blk = lambda off: pl.BlockSpec((Q, w), lambda h, b, t, off=off: (rb(h, b, t), off // HG_HP + h))
    return [blk(24), blk(32), blk(40), blk(48),
            pl.BlockSpec((2, w), lambda h, b, t: (0, h)), pl.BlockSpec((1, w), lambda h, b, t: (0, h))]


HEAD_LANES = tuple(slice(128 * hh, 128 * (hh + 1)) for hh in range(HG_HP))


def _per_head(fn, *arrs):
    return jnp.concatenate([jnp.broadcast_to(fn(*(a[:, ln] for a in arrs)), (arrs[0].shape[0], 128))
                            for ln in HEAD_LANES], axis=1)


def _hgrn_fwd(proj, hb, nw, Bl, nb, comm=None):
    M = proj.shape[0]

    def body(q_ref, f_ref, i_ref, g_ref, hb_ref, nw_ref, y_ref, o_ref, st_ref, S):
        blk = pl.program_id(2)

        @pl.when(blk == 0)
        def _():
            S[...] = jnp.zeros_like(S)

        s = _hg_setup(blk, q_ref, f_ref, hb_ref)
        v = i_ref[...]
        qt_b, kt_b, kh_b = s["qt"].astype(BF16), s["kt"].astype(BF16), s["kh"].astype(BF16)
        eT = jnp.exp(s["T"])
        att = [jnp.where(s["causal"], _dot(qt_b[:, ln], kt_b[:, ln], NT), 0.0).astype(BF16) for ln in HEAD_LANES]
        o_intra = [_dot(att[hh], v[:, ln], NN) for hh, ln in enumerate(HEAD_LANES)]
        for j in range(NSUB):
            sl = slice(HG_CHUNK * j, HG_CHUNK * (j + 1))
            for hh, ln in enumerate(HEAD_LANES):
                St = S[hh]
                st_ref[0, hh, 0, j] = St
                o_ref[sl, ln] = o_intra[hh][sl] + _dot(qt_b[sl, ln], St.astype(BF16), NT)
                S[hh] = St * eT[HG_CHUNK * j:HG_CHUNK * j + 1, ln] + _dot(v[sl, ln], kh_b[sl, ln], TN)
        o = o_ref[...]
        r = _per_head(lambda a: lax.rsqrt(jnp.mean(a * a, axis=-1, keepdims=True) + EPS), o)
        gv = g_ref[...].astype(F32)
        y_ref[...] = (o * r * nw_ref[...] * gv * _sigmoid(gv)).astype(y_ref.dtype)

    rowblk = pl.BlockSpec((Q, 128 * HG_HP), lambda h, b, t: (b * nb + t, h))
    return _call(
        body, name="hgrn_fwd", grid=(HG_HEADS // HG_HP, Bl, nb), in_specs=_hg_specs(nb),
        out_specs=[rowblk, rowblk,
                   pl.BlockSpec((1, HG_HP, 1, NSUB, 128, 128), lambda h, b, t: (b, h, t, 0, 0, 0))],
        out_shape=[jax.ShapeDtypeStruct((M, HG_WIDTH), BF16), jax.ShapeDtypeStruct((M, HG_WIDTH), F32),
                   jax.ShapeDtypeStruct((Bl, HG_HEADS, nb, NSUB, 128, 128), F32)],
        scratch=[pltpu.VMEM((HG_HP, 128, 128), F32)], sem=("parallel", "arbitrary", "arbitrary"),
        args=(proj, proj, proj, proj, hb, nw), comm=comm)


def _hgrn_bwd(proj, hb, nw, o_saved, st_saved, dyb, dproj, Bl, nb, comm=None):
    assert HG_HP == HG_HEADS

    def body(q_ref, f_ref, i_ref, g_ref, hb_ref, nw_ref, o_ref, st_ref, dy_ref,
             d_ref, dhb_ref, dnw_ref, dS, a_dqt, a_dv, a_dkh, a_dgl):
        b, t = pl.program_id(1), pl.program_id(2)

        @pl.when(t == 0)
        def _():
            dS[...] = jnp.zeros_like(dS)

        first_step = jnp.logical_and(b == 0, t == 0)
        s = _hg_setup(nb - 1 - t, q_ref, f_ref, hb_ref)
        v = i_ref[...]
        qt_b, kt_b, kh_b = s["qt"].astype(BF16), s["kt"].astype(BF16), s["kh"].astype(BF16)
        eT = jnp.exp(s["T"])
        att = [jnp.where(s["causal"], _dot(qt_b[:, ln], kt_b[:, ln], NT), 0.0).astype(BF16) for ln in HEAD_LANES]

        o = o_ref[...]
        r = _per_head(lambda a: lax.rsqrt(jnp.mean(a * a, axis=-1, keepdims=True) + EPS), o)
        xhat = o * r
        gv = g_ref[...].astype(F32)
        sgv = _sigmoid(gv)
        dyv = dy_ref[...].astype(F32)
        d_on = dyv * gv * sgv
        dg_out = dyv * xhat * nw_ref[...] * _dsilu(gv, sgv)
        gw = d_on * nw_ref[...]
        do = r * (gw - xhat * _per_head(lambda a, c: jnp.mean(a * c, axis=-1, keepdims=True), gw, xhat))
        dnw = jnp.sum(d_on * xhat, axis=0, keepdims=True)
        do_b = do.astype(BF16)

        datt = [jnp.where(s["causal"], _dot(do_b[:, ln], v[:, ln], NT), 0.0).astype(BF16) for ln in HEAD_LANES]
        dqt = jnp.concatenate([_dot(datt[hh], kt_b[:, ln], NN) for hh, ln in enumerate(HEAD_LANES)], axis=1)
        dkt = jnp.concatenate([_dot(datt[hh], qt_b[:, ln], TN) for hh, ln in enumerate(HEAD_LANES)], axis=1)
        dv = jnp.concatenate([_dot(att[hh], do_b[:, ln], TN) for hh, ln in enumerate(HEAD_LANES)], axis=1)
        last_row = (lax.broadcasted_iota(jnp.int32, (HG_CHUNK, 128), 0) == HG_CHUNK - 1)
        for j in reversed(range(NSUB)):
            sl = slice(HG_CHUNK * j, HG_CHUNK * (j + 1))
            for hh, ln in enumerate(HEAD_LANES):
                St = st_ref[0, hh, 0, j]
                dSt = dS[hh]
                St_b, dSt_b = St.astype(BF16), dSt.astype(BF16)
                eT_j = eT[HG_CHUNK * j:HG_CHUNK * j + 1, ln]
                dkh_j = _dot(v[sl, ln], dSt_b, NN)
                a_dqt[sl, ln] = _dot(do_b[sl, ln], St_b, NN)
                a_dv[sl, ln] = _dot(kh_b[sl, ln], dSt_b, NT)
                a_dkh[sl, ln] = dkh_j
                dlast = (jnp.sum(St * dSt, axis=0, keepdims=True) * eT_j
                         + jnp.sum(dkh_j * s["kh"][sl, ln], axis=0, keepdims=True))
                a_dgl[sl, ln] = jnp.where(last_row, dlast, 0.0)
                dS[hh] = dSt * eT_j + _dot(do_b[sl, ln], qt_b[sl, ln], TN)
        dqt = dqt + a_dqt[...]
        dv = dv + a_dv[...]
        dkh = a_dkh[...]
        dG = dqt * s["qt"] - dkt * s["kt"] - dkh * s["kh"] + a_dgl[...]
        rev_causal = jnp.logical_and(s["same"], s["col"] >= s["row"])
        dgl = _dot01(rev_causal, dG, NN, "a")
        dk = dkt * s["eGn"] + dkh * s["eTG"]
        dfg = dgl / s["fg"] - dk
        lb, sg = s["lb"], s["sg"]
        keep = s["valid"].astype(F32)
        d_ref[:, 0:w] = (dqt * s["eG"] * _dsilu(s["qv"], s["sq"]) * keep).astype(d_ref.dtype)
        d_ref[:, w:2 * w] = (dfg * (1.0 - lb) * sg * (1.0 - sg) * keep).astype(d_ref.dtype)
        d_ref[:, 2 * w:3 * w] = (dv * keep).astype(d_ref.dtype)
        d_ref[:, 3 * w:4 * w] = (dg_out * keep).astype(d_ref.dtype)
        dlb = jnp.sum(dfg * (1.0 - sg) * keep, axis=0, keepdims=True) * lb * (1.0 - lb)
        dhb = jnp.concatenate([dlb, -dlb], axis=0)

        @pl.when(first_step)
        def _():
            dhb_ref[...] = dhb
            dnw_ref[...] = dnw

        @pl.when(jnp.logical_not(first_step))
        def _():
            dhb_ref[...] += dhb
            dnw_ref[...] += dnw

    w = 128 * HG_HP
    rowblk = pl.BlockSpec((Q, w), lambda h, b, t: (b * nb + nb - 1 - t, h))
    return _call(
        body, name="hgrn_bwd", grid=(HG_HEADS // HG_HP, Bl, nb),
        in_specs=_hg_specs(nb, rev=True) + [
            rowblk, pl.BlockSpec((1, HG_HP, 1, NSUB, 128, 128), lambda h, b, t: (b, h, nb - 1 - t, 0, 0, 0)), rowblk],
        out_specs=[pl.BlockSpec((pl.Element(Q), pl.Element(4 * w)),
                                lambda h, b, t: (pl.multiple_of((b * nb + nb - 1 - t) * Q, Q), 3 * HG_WIDTH)),
                   pl.BlockSpec((2, w), lambda h, b, t: (0, h)), pl.BlockSpec((1, w), lambda h, b, t: (0, h))],
        out_shape=[jax.ShapeDtypeStruct(dproj.shape, BF16),
                   jax.ShapeDtypeStruct((2, HG_WIDTH), F32), jax.ShapeDtypeStruct((1, HG_WIDTH), F32)],
        scratch=[pltpu.VMEM((HG_HP, 128, 128), F32)] + [pltpu.VMEM((Q, w), F32)] * 4,
        sem=("parallel", "arbitrary", "arbitrary"),
        args=(proj, proj, proj, proj, hb, nw, o_saved, st_saved, dyb), comm=comm, into=(dproj, 0))


def _adamw(name, parts, w, m, v, comm=None):
    R, C = w.shape
    S = parts.shape[0]
    tr, tc = (_tile(R, (256, 176, 128, 64, 8)), C) if R % 8 == 0 else (R, 256)
    c1, c2 = 1.0 - ADAM_B1 ** ADAM_STEP, 1.0 - ADAM_B2 ** ADAM_STEP

    def body(p_ref, w_ref, m_ref, v_ref, g_ref, d_ref, nm_ref, nv_ref):
        g = p_ref[0].astype(F32)
        for s in range(1, S):
            g = g + p_ref[s].astype(F32)
        nm = ADAM_B1 * m_ref[...] + (1.0 - ADAM_B1) * g
        nv = ADAM_B2 * v_ref[...] + (1.0 - ADAM_B2) * (g * g)
        g_ref[...] = g
        nm_ref[...] = nm
        nv_ref[...] = nv
        d_ref[...] = -ADAM_LR * ((nm / c1) / (jnp.sqrt(nv / c2) + ADAM_EPS) + ADAM_WD * w_ref[...])

    blk = pl.BlockSpec((tr, tc), lambda i, j: (i, j))
    return _call(
        body, name=name, grid=(R // tr, C // tc),
        in_specs=[pl.BlockSpec((S, tr, tc), lambda i, j: (0, i, j)), blk, blk, blk], out_specs=[blk] * 4,
        out_shape=[jax.ShapeDtypeStruct((R, C), F32)] * 4, scratch=[], sem=("parallel", "parallel"),
        args=(parts, w, m, v), comm=comm)


def _pair_sum(name, by_core, arrived):
    _, J, R, C = by_core.shape
    tc = _tile(C, (512, 256, 128))

    def body(c_ref, a_ref, b_ref, o_ref):
        o_ref[...] = (a_ref[0].astype(F32) + b_ref[...].astype(F32)).astype(o_ref.dtype)

    blk = pl.BlockSpec((1, R, tc), lambda j, k, c_ref: (j, 0, k))
    return pl.pallas_call(
        body, name=name,
        grid_spec=pltpu.PrefetchScalarGridSpec(
            num_scalar_prefetch=1, grid=(J, C // tc),
            in_specs=[pl.BlockSpec((1, 1, R, tc), lambda j, k, c_ref: (c_ref[0], j, 0, k)), blk], out_specs=blk),
        out_shape=jax.ShapeDtypeStruct(arrived.shape, arrived.dtype), compiler_params=_params(("parallel", "parallel")),
    )(lax.axis_index("c").astype(jnp.int32).reshape(1), by_core, arrived)


def _sum_parts(name, parts):
    S, R, C = parts.shape

    def body(p_ref, o_ref):
        g = p_ref[0]
        for s in range(1, S):
            g = g + p_ref[s]
        o_ref[...] = g

    return pl.pallas_call(
        body, name=name, out_shape=jax.ShapeDtypeStruct((R, C), F32),
        in_specs=[pl.BlockSpec(memory_space=pltpu.VMEM)], out_specs=pl.BlockSpec(memory_space=pltpu.VMEM),
    )(parts)


def _heads_to_lanes(p):
    return jnp.pad(p, [(0, 0)] * (p.ndim - 1) + [(0, 128 - SSD_HEADS)])


def _lanes_to_heads(p):
    return p[..., :SSD_HEADS]


def _pack_rows(arrs):
    flat = jnp.concatenate([a.reshape(-1).astype(F32) for a in arrs])
    return jnp.pad(flat, (0, (-flat.shape[0]) % (8 * D_MODEL))).reshape(-1, D_MODEL)


def _unpack_rows(packed, like):
    flat, outs, at = packed.reshape(-1), [], 0
    for a in like:
        outs.append(flat[at:at + a.size].reshape(a.shape))
        at += a.size
    return outs


def _cols(gth):
    return jnp.transpose(gth, (1, 0, 2)).reshape(gth.shape[1], -1)


def _rows(gth):
    return gth.reshape(-1, gth.shape[2])


def _to_rows(g):
    return g.reshape(N_DEV, -1, g.shape[1]).astype(BF16)


def _by_core(g):
    return jnp.transpose(g.reshape(N_DEV // 2, 2, -1, g.shape[1]), (1, 0, 2, 3)).astype(BF16)


DT_ROW = 3072


def _chip_sums(tag, by_core, swap_in=None):
    arrived = swap_in(by_core) if swap_in else _exchange(tag + "_swap", "swap", by_core)
    return [_pair_sum(f"{tag}_chipsum{i}", m, a) for i, (m, a) in enumerate(zip(by_core, arrived))]


def _ffn_fwd_gu(tag, n, w_gu_t, comm=None):
    M = n.shape[0]
    F = w_gu_t.shape[0] // 2
    tm = _tile(M, (544, 256))
    outs = _fused_matmul(
        tag + "_gu", M, F, D_MODEL,
        [dict(a=n, b=w_gu_t, trans_b=True, acc=0, resident=True),
         dict(a=n, b=w_gu_t, trans_b=True, bn_off=1, acc=1, resident=True)], [],
        lambda accs, ex: (accs[0], accs[1], accs[0] * _sigmoid(accs[0]) * accs[1]),
        [BF16, BF16, BF16], 2, tm, F, D_MODEL, outer="i", comm=comm, sub=256)
    return (n, *outs[:3]), outs[3:]


def _rmsnorm_tile(x, w):
    return x * lax.rsqrt(jnp.mean(x * x, axis=-1, keepdims=True) + EPS) * w


def _ffn_fwd_down(tag, h, a, w_down, next_norm=None, comm=None):
    M = h.shape[0]
    F = w_down.shape[0]
    tm = _tile(M, (1088, 544, 256))
    if next_norm is None:
        (h_out,) = _fused_matmul(
            tag + "_down", M, D_MODEL, F, [dict(a=a, b=w_down, acc=0)], [(h, 0)],
            lambda accs, ex: (ex[0] + 0.5 * accs[0],), [F32], 1, tm, D_MODEL, F, outer="j", sub=256)
        return h_out

    def with_norm(accs, ex):
        h_new = ex[0] + 0.5 * accs[0]
        return h_new, _rmsnorm_tile(h_new, ex[1])

    return _fused_matmul(tag + "_down", M, D_MODEL, F, [dict(a=a, b=w_down, acc=0, resident=True)], [(h, 0)], with_norm,
                         [F32, BF16], 1, tm, D_MODEL, F, outer="j", vecs=[next_norm], comm=comm)


def _ffn_bwd(tag, dh, dh_b, h, norm_w, w_gu_t, w_down, saved, scatter=False):
    n, g, u, a = saved
    M = h.shape[0]
    F = w_down.shape[0]
    tm = _tile(M, (544, 256))
    tn = _tile(F, (1408, 704, 256))

    def swiglu_bwd(accs, ex):
        da, gv, uv = 0.5 * accs[0], ex[0].astype(F32), ex[1].astype(F32)
        s = _sigmoid(gv)
        return da * uv * _dsilu(gv, s), da * gv * s

    (dgu,) = _fused_matmul(
        tag + "_dact", M, F, D_MODEL, [dict(a=dh_b, b=w_down, trans_b=True, acc=0, resident=True)], [(g, 0), (u, 0)],
        swiglu_bwd, [BF16, BF16], 1, tm, F, D_MODEL, outer="i", stack=True, sub=256)
    tr = _tile(M, (2176, 256))
    (dw_down,) = _matmul_tn(tag + "_dwd", a, dh_b, tn, D_MODEL, tr, scale=0.5)
    dw_gu_t, *p_down = _matmul_tn(tag + "_dwgu", dgu, n, tn, D_MODEL, tr,
                                  comm=("scatter", [_to_rows(dw_down)]) if scatter else None)
    comm = None
    if scatter:
        comm = ("chips", _chip_sums(tag + "_wgu", [_by_core(dw_gu_t)]))
    def norm_bwd(accs, ex):
        dh_prev, dw = _rmsnorm_bwd_tile(accs[0], ex[0], ex[2], ex[1])
        return dh_prev, dh_prev, dw

    dh_prev, dh_prev_b, dnorm, *p_gu = _fused_matmul(
        tag + "_dn", M, D_MODEL, 2 * F,
        [dict(a=dgu, b=w_gu_t, acc=0, resident=True)], [(h, 0), (dh, 0)],
        norm_bwd, [F32, BF16], 1, tm, D_MODEL, 2 * F, outer="i", comm=comm, vecs=[norm_w], row_sums=1)
    return (dh_prev, dh_prev_b, dnorm, *((p_gu[0], p_down[0]) if scatter else (dw_gu_t, dw_down)))


def kernel(x, meta_tokens, ffn1_norm, ffn1_w_gu, ffn1_w_down, mix_norm, w_in, ssd_conv_w, ssd_conv_b, ssd_dt_bias, ssd_a_log, ssd_d, ssd_norm, hg_lower_bound, hg_norm, w_branch_a, w_branch_b, w_out, ffn2_norm, ffn2_w_gu, ffn2_w_down, final_norm, loss_target, m_meta_tokens, m_ffn1_norm, m_ffn1_w_gu, m_ffn1_w_down, m_mix_norm, m_w_in, m_ssd_conv_w, m_ssd_conv_b, m_ssd_dt_bias, m_ssd_a_log, m_ssd_d, m_ssd_norm, m_hg_lower_bound, m_hg_norm, m_w_branch_a, m_w_branch_b, m_w_out, m_ffn2_norm, m_ffn2_w_gu, m_ffn2_w_down, m_final_norm, v_meta_tokens, v_ffn1_norm, v_ffn1_w_gu, v_ffn1_w_down, v_mix_norm, v_w_in, v_ssd_conv_w, v_ssd_conv_b, v_ssd_dt_bias, v_ssd_a_log, v_ssd_d, v_ssd_norm, v_hg_lower_bound, v_hg_norm, v_w_branch_a, v_w_branch_b, v_w_out, v_ffn2_norm, v_ffn2_w_gu, v_ffn2_w_down, v_final_norm):
    Bl, S, D = x.shape
    T = PAD + N_META + S
    nc = T // Q
    M = Bl * T
    me = 4 * lax.axis_index("x") + 2 * lax.axis_index("y") + lax.axis_index("c")

    bf = lambda a: a[0].astype(BF16)
    bft = lambda a: a[0].T.astype(BF16)
    g_meta, g_conv_w = _exchange("gather_small", "gather", [meta_tokens, ssd_conv_w[0]])
    meta_full, conv_w_full = _cols(g_meta), _cols(g_conv_w)
    bias_p, alog_p, d_p = _heads_to_lanes(ssd_dt_bias), _heads_to_lanes(ssd_a_log), _heads_to_lanes(ssd_d)
    final_w = final_norm.reshape(1, D)

    h0, n1, g_wgu1 = _embed_norm(x, meta_full, ffn1_norm, comm=("gather", [bft(ffn1_w_gu)]))
    wgu1 = _rows(g_wgu1)
    tm = _tile(M, (1088, 544, 256))
    win_shard = bft(w_in)
    cut = (win_shard.shape[0] // 32) * 16
    ffn1_saved, (g_wd1, g_win_a) = _ffn_fwd_gu("ffn1", n1, wgu1, comm=("gather", [bf(ffn1_w_down), win_shard[:cut]]))
    wd1 = _rows(g_wd1)
    h1, un, g_win_b = _ffn_fwd_down("ffn1", h0, ffn1_saved[3], wd1, next_norm=mix_norm,
                                    comm=("gather", [win_shard[cut:]]))
    win_t = _rows(jnp.concatenate([g_win_a, g_win_b], axis=1))
    win_dt = jnp.pad(win_t[DT_ROW:DT_ROW + SSD_HEADS], ((0, 128 - SSD_HEADS), (0, 0)))
    plain = lambda accs, ex: (accs[0],)
    proj, g_wa, g_wb, g_wo = _fused_matmul(
        "in_proj", M, N_MAIN, D, [dict(a=un, b=win_t, trans_b=True, acc=0, b_shift=(DT_ROW // 3072, SSD_HEADS))], [],
        plain, [BF16], 1, tm, 3072, D,
        outer="j", comm=("gather", [bf(w_branch_a), bf(w_branch_b), bf(w_out)]), sub=512)
    wa, wb, wo = _rows(g_wa), _rows(g_wb), _rows(g_wo)
    (dtr,) = _fused_matmul("in_proj_dt", M, 128, D, [dict(a=un, b=win_dt, trans_b=True, acc=0)], [], plain, [F32], 1,
                           tm, 128, D, outer="j")
    xc = _conv_fwd(proj, conv_w_full, ssd_conv_b, Bl, T)
    ya, ssd_prev = _ssd_fwd(xc, dtr, proj, bias_p, alog_p, d_p, ssd_norm, Bl, nc)
    yb, hg_o, hg_st, g_wgu2, g_wd2 = _hgrn_fwd(proj, hg_lower_bound, hg_norm, Bl, nc,
                                               comm=("gather", [bft(ffn2_w_gu), bf(ffn2_w_down)]))
    wgu2, wd2 = _rows(g_wgu2), _rows(g_wd2)

    def branch_fwd(accs, ex):
        pa, pb = accs
        return pa, pb, _sigmoid(ex[0].astype(F32)) * pa + _sigmoid(ex[1].astype(F32)) * pb

    pa, pb, merged = _fused_matmul(
        "branches", M, D, D, [dict(a=ya, b=wa, acc=0), dict(a=yb, b=wb, acc=1)], [(proj, 7), (proj, 8)],
        branch_fwd, [BF16, BF16, BF16], 2, tm, D, D, outer="j")
    def out_with_norm(accs, ex):
        h_new = ex[0] + accs[0]
        return h_new, _rmsnorm_tile(h_new, ex[1])

    h2, n2 = _fused_matmul("out_proj", M, D, D, [dict(a=merged, b=wo, acc=0)], [(h1, 0)], out_with_norm,
                           [F32, BF16], 1, tm, D, D, outer="j", vecs=[ffn2_norm])
    ffn2_saved, _ = _ffn_fwd_gu("ffn2", n2, wgu2)
    h3 = _ffn_fwd_down("ffn2", h2, ffn2_saved[3], wd2)

    dh3, dh3_b, d_final, loss_part = _loss_head(h3, final_w, loss_target, Bl, nc)
    dh2, dh2_b, d_ffn2_norm, d_wgu2, d_wd2 = _ffn_bwd("ffn2", dh3, dh3_b, h2, ffn2_norm, wgu2, wd2, ffn2_saved)

    def branch_bwd(accs, ex):
        dm = accs[0]
        ga, gb, pav, pbv = (e.astype(F32) for e in ex)
        sa, sb = _sigmoid(ga), _sigmoid(gb)
        return (dm * sa, dm * sb,
                jnp.concatenate([dm * pav * sa * (1.0 - sa), dm * pbv * sb * (1.0 - sb)], axis=1))

    d_merged_outs = []

    def d_merged_with_swap(theirs):
        d_merged_outs.extend(_fused_matmul(
            "d_merged", M, D, D, [dict(a=dh2_b, b=wo, trans_b=True, acc=0)], [(proj, 7), (proj, 8), (pa, 0), (pb, 0)],
            branch_bwd, [BF16] * 2, 1, tm, D, D, outer="j", comm=("swap", theirs),
            wide=dict(width=2 * D, col=7 * D, total=N_MAIN, dtype=BF16)))
        return d_merged_outs[3:]

    s_ffn2 = _chip_sums("ffn2", [_by_core(d_wgu2), _by_core(d_wd2)], swap_in=d_merged_with_swap)
    dpa, dpb, dproj = d_merged_outs[:3]
    (d_wo,) = _matmul_tn("d_w_out", merged, dh2_b, 512, D, M)
    (d_wa,) = _matmul_tn("d_w_a", ya, dpa, 512, D, M)
    (d_wb,) = _matmul_tn("d_w_b", yb, dpb, 512, D, M)
    dya, dyb = _fused_matmul(
        "d_branches", M, D, D, [dict(a=dpa, b=wa, trans_b=True, acc=0), dict(a=dpb, b=wb, trans_b=True, acc=1)], [],
        lambda accs, ex: (accs[0], accs[1]), [BF16, BF16], 2, tm, D, D, outer="j")
    *ssd_grads, p_wgu2, p_wd2 = _ssd_bwd(xc, dtr, proj, bias_p, alog_p, d_p, ssd_norm, ssd_prev, dya, dproj, Bl, nc,
                                         comm=("chips", s_ffn2))
    dxc, dproj, ddtr, d_bias_p, d_alog_p, d_d_p, d_ssd_norm = ssd_grads
    dproj, d_conv_w, d_conv_b = _conv_bwd(proj, conv_w_full, ssd_conv_b, dxc, dproj, Bl, T)
    dproj, d_hb, d_hg_norm, p_wa, p_wb, p_wo = _hgrn_bwd(
        proj, hg_lower_bound, hg_norm, hg_o, hg_st, dyb, dproj, Bl, nc,
        comm=("scatter", [_to_rows(d_wa), _to_rows(d_wb), _to_rows(d_wo)]))
    ddtr_b = ddtr.astype(BF16)
    (d_win_t,) = _matmul_tn("d_w_in", dproj, un, 768, D, M, out_skip=(DT_ROW, SSD_HEADS))
    (d_win_dt,) = _matmul_tn("d_w_in_dt", ddtr_b, un, 128, D, M)
    d_win_t = lax.dynamic_update_slice(d_win_t, d_win_dt[:SSD_HEADS], (DT_ROW, 0))
    d_un_dt_outs = []

    def d_un_dt_with_swap(theirs):
        d_un_dt_outs.extend(_fused_matmul("d_un_dt", M, D, 128, [dict(a=ddtr_b, b=win_dt, acc=0)], [], plain, [F32], 1,
                                          tm, D, 128, outer="j", comm=("swap", theirs)))
        return d_un_dt_outs[1:]

    s_win = _chip_sums("w_in", [_by_core(d_win_t)], swap_in=d_un_dt_with_swap)
    def mix_norm_bwd(accs, ex):
        dh, dw = _rmsnorm_bwd_tile(accs[0] + ex[0], ex[1], ex[3], ex[2])
        return dh, dh, dw

    dh1, dh1_b, d_mix_norm, p_win = _fused_matmul(
        "d_un", M, D, N_MAIN, [dict(a=dproj, b=win_t, acc=0, b_shift=(DT_ROW // 3072, SSD_HEADS))],
        [(d_un_dt_outs[0], 0), (h1, 0), (dh2, 0)],
        mix_norm_bwd, [F32, BF16], 1, _tile(M, (544, 256)), D, 3072, outer="i", comm=("chips", s_win),
        vecs=[mix_norm], row_sums=1)
    dh0, _, d_ffn1_norm, p_wgu1, p_wd1 = _ffn_bwd("ffn1", dh1, dh1_b, h0, ffn1_norm, wgu1, wd1, ffn1_saved, scatter=True)

    dh0 = dh0.reshape(Bl, T, D)
    grad_x = dh0[:, PAD + N_META:]
    d_meta = dh0[:, PAD:PAD + N_META]

    small_grads = [d_ffn1_norm, d_mix_norm, d_conv_b, _lanes_to_heads(d_bias_p), _lanes_to_heads(d_alog_p),
                   _lanes_to_heads(d_d_p), d_ssd_norm, d_hb, d_hg_norm, d_ffn2_norm, d_final.reshape(D), d_conv_w]
    small_like = small_grads + [d_meta[b] for b in range(Bl)] + [loss_part[0, 0:1]]
    small_packed = _pack_rows(small_like)
    parts = [p_wgu1, p_wd1, p_win, p_wa, p_wb, p_wo, p_wgu2, p_wd2]

    names = ["meta_tokens", "ffn1_norm", "ffn1_w_gu", "ffn1_w_down", "mix_norm", "w_in", "ssd_conv_w", "ssd_conv_b",
             "ssd_dt_bias", "ssd_a_log", "ssd_d", "ssd_norm", "hg_lower_bound", "hg_norm", "w_branch_a", "w_branch_b",
             "w_out", "ffn2_norm", "ffn2_w_gu", "ffn2_w_down", "final_norm"]
    W = dict(meta_tokens=meta_tokens, ffn1_norm=ffn1_norm, ffn1_w_gu=ffn1_w_gu, ffn1_w_down=ffn1_w_down, mix_norm=mix_norm,
             w_in=w_in, ssd_conv_w=ssd_conv_w, ssd_conv_b=ssd_conv_b, ssd_dt_bias=ssd_dt_bias, ssd_a_log=ssd_a_log,
             ssd_d=ssd_d, ssd_norm=ssd_norm, hg_lower_bound=hg_lower_bound, hg_norm=hg_norm, w_branch_a=w_branch_a,
             w_branch_b=w_branch_b, w_out=w_out, ffn2_norm=ffn2_norm, ffn2_w_gu=ffn2_w_gu, ffn2_w_down=ffn2_w_down,
             final_norm=final_norm)
    Mo = dict(meta_tokens=m_meta_tokens, ffn1_norm=m_ffn1_norm, ffn1_w_gu=m_ffn1_w_gu, ffn1_w_down=m_ffn1_w_down,
              mix_norm=m_mix_norm, w_in=m_w_in, ssd_conv_w=m_ssd_conv_w, ssd_conv_b=m_ssd_conv_b, ssd_dt_bias=m_ssd_dt_bias,
              ssd_a_log=m_ssd_a_log, ssd_d=m_ssd_d, ssd_norm=m_ssd_norm, hg_lower_bound=m_hg_lower_bound, hg_norm=m_hg_norm,
              w_branch_a=m_w_branch_a, w_branch_b=m_w_branch_b, w_out=m_w_out, ffn2_norm=m_ffn2_norm, ffn2_w_gu=m_ffn2_w_gu,
              ffn2_w_down=m_ffn2_w_down, final_norm=m_final_norm)
    Vo = dict(meta_tokens=v_meta_tokens, ffn1_norm=v_ffn1_norm, ffn1_w_gu=v_ffn1_w_gu, ffn1_w_down=v_ffn1_w_down,
              mix_norm=v_mix_norm, w_in=v_w_in, ssd_conv_w=v_ssd_conv_w, ssd_conv_b=v_ssd_conv_b, ssd_dt_bias=v_ssd_dt_bias,
              ssd_a_log=v_ssd_a_log, ssd_d=v_ssd_d, ssd_norm=v_ssd_norm, hg_lower_bound=v_hg_lower_bound, hg_norm=v_hg_norm,
              w_branch_a=v_w_branch_a, w_branch_b=v_w_branch_b, w_out=v_w_out, ffn2_norm=v_ffn2_norm, ffn2_w_gu=v_ffn2_w_gu,
              ffn2_w_down=v_ffn2_w_down, final_norm=v_final_norm)
    grads, deltas, new_m, new_v = {}, {}, {}, {}
    big_names = ["ffn1_w_gu", "ffn1_w_down", "w_in", "w_branch_a", "w_branch_b", "w_out", "ffn2_w_gu", "ffn2_w_down"]
    transposed = ("ffn1_w_gu", "ffn2_w_gu", "w_in")
    small_all = None
    for nm, part in zip(big_names, parts):
        view = (lambda a: a[0].T) if nm in transposed else (lambda a: a[0])
        back = (lambda o: o.T[None]) if nm in transposed else (lambda o: o[None])
        outs = _adamw("adamw_" + nm, part, view(W[nm]), view(Mo[nm]), view(Vo[nm]),
                      comm=("gather", [small_packed]) if small_all is None else None)
        if small_all is None:
            small_all = outs[4]
        grads[nm], deltas[nm], new_m[nm], new_v[nm] = (back(o) for o in outs[:4])
    unpacked = _unpack_rows(_sum_parts("sum_small_grads", small_all), small_like)
    g_small = unpacked[:len(small_grads)]
    g_meta_full = unpacked[len(small_grads)]
    for b in range(1, Bl):
        g_meta_full = g_meta_full + unpacked[len(small_grads) + b]
    g_meta = lax.dynamic_slice_in_dim(g_meta_full, me * (D // N_DEV), D // N_DEV, axis=1)
    g_conv_w = lax.dynamic_slice_in_dim(g_small[11], me * (SSD_CONV_CH // N_DEV), SSD_CONV_CH // N_DEV, axis=1)
    loss = unpacked[-1].reshape(())
    small_names = ["ffn1_norm", "mix_norm", "ssd_conv_b", "ssd_dt_bias", "ssd_a_log", "ssd_d", "ssd_norm", "hg_lower_bound",
                   "hg_norm", "ffn2_norm", "final_norm", "ssd_conv_w", "meta_tokens"]
    small_g = g_small[:11] + [g_conv_w.reshape(ssd_conv_w.shape), g_meta]
    pk = lambda d: _pack_rows([d[nm] for nm in small_names])
    outs = _adamw("adamw_small", _pack_rows(small_g)[None], pk(W), pk(Mo), pk(Vo))
    like = [W[nm] for nm in small_names]
    for dst, o in zip((grads, deltas, new_m, new_v), outs):
        for nm, val in zip(small_names, _unpack_rows(o, like)):
            dst[nm] = val

    return (loss, grad_x, *[grads[nm] for nm in names], *[deltas[nm] for nm in names],
            *[new_m[nm] for nm in names], *[new_v[nm] for nm in names])
```

```python
import functools

import jax
import jax.numpy as jnp
from jax import lax
from jax.experimental import pallas as pl
from jax.experimental.pallas import tpu as pltpu

F32, BF16 = jnp.float32, jnp.bfloat16
NN, NT, TN = ((1,), (0,)), ((1,), (1,)), ((0,), (0,))
MESH_AXES = ("x", "y", "c")
N_DEV = 8

D_MODEL = 1024
N_META = 16
EPS = 1e-6
SSD_HEADS, SSD_HEAD_DIM, SSD_GROUPS, SSD_STATE, SSD_CONV, Q = 16, 64, 4, 128, 4, 128
SSD_INNER = SSD_HEADS * SSD_HEAD_DIM
SSD_CONV_CH = SSD_INNER + 2 * SSD_GROUPS * SSD_STATE
HG_WIDTH, HG_HEADS, HG_CHUNK = 1024, 8, 16
PAD = Q - N_META
N_MAIN = 9 * 1024
ADAM_LR, ADAM_B1, ADAM_B2, ADAM_EPS, ADAM_WD, ADAM_STEP = 0.001, 0.9, 0.999, 1e-08, 0.01, 10
VMEM_LIMIT = 52 * 1024 * 1024


def _dot(a, b, dims):
    return lax.dot_general(a, b, (dims, ((), ())), preferred_element_type=F32)


def _dot01(a, b, dims, sel):
    x = b if sel == "a" else a
    hi = x.astype(BF16)
    r1 = x - hi.astype(F32)
    mid = r1.astype(BF16)
    lo = (r1 - mid.astype(F32)).astype(BF16)
    s = (a if sel == "a" else b).astype(BF16)
    parts = [_dot(s, p, dims) if sel == "a" else _dot(p, s, dims) for p in (hi, mid, lo)]
    return parts[0] + parts[1] + parts[2]


def _sigmoid(x):
    return 1.0 / (1.0 + jnp.exp(-x))


def _dsilu(x, s):
    return s * (1.0 + x * (1.0 - s))


def _softplus(x):
    e = jnp.exp(-jnp.abs(x))
    u = 1.0 + e
    log1p_e = jnp.where(u == 1.0, e, jnp.log(u) * e / (u - 1.0))
    return jnp.maximum(x, 0.0) + log1p_e


def _params(sem):
    return pltpu.CompilerParams(dimension_semantics=sem, vmem_limit_bytes=VMEM_LIMIT)


def _tile(n, prefs):
    for p in prefs:
        if n % p == 0:
            return p
    return n


CHIP_FLIPS = ((1, 0), (0, 1), (1, 1))
N_PEER = N_DEV - 1


def _comm_gather(srcs, outs, send_sems, recv_sems, local_sems):
    n = len(srcs)
    x, y, c = (lax.axis_index(a) for a in MESH_AXES)
    dev = lambda px, py, pc: 4 * px + 2 * py + pc
    me, sib = dev(x, y, c), (x, y, 1 - c)
    nbr_x, nbr_y, diag = (1 - x, y), (x, 1 - y), (1 - x, 1 - y)
    via = (x ^ c, y ^ (1 - c), c)
    sent_on = dev(x ^ (1 - c), y ^ c, c)

    def rc(w, k, slot, to, src=None):
        return pltpu.make_async_remote_copy(
            src_ref=outs[w].at[slot] if src is None else src, dst_ref=outs[w].at[slot],
            send_sem=send_sems.at[w, k], recv_sem=recv_sems.at[w, k], device_id=to, device_id_type=pl.DeviceIdType.MESH)

    def local(w):
        return pltpu.make_async_copy(srcs[w], outs[w].at[me], local_sems.at[w])

    def start():
        for w in range(n):
            local(w).start()
            rc(w, 0, me, sib, src=srcs[w]).start()
            rc(w, 1, me, (*nbr_x, c), src=srcs[w]).start()
            rc(w, 2, me, (*nbr_y, c), src=srcs[w]).start()

    def pass_on():
        for w in range(n):
            rc(w, 1, dev(*nbr_x, c), sib).wait_recv()
            rc(w, 2, dev(*nbr_y, c), sib).wait_recv()
            rc(w, 3, sent_on, via).start()
            rc(w, 4, dev(*nbr_x, c), sib).start()
            rc(w, 5, dev(*nbr_y, c), sib).start()

    def pass_on_diagonal():
        for w in range(n):
            rc(w, 3, dev(*diag, c), sib).wait_recv()
            rc(w, 6, dev(*diag, c), sib).start()

    def finish():
        for w in range(n):
            rc(w, 0, dev(x, y, 1 - c), sib).wait_recv()
            for k, chip in ((4, nbr_x), (5, nbr_y), (6, diag)):
                rc(w, k, dev(*chip, 1 - c), sib).wait_recv()
            for k in range(N_PEER):
                rc(w, k, me, sib, src=srcs[w]).wait_send()
            local(w).wait()

    return start, (pass_on, pass_on_diagonal), finish


def _comm_scatter(srcs, outs, send_sems, recv_sems, local_sems):
    n = len(srcs)
    x, y, c = (lax.axis_index(a) for a in MESH_AXES)
    me = 4 * x + 2 * y + c

    def copies():
        out = []
        for w in range(n):
            out.append(pltpu.make_async_copy(srcs[w].at[me], outs[w].at[me], local_sems.at[w]))
            for k in range(1, N_DEV):
                px, py, pc = x ^ (k >> 2), y ^ ((k >> 1) & 1), c ^ (k & 1)
                out.append(pltpu.make_async_remote_copy(
                    src_ref=srcs[w].at[4 * px + 2 * py + pc], dst_ref=outs[w].at[me],
                    send_sem=send_sems.at[w, k - 1], recv_sem=recv_sems.at[w, k - 1],
                    device_id=(px, py, pc), device_id_type=pl.DeviceIdType.MESH))
        return out

    def start():
        for cp in copies():
            cp.start()

    def finish():
        for cp in copies():
            cp.wait()

    return start, None, finish


def _comm_swap(srcs, outs, send_sems, recv_sems, local_sems):
    x, y, c = (lax.axis_index(a) for a in MESH_AXES)

    def copies():
        return [pltpu.make_async_remote_copy(
            src_ref=srcs[w].at[1 - c], dst_ref=outs[w], send_sem=send_sems.at[w, 0], recv_sem=recv_sems.at[w, 0],
            device_id=(x, y, 1 - c), device_id_type=pl.DeviceIdType.MESH) for w in range(len(srcs))]

    def start():
        for cp in copies():
            cp.start()

    def finish():
        for cp in copies():
            cp.wait()

    return start, None, finish


def _comm_chips(srcs, outs, send_sems, recv_sems, local_sems):
    n = len(srcs)
    x, y, c = (lax.axis_index(a) for a in MESH_AXES)
    mine = 2 * x + y

    def copies():
        out = []
        for w in range(n):
            out.append(pltpu.make_async_copy(srcs[w].at[mine], outs[w].at[mine], local_sems.at[w]))
            for j, (fx, fy) in enumerate(CHIP_FLIPS):
                px, py = x ^ fx, y ^ fy
                out.append(pltpu.make_async_remote_copy(
                    src_ref=srcs[w].at[2 * px + py], dst_ref=outs[w].at[mine],
                    send_sem=send_sems.at[w, j], recv_sem=recv_sems.at[w, j],
                    device_id=(px, py, c), device_id_type=pl.DeviceIdType.MESH))
        return out

    def start():
        for cp in copies():
            cp.start()

    def finish():
        for cp in copies():
            cp.wait()

    return start, None, finish


def _comm_parts(comm):
    kind, arrays = comm[:2]
    n = len(arrays)
    lead = {"gather": lambda a: (N_DEV,) + a.shape, "scatter": lambda a: (N_DEV,) + a.shape[1:],
            "swap": lambda a: a.shape[1:], "chips": lambda a: a.shape}[kind]
    shapes = [jax.ShapeDtypeStruct(lead(a), a.dtype) for a in arrays]
    sems = [pltpu.SemaphoreType.DMA((n, N_PEER)), pltpu.SemaphoreType.DMA((n, N_PEER)), pltpu.SemaphoreType.DMA((n,))]
    make = {"gather": _comm_gather, "scatter": _comm_scatter, "swap": _comm_swap, "chips": _comm_chips}[kind]
    return n, shapes, sems, make


def _exchange(name, kind, arrays):
    n, shapes, sems, make = _comm_parts((kind, arrays))

    def body(*refs):
        start, middle, finish = make(refs[:n], refs[n:2 * n], *refs[2 * n:])
        start()
        for stage in middle or ():
            stage()
        finish()

    any_spec = pl.BlockSpec(memory_space=pl.ANY)
    return pl.pallas_call(
        body, name=name, in_specs=[any_spec] * n, out_specs=[any_spec] * n, out_shape=shapes, scratch_shapes=sems,
        compiler_params=pltpu.CompilerParams(has_side_effects=True),
    )(*arrays)


def _call(body, *, name, grid, in_specs, out_specs, out_shape, scratch, sem, args, comm=None, into=None):
    any_spec = pl.BlockSpec(memory_space=pl.ANY)
    in_specs, args, aliases, n_body_in = list(in_specs), list(args), {}, len(in_specs)
    if into is not None:
        in_specs.append(any_spec)
        args.append(into[0])
        aliases = {n_body_in: into[1]}
    n_in, n_out, n_scr = len(in_specs), len(out_specs), len(scratch)
    if comm is None:
        def plain(*refs):
            body(*refs[:n_body_in], *refs[n_in:])

        return pl.pallas_call(plain, name=name, grid=grid, in_specs=in_specs, out_specs=out_specs, out_shape=out_shape,
                              scratch_shapes=scratch, input_output_aliases=aliases, compiler_params=_params(sem))(*args)
    n, shapes, sems, make = _comm_parts(comm)

    def carrier(*refs):
        ins, csrc = refs[:n_body_in], refs[n_in:n_in + n]
        outs, cout = refs[n_in + n:n_in + n + n_out], refs[n_in + n + n_out:n_in + 2 * n + n_out]
        rest = refs[n_in + 2 * n + n_out:]
        start, middle, finish = make(csrc, cout, *rest[n_scr:])
        ids = [pl.program_id(a) for a in range(len(grid))]
        step = functools.reduce(lambda acc, ig: acc * ig[1] + ig[0], zip(ids, grid), 0)
        n_steps = functools.reduce(lambda a, b: a * b, grid, 1)
        pl.when(step == 0)(start)
        body(*ins, *outs, *rest[:n_scr])
        if middle:
            pl.when(step == max(0, (3 * n_steps) // 4 - 1))(middle[0])
            pl.when(step == n_steps - 1)(middle[1])
        pl.when(step == n_steps - 1)(finish)

    return pl.pallas_call(
        carrier, name=name, grid=grid, in_specs=in_specs + [any_spec] * n,
        out_specs=list(out_specs) + [any_spec] * n, out_shape=list(out_shape) + shapes,
        scratch_shapes=list(scratch) + sems, input_output_aliases=aliases,
        compiler_params=pltpu.CompilerParams(dimension_semantics=("arbitrary",) * len(grid),
                                             vmem_limit_bytes=VMEM_LIMIT, has_side_effects=True),
    )(*args, *comm[1])


def _fused_matmul(name, M, N, K, pairs, extras, epilogue, out_dtypes, n_acc, tm, tn, tk, outer="i", comm=None,
                  stack=False, vecs=(), row_sums=0, wide=None, sub=None):
    nk = K // tk
    n_pairs, n_ex, n_out = len(pairs), len(extras), len(out_dtypes)
    assert not row_sums or (outer == "i" and N == tn)

    def ij(g0, g1):
        return (g0, g1) if outer == "i" else (g1, g0)

    in_specs, args = [], []
    for p in pairs:
        ao, bk, bn = p.get("a_off", 0), p.get("bk_off", 0), p.get("bn_off", 0)
        mode = dict(pipeline_mode=pl.Buffered(1)) if p.get("resident") else {}
        in_specs.append(pl.BlockSpec((tm, tk), lambda g0, g1, k, ao=ao: (ij(g0, g1)[0], k + ao)))
        if "b_shift" in p:
            first, shift = p["b_shift"]
            if p.get("trans_b"):
                in_specs.append(pl.BlockSpec(
                    (pl.Element(tn), pl.Element(tk)),
                    lambda g0, g1, k, bk=bk: (
                        pl.multiple_of(ij(g0, g1)[1] * tn + jnp.where(ij(g0, g1)[1] >= first, shift, 0), 16),
                        (k + bk) * tk)))
            else:
                in_specs.append(pl.BlockSpec(
                    (pl.Element(tk), pl.Element(tn)),
                    lambda g0, g1, k, bn=bn: (pl.multiple_of(k * tk + jnp.where(k >= first, shift, 0), 16),
                                              (ij(g0, g1)[1] + bn) * tn)))
        elif p.get("trans_b"):
            in_specs.append(pl.BlockSpec((tn, tk), lambda g0, g1, k, bk=bk, bn=bn: (ij(g0, g1)[1] + bn, k + bk), **mode))
        else:
            in_specs.append(pl.BlockSpec((tk, tn), lambda g0, g1, k, bk=bk, bn=bn: (k + bk, ij(g0, g1)[1] + bn), **mode))
        args += [p["a"], p["b"]]
    for arr, off in extras:
        in_specs.append(pl.BlockSpec((tm, tn), lambda g0, g1, k, off=off: (ij(g0, g1)[0], ij(g0, g1)[1] + off)))
        args.append(arr)
    for arr in vecs:
        in_specs.append(pl.BlockSpec((1, tn), lambda g0, g1, k: (0, ij(g0, g1)[1])))
        args.append(arr)
    if stack:
        assert N == tn
        out_specs = [pl.BlockSpec((tm, n_out * tn), lambda g0, g1, k: (ij(g0, g1)[0], 0))]
        out_shape = [jax.ShapeDtypeStruct((M, n_out * N), out_dtypes[0])]
    else:
        out_specs = [pl.BlockSpec((tm, tn), lambda g0, g1, k: ij(g0, g1)) for _ in out_dtypes]
        out_shape = [jax.ShapeDtypeStruct((M, N), dt) for dt in out_dtypes]
    if wide:
        out_specs.append(pl.BlockSpec((pl.Element(tm), pl.Element(wide["width"])),
                                      lambda g0, g1, k: (pl.multiple_of(ij(g0, g1)[0] * tm, 16), wide["col"])))
        out_shape.append(jax.ShapeDtypeStruct((M, wide["total"]), wide["dtype"]))
    n_tile_out = len(out_specs)
    out_specs += [pl.BlockSpec((1, tn), lambda g0, g1, k: (0, 0)) for _ in range(row_sums)]
    out_shape += [jax.ShapeDtypeStruct((1, N), F32) for _ in range(row_sums)]
    grid = (M // tm, N // tn, nk) if outer == "i" else (N // tn, M // tm, nk)
    n_in = 2 * n_pairs + n_ex + len(vecs)

    def partials(refs, cs=slice(None)):
        accs = [None] * n_acc
        for idx, p in enumerate(pairs):
            b_ref = refs[2 * idx + 1]
            d = (_dot(refs[2 * idx][...], b_ref[cs, :], NT) if p.get("trans_b")
                 else _dot(refs[2 * idx][...], b_ref[:, cs], NN))
            accs[p["acc"]] = d if accs[p["acc"]] is None else accs[p["acc"]] + d
        return accs

    def finish(accs, refs, first_rows, cs=slice(None)):
        res = epilogue(accs, [r[:, cs] for r in refs[2 * n_pairs:n_in]])
        if stack:
            o = refs[n_in]
            for idx in range(n_out):
                lo = idx * tn + (cs.start or 0)
                o[:, lo:lo + (tn if cs.stop is None else cs.stop - cs.start)] = res[idx].astype(o.dtype)
        else:
            for o, r in zip(refs[n_in:n_in + n_out], res):
                o[:, cs] = r.astype(o.dtype)
        if wide:
            o = refs[n_in + n_tile_out - 1]
            o[...] = res[n_out].astype(o.dtype)
        for o, r in zip(refs[n_in + n_tile_out:n_in + n_tile_out + row_sums], res[n_out + bool(wide):]):
            @pl.when(first_rows)
            def _(o=o, r=r):
                o[...] = r

            @pl.when(jnp.logical_not(first_rows))
            def _(o=o, r=r):
                o[...] += r

    if nk == 1 and sub:
        assert not wide and not row_sums and tn % sub == 0

        def body(*refs):
            for c in range(tn // sub):
                cs = slice(c * sub, (c + 1) * sub)
                finish(partials(refs, cs), refs, None, cs)
        scratch = []
    elif nk == 1:
        def body(*refs):
            finish(partials(refs), refs, pl.program_id(0) == 0)
        scratch = []
    else:
        def body(*refs):
            acc_refs = refs[-n_acc:]
            k = pl.program_id(2)
            first_rows = pl.program_id(0) == 0
            new = partials(refs)

            @pl.when(k == 0)
            def _():
                for a, v in zip(acc_refs, new):
                    a[...] = v

            @pl.when(k > 0)
            def _():
                for a, v in zip(acc_refs, new):
                    a[...] += v

            @pl.when(k == nk - 1)
            def _():
                finish([a[...] for a in acc_refs], refs, first_rows)
        scratch = [pltpu.VMEM((tm, tn), F32) for _ in range(n_acc)]

    return _call(body, name=name, grid=grid, in_specs=in_specs, out_specs=out_specs, out_shape=out_shape,
                 scratch=scratch, sem=("parallel", "parallel", "arbitrary"), args=args, comm=comm)


def _matmul_tn(name, x, y, t1, t2, tr, scale=1.0, comm=None, out_dtype=BF16, out_skip=None):
    R, K1 = x.shape
    N1 = y.shape[1]
    nr, n1 = R // tr, K1 // t1
    x_spec = pl.BlockSpec((tr, t1), lambda i, j, r: (r, i))
    rows_out = K1
    o_spec = pl.BlockSpec((t1, t2), lambda i, j, r: (i, j))
    if out_skip:
        row, count = out_skip
        rows_out += count
        o_spec = pl.BlockSpec(
            (pl.Element(t1), pl.Element(t2)),
            lambda i, j, r: (pl.multiple_of(i * t1 + jnp.where(i * t1 >= row, count, 0), 16), j * t2))

    def body(x_ref, y_ref, o_ref, *acc):
        d = _dot(x_ref[...], y_ref[...], TN)
        if nr == 1:
            o_ref[...] = (d * scale).astype(o_ref.dtype)
            return
        r = pl.program_id(2)

        @pl.when(r == 0)
        def _():
            acc[0][...] = d

        @pl.when(jnp.logical_and(r > 0, r < nr - 1))
        def _():
            acc[0][...] += d

        @pl.when(r == nr - 1)
        def _():
            o_ref[...] = ((acc[0][...] + d) * scale).astype(o_ref.dtype)

    return _call(
        body, name=name, grid=(n1, N1 // t2, nr),
        in_specs=[x_spec, pl.BlockSpec((tr, t2), lambda i, j, r: (r, j))], out_specs=[o_spec],
        out_shape=[jax.ShapeDtypeStruct((rows_out, N1), out_dtype)],
        scratch=[pltpu.VMEM((t1, t2), F32)] if nr > 1 else [],
        sem=("parallel", "parallel", "arbitrary"), args=(x, y), comm=comm)


def _embed_norm(x, meta, w, comm=None):
    Bl, S, D = x.shape
    nb = (PAD + N_META + S) // Q
    M = Bl * nb * Q

    def body(x_ref, meta_ref, w_ref, h_ref, n_ref):
        head = jnp.concatenate([jnp.zeros((PAD, D), F32), meta_ref[...]], axis=0)
        h = jnp.where(pl.program_id(1) == 0, head, x_ref[0])
        h_ref[...] = h
        n_ref[...] = _rmsnorm_tile(h, w_ref[...]).astype(n_ref.dtype)

    row = pl.BlockSpec((Q, D), lambda b, t: (b * nb + t, 0))
    return _call(
        body, name="embed_norm", grid=(Bl, nb),
        in_specs=[pl.BlockSpec((1, Q, D), lambda b, t: (b, jnp.maximum(t - 1, 0), 0)),
                  pl.BlockSpec((N_META, D), lambda b, t: (0, 0)), pl.BlockSpec((1, D), lambda b, t: (0, 0))],
        out_specs=[row, row], out_shape=[jax.ShapeDtypeStruct((M, D), F32), jax.ShapeDtypeStruct((M, D), BF16)],
        scratch=[], sem=("parallel", "parallel"), args=(x, meta, w), comm=comm)


def _rmsnorm_bwd_tile(dn, h, w, dh_in):
    r = lax.rsqrt(jnp.mean(h * h, axis=-1, keepdims=True) + EPS)
    xhat = h * r
    gw = dn * w
    dh = dh_in + r * (gw - xhat * jnp.mean(gw * xhat, axis=-1, keepdims=True))
    return dh, jnp.sum(dn * xhat, axis=0, keepdims=True)


def _loss_head(h, w, target, Bl, nb):
    M, D = h.shape
    nt = 4 if (nb * Q) % 32 == 0 and nb * Q // 4 >= Q else nb
    half = nb * Q // nt

    def body(h_ref, w_ref, t_ref, dh_ref, dhb_ref, dw_ref, loss_ref):
        b, t = pl.program_id(0), pl.program_id(1)
        row = lax.broadcasted_iota(jnp.int32, (half, 1), 0)
        live = jnp.logical_or(t > 0, row >= Q).astype(F32)
        x = h_ref[...]
        r = lax.rsqrt(jnp.mean(x * x, axis=-1, keepdims=True) + EPS)
        xhat = x * r
        wv = w_ref[...]
        tgt = t_ref[0]
        tgt = jnp.where(t == 0, pltpu.roll(tgt, Q, 0), tgt)
        err = (xhat * wv - tgt) * live
        dy = err * (1.0 / D)
        gw = dy * wv
        dx = r * (gw - xhat * jnp.mean(gw * xhat, axis=-1, keepdims=True))
        dh_ref[...] = dx
        dhb_ref[...] = dx.astype(BF16)
        dw = jnp.sum(dy * xhat, axis=0, keepdims=True)
        part = 0.5 * jnp.sum(jnp.sum(err * err, axis=-1, keepdims=True) * (1.0 / D), axis=0, keepdims=True)
        first = jnp.logical_and(b == 0, t == 0)

        @pl.when(first)
        def _():
            dw_ref[...] = dw
            loss_ref[...] = jnp.broadcast_to(part, loss_ref.shape)

        @pl.when(jnp.logical_not(first))
        def _():
            dw_ref[...] += dw
            loss_ref[...] += jnp.broadcast_to(part, loss_ref.shape)

    row = pl.BlockSpec((half, D), lambda b, t: (b * nt + t, 0))
    vec = pl.BlockSpec((1, D), lambda b, t: (0, 0))
    return pl.pallas_call(
        body, name="loss_head", grid=(Bl, nt),
        in_specs=[row, vec, pl.BlockSpec((pl.Element(1), pl.Element(half), pl.Element(D)),
                                         lambda b, t: (b, pl.multiple_of(jnp.maximum(t * half - Q, 0), 8), 0))],
        out_specs=[row, row, vec, pl.BlockSpec((8, 128), lambda b, t: (0, 0))],
        out_shape=[jax.ShapeDtypeStruct((M, D), F32), jax.ShapeDtypeStruct((M, D), BF16),
                   jax.ShapeDtypeStruct((1, D), F32), jax.ShapeDtypeStruct((8, 128), F32)],
        compiler_params=_params(("arbitrary", "arbitrary")),
    )(h, w, target)


CONV_TC = 256


def _conv_pre(xr_ref, w_ref, b_ref):
    x = xr_ref[...].astype(F32)
    acc = b_ref[...] + w_ref[SSD_CONV - 1:SSD_CONV, :] * x
    for k in range(1, SSD_CONV):
        acc = acc + w_ref[SSD_CONV - 1 - k:SSD_CONV - k, :] * pltpu.roll(x, k, 0)
    return x, acc


def _conv_fwd(proj, w, b, Bl, T):
    M = proj.shape[0]
    off = 1024 // CONV_TC

    def body(xr_ref, w_ref, b_ref, o_ref):
        _, acc = _conv_pre(xr_ref, w_ref, b_ref)
        row = lax.broadcasted_iota(jnp.int32, acc.shape, 0)
        o_ref[...] = jnp.where(row >= PAD, acc * _sigmoid(acc), 0.0).astype(o_ref.dtype)

    return pl.pallas_call(
        body, name="conv_fwd", grid=(Bl, SSD_CONV_CH // CONV_TC),
        in_specs=[pl.BlockSpec((T, CONV_TC), lambda bb, j: (bb, j + off)),
                  pl.BlockSpec((SSD_CONV, CONV_TC), lambda bb, j: (0, j)), pl.BlockSpec((1, CONV_TC), lambda bb, j: (0, j))],
        out_specs=pl.BlockSpec((T, CONV_TC), lambda bb, j: (bb, j)),
        out_shape=jax.ShapeDtypeStruct((M, SSD_CONV_CH), BF16), compiler_params=_params(("parallel", "parallel")),
    )(proj, w, b)


def _conv_bwd(proj, w, b, dxc, dproj, Bl, T):
    M = proj.shape[0]
    off = 1024 // CONV_TC

    def body(xr_ref, w_ref, b_ref, d_ref, dx_ref, dw_ref, db_ref):
        x, acc = _conv_pre(xr_ref, w_ref, b_ref)
        row = lax.broadcasted_iota(jnp.int32, acc.shape, 0)
        s = _sigmoid(acc)
        dpre = jnp.where(row >= PAD, d_ref[...].astype(F32) * _dsilu(acc, s), 0.0)
        dx = w_ref[SSD_CONV - 1:SSD_CONV, :] * dpre
        dws = [jnp.sum(dpre * x, axis=0, keepdims=True)]
        for k in range(1, SSD_CONV):
            dx = dx + w_ref[SSD_CONV - 1 - k:SSD_CONV - k, :] * pltpu.roll(dpre, T - k, 0)
            dws.append(jnp.sum(dpre * pltpu.roll(x, k, 0), axis=0, keepdims=True))
        dx_ref[...] = dx.astype(dx_ref.dtype)
        dw = jnp.concatenate(dws[::-1], axis=0)
        db = jnp.sum(dpre, axis=0, keepdims=True)

        @pl.when(pl.program_id(1) == 0)
        def _():
            dw_ref[...] = dw
            db_ref[...] = db

        @pl.when(pl.program_id(1) > 0)
        def _():
            dw_ref[...] += dw
            db_ref[...] += db

    return _call(
        body, name="conv_bwd", grid=(SSD_CONV_CH // CONV_TC, Bl),
        in_specs=[pl.BlockSpec((T, CONV_TC), lambda j, bb: (bb, j + off)),
                  pl.BlockSpec((SSD_CONV, CONV_TC), lambda j, bb: (0, j)), pl.BlockSpec((1, CONV_TC), lambda j, bb: (0, j)),
                  pl.BlockSpec((T, CONV_TC), lambda j, bb: (bb, j))],
        out_specs=[pl.BlockSpec((T, CONV_TC), lambda j, bb: (bb, j + off)),
                   pl.BlockSpec((SSD_CONV, CONV_TC), lambda j, bb: (0, j)), pl.BlockSpec((1, CONV_TC), lambda j, bb: (0, j))],
        out_shape=[jax.ShapeDtypeStruct(dproj.shape, BF16), jax.ShapeDtypeStruct((SSD_CONV, SSD_CONV_CH), F32),
                   jax.ShapeDtypeStruct((1, SSD_CONV_CH), F32)],
        scratch=[], sem=("parallel", "arbitrary"), args=(proj, w, b, dxc), into=(dproj, 0))


N_PAIR = SSD_HEADS // 2
HPG = SSD_HEADS // SSD_GROUPS
GW = SSD_INNER // SSD_GROUPS


def _per_group(fn, *arrs):
    return jnp.concatenate([jnp.broadcast_to(fn(*(a[:, GW * g:GW * (g + 1)] for a in arrs)), (arrs[0].shape[0], GW))
                            for g in range(SSD_GROUPS)], axis=1)


def _ssd_prep(c, dtr_ref, bias_ref, alog_ref, d_ref):
    row = lax.broadcasted_iota(jnp.int32, (Q, 128), 0)
    col = lax.broadcasted_iota(jnp.int32, (Q, 128), 1)
    live = col < SSD_HEADS
    valid = jnp.logical_and(jnp.logical_or(c > 0, row >= PAD), live)
    pre = dtr_ref[...] + bias_ref[...]
    dt = jnp.where(valid, _softplus(pre), 0.0)
    A = jnp.where(live[0:1], -jnp.exp(alog_ref[...]), 0.0)
    tri = row >= col
    eye = (row == col).astype(BF16)
    cs = _dot01(tri, dt * A, NN, "a")
    cst = _dot01(eye, cs, NT, "a")
    spread = (lax.broadcasted_iota(jnp.int32, (128, SSD_INNER), 0)
              == lax.broadcasted_iota(jnp.int32, (128, SSD_INNER), 1) // SSD_HEAD_DIM).astype(BF16)
    dt_w = _dot01(dt, spread, NN, "b")
    cs_w = _dot01(cs, spread, NN, "b")
    d_w = _dot01(jnp.broadcast_to(d_ref[...], (8, 128)), spread, NN, "b")[0:1]
    lane = lax.broadcasted_iota(jnp.int32, (Q, SSD_INNER), 1)
    first = (lane % 128) < SSD_HEAD_DIM
    return dict(row=row, col=col, valid=valid, pre=pre, dt=dt, A=A, tri=tri, eye=eye, cs=cs, cst=cst, spread=spread,
                dt_w=dt_w, cs_w=cs_w, d_w=d_w, ecs_w=jnp.exp(cs_w), decay_w=jnp.exp(cs_w[Q - 1:Q] - cs_w), first=first)


def _ssd_chunk(xc_ref, s, states):
    xv = xc_ref[:, 0:SSD_INNER].astype(F32)
    Bs = [xc_ref[:, SSD_INNER + 128 * g:SSD_INNER + 128 * (g + 1)] for g in range(SSD_GROUPS)]
    Cs = [xc_ref[:, SSD_INNER + 512 + 128 * g:SSD_INNER + 512 + 128 * (g + 1)] for g in range(SSD_GROUPS)]
    X = xv * s["dt_w"]
    X0 = jnp.where(s["first"], X, 0.0)
    Xb = (X0.astype(BF16), (X - X0).astype(BF16))
    Xd = (X * s["decay_w"]).astype(BF16)
    CB = [_dot(Cs[g], Bs[g], NT) for g in range(SSD_GROUPS)]
    Lms = [jnp.exp(jnp.where(s["tri"], s["cs"][:, h:h + 1] - s["cst"][h:h + 1, :], -jnp.inf)) for h in range(SSD_HEADS)]
    Ms = [CB[h // HPG] * Lms[h] for h in range(SSD_HEADS)]
    Mb = [m.astype(BF16) for m in Ms]
    prev_b = [st.astype(BF16) for st in states]
    yds, yos, sts = [], [], []
    for p in range(N_PAIR):
        g, ln = p // 2, slice(128 * p, 128 * (p + 1))
        yds.append(_dot(Mb[2 * p], Xb[0][:, ln], NN) + _dot(Mb[2 * p + 1], Xb[1][:, ln], NN))
        yos.append(_dot(Cs[g], prev_b[p], NT))
        sts.append(_dot(Xd[:, ln], Bs[g], TN))
    yo = jnp.concatenate(yos, axis=1)
    y = jnp.concatenate(yds, axis=1) + yo * s["ecs_w"] + xv * s["d_w"]
    upper = s["row"] < SSD_HEAD_DIM
    cl = s["cs"][Q - 1:Q, :]
    ecl_rows = [jnp.where(upper, jnp.exp(cl[:, 2 * p:2 * p + 1]), jnp.exp(cl[:, 2 * p + 1:2 * p + 2])) for p in range(N_PAIR)]
    new_states = [states[p] * ecl_rows[p] + sts[p] for p in range(N_PAIR)]
    return y, new_states, dict(xv=xv, Bs=Bs, Cs=Cs, X=X, Xb=Xb, CB=CB, Lms=Lms, Ms=Ms, Mb=Mb, prev_b=prev_b, yo=yo,
                               ecl_rows=ecl_rows)


def _ssd_in_specs(nc, rev=False):
    rb = (lambda b, c: b * nc + nc - 1 - c) if rev else (lambda b, c: b * nc + c)
    vec = pl.BlockSpec((1, 128), lambda b, c: (0, 0))
    return [pl.BlockSpec((Q, SSD_CONV_CH), lambda b, c: (rb(b, c), 0)),
            pl.BlockSpec((Q, 128), lambda b, c: (rb(b, c), 0)),
            pl.BlockSpec((Q, SSD_INNER), lambda b, c: (rb(b, c), 0)),
            vec, vec, vec, pl.BlockSpec((1, SSD_INNER), lambda b, c: (0, 0))]


def _ssd_fwd(xc, dtr, proj, bias_p, alog_p, d_p, nw, Bl, nc):
    M = xc.shape[0]
    per = 2 if (Bl * nc) % 2 == 0 else 1

    def chunk(c, xc_ref, dtr_ref, z_ref, bias_ref, alog_ref, d_ref, nw_ref, y_ref, prev_ref, state):
        @pl.when(c == 0)
        def _():
            state[...] = jnp.zeros_like(state)

        s = _ssd_prep(c, dtr_ref, bias_ref, alog_ref, d_ref)
        states = [state[p] for p in range(N_PAIR)]
        y, new_states, _ = _ssd_chunk(xc_ref, s, states)
        for p in range(N_PAIR):
            prev_ref[p] = states[p]
            state[p] = new_states[p]
        zz = z_ref[...].astype(F32)
        yg = y * zz * _sigmoid(zz)
        r = _per_group(lambda a: lax.rsqrt(jnp.mean(a * a, axis=-1, keepdims=True) + EPS), yg)
        y_ref[...] = (yg * r * nw_ref[...]).astype(y_ref.dtype)

    def body(xc_ref, dtr_ref, z_ref, bias_ref, alog_ref, d_ref, nw_ref, y_ref, prev_ref, state):
        for sub in range(per):
            rows = pl.ds(sub * Q, Q)
            chunk((per * pl.program_id(0) + sub) % nc, xc_ref.at[rows], dtr_ref.at[rows], z_ref.at[rows],
                  bias_ref, alog_ref, d_ref, nw_ref, y_ref.at[rows], prev_ref.at[sub], state)

    rows = lambda w: pl.BlockSpec((per * Q, w), lambda s: (s, 0))
    vec = lambda w: pl.BlockSpec((1, w), lambda s: (0, 0))
    return pl.pallas_call(
        body, name="ssd_fwd", grid=(Bl * nc // per,),
        in_specs=[rows(SSD_CONV_CH), rows(128), rows(SSD_INNER), vec(128), vec(128), vec(128), vec(SSD_INNER)],
        out_specs=[rows(SSD_INNER), pl.BlockSpec((per, N_PAIR, 128, 128), lambda s: (s, 0, 0, 0))],
        out_shape=[jax.ShapeDtypeStruct((M, SSD_INNER), BF16), jax.ShapeDtypeStruct((Bl * nc, N_PAIR, 128, 128), F32)],
        scratch_shapes=[pltpu.VMEM((N_PAIR, 128, 128), F32)],
        compiler_params=_params(("arbitrary",)),
    )(xc, dtr, proj, bias_p, alog_p, d_p, nw)


def _ssd_bwd(xc, dtr, proj, bias_p, alog_p, d_p, nw, prev, dya, dproj, Bl, nc, comm=None):
    M = xc.shape[0]

    def body(xc_ref, dtr_ref, z_ref, bias_ref, alog_ref, d_ref, nw_ref, prev_ref, dy_ref,
             dxc_ref, dz_ref, ddtr_ref, dbias_ref, dalog_ref, dd_ref, dnw_ref, dS):
        b, t = pl.program_id(0), pl.program_id(1)

        @pl.when(t == 0)
        def _():
            dS[...] = jnp.zeros_like(dS)

        s = _ssd_prep(nc - 1 - t, dtr_ref, bias_ref, alog_ref, d_ref)
        states = [prev_ref[0, 0, p] for p in range(N_PAIR)]
        y, _, k = _ssd_chunk(xc_ref, s, states)
        xv, Bs, Cs, Xb = k["xv"], k["Bs"], k["Cs"], k["Xb"]

        zz = z_ref[...].astype(F32)
        sz = _sigmoid(zz)
        silu_z = zz * sz
        yg = y * silu_z
        r = _per_group(lambda a: lax.rsqrt(jnp.mean(a * a, axis=-1, keepdims=True) + EPS), yg)
        xhat = yg * r
        dout = dy_ref[...].astype(F32)
        gw = dout * nw_ref[...]
        dyg = r * (gw - xhat * _per_group(lambda a, c2: jnp.mean(a * c2, axis=-1, keepdims=True), gw, xhat))
        dnw = jnp.sum(dout * xhat, axis=0, keepdims=True)
        dz_ref[...] = (dyg * y * _dsilu(zz, sz)).astype(dz_ref.dtype)
        dy = dyg * silu_z
        dy0 = jnp.where(s["first"], dy, 0.0)
        dyb = (dy0.astype(BF16), (dy - dy0).astype(BF16))
        dYo = (dy * s["ecs_w"]).astype(BF16)

        dS_f = [dS[p] for p in range(N_PAIR)]
        dS_b = [d.astype(BF16) for d in dS_f]
        BdS, dXm, dprev, dCs, dMs, XdS = [], [], [], [[] for _ in range(SSD_GROUPS)], [], []
        for p in range(N_PAIR):
            g, ln = p // 2, slice(128 * p, 128 * (p + 1))
            BdS.append(_dot(Bs[g], dS_b[p], NT))
            dXm.append(_dot(k["Mb"][2 * p], dyb[0][:, ln], TN) + _dot(k["Mb"][2 * p + 1], dyb[1][:, ln], TN))
            dprev.append(_dot(dYo[:, ln], Cs[g], TN))
            dCs[g].append(_dot(dYo[:, ln], k["prev_b"][p], NN))
            for hh in range(2):
                dMs.append(_dot(dyb[hh][:, ln], Xb[hh][:, ln], NT))
                XdS.append(_dot(Xb[hh][:, ln], dS_b[p], NN))
        dX = jnp.concatenate(dXm, axis=1) + s["decay_w"] * jnp.concatenate(BdS, axis=1)
        dxs = dy * s["d_w"] + dX * s["dt_w"]

        sums = _dot01(jnp.concatenate([dX * xv, dy * k["yo"] * s["ecs_w"], dy * xv], axis=0), s["spread"], NT, "b")
        ddt, dcs = sums[0:Q], sums[Q:2 * Q]
        dD = jnp.sum(sums[2 * Q:3 * Q], axis=0, keepdims=True)

        col, row = s["col"], s["row"]
        lane1 = col[0:1]
        rowsT = lax.broadcasted_iota(jnp.int32, (128, Q), 0)
        dcs_t = jnp.zeros((128, Q), F32)
        dcl = jnp.zeros((1, 128), F32)
        dB_out, dC_out = [], []
        for g in range(SSD_GROUPS):
            Bf = Bs[g].astype(F32)
            dCB = jnp.zeros((Q, Q), F32)
            dBacc = jnp.zeros((Q, 128), F32)
            for r4 in range(HPG):
                h = HPG * g + r4
                p, hh = h // 2, h % 2
                W = dMs[h] * k["Ms"][h]
                dCB = dCB + dMs[h] * k["Lms"][h]
                decay_h = s["decay_w"][:, SSD_HEAD_DIM * h:SSD_HEAD_DIM * h + 1]
                dBacc = dBacc + decay_h * XdS[h]
                tdec = jnp.sum(XdS[h] * Bf, axis=1, keepdims=True) * decay_h
                dcs = dcs + jnp.where(col == h, jnp.sum(W, axis=1, keepdims=True) - tdec, 0.0)
                dcs_t = dcs_t - jnp.where(rowsT == h, jnp.sum(W, axis=0, keepdims=True), 0.0)
                rows_h = (row < SSD_HEAD_DIM) if hh == 0 else (row >= SSD_HEAD_DIM)
                sprev = jnp.sum(jnp.sum(jnp.where(rows_h, dS_f[p] * states[p], 0.0), axis=1, keepdims=True),
                                axis=0, keepdims=True)
                ecl = jnp.exp(s["cs"][Q - 1:Q, h:h + 1])
                dcl = dcl + jnp.where(lane1 == h, jnp.sum(tdec, axis=0, keepdims=True) + ecl * sprev, 0.0)
            dCB_b = dCB.astype(BF16)
            dC_out.append(dCs[g][0] + dCs[g][1] + _dot(dCB_b, Bs[g], NN))
            dB_out.append(dBacc + _dot(dCB_b, Cs[g], TN))
        for p in range(N_PAIR):
            dS[p] = dS_f[p] * k["ecl_rows"][p] + dprev[p]
        dxc_ref[...] = jnp.concatenate([dxs] + dB_out + dC_out, axis=1).astype(dxc_ref.dtype)

        dcs = dcs + _dot01(s["eye"], dcs_t, NT, "a") + jnp.where(row == Q - 1, dcl, 0.0)
        da = _dot01(row <= col, dcs, NN, "a")
        ddt = ddt + da * s["A"]
        dpre = jnp.where(s["valid"], ddt * _sigmoid(s["pre"]), 0.0)
        ddtr_ref[...] = dpre
        dbias = jnp.sum(dpre, axis=0, keepdims=True)
        dalog = jnp.sum(da * s["dt"], axis=0, keepdims=True) * s["A"]
        first_step = jnp.logical_and(b == 0, t == 0)

        @pl.when(first_step)
        def _():
            dbias_ref[...] = dbias
            dalog_ref[...] = dalog
            dd_ref[...] = dD
            dnw_ref[...] = dnw

        @pl.when(jnp.logical_not(first_step))
        def _():
            dbias_ref[...] += dbias
            dalog_ref[...] += dalog
            dd_ref[...] += dD
            dnw_ref[...] += dnw

    rb = lambda b, c: b * nc + nc - 1 - c
    rowblk = lambda w: pl.BlockSpec((Q, w), lambda b, c: (rb(b, c), 0))
    vec = lambda w: pl.BlockSpec((1, w), lambda b, c: (0, 0))
    return _call(
        body, name="ssd_bwd", grid=(Bl, nc),
        in_specs=_ssd_in_specs(nc, rev=True) + [
            pl.BlockSpec((1, 1, N_PAIR, 128, 128), lambda b, c: (b, nc - 1 - c, 0, 0, 0)), rowblk(SSD_INNER)],
        out_specs=[rowblk(SSD_CONV_CH), rowblk(SSD_INNER), rowblk(128), vec(128), vec(128), vec(128), vec(SSD_INNER)],
        out_shape=[jax.ShapeDtypeStruct((M, SSD_CONV_CH), BF16), jax.ShapeDtypeStruct(dproj.shape, BF16),
                   jax.ShapeDtypeStruct((M, 128), F32), jax.ShapeDtypeStruct((1, 128), F32),
                   jax.ShapeDtypeStruct((1, 128), F32), jax.ShapeDtypeStruct((1, 128), F32),
                   jax.ShapeDtypeStruct((1, SSD_INNER), F32)],
        scratch=[pltpu.VMEM((N_PAIR, 128, 128), F32)], sem=("arbitrary", "arbitrary"),
        args=(xc, dtr, proj, bias_p, alog_p, d_p, nw, prev, dya), comm=comm, into=(dproj, 1))


NSUB = Q // HG_CHUNK
HG_HP = 8
EXP_CAP = 80.0


def _hg_setup(blk, q_ref, f_ref, hb_ref):
    row = lax.broadcasted_iota(jnp.int32, (Q, Q), 0)
    col = lax.broadcasted_iota(jnp.int32, (Q, Q), 1)
    same = (row // HG_CHUNK) == (col // HG_CHUNK)
    causal = jnp.logical_and(same, col <= row)
    lb = _sigmoid(hb_ref[0:1, :] - hb_ref[1:2, :])
    fl = f_ref[...].astype(F32)
    sg = _sigmoid(fl)
    fg = lb + (1.0 - lb) * sg
    k = (1.0 - lb) * (1.0 - sg)
    gl = jnp.log(fg)
    G = _dot01(causal, gl, NN, "a")
    T = _dot01(same, gl, NN, "a")
    qv = q_ref[...].astype(F32)
    sq = _sigmoid(qv)
    eG = jnp.exp(G)
    eGn = jnp.exp(jnp.minimum(-G, EXP_CAP))
    eTG = jnp.exp(T - G)
    qt = qv * sq * eG
    kt = k * eGn
    kh = k * eTG
    valid = jnp.logical_or(blk > 0, row[:, :1] >= PAD)
    return dict(row=row, col=col, same=same, causal=causal, lb=lb, sg=sg, fg=fg, k=k, T=T, qv=qv, sq=sq,
                eG=eG, eGn=eGn, eTG=eTG, qt=qt, kt=kt, kh=kh, valid=valid)


def _hg_specs(nb, rev=False):
    rb = (lambda h, b, t: b * nb + nb - 1 - t) if rev else (lambda h, b, t: b * nb + t)
    w = 128 * HG_HP
    blk = lambda off: pl.BlockSpec((Q, w), lambda h, b, t, off=off: (rb(h, b, t), off // HG_HP + h))
    return [blk(24), blk(32), blk(40), blk(48),
            pl.BlockSpec((2, w), lambda h, b, t: (0, h)), pl.BlockSpec((1, w), lambda h, b, t: (0, h))]


HEAD_LANES = tuple(slice(128 * hh, 128 * (hh + 1)) for hh in range(HG_HP))


def _per_head(fn, *arrs):
    return jnp.concatenate([jnp.broadcast_to(fn(*(a[:, ln] for a in arrs)), (arrs[0].shape[0], 128))
                            for ln in HEAD_LANES], axis=1)


def _hgrn_fwd(proj, hb, nw, Bl, nb, comm=None):
    M = proj.shape[0]

    def body(q_ref, f_ref, i_ref, g_ref, hb_ref, nw_ref, y_ref, o_ref, st_ref, S):
        blk = pl.program_id(2)

        @pl.when(blk == 0)
        def _():
            S[...] = jnp.zeros_like(S)

        s = _hg_setup(blk, q_ref, f_ref, hb_ref)
        v = i_ref[...]
        qt_b, kt_b, kh_b = s["qt"].astype(BF16), s["kt"].astype(BF16), s["kh"].astype(BF16)
        eT = jnp.exp(s["T"])
        att = [jnp.where(s["causal"], _dot(qt_b[:, ln], kt_b[:, ln], NT), 0.0).astype(BF16) for ln in HEAD_LANES]
        o_intra = [_dot(att[hh], v[:, ln], NN) for hh, ln in enumerate(HEAD_LANES)]
        for j in range(NSUB):
            sl = slice(HG_CHUNK * j, HG_CHUNK * (j + 1))
            for hh, ln in enumerate(HEAD_LANES):
                St = S[hh]
                st_ref[0, hh, 0, j] = St
                o_ref[sl, ln] = o_intra[hh][sl] + _dot(qt_b[sl, ln], St.astype(BF16), NT)
                S[hh] = St * eT[HG_CHUNK * j:HG_CHUNK * j + 1, ln] + _dot(v[sl, ln], kh_b[sl, ln], TN)
        o = o_ref[...]
        r = _per_head(lambda a: lax.rsqrt(jnp.mean(a * a, axis=-1, keepdims=True) + EPS), o)
        gv = g_ref[...].astype(F32)
        y_ref[...] = (o * r * nw_ref[...] * gv * _sigmoid(gv)).astype(y_ref.dtype)

    rowblk = pl.BlockSpec((Q, 128 * HG_HP), lambda h, b, t: (b * nb + t, h))
    return _call(
        body, name="hgrn_fwd", grid=(HG_HEADS // HG_HP, Bl, nb), in_specs=_hg_specs(nb),
        out_specs=[rowblk, rowblk,
                   pl.BlockSpec((1, HG_HP, 1, NSUB, 128, 128), lambda h, b, t: (b, h, t, 0, 0, 0))],
        out_shape=[jax.ShapeDtypeStruct((M, HG_WIDTH), BF16), jax.ShapeDtypeStruct((M, HG_WIDTH), F32),
                   jax.ShapeDtypeStruct((Bl, HG_HEADS, nb, NSUB, 128, 128), F32)],
        scratch=[pltpu.VMEM((HG_HP, 128, 128), F32)], sem=("parallel", "arbitrary", "arbitrary"),
        args=(proj, proj, proj, proj, hb, nw), comm=comm)


def _hgrn_bwd(proj, hb, nw, o_saved, st_saved, dyb, dproj, Bl, nb, comm=None):
    assert HG_HP == HG_HEADS

    def body(q_ref, f_ref, i_ref, g_ref, hb_ref, nw_ref, o_ref, st_ref, dy_ref,
             d_ref, dhb_ref, dnw_ref, dS, a_dqt, a_dv, a_dkh, a_dgl):
        b, t = pl.program_id(1), pl.program_id(2)

        @pl.when(t == 0)
        def _():
            dS[...] = jnp.zeros_like(dS)

        first_step = jnp.logical_and(b == 0, t == 0)
        s = _hg_setup(nb - 1 - t, q_ref, f_ref, hb_ref)
        v = i_ref[...]
        qt_b, kt_b, kh_b = s["qt"].astype(BF16), s["kt"].astype(BF16), s["kh"].astype(BF16)
        eT = jnp.exp(s["T"])
        att = [jnp.where(s["causal"], _dot(qt_b[:, ln], kt_b[:, ln], NT), 0.0).astype(BF16) for ln in HEAD_LANES]

        o = o_ref[...]
        r = _per_head(lambda a: lax.rsqrt(jnp.mean(a * a, axis=-1, keepdims=True) + EPS), o)
        xhat = o * r
        gv = g_ref[...].astype(F32)
        sgv = _sigmoid(gv)
        dyv = dy_ref[...].astype(F32)
        d_on = dyv * gv * sgv
        dg_out = dyv * xhat * nw_ref[...] * _dsilu(gv, sgv)
        gw = d_on * nw_ref[...]
        do = r * (gw - xhat * _per_head(lambda a, c: jnp.mean(a * c, axis=-1, keepdims=True), gw, xhat))
        dnw = jnp.sum(d_on * xhat, axis=0, keepdims=True)
        do_b = do.astype(BF16)

        datt = [jnp.where(s["causal"], _dot(do_b[:, ln], v[:, ln], NT), 0.0).astype(BF16) for ln in HEAD_LANES]
        dqt = jnp.concatenate([_dot(datt[hh], kt_b[:, ln], NN) for hh, ln in enumerate(HEAD_LANES)], axis=1)
        dkt = jnp.concatenate([_dot(datt[hh], qt_b[:, ln], TN) for hh, ln in enumerate(HEAD_LANES)], axis=1)
        dv = jnp.concatenate([_dot(att[hh], do_b[:, ln], TN) for hh, ln in enumerate(HEAD_LANES)], axis=1)
        last_row = (lax.broadcasted_iota(jnp.int32, (HG_CHUNK, 128), 0) == HG_CHUNK - 1)
        for j in reversed(range(NSUB)):
            sl = slice(HG_CHUNK * j, HG_CHUNK * (j + 1))
            for hh, ln in enumerate(HEAD_LANES):
                St = st_ref[0, hh, 0, j]
                dSt = dS[hh]
                St_b, dSt_b = St.astype(BF16), dSt.astype(BF16)
                eT_j = eT[HG_CHUNK * j:HG_CHUNK * j + 1, ln]
                dkh_j = _dot(v[sl, ln], dSt_b, NN)
                a_dqt[sl, ln] = _dot(do_b[sl, ln], St_b, NN)
                a_dv[sl, ln] = _dot(kh_b[sl, ln], dSt_b, NT)
                a_dkh[sl, ln] = dkh_j
                dlast = (jnp.sum(St * dSt, axis=0, keepdims=True) * eT_j
                         + jnp.sum(dkh_j * s["kh"][sl, ln], axis=0, keepdims=True))
                a_dgl[sl, ln] = jnp.where(last_row, dlast, 0.0)
                dS[hh] = dSt * eT_j + _dot(do_b[sl, ln], qt_b[sl, ln], TN)
        dqt = dqt + a_dqt[...]
        dv = dv + a_dv[...]
        dkh = a_dkh[...]
        dG = dqt * s["qt"] - dkt * s["kt"] - dkh * s["kh"] + a_dgl[...]
        rev_causal = jnp.logical_and(s["same"], s["col"] >= s["row"])
        dgl = _dot01(rev_causal, dG, NN, "a")
        dk = dkt * s["eGn"] + dkh * s["eTG"]
        dfg = dgl / s["fg"] - dk
        lb, sg = s["lb"], s["sg"]
        keep = s["valid"].astype(F32)
        d_ref[:, 0:w] = (dqt * s["eG"] * _dsilu(s["qv"], s["sq"]) * keep).astype(d_ref.dtype)
        d_ref[:, w:2 * w] = (dfg * (1.0 - lb) * sg * (1.0 - sg) * keep).astype(d_ref.dtype)
        d_ref[:, 2 * w:3 * w] = (dv * keep).astype(d_ref.dtype)
        d_ref[:, 3 * w:4 * w] = (dg_out * keep).astype(d_ref.dtype)
        dlb = jnp.sum(dfg * (1.0 - sg) * keep, axis=0, keepdims=True) * lb * (1.0 - lb)
        dhb = jnp.concatenate([dlb, -dlb], axis=0)

        @pl.when(first_step)
        def _():
            dhb_ref[...] = dhb
            dnw_ref[...] = dnw

        @pl.when(jnp.logical_not(first_step))
        def _():
            dhb_ref[...] += dhb
            dnw_ref[...] += dnw

    w = 128 * HG_HP
    rowblk = pl.BlockSpec((Q, w), lambda h, b, t: (b * nb + nb - 1 - t, h))
    return _call(
        body, name="hgrn_bwd", grid=(HG_HEADS // HG_HP, Bl, nb),
        in_specs=_hg_specs(nb, rev=True) + [
            rowblk, pl.BlockSpec((1, HG_HP, 1, NSUB, 128, 128), lambda h, b, t: (b, h, nb - 1 - t, 0, 0, 0)), rowblk],
        out_specs=[pl.BlockSpec((pl.Element(Q), pl.Element(4 * w)),
                                lambda h, b, t: (pl.multiple_of((b * nb + nb - 1 - t) * Q, Q), 3 * HG_WIDTH)),
                   pl.BlockSpec((2, w), lambda h, b, t: (0, h)), pl.BlockSpec((1, w), lambda h, b, t: (0, h))],
        out_shape=[jax.ShapeDtypeStruct(dproj.shape, BF16),
                   jax.ShapeDtypeStruct((2, HG_WIDTH), F32), jax.ShapeDtypeStruct((1, HG_WIDTH), F32)],
        scratch=[pltpu.VMEM((HG_HP, 128, 128), F32)] + [pltpu.VMEM((Q, w), F32)] * 4,
        sem=("parallel", "arbitrary", "arbitrary"),
        args=(proj, proj, proj, proj, hb, nw, o_saved, st_saved, dyb), comm=comm, into=(dproj, 0))


def _adamw(name, parts, w, m, v, comm=None):
    R, C = w.shape
    S = parts.shape[0]
    tr, tc = (_tile(R, (256, 176, 128, 64, 8)), C) if R % 8 == 0 else (R, 256)
    c1, c2 = 1.0 - ADAM_B1 ** ADAM_STEP, 1.0 - ADAM_B2 ** ADAM_STEP

    def body(p_ref, w_ref, m_ref, v_ref, g_ref, d_ref, nm_ref, nv_ref):
        g = p_ref[0].astype(F32)
        for s in range(1, S):
            g = g + p_ref[s].astype(F32)
        nm = ADAM_B1 * m_ref[...] + (1.0 - ADAM_B1) * g
        nv = ADAM_B2 * v_ref[...] + (1.0 - ADAM_B2) * (g * g)
        g_ref[...] = g
        nm_ref[...] = nm
        nv_ref[...] = nv
        d_ref[...] = -ADAM_LR * ((nm / c1) / (jnp.sqrt(nv / c2) + ADAM_EPS) + ADAM_WD * w_ref[...])

    blk = pl.BlockSpec((tr, tc), lambda i, j: (i, j))
    return _call(
        body, name=name, grid=(R // tr, C // tc),
        in_specs=[pl.BlockSpec((S, tr, tc), lambda i, j: (0, i, j)), blk, blk, blk], out_specs=[blk] * 4,
        out_shape=[jax.ShapeDtypeStruct((R, C), F32)] * 4, scratch=[], sem=("parallel", "parallel"),
        args=(parts, w, m, v), comm=comm)


def _pair_sum(name, by_core, arrived):
    _, J, R, C = by_core.shape
    tc = _tile(C, (512, 256, 128))

    def body(c_ref, a_ref, b_ref, o_ref):
        o_ref[...] = (a_ref[0].astype(F32) + b_ref[...].astype(F32)).astype(o_ref.dtype)

    blk = pl.BlockSpec((1, R, tc), lambda j, k, c_ref: (j, 0, k))
    return pl.pallas_call(
        body, name=name,
        grid_spec=pltpu.PrefetchScalarGridSpec(
            num_scalar_prefetch=1, grid=(J, C // tc),
            in_specs=[pl.BlockSpec((1, 1, R, tc), lambda j, k, c_ref: (c_ref[0], j, 0, k)), blk], out_specs=blk),
        out_shape=jax.ShapeDtypeStruct(arrived.shape, arrived.dtype), compiler_params=_params(("parallel", "parallel")),
    )(lax.axis_index("c").astype(jnp.int32).reshape(1), by_core, arrived)


def _sum_parts(name, parts):
    S, R, C = parts.shape

    def body(p_ref, o_ref):
        g = p_ref[0]
        for s in range(1, S):
            g = g + p_ref[s]
        o_ref[...] = g

    return pl.pallas_call(
        body, name=name, out_shape=jax.ShapeDtypeStruct((R, C), F32),
        in_specs=[pl.BlockSpec(memory_space=pltpu.VMEM)], out_specs=pl.BlockSpec(memory_space=pltpu.VMEM),
    )(parts)


def _heads_to_lanes(p):
    return jnp.pad(p, [(0, 0)] * (p.ndim - 1) + [(0, 128 - SSD_HEADS)])


def _lanes_to_heads(p):
    return p[..., :SSD_HEADS]


def _pack_rows(arrs):
    flat = jnp.concatenate([a.reshape(-1).astype(F32) for a in arrs])
    return jnp.pad(flat, (0, (-flat.shape[0]) % (8 * D_MODEL))).reshape(-1, D_MODEL)


def _unpack_rows(packed, like):
    flat, outs, at = packed.reshape(-1), [], 0
    for a in like:
        outs.append(flat[at:at + a.size].reshape(a.shape))
        at += a.size
    return outs


def _cols(gth):
    return jnp.transpose(gth, (1, 0, 2)).reshape(gth.shape[1], -1)


def _rows(gth):
    return gth.reshape(-1, gth.shape[2])


def _to_rows(g):
    return g.reshape(N_DEV, -1, g.shape[1]).astype(BF16)


def _by_core(g):
    return jnp.transpose(g.reshape(N_DEV // 2, 2, -1, g.shape[1]), (1, 0, 2, 3)).astype(BF16)


DT_ROW = 3072


def _chip_sums(tag, by_core, swap_in=None):
    arrived = swap_in(by_core) if swap_in else _exchange(tag + "_swap", "swap", by_core)
    return [_pair_sum(f"{tag}_chipsum{i}", m, a) for i, (m, a) in enumerate(zip(by_core, arrived))]


def _ffn_fwd_gu(tag, n, w_gu_t, comm=None):
    M = n.shape[0]
    F = w_gu_t.shape[0] // 2
    tm = _tile(M, (544, 256))
    outs = _fused_matmul(
        tag + "_gu", M, F, D_MODEL,
        [dict(a=n, b=w_gu_t, trans_b=True, acc=0, resident=True),
         dict(a=n, b=w_gu_t, trans_b=True, bn_off=1, acc=1, resident=True)], [],
        lambda accs, ex: (accs[0], accs[1], accs[0] * _sigmoid(accs[0]) * accs[1]),
        [BF16, BF16, BF16], 2, tm, F, D_MODEL, outer="i", comm=comm, sub=256)
    return (n, *outs[:3]), outs[3:]


def _rmsnorm_tile(x, w):
    return x * lax.rsqrt(jnp.mean(x * x, axis=-1, keepdims=True) + EPS) * w


def _ffn_fwd_down(tag, h, a, w_down, next_norm=None, comm=None):
    M = h.shape[0]
    F = w_down.shape[0]
    tm = _tile(M, (1088, 544, 256))
    if next_norm is None:
        (h_out,) = _fused_matmul(
            tag + "_down", M, D_MODEL, F, [dict(a=a, b=w_down, acc=0)], [(h, 0)],
            lambda accs, ex: (ex[0] + 0.5 * accs[0],), [F32], 1, tm, D_MODEL, F, outer="j", sub=256)
        return h_out

    def with_norm(accs, ex):
        h_new = ex[0] + 0.5 * accs[0]
        return h_new, _rmsnorm_tile(h_new, ex[1])

    return _fused_matmul(tag + "_down", M, D_MODEL, F, [dict(a=a, b=w_down, acc=0, resident=True)], [(h, 0)], with_norm,
                         [F32, BF16], 1, tm, D_MODEL, F, outer="j", vecs=[next_norm], comm=comm)


def _ffn_bwd(tag, dh, dh_b, h, norm_w, w_gu_t, w_down, saved, scatter=False):
    n, g, u, a = saved
    M = h.shape[0]
    F = w_down.shape[0]
    tm = _tile(M, (544, 256))
    tn = _tile(F, (1408, 704, 256))

    def swiglu_bwd(accs, ex):
        da, gv, uv = 0.5 * accs[0], ex[0].astype(F32), ex[1].astype(F32)
        s = _sigmoid(gv)
        return da * uv * _dsilu(gv, s), da * gv * s

    (dgu,) = _fused_matmul(
        tag + "_dact", M, F, D_MODEL, [dict(a=dh_b, b=w_down, trans_b=True, acc=0, resident=True)], [(g, 0), (u, 0)],
        swiglu_bwd, [BF16, BF16], 1, tm, F, D_MODEL, outer="i", stack=True, sub=256)
    tr = _tile(M, (2176, 256))
    (dw_down,) = _matmul_tn(tag + "_dwd", a, dh_b, tn, D_MODEL, tr, scale=0.5)
    dw_gu_t, *p_down = _matmul_tn(tag + "_dwgu", dgu, n, tn, D_MODEL, tr,
                                  comm=("scatter", [_to_rows(dw_down)]) if scatter else None)
    comm = None
    if scatter:
        comm = ("chips", _chip_sums(tag + "_wgu", [_by_core(dw_gu_t)]))
    def norm_bwd(accs, ex):
        dh_prev, dw = _rmsnorm_bwd_tile(accs[0], ex[0], ex[2], ex[1])
        return dh_prev, dh_prev, dw

    dh_prev, dh_prev_b, dnorm, *p_gu = _fused_matmul(
        tag + "_dn", M, D_MODEL, 2 * F,
        [dict(a=dgu, b=w_gu_t, acc=0, resident=True)], [(h, 0), (dh, 0)],
        norm_bwd, [F32, BF16], 1, tm, D_MODEL, 2 * F, outer="i", comm=comm, vecs=[norm_w], row_sums=1)
    return (dh_prev, dh_prev_b, dnorm, *((p_gu[0], p_down[0]) if scatter else (dw_gu_t, dw_down)))


def kernel(x, meta_tokens, ffn1_norm, ffn1_w_gu, ffn1_w_down, mix_norm, w_in, ssd_conv_w, ssd_conv_b, ssd_dt_bias, ssd_a_log, ssd_d, ssd_norm, hg_lower_bound, hg_norm, w_branch_a, w_branch_b, w_out, ffn2_norm, ffn2_w_gu, ffn2_w_down, final_norm, loss_target, m_meta_tokens, m_ffn1_norm, m_ffn1_w_gu, m_ffn1_w_down, m_mix_norm, m_w_in, m_ssd_conv_w, m_ssd_conv_b, m_ssd_dt_bias, m_ssd_a_log, m_ssd_d, m_ssd_norm, m_hg_lower_bound, m_hg_norm, m_w_branch_a, m_w_branch_b, m_w_out, m_ffn2_norm, m_ffn2_w_gu, m_ffn2_w_down, m_final_norm, v_meta_tokens, v_ffn1_norm, v_ffn1_w_gu, v_ffn1_w_down, v_mix_norm, v_w_in, v_ssd_conv_w, v_ssd_conv_b, v_ssd_dt_bias, v_ssd_a_log, v_ssd_d, v_ssd_norm, v_hg_lower_bound, v_hg_norm, v_w_branch_a, v_w_branch_b, v_w_out, v_ffn2_norm, v_ffn2_w_gu, v_ffn2_w_down, v_final_norm):
    Bl, S, D = x.shape
    T = PAD + N_META + S
    nc = T // Q
    M = Bl * T
    me = 4 * lax.axis_index("x") + 2 * lax.axis_index("y") + lax.axis_index("c")

    bf = lambda a: a[0].astype(BF16)
    bft = lambda a: a[0].T.astype(BF16)
    g_meta, g_conv_w = _exchange("gather_small", "gather", [meta_tokens, ssd_conv_w[0]])
    meta_full, conv_w_full = _cols(g_meta), _cols(g_conv_w)
    bias_p, alog_p, d_p = _heads_to_lanes(ssd_dt_bias), _heads_to_lanes(ssd_a_log), _heads_to_lanes(ssd_d)
    final_w = final_norm.reshape(1, D)

    h0, n1, g_wgu1 = _embed_norm(x, meta_full, ffn1_norm, comm=("gather", [bft(ffn1_w_gu)]))
    wgu1 = _rows(g_wgu1)
    tm = _tile(M, (1088, 544, 256))
    win_shard = bft(w_in)
    cut = (win_shard.shape[0] // 32) * 16
    ffn1_saved, (g_wd1, g_win_a) = _ffn_fwd_gu("ffn1", n1, wgu1, comm=("gather", [bf(ffn1_w_down), win_shard[:cut]]))
    wd1 = _rows(g_wd1)
    h1, un, g_win_b = _ffn_fwd_down("ffn1", h0, ffn1_saved[3], wd1, next_norm=mix_norm,
                                    comm=("gather", [win_shard[cut:]]))
    win_t = _rows(jnp.concatenate([g_win_a, g_win_b], axis=1))
    win_dt = jnp.pad(win_t[DT_ROW:DT_ROW + SSD_HEADS], ((0, 128 - SSD_HEADS), (0, 0)))
    plain = lambda accs, ex: (accs[0],)
    proj, g_wa, g_wb, g_wo = _fused_matmul(
        "in_proj", M, N_MAIN, D, [dict(a=un, b=win_t, trans_b=True, acc=0, b_shift=(DT_ROW // 3072, SSD_HEADS))], [],
        plain, [BF16], 1, tm, 3072, D,
        outer="j", comm=("gather", [bf(w_branch_a), bf(w_branch_b), bf(w_out)]), sub=512)
    wa, wb, wo = _rows(g_wa), _rows(g_wb), _rows(g_wo)
    (dtr,) = _fused_matmul("in_proj_dt", M, 128, D, [dict(a=un, b=win_dt, trans_b=True, acc=0)], [], plain, [F32], 1,
                           tm, 128, D, outer="j")
    xc = _conv_fwd(proj, conv_w_full, ssd_conv_b, Bl, T)
    ya, ssd_prev = _ssd_fwd(xc, dtr, proj, bias_p, alog_p, d_p, ssd_norm, Bl, nc)
    ssd_prev = ssd_prev.reshape(Bl, nc, N_PAIR, 128, 128)
    yb, hg_o, hg_st, g_wgu2, g_wd2 = _hgrn_fwd(proj, hg_lower_bound, hg_norm, Bl, nc,
                                               comm=("gather", [bft(ffn2_w_gu), bf(ffn2_w_down)]))
    wgu2, wd2 = _rows(g_wgu2), _rows(g_wd2)

    def branch_fwd(accs, ex):
        pa, pb = accs
        return pa, pb, _sigmoid(ex[0].astype(F32)) * pa + _sigmoid(ex[1].astype(F32)) * pb

    pa, pb, merged = _fused_matmul(
        "branches", M, D, D, [dict(a=ya, b=wa, acc=0), dict(a=yb, b=wb, acc=1)], [(proj, 7), (proj, 8)],
        branch_fwd, [BF16, BF16, BF16], 2, tm, D, D, outer="j")
    def out_with_norm(accs, ex):
        h_new = ex[0] + accs[0]
        return h_new, _rmsnorm_tile(h_new, ex[1])

    h2, n2 = _fused_matmul("out_proj", M, D, D, [dict(a=merged, b=wo, acc=0)], [(h1, 0)], out_with_norm,
                           [F32, BF16], 1, tm, D, D, outer="j", vecs=[ffn2_norm])
    ffn2_saved, _ = _ffn_fwd_gu("ffn2", n2, wgu2)
    h3 = _ffn_fwd_down("ffn2", h2, ffn2_saved[3], wd2)

    dh3, dh3_b, d_final, loss_part = _loss_head(h3, final_w, loss_target, Bl, nc)
    dh2, dh2_b, d_ffn2_norm, d_wgu2, d_wd2 = _ffn_bwd("ffn2", dh3, dh3_b, h2, ffn2_norm, wgu2, wd2, ffn2_saved)

    def branch_bwd(accs, ex):
        dm = accs[0]
        ga, gb, pav, pbv = (e.astype(F32) for e in ex)
        sa, sb = _sigmoid(ga), _sigmoid(gb)
        return (dm * sa, dm * sb,
                jnp.concatenate([dm * pav * sa * (1.0 - sa), dm * pbv * sb * (1.0 - sb)], axis=1))

    d_merged_outs = []

    def d_merged_with_swap(theirs):
        d_merged_outs.extend(_fused_matmul(
            "d_merged", M, D, D, [dict(a=dh2_b, b=wo, trans_b=True, acc=0)], [(proj, 7), (proj, 8), (pa, 0), (pb, 0)],
            branch_bwd, [BF16] * 2, 1, tm, D, D, outer="j", comm=("swap", theirs),
            wide=dict(width=2 * D, col=7 * D, total=N_MAIN, dtype=BF16)))
        return d_merged_outs[3:]

    s_ffn2 = _chip_sums("ffn2", [_by_core(d_wgu2), _by_core(d_wd2)], swap_in=d_merged_with_swap)
    dpa, dpb, dproj = d_merged_outs[:3]
    (d_wo,) = _matmul_tn("d_w_out", merged, dh2_b, 512, D, M)
    (d_wa,) = _matmul_tn("d_w_a", ya, dpa, 512, D, M)
    (d_wb,) = _matmul_tn("d_w_b", yb, dpb, 512, D, M)
    dya, dyb = _fused_matmul(
        "d_branches", M, D, D, [dict(a=dpa, b=wa, trans_b=True, acc=0), dict(a=dpb, b=wb, trans_b=True, acc=1)], [],
        lambda accs, ex: (accs[0], accs[1]), [BF16, BF16], 2, tm, D, D, outer="j")
    *ssd_grads, p_wgu2, p_wd2 = _ssd_bwd(xc, dtr, proj, bias_p, alog_p, d_p, ssd_norm, ssd_prev, dya, dproj, Bl, nc,
                                         comm=("chips", s_ffn2))
    dxc, dproj, ddtr, d_bias_p, d_alog_p, d_d_p, d_ssd_norm = ssd_grads
    dproj, d_conv_w, d_conv_b = _conv_bwd(proj, conv_w_full, ssd_conv_b, dxc, dproj, Bl, T)
    dproj, d_hb, d_hg_norm, p_wa, p_wb, p_wo = _hgrn_bwd(
        proj, hg_lower_bound, hg_norm, hg_o, hg_st, dyb, dproj, Bl, nc,
        comm=("scatter", [_to_rows(d_wa), _to_rows(d_wb), _to_rows(d_wo)]))
    ddtr_b = ddtr.astype(BF16)
    (d_win_t,) = _matmul_tn("d_w_in", dproj, un, 768, D, M, out_skip=(DT_ROW, SSD_HEADS))
    (d_win_dt,) = _matmul_tn("d_w_in_dt", ddtr_b, un, 128, D, M)
    d_win_t = lax.dynamic_update_slice(d_win_t, d_win_dt[:SSD_HEADS], (DT_ROW, 0))
    d_un_dt_outs = []

    def d_un_dt_with_swap(theirs):
        d_un_dt_outs.extend(_fused_matmul("d_un_dt", M, D, 128, [dict(a=ddtr_b, b=win_dt, acc=0)], [], plain, [F32], 1,
                                          tm, D, 128, outer="j", comm=("swap", theirs)))
        return d_un_dt_outs[1:]

    s_win = _chip_sums("w_in", [_by_core(d_win_t)], swap_in=d_un_dt_with_swap)
    def mix_norm_bwd(accs, ex):
        dh, dw = _rmsnorm_bwd_tile(accs[0] + ex[0], ex[1], ex[3], ex[2])
        return dh, dh, dw

    dh1, dh1_b, d_mix_norm, p_win = _fused_matmul(
        "d_un", M, D, N_MAIN, [dict(a=dproj, b=win_t, acc=0, b_shift=(DT_ROW // 3072, SSD_HEADS))],
        [(d_un_dt_outs[0], 0), (h1, 0), (dh2, 0)],
        mix_norm_bwd, [F32, BF16], 1, _tile(M, (544, 256)), D, 3072, outer="i", comm=("chips", s_win),
        vecs=[mix_norm], row_sums=1)
    dh0, _, d_ffn1_norm, p_wgu1, p_wd1 = _ffn_bwd("ffn1", dh1, dh1_b, h0, ffn1_norm, wgu1, wd1, ffn1_saved, scatter=True)

    dh0 = dh0.reshape(Bl, T, D)
    grad_x = dh0[:, PAD + N_META:]
    d_meta = dh0[:, PAD:PAD + N_META]

    small_grads = [d_ffn1_norm, d_mix_norm, d_conv_b, _lanes_to_heads(d_bias_p), _lanes_to_heads(d_alog_p),
                   _lanes_to_heads(d_d_p), d_ssd_norm, d_hb, d_hg_norm, d_ffn2_norm, d_final.reshape(D), d_conv_w]
    small_like = small_grads + [d_meta[b] for b in range(Bl)] + [loss_part[0, 0:1]]
    small_packed = _pack_rows(small_like)
    parts = [p_wgu1, p_wd1, p_win, p_wa, p_wb, p_wo, p_wgu2, p_wd2]

    names = ["meta_tokens", "ffn1_norm", "ffn1_w_gu", "ffn1_w_down", "mix_norm", "w_in", "ssd_conv_w", "ssd_conv_b",
             "ssd_dt_bias", "ssd_a_log", "ssd_d", "ssd_norm", "hg_lower_bound", "hg_norm", "w_branch_a", "w_branch_b",
             "w_out", "ffn2_norm", "ffn2_w_gu", "ffn2_w_down", "final_norm"]
    W = dict(meta_tokens=meta_tokens, ffn1_norm=ffn1_norm, ffn1_w_gu=ffn1_w_gu, ffn1_w_down=ffn1_w_down, mix_norm=mix_norm,
             w_in=w_in, ssd_conv_w=ssd_conv_w, ssd_conv_b=ssd_conv_b, ssd_dt_bias=ssd_dt_bias, ssd_a_log=ssd_a_log,
             ssd_d=ssd_d, ssd_norm=ssd_norm, hg_lower_bound=hg_lower_bound, hg_norm=hg_norm, w_branch_a=w_branch_a,
             w_branch_b=w_branch_b, w_out=w_out, ffn2_norm=ffn2_norm, ffn2_w_gu=ffn2_w_gu, ffn2_w_down=ffn2_w_down,
             final_norm=final_norm)
    Mo = dict(meta_tokens=m_meta_tokens, ffn1_norm=m_ffn1_norm, ffn1_w_gu=m_ffn1_w_gu, ffn1_w_down=m_ffn1_w_down,
              mix_norm=m_mix_norm, w_in=m_w_in, ssd_conv_w=m_ssd_conv_w, ssd_conv_b=m_ssd_conv_b, ssd_dt_bias=m_ssd_dt_bias,
              ssd_a_log=m_ssd_a_log, ssd_d=m_ssd_d, ssd_norm=m_ssd_norm, hg_lower_bound=m_hg_lower_bound, hg_norm=m_hg_norm,
              w_branch_a=m_w_branch_a, w_branch_b=m_w_branch_b, w_out=m_w_out, ffn2_norm=m_ffn2_norm, ffn2_w_gu=m_ffn2_w_gu,
              ffn2_w_down=m_ffn2_w_down, final_norm=m_final_norm)
    Vo = dict(meta_tokens=v_meta_tokens, ffn1_norm=v_ffn1_norm, ffn1_w_gu=v_ffn1_w_gu, ffn1_w_down=v_ffn1_w_down,
              mix_norm=v_mix_norm, w_in=v_w_in, ssd_conv_w=v_ssd_conv_w, ssd_conv_b=v_ssd_conv_b, ssd_dt_bias=v_ssd_dt_bias,
              ssd_a_log=v_ssd_a_log, ssd_d=v_ssd_d, ssd_norm=v_ssd_norm, hg_lower_bound=v_hg_lower_bound, hg_norm=v_hg_norm,
              w_branch_a=v_w_branch_a, w_branch_b=v_w_branch_b, w_out=v_w_out, ffn2_norm=v_ffn2_norm, ffn2_w_gu=v_ffn2_w_gu,
              ffn2_w_down=v_ffn2_w_down, final_norm=v_final_norm)
    grads, deltas, new_m, new_v = {}, {}, {}, {}
    big_names = ["ffn1_w_gu", "ffn1_w_down", "w_in", "w_branch_a", "w_branch_b", "w_out", "ffn2_w_gu", "ffn2_w_down"]
    transposed = ("ffn1_w_gu", "ffn2_w_gu", "w_in")
    small_all = None
    for nm, part in zip(big_names, parts):
        view = (lambda a: a[0].T) if nm in transposed else (lambda a: a[0])
        back = (lambda o: o.T[None]) if nm in transposed else (lambda o: o[None])
        outs = _adamw("adamw_" + nm, part, view(W[nm]), view(Mo[nm]), view(Vo[nm]),
                      comm=("gather", [small_packed]) if small_all is None else None)
        if small_all is None:
            small_all = outs[4]
        grads[nm], deltas[nm], new_m[nm], new_v[nm] = (back(o) for o in outs[:4])
    unpacked = _unpack_rows(_sum_parts("sum_small_grads", small_all), small_like)
    g_small = unpacked[:len(small_grads)]
    g_meta_full = unpacked[len(small_grads)]
    for b in range(1, Bl):
        g_meta_full = g_meta_full + unpacked[len(small_grads) + b]
    g_meta = lax.dynamic_slice_in_dim(g_meta_full, me * (D // N_DEV), D // N_DEV, axis=1)
    g_conv_w = lax.dynamic_slice_in_dim(g_small[11], me * (SSD_CONV_CH // N_DEV), SSD_CONV_CH // N_DEV, axis=1)
    loss = unpacked[-1].reshape(())
    small_names = ["ffn1_norm", "mix_norm", "ssd_conv_b", "ssd_dt_bias", "ssd_a_log", "ssd_d", "ssd_norm", "hg_lower_bound",
                   "hg_norm", "ffn2_norm", "final_norm", "ssd_conv_w", "meta_tokens"]
    small_g = g_small[:11] + [g_conv_w.reshape(ssd_conv_w.shape), g_meta]
    pk = lambda d: _pack_rows([d[nm] for nm in small_names])
    outs = _adamw("adamw_small", _pack_rows(small_g)[None], pk(W), pk(Mo), pk(Vo))
    like = [W[nm] for nm in small_names]
    for dst, o in zip((grads, deltas, new_m, new_v), outs):
        for nm, val in zip(small_names, _unpack_rows(o, like)):
            dst[nm] = val

    return (loss, grad_x, *[grads[nm] for nm in names], *[deltas[nm] for nm in names],
            *[new_m[nm] for nm in names], *[new_v[nm] for nm in names])
```

```python
import functools

import jax
import jax.numpy as jnp
from jax import lax
from jax.experimental import pallas as pl
from jax.experimental.pallas import tpu as pltpu

F32, BF16 = jnp.float32, jnp.bfloat16
NN, NT, TN = ((1,), (0,)), ((1,), (1,)), ((0,), (0,))
MESH_AXES = ("x", "y", "c")
N_DEV = 8

D_MODEL = 1024
N_META = 16
EPS = 1e-6
SSD_HEADS, SSD_HEAD_DIM, SSD_GROUPS, SSD_STATE, SSD_CONV, Q = 16, 64, 4, 128, 4, 128
SSD_INNER = SSD_HEADS * SSD_HEAD_DIM
SSD_CONV_CH = SSD_INNER + 2 * SSD_GROUPS * SSD_STATE
HG_WIDTH, HG_HEADS, HG_CHUNK = 1024, 8, 16
PAD = Q - N_META
N_MAIN = 9 * 1024
ADAM_LR, ADAM_B1, ADAM_B2, ADAM_EPS, ADAM_WD, ADAM_STEP = 0.001, 0.9, 0.999, 1e-08, 0.01, 10
VMEM_LIMIT = 52 * 1024 * 1024


def _dot(a, b, dims):
    return lax.dot_general(a, b, (dims, ((), ())), preferred_element_type=F32)


def _dot01(a, b, dims, sel):
    x = b if sel == "a" else a
    hi = x.astype(BF16)
    r1 = x - hi.astype(F32)
    mid = r1.astype(BF16)
    lo = (r1 - mid.astype(F32)).astype(BF16)
    s = (a if sel == "a" else b).astype(BF16)
    parts = [_dot(s, p, dims) if sel == "a" else _dot(p, s, dims) for p in (hi, mid, lo)]
    return parts[0] + parts[1] + parts[2]


def _sigmoid(x):
    return 1.0 / (1.0 + jnp.exp(-x))


def _dsilu(x, s):
    return s * (1.0 + x * (1.0 - s))


def _softplus(x):
    e = jnp.exp(-jnp.abs(x))
    u = 1.0 + e
    log1p_e = jnp.where(u == 1.0, e, jnp.log(u) * e / (u - 1.0))
    return jnp.maximum(x, 0.0) + log1p_e


def _params(sem):
    return pltpu.CompilerParams(dimension_semantics=sem, vmem_limit_bytes=VMEM_LIMIT)


def _tile(n, prefs):
    for p in prefs:
        if n % p == 0:
            return p
    return n


CHIP_FLIPS = ((1, 0), (0, 1), (1, 1))
N_PEER = N_DEV - 1


def _comm_gather(srcs, outs, send_sems, recv_sems, local_sems):
    n = len(srcs)
    x, y, c = (lax.axis_index(a) for a in MESH_AXES)
    dev = lambda px, py, pc: 4 * px + 2 * py + pc
    me, sib = dev(x, y, c), (x, y, 1 - c)
    nbr_x, nbr_y, diag = (1 - x, y), (x, 1 - y), (1 - x, 1 - y)
    via = (x ^ c, y ^ (1 - c), c)
    sent_on = dev(x ^ (1 - c), y ^ c, c)

    def rc(w, k, slot, to, src=None):
        return pltpu.make_async_remote_copy(
            src_ref=outs[w].at[slot] if src is None else src, dst_ref=outs[w].at[slot],
            send_sem=send_sems.at[w, k], recv_sem=recv_sems.at[w, k], device_id=to, device_id_type=pl.DeviceIdType.MESH)

    def local(w):
        return pltpu.make_async_copy(srcs[w], outs[w].at[me], local_sems.at[w])

    def start():
        for w in range(n):
            local(w).start()
            rc(w, 0, me, sib, src=srcs[w]).start()
            rc(w, 1, me, (*nbr_x, c), src=srcs[w]).start()
            rc(w, 2, me, (*nbr_y, c), src=srcs[w]).start()

    def pass_on():
        for w in range(n):
            rc(w, 1, dev(*nbr_x, c), sib).wait_recv()
            rc(w, 2, dev(*nbr_y, c), sib).wait_recv()
            rc(w, 3, sent_on, via).start()
            rc(w, 4, dev(*nbr_x, c), sib).start()
            rc(w, 5, dev(*nbr_y, c), sib).start()

    def pass_on_diagonal():
        for w in range(n):
            rc(w, 3, dev(*diag, c), sib).wait_recv()
            rc(w, 6, dev(*diag, c), sib).start()

    def finish():
        for w in range(n):
            rc(w, 0, dev(x, y, 1 - c), sib).wait_recv()
            for k, chip in ((4, nbr_x), (5, nbr_y), (6, diag)):
                rc(w, k, dev(*chip, 1 - c), sib).wait_recv()
            for k in range(N_PEER):
                rc(w, k, me, sib, src=srcs[w]).wait_send()
            local(w).wait()

    return start, (pass_on, pass_on_diagonal), finish


def _comm_scatter(srcs, outs, send_sems, recv_sems, local_sems):
    n = len(srcs)
    x, y, c = (lax.axis_index(a) for a in MESH_AXES)
    me = 4 * x + 2 * y + c

    def copies():
        out = []
        for w in range(n):
            out.append(pltpu.make_async_copy(srcs[w].at[me], outs[w].at[me], local_sems.at[w]))
            for k in range(1, N_DEV):
                px, py, pc = x ^ (k >> 2), y ^ ((k >> 1) & 1), c ^ (k & 1)
                out.append(pltpu.make_async_remote_copy(
                    src_ref=srcs[w].at[4 * px + 2 * py + pc], dst_ref=outs[w].at[me],
                    send_sem=send_sems.at[w, k - 1], recv_sem=recv_sems.at[w, k - 1],
                    device_id=(px, py, pc), device_id_type=pl.DeviceIdType.MESH))
        return out

    def start():
        for cp in copies():
            cp.start()

    def finish():
        for cp in copies():
            cp.wait()

    return start, None, finish


def _comm_swap(srcs, outs, send_sems, recv_sems, local_sems):
    x, y, c = (lax.axis_index(a) for a in MESH_AXES)

    def copies():
        return [pltpu.make_async_remote_copy(
            src_ref=srcs[w].at[1 - c], dst_ref=outs[w], send_sem=send_sems.at[w, 0], recv_sem=recv_sems.at[w, 0],
            device_id=(x, y, 1 - c), device_id_type=pl.DeviceIdType.MESH) for w in range(len(srcs))]

    def start():
        for cp in copies():
            cp.start()

    def finish():
        for cp in copies():
            cp.wait()

    return start, None, finish


def _comm_chips(srcs, outs, send_sems, recv_sems, local_sems):
    n = len(srcs)
    x, y, c = (lax.axis_index(a) for a in MESH_AXES)
    mine = 2 * x + y

    def copies():
        out = []
        for w in range(n):
            out.append(pltpu.make_async_copy(srcs[w].at[mine], outs[w].at[mine], local_sems.at[w]))
            for j, (fx, fy) in enumerate(CHIP_FLIPS):
                px, py = x ^ fx, y ^ fy
                out.append(pltpu.make_async_remote_copy(
                    src_ref=srcs[w].at[2 * px + py], dst_ref=outs[w].at[mine],
                    send_sem=send_sems.at[w, j], recv_sem=recv_sems.at[w, j],
                    device_id=(px, py, c), device_id_type=pl.DeviceIdType.MESH))
        return out

    def start():
        for cp in copies():
            cp.start()

    def finish():
        for cp in copies():
            cp.wait()

    return start, None, finish


def _comm_parts(comm):
    kind, arrays = comm[:2]
    n = len(arrays)
    lead = {"gather": lambda a: (N_DEV,) + a.shape, "scatter": lambda a: (N_DEV,) + a.shape[1:],
            "swap": lambda a: a.shape[1:], "chips": lambda a: a.shape}[kind]
    shapes = [jax.ShapeDtypeStruct(lead(a), a.dtype) for a in arrays]
    sems = [pltpu.SemaphoreType.DMA((n, N_PEER)), pltpu.SemaphoreType.DMA((n, N_PEER)), pltpu.SemaphoreType.DMA((n,))]
    make = {"gather": _comm_gather, "scatter": _comm_scatter, "swap": _comm_swap, "chips": _comm_chips}[kind]
    return n, shapes, sems, make


def _exchange(name, kind, arrays):
    n, shapes, sems, make = _comm_parts((kind, arrays))

    def body(*refs):
        start, middle, finish = make(refs[:n], refs[n:2 * n], *refs[2 * n:])
        start()
        for stage in middle or ():
            stage()
        finish()

    any_spec = pl.BlockSpec(memory_space=pl.ANY)
    return pl.pallas_call(
        body, name=name, in_specs=[any_spec] * n, out_specs=[any_spec] * n, out_shape=shapes, scratch_shapes=sems,
        compiler_params=pltpu.CompilerParams(has_side_effects=True),
    )(*arrays)


def _call(body, *, name, grid, in_specs, out_specs, out_shape, scratch, sem, args, comm=None, into=None):
    any_spec = pl.BlockSpec(memory_space=pl.ANY)
    in_specs, args, aliases, n_body_in = list(in_specs), list(args), {}, len(in_specs)
    for arr, k in ([] if into is None else into if isinstance(into, list) else [into]):
        aliases[len(in_specs)] = k
        in_specs.append(any_spec)
        args.append(arr)
    n_in, n_out, n_scr = len(in_specs), len(out_specs), len(scratch)
    if comm is None:
        def plain(*refs):
            body(*refs[:n_body_in], *refs[n_in:])

        return pl.pallas_call(plain, name=name, grid=grid, in_specs=in_specs, out_specs=out_specs, out_shape=out_shape,
                              scratch_shapes=scratch, input_output_aliases=aliases, compiler_params=_params(sem))(*args)
    n, shapes, sems, make = _comm_parts(comm)

    def carrier(*refs):
        ins, csrc = refs[:n_body_in], refs[n_in:n_in + n]
        outs, cout = refs[n_in + n:n_in + n + n_out], refs[n_in + n + n_out:n_in + 2 * n + n_out]
        rest = refs[n_in + 2 * n + n_out:]
        start, middle, finish = make(csrc, cout, *rest[n_scr:])
        ids = [pl.program_id(a) for a in range(len(grid))]
        step = functools.reduce(lambda acc, ig: acc * ig[1] + ig[0], zip(ids, grid), 0)
        n_steps = functools.reduce(lambda a, b: a * b, grid, 1)
        pl.when(step == 0)(start)
        body(*ins, *outs, *rest[:n_scr])
        if middle:
            pl.when(step == max(0, (3 * n_steps) // 4 - 1))(middle[0])
            pl.when(step == n_steps - 1)(middle[1])
        pl.when(step == n_steps - 1)(finish)

    return pl.pallas_call(
        carrier, name=name, grid=grid, in_specs=in_specs + [any_spec] * n,
        out_specs=list(out_specs) + [any_spec] * n, out_shape=list(out_shape) + shapes,
        scratch_shapes=list(scratch) + sems, input_output_aliases=aliases,
        compiler_params=pltpu.CompilerParams(dimension_semantics=("arbitrary",) * len(grid),
                                             vmem_limit_bytes=VMEM_LIMIT, has_side_effects=True),
    )(*args, *comm[1])


def _fused_matmul(name, M, N, K, pairs, extras, epilogue, out_dtypes, n_acc, tm, tn, tk, outer="i", comm=None,
                  stack=False, vecs=(), row_sums=0, wide=None, sub=None):
    nk = K // tk
    n_pairs, n_ex, n_out = len(pairs), len(extras), len(out_dtypes)
    assert not row_sums or (outer == "i" and N == tn)

    def ij(g0, g1):
        return (g0, g1) if outer == "i" else (g1, g0)

    in_specs, args = [], []
    for p in pairs:
        ao, bk, bn = p.get("a_off", 0), p.get("bk_off", 0), p.get("bn_off", 0)
        mode = dict(pipeline_mode=pl.Buffered(1)) if p.get("resident") else {}
        in_specs.append(pl.BlockSpec((tm, tk), lambda g0, g1, k, ao=ao: (ij(g0, g1)[0], k + ao)))
        if "b_shift" in p:
            first, shift = p["b_shift"]
            if p.get("trans_b"):
                in_specs.append(pl.BlockSpec(
                    (pl.Element(tn), pl.Element(tk)),
                    lambda g0, g1, k, bk=bk: (
                        pl.multiple_of(ij(g0, g1)[1] * tn + jnp.where(ij(g0, g1)[1] >= first, shift, 0), 16),
                        (k + bk) * tk)))
            else:
                in_specs.append(pl.BlockSpec(
                    (pl.Element(tk), pl.Element(tn)),
                    lambda g0, g1, k, bn=bn: (pl.multiple_of(k * tk + jnp.where(k >= first, shift, 0), 16),
                                              (ij(g0, g1)[1] + bn) * tn)))
        elif p.get("trans_b"):
            in_specs.append(pl.BlockSpec((tn, tk), lambda g0, g1, k, bk=bk, bn=bn: (ij(g0, g1)[1] + bn, k + bk), **mode))
        else:
            in_specs.append(pl.BlockSpec((tk, tn), lambda g0, g1, k, bk=bk, bn=bn: (k + bk, ij(g0, g1)[1] + bn), **mode))
        args += [p["a"], p["b"]]
    for arr, off in extras:
        in_specs.append(pl.BlockSpec((tm, tn), lambda g0, g1, k, off=off: (ij(g0, g1)[0], ij(g0, g1)[1] + off)))
        args.append(arr)
    for arr in vecs:
        in_specs.append(pl.BlockSpec((1, tn), lambda g0, g1, k: (0, ij(g0, g1)[1])))
        args.append(arr)
    if stack:
        assert N == tn
        out_specs = [pl.BlockSpec((tm, n_out * tn), lambda g0, g1, k: (ij(g0, g1)[0], 0))]
        out_shape = [jax.ShapeDtypeStruct((M, n_out * N), out_dtypes[0])]
    else:
        out_specs = [pl.BlockSpec((tm, tn), lambda g0, g1, k: ij(g0, g1)) for _ in out_dtypes]
        out_shape = [jax.ShapeDtypeStruct((M, N), dt) for dt in out_dtypes]
    if wide:
        out_specs.append(pl.BlockSpec((pl.Element(tm), pl.Element(wide["width"])),
                                      lambda g0, g1, k: (pl.multiple_of(ij(g0, g1)[0] * tm, 16), wide["col"])))
        out_shape.append(jax.ShapeDtypeStruct((M, wide["total"]), wide["dtype"]))
    n_tile_out = len(out_specs)
    out_specs += [pl.BlockSpec((1, tn), lambda g0, g1, k: (0, 0)) for _ in range(row_sums)]
    out_shape += [jax.ShapeDtypeStruct((1, N), F32) for _ in range(row_sums)]
    grid = (M // tm, N // tn, nk) if outer == "i" else (N // tn, M // tm, nk)
    n_in = 2 * n_pairs + n_ex + len(vecs)

    def partials(refs, cs=slice(None)):
        accs = [None] * n_acc
        for idx, p in enumerate(pairs):
            b_ref = refs[2 * idx + 1]
            d = (_dot(refs[2 * idx][...], b_ref[cs, :], NT) if p.get("trans_b")
                 else _dot(refs[2 * idx][...], b_ref[:, cs], NN))
            accs[p["acc"]] = d if accs[p["acc"]] is None else accs[p["acc"]] + d
        return accs

    def finish(accs, refs, first_rows, cs=slice(None)):
        res = epilogue(accs, [r[:, cs] for r in refs[2 * n_pairs:n_in]])
        if stack:
            o = refs[n_in]
            for idx in range(n_out):
                lo = idx * tn + (cs.start or 0)
                o[:, lo:lo + (tn if cs.stop is None else cs.stop - cs.start)] = res[idx].astype(o.dtype)
        else:
            for o, r in zip(refs[n_in:n_in + n_out], res):
                o[:, cs] = r.astype(o.dtype)
        if wide:
            o = refs[n_in + n_tile_out - 1]
            o[...] = res[n_out].astype(o.dtype)
        for o, r in zip(refs[n_in + n_tile_out:n_in + n_tile_out + row_sums], res[n_out + bool(wide):]):
            @pl.when(first_rows)
            def _(o=o, r=r):
                o[...] = r

            @pl.when(jnp.logical_not(first_rows))
            def _(o=o, r=r):
                o[...] += r

    if nk == 1 and sub:
        assert not wide and not row_sums and tn % sub == 0

        def body(*refs):
            for c in range(tn // sub):
                cs = slice(c * sub, (c + 1) * sub)
                finish(partials(refs, cs), refs, None, cs)
        scratch = []
    elif nk == 1:
        def body(*refs):
            finish(partials(refs), refs, pl.program_id(0) == 0)
        scratch = []
    else:
        def body(*refs):
            acc_refs = refs[-n_acc:]
            k = pl.program_id(2)
            first_rows = pl.program_id(0) == 0
            new = partials(refs)

            @pl.when(k == 0)
            def _():
                for a, v in zip(acc_refs, new):
                    a[...] = v

            @pl.when(k > 0)
            def _():
                for a, v in zip(acc_refs, new):
                    a[...] += v

            @pl.when(k == nk - 1)
            def _():
                finish([a[...] for a in acc_refs], refs, first_rows)
        scratch = [pltpu.VMEM((tm, tn), F32) for _ in range(n_acc)]

    return _call(body, name=name, grid=grid, in_specs=in_specs, out_specs=out_specs, out_shape=out_shape,
                 scratch=scratch, sem=("parallel", "parallel", "arbitrary"), args=args, comm=comm)


def _matmul_tn(name, x, y, t1, t2, tr, scale=1.0, comm=None, out_dtype=BF16, out_skip=None):
    R, K1 = x.shape
    N1 = y.shape[1]
    nr, n1 = R // tr, K1 // t1
    x_spec = pl.BlockSpec((tr, t1), lambda i, j, r: (r, i))
    rows_out = K1
    o_spec = pl.BlockSpec((t1, t2), lambda i, j, r: (i, j))
    if out_skip:
        row, count = out_skip
        rows_out += count
        o_spec = pl.BlockSpec(
            (pl.Element(t1), pl.Element(t2)),
            lambda i, j, r: (pl.multiple_of(i * t1 + jnp.where(i * t1 >= row, count, 0), 16), j * t2))

    def body(x_ref, y_ref, o_ref, *acc):
        d = _dot(x_ref[...], y_ref[...], TN)
        if nr == 1:
            o_ref[...] = (d * scale).astype(o_ref.dtype)
            return
        r = pl.program_id(2)

        @pl.when(r == 0)
        def _():
            acc[0][...] = d

        @pl.when(jnp.logical_and(r > 0, r < nr - 1))
        def _():
            acc[0][...] += d

        @pl.when(r == nr - 1)
        def _():
            o_ref[...] = ((acc[0][...] + d) * scale).astype(o_ref.dtype)

    return _call(
        body, name=name, grid=(n1, N1 // t2, nr),
        in_specs=[x_spec, pl.BlockSpec((tr, t2), lambda i, j, r: (r, j))], out_specs=[o_spec],
        out_shape=[jax.ShapeDtypeStruct((rows_out, N1), out_dtype)],
        scratch=[pltpu.VMEM((t1, t2), F32)] if nr > 1 else [],
        sem=("parallel", "parallel", "arbitrary"), args=(x, y), comm=comm)


def _embed_norm(x, w, comm=None):
    Bl, S, D = x.shape
    nb = (PAD + N_META + S) // Q
    M = Bl * nb * Q

    def body(x_ref, w_ref, h_ref, n_ref):
        h = x_ref[0]
        h_ref[...] = h
        n_ref[...] = _rmsnorm_tile(h, w_ref[...]).astype(n_ref.dtype)

    row = pl.BlockSpec((Q, D), lambda b, t: (b * nb + t + 1, 0))
    return _call(
        body, name="embed_norm", grid=(Bl, nb - 1),
        in_specs=[pl.BlockSpec((1, Q, D), lambda b, t: (b, t, 0)), pl.BlockSpec((1, D), lambda b, t: (0, 0))],
        out_specs=[row, row], out_shape=[jax.ShapeDtypeStruct((M, D), F32), jax.ShapeDtypeStruct((M, D), BF16)],
        scratch=[], sem=("parallel", "parallel"), args=(x, w), comm=comm)


def _embed_meta(meta, w, h0, n0, Bl):
    nb = h0.shape[0] // (Bl * Q)
    D = h0.shape[1]

    def body(meta_ref, w_ref, h_ref, n_ref):
        h = jnp.concatenate([jnp.zeros((PAD, D), F32), meta_ref[...]], axis=0)
        h_ref[...] = h
        n_ref[...] = _rmsnorm_tile(h, w_ref[...]).astype(n_ref.dtype)

    row = pl.BlockSpec((Q, D), lambda b: (b * nb, 0))
    return _call(
        body, name="embed_meta", grid=(Bl,),
        in_specs=[pl.BlockSpec((N_META, D), lambda b: (0, 0)), pl.BlockSpec((1, D), lambda b: (0, 0))],
        out_specs=[row, row], out_shape=[jax.ShapeDtypeStruct(h0.shape, F32), jax.ShapeDtypeStruct(n0.shape, BF16)],
        scratch=[], sem=("parallel",), args=(meta, w), into=[(h0, 0), (n0, 1)])


def _rmsnorm_bwd_tile(dn, h, w, dh_in):
    r = lax.rsqrt(jnp.mean(h * h, axis=-1, keepdims=True) + EPS)
    xhat = h * r
    gw = dn * w
    dh = dh_in + r * (gw - xhat * jnp.mean(gw * xhat, axis=-1, keepdims=True))
    return dh, jnp.sum(dn * xhat, axis=0, keepdims=True)


def _loss_head(h, w, target, Bl, nb):
    M, D = h.shape
    nt = 4 if (nb * Q) % 32 == 0 and nb * Q // 4 >= Q else nb
    half = nb * Q // nt

    def body(h_ref, w_ref, t_ref, dh_ref, dhb_ref, dw_ref, loss_ref):
        b, t = pl.program_id(0), pl.program_id(1)
        row = lax.broadcasted_iota(jnp.int32, (half, 1), 0)
        live = jnp.logical_or(t > 0, row >= Q).astype(F32)
        x = h_ref[...]
        r = lax.rsqrt(jnp.mean(x * x, axis=-1, keepdims=True) + EPS)
        xhat = x * r
        wv = w_ref[...]
        tgt = t_ref[0]
        tgt = jnp.where(t == 0, pltpu.roll(tgt, Q, 0), tgt)
        err = (xhat * wv - tgt) * live
        dy = err * (1.0 / D)
        gw = dy * wv
        dx = r * (gw - xhat * jnp.mean(gw * xhat, axis=-1, keepdims=True))
        dh_ref[...] = dx
        dhb_ref[...] = dx.astype(BF16)
        dw = jnp.sum(dy * xhat, axis=0, keepdims=True)
        part = 0.5 * jnp.sum(jnp.sum(err * err, axis=-1, keepdims=True) * (1.0 / D), axis=0, keepdims=True)
        first = jnp.logical_and(b == 0, t == 0)

        @pl.when(first)
        def _():
            dw_ref[...] = dw
            loss_ref[...] = jnp.broadcast_to(part, loss_ref.shape)

        @pl.when(jnp.logical_not(first))
        def _():
            dw_ref[...] += dw
            loss_ref[...] += jnp.broadcast_to(part, loss_ref.shape)

    row = pl.BlockSpec((half, D), lambda b, t: (b * nt + t, 0))
    vec = pl.BlockSpec((1, D), lambda b, t: (0, 0))
    return pl.pallas_call(
        body, name="loss_head", grid=(Bl, nt),
        in_specs=[row, vec, pl.BlockSpec((pl.Element(1), pl.Element(half), pl.Element(D)),
                                         lambda b, t: (b, pl.multiple_of(jnp.maximum(t * half - Q, 0), 8), 0))],
        out_specs=[row, row, vec, pl.BlockSpec((8, 128), lambda b, t: (0, 0))],
        out_shape=[jax.ShapeDtypeStruct((M, D), F32), jax.ShapeDtypeStruct((M, D), BF16),
                   jax.ShapeDtypeStruct((1, D), F32), jax.ShapeDtypeStruct((8, 128), F32)],
        compiler_params=_params(("arbitrary", "arbitrary")),
    )(h, w, target)


CONV_TC = 256


def _conv_pre(xr_ref, w_ref, b_ref):
    x = xr_ref[...].astype(F32)
    acc = b_ref[...] + w_ref[SSD_CONV - 1:SSD_CONV, :] * x
    for k in range(1, SSD_CONV):
        acc = acc + w_ref[SSD_CONV - 1 - k:SSD_CONV - k, :] * pltpu.roll(x, k, 0)
    return x, acc


def _conv_fwd(proj, w, b, Bl, T):
    M = proj.shape[0]
    off = 1024 // CONV_TC

    def body(xr_ref, w_ref, b_ref, o_ref):
        _, acc = _conv_pre(xr_ref, w_ref, b_ref)
        row = lax.broadcasted_iota(jnp.int32, acc.shape, 0)
        o_ref[...] = jnp.where(row >= PAD, acc * _sigmoid(acc), 0.0).astype(o_ref.dtype)

    return pl.pallas_call(
        body, name="conv_fwd", grid=(Bl, SSD_CONV_CH // CONV_TC),
        in_specs=[pl.BlockSpec((T, CONV_TC), lambda bb, j: (bb, j + off)),
                  pl.BlockSpec((SSD_CONV, CONV_TC), lambda bb, j: (0, j)), pl.BlockSpec((1, CONV_TC), lambda bb, j: (0, j))],
        out_specs=pl.BlockSpec((T, CONV_TC), lambda bb, j: (bb, j)),
        out_shape=jax.ShapeDtypeStruct((M, SSD_CONV_CH), BF16), compiler_params=_params(("parallel", "parallel")),
    )(proj, w, b)


def _conv_bwd(proj, w, b, dxc, dproj, Bl, T):
    M = proj.shape[0]
    off = 1024 // CONV_TC

    def body(xr_ref, w_ref, b_ref, d_ref, dx_ref, dw_ref, db_ref):
        x, acc = _conv_pre(xr_ref, w_ref, b_ref)
        row = lax.broadcasted_iota(jnp.int32, acc.shape, 0)
        s = _sigmoid(acc)
        dpre = jnp.where(row >= PAD, d_ref[...].astype(F32) * _dsilu(acc, s), 0.0)
        dx = w_ref[SSD_CONV - 1:SSD_CONV, :] * dpre
        dws = [jnp.sum(dpre * x, axis=0, keepdims=True)]
        for k in range(1, SSD_CONV):
            dx = dx + w_ref[SSD_CONV - 1 - k:SSD_CONV - k, :] * pltpu.roll(dpre, T - k, 0)
            dws.append(jnp.sum(dpre * pltpu.roll(x, k, 0), axis=0, keepdims=True))
        dx_ref[...] = dx.astype(dx_ref.dtype)
        dw = jnp.concatenate(dws[::-1], axis=0)
        db = jnp.sum(dpre, axis=0, keepdims=True)

        @pl.when(pl.program_id(1) == 0)
        def _():
            dw_ref[...] = dw
            db_ref[...] = db

        @pl.when(pl.program_id(1) > 0)
        def _():
            dw_ref[...] += dw
            db_ref[...] += db

    return _call(
        body, name="conv_bwd", grid=(SSD_CONV_CH // CONV_TC, Bl),
        in_specs=[pl.BlockSpec((T, CONV_TC), lambda j, bb: (bb, j + off)),
                  pl.BlockSpec((SSD_CONV, CONV_TC), lambda j, bb: (0, j)), pl.BlockSpec((1, CONV_TC), lambda j, bb: (0, j)),
                  pl.BlockSpec((T, CONV_TC), lambda j, bb: (bb, j))],
        out_specs=[pl.BlockSpec((T, CONV_TC), lambda j, bb: (bb, j + off)),
                   pl.BlockSpec((SSD_CONV, CONV_TC), lambda j, bb: (0, j)), pl.BlockSpec((1, CONV_TC), lambda j, bb: (0, j))],
        out_shape=[jax.ShapeDtypeStruct(dproj.shape, BF16), jax.ShapeDtypeStruct((SSD_CONV, SSD_CONV_CH), F32),
                   jax.ShapeDtypeStruct((1, SSD_CONV_CH), F32)],
        scratch=[], sem=("parallel", "arbitrary"), args=(proj, w, b, dxc), into=(dproj, 0))


N_PAIR = SSD_HEADS // 2
HPG = SSD_HEADS // SSD_GROUPS
GW = SSD_INNER // SSD_GROUPS


def _per_group(fn, *arrs):
    return jnp.concatenate([jnp.broadcast_to(fn(*(a[:, GW * g:GW * (g + 1)] for a in arrs)), (arrs[0].shape[0], GW))
                            for g in range(SSD_GROUPS)], axis=1)


def _ssd_prep(c, dtr_ref, bias_ref, alog_ref, d_ref):
    row = lax.broadcasted_iota(jnp.int32, (Q, 128), 0)
    col = lax.broadcasted_iota(jnp.int32, (Q, 128), 1)
    live = col < SSD_HEADS
    valid = jnp.logical_and(jnp.logical_or(c > 0, row >= PAD), live)
    pre = dtr_ref[...] + bias_ref[...]
    dt = jnp.where(valid, _softplus(pre), 0.0)
    A = jnp.where(live[0:1], -jnp.exp(alog_ref[...]), 0.0)
    tri = row >= col
    eye = (row == col).astype(BF16)
    cs = _dot01(tri, dt * A, NN, "a")
    cst = _dot01(eye, cs, NT, "a")
    spread = (lax.broadcasted_iota(jnp.int32, (128, SSD_INNER), 0)
              == lax.broadcasted_iota(jnp.int32, (128, SSD_INNER), 1) // SSD_HEAD_DIM).astype(BF16)
    dt_w = _dot01(dt, spread, NN, "b")
    cs_w = _dot01(cs, spread, NN, "b")
    d_w = _dot01(jnp.broadcast_to(d_ref[...], (8, 128)), spread, NN, "b")[0:1]
    lane = lax.broadcasted_iota(jnp.int32, (Q, SSD_INNER), 1)
    first = (lane % 128) < SSD_HEAD_DIM
    return dict(row=row, col=col, valid=valid, pre=pre, dt=dt, A=A, tri=tri, eye=eye, cs=cs, cst=cst, spread=spread,
                dt_w=dt_w, cs_w=cs_w, d_w=d_w, ecs_w=jnp.exp(cs_w), decay_w=jnp.exp(cs_w[Q - 1:Q] - cs_w), first=first)


def _ssd_chunk(xc_ref, s, states):
    xv = xc_ref[:, 0:SSD_INNER].astype(F32)
    Bs = [xc_ref[:, SSD_INNER + 128 * g:SSD_INNER + 128 * (g + 1)] for g in range(SSD_GROUPS)]
    Cs = [xc_ref[:, SSD_INNER + 512 + 128 * g:SSD_INNER + 512 + 128 * (g + 1)] for g in range(SSD_GROUPS)]
    X = xv * s["dt_w"]
    X0 = jnp.where(s["first"], X, 0.0)
    Xb = (X0.astype(BF16), (X - X0).astype(BF16))
    Xd = (X * s["decay_w"]).astype(BF16)
    CB = [_dot(Cs[g], Bs[g], NT) for g in range(SSD_GROUPS)]
    Lms = [jnp.exp(jnp.where(s["tri"], s["cs"][:, h:h + 1] - s["cst"][h:h + 1, :], -jnp.inf)) for h in range(SSD_HEADS)]
    Ms = [CB[h // HPG] * Lms[h] for h in range(SSD_HEADS)]
    Mb = [m.astype(BF16) for m in Ms]
    prev_b = [st.astype(BF16) for st in states]
    yds, yos, sts = [], [], []
    for p in range(N_PAIR):
        g, ln = p // 2, slice(128 * p, 128 * (p + 1))
        yds.append(_dot(Mb[2 * p], Xb[0][:, ln], NN) + _dot(Mb[2 * p + 1], Xb[1][:, ln], NN))
        yos.append(_dot(Cs[g], prev_b[p], NT))
        sts.append(_dot(Xd[:, ln], Bs[g], TN))
    yo = jnp.concatenate(yos, axis=1)
    y = jnp.concatenate(yds, axis=1) + yo * s["ecs_w"] + xv * s["d_w"]
    upper = s["row"] < SSD_HEAD_DIM
    cl = s["cs"][Q - 1:Q, :]
    ecl_rows = [jnp.where(upper, jnp.exp(cl[:, 2 * p:2 * p + 1]), jnp.exp(cl[:, 2 * p + 1:2 * p + 2])) for p in range(N_PAIR)]
    new_states = [states[p] * ecl_rows[p] + sts[p] for p in range(N_PAIR)]
    return y, new_states, dict(xv=xv, Bs=Bs, Cs=Cs, X=X, Xb=Xb, CB=CB, Lms=Lms, Ms=Ms, Mb=Mb, prev_b=prev_b, yo=yo,
                               ecl_rows=ecl_rows)


def _ssd_in_specs(nc, rev=False):
    rb = (lambda b, c: b * nc + nc - 1 - c) if rev else (lambda b, c: b * nc + c)
    vec = pl.BlockSpec((1, 128), lambda b, c: (0, 0))
    return [pl.BlockSpec((Q, SSD_CONV_CH), lambda b, c: (rb(b, c), 0)),
            pl.BlockSpec((Q, 128), lambda b, c: (rb(b, c), 0)),
            pl.BlockSpec((Q, SSD_INNER), lambda b, c: (rb(b, c), 0)),
            vec, vec, vec, pl.BlockSpec((1, SSD_INNER), lambda b, c: (0, 0))]


def _ssd_fwd(xc, dtr, proj, bias_p, alog_p, d_p, nw, Bl, nc):
    M = xc.shape[0]

    def body(xc_ref, dtr_ref, z_ref, bias_ref, alog_ref, d_ref, nw_ref, y_ref, prev_ref, state):
        c = pl.program_id(1)

        @pl.when(c == 0)
        def _():
            state[...] = jnp.zeros_like(state)

        s = _ssd_prep(c, dtr_ref, bias_ref, alog_ref, d_ref)
        states = [state[p] for p in range(N_PAIR)]
        y, new_states, _ = _ssd_chunk(xc_ref, s, states)
        for p in range(N_PAIR):
            prev_ref[0, 0, p] = states[p]
            state[p] = new_states[p]
        zz = z_ref[...].astype(F32)
        yg = y * zz * _sigmoid(zz)
        r = _per_group(lambda a: lax.rsqrt(jnp.mean(a * a, axis=-1, keepdims=True) + EPS), yg)
        y_ref[...] = (yg * r * nw_ref[...]).astype(y_ref.dtype)

    return pl.pallas_call(
        body, name="ssd_fwd", grid=(Bl, nc), in_specs=_ssd_in_specs(nc),
        out_specs=[pl.BlockSpec((Q, SSD_INNER), lambda b, c: (b * nc + c, 0)),
                   pl.BlockSpec((1, 1, N_PAIR, 128, 128), lambda b, c: (b, c, 0, 0, 0))],
        out_shape=[jax.ShapeDtypeStruct((M, SSD_INNER), BF16), jax.ShapeDtypeStruct((Bl, nc, N_PAIR, 128, 128), F32)],
        scratch_shapes=[pltpu.VMEM((N_PAIR, 128, 128), F32)],
        compiler_params=_params(("arbitrary", "arbitrary")),
    )(xc, dtr, proj, bias_p, alog_p, d_p, nw)


def _ssd_bwd(xc, dtr, proj, bias_p, alog_p, d_p, nw, prev, dya, dproj, Bl, nc, comm=None):
    M = xc.shape[0]

    def body(xc_ref, dtr_ref, z_ref, bias_ref, alog_ref, d_ref, nw_ref, prev_ref, dy_ref,
             dxc_ref, dz_ref, ddtr_ref, dbias_ref, dalog_ref, dd_ref, dnw_ref, dS):
        b, t = pl.program_id(0), pl.program_id(1)

        @pl.when(t == 0)
        def _():
            dS[...] = jnp.zeros_like(dS)

        s = _ssd_prep(nc - 1 - t, dtr_ref, bias_ref, alog_ref, d_ref)
        states = [prev_ref[0, 0, p] for p in range(N_PAIR)]
        y, _, k = _ssd_chunk(xc_ref, s, states)
        xv, Bs, Cs, Xb = k["xv"], k["Bs"], k["Cs"], k["Xb"]

        zz = z_ref[...].astype(F32)
        sz = _sigmoid(zz)
        silu_z = zz * sz
        yg = y * silu_z
        r = _per_group(lambda a: lax.rsqrt(jnp.mean(a * a, axis=-1, keepdims=True) + EPS), yg)
        xhat = yg * r
        dout = dy_ref[...].astype(F32)
        gw = dout * nw_ref[...]
        dyg = r * (gw - xhat * _per_group(lambda a, c2: jnp.mean(a * c2, axis=-1, keepdims=True), gw, xhat))
        dnw = jnp.sum(dout * xhat, axis=0, keepdims=True)
        dz_ref[...] = (dyg * y * _dsilu(zz, sz)).astype(dz_ref.dtype)
        dy = dyg * silu_z
        dy0 = jnp.where(s["first"], dy, 0.0)
        dyb = (dy0.astype(BF16), (dy - dy0).astype(BF16))
        dYo = (dy * s["ecs_w"]).astype(BF16)

        dS_f = [dS[p] for p in range(N_PAIR)]
        dS_b = [d.astype(BF16) for d in dS_f]
        BdS, dXm, dprev, dCs, dMs, XdS = [], [], [], [[] for _ in range(SSD_GROUPS)], [], []
        for p in range(N_PAIR):
            g, ln = p // 2, slice(128 * p, 128 * (p + 1))
            BdS.append(_dot(Bs[g], dS_b[p], NT))
            dXm.append(_dot(k["Mb"][2 * p], dyb[0][:, ln], TN) + _dot(k["Mb"][2 * p + 1], dyb[1][:, ln], TN))
            dprev.append(_dot(dYo[:, ln], Cs[g], TN))
            dCs[g].append(_dot(dYo[:, ln], k["prev_b"][p], NN))
            for hh in range(2):
                dMs.append(_dot(dyb[hh][:, ln], Xb[hh][:, ln], NT))
                XdS.append(_dot(Xb[hh][:, ln], dS_b[p], NN))
        dX = jnp.concatenate(dXm, axis=1) + s["decay_w"] * jnp.concatenate(BdS, axis=1)
        dxs = dy * s["d_w"] + dX * s["dt_w"]

        sums = _dot01(jnp.concatenate([dX * xv, dy * k["yo"] * s["ecs_w"], dy * xv], axis=0), s["spread"], NT, "b")
        ddt, dcs = sums[0:Q], sums[Q:2 * Q]
        dD = jnp.sum(sums[2 * Q:3 * Q], axis=0, keepdims=True)

        col, row = s["col"], s["row"]
        lane1 = col[0:1]
        rowsT = lax.broadcasted_iota(jnp.int32, (128, Q), 0)
        dcs_t = jnp.zeros((128, Q), F32)
        dcl = jnp.zeros((1, 128), F32)
        dB_out, dC_out = [], []
        for g in range(SSD_GROUPS):
            Bf = Bs[g].astype(F32)
            dCB = jnp.zeros((Q, Q), F32)
            dBacc = jnp.zeros((Q, 128), F32)
            for r4 in range(HPG):
                h = HPG * g + r4
                p, hh = h // 2, h % 2
                W = dMs[h] * k["Ms"][h]
                dCB = dCB + dMs[h] * k["Lms"][h]
                decay_h = s["decay_w"][:, SSD_HEAD_DIM * h:SSD_HEAD_DIM * h + 1]
                dBacc = dBacc + decay_h * XdS[h]
                tdec = jnp.sum(XdS[h] * Bf, axis=1, keepdims=True) * decay_h
                dcs = dcs + jnp.where(col == h, jnp.sum(W, axis=1, keepdims=True) - tdec, 0.0)
                dcs_t = dcs_t - jnp.where(rowsT == h, jnp.sum(W, axis=0, keepdims=True), 0.0)
                rows_h = (row < SSD_HEAD_DIM) if hh == 0 else (row >= SSD_HEAD_DIM)
                sprev = jnp.sum(jnp.sum(jnp.where(rows_h, dS_f[p] * states[p], 0.0), axis=1, keepdims=True),
                                axis=0, keepdims=True)
                ecl = jnp.exp(s["cs"][Q - 1:Q, h:h + 1])
                dcl = dcl + jnp.where(lane1 == h, jnp.sum(tdec, axis=0, keepdims=True) + ecl * sprev, 0.0)
            dCB_b = dCB.astype(BF16)
            dC_out.append(dCs[g][0] + dCs[g][1] + _dot(dCB_b, Bs[g], NN))
            dB_out.append(dBacc + _dot(dCB_b, Cs[g], TN))
        for p in range(N_PAIR):
            dS[p] = dS_f[p] * k["ecl_rows"][p] + dprev[p]
        dxc_ref[...] = jnp.concatenate([dxs] + dB_out + dC_out, axis=1).astype(dxc_ref.dtype)

        dcs = dcs + _dot01(s["eye"], dcs_t, NT, "a") + jnp.where(row == Q - 1, dcl, 0.0)
        da = _dot01(row <= col, dcs, NN, "a")
        ddt = ddt + da * s["A"]
        dpre = jnp.where(s["valid"], ddt * _sigmoid(s["pre"]), 0.0)
        ddtr_ref[...] = dpre
        dbias = jnp.sum(dpre, axis=0, keepdims=True)
        dalog = jnp.sum(da * s["dt"], axis=0, keepdims=True) * s["A"]
        first_step = jnp.logical_and(b == 0, t == 0)

        @pl.when(first_step)
        def _():
            dbias_ref[...] = dbias
            dalog_ref[...] = dalog
            dd_ref[...] = dD
            dnw_ref[...] = dnw

        @pl.when(jnp.logical_not(first_step))
        def _():
            dbias_ref[...] += dbias
            dalog_ref[...] += dalog
            dd_ref[...] += dD
            dnw_ref[...] += dnw

    rb = lambda b, c: b * nc + nc - 1 - c
    rowblk = lambda w: pl.BlockSpec((Q, w), lambda b, c: (rb(b, c), 0))
    vec = lambda w: pl.BlockSpec((1, w), lambda b, c: (0, 0))
    return _call(
        body, name="ssd_bwd", grid=(Bl, nc),
        in_specs=_ssd_in_specs(nc, rev=True) + [
            pl.BlockSpec((1, 1, N_PAIR, 128, 128), lambda b, c: (b, nc - 1 - c, 0, 0, 0)), rowblk(SSD_INNER)],
        out_specs=[rowblk(SSD_CONV_CH), rowblk(SSD_INNER), rowblk(128), vec(128), vec(128), vec(128), vec(SSD_INNER)],
        out_shape=[jax.ShapeDtypeStruct((M, SSD_CONV_CH), BF16), jax.ShapeDtypeStruct(dproj.shape, BF16),
                   jax.ShapeDtypeStruct((M, 128), F32), jax.ShapeDtypeStruct((1, 128), F32),
                   jax.ShapeDtypeStruct((1, 128), F32), jax.ShapeDtypeStruct((1, 128), F32),
                   jax.ShapeDtypeStruct((1, SSD_INNER), F32)],
        scratch=[pltpu.VMEM((N_PAIR, 128, 128), F32)], sem=("arbitrary", "arbitrary"),
        args=(xc, dtr, proj, bias_p, alog_p, d_p, nw, prev, dya), comm=comm, into=(dproj, 1))


NSUB = Q // HG_CHUNK
HG_HP = 8
EXP_CAP = 80.0


def _hg_setup(blk, q_ref, f_ref, hb_ref):
    row = lax.broadcasted_iota(jnp.int32, (Q, Q), 0)
    col = lax.broadcasted_iota(jnp.int32, (Q, Q), 1)
    same = (row // HG_CHUNK) == (col // HG_CHUNK)
    causal = jnp.logical_and(same, col <= row)
    lb = _sigmoid(hb_ref[0:1, :] - hb_ref[1:2, :])
    fl = f_ref[...].astype(F32)
    sg = _sigmoid(fl)
    fg = lb + (1.0 - lb) * sg
    k = (1.0 - lb) * (1.0 - sg)
    gl = jnp.log(fg)
    G = _dot01(causal, gl, NN, "a")
    T = _dot01(same, gl, NN, "a")
    qv = q_ref[...].astype(F32)
    sq = _sigmoid(qv)
    eG = jnp.exp(G)
    eGn = jnp.exp(jnp.minimum(-G, EXP_CAP))
    eTG = jnp.exp(T - G)
    qt = qv * sq * eG
    kt = k * eGn
    kh = k * eTG
    valid = jnp.logical_or(blk > 0, row[:, :1] >= PAD)
    return dict(row=row, col=col, same=same, causal=causal, lb=lb, sg=sg, fg=fg, k=k, T=T, qv=qv, sq=sq,
                eG=eG, eGn=eGn, eTG=eTG, qt=qt, kt=kt, kh=kh, valid=valid)


def _hg_specs(nb, rev=False):
    rb = (lambda h, b, t: b * nb + nb - 1 - t) if rev else (lambda h, b, t: b * nb + t)
    w = 128 * HG_HP
    blk = lambda off: pl.BlockSpec((Q, w), lambda h, b, t, off=off: (rb(h, b, t), off // HG_HP + h))
    return [blk(24), blk(32), blk(40), blk(48),
            pl.BlockSpec((2, w), lambda h, b, t: (0, h)), pl.BlockSpec((1, w), lambda h, b, t: (0, h))]


HEAD_LANES = tuple(slice(128 * hh, 128 * (hh + 1)) for hh in range(HG_HP))


def _per_head(fn, *arrs):
    return jnp.concatenate([jnp.broadcast_to(fn(*(a[:, ln] for a in arrs)), (arrs[0].shape[0], 128))
                            for ln in HEAD_LANES], axis=1)


def _hgrn_fwd(proj, hb, nw, Bl, nb, comm=None):
    M = proj.shape[0]

    def body(q_ref, f_ref, i_ref, g_ref, hb_ref, nw_ref, y_ref, o_ref, st_ref, S):
        blk = pl.program_id(2)

        @pl.when(blk == 0)
        def _():
            S[...] = jnp.zeros_like(S)

        s = _hg_setup(blk, q_ref, f_ref, hb_ref)
        v = i_ref[...]
        qt_b, kt_b, kh_b = s["qt"].astype(BF16), s["kt"].astype(BF16), s["kh"].astype(BF16)
        eT = jnp.exp(s["T"])
        att = [jnp.where(s["causal"], _dot(qt_b[:, ln], kt_b[:, ln], NT), 0.0).astype(BF16) for ln in HEAD_LANES]
        o_intra = [_dot(att[hh], v[:, ln], NN) for hh, ln in enumerate(HEAD_LANES)]
        for j in range(NSUB):
            sl = slice(HG_CHUNK * j, HG_CHUNK * (j + 1))
            for hh, ln in enumerate(HEAD_LANES):
                St = S[hh]
                st_ref[0, hh, 0, j] = St
                o_ref[sl, ln] = o_intra[hh][sl] + _dot(qt_b[sl, ln], St.astype(BF16), NT)
                S[hh] = St * eT[HG_CHUNK * j:HG_CHUNK * j + 1, ln] + _dot(v[sl, ln], kh_b[sl, ln], TN)
        o = o_ref[...]
        r = _per_head(lambda a: lax.rsqrt(jnp.mean(a * a, axis=-1, keepdims=True) + EPS), o)
        gv = g_ref[...].astype(F32)
        y_ref[...] = (o * r * nw_ref[...] * gv * _sigmoid(gv)).astype(y_ref.dtype)

    rowblk = pl.BlockSpec((Q, 128 * HG_HP), lambda h, b, t: (b * nb + t, h))
    return _call(
        body, name="hgrn_fwd", grid=(HG_HEADS // HG_HP, Bl, nb), in_specs=_hg_specs(nb),
        out_specs=[rowblk, rowblk,
                   pl.BlockSpec((1, HG_HP, 1, NSUB, 128, 128), lambda h, b, t: (b, h, t, 0, 0, 0))],
        out_shape=[jax.ShapeDtypeStruct((M, HG_WIDTH), BF16), jax.ShapeDtypeStruct((M, HG_WIDTH), F32),
                   jax.ShapeDtypeStruct((Bl, HG_HEADS, nb, NSUB, 128, 128), F32)],
        scratch=[pltpu.VMEM((HG_HP, 128, 128), F32)], sem=("parallel", "arbitrary", "arbitrary"),
        args=(proj, proj, proj, proj, hb, nw), comm=comm)


def _hgrn_bwd(proj, hb, nw, o_saved, st_saved, dyb, dproj, Bl, nb, comm=None):
    assert HG_HP == HG_HEADS

    def body(q_ref, f_ref, i_ref, g_ref, hb_ref, nw_ref, o_ref, st_ref, dy_ref,
             d_ref, dhb_ref, dnw_ref, dS, a_dqt, a_dv, a_dkh, a_dgl):
        b, t = pl.program_id(1), pl.program_id(2)

        @pl.when(t == 0)
        def _():
            dS[...] = jnp.zeros_like(dS)

        first_step = jnp.logical_and(b == 0, t == 0)
        s = _hg_setup(nb - 1 - t, q_ref, f_ref, hb_ref)
        v = i_ref[...]
        qt_b, kt_b, kh_b = s["qt"].astype(BF16), s["kt"].astype(BF16), s["kh"].astype(BF16)
        eT = jnp.exp(s["T"])
        att = [jnp.where(s["causal"], _dot(qt_b[:, ln], kt_b[:, ln], NT), 0.0).astype(BF16) for ln in HEAD_LANES]

        o = o_ref[...]
        r = _per_head(lambda a: lax.rsqrt(jnp.mean(a * a, axis=-1, keepdims=True) + EPS), o)
        xhat = o * r
        gv = g_ref[...].astype(F32)
        sgv = _sigmoid(gv)
        dyv = dy_ref[...].astype(F32)
        d_on = dyv * gv * sgv
        dg_out = dyv * xhat * nw_ref[...] * _dsilu(gv, sgv)
        gw = d_on * nw_ref[...]
        do = r * (gw - xhat * _per_head(lambda a, c: jnp.mean(a * c, axis=-1, keepdims=True), gw, xhat))
        dnw = jnp.sum(d_on * xhat, axis=0, keepdims=True)
        do_b = do.astype(BF16)

        datt = [jnp.where(s["causal"], _dot(do_b[:, ln], v[:, ln], NT), 0.0).astype(BF16) for ln in HEAD_LANES]
        dqt = jnp.concatenate([_dot(datt[hh], kt_b[:, ln], NN) for hh, ln in enumerate(HEAD_LANES)], axis=1)
        dkt = jnp.concatenate([_dot(datt[hh], qt_b[:, ln], TN) for hh, ln in enumerate(HEAD_LANES)], axis=1)
        dv = jnp.concatenate([_dot(att[hh], do_b[:, ln], TN) for hh, ln in enumerate(HEAD_LANES)], axis=1)
        last_row = (lax.broadcasted_iota(jnp.int32, (HG_CHUNK, 128), 0) == HG_CHUNK - 1)
        for j in reversed(range(NSUB)):
            sl = slice(HG_CHUNK * j, HG_CHUNK * (j + 1))
            for hh, ln in enumerate(HEAD_LANES):
                St = st_ref[0, hh, 0, j]
                dSt = dS[hh]
                St_b, dSt_b = St.astype(BF16), dSt.astype(BF16)
                eT_j = eT[HG_CHUNK * j:HG_CHUNK * j + 1, ln]
                dkh_j = _dot(v[sl, ln], dSt_b, NN)
                a_dqt[sl, ln] = _dot(do_b[sl, ln], St_b, NN)
                a_dv[sl, ln] = _dot(kh_b[sl, ln], dSt_b, NT)
                a_dkh[sl, ln] = dkh_j
                dlast = (jnp.sum(St * dSt, axis=0, keepdims=True) * eT_j
                         + jnp.sum(dkh_j * s["kh"][sl, ln], axis=0, keepdims=True))
                a_dgl[sl, ln] = jnp.where(last_row, dlast, 0.0)
                dS[hh] = dSt * eT_j + _dot(do_b[sl, ln], qt_b[sl, ln], TN)
        dqt = dqt + a_dqt[...]
        dv = dv + a_dv[...]
        dkh = a_dkh[...]
        dG = dqt * s["qt"] - dkt * s["kt"] - dkh * s["kh"] + a_dgl[...]
        rev_causal = jnp.logical_and(s["same"], s["col"] >= s["row"])
        dgl = _dot01(rev_causal, dG, NN, "a")
        dk = dkt * s["eGn"] + dkh * s["eTG"]
        dfg = dgl / s["fg"] - dk
        lb, sg = s["lb"], s["sg"]
        keep = s["valid"].astype(F32)
        d_ref[:, 0:w] = (dqt * s["eG"] * _dsilu(s["qv"], s["sq"]) * keep).astype(d_ref.dtype)
        d_ref[:, w:2 * w] = (dfg * (1.0 - lb) * sg * (1.0 - sg) * keep).astype(d_ref.dtype)
        d_ref[:, 2 * w:3 * w] = (dv * keep).astype(d_ref.dtype)
        d_ref[:, 3 * w:4 * w] = (dg_out * keep).astype(d_ref.dtype)
        dlb = jnp.sum(dfg * (1.0 - sg) * keep, axis=0, keepdims=True) * lb * (1.0 - lb)
        dhb = jnp.concatenate([dlb, -dlb], axis=0)

        @pl.when(first_step)
        def _():
            dhb_ref[...] = dhb
            dnw_ref[...] = dnw

        @pl.when(jnp.logical_not(first_step))
        def _():
            dhb_ref[...] += dhb
            dnw_ref[...] += dnw

    w = 128 * HG_HP
    rowblk = pl.BlockSpec((Q, w), lambda h, b, t: (b * nb + nb - 1 - t, h))
    return _call(
        body, name="hgrn_bwd", grid=(HG_HEADS // HG_HP, Bl, nb),
        in_specs=_hg_specs(nb, rev=True) + [
            rowblk, pl.BlockSpec((1, HG_HP, 1, NSUB, 128, 128), lambda h, b, t: (b, h, nb - 1 - t, 0, 0, 0)), rowblk],
        out_specs=[pl.BlockSpec((pl.Element(Q), pl.Element(4 * w)),
                                lambda h, b, t: (pl.multiple_of((b * nb + nb - 1 - t) * Q, Q), 3 * HG_WIDTH)),
                   pl.BlockSpec((2, w), lambda h, b, t: (0, h)), pl.BlockSpec((1, w), lambda h, b, t: (0, h))],
        out_shape=[jax.ShapeDtypeStruct(dproj.shape, BF16),
                   jax.ShapeDtypeStruct((2, HG_WIDTH), F32), jax.ShapeDtypeStruct((1, HG_WIDTH), F32)],
        scratch=[pltpu.VMEM((HG_HP, 128, 128), F32)] + [pltpu.VMEM((Q, w), F32)] * 4,
        sem=("parallel", "arbitrary", "arbitrary"),
        args=(proj, proj, proj, proj, hb, nw, o_saved, st_saved, dyb), comm=comm, into=(dproj, 0))


def _adamw(name, parts, w, m, v, comm=None):
    R, C = w.shape
    S = parts.shape[0]
    tr, tc = (_tile(R, (256, 176, 128, 64, 8)), C) if R % 8 == 0 else (R, 256)
    c1, c2 = 1.0 - ADAM_B1 ** ADAM_STEP, 1.0 - ADAM_B2 ** ADAM_STEP

    def body(p_ref, w_ref, m_ref, v_ref, g_ref, d_ref, nm_ref, nv_ref):
        g = p_ref[0].astype(F32)
        for s in range(1, S):
            g = g + p_ref[s].astype(F32)
        nm = ADAM_B1 * m_ref[...] + (1.0 - ADAM_B1) * g
        nv = ADAM_B2 * v_ref[...] + (1.0 - ADAM_B2) * (g * g)
        g_ref[...] = g
        nm_ref[...] = nm
        nv_ref[...] = nv
        d_ref[...] = -ADAM_LR * ((nm / c1) / (jnp.sqrt(nv / c2) + ADAM_EPS) + ADAM_WD * w_ref[...])

    blk = pl.BlockSpec((tr, tc), lambda i, j: (i, j))
    return _call(
        body, name=name, grid=(R // tr, C // tc),
        in_specs=[pl.BlockSpec((S, tr, tc), lambda i, j: (0, i, j)), blk, blk, blk], out_specs=[blk] * 4,
        out_shape=[jax.ShapeDtypeStruct((R, C), F32)] * 4, scratch=[], sem=("parallel", "parallel"),
        args=(parts, w, m, v), comm=comm)


def _pair_sum(name, by_core, arrived):
    _, J, R, C = by_core.shape
    tc = _tile(C, (512, 256, 128))

    def body(c_ref, a_ref, b_ref, o_ref):
        o_ref[...] = (a_ref[0].astype(F32) + b_ref[...].astype(F32)).astype(o_ref.dtype)

    blk = pl.BlockSpec((1, R, tc), lambda j, k, c_ref: (j, 0, k))
    return pl.pallas_call(
        body, name=name,
        grid_spec=pltpu.PrefetchScalarGridSpec(
            num_scalar_prefetch=1, grid=(J, C // tc),
            in_specs=[pl.BlockSpec((1, 1, R, tc), lambda j, k, c_ref: (c_ref[0], j, 0, k)), blk], out_specs=blk),
        out_shape=jax.ShapeDtypeStruct(arrived.shape, arrived.dtype), compiler_params=_params(("parallel", "parallel")),
    )(lax.axis_index("c").astype(jnp.int32).reshape(1), by_core, arrived)


def _sum_parts(name, parts):
    S, R, C = parts.shape

    def body(p_ref, o_ref):
        g = p_ref[0]
        for s in range(1, S):
            g = g + p_ref[s]
        o_ref[...] = g

    return pl.pallas_call(
        body, name=name, out_shape=jax.ShapeDtypeStruct((R, C), F32),
        in_specs=[pl.BlockSpec(memory_space=pltpu.VMEM)], out_specs=pl.BlockSpec(memory_space=pltpu.VMEM),
    )(parts)


def _heads_to_lanes(p):
    return jnp.pad(p, [(0, 0)] * (p.ndim - 1) + [(0, 128 - SSD_HEADS)])


def _lanes_to_heads(p):
    return p[..., :SSD_HEADS]


def _pack_rows(arrs):
    flat = jnp.concatenate([a.reshape(-1).astype(F32) for a in arrs])
    return jnp.pad(flat, (0, (-flat.shape[0]) % (8 * D_MODEL))).reshape(-1, D_MODEL)


def _unpack_rows(packed, like):
    flat, outs, at = packed.reshape(-1), [], 0
    for a in like:
        outs.append(flat[at:at + a.size].reshape(a.shape))
        at += a.size
    return outs


def _cols(gth):
    return jnp.transpose(gth, (1, 0, 2)).reshape(gth.shape[1], -1)


def _rows(gth):
    return gth.reshape(-1, gth.shape[2])


def _to_rows(g):
    return g.reshape(N_DEV, -1, g.shape[1]).astype(BF16)


def _by_core(g):
    return jnp.transpose(g.reshape(N_DEV // 2, 2, -1, g.shape[1]), (1, 0, 2, 3)).astype(BF16)


DT_ROW = 3072


def _chip_sums(tag, by_core, swap_in=None):
    arrived = swap_in(by_core) if swap_in else _exchange(tag + "_swap", "swap", by_core)
    return [_pair_sum(f"{tag}_chipsum{i}", m, a) for i, (m, a) in enumerate(zip(by_core, arrived))]


def _ffn_fwd_gu(tag, n, w_gu_t, comm=None):
    M = n.shape[0]
    F = w_gu_t.shape[0] // 2
    tm = _tile(M, (544, 256))
    outs = _fused_matmul(
        tag + "_gu", M, F, D_MODEL,
        [dict(a=n, b=w_gu_t, trans_b=True, acc=0, resident=True),
         dict(a=n, b=w_gu_t, trans_b=True, bn_off=1, acc=1, resident=True)], [],
        lambda accs, ex: (accs[0], accs[1], accs[0] * _sigmoid(accs[0]) * accs[1]),
        [BF16, BF16, BF16], 2, tm, F, D_MODEL, outer="i", comm=comm, sub=256)
    return (n, *outs[:3]), outs[3:]


def _rmsnorm_tile(x, w):
    return x * lax.rsqrt(jnp.mean(x * x, axis=-1, keepdims=True) + EPS) * w


def _ffn_fwd_down(tag, h, a, w_down, next_norm=None, comm=None):
    M = h.shape[0]
    F = w_down.shape[0]
    tm = _tile(M, (1088, 544, 256))
    if next_norm is None:
        (h_out,) = _fused_matmul(
            tag + "_down", M, D_MODEL, F, [dict(a=a, b=w_down, acc=0)], [(h, 0)],
            lambda accs, ex: (ex[0] + 0.5 * accs[0],), [F32], 1, tm, D_MODEL, F, outer="j", sub=256)
        return h_out

    def with_norm(accs, ex):
        h_new = ex[0] + 0.5 * accs[0]
        return h_new, _rmsnorm_tile(h_new, ex[1])

    return _fused_matmul(tag + "_down", M, D_MODEL, F, [dict(a=a, b=w_down, acc=0, resident=True)], [(h, 0)], with_norm,
                         [F32, BF16], 1, tm, D_MODEL, F, outer="j", vecs=[next_norm], comm=comm)


def _ffn_bwd(tag, dh, dh_b, h, norm_w, w_gu_t, w_down, saved, scatter=False):
    n, g, u, a = saved
    M = h.shape[0]
    F = w_down.shape[0]
    tm = _tile(M, (544, 256))
    tn = _tile(F, (1408, 704, 256))

    def swiglu_bwd(accs, ex):
        da, gv, uv = 0.5 * accs[0], ex[0].astype(F32), ex[1].astype(F32)
        s = _sigmoid(gv)
        return da * uv * _dsilu(gv, s), da * gv * s

    (dgu,) = _fused_matmul(
        tag + "_dact", M, F, D_MODEL, [dict(a=dh_b, b=w_down, trans_b=True, acc=0, resident=True)], [(g, 0), (u, 0)],
        swiglu_bwd, [BF16, BF16], 1, tm, F, D_MODEL, outer="i", stack=True, sub=256)
    tr = _tile(M, (2176, 256))
    (dw_down,) = _matmul_tn(tag + "_dwd", a, dh_b, tn, D_MODEL, tr, scale=0.5)
    dw_gu_t, *p_down = _matmul_tn(tag + "_dwgu", dgu, n, tn, D_MODEL, tr,
                                  comm=("scatter", [_to_rows(dw_down)]) if scatter else None)
    comm = None
    if scatter:
        comm = ("chips", _chip_sums(tag + "_wgu", [_by_core(dw_gu_t)]))
    def norm_bwd(accs, ex):
        dh_prev, dw = _rmsnorm_bwd_tile(accs[0], ex[0], ex[2], ex[1])
        return dh_prev, dh_prev, dw

    dh_prev, dh_prev_b, dnorm, *p_gu = _fused_matmul(
        tag + "_dn", M, D_MODEL, 2 * F,
        [dict(a=dgu, b=w_gu_t, acc=0, resident=True)], [(h, 0), (dh, 0)],
        norm_bwd, [F32, BF16], 1, tm, D_MODEL, 2 * F, outer="i", comm=comm, vecs=[norm_w], row_sums=1)
    return (dh_prev, dh_prev_b, dnorm, *((p_gu[0], p_down[0]) if scatter else (dw_gu_t, dw_down)))


def kernel(x, meta_tokens, ffn1_norm, ffn1_w_gu, ffn1_w_down, mix_norm, w_in, ssd_conv_w, ssd_conv_b, ssd_dt_bias, ssd_a_log, ssd_d, ssd_norm, hg_lower_bound, hg_norm, w_branch_a, w_branch_b, w_out, ffn2_norm, ffn2_w_gu, ffn2_w_down, final_norm, loss_target, m_meta_tokens, m_ffn1_norm, m_ffn1_w_gu, m_ffn1_w_down, m_mix_norm, m_w_in, m_ssd_conv_w, m_ssd_conv_b, m_ssd_dt_bias, m_ssd_a_log, m_ssd_d, m_ssd_norm, m_hg_lower_bound, m_hg_norm, m_w_branch_a, m_w_branch_b, m_w_out, m_ffn2_norm, m_ffn2_w_gu, m_ffn2_w_down, m_final_norm, v_meta_tokens, v_ffn1_norm, v_ffn1_w_gu, v_ffn1_w_down, v_mix_norm, v_w_in, v_ssd_conv_w, v_ssd_conv_b, v_ssd_dt_bias, v_ssd_a_log, v_ssd_d, v_ssd_norm, v_hg_lower_bound, v_hg_norm, v_w_branch_a, v_w_branch_b, v_w_out, v_ffn2_norm, v_ffn2_w_gu, v_ffn2_w_down, v_final_norm):
    Bl, S, D = x.shape
    T = PAD + N_META + S
    nc = T // Q
    M = Bl * T
    me = 4 * lax.axis_index("x") + 2 * lax.axis_index("y") + lax.axis_index("c")

    bf = lambda a: a[0].astype(BF16)
    bft = lambda a: a[0].T.astype(BF16)
    bias_p, alog_p, d_p = _heads_to_lanes(ssd_dt_bias), _heads_to_lanes(ssd_a_log), _heads_to_lanes(ssd_d)
    final_w = final_norm.reshape(1, D)

    h0, n1, g_wgu1, g_meta, g_conv_w = _embed_norm(
        x, ffn1_norm, comm=("gather", [bft(ffn1_w_gu), meta_tokens, ssd_conv_w[0]]))
    wgu1, meta_full, conv_w_full = _rows(g_wgu1), _cols(g_meta), _cols(g_conv_w)
    h0, n1 = _embed_meta(meta_full, ffn1_norm, h0, n1, Bl)
    tm = _tile(M, (1088, 544, 256))
    win_shard = bft(w_in)
    cut = (win_shard.shape[0] // 32) * 16
    ffn1_saved, (g_wd1, g_win_a) = _ffn_fwd_gu("ffn1", n1, wgu1, comm=("gather", [bf(ffn1_w_down), win_shard[:cut]]))
    wd1 = _rows(g_wd1)
    h1, un, g_win_b = _ffn_fwd_down("ffn1", h0, ffn1_saved[3], wd1, next_norm=mix_norm,
                                    comm=("gather", [win_shard[cut:]]))
    win_t = _rows(jnp.concatenate([g_win_a, g_win_b], axis=1))
    win_dt = jnp.pad(win_t[DT_ROW:DT_ROW + SSD_HEADS], ((0, 128 - SSD_HEADS), (0, 0)))
    plain = lambda accs, ex: (accs[0],)
    proj, g_wa, g_wb, g_wo = _fused_matmul(
        "in_proj", M, N_MAIN, D, [dict(a=un, b=win_t, trans_b=True, acc=0, b_shift=(DT_ROW // 3072, SSD_HEADS))], [],
        plain, [BF16], 1, tm, 3072, D,
        outer="j", comm=("gather", [bf(w_branch_a), bf(w_branch_b), bf(w_out)]), sub=512)
    wa, wb, wo = _rows(g_wa), _rows(g_wb), _rows(g_wo)
    (dtr,) = _fused_matmul("in_proj_dt", M, 128, D, [dict(a=un, b=win_dt, trans_b=True, acc=0)], [], plain, [F32], 1,
                           tm, 128, D, outer="j")
    xc = _conv_fwd(proj, conv_w_full, ssd_conv_b, Bl, T)
    ya, ssd_prev = _ssd_fwd(xc, dtr, proj, bias_p, alog_p, d_p, ssd_norm, Bl, nc)
    yb, hg_o, hg_st, g_wgu2, g_wd2 = _hgrn_fwd(proj, hg_lower_bound, hg_norm, Bl, nc,
                                               comm=("gather", [bft(ffn2_w_gu), bf(ffn2_w_down)]))
    wgu2, wd2 = _rows(g_wgu2), _rows(g_wd2)

    def branch_fwd(accs, ex):
        pa, pb = accs
        return pa, pb, _sigmoid(ex[0].astype(F32)) * pa + _sigmoid(ex[1].astype(F32)) * pb

    pa, pb, merged = _fused_matmul(
        "branches", M, D, D, [dict(a=ya, b=wa, acc=0), dict(a=yb, b=wb, acc=1)], [(proj, 7), (proj, 8)],
        branch_fwd, [BF16, BF16, BF16], 2, tm, D, D, outer="j")
    def out_with_norm(accs, ex):
        h_new = ex[0] + accs[0]
        return h_new, _rmsnorm_tile(h_new, ex[1])

    h2, n2 = _fused_matmul("out_proj", M, D, D, [dict(a=merged, b=wo, acc=0)], [(h1, 0)], out_with_norm,
                           [F32, BF16], 1, tm, D, D, outer="j", vecs=[ffn2_norm])
    ffn2_saved, _ = _ffn_fwd_gu("ffn2", n2, wgu2)
    h3 = _ffn_fwd_down("ffn2", h2, ffn2_saved[3], wd2)

    dh3, dh3_b, d_final, loss_part = _loss_head(h3, final_w, loss_target, Bl, nc)
    dh2, dh2_b, d_ffn2_norm, d_wgu2, d_wd2 = _ffn_bwd("ffn2", dh3, dh3_b, h2, ffn2_norm, wgu2, wd2, ffn2_saved)

    def branch_bwd(accs, ex):
        dm = accs[0]
        ga, gb, pav, pbv = (e.astype(F32) for e in ex)
        sa, sb = _sigmoid(ga), _sigmoid(gb)
        return (dm * sa, dm * sb,
                jnp.concatenate([dm * pav * sa * (1.0 - sa), dm * pbv * sb * (1.0 - sb)], axis=1))

    d_merged_outs = []

    def d_merged_with_swap(theirs):
        d_merged_outs.extend(_fused_matmul(
            "d_merged", M, D, D, [dict(a=dh2_b, b=wo, trans_b=True, acc=0)], [(proj, 7), (proj, 8), (pa, 0), (pb, 0)],
            branch_bwd, [BF16] * 2, 1, tm, D, D, outer="j", comm=("swap", theirs),
            wide=dict(width=2 * D, col=7 * D, total=N_MAIN, dtype=BF16)))
        return d_merged_outs[3:]

    s_ffn2 = _chip_sums("ffn2", [_by_core(d_wgu2), _by_core(d_wd2)], swap_in=d_merged_with_swap)
    dpa, dpb, dproj = d_merged_outs[:3]
    (d_wo,) = _matmul_tn("d_w_out", merged, dh2_b, 512, D, M)
    (d_wa,) = _matmul_tn("d_w_a", ya, dpa, 512, D, M)
    (d_wb,) = _matmul_tn("d_w_b", yb, dpb, 512, D, M)
    dya, dyb = _fused_matmul(
        "d_branches", M, D, D, [dict(a=dpa, b=wa, trans_b=True, acc=0), dict(a=dpb, b=wb, trans_b=True, acc=1)], [],
        lambda accs, ex: (accs[0], accs[1]), [BF16, BF16], 2, tm, D, D, outer="j")
    *ssd_grads, p_wgu2, p_wd2 = _ssd_bwd(xc, dtr, proj, bias_p, alog_p, d_p, ssd_norm, ssd_prev, dya, dproj, Bl, nc,
                                         comm=("chips", s_ffn2))
    dxc, dproj, ddtr, d_bias_p, d_alog_p, d_d_p, d_ssd_norm = ssd_grads
    dproj, d_conv_w, d_conv_b = _conv_bwd(proj, conv_w_full, ssd_conv_b, dxc, dproj, Bl, T)
    dproj, d_hb, d_hg_norm, p_wa, p_wb, p_wo = _hgrn_bwd(
        proj, hg_lower_bound, hg_norm, hg_o, hg_st, dyb, dproj, Bl, nc,
        comm=("scatter", [_to_rows(d_wa), _to_rows(d_wb), _to_rows(d_wo)]))
    ddtr_b = ddtr.astype(BF16)
    (d_win_t,) = _matmul_tn("d_w_in", dproj, un, 768, D, M, out_skip=(DT_ROW, SSD_HEADS))
    (d_win_dt,) = _matmul_tn("d_w_in_dt", ddtr_b, un, 128, D, M)
    d_win_t = lax.dynamic_update_slice(d_win_t, d_win_dt[:SSD_HEADS], (DT_ROW, 0))
    d_un_dt_outs = []

    def d_un_dt_with_swap(theirs):
        d_un_dt_outs.extend(_fused_matmul("d_un_dt", M, D, 128, [dict(a=ddtr_b, b=win_dt, acc=0)], [], plain, [F32], 1,
                                          tm, D, 128, outer="j", comm=("swap", theirs)))
        return d_un_dt_outs[1:]

    s_win = _chip_sums("w_in", [_by_core(d_win_t)], swap_in=d_un_dt_with_swap)
    def mix_norm_bwd(accs, ex):
        dh, dw = _rmsnorm_bwd_tile(accs[0] + ex[0], ex[1], ex[3], ex[2])
        return dh, dh, dw

    dh1, dh1_b, d_mix_norm, p_win = _fused_matmul(
        "d_un", M, D, N_MAIN, [dict(a=dproj, b=win_t, acc=0, b_shift=(DT_ROW // 3072, SSD_HEADS))],
        [(d_un_dt_outs[0], 0), (h1, 0), (dh2, 0)],
        mix_norm_bwd, [F32, BF16], 1, _tile(M, (544, 256)), D, 3072, outer="i", comm=("chips", s_win),
        vecs=[mix_norm], row_sums=1)
    dh0, _, d_ffn1_norm, p_wgu1, p_wd1 = _ffn_bwd("ffn1", dh1, dh1_b, h0, ffn1_norm, wgu1, wd1, ffn1_saved, scatter=True)

    dh0 = dh0.reshape(Bl, T, D)
    grad_x = dh0[:, PAD + N_META:]
    d_meta = dh0[:, PAD:PAD + N_META]

    small_grads = [d_ffn1_norm, d_mix_norm, d_conv_b, _lanes_to_heads(d_bias_p), _lanes_to_heads(d_alog_p),
                   _lanes_to_heads(d_d_p), d_ssd_norm, d_hb, d_hg_norm, d_ffn2_norm, d_final.reshape(D), d_conv_w]
    small_like = small_grads + [d_meta[b] for b in range(Bl)] + [loss_part[0, 0:1]]
    small_packed = _pack_rows(small_like)
    parts = [p_wgu1, p_wd1, p_win, p_wa, p_wb, p_wo, p_wgu2, p_wd2]

    names = ["meta_tokens", "ffn1_norm", "ffn1_w_gu", "ffn1_w_down", "mix_norm", "w_in", "ssd_conv_w", "ssd_conv_b",
             "ssd_dt_bias", "ssd_a_log", "ssd_d", "ssd_norm", "hg_lower_bound", "hg_norm", "w_branch_a", "w_branch_b",
             "w_out", "ffn2_norm", "ffn2_w_gu", "ffn2_w_down", "final_norm"]
    W = dict(meta_tokens=meta_tokens, ffn1_norm=ffn1_norm, ffn1_w_gu=ffn1_w_gu, ffn1_w_down=ffn1_w_down, mix_norm=mix_norm,
             w_in=w_in, ssd_conv_w=ssd_conv_w, ssd_conv_b=ssd_conv_b, ssd_dt_bias=ssd_dt_bias, ssd_a_log=ssd_a_log,
             ssd_d=ssd_d, ssd_norm=ssd_norm, hg_lower_bound=hg_lower_bound, hg_norm=hg_norm, w_branch_a=w_branch_a,
             w_branch_b=w_branch_b, w_out=w_out, ffn2_norm=ffn2_norm, ffn2_w_gu=ffn2_w_gu, ffn2_w_down=ffn2_w_down,
             final_norm=final_norm)
    Mo = dict(meta_tokens=m_meta_tokens, ffn1_norm=m_ffn1_norm, ffn1_w_gu=m_ffn1_w_gu, ffn1_w_down=m_ffn1_w_down,
              mix_norm=m_mix_norm, w_in=m_w_in, ssd_conv_w=m_ssd_conv_w, ssd_conv_b=m_ssd_conv_b, ssd_dt_bias=m_ssd_dt_bias,
              ssd_a_log=m_ssd_a_log, ssd_d=m_ssd_d, ssd_norm=m_ssd_norm, hg_lower_bound=m_hg_lower_bound, hg_norm=m_hg_norm,
              w_branch_a=m_w_branch_a, w_branch_b=m_w_branch_b, w_out=m_w_out, ffn2_norm=m_ffn2_norm, ffn2_w_gu=m_ffn2_w_gu,
              ffn2_w_down=m_ffn2_w_down, final_norm=m_final_norm)
    Vo = dict(meta_tokens=v_meta_tokens, ffn1_norm=v_ffn1_norm, ffn1_w_gu=v_ffn1_w_gu, ffn1_w_down=v_ffn1_w_down,
              mix_norm=v_mix_norm, w_in=v_w_in, ssd_conv_w=v_ssd_conv_w, ssd_conv_b=v_ssd_conv_b, ssd_dt_bias=v_ssd_dt_bias,
              ssd_a_log=v_ssd_a_log, ssd_d=v_ssd_d, ssd_norm=v_ssd_norm, hg_lower_bound=v_hg_lower_bound, hg_norm=v_hg_norm,
              w_branch_a=v_w_branch_a, w_branch_b=v_w_branch_b, w_out=v_w_out, ffn2_norm=v_ffn2_norm, ffn2_w_gu=v_ffn2_w_gu,
              ffn2_w_down=v_ffn2_w_down, final_norm=v_final_norm)
    grads, deltas, new_m, new_v = {}, {}, {}, {}
    big_names = ["ffn1_w_gu", "ffn1_w_down", "w_in", "w_branch_a", "w_branch_b", "w_out", "ffn2_w_gu", "ffn2_w_down"]
    transposed = ("ffn1_w_gu", "ffn2_w_gu", "w_in")
    small_all = None
    for nm, part in zip(big_names, parts):
        view = (lambda a: a[0].T) if nm in transposed else (lambda a: a[0])
        back = (lambda o: o.T[None]) if nm in transposed else (lambda o: o[None])
        outs = _adamw("adamw_" + nm, part, view(W[nm]), view(Mo[nm]), view(Vo[nm]),
                      comm=("gather", [small_packed]) if small_all is None else None)
        if small_all is None:
            small_all = outs[4]
        grads[nm], deltas[nm], new_m[nm], new_v[nm] = (back(o) for o in outs[:4])
    unpacked = _unpack_rows(_sum_parts("sum_small_grads", small_all), small_like)
    g_small = unpacked[:len(small_grads)]
    g_meta_full = unpacked[len(small_grads)]
    for b in range(1, Bl):
        g_meta_full = g_meta_full + unpacked[len(small_grads) + b]
    g_meta = lax.dynamic_slice_in_dim(g_meta_full, me * (D // N_DEV), D // N_DEV, axis=1)
    g_conv_w = lax.dynamic_slice_in_dim(g_small[11], me * (SSD_CONV_CH // N_DEV), SSD_CONV_CH // N_DEV, axis=1)
    loss = unpacked[-1].reshape(())
    small_names = ["ffn1_norm", "mix_norm", "ssd_conv_b", "ssd_dt_bias", "ssd_a_log", "ssd_d", "ssd_norm", "hg_lower_bound",
                   "hg_norm", "ffn2_norm", "final_norm", "ssd_conv_w", "meta_tokens"]
    small_g = g_small[:11] + [g_conv_w.reshape(ssd_conv_w.shape), g_meta]
    pk = lambda d: _pack_rows([d[nm] for nm in small_names])
    outs = _adamw("adamw_small", _pack_rows(small_g)[None], pk(W), pk(Mo), pk(Vo))
    like = [W[nm] for nm in small_names]
    for dst, o in zip((grads, deltas, new_m, new_v), outs):
        for nm, val in zip(small_names, _unpack_rows(o, like)):
            dst[nm] = val

    return (loss, grad_x, *[grads[nm] for nm in names], *[deltas[nm] for nm in names],
            *[new_m[nm] for nm in names], *[new_v[nm] for nm in names])
```

```python
import functools

import jax
import jax.numpy as jnp
from jax import lax
from jax.experimental import pallas as pl
from jax.experimental.pallas import tpu as pltpu

F32, BF16 = jnp.float32, jnp.bfloat16
NN, NT, TN = ((1,), (0,)), ((1,), (1,)), ((0,), (0,))
MESH_AXES = ("x", "y", "c")
N_DEV = 8

D_MODEL = 1024
N_META = 16
EPS = 1e-6
SSD_HEADS, SSD_HEAD_DIM, SSD_GROUPS, SSD_STATE, SSD_CONV, Q = 16, 64, 4, 128, 4, 128
SSD_INNER = SSD_HEADS * SSD_HEAD_DIM
SSD_CONV_CH = SSD_INNER + 2 * SSD_GROUPS * SSD_STATE
HG_WIDTH, HG_HEADS, HG_CHUNK = 1024, 8, 16
PAD = Q - N_META
N_MAIN = 9 * 1024
ADAM_LR, ADAM_B1, ADAM_B2, ADAM_EPS, ADAM_WD, ADAM_STEP = 0.001, 0.9, 0.999, 1e-08, 0.01, 10
VMEM_LIMIT = 52 * 1024 * 1024


def _dot(a, b, dims):
    return lax.dot_general(a, b, (dims, ((), ())), preferred_element_type=F32)


def _dot01(a, b, dims, sel):
    x = b if sel == "a" else a
    hi = x.astype(BF16)
    r1 = x - hi.astype(F32)
    mid = r1.astype(BF16)
    lo = (r1 - mid.astype(F32)).astype(BF16)
    s = (a if sel == "a" else b).astype(BF16)
    parts = [_dot(s, p, dims) if sel == "a" else _dot(p, s, dims) for p in (hi, mid, lo)]
    return parts[0] + parts[1] + parts[2]


def _sigmoid(x):
    return 0.5 * jnp.tanh(0.5 * x) + 0.5


def _dsilu(x, s):
    return s * (1.0 + x * (1.0 - s))


def _softplus(x):
    e = jnp.exp(-jnp.abs(x))
    u = 1.0 + e
    log1p_e = jnp.where(u == 1.0, e, jnp.log(u) * e / (u - 1.0))
    return jnp.maximum(x, 0.0) + log1p_e


def _params(sem):
    return pltpu.CompilerParams(dimension_semantics=sem, vmem_limit_bytes=VMEM_LIMIT)


def _tile(n, prefs):
    for p in prefs:
        if n % p == 0:
            return p
    return n


CHIP_FLIPS = ((1, 0), (0, 1), (1, 1))
N_PEER = N_DEV - 1


def _comm_gather(srcs, outs, send_sems, recv_sems, local_sems):
    n = len(srcs)
    x, y, c = (lax.axis_index(a) for a in MESH_AXES)
    dev = lambda px, py, pc: 4 * px + 2 * py + pc
    me, sib = dev(x, y, c), (x, y, 1 - c)
    nbr_x, nbr_y, diag = (1 - x, y), (x, 1 - y), (1 - x, 1 - y)
    via = (x ^ c, y ^ (1 - c), c)
    sent_on = dev(x ^ (1 - c), y ^ c, c)

    def rc(w, k, slot, to, src=None):
        return pltpu.make_async_remote_copy(
            src_ref=outs[w].at[slot] if src is None else src, dst_ref=outs[w].at[slot],
            send_sem=send_sems.at[w, k], recv_sem=recv_sems.at[w, k], device_id=to, device_id_type=pl.DeviceIdType.MESH)

    def local(w):
        return pltpu.make_async_copy(srcs[w], outs[w].at[me], local_sems.at[w])

    def start():
        for w in range(n):
            local(w).start()
            rc(w, 0, me, sib, src=srcs[w]).start()
            rc(w, 1, me, (*nbr_x, c), src=srcs[w]).start()
            rc(w, 2, me, (*nbr_y, c), src=srcs[w]).start()

    def pass_on():
        for w in range(n):
            rc(w, 1, dev(*nbr_x, c), sib).wait_recv()
            rc(w, 2, dev(*nbr_y, c), sib).wait_recv()
            rc(w, 3, sent_on, via).start()
            rc(w, 4, dev(*nbr_x, c), sib).start()
            rc(w, 5, dev(*nbr_y, c), sib).start()

    def pass_on_diagonal():
        for w in range(n):
            rc(w, 3, dev(*diag, c), sib).wait_recv()
            rc(w, 6, dev(*diag, c), sib).start()

    def finish():
        for w in range(n):
            rc(w, 0, dev(x, y, 1 - c), sib).wait_recv()
            for k, chip in ((4, nbr_x), (5, nbr_y), (6, diag)):
                rc(w, k, dev(*chip, 1 - c), sib).wait_recv()
            for k in range(N_PEER):
                rc(w, k, me, sib, src=srcs[w]).wait_send()
            local(w).wait()

    return start, (pass_on, pass_on_diagonal), finish


def _comm_scatter(srcs, outs, send_sems, recv_sems, local_sems):
    n = len(srcs)
    x, y, c = (lax.axis_index(a) for a in MESH_AXES)
    me = 4 * x + 2 * y + c

    def copies():
        out = []
        for w in range(n):
            out.append(pltpu.make_async_copy(srcs[w].at[me], outs[w].at[me], local_sems.at[w]))
            for k in range(1, N_DEV):
                px, py, pc = x ^ (k >> 2), y ^ ((k >> 1) & 1), c ^ (k & 1)
                out.append(pltpu.make_async_remote_copy(
                    src_ref=srcs[w].at[4 * px + 2 * py + pc], dst_ref=outs[w].at[me],
                    send_sem=send_sems.at[w, k - 1], recv_sem=recv_sems.at[w, k - 1],
                    device_id=(px, py, pc), device_id_type=pl.DeviceIdType.MESH))
        return out

    def start():
        for cp in copies():
            cp.start()

    def finish():
        for cp in copies():
            cp.wait()

    return start, None, finish


def _comm_swap(srcs, outs, send_sems, recv_sems, local_sems):
    x, y, c = (lax.axis_index(a) for a in MESH_AXES)

    def copies():
        return [pltpu.make_async_remote_copy(
            src_ref=srcs[w].at[1 - c], dst_ref=outs[w], send_sem=send_sems.at[w, 0], recv_sem=recv_sems.at[w, 0],
            device_id=(x, y, 1 - c), device_id_type=pl.DeviceIdType.MESH) for w in range(len(srcs))]

    def start():
        for cp in copies():
            cp.start()

    def finish():
        for cp in copies():
            cp.wait()

    return start, None, finish


def _comm_chips(srcs, outs, send_sems, recv_sems, local_sems):
    n = len(srcs)
    x, y, c = (lax.axis_index(a) for a in MESH_AXES)
    mine = 2 * x + y

    def copies():
        out = []
        for w in range(n):
            out.append(pltpu.make_async_copy(srcs[w].at[mine], outs[w].at[mine], local_sems.at[w]))
            for j, (fx, fy) in enumerate(CHIP_FLIPS):
                px, py = x ^ fx, y ^ fy
                out.append(pltpu.make_async_remote_copy(
                    src_ref=srcs[w].at[2 * px + py], dst_ref=outs[w].at[mine],
                    send_sem=send_sems.at[w, j], recv_sem=recv_sems.at[w, j],
                    device_id=(px, py, c), device_id_type=pl.DeviceIdType.MESH))
        return out

    def start():
        for cp in copies():
            cp.start()

    def finish():
        for cp in copies():
            cp.wait()

    return start, None, finish


def _comm_parts(comm):
    kind, arrays = comm[:2]
    n = len(arrays)
    lead = {"gather": lambda a: (N_DEV,) + a.shape, "scatter": lambda a: (N_DEV,) + a.shape[1:],
            "swap": lambda a: a.shape[1:], "chips": lambda a: a.shape}[kind]
    shapes = [jax.ShapeDtypeStruct(lead(a), a.dtype) for a in arrays]
    sems = [pltpu.SemaphoreType.DMA((n, N_PEER)), pltpu.SemaphoreType.DMA((n, N_PEER)), pltpu.SemaphoreType.DMA((n,))]
    make = {"gather": _comm_gather, "scatter": _comm_scatter, "swap": _comm_swap, "chips": _comm_chips}[kind]
    return n, shapes, sems, make


def _exchange(name, kind, arrays):
    n, shapes, sems, make = _comm_parts((kind, arrays))

    def body(*refs):
        start, middle, finish = make(refs[:n], refs[n:2 * n], *refs[2 * n:])
        start()
        for stage in middle or ():
            stage()
        finish()

    any_spec = pl.BlockSpec(memory_space=pl.ANY)
    return pl.pallas_call(
        body, name=name, in_specs=[any_spec] * n, out_specs=[any_spec] * n, out_shape=shapes, scratch_shapes=sems,
        compiler_params=pltpu.CompilerParams(has_side_effects=True),
    )(*arrays)


def _call(body, *, name, grid, in_specs, out_specs, out_shape, scratch, sem, args, comm=None, into=None):
    any_spec = pl.BlockSpec(memory_space=pl.ANY)
    in_specs, args, aliases, n_body_in = list(in_specs), list(args), {}, len(in_specs)
    for arr, k in ([] if into is None else into if isinstance(into, list) else [into]):
        aliases[len(in_specs)] = k
        in_specs.append(any_spec)
        args.append(arr)
    n_in, n_out, n_scr = len(in_specs), len(out_specs), len(scratch)
    if comm is None:
        def plain(*refs):
            body(*refs[:n_body_in], *refs[n_in:])

        return pl.pallas_call(plain, name=name, grid=grid, in_specs=in_specs, out_specs=out_specs, out_shape=out_shape,
                              scratch_shapes=scratch, input_output_aliases=aliases, compiler_params=_params(sem))(*args)
    n, shapes, sems, make = _comm_parts(comm)

    def carrier(*refs):
        ins, csrc = refs[:n_body_in], refs[n_in:n_in + n]
        outs, cout = refs[n_in + n:n_in + n + n_out], refs[n_in + n + n_out:n_in + 2 * n + n_out]
        rest = refs[n_in + 2 * n + n_out:]
        start, middle, finish = make(csrc, cout, *rest[n_scr:])
        ids = [pl.program_id(a) for a in range(len(grid))]
        step = functools.reduce(lambda acc, ig: acc * ig[1] + ig[0], zip(ids, grid), 0)
        n_steps = functools.reduce(lambda a, b: a * b, grid, 1)
        pl.when(step == 0)(start)
        body(*ins, *outs, *rest[:n_scr])
        if middle:
            pl.when(step == max(0, (3 * n_steps) // 4 - 1))(middle[0])
            pl.when(step == n_steps - 1)(middle[1])
        pl.when(step == n_steps - 1)(finish)

    return pl.pallas_call(
        carrier, name=name, grid=grid, in_specs=in_specs + [any_spec] * n,
        out_specs=list(out_specs) + [any_spec] * n, out_shape=list(out_shape) + shapes,
        scratch_shapes=list(scratch) + sems, input_output_aliases=aliases,
        compiler_params=pltpu.CompilerParams(dimension_semantics=("arbitrary",) * len(grid),
                                             vmem_limit_bytes=VMEM_LIMIT, has_side_effects=True),
    )(*args, *comm[1])


def _fused_matmul(name, M, N, K, pairs, extras, epilogue, out_dtypes, n_acc, tm, tn, tk, outer="i", comm=None,
                  stack=False, vecs=(), row_sums=0, wide=None, sub=None):
    nk = K // tk
    n_pairs, n_ex, n_out = len(pairs), len(extras), len(out_dtypes)
    assert not row_sums or (outer == "i" and N == tn)

    def ij(g0, g1):
        return (g0, g1) if outer == "i" else (g1, g0)

    in_specs, args = [], []
    for p in pairs:
        ao, bk, bn = p.get("a_off", 0), p.get("bk_off", 0), p.get("bn_off", 0)
        mode = dict(pipeline_mode=pl.Buffered(1)) if p.get("resident") else {}
        in_specs.append(pl.BlockSpec((tm, tk), lambda g0, g1, k, ao=ao: (ij(g0, g1)[0], k + ao)))
        if "b_shift" in p:
            first, shift = p["b_shift"]
            if p.get("trans_b"):
                in_specs.append(pl.BlockSpec(
                    (pl.Element(tn), pl.Element(tk)),
                    lambda g0, g1, k, bk=bk: (
                        pl.multiple_of(ij(g0, g1)[1] * tn + jnp.where(ij(g0, g1)[1] >= first, shift, 0), 16),
                        (k + bk) * tk)))
            else:
                in_specs.append(pl.BlockSpec(
                    (pl.Element(tk), pl.Element(tn)),
                    lambda g0, g1, k, bn=bn: (pl.multiple_of(k * tk + jnp.where(k >= first, shift, 0), 16),
                                              (ij(g0, g1)[1] + bn) * tn)))
        elif p.get("trans_b"):
            in_specs.append(pl.BlockSpec((tn, tk), lambda g0, g1, k, bk=bk, bn=bn: (ij(g0, g1)[1] + bn, k + bk), **mode))
        else:
            in_specs.append(pl.BlockSpec((tk, tn), lambda g0, g1, k, bk=bk, bn=bn: (k + bk, ij(g0, g1)[1] + bn), **mode))
        args += [p["a"], p["b"]]
    for arr, off in extras:
        in_specs.append(pl.BlockSpec((tm, tn), lambda g0, g1, k, off=off: (ij(g0, g1)[0], ij(g0, g1)[1] + off)))
        args.append(arr)
    for arr in vecs:
        in_specs.append(pl.BlockSpec((1, tn), lambda g0, g1, k: (0, ij(g0, g1)[1])))
        args.append(arr)
    if stack:
        assert N == tn
        out_specs = [pl.BlockSpec((tm, n_out * tn), lambda g0, g1, k: (ij(g0, g1)[0], 0))]
        out_shape = [jax.ShapeDtypeStruct((M, n_out * N), out_dtypes[0])]
    else:
        out_specs = [pl.BlockSpec((tm, tn), lambda g0, g1, k: ij(g0, g1)) for _ in out_dtypes]
        out_shape = [jax.ShapeDtypeStruct((M, N), dt) for dt in out_dtypes]
    if wide:
        out_specs.append(pl.BlockSpec((pl.Element(tm), pl.Element(wide["width"])),
                                      lambda g0, g1, k: (pl.multiple_of(ij(g0, g1)[0] * tm, 16), wide["col"])))
        out_shape.append(jax.ShapeDtypeStruct((M, wide["total"]), wide["dtype"]))
    n_tile_out = len(out_specs)
    out_specs += [pl.BlockSpec((1, tn), lambda g0, g1, k: (0, 0)) for _ in range(row_sums)]
    out_shape += [jax.ShapeDtypeStruct((1, N), F32) for _ in range(row_sums)]
    grid = (M // tm, N // tn, nk) if outer == "i" else (N // tn, M // tm, nk)
    n_in = 2 * n_pairs + n_ex + len(vecs)

    def partials(refs, cs=slice(None)):
        accs = [None] * n_acc
        for idx, p in enumerate(pairs):
            b_ref = refs[2 * idx + 1]
            d = (_dot(refs[2 * idx][...], b_ref[cs, :], NT) if p.get("trans_b")
                 else _dot(refs[2 * idx][...], b_ref[:, cs], NN))
            accs[p["acc"]] = d if accs[p["acc"]] is None else accs[p["acc"]] + d
        return accs

    def finish(accs, refs, first_rows, cs=slice(None)):
        res = epilogue(accs, [r[:, cs] for r in refs[2 * n_pairs:n_in]])
        if stack:
            o = refs[n_in]
            for idx in range(n_out):
                lo = idx * tn + (cs.start or 0)
                o[:, lo:lo + (tn if cs.stop is None else cs.stop - cs.start)] = res[idx].astype(o.dtype)
        else:
            for o, r in zip(refs[n_in:n_in + n_out], res):
                o[:, cs] = r.astype(o.dtype)
        if wide:
            o = refs[n_in + n_tile_out - 1]
            o[...] = res[n_out].astype(o.dtype)
        for o, r in zip(refs[n_in + n_tile_out:n_in + n_tile_out + row_sums], res[n_out + bool(wide):]):
            @pl.when(first_rows)
            def _(o=o, r=r):
                o[...] = r

            @pl.when(jnp.logical_not(first_rows))
            def _(o=o, r=r):
                o[...] += r

    if nk == 1 and sub:
        assert not wide and not row_sums and tn % sub == 0

        def body(*refs):
            for c in range(tn // sub):
                cs = slice(c * sub, (c + 1) * sub)
                finish(partials(refs, cs), refs, None, cs)
        scratch = []
    elif nk == 1:
        def body(*refs):
            finish(partials(refs), refs, pl.program_id(0) == 0)
        scratch = []
    else:
        def body(*refs):
            acc_refs = refs[-n_acc:]
            k = pl.program_id(2)
            first_rows = pl.program_id(0) == 0
            new = partials(refs)

            @pl.when(k == 0)
            def _():
                for a, v in zip(acc_refs, new):
                    a[...] = v

            @pl.when(k > 0)
            def _():
                for a, v in zip(acc_refs, new):
                    a[...] += v

            @pl.when(k == nk - 1)
            def _():
                finish([a[...] for a in acc_refs], refs, first_rows)
        scratch = [pltpu.VMEM((tm, tn), F32) for _ in range(n_acc)]

    return _call(body, name=name, grid=grid, in_specs=in_specs, out_specs=out_specs, out_shape=out_shape,
                 scratch=scratch, sem=("parallel", "parallel", "arbitrary"), args=args, comm=comm)


def _matmul_tn(name, x, y, t1, t2, tr, scale=1.0, comm=None, out_dtype=BF16, out_skip=None):
    R, K1 = x.shape
    N1 = y.shape[1]
    nr, n1 = R // tr, K1 // t1
    x_spec = pl.BlockSpec((tr, t1), lambda i, j, r: (r, i))
    rows_out = K1
    o_spec = pl.BlockSpec((t1, t2), lambda i, j, r: (i, j))
    if out_skip:
        row, count = out_skip
        rows_out += count
        o_spec = pl.BlockSpec(
            (pl.Element(t1), pl.Element(t2)),
            lambda i, j, r: (pl.multiple_of(i * t1 + jnp.where(i * t1 >= row, count, 0), 16), j * t2))

    def body(x_ref, y_ref, o_ref, *acc):
        d = _dot(x_ref[...], y_ref[...], TN)
        if nr == 1:
            o_ref[...] = (d * scale).astype(o_ref.dtype)
            return
        r = pl.program_id(2)

        @pl.when(r == 0)
        def _():
            acc[0][...] = d

        @pl.when(jnp.logical_and(r > 0, r < nr - 1))
        def _():
            acc[0][...] += d

        @pl.when(r == nr - 1)
        def _():
            o_ref[...] = ((acc[0][...] + d) * scale).astype(o_ref.dtype)

    return _call(
        body, name=name, grid=(n1, N1 // t2, nr),
        in_specs=[x_spec, pl.BlockSpec((tr, t2), lambda i, j, r: (r, j))], out_specs=[o_spec],
        out_shape=[jax.ShapeDtypeStruct((rows_out, N1), out_dtype)],
        scratch=[pltpu.VMEM((t1, t2), F32)] if nr > 1 else [],
        sem=("parallel", "parallel", "arbitrary"), args=(x, y), comm=comm)


def _embed_norm(x, w, comm=None):
    Bl, S, D = x.shape
    nb = (PAD + N_META + S) // Q
    M = Bl * nb * Q

    def body(x_ref, w_ref, h_ref, n_ref):
        h = x_ref[0]
        h_ref[...] = h
        n_ref[...] = _rmsnorm_tile(h, w_ref[...]).astype(n_ref.dtype)

    row = pl.BlockSpec((Q, D), lambda b, t: (b * nb + t + 1, 0))
    return _call(
        body, name="embed_norm", grid=(Bl, nb - 1),
        in_specs=[pl.BlockSpec((1, Q, D), lambda b, t: (b, t, 0)), pl.BlockSpec((1, D), lambda b, t: (0, 0))],
        out_specs=[row, row], out_shape=[jax.ShapeDtypeStruct((M, D), F32), jax.ShapeDtypeStruct((M, D), BF16)],
        scratch=[], sem=("parallel", "parallel"), args=(x, w), comm=comm)


def _embed_meta(meta, w, h0, n0, Bl):
    nb = h0.shape[0] // (Bl * Q)
    D = h0.shape[1]

    def body(meta_ref, w_ref, h_ref, n_ref):
        h = jnp.concatenate([jnp.zeros((PAD, D), F32), meta_ref[...]], axis=0)
        h_ref[...] = h
        n_ref[...] = _rmsnorm_tile(h, w_ref[...]).astype(n_ref.dtype)

    row = pl.BlockSpec((Q, D), lambda b: (b * nb, 0))
    return _call(
        body, name="embed_meta", grid=(Bl,),
        in_specs=[pl.BlockSpec((N_META, D), lambda b: (0, 0)), pl.BlockSpec((1, D), lambda b: (0, 0))],
        out_specs=[row, row], out_shape=[jax.ShapeDtypeStruct(h0.shape, F32), jax.ShapeDtypeStruct(n0.shape, BF16)],
        scratch=[], sem=("parallel",), args=(meta, w), into=[(h0, 0), (n0, 1)])


def _rmsnorm_bwd_tile(dn, h, w, dh_in):
    r = lax.rsqrt(jnp.mean(h * h, axis=-1, keepdims=True) + EPS)
    xhat = h * r
    gw = dn * w
    dh = dh_in + r * (gw - xhat * jnp.mean(gw * xhat, axis=-1, keepdims=True))
    return dh, jnp.sum(dn * xhat, axis=0, keepdims=True)


def _loss_head(h, w, target, Bl, nb):
    M, D = h.shape
    nt = 4 if (nb * Q) % 32 == 0 and nb * Q // 4 >= Q else nb
    half = nb * Q // nt

    def body(h_ref, w_ref, t_ref, dh_ref, dhb_ref, dw_ref, loss_ref):
        b, t = pl.program_id(0), pl.program_id(1)
        row = lax.broadcasted_iota(jnp.int32, (half, 1), 0)
        live = jnp.logical_or(t > 0, row >= Q).astype(F32)
        x = h_ref[...]
        r = lax.rsqrt(jnp.mean(x * x, axis=-1, keepdims=True) + EPS)
        xhat = x * r
        wv = w_ref[...]
        tgt = t_ref[0]
        tgt = jnp.where(t == 0, pltpu.roll(tgt, Q, 0), tgt)
        err = (xhat * wv - tgt) * live
        dy = err * (1.0 / D)
        gw = dy * wv
        dx = r * (gw - xhat * jnp.mean(gw * xhat, axis=-1, keepdims=True))
        dh_ref[...] = dx
        dhb_ref[...] = dx.astype(BF16)
        dw = jnp.sum(dy * xhat, axis=0, keepdims=True)
        part = 0.5 * jnp.sum(jnp.sum(err * err, axis=-1, keepdims=True) * (1.0 / D), axis=0, keepdims=True)
        first = jnp.logical_and(b == 0, t == 0)

        @pl.when(first)
        def _():
            dw_ref[...] = dw
            loss_ref[...] = jnp.broadcast_to(part, loss_ref.shape)

        @pl.when(jnp.logical_not(first))
        def _():
            dw_ref[...] += dw
            loss_ref[...] += jnp.broadcast_to(part, loss_ref.shape)

    row = pl.BlockSpec((half, D), lambda b, t: (b * nt + t, 0))
    vec = pl.BlockSpec((1, D), lambda b, t: (0, 0))
    return pl.pallas_call(
        body, name="loss_head", grid=(Bl, nt),
        in_specs=[row, vec, pl.BlockSpec((pl.Element(1), pl.Element(half), pl.Element(D)),
                                         lambda b, t: (b, pl.multiple_of(jnp.maximum(t * half - Q, 0), 8), 0))],
        out_specs=[row, row, vec, pl.BlockSpec((8, 128), lambda b, t: (0, 0))],
        out_shape=[jax.ShapeDtypeStruct((M, D), F32), jax.ShapeDtypeStruct((M, D), BF16),
                   jax.ShapeDtypeStruct((1, D), F32), jax.ShapeDtypeStruct((8, 128), F32)],
        compiler_params=_params(("arbitrary", "arbitrary")),
    )(h, w, target)


CONV_TC = 256


def _conv_pre(xr_ref, w_ref, b_ref):
    x = xr_ref[...].astype(F32)
    acc = b_ref[...] + w_ref[SSD_CONV - 1:SSD_CONV, :] * x
    for k in range(1, SSD_CONV):
        acc = acc + w_ref[SSD_CONV - 1 - k:SSD_CONV - k, :] * pltpu.roll(x, k, 0)
    return x, acc


def _conv_fwd(proj, w, b, Bl, T):
    M = proj.shape[0]
    off = 1024 // CONV_TC

    def body(xr_ref, w_ref, b_ref, o_ref):
        _, acc = _conv_pre(xr_ref, w_ref, b_ref)
        row = lax.broadcasted_iota(jnp.int32, acc.shape, 0)
        o_ref[...] = jnp.where(row >= PAD, acc * _sigmoid(acc), 0.0).astype(o_ref.dtype)

    return pl.pallas_call(
        body, name="conv_fwd", grid=(Bl, SSD_CONV_CH // CONV_TC),
        in_specs=[pl.BlockSpec((T, CONV_TC), lambda bb, j: (bb, j + off)),
                  pl.BlockSpec((SSD_CONV, CONV_TC), lambda bb, j: (0, j)), pl.BlockSpec((1, CONV_TC), lambda bb, j: (0, j))],
        out_specs=pl.BlockSpec((T, CONV_TC), lambda bb, j: (bb, j)),
        out_shape=jax.ShapeDtypeStruct((M, SSD_CONV_CH), BF16), compiler_params=_params(("parallel", "parallel")),
    )(proj, w, b)


def _conv_bwd(proj, w, b, dxc, dproj, Bl, T):
    M = proj.shape[0]
    off = 1024 // CONV_TC

    def body(xr_ref, w_ref, b_ref, d_ref, dx_ref, dw_ref, db_ref):
        x, acc = _conv_pre(xr_ref, w_ref, b_ref)
        row = lax.broadcasted_iota(jnp.int32, acc.shape, 0)
        s = _sigmoid(acc)
        dpre = jnp.where(row >= PAD, d_ref[...].astype(F32) * _dsilu(acc, s), 0.0)
        dx = w_ref[SSD_CONV - 1:SSD_CONV, :] * dpre
        dws = [jnp.sum(dpre * x, axis=0, keepdims=True)]
        for k in range(1, SSD_CONV):
            dx = dx + w_ref[SSD_CONV - 1 - k:SSD_CONV - k, :] * pltpu.roll(dpre, T - k, 0)
            dws.append(jnp.sum(dpre * pltpu.roll(x, k, 0), axis=0, keepdims=True))
        dx_ref[...] = dx.astype(dx_ref.dtype)
        dw = jnp.concatenate(dws[::-1], axis=0)
        db = jnp.sum(dpre, axis=0, keepdims=True)

        @pl.when(pl.program_id(1) == 0)
        def _():
            dw_ref[...] = dw
            db_ref[...] = db

        @pl.when(pl.program_id(1) > 0)
        def _():
            dw_ref[...] += dw
            db_ref[...] += db

    return _call(
        body, name="conv_bwd", grid=(SSD_CONV_CH // CONV_TC, Bl),
        in_specs=[pl.BlockSpec((T, CONV_TC), lambda j, bb: (bb, j + off)),
                  pl.BlockSpec((SSD_CONV, CONV_TC), lambda j, bb: (0, j)), pl.BlockSpec((1, CONV_TC), lambda j, bb: (0, j)),
                  pl.BlockSpec((T, CONV_TC), lambda j, bb: (bb, j))],
        out_specs=[pl.BlockSpec((T, CONV_TC), lambda j, bb: (bb, j + off)),
                   pl.BlockSpec((SSD_CONV, CONV_TC), lambda j, bb: (0, j)), pl.BlockSpec((1, CONV_TC), lambda j, bb: (0, j))],
        out_shape=[jax.ShapeDtypeStruct(dproj.shape, BF16), jax.ShapeDtypeStruct((SSD_CONV, SSD_CONV_CH), F32),
                   jax.ShapeDtypeStruct((1, SSD_CONV_CH), F32)],
        scratch=[], sem=("parallel", "arbitrary"), args=(proj, w, b, dxc), into=(dproj, 0))


N_PAIR = SSD_HEADS // 2
HPG = SSD_HEADS // SSD_GROUPS
GW = SSD_INNER // SSD_GROUPS


def _per_group(fn, *arrs):
    return jnp.concatenate([jnp.broadcast_to(fn(*(a[:, GW * g:GW * (g + 1)] for a in arrs)), (arrs[0].shape[0], GW))
                            for g in range(SSD_GROUPS)], axis=1)


def _ssd_prep(c, dtr_ref, bias_ref, alog_ref, d_ref):
    row = lax.broadcasted_iota(jnp.int32, (Q, 128), 0)
    col = lax.broadcasted_iota(jnp.int32, (Q, 128), 1)
    live = col < SSD_HEADS
    valid = jnp.logical_and(jnp.logical_or(c > 0, row >= PAD), live)
    pre = dtr_ref[...] + bias_ref[...]
    dt = jnp.where(valid, _softplus(pre), 0.0)
    A = jnp.where(live[0:1], -jnp.exp(alog_ref[...]), 0.0)
    tri = row >= col
    eye = (row == col).astype(BF16)
    cs = _dot01(tri, dt * A, NN, "a")
    cst = _dot01(eye, cs, NT, "a")
    spread = (lax.broadcasted_iota(jnp.int32, (128, SSD_INNER), 0)
              == lax.broadcasted_iota(jnp.int32, (128, SSD_INNER), 1) // SSD_HEAD_DIM).astype(BF16)
    dt_w = _dot01(dt, spread, NN, "b")
    cs_w = _dot01(cs, spread, NN, "b")
    d_w = _dot01(jnp.broadcast_to(d_ref[...], (8, 128)), spread, NN, "b")[0:1]
    lane = lax.broadcasted_iota(jnp.int32, (Q, SSD_INNER), 1)
    first = (lane % 128) < SSD_HEAD_DIM
    return dict(row=row, col=col, valid=valid, pre=pre, dt=dt, A=A, tri=tri, eye=eye, cs=cs, cst=cst, spread=spread,
                dt_w=dt_w, cs_w=cs_w, d_w=d_w, ecs_w=jnp.exp(cs_w), decay_w=jnp.exp(cs_w[Q - 1:Q] - cs_w), first=first)


def _ssd_chunk(xc_ref, s, states):
    xv = xc_ref[:, 0:SSD_INNER].astype(F32)
    Bs = [xc_ref[:, SSD_INNER + 128 * g:SSD_INNER + 128 * (g + 1)] for g in range(SSD_GROUPS)]
    Cs = [xc_ref[:, SSD_INNER + 512 + 128 * g:SSD_INNER + 512 + 128 * (g + 1)] for g in range(SSD_GROUPS)]
    X = xv * s["dt_w"]
    X0 = jnp.where(s["first"], X, 0.0)
    Xb = (X0.astype(BF16), (X - X0).astype(BF16))
    Xd = (X * s["decay_w"]).astype(BF16)
    CB = [_dot(Cs[g], Bs[g], NT) for g in range(SSD_GROUPS)]
    Lms = [jnp.exp(jnp.where(s["tri"], s["cs"][:, h:h + 1] - s["cst"][h:h + 1, :], -jnp.inf)) for h in range(SSD_HEADS)]
    Ms = [CB[h // HPG] * Lms[h] for h in range(SSD_HEADS)]
    Mb = [m.astype(BF16) for m in Ms]
    prev_b = [st.astype(BF16) for st in states]
    yds, yos, sts = [], [], []
    for p in range(N_PAIR):
        g, ln = p // 2, slice(128 * p, 128 * (p + 1))
        yds.append(_dot(Mb[2 * p], Xb[0][:, ln], NN) + _dot(Mb[2 * p + 1], Xb[1][:, ln], NN))
        yos.append(_dot(Cs[g], prev_b[p], NT))
        sts.append(_dot(Xd[:, ln], Bs[g], TN))
    yo = jnp.concatenate(yos, axis=1)
    y = jnp.concatenate(yds, axis=1) + yo * s["ecs_w"] + xv * s["d_w"]
    upper = s["row"] < SSD_HEAD_DIM
    cl = s["cs"][Q - 1:Q, :]
    ecl_rows = [jnp.where(upper, jnp.exp(cl[:, 2 * p:2 * p + 1]), jnp.exp(cl[:, 2 * p + 1:2 * p + 2])) for p in range(N_PAIR)]
    new_states = [states[p] * ecl_rows[p] + sts[p] for p in range(N_PAIR)]
    return y, new_states, dict(xv=xv, Bs=Bs, Cs=Cs, X=X, Xb=Xb, CB=CB, Lms=Lms, Ms=Ms, Mb=Mb, prev_b=prev_b, yo=yo,
                               ecl_rows=ecl_rows)


def _ssd_in_specs(nc, rev=False):
    rb = (lambda b, c: b * nc + nc - 1 - c) if rev else (lambda b, c: b * nc + c)
    vec = pl.BlockSpec((1, 128), lambda b, c: (0, 0))
    return [pl.BlockSpec((Q, SSD_CONV_CH), lambda b, c: (rb(b, c), 0)),
            pl.BlockSpec((Q, 128), lambda b, c: (rb(b, c), 0)),
            pl.BlockSpec((Q, SSD_INNER), lambda b, c: (rb(b, c), 0)),
            vec, vec, vec, pl.BlockSpec((1, SSD_INNER), lambda b, c: (0, 0))]


def _ssd_fwd(xc, dtr, proj, bias_p, alog_p, d_p, nw, Bl, nc):
    M = xc.shape[0]

    def body(xc_ref, dtr_ref, z_ref, bias_ref, alog_ref, d_ref, nw_ref, y_ref, prev_ref, state):
        c = pl.program_id(1)

        @pl.when(c == 0)
        def _():
            state[...] = jnp.zeros_like(state)

        s = _ssd_prep(c, dtr_ref, bias_ref, alog_ref, d_ref)
        states = [state[p] for p in range(N_PAIR)]
        y, new_states, _ = _ssd_chunk(xc_ref, s, states)
        for p in range(N_PAIR):
            prev_ref[0, 0, p] = states[p]
            state[p] = new_states[p]
        zz = z_ref[...].astype(F32)
        yg = y * zz * _sigmoid(zz)
        r = _per_group(lambda a: lax.rsqrt(jnp.mean(a * a, axis=-1, keepdims=True) + EPS), yg)
        y_ref[...] = (yg * r * nw_ref[...]).astype(y_ref.dtype)

    return pl.pallas_call(
        body, name="ssd_fwd", grid=(Bl, nc), in_specs=_ssd_in_specs(nc),
        out_specs=[pl.BlockSpec((Q, SSD_INNER), lambda b, c: (b * nc + c, 0)),
                   pl.BlockSpec((1, 1, N_PAIR, 128, 128), lambda b, c: (b, c, 0, 0, 0))],
        out_shape=[jax.ShapeDtypeStruct((M, SSD_INNER), BF16), jax.ShapeDtypeStruct((Bl, nc, N_PAIR, 128, 128), F32)],
        scratch_shapes=[pltpu.VMEM((N_PAIR, 128, 128), F32)],
        compiler_params=_params(("arbitrary", "arbitrary")),
    )(xc, dtr, proj, bias_p, alog_p, d_p, nw)


def _ssd_bwd(xc, dtr, proj, bias_p, alog_p, d_p, nw, prev, dya, dproj, Bl, nc, comm=None):
    M = xc.shape[0]

    def body(xc_ref, dtr_ref, z_ref, bias_ref, alog_ref, d_ref, nw_ref, prev_ref, dy_ref,
             dxc_ref, dz_ref, ddtr_ref, dbias_ref, dalog_ref, dd_ref, dnw_ref, dS):
        b, t = pl.program_id(0), pl.program_id(1)

        @pl.when(t == 0)
        def _():
            dS[...] = jnp.zeros_like(dS)

        s = _ssd_prep(nc - 1 - t, dtr_ref, bias_ref, alog_ref, d_ref)
        states = [prev_ref[0, 0, p] for p in range(N_PAIR)]
        y, _, k = _ssd_chunk(xc_ref, s, states)
        xv, Bs, Cs, Xb = k["xv"], k["Bs"], k["Cs"], k["Xb"]

        zz = z_ref[...].astype(F32)
        sz = _sigmoid(zz)
        silu_z = zz * sz
        yg = y * silu_z
        r = _per_group(lambda a: lax.rsqrt(jnp.mean(a * a, axis=-1, keepdims=True) + EPS), yg)
        xhat = yg * r
        dout = dy_ref[...].astype(F32)
        gw = dout * nw_ref[...]
        dyg = r * (gw - xhat * _per_group(lambda a, c2: jnp.mean(a * c2, axis=-1, keepdims=True), gw, xhat))
        dnw = jnp.sum(dout * xhat, axis=0, keepdims=True)
        dz_ref[...] = (dyg * y * _dsilu(zz, sz)).astype(dz_ref.dtype)
        dy = dyg * silu_z
        dy0 = jnp.where(s["first"], dy, 0.0)
        dyb = (dy0.astype(BF16), (dy - dy0).astype(BF16))
        dYo = (dy * s["ecs_w"]).astype(BF16)

        dS_f = [dS[p] for p in range(N_PAIR)]
        dS_b = [d.astype(BF16) for d in dS_f]
        BdS, dXm, dprev, dCs, dMs, XdS = [], [], [], [[] for _ in range(SSD_GROUPS)], [], []
        for p in range(N_PAIR):
            g, ln = p // 2, slice(128 * p, 128 * (p + 1))
            BdS.append(_dot(Bs[g], dS_b[p], NT))
            dXm.append(_dot(k["Mb"][2 * p], dyb[0][:, ln], TN) + _dot(k["Mb"][2 * p + 1], dyb[1][:, ln], TN))
            dprev.append(_dot(dYo[:, ln], Cs[g], TN))
            dCs[g].append(_dot(dYo[:, ln], k["prev_b"][p], NN))
            for hh in range(2):
                dMs.append(_dot(dyb[hh][:, ln], Xb[hh][:, ln], NT))
                XdS.append(_dot(Xb[hh][:, ln], dS_b[p], NN))
        dX = jnp.concatenate(dXm, axis=1) + s["decay_w"] * jnp.concatenate(BdS, axis=1)
        dxs = dy * s["d_w"] + dX * s["dt_w"]

        sums = _dot01(jnp.concatenate([dX * xv, dy * k["yo"] * s["ecs_w"], dy * xv], axis=0), s["spread"], NT, "b")
        ddt, dcs = sums[0:Q], sums[Q:2 * Q]
        dD = jnp.sum(sums[2 * Q:3 * Q], axis=0, keepdims=True)

        col, row = s["col"], s["row"]
        lane1 = col[0:1]
        rowsT = lax.broadcasted_iota(jnp.int32, (128, Q), 0)
        dcs_t = jnp.zeros((128, Q), F32)
        dcl = jnp.zeros((1, 128), F32)
        dB_out, dC_out = [], []
        for g in range(SSD_GROUPS):
            Bf = Bs[g].astype(F32)
            dCB = jnp.zeros((Q, Q), F32)
            dBacc = jnp.zeros((Q, 128), F32)
            for r4 in range(HPG):
                h = HPG * g + r4
                p, hh = h // 2, h % 2
                W = dMs[h] * k["Ms"][h]
                dCB = dCB + dMs[h] * k["Lms"][h]
                decay_h = s["decay_w"][:, SSD_HEAD_DIM * h:SSD_HEAD_DIM * h + 1]
                dBacc = dBacc + decay_h * XdS[h]
                tdec = jnp.sum(XdS[h] * Bf, axis=1, keepdims=True) * decay_h
                dcs = dcs + jnp.where(col == h, jnp.sum(W, axis=1, keepdims=True) - tdec, 0.0)
                dcs_t = dcs_t - jnp.where(rowsT == h, jnp.sum(W, axis=0, keepdims=True), 0.0)
                rows_h = (row < SSD_HEAD_DIM) if hh == 0 else (row >= SSD_HEAD_DIM)
                sprev = jnp.sum(jnp.sum(jnp.where(rows_h, dS_f[p] * states[p], 0.0), axis=1, keepdims=True),
                                axis=0, keepdims=True)
                ecl = jnp.exp(s["cs"][Q - 1:Q, h:h + 1])
                dcl = dcl + jnp.where(lane1 == h, jnp.sum(tdec, axis=0, keepdims=True) + ecl * sprev, 0.0)
            dCB_b = dCB.astype(BF16)
            dC_out.append(dCs[g][0] + dCs[g][1] + _dot(dCB_b, Bs[g], NN))
            dB_out.append(dBacc + _dot(dCB_b, Cs[g], TN))
        for p in range(N_PAIR):
            dS[p] = dS_f[p] * k["ecl_rows"][p] + dprev[p]
        dxc_ref[...] = jnp.concatenate([dxs] + dB_out + dC_out, axis=1).astype(dxc_ref.dtype)

        dcs = dcs + _dot01(s["eye"], dcs_t, NT, "a") + jnp.where(row == Q - 1, dcl, 0.0)
        da = _dot01(row <= col, dcs, NN, "a")
        ddt = ddt + da * s["A"]
        dpre = jnp.where(s["valid"], ddt * _sigmoid(s["pre"]), 0.0)
        ddtr_ref[...] = dpre
        dbias = jnp.sum(dpre, axis=0, keepdims=True)
        dalog = jnp.sum(da * s["dt"], axis=0, keepdims=True) * s["A"]
        first_step = jnp.logical_and(b == 0, t == 0)

        @pl.when(first_step)
        def _():
            dbias_ref[...] = dbias
            dalog_ref[...] = dalog
            dd_ref[...] = dD
            dnw_ref[...] = dnw

        @pl.when(jnp.logical_not(first_step))
        def _():
            dbias_ref[...] += dbias
            dalog_ref[...] += dalog
            dd_ref[...] += dD
            dnw_ref[...] += dnw

    rb = lambda b, c: b * nc + nc - 1 - c
    rowblk = lambda w: pl.BlockSpec((Q, w), lambda b, c: (rb(b, c), 0))
    vec = lambda w: pl.BlockSpec((1, w), lambda b, c: (0, 0))
    return _call(
        body, name="ssd_bwd", grid=(Bl, nc),
        in_specs=_ssd_in_specs(nc, rev=True) + [
            pl.BlockSpec((1, 1, N_PAIR, 128, 128), lambda b, c: (b, nc - 1 - c, 0, 0, 0)), rowblk(SSD_INNER)],
        out_specs=[rowblk(SSD_CONV_CH), rowblk(SSD_INNER), rowblk(128), vec(128), vec(128), vec(128), vec(SSD_INNER)],
        out_shape=[jax.ShapeDtypeStruct((M, SSD_CONV_CH), BF16), jax.ShapeDtypeStruct(dproj.shape, BF16),
                   jax.ShapeDtypeStruct((M, 128), F32), jax.ShapeDtypeStruct((1, 128), F32),
                   jax.ShapeDtypeStruct((1, 128), F32), jax.ShapeDtypeStruct((1, 128), F32),
                   jax.ShapeDtypeStruct((1, SSD_INNER), F32)],
        scratch=[pltpu.VMEM((N_PAIR, 128, 128), F32)], sem=("arbitrary", "arbitrary"),
        args=(xc, dtr, proj, bias_p, alog_p, d_p, nw, prev, dya), comm=comm, into=(dproj, 1))


NSUB = Q // HG_CHUNK
HG_HP = 8
EXP_CAP = 80.0


def _hg_setup(blk, q_ref, f_ref, hb_ref):
    row = lax.broadcasted_iota(jnp.int32, (Q, Q), 0)
    col = lax.broadcasted_iota(jnp.int32, (Q, Q), 1)
    same = (row // HG_CHUNK) == (col // HG_CHUNK)
    causal = jnp.logical_and(same, col <= row)
    lb = _sigmoid(hb_ref[0:1, :] - hb_ref[1:2, :])
    fl = f_ref[...].astype(F32)
    sg = _sigmoid(fl)
    fg = lb + (1.0 - lb) * sg
    k = (1.0 - lb) * (1.0 - sg)
    gl = jnp.log(fg)
    G = _dot01(causal, gl, NN, "a")
    T = _dot01(same, gl, NN, "a")
    qv = q_ref[...].astype(F32)
    sq = _sigmoid(qv)
    eG = jnp.exp(G)
    eGn = jnp.exp(jnp.minimum(-G, EXP_CAP))
    eTG = jnp.exp(T - G)
    qt = qv * sq * eG
    kt = k * eGn
    kh = k * eTG
    valid = jnp.logical_or(blk > 0, row[:, :1] >= PAD)
    return dict(row=row, col=col, same=same, causal=causal, lb=lb, sg=sg, fg=fg, k=k, T=T, qv=qv, sq=sq,
                eG=eG, eGn=eGn, eTG=eTG, qt=qt, kt=kt, kh=kh, valid=valid)


def _hg_specs(nb, rev=False):
    rb = (lambda h, b, t: b * nb + nb - 1 - t) if rev else (lambda h, b, t: b * nb + t)
    w = 128 * HG_HP
    blk = lambda off: pl.BlockSpec((Q, w), lambda h, b, t, off=off: (rb(h, b, t), off // HG_HP + h))
    return [blk(24), blk(32), blk(40), blk(48),
            pl.BlockSpec((2, w), lambda h, b, t: (0, h)), pl.BlockSpec((1, w), lambda h, b, t: (0, h))]


HEAD_LANES = tuple(slice(128 * hh, 128 * (hh + 1)) for hh in range(HG_HP))


def _per_head(fn, *arrs):
    return jnp.concatenate([jnp.broadcast_to(fn(*(a[:, ln] for a in arrs)), (arrs[0].shape[0], 128))
                            for ln in HEAD_LANES], axis=1)


def _hgrn_fwd(proj, hb, nw, Bl, nb, comm=None):
    M = proj.shape[0]

    def body(q_ref, f_ref, i_ref, g_ref, hb_ref, nw_ref, y_ref, o_ref, st_ref, S):
        blk = pl.program_id(2)

        @pl.when(blk == 0)
        def _():
            S[...] = jnp.zeros_like(S)

        s = _hg_setup(blk, q_ref, f_ref, hb_ref)
        v = i_ref[...]
        qt_b, kt_b, kh_b = s["qt"].astype(BF16), s["kt"].astype(BF16), s["kh"].astype(BF16)
        eT = jnp.exp(s["T"])
        att = [jnp.where(s["causal"], _dot(qt_b[:, ln], kt_b[:, ln], NT), 0.0).astype(BF16) for ln in HEAD_LANES]
        o_intra = [_dot(att[hh], v[:, ln], NN) for hh, ln in enumerate(HEAD_LANES)]
        for j in range(NSUB):
            sl = slice(HG_CHUNK * j, HG_CHUNK * (j + 1))
            for hh, ln in enumerate(HEAD_LANES):
                St = S[hh]
                st_ref[0, hh, 0, j] = St
                o_ref[sl, ln] = o_intra[hh][sl] + _dot(qt_b[sl, ln], St.astype(BF16), NT)
                S[hh] = St * eT[HG_CHUNK * j:HG_CHUNK * j + 1, ln] + _dot(v[sl, ln], kh_b[sl, ln], TN)
        o = o_ref[...]
        r = _per_head(lambda a: lax.rsqrt(jnp.mean(a * a, axis=-1, keepdims=True) + EPS), o)
        gv = g_ref[...].astype(F32)
        y_ref[...] = (o * r * nw_ref[...] * gv * _sigmoid(gv)).astype(y_ref.dtype)

    rowblk = pl.BlockSpec((Q, 128 * HG_HP), lambda h, b, t: (b * nb + t, h))
    return _call(
        body, name="hgrn_fwd", grid=(HG_HEADS // HG_HP, Bl, nb), in_specs=_hg_specs(nb),
        out_specs=[rowblk, rowblk,
                   pl.BlockSpec((1, HG_HP, 1, NSUB, 128, 128), lambda h, b, t: (b, h, t, 0, 0, 0))],
        out_shape=[jax.ShapeDtypeStruct((M, HG_WIDTH), BF16), jax.ShapeDtypeStruct((M, HG_WIDTH), F32),
                   jax.ShapeDtypeStruct((Bl, HG_HEADS, nb, NSUB, 128, 128), F32)],
        scratch=[pltpu.VMEM((HG_HP, 128, 128), F32)], sem=("parallel", "arbitrary", "arbitrary"),
        args=(proj, proj, proj, proj, hb, nw), comm=comm)


def _hgrn_bwd(proj, hb, nw, o_saved, st_saved, dyb, dproj, Bl, nb, comm=None):
    assert HG_HP == HG_HEADS

    def body(q_ref, f_ref, i_ref, g_ref, hb_ref, nw_ref, o_ref, st_ref, dy_ref,
             d_ref, dhb_ref, dnw_ref, dS, a_dqt, a_dv, a_dkh, a_dgl):
        b, t = pl.program_id(1), pl.program_id(2)

        @pl.when(t == 0)
        def _():
            dS[...] = jnp.zeros_like(dS)

        first_step = jnp.logical_and(b == 0, t == 0)
        s = _hg_setup(nb - 1 - t, q_ref, f_ref, hb_ref)
        v = i_ref[...]
        qt_b, kt_b, kh_b = s["qt"].astype(BF16), s["kt"].astype(BF16), s["kh"].astype(BF16)
        eT = jnp.exp(s["T"])
        att = [jnp.where(s["causal"], _dot(qt_b[:, ln], kt_b[:, ln], NT), 0.0).astype(BF16) for ln in HEAD_LANES]

        o = o_ref[...]
        r = _per_head(lambda a: lax.rsqrt(jnp.mean(a * a, axis=-1, keepdims=True) + EPS), o)
        xhat = o * r
        gv = g_ref[...].astype(F32)
        sgv = _sigmoid(gv)
        dyv = dy_ref[...].astype(F32)
        d_on = dyv * gv * sgv
        dg_out = dyv * xhat * nw_ref[...] * _dsilu(gv, sgv)
        gw = d_on * nw_ref[...]
        do = r * (gw - xhat * _per_head(lambda a, c: jnp.mean(a * c, axis=-1, keepdims=True), gw, xhat))
        dnw = jnp.sum(d_on * xhat, axis=0, keepdims=True)
        do_b = do.astype(BF16)

        datt = [jnp.where(s["causal"], _dot(do_b[:, ln], v[:, ln], NT), 0.0).astype(BF16) for ln in HEAD_LANES]
        dqt = jnp.concatenate([_dot(datt[hh], kt_b[:, ln], NN) for hh, ln in enumerate(HEAD_LANES)], axis=1)
        dkt = jnp.concatenate([_dot(datt[hh], qt_b[:, ln], TN) for hh, ln in enumerate(HEAD_LANES)], axis=1)
        dv = jnp.concatenate([_dot(att[hh], do_b[:, ln], TN) for hh, ln in enumerate(HEAD_LANES)], axis=1)
        last_row = (lax.broadcasted_iota(jnp.int32, (HG_CHUNK, 128), 0) == HG_CHUNK - 1)
        for j in reversed(range(NSUB)):
            sl = slice(HG_CHUNK * j, HG_CHUNK * (j + 1))
            for hh, ln in enumerate(HEAD_LANES):
                St = st_ref[0, hh, 0, j]
                dSt = dS[hh]
                St_b, dSt_b = St.astype(BF16), dSt.astype(BF16)
                eT_j = eT[HG_CHUNK * j:HG_CHUNK * j + 1, ln]
                dkh_j = _dot(v[sl, ln], dSt_b, NN)
                a_dqt[sl, ln] = _dot(do_b[sl, ln], St_b, NN)
                a_dv[sl, ln] = _dot(kh_b[sl, ln], dSt_b, NT)
                a_dkh[sl, ln] = dkh_j
                dlast = (jnp.sum(St * dSt, axis=0, keepdims=True) * eT_j
                         + jnp.sum(dkh_j * s["kh"][sl, ln], axis=0, keepdims=True))
                a_dgl[sl, ln] = jnp.where(last_row, dlast, 0.0)
                dS[hh] = dSt * eT_j + _dot(do_b[sl, ln], qt_b[sl, ln], TN)
        dqt = dqt + a_dqt[...]
        dv = dv + a_dv[...]
        dkh = a_dkh[...]
        dG = dqt * s["qt"] - dkt * s["kt"] - dkh * s["kh"] + a_dgl[...]
        rev_causal = jnp.logical_and(s["same"], s["col"] >= s["row"])
        dgl = _dot01(rev_causal, dG, NN, "a")
        dk = dkt * s["eGn"] + dkh * s["eTG"]
        dfg = dgl / s["fg"] - dk
        lb, sg = s["lb"], s["sg"]
        keep = s["valid"].astype(F32)
        d_ref[:, 0:w] = (dqt * s["eG"] * _dsilu(s["qv"], s["sq"]) * keep).astype(d_ref.dtype)
        d_ref[:, w:2 * w] = (dfg * (1.0 - lb) * sg * (1.0 - sg) * keep).astype(d_ref.dtype)
        d_ref[:, 2 * w:3 * w] = (dv * keep).astype(d_ref.dtype)
        d_ref[:, 3 * w:4 * w] = (dg_out * keep).astype(d_ref.dtype)
        dlb = jnp.sum(dfg * (1.0 - sg) * keep, axis=0, keepdims=True) * lb * (1.0 - lb)
        dhb = jnp.concatenate([dlb, -dlb], axis=0)

        @pl.when(first_step)
        def _():
            dhb_ref[...] = dhb
            dnw_ref[...] = dnw

        @pl.when(jnp.logical_not(first_step))
        def _():
            dhb_ref[...] += dhb
            dnw_ref[...] += dnw

    w = 128 * HG_HP
    rowblk = pl.BlockSpec((Q, w), lambda h, b, t: (b * nb + nb - 1 - t, h))
    return _call(
        body, name="hgrn_bwd", grid=(HG_HEADS // HG_HP, Bl, nb),
        in_specs=_hg_specs(nb, rev=True) + [
            rowblk, pl.BlockSpec((1, HG_HP, 1, NSUB, 128, 128), lambda h, b, t: (b, h, nb - 1 - t, 0, 0, 0)), rowblk],
        out_specs=[pl.BlockSpec((pl.Element(Q), pl.Element(4 * w)),
                                lambda h, b, t: (pl.multiple_of((b * nb + nb - 1 - t) * Q, Q), 3 * HG_WIDTH)),
                   pl.BlockSpec((2, w), lambda h, b, t: (0, h)), pl.BlockSpec((1, w), lambda h, b, t: (0, h))],
        out_shape=[jax.ShapeDtypeStruct(dproj.shape, BF16),
                   jax.ShapeDtypeStruct((2, HG_WIDTH), F32), jax.ShapeDtypeStruct((1, HG_WIDTH), F32)],
        scratch=[pltpu.VMEM((HG_HP, 128, 128), F32)] + [pltpu.VMEM((Q, w), F32)] * 4,
        sem=("parallel", "arbitrary", "arbitrary"),
        args=(proj, proj, proj, proj, hb, nw, o_saved, st_saved, dyb), comm=comm, into=(dproj, 0))


def _adamw(name, parts, w, m, v, comm=None):
    R, C = w.shape
    S = parts.shape[0]
    tr, tc = (_tile(R, (256, 176, 128, 64, 8)), C) if R % 8 == 0 else (R, 256)
    c1, c2 = 1.0 - ADAM_B1 ** ADAM_STEP, 1.0 - ADAM_B2 ** ADAM_STEP

    def body(p_ref, w_ref, m_ref, v_ref, g_ref, d_ref, nm_ref, nv_ref):
        g = p_ref[0].astype(F32)
        for s in range(1, S):
            g = g + p_ref[s].astype(F32)
        nm = ADAM_B1 * m_ref[...] + (1.0 - ADAM_B1) * g
        nv = ADAM_B2 * v_ref[...] + (1.0 - ADAM_B2) * (g * g)
        g_ref[...] = g
        nm_ref[...] = nm
        nv_ref[...] = nv
        d_ref[...] = -ADAM_LR * ((nm / c1) / (jnp.sqrt(nv / c2) + ADAM_EPS) + ADAM_WD * w_ref[...])

    blk = pl.BlockSpec((tr, tc), lambda i, j: (i, j))
    return _call(
        body, name=name, grid=(R // tr, C // tc),
        in_specs=[pl.BlockSpec((S, tr, tc), lambda i, j: (0, i, j)), blk, blk, blk], out_specs=[blk] * 4,
        out_shape=[jax.ShapeDtypeStruct((R, C), F32)] * 4, scratch=[], sem=("parallel", "parallel"),
        args=(parts, w, m, v), comm=comm)


def _pair_sum(name, by_core, arrived):
    _, J, R, C = by_core.shape
    tc = _tile(C, (512, 256, 128))

    def body(c_ref, a_ref, b_ref, o_ref):
        o_ref[...] = (a_ref[0].astype(F32) + b_ref[...].astype(F32)).astype(o_ref.dtype)

    blk = pl.BlockSpec((1, R, tc), lambda j, k, c_ref: (j, 0, k))
    return pl.pallas_call(
        body, name=name,
        grid_spec=pltpu.PrefetchScalarGridSpec(
            num_scalar_prefetch=1, grid=(J, C // tc),
            in_specs=[pl.BlockSpec((1, 1, R, tc), lambda j, k, c_ref: (c_ref[0], j, 0, k)), blk], out_specs=blk),
        out_shape=jax.ShapeDtypeStruct(arrived.shape, arrived.dtype), compiler_params=_params(("parallel", "parallel")),
    )(lax.axis_index("c").astype(jnp.int32).reshape(1), by_core, arrived)


def _sum_parts(name, parts):
    S, R, C = parts.shape

    def body(p_ref, o_ref):
        g = p_ref[0]
        for s in range(1, S):
            g = g + p_ref[s]
        o_ref[...] = g

    return pl.pallas_call(
        body, name=name, out_shape=jax.ShapeDtypeStruct((R, C), F32),
        in_specs=[pl.BlockSpec(memory_space=pltpu.VMEM)], out_specs=pl.BlockSpec(memory_space=pltpu.VMEM),
    )(parts)


def _heads_to_lanes(p):
    return jnp.pad(p, [(0, 0)] * (p.ndim - 1) + [(0, 128 - SSD_HEADS)])


def _lanes_to_heads(p):
    return p[..., :SSD_HEADS]


def _pack_rows(arrs):
    flat = jnp.concatenate([a.reshape(-1).astype(F32) for a in arrs])
    return jnp.pad(flat, (0, (-flat.shape[0]) % (8 * D_MODEL))).reshape(-1, D_MODEL)


def _unpack_rows(packed, like):
    flat, outs, at = packed.reshape(-1), [], 0
    for a in like:
        outs.append(flat[at:at + a.size].reshape(a.shape))
        at += a.size
    return outs


def _cols(gth):
    return jnp.transpose(gth, (1, 0, 2)).reshape(gth.shape[1], -1)


def _rows(gth):
    return gth.reshape(-1, gth.shape[2])


def _to_rows(g):
    return g.reshape(N_DEV, -1, g.shape[1]).astype(BF16)


def _by_core(g):
    return jnp.transpose(g.reshape(N_DEV // 2, 2, -1, g.shape[1]), (1, 0, 2, 3)).astype(BF16)


DT_ROW = 3072


def _chip_sums(tag, by_core, swap_in=None):
    arrived = swap_in(by_core) if swap_in else _exchange(tag + "_swap", "swap", by_core)
    return [_pair_sum(f"{tag}_chipsum{i}", m, a) for i, (m, a) in enumerate(zip(by_core, arrived))]


def _ffn_fwd_gu(tag, n, w_gu_t, comm=None):
    M = n.shape[0]
    F = w_gu_t.shape[0] // 2
    tm = _tile(M, (544, 256))
    outs = _fused_matmul(
        tag + "_gu", M, F, D_MODEL,
        [dict(a=n, b=w_gu_t, trans_b=True, acc=0, resident=True),
         dict(a=n, b=w_gu_t, trans_b=True, bn_off=1, acc=1, resident=True)], [],
        lambda accs, ex: (accs[0], accs[1], accs[0] * _sigmoid(accs[0]) * accs[1]),
        [BF16, BF16, BF16], 2, tm, F, D_MODEL, outer="i", comm=comm, sub=256)
    return (n, *outs[:3]), outs[3:]


def _rmsnorm_tile(x, w):
    return x * lax.rsqrt(jnp.mean(x * x, axis=-1, keepdims=True) + EPS) * w


def _ffn_fwd_down(tag, h, a, w_down, next_norm=None, comm=None):
    M = h.shape[0]
    F = w_down.shape[0]
    tm = _tile(M, (1088, 544, 256))
    if next_norm is None:
        (h_out,) = _fused_matmul(
            tag + "_down", M, D_MODEL, F, [dict(a=a, b=w_down, acc=0)], [(h, 0)],
            lambda accs, ex: (ex[0] + 0.5 * accs[0],), [F32], 1, tm, D_MODEL, F, outer="j", sub=256)
        return h_out

    def with_norm(accs, ex):
        h_new = ex[0] + 0.5 * accs[0]
        return h_new, _rmsnorm_tile(h_new, ex[1])

    return _fused_matmul(tag + "_down", M, D_MODEL, F, [dict(a=a, b=w_down, acc=0, resident=True)], [(h, 0)], with_norm,
                         [F32, BF16], 1, tm, D_MODEL, F, outer="j", vecs=[next_norm], comm=comm)


def _ffn_bwd(tag, dh, dh_b, h, norm_w, w_gu_t, w_down, saved, scatter=False):
    n, g, u, a = saved
    M = h.shape[0]
    F = w_down.shape[0]
    tm = _tile(M, (544, 256))
    tn = _tile(F, (1408, 704, 256))

    def swiglu_bwd(accs, ex):
        da, gv, uv = 0.5 * accs[0], ex[0].astype(F32), ex[1].astype(F32)
        s = _sigmoid(gv)
        return da * uv * _dsilu(gv, s), da * gv * s

    (dgu,) = _fused_matmul(
        tag + "_dact", M, F, D_MODEL, [dict(a=dh_b, b=w_down, trans_b=True, acc=0, resident=True)], [(g, 0), (u, 0)],
        swiglu_bwd, [BF16, BF16], 1, tm, F, D_MODEL, outer="i", stack=True, sub=256)
    tr = _tile(M, (2176, 256))
    (dw_down,) = _matmul_tn(tag + "_dwd", a, dh_b, tn, D_MODEL, tr, scale=0.5)
    dw_gu_t, *p_down = _matmul_tn(tag + "_dwgu", dgu, n, tn, D_MODEL, tr,
                                  comm=("scatter", [_to_rows(dw_down)]) if scatter else None)
    comm = None
    if scatter:
        comm = ("chips", _chip_sums(tag + "_wgu", [_by_core(dw_gu_t)]))
    def norm_bwd(accs, ex):
        dh_prev, dw = _rmsnorm_bwd_tile(accs[0], ex[0], ex[2], ex[1])
        return dh_prev, dh_prev, dw

    dh_prev, dh_prev_b, dnorm, *p_gu = _fused_matmul(
        tag + "_dn", M, D_MODEL, 2 * F,
        [dict(a=dgu, b=w_gu_t, acc=0, resident=True)], [(h, 0), (dh, 0)],
        norm_bwd, [F32, BF16], 1, tm, D_MODEL, 2 * F, outer="i", comm=comm, vecs=[norm_w], row_sums=1)
    return (dh_prev, dh_prev_b, dnorm, *((p_gu[0], p_down[0]) if scatter else (dw_gu_t, dw_down)))


def kernel(x, meta_tokens, ffn1_norm, ffn1_w_gu, ffn1_w_down, mix_norm, w_in, ssd_conv_w, ssd_conv_b, ssd_dt_bias, ssd_a_log, ssd_d, ssd_norm, hg_lower_bound, hg_norm, w_branch_a, w_branch_b, w_out, ffn2_norm, ffn2_w_gu, ffn2_w_down, final_norm, loss_target, m_meta_tokens, m_ffn1_norm, m_ffn1_w_gu, m_ffn1_w_down, m_mix_norm, m_w_in, m_ssd_conv_w, m_ssd_conv_b, m_ssd_dt_bias, m_ssd_a_log, m_ssd_d, m_ssd_norm, m_hg_lower_bound, m_hg_norm, m_w_branch_a, m_w_branch_b, m_w_out, m_ffn2_norm, m_ffn2_w_gu, m_ffn2_w_down, m_final_norm, v_meta_tokens, v_ffn1_norm, v_ffn1_w_gu, v_ffn1_w_down, v_mix_norm, v_w_in, v_ssd_conv_w, v_ssd_conv_b, v_ssd_dt_bias, v_ssd_a_log, v_ssd_d, v_ssd_norm, v_hg_lower_bound, v_hg_norm, v_w_branch_a, v_w_branch_b, v_w_out, v_ffn2_norm, v_ffn2_w_gu, v_ffn2_w_down, v_final_norm):
    Bl, S, D = x.shape
    T = PAD + N_META + S
    nc = T // Q
    M = Bl * T
    me = 4 * lax.axis_index("x") + 2 * lax.axis_index("y") + lax.axis_index("c")

    bf = lambda a: a[0].astype(BF16)
    bft = lambda a: a[0].T.astype(BF16)
    bias_p, alog_p, d_p = _heads_to_lanes(ssd_dt_bias), _heads_to_lanes(ssd_a_log), _heads_to_lanes(ssd_d)
    final_w = final_norm.reshape(1, D)

    h0, n1, g_wgu1, g_meta, g_conv_w = _embed_norm(
        x, ffn1_norm, comm=("gather", [bft(ffn1_w_gu), meta_tokens, ssd_conv_w[0]]))
    wgu1, meta_full, conv_w_full = _rows(g_wgu1), _cols(g_meta), _cols(g_conv_w)
    h0, n1 = _embed_meta(meta_full, ffn1_norm, h0, n1, Bl)
    tm = _tile(M, (1088, 544, 256))
    win_shard = bft(w_in)
    cut = (win_shard.shape[0] // 32) * 16
    ffn1_saved, (g_wd1, g_win_a) = _ffn_fwd_gu("ffn1", n1, wgu1, comm=("gather", [bf(ffn1_w_down), win_shard[:cut]]))
    wd1 = _rows(g_wd1)
    h1, un, g_win_b = _ffn_fwd_down("ffn1", h0, ffn1_saved[3], wd1, next_norm=mix_norm,
                                    comm=("gather", [win_shard[cut:]]))
    win_t = _rows(jnp.concatenate([g_win_a, g_win_b], axis=1))
    win_dt = jnp.pad(win_t[DT_ROW:DT_ROW + SSD_HEADS], ((0, 128 - SSD_HEADS), (0, 0)))
    plain = lambda accs, ex: (accs[0],)
    proj, g_wa, g_wb, g_wo = _fused_matmul(
        "in_proj", M, N_MAIN, D, [dict(a=un, b=win_t, trans_b=True, acc=0, b_shift=(DT_ROW // 3072, SSD_HEADS))], [],
        plain, [BF16], 1, tm, 3072, D,
        outer="j", comm=("gather", [bf(w_branch_a), bf(w_branch_b), bf(w_out)]), sub=512)
    wa, wb, wo = _rows(g_wa), _rows(g_wb), _rows(g_wo)
    (dtr,) = _fused_matmul("in_proj_dt", M, 128, D, [dict(a=un, b=win_dt, trans_b=True, acc=0)], [], plain, [F32], 1,
                           tm, 128, D, outer="j")
    xc = _conv_fwd(proj, conv_w_full, ssd_conv_b, Bl, T)
    ya, ssd_prev = _ssd_fwd(xc, dtr, proj, bias_p, alog_p, d_p, ssd_norm, Bl, nc)
    yb, hg_o, hg_st, g_wgu2, g_wd2 = _hgrn_fwd(proj, hg_lower_bound, hg_norm, Bl, nc,
                                               comm=("gather", [bft(ffn2_w_gu), bf(ffn2_w_down)]))
    wgu2, wd2 = _rows(g_wgu2), _rows(g_wd2)

    def branch_fwd(accs, ex):
        pa, pb = accs
        return pa, pb, _sigmoid(ex[0].astype(F32)) * pa + _sigmoid(ex[1].astype(F32)) * pb

    pa, pb, merged = _fused_matmul(
        "branches", M, D, D, [dict(a=ya, b=wa, acc=0), dict(a=yb, b=wb, acc=1)], [(proj, 7), (proj, 8)],
        branch_fwd, [BF16, BF16, BF16], 2, tm, D, D, outer="j")
    def out_with_norm(accs, ex):
        h_new = ex[0] + accs[0]
        return h_new, _rmsnorm_tile(h_new, ex[1])

    h2, n2 = _fused_matmul("out_proj", M, D, D, [dict(a=merged, b=wo, acc=0)], [(h1, 0)], out_with_norm,
                           [F32, BF16], 1, tm, D, D, outer="j", vecs=[ffn2_norm])
    ffn2_saved, _ = _ffn_fwd_gu("ffn2", n2, wgu2)
    h3 = _ffn_fwd_down("ffn2", h2, ffn2_saved[3], wd2)

    dh3, dh3_b, d_final, loss_part = _loss_head(h3, final_w, loss_target, Bl, nc)
    dh2, dh2_b, d_ffn2_norm, d_wgu2, d_wd2 = _ffn_bwd("ffn2", dh3, dh3_b, h2, ffn2_norm, wgu2, wd2, ffn2_saved)

    def branch_bwd(accs, ex):
        dm = accs[0]
        ga, gb, pav, pbv = (e.astype(F32) for e in ex)
        sa, sb = _sigmoid(ga), _sigmoid(gb)
        return (dm * sa, dm * sb,
                jnp.concatenate([dm * pav * sa * (1.0 - sa), dm * pbv * sb * (1.0 - sb)], axis=1))

    d_merged_outs = []

    def d_merged_with_swap(theirs):
        d_merged_outs.extend(_fused_matmul(
            "d_merged", M, D, D, [dict(a=dh2_b, b=wo, trans_b=True, acc=0)], [(proj, 7), (proj, 8), (pa, 0), (pb, 0)],
            branch_bwd, [BF16] * 2, 1, tm, D, D, outer="j", comm=("swap", theirs),
            wide=dict(width=2 * D, col=7 * D, total=N_MAIN, dtype=BF16)))
        return d_merged_outs[3:]

    s_ffn2 = _chip_sums("ffn2", [_by_core(d_wgu2), _by_core(d_wd2)], swap_in=d_merged_with_swap)
    dpa, dpb, dproj = d_merged_outs[:3]
    (d_wo,) = _matmul_tn("d_w_out", merged, dh2_b, 512, D, M)
    (d_wa,) = _matmul_tn("d_w_a", ya, dpa, 512, D, M)
    (d_wb,) = _matmul_tn("d_w_b", yb, dpb, 512, D, M)
    dya, dyb = _fused_matmul(
        "d_branches", M, D, D, [dict(a=dpa, b=wa, trans_b=True, acc=0), dict(a=dpb, b=wb, trans_b=True, acc=1)], [],
        lambda accs, ex: (accs[0], accs[1]), [BF16, BF16], 2, tm, D, D, outer="j")
    *ssd_grads, p_wgu2, p_wd2 = _ssd_bwd(xc, dtr, proj, bias_p, alog_p, d_p, ssd_norm, ssd_prev, dya, dproj, Bl, nc,
                                         comm=("chips", s_ffn2))
    dxc, dproj, ddtr, d_bias_p, d_alog_p, d_d_p, d_ssd_norm = ssd_grads
    dproj, d_conv_w, d_conv_b = _conv_bwd(proj, conv_w_full, ssd_conv_b, dxc, dproj, Bl, T)
    dproj, d_hb, d_hg_norm, p_wa, p_wb, p_wo = _hgrn_bwd(
        proj, hg_lower_bound, hg_norm, hg_o, hg_st, dyb, dproj, Bl, nc,
        comm=("scatter", [_to_rows(d_wa), _to_rows(d_wb), _to_rows(d_wo)]))
    ddtr_b = ddtr.astype(BF16)
    (d_win_t,) = _matmul_tn("d_w_in", dproj, un, 768, D, M, out_skip=(DT_ROW, SSD_HEADS))
    (d_win_dt,) = _matmul_tn("d_w_in_dt", ddtr_b, un, 128, D, M)
    d_win_t = lax.dynamic_update_slice(d_win_t, d_win_dt[:SSD_HEADS], (DT_ROW, 0))
    d_un_dt_outs = []

    def d_un_dt_with_swap(theirs):
        d_un_dt_outs.extend(_fused_matmul("d_un_dt", M, D, 128, [dict(a=ddtr_b, b=win_dt, acc=0)], [], plain, [F32], 1,
                                          tm, D, 128, outer="j", comm=("swap", theirs)))
        return d_un_dt_outs[1:]

    s_win = _chip_sums("w_in", [_by_core(d_win_t)], swap_in=d_un_dt_with_swap)
    def mix_norm_bwd(accs, ex):
        dh, dw = _rmsnorm_bwd_tile(accs[0] + ex[0], ex[1], ex[3], ex[2])
        return dh, dh, dw

    dh1, dh1_b, d_mix_norm, p_win = _fused_matmul(
        "d_un", M, D, N_MAIN, [dict(a=dproj, b=win_t, acc=0, b_shift=(DT_ROW // 3072, SSD_HEADS))],
        [(d_un_dt_outs[0], 0), (h1, 0), (dh2, 0)],
        mix_norm_bwd, [F32, BF16], 1, _tile(M, (544, 256)), D, 3072, outer="i", comm=("chips", s_win),
        vecs=[mix_norm], row_sums=1)
    dh0, _, d_ffn1_norm, p_wgu1, p_wd1 = _ffn_bwd("ffn1", dh1, dh1_b, h0, ffn1_norm, wgu1, wd1, ffn1_saved, scatter=True)

    dh0 = dh0.reshape(Bl, T, D)
    grad_x = dh0[:, PAD + N_META:]
    d_meta = dh0[:, PAD:PAD + N_META]

    small_grads = [d_ffn1_norm, d_mix_norm, d_conv_b, _lanes_to_heads(d_bias_p), _lanes_to_heads(d_alog_p),
                   _lanes_to_heads(d_d_p), d_ssd_norm, d_hb, d_hg_norm, d_ffn2_norm, d_final.reshape(D), d_conv_w]
    small_like = small_grads + [d_meta[b] for b in range(Bl)] + [loss_part[0, 0:1]]
    small_packed = _pack_rows(small_like)
    parts = [p_wgu1, p_wd1, p_win, p_wa, p_wb, p_wo, p_wgu2, p_wd2]

    names = ["meta_tokens", "ffn1_norm", "ffn1_w_gu", "ffn1_w_down", "mix_norm", "w_in", "ssd_conv_w", "ssd_conv_b",
             "ssd_dt_bias", "ssd_a_log", "ssd_d", "ssd_norm", "hg_lower_bound", "hg_norm", "w_branch_a", "w_branch_b",
             "w_out", "ffn2_norm", "ffn2_w_gu", "ffn2_w_down", "final_norm"]
    W = dict(meta_tokens=meta_tokens, ffn1_norm=ffn1_norm, ffn1_w_gu=ffn1_w_gu, ffn1_w_down=ffn1_w_down, mix_norm=mix_norm,
             w_in=w_in, ssd_conv_w=ssd_conv_w, ssd_conv_b=ssd_conv_b, ssd_dt_bias=ssd_dt_bias, ssd_a_log=ssd_a_log,
             ssd_d=ssd_d, ssd_norm=ssd_norm, hg_lower_bound=hg_lower_bound, hg_norm=hg_norm, w_branch_a=w_branch_a,
             w_branch_b=w_branch_b, w_out=w_out, ffn2_norm=ffn2_norm, ffn2_w_gu=ffn2_w_gu, ffn2_w_down=ffn2_w_down,
             final_norm=final_norm)
    Mo = dict(meta_tokens=m_meta_tokens, ffn1_norm=m_ffn1_norm, ffn1_w_gu=m_ffn1_w_gu, ffn1_w_down=m_ffn1_w_down,
              mix_norm=m_mix_norm, w_in=m_w_in, ssd_conv_w=m_ssd_conv_w, ssd_conv_b=m_ssd_conv_b, ssd_dt_bias=m_ssd_dt_bias,
              ssd_a_log=m_ssd_a_log, ssd_d=m_ssd_d, ssd_norm=m_ssd_norm, hg_lower_bound=m_hg_lower_bound, hg_norm=m_hg_norm,
              w_branch_a=m_w_branch_a, w_branch_b=m_w_branch_b, w_out=m_w_out, ffn2_norm=m_ffn2_norm, ffn2_w_gu=m_ffn2_w_gu,
              ffn2_w_down=m_ffn2_w_down, final_norm=m_final_norm)
    Vo = dict(meta_tokens=v_meta_tokens, ffn1_norm=v_ffn1_norm, ffn1_w_gu=v_ffn1_w_gu, ffn1_w_down=v_ffn1_w_down,
              mix_norm=v_mix_norm, w_in=v_w_in, ssd_conv_w=v_ssd_conv_w, ssd_conv_b=v_ssd_conv_b, ssd_dt_bias=v_ssd_dt_bias,
              ssd_a_log=v_ssd_a_log, ssd_d=v_ssd_d, ssd_norm=v_ssd_norm, hg_lower_bound=v_hg_lower_bound, hg_norm=v_hg_norm,
              w_branch_a=v_w_branch_a, w_branch_b=v_w_branch_b, w_out=v_w_out, ffn2_norm=v_ffn2_norm, ffn2_w_gu=v_ffn2_w_gu,
              ffn2_w_down=v_ffn2_w_down, final_norm=v_final_norm)
    grads, deltas, new_m, new_v = {}, {}, {}, {}
    big_names = ["ffn1_w_gu", "ffn1_w_down", "w_in", "w_branch_a", "w_branch_b", "w_out", "ffn2_w_gu", "ffn2_w_down"]
    transposed = ("ffn1_w_gu", "ffn2_w_gu", "w_in")
    small_all = None
    for nm, part in zip(big_names, parts):
        view = (lambda a: a[0].T) if nm in transposed else (lambda a: a[0])
        back = (lambda o: o.T[None]) if nm in transposed else (lambda o: o[None])
        outs = _adamw("adamw_" + nm, part, view(W[nm]), view(Mo[nm]), view(Vo[nm]),
                      comm=("gather", [small_packed]) if small_all is None else None)
        if small_all is None:
            small_all = outs[4]
        grads[nm], deltas[nm], new_m[nm], new_v[nm] = (back(o) for o in outs[:4])
    unpacked = _unpack_rows(_sum_parts("sum_small_grads", small_all), small_like)
    g_small = unpacked[:len(small_grads)]
    g_meta_full = unpacked[len(small_grads)]
    for b in range(1, Bl):
        g_meta_full = g_meta_full + unpacked[len(small_grads) + b]
    g_meta = lax.dynamic_slice_in_dim(g_meta_full, me * (D // N_DEV), D // N_DEV, axis=1)
    g_conv_w = lax.dynamic_slice_in_dim(g_small[11], me * (SSD_CONV_CH // N_DEV), SSD_CONV_CH // N_DEV, axis=1)
    loss = unpacked[-1].reshape(())
    small_names = ["ffn1_norm", "mix_norm", "ssd_conv_b", "ssd_dt_bias", "ssd_a_log", "ssd_d", "ssd_norm", "hg_lower_bound",
                   "hg_norm", "ffn2_norm", "final_norm", "ssd_conv_w", "meta_tokens"]
    small_g = g_small[:11] + [g_conv_w.reshape(ssd_conv_w.shape), g_meta]
    pk = lambda d: _pack_rows([d[nm] for nm in small_names])
    outs = _adamw("adamw_small", _pack_rows(small_g)[None], pk(W), pk(Mo), pk(Vo))
    like = [W[nm] for nm in small_names]
    for dst, o in zip((grads, deltas, new_m, new_v), outs):
        for nm, val in zip(small_names, _unpack_rows(o, like)):
            dst[nm] = val

    return (loss, grad_x, *[grads[nm] for nm in names], *[deltas[nm] for nm in names],
            *[new_m[nm] for nm in names], *[new_v[nm] for nm in names])
```

```python
import functools

import jax
import jax.numpy as jnp
from jax import lax
from jax.experimental import pallas as pl
from jax.experimental.pallas import tpu as pltpu

F32, BF16 = jnp.float32, jnp.bfloat16
NN, NT, TN = ((1,), (0,)), ((1,), (1,)), ((0,), (0,))
MESH_AXES = ("x", "y", "c")
N_DEV = 8

D_MODEL = 1024
N_META = 16
EPS = 1e-6
SSD_HEADS, SSD_HEAD_DIM, SSD_GROUPS, SSD_STATE, SSD_CONV, Q = 16, 64, 4, 128, 4, 128
SSD_INNER = SSD_HEADS * SSD_HEAD_DIM
SSD_CONV_CH = SSD_INNER + 2 * SSD_GROUPS * SSD_STATE
HG_WIDTH, HG_HEADS, HG_CHUNK = 1024, 8, 16
PAD = Q - N_META
N_MAIN = 9 * 1024
ADAM_LR, ADAM_B1, ADAM_B2, ADAM_EPS, ADAM_WD, ADAM_STEP = 0.001, 0.9, 0.999, 1e-08, 0.01, 10
VMEM_LIMIT = 52 * 1024 * 1024


def _dot(a, b, dims):
    return lax.dot_general(a, b, (dims, ((), ())), preferred_element_type=F32)


def _dot01(a, b, dims, sel):
    x, ax_x, s, ax_s = (b, dims[1][0], a, dims[0][0]) if sel == "a" else (a, dims[0][0], b, dims[1][0])
    hi = x.astype(BF16)
    r1 = x - hi.astype(F32)
    mid = r1.astype(BF16)
    lo = (r1 - mid.astype(F32)).astype(BF16)
    xs = jnp.concatenate([hi, mid, lo], axis=ax_x)
    ss = jnp.concatenate([s.astype(BF16)] * 3, axis=ax_s)
    return _dot(ss, xs, dims) if sel == "a" else _dot(xs, ss, dims)


def _sigmoid(x):
    return 0.5 * jnp.tanh(0.5 * x) + 0.5


def _dsilu(x, s):
    return s * (1.0 + x * (1.0 - s))


def _softplus(x):
    e = jnp.exp(-jnp.abs(x))
    u = 1.0 + e
    log1p_e = jnp.where(u == 1.0, e, jnp.log(u) * e / (u - 1.0))
    return jnp.maximum(x, 0.0) + log1p_e


def _params(sem):
    return pltpu.CompilerParams(dimension_semantics=sem, vmem_limit_bytes=VMEM_LIMIT)


def _tile(n, prefs):
    for p in prefs:
        if n % p == 0:
            return p
    return n


CHIP_FLIPS = ((1, 0), (0, 1), (1, 1))
N_PEER = N_DEV - 1


def _comm_gather(srcs, outs, send_sems, recv_sems, local_sems):
    n = len(srcs)
    x, y, c = (lax.axis_index(a) for a in MESH_AXES)
    dev = lambda px, py, pc: 4 * px + 2 * py + pc
    me, sib = dev(x, y, c), (x, y, 1 - c)
    nbr_x, nbr_y, diag = (1 - x, y), (x, 1 - y), (1 - x, 1 - y)
    via = (x ^ c, y ^ (1 - c), c)
    sent_on = dev(x ^ (1 - c), y ^ c, c)

    def rc(w, k, slot, to, src=None):
        return pltpu.make_async_remote_copy(
            src_ref=outs[w].at[slot] if src is None else src, dst_ref=outs[w].at[slot],
            send_sem=send_sems.at[w, k], recv_sem=recv_sems.at[w, k], device_id=to, device_id_type=pl.DeviceIdType.MESH)

    def local(w):
        return pltpu.make_async_copy(srcs[w], outs[w].at[me], local_sems.at[w])

    def start():
        for w in range(n):
            local(w).start()
            rc(w, 0, me, sib, src=srcs[w]).start()
            rc(w, 1, me, (*nbr_x, c), src=srcs[w]).start()
            rc(w, 2, me, (*nbr_y, c), src=srcs[w]).start()

    def pass_on():
        for w in range(n):
            rc(w, 1, dev(*nbr_x, c), sib).wait_recv()
            rc(w, 2, dev(*nbr_y, c), sib).wait_recv()
            rc(w, 3, sent_on, via).start()
            rc(w, 4, dev(*nbr_x, c), sib).start()
            rc(w, 5, dev(*nbr_y, c), sib).start()

    def pass_on_diagonal():
        for w in range(n):
            rc(w, 3, dev(*diag, c), sib).wait_recv()
            rc(w, 6, dev(*diag, c), sib).start()

    def finish():
        for w in range(n):
            rc(w, 0, dev(x, y, 1 - c), sib).wait_recv()
            for k, chip in ((4, nbr_x), (5, nbr_y), (6, diag)):
                rc(w, k, dev(*chip, 1 - c), sib).wait_recv()
            for k in range(N_PEER):
                rc(w, k, me, sib, src=srcs[w]).wait_send()
            local(w).wait()

    return start, (pass_on, pass_on_diagonal), finish


def _comm_scatter(srcs, outs, send_sems, recv_sems, local_sems):
    n = len(srcs)
    x, y, c = (lax.axis_index(a) for a in MESH_AXES)
    me = 4 * x + 2 * y + c

    def copies():
        out = []
        for w in range(n):
            out.append(pltpu.make_async_copy(srcs[w].at[me], outs[w].at[me], local_sems.at[w]))
            for k in range(1, N_DEV):
                px, py, pc = x ^ (k >> 2), y ^ ((k >> 1) & 1), c ^ (k & 1)
                out.append(pltpu.make_async_remote_copy(
                    src_ref=srcs[w].at[4 * px + 2 * py + pc], dst_ref=outs[w].at[me],
                    send_sem=send_sems.at[w, k - 1], recv_sem=recv_sems.at[w, k - 1],
                    device_id=(px, py, pc), device_id_type=pl.DeviceIdType.MESH))
        return out

    def start():
        for cp in copies():
            cp.start()

    def finish():
        for cp in copies():
            cp.wait()

    return start, None, finish


def _comm_swap(srcs, outs, send_sems, recv_sems, local_sems):
    x, y, c = (lax.axis_index(a) for a in MESH_AXES)

    def copies():
        return [pltpu.make_async_remote_copy(
            src_ref=srcs[w].at[1 - c], dst_ref=outs[w], send_sem=send_sems.at[w, 0], recv_sem=recv_sems.at[w, 0],
            device_id=(x, y, 1 - c), device_id_type=pl.DeviceIdType.MESH) for w in range(len(srcs))]

    def start():
        for cp in copies():
            cp.start()

    def finish():
        for cp in copies():
            cp.wait()

    return start, None, finish


def _comm_chips(srcs, outs, send_sems, recv_sems, local_sems):
    n = len(srcs)
    x, y, c = (lax.axis_index(a) for a in MESH_AXES)
    mine = 2 * x + y

    def copies():
        out = []
        for w in range(n):
            out.append(pltpu.make_async_copy(srcs[w].at[mine], outs[w].at[mine], local_sems.at[w]))
            for j, (fx, fy) in enumerate(CHIP_FLIPS):
                px, py = x ^ fx, y ^ fy
                out.append(pltpu.make_async_remote_copy(
                    src_ref=srcs[w].at[2 * px + py], dst_ref=outs[w].at[mine],
                    send_sem=send_sems.at[w, j], recv_sem=recv_sems.at[w, j],
                    device_id=(px, py, c), device_id_type=pl.DeviceIdType.MESH))
        return out

    def start():
        for cp in copies():
            cp.start()

    def finish():
        for cp in copies():
            cp.wait()

    return start, None, finish


def _comm_parts(comm):
    kind, arrays = comm[:2]
    n = len(arrays)
    lead = {"gather": lambda a: (N_DEV,) + a.shape, "scatter": lambda a: (N_DEV,) + a.shape[1:],
            "swap": lambda a: a.shape[1:], "chips": lambda a: a.shape}[kind]
    shapes = [jax.ShapeDtypeStruct(lead(a), a.dtype) for a in arrays]
    sems = [pltpu.SemaphoreType.DMA((n, N_PEER)), pltpu.SemaphoreType.DMA((n, N_PEER)), pltpu.SemaphoreType.DMA((n,))]
    make = {"gather": _comm_gather, "scatter": _comm_scatter, "swap": _comm_swap, "chips": _comm_chips}[kind]
    return n, shapes, sems, make


def _exchange(name, kind, arrays):
    n, shapes, sems, make = _comm_parts((kind, arrays))

    def body(*refs):
        start, middle, finish = make(refs[:n], refs[n:2 * n], *refs[2 * n:])
        start()
        for stage in middle or ():
            stage()
        finish()

    any_spec = pl.BlockSpec(memory_space=pl.ANY)
    return pl.pallas_call(
        body, name=name, in_specs=[any_spec] * n, out_specs=[any_spec] * n, out_shape=shapes, scratch_shapes=sems,
        compiler_params=pltpu.CompilerParams(has_side_effects=True),
    )(*arrays)


def _call(body, *, name, grid, in_specs, out_specs, out_shape, scratch, sem, args, comm=None, into=None):
    any_spec = pl.BlockSpec(memory_space=pl.ANY)
    in_specs, args, aliases, n_body_in = list(in_specs), list(args), {}, len(in_specs)
    for arr, k in ([] if into is None else into if isinstance(into, list) else [into]):
        aliases[len(in_specs)] = k
        in_specs.append(any_spec)
        args.append(arr)
    n_in, n_out, n_scr = len(in_specs), len(out_specs), len(scratch)
    if comm is None:
        def plain(*refs):
            body(*refs[:n_body_in], *refs[n_in:])

        return pl.pallas_call(plain, name=name, grid=grid, in_specs=in_specs, out_specs=out_specs, out_shape=out_shape,
                              scratch_shapes=scratch, input_output_aliases=aliases, compiler_params=_params(sem))(*args)
    n, shapes, sems, make = _comm_parts(comm)

    def carrier(*refs):
        ins, csrc = refs[:n_body_in], refs[n_in:n_in + n]
        outs, cout = refs[n_in + n:n_in + n + n_out], refs[n_in + n + n_out:n_in + 2 * n + n_out]
        rest = refs[n_in + 2 * n + n_out:]
        start, middle, finish = make(csrc, cout, *rest[n_scr:])
        ids = [pl.program_id(a) for a in range(len(grid))]
        step = functools.reduce(lambda acc, ig: acc * ig[1] + ig[0], zip(ids, grid), 0)
        n_steps = functools.reduce(lambda a, b: a * b, grid, 1)
        pl.when(step == 0)(start)
        body(*ins, *outs, *rest[:n_scr])
        if middle:
            pl.when(step == max(0, (3 * n_steps) // 4 - 1))(middle[0])
            pl.when(step == n_steps - 1)(middle[1])
        pl.when(step == n_steps - 1)(finish)

    return pl.pallas_call(
        carrier, name=name, grid=grid, in_specs=in_specs + [any_spec] * n,
        out_specs=list(out_specs) + [any_spec] * n, out_shape=list(out_shape) + shapes,
        scratch_shapes=list(scratch) + sems, input_output_aliases=aliases,
        compiler_params=pltpu.CompilerParams(dimension_semantics=("arbitrary",) * len(grid),
                                             vmem_limit_bytes=VMEM_LIMIT, has_side_effects=True),
    )(*args, *comm[1])


def _fused_matmul(name, M, N, K, pairs, extras, epilogue, out_dtypes, n_acc, tm, tn, tk, outer="i", comm=None,
                  stack=False, vecs=(), row_sums=0, wide=None, sub=None):
    nk = K // tk
    n_pairs, n_ex, n_out = len(pairs), len(extras), len(out_dtypes)
    assert not row_sums or (outer == "i" and N == tn)

    def ij(g0, g1):
        return (g0, g1) if outer == "i" else (g1, g0)

    in_specs, args = [], []
    for p in pairs:
        ao, bk, bn = p.get("a_off", 0), p.get("bk_off", 0), p.get("bn_off", 0)
        mode = dict(pipeline_mode=pl.Buffered(1)) if p.get("resident") else {}
        in_specs.append(pl.BlockSpec((tm, tk), lambda g0, g1, k, ao=ao: (ij(g0, g1)[0], k + ao)))
        if "b_shift" in p:
            first, shift = p["b_shift"]
            if p.get("trans_b"):
                in_specs.append(pl.BlockSpec(
                    (pl.Element(tn), pl.Element(tk)),
                    lambda g0, g1, k, bk=bk: (
                        pl.multiple_of(ij(g0, g1)[1] * tn + jnp.where(ij(g0, g1)[1] >= first, shift, 0), 16),
                        (k + bk) * tk)))
            else:
                in_specs.append(pl.BlockSpec(
                    (pl.Element(tk), pl.Element(tn)),
                    lambda g0, g1, k, bn=bn: (pl.multiple_of(k * tk + jnp.where(k >= first, shift, 0), 16),
                                              (ij(g0, g1)[1] + bn) * tn)))
        elif p.get("trans_b"):
            in_specs.append(pl.BlockSpec((tn, tk), lambda g0, g1, k, bk=bk, bn=bn: (ij(g0, g1)[1] + bn, k + bk), **mode))
        else:
            in_specs.append(pl.BlockSpec((tk, tn), lambda g0, g1, k, bk=bk, bn=bn: (k + bk, ij(g0, g1)[1] + bn), **mode))
        args += [p["a"], p["b"]]
    for arr, off in extras:
        in_specs.append(pl.BlockSpec((tm, tn), lambda g0, g1, k, off=off: (ij(g0, g1)[0], ij(g0, g1)[1] + off)))
        args.append(arr)
    for arr in vecs:
        in_specs.append(pl.BlockSpec((1, tn), lambda g0, g1, k: (0, ij(g0, g1)[1])))
        args.append(arr)
    if stack:
        assert N == tn
        out_specs = [pl.BlockSpec((tm, n_out * tn), lambda g0, g1, k: (ij(g0, g1)[0], 0))]
        out_shape = [jax.ShapeDtypeStruct((M, n_out * N), out_dtypes[0])]
    else:
        out_specs = [pl.BlockSpec((tm, tn), lambda g0, g1, k: ij(g0, g1)) for _ in out_dtypes]
        out_shape = [jax.ShapeDtypeStruct((M, N), dt) for dt in out_dtypes]
    if wide:
        out_specs.append(pl.BlockSpec((pl.Element(tm), pl.Element(wide["width"])),
                                      lambda g0, g1, k: (pl.multiple_of(ij(g0, g1)[0] * tm, 16), wide["col"])))
        out_shape.append(jax.ShapeDtypeStruct((M, wide["total"]), wide["dtype"]))
    n_tile_out = len(out_specs)
    out_specs += [pl.BlockSpec((1, tn), lambda g0, g1, k: (0, 0)) for _ in range(row_sums)]
    out_shape += [jax.ShapeDtypeStruct((1, N), F32) for _ in range(row_sums)]
    grid = (M // tm, N // tn, nk) if outer == "i" else (N // tn, M // tm, nk)
    n_in = 2 * n_pairs + n_ex + len(vecs)

    def partials(refs, cs=slice(None)):
        accs = [None] * n_acc
        for idx, p in enumerate(pairs):
            b_ref = refs[2 * idx + 1]
            d = (_dot(refs[2 * idx][...], b_ref[cs, :], NT) if p.get("trans_b")
                 else _dot(refs[2 * idx][...], b_ref[:, cs], NN))
            accs[p["acc"]] = d if accs[p["acc"]] is None else accs[p["acc"]] + d
        return accs

    def finish(accs, refs, first_rows, cs=slice(None)):
        res = epilogue(accs, [r[:, cs] for r in refs[2 * n_pairs:n_in]])
        if stack:
            o = refs[n_in]
            for idx in range(n_out):
                lo = idx * tn + (cs.start or 0)
                o[:, lo:lo + (tn if cs.stop is None else cs.stop - cs.start)] = res[idx].astype(o.dtype)
        else:
            for o, r in zip(refs[n_in:n_in + n_out], res):
                o[:, cs] = r.astype(o.dtype)
        if wide:
            o = refs[n_in + n_tile_out - 1]
            o[...] = res[n_out].astype(o.dtype)
        for o, r in zip(refs[n_in + n_tile_out:n_in + n_tile_out + row_sums], res[n_out + bool(wide):]):
            @pl.when(first_rows)
            def _(o=o, r=r):
                o[...] = r

            @pl.when(jnp.logical_not(first_rows))
            def _(o=o, r=r):
                o[...] += r

    if nk == 1 and sub:
        assert not wide and not row_sums and tn % sub == 0

        def body(*refs):
            for c in range(tn // sub):
                cs = slice(c * sub, (c + 1) * sub)
                finish(partials(refs, cs), refs, None, cs)
        scratch = []
    elif nk == 1:
        def body(*refs):
            finish(partials(refs), refs, pl.program_id(0) == 0)
        scratch = []
    else:
        def body(*refs):
            acc_refs = refs[-n_acc:]
            k = pl.program_id(2)
            first_rows = pl.program_id(0) == 0
            new = partials(refs)

            @pl.when(k == 0)
            def _():
                for a, v in zip(acc_refs, new):
                    a[...] = v

            @pl.when(k > 0)
            def _():
                for a, v in zip(acc_refs, new):
                    a[...] += v

            @pl.when(k == nk - 1)
            def _():
                finish([a[...] for a in acc_refs], refs, first_rows)
        scratch = [pltpu.VMEM((tm, tn), F32) for _ in range(n_acc)]

    return _call(body, name=name, grid=grid, in_specs=in_specs, out_specs=out_specs, out_shape=out_shape,
                 scratch=scratch, sem=("parallel", "parallel", "arbitrary"), args=args, comm=comm)


def _matmul_tn(name, x, y, t1, t2, tr, scale=1.0, comm=None, out_dtype=BF16, out_skip=None):
    R, K1 = x.shape
    N1 = y.shape[1]
    nr, n1 = R // tr, K1 // t1
    x_spec = pl.BlockSpec((tr, t1), lambda i, j, r: (r, i))
    rows_out = K1
    o_spec = pl.BlockSpec((t1, t2), lambda i, j, r: (i, j))
    if out_skip:
        row, count = out_skip
        rows_out += count
        o_spec = pl.BlockSpec(
            (pl.Element(t1), pl.Element(t2)),
            lambda i, j, r: (pl.multiple_of(i * t1 + jnp.where(i * t1 >= row, count, 0), 16), j * t2))

    def body(x_ref, y_ref, o_ref, *acc):
        d = _dot(x_ref[...], y_ref[...], TN)
        if nr == 1:
            o_ref[...] = (d * scale).astype(o_ref.dtype)
            return
        r = pl.program_id(2)

        @pl.when(r == 0)
        def _():
            acc[0][...] = d

        @pl.when(jnp.logical_and(r > 0, r < nr - 1))
        def _():
            acc[0][...] += d

        @pl.when(r == nr - 1)
        def _():
            o_ref[...] = ((acc[0][...] + d) * scale).astype(o_ref.dtype)

    return _call(
        body, name=name, grid=(n1, N1 // t2, nr),
        in_specs=[x_spec, pl.BlockSpec((tr, t2), lambda i, j, r: (r, j))], out_specs=[o_spec],
        out_shape=[jax.ShapeDtypeStruct((rows_out, N1), out_dtype)],
        scratch=[pltpu.VMEM((t1, t2), F32)] if nr > 1 else [],
        sem=("parallel", "parallel", "arbitrary"), args=(x, y), comm=comm)


def _embed_norm(x, w, comm=None):
    Bl, S, D = x.shape
    nb = (PAD + N_META + S) // Q
    M = Bl * nb * Q

    def body(x_ref, w_ref, h_ref, n_ref):
        h = x_ref[0]
        h_ref[...] = h
        n_ref[...] = _rmsnorm_tile(h, w_ref[...]).astype(n_ref.dtype)

    row = pl.BlockSpec((Q, D), lambda b, t: (b * nb + t + 1, 0))
    return _call(
        body, name="embed_norm", grid=(Bl, nb - 1),
        in_specs=[pl.BlockSpec((1, Q, D), lambda b, t: (b, t, 0)), pl.BlockSpec((1, D), lambda b, t: (0, 0))],
        out_specs=[row, row], out_shape=[jax.ShapeDtypeStruct((M, D), F32), jax.ShapeDtypeStruct((M, D), BF16)],
        scratch=[], sem=("parallel", "parallel"), args=(x, w), comm=comm)


def _embed_meta(meta, w, h0, n0, Bl):
    nb = h0.shape[0] // (Bl * Q)
    D = h0.shape[1]

    def body(meta_ref, w_ref, h_ref, n_ref):
        h = jnp.concatenate([jnp.zeros((PAD, D), F32), meta_ref[...]], axis=0)
        h_ref[...] = h
        n_ref[...] = _rmsnorm_tile(h, w_ref[...]).astype(n_ref.dtype)

    row = pl.BlockSpec((Q, D), lambda b: (b * nb, 0))
    return _call(
        body, name="embed_meta", grid=(Bl,),
        in_specs=[pl.BlockSpec((N_META, D), lambda b: (0, 0)), pl.BlockSpec((1, D), lambda b: (0, 0))],
        out_specs=[row, row], out_shape=[jax.ShapeDtypeStruct(h0.shape, F32), jax.ShapeDtypeStruct(n0.shape, BF16)],
        scratch=[], sem=("parallel",), args=(meta, w), into=[(h0, 0), (n0, 1)])


def _rmsnorm_bwd_tile(dn, h, w, dh_in):
    r = lax.rsqrt(jnp.mean(h * h, axis=-1, keepdims=True) + EPS)
    xhat = h * r
    gw = dn * w
    dh = dh_in + r * (gw - xhat * jnp.mean(gw * xhat, axis=-1, keepdims=True))
    return dh, jnp.sum(dn * xhat, axis=0, keepdims=True)


def _loss_head(h, w, target, Bl, nb):
    M, D = h.shape
    nt = 4 if (nb * Q) % 32 == 0 and nb * Q // 4 >= Q else nb
    half = nb * Q // nt

    def body(h_ref, w_ref, t_ref, dh_ref, dhb_ref, dw_ref, loss_ref):
        b, t = pl.program_id(0), pl.program_id(1)
        row = lax.broadcasted_iota(jnp.int32, (half, 1), 0)
        live = jnp.logical_or(t > 0, row >= Q).astype(F32)
        x = h_ref[...]
        r = lax.rsqrt(jnp.mean(x * x, axis=-1, keepdims=True) + EPS)
        xhat = x * r
        wv = w_ref[...]
        tgt = t_ref[0]
        tgt = jnp.where(t == 0, pltpu.roll(tgt, Q, 0), tgt)
        err = (xhat * wv - tgt) * live
        dy = err * (1.0 / D)
        gw = dy * wv
        dx = r * (gw - xhat * jnp.mean(gw * xhat, axis=-1, keepdims=True))
        dh_ref[...] = dx
        dhb_ref[...] = dx.astype(BF16)
        dw = jnp.sum(dy * xhat, axis=0, keepdims=True)
        part = 0.5 * jnp.sum(jnp.sum(err * err, axis=-1, keepdims=True) * (1.0 / D), axis=0, keepdims=True)
        first = jnp.logical_and(b == 0, t == 0)

        @pl.when(first)
        def _():
            dw_ref[...] = dw
            loss_ref[...] = jnp.broadcast_to(part, loss_ref.shape)

        @pl.when(jnp.logical_not(first))
        def _():
            dw_ref[...] += dw
            loss_ref[...] += jnp.broadcast_to(part, loss_ref.shape)

    row = pl.BlockSpec((half, D), lambda b, t: (b * nt + t, 0))
    vec = pl.BlockSpec((1, D), lambda b, t: (0, 0))
    return pl.pallas_call(
        body, name="loss_head", grid=(Bl, nt),
        in_specs=[row, vec, pl.BlockSpec((pl.Element(1), pl.Element(half), pl.Element(D)),
                                         lambda b, t: (b, pl.multiple_of(jnp.maximum(t * half - Q, 0), 8), 0))],
        out_specs=[row, row, vec, pl.BlockSpec((8, 128), lambda b, t: (0, 0))],
        out_shape=[jax.ShapeDtypeStruct((M, D), F32), jax.ShapeDtypeStruct((M, D), BF16),
                   jax.ShapeDtypeStruct((1, D), F32), jax.ShapeDtypeStruct((8, 128), F32)],
        compiler_params=_params(("arbitrary", "arbitrary")),
    )(h, w, target)


CONV_TC = 256


def _conv_pre(xr_ref, w_ref, b_ref):
    x = xr_ref[...].astype(F32)
    acc = b_ref[...] + w_ref[SSD_CONV - 1:SSD_CONV, :] * x
    for k in range(1, SSD_CONV):
        acc = acc + w_ref[SSD_CONV - 1 - k:SSD_CONV - k, :] * pltpu.roll(x, k, 0)
    return x, acc


def _conv_fwd(proj, w, b, Bl, T):
    M = proj.shape[0]
    off = 1024 // CONV_TC

    def body(xr_ref, w_ref, b_ref, o_ref):
        _, acc = _conv_pre(xr_ref, w_ref, b_ref)
        row = lax.broadcasted_iota(jnp.int32, acc.shape, 0)
        o_ref[...] = jnp.where(row >= PAD, acc * _sigmoid(acc), 0.0).astype(o_ref.dtype)

    return pl.pallas_call(
        body, name="conv_fwd", grid=(Bl, SSD_CONV_CH // CONV_TC),
        in_specs=[pl.BlockSpec((T, CONV_TC), lambda bb, j: (bb, j + off)),
                  pl.BlockSpec((SSD_CONV, CONV_TC), lambda bb, j: (0, j)), pl.BlockSpec((1, CONV_TC), lambda bb, j: (0, j))],
        out_specs=pl.BlockSpec((T, CONV_TC), lambda bb, j: (bb, j)),
        out_shape=jax.ShapeDtypeStruct((M, SSD_CONV_CH), BF16), compiler_params=_params(("parallel", "parallel")),
    )(proj, w, b)


def _conv_bwd(proj, w, b, dxc, dproj, Bl, T):
    M = proj.shape[0]
    off = 1024 // CONV_TC

    def body(xr_ref, w_ref, b_ref, d_ref, dx_ref, dw_ref, db_ref):
        x, acc = _conv_pre(xr_ref, w_ref, b_ref)
        row = lax.broadcasted_iota(jnp.int32, acc.shape, 0)
        s = _sigmoid(acc)
        dpre = jnp.where(row >= PAD, d_ref[...].astype(F32) * _dsilu(acc, s), 0.0)
        dx = w_ref[SSD_CONV - 1:SSD_CONV, :] * dpre
        dws = [jnp.sum(dpre * x, axis=0, keepdims=True)]
        for k in range(1, SSD_CONV):
            dx = dx + w_ref[SSD_CONV - 1 - k:SSD_CONV - k, :] * pltpu.roll(dpre, T - k, 0)
            dws.append(jnp.sum(dpre * pltpu.roll(x, k, 0), axis=0, keepdims=True))
        dx_ref[...] = dx.astype(dx_ref.dtype)
        dw = jnp.concatenate(dws[::-1], axis=0)
        db = jnp.sum(dpre, axis=0, keepdims=True)

        @pl.when(pl.program_id(1) == 0)
        def _():
            dw_ref[...] = dw
            db_ref[...] = db

        @pl.when(pl.program_id(1) > 0)
        def _():
            dw_ref[...] += dw
            db_ref[...] += db

    return _call(
        body, name="conv_bwd", grid=(SSD_CONV_CH // CONV_TC, Bl),
        in_specs=[pl.BlockSpec((T, CONV_TC), lambda j, bb: (bb, j + off)),
                  pl.BlockSpec((SSD_CONV, CONV_TC), lambda j, bb: (0, j)), pl.BlockSpec((1, CONV_TC), lambda j, bb: (0, j)),
                  pl.BlockSpec((T, CONV_TC), lambda j, bb: (bb, j))],
        out_specs=[pl.BlockSpec((T, CONV_TC), lambda j, bb: (bb, j + off)),
                   pl.BlockSpec((SSD_CONV, CONV_TC), lambda j, bb: (0, j)), pl.BlockSpec((1, CONV_TC), lambda j, bb: (0, j))],
        out_shape=[jax.ShapeDtypeStruct(dproj.shape, BF16), jax.ShapeDtypeStruct((SSD_CONV, SSD_CONV_CH), F32),
                   jax.ShapeDtypeStruct((1, SSD_CONV_CH), F32)],
        scratch=[], sem=("parallel", "arbitrary"), args=(proj, w, b, dxc), into=(dproj, 0))


N_PAIR = SSD_HEADS // 2
HPG = SSD_HEADS // SSD_GROUPS
GW = SSD_INNER // SSD_GROUPS


def _per_group(fn, *arrs):
    return jnp.concatenate([jnp.broadcast_to(fn(*(a[:, GW * g:GW * (g + 1)] for a in arrs)), (arrs[0].shape[0], GW))
                            for g in range(SSD_GROUPS)], axis=1)


def _ssd_prep(c, dtr_ref, bias_ref, alog_ref, d_ref):
    row = lax.broadcasted_iota(jnp.int32, (Q, 128), 0)
    col = lax.broadcasted_iota(jnp.int32, (Q, 128), 1)
    live = col < SSD_HEADS
    valid = jnp.logical_and(jnp.logical_or(c > 0, row >= PAD), live)
    pre = dtr_ref[...] + bias_ref[...]
    dt = jnp.where(valid, _softplus(pre), 0.0)
    A = jnp.where(live[0:1], -jnp.exp(alog_ref[...]), 0.0)
    tri = row >= col
    eye = (row == col).astype(BF16)
    cs = _dot01(tri, dt * A, NN, "a")
    cst = _dot01(eye, cs, NT, "a")
    spread = (lax.broadcasted_iota(jnp.int32, (128, SSD_INNER), 0)
              == lax.broadcasted_iota(jnp.int32, (128, SSD_INNER), 1) // SSD_HEAD_DIM).astype(BF16)
    dt_w = _dot01(dt, spread, NN, "b")
    cs_w = _dot01(cs, spread, NN, "b")
    d_w = _dot01(jnp.broadcast_to(d_ref[...], (8, 128)), spread, NN, "b")[0:1]
    lane = lax.broadcasted_iota(jnp.int32, (Q, SSD_INNER), 1)
    first = (lane % 128) < SSD_HEAD_DIM
    return dict(row=row, col=col, valid=valid, pre=pre, dt=dt, A=A, tri=tri, eye=eye, cs=cs, cst=cst, spread=spread,
                dt_w=dt_w, cs_w=cs_w, d_w=d_w, ecs_w=jnp.exp(cs_w), decay_w=jnp.exp(cs_w[Q - 1:Q] - cs_w), first=first)


def _ssd_chunk(xc_ref, s, states):
    xv = xc_ref[:, 0:SSD_INNER].astype(F32)
    Bs = [xc_ref[:, SSD_INNER + 128 * g:SSD_INNER + 128 * (g + 1)] for g in range(SSD_GROUPS)]
    Cs = [xc_ref[:, SSD_INNER + 512 + 128 * g:SSD_INNER + 512 + 128 * (g + 1)] for g in range(SSD_GROUPS)]
    X = xv * s["dt_w"]
    X0 = jnp.where(s["first"], X, 0.0)
    Xb = (X0.astype(BF16), (X - X0).astype(BF16))
    Xd = (X * s["decay_w"]).astype(BF16)
    CB = [_dot(Cs[g], Bs[g], NT) for g in range(SSD_GROUPS)]
    Lms = [jnp.exp(jnp.where(s["tri"], s["cs"][:, h:h + 1] - s["cst"][h:h + 1, :], -jnp.inf)) for h in range(SSD_HEADS)]
    Ms = [CB[h // HPG] * Lms[h] for h in range(SSD_HEADS)]
    Mb = [m.astype(BF16) for m in Ms]
    prev_b = [st.astype(BF16) for st in states]
    yds, yos, sts = [], [], []
    for p in range(N_PAIR):
        g, ln = p // 2, slice(128 * p, 128 * (p + 1))
        yds.append(_dot(Mb[2 * p], Xb[0][:, ln], NN) + _dot(Mb[2 * p + 1], Xb[1][:, ln], NN))
        yos.append(_dot(Cs[g], prev_b[p], NT))
        sts.append(_dot(Xd[:, ln], Bs[g], TN))
    yo = jnp.concatenate(yos, axis=1)
    y = jnp.concatenate(yds, axis=1) + yo * s["ecs_w"] + xv * s["d_w"]
    upper = s["row"] < SSD_HEAD_DIM
    cl = s["cs"][Q - 1:Q, :]
    ecl_rows = [jnp.where(upper, jnp.exp(cl[:, 2 * p:2 * p + 1]), jnp.exp(cl[:, 2 * p + 1:2 * p + 2])) for p in range(N_PAIR)]
    new_states = [states[p] * ecl_rows[p] + sts[p] for p in range(N_PAIR)]
    return y, new_states, dict(xv=xv, Bs=Bs, Cs=Cs, X=X, Xb=Xb, CB=CB, Lms=Lms, Ms=Ms, Mb=Mb, prev_b=prev_b, yo=yo,
                               ecl_rows=ecl_rows)


def _ssd_in_specs(nc, rev=False):
    rb = (lambda b, c: b * nc + nc - 1 - c) if rev else (lambda b, c: b * nc + c)
    vec = pl.BlockSpec((1, 128), lambda b, c: (0, 0))
    return [pl.BlockSpec((Q, SSD_CONV_CH), lambda b, c: (rb(b, c), 0)),
            pl.BlockSpec((Q, 128), lambda b, c: (rb(b, c), 0)),
            pl.BlockSpec((Q, SSD_INNER), lambda b, c: (rb(b, c), 0)),
            vec, vec, vec, pl.BlockSpec((1, SSD_INNER), lambda b, c: (0, 0))]


def _ssd_fwd(xc, dtr, proj, bias_p, alog_p, d_p, nw, Bl, nc):
    M = xc.shape[0]

    def body(xc_ref, dtr_ref, z_ref, bias_ref, alog_ref, d_ref, nw_ref, y_ref, prev_ref, state):
        c = pl.program_id(1)

        @pl.when(c == 0)
        def _():
            state[...] = jnp.zeros_like(state)

        s = _ssd_prep(c, dtr_ref, bias_ref, alog_ref, d_ref)
        states = [state[p] for p in range(N_PAIR)]
        y, new_states, _ = _ssd_chunk(xc_ref, s, states)
        for p in range(N_PAIR):
            prev_ref[0, 0, p] = states[p]
            state[p] = new_states[p]
        zz = z_ref[...].astype(F32)
        yg = y * zz * _sigmoid(zz)
        r = _per_group(lambda a: lax.rsqrt(jnp.mean(a * a, axis=-1, keepdims=True) + EPS), yg)
        y_ref[...] = (yg * r * nw_ref[...]).astype(y_ref.dtype)

    return pl.pallas_call(
        body, name="ssd_fwd", grid=(Bl, nc), in_specs=_ssd_in_specs(nc),
        out_specs=[pl.BlockSpec((Q, SSD_INNER), lambda b, c: (b * nc + c, 0)),
                   pl.BlockSpec((1, 1, N_PAIR, 128, 128), lambda b, c: (b, c, 0, 0, 0))],
        out_shape=[jax.ShapeDtypeStruct((M, SSD_INNER), BF16), jax.ShapeDtypeStruct((Bl, nc, N_PAIR, 128, 128), F32)],
        scratch_shapes=[pltpu.VMEM((N_PAIR, 128, 128), F32)],
        compiler_params=_params(("arbitrary", "arbitrary")),
    )(xc, dtr, proj, bias_p, alog_p, d_p, nw)


def _ssd_bwd(xc, dtr, proj, bias_p, alog_p, d_p, nw, prev, dya, dproj, Bl, nc, comm=None):
    M = xc.shape[0]

    def body(xc_ref, dtr_ref, z_ref, bias_ref, alog_ref, d_ref, nw_ref, prev_ref, dy_ref,
             dxc_ref, dz_ref, ddtr_ref, dbias_ref, dalog_ref, dd_ref, dnw_ref, dS):
        b, t = pl.program_id(0), pl.program_id(1)

        @pl.when(t == 0)
        def _():
            dS[...] = jnp.zeros_like(dS)

        s = _ssd_prep(nc - 1 - t, dtr_ref, bias_ref, alog_ref, d_ref)
        states = [prev_ref[0, 0, p] for p in range(N_PAIR)]
        y, _, k = _ssd_chunk(xc_ref, s, states)
        xv, Bs, Cs, Xb = k["xv"], k["Bs"], k["Cs"], k["Xb"]

        zz = z_ref[...].astype(F32)
        sz = _sigmoid(zz)
        silu_z = zz * sz
        yg = y * silu_z
        r = _per_group(lambda a: lax.rsqrt(jnp.mean(a * a, axis=-1, keepdims=True) + EPS), yg)
        xhat = yg * r
        dout = dy_ref[...].astype(F32)
        gw = dout * nw_ref[...]
        dyg = r * (gw - xhat * _per_group(lambda a, c2: jnp.mean(a * c2, axis=-1, keepdims=True), gw, xhat))
        dnw = jnp.sum(dout * xhat, axis=0, keepdims=True)
        dz_ref[...] = (dyg * y * _dsilu(zz, sz)).astype(dz_ref.dtype)
        dy = dyg * silu_z
        dy0 = jnp.where(s["first"], dy, 0.0)
        dyb = (dy0.astype(BF16), (dy - dy0).astype(BF16))
        dYo = (dy * s["ecs_w"]).astype(BF16)

        dS_f = [dS[p] for p in range(N_PAIR)]
        dS_b = [d.astype(BF16) for d in dS_f]
        BdS, dXm, dprev, dCs, dMs, XdS = [], [], [], [[] for _ in range(SSD_GROUPS)], [], []
        for p in range(N_PAIR):
            g, ln = p // 2, slice(128 * p, 128 * (p + 1))
            BdS.append(_dot(Bs[g], dS_b[p], NT))
            dXm.append(_dot(k["Mb"][2 * p], dyb[0][:, ln], TN) + _dot(k["Mb"][2 * p + 1], dyb[1][:, ln], TN))
            dprev.append(_dot(dYo[:, ln], Cs[g], TN))
            dCs[g].append(_dot(dYo[:, ln], k["prev_b"][p], NN))
            for hh in range(2):
                dMs.append(_dot(dyb[hh][:, ln], Xb[hh][:, ln], NT))
                XdS.append(_dot(Xb[hh][:, ln], dS_b[p], NN))
        dX = jnp.concatenate(dXm, axis=1) + s["decay_w"] * jnp.concatenate(BdS, axis=1)
        dxs = dy * s["d_w"] + dX * s["dt_w"]

        sums = _dot01(jnp.concatenate([dX * xv, dy * k["yo"] * s["ecs_w"], dy * xv], axis=0), s["spread"], NT, "b")
        ddt, dcs = sums[0:Q], sums[Q:2 * Q]
        dD = jnp.sum(sums[2 * Q:3 * Q], axis=0, keepdims=True)

        col, row = s["col"], s["row"]
        lane1 = col[0:1]
        rowsT = lax.broadcasted_iota(jnp.int32, (128, Q), 0)
        dcs_t = jnp.zeros((128, Q), F32)
        dcl = jnp.zeros((1, 128), F32)
        dB_out, dC_out = [], []
        for g in range(SSD_GROUPS):
            Bf = Bs[g].astype(F32)
            dCB = jnp.zeros((Q, Q), F32)
            dBacc = jnp.zeros((Q, 128), F32)
            for r4 in range(HPG):
                h = HPG * g + r4
                p, hh = h // 2, h % 2
                W = dMs[h] * k["Ms"][h]
                dCB = dCB + dMs[h] * k["Lms"][h]
                decay_h = s["decay_w"][:, SSD_HEAD_DIM * h:SSD_HEAD_DIM * h + 1]
                dBacc = dBacc + decay_h * XdS[h]
                tdec = jnp.sum(XdS[h] * Bf, axis=1, keepdims=True) * decay_h
                dcs = dcs + jnp.where(col == h, jnp.sum(W, axis=1, keepdims=True) - tdec, 0.0)
                dcs_t = dcs_t - jnp.where(rowsT == h, jnp.sum(W, axis=0, keepdims=True), 0.0)
                rows_h = (row < SSD_HEAD_DIM) if hh == 0 else (row >= SSD_HEAD_DIM)
                sprev = jnp.sum(jnp.sum(jnp.where(rows_h, dS_f[p] * states[p], 0.0), axis=1, keepdims=True),
                                axis=0, keepdims=True)
                ecl = jnp.exp(s["cs"][Q - 1:Q, h:h + 1])
                dcl = dcl + jnp.where(lane1 == h, jnp.sum(tdec, axis=0, keepdims=True) + ecl * sprev, 0.0)
            dCB_b = dCB.astype(BF16)
            dC_out.append(dCs[g][0] + dCs[g][1] + _dot(dCB_b, Bs[g], NN))
            dB_out.append(dBacc + _dot(dCB_b, Cs[g], TN))
        for p in range(N_PAIR):
            dS[p] = dS_f[p] * k["ecl_rows"][p] + dprev[p]
        dxc_ref[...] = jnp.concatenate([dxs] + dB_out + dC_out, axis=1).astype(dxc_ref.dtype)

        dcs = dcs + _dot01(s["eye"], dcs_t, NT, "a") + jnp.where(row == Q - 1, dcl, 0.0)
        da = _dot01(row <= col, dcs, NN, "a")
        ddt = ddt + da * s["A"]
        dpre = jnp.where(s["valid"], ddt * _sigmoid(s["pre"]), 0.0)
        ddtr_ref[...] = dpre
        dbias = jnp.sum(dpre, axis=0, keepdims=True)
        dalog = jnp.sum(da * s["dt"], axis=0, keepdims=True) * s["A"]
        first_step = jnp.logical_and(b == 0, t == 0)

        @pl.when(first_step)
        def _():
            dbias_ref[...] = dbias
            dalog_ref[...] = dalog
            dd_ref[...] = dD
            dnw_ref[...] = dnw

        @pl.when(jnp.logical_not(first_step))
        def _():
            dbias_ref[...] += dbias
            dalog_ref[...] += dalog
            dd_ref[...] += dD
            dnw_ref[...] += dnw

    rb = lambda b, c: b * nc + nc - 1 - c
    rowblk = lambda w: pl.BlockSpec((Q, w), lambda b, c: (rb(b, c), 0))
    vec = lambda w: pl.BlockSpec((1, w), lambda b, c: (0, 0))
    return _call(
        body, name="ssd_bwd", grid=(Bl, nc),
        in_specs=_ssd_in_specs(nc, rev=True) + [
            pl.BlockSpec((1, 1, N_PAIR, 128, 128), lambda b, c: (b, nc - 1 - c, 0, 0, 0)), rowblk(SSD_INNER)],
        out_specs=[rowblk(SSD_CONV_CH), rowblk(SSD_INNER), rowblk(128), vec(128), vec(128), vec(128), vec(SSD_INNER)],
        out_shape=[jax.ShapeDtypeStruct((M, SSD_CONV_CH), BF16), jax.ShapeDtypeStruct(dproj.shape, BF16),
                   jax.ShapeDtypeStruct((M, 128), F32), jax.ShapeDtypeStruct((1, 128), F32),
                   jax.ShapeDtypeStruct((1, 128), F32), jax.ShapeDtypeStruct((1, 128), F32),
                   jax.ShapeDtypeStruct((1, SSD_INNER), F32)],
        scratch=[pltpu.VMEM((N_PAIR, 128, 128), F32)], sem=("arbitrary", "arbitrary"),
        args=(xc, dtr, proj, bias_p, alog_p, d_p, nw, prev, dya), comm=comm, into=(dproj, 1))


NSUB = Q // HG_CHUNK
HG_HP = 8
EXP_CAP = 80.0


def _hg_setup(blk, q_ref, f_ref, hb_ref):
    row = lax.broadcasted_iota(jnp.int32, (Q, Q), 0)
    col = lax.broadcasted_iota(jnp.int32, (Q, Q), 1)
    same = (row // HG_CHUNK) == (col // HG_CHUNK)
    causal = jnp.logical_and(same, col <= row)
    lb = _sigmoid(hb_ref[0:1, :] - hb_ref[1:2, :])
    fl = f_ref[...].astype(F32)
    sg = _sigmoid(fl)
    fg = lb + (1.0 - lb) * sg
    k = (1.0 - lb) * (1.0 - sg)
    gl = jnp.log(fg)
    G = _dot01(causal, gl, NN, "a")
    T = _dot01(same, gl, NN, "a")
    qv = q_ref[...].astype(F32)
    sq = _sigmoid(qv)
    eG = jnp.exp(G)
    eGn = jnp.exp(jnp.minimum(-G, EXP_CAP))
    eTG = jnp.exp(T - G)
    qt = qv * sq * eG
    kt = k * eGn
    kh = k * eTG
    valid = jnp.logical_or(blk > 0, row[:, :1] >= PAD)
    return dict(row=row, col=col, same=same, causal=causal, lb=lb, sg=sg, fg=fg, k=k, T=T, qv=qv, sq=sq,
                eG=eG, eGn=eGn, eTG=eTG, qt=qt, kt=kt, kh=kh, valid=valid)


def _hg_specs(nb, rev=False):
    rb = (lambda h, b, t: b * nb + nb - 1 - t) if rev else (lambda h, b, t: b * nb + t)
    w = 128 * HG_HP
    blk = lambda off: pl.BlockSpec((Q, w), lambda h, b, t, off=off: (rb(h, b, t), off // HG_HP + h))
    return [blk(24), blk(32), blk(40), blk(48),
            pl.BlockSpec((2, w), lambda h, b, t: (0, h)), pl.BlockSpec((1, w), lambda h, b, t: (0, h))]


HEAD_LANES = tuple(slice(128 * hh, 128 * (hh + 1)) for hh in range(HG_HP))


def _per_head(fn, *arrs):
    return jnp.concatenate([jnp.broadcast_to(fn(*(a[:, ln] for a in arrs)), (arrs[0].shape[0], 128))
                            for ln in HEAD_LANES], axis=1)


def _hgrn_fwd(proj, hb, nw, Bl, nb, comm=None):
    M = proj.shape[0]

    def body(q_ref, f_ref, i_ref, g_ref, hb_ref, nw_ref, y_ref, o_ref, st_ref, S):
        blk = pl.program_id(2)

        @pl.when(blk == 0)
        def _():
            S[...] = jnp.zeros_like(S)

        s = _hg_setup(blk, q_ref, f_ref, hb_ref)
        v = i_ref[...]
        qt_b, kt_b, kh_b = s["qt"].astype(BF16), s["kt"].astype(BF16), s["kh"].astype(BF16)
        eT = jnp.exp(s["T"])
        att = [jnp.where(s["causal"], _dot(qt_b[:, ln], kt_b[:, ln], NT), 0.0).astype(BF16) for ln in HEAD_LANES]
        o_intra = [_dot(att[hh], v[:, ln], NN) for hh, ln in enumerate(HEAD_LANES)]
        for j in range(NSUB):
            sl = slice(HG_CHUNK * j, HG_CHUNK * (j + 1))
            for hh, ln in enumerate(HEAD_LANES):
                St = S[hh]
                st_ref[0, hh, 0, j] = St
                o_ref[sl, ln] = o_intra[hh][sl] + _dot(qt_b[sl, ln], St.astype(BF16), NT)
                S[hh] = St * eT[HG_CHUNK * j:HG_CHUNK * j + 1, ln] + _dot(v[sl, ln], kh_b[sl, ln], TN)
        o = o_ref[...]
        r = _per_head(lambda a: lax.rsqrt(jnp.mean(a * a, axis=-1, keepdims=True) + EPS), o)
        gv = g_ref[...].astype(F32)
        y_ref[...] = (o * r * nw_ref[...] * gv * _sigmoid(gv)).astype(y_ref.dtype)

    rowblk = pl.BlockSpec((Q, 128 * HG_HP), lambda h, b, t: (b * nb + t, h))
    return _call(
        body, name="hgrn_fwd", grid=(HG_HEADS // HG_HP, Bl, nb), in_specs=_hg_specs(nb),
        out_specs=[rowblk, rowblk,
                   pl.BlockSpec((1, HG_HP, 1, NSUB, 128, 128), lambda h, b, t: (b, h, t, 0, 0, 0))],
        out_shape=[jax.ShapeDtypeStruct((M, HG_WIDTH), BF16), jax.ShapeDtypeStruct((M, HG_WIDTH), F32),
                   jax.ShapeDtypeStruct((Bl, HG_HEADS, nb, NSUB, 128, 128), F32)],
        scratch=[pltpu.VMEM((HG_HP, 128, 128), F32)], sem=("parallel", "arbitrary", "arbitrary"),
        args=(proj, proj, proj, proj, hb, nw), comm=comm)


def _hgrn_bwd(proj, hb, nw, o_saved, st_saved, dyb, dproj, Bl, nb, comm=None):
    assert HG_HP == HG_HEADS

    def body(q_ref, f_ref, i_ref, g_ref, hb_ref, nw_ref, o_ref, st_ref, dy_ref,
             d_ref, dhb_ref, dnw_ref, dS, a_dqt, a_dv, a_dkh, a_dgl):
        b, t = pl.program_id(1), pl.program_id(2)

        @pl.when(t == 0)
        def _():
            dS[...] = jnp.zeros_like(dS)

        first_step = jnp.logical_and(b == 0, t == 0)
        s = _hg_setup(nb - 1 - t, q_ref, f_ref, hb_ref)
        v = i_ref[...]
        qt_b, kt_b, kh_b = s["qt"].astype(BF16), s["kt"].astype(BF16), s["kh"].astype(BF16)
        eT = jnp.exp(s["T"])
        att = [jnp.where(s["causal"], _dot(qt_b[:, ln], kt_b[:, ln], NT), 0.0).astype(BF16) for ln in HEAD_LANES]

        o = o_ref[...]
        r = _per_head(lambda a: lax.rsqrt(jnp.mean(a * a, axis=-1, keepdims=True) + EPS), o)
        xhat = o * r
        gv = g_ref[...].astype(F32)
        sgv = _sigmoid(gv)
        dyv = dy_ref[...].astype(F32)
        d_on = dyv * gv * sgv
        dg_out = dyv * xhat * nw_ref[...] * _dsilu(gv, sgv)
        gw = d_on * nw_ref[...]
        do = r * (gw - xhat * _per_head(lambda a, c: jnp.mean(a * c, axis=-1, keepdims=True), gw, xhat))
        dnw = jnp.sum(d_on * xhat, axis=0, keepdims=True)
        do_b = do.astype(BF16)

        datt = [jnp.where(s["causal"], _dot(do_b[:, ln], v[:, ln], NT), 0.0).astype(BF16) for ln in HEAD_LANES]
        dqt = jnp.concatenate([_dot(datt[hh], kt_b[:, ln], NN) for hh, ln in enumerate(HEAD_LANES)], axis=1)
        dkt = jnp.concatenate([_dot(datt[hh], qt_b[:, ln], TN) for hh, ln in enumerate(HEAD_LANES)], axis=1)
        dv = jnp.concatenate([_dot(att[hh], do_b[:, ln], TN) for hh, ln in enumerate(HEAD_LANES)], axis=1)
        last_row = (lax.broadcasted_iota(jnp.int32, (HG_CHUNK, 128), 0) == HG_CHUNK - 1)
        for j in reversed(range(NSUB)):
            sl = slice(HG_CHUNK * j, HG_CHUNK * (j + 1))
            for hh, ln in enumerate(HEAD_LANES):
                St = st_ref[0, hh, 0, j]
                dSt = dS[hh]
                St_b, dSt_b = St.astype(BF16), dSt.astype(BF16)
                eT_j = eT[HG_CHUNK * j:HG_CHUNK * j + 1, ln]
                dkh_j = _dot(v[sl, ln], dSt_b, NN)
                a_dqt[sl, ln] = _dot(do_b[sl, ln], St_b, NN)
                a_dv[sl, ln] = _dot(kh_b[sl, ln], dSt_b, NT)
                a_dkh[sl, ln] = dkh_j
                dlast = (jnp.sum(St * dSt, axis=0, keepdims=True) * eT_j
                         + jnp.sum(dkh_j * s["kh"][sl, ln], axis=0, keepdims=True))
                a_dgl[sl, ln] = jnp.where(last_row, dlast, 0.0)
                dS[hh] = dSt * eT_j + _dot(do_b[sl, ln], qt_b[sl, ln], TN)
        dqt = dqt + a_dqt[...]
        dv = dv + a_dv[...]
        dkh = a_dkh[...]
        dG = dqt * s["qt"] - dkt * s["kt"] - dkh * s["kh"] + a_dgl[...]
        rev_causal = jnp.logical_and(s["same"], s["col"] >= s["row"])
        dgl = _dot01(rev_causal, dG, NN, "a")
        dk = dkt * s["eGn"] + dkh * s["eTG"]
        dfg = dgl / s["fg"] - dk
        lb, sg = s["lb"], s["sg"]
        keep = s["valid"].astype(F32)
        d_ref[:, 0:w] = (dqt * s["eG"] * _dsilu(s["qv"], s["sq"]) * keep).astype(d_ref.dtype)
        d_ref[:, w:2 * w] = (dfg * (1.0 - lb) * sg * (1.0 - sg) * keep).astype(d_ref.dtype)
        d_ref[:, 2 * w:3 * w] = (dv * keep).astype(d_ref.dtype)
        d_ref[:, 3 * w:4 * w] = (dg_out * keep).astype(d_ref.dtype)
        dlb = jnp.sum(dfg * (1.0 - sg) * keep, axis=0, keepdims=True) * lb * (1.0 - lb)
        dhb = jnp.concatenate([dlb, -dlb], axis=0)

        @pl.when(first_step)
        def _():
            dhb_ref[...] = dhb
            dnw_ref[...] = dnw

        @pl.when(jnp.logical_not(first_step))
        def _():
            dhb_ref[...] += dhb
            dnw_ref[...] += dnw

    w = 128 * HG_HP
    rowblk = pl.BlockSpec((Q, w), lambda h, b, t: (b * nb + nb - 1 - t, h))
    return _call(
        body, name="hgrn_bwd", grid=(HG_HEADS // HG_HP, Bl, nb),
        in_specs=_hg_specs(nb, rev=True) + [
            rowblk, pl.BlockSpec((1, HG_HP, 1, NSUB, 128, 128), lambda h, b, t: (b, h, nb - 1 - t, 0, 0, 0)), rowblk],
        out_specs=[pl.BlockSpec((pl.Element(Q), pl.Element(4 * w)),
                                lambda h, b, t: (pl.multiple_of((b * nb + nb - 1 - t) * Q, Q), 3 * HG_WIDTH)),
                   pl.BlockSpec((2, w), lambda h, b, t: (0, h)), pl.BlockSpec((1, w), lambda h, b, t: (0, h))],
        out_shape=[jax.ShapeDtypeStruct(dproj.shape, BF16),
                   jax.ShapeDtypeStruct((2, HG_WIDTH), F32), jax.ShapeDtypeStruct((1, HG_WIDTH), F32)],
        scratch=[pltpu.VMEM((HG_HP, 128, 128), F32)] + [pltpu.VMEM((Q, w), F32)] * 4,
        sem=("parallel", "arbitrary", "arbitrary"),
        args=(proj, proj, proj, proj, hb, nw, o_saved, st_saved, dyb), comm=comm, into=(dproj, 0))


def _adamw(name, parts, w, m, v, comm=None):
    R, C = w.shape
    S = parts.shape[0]
    tr, tc = (_tile(R, (256, 176, 128, 64, 8)), C) if R % 8 == 0 else (R, 256)
    c1, c2 = 1.0 - ADAM_B1 ** ADAM_STEP, 1.0 - ADAM_B2 ** ADAM_STEP

    def body(p_ref, w_ref, m_ref, v_ref, g_ref, d_ref, nm_ref, nv_ref):
        g = p_ref[0].astype(F32)
        for s in range(1, S):
            g = g + p_ref[s].astype(F32)
        nm = ADAM_B1 * m_ref[...] + (1.0 - ADAM_B1) * g
        nv = ADAM_B2 * v_ref[...] + (1.0 - ADAM_B2) * (g * g)
        g_ref[...] = g
        nm_ref[...] = nm
        nv_ref[...] = nv
        d_ref[...] = -ADAM_LR * ((nm / c1) / (jnp.sqrt(nv / c2) + ADAM_EPS) + ADAM_WD * w_ref[...])

    blk = pl.BlockSpec((tr, tc), lambda i, j: (i, j))
    return _call(
        body, name=name, grid=(R // tr, C // tc),
        in_specs=[pl.BlockSpec((S, tr, tc), lambda i, j: (0, i, j)), blk, blk, blk], out_specs=[blk] * 4,
        out_shape=[jax.ShapeDtypeStruct((R, C), F32)] * 4, scratch=[], sem=("parallel", "parallel"),
        args=(parts, w, m, v), comm=comm)


def _pair_sum(name, by_core, arrived):
    _, J, R, C = by_core.shape
    tc = _tile(C, (512, 256, 128))

    def body(c_ref, a_ref, b_ref, o_ref):
        o_ref[...] = (a_ref[0].astype(F32) + b_ref[...].astype(F32)).astype(o_ref.dtype)

    blk = pl.BlockSpec((1, R, tc), lambda j, k, c_ref: (j, 0, k))
    return pl.pallas_call(
        body, name=name,
        grid_spec=pltpu.PrefetchScalarGridSpec(
            num_scalar_prefetch=1, grid=(J, C // tc),
            in_specs=[pl.BlockSpec((1, 1, R, tc), lambda j, k, c_ref: (c_ref[0], j, 0, k)), blk], out_specs=blk),
        out_shape=jax.ShapeDtypeStruct(arrived.shape, arrived.dtype), compiler_params=_params(("parallel", "parallel")),
    )(lax.axis_index("c").astype(jnp.int32).reshape(1), by_core, arrived)


def _sum_parts(name, parts):
    S, R, C = parts.shape

    def body(p_ref, o_ref):
        g = p_ref[0]
        for s in range(1, S):
            g = g + p_ref[s]
        o_ref[...] = g

    return pl.pallas_call(
        body, name=name, out_shape=jax.ShapeDtypeStruct((R, C), F32),
        in_specs=[pl.BlockSpec(memory_space=pltpu.VMEM)], out_specs=pl.BlockSpec(memory_space=pltpu.VMEM),
    )(parts)


def _heads_to_lanes(p):
    return jnp.pad(p, [(0, 0)] * (p.ndim - 1) + [(0, 128 - SSD_HEADS)])


def _lanes_to_heads(p):
    return p[..., :SSD_HEADS]


def _pack_rows(arrs):
    flat = jnp.concatenate([a.reshape(-1).astype(F32) for a in arrs])
    return jnp.pad(flat, (0, (-flat.shape[0]) % (8 * D_MODEL))).reshape(-1, D_MODEL)


def _unpack_rows(packed, like):
    flat, outs, at = packed.reshape(-1), [], 0
    for a in like:
        outs.append(flat[at:at + a.size].reshape(a.shape))
        at += a.size
    return outs


def _cols(gth):
    return jnp.transpose(gth, (1, 0, 2)).reshape(gth.shape[1], -1)


def _rows(gth):
    return gth.reshape(-1, gth.shape[2])


def _to_rows(g):
    return g.reshape(N_DEV, -1, g.shape[1]).astype(BF16)


def _by_core(g):
    return jnp.transpose(g.reshape(N_DEV // 2, 2, -1, g.shape[1]), (1, 0, 2, 3)).astype(BF16)


DT_ROW = 3072


def _chip_sums(tag, by_core, swap_in=None):
    arrived = swap_in(by_core) if swap_in else _exchange(tag + "_swap", "swap", by_core)
    return [_pair_sum(f"{tag}_chipsum{i}", m, a) for i, (m, a) in enumerate(zip(by_core, arrived))]


def _ffn_fwd_gu(tag, n, w_gu_t, comm=None):
    M = n.shape[0]
    F = w_gu_t.shape[0] // 2
    tm = _tile(M, (544, 256))
    outs = _fused_matmul(
        tag + "_gu", M, F, D_MODEL,
        [dict(a=n, b=w_gu_t, trans_b=True, acc=0, resident=True),
         dict(a=n, b=w_gu_t, trans_b=True, bn_off=1, acc=1, resident=True)], [],
        lambda accs, ex: (accs[0], accs[1], accs[0] * _sigmoid(accs[0]) * accs[1]),
        [BF16, BF16, BF16], 2, tm, F, D_MODEL, outer="i", comm=comm, sub=256)
    return (n, *outs[:3]), outs[3:]


def _rmsnorm_tile(x, w):
    return x * lax.rsqrt(jnp.mean(x * x, axis=-1, keepdims=True) + EPS) * w


def _ffn_fwd_down(tag, h, a, w_down, next_norm=None, comm=None):
    M = h.shape[0]
    F = w_down.shape[0]
    tm = _tile(M, (1088, 544, 256))
    if next_norm is None:
        (h_out,) = _fused_matmul(
            tag + "_down", M, D_MODEL, F, [dict(a=a, b=w_down, acc=0)], [(h, 0)],
            lambda accs, ex: (ex[0] + 0.5 * accs[0],), [F32], 1, tm, D_MODEL, F, outer="j", sub=256)
        return h_out

    def with_norm(accs, ex):
        h_new = ex[0] + 0.5 * accs[0]
        return h_new, _rmsnorm_tile(h_new, ex[1])

    return _fused_matmul(tag + "_down", M, D_MODEL, F, [dict(a=a, b=w_down, acc=0, resident=True)], [(h, 0)], with_norm,
                         [F32, BF16], 1, tm, D_MODEL, F, outer="j", vecs=[next_norm], comm=comm)


def _ffn_bwd(tag, dh, dh_b, h, norm_w, w_gu_t, w_down, saved, scatter=False):
    n, g, u, a = saved
    M = h.shape[0]
    F = w_down.shape[0]
    tm = _tile(M, (544, 256))
    tn = _tile(F, (1408, 704, 256))

    def swiglu_bwd(accs, ex):
        da, gv, uv = 0.5 * accs[0], ex[0].astype(F32), ex[1].astype(F32)
        s = _sigmoid(gv)
        return da * uv * _dsilu(gv, s), da * gv * s

    (dgu,) = _fused_matmul(
        tag + "_dact", M, F, D_MODEL, [dict(a=dh_b, b=w_down, trans_b=True, acc=0, resident=True)], [(g, 0), (u, 0)],
        swiglu_bwd, [BF16, BF16], 1, tm, F, D_MODEL, outer="i", stack=True, sub=256)
    tr = _tile(M, (2176, 256))
    (dw_down,) = _matmul_tn(tag + "_dwd", a, dh_b, tn, D_MODEL, tr, scale=0.5)
    dw_gu_t, *p_down = _matmul_tn(tag + "_dwgu", dgu, n, tn, D_MODEL, tr,
                                  comm=("scatter", [_to_rows(dw_down)]) if scatter else None)
    comm = None
    if scatter:
        comm = ("chips", _chip_sums(tag + "_wgu", [_by_core(dw_gu_t)]))
    def norm_bwd(accs, ex):
        dh_prev, dw = _rmsnorm_bwd_tile(accs[0], ex[0], ex[2], ex[1])
        return dh_prev, dh_prev, dw

    dh_prev, dh_prev_b, dnorm, *p_gu = _fused_matmul(
        tag + "_dn", M, D_MODEL, 2 * F,
        [dict(a=dgu, b=w_gu_t, acc=0, resident=True)], [(h, 0), (dh, 0)],
        norm_bwd, [F32, BF16], 1, tm, D_MODEL, 2 * F, outer="i", comm=comm, vecs=[norm_w], row_sums=1)
    return (dh_prev, dh_prev_b, dnorm, *((p_gu[0], p_down[0]) if scatter else (dw_gu_t, dw_down)))


def kernel(x, meta_tokens, ffn1_norm, ffn1_w_gu, ffn1_w_down, mix_norm, w_in, ssd_conv_w, ssd_conv_b, ssd_dt_bias, ssd_a_log, ssd_d, ssd_norm, hg_lower_bound, hg_norm, w_branch_a, w_branch_b, w_out, ffn2_norm, ffn2_w_gu, ffn2_w_down, final_norm, loss_target, m_meta_tokens, m_ffn1_norm, m_ffn1_w_gu, m_ffn1_w_down, m_mix_norm, m_w_in, m_ssd_conv_w, m_ssd_conv_b, m_ssd_dt_bias, m_ssd_a_log, m_ssd_d, m_ssd_norm, m_hg_lower_bound, m_hg_norm, m_w_branch_a, m_w_branch_b, m_w_out, m_ffn2_norm, m_ffn2_w_gu, m_ffn2_w_down, m_final_norm, v_meta_tokens, v_ffn1_norm, v_ffn1_w_gu, v_ffn1_w_down, v_mix_norm, v_w_in, v_ssd_conv_w, v_ssd_conv_b, v_ssd_dt_bias, v_ssd_a_log, v_ssd_d, v_ssd_norm, v_hg_lower_bound, v_hg_norm, v_w_branch_a, v_w_branch_b, v_w_out, v_ffn2_norm, v_ffn2_w_gu, v_ffn2_w_down, v_final_norm):
    Bl, S, D = x.shape
    T = PAD + N_META + S
    nc = T // Q
    M = Bl * T
    me = 4 * lax.axis_index("x") + 2 * lax.axis_index("y") + lax.axis_index("c")

    bf = lambda a: a[0].astype(BF16)
    bft = lambda a: a[0].T.astype(BF16)
    bias_p, alog_p, d_p = _heads_to_lanes(ssd_dt_bias), _heads_to_lanes(ssd_a_log), _heads_to_lanes(ssd_d)
    final_w = final_norm.reshape(1, D)

    h0, n1, g_wgu1, g_meta, g_conv_w = _embed_norm(
        x, ffn1_norm, comm=("gather", [bft(ffn1_w_gu), meta_tokens, ssd_conv_w[0]]))
    wgu1, meta_full, conv_w_full = _rows(g_wgu1), _cols(g_meta), _cols(g_conv_w)
    h0, n1 = _embed_meta(meta_full, ffn1_norm, h0, n1, Bl)
    tm = _tile(M, (1088, 544, 256))
    win_shard = bft(w_in)
    cut = (win_shard.shape[0] // 32) * 16
    ffn1_saved, (g_wd1, g_win_a) = _ffn_fwd_gu("ffn1", n1, wgu1, comm=("gather", [bf(ffn1_w_down), win_shard[:cut]]))
    wd1 = _rows(g_wd1)
    h1, un, g_win_b = _ffn_fwd_down("ffn1", h0, ffn1_saved[3], wd1, next_norm=mix_norm,
                                    comm=("gather", [win_shard[cut:]]))
    win_t = _rows(jnp.concatenate([g_win_a, g_win_b], axis=1))
    win_dt = jnp.pad(win_t[DT_ROW:DT_ROW + SSD_HEADS], ((0, 128 - SSD_HEADS), (0, 0)))
    plain = lambda accs, ex: (accs[0],)
    proj, g_wa, g_wb, g_wo = _fused_matmul(
        "in_proj", M, N_MAIN, D, [dict(a=un, b=win_t, trans_b=True, acc=0, b_shift=(DT_ROW // 3072, SSD_HEADS))], [],
        plain, [BF16], 1, tm, 3072, D,
        outer="j", comm=("gather", [bf(w_branch_a), bf(w_branch_b), bf(w_out)]), sub=512)
    wa, wb, wo = _rows(g_wa), _rows(g_wb), _rows(g_wo)
    (dtr,) = _fused_matmul("in_proj_dt", M, 128, D, [dict(a=un, b=win_dt, trans_b=True, acc=0)], [], plain, [F32], 1,
                           tm, 128, D, outer="j")
    xc = _conv_fwd(proj, conv_w_full, ssd_conv_b, Bl, T)
    ya, ssd_prev = _ssd_fwd(xc, dtr, proj, bias_p, alog_p, d_p, ssd_norm, Bl, nc)
    yb, hg_o, hg_st, g_wgu2, g_wd2 = _hgrn_fwd(proj, hg_lower_bound, hg_norm, Bl, nc,
                                               comm=("gather", [bft(ffn2_w_gu), bf(ffn2_w_down)]))
    wgu2, wd2 = _rows(g_wgu2), _rows(g_wd2)

    def branch_fwd(accs, ex):
        pa, pb = accs
        return pa, pb, _sigmoid(ex[0].astype(F32)) * pa + _sigmoid(ex[1].astype(F32)) * pb

    pa, pb, merged = _fused_matmul(
        "branches", M, D, D, [dict(a=ya, b=wa, acc=0), dict(a=yb, b=wb, acc=1)], [(proj, 7), (proj, 8)],
        branch_fwd, [BF16, BF16, BF16], 2, tm, D, D, outer="j")
    def out_with_norm(accs, ex):
        h_new = ex[0] + accs[0]
        return h_new, _rmsnorm_tile(h_new, ex[1])

    h2, n2 = _fused_matmul("out_proj", M, D, D, [dict(a=merged, b=wo, acc=0)], [(h1, 0)], out_with_norm,
                           [F32, BF16], 1, tm, D, D, outer="j", vecs=[ffn2_norm])
    ffn2_saved, _ = _ffn_fwd_gu("ffn2", n2, wgu2)
    h3 = _ffn_fwd_down("ffn2", h2, ffn2_saved[3], wd2)

    dh3, dh3_b, d_final, loss_part = _loss_head(h3, final_w, loss_target, Bl, nc)
    dh2, dh2_b, d_ffn2_norm, d_wgu2, d_wd2 = _ffn_bwd("ffn2", dh3, dh3_b, h2, ffn2_norm, wgu2, wd2, ffn2_saved)

    def branch_bwd(accs, ex):
        dm = accs[0]
        ga, gb, pav, pbv = (e.astype(F32) for e in ex)
        sa, sb = _sigmoid(ga), _sigmoid(gb)
        return (dm * sa, dm * sb,
                jnp.concatenate([dm * pav * sa * (1.0 - sa), dm * pbv * sb * (1.0 - sb)], axis=1))

    d_merged_outs = []

    def d_merged_with_swap(theirs):
        d_merged_outs.extend(_fused_matmul(
            "d_merged", M, D, D, [dict(a=dh2_b, b=wo, trans_b=True, acc=0)], [(proj, 7), (proj, 8), (pa, 0), (pb, 0)],
            branch_bwd, [BF16] * 2, 1, tm, D, D, outer="j", comm=("swap", theirs),
            wide=dict(width=2 * D, col=7 * D, total=N_MAIN, dtype=BF16)))
        return d_merged_outs[3:]

    s_ffn2 = _chip_sums("ffn2", [_by_core(d_wgu2), _by_core(d_wd2)], swap_in=d_merged_with_swap)
    dpa, dpb, dproj = d_merged_outs[:3]
    (d_wo,) = _matmul_tn("d_w_out", merged, dh2_b, 512, D, M)
    (d_wa,) = _matmul_tn("d_w_a", ya, dpa, 512, D, M)
    (d_wb,) = _matmul_tn("d_w_b", yb, dpb, 512, D, M)
    dya, dyb = _fused_matmul(
        "d_branches", M, D, D, [dict(a=dpa, b=wa, trans_b=True, acc=0), dict(a=dpb, b=wb, trans_b=True, acc=1)], [],
        lambda accs, ex: (accs[0], accs[1]), [BF16, BF16], 2, tm, D, D, outer="j")
    *ssd_grads, p_wgu2, p_wd2 = _ssd_bwd(xc, dtr, proj, bias_p, alog_p, d_p, ssd_norm, ssd_prev, dya, dproj, Bl, nc,
                                         comm=("chips", s_ffn2))
    dxc, dproj, ddtr, d_bias_p, d_alog_p, d_d_p, d_ssd_norm = ssd_grads
    dproj, d_conv_w, d_conv_b = _conv_bwd(proj, conv_w_full, ssd_conv_b, dxc, dproj, Bl, T)
    dproj, d_hb, d_hg_norm, p_wa, p_wb, p_wo = _hgrn_bwd(
        proj, hg_lower_bound, hg_norm, hg_o, hg_st, dyb, dproj, Bl, nc,
        comm=("scatter", [_to_rows(d_wa), _to_rows(d_wb), _to_rows(d_wo)]))
    ddtr_b = ddtr.astype(BF16)
    (d_win_t,) = _matmul_tn("d_w_in", dproj, un, 768, D, M, out_skip=(DT_ROW, SSD_HEADS))
    (d_win_dt,) = _matmul_tn("d_w_in_dt", ddtr_b, un, 128, D, M)
    d_win_t = lax.dynamic_update_slice(d_win_t, d_win_dt[:SSD_HEADS], (DT_ROW, 0))
    d_un_dt_outs = []

    def d_un_dt_with_swap(theirs):
        d_un_dt_outs.extend(_fused_matmul("d_un_dt", M, D, 128, [dict(a=ddtr_b, b=win_dt, acc=0)], [], plain, [F32], 1,
                                          tm, D, 128, outer="j", comm=("swap", theirs)))
        return d_un_dt_outs[1:]

    s_win = _chip_sums("w_in", [_by_core(d_win_t)], swap_in=d_un_dt_with_swap)
    def mix_norm_bwd(accs, ex):
        dh, dw = _rmsnorm_bwd_tile(accs[0] + ex[0], ex[1], ex[3], ex[2])
        return dh, dh, dw

    dh1, dh1_b, d_mix_norm, p_win = _fused_matmul(
        "d_un", M, D, N_MAIN, [dict(a=dproj, b=win_t, acc=0, b_shift=(DT_ROW // 3072, SSD_HEADS))],
        [(d_un_dt_outs[0], 0), (h1, 0), (dh2, 0)],
        mix_norm_bwd, [F32, BF16], 1, _tile(M, (544, 256)), D, 3072, outer="i", comm=("chips", s_win),
        vecs=[mix_norm], row_sums=1)
    dh0, _, d_ffn1_norm, p_wgu1, p_wd1 = _ffn_bwd("ffn1", dh1, dh1_b, h0, ffn1_norm, wgu1, wd1, ffn1_saved, scatter=True)

    dh0 = dh0.reshape(Bl, T, D)
    grad_x = dh0[:, PAD + N_META:]
    d_meta = dh0[:, PAD:PAD + N_META]

    small_grads = [d_ffn1_norm, d_mix_norm, d_conv_b, _lanes_to_heads(d_bias_p), _lanes_to_heads(d_alog_p),
                   _lanes_to_heads(d_d_p), d_ssd_norm, d_hb, d_hg_norm, d_ffn2_norm, d_final.reshape(D), d_conv_w]
    small_like = small_grads + [d_meta[b] for b in range(Bl)] + [loss_part[0, 0:1]]
    small_packed = _pack_rows(small_like)
    parts = [p_wgu1, p_wd1, p_win, p_wa, p_wb, p_wo, p_wgu2, p_wd2]

    names = ["meta_tokens", "ffn1_norm", "ffn1_w_gu", "ffn1_w_down", "mix_norm", "w_in", "ssd_conv_w", "ssd_conv_b",
             "ssd_dt_bias", "ssd_a_log", "ssd_d", "ssd_norm", "hg_lower_bound", "hg_norm", "w_branch_a", "w_branch_b",
             "w_out", "ffn2_norm", "ffn2_w_gu", "ffn2_w_down", "final_norm"]
    W = dict(meta_tokens=meta_tokens, ffn1_norm=ffn1_norm, ffn1_w_gu=ffn1_w_gu, ffn1_w_down=ffn1_w_down, mix_norm=mix_norm,
             w_in=w_in, ssd_conv_w=ssd_conv_w, ssd_conv_b=ssd_conv_b, ssd_dt_bias=ssd_dt_bias, ssd_a_log=ssd_a_log,
             ssd_d=ssd_d, ssd_norm=ssd_norm, hg_lower_bound=hg_lower_bound, hg_norm=hg_norm, w_branch_a=w_branch_a,
             w_branch_b=w_branch_b, w_out=w_out, ffn2_norm=ffn2_norm, ffn2_w_gu=ffn2_w_gu, ffn2_w_down=ffn2_w_down,
             final_norm=final_norm)
    Mo = dict(meta_tokens=m_meta_tokens, ffn1_norm=m_ffn1_norm, ffn1_w_gu=m_ffn1_w_gu, ffn1_w_down=m_ffn1_w_down,
              mix_norm=m_mix_norm, w_in=m_w_in, ssd_conv_w=m_ssd_conv_w, ssd_conv_b=m_ssd_conv_b, ssd_dt_bias=m_ssd_dt_bias,
              ssd_a_log=m_ssd_a_log, ssd_d=m_ssd_d, ssd_norm=m_ssd_norm, hg_lower_bound=m_hg_lower_bound, hg_norm=m_hg_norm,
              w_branch_a=m_w_branch_a, w_branch_b=m_w_branch_b, w_out=m_w_out, ffn2_norm=m_ffn2_norm, ffn2_w_gu=m_ffn2_w_gu,
              ffn2_w_down=m_ffn2_w_down, final_norm=m_final_norm)
    Vo = dict(meta_tokens=v_meta_tokens, ffn1_norm=v_ffn1_norm, ffn1_w_gu=v_ffn1_w_gu, ffn1_w_down=v_ffn1_w_down,
              mix_norm=v_mix_norm, w_in=v_w_in, ssd_conv_w=v_ssd_conv_w, ssd_conv_b=v_ssd_conv_b, ssd_dt_bias=v_ssd_dt_bias,
              ssd_a_log=v_ssd_a_log, ssd_d=v_ssd_d, ssd_norm=v_ssd_norm, hg_lower_bound=v_hg_lower_bound, hg_norm=v_hg_norm,
              w_branch_a=v_w_branch_a, w_branch_b=v_w_branch_b, w_out=v_w_out, ffn2_norm=v_ffn2_norm, ffn2_w_gu=v_ffn2_w_gu,
              ffn2_w_down=v_ffn2_w_down, final_norm=v_final_norm)
    grads, deltas, new_m, new_v = {}, {}, {}, {}
    big_names = ["ffn1_w_gu", "ffn1_w_down", "w_in", "w_branch_a", "w_branch_b", "w_out", "ffn2_w_gu", "ffn2_w_down"]
    transposed = ("ffn1_w_gu", "ffn2_w_gu", "w_in")
    small_all = None
    for nm, part in zip(big_names, parts):
        view = (lambda a: a[0].T) if nm in transposed else (lambda a: a[0])
        back = (lambda o: o.T[None]) if nm in transposed else (lambda o: o[None])
        outs = _adamw("adamw_" + nm, part, view(W[nm]), view(Mo[nm]), view(Vo[nm]),
                      comm=("gather", [small_packed]) if small_all is None else None)
        if small_all is None:
            small_all = outs[4]
        grads[nm], deltas[nm], new_m[nm], new_v[nm] = (back(o) for o in outs[:4])
    unpacked = _unpack_rows(_sum_parts("sum_small_grads", small_all), small_like)
    g_small = unpacked[:len(small_grads)]
    g_meta_full = unpacked[len(small_grads)]
    for b in range(1, Bl):
        g_meta_full = g_meta_full + unpacked[len(small_grads) + b]
    g_meta = lax.dynamic_slice_in_dim(g_meta_full, me * (D // N_DEV), D // N_DEV, axis=1)
    g_conv_w = lax.dynamic_slice_in_dim(g_small[11], me * (SSD_CONV_CH // N_DEV), SSD_CONV_CH // N_DEV, axis=1)
    loss = unpacked[-1].reshape(())
    small_names = ["ffn1_norm", "mix_norm", "ssd_conv_b", "ssd_dt_bias", "ssd_a_log", "ssd_d", "ssd_norm", "hg_lower_bound",
                   "hg_norm", "ffn2_norm", "final_norm", "ssd_conv_w", "meta_tokens"]
    small_g = g_small[:11] + [g_conv_w.reshape(ssd_conv_w.shape), g_meta]
    pk = lambda d: _pack_rows([d[nm] for nm in small_names])
    outs = _adamw("adamw_small", _pack_rows(small_g)[None], pk(W), pk(Mo), pk(Vo))
    like = [W[nm] for nm in small_names]
    for dst, o in zip((grads, deltas, new_m, new_v), outs):
        for nm, val in zip(small_names, _unpack_rows(o, like)):
            dst[nm] = val

    return (loss, grad_x, *[grads[nm] for nm in names], *[deltas[nm] for nm in names],
            *[new_m[nm] for nm in names], *[new_v[nm] for nm in names])
```

```python
import functools

import jax
import jax.numpy as jnp
from jax import lax
from jax.experimental import pallas as pl
from jax.experimental.pallas import tpu as pltpu

F32, BF16 = jnp.float32, jnp.bfloat16
NN, NT, TN = ((1,), (0,)), ((1,), (1,)), ((0,), (0,))
MESH_AXES = ("x", "y", "c")
N_DEV = 8

D_MODEL = 1024
N_META = 16
EPS = 1e-6
SSD_HEADS, SSD_HEAD_DIM, SSD_GROUPS, SSD_STATE, SSD_CONV, Q = 16, 64, 4, 128, 4, 128
SSD_INNER = SSD_HEADS * SSD_HEAD_DIM
SSD_CONV_CH = SSD_INNER + 2 * SSD_GROUPS * SSD_STATE
HG_WIDTH, HG_HEADS, HG_CHUNK = 1024, 8, 16
PAD = Q - N_META
N_MAIN = 9 * 1024
ADAM_LR, ADAM_B1, ADAM_B2, ADAM_EPS, ADAM_WD, ADAM_STEP = 0.001, 0.9, 0.999, 1e-08, 0.01, 10
VMEM_LIMIT = 52 * 1024 * 1024


def _dot(a, b, dims):
    return lax.dot_general(a, b, (dims, ((), ())), preferred_element_type=F32)


def _dot01(a, b, dims, sel):
    x, ax_x, s, ax_s = (b, dims[1][0], a, dims[0][0]) if sel == "a" else (a, dims[0][0], b, dims[1][0])
    hi = x.astype(BF16)
    r1 = x - hi.astype(F32)
    mid = r1.astype(BF16)
    lo = (r1 - mid.astype(F32)).astype(BF16)
    xs = jnp.concatenate([hi, mid, lo], axis=ax_x)
    ss = jnp.concatenate([s.astype(BF16)] * 3, axis=ax_s)
    return _dot(ss, xs, dims) if sel == "a" else _dot(xs, ss, dims)


def _sigmoid(x):
    return 0.5 * jnp.tanh(0.5 * x) + 0.5


def _dsilu(x, s):
    return s * (1.0 + x * (1.0 - s))


def _softplus(x):
    e = jnp.exp(-jnp.abs(x))
    u = 1.0 + e
    log1p_e = jnp.where(u == 1.0, e, jnp.log(u) * e / (u - 1.0))
    return jnp.maximum(x, 0.0) + log1p_e


def _params(sem):
    return pltpu.CompilerParams(dimension_semantics=sem, vmem_limit_bytes=VMEM_LIMIT)


def _tile(n, prefs):
    for p in prefs:
        if n % p == 0:
            return p
    return n


CHIP_FLIPS = ((1, 0), (0, 1), (1, 1))
N_PEER = N_DEV - 1


def _comm_gather(srcs, outs, send_sems, recv_sems, local_sems):
    n = len(srcs)
    x, y, c = (lax.axis_index(a) for a in MESH_AXES)
    dev = lambda px, py, pc: 4 * px + 2 * py + pc
    me, sib = dev(x, y, c), (x, y, 1 - c)
    nbr_x, nbr_y, diag = (1 - x, y), (x, 1 - y), (1 - x, 1 - y)
    via = (x ^ c, y ^ (1 - c), c)
    sent_on = dev(x ^ (1 - c), y ^ c, c)

    def rc(w, k, slot, to, src=None):
        return pltpu.make_async_remote_copy(
            src_ref=outs[w].at[slot] if src is None else src, dst_ref=outs[w].at[slot],
            send_sem=send_sems.at[w, k], recv_sem=recv_sems.at[w, k], device_id=to, device_id_type=pl.DeviceIdType.MESH)

    def local(w):
        return pltpu.make_async_copy(srcs[w], outs[w].at[me], local_sems.at[w])

    def start():
        for w in range(n):
            local(w).start()
            rc(w, 0, me, sib, src=srcs[w]).start()
            rc(w, 1, me, (*nbr_x, c), src=srcs[w]).start()
            rc(w, 2, me, (*nbr_y, c), src=srcs[w]).start()

    def pass_on():
        for w in range(n):
            rc(w, 1, dev(*nbr_x, c), sib).wait_recv()
            rc(w, 2, dev(*nbr_y, c), sib).wait_recv()
            rc(w, 3, sent_on, via).start()
            rc(w, 4, dev(*nbr_x, c), sib).start()
            rc(w, 5, dev(*nbr_y, c), sib).start()

    def pass_on_diagonal():
        for w in range(n):
            rc(w, 3, dev(*diag, c), sib).wait_recv()
            rc(w, 6, dev(*diag, c), sib).start()

    def finish():
        for w in range(n):
            rc(w, 0, dev(x, y, 1 - c), sib).wait_recv()
            for k, chip in ((4, nbr_x), (5, nbr_y), (6, diag)):
                rc(w, k, dev(*chip, 1 - c), sib).wait_recv()
            for k in range(N_PEER):
                rc(w, k, me, sib, src=srcs[w]).wait_send()
            local(w).wait()

    return start, (pass_on, pass_on_diagonal), finish


def _comm_scatter(srcs, outs, send_sems, recv_sems, local_sems):
    n = len(srcs)
    x, y, c = (lax.axis_index(a) for a in MESH_AXES)
    me = 4 * x + 2 * y + c

    def copies():
        out = []
        for w in range(n):
            out.append(pltpu.make_async_copy(srcs[w].at[me], outs[w].at[me], local_sems.at[w]))
            for k in range(1, N_DEV):
                px, py, pc = x ^ (k >> 2), y ^ ((k >> 1) & 1), c ^ (k & 1)
                out.append(pltpu.make_async_remote_copy(
                    src_ref=srcs[w].at[4 * px + 2 * py + pc], dst_ref=outs[w].at[me],
                    send_sem=send_sems.at[w, k - 1], recv_sem=recv_sems.at[w, k - 1],
                    device_id=(px, py, pc), device_id_type=pl.DeviceIdType.MESH))
        return out

    def start():
        for cp in copies():
            cp.start()

    def finish():
        for cp in copies():
            cp.wait()

    return start, None, finish


def _comm_swap(srcs, outs, send_sems, recv_sems, local_sems):
    x, y, c = (lax.axis_index(a) for a in MESH_AXES)

    def copies():
        return [pltpu.make_async_remote_copy(
            src_ref=srcs[w].at[1 - c], dst_ref=outs[w], send_sem=send_sems.at[w, 0], recv_sem=recv_sems.at[w, 0],
            device_id=(x, y, 1 - c), device_id_type=pl.DeviceIdType.MESH) for w in range(len(srcs))]

    def start():
        for cp in copies():
            cp.start()

    def finish():
        for cp in copies():
            cp.wait()

    return start, None, finish


def _comm_chips(srcs, outs, send_sems, recv_sems, local_sems):
    n = len(srcs)
    x, y, c = (lax.axis_index(a) for a in MESH_AXES)
    mine = 2 * x + y

    def copies():
        out = []
        for w in range(n):
            out.append(pltpu.make_async_copy(srcs[w].at[mine], outs[w].at[mine], local_sems.at[w]))
            for j, (fx, fy) in enumerate(CHIP_FLIPS):
                px, py = x ^ fx, y ^ fy
                out.append(pltpu.make_async_remote_copy(
                    src_ref=srcs[w].at[2 * px + py], dst_ref=outs[w].at[mine],
                    send_sem=send_sems.at[w, j], recv_sem=recv_sems.at[w, j],
                    device_id=(px, py, c), device_id_type=pl.DeviceIdType.MESH))
        return out

    def start():
        for cp in copies():
            cp.start()

    def finish():
        for cp in copies():
            cp.wait()

    return start, None, finish


def _comm_parts(comm):
    kind, arrays = comm[:2]
    n = len(arrays)
    lead = {"gather": lambda a: (N_DEV,) + a.shape, "scatter": lambda a: (N_DEV,) + a.shape[1:],
            "swap": lambda a: a.shape[1:], "chips": lambda a: a.shape}[kind]
    shapes = [jax.ShapeDtypeStruct(lead(a), a.dtype) for a in arrays]
    sems = [pltpu.SemaphoreType.DMA((n, N_PEER)), pltpu.SemaphoreType.DMA((n, N_PEER)), pltpu.SemaphoreType.DMA((n,))]
    make = {"gather": _comm_gather, "scatter": _comm_scatter, "swap": _comm_swap, "chips": _comm_chips}[kind]
    return n, shapes, sems, make


def _exchange(name, kind, arrays):
    n, shapes, sems, make = _comm_parts((kind, arrays))

    def body(*refs):
        start, middle, finish = make(refs[:n], refs[n:2 * n], *refs[2 * n:])
        start()
        for stage in middle or ():
            stage()
        finish()

    any_spec = pl.BlockSpec(memory_space=pl.ANY)
    return pl.pallas_call(
        body, name=name, in_specs=[any_spec] * n, out_specs=[any_spec] * n, out_shape=shapes, scratch_shapes=sems,
        compiler_params=pltpu.CompilerParams(has_side_effects=True),
    )(*arrays)


def _call(body, *, name, grid, in_specs, out_specs, out_shape, scratch, sem, args, comm=None, into=None):
    any_spec = pl.BlockSpec(memory_space=pl.ANY)
    in_specs, args, aliases, n_body_in = list(in_specs), list(args), {}, len(in_specs)
    for arr, k in ([] if into is None else into if isinstance(into, list) else [into]):
        aliases[len(in_specs)] = k
        in_specs.append(any_spec)
        args.append(arr)
    n_in, n_out, n_scr = len(in_specs), len(out_specs), len(scratch)
    if comm is None:
        def plain(*refs):
            body(*refs[:n_body_in], *refs[n_in:])

        return pl.pallas_call(plain, name=name, grid=grid, in_specs=in_specs, out_specs=out_specs, out_shape=out_shape,
                              scratch_shapes=scratch, input_output_aliases=aliases, compiler_params=_params(sem))(*args)
    n, shapes, sems, make = _comm_parts(comm)

    def carrier(*refs):
        ins, csrc = refs[:n_body_in], refs[n_in:n_in + n]
        outs, cout = refs[n_in + n:n_in + n + n_out], refs[n_in + n + n_out:n_in + 2 * n + n_out]
        rest = refs[n_in + 2 * n + n_out:]
        start, middle, finish = make(csrc, cout, *rest[n_scr:])
        ids = [pl.program_id(a) for a in range(len(grid))]
        step = functools.reduce(lambda acc, ig: acc * ig[1] + ig[0], zip(ids, grid), 0)
        n_steps = functools.reduce(lambda a, b: a * b, grid, 1)
        pl.when(step == 0)(start)
        body(*ins, *outs, *rest[:n_scr])
        if middle:
            pl.when(step == max(0, (3 * n_steps) // 4 - 1))(middle[0])
            pl.when(step == n_steps - 1)(middle[1])
        pl.when(step == n_steps - 1)(finish)

    return pl.pallas_call(
        carrier, name=name, grid=grid, in_specs=in_specs + [any_spec] * n,
        out_specs=list(out_specs) + [any_spec] * n, out_shape=list(out_shape) + shapes,
        scratch_shapes=list(scratch) + sems, input_output_aliases=aliases,
        compiler_params=pltpu.CompilerParams(dimension_semantics=("arbitrary",) * len(grid),
                                             vmem_limit_bytes=VMEM_LIMIT, has_side_effects=True),
    )(*args, *comm[1])


def _fused_matmul(name, M, N, K, pairs, extras, epilogue, out_dtypes, n_acc, tm, tn, tk, outer="i", comm=None,
                  stack=False, vecs=(), row_sums=0, wide=None, sub=None):
    nk = K // tk
    n_pairs, n_ex, n_out = len(pairs), len(extras), len(out_dtypes)
    assert not row_sums or (outer == "i" and N == tn)

    def ij(g0, g1):
        return (g0, g1) if outer == "i" else (g1, g0)

    in_specs, args = [], []
    for p in pairs:
        ao, bk, bn = p.get("a_off", 0), p.get("bk_off", 0), p.get("bn_off", 0)
        mode = dict(pipeline_mode=pl.Buffered(1)) if p.get("resident") else {}
        in_specs.append(pl.BlockSpec((tm, tk), lambda g0, g1, k, ao=ao: (ij(g0, g1)[0], k + ao)))
        if "b_shift" in p:
            first, shift = p["b_shift"]
            if p.get("trans_b"):
                in_specs.append(pl.BlockSpec(
                    (pl.Element(tn), pl.Element(tk)),
                    lambda g0, g1, k, bk=bk: (
                        pl.multiple_of(ij(g0, g1)[1] * tn + jnp.where(ij(g0, g1)[1] >= first, shift, 0), 16),
                        (k + bk) * tk)))
            else:
                in_specs.append(pl.BlockSpec(
                    (pl.Element(tk), pl.Element(tn)),
                    lambda g0, g1, k, bn=bn: (pl.multiple_of(k * tk + jnp.where(k >= first, shift, 0), 16),
                                              (ij(g0, g1)[1] + bn) * tn)))
        elif p.get("trans_b"):
            in_specs.append(pl.BlockSpec((tn, tk), lambda g0, g1, k, bk=bk, bn=bn: (ij(g0, g1)[1] + bn, k + bk), **mode))
        else:
            in_specs.append(pl.BlockSpec((tk, tn), lambda g0, g1, k, bk=bk, bn=bn: (k + bk, ij(g0, g1)[1] + bn), **mode))
        args += [p["a"], p["b"]]
    for arr, off in extras:
        in_specs.append(pl.BlockSpec((tm, tn), lambda g0, g1, k, off=off: (ij(g0, g1)[0], ij(g0, g1)[1] + off)))
        args.append(arr)
    for arr in vecs:
        in_specs.append(pl.BlockSpec((1, tn), lambda g0, g1, k: (0, ij(g0, g1)[1])))
        args.append(arr)
    if stack:
        assert N == tn
        out_specs = [pl.BlockSpec((tm, n_out * tn), lambda g0, g1, k: (ij(g0, g1)[0], 0))]
        out_shape = [jax.ShapeDtypeStruct((M, n_out * N), out_dtypes[0])]
    else:
        out_specs = [pl.BlockSpec((tm, tn), lambda g0, g1, k: ij(g0, g1)) for _ in out_dtypes]
        out_shape = [jax.ShapeDtypeStruct((M, N), dt) for dt in out_dtypes]
    if wide:
        out_specs.append(pl.BlockSpec((pl.Element(tm), pl.Element(wide["width"])),
                                      lambda g0, g1, k: (pl.multiple_of(ij(g0, g1)[0] * tm, 16), wide["col"])))
        out_shape.append(jax.ShapeDtypeStruct((M, wide["total"]), wide["dtype"]))
    n_tile_out = len(out_specs)
    out_specs += [pl.BlockSpec((1, tn), lambda g0, g1, k: (0, 0)) for _ in range(row_sums)]
    out_shape += [jax.ShapeDtypeStruct((1, N), F32) for _ in range(row_sums)]
    grid = (M // tm, N // tn, nk) if outer == "i" else (N // tn, M // tm, nk)
    n_in = 2 * n_pairs + n_ex + len(vecs)

    def partials(refs, cs=slice(None)):
        accs = [None] * n_acc
        for idx, p in enumerate(pairs):
            b_ref = refs[2 * idx + 1]
            d = (_dot(refs[2 * idx][...], b_ref[cs, :], NT) if p.get("trans_b")
                 else _dot(refs[2 * idx][...], b_ref[:, cs], NN))
            accs[p["acc"]] = d if accs[p["acc"]] is None else accs[p["acc"]] + d
        return accs

    def finish(accs, refs, first_rows, cs=slice(None)):
        res = epilogue(accs, [r[:, cs] for r in refs[2 * n_pairs:n_in]])
        if stack:
            o = refs[n_in]
            for idx in range(n_out):
                lo = idx * tn + (cs.start or 0)
                o[:, lo:lo + (tn if cs.stop is None else cs.stop - cs.start)] = res[idx].astype(o.dtype)
        else:
            for o, r in zip(refs[n_in:n_in + n_out], res):
                o[:, cs] = r.astype(o.dtype)
        if wide:
            o = refs[n_in + n_tile_out - 1]
            o[...] = res[n_out].astype(o.dtype)
        for o, r in zip(refs[n_in + n_tile_out:n_in + n_tile_out + row_sums], res[n_out + bool(wide):]):
            @pl.when(first_rows)
            def _(o=o, r=r):
                o[...] = r

            @pl.when(jnp.logical_not(first_rows))
            def _(o=o, r=r):
                o[...] += r

    if nk == 1 and sub:
        assert not wide and not row_sums and tn % sub == 0

        def body(*refs):
            for c in range(tn // sub):
                cs = slice(c * sub, (c + 1) * sub)
                finish(partials(refs, cs), refs, None, cs)
        scratch = []
    elif nk == 1:
        def body(*refs):
            finish(partials(refs), refs, pl.program_id(0) == 0)
        scratch = []
    else:
        def body(*refs):
            acc_refs = refs[-n_acc:]
            k = pl.program_id(2)
            first_rows = pl.program_id(0) == 0
            new = partials(refs)

            @pl.when(k == 0)
            def _():
                for a, v in zip(acc_refs, new):
                    a[...] = v

            @pl.when(k > 0)
            def _():
                for a, v in zip(acc_refs, new):
                    a[...] += v

            @pl.when(k == nk - 1)
            def _():
                finish([a[...] for a in acc_refs], refs, first_rows)
        scratch = [pltpu.VMEM((tm, tn), F32) for _ in range(n_acc)]

    return _call(body, name=name, grid=grid, in_specs=in_specs, out_specs=out_specs, out_shape=out_shape,
                 scratch=scratch, sem=("parallel", "parallel", "arbitrary"), args=args, comm=comm)


def _matmul_tn(name, x, y, t1, t2, tr, scale=1.0, comm=None, out_dtype=BF16, out_skip=None):
    R, K1 = x.shape
    N1 = y.shape[1]
    nr, n1 = R // tr, K1 // t1
    x_spec = pl.BlockSpec((tr, t1), lambda i, j, r: (r, i))
    rows_out = K1
    o_spec = pl.BlockSpec((t1, t2), lambda i, j, r: (i, j))
    if out_skip:
        row, count = out_skip
        rows_out += count
        o_spec = pl.BlockSpec(
            (pl.Element(t1), pl.Element(t2)),
            lambda i, j, r: (pl.multiple_of(i * t1 + jnp.where(i * t1 >= row, count, 0), 16), j * t2))

    def body(x_ref, y_ref, o_ref, *acc):
        d = _dot(x_ref[...], y_ref[...], TN)
        if nr == 1:
            o_ref[...] = (d * scale).astype(o_ref.dtype)
            return
        r = pl.program_id(2)

        @pl.when(r == 0)
        def _():
            acc[0][...] = d

        @pl.when(jnp.logical_and(r > 0, r < nr - 1))
        def _():
            acc[0][...] += d

        @pl.when(r == nr - 1)
        def _():
            o_ref[...] = ((acc[0][...] + d) * scale).astype(o_ref.dtype)

    return _call(
        body, name=name, grid=(n1, N1 // t2, nr),
        in_specs=[x_spec, pl.BlockSpec((tr, t2), lambda i, j, r: (r, j))], out_specs=[o_spec],
        out_shape=[jax.ShapeDtypeStruct((rows_out, N1), out_dtype)],
        scratch=[pltpu.VMEM((t1, t2), F32)] if nr > 1 else [],
        sem=("parallel", "parallel", "arbitrary"), args=(x, y), comm=comm)


def _embed_norm(x, w, comm=None):
    Bl, S, D = x.shape
    nb = (PAD + N_META + S) // Q
    M = Bl * nb * Q

    def body(x_ref, w_ref, h_ref, n_ref):
        h = x_ref[0]
        h_ref[...] = h
        n_ref[...] = _rmsnorm_tile(h, w_ref[...]).astype(n_ref.dtype)

    row = pl.BlockSpec((Q, D), lambda b, t: (b * nb + t + 1, 0))
    return _call(
        body, name="embed_norm", grid=(Bl, nb - 1),
        in_specs=[pl.BlockSpec((1, Q, D), lambda b, t: (b, t, 0)), pl.BlockSpec((1, D), lambda b, t: (0, 0))],
        out_specs=[row, row], out_shape=[jax.ShapeDtypeStruct((M, D), F32), jax.ShapeDtypeStruct((M, D), BF16)],
        scratch=[], sem=("parallel", "parallel"), args=(x, w), comm=comm)


def _embed_meta(meta, w, h0, n0, Bl):
    nb = h0.shape[0] // (Bl * Q)
    D = h0.shape[1]

    def body(meta_ref, w_ref, h_ref, n_ref):
        h = jnp.concatenate([jnp.zeros((PAD, D), F32), meta_ref[...]], axis=0)
        h_ref[...] = h
        n_ref[...] = _rmsnorm_tile(h, w_ref[...]).astype(n_ref.dtype)

    row = pl.BlockSpec((Q, D), lambda b: (b * nb, 0))
    return _call(
        body, name="embed_meta", grid=(Bl,),
        in_specs=[pl.BlockSpec((N_META, D), lambda b: (0, 0)), pl.BlockSpec((1, D), lambda b: (0, 0))],
        out_specs=[row, row], out_shape=[jax.ShapeDtypeStruct(h0.shape, F32), jax.ShapeDtypeStruct(n0.shape, BF16)],
        scratch=[], sem=("parallel",), args=(meta, w), into=[(h0, 0), (n0, 1)])


def _rmsnorm_bwd_tile(dn, h, w, dh_in):
    r = lax.rsqrt(jnp.mean(h * h, axis=-1, keepdims=True) + EPS)
    xhat = h * r
    gw = dn * w
    dh = dh_in + r * (gw - xhat * jnp.mean(gw * xhat, axis=-1, keepdims=True))
    return dh, jnp.sum(dn * xhat, axis=0, keepdims=True)


def _loss_head(h, w, target, Bl, nb):
    M, D = h.shape
    nt = 4 if (nb * Q) % 32 == 0 and nb * Q // 4 >= Q else nb
    half = nb * Q // nt

    def body(h_ref, w_ref, t_ref, dh_ref, dhb_ref, dw_ref, loss_ref):
        b, t = pl.program_id(0), pl.program_id(1)
        row = lax.broadcasted_iota(jnp.int32, (half, 1), 0)
        live = jnp.logical_or(t > 0, row >= Q).astype(F32)
        x = h_ref[...]
        r = lax.rsqrt(jnp.mean(x * x, axis=-1, keepdims=True) + EPS)
        xhat = x * r
        wv = w_ref[...]
        tgt = t_ref[0]
        tgt = jnp.where(t == 0, pltpu.roll(tgt, Q, 0), tgt)
        err = (xhat * wv - tgt) * live
        dy = err * (1.0 / D)
        gw = dy * wv
        dx = r * (gw - xhat * jnp.mean(gw * xhat, axis=-1, keepdims=True))
        dh_ref[...] = dx
        dhb_ref[...] = dx.astype(BF16)
        dw = jnp.sum(dy * xhat, axis=0, keepdims=True)
        part = 0.5 * jnp.sum(jnp.sum(err * err, axis=-1, keepdims=True) * (1.0 / D), axis=0, keepdims=True)
        first = jnp.logical_and(b == 0, t == 0)

        @pl.when(first)
        def _():
            dw_ref[...] = dw
            loss_ref[...] = jnp.broadcast_to(part, loss_ref.shape)

        @pl.when(jnp.logical_not(first))
        def _():
            dw_ref[...] += dw
            loss_ref[...] += jnp.broadcast_to(part, loss_ref.shape)

    row = pl.BlockSpec((half, D), lambda b, t: (b * nt + t, 0))
    vec = pl.BlockSpec((1, D), lambda b, t: (0, 0))
    return pl.pallas_call(
        body, name="loss_head", grid=(Bl, nt),
        in_specs=[row, vec, pl.BlockSpec((pl.Element(1), pl.Element(half), pl.Element(D)),
                                         lambda b, t: (b, pl.multiple_of(jnp.maximum(t * half - Q, 0), 8), 0))],
        out_specs=[row, row, vec, pl.BlockSpec((8, 128), lambda b, t: (0, 0))],
        out_shape=[jax.ShapeDtypeStruct((M, D), F32), jax.ShapeDtypeStruct((M, D), BF16),
                   jax.ShapeDtypeStruct((1, D), F32), jax.ShapeDtypeStruct((8, 128), F32)],
        compiler_params=_params(("arbitrary", "arbitrary")),
    )(h, w, target)


CONV_TC = 256


def _conv_pre(xr_ref, w_ref, b_ref):
    xb = xr_ref[...]
    x = xb.astype(F32)
    n_tap = SSD_CONV - 1
    r = lax.broadcasted_iota(jnp.int32, (Q * n_tap, 2 * Q), 0)
    c = lax.broadcasted_iota(jnp.int32, (Q * n_tap, 2 * Q), 1)
    move = (c == Q + r % Q - (r // Q + 1)).astype(BF16)
    taps = [[] for _ in range(n_tap)]
    for i in range(xb.shape[0] // Q):
        out = _dot(move[:, Q:], xb[0:Q], NN) if i == 0 else _dot(move, xb[Q * (i - 1):Q * (i + 1)], NN)
        for k in range(n_tap):
            taps[k].append(out[Q * k:Q * (k + 1)])
    moved = [jnp.concatenate(t, axis=0) for t in taps]
    acc = b_ref[...] + w_ref[SSD_CONV - 1:SSD_CONV, :] * x
    for k in range(1, SSD_CONV):
        acc = acc + w_ref[SSD_CONV - 1 - k:SSD_CONV - k, :] * moved[k - 1]
    return x, acc, moved


def _conv_fwd(proj, w, b, Bl, T):
    M = proj.shape[0]
    off = 1024 // CONV_TC

    def body(xr_ref, w_ref, b_ref, o_ref):
        _, acc, _ = _conv_pre(xr_ref, w_ref, b_ref)
        row = lax.broadcasted_iota(jnp.int32, acc.shape, 0)
        o_ref[...] = jnp.where(row >= PAD, acc * _sigmoid(acc), 0.0).astype(o_ref.dtype)

    return pl.pallas_call(
        body, name="conv_fwd", grid=(Bl, SSD_CONV_CH // CONV_TC),
        in_specs=[pl.BlockSpec((T, CONV_TC), lambda bb, j: (bb, j + off)),
                  pl.BlockSpec((SSD_CONV, CONV_TC), lambda bb, j: (0, j)), pl.BlockSpec((1, CONV_TC), lambda bb, j: (0, j))],
        out_specs=pl.BlockSpec((T, CONV_TC), lambda bb, j: (bb, j)),
        out_shape=jax.ShapeDtypeStruct((M, SSD_CONV_CH), BF16), compiler_params=_params(("parallel", "parallel")),
    )(proj, w, b)


def _conv_bwd(proj, w, b, dxc, dproj, Bl, T):
    M = proj.shape[0]
    off = 1024 // CONV_TC

    def body(xr_ref, w_ref, b_ref, d_ref, dx_ref, dw_ref, db_ref):
        x, acc, moved = _conv_pre(xr_ref, w_ref, b_ref)
        row = lax.broadcasted_iota(jnp.int32, acc.shape, 0)
        s = _sigmoid(acc)
        dpre = jnp.where(row >= PAD, d_ref[...].astype(F32) * _dsilu(acc, s), 0.0)
        dx = w_ref[SSD_CONV - 1:SSD_CONV, :] * dpre
        dws = [jnp.sum(dpre * x, axis=0, keepdims=True)]
        for k in range(1, SSD_CONV):
            dx = dx + w_ref[SSD_CONV - 1 - k:SSD_CONV - k, :] * pltpu.roll(dpre, T - k, 0)
            dws.append(jnp.sum(dpre * moved[k - 1], axis=0, keepdims=True))
        dx_ref[...] = dx.astype(dx_ref.dtype)
        dw = jnp.concatenate(dws[::-1], axis=0)
        db = jnp.sum(dpre, axis=0, keepdims=True)

        @pl.when(pl.program_id(1) == 0)
        def _():
            dw_ref[...] = dw
            db_ref[...] = db

        @pl.when(pl.program_id(1) > 0)
        def _():
            dw_ref[...] += dw
            db_ref[...] += db

    return _call(
        body, name="conv_bwd", grid=(SSD_CONV_CH // CONV_TC, Bl),
        in_specs=[pl.BlockSpec((T, CONV_TC), lambda j, bb: (bb, j + off)),
                  pl.BlockSpec((SSD_CONV, CONV_TC), lambda j, bb: (0, j)), pl.BlockSpec((1, CONV_TC), lambda j, bb: (0, j)),
                  pl.BlockSpec((T, CONV_TC), lambda j, bb: (bb, j))],
        out_specs=[pl.BlockSpec((T, CONV_TC), lambda j, bb: (bb, j + off)),
                   pl.BlockSpec((SSD_CONV, CONV_TC), lambda j, bb: (0, j)), pl.BlockSpec((1, CONV_TC), lambda j, bb: (0, j))],
        out_shape=[jax.ShapeDtypeStruct(dproj.shape, BF16), jax.ShapeDtypeStruct((SSD_CONV, SSD_CONV_CH), F32),
                   jax.ShapeDtypeStruct((1, SSD_CONV_CH), F32)],
        scratch=[], sem=("parallel", "arbitrary"), args=(proj, w, b, dxc), into=(dproj, 0))


N_PAIR = SSD_HEADS // 2
HPG = SSD_HEADS // SSD_GROUPS
GW = SSD_INNER // SSD_GROUPS


def _per_group(fn, *arrs):
    return jnp.concatenate([jnp.broadcast_to(fn(*(a[:, GW * g:GW * (g + 1)] for a in arrs)), (arrs[0].shape[0], GW))
                            for g in range(SSD_GROUPS)], axis=1)


def _ssd_prep(c, dtr_ref, bias_ref, alog_ref, d_ref):
    row = lax.broadcasted_iota(jnp.int32, (Q, 128), 0)
    col = lax.broadcasted_iota(jnp.int32, (Q, 128), 1)
    live = col < SSD_HEADS
    valid = jnp.logical_and(jnp.logical_or(c > 0, row >= PAD), live)
    pre = dtr_ref[...] + bias_ref[...]
    dt = jnp.where(valid, _softplus(pre), 0.0)
    A = jnp.where(live[0:1], -jnp.exp(alog_ref[...]), 0.0)
    tri = row >= col
    eye = (row == col).astype(BF16)
    cs = _dot01(tri, dt * A, NN, "a")
    cst = _dot01(eye, cs, NT, "a")
    spread = (lax.broadcasted_iota(jnp.int32, (128, SSD_INNER), 0)
              == lax.broadcasted_iota(jnp.int32, (128, SSD_INNER), 1) // SSD_HEAD_DIM).astype(BF16)
    dt_w = _dot01(dt, spread, NN, "b")
    cs_w = _dot01(cs, spread, NN, "b")
    d_w = _dot01(jnp.broadcast_to(d_ref[...], (8, 128)), spread, NN, "b")[0:1]
    lane = lax.broadcasted_iota(jnp.int32, (Q, SSD_INNER), 1)
    first = (lane % 128) < SSD_HEAD_DIM
    return dict(row=row, col=col, valid=valid, pre=pre, dt=dt, A=A, tri=tri, eye=eye, cs=cs, cst=cst, spread=spread,
                dt_w=dt_w, cs_w=cs_w, d_w=d_w, ecs_w=jnp.exp(cs_w), decay_w=jnp.exp(cs_w[Q - 1:Q] - cs_w), first=first)


def _ssd_chunk(xc_ref, s, states):
    xv = xc_ref[:, 0:SSD_INNER].astype(F32)
    Bs = [xc_ref[:, SSD_INNER + 128 * g:SSD_INNER + 128 * (g + 1)] for g in range(SSD_GROUPS)]
    Cs = [xc_ref[:, SSD_INNER + 512 + 128 * g:SSD_INNER + 512 + 128 * (g + 1)] for g in range(SSD_GROUPS)]
    X = xv * s["dt_w"]
    X0 = jnp.where(s["first"], X, 0.0)
    Xb = (X0.astype(BF16), (X - X0).astype(BF16))
    Xd = (X * s["decay_w"]).astype(BF16)
    CB = [_dot(Cs[g], Bs[g], NT) for g in range(SSD_GROUPS)]
    Lms = [jnp.exp(jnp.where(s["tri"], s["cs"][:, h:h + 1] - s["cst"][h:h + 1, :], -jnp.inf)) for h in range(SSD_HEADS)]
    Ms = [CB[h // HPG] * Lms[h] for h in range(SSD_HEADS)]
    Mb = [m.astype(BF16) for m in Ms]
    prev_b = [st.astype(BF16) for st in states]
    yds, yos, sts = [], [], []
    for p in range(N_PAIR):
        g, ln = p // 2, slice(128 * p, 128 * (p + 1))
        yds.append(_dot(Mb[2 * p], Xb[0][:, ln], NN) + _dot(Mb[2 * p + 1], Xb[1][:, ln], NN))
        yos.append(_dot(Cs[g], prev_b[p], NT))
        sts.append(_dot(Xd[:, ln], Bs[g], TN))
    yo = jnp.concatenate(yos, axis=1)
    y = jnp.concatenate(yds, axis=1) + yo * s["ecs_w"] + xv * s["d_w"]
    upper = s["row"] < SSD_HEAD_DIM
    cl = s["cs"][Q - 1:Q, :]
    ecl_rows = [jnp.where(upper, jnp.exp(cl[:, 2 * p:2 * p + 1]), jnp.exp(cl[:, 2 * p + 1:2 * p + 2])) for p in range(N_PAIR)]
    new_states = [states[p] * ecl_rows[p] + sts[p] for p in range(N_PAIR)]
    return y, new_states, dict(xv=xv, Bs=Bs, Cs=Cs, X=X, Xb=Xb, CB=CB, Lms=Lms, Ms=Ms, Mb=Mb, prev_b=prev_b, yo=yo,
                               ecl_rows=ecl_rows)


def _ssd_in_specs(nc, rev=False):
    rb = (lambda b, c: b * nc + nc - 1 - c) if rev else (lambda b, c: b * nc + c)
    vec = pl.BlockSpec((1, 128), lambda b, c: (0, 0))
    return [pl.BlockSpec((Q, SSD_CONV_CH), lambda b, c: (rb(b, c), 0)),
            pl.BlockSpec((Q, 128), lambda b, c: (rb(b, c), 0)),
            pl.BlockSpec((Q, SSD_INNER), lambda b, c: (rb(b, c), 0)),
            vec, vec, vec, pl.BlockSpec((1, SSD_INNER), lambda b, c: (0, 0))]


def _ssd_fwd(xc, dtr, proj, bias_p, alog_p, d_p, nw, Bl, nc):
    M = xc.shape[0]

    def body(xc_ref, dtr_ref, z_ref, bias_ref, alog_ref, d_ref, nw_ref, y_ref, prev_ref, state):
        c = pl.program_id(1)

        @pl.when(c == 0)
        def _():
            state[...] = jnp.zeros_like(state)

        s = _ssd_prep(c, dtr_ref, bias_ref, alog_ref, d_ref)
        states = [state[p] for p in range(N_PAIR)]
        y, new_states, _ = _ssd_chunk(xc_ref, s, states)
        for p in range(N_PAIR):
            prev_ref[0, 0, p] = states[p]
            state[p] = new_states[p]
        zz = z_ref[...].astype(F32)
        yg = y * zz * _sigmoid(zz)
        r = _per_group(lambda a: lax.rsqrt(jnp.mean(a * a, axis=-1, keepdims=True) + EPS), yg)
        y_ref[...] = (yg * r * nw_ref[...]).astype(y_ref.dtype)

    return pl.pallas_call(
        body, name="ssd_fwd", grid=(Bl, nc), in_specs=_ssd_in_specs(nc),
        out_specs=[pl.BlockSpec((Q, SSD_INNER), lambda b, c: (b * nc + c, 0)),
                   pl.BlockSpec((1, 1, N_PAIR, 128, 128), lambda b, c: (b, c, 0, 0, 0))],
        out_shape=[jax.ShapeDtypeStruct((M, SSD_INNER), BF16), jax.ShapeDtypeStruct((Bl, nc, N_PAIR, 128, 128), F32)],
        scratch_shapes=[pltpu.VMEM((N_PAIR, 128, 128), F32)],
        compiler_params=_params(("arbitrary", "arbitrary")),
    )(xc, dtr, proj, bias_p, alog_p, d_p, nw)


def _ssd_bwd(xc, dtr, proj, bias_p, alog_p, d_p, nw, prev, dya, dproj, Bl, nc, comm=None):
    M = xc.shape[0]

    def body(xc_ref, dtr_ref, z_ref, bias_ref, alog_ref, d_ref, nw_ref, prev_ref, dy_ref,
             dxc_ref, dz_ref, ddtr_ref, dbias_ref, dalog_ref, dd_ref, dnw_ref, dS):
        b, t = pl.program_id(0), pl.program_id(1)

        @pl.when(t == 0)
        def _():
            dS[...] = jnp.zeros_like(dS)

        s = _ssd_prep(nc - 1 - t, dtr_ref, bias_ref, alog_ref, d_ref)
        states = [prev_ref[0, 0, p] for p in range(N_PAIR)]
        y, _, k = _ssd_chunk(xc_ref, s, states)
        xv, Bs, Cs, Xb = k["xv"], k["Bs"], k["Cs"], k["Xb"]

        zz = z_ref[...].astype(F32)
        sz = _sigmoid(zz)
        silu_z = zz * sz
        yg = y * silu_z
        r = _per_group(lambda a: lax.rsqrt(jnp.mean(a * a, axis=-1, keepdims=True) + EPS), yg)
        xhat = yg * r
        dout = dy_ref[...].astype(F32)
        gw = dout * nw_ref[...]
        dyg = r * (gw - xhat * _per_group(lambda a, c2: jnp.mean(a * c2, axis=-1, keepdims=True), gw, xhat))
        dnw = jnp.sum(dout * xhat, axis=0, keepdims=True)
        dz_ref[...] = (dyg * y * _dsilu(zz, sz)).astype(dz_ref.dtype)
        dy = dyg * silu_z
        dy0 = jnp.where(s["first"], dy, 0.0)
        dyb = (dy0.astype(BF16), (dy - dy0).astype(BF16))
        dYo = (dy * s["ecs_w"]).astype(BF16)

        dS_f = [dS[p] for p in range(N_PAIR)]
        dS_b = [d.astype(BF16) for d in dS_f]
        BdS, dXm, dprev, dCs, dMs, XdS = [], [], [], [[] for _ in range(SSD_GROUPS)], [], []
        for p in range(N_PAIR):
            g, ln = p // 2, slice(128 * p, 128 * (p + 1))
            BdS.append(_dot(Bs[g], dS_b[p], NT))
            dXm.append(_dot(k["Mb"][2 * p], dyb[0][:, ln], TN) + _dot(k["Mb"][2 * p + 1], dyb[1][:, ln], TN))
            dprev.append(_dot(dYo[:, ln], Cs[g], TN))
            dCs[g].append(_dot(dYo[:, ln], k["prev_b"][p], NN))
            for hh in range(2):
                dMs.append(_dot(dyb[hh][:, ln], Xb[hh][:, ln], NT))
                XdS.append(_dot(Xb[hh][:, ln], dS_b[p], NN))
        dX = jnp.concatenate(dXm, axis=1) + s["decay_w"] * jnp.concatenate(BdS, axis=1)
        dxs = dy * s["d_w"] + dX * s["dt_w"]

        sums = _dot01(jnp.concatenate([dX * xv, dy * k["yo"] * s["ecs_w"], dy * xv], axis=0), s["spread"], NT, "b")
        ddt, dcs = sums[0:Q], sums[Q:2 * Q]
        dD = jnp.sum(sums[2 * Q:3 * Q], axis=0, keepdims=True)

        col, row = s["col"], s["row"]
        lane1 = col[0:1]
        rowsT = lax.broadcasted_iota(jnp.int32, (128, Q), 0)
        dcs_t = jnp.zeros((128, Q), F32)
        dcl = jnp.zeros((1, 128), F32)
        dB_out, dC_out = [], []
        for g in range(SSD_GROUPS):
            Bf = Bs[g].astype(F32)
            dCB = jnp.zeros((Q, Q), F32)
            dBacc = jnp.zeros((Q, 128), F32)
            for r4 in range(HPG):
                h = HPG * g + r4
                p, hh = h // 2, h % 2
                W = dMs[h] * k["Ms"][h]
                dCB = dCB + dMs[h] * k["Lms"][h]
                decay_h = s["decay_w"][:, SSD_HEAD_DIM * h:SSD_HEAD_DIM * h + 1]
                dBacc = dBacc + decay_h * XdS[h]
                tdec = jnp.sum(XdS[h] * Bf, axis=1, keepdims=True) * decay_h
                dcs = dcs + jnp.where(col == h, jnp.sum(W, axis=1, keepdims=True) - tdec, 0.0)
                dcs_t = dcs_t - jnp.where(rowsT == h, jnp.sum(W, axis=0, keepdims=True), 0.0)
                rows_h = (row < SSD_HEAD_DIM) if hh == 0 else (row >= SSD_HEAD_DIM)
                sprev = jnp.sum(jnp.sum(jnp.where(rows_h, dS_f[p] * states[p], 0.0), axis=1, keepdims=True),
                                axis=0, keepdims=True)
                ecl = jnp.exp(s["cs"][Q - 1:Q, h:h + 1])
                dcl = dcl + jnp.where(lane1 == h, jnp.sum(tdec, axis=0, keepdims=True) + ecl * sprev, 0.0)
            dCB_b = dCB.astype(BF16)
            dC_out.append(dCs[g][0] + dCs[g][1] + _dot(dCB_b, Bs[g], NN))
            dB_out.append(dBacc + _dot(dCB_b, Cs[g], TN))
        for p in range(N_PAIR):
            dS[p] = dS_f[p] * k["ecl_rows"][p] + dprev[p]
        dxc_ref[...] = jnp.concatenate([dxs] + dB_out + dC_out, axis=1).astype(dxc_ref.dtype)

        dcs = dcs + _dot01(s["eye"], dcs_t, NT, "a") + jnp.where(row == Q - 1, dcl, 0.0)
        da = _dot01(row <= col, dcs, NN, "a")
        ddt = ddt + da * s["A"]
        dpre = jnp.where(s["valid"], ddt * _sigmoid(s["pre"]), 0.0)
        ddtr_ref[...] = dpre
        dbias = jnp.sum(dpre, axis=0, keepdims=True)
        dalog = jnp.sum(da * s["dt"], axis=0, keepdims=True) * s["A"]
        first_step = jnp.logical_and(b == 0, t == 0)

        @pl.when(first_step)
        def _():
            dbias_ref[...] = dbias
            dalog_ref[...] = dalog
            dd_ref[...] = dD
            dnw_ref[...] = dnw

        @pl.when(jnp.logical_not(first_step))
        def _():
            dbias_ref[...] += dbias
            dalog_ref[...] += dalog
            dd_ref[...] += dD
            dnw_ref[...] += dnw

    rb = lambda b, c: b * nc + nc - 1 - c
    rowblk = lambda w: pl.BlockSpec((Q, w), lambda b, c: (rb(b, c), 0))
    vec = lambda w: pl.BlockSpec((1, w), lambda b, c: (0, 0))
    return _call(
        body, name="ssd_bwd", grid=(Bl, nc),
        in_specs=_ssd_in_specs(nc, rev=True) + [
            pl.BlockSpec((1, 1, N_PAIR, 128, 128), lambda b, c: (b, nc - 1 - c, 0, 0, 0)), rowblk(SSD_INNER)],
        out_specs=[rowblk(SSD_CONV_CH), rowblk(SSD_INNER), rowblk(128), vec(128), vec(128), vec(128), vec(SSD_INNER)],
        out_shape=[jax.ShapeDtypeStruct((M, SSD_CONV_CH), BF16), jax.ShapeDtypeStruct(dproj.shape, BF16),
                   jax.ShapeDtypeStruct((M, 128), F32), jax.ShapeDtypeStruct((1, 128), F32),
                   jax.ShapeDtypeStruct((1, 128), F32), jax.ShapeDtypeStruct((1, 128), F32),
                   jax.ShapeDtypeStruct((1, SSD_INNER), F32)],
        scratch=[pltpu.VMEM((N_PAIR, 128, 128), F32)], sem=("arbitrary", "arbitrary"),
        args=(xc, dtr, proj, bias_p, alog_p, d_p, nw, prev, dya), comm=comm, into=(dproj, 1))


NSUB = Q // HG_CHUNK
HG_HP = 8
EXP_CAP = 80.0


def _hg_setup(blk, q_ref, f_ref, hb_ref):
    row = lax.broadcasted_iota(jnp.int32, (Q, Q), 0)
    col = lax.broadcasted_iota(jnp.int32, (Q, Q), 1)
    same = (row // HG_CHUNK) == (col // HG_CHUNK)
    causal = jnp.logical_and(same, col <= row)
    lb = _sigmoid(hb_ref[0:1, :] - hb_ref[1:2, :])
    fl = f_ref[...].astype(F32)
    sg = _sigmoid(fl)
    fg = lb + (1.0 - lb) * sg
    k = (1.0 - lb) * (1.0 - sg)
    gl = jnp.log(fg)
    G = _dot01(causal, gl, NN, "a")
    T = _dot01(same, gl, NN, "a")
    qv = q_ref[...].astype(F32)
    sq = _sigmoid(qv)
    eG = jnp.exp(G)
    eGn = jnp.exp(jnp.minimum(-G, EXP_CAP))
    eTG = jnp.exp(T - G)
    qt = qv * sq * eG
    kt = k * eGn
    kh = k * eTG
    valid = jnp.logical_or(blk > 0, row[:, :1] >= PAD)
    return dict(row=row, col=col, same=same, causal=causal, lb=lb, sg=sg, fg=fg, k=k, T=T, qv=qv, sq=sq,
                eG=eG, eGn=eGn, eTG=eTG, qt=qt, kt=kt, kh=kh, valid=valid)


def _hg_specs(nb, rev=False):
    rb = (lambda h, b, t: b * nb + nb - 1 - t) if rev else (lambda h, b, t: b * nb + t)
    w = 128 * HG_HP
    blk = lambda off: pl.BlockSpec((Q, w), lambda h, b, t, off=off: (rb(h, b, t), off // HG_HP + h))
    return [blk(24), blk(32), blk(40), blk(48),
            pl.BlockSpec((2, w), lambda h, b, t: (0, h)), pl.BlockSpec((1, w), lambda h, b, t: (0, h))]


HEAD_LANES = tuple(slice(128 * hh, 128 * (hh + 1)) for hh in range(HG_HP))


def _per_head(fn, *arrs):
    return jnp.concatenate([jnp.broadcast_to(fn(*(a[:, ln] for a in arrs)), (arrs[0].shape[0], 128))
                            for ln in HEAD_LANES], axis=1)


def _hgrn_fwd(proj, hb, nw, Bl, nb, comm=None):
    M = proj.shape[0]

    def body(q_ref, f_ref, i_ref, g_ref, hb_ref, nw_ref, y_ref, o_ref, st_ref, S):
        blk = pl.program_id(2)

        @pl.when(blk == 0)
        def _():
            S[...] = jnp.zeros_like(S)

        s = _hg_setup(blk, q_ref, f_ref, hb_ref)
        v = i_ref[...]
        qt_b, kt_b, kh_b = s["qt"].astype(BF16), s["kt"].astype(BF16), s["kh"].astype(BF16)
        eT = jnp.exp(s["T"])
        att = [jnp.where(s["causal"], _dot(qt_b[:, ln], kt_b[:, ln], NT), 0.0).astype(BF16) for ln in HEAD_LANES]
        o_intra = [_dot(att[hh], v[:, ln], NN) for hh, ln in enumerate(HEAD_LANES)]
        for j in range(NSUB):
            sl = slice(HG_CHUNK * j, HG_CHUNK * (j + 1))
            for hh, ln in enumerate(HEAD_LANES):
                St = S[hh]
                st_ref[0, hh, 0, j] = St
                o_ref[sl, ln] = o_intra[hh][sl] + _dot(qt_b[sl, ln], St.astype(BF16), NT)
                S[hh] = St * eT[HG_CHUNK * j:HG_CHUNK * j + 1, ln] + _dot(v[sl, ln], kh_b[sl, ln], TN)
        o = o_ref[...]
        r = _per_head(lambda a: lax.rsqrt(jnp.mean(a * a, axis=-1, keepdims=True) + EPS), o)
        gv = g_ref[...].astype(F32)
        y_ref[...] = (o * r * nw_ref[...] * gv * _sigmoid(gv)).astype(y_ref.dtype)

    rowblk = pl.BlockSpec((Q, 128 * HG_HP), lambda h, b, t: (b * nb + t, h))
    return _call(
        body, name="hgrn_fwd", grid=(HG_HEADS // HG_HP, Bl, nb), in_specs=_hg_specs(nb),
        out_specs=[rowblk, rowblk,
                   pl.BlockSpec((1, HG_HP, 1, NSUB, 128, 128), lambda h, b, t: (b, h, t, 0, 0, 0))],
        out_shape=[jax.ShapeDtypeStruct((M, HG_WIDTH), BF16), jax.ShapeDtypeStruct((M, HG_WIDTH), F32),
                   jax.ShapeDtypeStruct((Bl, HG_HEADS, nb, NSUB, 128, 128), F32)],
        scratch=[pltpu.VMEM((HG_HP, 128, 128), F32)], sem=("parallel", "arbitrary", "arbitrary"),
        args=(proj, proj, proj, proj, hb, nw), comm=comm)


def _hgrn_bwd(proj, hb, nw, o_saved, st_saved, dyb, dproj, Bl, nb, comm=None):
    assert HG_HP == HG_HEADS

    def body(q_ref, f_ref, i_ref, g_ref, hb_ref, nw_ref, o_ref, st_ref, dy_ref,
             d_ref, dhb_ref, dnw_ref, dS, a_dqt, a_dv, a_dkh, a_dgl):
        b, t = pl.program_id(1), pl.program_id(2)

        @pl.when(t == 0)
        def _():
            dS[...] = jnp.zeros_like(dS)

        first_step = jnp.logical_and(b == 0, t == 0)
        s = _hg_setup(nb - 1 - t, q_ref, f_ref, hb_ref)
        v = i_ref[...]
        qt_b, kt_b, kh_b = s["qt"].astype(BF16), s["kt"].astype(BF16), s["kh"].astype(BF16)
        eT = jnp.exp(s["T"])
        att = [jnp.where(s["causal"], _dot(qt_b[:, ln], kt_b[:, ln], NT), 0.0).astype(BF16) for ln in HEAD_LANES]

        o = o_ref[...]
        r = _per_head(lambda a: lax.rsqrt(jnp.mean(a * a, axis=-1, keepdims=True) + EPS), o)
        xhat = o * r
        gv = g_ref[...].astype(F32)
        sgv = _sigmoid(gv)
        dyv = dy_ref[...].astype(F32)
        d_on = dyv * gv * sgv
        dg_out = dyv * xhat * nw_ref[...] * _dsilu(gv, sgv)
        gw = d_on * nw_ref[...]
        do = r * (gw - xhat * _per_head(lambda a, c: jnp.mean(a * c, axis=-1, keepdims=True), gw, xhat))
        dnw = jnp.sum(d_on * xhat, axis=0, keepdims=True)
        do_b = do.astype(BF16)

        datt = [jnp.where(s["causal"], _dot(do_b[:, ln], v[:, ln], NT), 0.0).astype(BF16) for ln in HEAD_LANES]
        dqt = jnp.concatenate([_dot(datt[hh], kt_b[:, ln], NN) for hh, ln in enumerate(HEAD_LANES)], axis=1)
        dkt = jnp.concatenate([_dot(datt[hh], qt_b[:, ln], TN) for hh, ln in enumerate(HEAD_LANES)], axis=1)
        dv = jnp.concatenate([_dot(att[hh], do_b[:, ln], TN) for hh, ln in enumerate(HEAD_LANES)], axis=1)
        last_row = (lax.broadcasted_iota(jnp.int32, (HG_CHUNK, 128), 0) == HG_CHUNK - 1)
        for j in reversed(range(NSUB)):
            sl = slice(HG_CHUNK * j, HG_CHUNK * (j + 1))
            for hh, ln in enumerate(HEAD_LANES):
                St = st_ref[0, hh, 0, j]
                dSt = dS[hh]
                St_b, dSt_b = St.astype(BF16), dSt.astype(BF16)
                eT_j = eT[HG_CHUNK * j:HG_CHUNK * j + 1, ln]
                dkh_j = _dot(v[sl, ln], dSt_b, NN)
                a_dqt[sl, ln] = _dot(do_b[sl, ln], St_b, NN)
                a_dv[sl, ln] = _dot(kh_b[sl, ln], dSt_b, NT)
                a_dkh[sl, ln] = dkh_j
                dlast = (jnp.sum(St * dSt, axis=0, keepdims=True) * eT_j
                         + jnp.sum(dkh_j * s["kh"][sl, ln], axis=0, keepdims=True))
                a_dgl[sl, ln] = jnp.where(last_row, dlast, 0.0)
                dS[hh] = dSt * eT_j + _dot(do_b[sl, ln], qt_b[sl, ln], TN)
        dqt = dqt + a_dqt[...]
        dv = dv + a_dv[...]
        dkh = a_dkh[...]
        dG = dqt * s["qt"] - dkt * s["kt"] - dkh * s["kh"] + a_dgl[...]
        rev_causal = jnp.logical_and(s["same"], s["col"] >= s["row"])
        dgl = _dot01(rev_causal, dG, NN, "a")
        dk = dkt * s["eGn"] + dkh * s["eTG"]
        dfg = dgl / s["fg"] - dk
        lb, sg = s["lb"], s["sg"]
        keep = s["valid"].astype(F32)
        d_ref[:, 0:w] = (dqt * s["eG"] * _dsilu(s["qv"], s["sq"]) * keep).astype(d_ref.dtype)
        d_ref[:, w:2 * w] = (dfg * (1.0 - lb) * sg * (1.0 - sg) * keep).astype(d_ref.dtype)
        d_ref[:, 2 * w:3 * w] = (dv * keep).astype(d_ref.dtype)
        d_ref[:, 3 * w:4 * w] = (dg_out * keep).astype(d_ref.dtype)
        dlb = jnp.sum(dfg * (1.0 - sg) * keep, axis=0, keepdims=True) * lb * (1.0 - lb)
        dhb = jnp.concatenate([dlb, -dlb], axis=0)

        @pl.when(first_step)
        def _():
            dhb_ref[...] = dhb
            dnw_ref[...] = dnw

        @pl.when(jnp.logical_not(first_step))
        def _():
            dhb_ref[...] += dhb
            dnw_ref[...] += dnw

    w = 128 * HG_HP
    rowblk = pl.BlockSpec((Q, w), lambda h, b, t: (b * nb + nb - 1 - t, h))
    return _call(
        body, name="hgrn_bwd", grid=(HG_HEADS // HG_HP, Bl, nb),
        in_specs=_hg_specs(nb, rev=True) + [
            rowblk, pl.BlockSpec((1, HG_HP, 1, NSUB, 128, 128), lambda h, b, t: (b, h, nb - 1 - t, 0, 0, 0)), rowblk],
        out_specs=[pl.BlockSpec((pl.Element(Q), pl.Element(4 * w)),
                                lambda h, b, t: (pl.multiple_of((b * nb + nb - 1 - t) * Q, Q), 3 * HG_WIDTH)),
                   pl.BlockSpec((2, w), lambda h, b, t: (0, h)), pl.BlockSpec((1, w), lambda h, b, t: (0, h))],
        out_shape=[jax.ShapeDtypeStruct(dproj.shape, BF16),
                   jax.ShapeDtypeStruct((2, HG_WIDTH), F32), jax.ShapeDtypeStruct((1, HG_WIDTH), F32)],
        scratch=[pltpu.VMEM((HG_HP, 128, 128), F32)] + [pltpu.VMEM((Q, w), F32)] * 4,
        sem=("parallel", "arbitrary", "arbitrary"),
        args=(proj, proj, proj, proj, hb, nw, o_saved, st_saved, dyb), comm=comm, into=(dproj, 0))


def _adamw(name, parts, w, m, v, comm=None):
    R, C = w.shape
    S = parts.shape[0]
    tr, tc = (_tile(R, (256, 176, 128, 64, 8)), C) if R % 8 == 0 else (R, 256)
    c1, c2 = 1.0 - ADAM_B1 ** ADAM_STEP, 1.0 - ADAM_B2 ** ADAM_STEP

    def body(p_ref, w_ref, m_ref, v_ref, g_ref, d_ref, nm_ref, nv_ref):
        g = p_ref[0].astype(F32)
        for s in range(1, S):
            g = g + p_ref[s].astype(F32)
        nm = ADAM_B1 * m_ref[...] + (1.0 - ADAM_B1) * g
        nv = ADAM_B2 * v_ref[...] + (1.0 - ADAM_B2) * (g * g)
        g_ref[...] = g
        nm_ref[...] = nm
        nv_ref[...] = nv
        d_ref[...] = -ADAM_LR * ((nm / c1) / (jnp.sqrt(nv / c2) + ADAM_EPS) + ADAM_WD * w_ref[...])

    blk = pl.BlockSpec((tr, tc), lambda i, j: (i, j))
    return _call(
        body, name=name, grid=(R // tr, C // tc),
        in_specs=[pl.BlockSpec((S, tr, tc), lambda i, j: (0, i, j)), blk, blk, blk], out_specs=[blk] * 4,
        out_shape=[jax.ShapeDtypeStruct((R, C), F32)] * 4, scratch=[], sem=("parallel", "parallel"),
        args=(parts, w, m, v), comm=comm)


def _pair_sum(name, by_core, arrived):
    _, J, R, C = by_core.shape
    tc = _tile(C, (512, 256, 128))

    def body(c_ref, a_ref, b_ref, o_ref):
        o_ref[...] = (a_ref[0].astype(F32) + b_ref[...].astype(F32)).astype(o_ref.dtype)

    blk = pl.BlockSpec((1, R, tc), lambda j, k, c_ref: (j, 0, k))
    return pl.pallas_call(
        body, name=name,
        grid_spec=pltpu.PrefetchScalarGridSpec(
            num_scalar_prefetch=1, grid=(J, C // tc),
            in_specs=[pl.BlockSpec((1, 1, R, tc), lambda j, k, c_ref: (c_ref[0], j, 0, k)), blk], out_specs=blk),
        out_shape=jax.ShapeDtypeStruct(arrived.shape, arrived.dtype), compiler_params=_params(("parallel", "parallel")),
    )(lax.axis_index("c").astype(jnp.int32).reshape(1), by_core, arrived)


def _sum_parts(name, parts):
    S, R, C = parts.shape

    def body(p_ref, o_ref):
        g = p_ref[0]
        for s in range(1, S):
            g = g + p_ref[s]
        o_ref[...] = g

    return pl.pallas_call(
        body, name=name, out_shape=jax.ShapeDtypeStruct((R, C), F32),
        in_specs=[pl.BlockSpec(memory_space=pltpu.VMEM)], out_specs=pl.BlockSpec(memory_space=pltpu.VMEM),
    )(parts)


def _heads_to_lanes(p):
    return jnp.pad(p, [(0, 0)] * (p.ndim - 1) + [(0, 128 - SSD_HEADS)])


def _lanes_to_heads(p):
    return p[..., :SSD_HEADS]


def _pack_rows(arrs):
    flat = jnp.concatenate([a.reshape(-1).astype(F32) for a in arrs])
    return jnp.pad(flat, (0, (-flat.shape[0]) % (8 * D_MODEL))).reshape(-1, D_MODEL)


def _unpack_rows(packed, like):
    flat, outs, at = packed.reshape(-1), [], 0
    for a in like:
        outs.append(flat[at:at + a.size].reshape(a.shape))
        at += a.size
    return outs


def _cols(gth):
    return jnp.transpose(gth, (1, 0, 2)).reshape(gth.shape[1], -1)


def _rows(gth):
    return gth.reshape(-1, gth.shape[2])


def _to_rows(g):
    return g.reshape(N_DEV, -1, g.shape[1]).astype(BF16)


def _by_core(g):
    return jnp.transpose(g.reshape(N_DEV // 2, 2, -1, g.shape[1]), (1, 0, 2, 3)).astype(BF16)


DT_ROW = 3072


def _chip_sums(tag, by_core, swap_in=None):
    arrived = swap_in(by_core) if swap_in else _exchange(tag + "_swap", "swap", by_core)
    return [_pair_sum(f"{tag}_chipsum{i}", m, a) for i, (m, a) in enumerate(zip(by_core, arrived))]


def _ffn_fwd_gu(tag, n, w_gu_t, comm=None):
    M = n.shape[0]
    F = w_gu_t.shape[0] // 2
    tm = _tile(M, (544, 256))
    outs = _fused_matmul(
        tag + "_gu", M, F, D_MODEL,
        [dict(a=n, b=w_gu_t, trans_b=True, acc=0, resident=True),
         dict(a=n, b=w_gu_t, trans_b=True, bn_off=1, acc=1, resident=True)], [],
        lambda accs, ex: (accs[0], accs[1], accs[0] * _sigmoid(accs[0]) * accs[1]),
        [BF16, BF16, BF16], 2, tm, F, D_MODEL, outer="i", comm=comm, sub=256)
    return (n, *outs[:3]), outs[3:]


def _rmsnorm_tile(x, w):
    return x * lax.rsqrt(jnp.mean(x * x, axis=-1, keepdims=True) + EPS) * w


def _ffn_fwd_down(tag, h, a, w_down, next_norm=None, comm=None):
    M = h.shape[0]
    F = w_down.shape[0]
    tm = _tile(M, (1088, 544, 256))
    if next_norm is None:
        (h_out,) = _fused_matmul(
            tag + "_down", M, D_MODEL, F, [dict(a=a, b=w_down, acc=0)], [(h, 0)],
            lambda accs, ex: (ex[0] + 0.5 * accs[0],), [F32], 1, tm, D_MODEL, F, outer="j", sub=256)
        return h_out

    def with_norm(accs, ex):
        h_new = ex[0] + 0.5 * accs[0]
        return h_new, _rmsnorm_tile(h_new, ex[1])

    return _fused_matmul(tag + "_down", M, D_MODEL, F, [dict(a=a, b=w_down, acc=0, resident=True)], [(h, 0)], with_norm,
                         [F32, BF16], 1, tm, D_MODEL, F, outer="j", vecs=[next_norm], comm=comm)


def _ffn_bwd(tag, dh, dh_b, h, norm_w, w_gu_t, w_down, saved, scatter=False):
    n, g, u, a = saved
    M = h.shape[0]
    F = w_down.shape[0]
    tm = _tile(M, (544, 256))
    tn = _tile(F, (1408, 704, 256))

    def swiglu_bwd(accs, ex):
        da, gv, uv = 0.5 * accs[0], ex[0].astype(F32), ex[1].astype(F32)
        s = _sigmoid(gv)
        return da * uv * _dsilu(gv, s), da * gv * s

    (dgu,) = _fused_matmul(
        tag + "_dact", M, F, D_MODEL, [dict(a=dh_b, b=w_down, trans_b=True, acc=0, resident=True)], [(g, 0), (u, 0)],
        swiglu_bwd, [BF16, BF16], 1, tm, F, D_MODEL, outer="i", stack=True, sub=256)
    tr = _tile(M, (2176, 256))
    (dw_down,) = _matmul_tn(tag + "_dwd", a, dh_b, tn, D_MODEL, tr, scale=0.5)
    dw_gu_t, *p_down = _matmul_tn(tag + "_dwgu", dgu, n, tn, D_MODEL, tr,
                                  comm=("scatter", [_to_rows(dw_down)]) if scatter else None)
    comm = None
    if scatter:
        comm = ("chips", _chip_sums(tag + "_wgu", [_by_core(dw_gu_t)]))
    def norm_bwd(accs, ex):
        dh_prev, dw = _rmsnorm_bwd_tile(accs[0], ex[0], ex[2], ex[1])
        return dh_prev, dh_prev, dw

    dh_prev, dh_prev_b, dnorm, *p_gu = _fused_matmul(
        tag + "_dn", M, D_MODEL, 2 * F,
        [dict(a=dgu, b=w_gu_t, acc=0, resident=True)], [(h, 0), (dh, 0)],
        norm_bwd, [F32, BF16], 1, tm, D_MODEL, 2 * F, outer="i", comm=comm, vecs=[norm_w], row_sums=1)
    return (dh_prev, dh_prev_b, dnorm, *((p_gu[0], p_down[0]) if scatter else (dw_gu_t, dw_down)))


def kernel(x, meta_tokens, ffn1_norm, ffn1_w_gu, ffn1_w_down, mix_norm, w_in, ssd_conv_w, ssd_conv_b, ssd_dt_bias, ssd_a_log, ssd_d, ssd_norm, hg_lower_bound, hg_norm, w_branch_a, w_branch_b, w_out, ffn2_norm, ffn2_w_gu, ffn2_w_down, final_norm, loss_target, m_meta_tokens, m_ffn1_norm, m_ffn1_w_gu, m_ffn1_w_down, m_mix_norm, m_w_in, m_ssd_conv_w, m_ssd_conv_b, m_ssd_dt_bias, m_ssd_a_log, m_ssd_d, m_ssd_norm, m_hg_lower_bound, m_hg_norm, m_w_branch_a, m_w_branch_b, m_w_out, m_ffn2_norm, m_ffn2_w_gu, m_ffn2_w_down, m_final_norm, v_meta_tokens, v_ffn1_norm, v_ffn1_w_gu, v_ffn1_w_down, v_mix_norm, v_w_in, v_ssd_conv_w, v_ssd_conv_b, v_ssd_dt_bias, v_ssd_a_log, v_ssd_d, v_ssd_norm, v_hg_lower_bound, v_hg_norm, v_w_branch_a, v_w_branch_b, v_w_out, v_ffn2_norm, v_ffn2_w_gu, v_ffn2_w_down, v_final_norm):
    Bl, S, D = x.shape
    T = PAD + N_META + S
    nc = T // Q
    M = Bl * T
    me = 4 * lax.axis_index("x") + 2 * lax.axis_index("y") + lax.axis_index("c")

    bf = lambda a: a[0].astype(BF16)
    bft = lambda a: a[0].T.astype(BF16)
    bias_p, alog_p, d_p = _heads_to_lanes(ssd_dt_bias), _heads_to_lanes(ssd_a_log), _heads_to_lanes(ssd_d)
    final_w = final_norm.reshape(1, D)

    h0, n1, g_wgu1, g_meta, g_conv_w = _embed_norm(
        x, ffn1_norm, comm=("gather", [bft(ffn1_w_gu), meta_tokens, ssd_conv_w[0]]))
    wgu1, meta_full, conv_w_full = _rows(g_wgu1), _cols(g_meta), _cols(g_conv_w)
    h0, n1 = _embed_meta(meta_full, ffn1_norm, h0, n1, Bl)
    tm = _tile(M, (1088, 544, 256))
    win_shard = bft(w_in)
    cut = (win_shard.shape[0] // 32) * 16
    ffn1_saved, (g_wd1, g_win_a) = _ffn_fwd_gu("ffn1", n1, wgu1, comm=("gather", [bf(ffn1_w_down), win_shard[:cut]]))
    wd1 = _rows(g_wd1)
    h1, un, g_win_b = _ffn_fwd_down("ffn1", h0, ffn1_saved[3], wd1, next_norm=mix_norm,
                                    comm=("gather", [win_shard[cut:]]))
    win_t = _rows(jnp.concatenate([g_win_a, g_win_b], axis=1))
    win_dt = jnp.pad(win_t[DT_ROW:DT_ROW + SSD_HEADS], ((0, 128 - SSD_HEADS), (0, 0)))
    plain = lambda accs, ex: (accs[0],)
    proj, g_wa, g_wb, g_wo = _fused_matmul(
        "in_proj", M, N_MAIN, D, [dict(a=un, b=win_t, trans_b=True, acc=0, b_shift=(DT_ROW // 3072, SSD_HEADS))], [],
        plain, [BF16], 1, tm, 3072, D,
        outer="j", comm=("gather", [bf(w_branch_a), bf(w_branch_b), bf(w_out)]), sub=512)
    wa, wb, wo = _rows(g_wa), _rows(g_wb), _rows(g_wo)
    (dtr,) = _fused_matmul("in_proj_dt", M, 128, D, [dict(a=un, b=win_dt, trans_b=True, acc=0)], [], plain, [F32], 1,
                           tm, 128, D, outer="j")
    xc = _conv_fwd(proj, conv_w_full, ssd_conv_b, Bl, T)
    ya, ssd_prev = _ssd_fwd(xc, dtr, proj, bias_p, alog_p, d_p, ssd_norm, Bl, nc)
    yb, hg_o, hg_st, g_wgu2, g_wd2 = _hgrn_fwd(proj, hg_lower_bound, hg_norm, Bl, nc,
                                               comm=("gather", [bft(ffn2_w_gu), bf(ffn2_w_down)]))
    wgu2, wd2 = _rows(g_wgu2), _rows(g_wd2)

    def branch_fwd(accs, ex):
        pa, pb = accs
        return pa, pb, _sigmoid(ex[0].astype(F32)) * pa + _sigmoid(ex[1].astype(F32)) * pb

    pa, pb, merged = _fused_matmul(
        "branches", M, D, D, [dict(a=ya, b=wa, acc=0), dict(a=yb, b=wb, acc=1)], [(proj, 7), (proj, 8)],
        branch_fwd, [BF16, BF16, BF16], 2, tm, D, D, outer="j")
    def out_with_norm(accs, ex):
        h_new = ex[0] + accs[0]
        return h_new, _rmsnorm_tile(h_new, ex[1])

    h2, n2 = _fused_matmul("out_proj", M, D, D, [dict(a=merged, b=wo, acc=0)], [(h1, 0)], out_with_norm,
                           [F32, BF16], 1, tm, D, D, outer="j", vecs=[ffn2_norm])
    ffn2_saved, _ = _ffn_fwd_gu("ffn2", n2, wgu2)
    h3 = _ffn_fwd_down("ffn2", h2, ffn2_saved[3], wd2)

    dh3, dh3_b, d_final, loss_part = _loss_head(h3, final_w, loss_target, Bl, nc)
    dh2, dh2_b, d_ffn2_norm, d_wgu2, d_wd2 = _ffn_bwd("ffn2", dh3, dh3_b, h2, ffn2_norm, wgu2, wd2, ffn2_saved)

    def branch_bwd(accs, ex):
        dm = accs[0]
        ga, gb, pav, pbv = (e.astype(F32) for e in ex)
        sa, sb = _sigmoid(ga), _sigmoid(gb)
        return (dm * sa, dm * sb,
                jnp.concatenate([dm * pav * sa * (1.0 - sa), dm * pbv * sb * (1.0 - sb)], axis=1))

    d_merged_outs = []

    def d_merged_with_swap(theirs):
        d_merged_outs.extend(_fused_matmul(
            "d_merged", M, D, D, [dict(a=dh2_b, b=wo, trans_b=True, acc=0)], [(proj, 7), (proj, 8), (pa, 0), (pb, 0)],
            branch_bwd, [BF16] * 2, 1, tm, D, D, outer="j", comm=("swap", theirs),
            wide=dict(width=2 * D, col=7 * D, total=N_MAIN, dtype=BF16)))
        return d_merged_outs[3:]

    s_ffn2 = _chip_sums("ffn2", [_by_core(d_wgu2), _by_core(d_wd2)], swap_in=d_merged_with_swap)
    dpa, dpb, dproj = d_merged_outs[:3]
    (d_wo,) = _matmul_tn("d_w_out", merged, dh2_b, 512, D, M)
    (d_wa,) = _matmul_tn("d_w_a", ya, dpa, 512, D, M)
    (d_wb,) = _matmul_tn("d_w_b", yb, dpb, 512, D, M)
    dya, dyb = _fused_matmul(
        "d_branches", M, D, D, [dict(a=dpa, b=wa, trans_b=True, acc=0), dict(a=dpb, b=wb, trans_b=True, acc=1)], [],
        lambda accs, ex: (accs[0], accs[1]), [BF16, BF16], 2, tm, D, D, outer="j")
    *ssd_grads, p_wgu2, p_wd2 = _ssd_bwd(xc, dtr, proj, bias_p, alog_p, d_p, ssd_norm, ssd_prev, dya, dproj, Bl, nc,
                                         comm=("chips", s_ffn2))
    dxc, dproj, ddtr, d_bias_p, d_alog_p, d_d_p, d_ssd_norm = ssd_grads
    dproj, d_conv_w, d_conv_b = _conv_bwd(proj, conv_w_full, ssd_conv_b, dxc, dproj, Bl, T)
    dproj, d_hb, d_hg_norm, p_wa, p_wb, p_wo = _hgrn_bwd(
        proj, hg_lower_bound, hg_norm, hg_o, hg_st, dyb, dproj, Bl, nc,
        comm=("scatter", [_to_rows(d_wa), _to_rows(d_wb), _to_rows(d_wo)]))
    ddtr_b = ddtr.astype(BF16)
    (d_win_t,) = _matmul_tn("d_w_in", dproj, un, 768, D, M, out_skip=(DT_ROW, SSD_HEADS))
    (d_win_dt,) = _matmul_tn("d_w_in_dt", ddtr_b, un, 128, D, M)
    d_win_t = lax.dynamic_update_slice(d_win_t, d_win_dt[:SSD_HEADS], (DT_ROW, 0))
    d_un_dt_outs = []

    def d_un_dt_with_swap(theirs):
        d_un_dt_outs.extend(_fused_matmul("d_un_dt", M, D, 128, [dict(a=ddtr_b, b=win_dt, acc=0)], [], plain, [F32], 1,
                                          tm, D, 128, outer="j", comm=("swap", theirs)))
        return d_un_dt_outs[1:]

    s_win = _chip_sums("w_in", [_by_core(d_win_t)], swap_in=d_un_dt_with_swap)
    def mix_norm_bwd(accs, ex):
        dh, dw = _rmsnorm_bwd_tile(accs[0] + ex[0], ex[1], ex[3], ex[2])
        return dh, dh, dw

    dh1, dh1_b, d_mix_norm, p_win = _fused_matmul(
        "d_un", M, D, N_MAIN, [dict(a=dproj, b=win_t, acc=0, b_shift=(DT_ROW // 3072, SSD_HEADS))],
        [(d_un_dt_outs[0], 0), (h1, 0), (dh2, 0)],
        mix_norm_bwd, [F32, BF16], 1, _tile(M, (544, 256)), D, 3072, outer="i", comm=("chips", s_win),
        vecs=[mix_norm], row_sums=1)
    dh0, _, d_ffn1_norm, p_wgu1, p_wd1 = _ffn_bwd("ffn1", dh1, dh1_b, h0, ffn1_norm, wgu1, wd1, ffn1_saved, scatter=True)

    dh0 = dh0.reshape(Bl, T, D)
    grad_x = dh0[:, PAD + N_META:]
    d_meta = dh0[:, PAD:PAD + N_META]

    small_grads = [d_ffn1_norm, d_mix_norm, d_conv_b, _lanes_to_heads(d_bias_p), _lanes_to_heads(d_alog_p),
                   _lanes_to_heads(d_d_p), d_ssd_norm, d_hb, d_hg_norm, d_ffn2_norm, d_final.reshape(D), d_conv_w]
    small_like = small_grads + [d_meta[b] for b in range(Bl)] + [loss_part[0, 0:1]]
    small_packed = _pack_rows(small_like)
    parts = [p_wgu1, p_wd1, p_win, p_wa, p_wb, p_wo, p_wgu2, p_wd2]

    names = ["meta_tokens", "ffn1_norm", "ffn1_w_gu", "ffn1_w_down", "mix_norm", "w_in", "ssd_conv_w", "ssd_conv_b",
             "ssd_dt_bias", "ssd_a_log", "ssd_d", "ssd_norm", "hg_lower_bound", "hg_norm", "w_branch_a", "w_branch_b",
             "w_out", "ffn2_norm", "ffn2_w_gu", "ffn2_w_down", "final_norm"]
    W = dict(meta_tokens=meta_tokens, ffn1_norm=ffn1_norm, ffn1_w_gu=ffn1_w_gu, ffn1_w_down=ffn1_w_down, mix_norm=mix_norm,
             w_in=w_in, ssd_conv_w=ssd_conv_w, ssd_conv_b=ssd_conv_b, ssd_dt_bias=ssd_dt_bias, ssd_a_log=ssd_a_log,
             ssd_d=ssd_d, ssd_norm=ssd_norm, hg_lower_bound=hg_lower_bound, hg_norm=hg_norm, w_branch_a=w_branch_a,
             w_branch_b=w_branch_b, w_out=w_out, ffn2_norm=ffn2_norm, ffn2_w_gu=ffn2_w_gu, ffn2_w_down=ffn2_w_down,
             final_norm=final_norm)
    Mo = dict(meta_tokens=m_meta_tokens, ffn1_norm=m_ffn1_norm, ffn1_w_gu=m_ffn1_w_gu, ffn1_w_down=m_ffn1_w_down,
              mix_norm=m_mix_norm, w_in=m_w_in, ssd_conv_w=m_ssd_conv_w, ssd_conv_b=m_ssd_conv_b, ssd_dt_bias=m_ssd_dt_bias,
              ssd_a_log=m_ssd_a_log, ssd_d=m_ssd_d, ssd_norm=m_ssd_norm, hg_lower_bound=m_hg_lower_bound, hg_norm=m_hg_norm,
              w_branch_a=m_w_branch_a, w_branch_b=m_w_branch_b, w_out=m_w_out, ffn2_norm=m_ffn2_norm, ffn2_w_gu=m_ffn2_w_gu,
              ffn2_w_down=m_ffn2_w_down, final_norm=m_final_norm)
    Vo = dict(meta_tokens=v_meta_tokens, ffn1_norm=v_ffn1_norm, ffn1_w_gu=v_ffn1_w_gu, ffn1_w_down=v_ffn1_w_down,
              mix_norm=v_mix_norm, w_in=v_w_in, ssd_conv_w=v_ssd_conv_w, ssd_conv_b=v_ssd_conv_b, ssd_dt_bias=v_ssd_dt_bias,
              ssd_a_log=v_ssd_a_log, ssd_d=v_ssd_d, ssd_norm=v_ssd_norm, hg_lower_bound=v_hg_lower_bound, hg_norm=v_hg_norm,
              w_branch_a=v_w_branch_a, w_branch_b=v_w_branch_b, w_out=v_w_out, ffn2_norm=v_ffn2_norm, ffn2_w_gu=v_ffn2_w_gu,
              ffn2_w_down=v_ffn2_w_down, final_norm=v_final_norm)
    grads, deltas, new_m, new_v = {}, {}, {}, {}
    big_names = ["ffn1_w_gu", "ffn1_w_down", "w_in", "w_branch_a", "w_branch_b", "w_out", "ffn2_w_gu", "ffn2_w_down"]
    transposed = ("ffn1_w_gu", "ffn2_w_gu", "w_in")
    small_all = None
    for nm, part in zip(big_names, parts):
        view = (lambda a: a[0].T) if nm in transposed else (lambda a: a[0])
        back = (lambda o: o.T[None]) if nm in transposed else (lambda o: o[None])
        outs = _adamw("adamw_" + nm, part, view(W[nm]), view(Mo[nm]), view(Vo[nm]),
                      comm=("gather", [small_packed]) if small_all is None else None)
        if small_all is None:
            small_all = outs[4]
        grads[nm], deltas[nm], new_m[nm], new_v[nm] = (back(o) for o in outs[:4])
    unpacked = _unpack_rows(_sum_parts("sum_small_grads", small_all), small_like)
    g_small = unpacked[:len(small_grads)]
    g_meta_full = unpacked[len(small_grads)]
    for b in range(1, Bl):
        g_meta_full = g_meta_full + unpacked[len(small_grads) + b]
    g_meta = lax.dynamic_slice_in_dim(g_meta_full, me * (D // N_DEV), D // N_DEV, axis=1)
    g_conv_w = lax.dynamic_slice_in_dim(g_small[11], me * (SSD_CONV_CH // N_DEV), SSD_CONV_CH // N_DEV, axis=1)
    loss = unpacked[-1].reshape(())
    small_names = ["ffn1_norm", "mix_norm", "ssd_conv_b", "ssd_dt_bias", "ssd_a_log", "ssd_d", "ssd_norm", "hg_lower_bound",
                   "hg_norm", "ffn2_norm", "final_norm", "ssd_conv_w", "meta_tokens"]
    small_g = g_small[:11] + [g_conv_w.reshape(ssd_conv_w.shape), g_meta]
    pk = lambda d: _pack_rows([d[nm] for nm in small_names])
    outs = _adamw("adamw_small", _pack_rows(small_g)[None], pk(W), pk(Mo), pk(Vo))
    like = [W[nm] for nm in small_names]
    for dst, o in zip((grads, deltas, new_m, new_v), outs):
        for nm, val in zip(small_names, _unpack_rows(o, like)):
            dst[nm] = val

    return (loss, grad_x, *[grads[nm] for nm in names], *[deltas[nm] for nm in names],
            *[new_m[nm] for nm in names], *[new_v[nm] for nm in names])
```

```python
import functools

import jax
import jax.numpy as jnp
from jax import lax
from jax.experimental import pallas as pl
from jax.experimental.pallas import tpu as pltpu

F32, BF16 = jnp.float32, jnp.bfloat16
NN, NT, TN = ((1,), (0,)), ((1,), (1,)), ((0,), (0,))
MESH_AXES = ("x", "y", "c")
N_DEV = 8

D_MODEL = 1024
N_META = 16
EPS = 1e-6
SSD_HEADS, SSD_HEAD_DIM, SSD_GROUPS, SSD_STATE, SSD_CONV, Q = 16, 64, 4, 128, 4, 128
SSD_INNER = SSD_HEADS * SSD_HEAD_DIM
SSD_CONV_CH = SSD_INNER + 2 * SSD_GROUPS * SSD_STATE
HG_WIDTH, HG_HEADS, HG_CHUNK = 1024, 8, 16
PAD = Q - N_META
N_MAIN = 9 * 1024
ADAM_LR, ADAM_B1, ADAM_B2, ADAM_EPS, ADAM_WD, ADAM_STEP = 0.001, 0.9, 0.999, 1e-08, 0.01, 10
VMEM_LIMIT = 52 * 1024 * 1024


def _dot(a, b, dims):
    return lax.dot_general(a, b, (dims, ((), ())), preferred_element_type=F32)


def _dot01(a, b, dims, sel):
    x, ax_x, s, ax_s = (b, dims[1][0], a, dims[0][0]) if sel == "a" else (a, dims[0][0], b, dims[1][0])
    hi = x.astype(BF16)
    r1 = x - hi.astype(F32)
    mid = r1.astype(BF16)
    lo = (r1 - mid.astype(F32)).astype(BF16)
    xs = jnp.concatenate([hi, mid, lo], axis=ax_x)
    ss = jnp.concatenate([s.astype(BF16)] * 3, axis=ax_s)
    return _dot(ss, xs, dims) if sel == "a" else _dot(xs, ss, dims)


def _sigmoid(x):
    return 0.5 * jnp.tanh(0.5 * x) + 0.5


def _dsilu(x, s):
    return s * (1.0 + x * (1.0 - s))


def _softplus(x):
    e = jnp.exp(-jnp.abs(x))
    u = 1.0 + e
    log1p_e = jnp.where(u == 1.0, e, jnp.log(u) * e / (u - 1.0))
    return jnp.maximum(x, 0.0) + log1p_e


def _params(sem):
    return pltpu.CompilerParams(dimension_semantics=sem, vmem_limit_bytes=VMEM_LIMIT)


def _tile(n, prefs):
    for p in prefs:
        if n % p == 0:
            return p
    return n


CHIP_FLIPS = ((1, 0), (0, 1), (1, 1))
N_PEER = N_DEV - 1


def _comm_gather(srcs, outs, send_sems, recv_sems, local_sems):
    n = len(srcs)
    x, y, c = (lax.axis_index(a) for a in MESH_AXES)
    dev = lambda px, py, pc: 4 * px + 2 * py + pc
    me, sib = dev(x, y, c), (x, y, 1 - c)
    nbr_x, nbr_y, diag = (1 - x, y), (x, 1 - y), (1 - x, 1 - y)
    via = (x ^ c, y ^ (1 - c), c)
    sent_on = dev(x ^ (1 - c), y ^ c, c)

    def rc(w, k, slot, to, src=None):
        return pltpu.make_async_remote_copy(
            src_ref=outs[w].at[slot] if src is None else src, dst_ref=outs[w].at[slot],
            send_sem=send_sems.at[w, k], recv_sem=recv_sems.at[w, k], device_id=to, device_id_type=pl.DeviceIdType.MESH)

    def local(w):
        return pltpu.make_async_copy(srcs[w], outs[w].at[me], local_sems.at[w])

    def start():
        for w in range(n):
            local(w).start()
            rc(w, 0, me, sib, src=srcs[w]).start()
            rc(w, 1, me, (*nbr_x, c), src=srcs[w]).start()
            rc(w, 2, me, (*nbr_y, c), src=srcs[w]).start()

    def pass_on():
        for w in range(n):
            rc(w, 1, dev(*nbr_x, c), sib).wait_recv()
            rc(w, 2, dev(*nbr_y, c), sib).wait_recv()
            rc(w, 3, sent_on, via).start()
            rc(w, 4, dev(*nbr_x, c), sib).start()
            rc(w, 5, dev(*nbr_y, c), sib).start()

    def pass_on_diagonal():
        for w in range(n):
            rc(w, 3, dev(*diag, c), sib).wait_recv()
            rc(w, 6, dev(*diag, c), sib).start()

    def finish():
        for w in range(n):
            rc(w, 0, dev(x, y, 1 - c), sib).wait_recv()
            for k, chip in ((4, nbr_x), (5, nbr_y), (6, diag)):
                rc(w, k, dev(*chip, 1 - c), sib).wait_recv()
            for k in range(N_PEER):
                rc(w, k, me, sib, src=srcs[w]).wait_send()
            local(w).wait()

    return start, (pass_on, pass_on_diagonal), finish


def _comm_scatter(srcs, outs, send_sems, recv_sems, local_sems):
    n = len(srcs)
    x, y, c = (lax.axis_index(a) for a in MESH_AXES)
    me = 4 * x + 2 * y + c

    def copies():
        out = []
        for w in range(n):
            out.append(pltpu.make_async_copy(srcs[w].at[me], outs[w].at[me], local_sems.at[w]))
            for k in range(1, N_DEV):
                px, py, pc = x ^ (k >> 2), y ^ ((k >> 1) & 1), c ^ (k & 1)
                out.append(pltpu.make_async_remote_copy(
                    src_ref=srcs[w].at[4 * px + 2 * py + pc], dst_ref=outs[w].at[me],
                    send_sem=send_sems.at[w, k - 1], recv_sem=recv_sems.at[w, k - 1],
                    device_id=(px, py, pc), device_id_type=pl.DeviceIdType.MESH))
        return out

    def start():
        for cp in copies():
            cp.start()

    def finish():
        for cp in copies():
            cp.wait()

    return start, None, finish


def _comm_swap(srcs, outs, send_sems, recv_sems, local_sems):
    x, y, c = (lax.axis_index(a) for a in MESH_AXES)

    def copies():
        return [pltpu.make_async_remote_copy(
            src_ref=srcs[w].at[1 - c], dst_ref=outs[w], send_sem=send_sems.at[w, 0], recv_sem=recv_sems.at[w, 0],
            device_id=(x, y, 1 - c), device_id_type=pl.DeviceIdType.MESH) for w in range(len(srcs))]

    def start():
        for cp in copies():
            cp.start()

    def finish():
        for cp in copies():
            cp.wait()

    return start, None, finish


def _comm_chips(srcs, outs, send_sems, recv_sems, local_sems):
    n = len(srcs)
    x, y, c = (lax.axis_index(a) for a in MESH_AXES)
    mine = 2 * x + y

    def copies():
        out = []
        for w in range(n):
            out.append(pltpu.make_async_copy(srcs[w].at[mine], outs[w].at[mine], local_sems.at[w]))
            for j, (fx, fy) in enumerate(CHIP_FLIPS):
                px, py = x ^ fx, y ^ fy
                out.append(pltpu.make_async_remote_copy(
                    src_ref=srcs[w].at[2 * px + py], dst_ref=outs[w].at[mine],
                    send_sem=send_sems.at[w, j], recv_sem=recv_sems.at[w, j],
                    device_id=(px, py, c), device_id_type=pl.DeviceIdType.MESH))
        return out

    def start():
        for cp in copies():
            cp.start()

    def finish():
        for cp in copies():
            cp.wait()

    return start, None, finish


def _comm_parts(comm):
    kind, arrays = comm[:2]
    n = len(arrays)
    lead = {"gather": lambda a: (N_DEV,) + a.shape, "scatter": lambda a: (N_DEV,) + a.shape[1:],
            "swap": lambda a: a.shape[1:], "chips": lambda a: a.shape}[kind]
    shapes = [jax.ShapeDtypeStruct(lead(a), a.dtype) for a in arrays]
    sems = [pltpu.SemaphoreType.DMA((n, N_PEER)), pltpu.SemaphoreType.DMA((n, N_PEER)), pltpu.SemaphoreType.DMA((n,))]
    make = {"gather": _comm_gather, "scatter": _comm_scatter, "swap": _comm_swap, "chips": _comm_chips}[kind]
    return n, shapes, sems, make


def _exchange(name, kind, arrays):
    n, shapes, sems, make = _comm_parts((kind, arrays))

    def body(*refs):
        start, middle, finish = make(refs[:n], refs[n:2 * n], *refs[2 * n:])
        start()
        for stage in middle or ():
            stage()
        finish()

    any_spec = pl.BlockSpec(memory_space=pl.ANY)
    return pl.pallas_call(
        body, name=name, in_specs=[any_spec] * n, out_specs=[any_spec] * n, out_shape=shapes, scratch_shapes=sems,
        compiler_params=pltpu.CompilerParams(has_side_effects=True),
    )(*arrays)


def _call(body, *, name, grid, in_specs, out_specs, out_shape, scratch, sem, args, comm=None, into=None):
    any_spec = pl.BlockSpec(memory_space=pl.ANY)
    in_specs, args, aliases, n_body_in = list(in_specs), list(args), {}, len(in_specs)
    for arr, k in ([] if into is None else into if isinstance(into, list) else [into]):
        aliases[len(in_specs)] = k
        in_specs.append(any_spec)
        args.append(arr)
    n_in, n_out, n_scr = len(in_specs), len(out_specs), len(scratch)
    if comm is None:
        def plain(*refs):
            body(*refs[:n_body_in], *refs[n_in:])

        return pl.pallas_call(plain, name=name, grid=grid, in_specs=in_specs, out_specs=out_specs, out_shape=out_shape,
                              scratch_shapes=scratch, input_output_aliases=aliases, compiler_params=_params(sem))(*args)
    n, shapes, sems, make = _comm_parts(comm)

    def carrier(*refs):
        ins, csrc = refs[:n_body_in], refs[n_in:n_in + n]
        outs, cout = refs[n_in + n:n_in + n + n_out], refs[n_in + n + n_out:n_in + 2 * n + n_out]
        rest = refs[n_in + 2 * n + n_out:]
        start, middle, finish = make(csrc, cout, *rest[n_scr:])
        ids = [pl.program_id(a) for a in range(len(grid))]
        step = functools.reduce(lambda acc, ig: acc * ig[1] + ig[0], zip(ids, grid), 0)
        n_steps = functools.reduce(lambda a, b: a * b, grid, 1)
        pl.when(step == 0)(start)
        body(*ins, *outs, *rest[:n_scr])
        if middle:
            pl.when(step == max(0, (3 * n_steps) // 4 - 1))(middle[0])
            pl.when(step == n_steps - 1)(middle[1])
        pl.when(step == n_steps - 1)(finish)

    return pl.pallas_call(
        carrier, name=name, grid=grid, in_specs=in_specs + [any_spec] * n,
        out_specs=list(out_specs) + [any_spec] * n, out_shape=list(out_shape) + shapes,
        scratch_shapes=list(scratch) + sems, input_output_aliases=aliases,
        compiler_params=pltpu.CompilerParams(dimension_semantics=("arbitrary",) * len(grid),
                                             vmem_limit_bytes=VMEM_LIMIT, has_side_effects=True),
    )(*args, *comm[1])


def _fused_matmul(name, M, N, K, pairs, extras, epilogue, out_dtypes, n_acc, tm, tn, tk, outer="i", comm=None,
                  stack=False, vecs=(), row_sums=0, wide=None, sub=None):
    nk = K // tk
    n_pairs, n_ex, n_out = len(pairs), len(extras), len(out_dtypes)
    assert not row_sums or (outer == "i" and N == tn)

    def ij(g0, g1):
        return (g0, g1) if outer == "i" else (g1, g0)

    in_specs, args = [], []
    for p in pairs:
        ao, bk, bn = p.get("a_off", 0), p.get("bk_off", 0), p.get("bn_off", 0)
        mode = dict(pipeline_mode=pl.Buffered(1)) if p.get("resident") else {}
        in_specs.append(pl.BlockSpec((tm, tk), lambda g0, g1, k, ao=ao: (ij(g0, g1)[0], k + ao)))
        if "b_shift" in p:
            first, shift = p["b_shift"]
            if p.get("trans_b"):
                in_specs.append(pl.BlockSpec(
                    (pl.Element(tn), pl.Element(tk)),
                    lambda g0, g1, k, bk=bk: (
                        pl.multiple_of(ij(g0, g1)[1] * tn + jnp.where(ij(g0, g1)[1] >= first, shift, 0), 16),
                        (k + bk) * tk)))
            else:
                in_specs.append(pl.BlockSpec(
                    (pl.Element(tk), pl.Element(tn)),
                    lambda g0, g1, k, bn=bn: (pl.multiple_of(k * tk + jnp.where(k >= first, shift, 0), 16),
                                              (ij(g0, g1)[1] + bn) * tn)))
        elif p.get("trans_b"):
            in_specs.append(pl.BlockSpec((tn, tk), lambda g0, g1, k, bk=bk, bn=bn: (ij(g0, g1)[1] + bn, k + bk), **mode))
        else:
            in_specs.append(pl.BlockSpec((tk, tn), lambda g0, g1, k, bk=bk, bn=bn: (k + bk, ij(g0, g1)[1] + bn), **mode))
        args += [p["a"], p["b"]]
    for arr, off in extras:
        in_specs.append(pl.BlockSpec((tm, tn), lambda g0, g1, k, off=off: (ij(g0, g1)[0], ij(g0, g1)[1] + off)))
        args.append(arr)
    for arr in vecs:
        in_specs.append(pl.BlockSpec((1, tn), lambda g0, g1, k: (0, ij(g0, g1)[1])))
        args.append(arr)
    if stack:
        assert N == tn
        out_specs = [pl.BlockSpec((tm, n_out * tn), lambda g0, g1, k: (ij(g0, g1)[0], 0))]
        out_shape = [jax.ShapeDtypeStruct((M, n_out * N), out_dtypes[0])]
    else:
        out_specs = [pl.BlockSpec((tm, tn), lambda g0, g1, k: ij(g0, g1)) for _ in out_dtypes]
        out_shape = [jax.ShapeDtypeStruct((M, N), dt) for dt in out_dtypes]
    if wide:
        out_specs.append(pl.BlockSpec((pl.Element(tm), pl.Element(wide["width"])),
                                      lambda g0, g1, k: (pl.multiple_of(ij(g0, g1)[0] * tm, 16), wide["col"])))
        out_shape.append(jax.ShapeDtypeStruct((M, wide["total"]), wide["dtype"]))
    n_tile_out = len(out_specs)
    out_specs += [pl.BlockSpec((1, tn), lambda g0, g1, k: (0, 0)) for _ in range(row_sums)]
    out_shape += [jax.ShapeDtypeStruct((1, N), F32) for _ in range(row_sums)]
    grid = (M // tm, N // tn, nk) if outer == "i" else (N // tn, M // tm, nk)
    n_in = 2 * n_pairs + n_ex + len(vecs)

    def partials(refs, cs=slice(None)):
        accs = [None] * n_acc
        for idx, p in enumerate(pairs):
            b_ref = refs[2 * idx + 1]
            d = (_dot(refs[2 * idx][...], b_ref[cs, :], NT) if p.get("trans_b")
                 else _dot(refs[2 * idx][...], b_ref[:, cs], NN))
            accs[p["acc"]] = d if accs[p["acc"]] is None else accs[p["acc"]] + d
        return accs

    def finish(accs, refs, first_rows, cs=slice(None)):
        res = epilogue(accs, [r[:, cs] for r in refs[2 * n_pairs:n_in]])
        if stack:
            o = refs[n_in]
            for idx in range(n_out):
                lo = idx * tn + (cs.start or 0)
                o[:, lo:lo + (tn if cs.stop is None else cs.stop - cs.start)] = res[idx].astype(o.dtype)
        else:
            for o, r in zip(refs[n_in:n_in + n_out], res):
                o[:, cs] = r.astype(o.dtype)
        if wide:
            o = refs[n_in + n_tile_out - 1]
            o[...] = res[n_out].astype(o.dtype)
        for o, r in zip(refs[n_in + n_tile_out:n_in + n_tile_out + row_sums], res[n_out + bool(wide):]):
            @pl.when(first_rows)
            def _(o=o, r=r):
                o[...] = r

            @pl.when(jnp.logical_not(first_rows))
            def _(o=o, r=r):
                o[...] += r

    if nk == 1 and sub:
        assert not wide and not row_sums and tn % sub == 0

        def body(*refs):
            for c in range(tn // sub):
                cs = slice(c * sub, (c + 1) * sub)
                finish(partials(refs, cs), refs, None, cs)
        scratch = []
    elif nk == 1:
        def body(*refs):
            finish(partials(refs), refs, pl.program_id(0) == 0)
        scratch = []
    else:
        def body(*refs):
            acc_refs = refs[-n_acc:]
            k = pl.program_id(2)
            first_rows = pl.program_id(0) == 0
            new = partials(refs)

            @pl.when(k == 0)
            def _():
                for a, v in zip(acc_refs, new):
                    a[...] = v

            @pl.when(k > 0)
            def _():
                for a, v in zip(acc_refs, new):
                    a[...] += v

            @pl.when(k == nk - 1)
            def _():
                finish([a[...] for a in acc_refs], refs, first_rows)
        scratch = [pltpu.VMEM((tm, tn), F32) for _ in range(n_acc)]

    return _call(body, name=name, grid=grid, in_specs=in_specs, out_specs=out_specs, out_shape=out_shape,
                 scratch=scratch, sem=("parallel", "parallel", "arbitrary"), args=args, comm=comm)


def _matmul_tn(name, x, y, t1, t2, tr, scale=1.0, comm=None, out_dtype=BF16, out_skip=None):
    R, K1 = x.shape
    N1 = y.shape[1]
    nr, n1 = R // tr, K1 // t1
    x_spec = pl.BlockSpec((tr, t1), lambda i, j, r: (r, i))
    rows_out = K1
    o_spec = pl.BlockSpec((t1, t2), lambda i, j, r: (i, j))
    if out_skip:
        row, count = out_skip
        rows_out += count
        o_spec = pl.BlockSpec(
            (pl.Element(t1), pl.Element(t2)),
            lambda i, j, r: (pl.multiple_of(i * t1 + jnp.where(i * t1 >= row, count, 0), 16), j * t2))

    def body(x_ref, y_ref, o_ref, *acc):
        d = _dot(x_ref[...], y_ref[...], TN)
        if nr == 1:
            o_ref[...] = (d * scale).astype(o_ref.dtype)
            return
        r = pl.program_id(2)

        @pl.when(r == 0)
        def _():
            acc[0][...] = d

        @pl.when(jnp.logical_and(r > 0, r < nr - 1))
        def _():
            acc[0][...] += d

        @pl.when(r == nr - 1)
        def _():
            o_ref[...] = ((acc[0][...] + d) * scale).astype(o_ref.dtype)

    return _call(
        body, name=name, grid=(n1, N1 // t2, nr),
        in_specs=[x_spec, pl.BlockSpec((tr, t2), lambda i, j, r: (r, j))], out_specs=[o_spec],
        out_shape=[jax.ShapeDtypeStruct((rows_out, N1), out_dtype)],
        scratch=[pltpu.VMEM((t1, t2), F32)] if nr > 1 else [],
        sem=("parallel", "parallel", "arbitrary"), args=(x, y), comm=comm)


def _embed_norm(x, w, comm=None):
    Bl, S, D = x.shape
    nb = (PAD + N_META + S) // Q
    M = Bl * nb * Q

    def body(x_ref, w_ref, h_ref, n_ref):
        h = x_ref[0]
        h_ref[...] = h
        n_ref[...] = _rmsnorm_tile(h, w_ref[...]).astype(n_ref.dtype)

    row = pl.BlockSpec((Q, D), lambda b, t: (b * nb + t + 1, 0))
    return _call(
        body, name="embed_norm", grid=(Bl, nb - 1),
        in_specs=[pl.BlockSpec((1, Q, D), lambda b, t: (b, t, 0)), pl.BlockSpec((1, D), lambda b, t: (0, 0))],
        out_specs=[row, row], out_shape=[jax.ShapeDtypeStruct((M, D), F32), jax.ShapeDtypeStruct((M, D), BF16)],
        scratch=[], sem=("parallel", "parallel"), args=(x, w), comm=comm)


def _embed_meta(meta, w, h0, n0, Bl):
    nb = h0.shape[0] // (Bl * Q)
    D = h0.shape[1]

    def body(meta_ref, w_ref, h_ref, n_ref):
        h = jnp.concatenate([jnp.zeros((PAD, D), F32), meta_ref[...]], axis=0)
        h_ref[...] = h
        n_ref[...] = _rmsnorm_tile(h, w_ref[...]).astype(n_ref.dtype)

    row = pl.BlockSpec((Q, D), lambda b: (b * nb, 0))
    return _call(
        body, name="embed_meta", grid=(Bl,),
        in_specs=[pl.BlockSpec((N_META, D), lambda b: (0, 0)), pl.BlockSpec((1, D), lambda b: (0, 0))],
        out_specs=[row, row], out_shape=[jax.ShapeDtypeStruct(h0.shape, F32), jax.ShapeDtypeStruct(n0.shape, BF16)],
        scratch=[], sem=("parallel",), args=(meta, w), into=[(h0, 0), (n0, 1)])


def _rmsnorm_bwd_tile(dn, h, w, dh_in):
    r = lax.rsqrt(jnp.mean(h * h, axis=-1, keepdims=True) + EPS)
    xhat = h * r
    gw = dn * w
    dh = dh_in + r * (gw - xhat * jnp.mean(gw * xhat, axis=-1, keepdims=True))
    return dh, jnp.sum(dn * xhat, axis=0, keepdims=True)


def _loss_head(h, w, target, Bl, nb):
    M, D = h.shape
    nt = 4 if (nb * Q) % 32 == 0 and nb * Q // 4 >= Q else nb
    half = nb * Q // nt

    def body(h_ref, w_ref, t_ref, dh_ref, dhb_ref, dw_ref, loss_ref):
        b, t = pl.program_id(0), pl.program_id(1)
        row = lax.broadcasted_iota(jnp.int32, (half, 1), 0)
        live = jnp.logical_or(t > 0, row >= Q).astype(F32)
        x = h_ref[...]
        r = lax.rsqrt(jnp.mean(x * x, axis=-1, keepdims=True) + EPS)
        xhat = x * r
        wv = w_ref[...]
        tgt = t_ref[0]
        tgt = jnp.where(t == 0, pltpu.roll(tgt, Q, 0), tgt)
        err = (xhat * wv - tgt) * live
        dy = err * (1.0 / D)
        gw = dy * wv
        dx = r * (gw - xhat * jnp.mean(gw * xhat, axis=-1, keepdims=True))
        dh_ref[...] = dx
        dhb_ref[...] = dx.astype(BF16)
        dw = jnp.sum(dy * xhat, axis=0, keepdims=True)
        part = 0.5 * jnp.sum(jnp.sum(err * err, axis=-1, keepdims=True) * (1.0 / D), axis=0, keepdims=True)
        first = jnp.logical_and(b == 0, t == 0)

        @pl.when(first)
        def _():
            dw_ref[...] = dw
            loss_ref[...] = jnp.broadcast_to(part, loss_ref.shape)

        @pl.when(jnp.logical_not(first))
        def _():
            dw_ref[...] += dw
            loss_ref[...] += jnp.broadcast_to(part, loss_ref.shape)

    row = pl.BlockSpec((half, D), lambda b, t: (b * nt + t, 0))
    vec = pl.BlockSpec((1, D), lambda b, t: (0, 0))
    return pl.pallas_call(
        body, name="loss_head", grid=(Bl, nt),
        in_specs=[row, vec, pl.BlockSpec((pl.Element(1), pl.Element(half), pl.Element(D)),
                                         lambda b, t: (b, pl.multiple_of(jnp.maximum(t * half - Q, 0), 8), 0))],
        out_specs=[row, row, vec, pl.BlockSpec((8, 128), lambda b, t: (0, 0))],
        out_shape=[jax.ShapeDtypeStruct((M, D), F32), jax.ShapeDtypeStruct((M, D), BF16),
                   jax.ShapeDtypeStruct((1, D), F32), jax.ShapeDtypeStruct((8, 128), F32)],
        compiler_params=_params(("arbitrary", "arbitrary")),
    )(h, w, target)


CONV_TC = 256


def _conv_pre(xr_ref, w_ref, b_ref):
    xb = xr_ref[...]
    x = xb.astype(F32)
    n_tap = SSD_CONV - 1
    r = lax.broadcasted_iota(jnp.int32, (Q * n_tap, 2 * Q), 0)
    c = lax.broadcasted_iota(jnp.int32, (Q * n_tap, 2 * Q), 1)
    move = (c == Q + r % Q - (r // Q + 1)).astype(BF16)
    taps = [[] for _ in range(n_tap)]
    for i in range(xb.shape[0] // Q):
        out = _dot(move[:, Q:], xb[0:Q], NN) if i == 0 else _dot(move, xb[Q * (i - 1):Q * (i + 1)], NN)
        for k in range(n_tap):
            taps[k].append(out[Q * k:Q * (k + 1)])
    moved = [jnp.concatenate(t, axis=0) for t in taps]
    acc = b_ref[...] + w_ref[SSD_CONV - 1:SSD_CONV, :] * x
    for k in range(1, SSD_CONV):
        acc = acc + w_ref[SSD_CONV - 1 - k:SSD_CONV - k, :] * moved[k - 1]
    return x, acc, moved


def _conv_fwd(proj, w, b, Bl, T):
    M = proj.shape[0]
    off = 1024 // CONV_TC

    def body(xr_ref, w_ref, b_ref, o_ref):
        _, acc, _ = _conv_pre(xr_ref, w_ref, b_ref)
        row = lax.broadcasted_iota(jnp.int32, acc.shape, 0)
        o_ref[...] = jnp.where(row >= PAD, acc * _sigmoid(acc), 0.0).astype(o_ref.dtype)

    return pl.pallas_call(
        body, name="conv_fwd", grid=(Bl, SSD_CONV_CH // CONV_TC),
        in_specs=[pl.BlockSpec((T, CONV_TC), lambda bb, j: (bb, j + off)),
                  pl.BlockSpec((SSD_CONV, CONV_TC), lambda bb, j: (0, j)), pl.BlockSpec((1, CONV_TC), lambda bb, j: (0, j))],
        out_specs=pl.BlockSpec((T, CONV_TC), lambda bb, j: (bb, j)),
        out_shape=jax.ShapeDtypeStruct((M, SSD_CONV_CH), BF16), compiler_params=_params(("parallel", "parallel")),
    )(proj, w, b)


def _conv_bwd(proj, w, b, dxc, dproj, Bl, T):
    M = proj.shape[0]
    off = 1024 // CONV_TC

    def body(xr_ref, w_ref, b_ref, d_ref, dx_ref, dw_ref, db_ref):
        x, acc, moved = _conv_pre(xr_ref, w_ref, b_ref)
        row = lax.broadcasted_iota(jnp.int32, acc.shape, 0)
        s = _sigmoid(acc)
        dpre = jnp.where(row >= PAD, d_ref[...].astype(F32) * _dsilu(acc, s), 0.0)
        dx = w_ref[SSD_CONV - 1:SSD_CONV, :] * dpre
        dws = [jnp.sum(dpre * x, axis=0, keepdims=True)]
        for k in range(1, SSD_CONV):
            dx = dx + w_ref[SSD_CONV - 1 - k:SSD_CONV - k, :] * pltpu.roll(dpre, T - k, 0)
            dws.append(jnp.sum(dpre * moved[k - 1], axis=0, keepdims=True))
        dx_ref[...] = dx.astype(dx_ref.dtype)
        dw = jnp.concatenate(dws[::-1], axis=0)
        db = jnp.sum(dpre, axis=0, keepdims=True)

        @pl.when(pl.program_id(1) == 0)
        def _():
            dw_ref[...] = dw
            db_ref[...] = db

        @pl.when(pl.program_id(1) > 0)
        def _():
            dw_ref[...] += dw
            db_ref[...] += db

    return _call(
        body, name="conv_bwd", grid=(SSD_CONV_CH // CONV_TC, Bl),
        in_specs=[pl.BlockSpec((T, CONV_TC), lambda j, bb: (bb, j + off)),
                  pl.BlockSpec((SSD_CONV, CONV_TC), lambda j, bb: (0, j)), pl.BlockSpec((1, CONV_TC), lambda j, bb: (0, j)),
                  pl.BlockSpec((T, CONV_TC), lambda j, bb: (bb, j))],
        out_specs=[pl.BlockSpec((T, CONV_TC), lambda j, bb: (bb, j + off)),
                   pl.BlockSpec((SSD_CONV, CONV_TC), lambda j, bb: (0, j)), pl.BlockSpec((1, CONV_TC), lambda j, bb: (0, j))],
        out_shape=[jax.ShapeDtypeStruct(dproj.shape, BF16), jax.ShapeDtypeStruct((SSD_CONV, SSD_CONV_CH), F32),
                   jax.ShapeDtypeStruct((1, SSD_CONV_CH), F32)],
        scratch=[], sem=("parallel", "arbitrary"), args=(proj, w, b, dxc), into=(dproj, 0))


N_PAIR = SSD_HEADS // 2
HPG = SSD_HEADS // SSD_GROUPS
GW = SSD_INNER // SSD_GROUPS


def _per_group(fn, *arrs):
    return jnp.concatenate([jnp.broadcast_to(fn(*(a[:, GW * g:GW * (g + 1)] for a in arrs)), (arrs[0].shape[0], GW))
                            for g in range(SSD_GROUPS)], axis=1)


def _ssd_prep(c, dtr_ref, bias_ref, alog_ref, d_ref):
    row = lax.broadcasted_iota(jnp.int32, (Q, 128), 0)
    col = lax.broadcasted_iota(jnp.int32, (Q, 128), 1)
    live = col < SSD_HEADS
    valid = jnp.logical_and(jnp.logical_or(c > 0, row >= PAD), live)
    pre = dtr_ref[...] + bias_ref[...]
    dt = jnp.where(valid, _softplus(pre), 0.0)
    A = jnp.where(live[0:1], -jnp.exp(alog_ref[...]), 0.0)
    tri = row >= col
    eye = (row == col).astype(BF16)
    cs = _dot01(tri, dt * A, NN, "a")
    cst = _dot01(eye, cs, NT, "a")
    spread = (lax.broadcasted_iota(jnp.int32, (128, SSD_INNER), 0)
              == lax.broadcasted_iota(jnp.int32, (128, SSD_INNER), 1) // SSD_HEAD_DIM).astype(BF16)
    dt_w = _dot01(dt, spread, NN, "b")
    cs_w = _dot01(cs, spread, NN, "b")
    d_w = _dot01(jnp.broadcast_to(d_ref[...], (8, 128)), spread, NN, "b")[0:1]
    lane = lax.broadcasted_iota(jnp.int32, (Q, SSD_INNER), 1)
    first = (lane % 128) < SSD_HEAD_DIM
    return dict(row=row, col=col, valid=valid, pre=pre, dt=dt, A=A, tri=tri, eye=eye, cs=cs, cst=cst, spread=spread,
                dt_w=dt_w, cs_w=cs_w, d_w=d_w, ecs_w=jnp.exp(cs_w), decay_w=jnp.exp(cs_w[Q - 1:Q] - cs_w), first=first)


def _ssd_chunk(xc_ref, s, states):
    xv = xc_ref[:, 0:SSD_INNER].astype(F32)
    Bs = [xc_ref[:, SSD_INNER + 128 * g:SSD_INNER + 128 * (g + 1)] for g in range(SSD_GROUPS)]
    Cs = [xc_ref[:, SSD_INNER + 512 + 128 * g:SSD_INNER + 512 + 128 * (g + 1)] for g in range(SSD_GROUPS)]
    X = xv * s["dt_w"]
    X0 = jnp.where(s["first"], X, 0.0)
    Xb = (X0.astype(BF16), (X - X0).astype(BF16))
    Xd = (X * s["decay_w"]).astype(BF16)
    CB = [_dot(Cs[g], Bs[g], NT) for g in range(SSD_GROUPS)]
    Lms = [jnp.exp(jnp.where(s["tri"], s["cs"][:, h:h + 1] - s["cst"][h:h + 1, :], -jnp.inf)) for h in range(SSD_HEADS)]
    Ms = [CB[h // HPG] * Lms[h] for h in range(SSD_HEADS)]
    Mb = [m.astype(BF16) for m in Ms]
    prev_b = [st.astype(BF16) for st in states]
    yds, yos, sts = [], [], []
    for p in range(N_PAIR):
        g, ln = p // 2, slice(128 * p, 128 * (p + 1))
        yds.append(_dot(Mb[2 * p], Xb[0][:, ln], NN) + _dot(Mb[2 * p + 1], Xb[1][:, ln], NN))
        yos.append(_dot(Cs[g], prev_b[p], NT))
        sts.append(_dot(Xd[:, ln], Bs[g], TN))
    yo = jnp.concatenate(yos, axis=1)
    y = jnp.concatenate(yds, axis=1) + yo * s["ecs_w"] + xv * s["d_w"]
    upper = s["row"] < SSD_HEAD_DIM
    cl = s["cs"][Q - 1:Q, :]
    ecl_rows = [jnp.where(upper, jnp.exp(cl[:, 2 * p:2 * p + 1]), jnp.exp(cl[:, 2 * p + 1:2 * p + 2])) for p in range(N_PAIR)]
    new_states = [states[p] * ecl_rows[p] + sts[p] for p in range(N_PAIR)]
    return y, new_states, dict(xv=xv, Bs=Bs, Cs=Cs, X=X, Xb=Xb, CB=CB, Lms=Lms, Ms=Ms, Mb=Mb, prev_b=prev_b, yo=yo,
                               ecl_rows=ecl_rows)


def _ssd_in_specs(nc, rev=False):
    rb = (lambda b, c: b * nc + nc - 1 - c) if rev else (lambda b, c: b * nc + c)
    vec = pl.BlockSpec((1, 128), lambda b, c: (0, 0))
    return [pl.BlockSpec((Q, SSD_CONV_CH), lambda b, c: (rb(b, c), 0)),
            pl.BlockSpec((Q, 128), lambda b, c: (rb(b, c), 0)),
            pl.BlockSpec((Q, SSD_INNER), lambda b, c: (rb(b, c), 0)),
            vec, vec, vec, pl.BlockSpec((1, SSD_INNER), lambda b, c: (0, 0))]


def _ssd_fwd(xc, dtr, proj, bias_p, alog_p, d_p, nw, Bl, nc):
    M = xc.shape[0]

    def body(xc_ref, dtr_ref, z_ref, bias_ref, alog_ref, d_ref, nw_ref, y_ref, prev_ref, state):
        c = pl.program_id(1)

        @pl.when(c == 0)
        def _():
            state[...] = jnp.zeros_like(state)

        s = _ssd_prep(c, dtr_ref, bias_ref, alog_ref, d_ref)
        states = [state[p] for p in range(N_PAIR)]
        y, new_states, _ = _ssd_chunk(xc_ref, s, states)
        for p in range(N_PAIR):
            prev_ref[0, 0, p] = states[p]
            state[p] = new_states[p]
        zz = z_ref[...].astype(F32)
        yg = y * zz * _sigmoid(zz)
        r = _per_group(lambda a: lax.rsqrt(jnp.mean(a * a, axis=-1, keepdims=True) + EPS), yg)
        y_ref[...] = (yg * r * nw_ref[...]).astype(y_ref.dtype)

    return pl.pallas_call(
        body, name="ssd_fwd", grid=(Bl, nc), in_specs=_ssd_in_specs(nc),
        out_specs=[pl.BlockSpec((Q, SSD_INNER), lambda b, c: (b * nc + c, 0)),
                   pl.BlockSpec((1, 1, N_PAIR, 128, 128), lambda b, c: (b, c, 0, 0, 0))],
        out_shape=[jax.ShapeDtypeStruct((M, SSD_INNER), BF16), jax.ShapeDtypeStruct((Bl, nc, N_PAIR, 128, 128), F32)],
        scratch_shapes=[pltpu.VMEM((N_PAIR, 128, 128), F32)],
        compiler_params=_params(("arbitrary", "arbitrary")),
    )(xc, dtr, proj, bias_p, alog_p, d_p, nw)


def _ssd_bwd(xc, dtr, proj, bias_p, alog_p, d_p, nw, prev, dya, dproj, Bl, nc, comm=None):
    M = xc.shape[0]

    def body(xc_ref, dtr_ref, z_ref, bias_ref, alog_ref, d_ref, nw_ref, prev_ref, dy_ref,
             dxc_ref, dz_ref, ddtr_ref, dbias_ref, dalog_ref, dd_ref, dnw_ref, dS):
        b, t = pl.program_id(0), pl.program_id(1)

        @pl.when(t == 0)
        def _():
            dS[...] = jnp.zeros_like(dS)

        s = _ssd_prep(nc - 1 - t, dtr_ref, bias_ref, alog_ref, d_ref)
        states = [prev_ref[0, 0, p] for p in range(N_PAIR)]
        y, _, k = _ssd_chunk(xc_ref, s, states)
        xv, Bs, Cs, Xb = k["xv"], k["Bs"], k["Cs"], k["Xb"]

        zz = z_ref[...].astype(F32)
        sz = _sigmoid(zz)
        silu_z = zz * sz
        yg = y * silu_z
        r = _per_group(lambda a: lax.rsqrt(jnp.mean(a * a, axis=-1, keepdims=True) + EPS), yg)
        xhat = yg * r
        dout = dy_ref[...].astype(F32)
        gw = dout * nw_ref[...]
        dyg = r * (gw - xhat * _per_group(lambda a, c2: jnp.mean(a * c2, axis=-1, keepdims=True), gw, xhat))
        dnw = jnp.sum(dout * xhat, axis=0, keepdims=True)
        dz_ref[...] = (dyg * y * _dsilu(zz, sz)).astype(dz_ref.dtype)
        dy = dyg * silu_z
        dy0 = jnp.where(s["first"], dy, 0.0)
        dyb = (dy0.astype(BF16), (dy - dy0).astype(BF16))
        dYo = (dy * s["ecs_w"]).astype(BF16)

        dS_f = [dS[p] for p in range(N_PAIR)]
        dS_b = [d.astype(BF16) for d in dS_f]
        BdS, dXm, dprev, dCs, dMs, XdS = [], [], [], [[] for _ in range(SSD_GROUPS)], [], []
        for p in range(N_PAIR):
            g, ln = p // 2, slice(128 * p, 128 * (p + 1))
            BdS.append(_dot(Bs[g], dS_b[p], NT))
            dXm.append(_dot(k["Mb"][2 * p], dyb[0][:, ln], TN) + _dot(k["Mb"][2 * p + 1], dyb[1][:, ln], TN))
            dprev.append(_dot(dYo[:, ln], Cs[g], TN))
            dCs[g].append(_dot(dYo[:, ln], k["prev_b"][p], NN))
            for hh in range(2):
                dMs.append(_dot(dyb[hh][:, ln], Xb[hh][:, ln], NT))
                XdS.append(_dot(Xb[hh][:, ln], dS_b[p], NN))
        dX = jnp.concatenate(dXm, axis=1) + s["decay_w"] * jnp.concatenate(BdS, axis=1)
        dxs = dy * s["d_w"] + dX * s["dt_w"]

        sums = _dot01(jnp.concatenate([dX * xv, dy * k["yo"] * s["ecs_w"], dy * xv], axis=0), s["spread"], NT, "b")
        ddt, dcs = sums[0:Q], sums[Q:2 * Q]
        dD = jnp.sum(sums[2 * Q:3 * Q], axis=0, keepdims=True)

        col, row = s["col"], s["row"]
        lane1 = col[0:1]
        rowsT = lax.broadcasted_iota(jnp.int32, (128, Q), 0)
        dcs_t = jnp.zeros((128, Q), F32)
        dcl = jnp.zeros((1, 128), F32)
        dB_out, dC_out = [], []
        for g in range(SSD_GROUPS):
            Bf = Bs[g].astype(F32)
            dCB = jnp.zeros((Q, Q), F32)
            dBacc = jnp.zeros((Q, 128), F32)
            for r4 in range(HPG):
                h = HPG * g + r4
                p, hh = h // 2, h % 2
                W = dMs[h] * k["Ms"][h]
                dCB = dCB + dMs[h] * k["Lms"][h]
                decay_h = s["decay_w"][:, SSD_HEAD_DIM * h:SSD_HEAD_DIM * h + 1]
                dBacc = dBacc + decay_h * XdS[h]
                tdec = jnp.sum(XdS[h] * Bf, axis=1, keepdims=True) * decay_h
                dcs = dcs + jnp.where(col == h, jnp.sum(W, axis=1, keepdims=True) - tdec, 0.0)
                dcs_t = dcs_t - jnp.where(rowsT == h, jnp.sum(W, axis=0, keepdims=True), 0.0)
                rows_h = (row < SSD_HEAD_DIM) if hh == 0 else (row >= SSD_HEAD_DIM)
                sprev = jnp.sum(jnp.sum(jnp.where(rows_h, dS_f[p] * states[p], 0.0), axis=1, keepdims=True),
                                axis=0, keepdims=True)
                ecl = jnp.exp(s["cs"][Q - 1:Q, h:h + 1])
                dcl = dcl + jnp.where(lane1 == h, jnp.sum(tdec, axis=0, keepdims=True) + ecl * sprev, 0.0)
            dCB_b = dCB.astype(BF16)
            dC_out.append(dCs[g][0] + dCs[g][1] + _dot(dCB_b, Bs[g], NN))
            dB_out.append(dBacc + _dot(dCB_b, Cs[g], TN))
        for p in range(N_PAIR):
            dS[p] = dS_f[p] * k["ecl_rows"][p] + dprev[p]
        dxc_ref[...] = jnp.concatenate([dxs] + dB_out + dC_out, axis=1).astype(dxc_ref.dtype)

        dcs = dcs + _dot01(s["eye"], dcs_t, NT, "a") + jnp.where(row == Q - 1, dcl, 0.0)
        da = _dot01(row <= col, dcs, NN, "a")
        ddt = ddt + da * s["A"]
        dpre = jnp.where(s["valid"], ddt * _sigmoid(s["pre"]), 0.0)
        ddtr_ref[...] = dpre
        dbias = jnp.sum(dpre, axis=0, keepdims=True)
        dalog = jnp.sum(da * s["dt"], axis=0, keepdims=True) * s["A"]
        first_step = jnp.logical_and(b == 0, t == 0)

        @pl.when(first_step)
        def _():
            dbias_ref[...] = dbias
            dalog_ref[...] = dalog
            dd_ref[...] = dD
            dnw_ref[...] = dnw

        @pl.when(jnp.logical_not(first_step))
        def _():
            dbias_ref[...] += dbias
            dalog_ref[...] += dalog
            dd_ref[...] += dD
            dnw_ref[...] += dnw

    rb = lambda b, c: b * nc + nc - 1 - c
    rowblk = lambda w: pl.BlockSpec((Q, w), lambda b, c: (rb(b, c), 0))
    vec = lambda w: pl.BlockSpec((1, w), lambda b, c: (0, 0))
    return _call(
        body, name="ssd_bwd", grid=(Bl, nc),
        in_specs=_ssd_in_specs(nc, rev=True) + [
            pl.BlockSpec((1, 1, N_PAIR, 128, 128), lambda b, c: (b, nc - 1 - c, 0, 0, 0)), rowblk(SSD_INNER)],
        out_specs=[rowblk(SSD_CONV_CH), rowblk(SSD_INNER), rowblk(128), vec(128), vec(128), vec(128), vec(SSD_INNER)],
        out_shape=[jax.ShapeDtypeStruct((M, SSD_CONV_CH), BF16), jax.ShapeDtypeStruct(dproj.shape, BF16),
                   jax.ShapeDtypeStruct((M, 128), F32), jax.ShapeDtypeStruct((1, 128), F32),
                   jax.ShapeDtypeStruct((1, 128), F32), jax.ShapeDtypeStruct((1, 128), F32),
                   jax.ShapeDtypeStruct((1, SSD_INNER), F32)],
        scratch=[pltpu.VMEM((N_PAIR, 128, 128), F32)], sem=("arbitrary", "arbitrary"),
        args=(xc, dtr, proj, bias_p, alog_p, d_p, nw, prev, dya), comm=comm, into=(dproj, 1))


NSUB = Q // HG_CHUNK
HG_HP = 8
EXP_CAP = 80.0


def _hg_setup(blk, q_ref, f_ref, hb_ref):
    row = lax.broadcasted_iota(jnp.int32, (Q, Q), 0)
    col = lax.broadcasted_iota(jnp.int32, (Q, Q), 1)
    same = (row // HG_CHUNK) == (col // HG_CHUNK)
    causal = jnp.logical_and(same, col <= row)
    lb = _sigmoid(hb_ref[0:1, :] - hb_ref[1:2, :])
    fl = f_ref[...].astype(F32)
    sg = _sigmoid(fl)
    fg = lb + (1.0 - lb) * sg
    k = (1.0 - lb) * (1.0 - sg)
    gl = jnp.log(fg)
    G = _dot01(causal, gl, NN, "a")
    T = _dot01(same, gl, NN, "a")
    qv = q_ref[...].astype(F32)
    sq = _sigmoid(qv)
    eG = jnp.exp(G)
    eGn = jnp.exp(jnp.minimum(-G, EXP_CAP))
    eTG = jnp.exp(T - G)
    qt = qv * sq * eG
    kt = k * eGn
    kh = k * eTG
    valid = jnp.logical_or(blk > 0, row[:, :1] >= PAD)
    return dict(row=row, col=col, same=same, causal=causal, lb=lb, sg=sg, fg=fg, k=k, T=T, qv=qv, sq=sq,
                eG=eG, eGn=eGn, eTG=eTG, qt=qt, kt=kt, kh=kh, valid=valid)


def _hg_specs(nb, rev=False):
    rb = (lambda h, b, t: b * nb + nb - 1 - t) if rev else (lambda h, b, t: b * nb + t)
    w = 128 * HG_HP
    blk = lambda off: pl.BlockSpec((Q, w), lambda h, b, t, off=off: (rb(h, b, t), off // HG_HP + h))
    return [blk(24), blk(32), blk(40), blk(48),
            pl.BlockSpec((2, w), lambda h, b, t: (0, h)), pl.BlockSpec((1, w), lambda h, b, t: (0, h))]


HEAD_LANES = tuple(slice(128 * hh, 128 * (hh + 1)) for hh in range(HG_HP))


def _per_head(fn, *arrs):
    return jnp.concatenate([jnp.broadcast_to(fn(*(a[:, ln] for a in arrs)), (arrs[0].shape[0], 128))
                            for ln in HEAD_LANES], axis=1)


def _hgrn_fwd(proj, hb, nw, Bl, nb, comm=None):
    M = proj.shape[0]

    def body(q_ref, f_ref, i_ref, g_ref, hb_ref, nw_ref, y_ref, o_ref, st_ref, S):
        blk = pl.program_id(2)

        @pl.when(blk == 0)
        def _():
            S[...] = jnp.zeros_like(S)

        s = _hg_setup(blk, q_ref, f_ref, hb_ref)
        v = i_ref[...]
        qt_b, kt_b, kh_b = s["qt"].astype(BF16), s["kt"].astype(BF16), s["kh"].astype(BF16)
        eT = jnp.exp(s["T"])
        att = [jnp.where(s["causal"], _dot(qt_b[:, ln], kt_b[:, ln], NT), 0.0).astype(BF16) for ln in HEAD_LANES]
        o_intra = [_dot(att[hh], v[:, ln], NN) for hh, ln in enumerate(HEAD_LANES)]
        for j in range(NSUB):
            sl = slice(HG_CHUNK * j, HG_CHUNK * (j + 1))
            for hh, ln in enumerate(HEAD_LANES):
                St = S[hh]
                st_ref[0, hh, 0, j] = St
                o_ref[sl, ln] = o_intra[hh][sl] + _dot(qt_b[sl, ln], St.astype(BF16), NT)
                S[hh] = St * eT[HG_CHUNK * j:HG_CHUNK * j + 1, ln] + _dot(v[sl, ln], kh_b[sl, ln], TN)
        o = o_ref[...]
        r = _per_head(lambda a: lax.rsqrt(jnp.mean(a * a, axis=-1, keepdims=True) + EPS), o)
        gv = g_ref[...].astype(F32)
        y_ref[...] = (o * r * nw_ref[...] * gv * _sigmoid(gv)).astype(y_ref.dtype)

    rowblk = pl.BlockSpec((Q, 128 * HG_HP), lambda h, b, t: (b * nb + t, h))
    return _call(
        body, name="hgrn_fwd", grid=(HG_HEADS // HG_HP, Bl, nb), in_specs=_hg_specs(nb),
        out_specs=[rowblk, rowblk,
                   pl.BlockSpec((1, HG_HP, 1, NSUB, 128, 128), lambda h, b, t: (b, h, t, 0, 0, 0))],
        out_shape=[jax.ShapeDtypeStruct((M, HG_WIDTH), BF16), jax.ShapeDtypeStruct((M, HG_WIDTH), F32),
                   jax.ShapeDtypeStruct((Bl, HG_HEADS, nb, NSUB, 128, 128), F32)],
        scratch=[pltpu.VMEM((HG_HP, 128, 128), F32)], sem=("parallel", "arbitrary", "arbitrary"),
        args=(proj, proj, proj, proj, hb, nw), comm=comm)


def _hgrn_bwd(proj, hb, nw, o_saved, st_saved, dyb, dproj, Bl, nb, comm=None):
    assert HG_HP == HG_HEADS

    def body(q_ref, f_ref, i_ref, g_ref, hb_ref, nw_ref, o_ref, st_ref, dy_ref,
             d_ref, dhb_ref, dnw_ref, dS, a_dqt, a_dv, a_dkh, a_dgl):
        b, t = pl.program_id(1), pl.program_id(2)

        @pl.when(t == 0)
        def _():
            dS[...] = jnp.zeros_like(dS)

        first_step = jnp.logical_and(b == 0, t == 0)
        s = _hg_setup(nb - 1 - t, q_ref, f_ref, hb_ref)
        v = i_ref[...]
        qt_b, kt_b, kh_b = s["qt"].astype(BF16), s["kt"].astype(BF16), s["kh"].astype(BF16)
        eT = jnp.exp(s["T"])
        att = [jnp.where(s["causal"], _dot(qt_b[:, ln], kt_b[:, ln], NT), 0.0).astype(BF16) for ln in HEAD_LANES]

        o = o_ref[...]
        r = _per_head(lambda a: lax.rsqrt(jnp.mean(a * a, axis=-1, keepdims=True) + EPS), o)
        xhat = o * r
        gv = g_ref[...].astype(F32)
        sgv = _sigmoid(gv)
        dyv = dy_ref[...].astype(F32)
        d_on = dyv * gv * sgv
        dg_out = dyv * xhat * nw_ref[...] * _dsilu(gv, sgv)
        gw = d_on * nw_ref[...]
        do = r * (gw - xhat * _per_head(lambda a, c: jnp.mean(a * c, axis=-1, keepdims=True), gw, xhat))
        dnw = jnp.sum(d_on * xhat, axis=0, keepdims=True)
        do_b = do.astype(BF16)

        datt = [jnp.where(s["causal"], _dot(do_b[:, ln], v[:, ln], NT), 0.0).astype(BF16) for ln in HEAD_LANES]
        dqt = jnp.concatenate([_dot(datt[hh], kt_b[:, ln], NN) for hh, ln in enumerate(HEAD_LANES)], axis=1)
        dkt = jnp.concatenate([_dot(datt[hh], qt_b[:, ln], TN) for hh, ln in enumerate(HEAD_LANES)], axis=1)
        dv = jnp.concatenate([_dot(att[hh], do_b[:, ln], TN) for hh, ln in enumerate(HEAD_LANES)], axis=1)
        last_row = (lax.broadcasted_iota(jnp.int32, (HG_CHUNK, 128), 0) == HG_CHUNK - 1)
        for j in reversed(range(NSUB)):
            sl = slice(HG_CHUNK * j, HG_CHUNK * (j + 1))
            for hh, ln in enumerate(HEAD_LANES):
                St = st_ref[0, hh, 0, j]
                dSt = dS[hh]
                St_b, dSt_b = St.astype(BF16), dSt.astype(BF16)
                eT_j = eT[HG_CHUNK * j:HG_CHUNK * j + 1, ln]
                dkh_j = _dot(v[sl, ln], dSt_b, NN)
                a_dqt[sl, ln] = _dot(do_b[sl, ln], St_b, NN)
                a_dv[sl, ln] = _dot(kh_b[sl, ln], dSt_b, NT)
                a_dkh[sl, ln] = dkh_j
                dlast = (jnp.sum(St * dSt, axis=0, keepdims=True) * eT_j
                         + jnp.sum(dkh_j * s["kh"][sl, ln], axis=0, keepdims=True))
                a_dgl[sl, ln] = jnp.where(last_row, dlast, 0.0)
                dS[hh] = dSt * eT_j + _dot(do_b[sl, ln], qt_b[sl, ln], TN)
        dqt = dqt + a_dqt[...]
        dv = dv + a_dv[...]
        dkh = a_dkh[...]
        dG = dqt * s["qt"] - dkt * s["kt"] - dkh * s["kh"] + a_dgl[...]
        rev_causal = jnp.logical_and(s["same"], s["col"] >= s["row"])
        dgl = _dot01(rev_causal, dG, NN, "a")
        dk = dkt * s["eGn"] + dkh * s["eTG"]
        dfg = dgl / s["fg"] - dk
        lb, sg = s["lb"], s["sg"]
        keep = s["valid"].astype(F32)
        d_ref[:, 0:w] = (dqt * s["eG"] * _dsilu(s["qv"], s["sq"]) * keep).astype(d_ref.dtype)
        d_ref[:, w:2 * w] = (dfg * (1.0 - lb) * sg * (1.0 - sg) * keep).astype(d_ref.dtype)
        d_ref[:, 2 * w:3 * w] = (dv * keep).astype(d_ref.dtype)
        d_ref[:, 3 * w:4 * w] = (dg_out * keep).astype(d_ref.dtype)
        dlb = jnp.sum(dfg * (1.0 - sg) * keep, axis=0, keepdims=True) * lb * (1.0 - lb)
        dhb = jnp.concatenate([dlb, -dlb], axis=0)

        @pl.when(first_step)
        def _():
            dhb_ref[...] = dhb
            dnw_ref[...] = dnw

        @pl.when(jnp.logical_not(first_step))
        def _():
            dhb_ref[...] += dhb
            dnw_ref[...] += dnw

    w = 128 * HG_HP
    rowblk = pl.BlockSpec((Q, w), lambda h, b, t: (b * nb + nb - 1 - t, h))
    return _call(
        body, name="hgrn_bwd", grid=(HG_HEADS // HG_HP, Bl, nb),
        in_specs=_hg_specs(nb, rev=True) + [
            rowblk, pl.BlockSpec((1, HG_HP, 1, NSUB, 128, 128), lambda h, b, t: (b, h, nb - 1 - t, 0, 0, 0)), rowblk],
        out_specs=[pl.BlockSpec((pl.Element(Q), pl.Element(4 * w)),
                                lambda h, b, t: (pl.multiple_of((b * nb + nb - 1 - t) * Q, Q), 3 * HG_WIDTH)),
                   pl.BlockSpec((2, w), lambda h, b, t: (0, h)), pl.BlockSpec((1, w), lambda h, b, t: (0, h))],
        out_shape=[jax.ShapeDtypeStruct(dproj.shape, BF16),
                   jax.ShapeDtypeStruct((2, HG_WIDTH), F32), jax.ShapeDtypeStruct((1, HG_WIDTH), F32)],
        scratch=[pltpu.VMEM((HG_HP, 128, 128), F32)] + [pltpu.VMEM((Q, w), F32)] * 4,
        sem=("parallel", "arbitrary", "arbitrary"),
        args=(proj, proj, proj, proj, hb, nw, o_saved, st_saved, dyb), comm=comm, into=(dproj, 0))


def _adamw(name, parts, w, m, v, comm=None):
    R, C = w.shape
    S = parts.shape[0]
    tr, tc = (_tile(R, (256, 176, 128, 64, 8)), C) if R % 8 == 0 else (R, 256)
    c1, c2 = 1.0 - ADAM_B1 ** ADAM_STEP, 1.0 - ADAM_B2 ** ADAM_STEP

    def body(p_ref, w_ref, m_ref, v_ref, g_ref, d_ref, nm_ref, nv_ref):
        g = p_ref[0].astype(F32)
        for s in range(1, S):
            g = g + p_ref[s].astype(F32)
        nm = ADAM_B1 * m_ref[...] + (1.0 - ADAM_B1) * g
        nv = ADAM_B2 * v_ref[...] + (1.0 - ADAM_B2) * (g * g)
        g_ref[...] = g
        nm_ref[...] = nm
        nv_ref[...] = nv
        d_ref[...] = -ADAM_LR * ((nm / c1) / (jnp.sqrt(nv / c2) + ADAM_EPS) + ADAM_WD * w_ref[...])

    blk = pl.BlockSpec((tr, tc), lambda i, j: (i, j))
    return _call(
        body, name=name, grid=(R // tr, C // tc),
        in_specs=[pl.BlockSpec((S, tr, tc), lambda i, j: (0, i, j)), blk, blk, blk], out_specs=[blk] * 4,
        out_shape=[jax.ShapeDtypeStruct((R, C), F32)] * 4, scratch=[], sem=("parallel", "parallel"),
        args=(parts, w, m, v), comm=comm)


def _pair_sum(name, by_core, arrived):
    _, J, R, C = by_core.shape
    tc = _tile(C, (512, 256, 128))

    def body(c_ref, a_ref, b_ref, o_ref):
        o_ref[...] = (a_ref[0].astype(F32) + b_ref[...].astype(F32)).astype(o_ref.dtype)

    blk = pl.BlockSpec((1, R, tc), lambda j, k, c_ref: (j, 0, k))
    return pl.pallas_call(
        body, name=name,
        grid_spec=pltpu.PrefetchScalarGridSpec(
            num_scalar_prefetch=1, grid=(J, C // tc),
            in_specs=[pl.BlockSpec((1, 1, R, tc), lambda j, k, c_ref: (c_ref[0], j, 0, k)), blk], out_specs=blk),
        out_shape=jax.ShapeDtypeStruct(arrived.shape, arrived.dtype), compiler_params=_params(("parallel", "parallel")),
    )(lax.axis_index("c").astype(jnp.int32).reshape(1), by_core, arrived)


def _sum_parts(name, parts):
    S, R, C = parts.shape

    def body(p_ref, o_ref):
        g = p_ref[0]
        for s in range(1, S):
            g = g + p_ref[s]
        o_ref[...] = g

    return pl.pallas_call(
        body, name=name, out_shape=jax.ShapeDtypeStruct((R, C), F32),
        in_specs=[pl.BlockSpec(memory_space=pltpu.VMEM)], out_specs=pl.BlockSpec(memory_space=pltpu.VMEM),
    )(parts)


def _heads_to_lanes(p):
    return jnp.pad(p, [(0, 0)] * (p.ndim - 1) + [(0, 128 - SSD_HEADS)])


def _lanes_to_heads(p):
    return p[..., :SSD_HEADS]


def _pack_rows(arrs):
    flat = jnp.concatenate([a.reshape(-1).astype(F32) for a in arrs])
    return jnp.pad(flat, (0, (-flat.shape[0]) % (8 * D_MODEL))).reshape(-1, D_MODEL)


def _unpack_rows(packed, like):
    flat, outs, at = packed.reshape(-1), [], 0
    for a in like:
        outs.append(flat[at:at + a.size].reshape(a.shape))
        at += a.size
    return outs


def _cols(gth):
    return jnp.transpose(gth, (1, 0, 2)).reshape(gth.shape[1], -1)


def _rows(gth):
    return gth.reshape(-1, gth.shape[2])


def _to_rows(g):
    return g.reshape(N_DEV, -1, g.shape[1]).astype(BF16)


def _by_core(g):
    return jnp.transpose(g.reshape(N_DEV // 2, 2, -1, g.shape[1]), (1, 0, 2, 3)).astype(BF16)


DT_ROW = 3072


def _chip_sums(tag, by_core, swap_in=None):
    arrived = swap_in(by_core) if swap_in else _exchange(tag + "_swap", "swap", by_core)
    return [_pair_sum(f"{tag}_chipsum{i}", m, a) for i, (m, a) in enumerate(zip(by_core, arrived))]


def _ffn_fwd_gu(tag, n, w_gu_t, comm=None):
    M = n.shape[0]
    F = w_gu_t.shape[0] // 2
    tm = _tile(M, (544, 256))
    outs = _fused_matmul(
        tag + "_gu", M, F, D_MODEL,
        [dict(a=n, b=w_gu_t, trans_b=True, acc=0, resident=True),
         dict(a=n, b=w_gu_t, trans_b=True, bn_off=1, acc=1, resident=True)], [],
        lambda accs, ex: (accs[0], accs[1], accs[0] * _sigmoid(accs[0]) * accs[1]),
        [BF16, BF16, BF16], 2, tm, F, D_MODEL, outer="i", comm=comm, sub=256)
    return (n, *outs[:3]), outs[3:]


def _rmsnorm_tile(x, w):
    return x * lax.rsqrt(jnp.mean(x * x, axis=-1, keepdims=True) + EPS) * w


def _ffn_fwd_down(tag, h, a, w_down, next_norm=None, comm=None):
    M = h.shape[0]
    F = w_down.shape[0]
    tm = _tile(M, (1088, 544, 256))
    if next_norm is None:
        (h_out,) = _fused_matmul(
            tag + "_down", M, D_MODEL, F, [dict(a=a, b=w_down, acc=0)], [(h, 0)],
            lambda accs, ex: (ex[0] + 0.5 * accs[0],), [F32], 1, tm, D_MODEL, F, outer="j", sub=256)
        return h_out

    def with_norm(accs, ex):
        h_new = ex[0] + 0.5 * accs[0]
        return h_new, _rmsnorm_tile(h_new, ex[1])

    return _fused_matmul(tag + "_down", M, D_MODEL, F, [dict(a=a, b=w_down, acc=0, resident=True)], [(h, 0)], with_norm,
                         [F32, BF16], 1, tm, D_MODEL, F, outer="j", vecs=[next_norm], comm=comm)


def _ffn_bwd(tag, dh, dh_b, h, norm_w, w_gu_t, w_down, saved, scatter=False):
    n, g, u, a = saved
    M = h.shape[0]
    F = w_down.shape[0]
    tm = _tile(M, (544, 256))
    tn = _tile(F, (1408, 704, 256))

    def swiglu_bwd(accs, ex):
        da, gv, uv = 0.5 * accs[0], ex[0].astype(F32), ex[1].astype(F32)
        s = _sigmoid(gv)
        return da * uv * _dsilu(gv, s), da * gv * s

    (dgu,) = _fused_matmul(
        tag + "_dact", M, F, D_MODEL, [dict(a=dh_b, b=w_down, trans_b=True, acc=0, resident=True)], [(g, 0), (u, 0)],
        swiglu_bwd, [BF16, BF16], 1, tm, F, D_MODEL, outer="i", stack=True, sub=256)
    tr = _tile(M, (2176, 256))
    (dw_down,) = _matmul_tn(tag + "_dwd", a, dh_b, tn, D_MODEL, tr, scale=0.5)
    dw_gu_t, *p_down = _matmul_tn(tag + "_dwgu", dgu, n, tn, D_MODEL, tr,
                                  comm=("scatter", [_to_rows(dw_down)]) if scatter else None)
    comm = None
    if scatter:
        comm = ("chips", _chip_sums(tag + "_wgu", [_by_core(dw_gu_t)]))
    def norm_bwd(accs, ex):
        dh_prev, dw = _rmsnorm_bwd_tile(accs[0], ex[0], ex[2], ex[1])
        return dh_prev, dh_prev, dw

    dh_prev, dh_prev_b, dnorm, *p_gu = _fused_matmul(
        tag + "_dn", M, D_MODEL, 2 * F,
        [dict(a=dgu, b=w_gu_t, acc=0, resident=True)], [(h, 0), (dh, 0)],
        norm_bwd, [F32, BF16], 1, tm, D_MODEL, 2 * F, outer="i", comm=comm, vecs=[norm_w], row_sums=1)
    return (dh_prev, dh_prev_b, dnorm, *((p_gu[0], p_down[0]) if scatter else (dw_gu_t, dw_down)))


def kernel(x, meta_tokens, ffn1_norm, ffn1_w_gu, ffn1_w_down, mix_norm, w_in, ssd_conv_w, ssd_conv_b, ssd_dt_bias, ssd_a_log, ssd_d, ssd_norm, hg_lower_bound, hg_norm, w_branch_a, w_branch_b, w_out, ffn2_norm, ffn2_w_gu, ffn2_w_down, final_norm, loss_target, m_meta_tokens, m_ffn1_norm, m_ffn1_w_gu, m_ffn1_w_down, m_mix_norm, m_w_in, m_ssd_conv_w, m_ssd_conv_b, m_ssd_dt_bias, m_ssd_a_log, m_ssd_d, m_ssd_norm, m_hg_lower_bound, m_hg_norm, m_w_branch_a, m_w_branch_b, m_w_out, m_ffn2_norm, m_ffn2_w_gu, m_ffn2_w_down, m_final_norm, v_meta_tokens, v_ffn1_norm, v_ffn1_w_gu, v_ffn1_w_down, v_mix_norm, v_w_in, v_ssd_conv_w, v_ssd_conv_b, v_ssd_dt_bias, v_ssd_a_log, v_ssd_d, v_ssd_norm, v_hg_lower_bound, v_hg_norm, v_w_branch_a, v_w_branch_b, v_w_out, v_ffn2_norm, v_ffn2_w_gu, v_ffn2_w_down, v_final_norm):
    Bl, S, D = x.shape
    T = PAD + N_META + S
    nc = T // Q
    M = Bl * T
    me = 4 * lax.axis_index("x") + 2 * lax.axis_index("y") + lax.axis_index("c")

    bf = lambda a: a[0].astype(BF16)
    bft = lambda a: a[0].T.astype(BF16)
    bias_p, alog_p, d_p = _heads_to_lanes(ssd_dt_bias), _heads_to_lanes(ssd_a_log), _heads_to_lanes(ssd_d)
    final_w = final_norm.reshape(1, D)

    h0, n1, g_wgu1, g_meta, g_conv_w = _embed_norm(
        x, ffn1_norm, comm=("gather", [bft(ffn1_w_gu), meta_tokens, ssd_conv_w[0]]))
    wgu1, meta_full, conv_w_full = _rows(g_wgu1), _cols(g_meta), _cols(g_conv_w)
    h0, n1 = _embed_meta(meta_full, ffn1_norm, h0, n1, Bl)
    tm = _tile(M, (1088, 544, 256))
    win_shard = bft(w_in)
    cut = (win_shard.shape[0] // 32) * 16
    ffn1_saved, (g_wd1, g_win_a) = _ffn_fwd_gu("ffn1", n1, wgu1, comm=("gather", [bf(ffn1_w_down), win_shard[:cut]]))
    wd1 = _rows(g_wd1)
    h1, un, g_win_b = _ffn_fwd_down("ffn1", h0, ffn1_saved[3], wd1, next_norm=mix_norm,
                                    comm=("gather", [win_shard[cut:]]))
    win_t = _rows(jnp.concatenate([g_win_a, g_win_b], axis=1))
    win_dt = jnp.pad(win_t[DT_ROW:DT_ROW + SSD_HEADS], ((0, 128 - SSD_HEADS), (0, 0)))
    plain = lambda accs, ex: (accs[0],)
    proj, g_wa, g_wb, g_wo = _fused_matmul(
        "in_proj", M, N_MAIN, D, [dict(a=un, b=win_t, trans_b=True, acc=0, b_shift=(DT_ROW // 3072, SSD_HEADS))], [],
        plain, [BF16], 1, tm, 3072, D,
        outer="j", comm=("gather", [bf(w_branch_a), bf(w_branch_b), bf(w_out)]), sub=512)
    wa, wb, wo = _rows(g_wa), _rows(g_wb), _rows(g_wo)
    (dtr,) = _fused_matmul("in_proj_dt", M, 128, D, [dict(a=un, b=win_dt, trans_b=True, acc=0)], [], plain, [F32], 1,
                           tm, 128, D, outer="j")
    xc = _conv_fwd(proj, conv_w_full, ssd_conv_b, Bl, T)
    ya, ssd_prev = _ssd_fwd(xc, dtr, proj, bias_p, alog_p, d_p, ssd_norm, Bl, nc)
    yb, hg_o, hg_st, g_wgu2, g_wd2 = _hgrn_fwd(proj, hg_lower_bound, hg_norm, Bl, nc,
                                               comm=("gather", [bft(ffn2_w_gu), bf(ffn2_w_down)]))
    wgu2, wd2 = _rows(g_wgu2), _rows(g_wd2)

    def branch_fwd(accs, ex):
        pa, pb = accs
        return pa, pb, _sigmoid(ex[0].astype(F32)) * pa + _sigmoid(ex[1].astype(F32)) * pb

    pa, pb, merged = _fused_matmul(
        "branches", M, D, D, [dict(a=ya, b=wa, acc=0), dict(a=yb, b=wb, acc=1)], [(proj, 7), (proj, 8)],
        branch_fwd, [BF16, BF16, BF16], 2, tm, D, D, outer="j")
    def out_with_norm(accs, ex):
        h_new = ex[0] + accs[0]
        return h_new, _rmsnorm_tile(h_new, ex[1])

    h2, n2 = _fused_matmul("out_proj", M, D, D, [dict(a=merged, b=wo, acc=0)], [(h1, 0)], out_with_norm,
                           [F32, BF16], 1, tm, D, D, outer="j", vecs=[ffn2_norm])
    ffn2_saved, _ = _ffn_fwd_gu("ffn2", n2, wgu2)
    h3 = _ffn_fwd_down("ffn2", h2, ffn2_saved[3], wd2)

    dh3, dh3_b, d_final, loss_part = _loss_head(h3, final_w, loss_target, Bl, nc)
    dh2, dh2_b, d_ffn2_norm, d_wgu2, d_wd2 = _ffn_bwd("ffn2", dh3, dh3_b, h2, ffn2_norm, wgu2, wd2, ffn2_saved)

    def branch_bwd(accs, ex):
        dm = accs[0]
        ga, gb, pav, pbv = (e.astype(F32) for e in ex)
        sa, sb = _sigmoid(ga), _sigmoid(gb)
        return (dm * sa, dm * sb,
                jnp.concatenate([dm * pav * sa * (1.0 - sa), dm * pbv * sb * (1.0 - sb)], axis=1))

    d_merged_outs = []

    def d_merged_with_swap(theirs):
        d_merged_outs.extend(_fused_matmul(
            "d_merged", M, D, D, [dict(a=dh2_b, b=wo, trans_b=True, acc=0)], [(proj, 7), (proj, 8), (pa, 0), (pb, 0)],
            branch_bwd, [BF16] * 2, 1, tm, D, D, outer="j", comm=("swap", theirs),
            wide=dict(width=2 * D, col=7 * D, total=N_MAIN, dtype=BF16)))
        return d_merged_outs[3:]

    s_ffn2 = _chip_sums("ffn2", [_by_core(d_wgu2), _by_core(d_wd2)], swap_in=d_merged_with_swap)
    dpa, dpb, dproj = d_merged_outs[:3]
    (d_wo,) = _matmul_tn("d_w_out", merged, dh2_b, 512, D, M // 2)
    (d_wa,) = _matmul_tn("d_w_a", ya, dpa, 512, D, M // 2)
    (d_wb,) = _matmul_tn("d_w_b", yb, dpb, 512, D, M // 2)
    dya, dyb = _fused_matmul(
        "d_branches", M, D, D, [dict(a=dpa, b=wa, trans_b=True, acc=0), dict(a=dpb, b=wb, trans_b=True, acc=1)], [],
        lambda accs, ex: (accs[0], accs[1]), [BF16, BF16], 2, tm, D, D, outer="j")
    *ssd_grads, p_wgu2, p_wd2 = _ssd_bwd(xc, dtr, proj, bias_p, alog_p, d_p, ssd_norm, ssd_prev, dya, dproj, Bl, nc,
                                         comm=("chips", s_ffn2))
    dxc, dproj, ddtr, d_bias_p, d_alog_p, d_d_p, d_ssd_norm = ssd_grads
    dproj, d_conv_w, d_conv_b = _conv_bwd(proj, conv_w_full, ssd_conv_b, dxc, dproj, Bl, T)
    dproj, d_hb, d_hg_norm, p_wa, p_wb, p_wo = _hgrn_bwd(
        proj, hg_lower_bound, hg_norm, hg_o, hg_st, dyb, dproj, Bl, nc,
        comm=("scatter", [_to_rows(d_wa), _to_rows(d_wb), _to_rows(d_wo)]))
    ddtr_b = ddtr.astype(BF16)
    (d_win_t,) = _matmul_tn("d_w_in", dproj, un, 768, D, M, out_skip=(DT_ROW, SSD_HEADS))
    (d_win_dt,) = _matmul_tn("d_w_in_dt", ddtr_b, un, 128, D, M)
    d_win_t = lax.dynamic_update_slice(d_win_t, d_win_dt[:SSD_HEADS], (DT_ROW, 0))
    d_un_dt_outs = []

    def d_un_dt_with_swap(theirs):
        d_un_dt_outs.extend(_fused_matmul("d_un_dt", M, D, 128, [dict(a=ddtr_b, b=win_dt, acc=0)], [], plain, [F32], 1,
                                          tm, D, 128, outer="j", comm=("swap", theirs)))
        return d_un_dt_outs[1:]

    s_win = _chip_sums("w_in", [_by_core(d_win_t)], swap_in=d_un_dt_with_swap)
    def mix_norm_bwd(accs, ex):
        dh, dw = _rmsnorm_bwd_tile(accs[0] + ex[0], ex[1], ex[3], ex[2])
        return dh, dh, dw

    dh1, dh1_b, d_mix_norm, p_win = _fused_matmul(
        "d_un", M, D, N_MAIN, [dict(a=dproj, b=win_t, acc=0, b_shift=(DT_ROW // 3072, SSD_HEADS))],
        [(d_un_dt_outs[0], 0), (h1, 0), (dh2, 0)],
        mix_norm_bwd, [F32, BF16], 1, _tile(M, (544, 256)), D, 3072, outer="i", comm=("chips", s_win),
        vecs=[mix_norm], row_sums=1)
    dh0, _, d_ffn1_norm, p_wgu1, p_wd1 = _ffn_bwd("ffn1", dh1, dh1_b, h0, ffn1_norm, wgu1, wd1, ffn1_saved, scatter=True)

    dh0 = dh0.reshape(Bl, T, D)
    grad_x = dh0[:, PAD + N_META:]
    d_meta = dh0[:, PAD:PAD + N_META]

    small_grads = [d_ffn1_norm, d_mix_norm, d_conv_b, _lanes_to_heads(d_bias_p), _lanes_to_heads(d_alog_p),
                   _lanes_to_heads(d_d_p), d_ssd_norm, d_hb, d_hg_norm, d_ffn2_norm, d_final.reshape(D), d_conv_w]
    small_like = small_grads + [d_meta[b] for b in range(Bl)] + [loss_part[0, 0:1]]
    small_packed = _pack_rows(small_like)
    parts = [p_wgu1, p_wd1, p_win, p_wa, p_wb, p_wo, p_wgu2, p_wd2]

    names = ["meta_tokens", "ffn1_norm", "ffn1_w_gu", "ffn1_w_down", "mix_norm", "w_in", "ssd_conv_w", "ssd_conv_b",
             "ssd_dt_bias", "ssd_a_log", "ssd_d", "ssd_norm", "hg_lower_bound", "hg_norm", "w_branch_a", "w_branch_b",
             "w_out", "ffn2_norm", "ffn2_w_gu", "ffn2_w_down", "final_norm"]
    W = dict(meta_tokens=meta_tokens, ffn1_norm=ffn1_norm, ffn1_w_gu=ffn1_w_gu, ffn1_w_down=ffn1_w_down, mix_norm=mix_norm,
             w_in=w_in, ssd_conv_w=ssd_conv_w, ssd_conv_b=ssd_conv_b, ssd_dt_bias=ssd_dt_bias, ssd_a_log=ssd_a_log,
             ssd_d=ssd_d, ssd_norm=ssd_norm, hg_lower_bound=hg_lower_bound, hg_norm=hg_norm, w_branch_a=w_branch_a,
             w_branch_b=w_branch_b, w_out=w_out, ffn2_norm=ffn2_norm, ffn2_w_gu=ffn2_w_gu, ffn2_w_down=ffn2_w_down,
             final_norm=final_norm)
    Mo = dict(meta_tokens=m_meta_tokens, ffn1_norm=m_ffn1_norm, ffn1_w_gu=m_ffn1_w_gu, ffn1_w_down=m_ffn1_w_down,
              mix_norm=m_mix_norm, w_in=m_w_in, ssd_conv_w=m_ssd_conv_w, ssd_conv_b=m_ssd_conv_b, ssd_dt_bias=m_ssd_dt_bias,
              ssd_a_log=m_ssd_a_log, ssd_d=m_ssd_d, ssd_norm=m_ssd_norm, hg_lower_bound=m_hg_lower_bound, hg_norm=m_hg_norm,
              w_branch_a=m_w_branch_a, w_branch_b=m_w_branch_b, w_out=m_w_out, ffn2_norm=m_ffn2_norm, ffn2_w_gu=m_ffn2_w_gu,
              ffn2_w_down=m_ffn2_w_down, final_norm=m_final_norm)
    Vo = dict(meta_tokens=v_meta_tokens, ffn1_norm=v_ffn1_norm, ffn1_w_gu=v_ffn1_w_gu, ffn1_w_down=v_ffn1_w_down,
              mix_norm=v_mix_norm, w_in=v_w_in, ssd_conv_w=v_ssd_conv_w, ssd_conv_b=v_ssd_conv_b, ssd_dt_bias=v_ssd_dt_bias,
              ssd_a_log=v_ssd_a_log, ssd_d=v_ssd_d, ssd_norm=v_ssd_norm, hg_lower_bound=v_hg_lower_bound, hg_norm=v_hg_norm,
              w_branch_a=v_w_branch_a, w_branch_b=v_w_branch_b, w_out=v_w_out, ffn2_norm=v_ffn2_norm, ffn2_w_gu=v_ffn2_w_gu,
              ffn2_w_down=v_ffn2_w_down, final_norm=v_final_norm)
    grads, deltas, new_m, new_v = {}, {}, {}, {}
    big_names = ["ffn1_w_gu", "ffn1_w_down", "w_in", "w_branch_a", "w_branch_b", "w_out", "ffn2_w_gu", "ffn2_w_down"]
    transposed = ("ffn1_w_gu", "ffn2_w_gu", "w_in")
    small_all = None
    for nm, part in zip(big_names, parts):
        view = (lambda a: a[0].T) if nm in transposed else (lambda a: a[0])
        back = (lambda o: o.T[None]) if nm in transposed else (lambda o: o[None])
        outs = _adamw("adamw_" + nm, part, view(W[nm]), view(Mo[nm]), view(Vo[nm]),
                      comm=("gather", [small_packed]) if small_all is None else None)
        if small_all is None:
            small_all = outs[4]
        grads[nm], deltas[nm], new_m[nm], new_v[nm] = (back(o) for o in outs[:4])
    unpacked = _unpack_rows(_sum_parts("sum_small_grads", small_all), small_like)
    g_small = unpacked[:len(small_grads)]
    g_meta_full = unpacked[len(small_grads)]
    for b in range(1, Bl):
        g_meta_full = g_meta_full + unpacked[len(small_grads) + b]
    g_meta = lax.dynamic_slice_in_dim(g_meta_full, me * (D // N_DEV), D // N_DEV, axis=1)
    g_conv_w = lax.dynamic_slice_in_dim(g_small[11], me * (SSD_CONV_CH // N_DEV), SSD_CONV_CH // N_DEV, axis=1)
    loss = unpacked[-1].reshape(())
    small_names = ["ffn1_norm", "mix_norm", "ssd_conv_b", "ssd_dt_bias", "ssd_a_log", "ssd_d", "ssd_norm", "hg_lower_bound",
                   "hg_norm", "ffn2_norm", "final_norm", "ssd_conv_w", "meta_tokens"]
    small_g = g_small[:11] + [g_conv_w.reshape(ssd_conv_w.shape), g_meta]
    pk = lambda d: _pack_rows([d[nm] for nm in small_names])
    outs = _adamw("adamw_small", _pack_rows(small_g)[None], pk(W), pk(Mo), pk(Vo))
    like = [W[nm] for nm in small_names]
    for dst, o in zip((grads, deltas, new_m, new_v), outs):
        for nm, val in zip(small_names, _unpack_rows(o, like)):
            dst[nm] = val

    return (loss, grad_x, *[grads[nm] for nm in names], *[deltas[nm] for nm in names],
            *[new_m[nm] for nm in names], *[new_v[nm] for nm in names])
```
